```python
import jax, jax.numpy as jnp
from jax import lax
import numpy as np

D_MODEL = 1024
BATCH = 16
SEQ = 2048
DEPTH = 4

N_MIXERS = 2
CONV_CH = D_MODEL
CONV_KERNEL = 31
GMLP_CHUNK = 128
GMLP_GROUPS = 8
GMLP_DFF = 2 * D_MODEL
GMLP_HALF = GMLP_DFF // 2
GMLP_GROUP_CH = GMLP_HALF // GMLP_GROUPS
FFN_HIDDEN = 2816
FFN_CONV = 3
DEEPNORM_ALPHA = (2.0 * DEPTH) ** 0.25
DEEPNORM_BETA = (8.0 * DEPTH) ** -0.25
LN_EPS = 1e-5
N_CONV_LAYERS = (DEPTH + 1) // 2
N_GMLP_LAYERS = DEPTH // 2

kernel_name = "hybrid_conformer_gmlp_deepnorm"


def layer_norm(x, g, b):
    xf = x.astype(jnp.float32)
    mu = jnp.mean(xf, axis=-1, keepdims=True)
    xc = xf - mu
    var = jnp.mean(xc * xc, axis=-1, keepdims=True)
    y = xc * lax.rsqrt(var + LN_EPS)
    return (y * g.astype(jnp.float32) + b.astype(jnp.float32)).astype(x.dtype)


def causal_depthwise_conv(x, w, b):
    k, c = w.shape
    y = lax.conv_general_dilated(
        x, w[:, None, :].astype(x.dtype), window_strides=(1,), padding=[(k - 1, 0)],
        dimension_numbers=("NWC", "WIO", "NWC"), feature_group_count=c)
    return y + b


def conformer_conv_module(x, w_in, b_in, w_dw, b_dw, ln_g, ln_b, w_out, b_out):
    h = x @ w_in + b_in
    a, gate = jnp.split(h, 2, axis=-1)
    h = a * jax.nn.sigmoid(gate)
    h = causal_depthwise_conv(h, w_dw, b_dw)
    h = jax.nn.silu(layer_norm(h, ln_g, ln_b))
    return h @ w_out + b_out


def chunked_spatial_gating(x, w_in, b_in, ln_g, ln_b, w_s, b_s, w_out, b_out):
    bsz, t, _ = x.shape
    z = jax.nn.gelu(x @ w_in + b_in, approximate=False)
    u, v = jnp.split(z, 2, axis=-1)
    v = layer_norm(v, ln_g, ln_b)
    n_chunks = t // GMLP_CHUNK
    v = v.reshape(bsz, n_chunks, GMLP_CHUNK, GMLP_GROUPS, GMLP_GROUP_CH)
    mask = jnp.tril(jnp.ones((GMLP_CHUNK, GMLP_CHUNK), dtype=bool))
    w_causal = jnp.where(mask[None], w_s, jnp.zeros((), w_s.dtype))
    s = jnp.einsum("gts,bnsgc->bntgc", w_causal, v) + b_s.T[:, :, None]
    s = s.reshape(bsz, t, GMLP_HALF)
    return (u * s) @ w_out + b_out


def conv_ffn(x, w_up, b_up, w_dw, b_dw, w_down, b_down):
    h = x @ w_up + b_up
    h = causal_depthwise_conv(h, w_dw, b_dw)
    g, val = jnp.split(h, 2, axis=-1)
    return (jax.nn.silu(g) * val) @ w_down + b_down


def _fwd_setup_inputs(seed: int = 0) -> dict:
    key = jax.random.key(seed)
    ks = iter(jax.random.split(key, 32))

    def nrm(shape, std):
        return jax.random.normal(next(ks), shape, jnp.float32) * std

    def xavier_out(shape):
        fan_in, fan_out = shape[-2], shape[-1]
        return nrm(shape, DEEPNORM_BETA * (2.0 / (fan_in + fan_out)) ** 0.5)

    na, nb, d = N_CONV_LAYERS, N_GMLP_LAYERS, D_MODEL
    inp = {}
    inp["x"] = nrm((BATCH, SEQ, d), 1.0)
    inp["conv_w_in"] = nrm((na, d, 2 * CONV_CH), d ** -0.5)
    inp["conv_b_in"] = nrm((na, 2 * CONV_CH), 0.02)
    inp["conv_w_dw"] = nrm((na, CONV_KERNEL, CONV_CH), CONV_KERNEL ** -0.5)
    inp["conv_b_dw"] = nrm((na, CONV_CH), 0.02)
    inp["conv_ln_g"] = 1.0 + nrm((na, CONV_CH), 0.05)
    inp["conv_ln_b"] = nrm((na, CONV_CH), 0.02)
    inp["conv_w_out"] = xavier_out((na, CONV_CH, d))
    inp["conv_b_out"] = nrm((na, d), 0.02)
    inp["gmlp_w_in"] = nrm((nb, d, GMLP_DFF), d ** -0.5)
    inp["gmlp_b_in"] = nrm((nb, GMLP_DFF), 0.02)
    inp["gmlp_ln_g"] = 1.0 + nrm((nb, GMLP_HALF), 0.05)
    inp["gmlp_ln_b"] = nrm((nb, GMLP_HALF), 0.02)
    inp["gmlp_w_s"] = nrm((nb, GMLP_GROUPS, GMLP_CHUNK, GMLP_CHUNK), 0.5 * GMLP_CHUNK ** -0.5)
    inp["gmlp_b_s"] = 1.0 + nrm((nb, GMLP_GROUPS, GMLP_CHUNK), 0.1)
    inp["gmlp_w_out"] = xavier_out((nb, GMLP_HALF, d))
    inp["gmlp_b_out"] = nrm((nb, d), 0.02)
    inp["ffn_w_up"] = nrm((DEPTH, d, 2 * FFN_HIDDEN), d ** -0.5)
    inp["ffn_b_up"] = nrm((DEPTH, 2 * FFN_HIDDEN), 0.02)
    inp["ffn_w_dw"] = nrm((DEPTH, FFN_CONV, 2 * FFN_HIDDEN), FFN_CONV ** -0.5)
    inp["ffn_b_dw"] = nrm((DEPTH, 2 * FFN_HIDDEN), 0.02)
    inp["ffn_w_down"] = xavier_out((DEPTH, FFN_HIDDEN, d))
    inp["ffn_b_down"] = nrm((DEPTH, d), 0.02)
    inp["norm1_g"] = 1.0 + nrm((DEPTH, d), 0.05)
    inp["norm1_b"] = nrm((DEPTH, d), 0.02)
    inp["norm2_g"] = 1.0 + nrm((DEPTH, d), 0.05)
    inp["norm2_b"] = nrm((DEPTH, d), 0.02)
    return inp


def _fwd_reference(x, conv_w_in, conv_b_in, conv_w_dw, conv_b_dw, conv_ln_g, conv_ln_b,
              conv_w_out, conv_b_out, gmlp_w_in, gmlp_b_in, gmlp_ln_g, gmlp_ln_b,
              gmlp_w_s, gmlp_b_s, gmlp_w_out, gmlp_b_out, ffn_w_up, ffn_b_up,
              ffn_w_dw, ffn_b_dw, ffn_w_down, ffn_b_down, norm1_g, norm1_b,
              norm2_g, norm2_b):
    for i in range(DEPTH):
        j = i // N_MIXERS
        if i % N_MIXERS == 0:
            y = conformer_conv_module(x, conv_w_in[j], conv_b_in[j], conv_w_dw[j], conv_b_dw[j],
                                      conv_ln_g[j], conv_ln_b[j], conv_w_out[j], conv_b_out[j])
        else:
            y = chunked_spatial_gating(x, gmlp_w_in[j], gmlp_b_in[j], gmlp_ln_g[j], gmlp_ln_b[j],
                                       gmlp_w_s[j], gmlp_b_s[j], gmlp_w_out[j], gmlp_b_out[j])
        x = layer_norm(DEEPNORM_ALPHA * x + y, norm1_g[i], norm1_b[i])
        y = conv_ffn(x, ffn_w_up[i], ffn_b_up[i], ffn_w_dw[i], ffn_b_dw[i],
                     ffn_w_down[i], ffn_b_down[i])
        x = layer_norm(DEEPNORM_ALPHA * x + y, norm2_g[i], norm2_b[i])
    return x


import jax as _jax
import jax.numpy as _jnp

TWIN_FORMAT = 'train_step'
FWD_PARAMS = ['x', 'conv_w_in', 'conv_b_in', 'conv_w_dw', 'conv_b_dw', 'conv_ln_g', 'conv_ln_b', 'conv_w_out', 'conv_b_out', 'gmlp_w_in', 'gmlp_b_in', 'gmlp_ln_g', 'gmlp_ln_b', 'gmlp_w_s', 'gmlp_b_s', 'gmlp_w_out', 'gmlp_b_out', 'ffn_w_up', 'ffn_b_up', 'ffn_w_dw', 'ffn_b_dw', 'ffn_w_down', 'ffn_b_down', 'norm1_g', 'norm1_b', 'norm2_g', 'norm2_b']
TWIN_WEIGHTS = ['conv_w_in', 'conv_b_in', 'conv_w_dw', 'conv_b_dw', 'conv_ln_g', 'conv_ln_b', 'conv_w_out', 'conv_b_out', 'gmlp_w_in', 'gmlp_b_in', 'gmlp_ln_g', 'gmlp_ln_b', 'gmlp_w_s', 'gmlp_b_s', 'gmlp_w_out', 'gmlp_b_out', 'ffn_w_up', 'ffn_b_up', 'ffn_w_dw', 'ffn_b_dw', 'ffn_w_down', 'ffn_b_down', 'norm1_g', 'norm1_b', 'norm2_g', 'norm2_b']
TWIN_DIFF_INPUT = 'x'
TWIN_INPUTS = ['x', 'conv_w_in', 'conv_b_in', 'conv_w_dw', 'conv_b_dw', 'conv_ln_g', 'conv_ln_b', 'conv_w_out', 'conv_b_out', 'gmlp_w_in', 'gmlp_b_in', 'gmlp_ln_g', 'gmlp_ln_b', 'gmlp_w_s', 'gmlp_b_s', 'gmlp_w_out', 'gmlp_b_out', 'ffn_w_up', 'ffn_b_up', 'ffn_w_dw', 'ffn_b_dw', 'ffn_w_down', 'ffn_b_down', 'norm1_g', 'norm1_b', 'norm2_g', 'norm2_b', 'loss_target', 'm_conv_w_in', 'm_conv_b_in', 'm_conv_w_dw', 'm_conv_b_dw', 'm_conv_ln_g', 'm_conv_ln_b', 'm_conv_w_out', 'm_conv_b_out', 'm_gmlp_w_in', 'm_gmlp_b_in', 'm_gmlp_ln_g', 'm_gmlp_ln_b', 'm_gmlp_w_s', 'm_gmlp_b_s', 'm_gmlp_w_out', 'm_gmlp_b_out', 'm_ffn_w_up', 'm_ffn_b_up', 'm_ffn_w_dw', 'm_ffn_b_dw', 'm_ffn_w_down', 'm_ffn_b_down', 'm_norm1_g', 'm_norm1_b', 'm_norm2_g', 'm_norm2_b', 'v_conv_w_in', 'v_conv_b_in', 'v_conv_w_dw', 'v_conv_b_dw', 'v_conv_ln_g', 'v_conv_ln_b', 'v_conv_w_out', 'v_conv_b_out', 'v_gmlp_w_in', 'v_gmlp_b_in', 'v_gmlp_ln_g', 'v_gmlp_ln_b', 'v_gmlp_w_s', 'v_gmlp_b_s', 'v_gmlp_w_out', 'v_gmlp_b_out', 'v_ffn_w_up', 'v_ffn_b_up', 'v_ffn_w_dw', 'v_ffn_b_dw', 'v_ffn_w_down', 'v_ffn_b_down', 'v_norm1_g', 'v_norm1_b', 'v_norm2_g', 'v_norm2_b']
TWIN_OUTPUTS = ['loss', 'grad_x', 'grad_conv_w_in', 'grad_conv_b_in', 'grad_conv_w_dw', 'grad_conv_b_dw', 'grad_conv_ln_g', 'grad_conv_ln_b', 'grad_conv_w_out', 'grad_conv_b_out', 'grad_gmlp_w_in', 'grad_gmlp_b_in', 'grad_gmlp_ln_g', 'grad_gmlp_ln_b', 'grad_gmlp_w_s', 'grad_gmlp_b_s', 'grad_gmlp_w_out', 'grad_gmlp_b_out', 'grad_ffn_w_up', 'grad_ffn_b_up', 'grad_ffn_w_dw', 'grad_ffn_b_dw', 'grad_ffn_w_down', 'grad_ffn_b_down', 'grad_norm1_g', 'grad_norm1_b', 'grad_norm2_g', 'grad_norm2_b', 'delta_conv_w_in', 'delta_conv_b_in', 'delta_conv_w_dw', 'delta_conv_b_dw', 'delta_conv_ln_g', 'delta_conv_ln_b', 'delta_conv_w_out', 'delta_conv_b_out', 'delta_gmlp_w_in', 'delta_gmlp_b_in', 'delta_gmlp_ln_g', 'delta_gmlp_ln_b', 'delta_gmlp_w_s', 'delta_gmlp_b_s', 'delta_gmlp_w_out', 'delta_gmlp_b_out', 'delta_ffn_w_up', 'delta_ffn_b_up', 'delta_ffn_w_dw', 'delta_ffn_b_dw', 'delta_ffn_w_down', 'delta_ffn_b_down', 'delta_norm1_g', 'delta_norm1_b', 'delta_norm2_g', 'delta_norm2_b', 'new_m_conv_w_in', 'new_m_conv_b_in', 'new_m_conv_w_dw', 'new_m_conv_b_dw', 'new_m_conv_ln_g', 'new_m_conv_ln_b', 'new_m_conv_w_out', 'new_m_conv_b_out', 'new_m_gmlp_w_in', 'new_m_gmlp_b_in', 'new_m_gmlp_ln_g', 'new_m_gmlp_ln_b', 'new_m_gmlp_w_s', 'new_m_gmlp_b_s', 'new_m_gmlp_w_out', 'new_m_gmlp_b_out', 'new_m_ffn_w_up', 'new_m_ffn_b_up', 'new_m_ffn_w_dw', 'new_m_ffn_b_dw', 'new_m_ffn_w_down', 'new_m_ffn_b_down', 'new_m_norm1_g', 'new_m_norm1_b', 'new_m_norm2_g', 'new_m_norm2_b', 'new_v_conv_w_in', 'new_v_conv_b_in', 'new_v_conv_w_dw', 'new_v_conv_b_dw', 'new_v_conv_ln_g', 'new_v_conv_ln_b', 'new_v_conv_w_out', 'new_v_conv_b_out', 'new_v_gmlp_w_in', 'new_v_gmlp_b_in', 'new_v_gmlp_ln_g', 'new_v_gmlp_ln_b', 'new_v_gmlp_w_s', 'new_v_gmlp_b_s', 'new_v_gmlp_w_out', 'new_v_gmlp_b_out', 'new_v_ffn_w_up', 'new_v_ffn_b_up', 'new_v_ffn_w_dw', 'new_v_ffn_b_dw', 'new_v_ffn_w_down', 'new_v_ffn_b_down', 'new_v_norm1_g', 'new_v_norm1_b', 'new_v_norm2_g', 'new_v_norm2_b']
TWIN_LEAF_KINDS = {'loss': 'loss', 'grad_x': 'grad_x', 'grad_conv_w_in': 'grad_w', 'grad_conv_b_in': 'grad_w', 'grad_conv_w_dw': 'grad_w', 'grad_conv_b_dw': 'grad_w', 'grad_conv_ln_g': 'grad_w', 'grad_conv_ln_b': 'grad_w', 'grad_conv_w_out': 'grad_w', 'grad_conv_b_out': 'grad_w', 'grad_gmlp_w_in': 'grad_w', 'grad_gmlp_b_in': 'grad_w', 'grad_gmlp_ln_g': 'grad_w', 'grad_gmlp_ln_b': 'grad_w', 'grad_gmlp_w_s': 'grad_w', 'grad_gmlp_b_s': 'grad_w', 'grad_gmlp_w_out': 'grad_w', 'grad_gmlp_b_out': 'grad_w', 'grad_ffn_w_up': 'grad_w', 'grad_ffn_b_up': 'grad_w', 'grad_ffn_w_dw': 'grad_w', 'grad_ffn_b_dw': 'grad_w', 'grad_ffn_w_down': 'grad_w', 'grad_ffn_b_down': 'grad_w', 'grad_norm1_g': 'grad_w', 'grad_norm1_b': 'grad_w', 'grad_norm2_g': 'grad_w', 'grad_norm2_b': 'grad_w', 'delta_conv_w_in': 'delta_w', 'delta_conv_b_in': 'delta_w', 'delta_conv_w_dw': 'delta_w', 'delta_conv_b_dw': 'delta_w', 'delta_conv_ln_g': 'delta_w', 'delta_conv_ln_b': 'delta_w', 'delta_conv_w_out': 'delta_w', 'delta_conv_b_out': 'delta_w', 'delta_gmlp_w_in': 'delta_w', 'delta_gmlp_b_in': 'delta_w', 'delta_gmlp_ln_g': 'delta_w', 'delta_gmlp_ln_b': 'delta_w', 'delta_gmlp_w_s': 'delta_w', 'delta_gmlp_b_s': 'delta_w', 'delta_gmlp_w_out': 'delta_w', 'delta_gmlp_b_out': 'delta_w', 'delta_ffn_w_up': 'delta_w', 'delta_ffn_b_up': 'delta_w', 'delta_ffn_w_dw': 'delta_w', 'delta_ffn_b_dw': 'delta_w', 'delta_ffn_w_down': 'delta_w', 'delta_ffn_b_down': 'delta_w', 'delta_norm1_g': 'delta_w', 'delta_norm1_b': 'delta_w', 'delta_norm2_g': 'delta_w', 'delta_norm2_b': 'delta_w', 'new_m_conv_w_in': 'new_m', 'new_m_conv_b_in': 'new_m', 'new_m_conv_w_dw': 'new_m', 'new_m_conv_b_dw': 'new_m', 'new_m_conv_ln_g': 'new_m', 'new_m_conv_ln_b': 'new_m', 'new_m_conv_w_out': 'new_m', 'new_m_conv_b_out': 'new_m', 'new_m_gmlp_w_in': 'new_m', 'new_m_gmlp_b_in': 'new_m', 'new_m_gmlp_ln_g': 'new_m', 'new_m_gmlp_ln_b': 'new_m', 'new_m_gmlp_w_s': 'new_m', 'new_m_gmlp_b_s': 'new_m', 'new_m_gmlp_w_out': 'new_m', 'new_m_gmlp_b_out': 'new_m', 'new_m_ffn_w_up': 'new_m', 'new_m_ffn_b_up': 'new_m', 'new_m_ffn_w_dw': 'new_m', 'new_m_ffn_b_dw': 'new_m', 'new_m_ffn_w_down': 'new_m', 'new_m_ffn_b_down': 'new_m', 'new_m_norm1_g': 'new_m', 'new_m_norm1_b': 'new_m', 'new_m_norm2_g': 'new_m', 'new_m_norm2_b': 'new_m', 'new_v_conv_w_in': 'new_v', 'new_v_conv_b_in': 'new_v', 'new_v_conv_w_dw': 'new_v', 'new_v_conv_b_dw': 'new_v', 'new_v_conv_ln_g': 'new_v', 'new_v_conv_ln_b': 'new_v', 'new_v_conv_w_out': 'new_v', 'new_v_conv_b_out': 'new_v', 'new_v_gmlp_w_in': 'new_v', 'new_v_gmlp_b_in': 'new_v', 'new_v_gmlp_ln_g': 'new_v', 'new_v_gmlp_ln_b': 'new_v', 'new_v_gmlp_w_s': 'new_v', 'new_v_gmlp_b_s': 'new_v', 'new_v_gmlp_w_out': 'new_v', 'new_v_gmlp_b_out': 'new_v', 'new_v_ffn_w_up': 'new_v', 'new_v_ffn_b_up': 'new_v', 'new_v_ffn_w_dw': 'new_v', 'new_v_ffn_b_dw': 'new_v', 'new_v_ffn_w_down': 'new_v', 'new_v_ffn_b_down': 'new_v', 'new_v_norm1_g': 'new_v', 'new_v_norm1_b': 'new_v', 'new_v_norm2_g': 'new_v', 'new_v_norm2_b': 'new_v'}


def _forward(args):
    return _fwd_reference(*[args[k] for k in FWD_PARAMS])


def _output_shape():
    out = _jax.eval_shape(lambda: _forward(_fwd_setup_inputs(0)))
    return out.shape, out.dtype

N_MICROBATCH = 1
ADAM_LR = 0.001
ADAM_B1 = 0.9
ADAM_B2 = 0.999
ADAM_EPS = 1e-08
ADAM_WD = 0.01
ADAM_STEP = 10
PER_EXAMPLE_BATCH_AXIS = {'x': 0, 'loss_target': 0}
SHARED_INPUTS = []
_WEIGHT_DTYPES = {'conv_w_in': _jnp.float32, 'conv_b_in': _jnp.float32, 'conv_w_dw': _jnp.float32, 'conv_b_dw': _jnp.float32, 'conv_ln_g': _jnp.float32, 'conv_ln_b': _jnp.float32, 'conv_w_out': _jnp.float32, 'conv_b_out': _jnp.float32, 'gmlp_w_in': _jnp.float32, 'gmlp_b_in': _jnp.float32, 'gmlp_ln_g': _jnp.float32, 'gmlp_ln_b': _jnp.float32, 'gmlp_w_s': _jnp.float32, 'gmlp_b_s': _jnp.float32, 'gmlp_w_out': _jnp.float32, 'gmlp_b_out': _jnp.float32, 'ffn_w_up': _jnp.float32, 'ffn_b_up': _jnp.float32, 'ffn_w_dw': _jnp.float32, 'ffn_b_dw': _jnp.float32, 'ffn_w_down': _jnp.float32, 'ffn_b_down': _jnp.float32, 'norm1_g': _jnp.float32, 'norm1_b': _jnp.float32, 'norm2_g': _jnp.float32, 'norm2_b': _jnp.float32}
MOMENT_SCALE = {'conv_w_in': 2.204896e-02, 'conv_b_in': 4.354850e-02, 'conv_w_dw': 2.932682e-02, 'conv_b_dw': 9.901254e-02, 'conv_ln_g': 4.814773e-02, 'conv_ln_b': 6.033674e-02, 'conv_w_out': 8.270490e-02, 'conv_b_out': 3.023477e-01, 'gmlp_w_in': 2.528527e-02, 'gmlp_b_in': 4.930909e-02, 'gmlp_ln_g': 1.071940e-02, 'gmlp_ln_b': 1.046300e-02, 'gmlp_w_s': 2.108026e-02, 'gmlp_b_s': 2.967530e-02, 'gmlp_w_out': 1.026919e-01, 'gmlp_b_out': 3.067038e-01, 'ffn_w_up': 2.058652e-02, 'ffn_b_up': 2.351271e-02, 'ffn_w_dw': 2.046600e-02, 'ffn_b_dw': 2.276344e-02, 'ffn_w_down': 6.629022e-02, 'ffn_b_down': 3.106191e-01, 'norm1_g': 2.731213e+00, 'norm1_b': 5.189196e-01, 'norm2_g': 1.656532e+01, 'norm2_b': 1.996229e+00}


def _to_microbatches(a, axis):
    t = _jnp.moveaxis(a, axis, 0)
    t = t.reshape((N_MICROBATCH, t.shape[0] // N_MICROBATCH) + t.shape[1:])
    return _jnp.moveaxis(t, 1, axis + 1)


def setup_inputs(seed: int = 0) -> dict:
    inp = _fwd_setup_inputs(seed)
    key = _jax.random.fold_in(_jax.random.key(seed), 7919)
    shape, _ = _output_shape()
    out = dict(inp)
    out["loss_target"] = _jax.random.normal(_jax.random.fold_in(key, 0), shape, _jnp.float32)
    for i, name in enumerate(TWIN_WEIGHTS):
        w = inp[name].astype(_jnp.float32)
        if MOMENT_SCALE is None:
            s = _jnp.sqrt(_jnp.mean(_jnp.square(w)) + 1e-30)
        else:
            s = MOMENT_SCALE[name]
        km, kv = _jax.random.split(_jax.random.fold_in(key, i + 1))
        out[name] = w
        out["m_" + name] = s * _jax.random.normal(km, w.shape, _jnp.float32)
        out["v_" + name] = (s * s) * _jax.random.uniform(kv, w.shape, _jnp.float32, 0.5, 1.5)
    if N_MICROBATCH > 1:
        for name, axis in PER_EXAMPLE_BATCH_AXIS.items():
            out[name] = _to_microbatches(out[name], axis)
    return {'x': out['x'], 'conv_w_in': out['conv_w_in'], 'conv_b_in': out['conv_b_in'], 'conv_w_dw': out['conv_w_dw'], 'conv_b_dw': out['conv_b_dw'], 'conv_ln_g': out['conv_ln_g'], 'conv_ln_b': out['conv_ln_b'], 'conv_w_out': out['conv_w_out'], 'conv_b_out': out['conv_b_out'], 'gmlp_w_in': out['gmlp_w_in'], 'gmlp_b_in': out['gmlp_b_in'], 'gmlp_ln_g': out['gmlp_ln_g'], 'gmlp_ln_b': out['gmlp_ln_b'], 'gmlp_w_s': out['gmlp_w_s'], 'gmlp_b_s': out['gmlp_b_s'], 'gmlp_w_out': out['gmlp_w_out'], 'gmlp_b_out': out['gmlp_b_out'], 'ffn_w_up': out['ffn_w_up'], 'ffn_b_up': out['ffn_b_up'], 'ffn_w_dw': out['ffn_w_dw'], 'ffn_b_dw': out['ffn_b_dw'], 'ffn_w_down': out['ffn_w_down'], 'ffn_b_down': out['ffn_b_down'], 'norm1_g': out['norm1_g'], 'norm1_b': out['norm1_b'], 'norm2_g': out['norm2_g'], 'norm2_b': out['norm2_b'], 'loss_target': out['loss_target'], 'm_conv_w_in': out['m_conv_w_in'], 'm_conv_b_in': out['m_conv_b_in'], 'm_conv_w_dw': out['m_conv_w_dw'], 'm_conv_b_dw': out['m_conv_b_dw'], 'm_conv_ln_g': out['m_conv_ln_g'], 'm_conv_ln_b': out['m_conv_ln_b'], 'm_conv_w_out': out['m_conv_w_out'], 'm_conv_b_out': out['m_conv_b_out'], 'm_gmlp_w_in': out['m_gmlp_w_in'], 'm_gmlp_b_in': out['m_gmlp_b_in'], 'm_gmlp_ln_g': out['m_gmlp_ln_g'], 'm_gmlp_ln_b': out['m_gmlp_ln_b'], 'm_gmlp_w_s': out['m_gmlp_w_s'], 'm_gmlp_b_s': out['m_gmlp_b_s'], 'm_gmlp_w_out': out['m_gmlp_w_out'], 'm_gmlp_b_out': out['m_gmlp_b_out'], 'm_ffn_w_up': out['m_ffn_w_up'], 'm_ffn_b_up': out['m_ffn_b_up'], 'm_ffn_w_dw': out['m_ffn_w_dw'], 'm_ffn_b_dw': out['m_ffn_b_dw'], 'm_ffn_w_down': out['m_ffn_w_down'], 'm_ffn_b_down': out['m_ffn_b_down'], 'm_norm1_g': out['m_norm1_g'], 'm_norm1_b': out['m_norm1_b'], 'm_norm2_g': out['m_norm2_g'], 'm_norm2_b': out['m_norm2_b'], 'v_conv_w_in': out['v_conv_w_in'], 'v_conv_b_in': out['v_conv_b_in'], 'v_conv_w_dw': out['v_conv_w_dw'], 'v_conv_b_dw': out['v_conv_b_dw'], 'v_conv_ln_g': out['v_conv_ln_g'], 'v_conv_ln_b': out['v_conv_ln_b'], 'v_conv_w_out': out['v_conv_w_out'], 'v_conv_b_out': out['v_conv_b_out'], 'v_gmlp_w_in': out['v_gmlp_w_in'], 'v_gmlp_b_in': out['v_gmlp_b_in'], 'v_gmlp_ln_g': out['v_gmlp_ln_g'], 'v_gmlp_ln_b': out['v_gmlp_ln_b'], 'v_gmlp_w_s': out['v_gmlp_w_s'], 'v_gmlp_b_s': out['v_gmlp_b_s'], 'v_gmlp_w_out': out['v_gmlp_w_out'], 'v_gmlp_b_out': out['v_gmlp_b_out'], 'v_ffn_w_up': out['v_ffn_w_up'], 'v_ffn_b_up': out['v_ffn_b_up'], 'v_ffn_w_dw': out['v_ffn_w_dw'], 'v_ffn_b_dw': out['v_ffn_b_dw'], 'v_ffn_w_down': out['v_ffn_w_down'], 'v_ffn_b_down': out['v_ffn_b_down'], 'v_norm1_g': out['v_norm1_g'], 'v_norm1_b': out['v_norm1_b'], 'v_norm2_g': out['v_norm2_g'], 'v_norm2_b': out['v_norm2_b']}


def _loss(weights, diff, rest, loss_target):
    with _jax.named_scope("forward"):
        args = {**rest, TWIN_DIFF_INPUT: diff, **{k: w.astype(_WEIGHT_DTYPES[k]) for k, w in weights.items()}}
        y = _forward(args)
    with _jax.named_scope("loss_head"):
        err = _jnp.square(y.astype(_jnp.float32) - loss_target)
        return 0.5 * _jnp.sum(_jnp.mean(err, axis=-1)) if err.ndim else 0.5 * err


def _adamw(w, g, m, v):
    m = ADAM_B1 * m + (1.0 - ADAM_B1) * g
    v = ADAM_B2 * v + (1.0 - ADAM_B2) * _jnp.square(g)
    m_hat = m / (1.0 - ADAM_B1 ** ADAM_STEP)
    v_hat = v / (1.0 - ADAM_B2 ** ADAM_STEP)
    delta = -ADAM_LR * (m_hat / (_jnp.sqrt(v_hat) + ADAM_EPS) + ADAM_WD * w)
    return delta, m, v


def reference(x, conv_w_in, conv_b_in, conv_w_dw, conv_b_dw, conv_ln_g, conv_ln_b, conv_w_out, conv_b_out, gmlp_w_in, gmlp_b_in, gmlp_ln_g, gmlp_ln_b, gmlp_w_s, gmlp_b_s, gmlp_w_out, gmlp_b_out, ffn_w_up, ffn_b_up, ffn_w_dw, ffn_b_dw, ffn_w_down, ffn_b_down, norm1_g, norm1_b, norm2_g, norm2_b, loss_target, m_conv_w_in, m_conv_b_in, m_conv_w_dw, m_conv_b_dw, m_conv_ln_g, m_conv_ln_b, m_conv_w_out, m_conv_b_out, m_gmlp_w_in, m_gmlp_b_in, m_gmlp_ln_g, m_gmlp_ln_b, m_gmlp_w_s, m_gmlp_b_s, m_gmlp_w_out, m_gmlp_b_out, m_ffn_w_up, m_ffn_b_up, m_ffn_w_dw, m_ffn_b_dw, m_ffn_w_down, m_ffn_b_down, m_norm1_g, m_norm1_b, m_norm2_g, m_norm2_b, v_conv_w_in, v_conv_b_in, v_conv_w_dw, v_conv_b_dw, v_conv_ln_g, v_conv_ln_b, v_conv_w_out, v_conv_b_out, v_gmlp_w_in, v_gmlp_b_in, v_gmlp_ln_g, v_gmlp_ln_b, v_gmlp_w_s, v_gmlp_b_s, v_gmlp_w_out, v_gmlp_b_out, v_ffn_w_up, v_ffn_b_up, v_ffn_w_dw, v_ffn_b_dw, v_ffn_w_down, v_ffn_b_down, v_norm1_g, v_norm1_b, v_norm2_g, v_norm2_b):
    given = dict(x=x, conv_w_in=conv_w_in, conv_b_in=conv_b_in, conv_w_dw=conv_w_dw, conv_b_dw=conv_b_dw, conv_ln_g=conv_ln_g, conv_ln_b=conv_ln_b, conv_w_out=conv_w_out, conv_b_out=conv_b_out, gmlp_w_in=gmlp_w_in, gmlp_b_in=gmlp_b_in, gmlp_ln_g=gmlp_ln_g, gmlp_ln_b=gmlp_ln_b, gmlp_w_s=gmlp_w_s, gmlp_b_s=gmlp_b_s, gmlp_w_out=gmlp_w_out, gmlp_b_out=gmlp_b_out, ffn_w_up=ffn_w_up, ffn_b_up=ffn_b_up, ffn_w_dw=ffn_w_dw, ffn_b_dw=ffn_b_dw, ffn_w_down=ffn_w_down, ffn_b_down=ffn_b_down, norm1_g=norm1_g, norm1_b=norm1_b, norm2_g=norm2_g, norm2_b=norm2_b, loss_target=loss_target, m_conv_w_in=m_conv_w_in, m_conv_b_in=m_conv_b_in, m_conv_w_dw=m_conv_w_dw, m_conv_b_dw=m_conv_b_dw, m_conv_ln_g=m_conv_ln_g, m_conv_ln_b=m_conv_ln_b, m_conv_w_out=m_conv_w_out, m_conv_b_out=m_conv_b_out, m_gmlp_w_in=m_gmlp_w_in, m_gmlp_b_in=m_gmlp_b_in, m_gmlp_ln_g=m_gmlp_ln_g, m_gmlp_ln_b=m_gmlp_ln_b, m_gmlp_w_s=m_gmlp_w_s, m_gmlp_b_s=m_gmlp_b_s, m_gmlp_w_out=m_gmlp_w_out, m_gmlp_b_out=m_gmlp_b_out, m_ffn_w_up=m_ffn_w_up, m_ffn_b_up=m_ffn_b_up, m_ffn_w_dw=m_ffn_w_dw, m_ffn_b_dw=m_ffn_b_dw, m_ffn_w_down=m_ffn_w_down, m_ffn_b_down=m_ffn_b_down, m_norm1_g=m_norm1_g, m_norm1_b=m_norm1_b, m_norm2_g=m_norm2_g, m_norm2_b=m_norm2_b, v_conv_w_in=v_conv_w_in, v_conv_b_in=v_conv_b_in, v_conv_w_dw=v_conv_w_dw, v_conv_b_dw=v_conv_b_dw, v_conv_ln_g=v_conv_ln_g, v_conv_ln_b=v_conv_ln_b, v_conv_w_out=v_conv_w_out, v_conv_b_out=v_conv_b_out, v_gmlp_w_in=v_gmlp_w_in, v_gmlp_b_in=v_gmlp_b_in, v_gmlp_ln_g=v_gmlp_ln_g, v_gmlp_ln_b=v_gmlp_ln_b, v_gmlp_w_s=v_gmlp_w_s, v_gmlp_b_s=v_gmlp_b_s, v_gmlp_w_out=v_gmlp_w_out, v_gmlp_b_out=v_gmlp_b_out, v_ffn_w_up=v_ffn_w_up, v_ffn_b_up=v_ffn_b_up, v_ffn_w_dw=v_ffn_w_dw, v_ffn_b_dw=v_ffn_b_dw, v_ffn_w_down=v_ffn_w_down, v_ffn_b_down=v_ffn_b_down, v_norm1_g=v_norm1_g, v_norm1_b=v_norm1_b, v_norm2_g=v_norm2_g, v_norm2_b=v_norm2_b)
    weights = {n: given[n] for n in TWIN_WEIGHTS}
    shared = {n: given[n] for n in SHARED_INPUTS}
    per_example = {n: given[n] for n in ['x']}
    grad_fn = _jax.value_and_grad(_loss, argnums=(0, 1))

    def one_microbatch(ex, loss_target):
        ex = dict(ex)
        diff = ex.pop(TWIN_DIFF_INPUT)
        return grad_fn(weights, diff, {**shared, **ex}, loss_target)

    if N_MICROBATCH == 1:
        loss, (grad_w, grad_x) = one_microbatch(per_example, given["loss_target"])
    else:
        def body(carry, xs):
            loss_sum, grad_sum = carry
            l_k, (gw_k, gx_k) = one_microbatch(xs[0], xs[1])
            with _jax.named_scope("update"):
                return (loss_sum + l_k, _jax.tree.map(_jnp.add, grad_sum, gw_k)), gx_k

        init = (_jnp.zeros((), _jnp.float32), _jax.tree.map(_jnp.zeros_like, weights))
        (loss, grad_w), grad_x = _jax.lax.scan(body, init, (per_example, given["loss_target"]))
    with _jax.named_scope("update"):
        delta_w, new_m, new_v = {}, {}, {}
        for n in TWIN_WEIGHTS:
            delta_w[n], new_m[n], new_v[n] = _adamw(weights[n], grad_w[n], given["m_" + n], given["v_" + n])
    return (loss, grad_x, *[grad_w[n] for n in TWIN_WEIGHTS], *[delta_w[n] for n in TWIN_WEIGHTS],
            *[new_m[n] for n in TWIN_WEIGHTS], *[new_v[n] for n in TWIN_WEIGHTS])
```

```python
import math

import jax
import jax.numpy as jnp
from jax import lax
from jax.experimental import pallas as pl
from jax.experimental.pallas import tpu as pltpu

F32 = jnp.float32
BF16 = jnp.bfloat16
MESH = pl.DeviceIdType.MESH

N_DEV = 8
N_CHIP = 4
DEPTH = 4
ALPHA = (2.0 * DEPTH) ** 0.25
LN_EPS = 1e-5
CONV_K = 31
FFN_K = 3
CHUNK = 128
GROUPS = 8
ADAM_LR = 0.001
ADAM_B1 = 0.9
ADAM_B2 = 0.999
ADAM_EPS = 1e-08
ADAM_WD = 0.01
ADAM_STEP = 10
INV_SQRT2 = 1.0 / math.sqrt(2.0)
INV_SQRT2PI = 1.0 / math.sqrt(2.0 * math.pi)

LANES = 128
SUBLANES = 8
VMEM_LIMIT = 48 * 1024 * 1024
TM_MM = 1024
TM_EW = 256
HALO31 = 32
HALO3 = 8


def _call(body, name, grid, in_specs, out_specs, out_shape, scratch=(), aliases=None):
    return pl.pallas_call(
        body, name=name, grid=grid, in_specs=in_specs, out_specs=out_specs, out_shape=out_shape,
        scratch_shapes=list(scratch), input_output_aliases=aliases or {},
        compiler_params=pltpu.CompilerParams(vmem_limit_bytes=VMEM_LIMIT))


def _sds(shape, dtype):
    return jax.ShapeDtypeStruct(tuple(shape), dtype)


def _sigmoid(x):
    return 1.0 / (1.0 + jnp.exp(-x))


def _acc_rows(ref, val, first):
    @pl.when(first)
    def _():
        ref[...] = val

    @pl.when(jnp.logical_not(first))
    def _():
        ref[...] += val


def _colsum(v):
    return jnp.sum(v, axis=0, keepdims=True)


_DIMS = {"nn": ((1,), (0,)), "nt": ((1,), (1,)), "tn": ((0,), (0,))}


def _matmul(name, a, b, mode, *, grid, a_spec, b_spec, o_spec, o_shape, o_dtype, k_axis=None, nk=1,
            acc_shape=None, bias=None, bias_spec=None, res=None, res_spec=None, res_scale=1.0,
            prev=None):
    dims = (_DIMS[mode], ((), ()))
    has_bias, has_res, has_prev = bias is not None, res is not None, prev is not None

    def body(*refs):
        a_ref, b_ref = refs[0], refs[1]
        pos = 2
        bias_ref = res_ref = None
        if has_bias:
            bias_ref = refs[pos]
            pos += 1
        if has_res:
            res_ref = refs[pos]
            pos += 1
        if has_prev:
            pos += 1
        o_ref = refs[pos]
        acc_ref = refs[pos + 1] if nk > 1 else None
        p = lax.dot_general(a_ref[...].astype(BF16), b_ref[...].astype(BF16), dims, preferred_element_type=F32)

        def finish(acc):
            if has_bias:
                acc = acc + bias_ref[...]
            if has_res:
                acc = acc + res_scale * res_ref[...]
            o_ref[...] = acc.astype(o_dtype)

        if nk == 1:
            finish(p)
        else:
            k = pl.program_id(k_axis)

            @pl.when(k == 0)
            def _():
                acc_ref[...] = p

            @pl.when(k > 0)
            def _():
                acc_ref[...] += p

            @pl.when(k == nk - 1)
            def _():
                finish(acc_ref[...])

    ins, specs = [a, b], [a_spec, b_spec]
    if has_bias:
        ins.append(bias)
        specs.append(bias_spec)
    if has_res:
        ins.append(res)
        specs.append(res_spec)
    aliases = None
    if has_prev:
        aliases = {len(ins): 0}
        ins.append(prev)
        specs.append(pl.BlockSpec(memory_space=pl.ANY))
    scratch = [pltpu.VMEM(acc_shape, F32)] if nk > 1 else []
    return _call(body, name, grid, specs, o_spec, _sds(o_shape, o_dtype), scratch, aliases)(*ins)


def _mesh_pos():
    return lax.axis_index("x"), lax.axis_index("y"), lax.axis_index("c")


def _any_specs(n):
    return [pl.BlockSpec(memory_space=pl.ANY)] * n


def _all_gather(name, srcs):
    n = len(srcs)

    def body(*refs):
        src, out = refs[:n], refs[n:2 * n]
        send_sems, recv_sems, local_sems = refs[2 * n:]
        x, y, c = _mesh_pos()
        me, sibling = (x, y, c), (x, y, 1 - c)
        chips = [(1 - x, y), (x, 1 - y), (1 - x, 1 - y)]

        def slot(k, p):
            return out[k].at[:, 4 * p[0] + 2 * p[1] + p[2]]

        def copy(k, idx, block, to, s=None):
            return pltpu.make_async_remote_copy(
                src_ref=slot(k, block) if s is None else s, dst_ref=slot(k, block),
                send_sem=send_sems.at[k * 7 + idx], recv_sem=recv_sems.at[k * 7 + idx],
                device_id=to, device_id_type=MESH)

        local = [pltpu.make_async_copy(src[k], slot(k, me), local_sems.at[k]) for k in range(n)]
        for cp in local:
            cp.start()
        first = []
        for k in range(n):
            first.append(copy(k, 0, me, sibling, src[k]))
            for j, chip in enumerate(chips):
                first.append(copy(k, 1 + j, me, (*chip, c), src[k]))
        for cp in first:
            cp.start()
        passed = []
        for j, chip in enumerate(chips):
            for k in range(n):
                copy(k, 1 + j, (*chip, c), me).wait_recv()
                cp = copy(k, 4 + j, (*chip, c), sibling)
                cp.start()
                passed.append(cp)
        for k in range(n):
            copy(k, 0, sibling, me).wait_recv()
            for j, chip in enumerate(chips):
                copy(k, 4 + j, (*chip, 1 - c), me).wait_recv()
        for cp in first + passed:
            cp.wait_send()
        for cp in local:
            cp.wait()

    out_shape = [_sds((s.shape[0], N_DEV) + s.shape[1:], s.dtype) for s in srcs]
    return _call(body, name, (), _any_specs(n), _any_specs(n), out_shape,
                 [pltpu.SemaphoreType.DMA((7 * n,)), pltpu.SemaphoreType.DMA((7 * n,)),
                  pltpu.SemaphoreType.DMA((n,))])(*srcs)


def _sibling_swap(name, grads):
    n = len(grads)

    def body(*refs):
        g, mine, theirs = refs[:n], refs[n:2 * n], refs[2 * n:3 * n]
        send_sems, recv_sems, local_sems = refs[3 * n:]
        x, y, c = _mesh_pos()
        swaps, keeps = [], []
        for k in range(n):
            swaps.append(pltpu.make_async_remote_copy(
                src_ref=g[k].at[:, :, 1 - c], dst_ref=theirs[k], send_sem=send_sems.at[k], recv_sem=recv_sems.at[k],
                device_id=(x, y, 1 - c), device_id_type=MESH))
            keeps.append(pltpu.make_async_copy(g[k].at[:, :, c], mine[k], local_sems.at[k]))
        for cp in swaps + keeps:
            cp.start()
        for cp in swaps + keeps:
            cp.wait()

    half = [_sds((g.shape[0], N_CHIP) + g.shape[3:], g.dtype) for g in grads]
    outs = _call(body, name, (), _any_specs(n), _any_specs(2 * n), half + half,
                 [pltpu.SemaphoreType.DMA((n,)), pltpu.SemaphoreType.DMA((n,)), pltpu.SemaphoreType.DMA((n,))])(*grads)
    return outs[:n], outs[n:]


def _chip_exchange(name, pairs):
    n = len(pairs)

    def body(*refs):
        p, q = refs[:n], refs[n:2 * n]
        send_sems, recv_sems, local_sems = refs[2 * n:]
        x, y, c = _mesh_pos()
        my_chip = 2 * x + y
        chips = [(1 - x, y), (x, 1 - y), (1 - x, 1 - y)]
        sends, keeps = [], []
        for k in range(n):
            keeps.append(pltpu.make_async_copy(p[k].at[:, my_chip], q[k].at[my_chip], local_sems.at[k]))
            for j, chip in enumerate(chips):
                sends.append(pltpu.make_async_remote_copy(
                    src_ref=p[k].at[:, 2 * chip[0] + chip[1]], dst_ref=q[k].at[my_chip],
                    send_sem=send_sems.at[3 * k + j], recv_sem=recv_sems.at[3 * k + j],
                    device_id=(*chip, c), device_id_type=MESH))
        for cp in keeps + sends:
            cp.start()
        for k in range(n):
            for j, chip in enumerate(chips):
                src_chip = 2 * chip[0] + chip[1]
                pltpu.make_async_remote_copy(
                    src_ref=q[k].at[src_chip], dst_ref=q[k].at[src_chip],
                    send_sem=send_sems.at[3 * k + j], recv_sem=recv_sems.at[3 * k + j],
                    device_id=(*chip, c), device_id_type=MESH).wait_recv()
        for cp in sends:
            cp.wait_send()
        for cp in keeps:
            cp.wait()

    out_shape = [_sds((N_CHIP, s.shape[0]) + s.shape[2:], s.dtype) for s in pairs]
    return _call(body, name, (), _any_specs(n), _any_specs(n), out_shape,
                 [pltpu.SemaphoreType.DMA((3 * n,)), pltpu.SemaphoreType.DMA((3 * n,)),
                  pltpu.SemaphoreType.DMA((n,))])(*pairs)


def _row_chunk(width):
    lanes = -(-width // LANES)
    return max(SUBLANES, (8 // lanes) * SUBLANES) if lanes <= 8 else SUBLANES


def _conv_taps(ext_ref, w_ref, k_taps, base, rows, forward, init):
    acc = init
    for k in range(k_taps):
        d = k_taps - 1 - k
        off = base - d if forward else base + d
        acc = acc + w_ref[k:k + 1, :] * ext_ref[off:off + rows, :]
    return acc


def _ln_stats(v):
    mu = jnp.mean(v, axis=-1, keepdims=True)
    vc = v - mu
    var = jnp.mean(vc * vc, axis=-1, keepdims=True)
    rstd = lax.rsqrt(var + LN_EPS)
    return vc * rstd, rstd


def _ln_backward(dxhat, xhat, rstd):
    m1 = jnp.mean(dxhat, axis=-1, keepdims=True)
    m2 = jnp.mean(dxhat * xhat, axis=-1, keepdims=True)
    return rstd * (dxhat - m1 - xhat * m2)


def _row_spec(tm, width):
    return pl.BlockSpec((tm, width), lambda i: (i, 0))


def _param_spec(l, width):
    return pl.BlockSpec((None, 1, width), lambda *_: (l, 0, 0))


def _ln_res(name, x, y, g3, b3, l):
    t, d = x.shape
    tm = min(TM_EW, t)

    def body(x_ref, y_ref, g_ref, b_ref, o_ref, xh_ref, rs_ref):
        xhat, rstd = _ln_stats(ALPHA * x_ref[...] + y_ref[...])
        o_ref[...] = xhat * g_ref[...] + b_ref[...]
        xh_ref[...] = xhat
        rs_ref[...] = rstd

    return _call(body, name, (t // tm,),
                 [_row_spec(tm, d), _row_spec(tm, d), _param_spec(l, d), _param_spec(l, d)],
                 [_row_spec(tm, d), _row_spec(tm, d), _row_spec(tm, 1)],
                 [_sds((t, d), F32), _sds((t, d), F32), _sds((t, 1), F32)])(x, y, g3, b3)


def _ln_res_bwd(name, dout, xhat, rstd, g3, l):
    t, d = dout.shape
    tm = min(TM_EW, t)

    def body(do_ref, xh_ref, rs_ref, g_ref, dr_ref, dg_ref, db_ref, dc_ref):
        first = pl.program_id(0) == 0
        do, xhat = do_ref[...], xh_ref[...]
        dr = _ln_backward(do * g_ref[...], xhat, rs_ref[...])
        dr_ref[...] = dr
        _acc_rows(dg_ref, _colsum(do * xhat), first)
        _acc_rows(db_ref, _colsum(do), first)
        _acc_rows(dc_ref, _colsum(dr), first)

    vec = pl.BlockSpec((1, d), lambda i: (0, 0))
    return _call(body, name, (t // tm,),
                 [_row_spec(tm, d), _row_spec(tm, d), _row_spec(tm, 1), _param_spec(l, d)],
                 [_row_spec(tm, d), vec, vec, vec],
                 [_sds((t, d), F32)] + [_sds((1, d), F32)] * 3)(dout, xhat, rstd, g3)


def _glu(name, h):
    t, c2 = h.shape
    c = c2 // 2
    tm = min(TM_EW, t)

    def body(a_ref, g_ref, o_ref):
        o_ref[...] = a_ref[...] * _sigmoid(g_ref[...])

    return _call(body, name, (t // tm,),
                 [pl.BlockSpec((tm, c), lambda i: (i, 0)), pl.BlockSpec((tm, c), lambda i: (i, 1))],
                 _row_spec(tm, c), _sds((t, c), F32))(h, h)


def _glu_bwd(name, du, h):
    t, c2 = h.shape
    c = c2 // 2
    tm = min(TM_EW, t)

    def body(du_ref, a_ref, g_ref, dh_ref, db_ref):
        first = pl.program_id(0) == 0
        du_v, a = du_ref[...], a_ref[...]
        sg = _sigmoid(g_ref[...])
        da = du_v * sg
        dg = du_v * a * sg * (1.0 - sg)
        dh_ref[:, :c] = da.astype(BF16)
        dh_ref[:, c:] = dg.astype(BF16)
        _acc_rows(db_ref.at[:, :c], _colsum(da), first)
        _acc_rows(db_ref.at[:, c:], _colsum(dg), first)

    return _call(body, name, (t // tm,),
                 [_row_spec(tm, c), pl.BlockSpec((tm, c), lambda i: (i, 0)), pl.BlockSpec((tm, c), lambda i: (i, 1))],
                 [_row_spec(tm, c2), pl.BlockSpec((1, c2), lambda i: (0, 0))],
                 [_sds((t, c2), BF16), _sds((1, c2), F32)])(du, h, h)


def _halo_specs(tm, cb, halo, seq_tiles, n_tiles):
    per = tm // halo
    tile = pl.BlockSpec((tm, cb), lambda n, i: (i, n))
    prev = pl.BlockSpec((halo, cb), lambda n, i: (jnp.maximum(i * per - 1, 0), n))
    nxt = pl.BlockSpec((halo, cb), lambda n, i: (jnp.minimum((i + 1) * per, n_tiles * per - 1), n))
    return tile, prev, nxt


def _dwconv31(name, u, w3, b3, l, seq):
    t, c = u.shape
    tm, cb = min(TM_EW, seq), 256
    seq_tiles, n_tiles = seq // tm, t // tm
    rc = _row_chunk(cb)
    tile, prev, _ = _halo_specs(tm, cb, HALO31, seq_tiles, n_tiles)

    def body(u_ref, halo_ref, w_ref, b_ref, o_ref, ext_ref):
        i = pl.program_id(1)
        keep = (i % seq_tiles != 0).astype(F32)
        ext_ref[0:HALO31, :] = halo_ref[...] * keep
        ext_ref[HALO31:HALO31 + tm, :] = u_ref[...]
        for r0 in range(0, tm, rc):
            init = jnp.broadcast_to(b_ref[...], (rc, cb))
            o_ref[r0:r0 + rc, :] = _conv_taps(ext_ref, w_ref, CONV_K, HALO31 + r0, rc, True, init)

    return _call(body, name, (c // cb, n_tiles),
                 [tile, prev, pl.BlockSpec((None, CONV_K, cb), lambda n, i: (l, 0, n)),
                  pl.BlockSpec((None, 1, cb), lambda n, i: (l, 0, n))],
                 tile, _sds((t, c), F32), [pltpu.VMEM((HALO31 + tm, cb), F32)])(u, u, w3, b3)


def _dwconv31_bwd(name, dc, u, w3, l, seq):
    t, c = dc.shape
    tm, cb = min(TM_EW, seq), 256
    seq_tiles, n_tiles = seq // tm, t // tm
    rc = _row_chunk(cb)
    tile, prev, nxt = _halo_specs(tm, cb, HALO31, seq_tiles, n_tiles)

    def body(dc_ref, dcn_ref, u_ref, up_ref, w_ref, du_ref, dw_ref, dext_ref, uext_ref):
        i = pl.program_id(1)
        keep_prev = (i % seq_tiles != 0).astype(F32)
        keep_next = (i % seq_tiles != seq_tiles - 1).astype(F32)
        dext_ref[0:tm, :] = dc_ref[...]
        dext_ref[tm:tm + HALO31, :] = dcn_ref[...] * keep_next
        uext_ref[0:HALO31, :] = up_ref[...] * keep_prev
        uext_ref[HALO31:HALO31 + tm, :] = u_ref[...]
        for r0 in range(0, tm, rc):
            du_ref[r0:r0 + rc, :] = _conv_taps(dext_ref, w_ref, CONV_K, r0, rc, False, jnp.zeros((rc, cb), F32))

        @pl.when(i == 0)
        def _():
            dw_ref[...] = jnp.zeros_like(dw_ref)

        for k in range(CONV_K):
            d = CONV_K - 1 - k
            part = jnp.zeros((rc, cb), F32)
            for r0 in range(0, tm, rc):
                part = part + dc_ref[r0:r0 + rc, :] * uext_ref[HALO31 - d + r0:HALO31 - d + r0 + rc, :]
            dw_ref[k:k + 1, :] += _colsum(part)

    return _call(body, name, (c // cb, n_tiles),
                 [tile, nxt, tile, prev, pl.BlockSpec((None, CONV_K, cb), lambda n, i: (l, 0, n))],
                 [tile, pl.BlockSpec((HALO31, cb), lambda n, i: (0, n))],
                 [_sds((t, c), F32), _sds((HALO31, c), F32)],
                 [pltpu.VMEM((tm + HALO31, cb), F32), pltpu.VMEM((HALO31 + tm, cb), F32)])(dc, dc, u, u, w3)


def _ln_silu(name, cx, g3, b3, l):
    t, d = cx.shape
    tm = min(TM_EW, t)

    def body(c_ref, g_ref, b_ref, o_ref):
        xhat, _ = _ln_stats(c_ref[...])
        nv = xhat * g_ref[...] + b_ref[...]
        o_ref[...] = (nv * _sigmoid(nv)).astype(BF16)

    return _call(body, name, (t // tm,), [_row_spec(tm, d), _param_spec(l, d), _param_spec(l, d)],
                 _row_spec(tm, d), _sds((t, d), BF16))(cx, g3, b3)


def _ln_silu_bwd(name, ds, cx, g3, b3, l):
    t, d = cx.shape
    tm = min(TM_EW, t)

    def body(ds_ref, c_ref, g_ref, b_ref, dc_ref, dg_ref, db_ref, dsum_ref):
        first = pl.program_id(0) == 0
        xhat, rstd = _ln_stats(c_ref[...])
        g = g_ref[...]
        nv = xhat * g + b_ref[...]
        sg = _sigmoid(nv)
        dn = ds_ref[...] * (sg * (1.0 + nv * (1.0 - sg)))
        dc = _ln_backward(dn * g, xhat, rstd)
        dc_ref[...] = dc
        _acc_rows(dg_ref, _colsum(dn * xhat), first)
        _acc_rows(db_ref, _colsum(dn), first)
        _acc_rows(dsum_ref, _colsum(dc), first)

    vec = pl.BlockSpec((1, d), lambda i: (0, 0))
    return _call(body, name, (t // tm,),
                 [_row_spec(tm, d), _row_spec(tm, d), _param_spec(l, d), _param_spec(l, d)],
                 [_row_spec(tm, d), vec, vec, vec],
                 [_sds((t, d), F32)] + [_sds((1, d), F32)] * 3)(ds, cx, g3, b3)


def _blk_specs(tm, fb, halo, n_tiles, shift):
    per = tm // halo
    tile = pl.BlockSpec((None, tm, fb), lambda n, i: (n + shift, i, 0))
    prev = pl.BlockSpec((None, halo, fb), lambda n, i: (n + shift, jnp.maximum(i * per - 1, 0), 0))
    nxt = pl.BlockSpec((None, halo, fb), lambda n, i: (n + shift, jnp.minimum((i + 1) * per, n_tiles * per - 1), 0))
    return tile, prev, nxt


def _ffn_conv(ext_ref, halo_ref, x_ref, w_ref, b_ref, keep, tm, fb, rc, out_ref):
    ext_ref[0:HALO3, :] = halo_ref[...] * keep
    ext_ref[HALO3:HALO3 + tm, :] = x_ref[...]
    for r0 in range(0, tm, rc):
        init = jnp.broadcast_to(b_ref[...], (rc, fb))
        out_ref[r0:r0 + rc, :] = _conv_taps(ext_ref, w_ref, FFN_K, HALO3 + r0, rc, True, init)


def _ffn_act(name, h, wdw, bdw, l, seq):
    nb, t, fb = h.shape
    half = nb // 2
    tm = min(TM_EW, seq)
    seq_tiles, n_tiles = seq // tm, t // tm
    rc = _row_chunk(fb)
    g_tile, g_prev, _ = _blk_specs(tm, fb, HALO3, n_tiles, 0)
    v_tile, v_prev, _ = _blk_specs(tm, fb, HALO3, n_tiles, half)

    def body(g_ref, gp_ref, v_ref, vp_ref, wg_ref, wv_ref, bg_ref, bv_ref, o_ref, ext_ref, cg_ref, cv_ref):
        keep = (pl.program_id(1) % seq_tiles != 0).astype(F32)
        _ffn_conv(ext_ref, gp_ref, g_ref, wg_ref, bg_ref, keep, tm, fb, rc, cg_ref)
        _ffn_conv(ext_ref, vp_ref, v_ref, wv_ref, bv_ref, keep, tm, fb, rc, cv_ref)
        cg = cg_ref[...]
        o_ref[...] = (cg * _sigmoid(cg) * cv_ref[...]).astype(BF16)

    def wspec(shift, rows):
        return pl.BlockSpec((None, None, rows, fb), lambda n, i: (l, n + shift, 0, 0))

    return _call(body, name, (half, n_tiles),
                 [g_tile, g_prev, v_tile, v_prev, wspec(0, FFN_K), wspec(half, FFN_K), wspec(0, 1), wspec(half, 1)],
                 pl.BlockSpec((None, tm, fb), lambda n, i: (n, i, 0)), _sds((half, t, fb), BF16),
                 [pltpu.VMEM((HALO3 + tm, fb), F32), pltpu.VMEM((tm, fb), F32), pltpu.VMEM((tm, fb), F32)]
                 )(h, h, h, h, wdw, wdw, bdw, bdw)


def _ffn_act_bwd(name, da, h, wdw, bdw, l, seq):
    nb, t, fb = h.shape
    half = nb // 2
    tm = min(TM_EW, seq)
    seq_tiles, n_tiles = seq // tm, t // tm
    rc = _row_chunk(fb)
    g_tile, g_prev, _ = _blk_specs(tm, fb, HALO3, n_tiles, 0)
    v_tile, v_prev, _ = _blk_specs(tm, fb, HALO3, n_tiles, half)

    def body(da_ref, g_ref, gp_ref, v_ref, vp_ref, wg_ref, wv_ref, bg_ref, bv_ref,
             dg_ref, dv_ref, dbg_ref, dbv_ref, dwg_ref, dwv_ref, gext_ref, vext_ref, cg_ref, cv_ref):
        i = pl.program_id(1)
        first = i == 0
        keep = (i % seq_tiles != 0).astype(F32)
        _ffn_conv(gext_ref, gp_ref, g_ref, wg_ref, bg_ref, keep, tm, fb, rc, cg_ref)
        _ffn_conv(vext_ref, vp_ref, v_ref, wv_ref, bv_ref, keep, tm, fb, rc, cv_ref)
        cg, cv, da_v = cg_ref[...], cv_ref[...], da_ref[...]
        sg = _sigmoid(cg)
        dcv = da_v * cg * sg
        dcg = da_v * cv * sg * (1.0 + cg * (1.0 - sg))
        dg_ref[...] = dcg
        dv_ref[...] = dcv
        _acc_rows(dbg_ref, _colsum(dcg), first)
        _acc_rows(dbv_ref, _colsum(dcv), first)

        @pl.when(first)
        def _():
            dwg_ref[...] = jnp.zeros_like(dwg_ref)
            dwv_ref[...] = jnp.zeros_like(dwv_ref)

        for k in range(FFN_K):
            d = FFN_K - 1 - k
            dwg_ref[k:k + 1, :] += _colsum(dcg * gext_ref[HALO3 - d:HALO3 - d + tm, :])
            dwv_ref[k:k + 1, :] += _colsum(dcv * vext_ref[HALO3 - d:HALO3 - d + tm, :])

    def wspec(shift, rows):
        return pl.BlockSpec((None, None, rows, fb), lambda n, i: (l, n + shift, 0, 0))

    ext = pltpu.VMEM((HALO3 + tm, fb), F32)
    tmp = pltpu.VMEM((tm, fb), F32)
    outs = _call(body, name, (half, n_tiles),
                 [pl.BlockSpec((None, tm, fb), lambda n, i: (n, i, 0)), g_tile, g_prev, v_tile, v_prev,
                  wspec(0, FFN_K), wspec(half, FFN_K), wspec(0, 1), wspec(half, 1)],
                 [pl.BlockSpec((None, tm, fb), lambda n, i: (n, i, 0)),
                  pl.BlockSpec((None, tm, fb), lambda n, i: (n, i, 0)),
                  pl.BlockSpec((None, 1, fb), lambda n, i: (n, 0, 0)), pl.BlockSpec((None, 1, fb), lambda n, i: (n, 0, 0)),
                  pl.BlockSpec((None, SUBLANES, fb), lambda n, i: (n, 0, 0)),
                  pl.BlockSpec((None, SUBLANES, fb), lambda n, i: (n, 0, 0))],
                 [_sds((half, t, fb), F32), _sds((half, t, fb), F32), _sds((half, 1, fb), F32),
                  _sds((half, 1, fb), F32), _sds((half, SUBLANES, fb), F32), _sds((half, SUBLANES, fb), F32)],
                 [ext, ext, tmp, tmp])(da, h, h, h, h, wdw, wdw, bdw, bdw)
    return outs


def _ffn_conv_t(name, dhc, wdw, l, shift, seq):
    half, t, fb = dhc.shape
    tm = min(TM_EW, seq)
    seq_tiles, n_tiles = seq // tm, t // tm
    rc = _row_chunk(fb)
    tile, _, nxt = _blk_specs(tm, fb, HALO3, n_tiles, 0)

    def body(d_ref, dn_ref, w_ref, dh_ref, db_ref, ext_ref, out_ref):
        i = pl.program_id(1)
        keep = (i % seq_tiles != seq_tiles - 1).astype(F32)
        ext_ref[0:tm, :] = d_ref[...]
        ext_ref[tm:tm + HALO3, :] = dn_ref[...] * keep
        for r0 in range(0, tm, rc):
            out_ref[r0:r0 + rc, :] = _conv_taps(ext_ref, w_ref, FFN_K, r0, rc, False, jnp.zeros((rc, fb), F32))
        dh = out_ref[...]
        dh_ref[...] = dh.astype(BF16)
        _acc_rows(db_ref, _colsum(dh), i == 0)

    return _call(body, name, (half, n_tiles),
                 [tile, nxt, pl.BlockSpec((None, None, FFN_K, fb), lambda n, i: (l, n + shift, 0, 0))],
                 [tile, pl.BlockSpec((None, 1, fb), lambda n, i: (n, 0, 0))],
                 [_sds((half, t, fb), BF16), _sds((half, 1, fb), F32)],
                 [pltpu.VMEM((tm + HALO3, fb), F32), pltpu.VMEM((tm, fb), F32)])(dhc, dhc, wdw)


def _gelu_parts(h):
    cdf = 0.5 * (1.0 + lax.erf(h * INV_SQRT2))
    return h * cdf, cdf


def _sgu(name, h, g3, b3, wc, bsb, l):
    t, c2 = h.shape
    c = c2 // 2
    tm = min(TM_EW, t)

    def body(h_ref, g_ref, b_ref, wc_ref, bs_ref, o_ref):
        z, _ = _gelu_parts(h_ref[...])
        u = z[:, :c]
        xhat, _ = _ln_stats(z[:, c:])
        vnb = (xhat * g_ref[...] + b_ref[...]).astype(BF16)
        for q in range(tm // CHUNK):
            for gi in range(GROUPS):
                rs, cs = slice(q * CHUNK, (q + 1) * CHUNK), slice(gi * CHUNK, (gi + 1) * CHUNK)
                sp = jnp.dot(wc_ref[gi], vnb[rs, cs], preferred_element_type=F32) + bs_ref[gi]
                o_ref[rs, cs] = (u[rs, cs] * sp).astype(BF16)

    grp = pl.BlockSpec((None, GROUPS, CHUNK, CHUNK), lambda i: (l, 0, 0, 0))
    return _call(body, name, (t // tm,), [_row_spec(tm, c2), _param_spec(l, c), _param_spec(l, c), grp, grp],
                 _row_spec(tm, c), _sds((t, c), BF16))(h, g3, b3, wc, bsb)


def _sgu_bwd(name, dq, h, g3, b3, wc, bsb, l):
    t, c2 = h.shape
    c = c2 // 2
    tm = min(TM_EW, t)
    n_tiles = t // tm

    def body(dq_ref, h_ref, g_ref, b_ref, wc_ref, bs_ref,
             dh_ref, dbin_ref, dw_ref, dbs_ref, dg_ref, db_ref, du_ref, dvn_ref, bsum_ref):
        i = pl.program_id(0)
        first = i == 0
        hv = h_ref[...]
        z, cdf = _gelu_parts(hv)
        u = z[:, :c]
        xhat, rstd = _ln_stats(z[:, c:])
        g = g_ref[...]
        vnb = (xhat * g + b_ref[...]).astype(BF16)

        @pl.when(first)
        def _():
            dw_ref[...] = jnp.zeros_like(dw_ref)
            bsum_ref[...] = jnp.zeros_like(bsum_ref)

        for q in range(tm // CHUNK):
            for gi in range(GROUPS):
                rs, cs = slice(q * CHUNK, (q + 1) * CHUNK), slice(gi * CHUNK, (gi + 1) * CHUNK)
                vb = vnb[rs, cs]
                w = wc_ref[gi]
                sp = jnp.dot(w, vb, preferred_element_type=F32) + bs_ref[gi]
                dqb = dq_ref[rs, cs]
                du_ref[rs, cs] = dqb * sp
                dsp = dqb * u[rs, cs]
                bsum_ref[gi] += dsp
                dspb = dsp.astype(BF16)
                dw_ref[gi] += lax.dot_general(dspb, vb, (_DIMS["nt"], ((), ())), preferred_element_type=F32)
                dvn_ref[rs, cs] = lax.dot_general(w, dspb, (_DIMS["tn"], ((), ())), preferred_element_type=F32)

        dvn = dvn_ref[...]
        dv = _ln_backward(dvn * g, xhat, rstd)
        pdf = jnp.exp(-0.5 * hv * hv) * INV_SQRT2PI
        dgelu = cdf + hv * pdf
        dhu = du_ref[...] * dgelu[:, :c]
        dhv = dv * dgelu[:, c:]
        dh_ref[:, :c] = dhu.astype(BF16)
        dh_ref[:, c:] = dhv.astype(BF16)
        _acc_rows(dbin_ref.at[:, :c], _colsum(dhu), first)
        _acc_rows(dbin_ref.at[:, c:], _colsum(dhv), first)
        _acc_rows(dg_ref, _colsum(dvn * xhat), first)
        _acc_rows(db_ref, _colsum(dvn), first)

        @pl.when(i == n_tiles - 1)
        def _():
            row = lax.broadcasted_iota(jnp.int32, (CHUNK, CHUNK), 0)
            col = lax.broadcasted_iota(jnp.int32, (CHUNK, CHUNK), 1)
            for gi in range(GROUPS):
                dw_ref[gi] = jnp.where(row >= col, dw_ref[gi], 0.0)
            dbs_ref[...] = jnp.sum(bsum_ref[...], axis=-1)

    grp = pl.BlockSpec((None, GROUPS, CHUNK, CHUNK), lambda i: (l, 0, 0, 0))
    vec = pl.BlockSpec((1, c), lambda i: (0, 0))
    return _call(body, name, (n_tiles,),
                 [_row_spec(tm, c), _row_spec(tm, c2), _param_spec(l, c), _param_spec(l, c), grp, grp],
                 [_row_spec(tm, c2), pl.BlockSpec((1, c2), lambda i: (0, 0)),
                  pl.BlockSpec((GROUPS, CHUNK, CHUNK), lambda i: (0, 0, 0)),
                  pl.BlockSpec((GROUPS, CHUNK), lambda i: (0, 0)), vec, vec],
                 [_sds((t, c2), BF16), _sds((1, c2), F32), _sds((GROUPS, CHUNK, CHUNK), F32),
                  _sds((GROUPS, CHUNK), F32), _sds((1, c), F32), _sds((1, c), F32)],
                 [pltpu.VMEM((tm, c), F32), pltpu.VMEM((tm, c), F32), pltpu.VMEM((GROUPS, CHUNK, CHUNK), F32)]
                 )(dq, h, g3, b3, wc, bsb)


def _loss(name, y, target):
    t, d = y.shape
    tm = min(TM_EW, t)
    n_tiles = t // tm

    def body(y_ref, t_ref, l_ref, dy_ref, acc_ref):
        i = pl.program_id(0)
        diff = y_ref[...] - t_ref[...]
        dy_ref[...] = diff * (1.0 / d)
        _acc_rows(acc_ref, _colsum(diff * diff), i == 0)

        @pl.when(i == n_tiles - 1)
        def _():
            l_ref[...] = jnp.broadcast_to(jnp.sum(acc_ref[...], axis=-1, keepdims=True) * (0.5 / d), (1, LANES))

    return _call(body, name, (n_tiles,), [_row_spec(tm, d), _row_spec(tm, d)],
                 [pl.BlockSpec((1, LANES), lambda i: (0, 0)), _row_spec(tm, d)],
                 [_sds((1, LANES), F32), _sds((t, d), F32)], [pltpu.VMEM((1, d), F32)])(y, target)


def _adamw(g, w, m, v):
    m2 = ADAM_B1 * m + (1.0 - ADAM_B1) * g
    v2 = ADAM_B2 * v + (1.0 - ADAM_B2) * (g * g)
    m_hat = m2 / (1.0 - ADAM_B1 ** ADAM_STEP)
    v_hat = v2 / (1.0 - ADAM_B2 ** ADAM_STEP)
    delta = -ADAM_LR * (m_hat / (jnp.sqrt(v_hat) + ADAM_EPS) + ADAM_WD * w)
    return delta, m2, v2


def _row_tile(rows):
    tr = min(rows, 256)
    while rows % tr:
        tr //= 2
    return tr


def _pair_sum(name, mine, theirs):
    nl, nchip, r, c = mine.shape
    tr = _row_tile(r)

    def body(a_ref, b_ref, o_ref):
        o_ref[...] = (a_ref[...].astype(F32) + b_ref[...].astype(F32)).astype(BF16)

    spec = pl.BlockSpec((None, None, tr, c), lambda a, b, i: (a, b, i, 0))
    return _call(body, name, (nl, nchip, r // tr), [spec, spec], spec, _sds(mine.shape, BF16))(mine, theirs)


def _sum4_adamw(name, q, w, m, v):
    _, nl, r, c = q.shape
    tr = _row_tile(r)

    def body(q_ref, w_ref, m_ref, v_ref, g_ref, d_ref, m2_ref, v2_ref):
        g = q_ref[0].astype(F32)
        for s in range(1, N_CHIP):
            g = g + q_ref[s].astype(F32)
        delta, m2, v2 = _adamw(g, w_ref[...], m_ref[...], v_ref[...])
        g_ref[...] = g
        d_ref[...] = delta
        m2_ref[...] = m2
        v2_ref[...] = v2

    spec = pl.BlockSpec((None, tr, c), lambda a, i: (a, i, 0))
    qspec = pl.BlockSpec((N_CHIP, None, tr, c), lambda a, i: (0, a, i, 0))
    return _call(body, name, (nl, r // tr), [qspec, spec, spec, spec], [spec] * 4,
                 [_sds(w.shape, F32)] * 4)(q, w, m, v)


def _sum8(name, parts):
    _, r, c = parts.shape
    tr = _row_tile(r)

    def body(p_ref, o_ref):
        acc = p_ref[0]
        for s in range(1, N_DEV):
            acc = acc + p_ref[s]
        o_ref[...] = acc

    return _call(body, name, (r // tr,), [pl.BlockSpec((N_DEV, tr, c), lambda i: (0, i, 0))],
                 pl.BlockSpec((tr, c), lambda i: (i, 0)), _sds((r, c), F32))(parts)


def _adamw_flat(name, g, w, m, v):
    r, c = g.shape
    tr = _row_tile(r)

    def body(g_ref, w_ref, m_ref, v_ref, d_ref, m2_ref, v2_ref):
        delta, m2, v2 = _adamw(g_ref[...], w_ref[...], m_ref[...], v_ref[...])
        d_ref[...] = delta
        m2_ref[...] = m2
        v2_ref[...] = v2

    spec = pl.BlockSpec((tr, c), lambda i: (i, 0))
    return _call(body, name, (r // tr,), [spec] * 4, [spec] * 3, [_sds((r, c), F32)] * 3)(g, w, m, v)


def _pack(arrs, lead=0):
    pieces = [a.reshape(a.shape[:lead] + (-1, LANES)) for a in arrs]
    rows = sum(p.shape[lead] for p in pieces)
    pad = (-rows) % SUBLANES
    if pad:
        pieces.append(jnp.zeros(pieces[0].shape[:lead] + (pad, LANES), pieces[0].dtype))
    return jnp.concatenate(pieces, axis=lead)


def _unpack(buf, shapes, lead=0):
    out, pos = [], 0
    for shp in shapes:
        rows = math.prod(shp) // LANES
        piece = lax.slice_in_dim(buf, pos, pos + rows, axis=lead)
        out.append(piece.reshape(buf.shape[:lead] + tuple(shp)))
        pos += rows
    return out


REPLICATED = ["conv_b_in", "conv_b_dw", "conv_ln_g", "conv_ln_b", "conv_b_out", "gmlp_w_s", "gmlp_b_s",
              "ffn_b_up", "ffn_b_dw", "ffn_b_down", "norm1_g", "norm1_b", "norm2_g", "norm2_b"]
SMALL_SHARDED = ["conv_w_dw", "gmlp_b_in", "gmlp_ln_g", "gmlp_ln_b", "gmlp_b_out", "ffn_w_dw"]
BIG = ["conv_w_in", "conv_w_out", "gmlp_w_in", "gmlp_w_out", "ffn_w_up", "ffn_w_down"]
WEIGHTS = ["conv_w_in", "conv_b_in", "conv_w_dw", "conv_b_dw", "conv_ln_g", "conv_ln_b", "conv_w_out", "conv_b_out",
           "gmlp_w_in", "gmlp_b_in", "gmlp_ln_g", "gmlp_ln_b", "gmlp_w_s", "gmlp_b_s", "gmlp_w_out", "gmlp_b_out",
           "ffn_w_up", "ffn_b_up", "ffn_w_dw", "ffn_b_dw", "ffn_w_down", "ffn_b_down",
           "norm1_g", "norm1_b", "norm2_g", "norm2_b"]


def _from_shards(g, lead_shape):
    nd = len(lead_shape)
    perm = tuple(range(1, nd + 1)) + (0, nd + 1)
    return g.transpose(perm).reshape(tuple(lead_shape) + (-1,))


def _to_shards(full, width):
    lead = full.shape[:-1]
    nd = len(lead)
    parts = full.reshape(lead + (N_DEV, width))
    return parts.transpose((nd,) + tuple(range(nd)) + (nd + 1,))


def _step(p):
    x_in, target_in = p["x"], p["loss_target"]
    bsz, seq, d = x_in.shape
    t = bsz * seq
    x0 = x_in.reshape(t, d)
    target = target_in.reshape(t, d)
    n_conv, n_gmlp = p["conv_w_in"].shape[0], p["gmlp_w_in"].shape[0]
    fb = p["ffn_w_up"].shape[-1]
    nblk = N_DEV
    half = nblk // 2
    cw = p["conv_w_in"].shape[-1]
    tm = min(TM_MM, t)
    nt = t // tm
    dev = 4 * lax.axis_index("x") + 2 * lax.axis_index("y") + lax.axis_index("c")

    small_shapes = [p[n].shape for n in SMALL_SHARDED]
    small_src = _pack([p[n] for n in SMALL_SHARDED])[None]
    big_src = [p[n].astype(BF16) for n in BIG]
    gathered = _all_gather("gather_weights", big_src + [small_src])
    cwin, cwout, gwin, gwout, fwup, fwdown = gathered[:6]
    cwout = cwout.reshape(n_conv, d, d)
    gwout = gwout.reshape(n_gmlp, d, d)
    fwdown = fwdown.reshape(DEPTH, half, -1, d)
    sm = _unpack(gathered[6][0], small_shapes, lead=1)
    conv_w_dw = _from_shards(sm[0], sm[0].shape[1:-1])
    gmlp_b_in = _from_shards(sm[1], sm[1].shape[1:-1])
    gmlp_ln_g = _from_shards(sm[2], sm[2].shape[1:-1])
    gmlp_ln_b = _from_shards(sm[3], sm[3].shape[1:-1])
    gmlp_b_out = _from_shards(sm[4], sm[4].shape[1:-1])
    ffn_w_dw = sm[5].transpose(1, 0, 2, 3)

    def rows3(a):
        return a.reshape(a.shape[0], 1, a.shape[-1])

    conv_b_in4 = p["conv_b_in"].reshape(n_conv, N_DEV, 1, cw)
    gmlp_b_in4 = gmlp_b_in.reshape(n_gmlp, N_DEV, 1, cw)
    ffn_b_up4 = p["ffn_b_up"].reshape(DEPTH, nblk, 1, fb)
    ffn_b_dw4 = p["ffn_b_dw"].reshape(DEPTH, nblk, 1, fb)
    conv_b_dw3, conv_ln_g3, conv_ln_b3 = rows3(p["conv_b_dw"]), rows3(p["conv_ln_g"]), rows3(p["conv_ln_b"])
    conv_b_out3, gmlp_b_out3, ffn_b_down3 = rows3(p["conv_b_out"]), rows3(gmlp_b_out), rows3(p["ffn_b_down"])
    gmlp_ln_g3, gmlp_ln_b3 = rows3(gmlp_ln_g), rows3(gmlp_ln_b)
    n1g3, n1b3, n2g3, n2b3 = rows3(p["norm1_g"]), rows3(p["norm1_b"]), rows3(p["norm2_g"]), rows3(p["norm2_b"])
    tril = jnp.tril(jnp.ones((CHUNK, CHUNK), dtype=bool))
    w_causal = jnp.where(tril, p["gmlp_w_s"], 0.0).astype(BF16)
    bs_rows = jnp.broadcast_to(p["gmlp_b_s"][..., None], p["gmlp_b_s"].shape + (CHUNK,))

    def mm_in(name, xa, wg, l, bias4):
        return _matmul(name, xa, wg, "nn", grid=(nt, N_DEV),
                       a_spec=pl.BlockSpec((tm, d), lambda i, n: (i, 0)),
                       b_spec=pl.BlockSpec((None, None, d, cw), lambda i, n: (l, n, 0, 0)),
                       o_spec=pl.BlockSpec((tm, cw), lambda i, n: (i, n)), o_shape=(t, N_DEV * cw), o_dtype=F32,
                       bias=bias4, bias_spec=pl.BlockSpec((None, None, 1, cw), lambda i, n: (l, n, 0, 0)))

    def mm_out(name, sa, w, l, bias3):
        return _matmul(name, sa, w, "nn", grid=(nt,),
                       a_spec=pl.BlockSpec((tm, d), lambda i: (i, 0)),
                       b_spec=pl.BlockSpec((None, d, d), lambda i: (l, 0, 0)),
                       o_spec=pl.BlockSpec((tm, d), lambda i: (i, 0)), o_shape=(t, d), o_dtype=F32,
                       bias=bias3, bias_spec=pl.BlockSpec((None, 1, d), lambda i: (l, 0, 0)))

    def mm_out_dx(name, dy, w, l):
        return _matmul(name, dy, w, "nt", grid=(nt,),
                       a_spec=pl.BlockSpec((tm, d), lambda i: (i, 0)),
                       b_spec=pl.BlockSpec((None, d, d), lambda i: (l, 0, 0)),
                       o_spec=pl.BlockSpec((tm, d), lambda i: (i, 0)), o_shape=(t, d), o_dtype=F32)

    def mm_out_dw(name, sa, dy, l, prev):
        return _matmul(name, sa, dy, "tn", grid=(nt,), k_axis=0, nk=nt, acc_shape=(d, d),
                       a_spec=pl.BlockSpec((tm, d), lambda k: (k, 0)),
                       b_spec=pl.BlockSpec((tm, d), lambda k: (k, 0)),
                       o_spec=pl.BlockSpec((None, d, d), lambda k: (l, 0, 0)), o_shape=prev.shape, o_dtype=BF16,
                       prev=prev)

    def mm_in_dx(name, dh, wg, l, res):
        return _matmul(name, dh, wg, "nt", grid=(nt, N_DEV), k_axis=1, nk=N_DEV, acc_shape=(tm, d),
                       a_spec=pl.BlockSpec((tm, cw), lambda i, n: (i, n)),
                       b_spec=pl.BlockSpec((None, None, d, cw), lambda i, n: (l, n, 0, 0)),
                       o_spec=pl.BlockSpec((tm, d), lambda i, n: (i, 0)), o_shape=(t, d), o_dtype=F32,
                       res=res, res_spec=pl.BlockSpec((tm, d), lambda i, n: (i, 0)), res_scale=ALPHA)

    def mm_in_dw(name, xa, dh, l, prev):
        return _matmul(name, xa, dh, "tn", grid=(N_DEV, nt), k_axis=1, nk=nt, acc_shape=(d, cw),
                       a_spec=pl.BlockSpec((tm, d), lambda n, k: (k, 0)),
                       b_spec=pl.BlockSpec((tm, cw), lambda n, k: (k, n)),
                       o_spec=pl.BlockSpec((None, None, d, cw), lambda n, k: (l, n, 0, 0)),
                       o_shape=prev.shape, o_dtype=BF16, prev=prev)

    def mm_up(name, xa, l):
        return _matmul(name, xa, fwup, "nn", grid=(nt, nblk),
                       a_spec=pl.BlockSpec((tm, d), lambda i, n: (i, 0)),
                       b_spec=pl.BlockSpec((None, None, d, fb), lambda i, n: (l, n, 0, 0)),
                       o_spec=pl.BlockSpec((None, tm, fb), lambda i, n: (n, i, 0)), o_shape=(nblk, t, fb),
                       o_dtype=F32, bias=ffn_b_up4,
                       bias_spec=pl.BlockSpec((None, None, 1, fb), lambda i, n: (l, n, 0, 0)))

    def mm_down(name, a, l):
        return _matmul(name, a, fwdown, "nn", grid=(nt, half), k_axis=1, nk=half, acc_shape=(tm, d),
                       a_spec=pl.BlockSpec((None, tm, fb), lambda i, n: (n, i, 0)),
                       b_spec=pl.BlockSpec((None, None, fb, d), lambda i, n: (l, n, 0, 0)),
                       o_spec=pl.BlockSpec((tm, d), lambda i, n: (i, 0)), o_shape=(t, d), o_dtype=F32,
                       bias=ffn_b_down3, bias_spec=pl.BlockSpec((None, 1, d), lambda i, n: (l, 0, 0)))

    def mm_down_da(name, dy, l):
        return _matmul(name, dy, fwdown, "nt", grid=(nt, half),
                       a_spec=pl.BlockSpec((tm, d), lambda i, n: (i, 0)),
                       b_spec=pl.BlockSpec((None, None, fb, d), lambda i, n: (l, n, 0, 0)),
                       o_spec=pl.BlockSpec((None, tm, fb), lambda i, n: (n, i, 0)), o_shape=(half, t, fb),
                       o_dtype=F32)

    def mm_down_dw(name, a, dy, l, prev):
        return _matmul(name, a, dy, "tn", grid=(half, nt), k_axis=1, nk=nt, acc_shape=(fb, d),
                       a_spec=pl.BlockSpec((None, tm, fb), lambda n, k: (n, k, 0)),
                       b_spec=pl.BlockSpec((tm, d), lambda n, k: (k, 0)),
                       o_spec=pl.BlockSpec((None, None, fb, d), lambda n, k: (l, n, 0, 0)),
                       o_shape=prev.shape, o_dtype=BF16, prev=prev)

    def mm_up_dx(name, dh_g, dh_v, l, res):
        part = _matmul(name + "_g", dh_g, fwup, "nt", grid=(nt, half), k_axis=1, nk=half, acc_shape=(tm, d),
                       a_spec=pl.BlockSpec((None, tm, fb), lambda i, n: (n, i, 0)),
                       b_spec=pl.BlockSpec((None, None, d, fb), lambda i, n: (l, n, 0, 0)),
                       o_spec=pl.BlockSpec((tm, d), lambda i, n: (i, 0)), o_shape=(t, d), o_dtype=F32,
                       res=res, res_spec=pl.BlockSpec((tm, d), lambda i, n: (i, 0)), res_scale=ALPHA)
        return _matmul(name + "_v", dh_v, fwup, "nt", grid=(nt, half), k_axis=1, nk=half, acc_shape=(tm, d),
                       a_spec=pl.BlockSpec((None, tm, fb), lambda i, n: (n, i, 0)),
                       b_spec=pl.BlockSpec((None, None, d, fb), lambda i, n: (l, n + half, 0, 0)),
                       o_spec=pl.BlockSpec((tm, d), lambda i, n: (i, 0)), o_shape=(t, d), o_dtype=F32,
                       res=part, res_spec=pl.BlockSpec((tm, d), lambda i, n: (i, 0)), res_scale=1.0)

    def mm_up_dw(name, xa, dh_half, l, shift, prev):
        return _matmul(name, xa, dh_half, "tn", grid=(half, nt), k_axis=1, nk=nt, acc_shape=(d, fb),
                       a_spec=pl.BlockSpec((tm, d), lambda n, k: (k, 0)),
                       b_spec=pl.BlockSpec((None, tm, fb), lambda n, k: (n, k, 0)),
                       o_spec=pl.BlockSpec((None, None, d, fb), lambda n, k: (l, n + shift, 0, 0)),
                       o_shape=prev.shape, o_dtype=BF16, prev=prev)

    saved = []
    xcur = x0
    for i in range(DEPTH):
        j = i // 2
        s = {"x": xcur}
        if i % 2 == 0:
            s["h"] = mm_in(f"l{i}_conv_in", xcur, cwin, j, conv_b_in4)
            s["u"] = _glu(f"l{i}_glu", s["h"])
            s["c"] = _dwconv31(f"l{i}_dwconv", s["u"], conv_w_dw, conv_b_dw3, j, seq)
            s["s"] = _ln_silu(f"l{i}_ln_silu", s["c"], conv_ln_g3, conv_ln_b3, j)
            y = mm_out(f"l{i}_conv_out", s["s"], cwout, j, conv_b_out3)
        else:
            s["h"] = mm_in(f"l{i}_gmlp_in", xcur, gwin, j, gmlp_b_in4)
            s["s"] = _sgu(f"l{i}_sgu", s["h"], gmlp_ln_g3, gmlp_ln_b3, w_causal, bs_rows, j)
            y = mm_out(f"l{i}_gmlp_out", s["s"], gwout, j, gmlp_b_out3)
        s["x1"], s["xhat1"], s["rstd1"] = _ln_res(f"l{i}_norm1", xcur, y, n1g3, n1b3, i)
        s["fh"] = mm_up(f"l{i}_ffn_up", s["x1"], i)
        s["a"] = _ffn_act(f"l{i}_ffn_act", s["fh"], ffn_w_dw, ffn_b_dw4, i, seq)
        y = mm_down(f"l{i}_ffn_down", s["a"], i)
        xcur, s["xhat2"], s["rstd2"] = _ln_res(f"l{i}_norm2", s["x1"], y, n2g3, n2b3, i)
        saved.append(s)

    loss_row, dx = _loss("loss", xcur, target)

    g_big = {
        "conv_w_in": lax.empty((n_conv, N_DEV, d, cw), BF16), "conv_w_out": lax.empty((n_conv, d, d), BF16),
        "gmlp_w_in": lax.empty((n_gmlp, N_DEV, d, cw), BF16), "gmlp_w_out": lax.empty((n_gmlp, d, d), BF16),
        "ffn_w_up": lax.empty((DEPTH, nblk, d, fb), BF16), "ffn_w_down": lax.empty((DEPTH, half, fb, d), BF16),
    }
    gl = {n: [None] * p[n].shape[0] for n in REPLICATED + SMALL_SHARDED}
    for i in reversed(range(DEPTH)):
        j = i // 2
        s = saved[i]
        dr2, gl["norm2_g"][i], gl["norm2_b"][i], gl["ffn_b_down"][i] = _ln_res_bwd(
            f"l{i}_norm2_bwd", dx, s["xhat2"], s["rstd2"], n2g3, i)
        da = mm_down_da(f"l{i}_ffn_down_da", dr2, i)
        g_big["ffn_w_down"] = mm_down_dw(f"l{i}_ffn_down_dw", s["a"], dr2, i, g_big["ffn_w_down"])
        dcg, dcv, dbg, dbv, dwg, dwv = _ffn_act_bwd(f"l{i}_ffn_act_bwd", da, s["fh"], ffn_w_dw, ffn_b_dw4, i, seq)
        gl["ffn_b_dw"][i] = jnp.concatenate([dbg, dbv], axis=0).reshape(1, nblk * fb)
        gl["ffn_w_dw"][i] = jnp.concatenate([dwg[:, :FFN_K], dwv[:, :FFN_K]], axis=0)
        dh_g, dbu_g = _ffn_conv_t(f"l{i}_ffn_conv_t_g", dcg, ffn_w_dw, i, 0, seq)
        dh_v, dbu_v = _ffn_conv_t(f"l{i}_ffn_conv_t_v", dcv, ffn_w_dw, i, half, seq)
        gl["ffn_b_up"][i] = jnp.concatenate([dbu_g, dbu_v], axis=0).reshape(1, nblk * fb)
        dx1 = mm_up_dx(f"l{i}_ffn_up_dx", dh_g, dh_v, i, dr2)
        g_big["ffn_w_up"] = mm_up_dw(f"l{i}_ffn_up_dw_g", s["x1"], dh_g, i, 0, g_big["ffn_w_up"])
        g_big["ffn_w_up"] = mm_up_dw(f"l{i}_ffn_up_dw_v", s["x1"], dh_v, i, half, g_big["ffn_w_up"])
        if i % 2 == 0:
            dr1, gl["norm1_g"][i], gl["norm1_b"][i], gl["conv_b_out"][j] = _ln_res_bwd(
                f"l{i}_norm1_bwd", dx1, s["xhat1"], s["rstd1"], n1g3, i)
            ds = mm_out_dx(f"l{i}_conv_out_dx", dr1, cwout, j)
            g_big["conv_w_out"] = mm_out_dw(f"l{i}_conv_out_dw", s["s"], dr1, j, g_big["conv_w_out"])
            dc, gl["conv_ln_g"][j], gl["conv_ln_b"][j], gl["conv_b_dw"][j] = _ln_silu_bwd(
                f"l{i}_ln_silu_bwd", ds, s["c"], conv_ln_g3, conv_ln_b3, j)
            du, dwdw = _dwconv31_bwd(f"l{i}_dwconv_bwd", dc, s["u"], conv_w_dw, j, seq)
            gl["conv_w_dw"][j] = dwdw[:CONV_K]
            dh, gl["conv_b_in"][j] = _glu_bwd(f"l{i}_glu_bwd", du, s["h"])
            dx = mm_in_dx(f"l{i}_conv_in_dx", dh, cwin, j, dr1)
            g_big["conv_w_in"] = mm_in_dw(f"l{i}_conv_in_dw", s["x"], dh, j, g_big["conv_w_in"])
        else:
            dr1, gl["norm1_g"][i], gl["norm1_b"][i], gl["gmlp_b_out"][j] = _ln_res_bwd(
                f"l{i}_norm1_bwd", dx1, s["xhat1"], s["rstd1"], n1g3, i)
            dq = mm_out_dx(f"l{i}_gmlp_out_dx", dr1, gwout, j)
            g_big["gmlp_w_out"] = mm_out_dw(f"l{i}_gmlp_out_dw", s["s"], dr1, j, g_big["gmlp_w_out"])
            dh, gl["gmlp_b_in"][j], gl["gmlp_w_s"][j], gl["gmlp_b_s"][j], gl["gmlp_ln_g"][j], gl["gmlp_ln_b"][j] = (
                _sgu_bwd(f"l{i}_sgu_bwd", dq, s["h"], gmlp_ln_g3, gmlp_ln_b3, w_causal, bs_rows, j))
            dx = mm_in_dx(f"l{i}_gmlp_in_dx", dh, gwin, j, dr1)
            g_big["gmlp_w_in"] = mm_in_dw(f"l{i}_gmlp_in_dw", s["x"], dh, j, g_big["gmlp_w_in"])
    grad_x = dx.reshape(bsz, seq, d)

    def chip_core(g):
        return g.reshape(g.shape[0], N_CHIP, 2, -1, g.shape[-1])

    big_grads = [chip_core(g_big[n]) for n in BIG]
    mine, theirs = _sibling_swap("grads_sibling_swap", big_grads)
    pairs = [_pair_sum(f"pair_sum_{n}", a, b) for n, a, b in zip(BIG, mine, theirs)]

    full_small = {n: jnp.stack(gl[n]).reshape(p[n].shape) for n in REPLICATED}
    shard_small = {}
    for n in SMALL_SHARDED:
        if n == "ffn_w_dw":
            shard_small[n] = jnp.stack(gl[n]).transpose(1, 0, 2, 3)
        else:
            width = p[n].shape[-1]
            lead = p[n].shape[:-1]
            shard_small[n] = _to_shards(jnp.stack(gl[n]).reshape(lead + (N_DEV * width,)), width)
    flat_shapes = [(1, LANES)] + [p[n].shape for n in REPLICATED] + [(N_DEV,) + p[n].shape for n in SMALL_SHARDED]
    flat_local = _pack([loss_row] + [full_small[n] for n in REPLICATED] + [shard_small[n] for n in SMALL_SHARDED])

    chip_sums = _chip_exchange("grads_chip_exchange", pairs)
    small_parts = _all_gather("gather_small_grads", [flat_local[None]])[0][0]
    small_sum = _sum8("sum_small_grads", small_parts)
    summed = _unpack(small_sum, flat_shapes)
    loss = summed[0][0, 0]
    grads = dict(zip(REPLICATED, summed[1:1 + len(REPLICATED)]))
    for n, g in zip(SMALL_SHARDED, summed[1 + len(REPLICATED):]):
        grads[n] = lax.dynamic_index_in_dim(g, dev, axis=0, keepdims=False)

    delta, new_m, new_v = {}, {}, {}
    for n, q in zip(BIG, chip_sums):
        shp = p[n].shape
        q = q.reshape((N_CHIP,) + shp)
        grads[n], delta[n], new_m[n], new_v[n] = _sum4_adamw(f"adamw_{n}", q, p[n], p["m_" + n], p["v_" + n])
    small = REPLICATED + SMALL_SHARDED
    small_shp = [p[n].shape for n in small]
    d_s, m_s, v_s = _adamw_flat("adamw_small", _pack([grads[n] for n in small]), _pack([p[n] for n in small]),
                                _pack([p["m_" + n] for n in small]), _pack([p["v_" + n] for n in small]))
    for n, dd, mm, vv in zip(small, _unpack(d_s, small_shp), _unpack(m_s, small_shp), _unpack(v_s, small_shp)):
        delta[n], new_m[n], new_v[n] = dd, mm, vv

    return (loss, grad_x, *[grads[n] for n in WEIGHTS], *[delta[n] for n in WEIGHTS],
            *[new_m[n] for n in WEIGHTS], *[new_v[n] for n in WEIGHTS])


def kernel(x, conv_w_in, conv_b_in, conv_w_dw, conv_b_dw, conv_ln_g, conv_ln_b, conv_w_out, conv_b_out, gmlp_w_in, gmlp_b_in, gmlp_ln_g, gmlp_ln_b, gmlp_w_s, gmlp_b_s, gmlp_w_out, gmlp_b_out, ffn_w_up, ffn_b_up, ffn_w_dw, ffn_b_dw, ffn_w_down, ffn_b_down, norm1_g, norm1_b, norm2_g, norm2_b, loss_target, m_conv_w_in, m_conv_b_in, m_conv_w_dw, m_conv_b_dw, m_conv_ln_g, m_conv_ln_b, m_conv_w_out, m_conv_b_out, m_gmlp_w_in, m_gmlp_b_in, m_gmlp_ln_g, m_gmlp_ln_b, m_gmlp_w_s, m_gmlp_b_s, m_gmlp_w_out, m_gmlp_b_out, m_ffn_w_up, m_ffn_b_up, m_ffn_w_dw, m_ffn_b_dw, m_ffn_w_down, m_ffn_b_down, m_norm1_g, m_norm1_b, m_norm2_g, m_norm2_b, v_conv_w_in, v_conv_b_in, v_conv_w_dw, v_conv_b_dw, v_conv_ln_g, v_conv_ln_b, v_conv_w_out, v_conv_b_out, v_gmlp_w_in, v_gmlp_b_in, v_gmlp_ln_g, v_gmlp_ln_b, v_gmlp_w_s, v_gmlp_b_s, v_gmlp_w_out, v_gmlp_b_out, v_ffn_w_up, v_ffn_b_up, v_ffn_w_dw, v_ffn_b_dw, v_ffn_w_down, v_ffn_b_down, v_norm1_g, v_norm1_b, v_norm2_g, v_norm2_b):
    return _step(dict(locals()))
```

```python
import math

import jax
import jax.numpy as jnp
from jax import lax
from jax.experimental import pallas as pl
from jax.experimental.pallas import tpu as pltpu

F32 = jnp.float32
BF16 = jnp.bfloat16
MESH = pl.DeviceIdType.MESH

N_DEV = 8
DEPTH = 4
ALPHA = (2.0 * DEPTH) ** 0.25
LN_EPS = 1e-5
CONV_K = 31
FFN_K = 3
CHUNK = 128
GROUPS = 8
ADAM_LR = 0.001
ADAM_B1 = 0.9
ADAM_B2 = 0.999
ADAM_EPS = 1e-08
ADAM_WD = 0.01
ADAM_STEP = 10
INV_SQRT2 = 1.0 / math.sqrt(2.0)
INV_SQRT2PI = 1.0 / math.sqrt(2.0 * math.pi)

LANES = 128
SUBLANES = 8
VMEM_LIMIT = 48 * 1024 * 1024
TM_MM = 1024
TM_EW = 256
HALO31 = 32
HALO3 = 8


def _call(body, name, grid, in_specs, out_specs, out_shape, scratch=(), aliases=None, deps=()):
    deps = list(deps)
    in_specs = list(in_specs)
    n_in = len(in_specs)
    if deps:
        inner = body

        def body(*refs):
            return inner(*refs[:n_in], *refs[n_in + len(deps):])

        in_specs = in_specs + [pl.BlockSpec(memory_space=pl.ANY)] * len(deps)
    fn = pl.pallas_call(
        body, name=name, grid=grid, in_specs=in_specs, out_specs=out_specs, out_shape=out_shape,
        scratch_shapes=list(scratch), input_output_aliases=aliases or {},
        compiler_params=pltpu.CompilerParams(vmem_limit_bytes=VMEM_LIMIT))
    return lambda *args: fn(*args, *deps)


def _sds(shape, dtype):
    return jax.ShapeDtypeStruct(tuple(shape), dtype)


def _sigmoid(x):
    return 1.0 / (1.0 + jnp.exp(-x))


def _acc_rows(ref, val, first):
    @pl.when(first)
    def _():
        ref[...] = val

    @pl.when(jnp.logical_not(first))
    def _():
        ref[...] += val


def _colsum(v):
    return jnp.sum(v, axis=0, keepdims=True)


_DIMS = {"nn": ((1,), (0,)), "nt": ((1,), (1,)), "tn": ((0,), (0,))}


def _matmul(name, a, b, mode, *, grid, a_spec, b_spec, o_spec, o_shape, o_dtype, k_axis=None, nk=1,
            acc_shape=None, bias=None, bias_spec=None, res=None, res_spec=None, res_scale=1.0,
            prev=None):
    dims = (_DIMS[mode], ((), ()))
    has_bias, has_res, has_prev = bias is not None, res is not None, prev is not None

    def body(*refs):
        a_ref, b_ref = refs[0], refs[1]
        pos = 2
        bias_ref = res_ref = None
        if has_bias:
            bias_ref = refs[pos]
            pos += 1
        if has_res:
            res_ref = refs[pos]
            pos += 1
        if has_prev:
            pos += 1
        o_ref = refs[pos]
        acc_ref = refs[pos + 1] if nk > 1 else None
        p = lax.dot_general(a_ref[...].astype(BF16), b_ref[...].astype(BF16), dims, preferred_element_type=F32)

        def finish(acc):
            if has_bias:
                acc = acc + bias_ref[...]
            if has_res:
                acc = acc + res_scale * res_ref[...]
            o_ref[...] = acc.astype(o_dtype)

        if nk == 1:
            finish(p)
        else:
            k = pl.program_id(k_axis)

            @pl.when(k == 0)
            def _():
                acc_ref[...] = p

            @pl.when(k > 0)
            def _():
                acc_ref[...] += p

            @pl.when(k == nk - 1)
            def _():
                finish(acc_ref[...])

    ins, specs = [a, b], [a_spec, b_spec]
    if has_bias:
        ins.append(bias)
        specs.append(bias_spec)
    if has_res:
        ins.append(res)
        specs.append(res_spec)
    aliases = None
    if has_prev:
        aliases = {len(ins): 0}
        ins.append(prev)
        specs.append(pl.BlockSpec(memory_space=pl.ANY))
    scratch = [pltpu.VMEM(acc_shape, F32)] if nk > 1 else []
    return _call(body, name, grid, specs, o_spec, _sds(o_shape, o_dtype), scratch, aliases)(*ins)


def _mesh_pos():
    return lax.axis_index("x"), lax.axis_index("y"), lax.axis_index("c")


def _any_specs(n):
    return [pl.BlockSpec(memory_space=pl.ANY)] * n


def _all_gather(name, srcs):
    n = len(srcs)

    def body(*refs):
        src, out = refs[:n], refs[n:2 * n]
        send_sems, recv_sems, local_sems = refs[2 * n:]
        x, y, c = _mesh_pos()
        me, sibling = (x, y, c), (x, y, 1 - c)
        chips = [(1 - x, y), (x, 1 - y), (1 - x, 1 - y)]

        def slot(k, p):
            return out[k].at[:, 4 * p[0] + 2 * p[1] + p[2]]

        def copy(k, idx, block, to, s=None):
            return pltpu.make_async_remote_copy(
                src_ref=slot(k, block) if s is None else s, dst_ref=slot(k, block),
                send_sem=send_sems.at[k * 7 + idx], recv_sem=recv_sems.at[k * 7 + idx],
                device_id=to, device_id_type=MESH)

        local = [pltpu.make_async_copy(src[k], slot(k, me), local_sems.at[k]) for k in range(n)]
        for cp in local:
            cp.start()
        first = []
        for k in range(n):
            first.append(copy(k, 0, me, sibling, src[k]))
            for j, chip in enumerate(chips):
                first.append(copy(k, 1 + j, me, (*chip, c), src[k]))
        for cp in first:
            cp.start()
        passed = []
        for j, chip in enumerate(chips):
            for k in range(n):
                copy(k, 1 + j, (*chip, c), me).wait_recv()
                cp = copy(k, 4 + j, (*chip, c), sibling)
                cp.start()
                passed.append(cp)
        for k in range(n):
            copy(k, 0, sibling, me).wait_recv()
            for j, chip in enumerate(chips):
                copy(k, 4 + j, (*chip, 1 - c), me).wait_recv()
        for cp in first + passed:
            cp.wait_send()
        for cp in local:
            cp.wait()

    out_shape = [_sds((s.shape[0], N_DEV) + s.shape[1:], s.dtype) for s in srcs]
    return _call(body, name, (), [pl.BlockSpec(memory_space=pltpu.VMEM)] * n, _any_specs(n), out_shape,
                 [pltpu.SemaphoreType.DMA((7 * n,)), pltpu.SemaphoreType.DMA((7 * n,)),
                  pltpu.SemaphoreType.DMA((n,))])(*srcs)


HBM_SPEC = pl.BlockSpec(memory_space=pltpu.HBM)
SEM_SPEC = pl.BlockSpec(memory_space=pltpu.SEMAPHORE)
N_PEER = N_DEV - 1


def _split_call(body, name, in_specs, out_specs, out_shape, aliases):
    return pl.pallas_call(
        body, name=name, in_specs=in_specs, out_specs=out_specs, out_shape=out_shape, input_output_aliases=aliases,
        compiler_params=pltpu.CompilerParams(has_side_effects=pltpu.SideEffectType.DATAFLOW_SIDE_EFFECTING))


def _peers(x, y, c):
    return [(1 - x if q & 4 else x, 1 - y if q & 2 else y, 1 - c if q & 1 else c) for q in range(1, N_DEV)]


def _in_hbm(a):
    return pltpu.with_memory_space_constraint(a, pltpu.HBM)


def _place_own(name, srcs):
    n = len(srcs)

    def body(*refs):
        src, out, sems = refs[:n], refs[n:2 * n], refs[2 * n]
        x, y, c = _mesh_pos()
        dev = 4 * x + 2 * y + c
        copies = [pltpu.make_async_copy(src[k], out[k].at[dev], sems.at[k]) for k in range(n)]
        for cp in copies:
            cp.start()
        for cp in copies:
            cp.wait()

    return _call(body, name, (), [pl.BlockSpec(memory_space=pltpu.VMEM)] * n, _any_specs(n),
                 [_sds((N_DEV,) + s.shape, s.dtype) for s in srcs], [pltpu.SemaphoreType.DMA((n,))])(*srcs)


def _gather_start(name, lands):
    n = len(lands)

    def body(*refs):
        land, send_sems, recv_sems = refs[:n], refs[n], refs[n + 1]
        x, y, c = _mesh_pos()
        dev = 4 * x + 2 * y + c
        for k in range(n):
            for peer in _peers(x, y, c):
                pltpu.make_async_remote_copy(
                    src_ref=land[k].at[dev], dst_ref=land[k].at[dev], send_sem=send_sems.at[k],
                    recv_sem=recv_sems.at[k], device_id=peer, device_id_type=MESH).start()

    outs = _split_call(
        body, name, [HBM_SPEC] * n, [SEM_SPEC, SEM_SPEC] + [HBM_SPEC] * n,
        [pltpu.SemaphoreType.DMA((n,)), pltpu.SemaphoreType.DMA((n,))] + [pltpu.HBM(a.shape, a.dtype) for a in lands],
        {k: 2 + k for k in range(n)})(*[_in_hbm(a) for a in lands])
    return outs[0], outs[1], list(outs[2:])


def _wait_seven(src_ref, dst_ref, send_sem, recv_sem):
    cp = pltpu.make_async_remote_copy(
        src_ref=src_ref.at[pl.ds(0, N_PEER)], dst_ref=dst_ref.at[pl.ds(0, N_PEER)], send_sem=send_sem,
        recv_sem=recv_sem, device_id=_mesh_pos(), device_id_type=MESH)
    cp.wait_send()
    cp.wait_recv()


def _gather_wait(name, land, send_sems, recv_sems, k, after):
    def body(land_ref, send_ref, recv_ref, after_ref, out_ref):
        _wait_seven(land_ref, land_ref, send_ref.at[k], recv_ref.at[k])

    return _split_call(body, name, [HBM_SPEC, SEM_SPEC, SEM_SPEC, pl.BlockSpec(memory_space=pl.ANY)], HBM_SPEC,
                       pltpu.HBM(land.shape, land.dtype), {0: 0})(land, send_sems, recv_sems, after)


def _scatter_start(name, parts):
    def body(parts_ref, land_ref, send_sem, recv_sem, parts_out, land_out, token):
        x, y, c = _mesh_pos()
        dev = 4 * x + 2 * y + c
        for peer in _peers(x, y, c):
            pltpu.make_async_remote_copy(
                src_ref=parts_ref.at[4 * peer[0] + 2 * peer[1] + peer[2]], dst_ref=land_ref.at[dev],
                send_sem=send_sem, recv_sem=recv_sem, device_id=peer, device_id_type=MESH).start()
        token[...] = jnp.zeros_like(token)

    buf = pltpu.HBM(parts.shape, parts.dtype)
    return _split_call(
        body, name, [HBM_SPEC, HBM_SPEC],
        [SEM_SPEC, SEM_SPEC, HBM_SPEC, HBM_SPEC, pl.BlockSpec(memory_space=pltpu.VMEM)],
        [pltpu.SemaphoreType.DMA(()), pltpu.SemaphoreType.DMA(()), buf, buf, _sds((SUBLANES, LANES), F32)],
        {0: 2, 1: 3})(_in_hbm(parts), _in_hbm(lax.empty(parts.shape, parts.dtype)))


def _scatter_wait(name, started, after):
    n = len(started)

    def body(*refs):
        for k in range(n):
            send_sem, recv_sem, parts_ref, land_ref = refs[4 * k:4 * k + 4]
            _wait_seven(parts_ref, land_ref, send_sem, recv_sem)

    flat = [a for s in started for a in s]
    outs = _split_call(
        body, name, [SEM_SPEC, SEM_SPEC, HBM_SPEC, HBM_SPEC] * n + [pl.BlockSpec(memory_space=pl.ANY)],
        [HBM_SPEC, HBM_SPEC] * n, [pltpu.HBM(a.shape, a.dtype) for s in started for a in s[2:]],
        {4 * k + 2 + t: 2 * k + t for k in range(n) for t in range(2)})(*flat, after)
    return list(outs[0::2]), list(outs[1::2])


def _row_chunk(width):
    lanes = -(-width // LANES)
    return max(SUBLANES, (8 // lanes) * SUBLANES) if lanes <= 8 else SUBLANES


def _conv_taps(ext_ref, w_ref, k_taps, base, rows, forward, init):
    acc = init
    for k in range(k_taps):
        d = k_taps - 1 - k
        off = base - d if forward else base + d
        acc = acc + w_ref[k:k + 1, :] * ext_ref[off:off + rows, :]
    return acc


def _ln_stats(v):
    mu = jnp.mean(v, axis=-1, keepdims=True)
    vc = v - mu
    var = jnp.mean(vc * vc, axis=-1, keepdims=True)
    rstd = lax.rsqrt(var + LN_EPS)
    return vc * rstd, rstd


def _ln_backward(dxhat, xhat, rstd):
    m1 = jnp.mean(dxhat, axis=-1, keepdims=True)
    m2 = jnp.mean(dxhat * xhat, axis=-1, keepdims=True)
    return rstd * (dxhat - m1 - xhat * m2)


def _row_spec(tm, width):
    return pl.BlockSpec((tm, width), lambda i: (i, 0))


def _param_spec(l, width):
    return pl.BlockSpec((None, 1, width), lambda *_: (l, 0, 0))


def _ln_res(name, x, y, g3, b3, l):
    t, d = x.shape
    tm = min(TM_EW, t)

    def body(x_ref, y_ref, g_ref, b_ref, o_ref, xh_ref, rs_ref):
        xhat, rstd = _ln_stats(ALPHA * x_ref[...] + y_ref[...])
        o_ref[...] = xhat * g_ref[...] + b_ref[...]
        xh_ref[...] = xhat
        rs_ref[...] = rstd

    return _call(body, name, (t // tm,),
                 [_row_spec(tm, d), _row_spec(tm, d), _param_spec(l, d), _param_spec(l, d)],
                 [_row_spec(tm, d), _row_spec(tm, d), _row_spec(tm, 1)],
                 [_sds((t, d), F32), _sds((t, d), F32), _sds((t, 1), F32)])(x, y, g3, b3)


def _ln_res_bwd(name, dout, xhat, rstd, g3, l, deps=()):
    t, d = dout.shape
    tm = min(TM_EW, t)

    def body(do_ref, xh_ref, rs_ref, g_ref, dr_ref, dg_ref, db_ref, dc_ref):
        first = pl.program_id(0) == 0
        do, xhat = do_ref[...], xh_ref[...]
        dr = _ln_backward(do * g_ref[...], xhat, rs_ref[...])
        dr_ref[...] = dr
        _acc_rows(dg_ref, _colsum(do * xhat), first)
        _acc_rows(db_ref, _colsum(do), first)
        _acc_rows(dc_ref, _colsum(dr), first)

    vec = pl.BlockSpec((1, d), lambda i: (0, 0))
    return _call(body, name, (t // tm,),
                 [_row_spec(tm, d), _row_spec(tm, d), _row_spec(tm, 1), _param_spec(l, d)],
                 [_row_spec(tm, d), vec, vec, vec],
                 [_sds((t, d), F32)] + [_sds((1, d), F32)] * 3, deps=deps)(dout, xhat, rstd, g3)


def _glu(name, h):
    t, c2 = h.shape
    c = c2 // 2
    tm = min(TM_EW, t)

    def body(a_ref, g_ref, o_ref):
        o_ref[...] = a_ref[...] * _sigmoid(g_ref[...])

    return _call(body, name, (t // tm,),
                 [pl.BlockSpec((tm, c), lambda i: (i, 0)), pl.BlockSpec((tm, c), lambda i: (i, 1))],
                 _row_spec(tm, c), _sds((t, c), F32))(h, h)


def _glu_bwd(name, du, h):
    t, c2 = h.shape
    c = c2 // 2
    tm = min(TM_EW, t)

    def body(du_ref, a_ref, g_ref, dh_ref, db_ref):
        first = pl.program_id(0) == 0
        du_v, a = du_ref[...], a_ref[...]
        sg = _sigmoid(g_ref[...])
        da = du_v * sg
        dg = du_v * a * sg * (1.0 - sg)
        dh_ref[:, :c] = da.astype(BF16)
        dh_ref[:, c:] = dg.astype(BF16)
        _acc_rows(db_ref.at[:, :c], _colsum(da), first)
        _acc_rows(db_ref.at[:, c:], _colsum(dg), first)

    return _call(body, name, (t // tm,),
                 [_row_spec(tm, c), pl.BlockSpec((tm, c), lambda i: (i, 0)), pl.BlockSpec((tm, c), lambda i: (i, 1))],
                 [_row_spec(tm, c2), pl.BlockSpec((1, c2), lambda i: (0, 0))],
                 [_sds((t, c2), BF16), _sds((1, c2), F32)])(du, h, h)


def _halo_specs(tm, cb, halo, seq_tiles, n_tiles):
    per = tm // halo
    tile = pl.BlockSpec((tm, cb), lambda n, i: (i, n))
    prev = pl.BlockSpec((halo, cb), lambda n, i: (jnp.maximum(i * per - 1, 0), n))
    nxt = pl.BlockSpec((halo, cb), lambda n, i: (jnp.minimum((i + 1) * per, n_tiles * per - 1), n))
    return tile, prev, nxt


def _dwconv31(name, u, w3, b3, l, seq):
    t, c = u.shape
    tm, cb = min(TM_EW, seq), 256
    seq_tiles, n_tiles = seq // tm, t // tm
    rc = _row_chunk(cb)
    tile, prev, _ = _halo_specs(tm, cb, HALO31, seq_tiles, n_tiles)

    def body(u_ref, halo_ref, w_ref, b_ref, o_ref, ext_ref):
        i = pl.program_id(1)
        keep = (i % seq_tiles != 0).astype(F32)
        ext_ref[0:HALO31, :] = halo_ref[...] * keep
        ext_ref[HALO31:HALO31 + tm, :] = u_ref[...]
        for r0 in range(0, tm, rc):
            init = jnp.broadcast_to(b_ref[...], (rc, cb))
            o_ref[r0:r0 + rc, :] = _conv_taps(ext_ref, w_ref, CONV_K, HALO31 + r0, rc, True, init)

    return _call(body, name, (c // cb, n_tiles),
                 [tile, prev, pl.BlockSpec((None, CONV_K, cb), lambda n, i: (l, 0, n)),
                  pl.BlockSpec((None, 1, cb), lambda n, i: (l, 0, n))],
                 tile, _sds((t, c), F32), [pltpu.VMEM((HALO31 + tm, cb), F32)])(u, u, w3, b3)


def _dwconv31_bwd(name, dc, u, w3, l, seq):
    t, c = dc.shape
    tm, cb = min(TM_EW, seq), 256
    seq_tiles, n_tiles = seq // tm, t // tm
    rc = _row_chunk(cb)
    tile, prev, nxt = _halo_specs(tm, cb, HALO31, seq_tiles, n_tiles)

    def body(dc_ref, dcn_ref, u_ref, up_ref, w_ref, du_ref, dw_ref, dext_ref, uext_ref):
        i = pl.program_id(1)
        keep_prev = (i % seq_tiles != 0).astype(F32)
        keep_next = (i % seq_tiles != seq_tiles - 1).astype(F32)
        dext_ref[0:tm, :] = dc_ref[...]
        dext_ref[tm:tm + HALO31, :] = dcn_ref[...] * keep_next
        uext_ref[0:HALO31, :] = up_ref[...] * keep_prev
        uext_ref[HALO31:HALO31 + tm, :] = u_ref[...]
        for r0 in range(0, tm, rc):
            du_ref[r0:r0 + rc, :] = _conv_taps(dext_ref, w_ref, CONV_K, r0, rc, False, jnp.zeros((rc, cb), F32))

        @pl.when(i == 0)
        def _():
            dw_ref[...] = jnp.zeros_like(dw_ref)

        for k in range(CONV_K):
            d = CONV_K - 1 - k
            part = jnp.zeros((rc, cb), F32)
            for r0 in range(0, tm, rc):
                part = part + dc_ref[r0:r0 + rc, :] * uext_ref[HALO31 - d + r0:HALO31 - d + r0 + rc, :]
            dw_ref[k:k + 1, :] += _colsum(part)

    return _call(body, name, (c // cb, n_tiles),
                 [tile, nxt, tile, prev, pl.BlockSpec((None, CONV_K, cb), lambda n, i: (l, 0, n))],
                 [tile, pl.BlockSpec((HALO31, cb), lambda n, i: (0, n))],
                 [_sds((t, c), F32), _sds((HALO31, c), F32)],
                 [pltpu.VMEM((tm + HALO31, cb), F32), pltpu.VMEM((HALO31 + tm, cb), F32)])(dc, dc, u, u, w3)


def _ln_silu(name, cx, g3, b3, l):
    t, d = cx.shape
    tm = min(TM_EW, t)

    def body(c_ref, g_ref, b_ref, o_ref):
        xhat, _ = _ln_stats(c_ref[...])
        nv = xhat * g_ref[...] + b_ref[...]
        o_ref[...] = (nv * _sigmoid(nv)).astype(BF16)

    return _call(body, name, (t // tm,), [_row_spec(tm, d), _param_spec(l, d), _param_spec(l, d)],
                 _row_spec(tm, d), _sds((t, d), BF16))(cx, g3, b3)


def _ln_silu_bwd(name, ds, cx, g3, b3, l, deps=()):
    t, d = cx.shape
    tm = min(TM_EW, t)

    def body(ds_ref, c_ref, g_ref, b_ref, dc_ref, dg_ref, db_ref, dsum_ref):
        first = pl.program_id(0) == 0
        xhat, rstd = _ln_stats(c_ref[...])
        g = g_ref[...]
        nv = xhat * g + b_ref[...]
        sg = _sigmoid(nv)
        dn = ds_ref[...] * (sg * (1.0 + nv * (1.0 - sg)))
        dc = _ln_backward(dn * g, xhat, rstd)
        dc_ref[...] = dc
        _acc_rows(dg_ref, _colsum(dn * xhat), first)
        _acc_rows(db_ref, _colsum(dn), first)
        _acc_rows(dsum_ref, _colsum(dc), first)

    vec = pl.BlockSpec((1, d), lambda i: (0, 0))
    return _call(body, name, (t // tm,),
                 [_row_spec(tm, d), _row_spec(tm, d), _param_spec(l, d), _param_spec(l, d)],
                 [_row_spec(tm, d), vec, vec, vec],
                 [_sds((t, d), F32)] + [_sds((1, d), F32)] * 3, deps=deps)(ds, cx, g3, b3)


def _blk_specs(tm, fb, halo, n_tiles, shift):
    per = tm // halo
    tile = pl.BlockSpec((None, tm, fb), lambda n, i: (n + shift, i, 0))
    prev = pl.BlockSpec((None, halo, fb), lambda n, i: (n + shift, jnp.maximum(i * per - 1, 0), 0))
    nxt = pl.BlockSpec((None, halo, fb), lambda n, i: (n + shift, jnp.minimum((i + 1) * per, n_tiles * per - 1), 0))
    return tile, prev, nxt


def _ffn_conv(ext_ref, halo_ref, x_ref, w_ref, b_ref, keep, tm, fb, rc, out_ref):
    ext_ref[0:HALO3, :] = halo_ref[...] * keep
    ext_ref[HALO3:HALO3 + tm, :] = x_ref[...]
    for r0 in range(0, tm, rc):
        init = jnp.broadcast_to(b_ref[...], (rc, fb))
        out_ref[r0:r0 + rc, :] = _conv_taps(ext_ref, w_ref, FFN_K, HALO3 + r0, rc, True, init)


def _ffn_act(name, h, wdw, bdw, l, seq):
    nb, t, fb = h.shape
    half = nb // 2
    tm = min(TM_EW, seq)
    seq_tiles, n_tiles = seq // tm, t // tm
    rc = _row_chunk(fb)
    g_tile, g_prev, _ = _blk_specs(tm, fb, HALO3, n_tiles, 0)
    v_tile, v_prev, _ = _blk_specs(tm, fb, HALO3, n_tiles, half)

    def body(g_ref, gp_ref, v_ref, vp_ref, wg_ref, wv_ref, bg_ref, bv_ref, o_ref, ext_ref, cg_ref, cv_ref):
        keep = (pl.program_id(1) % seq_tiles != 0).astype(F32)
        _ffn_conv(ext_ref, gp_ref, g_ref, wg_ref, bg_ref, keep, tm, fb, rc, cg_ref)
        _ffn_conv(ext_ref, vp_ref, v_ref, wv_ref, bv_ref, keep, tm, fb, rc, cv_ref)
        cg = cg_ref[...]
        o_ref[...] = (cg * _sigmoid(cg) * cv_ref[...]).astype(BF16)

    def wspec(shift, rows):
        return pl.BlockSpec((None, None, rows, fb), lambda n, i: (l, n + shift, 0, 0))

    return _call(body, name, (half, n_tiles),
                 [g_tile, g_prev, v_tile, v_prev, wspec(0, FFN_K), wspec(half, FFN_K), wspec(0, 1), wspec(half, 1)],
                 pl.BlockSpec((None, tm, fb), lambda n, i: (n, i, 0)), _sds((half, t, fb), BF16),
                 [pltpu.VMEM((HALO3 + tm, fb), F32), pltpu.VMEM((tm, fb), F32), pltpu.VMEM((tm, fb), F32)]
                 )(h, h, h, h, wdw, wdw, bdw, bdw)


def _ffn_act_bwd(name, da, h, wdw, bdw, l, seq, deps=()):
    nb, t, fb = h.shape
    half = nb // 2
    tm = min(TM_EW, seq)
    seq_tiles, n_tiles = seq // tm, t // tm
    rc = _row_chunk(fb)
    g_tile, g_prev, _ = _blk_specs(tm, fb, HALO3, n_tiles, 0)
    v_tile, v_prev, _ = _blk_specs(tm, fb, HALO3, n_tiles, half)

    def body(da_ref, g_ref, gp_ref, v_ref, vp_ref, wg_ref, wv_ref, bg_ref, bv_ref,
             dg_ref, dv_ref, dbg_ref, dbv_ref, dwg_ref, dwv_ref, gext_ref, vext_ref, cg_ref, cv_ref):
        i = pl.program_id(1)
        first = i == 0
        keep = (i % seq_tiles != 0).astype(F32)
        _ffn_conv(gext_ref, gp_ref, g_ref, wg_ref, bg_ref, keep, tm, fb, rc, cg_ref)
        _ffn_conv(vext_ref, vp_ref, v_ref, wv_ref, bv_ref, keep, tm, fb, rc, cv_ref)
        cg, cv, da_v = cg_ref[...], cv_ref[...], da_ref[...]
        sg = _sigmoid(cg)
        dcv = da_v * cg * sg
        dcg = da_v * cv * sg * (1.0 + cg * (1.0 - sg))
        dg_ref[...] = dcg
        dv_ref[...] = dcv
        _acc_rows(dbg_ref, _colsum(dcg), first)
        _acc_rows(dbv_ref, _colsum(dcv), first)

        @pl.when(first)
        def _():
            dwg_ref[...] = jnp.zeros_like(dwg_ref)
            dwv_ref[...] = jnp.zeros_like(dwv_ref)

        for k in range(FFN_K):
            d = FFN_K - 1 - k
            dwg_ref[k:k + 1, :] += _colsum(dcg * gext_ref[HALO3 - d:HALO3 - d + tm, :])
            dwv_ref[k:k + 1, :] += _colsum(dcv * vext_ref[HALO3 - d:HALO3 - d + tm, :])

    def wspec(shift, rows):
        return pl.BlockSpec((None, None, rows, fb), lambda n, i: (l, n + shift, 0, 0))

    ext = pltpu.VMEM((HALO3 + tm, fb), F32)
    tmp = pltpu.VMEM((tm, fb), F32)
    outs = _call(body, name, (half, n_tiles),
                 [pl.BlockSpec((None, tm, fb), lambda n, i: (n, i, 0)), g_tile, g_prev, v_tile, v_prev,
                  wspec(0, FFN_K), wspec(half, FFN_K), wspec(0, 1), wspec(half, 1)],
                 [pl.BlockSpec((None, tm, fb), lambda n, i: (n, i, 0)),
                  pl.BlockSpec((None, tm, fb), lambda n, i: (n, i, 0)),
                  pl.BlockSpec((None, 1, fb), lambda n, i: (n, 0, 0)), pl.BlockSpec((None, 1, fb), lambda n, i: (n, 0, 0)),
                  pl.BlockSpec((None, SUBLANES, fb), lambda n, i: (n, 0, 0)),
                  pl.BlockSpec((None, SUBLANES, fb), lambda n, i: (n, 0, 0))],
                 [_sds((half, t, fb), F32), _sds((half, t, fb), F32), _sds((half, 1, fb), F32),
                  _sds((half, 1, fb), F32), _sds((half, SUBLANES, fb), F32), _sds((half, SUBLANES, fb), F32)],
                 [ext, ext, tmp, tmp], deps=deps)(da, h, h, h, h, wdw, wdw, bdw, bdw)
    return outs


def _ffn_conv_t(name, dhc, wdw, l, shift, seq):
    half, t, fb = dhc.shape
    tm = min(TM_EW, seq)
    seq_tiles, n_tiles = seq // tm, t // tm
    rc = _row_chunk(fb)
    tile, _, nxt = _blk_specs(tm, fb, HALO3, n_tiles, 0)

    def body(d_ref, dn_ref, w_ref, dh_ref, db_ref, ext_ref, out_ref):
        i = pl.program_id(1)
        keep = (i % seq_tiles != seq_tiles - 1).astype(F32)
        ext_ref[0:tm, :] = d_ref[...]
        ext_ref[tm:tm + HALO3, :] = dn_ref[...] * keep
        for r0 in range(0, tm, rc):
            out_ref[r0:r0 + rc, :] = _conv_taps(ext_ref, w_ref, FFN_K, r0, rc, False, jnp.zeros((rc, fb), F32))
        dh = out_ref[...]
        dh_ref[...] = dh.astype(BF16)
        _acc_rows(db_ref, _colsum(dh), i == 0)

    return _call(body, name, (half, n_tiles),
                 [tile, nxt, pl.BlockSpec((None, None, FFN_K, fb), lambda n, i: (l, n + shift, 0, 0))],
                 [tile, pl.BlockSpec((None, 1, fb), lambda n, i: (n, 0, 0))],
                 [_sds((half, t, fb), BF16), _sds((half, 1, fb), F32)],
                 [pltpu.VMEM((tm + HALO3, fb), F32), pltpu.VMEM((tm, fb), F32)])(dhc, dhc, wdw)


def _gelu_parts(h):
    cdf = 0.5 * (1.0 + lax.erf(h * INV_SQRT2))
    return h * cdf, cdf


def _sgu(name, h, g3, b3, wc, bsb, l):
    t, c2 = h.shape
    c = c2 // 2
    tm = min(TM_EW, t)

    def body(h_ref, g_ref, b_ref, wc_ref, bs_ref, o_ref):
        z, _ = _gelu_parts(h_ref[...])
        u = z[:, :c]
        xhat, _ = _ln_stats(z[:, c:])
        vnb = (xhat * g_ref[...] + b_ref[...]).astype(BF16)
        for q in range(tm // CHUNK):
            for gi in range(GROUPS):
                rs, cs = slice(q * CHUNK, (q + 1) * CHUNK), slice(gi * CHUNK, (gi + 1) * CHUNK)
                sp = jnp.dot(wc_ref[gi], vnb[rs, cs], preferred_element_type=F32) + bs_ref[gi]
                o_ref[rs, cs] = (u[rs, cs] * sp).astype(BF16)

    grp = pl.BlockSpec((None, GROUPS, CHUNK, CHUNK), lambda i: (l, 0, 0, 0))
    return _call(body, name, (t // tm,), [_row_spec(tm, c2), _param_spec(l, c), _param_spec(l, c), grp, grp],
                 _row_spec(tm, c), _sds((t, c), BF16))(h, g3, b3, wc, bsb)


def _sgu_bwd(name, dq, h, g3, b3, wc, bsb, l, deps=()):
    t, c2 = h.shape
    c = c2 // 2
    tm = min(TM_EW, t)
    n_tiles = t // tm

    def body(dq_ref, h_ref, g_ref, b_ref, wc_ref, bs_ref,
             dh_ref, dbin_ref, dw_ref, dbs_ref, dg_ref, db_ref, du_ref, dvn_ref, bsum_ref):
        i = pl.program_id(0)
        first = i == 0
        hv = h_ref[...]
        z, cdf = _gelu_parts(hv)
        u = z[:, :c]
        xhat, rstd = _ln_stats(z[:, c:])
        g = g_ref[...]
        vnb = (xhat * g + b_ref[...]).astype(BF16)

        @pl.when(first)
        def _():
            dw_ref[...] = jnp.zeros_like(dw_ref)
            bsum_ref[...] = jnp.zeros_like(bsum_ref)

        for q in range(tm // CHUNK):
            for gi in range(GROUPS):
                rs, cs = slice(q * CHUNK, (q + 1) * CHUNK), slice(gi * CHUNK, (gi + 1) * CHUNK)
                vb = vnb[rs, cs]
                w = wc_ref[gi]
                sp = jnp.dot(w, vb, preferred_element_type=F32) + bs_ref[gi]
                dqb = dq_ref[rs, cs]
                du_ref[rs, cs] = dqb * sp
                dsp = dqb * u[rs, cs]
                bsum_ref[gi] += dsp
                dspb = dsp.astype(BF16)
                dw_ref[gi] += lax.dot_general(dspb, vb, (_DIMS["nt"], ((), ())), preferred_element_type=F32)
                dvn_ref[rs, cs] = lax.dot_general(w, dspb, (_DIMS["tn"], ((), ())), preferred_element_type=F32)

        dvn = dvn_ref[...]
        dv = _ln_backward(dvn * g, xhat, rstd)
        pdf = jnp.exp(-0.5 * hv * hv) * INV_SQRT2PI
        dgelu = cdf + hv * pdf
        dhu = du_ref[...] * dgelu[:, :c]
        dhv = dv * dgelu[:, c:]
        dh_ref[:, :c] = dhu.astype(BF16)
        dh_ref[:, c:] = dhv.astype(BF16)
        _acc_rows(dbin_ref.at[:, :c], _colsum(dhu), first)
        _acc_rows(dbin_ref.at[:, c:], _colsum(dhv), first)
        _acc_rows(dg_ref, _colsum(dvn * xhat), first)
        _acc_rows(db_ref, _colsum(dvn), first)

        @pl.when(i == n_tiles - 1)
        def _():
            row = lax.broadcasted_iota(jnp.int32, (CHUNK, CHUNK), 0)
            col = lax.broadcasted_iota(jnp.int32, (CHUNK, CHUNK), 1)
            for gi in range(GROUPS):
                dw_ref[gi] = jnp.where(row >= col, dw_ref[gi], 0.0)
            dbs_ref[...] = jnp.sum(bsum_ref[...], axis=-1)

    grp = pl.BlockSpec((None, GROUPS, CHUNK, CHUNK), lambda i: (l, 0, 0, 0))
    vec = pl.BlockSpec((1, c), lambda i: (0, 0))
    return _call(body, name, (n_tiles,),
                 [_row_spec(tm, c), _row_spec(tm, c2), _param_spec(l, c), _param_spec(l, c), grp, grp],
                 [_row_spec(tm, c2), pl.BlockSpec((1, c2), lambda i: (0, 0)),
                  pl.BlockSpec((GROUPS, CHUNK, CHUNK), lambda i: (0, 0, 0)),
                  pl.BlockSpec((GROUPS, CHUNK), lambda i: (0, 0)), vec, vec],
                 [_sds((t, c2), BF16), _sds((1, c2), F32), _sds((GROUPS, CHUNK, CHUNK), F32),
                  _sds((GROUPS, CHUNK), F32), _sds((1, c), F32), _sds((1, c), F32)],
                 [pltpu.VMEM((tm, c), F32), pltpu.VMEM((tm, c), F32), pltpu.VMEM((GROUPS, CHUNK, CHUNK), F32)],
                 deps=deps)(dq, h, g3, b3, wc, bsb)


def _loss(name, y, target):
    t, d = y.shape
    tm = min(TM_EW, t)
    n_tiles = t // tm

    def body(y_ref, t_ref, l_ref, dy_ref, acc_ref):
        i = pl.program_id(0)
        diff = y_ref[...] - t_ref[...]
        dy_ref[...] = diff * (1.0 / d)
        _acc_rows(acc_ref, _colsum(diff * diff), i == 0)

        @pl.when(i == n_tiles - 1)
        def _():
            l_ref[...] = jnp.broadcast_to(jnp.sum(acc_ref[...], axis=-1, keepdims=True) * (0.5 / d), (1, LANES))

    return _call(body, name, (n_tiles,), [_row_spec(tm, d), _row_spec(tm, d)],
                 [pl.BlockSpec((1, LANES), lambda i: (0, 0)), _row_spec(tm, d)],
                 [_sds((1, LANES), F32), _sds((t, d), F32)], [pltpu.VMEM((1, d), F32)])(y, target)


def _adamw(g, w, m, v):
    m2 = ADAM_B1 * m + (1.0 - ADAM_B1) * g
    v2 = ADAM_B2 * v + (1.0 - ADAM_B2) * (g * g)
    m_hat = m2 / (1.0 - ADAM_B1 ** ADAM_STEP)
    v_hat = v2 / (1.0 - ADAM_B2 ** ADAM_STEP)
    delta = -ADAM_LR * (m_hat / (jnp.sqrt(v_hat) + ADAM_EPS) + ADAM_WD * w)
    return delta, m2, v2


ROW_TILE_CAP = 512


def _row_tile(rows, cap=ROW_TILE_CAP):
    if rows <= cap:
        return rows
    for tr in range(cap, 15, -16):
        if rows % tr == 0:
            return tr
    return rows


def _sum8_adamw(name, dev, lands, parts, w, m, v):
    nl = len(lands)
    _, r, c = lands[0].shape
    tr = _row_tile(r, cap=128)

    def body(dev_ref, *refs):
        land, own = refs[:nl], refs[nl:2 * nl]
        w_ref, m_ref, v_ref, g_ref, d_ref, m2_ref, v2_ref = refs[2 * nl:]
        layer, me = pl.program_id(0), dev_ref[0]
        for l in range(nl):
            @pl.when(layer == l)
            def _(l=l):
                g = None
                for s in range(N_DEV):
                    part = jnp.where(me == s, own[l][...], land[l][s]).astype(F32)
                    g = part if g is None else g + part
                delta, m2, v2 = _adamw(g, w_ref[...], m_ref[...], v_ref[...])
                g_ref[...] = g
                d_ref[...] = delta
                m2_ref[...] = m2
                v2_ref[...] = v2

    def rows_of(l, a, i):
        return jnp.where(a == l, i, 0)

    spec = pl.BlockSpec((None, tr, c), lambda a, i, dev_ref: (a, i, 0))
    in_specs = [pl.BlockSpec((N_DEV, tr, c), lambda a, i, dev_ref, l=l: (0, rows_of(l, a, i), 0)) for l in range(nl)]
    in_specs += [pl.BlockSpec((None, tr, c), lambda a, i, dev_ref, l=l: (dev_ref[0], rows_of(l, a, i), 0))
                 for l in range(nl)]
    grid_spec = pltpu.PrefetchScalarGridSpec(
        num_scalar_prefetch=1, grid=(nl, r // tr), in_specs=in_specs + [spec] * 3, out_specs=[spec] * 4)
    return pl.pallas_call(
        body, name=name, grid_spec=grid_spec, out_shape=[_sds(w.shape, F32)] * 4,
        compiler_params=pltpu.CompilerParams(vmem_limit_bytes=VMEM_LIMIT))(dev, *lands, *parts, w, m, v)


def _sum8(name, parts):
    _, r, c = parts.shape
    tr = _row_tile(r)

    def body(p_ref, o_ref):
        acc = p_ref[0]
        for s in range(1, N_DEV):
            acc = acc + p_ref[s]
        o_ref[...] = acc

    return _call(body, name, (r // tr,), [pl.BlockSpec((N_DEV, tr, c), lambda i: (0, i, 0))],
                 pl.BlockSpec((tr, c), lambda i: (i, 0)), _sds((r, c), F32))(parts)


def _adamw_flat(name, g, w, m, v):
    r, c = g.shape
    tr = _row_tile(r)

    def body(g_ref, w_ref, m_ref, v_ref, d_ref, m2_ref, v2_ref):
        delta, m2, v2 = _adamw(g_ref[...], w_ref[...], m_ref[...], v_ref[...])
        d_ref[...] = delta
        m2_ref[...] = m2
        v2_ref[...] = v2

    spec = pl.BlockSpec((tr, c), lambda i: (i, 0))
    return _call(body, name, (r // tr,), [spec] * 4, [spec] * 3, [_sds((r, c), F32)] * 3)(g, w, m, v)


def _pack(arrs, row_multiple=SUBLANES):
    pieces, rows = [], 0
    for a in arrs:
        piece = a.reshape(-1, LANES)
        piece = jnp.pad(piece, ((0, (-piece.shape[0]) % SUBLANES), (0, 0)))
        pieces.append(piece)
        rows += piece.shape[0]
    if rows % row_multiple:
        pieces.append(jnp.zeros(((-rows) % row_multiple, LANES), pieces[0].dtype))
    return jnp.concatenate(pieces, axis=0)


def _unpack(buf, shapes, lead=0):
    out, pos = [], 0
    for shp in shapes:
        rows = math.prod(shp) // LANES
        piece = lax.slice_in_dim(buf, pos, pos + rows, axis=lead)
        out.append(piece.reshape(buf.shape[:lead] + tuple(shp)))
        pos += rows + (-rows) % SUBLANES
    return out


REPLICATED = ["conv_b_in", "conv_b_dw", "conv_ln_g", "conv_ln_b", "conv_b_out", "gmlp_w_s", "gmlp_b_s",
              "ffn_b_up", "ffn_b_dw", "ffn_b_down", "norm1_g", "norm1_b", "norm2_g", "norm2_b"]
SMALL_SHARDED = ["conv_w_dw", "gmlp_b_in", "gmlp_ln_g", "gmlp_ln_b", "gmlp_b_out", "ffn_w_dw"]
BIG = ["conv_w_in", "conv_w_out", "gmlp_w_in", "gmlp_w_out", "ffn_w_up", "ffn_w_down"]
WEIGHTS = ["conv_w_in", "conv_b_in", "conv_w_dw", "conv_b_dw", "conv_ln_g", "conv_ln_b", "conv_w_out", "conv_b_out",
           "gmlp_w_in", "gmlp_b_in", "gmlp_ln_g", "gmlp_ln_b", "gmlp_w_s", "gmlp_b_s", "gmlp_w_out", "gmlp_b_out",
           "ffn_w_up", "ffn_b_up", "ffn_w_dw", "ffn_b_dw", "ffn_w_down", "ffn_b_down",
           "norm1_g", "norm1_b", "norm2_g", "norm2_b"]


def _from_shards(g, lead_shape):
    nd = len(lead_shape)
    perm = tuple(range(1, nd + 1)) + (0, nd + 1)
    return g.transpose(perm).reshape(tuple(lead_shape) + (-1,))


def _to_shards(full, width):
    lead = full.shape[:-1]
    nd = len(lead)
    parts = full.reshape(lead + (N_DEV, width))
    return parts.transpose((nd,) + tuple(range(nd)) + (nd + 1,))


def _step(p):
    x_in, target_in = p["x"], p["loss_target"]
    bsz, seq, d = x_in.shape
    t = bsz * seq
    x0 = x_in.reshape(t, d)
    target = target_in.reshape(t, d)
    n_conv, n_gmlp = p["conv_w_in"].shape[0], p["gmlp_w_in"].shape[0]
    fb = p["ffn_w_up"].shape[-1]
    nblk = N_DEV
    half = nblk // 2
    cw = p["conv_w_in"].shape[-1]
    tm = min(TM_MM, t)
    nt = t // tm
    dev = 4 * lax.axis_index("x") + 2 * lax.axis_index("y") + lax.axis_index("c")

    w_src = []
    for i in range(DEPTH):
        mix = "conv" if i % 2 == 0 else "gmlp"
        w_src += [p[mix + "_w_in"][i // 2].astype(BF16), p[mix + "_w_out"][i // 2].astype(BF16),
                  p["ffn_w_up"][i].astype(BF16), p["ffn_w_down"][i].astype(BF16)]
    send_sems, recv_sems, w_land = _gather_start("weights_gather_start", _place_own("weights_place_own", w_src))
    W_IN, W_OUT, W_UP, W_DOWN = range(4)

    def wait_weight(i, k, after):
        return _gather_wait(f"l{i}_weights_wait{k}", w_land[4 * i + k], send_sems, recv_sems, 4 * i + k, after)

    small_shapes = [p[n].shape for n in SMALL_SHARDED]
    small_src = _pack([p[n] for n in SMALL_SHARDED])[None]
    sm = _unpack(_all_gather("gather_small_weights", [small_src])[0][0], small_shapes, lead=1)
    conv_w_dw = _from_shards(sm[0], sm[0].shape[1:-1])
    gmlp_b_in = _from_shards(sm[1], sm[1].shape[1:-1])
    gmlp_ln_g = _from_shards(sm[2], sm[2].shape[1:-1])
    gmlp_ln_b = _from_shards(sm[3], sm[3].shape[1:-1])
    gmlp_b_out = _from_shards(sm[4], sm[4].shape[1:-1])
    ffn_w_dw = sm[5].transpose(1, 0, 2, 3)

    def rows3(a):
        return a.reshape(a.shape[0], 1, a.shape[-1])

    conv_b_in4 = p["conv_b_in"].reshape(n_conv, N_DEV, 1, cw)
    gmlp_b_in4 = gmlp_b_in.reshape(n_gmlp, N_DEV, 1, cw)
    ffn_b_up4 = p["ffn_b_up"].reshape(DEPTH, nblk, 1, fb)
    ffn_b_dw4 = p["ffn_b_dw"].reshape(DEPTH, nblk, 1, fb)
    conv_b_dw3, conv_ln_g3, conv_ln_b3 = rows3(p["conv_b_dw"]), rows3(p["conv_ln_g"]), rows3(p["conv_ln_b"])
    conv_b_out3, gmlp_b_out3, ffn_b_down3 = rows3(p["conv_b_out"]), rows3(gmlp_b_out), rows3(p["ffn_b_down"])
    gmlp_ln_g3, gmlp_ln_b3 = rows3(gmlp_ln_g), rows3(gmlp_ln_b)
    n1g3, n1b3, n2g3, n2b3 = rows3(p["norm1_g"]), rows3(p["norm1_b"]), rows3(p["norm2_g"]), rows3(p["norm2_b"])
    tril = jnp.tril(jnp.ones((CHUNK, CHUNK), dtype=bool))
    w_causal = jnp.where(tril, p["gmlp_w_s"], 0.0).astype(BF16)
    bs_rows = jnp.broadcast_to(p["gmlp_b_s"][..., None], p["gmlp_b_s"].shape + (CHUNK,))

    def mm_in(name, xa, wg, l, bias4):
        return _matmul(name, xa, wg, "nn", grid=(nt, N_DEV),
                       a_spec=pl.BlockSpec((tm, d), lambda i, n: (i, 0)),
                       b_spec=pl.BlockSpec((None, d, cw), lambda i, n: (n, 0, 0)),
                       o_spec=pl.BlockSpec((tm, cw), lambda i, n: (i, n)), o_shape=(t, N_DEV * cw), o_dtype=F32,
                       bias=bias4, bias_spec=pl.BlockSpec((None, None, 1, cw), lambda i, n: (l, n, 0, 0)))

    def mm_out(name, sa, w, l, bias3):
        return _matmul(name, sa, w, "nn", grid=(nt,),
                       a_spec=pl.BlockSpec((tm, d), lambda i: (i, 0)),
                       b_spec=pl.BlockSpec((d, d), lambda i: (0, 0)),
                       o_spec=pl.BlockSpec((tm, d), lambda i: (i, 0)), o_shape=(t, d), o_dtype=F32,
                       bias=bias3, bias_spec=pl.BlockSpec((None, 1, d), lambda i: (l, 0, 0)))

    def mm_out_dx(name, dy, w):
        return _matmul(name, dy, w, "nt", grid=(nt,),
                       a_spec=pl.BlockSpec((tm, d), lambda i: (i, 0)),
                       b_spec=pl.BlockSpec((d, d), lambda i: (0, 0)),
                       o_spec=pl.BlockSpec((tm, d), lambda i: (i, 0)), o_shape=(t, d), o_dtype=F32)

    def mm_out_dw(name, sa, dy):
        return _matmul(name, sa, dy, "tn", grid=(nt,), k_axis=0, nk=nt, acc_shape=(d, d),
                       a_spec=pl.BlockSpec((tm, d), lambda k: (k, 0)),
                       b_spec=pl.BlockSpec((tm, d), lambda k: (k, 0)),
                       o_spec=pl.BlockSpec((d, d), lambda k: (0, 0)), o_shape=(d, d), o_dtype=BF16)

    def mm_in_dx(name, dh, wg, res):
        return _matmul(name, dh, wg, "nt", grid=(nt, N_DEV), k_axis=1, nk=N_DEV, acc_shape=(tm, d),
                       a_spec=pl.BlockSpec((tm, cw), lambda i, n: (i, n)),
                       b_spec=pl.BlockSpec((None, d, cw), lambda i, n: (n, 0, 0)),
                       o_spec=pl.BlockSpec((tm, d), lambda i, n: (i, 0)), o_shape=(t, d), o_dtype=F32,
                       res=res, res_spec=pl.BlockSpec((tm, d), lambda i, n: (i, 0)), res_scale=ALPHA)

    def mm_in_dw(name, xa, dh):
        return _matmul(name, xa, dh, "tn", grid=(N_DEV, nt), k_axis=1, nk=nt, acc_shape=(d, cw),
                       a_spec=pl.BlockSpec((tm, d), lambda n, k: (k, 0)),
                       b_spec=pl.BlockSpec((tm, cw), lambda n, k: (k, n)),
                       o_spec=pl.BlockSpec((None, d, cw), lambda n, k: (n, 0, 0)),
                       o_shape=(N_DEV, d, cw), o_dtype=BF16)

    def mm_up(name, xa, w, l):
        return _matmul(name, xa, w, "nn", grid=(nt, nblk),
                       a_spec=pl.BlockSpec((tm, d), lambda i, n: (i, 0)),
                       b_spec=pl.BlockSpec((None, d, fb), lambda i, n: (n, 0, 0)),
                       o_spec=pl.BlockSpec((None, tm, fb), lambda i, n: (n, i, 0)), o_shape=(nblk, t, fb),
                       o_dtype=F32, bias=ffn_b_up4,
                       bias_spec=pl.BlockSpec((None, None, 1, fb), lambda i, n: (l, n, 0, 0)))

    def mm_down(name, a, w, l):
        return _matmul(name, a, w, "nn", grid=(nt, half), k_axis=1, nk=half, acc_shape=(tm, d),
                       a_spec=pl.BlockSpec((None, tm, fb), lambda i, n: (n, i, 0)),
                       b_spec=pl.BlockSpec((None, fb, d), lambda i, n: (n, 0, 0)),
                       o_spec=pl.BlockSpec((tm, d), lambda i, n: (i, 0)), o_shape=(t, d), o_dtype=F32,
                       bias=ffn_b_down3, bias_spec=pl.BlockSpec((None, 1, d), lambda i, n: (l, 0, 0)))

    def mm_down_da(name, dy, w):
        return _matmul(name, dy, w, "nt", grid=(nt, half),
                       a_spec=pl.BlockSpec((tm, d), lambda i, n: (i, 0)),
                       b_spec=pl.BlockSpec((None, fb, d), lambda i, n: (n, 0, 0)),
                       o_spec=pl.BlockSpec((None, tm, fb), lambda i, n: (n, i, 0)), o_shape=(half, t, fb),
                       o_dtype=F32)

    def mm_down_dw(name, a, dy):
        return _matmul(name, a, dy, "tn", grid=(half, nt), k_axis=1, nk=nt, acc_shape=(fb, d),
                       a_spec=pl.BlockSpec((None, tm, fb), lambda n, k: (n, k, 0)),
                       b_spec=pl.BlockSpec((tm, d), lambda n, k: (k, 0)),
                       o_spec=pl.BlockSpec((None, fb, d), lambda n, k: (n, 0, 0)),
                       o_shape=(half, fb, d), o_dtype=BF16)

    def mm_up_dx(name, dh_g, dh_v, w, res):
        part = _matmul(name + "_g", dh_g, w, "nt", grid=(nt, half), k_axis=1, nk=half, acc_shape=(tm, d),
                       a_spec=pl.BlockSpec((None, tm, fb), lambda i, n: (n, i, 0)),
                       b_spec=pl.BlockSpec((None, d, fb), lambda i, n: (n, 0, 0)),
                       o_spec=pl.BlockSpec((tm, d), lambda i, n: (i, 0)), o_shape=(t, d), o_dtype=F32,
                       res=res, res_spec=pl.BlockSpec((tm, d), lambda i, n: (i, 0)), res_scale=ALPHA)
        return _matmul(name + "_v", dh_v, w, "nt", grid=(nt, half), k_axis=1, nk=half, acc_shape=(tm, d),
                       a_spec=pl.BlockSpec((None, tm, fb), lambda i, n: (n, i, 0)),
                       b_spec=pl.BlockSpec((None, d, fb), lambda i, n: (n + half, 0, 0)),
                       o_spec=pl.BlockSpec((tm, d), lambda i, n: (i, 0)), o_shape=(t, d), o_dtype=F32,
                       res=part, res_spec=pl.BlockSpec((tm, d), lambda i, n: (i, 0)), res_scale=1.0)

    def mm_up_dw(name, xa, dh_half, shift, prev=None):
        return _matmul(name, xa, dh_half, "tn", grid=(half, nt), k_axis=1, nk=nt, acc_shape=(d, fb),
                       a_spec=pl.BlockSpec((tm, d), lambda n, k: (k, 0)),
                       b_spec=pl.BlockSpec((None, tm, fb), lambda n, k: (n, k, 0)),
                       o_spec=pl.BlockSpec((None, d, fb), lambda n, k: (n + shift, 0, 0)),
                       o_shape=(nblk, d, fb), o_dtype=BF16, prev=prev)

    saved = []
    xcur = x0
    for i in range(DEPTH):
        j = i // 2
        s = {"x": xcur}
        s["w_in"] = wait_weight(i, W_IN, xcur)
        if i % 2 == 0:
            s["h"] = mm_in(f"l{i}_conv_in", xcur, s["w_in"], j, conv_b_in4)
            s["u"] = _glu(f"l{i}_glu", s["h"])
            s["c"] = _dwconv31(f"l{i}_dwconv", s["u"], conv_w_dw, conv_b_dw3, j, seq)
            s["s"] = _ln_silu(f"l{i}_ln_silu", s["c"], conv_ln_g3, conv_ln_b3, j)
            s["w_out"] = wait_weight(i, W_OUT, s["s"]).reshape(d, d)
            y = mm_out(f"l{i}_conv_out", s["s"], s["w_out"], j, conv_b_out3)
        else:
            s["h"] = mm_in(f"l{i}_gmlp_in", xcur, s["w_in"], j, gmlp_b_in4)
            s["s"] = _sgu(f"l{i}_sgu", s["h"], gmlp_ln_g3, gmlp_ln_b3, w_causal, bs_rows, j)
            s["w_out"] = wait_weight(i, W_OUT, s["s"]).reshape(d, d)
            y = mm_out(f"l{i}_gmlp_out", s["s"], s["w_out"], j, gmlp_b_out3)
        s["x1"], s["xhat1"], s["rstd1"] = _ln_res(f"l{i}_norm1", xcur, y, n1g3, n1b3, i)
        s["w_up"] = wait_weight(i, W_UP, s["x1"])
        s["fh"] = mm_up(f"l{i}_ffn_up", s["x1"], s["w_up"], i)
        s["a"] = _ffn_act(f"l{i}_ffn_act", s["fh"], ffn_w_dw, ffn_b_dw4, i, seq)
        s["w_down"] = wait_weight(i, W_DOWN, s["a"]).reshape(half, fb, d)
        y = mm_down(f"l{i}_ffn_down", s["a"], s["w_down"], i)
        xcur, s["xhat2"], s["rstd2"] = _ln_res(f"l{i}_norm2", s["x1"], y, n2g3, n2b3, i)
        saved.append(s)

    loss_row, dx = _loss("loss", xcur, target)

    started = {n: [None] * p[n].shape[0] for n in BIG}
    tokens = []

    def send_grad(n, l, g):
        send_sem, recv_sem, parts, land, token = _scatter_start(f"grad_{n}{l}_scatter_start", g)
        started[n][l] = (send_sem, recv_sem, parts, land)
        tokens.append(token)

    def take_tokens():
        out = list(tokens)
        tokens.clear()
        return out

    gl = {n: [None] * p[n].shape[0] for n in REPLICATED + SMALL_SHARDED}
    for i in reversed(range(DEPTH)):
        j = i // 2
        s = saved[i]
        mix = "conv" if i % 2 == 0 else "gmlp"
        dr2, gl["norm2_g"][i], gl["norm2_b"][i], gl["ffn_b_down"][i] = _ln_res_bwd(
            f"l{i}_norm2_bwd", dx, s["xhat2"], s["rstd2"], n2g3, i, deps=take_tokens())
        da = mm_down_da(f"l{i}_ffn_down_da", dr2, s["w_down"])
        send_grad("ffn_w_down", i, mm_down_dw(f"l{i}_ffn_down_dw", s["a"], dr2).reshape(N_DEV, -1, d))
        dcg, dcv, dbg, dbv, dwg, dwv = _ffn_act_bwd(f"l{i}_ffn_act_bwd", da, s["fh"], ffn_w_dw, ffn_b_dw4, i, seq,
                                                    deps=take_tokens())
        gl["ffn_b_dw"][i] = jnp.concatenate([dbg, dbv], axis=0).reshape(1, nblk * fb)
        gl["ffn_w_dw"][i] = jnp.concatenate([dwg[:, :FFN_K], dwv[:, :FFN_K]], axis=0)
        dh_g, dbu_g = _ffn_conv_t(f"l{i}_ffn_conv_t_g", dcg, ffn_w_dw, i, 0, seq)
        dh_v, dbu_v = _ffn_conv_t(f"l{i}_ffn_conv_t_v", dcv, ffn_w_dw, i, half, seq)
        gl["ffn_b_up"][i] = jnp.concatenate([dbu_g, dbu_v], axis=0).reshape(1, nblk * fb)
        dx1 = mm_up_dx(f"l{i}_ffn_up_dx", dh_g, dh_v, s["w_up"], dr2)
        g_up = mm_up_dw(f"l{i}_ffn_up_dw_g", s["x1"], dh_g, 0)
        send_grad("ffn_w_up", i, mm_up_dw(f"l{i}_ffn_up_dw_v", s["x1"], dh_v, half, g_up))
        dr1, gl["norm1_g"][i], gl["norm1_b"][i], gl[mix + "_b_out"][j] = _ln_res_bwd(
            f"l{i}_norm1_bwd", dx1, s["xhat1"], s["rstd1"], n1g3, i, deps=take_tokens())
        ds = mm_out_dx(f"l{i}_{mix}_out_dx", dr1, s["w_out"])
        send_grad(mix + "_w_out", j, mm_out_dw(f"l{i}_{mix}_out_dw", s["s"], dr1).reshape(N_DEV, -1, d))
        if i % 2 == 0:
            dc, gl["conv_ln_g"][j], gl["conv_ln_b"][j], gl["conv_b_dw"][j] = _ln_silu_bwd(
                f"l{i}_ln_silu_bwd", ds, s["c"], conv_ln_g3, conv_ln_b3, j, deps=take_tokens())
            du, dwdw = _dwconv31_bwd(f"l{i}_dwconv_bwd", dc, s["u"], conv_w_dw, j, seq)
            gl["conv_w_dw"][j] = dwdw[:CONV_K]
            dh, gl["conv_b_in"][j] = _glu_bwd(f"l{i}_glu_bwd", du, s["h"])
        else:
            dh, gl["gmlp_b_in"][j], gl["gmlp_w_s"][j], gl["gmlp_b_s"][j], gl["gmlp_ln_g"][j], gl["gmlp_ln_b"][j] = (
                _sgu_bwd(f"l{i}_sgu_bwd", ds, s["h"], gmlp_ln_g3, gmlp_ln_b3, w_causal, bs_rows, j,
                         deps=take_tokens()))
        dx = mm_in_dx(f"l{i}_{mix}_in_dx", dh, s["w_in"], dr1)
        send_grad(mix + "_w_in", j, mm_in_dw(f"l{i}_{mix}_in_dw", s["x"], dh))
    grad_x = dx.reshape(bsz, seq, d)

    full_small = {n: jnp.stack(gl[n]).reshape(p[n].shape) for n in REPLICATED}
    shard_small = {}
    for n in SMALL_SHARDED:
        if n == "ffn_w_dw":
            shard_small[n] = jnp.stack(gl[n]).transpose(1, 0, 2, 3)
        else:
            width = p[n].shape[-1]
            lead = p[n].shape[:-1]
            shard_small[n] = _to_shards(jnp.stack(gl[n]).reshape(lead + (N_DEV * width,)), width)
    flat_shapes = [(1, LANES)] + [p[n].shape for n in REPLICATED] + [(N_DEV,) + p[n].shape for n in SMALL_SHARDED]
    flat_local = _pack([loss_row] + [full_small[n] for n in REPLICATED] + [shard_small[n] for n in SMALL_SHARDED],
                       row_multiple=ROW_TILE_CAP)

    small_parts = _all_gather("gather_small_grads", [flat_local[None]])[0][0]
    small_sum = _sum8("sum_small_grads", small_parts)
    summed = _unpack(small_sum, flat_shapes)
    loss = summed[0][0, 0]
    grads = dict(zip(REPLICATED, summed[1:1 + len(REPLICATED)]))
    for n, g in zip(SMALL_SHARDED, summed[1 + len(REPLICATED):]):
        grads[n] = lax.dynamic_index_in_dim(g, dev, axis=0, keepdims=False)

    flat_started = [st for n in BIG for st in started[n]]
    parts_done, lands_done = _scatter_wait("grads_scatter_wait", flat_started, small_sum)

    delta, new_m, new_v = {}, {}, {}
    dev1 = jnp.reshape(dev, (1,)).astype(jnp.int32)
    pos = 0
    for n in BIG:
        nl = p[n].shape[0]
        grads[n], delta[n], new_m[n], new_v[n] = _sum8_adamw(
            f"adamw_{n}", dev1, lands_done[pos:pos + nl], parts_done[pos:pos + nl], p[n], p["m_" + n], p["v_" + n])
        pos += nl
    small = REPLICATED + SMALL_SHARDED
    small_shp = [p[n].shape for n in small]
    d_s, m_s, v_s = _adamw_flat("adamw_small", _pack([grads[n] for n in small]), _pack([p[n] for n in small]),
                                _pack([p["m_" + n] for n in small]), _pack([p["v_" + n] for n in small]))
    for n, dd, mm, vv in zip(small, _unpack(d_s, small_shp), _unpack(m_s, small_shp), _unpack(v_s, small_shp)):
        delta[n], new_m[n], new_v[n] = dd, mm, vv

    return (loss, grad_x, *[grads[n] for n in WEIGHTS], *[delta[n] for n in WEIGHTS],
            *[new_m[n] for n in WEIGHTS], *[new_v[n] for n in WEIGHTS])


def kernel(x, conv_w_in, conv_b_in, conv_w_dw, conv_b_dw, conv_ln_g, conv_ln_b, conv_w_out, conv_b_out, gmlp_w_in, gmlp_b_in, gmlp_ln_g, gmlp_ln_b, gmlp_w_s, gmlp_b_s, gmlp_w_out, gmlp_b_out, ffn_w_up, ffn_b_up, ffn_w_dw, ffn_b_dw, ffn_w_down, ffn_b_down, norm1_g, norm1_b, norm2_g, norm2_b, loss_target, m_conv_w_in, m_conv_b_in, m_conv_w_dw, m_conv_b_dw, m_conv_ln_g, m_conv_ln_b, m_conv_w_out, m_conv_b_out, m_gmlp_w_in, m_gmlp_b_in, m_gmlp_ln_g, m_gmlp_ln_b, m_gmlp_w_s, m_gmlp_b_s, m_gmlp_w_out, m_gmlp_b_out, m_ffn_w_up, m_ffn_b_up, m_ffn_w_dw, m_ffn_b_dw, m_ffn_w_down, m_ffn_b_down, m_norm1_g, m_norm1_b, m_norm2_g, m_norm2_b, v_conv_w_in, v_conv_b_in, v_conv_w_dw, v_conv_b_dw, v_conv_ln_g, v_conv_ln_b, v_conv_w_out, v_conv_b_out, v_gmlp_w_in, v_gmlp_b_in, v_gmlp_ln_g, v_gmlp_ln_b, v_gmlp_w_s, v_gmlp_b_s, v_gmlp_w_out, v_gmlp_b_out, v_ffn_w_up, v_ffn_b_up, v_ffn_w_dw, v_ffn_b_dw, v_ffn_w_down, v_ffn_b_down, v_norm1_g, v_norm1_b, v_norm2_g, v_norm2_b):
    return _step(dict(locals()))
```

```python
import math

import jax
import jax.numpy as jnp
from jax import lax
from jax.experimental import pallas as pl
from jax.experimental.pallas import tpu as pltpu

F32 = jnp.float32
BF16 = jnp.bfloat16
MESH = pl.DeviceIdType.MESH

N_DEV = 8
DEPTH = 4
ALPHA = (2.0 * DEPTH) ** 0.25
LN_EPS = 1e-5
CONV_K = 31
FFN_K = 3
CHUNK = 128
GROUPS = 8
ADAM_LR = 0.001
ADAM_B1 = 0.9
ADAM_B2 = 0.999
ADAM_EPS = 1e-08
ADAM_WD = 0.01
ADAM_STEP = 10
INV_SQRT2 = 1.0 / math.sqrt(2.0)
INV_SQRT2PI = 1.0 / math.sqrt(2.0 * math.pi)

LANES = 128
SUBLANES = 8
VMEM_LIMIT = 48 * 1024 * 1024
TM_MM = 1024
TM_EW = 256
HALO31 = 32
HALO3 = 8


def _call(body, name, grid, in_specs, out_specs, out_shape, scratch=(), aliases=None, deps=()):
    deps = list(deps)
    in_specs = list(in_specs)
    n_in = len(in_specs)
    if deps:
        inner = body

        def body(*refs):
            return inner(*refs[:n_in], *refs[n_in + len(deps):])

        in_specs = in_specs + [pl.BlockSpec(memory_space=pl.ANY)] * len(deps)
    fn = pl.pallas_call(
        body, name=name, grid=grid, in_specs=in_specs, out_specs=out_specs, out_shape=out_shape,
        scratch_shapes=list(scratch), input_output_aliases=aliases or {},
        compiler_params=pltpu.CompilerParams(vmem_limit_bytes=VMEM_LIMIT))
    return lambda *args: fn(*args, *deps)


def _sds(shape, dtype):
    return jax.ShapeDtypeStruct(tuple(shape), dtype)


def _sigmoid(x):
    return 1.0 / (1.0 + jnp.exp(-x))


def _acc_rows(ref, val, first):
    @pl.when(first)
    def _():
        ref[...] = val

    @pl.when(jnp.logical_not(first))
    def _():
        ref[...] += val


def _colsum(v):
    return jnp.sum(v, axis=0, keepdims=True)


_DIMS = {"nn": ((1,), (0,)), "nt": ((1,), (1,)), "tn": ((0,), (0,))}


def _matmul(name, a, b, mode, *, grid, a_spec, b_spec, o_spec, o_shape, o_dtype, k_axis=None, nk=1,
            acc_shape=None, bias=None, bias_spec=None, res=None, res_spec=None, res_scale=1.0,
            prev=None):
    dims = (_DIMS[mode], ((), ()))
    has_bias, has_res, has_prev = bias is not None, res is not None, prev is not None

    def body(*refs):
        a_ref, b_ref = refs[0], refs[1]
        pos = 2
        bias_ref = res_ref = None
        if has_bias:
            bias_ref = refs[pos]
            pos += 1
        if has_res:
            res_ref = refs[pos]
            pos += 1
        if has_prev:
            pos += 1
        o_ref = refs[pos]
        acc_ref = refs[pos + 1] if nk > 1 else None
        p = lax.dot_general(a_ref[...].astype(BF16), b_ref[...].astype(BF16), dims, preferred_element_type=F32)

        def finish(acc):
            if has_bias:
                acc = acc + bias_ref[...]
            if has_res:
                acc = acc + res_scale * res_ref[...]
            o_ref[...] = acc.astype(o_dtype)

        if nk == 1:
            finish(p)
        else:
            k = pl.program_id(k_axis)

            @pl.when(k == 0)
            def _():
                acc_ref[...] = p

            @pl.when(k > 0)
            def _():
                acc_ref[...] += p

            @pl.when(k == nk - 1)
            def _():
                finish(acc_ref[...])

    ins, specs = [a, b], [a_spec, b_spec]
    if has_bias:
        ins.append(bias)
        specs.append(bias_spec)
    if has_res:
        ins.append(res)
        specs.append(res_spec)
    aliases = None
    if has_prev:
        aliases = {len(ins): 0}
        ins.append(prev)
        specs.append(pl.BlockSpec(memory_space=pl.ANY))
    scratch = [pltpu.VMEM(acc_shape, F32)] if nk > 1 else []
    return _call(body, name, grid, specs, o_spec, _sds(o_shape, o_dtype), scratch, aliases)(*ins)


TM_LN = 512


def _matmul_ln(name, a, b, x_res, bias3, g3, b3, l_bias, l_norm, *, nk, a_block, b_block):
    t, d = x_res.shape
    tm = min(TM_LN, t)

    def body(a_ref, b_ref, x_ref, bias_ref, g_ref, be_ref, o_ref, xh_ref, rs_ref, *acc):
        p = jnp.dot(a_ref[...].astype(BF16), b_ref[...], preferred_element_type=F32)

        def finish(y):
            xhat, rstd = _ln_stats(ALPHA * x_ref[...] + y + bias_ref[...])
            o_ref[...] = xhat * g_ref[...] + be_ref[...]
            xh_ref[...] = xhat
            rs_ref[...] = rstd

        if nk == 1:
            finish(p)
        else:
            k = pl.program_id(1)

            @pl.when(k == 0)
            def _():
                acc[0][...] = p

            @pl.when(k > 0)
            def _():
                acc[0][...] += p

            @pl.when(k == nk - 1)
            def _():
                finish(acc[0][...])

    if nk == 1:
        grid = (t // tm,)
        a_spec = pl.BlockSpec(a_block, lambda i: (i, 0))
        b_spec = pl.BlockSpec(b_block, lambda i: (0, 0))
    else:
        grid = (t // tm, nk)
        a_spec = pl.BlockSpec(a_block, lambda i, k: (k, i, 0))
        b_spec = pl.BlockSpec(b_block, lambda i, k: (k, 0, 0))
    row = pl.BlockSpec((tm, d), lambda i, *_: (i, 0))
    stat = pl.BlockSpec((tm, 1), lambda i, *_: (i, 0))

    def vec(l):
        return pl.BlockSpec((None, 1, d), lambda *_: (l, 0, 0))

    return _call(body, name, grid, [a_spec, b_spec, row, vec(l_bias), vec(l_norm), vec(l_norm)], [row, row, stat],
                 [_sds((t, d), F32), _sds((t, d), F32), _sds((t, 1), F32)],
                 [pltpu.VMEM((tm, d), F32)] if nk > 1 else [])(a, b, x_res, bias3, g3, b3)


def _mesh_pos():
    return lax.axis_index("x"), lax.axis_index("y"), lax.axis_index("c")


def _any_specs(n):
    return [pl.BlockSpec(memory_space=pl.ANY)] * n


def _all_gather(name, srcs):
    n = len(srcs)

    def body(*refs):
        src, out = refs[:n], refs[n:2 * n]
        send_sems, recv_sems, local_sems = refs[2 * n:]
        x, y, c = _mesh_pos()
        me, sibling = (x, y, c), (x, y, 1 - c)
        chips = [(1 - x, y), (x, 1 - y), (1 - x, 1 - y)]

        def slot(k, p):
            return out[k].at[:, 4 * p[0] + 2 * p[1] + p[2]]

        def copy(k, idx, block, to, s=None):
            return pltpu.make_async_remote_copy(
                src_ref=slot(k, block) if s is None else s, dst_ref=slot(k, block),
                send_sem=send_sems.at[k * 7 + idx], recv_sem=recv_sems.at[k * 7 + idx],
                device_id=to, device_id_type=MESH)

        local = [pltpu.make_async_copy(src[k], slot(k, me), local_sems.at[k]) for k in range(n)]
        for cp in local:
            cp.start()
        first = []
        for k in range(n):
            first.append(copy(k, 0, me, sibling, src[k]))
            for j, chip in enumerate(chips):
                first.append(copy(k, 1 + j, me, (*chip, c), src[k]))
        for cp in first:
            cp.start()
        passed = []
        for j, chip in enumerate(chips):
            for k in range(n):
                copy(k, 1 + j, (*chip, c), me).wait_recv()
                cp = copy(k, 4 + j, (*chip, c), sibling)
                cp.start()
                passed.append(cp)
        for k in range(n):
            copy(k, 0, sibling, me).wait_recv()
            for j, chip in enumerate(chips):
                copy(k, 4 + j, (*chip, 1 - c), me).wait_recv()
        for cp in first + passed:
            cp.wait_send()
        for cp in local:
            cp.wait()

    out_shape = [_sds((s.shape[0], N_DEV) + s.shape[1:], s.dtype) for s in srcs]
    return _call(body, name, (), [pl.BlockSpec(memory_space=pltpu.VMEM)] * n, _any_specs(n), out_shape,
                 [pltpu.SemaphoreType.DMA((7 * n,)), pltpu.SemaphoreType.DMA((7 * n,)),
                  pltpu.SemaphoreType.DMA((n,))])(*srcs)


HBM_SPEC = pl.BlockSpec(memory_space=pltpu.HBM)
SEM_SPEC = pl.BlockSpec(memory_space=pltpu.SEMAPHORE)
N_PEER = N_DEV - 1


def _split_call(body, name, in_specs, out_specs, out_shape, aliases):
    return pl.pallas_call(
        body, name=name, in_specs=in_specs, out_specs=out_specs, out_shape=out_shape, input_output_aliases=aliases,
        compiler_params=pltpu.CompilerParams(has_side_effects=pltpu.SideEffectType.DATAFLOW_SIDE_EFFECTING))


def _peers(x, y, c):
    return [(1 - x if q & 4 else x, 1 - y if q & 2 else y, 1 - c if q & 1 else c) for q in range(1, N_DEV)]


def _in_hbm(a):
    return pltpu.with_memory_space_constraint(a, pltpu.HBM)


def _place_own(name, srcs, deps=()):
    n = len(srcs)

    def body(*refs):
        src, out, sems = refs[:n], refs[n:2 * n], refs[2 * n]
        x, y, c = _mesh_pos()
        dev = 4 * x + 2 * y + c
        copies = [pltpu.make_async_copy(src[k], out[k].at[dev], sems.at[k]) for k in range(n)]
        for cp in copies:
            cp.start()
        for cp in copies:
            cp.wait()

    return _call(body, name, (), [pl.BlockSpec(memory_space=pltpu.VMEM)] * n, _any_specs(n),
                 [_sds((N_DEV,) + s.shape, s.dtype) for s in srcs], [pltpu.SemaphoreType.DMA((n,))],
                 deps=deps)(*srcs)


def _gather_start(name, lands):
    n = len(lands)

    def body(*refs):
        land, send_sems, recv_sems = refs[:n], refs[n], refs[n + 1]
        x, y, c = _mesh_pos()
        dev = 4 * x + 2 * y + c
        for k in range(n):
            for peer in _peers(x, y, c):
                pltpu.make_async_remote_copy(
                    src_ref=land[k].at[dev], dst_ref=land[k].at[dev], send_sem=send_sems.at[k],
                    recv_sem=recv_sems.at[k], device_id=peer, device_id_type=MESH).start()

    outs = _split_call(
        body, name, [HBM_SPEC] * n, [SEM_SPEC, SEM_SPEC] + [HBM_SPEC] * n,
        [pltpu.SemaphoreType.DMA((n,)), pltpu.SemaphoreType.DMA((n,))] + [pltpu.HBM(a.shape, a.dtype) for a in lands],
        {k: 2 + k for k in range(n)})(*[_in_hbm(a) for a in lands])
    return outs[0], outs[1], list(outs[2:])


def _wait_seven(src_ref, dst_ref, send_sem, recv_sem):
    cp = pltpu.make_async_remote_copy(
        src_ref=src_ref.at[pl.ds(0, N_PEER)], dst_ref=dst_ref.at[pl.ds(0, N_PEER)], send_sem=send_sem,
        recv_sem=recv_sem, device_id=_mesh_pos(), device_id_type=MESH)
    cp.wait_send()
    cp.wait_recv()


def _gather_wait(name, land, send_sems, recv_sems, k, after):
    def body(land_ref, send_ref, recv_ref, after_ref, out_ref):
        _wait_seven(land_ref, land_ref, send_ref.at[k], recv_ref.at[k])

    return _split_call(body, name, [HBM_SPEC, SEM_SPEC, SEM_SPEC, pl.BlockSpec(memory_space=pl.ANY)], HBM_SPEC,
                       pltpu.HBM(land.shape, land.dtype), {0: 0})(land, send_sems, recv_sems, after)


def _scatter_start(name, parts):
    def body(parts_ref, land_ref, send_sem, recv_sem, parts_out, land_out, token):
        x, y, c = _mesh_pos()
        dev = 4 * x + 2 * y + c
        for peer in _peers(x, y, c):
            pltpu.make_async_remote_copy(
                src_ref=parts_ref.at[4 * peer[0] + 2 * peer[1] + peer[2]], dst_ref=land_ref.at[dev],
                send_sem=send_sem, recv_sem=recv_sem, device_id=peer, device_id_type=MESH).start()
        token[...] = jnp.zeros_like(token)

    buf = pltpu.HBM(parts.shape, parts.dtype)
    return _split_call(
        body, name, [HBM_SPEC, HBM_SPEC],
        [SEM_SPEC, SEM_SPEC, HBM_SPEC, HBM_SPEC, pl.BlockSpec(memory_space=pltpu.VMEM)],
        [pltpu.SemaphoreType.DMA(()), pltpu.SemaphoreType.DMA(()), buf, buf, _sds((SUBLANES, LANES), F32)],
        {0: 2, 1: 3})(_in_hbm(parts), _in_hbm(lax.empty(parts.shape, parts.dtype)))


def _scatter_wait(name, started, after):
    n = len(started)

    def body(*refs):
        for k in range(n):
            send_sem, recv_sem, parts_ref, land_ref = refs[4 * k:4 * k + 4]
            _wait_seven(parts_ref, land_ref, send_sem, recv_sem)

    flat = [a for s in started for a in s]
    outs = _split_call(
        body, name, [SEM_SPEC, SEM_SPEC, HBM_SPEC, HBM_SPEC] * n + [pl.BlockSpec(memory_space=pl.ANY)],
        [HBM_SPEC, HBM_SPEC] * n, [pltpu.HBM(a.shape, a.dtype) for s in started for a in s[2:]],
        {4 * k + 2 + t: 2 * k + t for k in range(n) for t in range(2)})(*flat, after)
    return list(outs[0::2]), list(outs[1::2])


def _row_chunk(width):
    lanes = -(-width // LANES)
    return max(SUBLANES, (8 // lanes) * SUBLANES) if lanes <= 8 else SUBLANES


def _conv_taps(ext_ref, w_ref, k_taps, base, rows, forward, init):
    acc = init
    for k in range(k_taps):
        d = k_taps - 1 - k
        off = base - d if forward else base + d
        acc = acc + w_ref[k:k + 1, :] * ext_ref[off:off + rows, :]
    return acc


def _ln_stats(v):
    mu = jnp.mean(v, axis=-1, keepdims=True)
    vc = v - mu
    var = jnp.mean(vc * vc, axis=-1, keepdims=True)
    rstd = lax.rsqrt(var + LN_EPS)
    return vc * rstd, rstd


def _ln_backward(dxhat, xhat, rstd):
    m1 = jnp.mean(dxhat, axis=-1, keepdims=True)
    m2 = jnp.mean(dxhat * xhat, axis=-1, keepdims=True)
    return rstd * (dxhat - m1 - xhat * m2)


def _row_spec(tm, width):
    return pl.BlockSpec((tm, width), lambda i: (i, 0))


def _param_spec(l, width):
    return pl.BlockSpec((None, 1, width), lambda *_: (l, 0, 0))


def _ln_res_bwd(name, dout, xhat, rstd, g3, l, deps=()):
    t, d = dout.shape
    tm = min(TM_EW, t)

    def body(do_ref, xh_ref, rs_ref, g_ref, dr_ref, dg_ref, db_ref, dc_ref):
        first = pl.program_id(0) == 0
        do, xhat = do_ref[...], xh_ref[...]
        dr = _ln_backward(do * g_ref[...], xhat, rs_ref[...])
        dr_ref[...] = dr
        _acc_rows(dg_ref, _colsum(do * xhat), first)
        _acc_rows(db_ref, _colsum(do), first)
        _acc_rows(dc_ref, _colsum(dr), first)

    vec = pl.BlockSpec((1, d), lambda i: (0, 0))
    return _call(body, name, (t // tm,),
                 [_row_spec(tm, d), _row_spec(tm, d), _row_spec(tm, 1), _param_spec(l, d)],
                 [_row_spec(tm, d), vec, vec, vec],
                 [_sds((t, d), F32)] + [_sds((1, d), F32)] * 3, deps=deps)(dout, xhat, rstd, g3)


def _glu(name, h):
    t, c2 = h.shape
    c = c2 // 2
    tm = min(TM_EW, t)

    def body(a_ref, g_ref, o_ref):
        o_ref[...] = a_ref[...] * _sigmoid(g_ref[...])

    return _call(body, name, (t // tm,),
                 [pl.BlockSpec((tm, c), lambda i: (i, 0)), pl.BlockSpec((tm, c), lambda i: (i, 1))],
                 _row_spec(tm, c), _sds((t, c), F32))(h, h)


def _glu_bwd(name, du, h):
    t, c2 = h.shape
    c = c2 // 2
    tm = min(TM_EW, t)

    def body(du_ref, a_ref, g_ref, dh_ref, db_ref):
        first = pl.program_id(0) == 0
        du_v, a = du_ref[...], a_ref[...]
        sg = _sigmoid(g_ref[...])
        da = du_v * sg
        dg = du_v * a * sg * (1.0 - sg)
        dh_ref[:, :c] = da.astype(BF16)
        dh_ref[:, c:] = dg.astype(BF16)
        _acc_rows(db_ref.at[:, :c], _colsum(da), first)
        _acc_rows(db_ref.at[:, c:], _colsum(dg), first)

    return _call(body, name, (t // tm,),
                 [_row_spec(tm, c), pl.BlockSpec((tm, c), lambda i: (i, 0)), pl.BlockSpec((tm, c), lambda i: (i, 1))],
                 [_row_spec(tm, c2), pl.BlockSpec((1, c2), lambda i: (0, 0))],
                 [_sds((t, c2), BF16), _sds((1, c2), F32)])(du, h, h)


def _halo_specs(tm, cb, halo, seq_tiles, n_tiles):
    per = tm // halo
    tile = pl.BlockSpec((tm, cb), lambda n, i: (i, n))
    prev = pl.BlockSpec((halo, cb), lambda n, i: (jnp.maximum(i * per - 1, 0), n))
    nxt = pl.BlockSpec((halo, cb), lambda n, i: (jnp.minimum((i + 1) * per, n_tiles * per - 1), n))
    return tile, prev, nxt


def _dwconv31(name, u, w3, b3, l, seq):
    t, c = u.shape
    tm, cb = min(TM_EW, seq), 256
    seq_tiles, n_tiles = seq // tm, t // tm
    rc = _row_chunk(cb)
    tile, prev, _ = _halo_specs(tm, cb, HALO31, seq_tiles, n_tiles)

    def body(u_ref, halo_ref, w_ref, b_ref, o_ref, ext_ref):
        i = pl.program_id(1)
        keep = (i % seq_tiles != 0).astype(F32)
        ext_ref[0:HALO31, :] = halo_ref[...] * keep
        ext_ref[HALO31:HALO31 + tm, :] = u_ref[...]
        for r0 in range(0, tm, rc):
            init = jnp.broadcast_to(b_ref[...], (rc, cb))
            o_ref[r0:r0 + rc, :] = _conv_taps(ext_ref, w_ref, CONV_K, HALO31 + r0, rc, True, init)

    return _call(body, name, (c // cb, n_tiles),
                 [tile, prev, pl.BlockSpec((None, CONV_K, cb), lambda n, i: (l, 0, n)),
                  pl.BlockSpec((None, 1, cb), lambda n, i: (l, 0, n))],
                 tile, _sds((t, c), F32), [pltpu.VMEM((HALO31 + tm, cb), F32)])(u, u, w3, b3)


def _dwconv31_bwd(name, dc, u, w3, l, seq):
    t, c = dc.shape
    tm, cb = min(TM_EW, seq), 256
    seq_tiles, n_tiles = seq // tm, t // tm
    rc = _row_chunk(cb)
    tile, prev, nxt = _halo_specs(tm, cb, HALO31, seq_tiles, n_tiles)

    def body(dc_ref, dcn_ref, u_ref, up_ref, w_ref, du_ref, dw_ref, dext_ref, uext_ref):
        i = pl.program_id(1)
        keep_prev = (i % seq_tiles != 0).astype(F32)
        keep_next = (i % seq_tiles != seq_tiles - 1).astype(F32)
        dext_ref[0:tm, :] = dc_ref[...]
        dext_ref[tm:tm + HALO31, :] = dcn_ref[...] * keep_next
        uext_ref[0:HALO31, :] = up_ref[...] * keep_prev
        uext_ref[HALO31:HALO31 + tm, :] = u_ref[...]
        for r0 in range(0, tm, rc):
            du_ref[r0:r0 + rc, :] = _conv_taps(dext_ref, w_ref, CONV_K, r0, rc, False, jnp.zeros((rc, cb), F32))

        @pl.when(i == 0)
        def _():
            dw_ref[...] = jnp.zeros_like(dw_ref)

        for k in range(CONV_K):
            d = CONV_K - 1 - k
            part = jnp.zeros((rc, cb), F32)
            for r0 in range(0, tm, rc):
                part = part + dc_ref[r0:r0 + rc, :] * uext_ref[HALO31 - d + r0:HALO31 - d + r0 + rc, :]
            dw_ref[k:k + 1, :] += _colsum(part)

    return _call(body, name, (c // cb, n_tiles),
                 [tile, nxt, tile, prev, pl.BlockSpec((None, CONV_K, cb), lambda n, i: (l, 0, n))],
                 [tile, pl.BlockSpec((HALO31, cb), lambda n, i: (0, n))],
                 [_sds((t, c), F32), _sds((HALO31, c), F32)],
                 [pltpu.VMEM((tm + HALO31, cb), F32), pltpu.VMEM((HALO31 + tm, cb), F32)])(dc, dc, u, u, w3)


def _ln_silu(name, cx, g3, b3, l):
    t, d = cx.shape
    tm = min(TM_EW, t)

    def body(c_ref, g_ref, b_ref, o_ref):
        xhat, _ = _ln_stats(c_ref[...])
        nv = xhat * g_ref[...] + b_ref[...]
        o_ref[...] = (nv * _sigmoid(nv)).astype(BF16)

    return _call(body, name, (t // tm,), [_row_spec(tm, d), _param_spec(l, d), _param_spec(l, d)],
                 _row_spec(tm, d), _sds((t, d), BF16))(cx, g3, b3)


def _ln_silu_bwd(name, ds, cx, g3, b3, l, deps=()):
    t, d = cx.shape
    tm = min(TM_EW, t)

    def body(ds_ref, c_ref, g_ref, b_ref, dc_ref, dg_ref, db_ref, dsum_ref):
        first = pl.program_id(0) == 0
        xhat, rstd = _ln_stats(c_ref[...])
        g = g_ref[...]
        nv = xhat * g + b_ref[...]
        sg = _sigmoid(nv)
        dn = ds_ref[...] * (sg * (1.0 + nv * (1.0 - sg)))
        dc = _ln_backward(dn * g, xhat, rstd)
        dc_ref[...] = dc
        _acc_rows(dg_ref, _colsum(dn * xhat), first)
        _acc_rows(db_ref, _colsum(dn), first)
        _acc_rows(dsum_ref, _colsum(dc), first)

    vec = pl.BlockSpec((1, d), lambda i: (0, 0))
    return _call(body, name, (t // tm,),
                 [_row_spec(tm, d), _row_spec(tm, d), _param_spec(l, d), _param_spec(l, d)],
                 [_row_spec(tm, d), vec, vec, vec],
                 [_sds((t, d), F32)] + [_sds((1, d), F32)] * 3, deps=deps)(ds, cx, g3, b3)


def _blk_specs(tm, fb, halo, n_tiles, shift):
    per = tm // halo
    tile = pl.BlockSpec((None, tm, fb), lambda n, i: (n + shift, i, 0))
    prev = pl.BlockSpec((None, halo, fb), lambda n, i: (n + shift, jnp.maximum(i * per - 1, 0), 0))
    nxt = pl.BlockSpec((None, halo, fb), lambda n, i: (n + shift, jnp.minimum((i + 1) * per, n_tiles * per - 1), 0))
    return tile, prev, nxt


def _ffn_conv(ext_ref, halo_ref, x_ref, w_ref, b_ref, keep, tm, fb, rc, out_ref):
    ext_ref[0:HALO3, :] = halo_ref[...] * keep
    ext_ref[HALO3:HALO3 + tm, :] = x_ref[...]
    for r0 in range(0, tm, rc):
        init = jnp.broadcast_to(b_ref[...], (rc, fb))
        out_ref[r0:r0 + rc, :] = _conv_taps(ext_ref, w_ref, FFN_K, HALO3 + r0, rc, True, init)


def _ffn_act(name, h, wdw, bdw, l, seq):
    nb, t, fb = h.shape
    half = nb // 2
    tm = min(TM_EW, seq)
    seq_tiles, n_tiles = seq // tm, t // tm
    rc = _row_chunk(fb)
    g_tile, g_prev, _ = _blk_specs(tm, fb, HALO3, n_tiles, 0)
    v_tile, v_prev, _ = _blk_specs(tm, fb, HALO3, n_tiles, half)

    def body(g_ref, gp_ref, v_ref, vp_ref, wg_ref, wv_ref, bg_ref, bv_ref, o_ref, ext_ref, cg_ref, cv_ref):
        keep = (pl.program_id(1) % seq_tiles != 0).astype(F32)
        _ffn_conv(ext_ref, gp_ref, g_ref, wg_ref, bg_ref, keep, tm, fb, rc, cg_ref)
        _ffn_conv(ext_ref, vp_ref, v_ref, wv_ref, bv_ref, keep, tm, fb, rc, cv_ref)
        cg = cg_ref[...]
        o_ref[...] = (cg * _sigmoid(cg) * cv_ref[...]).astype(BF16)

    def wspec(shift, rows):
        return pl.BlockSpec((None, None, rows, fb), lambda n, i: (l, n + shift, 0, 0))

    return _call(body, name, (half, n_tiles),
                 [g_tile, g_prev, v_tile, v_prev, wspec(0, FFN_K), wspec(half, FFN_K), wspec(0, 1), wspec(half, 1)],
                 pl.BlockSpec((None, tm, fb), lambda n, i: (n, i, 0)), _sds((half, t, fb), BF16),
                 [pltpu.VMEM((HALO3 + tm, fb), F32), pltpu.VMEM((tm, fb), F32), pltpu.VMEM((tm, fb), F32)]
                 )(h, h, h, h, wdw, wdw, bdw, bdw)


def _ffn_act_bwd(name, da, h, wdw, bdw, l, seq, deps=()):
    nb, t, fb = h.shape
    half = nb // 2
    tm = min(TM_EW, seq)
    seq_tiles, n_tiles = seq // tm, t // tm
    rc = _row_chunk(fb)
    g_tile, g_prev, _ = _blk_specs(tm, fb, HALO3, n_tiles, 0)
    v_tile, v_prev, _ = _blk_specs(tm, fb, HALO3, n_tiles, half)

    def body(da_ref, g_ref, gp_ref, v_ref, vp_ref, wg_ref, wv_ref, bg_ref, bv_ref,
             dg_ref, dv_ref, dbg_ref, dbv_ref, dwg_ref, dwv_ref, gext_ref, vext_ref, cg_ref, cv_ref):
        i = pl.program_id(1)
        first = i == 0
        keep = (i % seq_tiles != 0).astype(F32)
        _ffn_conv(gext_ref, gp_ref, g_ref, wg_ref, bg_ref, keep, tm, fb, rc, cg_ref)
        _ffn_conv(vext_ref, vp_ref, v_ref, wv_ref, bv_ref, keep, tm, fb, rc, cv_ref)
        cg, cv, da_v = cg_ref[...], cv_ref[...], da_ref[...]
        sg = _sigmoid(cg)
        dcv = da_v * cg * sg
        dcg = da_v * cv * sg * (1.0 + cg * (1.0 - sg))
        dg_ref[...] = dcg
        dv_ref[...] = dcv
        _acc_rows(dbg_ref, _colsum(dcg), first)
        _acc_rows(dbv_ref, _colsum(dcv), first)

        @pl.when(first)
        def _():
            dwg_ref[...] = jnp.zeros_like(dwg_ref)
            dwv_ref[...] = jnp.zeros_like(dwv_ref)

        for k in range(FFN_K):
            d = FFN_K - 1 - k
            dwg_ref[k:k + 1, :] += _colsum(dcg * gext_ref[HALO3 - d:HALO3 - d + tm, :])
            dwv_ref[k:k + 1, :] += _colsum(dcv * vext_ref[HALO3 - d:HALO3 - d + tm, :])

    def wspec(shift, rows):
        return pl.BlockSpec((None, None, rows, fb), lambda n, i: (l, n + shift, 0, 0))

    ext = pltpu.VMEM((HALO3 + tm, fb), F32)
    tmp = pltpu.VMEM((tm, fb), F32)
    outs = _call(body, name, (half, n_tiles),
                 [pl.BlockSpec((None, tm, fb), lambda n, i: (n, i, 0)), g_tile, g_prev, v_tile, v_prev,
                  wspec(0, FFN_K), wspec(half, FFN_K), wspec(0, 1), wspec(half, 1)],
                 [pl.BlockSpec((None, tm, fb), lambda n, i: (n, i, 0)),
                  pl.BlockSpec((None, tm, fb), lambda n, i: (n, i, 0)),
                  pl.BlockSpec((None, 1, fb), lambda n, i: (n, 0, 0)), pl.BlockSpec((None, 1, fb), lambda n, i: (n, 0, 0)),
                  pl.BlockSpec((None, SUBLANES, fb), lambda n, i: (n, 0, 0)),
                  pl.BlockSpec((None, SUBLANES, fb), lambda n, i: (n, 0, 0))],
                 [_sds((half, t, fb), F32), _sds((half, t, fb), F32), _sds((half, 1, fb), F32),
                  _sds((half, 1, fb), F32), _sds((half, SUBLANES, fb), F32), _sds((half, SUBLANES, fb), F32)],
                 [ext, ext, tmp, tmp], deps=deps)(da, h, h, h, h, wdw, wdw, bdw, bdw)
    return outs


def _ffn_conv_t(name, dhc, wdw, l, shift, seq):
    half, t, fb = dhc.shape
    tm = min(TM_EW, seq)
    seq_tiles, n_tiles = seq // tm, t // tm
    rc = _row_chunk(fb)
    tile, _, nxt = _blk_specs(tm, fb, HALO3, n_tiles, 0)

    def body(d_ref, dn_ref, w_ref, dh_ref, db_ref, ext_ref, out_ref):
        i = pl.program_id(1)
        keep = (i % seq_tiles != seq_tiles - 1).astype(F32)
        ext_ref[0:tm, :] = d_ref[...]
        ext_ref[tm:tm + HALO3, :] = dn_ref[...] * keep
        for r0 in range(0, tm, rc):
            out_ref[r0:r0 + rc, :] = _conv_taps(ext_ref, w_ref, FFN_K, r0, rc, False, jnp.zeros((rc, fb), F32))
        dh = out_ref[...]
        dh_ref[...] = dh.astype(BF16)
        _acc_rows(db_ref, _colsum(dh), i == 0)

    return _call(body, name, (half, n_tiles),
                 [tile, nxt, pl.BlockSpec((None, None, FFN_K, fb), lambda n, i: (l, n + shift, 0, 0))],
                 [tile, pl.BlockSpec((None, 1, fb), lambda n, i: (n, 0, 0))],
                 [_sds((half, t, fb), BF16), _sds((half, 1, fb), F32)],
                 [pltpu.VMEM((tm + HALO3, fb), F32), pltpu.VMEM((tm, fb), F32)])(dhc, dhc, wdw)


def _gelu_parts(h):
    cdf = 0.5 * (1.0 + lax.erf(h * INV_SQRT2))
    return h * cdf, cdf


def _sgu(name, h, g3, b3, wc, bsb, l):
    t, c2 = h.shape
    c = c2 // 2
    tm = min(TM_EW, t)

    def body(h_ref, g_ref, b_ref, wc_ref, bs_ref, o_ref):
        z, _ = _gelu_parts(h_ref[...])
        u = z[:, :c]
        xhat, _ = _ln_stats(z[:, c:])
        vnb = (xhat * g_ref[...] + b_ref[...]).astype(BF16)
        for q in range(tm // CHUNK):
            for gi in range(GROUPS):
                rs, cs = slice(q * CHUNK, (q + 1) * CHUNK), slice(gi * CHUNK, (gi + 1) * CHUNK)
                sp = jnp.dot(wc_ref[gi], vnb[rs, cs], preferred_element_type=F32) + bs_ref[gi]
                o_ref[rs, cs] = (u[rs, cs] * sp).astype(BF16)

    grp = pl.BlockSpec((None, GROUPS, CHUNK, CHUNK), lambda i: (l, 0, 0, 0))
    return _call(body, name, (t // tm,), [_row_spec(tm, c2), _param_spec(l, c), _param_spec(l, c), grp, grp],
                 _row_spec(tm, c), _sds((t, c), BF16))(h, g3, b3, wc, bsb)


def _sgu_bwd(name, dq, h, g3, b3, wc, bsb, l, deps=()):
    t, c2 = h.shape
    c = c2 // 2
    tm = min(TM_EW, t)
    n_tiles = t // tm

    def body(dq_ref, h_ref, g_ref, b_ref, wc_ref, bs_ref,
             dh_ref, dbin_ref, dw_ref, dbs_ref, dg_ref, db_ref, du_ref, dvn_ref, bsum_ref):
        i = pl.program_id(0)
        first = i == 0
        hv = h_ref[...]
        z, cdf = _gelu_parts(hv)
        u = z[:, :c]
        xhat, rstd = _ln_stats(z[:, c:])
        g = g_ref[...]
        vnb = (xhat * g + b_ref[...]).astype(BF16)

        @pl.when(first)
        def _():
            dw_ref[...] = jnp.zeros_like(dw_ref)
            bsum_ref[...] = jnp.zeros_like(bsum_ref)

        for q in range(tm // CHUNK):
            for gi in range(GROUPS):
                rs, cs = slice(q * CHUNK, (q + 1) * CHUNK), slice(gi * CHUNK, (gi + 1) * CHUNK)
                vb = vnb[rs, cs]
                w = wc_ref[gi]
                sp = jnp.dot(w, vb, preferred_element_type=F32) + bs_ref[gi]
                dqb = dq_ref[rs, cs]
                du_ref[rs, cs] = dqb * sp
                dsp = dqb * u[rs, cs]
                bsum_ref[gi] += dsp
                dspb = dsp.astype(BF16)
                dw_ref[gi] += lax.dot_general(dspb, vb, (_DIMS["nt"], ((), ())), preferred_element_type=F32)
                dvn_ref[rs, cs] = lax.dot_general(w, dspb, (_DIMS["tn"], ((), ())), preferred_element_type=F32)

        dvn = dvn_ref[...]
        dv = _ln_backward(dvn * g, xhat, rstd)
        pdf = jnp.exp(-0.5 * hv * hv) * INV_SQRT2PI
        dgelu = cdf + hv * pdf
        dhu = du_ref[...] * dgelu[:, :c]
        dhv = dv * dgelu[:, c:]
        dh_ref[:, :c] = dhu.astype(BF16)
        dh_ref[:, c:] = dhv.astype(BF16)
        _acc_rows(dbin_ref.at[:, :c], _colsum(dhu), first)
        _acc_rows(dbin_ref.at[:, c:], _colsum(dhv), first)
        _acc_rows(dg_ref, _colsum(dvn * xhat), first)
        _acc_rows(db_ref, _colsum(dvn), first)

        @pl.when(i == n_tiles - 1)
        def _():
            row = lax.broadcasted_iota(jnp.int32, (CHUNK, CHUNK), 0)
            col = lax.broadcasted_iota(jnp.int32, (CHUNK, CHUNK), 1)
            for gi in range(GROUPS):
                dw_ref[gi] = jnp.where(row >= col, dw_ref[gi], 0.0)
            dbs_ref[...] = jnp.sum(bsum_ref[...], axis=-1)

    grp = pl.BlockSpec((None, GROUPS, CHUNK, CHUNK), lambda i: (l, 0, 0, 0))
    vec = pl.BlockSpec((1, c), lambda i: (0, 0))
    return _call(body, name, (n_tiles,),
                 [_row_spec(tm, c), _row_spec(tm, c2), _param_spec(l, c), _param_spec(l, c), grp, grp],
                 [_row_spec(tm, c2), pl.BlockSpec((1, c2), lambda i: (0, 0)),
                  pl.BlockSpec((GROUPS, CHUNK, CHUNK), lambda i: (0, 0, 0)),
                  pl.BlockSpec((GROUPS, CHUNK), lambda i: (0, 0)), vec, vec],
                 [_sds((t, c2), BF16), _sds((1, c2), F32), _sds((GROUPS, CHUNK, CHUNK), F32),
                  _sds((GROUPS, CHUNK), F32), _sds((1, c), F32), _sds((1, c), F32)],
                 [pltpu.VMEM((tm, c), F32), pltpu.VMEM((tm, c), F32), pltpu.VMEM((GROUPS, CHUNK, CHUNK), F32)],
                 deps=deps)(dq, h, g3, b3, wc, bsb)


def _loss(name, y, target):
    t, d = y.shape
    tm = min(TM_EW, t)
    n_tiles = t // tm

    def body(y_ref, t_ref, l_ref, dy_ref, acc_ref):
        i = pl.program_id(0)
        diff = y_ref[...] - t_ref[...]
        dy_ref[...] = diff * (1.0 / d)
        _acc_rows(acc_ref, _colsum(diff * diff), i == 0)

        @pl.when(i == n_tiles - 1)
        def _():
            l_ref[...] = jnp.broadcast_to(jnp.sum(acc_ref[...], axis=-1, keepdims=True) * (0.5 / d), (1, LANES))

    return _call(body, name, (n_tiles,), [_row_spec(tm, d), _row_spec(tm, d)],
                 [pl.BlockSpec((1, LANES), lambda i: (0, 0)), _row_spec(tm, d)],
                 [_sds((1, LANES), F32), _sds((t, d), F32)], [pltpu.VMEM((1, d), F32)])(y, target)


def _adamw(g, w, m, v):
    m2 = ADAM_B1 * m + (1.0 - ADAM_B1) * g
    v2 = ADAM_B2 * v + (1.0 - ADAM_B2) * (g * g)
    m_hat = m2 / (1.0 - ADAM_B1 ** ADAM_STEP)
    v_hat = v2 / (1.0 - ADAM_B2 ** ADAM_STEP)
    delta = -ADAM_LR * (m_hat / (jnp.sqrt(v_hat) + ADAM_EPS) + ADAM_WD * w)
    return delta, m2, v2


ROW_TILE_CAP = 512


def _row_tile(rows, cap=ROW_TILE_CAP):
    if rows <= cap:
        return rows
    for tr in range(cap, 15, -16):
        if rows % tr == 0:
            return tr
    return rows


def _sum8_adamw(name, dev, lands, parts, w, m, v):
    nl = len(lands)
    _, r, c = lands[0].shape
    tr = _row_tile(r, cap=128)

    def body(dev_ref, *refs):
        land, own = refs[:nl], refs[nl:2 * nl]
        w_ref, m_ref, v_ref, g_ref, d_ref, m2_ref, v2_ref = refs[2 * nl:]
        layer, me = pl.program_id(0), dev_ref[0]
        for l in range(nl):
            @pl.when(layer == l)
            def _(l=l):
                g = None
                for s in range(N_DEV):
                    part = jnp.where(me == s, own[l][...], land[l][s]).astype(F32)
                    g = part if g is None else g + part
                delta, m2, v2 = _adamw(g, w_ref[...], m_ref[...], v_ref[...])
                g_ref[...] = g
                d_ref[...] = delta
                m2_ref[...] = m2
                v2_ref[...] = v2

    def rows_of(l, a, i):
        return jnp.where(a == l, i, 0)

    spec = pl.BlockSpec((None, tr, c), lambda a, i, dev_ref: (a, i, 0))
    in_specs = [pl.BlockSpec((N_DEV, tr, c), lambda a, i, dev_ref, l=l: (0, rows_of(l, a, i), 0)) for l in range(nl)]
    in_specs += [pl.BlockSpec((None, tr, c), lambda a, i, dev_ref, l=l: (dev_ref[0], rows_of(l, a, i), 0))
                 for l in range(nl)]
    grid_spec = pltpu.PrefetchScalarGridSpec(
        num_scalar_prefetch=1, grid=(nl, r // tr), in_specs=in_specs + [spec] * 3, out_specs=[spec] * 4)
    return pl.pallas_call(
        body, name=name, grid_spec=grid_spec, out_shape=[_sds(w.shape, F32)] * 4,
        compiler_params=pltpu.CompilerParams(vmem_limit_bytes=VMEM_LIMIT))(dev, *lands, *parts, w, m, v)


def _sum8(name, parts):
    _, r, c = parts.shape
    tr = _row_tile(r)

    def body(p_ref, o_ref):
        acc = p_ref[0]
        for s in range(1, N_DEV):
            acc = acc + p_ref[s]
        o_ref[...] = acc

    return _call(body, name, (r // tr,), [pl.BlockSpec((N_DEV, tr, c), lambda i: (0, i, 0))],
                 pl.BlockSpec((tr, c), lambda i: (i, 0)), _sds((r, c), F32))(parts)


def _adamw_flat(name, g, w, m, v):
    r, c = g.shape
    tr = _row_tile(r)

    def body(g_ref, w_ref, m_ref, v_ref, d_ref, m2_ref, v2_ref):
        delta, m2, v2 = _adamw(g_ref[...], w_ref[...], m_ref[...], v_ref[...])
        d_ref[...] = delta
        m2_ref[...] = m2
        v2_ref[...] = v2

    spec = pl.BlockSpec((tr, c), lambda i: (i, 0))
    return _call(body, name, (r // tr,), [spec] * 4, [spec] * 3, [_sds((r, c), F32)] * 3)(g, w, m, v)


def _pack(arrs, row_multiple=SUBLANES):
    pieces, rows = [], 0
    for a in arrs:
        piece = a.reshape(-1, LANES)
        piece = jnp.pad(piece, ((0, (-piece.shape[0]) % SUBLANES), (0, 0)))
        pieces.append(piece)
        rows += piece.shape[0]
    if rows % row_multiple:
        pieces.append(jnp.zeros(((-rows) % row_multiple, LANES), pieces[0].dtype))
    return jnp.concatenate(pieces, axis=0)


def _unpack(buf, shapes, lead=0):
    out, pos = [], 0
    for shp in shapes:
        rows = math.prod(shp) // LANES
        piece = lax.slice_in_dim(buf, pos, pos + rows, axis=lead)
        out.append(piece.reshape(buf.shape[:lead] + tuple(shp)))
        pos += rows + (-rows) % SUBLANES
    return out


REPLICATED = ["conv_b_in", "conv_b_dw", "conv_ln_g", "conv_ln_b", "conv_b_out", "gmlp_w_s", "gmlp_b_s",
              "ffn_b_up", "ffn_b_dw", "ffn_b_down", "norm1_g", "norm1_b", "norm2_g", "norm2_b"]
SMALL_SHARDED = ["conv_w_dw", "gmlp_b_in", "gmlp_ln_g", "gmlp_ln_b", "gmlp_b_out", "ffn_w_dw"]
BIG = ["conv_w_in", "conv_w_out", "gmlp_w_in", "gmlp_w_out", "ffn_w_up", "ffn_w_down"]
WEIGHTS = ["conv_w_in", "conv_b_in", "conv_w_dw", "conv_b_dw", "conv_ln_g", "conv_ln_b", "conv_w_out", "conv_b_out",
           "gmlp_w_in", "gmlp_b_in", "gmlp_ln_g", "gmlp_ln_b", "gmlp_w_s", "gmlp_b_s", "gmlp_w_out", "gmlp_b_out",
           "ffn_w_up", "ffn_b_up", "ffn_w_dw", "ffn_b_dw", "ffn_w_down", "ffn_b_down",
           "norm1_g", "norm1_b", "norm2_g", "norm2_b"]


def _from_shards(g, lead_shape):
    nd = len(lead_shape)
    perm = tuple(range(1, nd + 1)) + (0, nd + 1)
    return g.transpose(perm).reshape(tuple(lead_shape) + (-1,))


def _to_shards(full, width):
    lead = full.shape[:-1]
    nd = len(lead)
    parts = full.reshape(lead + (N_DEV, width))
    return parts.transpose((nd,) + tuple(range(nd)) + (nd + 1,))


def _step(p):
    x_in, target_in = p["x"], p["loss_target"]
    bsz, seq, d = x_in.shape
    t = bsz * seq
    x0 = x_in.reshape(t, d)
    target = target_in.reshape(t, d)
    n_conv, n_gmlp = p["conv_w_in"].shape[0], p["gmlp_w_in"].shape[0]
    fb = p["ffn_w_up"].shape[-1]
    nblk = N_DEV
    half = nblk // 2
    cw = p["conv_w_in"].shape[-1]
    tm = min(TM_MM, t)
    tm_ln = min(TM_LN, t)
    nt = t // tm
    dev = 4 * lax.axis_index("x") + 2 * lax.axis_index("y") + lax.axis_index("c")

    small_shapes = [p[n].shape for n in SMALL_SHARDED]
    small_src = _pack([p[n] for n in SMALL_SHARDED])[None]
    small_all = _all_gather("gather_small_weights", [small_src])[0][0]
    sm = _unpack(small_all, small_shapes, lead=1)
    w_src = []
    for i in range(DEPTH):
        mix = "conv" if i % 2 == 0 else "gmlp"
        w_src += [p[mix + "_w_in"][i // 2].astype(BF16), p[mix + "_w_out"][i // 2].astype(BF16),
                  p["ffn_w_up"][i].T.astype(BF16), p["ffn_w_down"][i].astype(BF16)]
    send_sems, recv_sems, w_land = _gather_start(
        "weights_gather_start", _place_own("weights_place_own", w_src, deps=[small_all]))
    W_IN, W_OUT, W_UP, W_DOWN = range(4)

    def wait_weight(i, k, after):
        return _gather_wait(f"l{i}_weights_wait{k}", w_land[4 * i + k], send_sems, recv_sems, 4 * i + k, after)
    conv_w_dw = _from_shards(sm[0], sm[0].shape[1:-1])
    gmlp_b_in = _from_shards(sm[1], sm[1].shape[1:-1])
    gmlp_ln_g = _from_shards(sm[2], sm[2].shape[1:-1])
    gmlp_ln_b = _from_shards(sm[3], sm[3].shape[1:-1])
    gmlp_b_out = _from_shards(sm[4], sm[4].shape[1:-1])
    ffn_w_dw = sm[5].transpose(1, 0, 2, 3)

    def rows3(a):
        return a.reshape(a.shape[0], 1, a.shape[-1])

    conv_b_in4 = p["conv_b_in"].reshape(n_conv, N_DEV, 1, cw)
    gmlp_b_in4 = gmlp_b_in.reshape(n_gmlp, N_DEV, 1, cw)
    ffn_b_up4 = p["ffn_b_up"].reshape(DEPTH, nblk, 1, fb)
    ffn_b_dw4 = p["ffn_b_dw"].reshape(DEPTH, nblk, 1, fb)
    conv_b_dw3, conv_ln_g3, conv_ln_b3 = rows3(p["conv_b_dw"]), rows3(p["conv_ln_g"]), rows3(p["conv_ln_b"])
    conv_b_out3, gmlp_b_out3, ffn_b_down3 = rows3(p["conv_b_out"]), rows3(gmlp_b_out), rows3(p["ffn_b_down"])
    gmlp_ln_g3, gmlp_ln_b3 = rows3(gmlp_ln_g), rows3(gmlp_ln_b)
    n1g3, n1b3, n2g3, n2b3 = rows3(p["norm1_g"]), rows3(p["norm1_b"]), rows3(p["norm2_g"]), rows3(p["norm2_b"])
    tril = jnp.tril(jnp.ones((CHUNK, CHUNK), dtype=bool))
    w_causal = jnp.where(tril, p["gmlp_w_s"], 0.0).astype(BF16)
    bs_rows = jnp.broadcast_to(p["gmlp_b_s"][..., None], p["gmlp_b_s"].shape + (CHUNK,))

    def mm_in(name, xa, wg, l, bias4):
        return _matmul(name, xa, wg, "nn", grid=(nt, N_DEV),
                       a_spec=pl.BlockSpec((tm, d), lambda i, n: (i, 0)),
                       b_spec=pl.BlockSpec((None, d, cw), lambda i, n: (n, 0, 0)),
                       o_spec=pl.BlockSpec((tm, cw), lambda i, n: (i, n)), o_shape=(t, N_DEV * cw), o_dtype=F32,
                       bias=bias4, bias_spec=pl.BlockSpec((None, None, 1, cw), lambda i, n: (l, n, 0, 0)))

    def mm_out_dx(name, dy, w):
        return _matmul(name, dy, w, "nt", grid=(nt,),
                       a_spec=pl.BlockSpec((tm, d), lambda i: (i, 0)),
                       b_spec=pl.BlockSpec((d, d), lambda i: (0, 0)),
                       o_spec=pl.BlockSpec((tm, d), lambda i: (i, 0)), o_shape=(t, d), o_dtype=F32)

    def mm_out_dw(name, sa, dy):
        return _matmul(name, sa, dy, "tn", grid=(nt,), k_axis=0, nk=nt, acc_shape=(d, d),
                       a_spec=pl.BlockSpec((tm, d), lambda k: (k, 0)),
                       b_spec=pl.BlockSpec((tm, d), lambda k: (k, 0)),
                       o_spec=pl.BlockSpec((d, d), lambda k: (0, 0)), o_shape=(d, d), o_dtype=BF16)

    def mm_in_dx(name, dh, wg, res):
        return _matmul(name, dh, wg, "nt", grid=(nt, N_DEV), k_axis=1, nk=N_DEV, acc_shape=(tm, d),
                       a_spec=pl.BlockSpec((tm, cw), lambda i, n: (i, n)),
                       b_spec=pl.BlockSpec((None, d, cw), lambda i, n: (n, 0, 0)),
                       o_spec=pl.BlockSpec((tm, d), lambda i, n: (i, 0)), o_shape=(t, d), o_dtype=F32,
                       res=res, res_spec=pl.BlockSpec((tm, d), lambda i, n: (i, 0)), res_scale=ALPHA)

    def mm_in_dw(name, xa, dh):
        return _matmul(name, xa, dh, "tn", grid=(N_DEV, nt), k_axis=1, nk=nt, acc_shape=(d, cw),
                       a_spec=pl.BlockSpec((tm, d), lambda n, k: (k, 0)),
                       b_spec=pl.BlockSpec((tm, cw), lambda n, k: (k, n)),
                       o_spec=pl.BlockSpec((None, d, cw), lambda n, k: (n, 0, 0)),
                       o_shape=(N_DEV, d, cw), o_dtype=BF16)

    def mm_up(name, xa, w, l):
        return _matmul(name, xa, w, "nt", grid=(nt, nblk),
                       a_spec=pl.BlockSpec((tm, d), lambda i, n: (i, 0)),
                       b_spec=pl.BlockSpec((None, fb, d), lambda i, n: (n, 0, 0)),
                       o_spec=pl.BlockSpec((None, tm, fb), lambda i, n: (n, i, 0)), o_shape=(nblk, t, fb),
                       o_dtype=F32, bias=ffn_b_up4,
                       bias_spec=pl.BlockSpec((None, None, 1, fb), lambda i, n: (l, n, 0, 0)))

    def mm_down_da(name, dy, w):
        return _matmul(name, dy, w, "nt", grid=(nt, half),
                       a_spec=pl.BlockSpec((tm, d), lambda i, n: (i, 0)),
                       b_spec=pl.BlockSpec((None, fb, d), lambda i, n: (n, 0, 0)),
                       o_spec=pl.BlockSpec((None, tm, fb), lambda i, n: (n, i, 0)), o_shape=(half, t, fb),
                       o_dtype=F32)

    def mm_down_dw(name, a, dy):
        return _matmul(name, a, dy, "tn", grid=(half, nt), k_axis=1, nk=nt, acc_shape=(fb, d),
                       a_spec=pl.BlockSpec((None, tm, fb), lambda n, k: (n, k, 0)),
                       b_spec=pl.BlockSpec((tm, d), lambda n, k: (k, 0)),
                       o_spec=pl.BlockSpec((None, fb, d), lambda n, k: (n, 0, 0)),
                       o_shape=(half, fb, d), o_dtype=BF16)

    def mm_up_dx(name, dh_g, dh_v, w, res):
        part = _matmul(name + "_g", dh_g, w, "nn", grid=(nt, half), k_axis=1, nk=half, acc_shape=(tm, d),
                       a_spec=pl.BlockSpec((None, tm, fb), lambda i, n: (n, i, 0)),
                       b_spec=pl.BlockSpec((None, fb, d), lambda i, n: (n, 0, 0)),
                       o_spec=pl.BlockSpec((tm, d), lambda i, n: (i, 0)), o_shape=(t, d), o_dtype=F32,
                       res=res, res_spec=pl.BlockSpec((tm, d), lambda i, n: (i, 0)), res_scale=ALPHA)
        return _matmul(name + "_v", dh_v, w, "nn", grid=(nt, half), k_axis=1, nk=half, acc_shape=(tm, d),
                       a_spec=pl.BlockSpec((None, tm, fb), lambda i, n: (n, i, 0)),
                       b_spec=pl.BlockSpec((None, fb, d), lambda i, n: (n + half, 0, 0)),
                       o_spec=pl.BlockSpec((tm, d), lambda i, n: (i, 0)), o_shape=(t, d), o_dtype=F32,
                       res=part, res_spec=pl.BlockSpec((tm, d), lambda i, n: (i, 0)), res_scale=1.0)

    def mm_up_dw(name, xa, dh_half, shift, prev=None):
        return _matmul(name, dh_half, xa, "tn", grid=(half, nt), k_axis=1, nk=nt, acc_shape=(fb, d),
                       a_spec=pl.BlockSpec((None, tm, fb), lambda n, k: (n, k, 0)),
                       b_spec=pl.BlockSpec((tm, d), lambda n, k: (k, 0)),
                       o_spec=pl.BlockSpec((None, fb, d), lambda n, k: (n + shift, 0, 0)),
                       o_shape=(nblk, fb, d), o_dtype=BF16, prev=prev)

    saved = []
    xcur = x0
    for i in range(DEPTH):
        j = i // 2
        s = {"x": xcur}
        s["w_in"] = wait_weight(i, W_IN, xcur)
        if i % 2 == 0:
            s["h"] = mm_in(f"l{i}_conv_in", xcur, s["w_in"], j, conv_b_in4)
            s["u"] = _glu(f"l{i}_glu", s["h"])
            s["c"] = _dwconv31(f"l{i}_dwconv", s["u"], conv_w_dw, conv_b_dw3, j, seq)
            s["s"] = _ln_silu(f"l{i}_ln_silu", s["c"], conv_ln_g3, conv_ln_b3, j)
            b_out3 = conv_b_out3
        else:
            s["h"] = mm_in(f"l{i}_gmlp_in", xcur, s["w_in"], j, gmlp_b_in4)
            s["s"] = _sgu(f"l{i}_sgu", s["h"], gmlp_ln_g3, gmlp_ln_b3, w_causal, bs_rows, j)
            b_out3 = gmlp_b_out3
        s["w_out"] = wait_weight(i, W_OUT, s["s"]).reshape(d, d)
        s["x1"], s["xhat1"], s["rstd1"] = _matmul_ln(
            f"l{i}_mixer_out_norm1", s["s"], s["w_out"], xcur, b_out3, n1g3, n1b3, j, i,
            nk=1, a_block=(tm_ln, d), b_block=(d, d))
        s["w_up"] = wait_weight(i, W_UP, s["x1"])
        s["fh"] = mm_up(f"l{i}_ffn_up", s["x1"], s["w_up"], i)
        s["a"] = _ffn_act(f"l{i}_ffn_act", s["fh"], ffn_w_dw, ffn_b_dw4, i, seq)
        s["w_down"] = wait_weight(i, W_DOWN, s["a"]).reshape(half, fb, d)
        xcur, s["xhat2"], s["rstd2"] = _matmul_ln(
            f"l{i}_ffn_down_norm2", s["a"], s["w_down"], s["x1"], ffn_b_down3, n2g3, n2b3, i, i,
            nk=half, a_block=(None, tm_ln, fb), b_block=(None, fb, d))
        saved.append(s)

    loss_row, dx = _loss("loss", xcur, target)

    started = {n: [None] * p[n].shape[0] for n in BIG}
    tokens = []

    def send_grad(n, l, g):
        send_sem, recv_sem, parts, land, token = _scatter_start(f"grad_{n}{l}_scatter_start", g)
        started[n][l] = (send_sem, recv_sem, parts, land)
        tokens.append(token)

    def take_tokens():
        out = list(tokens)
        tokens.clear()
        return out

    gl = {n: [None] * p[n].shape[0] for n in REPLICATED + SMALL_SHARDED}
    for i in reversed(range(DEPTH)):
        j = i // 2
        s = saved[i]
        mix = "conv" if i % 2 == 0 else "gmlp"
        dr2, gl["norm2_g"][i], gl["norm2_b"][i], gl["ffn_b_down"][i] = _ln_res_bwd(
            f"l{i}_norm2_bwd", dx, s["xhat2"], s["rstd2"], n2g3, i, deps=take_tokens())
        da = mm_down_da(f"l{i}_ffn_down_da", dr2, s["w_down"])
        send_grad("ffn_w_down", i, mm_down_dw(f"l{i}_ffn_down_dw", s["a"], dr2).reshape(N_DEV, -1, d))
        dcg, dcv, dbg, dbv, dwg, dwv = _ffn_act_bwd(f"l{i}_ffn_act_bwd", da, s["fh"], ffn_w_dw, ffn_b_dw4, i, seq,
                                                    deps=take_tokens())
        gl["ffn_b_dw"][i] = jnp.concatenate([dbg, dbv], axis=0).reshape(1, nblk * fb)
        gl["ffn_w_dw"][i] = jnp.concatenate([dwg[:, :FFN_K], dwv[:, :FFN_K]], axis=0)
        dh_g, dbu_g = _ffn_conv_t(f"l{i}_ffn_conv_t_g", dcg, ffn_w_dw, i, 0, seq)
        dh_v, dbu_v = _ffn_conv_t(f"l{i}_ffn_conv_t_v", dcv, ffn_w_dw, i, half, seq)
        gl["ffn_b_up"][i] = jnp.concatenate([dbu_g, dbu_v], axis=0).reshape(1, nblk * fb)
        dx1 = mm_up_dx(f"l{i}_ffn_up_dx", dh_g, dh_v, s["w_up"], dr2)
        g_up = mm_up_dw(f"l{i}_ffn_up_dw_g", s["x1"], dh_g, 0)
        send_grad("ffn_w_up", i, mm_up_dw(f"l{i}_ffn_up_dw_v", s["x1"], dh_v, half, g_up))
        dr1, gl["norm1_g"][i], gl["norm1_b"][i], gl[mix + "_b_out"][j] = _ln_res_bwd(
            f"l{i}_norm1_bwd", dx1, s["xhat1"], s["rstd1"], n1g3, i, deps=take_tokens())
        ds = mm_out_dx(f"l{i}_{mix}_out_dx", dr1, s["w_out"])
        send_grad(mix + "_w_out", j, mm_out_dw(f"l{i}_{mix}_out_dw", s["s"], dr1).reshape(N_DEV, -1, d))
        if i % 2 == 0:
            dc, gl["conv_ln_g"][j], gl["conv_ln_b"][j], gl["conv_b_dw"][j] = _ln_silu_bwd(
                f"l{i}_ln_silu_bwd", ds, s["c"], conv_ln_g3, conv_ln_b3, j, deps=take_tokens())
            du, dwdw = _dwconv31_bwd(f"l{i}_dwconv_bwd", dc, s["u"], conv_w_dw, j, seq)
            gl["conv_w_dw"][j] = dwdw[:CONV_K]
            dh, gl["conv_b_in"][j] = _glu_bwd(f"l{i}_glu_bwd", du, s["h"])
        else:
            dh, gl["gmlp_b_in"][j], gl["gmlp_w_s"][j], gl["gmlp_b_s"][j], gl["gmlp_ln_g"][j], gl["gmlp_ln_b"][j] = (
                _sgu_bwd(f"l{i}_sgu_bwd", ds, s["h"], gmlp_ln_g3, gmlp_ln_b3, w_causal, bs_rows, j,
                         deps=take_tokens()))
        dx = mm_in_dx(f"l{i}_{mix}_in_dx", dh, s["w_in"], dr1)
        send_grad(mix + "_w_in", j, mm_in_dw(f"l{i}_{mix}_in_dw", s["x"], dh))
    grad_x = dx.reshape(bsz, seq, d)

    full_small = {n: jnp.stack(gl[n]).reshape(p[n].shape) for n in REPLICATED}
    shard_small = {}
    for n in SMALL_SHARDED:
        if n == "ffn_w_dw":
            shard_small[n] = jnp.stack(gl[n]).transpose(1, 0, 2, 3)
        else:
            width = p[n].shape[-1]
            lead = p[n].shape[:-1]
            shard_small[n] = _to_shards(jnp.stack(gl[n]).reshape(lead + (N_DEV * width,)), width)
    flat_shapes = [(1, LANES)] + [p[n].shape for n in REPLICATED] + [(N_DEV,) + p[n].shape for n in SMALL_SHARDED]
    flat_local = _pack([loss_row] + [full_small[n] for n in REPLICATED] + [shard_small[n] for n in SMALL_SHARDED],
                       row_multiple=ROW_TILE_CAP)

    small_parts = _all_gather("gather_small_grads", [flat_local[None]])[0][0]
    small_sum = _sum8("sum_small_grads", small_parts)
    summed = _unpack(small_sum, flat_shapes)
    loss = summed[0][0, 0]
    grads = dict(zip(REPLICATED, summed[1:1 + len(REPLICATED)]))
    for n, g in zip(SMALL_SHARDED, summed[1 + len(REPLICATED):]):
        grads[n] = lax.dynamic_index_in_dim(g, dev, axis=0, keepdims=False)

    flat_started = [st for n in BIG for st in started[n]]
    parts_done, lands_done = _scatter_wait("grads_scatter_wait", flat_started, small_sum)

    delta, new_m, new_v = {}, {}, {}
    dev1 = jnp.reshape(dev, (1,)).astype(jnp.int32)
    pos = 0
    for n in BIG:
        nl = p[n].shape[0]
        state = [p[n], p["m_" + n], p["v_" + n]]
        if n == "ffn_w_up":
            state = [a.transpose(0, 2, 1) for a in state]
        outs = _sum8_adamw(f"adamw_{n}", dev1, lands_done[pos:pos + nl], parts_done[pos:pos + nl], *state)
        if n == "ffn_w_up":
            outs = [a.transpose(0, 2, 1) for a in outs]
        grads[n], delta[n], new_m[n], new_v[n] = outs
        pos += nl
    small = REPLICATED + SMALL_SHARDED
    small_shp = [p[n].shape for n in small]
    d_s, m_s, v_s = _adamw_flat("adamw_small", _pack([grads[n] for n in small]), _pack([p[n] for n in small]),
                                _pack([p["m_" + n] for n in small]), _pack([p["v_" + n] for n in small]))
    for n, dd, mm, vv in zip(small, _unpack(d_s, small_shp), _unpack(m_s, small_shp), _unpack(v_s, small_shp)):
        delta[n], new_m[n], new_v[n] = dd, mm, vv

    return (loss, grad_x, *[grads[n] for n in WEIGHTS], *[delta[n] for n in WEIGHTS],
            *[new_m[n] for n in WEIGHTS], *[new_v[n] for n in WEIGHTS])


def kernel(x, conv_w_in, conv_b_in, conv_w_dw, conv_b_dw, conv_ln_g, conv_ln_b, conv_w_out, conv_b_out, gmlp_w_in, gmlp_b_in, gmlp_ln_g, gmlp_ln_b, gmlp_w_s, gmlp_b_s, gmlp_w_out, gmlp_b_out, ffn_w_up, ffn_b_up, ffn_w_dw, ffn_b_dw, ffn_w_down, ffn_b_down, norm1_g, norm1_b, norm2_g, norm2_b, loss_target, m_conv_w_in, m_conv_b_in, m_conv_w_dw, m_conv_b_dw, m_conv_ln_g, m_conv_ln_b, m_conv_w_out, m_conv_b_out, m_gmlp_w_in, m_gmlp_b_in, m_gmlp_ln_g, m_gmlp_ln_b, m_gmlp_w_s, m_gmlp_b_s, m_gmlp_w_out, m_gmlp_b_out, m_ffn_w_up, m_ffn_b_up, m_ffn_w_dw, m_ffn_b_dw, m_ffn_w_down, m_ffn_b_down, m_norm1_g, m_norm1_b, m_norm2_g, m_norm2_b, v_conv_w_in, v_conv_b_in, v_conv_w_dw, v_conv_b_dw, v_conv_ln_g, v_conv_ln_b, v_conv_w_out, v_conv_b_out, v_gmlp_w_in, v_gmlp_b_in, v_gmlp_ln_g, v_gmlp_ln_b, v_gmlp_w_s, v_gmlp_b_s, v_gmlp_w_out, v_gmlp_b_out, v_ffn_w_up, v_ffn_b_up, v_ffn_w_dw, v_ffn_b_dw, v_ffn_w_down, v_ffn_b_down, v_norm1_g, v_norm1_b, v_norm2_g, v_norm2_b):
    return _step(dict(locals()))
```

```python
import math

import jax
import jax.numpy as jnp
from jax import lax
from jax.experimental import pallas as pl
from jax.experimental.pallas import tpu as pltpu

F32 = jnp.float32
BF16 = jnp.bfloat16
MESH = pl.DeviceIdType.MESH

N_DEV = 8
DEPTH = 4
ALPHA = (2.0 * DEPTH) ** 0.25
LN_EPS = 1e-5
CONV_K = 31
FFN_K = 3
CHUNK = 128
GROUPS = 8
ADAM_LR = 0.001
ADAM_B1 = 0.9
ADAM_B2 = 0.999
ADAM_EPS = 1e-08
ADAM_WD = 0.01
ADAM_STEP = 10
INV_SQRT2 = 1.0 / math.sqrt(2.0)
INV_SQRT2PI = 1.0 / math.sqrt(2.0 * math.pi)

LANES = 128
SUBLANES = 8
VMEM_LIMIT = 48 * 1024 * 1024
TM_MM = 1024
TM_EW = 256


def _call(body, name, grid, in_specs, out_specs, out_shape, scratch=(), aliases=None, deps=()):
    deps = list(deps)
    in_specs = list(in_specs)
    n_in = len(in_specs)
    if deps:
        inner = body

        def body(*refs):
            return inner(*refs[:n_in], *refs[n_in + len(deps):])

        in_specs = in_specs + [pl.BlockSpec(memory_space=pl.ANY)] * len(deps)
    fn = pl.pallas_call(
        body, name=name, grid=grid, in_specs=in_specs, out_specs=out_specs, out_shape=out_shape,
        scratch_shapes=list(scratch), input_output_aliases=aliases or {},
        compiler_params=pltpu.CompilerParams(vmem_limit_bytes=VMEM_LIMIT))
    return lambda *args: fn(*args, *deps)


def _sds(shape, dtype):
    return jax.ShapeDtypeStruct(tuple(shape), dtype)


def _sigmoid(x):
    return 1.0 / (1.0 + jnp.exp(-x))


def _acc_rows(ref, val, first):
    @pl.when(first)
    def _():
        ref[...] = val

    @pl.when(jnp.logical_not(first))
    def _():
        ref[...] += val


def _colsum(v):
    return jnp.sum(v, axis=0, keepdims=True)


_DIMS = {"nn": ((1,), (0,)), "nt": ((1,), (1,)), "tn": ((0,), (0,))}


def _matmul(name, a, b, mode, *, grid, a_spec, b_spec, o_spec, o_shape, o_dtype, k_axis=None, nk=1,
            acc_shape=None, bias=None, bias_spec=None, res=None, res_spec=None, res_scale=1.0,
            prev=None):
    dims = (_DIMS[mode], ((), ()))
    has_bias, has_res, has_prev = bias is not None, res is not None, prev is not None

    def body(*refs):
        a_ref, b_ref = refs[0], refs[1]
        pos = 2
        bias_ref = res_ref = None
        if has_bias:
            bias_ref = refs[pos]
            pos += 1
        if has_res:
            res_ref = refs[pos]
            pos += 1
        if has_prev:
            pos += 1
        o_ref = refs[pos]
        acc_ref = refs[pos + 1] if nk > 1 else None
        p = lax.dot_general(a_ref[...].astype(BF16), b_ref[...].astype(BF16), dims, preferred_element_type=F32)

        def finish(acc):
            if has_bias:
                acc = acc + bias_ref[...]
            if has_res:
                acc = acc + res_scale * res_ref[...]
            o_ref[...] = acc.astype(o_dtype)

        if nk == 1:
            finish(p)
        else:
            k = pl.program_id(k_axis)

            @pl.when(k == 0)
            def _():
                acc_ref[...] = p

            @pl.when(k > 0)
            def _():
                acc_ref[...] += p

            @pl.when(k == nk - 1)
            def _():
                finish(acc_ref[...])

    ins, specs = [a, b], [a_spec, b_spec]
    if has_bias:
        ins.append(bias)
        specs.append(bias_spec)
    if has_res:
        ins.append(res)
        specs.append(res_spec)
    aliases = None
    if has_prev:
        aliases = {len(ins): 0}
        ins.append(prev)
        specs.append(pl.BlockSpec(memory_space=pl.ANY))
    scratch = [pltpu.VMEM(acc_shape, F32)] if nk > 1 else []
    return _call(body, name, grid, specs, o_spec, _sds(o_shape, o_dtype), scratch, aliases)(*ins)


TM_LN = 512


def _matmul_ln(name, a, b, x_res, bias3, g3, b3, l_bias, l_norm, *, nk, a_block, b_block):
    t, d = x_res.shape
    tm = min(TM_LN, t)

    def body(a_ref, b_ref, x_ref, bias_ref, g_ref, be_ref, o_ref, xh_ref, rs_ref, *acc):
        p = jnp.dot(a_ref[...].astype(BF16), b_ref[...], preferred_element_type=F32)

        def finish(y):
            xhat, rstd = _ln_stats(ALPHA * x_ref[...] + y + bias_ref[...])
            o_ref[...] = xhat * g_ref[...] + be_ref[...]
            xh_ref[...] = xhat
            rs_ref[...] = rstd

        if nk == 1:
            finish(p)
        else:
            k = pl.program_id(1)

            @pl.when(k == 0)
            def _():
                acc[0][...] = p

            @pl.when(k > 0)
            def _():
                acc[0][...] += p

            @pl.when(k == nk - 1)
            def _():
                finish(acc[0][...])

    if nk == 1:
        grid = (t // tm,)
        a_spec = pl.BlockSpec(a_block, lambda i: (i, 0))
        b_spec = pl.BlockSpec(b_block, lambda i: (0, 0))
    else:
        grid = (t // tm, nk)
        a_spec = pl.BlockSpec(a_block, lambda i, k: (k, i, 0))
        b_spec = pl.BlockSpec(b_block, lambda i, k: (k, 0, 0))
    row = pl.BlockSpec((tm, d), lambda i, *_: (i, 0))
    stat = pl.BlockSpec((tm, 1), lambda i, *_: (i, 0))

    def vec(l):
        return pl.BlockSpec((None, 1, d), lambda *_: (l, 0, 0))

    return _call(body, name, grid, [a_spec, b_spec, row, vec(l_bias), vec(l_norm), vec(l_norm)], [row, row, stat],
                 [_sds((t, d), F32), _sds((t, d), F32), _sds((t, 1), F32)],
                 [pltpu.VMEM((tm, d), F32)] if nk > 1 else [])(a, b, x_res, bias3, g3, b3)


def _mesh_pos():
    return lax.axis_index("x"), lax.axis_index("y"), lax.axis_index("c")


def _any_specs(n):
    return [pl.BlockSpec(memory_space=pl.ANY)] * n


def _all_gather(name, srcs):
    n = len(srcs)

    def body(*refs):
        src, out = refs[:n], refs[n:2 * n]
        send_sems, recv_sems, local_sems = refs[2 * n:]
        x, y, c = _mesh_pos()
        me, sibling = (x, y, c), (x, y, 1 - c)
        chips = [(1 - x, y), (x, 1 - y), (1 - x, 1 - y)]

        def slot(k, p):
            return out[k].at[:, 4 * p[0] + 2 * p[1] + p[2]]

        def copy(k, idx, block, to, s=None):
            return pltpu.make_async_remote_copy(
                src_ref=slot(k, block) if s is None else s, dst_ref=slot(k, block),
                send_sem=send_sems.at[k * 7 + idx], recv_sem=recv_sems.at[k * 7 + idx],
                device_id=to, device_id_type=MESH)

        local = [pltpu.make_async_copy(src[k], slot(k, me), local_sems.at[k]) for k in range(n)]
        for cp in local:
            cp.start()
        first = []
        for k in range(n):
            first.append(copy(k, 0, me, sibling, src[k]))
            for j, chip in enumerate(chips):
                first.append(copy(k, 1 + j, me, (*chip, c), src[k]))
        for cp in first:
            cp.start()
        passed = []
        for j, chip in enumerate(chips):
            for k in range(n):
                copy(k, 1 + j, (*chip, c), me).wait_recv()
                cp = copy(k, 4 + j, (*chip, c), sibling)
                cp.start()
                passed.append(cp)
        for k in range(n):
            copy(k, 0, sibling, me).wait_recv()
            for j, chip in enumerate(chips):
                copy(k, 4 + j, (*chip, 1 - c), me).wait_recv()
        for cp in first + passed:
            cp.wait_send()
        for cp in local:
            cp.wait()

    out_shape = [_sds((s.shape[0], N_DEV) + s.shape[1:], s.dtype) for s in srcs]
    return _call(body, name, (), [pl.BlockSpec(memory_space=pltpu.VMEM)] * n, _any_specs(n), out_shape,
                 [pltpu.SemaphoreType.DMA((7 * n,)), pltpu.SemaphoreType.DMA((7 * n,)),
                  pltpu.SemaphoreType.DMA((n,))])(*srcs)


HBM_SPEC = pl.BlockSpec(memory_space=pltpu.HBM)
SEM_SPEC = pl.BlockSpec(memory_space=pltpu.SEMAPHORE)
N_PEER = N_DEV - 1


def _split_call(body, name, in_specs, out_specs, out_shape, aliases):
    return pl.pallas_call(
        body, name=name, in_specs=in_specs, out_specs=out_specs, out_shape=out_shape, input_output_aliases=aliases,
        compiler_params=pltpu.CompilerParams(has_side_effects=pltpu.SideEffectType.DATAFLOW_SIDE_EFFECTING))


def _peers(x, y, c):
    return [(1 - x if q & 4 else x, 1 - y if q & 2 else y, 1 - c if q & 1 else c) for q in range(1, N_DEV)]


def _in_hbm(a):
    return pltpu.with_memory_space_constraint(a, pltpu.HBM)


def _place_own(name, srcs, deps=()):
    n = len(srcs)

    def body(*refs):
        src, out, sems = refs[:n], refs[n:2 * n], refs[2 * n]
        x, y, c = _mesh_pos()
        dev = 4 * x + 2 * y + c
        copies = [pltpu.make_async_copy(src[k], out[k].at[dev], sems.at[k]) for k in range(n)]
        for cp in copies:
            cp.start()
        for cp in copies:
            cp.wait()

    return _call(body, name, (), [pl.BlockSpec(memory_space=pltpu.VMEM)] * n, _any_specs(n),
                 [_sds((N_DEV,) + s.shape, s.dtype) for s in srcs], [pltpu.SemaphoreType.DMA((n,))],
                 deps=deps)(*srcs)


def _gather_start(name, lands):
    n = len(lands)

    def body(*refs):
        land, send_sems, recv_sems = refs[:n], refs[n], refs[n + 1]
        x, y, c = _mesh_pos()
        dev = 4 * x + 2 * y + c
        for k in range(n):
            for peer in _peers(x, y, c):
                pltpu.make_async_remote_copy(
                    src_ref=land[k].at[dev], dst_ref=land[k].at[dev], send_sem=send_sems.at[k],
                    recv_sem=recv_sems.at[k], device_id=peer, device_id_type=MESH).start()

    outs = _split_call(
        body, name, [HBM_SPEC] * n, [SEM_SPEC, SEM_SPEC] + [HBM_SPEC] * n,
        [pltpu.SemaphoreType.DMA((n,)), pltpu.SemaphoreType.DMA((n,))] + [pltpu.HBM(a.shape, a.dtype) for a in lands],
        {k: 2 + k for k in range(n)})(*[_in_hbm(a) for a in lands])
    return outs[0], outs[1], list(outs[2:])


def _wait_seven(src_ref, dst_ref, send_sem, recv_sem):
    cp = pltpu.make_async_remote_copy(
        src_ref=src_ref.at[pl.ds(0, N_PEER)], dst_ref=dst_ref.at[pl.ds(0, N_PEER)], send_sem=send_sem,
        recv_sem=recv_sem, device_id=_mesh_pos(), device_id_type=MESH)
    cp.wait_send()
    cp.wait_recv()


def _gather_wait(name, land, send_sems, recv_sems, k, after):
    def body(land_ref, send_ref, recv_ref, after_ref, out_ref):
        _wait_seven(land_ref, land_ref, send_ref.at[k], recv_ref.at[k])

    return _split_call(body, name, [HBM_SPEC, SEM_SPEC, SEM_SPEC, pl.BlockSpec(memory_space=pl.ANY)], HBM_SPEC,
                       pltpu.HBM(land.shape, land.dtype), {0: 0})(land, send_sems, recv_sems, after)


def _scatter_start(name, parts):
    def body(parts_ref, land_ref, send_sem, recv_sem, parts_out, land_out, token):
        x, y, c = _mesh_pos()
        dev = 4 * x + 2 * y + c
        for peer in _peers(x, y, c):
            pltpu.make_async_remote_copy(
                src_ref=parts_ref.at[4 * peer[0] + 2 * peer[1] + peer[2]], dst_ref=land_ref.at[dev],
                send_sem=send_sem, recv_sem=recv_sem, device_id=peer, device_id_type=MESH).start()
        token[...] = jnp.zeros_like(token)

    buf = pltpu.HBM(parts.shape, parts.dtype)
    return _split_call(
        body, name, [HBM_SPEC, HBM_SPEC],
        [SEM_SPEC, SEM_SPEC, HBM_SPEC, HBM_SPEC, pl.BlockSpec(memory_space=pltpu.VMEM)],
        [pltpu.SemaphoreType.DMA(()), pltpu.SemaphoreType.DMA(()), buf, buf, _sds((SUBLANES, LANES), F32)],
        {0: 2, 1: 3})(_in_hbm(parts), _in_hbm(lax.empty(parts.shape, parts.dtype)))


def _scatter_wait(name, started, after):
    n = len(started)

    def body(*refs):
        for k in range(n):
            send_sem, recv_sem, parts_ref, land_ref = refs[4 * k:4 * k + 4]
            _wait_seven(parts_ref, land_ref, send_sem, recv_sem)

    flat = [a for s in started for a in s]
    outs = _split_call(
        body, name, [SEM_SPEC, SEM_SPEC, HBM_SPEC, HBM_SPEC] * n + [pl.BlockSpec(memory_space=pl.ANY)],
        [HBM_SPEC, HBM_SPEC] * n, [pltpu.HBM(a.shape, a.dtype) for s in started for a in s[2:]],
        {4 * k + 2 + t: 2 * k + t for k in range(n) for t in range(2)})(*flat, after)
    return list(outs[0::2]), list(outs[1::2])


def _to_segments(a, tile):
    seg = tile // SUBLANES
    return a.reshape((a.shape[0] // tile, SUBLANES, seg) + a.shape[1:]).swapaxes(1, 2).reshape(a.shape)


def _from_segments(a, tile):
    seg = tile // SUBLANES
    return a.reshape((a.shape[0] // tile, seg, SUBLANES) + a.shape[1:]).swapaxes(1, 2).reshape(a.shape)


def _chunk(ref, q):
    return ref[q * SUBLANES:(q + 1) * SUBLANES, :]


def _fill_wrap_prev(x_ref, halo_ref, wrap_ref, n_wrap, n_halo, seg, keep):
    sub = lax.broadcasted_iota(jnp.int32, (SUBLANES, x_ref.shape[-1]), 0)
    for j in range(n_wrap):
        q = seg - n_wrap + j
        hq = q - (seg - n_halo)
        row = halo_ref[hq * SUBLANES + SUBLANES - 1:(hq + 1) * SUBLANES, :] * keep
        wrap_ref[j * SUBLANES:(j + 1) * SUBLANES, :] = jnp.where(sub == 0, row, pltpu.roll(_chunk(x_ref, q), 1, 0))


def _fill_wrap_next(x_ref, halo_ref, wrap_ref, n_wrap, keep):
    sub = lax.broadcasted_iota(jnp.int32, (SUBLANES, x_ref.shape[-1]), 0)
    for j in range(n_wrap):
        row = halo_ref[j * SUBLANES:j * SUBLANES + 1, :] * keep
        wrap_ref[j * SUBLANES:(j + 1) * SUBLANES, :] = jnp.where(
            sub == SUBLANES - 1, row, pltpu.roll(_chunk(x_ref, j), SUBLANES - 1, 0))


def _past(x_ref, wrap_ref, q, d, n_wrap):
    return _chunk(x_ref, q - d) if q >= d else _chunk(wrap_ref, q - d + n_wrap)


def _future(x_ref, wrap_ref, q, d, seg):
    return _chunk(x_ref, q + d) if q + d < seg else _chunk(wrap_ref, q + d - seg)


def _conv_fwd(x_ref, wrap_ref, w_ref, b_ref, out_ref, seg, k_taps):
    bias = jnp.broadcast_to(b_ref[...], (SUBLANES, x_ref.shape[-1]))
    for q in range(seg):
        acc = bias
        for k in range(k_taps):
            acc = acc + w_ref[k:k + 1, :] * _past(x_ref, wrap_ref, q, k_taps - 1 - k, k_taps - 1)
        out_ref[q * SUBLANES:(q + 1) * SUBLANES, :] = acc


def _conv_bwd_data(d_ref, wrap_ref, w_ref, out_ref, seg, k_taps):
    for q in range(seg):
        acc = None
        for k in range(k_taps):
            term = w_ref[k:k + 1, :] * _future(d_ref, wrap_ref, q, k_taps - 1 - k, seg)
            acc = term if acc is None else acc + term
        out_ref[q * SUBLANES:(q + 1) * SUBLANES, :] = acc


def _conv_bwd_taps(d_ref, x_ref, wrap_ref, dw_ref, seg, k_taps):
    for k in range(k_taps):
        part = None
        for q in range(seg):
            term = _chunk(d_ref, q) * _past(x_ref, wrap_ref, q, k_taps - 1 - k, k_taps - 1)
            part = term if part is None else part + term
        dw_ref[k:k + 1, :] += _colsum(part)


def _tile_halo_specs(tm, width_block, n_halo, n_tiles, block_of):
    rows = n_halo * SUBLANES
    per = tm // rows
    tile = pl.BlockSpec(width_block(tm), lambda n, i: block_of(n, i))
    prev = pl.BlockSpec(width_block(rows), lambda n, i: block_of(n, jnp.maximum(i * per - 1, 0)))
    nxt = pl.BlockSpec(width_block(rows), lambda n, i: block_of(n, jnp.minimum((i + 1) * per, n_tiles * per - 1)))
    return tile, prev, nxt


def _ln_stats(v):
    mu = jnp.mean(v, axis=-1, keepdims=True)
    vc = v - mu
    var = jnp.mean(vc * vc, axis=-1, keepdims=True)
    rstd = lax.rsqrt(var + LN_EPS)
    return vc * rstd, rstd


def _ln_backward(dxhat, xhat, rstd):
    m1 = jnp.mean(dxhat, axis=-1, keepdims=True)
    m2 = jnp.mean(dxhat * xhat, axis=-1, keepdims=True)
    return rstd * (dxhat - m1 - xhat * m2)


def _row_spec(tm, width):
    return pl.BlockSpec((tm, width), lambda i: (i, 0))


def _param_spec(l, width):
    return pl.BlockSpec((None, 1, width), lambda *_: (l, 0, 0))


def _ln_res_bwd(name, dout, xhat, rstd, g3, l, deps=()):
    t, d = dout.shape
    tm = min(TM_EW, t)

    def body(do_ref, xh_ref, rs_ref, g_ref, dr_ref, dg_ref, db_ref, dc_ref):
        first = pl.program_id(0) == 0
        do, xhat = do_ref[...], xh_ref[...]
        dr = _ln_backward(do * g_ref[...], xhat, rs_ref[...])
        dr_ref[...] = dr
        _acc_rows(dg_ref, _colsum(do * xhat), first)
        _acc_rows(db_ref, _colsum(do), first)
        _acc_rows(dc_ref, _colsum(dr), first)

    vec = pl.BlockSpec((1, d), lambda i: (0, 0))
    return _call(body, name, (t // tm,),
                 [_row_spec(tm, d), _row_spec(tm, d), _row_spec(tm, 1), _param_spec(l, d)],
                 [_row_spec(tm, d), vec, vec, vec],
                 [_sds((t, d), F32)] + [_sds((1, d), F32)] * 3, deps=deps)(dout, xhat, rstd, g3)


def _glu(name, h):
    t, c2 = h.shape
    c = c2 // 2
    tm = min(TM_EW, t)

    def body(a_ref, g_ref, o_ref):
        o_ref[...] = a_ref[...] * _sigmoid(g_ref[...])

    return _call(body, name, (t // tm,),
                 [pl.BlockSpec((tm, c), lambda i: (i, 0)), pl.BlockSpec((tm, c), lambda i: (i, 1))],
                 _row_spec(tm, c), _sds((t, c), F32))(h, h)


def _glu_bwd(name, du, h):
    t, c2 = h.shape
    c = c2 // 2
    tm = min(TM_EW, t)

    def body(du_ref, a_ref, g_ref, dh_ref, db_ref):
        first = pl.program_id(0) == 0
        du_v, a = du_ref[...], a_ref[...]
        sg = _sigmoid(g_ref[...])
        da = du_v * sg
        dg = du_v * a * sg * (1.0 - sg)
        dh_ref[:, :c] = da.astype(BF16)
        dh_ref[:, c:] = dg.astype(BF16)
        _acc_rows(db_ref.at[:, :c], _colsum(da), first)
        _acc_rows(db_ref.at[:, c:], _colsum(dg), first)

    return _call(body, name, (t // tm,),
                 [_row_spec(tm, c), pl.BlockSpec((tm, c), lambda i: (i, 0)), pl.BlockSpec((tm, c), lambda i: (i, 1))],
                 [_row_spec(tm, c2), pl.BlockSpec((1, c2), lambda i: (0, 0))],
                 [_sds((t, c2), BF16), _sds((1, c2), F32)])(du, h, h)


CONV_CB = 256
TAPS_PAD = 32


def _dwconv31(name, u, w3, b3, l, seq):
    t, c = u.shape
    tm, cb = TM_EW, CONV_CB
    seg, seq_tiles, n_tiles = tm // SUBLANES, seq // tm, t // tm
    n_wrap = CONV_K - 1
    tile, prev, _ = _tile_halo_specs(tm, lambda rows: (rows, cb), seg, n_tiles, lambda n, r: (r, n))

    def body(u_ref, halo_ref, w_ref, b_ref, o_ref, wrap_ref):
        keep = (pl.program_id(1) % seq_tiles != 0).astype(F32)
        _fill_wrap_prev(u_ref, halo_ref, wrap_ref, n_wrap, seg, seg, keep)
        _conv_fwd(u_ref, wrap_ref, w_ref, b_ref, o_ref, seg, CONV_K)

    return _call(body, name, (c // cb, n_tiles),
                 [tile, prev, pl.BlockSpec((None, CONV_K, cb), lambda n, i: (l, 0, n)),
                  pl.BlockSpec((None, 1, cb), lambda n, i: (l, 0, n))],
                 tile, _sds((t, c), F32), [pltpu.VMEM((n_wrap * SUBLANES, cb), F32)])(u, u, w3, b3)


def _dwconv31_bwd(name, dc, u, w3, l, seq):
    t, c = dc.shape
    tm, cb = TM_EW, CONV_CB
    seg, seq_tiles, n_tiles = tm // SUBLANES, seq // tm, t // tm
    n_wrap = CONV_K - 1
    tile, prev, nxt = _tile_halo_specs(tm, lambda rows: (rows, cb), seg, n_tiles, lambda n, r: (r, n))

    def body(dc_ref, dcn_ref, u_ref, up_ref, w_ref, du_ref, dw_ref, dwrap_ref, uwrap_ref):
        i = pl.program_id(1)
        keep_prev = (i % seq_tiles != 0).astype(F32)
        keep_next = (i % seq_tiles != seq_tiles - 1).astype(F32)
        _fill_wrap_next(dc_ref, dcn_ref, dwrap_ref, n_wrap, keep_next)
        _conv_bwd_data(dc_ref, dwrap_ref, w_ref, du_ref, seg, CONV_K)

        @pl.when(i == 0)
        def _():
            dw_ref[...] = jnp.zeros_like(dw_ref)

        _fill_wrap_prev(u_ref, up_ref, uwrap_ref, n_wrap, seg, seg, keep_prev)
        _conv_bwd_taps(dc_ref, u_ref, uwrap_ref, dw_ref, seg, CONV_K)

    wrap = pltpu.VMEM((n_wrap * SUBLANES, cb), F32)
    return _call(body, name, (c // cb, n_tiles),
                 [tile, nxt, tile, prev, pl.BlockSpec((None, CONV_K, cb), lambda n, i: (l, 0, n))],
                 [tile, pl.BlockSpec((TAPS_PAD, cb), lambda n, i: (0, n))],
                 [_sds((t, c), F32), _sds((TAPS_PAD, c), F32)], [wrap, wrap])(dc, dc, u, u, w3)


def _ln_silu(name, cx, g3, b3, l):
    t, d = cx.shape
    tm = min(TM_EW, t)

    def body(c_ref, g_ref, b_ref, o_ref):
        xhat, _ = _ln_stats(c_ref[...])
        nv = xhat * g_ref[...] + b_ref[...]
        o_ref[...] = (nv * _sigmoid(nv)).astype(BF16)

    return _call(body, name, (t // tm,), [_row_spec(tm, d), _param_spec(l, d), _param_spec(l, d)],
                 _row_spec(tm, d), _sds((t, d), BF16))(cx, g3, b3)


def _ln_silu_bwd(name, ds, cx, g3, b3, l, deps=()):
    t, d = cx.shape
    tm = min(TM_EW, t)

    def body(ds_ref, c_ref, g_ref, b_ref, dc_ref, dg_ref, db_ref, dsum_ref):
        first = pl.program_id(0) == 0
        xhat, rstd = _ln_stats(c_ref[...])
        g = g_ref[...]
        nv = xhat * g + b_ref[...]
        sg = _sigmoid(nv)
        dn = ds_ref[...] * (sg * (1.0 + nv * (1.0 - sg)))
        dc = _ln_backward(dn * g, xhat, rstd)
        dc_ref[...] = dc
        _acc_rows(dg_ref, _colsum(dn * xhat), first)
        _acc_rows(db_ref, _colsum(dn), first)
        _acc_rows(dsum_ref, _colsum(dc), first)

    vec = pl.BlockSpec((1, d), lambda i: (0, 0))
    return _call(body, name, (t // tm,),
                 [_row_spec(tm, d), _row_spec(tm, d), _param_spec(l, d), _param_spec(l, d)],
                 [_row_spec(tm, d), vec, vec, vec],
                 [_sds((t, d), F32)] + [_sds((1, d), F32)] * 3, deps=deps)(ds, cx, g3, b3)


FFN_HALO = FFN_K - 1


def _blk_specs(tm, fb, n_tiles, shift):
    return _tile_halo_specs(tm, lambda rows: (None, rows, fb), FFN_HALO, n_tiles, lambda n, r: (n + shift, r, 0))


def _ffn_conv(x_ref, halo_ref, wrap_ref, w_ref, b_ref, keep, seg, out_ref):
    _fill_wrap_prev(x_ref, halo_ref, wrap_ref, FFN_K - 1, FFN_HALO, seg, keep)
    _conv_fwd(x_ref, wrap_ref, w_ref, b_ref, out_ref, seg, FFN_K)


def _ffn_act(name, h, wdw, bdw, l, seq):
    nb, t, fb = h.shape
    half = nb // 2
    tm = TM_EW
    seg, seq_tiles, n_tiles = tm // SUBLANES, seq // tm, t // tm
    g_tile, g_prev, _ = _blk_specs(tm, fb, n_tiles, 0)
    v_tile, v_prev, _ = _blk_specs(tm, fb, n_tiles, half)

    def body(g_ref, gp_ref, v_ref, vp_ref, wg_ref, wv_ref, bg_ref, bv_ref, o_ref, wrap_ref, cg_ref, cv_ref):
        keep = (pl.program_id(1) % seq_tiles != 0).astype(F32)
        _ffn_conv(g_ref, gp_ref, wrap_ref, wg_ref, bg_ref, keep, seg, cg_ref)
        _ffn_conv(v_ref, vp_ref, wrap_ref, wv_ref, bv_ref, keep, seg, cv_ref)
        cg = cg_ref[...]
        o_ref[...] = (cg * _sigmoid(cg) * cv_ref[...]).astype(BF16)

    def wspec(shift, rows):
        return pl.BlockSpec((None, None, rows, fb), lambda n, i: (l, n + shift, 0, 0))

    return _call(body, name, (half, n_tiles),
                 [g_tile, g_prev, v_tile, v_prev, wspec(0, FFN_K), wspec(half, FFN_K), wspec(0, 1), wspec(half, 1)],
                 pl.BlockSpec((None, tm, fb), lambda n, i: (n, i, 0)), _sds((half, t, fb), BF16),
                 [pltpu.VMEM((FFN_HALO * SUBLANES, fb), F32), pltpu.VMEM((tm, fb), F32), pltpu.VMEM((tm, fb), F32)]
                 )(h, h, h, h, wdw, wdw, bdw, bdw)


def _ffn_act_bwd(name, da, h, wdw, bdw, l, seq, deps=()):
    nb, t, fb = h.shape
    half = nb // 2
    tm = TM_EW
    seg, seq_tiles, n_tiles = tm // SUBLANES, seq // tm, t // tm
    g_tile, g_prev, _ = _blk_specs(tm, fb, n_tiles, 0)
    v_tile, v_prev, _ = _blk_specs(tm, fb, n_tiles, half)

    def body(da_ref, g_ref, gp_ref, v_ref, vp_ref, wg_ref, wv_ref, bg_ref, bv_ref,
             dg_ref, dv_ref, dbg_ref, dbv_ref, dwg_ref, dwv_ref, gwrap_ref, vwrap_ref, cg_ref, cv_ref):
        i = pl.program_id(1)
        first = i == 0
        keep = (i % seq_tiles != 0).astype(F32)
        _ffn_conv(g_ref, gp_ref, gwrap_ref, wg_ref, bg_ref, keep, seg, cg_ref)
        _ffn_conv(v_ref, vp_ref, vwrap_ref, wv_ref, bv_ref, keep, seg, cv_ref)
        cg, cv, da_v = cg_ref[...], cv_ref[...], da_ref[...]
        sg = _sigmoid(cg)
        dcv = da_v * cg * sg
        dcg = da_v * cv * sg * (1.0 + cg * (1.0 - sg))
        dg_ref[...] = dcg
        dv_ref[...] = dcv
        _acc_rows(dbg_ref, _colsum(dcg), first)
        _acc_rows(dbv_ref, _colsum(dcv), first)

        @pl.when(first)
        def _():
            dwg_ref[...] = jnp.zeros_like(dwg_ref)
            dwv_ref[...] = jnp.zeros_like(dwv_ref)

        _conv_bwd_taps(dg_ref, g_ref, gwrap_ref, dwg_ref, seg, FFN_K)
        _conv_bwd_taps(dv_ref, v_ref, vwrap_ref, dwv_ref, seg, FFN_K)

    def wspec(shift, rows):
        return pl.BlockSpec((None, None, rows, fb), lambda n, i: (l, n + shift, 0, 0))

    ext = pltpu.VMEM((FFN_HALO * SUBLANES, fb), F32)
    tmp = pltpu.VMEM((tm, fb), F32)
    outs = _call(body, name, (half, n_tiles),
                 [pl.BlockSpec((None, tm, fb), lambda n, i: (n, i, 0)), g_tile, g_prev, v_tile, v_prev,
                  wspec(0, FFN_K), wspec(half, FFN_K), wspec(0, 1), wspec(half, 1)],
                 [pl.BlockSpec((None, tm, fb), lambda n, i: (n, i, 0)),
                  pl.BlockSpec((None, tm, fb), lambda n, i: (n, i, 0)),
                  pl.BlockSpec((None, 1, fb), lambda n, i: (n, 0, 0)), pl.BlockSpec((None, 1, fb), lambda n, i: (n, 0, 0)),
                  pl.BlockSpec((None, SUBLANES, fb), lambda n, i: (n, 0, 0)),
                  pl.BlockSpec((None, SUBLANES, fb), lambda n, i: (n, 0, 0))],
                 [_sds((half, t, fb), F32), _sds((half, t, fb), F32), _sds((half, 1, fb), F32),
                  _sds((half, 1, fb), F32), _sds((half, SUBLANES, fb), F32), _sds((half, SUBLANES, fb), F32)],
                 [ext, ext, tmp, tmp], deps=deps)(da, h, h, h, h, wdw, wdw, bdw, bdw)
    return outs


def _ffn_conv_t(name, dhc, wdw, l, shift, seq):
    half, t, fb = dhc.shape
    tm = TM_EW
    seg, seq_tiles, n_tiles = tm // SUBLANES, seq // tm, t // tm
    tile, _, nxt = _blk_specs(tm, fb, n_tiles, 0)

    def body(d_ref, dn_ref, w_ref, dh_ref, db_ref, wrap_ref, out_ref):
        i = pl.program_id(1)
        keep = (i % seq_tiles != seq_tiles - 1).astype(F32)
        _fill_wrap_next(d_ref, dn_ref, wrap_ref, FFN_K - 1, keep)
        _conv_bwd_data(d_ref, wrap_ref, w_ref, out_ref, seg, FFN_K)
        dh = out_ref[...]
        dh_ref[...] = dh.astype(BF16)
        _acc_rows(db_ref, _colsum(dh), i == 0)

    return _call(body, name, (half, n_tiles),
                 [tile, nxt, pl.BlockSpec((None, None, FFN_K, fb), lambda n, i: (l, n + shift, 0, 0))],
                 [tile, pl.BlockSpec((None, 1, fb), lambda n, i: (n, 0, 0))],
                 [_sds((half, t, fb), BF16), _sds((half, 1, fb), F32)],
                 [pltpu.VMEM((FFN_HALO * SUBLANES, fb), F32), pltpu.VMEM((tm, fb), F32)])(dhc, dhc, wdw)


def _gelu_parts(h):
    cdf = 0.5 * (1.0 + lax.erf(h * INV_SQRT2))
    return h * cdf, cdf


def _seg_axis(a, axis, fn):
    return jnp.moveaxis(fn(jnp.moveaxis(a, axis, 0), TM_EW), 0, axis)


def _sgu_operands(w_s, b_s):
    nl = w_s.shape[0]
    n_sub = TM_EW // CHUNK
    tril = jnp.tril(jnp.ones((CHUNK, CHUNK), dtype=bool))
    w_causal = jnp.where(tril, w_s, 0.0)
    w_tile = (jnp.eye(n_sub, dtype=F32)[None, None, :, None, :, None] * w_causal[:, :, None, :, None, :]).reshape(
        nl, GROUPS, TM_EW, TM_EW)
    w_tile = _seg_axis(_seg_axis(w_tile, 2, _to_segments), 3, _to_segments).astype(BF16)
    bs_tile = jnp.broadcast_to(b_s[:, :, None, :, None], (nl, GROUPS, n_sub, CHUNK, CHUNK)).reshape(
        nl, GROUPS, TM_EW, CHUNK)
    return w_tile, _seg_axis(bs_tile, 2, _to_segments)


def _sgu_param_grads(dwt, dbt):
    n_sub = TM_EW // CHUNK
    tril = jnp.tril(jnp.ones((CHUNK, CHUNK), dtype=bool))
    dwt = _seg_axis(_seg_axis(dwt, 1, _from_segments), 2, _from_segments).reshape(GROUPS, n_sub, CHUNK, n_sub, CHUNK)
    dw = sum(dwt[:, a, :, a, :] for a in range(n_sub))
    db = _seg_axis(dbt, 1, _from_segments).reshape(GROUPS, n_sub, CHUNK).sum(axis=1)
    return jnp.where(tril, dw, 0.0), db


def _sgu(name, h, g3, b3, wt, bst, l):
    t, c2 = h.shape
    c = c2 // 2
    tm = TM_EW

    def body(h_ref, g_ref, b_ref, wt_ref, bs_ref, o_ref):
        z, _ = _gelu_parts(h_ref[...])
        u = z[:, :c]
        xhat, _ = _ln_stats(z[:, c:])
        vnb = (xhat * g_ref[...] + b_ref[...]).astype(BF16)
        for gi in range(GROUPS):
            cs = slice(gi * CHUNK, (gi + 1) * CHUNK)
            sp = jnp.dot(wt_ref[gi], vnb[:, cs], preferred_element_type=F32) + bs_ref[gi]
            o_ref[:, cs] = (u[:, cs] * sp).astype(BF16)

    return _call(body, name, (t // tm,),
                 [_row_spec(tm, c2), _param_spec(l, c), _param_spec(l, c),
                  pl.BlockSpec((None, GROUPS, tm, tm), lambda i: (l, 0, 0, 0)),
                  pl.BlockSpec((None, GROUPS, tm, CHUNK), lambda i: (l, 0, 0, 0))],
                 _row_spec(tm, c), _sds((t, c), BF16))(h, g3, b3, wt, bst)


def _sgu_bwd(name, dq, h, g3, b3, wt, bst, l, deps=()):
    t, c2 = h.shape
    c = c2 // 2
    tm = TM_EW
    n_tiles = t // tm

    def body(dq_ref, h_ref, g_ref, b_ref, wt_ref, bs_ref,
             dh_ref, dbin_ref, dw_ref, dbs_ref, dg_ref, db_ref, du_ref, dvn_ref, bsum_ref):
        i = pl.program_id(0)
        first = i == 0
        hv = h_ref[...]
        z, cdf = _gelu_parts(hv)
        u = z[:, :c]
        xhat, rstd = _ln_stats(z[:, c:])
        g = g_ref[...]
        vnb = (xhat * g + b_ref[...]).astype(BF16)

        @pl.when(first)
        def _():
            dw_ref[...] = jnp.zeros_like(dw_ref)
            bsum_ref[...] = jnp.zeros_like(bsum_ref)

        for gi in range(GROUPS):
            cs = slice(gi * CHUNK, (gi + 1) * CHUNK)
            vb = vnb[:, cs]
            w = wt_ref[gi]
            sp = jnp.dot(w, vb, preferred_element_type=F32) + bs_ref[gi]
            dqb = dq_ref[:, cs]
            du_ref[:, cs] = dqb * sp
            dsp = dqb * u[:, cs]
            bsum_ref[gi] += dsp
            dspb = dsp.astype(BF16)
            dw_ref[gi] += lax.dot_general(dspb, vb, (_DIMS["nt"], ((), ())), preferred_element_type=F32)
            dvn_ref[:, cs] = lax.dot_general(w, dspb, (_DIMS["tn"], ((), ())), preferred_element_type=F32)

        dvn = dvn_ref[...]
        dv = _ln_backward(dvn * g, xhat, rstd)
        pdf = jnp.exp(-0.5 * hv * hv) * INV_SQRT2PI
        dgelu = cdf + hv * pdf
        dhu = du_ref[...] * dgelu[:, :c]
        dhv = dv * dgelu[:, c:]
        dh_ref[:, :c] = dhu.astype(BF16)
        dh_ref[:, c:] = dhv.astype(BF16)
        _acc_rows(dbin_ref.at[:, :c], _colsum(dhu), first)
        _acc_rows(dbin_ref.at[:, c:], _colsum(dhv), first)
        _acc_rows(dg_ref, _colsum(dvn * xhat), first)
        _acc_rows(db_ref, _colsum(dvn), first)

        @pl.when(i == n_tiles - 1)
        def _():
            dbs_ref[...] = jnp.sum(bsum_ref[...], axis=-1)

    vec = pl.BlockSpec((1, c), lambda i: (0, 0))
    return _call(body, name, (n_tiles,),
                 [_row_spec(tm, c), _row_spec(tm, c2), _param_spec(l, c), _param_spec(l, c),
                  pl.BlockSpec((None, GROUPS, tm, tm), lambda i: (l, 0, 0, 0)),
                  pl.BlockSpec((None, GROUPS, tm, CHUNK), lambda i: (l, 0, 0, 0))],
                 [_row_spec(tm, c2), pl.BlockSpec((1, c2), lambda i: (0, 0)),
                  pl.BlockSpec((GROUPS, tm, tm), lambda i: (0, 0, 0)),
                  pl.BlockSpec((GROUPS, tm), lambda i: (0, 0)), vec, vec],
                 [_sds((t, c2), BF16), _sds((1, c2), F32), _sds((GROUPS, tm, tm), F32),
                  _sds((GROUPS, tm), F32), _sds((1, c), F32), _sds((1, c), F32)],
                 [pltpu.VMEM((tm, c), F32), pltpu.VMEM((tm, c), F32), pltpu.VMEM((GROUPS, tm, CHUNK), F32)],
                 deps=deps)(dq, h, g3, b3, wt, bst)


def _loss(name, y, target):
    t, d = y.shape
    tm = min(TM_EW, t)
    n_tiles = t // tm

    def body(y_ref, t_ref, l_ref, dy_ref, acc_ref):
        i = pl.program_id(0)
        diff = y_ref[...] - t_ref[...]
        dy_ref[...] = diff * (1.0 / d)
        _acc_rows(acc_ref, _colsum(diff * diff), i == 0)

        @pl.when(i == n_tiles - 1)
        def _():
            l_ref[...] = jnp.broadcast_to(jnp.sum(acc_ref[...], axis=-1, keepdims=True) * (0.5 / d), (1, LANES))

    return _call(body, name, (n_tiles,), [_row_spec(tm, d), _row_spec(tm, d)],
                 [pl.BlockSpec((1, LANES), lambda i: (0, 0)), _row_spec(tm, d)],
                 [_sds((1, LANES), F32), _sds((t, d), F32)], [pltpu.VMEM((1, d), F32)])(y, target)


def _adamw(g, w, m, v):
    m2 = ADAM_B1 * m + (1.0 - ADAM_B1) * g
    v2 = ADAM_B2 * v + (1.0 - ADAM_B2) * (g * g)
    m_hat = m2 / (1.0 - ADAM_B1 ** ADAM_STEP)
    v_hat = v2 / (1.0 - ADAM_B2 ** ADAM_STEP)
    delta = -ADAM_LR * (m_hat / (jnp.sqrt(v_hat) + ADAM_EPS) + ADAM_WD * w)
    return delta, m2, v2


ROW_TILE_CAP = 512


def _row_tile(rows, cap=ROW_TILE_CAP):
    if rows <= cap:
        return rows
    for tr in range(cap, 15, -16):
        if rows % tr == 0:
            return tr
    return rows


def _sum8_adamw(name, dev, lands, parts, w, m, v):
    nl = len(lands)
    _, r, c = lands[0].shape
    tr = _row_tile(r, cap=128)

    def body(dev_ref, *refs):
        land, own = refs[:nl], refs[nl:2 * nl]
        w_ref, m_ref, v_ref, g_ref, d_ref, m2_ref, v2_ref = refs[2 * nl:]
        layer, me = pl.program_id(0), dev_ref[0]
        for l in range(nl):
            @pl.when(layer == l)
            def _(l=l):
                g = None
                for s in range(N_DEV):
                    part = jnp.where(me == s, own[l][...], land[l][s]).astype(F32)
                    g = part if g is None else g + part
                delta, m2, v2 = _adamw(g, w_ref[...], m_ref[...], v_ref[...])
                g_ref[...] = g
                d_ref[...] = delta
                m2_ref[...] = m2
                v2_ref[...] = v2

    def rows_of(l, a, i):
        return jnp.where(a == l, i, 0)

    spec = pl.BlockSpec((None, tr, c), lambda a, i, dev_ref: (a, i, 0))
    in_specs = [pl.BlockSpec((N_DEV, tr, c), lambda a, i, dev_ref, l=l: (0, rows_of(l, a, i), 0)) for l in range(nl)]
    in_specs += [pl.BlockSpec((None, tr, c), lambda a, i, dev_ref, l=l: (dev_ref[0], rows_of(l, a, i), 0))
                 for l in range(nl)]
    grid_spec = pltpu.PrefetchScalarGridSpec(
        num_scalar_prefetch=1, grid=(nl, r // tr), in_specs=in_specs + [spec] * 3, out_specs=[spec] * 4)
    return pl.pallas_call(
        body, name=name, grid_spec=grid_spec, out_shape=[_sds(w.shape, F32)] * 4,
        compiler_params=pltpu.CompilerParams(vmem_limit_bytes=VMEM_LIMIT))(dev, *lands, *parts, w, m, v)


def _sum8(name, parts):
    _, r, c = parts.shape
    tr = _row_tile(r)

    def body(p_ref, o_ref):
        acc = p_ref[0]
        for s in range(1, N_DEV):
            acc = acc + p_ref[s]
        o_ref[...] = acc

    return _call(body, name, (r // tr,), [pl.BlockSpec((N_DEV, tr, c), lambda i: (0, i, 0))],
                 pl.BlockSpec((tr, c), lambda i: (i, 0)), _sds((r, c), F32))(parts)


def _adamw_flat(name, g, w, m, v):
    r, c = g.shape
    tr = _row_tile(r)

    def body(g_ref, w_ref, m_ref, v_ref, d_ref, m2_ref, v2_ref):
        delta, m2, v2 = _adamw(g_ref[...], w_ref[...], m_ref[...], v_ref[...])
        d_ref[...] = delta
        m2_ref[...] = m2
        v2_ref[...] = v2

    spec = pl.BlockSpec((tr, c), lambda i: (i, 0))
    return _call(body, name, (r // tr,), [spec] * 4, [spec] * 3, [_sds((r, c), F32)] * 3)(g, w, m, v)


def _pack(arrs, row_multiple=SUBLANES):
    pieces, rows = [], 0
    for a in arrs:
        piece = a.reshape(-1, LANES)
        piece = jnp.pad(piece, ((0, (-piece.shape[0]) % SUBLANES), (0, 0)))
        pieces.append(piece)
        rows += piece.shape[0]
    if rows % row_multiple:
        pieces.append(jnp.zeros(((-rows) % row_multiple, LANES), pieces[0].dtype))
    return jnp.concatenate(pieces, axis=0)


def _unpack(buf, shapes, lead=0):
    out, pos = [], 0
    for shp in shapes:
        rows = math.prod(shp) // LANES
        piece = lax.slice_in_dim(buf, pos, pos + rows, axis=lead)
        out.append(piece.reshape(buf.shape[:lead] + tuple(shp)))
        pos += rows + (-rows) % SUBLANES
    return out


REPLICATED = ["conv_b_in", "conv_b_dw", "conv_ln_g", "conv_ln_b", "conv_b_out", "gmlp_w_s", "gmlp_b_s",
              "ffn_b_up", "ffn_b_dw", "ffn_b_down", "norm1_g", "norm1_b", "norm2_g", "norm2_b"]
SMALL_SHARDED = ["conv_w_dw", "gmlp_b_in", "gmlp_ln_g", "gmlp_ln_b", "gmlp_b_out", "ffn_w_dw"]
BIG = ["conv_w_in", "conv_w_out", "gmlp_w_in", "gmlp_w_out", "ffn_w_up", "ffn_w_down"]
WEIGHTS = ["conv_w_in", "conv_b_in", "conv_w_dw", "conv_b_dw", "conv_ln_g", "conv_ln_b", "conv_w_out", "conv_b_out",
           "gmlp_w_in", "gmlp_b_in", "gmlp_ln_g", "gmlp_ln_b", "gmlp_w_s", "gmlp_b_s", "gmlp_w_out", "gmlp_b_out",
           "ffn_w_up", "ffn_b_up", "ffn_w_dw", "ffn_b_dw", "ffn_w_down", "ffn_b_down",
           "norm1_g", "norm1_b", "norm2_g", "norm2_b"]


def _from_shards(g, lead_shape):
    nd = len(lead_shape)
    perm = tuple(range(1, nd + 1)) + (0, nd + 1)
    return g.transpose(perm).reshape(tuple(lead_shape) + (-1,))


def _to_shards(full, width):
    lead = full.shape[:-1]
    nd = len(lead)
    parts = full.reshape(lead + (N_DEV, width))
    return parts.transpose((nd,) + tuple(range(nd)) + (nd + 1,))


def _step(p):
    x_in, target_in = p["x"], p["loss_target"]
    bsz, seq, d = x_in.shape
    t = bsz * seq
    assert seq % TM_EW == 0 and TM_EW % CHUNK == 0 and TM_EW // SUBLANES >= CONV_K - 1
    x0 = _to_segments(x_in.reshape(t, d), TM_EW)
    target = _to_segments(target_in.reshape(t, d), TM_EW)
    n_conv, n_gmlp = p["conv_w_in"].shape[0], p["gmlp_w_in"].shape[0]
    fb = p["ffn_w_up"].shape[-1]
    nblk = N_DEV
    half = nblk // 2
    cw = p["conv_w_in"].shape[-1]
    tm = min(TM_MM, t)
    tm_ln = min(TM_LN, t)
    nt = t // tm
    dev = 4 * lax.axis_index("x") + 2 * lax.axis_index("y") + lax.axis_index("c")

    small_shapes = [p[n].shape for n in SMALL_SHARDED]
    small_src = _pack([p[n] for n in SMALL_SHARDED])[None]
    small_all = _all_gather("gather_small_weights", [small_src])[0][0]
    sm = _unpack(small_all, small_shapes, lead=1)
    w_src = []
    for i in range(DEPTH):
        mix = "conv" if i % 2 == 0 else "gmlp"
        w_src += [p[mix + "_w_in"][i // 2].astype(BF16), p[mix + "_w_out"][i // 2].astype(BF16),
                  p["ffn_w_up"][i].T.astype(BF16), p["ffn_w_down"][i].astype(BF16)]
    send_sems, recv_sems, w_land = _gather_start(
        "weights_gather_start", _place_own("weights_place_own", w_src, deps=[small_all]))
    W_IN, W_OUT, W_UP, W_DOWN = range(4)

    def wait_weight(i, k, after):
        return _gather_wait(f"l{i}_weights_wait{k}", w_land[4 * i + k], send_sems, recv_sems, 4 * i + k, after)
    conv_w_dw = _from_shards(sm[0], sm[0].shape[1:-1])
    gmlp_b_in = _from_shards(sm[1], sm[1].shape[1:-1])
    gmlp_ln_g = _from_shards(sm[2], sm[2].shape[1:-1])
    gmlp_ln_b = _from_shards(sm[3], sm[3].shape[1:-1])
    gmlp_b_out = _from_shards(sm[4], sm[4].shape[1:-1])
    ffn_w_dw = sm[5].transpose(1, 0, 2, 3)

    def rows3(a):
        return a.reshape(a.shape[0], 1, a.shape[-1])

    conv_b_in4 = p["conv_b_in"].reshape(n_conv, N_DEV, 1, cw)
    gmlp_b_in4 = gmlp_b_in.reshape(n_gmlp, N_DEV, 1, cw)
    ffn_b_up4 = p["ffn_b_up"].reshape(DEPTH, nblk, 1, fb)
    ffn_b_dw4 = p["ffn_b_dw"].reshape(DEPTH, nblk, 1, fb)
    conv_b_dw3, conv_ln_g3, conv_ln_b3 = rows3(p["conv_b_dw"]), rows3(p["conv_ln_g"]), rows3(p["conv_ln_b"])
    conv_b_out3, gmlp_b_out3, ffn_b_down3 = rows3(p["conv_b_out"]), rows3(gmlp_b_out), rows3(p["ffn_b_down"])
    gmlp_ln_g3, gmlp_ln_b3 = rows3(gmlp_ln_g), rows3(gmlp_ln_b)
    n1g3, n1b3, n2g3, n2b3 = rows3(p["norm1_g"]), rows3(p["norm1_b"]), rows3(p["norm2_g"]), rows3(p["norm2_b"])
    w_tile, bs_tile = _sgu_operands(p["gmlp_w_s"], p["gmlp_b_s"])

    def mm_in(name, xa, wg, l, bias4):
        return _matmul(name, xa, wg, "nn", grid=(nt, N_DEV),
                       a_spec=pl.BlockSpec((tm, d), lambda i, n: (i, 0)),
                       b_spec=pl.BlockSpec((None, d, cw), lambda i, n: (n, 0, 0)),
                       o_spec=pl.BlockSpec((tm, cw), lambda i, n: (i, n)), o_shape=(t, N_DEV * cw), o_dtype=F32,
                       bias=bias4, bias_spec=pl.BlockSpec((None, None, 1, cw), lambda i, n: (l, n, 0, 0)))

    def mm_out_dx(name, dy, w):
        return _matmul(name, dy, w, "nt", grid=(nt,),
                       a_spec=pl.BlockSpec((tm, d), lambda i: (i, 0)),
                       b_spec=pl.BlockSpec((d, d), lambda i: (0, 0)),
                       o_spec=pl.BlockSpec((tm, d), lambda i: (i, 0)), o_shape=(t, d), o_dtype=F32)

    def mm_out_dw(name, sa, dy):
        return _matmul(name, sa, dy, "tn", grid=(nt,), k_axis=0, nk=nt, acc_shape=(d, d),
                       a_spec=pl.BlockSpec((tm, d), lambda k: (k, 0)),
                       b_spec=pl.BlockSpec((tm, d), lambda k: (k, 0)),
                       o_spec=pl.BlockSpec((d, d), lambda k: (0, 0)), o_shape=(d, d), o_dtype=BF16)

    def mm_in_dx(name, dh, wg, res):
        return _matmul(name, dh, wg, "nt", grid=(nt, N_DEV), k_axis=1, nk=N_DEV, acc_shape=(tm, d),
                       a_spec=pl.BlockSpec((tm, cw), lambda i, n: (i, n)),
                       b_spec=pl.BlockSpec((None, d, cw), lambda i, n: (n, 0, 0)),
                       o_spec=pl.BlockSpec((tm, d), lambda i, n: (i, 0)), o_shape=(t, d), o_dtype=F32,
                       res=res, res_spec=pl.BlockSpec((tm, d), lambda i, n: (i, 0)), res_scale=ALPHA)

    def mm_in_dw(name, xa, dh):
        return _matmul(name, xa, dh, "tn", grid=(N_DEV, nt), k_axis=1, nk=nt, acc_shape=(d, cw),
                       a_spec=pl.BlockSpec((tm, d), lambda n, k: (k, 0)),
                       b_spec=pl.BlockSpec((tm, cw), lambda n, k: (k, n)),
                       o_spec=pl.BlockSpec((None, d, cw), lambda n, k: (n, 0, 0)),
                       o_shape=(N_DEV, d, cw), o_dtype=BF16)

    def mm_up(name, xa, w, l):
        return _matmul(name, xa, w, "nt", grid=(nt, nblk),
                       a_spec=pl.BlockSpec((tm, d), lambda i, n: (i, 0)),
                       b_spec=pl.BlockSpec((None, fb, d), lambda i, n: (n, 0, 0)),
                       o_spec=pl.BlockSpec((None, tm, fb), lambda i, n: (n, i, 0)), o_shape=(nblk, t, fb),
                       o_dtype=F32, bias=ffn_b_up4,
                       bias_spec=pl.BlockSpec((None, None, 1, fb), lambda i, n: (l, n, 0, 0)))

    def mm_down_da(name, dy, w):
        return _matmul(name, dy, w, "nt", grid=(nt, half),
                       a_spec=pl.BlockSpec((tm, d), lambda i, n: (i, 0)),
                       b_spec=pl.BlockSpec((None, fb, d), lambda i, n: (n, 0, 0)),
                       o_spec=pl.BlockSpec((None, tm, fb), lambda i, n: (n, i, 0)), o_shape=(half, t, fb),
                       o_dtype=F32)

    def mm_down_dw(name, a, dy):
        return _matmul(name, a, dy, "tn", grid=(half, nt), k_axis=1, nk=nt, acc_shape=(fb, d),
                       a_spec=pl.BlockSpec((None, tm, fb), lambda n, k: (n, k, 0)),
                       b_spec=pl.BlockSpec((tm, d), lambda n, k: (k, 0)),
                       o_spec=pl.BlockSpec((None, fb, d), lambda n, k: (n, 0, 0)),
                       o_shape=(half, fb, d), o_dtype=BF16)

    def mm_up_dx(name, dh_g, dh_v, w, res):
        part = _matmul(name + "_g", dh_g, w, "nn", grid=(nt, half), k_axis=1, nk=half, acc_shape=(tm, d),
                       a_spec=pl.BlockSpec((None, tm, fb), lambda i, n: (n, i, 0)),
                       b_spec=pl.BlockSpec((None, fb, d), lambda i, n: (n, 0, 0)),
                       o_spec=pl.BlockSpec((tm, d), lambda i, n: (i, 0)), o_shape=(t, d), o_dtype=F32,
                       res=res, res_spec=pl.BlockSpec((tm, d), lambda i, n: (i, 0)), res_scale=ALPHA)
        return _matmul(name + "_v", dh_v, w, "nn", grid=(nt, half), k_axis=1, nk=half, acc_shape=(tm, d),
                       a_spec=pl.BlockSpec((None, tm, fb), lambda i, n: (n, i, 0)),
                       b_spec=pl.BlockSpec((None, fb, d), lambda i, n: (n + half, 0, 0)),
                       o_spec=pl.BlockSpec((tm, d), lambda i, n: (i, 0)), o_shape=(t, d), o_dtype=F32,
                       res=part, res_spec=pl.BlockSpec((tm, d), lambda i, n: (i, 0)), res_scale=1.0)

    def mm_up_dw(name, xa, dh_half, shift, prev=None):
        return _matmul(name, dh_half, xa, "tn", grid=(half, nt), k_axis=1, nk=nt, acc_shape=(fb, d),
                       a_spec=pl.BlockSpec((None, tm, fb), lambda n, k: (n, k, 0)),
                       b_spec=pl.BlockSpec((tm, d), lambda n, k: (k, 0)),
                       o_spec=pl.BlockSpec((None, fb, d), lambda n, k: (n + shift, 0, 0)),
                       o_shape=(nblk, fb, d), o_dtype=BF16, prev=prev)

    saved = []
    xcur = x0
    for i in range(DEPTH):
        j = i // 2
        s = {"x": xcur}
        s["w_in"] = wait_weight(i, W_IN, xcur)
        if i % 2 == 0:
            s["h"] = mm_in(f"l{i}_conv_in", xcur, s["w_in"], j, conv_b_in4)
            s["u"] = _glu(f"l{i}_glu", s["h"])
            s["c"] = _dwconv31(f"l{i}_dwconv", s["u"], conv_w_dw, conv_b_dw3, j, seq)
            s["s"] = _ln_silu(f"l{i}_ln_silu", s["c"], conv_ln_g3, conv_ln_b3, j)
            b_out3 = conv_b_out3
        else:
            s["h"] = mm_in(f"l{i}_gmlp_in", xcur, s["w_in"], j, gmlp_b_in4)
            s["s"] = _sgu(f"l{i}_sgu", s["h"], gmlp_ln_g3, gmlp_ln_b3, w_tile, bs_tile, j)
            b_out3 = gmlp_b_out3
        s["w_out"] = wait_weight(i, W_OUT, s["s"]).reshape(d, d)
        s["x1"], s["xhat1"], s["rstd1"] = _matmul_ln(
            f"l{i}_mixer_out_norm1", s["s"], s["w_out"], xcur, b_out3, n1g3, n1b3, j, i,
            nk=1, a_block=(tm_ln, d), b_block=(d, d))
        s["w_up"] = wait_weight(i, W_UP, s["x1"])
        s["fh"] = mm_up(f"l{i}_ffn_up", s["x1"], s["w_up"], i)
        s["a"] = _ffn_act(f"l{i}_ffn_act", s["fh"], ffn_w_dw, ffn_b_dw4, i, seq)
        s["w_down"] = wait_weight(i, W_DOWN, s["a"]).reshape(half, fb, d)
        xcur, s["xhat2"], s["rstd2"] = _matmul_ln(
            f"l{i}_ffn_down_norm2", s["a"], s["w_down"], s["x1"], ffn_b_down3, n2g3, n2b3, i, i,
            nk=half, a_block=(None, tm_ln, fb), b_block=(None, fb, d))
        saved.append(s)

    loss_row, dx = _loss("loss", xcur, target)

    started = {n: [None] * p[n].shape[0] for n in BIG}
    tokens = []

    def send_grad(n, l, g):
        send_sem, recv_sem, parts, land, token = _scatter_start(f"grad_{n}{l}_scatter_start", g)
        started[n][l] = (send_sem, recv_sem, parts, land)
        tokens.append(token)

    def take_tokens():
        out = list(tokens)
        tokens.clear()
        return out

    gl = {n: [None] * p[n].shape[0] for n in REPLICATED + SMALL_SHARDED}
    for i in reversed(range(DEPTH)):
        j = i // 2
        s = saved[i]
        mix = "conv" if i % 2 == 0 else "gmlp"
        dr2, gl["norm2_g"][i], gl["norm2_b"][i], gl["ffn_b_down"][i] = _ln_res_bwd(
            f"l{i}_norm2_bwd", dx, s["xhat2"], s["rstd2"], n2g3, i, deps=take_tokens())
        da = mm_down_da(f"l{i}_ffn_down_da", dr2, s["w_down"])
        send_grad("ffn_w_down", i, mm_down_dw(f"l{i}_ffn_down_dw", s["a"], dr2).reshape(N_DEV, -1, d))
        dcg, dcv, dbg, dbv, dwg, dwv = _ffn_act_bwd(f"l{i}_ffn_act_bwd", da, s["fh"], ffn_w_dw, ffn_b_dw4, i, seq,
                                                    deps=take_tokens())
        gl["ffn_b_dw"][i] = jnp.concatenate([dbg, dbv], axis=0).reshape(1, nblk * fb)
        gl["ffn_w_dw"][i] = jnp.concatenate([dwg[:, :FFN_K], dwv[:, :FFN_K]], axis=0)
        dh_g, dbu_g = _ffn_conv_t(f"l{i}_ffn_conv_t_g", dcg, ffn_w_dw, i, 0, seq)
        dh_v, dbu_v = _ffn_conv_t(f"l{i}_ffn_conv_t_v", dcv, ffn_w_dw, i, half, seq)
        gl["ffn_b_up"][i] = jnp.concatenate([dbu_g, dbu_v], axis=0).reshape(1, nblk * fb)
        dx1 = mm_up_dx(f"l{i}_ffn_up_dx", dh_g, dh_v, s["w_up"], dr2)
        g_up = mm_up_dw(f"l{i}_ffn_up_dw_g", s["x1"], dh_g, 0)
        send_grad("ffn_w_up", i, mm_up_dw(f"l{i}_ffn_up_dw_v", s["x1"], dh_v, half, g_up))
        dr1, gl["norm1_g"][i], gl["norm1_b"][i], gl[mix + "_b_out"][j] = _ln_res_bwd(
            f"l{i}_norm1_bwd", dx1, s["xhat1"], s["rstd1"], n1g3, i, deps=take_tokens())
        ds = mm_out_dx(f"l{i}_{mix}_out_dx", dr1, s["w_out"])
        send_grad(mix + "_w_out", j, mm_out_dw(f"l{i}_{mix}_out_dw", s["s"], dr1).reshape(N_DEV, -1, d))
        if i % 2 == 0:
            dc, gl["conv_ln_g"][j], gl["conv_ln_b"][j], gl["conv_b_dw"][j] = _ln_silu_bwd(
                f"l{i}_ln_silu_bwd", ds, s["c"], conv_ln_g3, conv_ln_b3, j, deps=take_tokens())
            du, dwdw = _dwconv31_bwd(f"l{i}_dwconv_bwd", dc, s["u"], conv_w_dw, j, seq)
            gl["conv_w_dw"][j] = dwdw[:CONV_K]
            dh, gl["conv_b_in"][j] = _glu_bwd(f"l{i}_glu_bwd", du, s["h"])
        else:
            dh, gl["gmlp_b_in"][j], dwt, dbt, gl["gmlp_ln_g"][j], gl["gmlp_ln_b"][j] = _sgu_bwd(
                f"l{i}_sgu_bwd", ds, s["h"], gmlp_ln_g3, gmlp_ln_b3, w_tile, bs_tile, j, deps=take_tokens())
            gl["gmlp_w_s"][j], gl["gmlp_b_s"][j] = _sgu_param_grads(dwt, dbt)
        dx = mm_in_dx(f"l{i}_{mix}_in_dx", dh, s["w_in"], dr1)
        send_grad(mix + "_w_in", j, mm_in_dw(f"l{i}_{mix}_in_dw", s["x"], dh))
    grad_x = _from_segments(dx, TM_EW).reshape(bsz, seq, d)

    full_small = {n: jnp.stack(gl[n]).reshape(p[n].shape) for n in REPLICATED}
    shard_small = {}
    for n in SMALL_SHARDED:
        if n == "ffn_w_dw":
            shard_small[n] = jnp.stack(gl[n]).transpose(1, 0, 2, 3)
        else:
            width = p[n].shape[-1]
            lead = p[n].shape[:-1]
            shard_small[n] = _to_shards(jnp.stack(gl[n]).reshape(lead + (N_DEV * width,)), width)
    flat_shapes = [(1, LANES)] + [p[n].shape for n in REPLICATED] + [(N_DEV,) + p[n].shape for n in SMALL_SHARDED]
    flat_local = _pack([loss_row] + [full_small[n] for n in REPLICATED] + [shard_small[n] for n in SMALL_SHARDED],
                       row_multiple=ROW_TILE_CAP)

    small_parts = _all_gather("gather_small_grads", [flat_local[None]])[0][0]
    small_sum = _sum8("sum_small_grads", small_parts)
    summed = _unpack(small_sum, flat_shapes)
    loss = summed[0][0, 0]
    grads = dict(zip(REPLICATED, summed[1:1 + len(REPLICATED)]))
    for n, g in zip(SMALL_SHARDED, summed[1 + len(REPLICATED):]):
        grads[n] = lax.dynamic_index_in_dim(g, dev, axis=0, keepdims=False)

    flat_started = [st for n in BIG for st in started[n]]
    parts_done, lands_done = _scatter_wait("grads_scatter_wait", flat_started, small_sum)

    delta, new_m, new_v = {}, {}, {}
    dev1 = jnp.reshape(dev, (1,)).astype(jnp.int32)
    pos = 0
    for n in BIG:
        nl = p[n].shape[0]
        state = [p[n], p["m_" + n], p["v_" + n]]
        if n == "ffn_w_up":
            state = [a.transpose(0, 2, 1) for a in state]
        outs = _sum8_adamw(f"adamw_{n}", dev1, lands_done[pos:pos + nl], parts_done[pos:pos + nl], *state)
        if n == "ffn_w_up":
            outs = [a.transpose(0, 2, 1) for a in outs]
        grads[n], delta[n], new_m[n], new_v[n] = outs
        pos += nl
    small = REPLICATED + SMALL_SHARDED
    small_shp = [p[n].shape for n in small]
    d_s, m_s, v_s = _adamw_flat("adamw_small", _pack([grads[n] for n in small]), _pack([p[n] for n in small]),
                                _pack([p["m_" + n] for n in small]), _pack([p["v_" + n] for n in small]))
    for n, dd, mm, vv in zip(small, _unpack(d_s, small_shp), _unpack(m_s, small_shp), _unpack(v_s, small_shp)):
        delta[n], new_m[n], new_v[n] = dd, mm, vv

    return (loss, grad_x, *[grads[n] for n in WEIGHTS], *[delta[n] for n in WEIGHTS],
            *[new_m[n] for n in WEIGHTS], *[new_v[n] for n in WEIGHTS])


def kernel(x, conv_w_in, conv_b_in, conv_w_dw, conv_b_dw, conv_ln_g, conv_ln_b, conv_w_out, conv_b_out, gmlp_w_in, gmlp_b_in, gmlp_ln_g, gmlp_ln_b, gmlp_w_s, gmlp_b_s, gmlp_w_out, gmlp_b_out, ffn_w_up, ffn_b_up, ffn_w_dw, ffn_b_dw, ffn_w_down, ffn_b_down, norm1_g, norm1_b, norm2_g, norm2_b, loss_target, m_conv_w_in, m_conv_b_in, m_conv_w_dw, m_conv_b_dw, m_conv_ln_g, m_conv_ln_b, m_conv_w_out, m_conv_b_out, m_gmlp_w_in, m_gmlp_b_in, m_gmlp_ln_g, m_gmlp_ln_b, m_gmlp_w_s, m_gmlp_b_s, m_gmlp_w_out, m_gmlp_b_out, m_ffn_w_up, m_ffn_b_up, m_ffn_w_dw, m_ffn_b_dw, m_ffn_w_down, m_ffn_b_down, m_norm1_g, m_norm1_b, m_norm2_g, m_norm2_b, v_conv_w_in, v_conv_b_in, v_conv_w_dw, v_conv_b_dw, v_conv_ln_g, v_conv_ln_b, v_conv_w_out, v_conv_b_out, v_gmlp_w_in, v_gmlp_b_in, v_gmlp_ln_g, v_gmlp_ln_b, v_gmlp_w_s, v_gmlp_b_s, v_gmlp_w_out, v_gmlp_b_out, v_ffn_w_up, v_ffn_b_up, v_ffn_w_dw, v_ffn_b_dw, v_ffn_w_down, v_ffn_b_down, v_norm1_g, v_norm1_b, v_norm2_g, v_norm2_b):
    return _step(dict(locals()))
```

```python
import math

import jax
import jax.numpy as jnp
from jax import lax
from jax.experimental import pallas as pl
from jax.experimental.pallas import tpu as pltpu

F32 = jnp.float32
BF16 = jnp.bfloat16
MESH = pl.DeviceIdType.MESH

N_DEV = 8
DEPTH = 4
ALPHA = (2.0 * DEPTH) ** 0.25
LN_EPS = 1e-5
CONV_K = 31
FFN_K = 3
CHUNK = 128
GROUPS = 8
ADAM_LR = 0.001
ADAM_B1 = 0.9
ADAM_B2 = 0.999
ADAM_EPS = 1e-08
ADAM_WD = 0.01
ADAM_STEP = 10
INV_SQRT2 = 1.0 / math.sqrt(2.0)
INV_SQRT2PI = 1.0 / math.sqrt(2.0 * math.pi)

LANES = 128
SUBLANES = 8
VMEM_LIMIT = 48 * 1024 * 1024
TM_MM = 1024
TM_EW = 256


def _call(body, name, grid, in_specs, out_specs, out_shape, scratch=(), aliases=None, deps=()):
    deps = list(deps)
    in_specs = list(in_specs)
    n_in = len(in_specs)
    if deps:
        inner = body

        def body(*refs):
            return inner(*refs[:n_in], *refs[n_in + len(deps):])

        in_specs = in_specs + [pl.BlockSpec(memory_space=pl.ANY)] * len(deps)
    fn = pl.pallas_call(
        body, name=name, grid=grid, in_specs=in_specs, out_specs=out_specs, out_shape=out_shape,
        scratch_shapes=list(scratch), input_output_aliases=aliases or {},
        compiler_params=pltpu.CompilerParams(vmem_limit_bytes=VMEM_LIMIT))
    return lambda *args: fn(*args, *deps)


def _sds(shape, dtype):
    return jax.ShapeDtypeStruct(tuple(shape), dtype)


def _sigmoid(x):
    return 1.0 / (1.0 + jnp.exp(-x))


def _acc_rows(ref, val, first):
    @pl.when(first)
    def _():
        ref[...] = val

    @pl.when(jnp.logical_not(first))
    def _():
        ref[...] += val


def _colsum(v):
    return jnp.sum(v, axis=0, keepdims=True)


_DIMS = {"nn": ((1,), (0,)), "nt": ((1,), (1,)), "tn": ((0,), (0,))}


def _matmul(name, a, b, mode, *, grid, a_spec, b_spec, o_spec, o_shape, o_dtype, k_axis=None, nk=1,
            acc_shape=None, bias=None, bias_spec=None, res=None, res_spec=None, res_scale=1.0,
            prev=None):
    dims = (_DIMS[mode], ((), ()))
    has_bias, has_res, has_prev = bias is not None, res is not None, prev is not None

    def body(*refs):
        a_ref, b_ref = refs[0], refs[1]
        pos = 2
        bias_ref = res_ref = None
        if has_bias:
            bias_ref = refs[pos]
            pos += 1
        if has_res:
            res_ref = refs[pos]
            pos += 1
        if has_prev:
            pos += 1
        o_ref = refs[pos]
        acc_ref = refs[pos + 1] if nk > 1 else None
        p = lax.dot_general(a_ref[...].astype(BF16), b_ref[...].astype(BF16), dims, preferred_element_type=F32)

        def finish(acc):
            if has_bias:
                acc = acc + bias_ref[...]
            if has_res:
                acc = acc + res_scale * res_ref[...]
            o_ref[...] = acc.astype(o_dtype)

        if nk == 1:
            finish(p)
        else:
            k = pl.program_id(k_axis)

            @pl.when(k == 0)
            def _():
                acc_ref[...] = p

            @pl.when(k > 0)
            def _():
                acc_ref[...] += p

            @pl.when(k == nk - 1)
            def _():
                finish(acc_ref[...])

    ins, specs = [a, b], [a_spec, b_spec]
    if has_bias:
        ins.append(bias)
        specs.append(bias_spec)
    if has_res:
        ins.append(res)
        specs.append(res_spec)
    aliases = None
    if has_prev:
        aliases = {len(ins): 0}
        ins.append(prev)
        specs.append(pl.BlockSpec(memory_space=pl.ANY))
    scratch = [pltpu.VMEM(acc_shape, F32)] if nk > 1 else []
    return _call(body, name, grid, specs, o_spec, _sds(o_shape, o_dtype), scratch, aliases)(*ins)


TM_LN = 512


def _matmul_ln(name, a, b, x_res, bias3, g3, b3, l_bias, l_norm, *, nk, a_block, b_block):
    t, d = x_res.shape
    tm = min(TM_LN, t)

    def body(a_ref, b_ref, x_ref, bias_ref, g_ref, be_ref, o_ref, xh_ref, rs_ref, *acc):
        p = jnp.dot(a_ref[...].astype(BF16), b_ref[...], preferred_element_type=F32)

        def finish(y):
            xhat, rstd = _ln_stats(ALPHA * x_ref[...] + y + bias_ref[...])
            o_ref[...] = xhat * g_ref[...] + be_ref[...]
            xh_ref[...] = xhat
            rs_ref[...] = rstd

        if nk == 1:
            finish(p)
        else:
            k = pl.program_id(1)

            @pl.when(k == 0)
            def _():
                acc[0][...] = p

            @pl.when(k > 0)
            def _():
                acc[0][...] += p

            @pl.when(k == nk - 1)
            def _():
                finish(acc[0][...])

    if nk == 1:
        grid = (t // tm,)
        a_spec = pl.BlockSpec(a_block, lambda i: (i, 0))
        b_spec = pl.BlockSpec(b_block, lambda i: (0, 0))
    else:
        grid = (t // tm, nk)
        a_spec = pl.BlockSpec(a_block, lambda i, k: (k, i, 0))
        b_spec = pl.BlockSpec(b_block, lambda i, k: (k, 0, 0))
    row = pl.BlockSpec((tm, d), lambda i, *_: (i, 0))
    stat = pl.BlockSpec((tm, 1), lambda i, *_: (i, 0))

    def vec(l):
        return pl.BlockSpec((None, 1, d), lambda *_: (l, 0, 0))

    return _call(body, name, grid, [a_spec, b_spec, row, vec(l_bias), vec(l_norm), vec(l_norm)], [row, row, stat],
                 [_sds((t, d), F32), _sds((t, d), F32), _sds((t, 1), F32)],
                 [pltpu.VMEM((tm, d), F32)] if nk > 1 else [])(a, b, x_res, bias3, g3, b3)


def _mesh_pos():
    return lax.axis_index("x"), lax.axis_index("y"), lax.axis_index("c")


def _any_specs(n):
    return [pl.BlockSpec(memory_space=pl.ANY)] * n


def _all_gather(name, srcs):
    n = len(srcs)

    def body(*refs):
        src, out = refs[:n], refs[n:2 * n]
        send_sems, recv_sems, local_sems = refs[2 * n:]
        x, y, c = _mesh_pos()
        me, sibling = (x, y, c), (x, y, 1 - c)
        chips = [(1 - x, y), (x, 1 - y), (1 - x, 1 - y)]

        def slot(k, p):
            return out[k].at[:, 4 * p[0] + 2 * p[1] + p[2]]

        def copy(k, idx, block, to, s=None):
            return pltpu.make_async_remote_copy(
                src_ref=slot(k, block) if s is None else s, dst_ref=slot(k, block),
                send_sem=send_sems.at[k * 7 + idx], recv_sem=recv_sems.at[k * 7 + idx],
                device_id=to, device_id_type=MESH)

        local = [pltpu.make_async_copy(src[k], slot(k, me), local_sems.at[k]) for k in range(n)]
        for cp in local:
            cp.start()
        first = []
        for k in range(n):
            first.append(copy(k, 0, me, sibling, src[k]))
            for j, chip in enumerate(chips):
                first.append(copy(k, 1 + j, me, (*chip, c), src[k]))
        for cp in first:
            cp.start()
        passed = []
        for j, chip in enumerate(chips):
            for k in range(n):
                copy(k, 1 + j, (*chip, c), me).wait_recv()
                cp = copy(k, 4 + j, (*chip, c), sibling)
                cp.start()
                passed.append(cp)
        for k in range(n):
            copy(k, 0, sibling, me).wait_recv()
            for j, chip in enumerate(chips):
                copy(k, 4 + j, (*chip, 1 - c), me).wait_recv()
        for cp in first + passed:
            cp.wait_send()
        for cp in local:
            cp.wait()

    out_shape = [_sds((s.shape[0], N_DEV) + s.shape[1:], s.dtype) for s in srcs]
    return _call(body, name, (), [pl.BlockSpec(memory_space=pltpu.VMEM)] * n, _any_specs(n), out_shape,
                 [pltpu.SemaphoreType.DMA((7 * n,)), pltpu.SemaphoreType.DMA((7 * n,)),
                  pltpu.SemaphoreType.DMA((n,))])(*srcs)


HBM_SPEC = pl.BlockSpec(memory_space=pltpu.HBM)
SEM_SPEC = pl.BlockSpec(memory_space=pltpu.SEMAPHORE)
N_PEER = N_DEV - 1


def _split_call(body, name, in_specs, out_specs, out_shape, aliases):
    return pl.pallas_call(
        body, name=name, in_specs=in_specs, out_specs=out_specs, out_shape=out_shape, input_output_aliases=aliases,
        compiler_params=pltpu.CompilerParams(has_side_effects=pltpu.SideEffectType.DATAFLOW_SIDE_EFFECTING))


def _peers(x, y, c):
    return [(1 - x if q & 4 else x, 1 - y if q & 2 else y, 1 - c if q & 1 else c) for q in range(1, N_DEV)]


def _in_hbm(a):
    return pltpu.with_memory_space_constraint(a, pltpu.HBM)


def _place_own(name, srcs, deps=()):
    n = len(srcs)

    def body(*refs):
        src, out, sems = refs[:n], refs[n:2 * n], refs[2 * n]
        x, y, c = _mesh_pos()
        dev = 4 * x + 2 * y + c
        copies = [pltpu.make_async_copy(src[k], out[k].at[dev], sems.at[k]) for k in range(n)]
        for cp in copies:
            cp.start()
        for cp in copies:
            cp.wait()

    return _call(body, name, (), [pl.BlockSpec(memory_space=pltpu.VMEM)] * n, _any_specs(n),
                 [_sds((N_DEV,) + s.shape, s.dtype) for s in srcs], [pltpu.SemaphoreType.DMA((n,))],
                 deps=deps)(*srcs)


def _gather_start(name, lands):
    n = len(lands)

    def body(*refs):
        land, send_sems, recv_sems = refs[:n], refs[n], refs[n + 1]
        x, y, c = _mesh_pos()
        dev = 4 * x + 2 * y + c
        for k in range(n):
            for peer in _peers(x, y, c):
                pltpu.make_async_remote_copy(
                    src_ref=land[k].at[dev], dst_ref=land[k].at[dev], send_sem=send_sems.at[k],
                    recv_sem=recv_sems.at[k], device_id=peer, device_id_type=MESH).start()

    outs = _split_call(
        body, name, [HBM_SPEC] * n, [SEM_SPEC, SEM_SPEC] + [HBM_SPEC] * n,
        [pltpu.SemaphoreType.DMA((n,)), pltpu.SemaphoreType.DMA((n,))] + [pltpu.HBM(a.shape, a.dtype) for a in lands],
        {k: 2 + k for k in range(n)})(*[_in_hbm(a) for a in lands])
    return outs[0], outs[1], list(outs[2:])


def _wait_seven(src_ref, dst_ref, send_sem, recv_sem):
    cp = pltpu.make_async_remote_copy(
        src_ref=src_ref.at[pl.ds(0, N_PEER)], dst_ref=dst_ref.at[pl.ds(0, N_PEER)], send_sem=send_sem,
        recv_sem=recv_sem, device_id=_mesh_pos(), device_id_type=MESH)
    cp.wait_send()
    cp.wait_recv()


def _gather_wait(name, land, send_sems, recv_sems, k, after):
    def body(land_ref, send_ref, recv_ref, after_ref, out_ref):
        _wait_seven(land_ref, land_ref, send_ref.at[k], recv_ref.at[k])

    return _split_call(body, name, [HBM_SPEC, SEM_SPEC, SEM_SPEC, pl.BlockSpec(memory_space=pl.ANY)], HBM_SPEC,
                       pltpu.HBM(land.shape, land.dtype), {0: 0})(land, send_sems, recv_sems, after)


def _scatter_start(name, parts):
    def body(parts_ref, land_ref, send_sem, recv_sem, parts_out, land_out, token):
        x, y, c = _mesh_pos()
        dev = 4 * x + 2 * y + c
        for peer in _peers(x, y, c):
            pltpu.make_async_remote_copy(
                src_ref=parts_ref.at[4 * peer[0] + 2 * peer[1] + peer[2]], dst_ref=land_ref.at[dev],
                send_sem=send_sem, recv_sem=recv_sem, device_id=peer, device_id_type=MESH).start()
        token[...] = jnp.zeros_like(token)

    buf = pltpu.HBM(parts.shape, parts.dtype)
    return _split_call(
        body, name, [HBM_SPEC, HBM_SPEC],
        [SEM_SPEC, SEM_SPEC, HBM_SPEC, HBM_SPEC, pl.BlockSpec(memory_space=pltpu.VMEM)],
        [pltpu.SemaphoreType.DMA(()), pltpu.SemaphoreType.DMA(()), buf, buf, _sds((SUBLANES, LANES), F32)],
        {0: 2, 1: 3})(_in_hbm(parts), _in_hbm(lax.empty(parts.shape, parts.dtype)))


def _scatter_wait(name, started, after):
    n = len(started)

    def body(*refs):
        for k in range(n):
            send_sem, recv_sem, parts_ref, land_ref = refs[4 * k:4 * k + 4]
            _wait_seven(parts_ref, land_ref, send_sem, recv_sem)

    flat = [a for s in started for a in s]
    outs = _split_call(
        body, name, [SEM_SPEC, SEM_SPEC, HBM_SPEC, HBM_SPEC] * n + [pl.BlockSpec(memory_space=pl.ANY)],
        [HBM_SPEC, HBM_SPEC] * n, [pltpu.HBM(a.shape, a.dtype) for s in started for a in s[2:]],
        {4 * k + 2 + t: 2 * k + t for k in range(n) for t in range(2)})(*flat, after)
    return list(outs[0::2]), list(outs[1::2])


def _to_segments(a, tile):
    seg = tile // SUBLANES
    return a.reshape((a.shape[0] // tile, SUBLANES, seg) + a.shape[1:]).swapaxes(1, 2).reshape(a.shape)


def _from_segments(a, tile):
    seg = tile // SUBLANES
    return a.reshape((a.shape[0] // tile, seg, SUBLANES) + a.shape[1:]).swapaxes(1, 2).reshape(a.shape)


def _chunk(ref, q):
    return ref[q * SUBLANES:(q + 1) * SUBLANES, :]


def _fill_wrap_prev(x_ref, halo_ref, wrap_ref, n_wrap, n_halo, seg, keep):
    sub = lax.broadcasted_iota(jnp.int32, (SUBLANES, x_ref.shape[-1]), 0)
    for j in range(n_wrap):
        q = seg - n_wrap + j
        hq = q - (seg - n_halo)
        row = halo_ref[hq * SUBLANES + SUBLANES - 1:(hq + 1) * SUBLANES, :] * keep
        wrap_ref[j * SUBLANES:(j + 1) * SUBLANES, :] = jnp.where(sub == 0, row, pltpu.roll(_chunk(x_ref, q), 1, 0))


def _fill_wrap_next(x_ref, halo_ref, wrap_ref, n_wrap, keep):
    sub = lax.broadcasted_iota(jnp.int32, (SUBLANES, x_ref.shape[-1]), 0)
    for j in range(n_wrap):
        row = halo_ref[j * SUBLANES:j * SUBLANES + 1, :] * keep
        wrap_ref[j * SUBLANES:(j + 1) * SUBLANES, :] = jnp.where(
            sub == SUBLANES - 1, row, pltpu.roll(_chunk(x_ref, j), SUBLANES - 1, 0))


def _past(x_ref, wrap_ref, q, d, n_wrap):
    return _chunk(x_ref, q - d) if q >= d else _chunk(wrap_ref, q - d + n_wrap)


def _future(x_ref, wrap_ref, q, d, seg):
    return _chunk(x_ref, q + d) if q + d < seg else _chunk(wrap_ref, q + d - seg)


def _conv_fwd(x_ref, wrap_ref, w_ref, b_ref, out_ref, seg, k_taps):
    bias = jnp.broadcast_to(b_ref[...], (SUBLANES, x_ref.shape[-1]))
    for q in range(seg):
        acc = bias
        for k in range(k_taps):
            acc = acc + w_ref[k:k + 1, :] * _past(x_ref, wrap_ref, q, k_taps - 1 - k, k_taps - 1)
        out_ref[q * SUBLANES:(q + 1) * SUBLANES, :] = acc


def _conv_bwd_data(d_ref, wrap_ref, w_ref, out_ref, seg, k_taps):
    for q in range(seg):
        acc = None
        for k in range(k_taps):
            term = w_ref[k:k + 1, :] * _future(d_ref, wrap_ref, q, k_taps - 1 - k, seg)
            acc = term if acc is None else acc + term
        out_ref[q * SUBLANES:(q + 1) * SUBLANES, :] = acc


def _conv_bwd_taps(d_ref, x_ref, wrap_ref, dw_ref, seg, k_taps):
    for k in range(k_taps):
        part = None
        for q in range(seg):
            term = _chunk(d_ref, q) * _past(x_ref, wrap_ref, q, k_taps - 1 - k, k_taps - 1)
            part = term if part is None else part + term
        dw_ref[k:k + 1, :] += _colsum(part)


def _tile_halo_specs(tm, width_block, n_halo, n_tiles, block_of):
    rows = n_halo * SUBLANES
    per = tm // rows
    tile = pl.BlockSpec(width_block(tm), lambda n, i: block_of(n, i))
    prev = pl.BlockSpec(width_block(rows), lambda n, i: block_of(n, jnp.maximum(i * per - 1, 0)))
    nxt = pl.BlockSpec(width_block(rows), lambda n, i: block_of(n, jnp.minimum((i + 1) * per, n_tiles * per - 1)))
    return tile, prev, nxt


def _ln_stats(v):
    mu = jnp.mean(v, axis=-1, keepdims=True)
    vc = v - mu
    var = jnp.mean(vc * vc, axis=-1, keepdims=True)
    rstd = lax.rsqrt(var + LN_EPS)
    return vc * rstd, rstd


def _ln_backward(dxhat, xhat, rstd):
    m1 = jnp.mean(dxhat, axis=-1, keepdims=True)
    m2 = jnp.mean(dxhat * xhat, axis=-1, keepdims=True)
    return rstd * (dxhat - m1 - xhat * m2)


def _row_spec(tm, width):
    return pl.BlockSpec((tm, width), lambda i: (i, 0))


def _param_spec(l, width):
    return pl.BlockSpec((None, 1, width), lambda *_: (l, 0, 0))


def _ln_res_bwd(name, dout, xhat, rstd, g3, l, deps=()):
    t, d = dout.shape
    tm = min(TM_EW, t)

    def body(do_ref, xh_ref, rs_ref, g_ref, dr_ref, dg_ref, db_ref, dc_ref):
        first = pl.program_id(0) == 0
        do, xhat = do_ref[...], xh_ref[...]
        dr = _ln_backward(do * g_ref[...], xhat, rs_ref[...])
        dr_ref[...] = dr
        _acc_rows(dg_ref, _colsum(do * xhat), first)
        _acc_rows(db_ref, _colsum(do), first)
        _acc_rows(dc_ref, _colsum(dr), first)

    vec = pl.BlockSpec((1, d), lambda i: (0, 0))
    return _call(body, name, (t // tm,),
                 [_row_spec(tm, d), _row_spec(tm, d), _row_spec(tm, 1), _param_spec(l, d)],
                 [_row_spec(tm, d), vec, vec, vec],
                 [_sds((t, d), F32)] + [_sds((1, d), F32)] * 3, deps=deps)(dout, xhat, rstd, g3)


def _glu(name, h):
    t, c2 = h.shape
    c = c2 // 2
    tm = min(TM_EW, t)

    def body(a_ref, g_ref, o_ref):
        o_ref[...] = a_ref[...] * _sigmoid(g_ref[...])

    return _call(body, name, (t // tm,),
                 [pl.BlockSpec((tm, c), lambda i: (i, 0)), pl.BlockSpec((tm, c), lambda i: (i, 1))],
                 _row_spec(tm, c), _sds((t, c), F32))(h, h)


def _glu_bwd(name, du, h):
    t, c2 = h.shape
    c = c2 // 2
    tm = min(TM_EW, t)

    def body(du_ref, a_ref, g_ref, dh_ref, db_ref):
        first = pl.program_id(0) == 0
        du_v, a = du_ref[...], a_ref[...]
        sg = _sigmoid(g_ref[...])
        da = du_v * sg
        dg = du_v * a * sg * (1.0 - sg)
        dh_ref[:, :c] = da.astype(BF16)
        dh_ref[:, c:] = dg.astype(BF16)
        _acc_rows(db_ref.at[:, :c], _colsum(da), first)
        _acc_rows(db_ref.at[:, c:], _colsum(dg), first)

    return _call(body, name, (t // tm,),
                 [_row_spec(tm, c), pl.BlockSpec((tm, c), lambda i: (i, 0)), pl.BlockSpec((tm, c), lambda i: (i, 1))],
                 [_row_spec(tm, c2), pl.BlockSpec((1, c2), lambda i: (0, 0))],
                 [_sds((t, c2), BF16), _sds((1, c2), F32)])(du, h, h)


CONV_CB = 256
TAPS_PAD = 32


def _dwconv31(name, u, w3, b3, l, seq):
    t, c = u.shape
    tm, cb = TM_EW, CONV_CB
    seg, seq_tiles, n_tiles = tm // SUBLANES, seq // tm, t // tm
    n_wrap = CONV_K - 1
    tile, prev, _ = _tile_halo_specs(tm, lambda rows: (rows, cb), seg, n_tiles, lambda n, r: (r, n))

    def body(u_ref, halo_ref, w_ref, b_ref, o_ref, wrap_ref):
        keep = (pl.program_id(1) % seq_tiles != 0).astype(F32)
        _fill_wrap_prev(u_ref, halo_ref, wrap_ref, n_wrap, seg, seg, keep)
        _conv_fwd(u_ref, wrap_ref, w_ref, b_ref, o_ref, seg, CONV_K)

    return _call(body, name, (c // cb, n_tiles),
                 [tile, prev, pl.BlockSpec((None, CONV_K, cb), lambda n, i: (l, 0, n)),
                  pl.BlockSpec((None, 1, cb), lambda n, i: (l, 0, n))],
                 tile, _sds((t, c), F32), [pltpu.VMEM((n_wrap * SUBLANES, cb), F32)])(u, u, w3, b3)


def _dwconv31_bwd(name, dc, u, w3, l, seq):
    t, c = dc.shape
    tm, cb = TM_EW, CONV_CB
    seg, seq_tiles, n_tiles = tm // SUBLANES, seq // tm, t // tm
    n_wrap = CONV_K - 1
    tile, prev, nxt = _tile_halo_specs(tm, lambda rows: (rows, cb), seg, n_tiles, lambda n, r: (r, n))

    def body(dc_ref, dcn_ref, u_ref, up_ref, w_ref, du_ref, dw_ref, dwrap_ref, uwrap_ref):
        i = pl.program_id(1)
        keep_prev = (i % seq_tiles != 0).astype(F32)
        keep_next = (i % seq_tiles != seq_tiles - 1).astype(F32)
        _fill_wrap_next(dc_ref, dcn_ref, dwrap_ref, n_wrap, keep_next)
        _conv_bwd_data(dc_ref, dwrap_ref, w_ref, du_ref, seg, CONV_K)

        @pl.when(i == 0)
        def _():
            dw_ref[...] = jnp.zeros_like(dw_ref)

        _fill_wrap_prev(u_ref, up_ref, uwrap_ref, n_wrap, seg, seg, keep_prev)
        _conv_bwd_taps(dc_ref, u_ref, uwrap_ref, dw_ref, seg, CONV_K)

    wrap = pltpu.VMEM((n_wrap * SUBLANES, cb), F32)
    return _call(body, name, (c // cb, n_tiles),
                 [tile, nxt, tile, prev, pl.BlockSpec((None, CONV_K, cb), lambda n, i: (l, 0, n))],
                 [tile, pl.BlockSpec((TAPS_PAD, cb), lambda n, i: (0, n))],
                 [_sds((t, c), F32), _sds((TAPS_PAD, c), F32)], [wrap, wrap])(dc, dc, u, u, w3)


def _ln_silu(name, cx, g3, b3, l):
    t, d = cx.shape
    tm = min(TM_EW, t)

    def body(c_ref, g_ref, b_ref, o_ref):
        xhat, _ = _ln_stats(c_ref[...])
        nv = xhat * g_ref[...] + b_ref[...]
        o_ref[...] = (nv * _sigmoid(nv)).astype(BF16)

    return _call(body, name, (t // tm,), [_row_spec(tm, d), _param_spec(l, d), _param_spec(l, d)],
                 _row_spec(tm, d), _sds((t, d), BF16))(cx, g3, b3)


def _ln_silu_bwd(name, ds, cx, g3, b3, l, deps=()):
    t, d = cx.shape
    tm = min(TM_EW, t)

    def body(ds_ref, c_ref, g_ref, b_ref, dc_ref, dg_ref, db_ref, dsum_ref):
        first = pl.program_id(0) == 0
        xhat, rstd = _ln_stats(c_ref[...])
        g = g_ref[...]
        nv = xhat * g + b_ref[...]
        sg = _sigmoid(nv)
        dn = ds_ref[...] * (sg * (1.0 + nv * (1.0 - sg)))
        dc = _ln_backward(dn * g, xhat, rstd)
        dc_ref[...] = dc
        _acc_rows(dg_ref, _colsum(dn * xhat), first)
        _acc_rows(db_ref, _colsum(dn), first)
        _acc_rows(dsum_ref, _colsum(dc), first)

    vec = pl.BlockSpec((1, d), lambda i: (0, 0))
    return _call(body, name, (t // tm,),
                 [_row_spec(tm, d), _row_spec(tm, d), _param_spec(l, d), _param_spec(l, d)],
                 [_row_spec(tm, d), vec, vec, vec],
                 [_sds((t, d), F32)] + [_sds((1, d), F32)] * 3, deps=deps)(ds, cx, g3, b3)


FFN_HALO = FFN_K - 1


def _ffn_conv(x_ref, halo_ref, wrap_ref, w_ref, b_ref, keep, seg, out_ref):
    _fill_wrap_prev(x_ref, halo_ref, wrap_ref, FFN_K - 1, FFN_HALO, seg, keep)
    _conv_fwd(x_ref, wrap_ref, w_ref, b_ref, out_ref, seg, FFN_K)


TM_FFN = 512
WRAP_ROWS = FFN_HALO * SUBLANES


def _sub_tiles(x_ref, prev_ref, next_ref, keep_prev, keep_next, n_sub):
    out = []
    for s in range(n_sub):
        tile = x_ref.at[pl.ds(s * TM_EW, TM_EW)]
        prev = prev_ref if s == 0 else x_ref.at[pl.ds(s * TM_EW - WRAP_ROWS, WRAP_ROWS)]
        nxt = next_ref if s == n_sub - 1 else x_ref.at[pl.ds((s + 1) * TM_EW, WRAP_ROWS)]
        out.append((tile, prev, keep_prev if s == 0 else 1.0, nxt, keep_next if s == n_sub - 1 else 1.0))
    return out


def _rows(ref, s, rows):
    return ref.at[pl.ds(s * rows, rows)]


def _ffn_specs(tm, fb, n_tiles):
    return _tile_halo_specs(tm, lambda rows: (None, rows, fb), FFN_HALO, n_tiles, lambda n, r: (n, r, 0))


def _ffn_up_act(name, x, w_up, b_up4, wdw, bdw, l, seq):
    t, d = x.shape
    nb, fb, _ = w_up.shape
    half = nb // 2
    tm = min(TM_FFN, seq)
    n_sub, seg, seq_steps, n_steps = tm // TM_EW, TM_EW // SUBLANES, seq // tm, t // tm
    per = tm // WRAP_ROWS
    nt_dims = (_DIMS["nt"], ((), ()))

    def body(x_ref, xp_ref, ug_ref, uv_ref, bug_ref, buv_ref, wg_ref, wv_ref, bg_ref, bv_ref,
             hg_ref, hv_ref, a_ref, pg_ref, pv_ref, wrap_ref, cg_ref, cv_ref):
        keep = (pl.program_id(1) % seq_steps != 0).astype(F32)
        xb, xpb = x_ref[...].astype(BF16), xp_ref[...].astype(BF16)
        hg_ref[...] = lax.dot_general(xb, ug_ref[...], nt_dims, preferred_element_type=F32) + bug_ref[...]
        pg_ref[...] = lax.dot_general(xpb, ug_ref[...], nt_dims, preferred_element_type=F32) + bug_ref[...]
        hv_ref[...] = lax.dot_general(xb, uv_ref[...], nt_dims, preferred_element_type=F32) + buv_ref[...]
        pv_ref[...] = lax.dot_general(xpb, uv_ref[...], nt_dims, preferred_element_type=F32) + buv_ref[...]
        for s, (tile, prev, kp, _, _) in enumerate(_sub_tiles(hg_ref, pg_ref, None, keep, None, n_sub)):
            _ffn_conv(tile, prev, wrap_ref, wg_ref, bg_ref, kp, seg, _rows(cg_ref, s, TM_EW))
        for s, (tile, prev, kp, _, _) in enumerate(_sub_tiles(hv_ref, pv_ref, None, keep, None, n_sub)):
            _ffn_conv(tile, prev, wrap_ref, wv_ref, bv_ref, kp, seg, _rows(cv_ref, s, TM_EW))
        cg = cg_ref[...]
        a_ref[...] = (cg * _sigmoid(cg) * cv_ref[...]).astype(BF16)

    def blk(shift):
        return pl.BlockSpec((None, fb, d), lambda n, i: (n + shift, 0, 0))

    def vec(shift, rows):
        return pl.BlockSpec((None, None, rows, fb), lambda n, i: (l, n + shift, 0, 0))

    out = pl.BlockSpec((None, tm, fb), lambda n, i: (n, i, 0))
    tmp = pltpu.VMEM((tm, fb), F32)
    halo = pltpu.VMEM((WRAP_ROWS, fb), F32)
    return _call(body, name, (half, n_steps),
                 [pl.BlockSpec((tm, d), lambda n, i: (i, 0)),
                  pl.BlockSpec((WRAP_ROWS, d), lambda n, i: (jnp.maximum(i * per - 1, 0), 0)),
                  blk(0), blk(half), vec(0, 1), vec(half, 1), vec(0, FFN_K), vec(half, FFN_K), vec(0, 1), vec(half, 1)],
                 [out, out, out],
                 [_sds((half, t, fb), F32), _sds((half, t, fb), F32), _sds((half, t, fb), BF16)],
                 [halo, halo, halo, tmp, tmp])(x, x, w_up, w_up, b_up4, b_up4, wdw, wdw, bdw, bdw)


def _ffn_act_bwd(name, dy, w_down, hg, hv, wdw, bdw, l, seq, deps=()):
    half, t, fb = hg.shape
    d = dy.shape[-1]
    tm = min(TM_FFN, seq)
    n_sub, seg, seq_steps, n_steps = tm // TM_EW, TM_EW // SUBLANES, seq // tm, t // tm
    tile, prev, _ = _ffn_specs(tm, fb, n_steps)

    def body(dy_ref, wd_ref, g_ref, gp_ref, v_ref, vp_ref, wg_ref, wv_ref, bg_ref, bv_ref,
             dg_ref, dv_ref, dbg_ref, dbv_ref, dwg_ref, dwv_ref, gwrap_ref, vwrap_ref, cg_ref, cv_ref):
        i = pl.program_id(1)
        first = i == 0
        keep = (i % seq_steps != 0).astype(F32)
        da = lax.dot_general(dy_ref[...].astype(BF16), wd_ref[...], (_DIMS["nt"], ((), ())),
                             preferred_element_type=F32)
        g_tiles = _sub_tiles(g_ref, gp_ref, None, keep, None, n_sub)
        v_tiles = _sub_tiles(v_ref, vp_ref, None, keep, None, n_sub)
        for s in range(n_sub):
            _ffn_conv(g_tiles[s][0], g_tiles[s][1], _rows(gwrap_ref, s, WRAP_ROWS), wg_ref, bg_ref, g_tiles[s][2],
                      seg, _rows(cg_ref, s, TM_EW))
            _ffn_conv(v_tiles[s][0], v_tiles[s][1], _rows(vwrap_ref, s, WRAP_ROWS), wv_ref, bv_ref, v_tiles[s][2],
                      seg, _rows(cv_ref, s, TM_EW))
        cg, cv = cg_ref[...], cv_ref[...]
        sg = _sigmoid(cg)
        dcv = da * cg * sg
        dcg = da * cv * sg * (1.0 + cg * (1.0 - sg))
        dg_ref[...] = dcg
        dv_ref[...] = dcv
        _acc_rows(dbg_ref, _colsum(dcg), first)
        _acc_rows(dbv_ref, _colsum(dcv), first)

        @pl.when(first)
        def _():
            dwg_ref[...] = jnp.zeros_like(dwg_ref)
            dwv_ref[...] = jnp.zeros_like(dwv_ref)

        for s in range(n_sub):
            _conv_bwd_taps(_rows(dg_ref, s, TM_EW), g_tiles[s][0], _rows(gwrap_ref, s, WRAP_ROWS), dwg_ref, seg, FFN_K)
            _conv_bwd_taps(_rows(dv_ref, s, TM_EW), v_tiles[s][0], _rows(vwrap_ref, s, WRAP_ROWS), dwv_ref, seg, FFN_K)

    def vec(shift, rows):
        return pl.BlockSpec((None, None, rows, fb), lambda n, i: (l, n + shift, 0, 0))

    def acc(rows):
        return pl.BlockSpec((None, rows, fb), lambda n, i: (n, 0, 0))

    wrap = pltpu.VMEM((n_sub * WRAP_ROWS, fb), F32)
    tmp = pltpu.VMEM((tm, fb), F32)
    return _call(body, name, (half, n_steps),
                 [pl.BlockSpec((tm, d), lambda n, i: (i, 0)), pl.BlockSpec((None, fb, d), lambda n, i: (n, 0, 0)),
                  tile, prev, tile, prev, vec(0, FFN_K), vec(half, FFN_K), vec(0, 1), vec(half, 1)],
                 [tile, tile, acc(1), acc(1), acc(SUBLANES), acc(SUBLANES)],
                 [_sds((half, t, fb), F32), _sds((half, t, fb), F32), _sds((half, 1, fb), F32),
                  _sds((half, 1, fb), F32), _sds((half, SUBLANES, fb), F32), _sds((half, SUBLANES, fb), F32)],
                 [wrap, wrap, tmp, tmp], deps=deps)(dy, w_down, hg, hg, hv, hv, wdw, wdw, bdw, bdw)


def _ffn_conv_t(name, dhc, wdw, l, shift, seq):
    half, t, fb = dhc.shape
    tm = min(TM_FFN, seq)
    n_sub, seg, seq_steps, n_steps = tm // TM_EW, TM_EW // SUBLANES, seq // tm, t // tm
    tile, _, nxt = _ffn_specs(tm, fb, n_steps)

    def body(d_ref, dn_ref, w_ref, dh_ref, db_ref, wrap_ref, out_ref):
        i = pl.program_id(1)
        keep = (i % seq_steps != seq_steps - 1).astype(F32)
        for s, (sub, _, _, nx, kn) in enumerate(_sub_tiles(d_ref, None, dn_ref, None, keep, n_sub)):
            _fill_wrap_next(sub, nx, wrap_ref, FFN_K - 1, kn)
            _conv_bwd_data(sub, wrap_ref, w_ref, _rows(out_ref, s, TM_EW), seg, FFN_K)
        dh = out_ref[...]
        dh_ref[...] = dh.astype(BF16)
        _acc_rows(db_ref, _colsum(dh), i == 0)

    return _call(body, name, (half, n_steps),
                 [tile, nxt, pl.BlockSpec((None, None, FFN_K, fb), lambda n, i: (l, n + shift, 0, 0))],
                 [tile, pl.BlockSpec((None, 1, fb), lambda n, i: (n, 0, 0))],
                 [_sds((half, t, fb), BF16), _sds((half, 1, fb), F32)],
                 [pltpu.VMEM((WRAP_ROWS, fb), F32), pltpu.VMEM((tm, fb), F32)])(dhc, dhc, wdw)


def _gelu_parts(h):
    cdf = 0.5 * (1.0 + lax.erf(h * INV_SQRT2))
    return h * cdf, cdf


def _seg_axis(a, axis, fn):
    return jnp.moveaxis(fn(jnp.moveaxis(a, axis, 0), TM_EW), 0, axis)


def _sgu_operands(w_s, b_s):
    nl = w_s.shape[0]
    n_sub = TM_EW // CHUNK
    tril = jnp.tril(jnp.ones((CHUNK, CHUNK), dtype=bool))
    w_causal = jnp.where(tril, w_s, 0.0)
    w_tile = (jnp.eye(n_sub, dtype=F32)[None, None, :, None, :, None] * w_causal[:, :, None, :, None, :]).reshape(
        nl, GROUPS, TM_EW, TM_EW)
    w_tile = _seg_axis(_seg_axis(w_tile, 2, _to_segments), 3, _to_segments).astype(BF16)
    bs_tile = jnp.broadcast_to(b_s[:, :, None, :, None], (nl, GROUPS, n_sub, CHUNK, CHUNK)).reshape(
        nl, GROUPS, TM_EW, CHUNK)
    return w_tile, _seg_axis(bs_tile, 2, _to_segments)


def _sgu_param_grads(dwt, dbt):
    n_sub = TM_EW // CHUNK
    tril = jnp.tril(jnp.ones((CHUNK, CHUNK), dtype=bool))
    dwt = _seg_axis(_seg_axis(dwt, 1, _from_segments), 2, _from_segments).reshape(GROUPS, n_sub, CHUNK, n_sub, CHUNK)
    dw = sum(dwt[:, a, :, a, :] for a in range(n_sub))
    db = _seg_axis(dbt, 1, _from_segments).reshape(GROUPS, n_sub, CHUNK).sum(axis=1)
    return jnp.where(tril, dw, 0.0), db


def _sgu(name, h, g3, b3, wt, bst, l):
    t, c2 = h.shape
    c = c2 // 2
    tm = TM_EW

    def body(h_ref, g_ref, b_ref, wt_ref, bs_ref, o_ref):
        z, _ = _gelu_parts(h_ref[...])
        u = z[:, :c]
        xhat, _ = _ln_stats(z[:, c:])
        vnb = (xhat * g_ref[...] + b_ref[...]).astype(BF16)
        for gi in range(GROUPS):
            cs = slice(gi * CHUNK, (gi + 1) * CHUNK)
            sp = jnp.dot(wt_ref[gi], vnb[:, cs], preferred_element_type=F32) + bs_ref[gi]
            o_ref[:, cs] = (u[:, cs] * sp).astype(BF16)

    return _call(body, name, (t // tm,),
                 [_row_spec(tm, c2), _param_spec(l, c), _param_spec(l, c),
                  pl.BlockSpec((None, GROUPS, tm, tm), lambda i: (l, 0, 0, 0)),
                  pl.BlockSpec((None, GROUPS, tm, CHUNK), lambda i: (l, 0, 0, 0))],
                 _row_spec(tm, c), _sds((t, c), BF16))(h, g3, b3, wt, bst)


def _sgu_bwd(name, dq, h, g3, b3, wt, bst, l, deps=()):
    t, c2 = h.shape
    c = c2 // 2
    tm = TM_EW
    n_tiles = t // tm

    def body(dq_ref, h_ref, g_ref, b_ref, wt_ref, bs_ref,
             dh_ref, dbin_ref, dw_ref, dbs_ref, dg_ref, db_ref, du_ref, dvn_ref, bsum_ref):
        i = pl.program_id(0)
        first = i == 0
        hv = h_ref[...]
        z, cdf = _gelu_parts(hv)
        u = z[:, :c]
        xhat, rstd = _ln_stats(z[:, c:])
        g = g_ref[...]
        vnb = (xhat * g + b_ref[...]).astype(BF16)

        @pl.when(first)
        def _():
            dw_ref[...] = jnp.zeros_like(dw_ref)
            bsum_ref[...] = jnp.zeros_like(bsum_ref)

        for gi in range(GROUPS):
            cs = slice(gi * CHUNK, (gi + 1) * CHUNK)
            vb = vnb[:, cs]
            w = wt_ref[gi]
            sp = jnp.dot(w, vb, preferred_element_type=F32) + bs_ref[gi]
            dqb = dq_ref[:, cs]
            du_ref[:, cs] = dqb * sp
            dsp = dqb * u[:, cs]
            bsum_ref[gi] += dsp
            dspb = dsp.astype(BF16)
            dw_ref[gi] += lax.dot_general(dspb, vb, (_DIMS["nt"], ((), ())), preferred_element_type=F32)
            dvn_ref[:, cs] = lax.dot_general(w, dspb, (_DIMS["tn"], ((), ())), preferred_element_type=F32)

        dvn = dvn_ref[...]
        dv = _ln_backward(dvn * g, xhat, rstd)
        pdf = jnp.exp(-0.5 * hv * hv) * INV_SQRT2PI
        dgelu = cdf + hv * pdf
        dhu = du_ref[...] * dgelu[:, :c]
        dhv = dv * dgelu[:, c:]
        dh_ref[:, :c] = dhu.astype(BF16)
        dh_ref[:, c:] = dhv.astype(BF16)
        _acc_rows(dbin_ref.at[:, :c], _colsum(dhu), first)
        _acc_rows(dbin_ref.at[:, c:], _colsum(dhv), first)
        _acc_rows(dg_ref, _colsum(dvn * xhat), first)
        _acc_rows(db_ref, _colsum(dvn), first)

        @pl.when(i == n_tiles - 1)
        def _():
            dbs_ref[...] = jnp.sum(bsum_ref[...], axis=-1)

    vec = pl.BlockSpec((1, c), lambda i: (0, 0))
    return _call(body, name, (n_tiles,),
                 [_row_spec(tm, c), _row_spec(tm, c2), _param_spec(l, c), _param_spec(l, c),
                  pl.BlockSpec((None, GROUPS, tm, tm), lambda i: (l, 0, 0, 0)),
                  pl.BlockSpec((None, GROUPS, tm, CHUNK), lambda i: (l, 0, 0, 0))],
                 [_row_spec(tm, c2), pl.BlockSpec((1, c2), lambda i: (0, 0)),
                  pl.BlockSpec((GROUPS, tm, tm), lambda i: (0, 0, 0)),
                  pl.BlockSpec((GROUPS, tm), lambda i: (0, 0)), vec, vec],
                 [_sds((t, c2), BF16), _sds((1, c2), F32), _sds((GROUPS, tm, tm), F32),
                  _sds((GROUPS, tm), F32), _sds((1, c), F32), _sds((1, c), F32)],
                 [pltpu.VMEM((tm, c), F32), pltpu.VMEM((tm, c), F32), pltpu.VMEM((GROUPS, tm, CHUNK), F32)],
                 deps=deps)(dq, h, g3, b3, wt, bst)


def _loss(name, y, target):
    t, d = y.shape
    tm = min(TM_EW, t)
    n_tiles = t // tm

    def body(y_ref, t_ref, l_ref, dy_ref, acc_ref):
        i = pl.program_id(0)
        diff = y_ref[...] - t_ref[...]
        dy_ref[...] = diff * (1.0 / d)
        _acc_rows(acc_ref, _colsum(diff * diff), i == 0)

        @pl.when(i == n_tiles - 1)
        def _():
            l_ref[...] = jnp.broadcast_to(jnp.sum(acc_ref[...], axis=-1, keepdims=True) * (0.5 / d), (1, LANES))

    return _call(body, name, (n_tiles,), [_row_spec(tm, d), _row_spec(tm, d)],
                 [pl.BlockSpec((1, LANES), lambda i: (0, 0)), _row_spec(tm, d)],
                 [_sds((1, LANES), F32), _sds((t, d), F32)], [pltpu.VMEM((1, d), F32)])(y, target)


def _adamw(g, w, m, v):
    m2 = ADAM_B1 * m + (1.0 - ADAM_B1) * g
    v2 = ADAM_B2 * v + (1.0 - ADAM_B2) * (g * g)
    m_hat = m2 / (1.0 - ADAM_B1 ** ADAM_STEP)
    v_hat = v2 / (1.0 - ADAM_B2 ** ADAM_STEP)
    delta = -ADAM_LR * (m_hat / (jnp.sqrt(v_hat) + ADAM_EPS) + ADAM_WD * w)
    return delta, m2, v2


ROW_TILE_CAP = 512


def _row_tile(rows, cap=ROW_TILE_CAP):
    if rows <= cap:
        return rows
    for tr in range(cap, 15, -16):
        if rows % tr == 0:
            return tr
    return rows


def _sum8_adamw(name, dev, lands, parts, w, m, v):
    nl = len(lands)
    _, r, c = lands[0].shape
    tr = _row_tile(r, cap=128)

    def body(dev_ref, *refs):
        land, own = refs[:nl], refs[nl:2 * nl]
        w_ref, m_ref, v_ref, g_ref, d_ref, m2_ref, v2_ref = refs[2 * nl:]
        layer, me = pl.program_id(0), dev_ref[0]
        for l in range(nl):
            @pl.when(layer == l)
            def _(l=l):
                g = None
                for s in range(N_DEV):
                    part = jnp.where(me == s, own[l][...], land[l][s]).astype(F32)
                    g = part if g is None else g + part
                delta, m2, v2 = _adamw(g, w_ref[...], m_ref[...], v_ref[...])
                g_ref[...] = g
                d_ref[...] = delta
                m2_ref[...] = m2
                v2_ref[...] = v2

    def rows_of(l, a, i):
        return jnp.where(a == l, i, 0)

    spec = pl.BlockSpec((None, tr, c), lambda a, i, dev_ref: (a, i, 0))
    in_specs = [pl.BlockSpec((N_DEV, tr, c), lambda a, i, dev_ref, l=l: (0, rows_of(l, a, i), 0)) for l in range(nl)]
    in_specs += [pl.BlockSpec((None, tr, c), lambda a, i, dev_ref, l=l: (dev_ref[0], rows_of(l, a, i), 0))
                 for l in range(nl)]
    grid_spec = pltpu.PrefetchScalarGridSpec(
        num_scalar_prefetch=1, grid=(nl, r // tr), in_specs=in_specs + [spec] * 3, out_specs=[spec] * 4)
    return pl.pallas_call(
        body, name=name, grid_spec=grid_spec, out_shape=[_sds(w.shape, F32)] * 4,
        compiler_params=pltpu.CompilerParams(vmem_limit_bytes=VMEM_LIMIT))(dev, *lands, *parts, w, m, v)


def _sum8(name, parts):
    _, r, c = parts.shape
    tr = _row_tile(r)

    def body(p_ref, o_ref):
        acc = p_ref[0]
        for s in range(1, N_DEV):
            acc = acc + p_ref[s]
        o_ref[...] = acc

    return _call(body, name, (r // tr,), [pl.BlockSpec((N_DEV, tr, c), lambda i: (0, i, 0))],
                 pl.BlockSpec((tr, c), lambda i: (i, 0)), _sds((r, c), F32))(parts)


def _adamw_flat(name, g, w, m, v):
    r, c = g.shape
    tr = _row_tile(r)

    def body(g_ref, w_ref, m_ref, v_ref, d_ref, m2_ref, v2_ref):
        delta, m2, v2 = _adamw(g_ref[...], w_ref[...], m_ref[...], v_ref[...])
        d_ref[...] = delta
        m2_ref[...] = m2
        v2_ref[...] = v2

    spec = pl.BlockSpec((tr, c), lambda i: (i, 0))
    return _call(body, name, (r // tr,), [spec] * 4, [spec] * 3, [_sds((r, c), F32)] * 3)(g, w, m, v)


def _pack(arrs, row_multiple=SUBLANES):
    pieces, rows = [], 0
    for a in arrs:
        piece = a.reshape(-1, LANES)
        piece = jnp.pad(piece, ((0, (-piece.shape[0]) % SUBLANES), (0, 0)))
        pieces.append(piece)
        rows += piece.shape[0]
    if rows % row_multiple:
        pieces.append(jnp.zeros(((-rows) % row_multiple, LANES), pieces[0].dtype))
    return jnp.concatenate(pieces, axis=0)


def _unpack(buf, shapes, lead=0):
    out, pos = [], 0
    for shp in shapes:
        rows = math.prod(shp) // LANES
        piece = lax.slice_in_dim(buf, pos, pos + rows, axis=lead)
        out.append(piece.reshape(buf.shape[:lead] + tuple(shp)))
        pos += rows + (-rows) % SUBLANES
    return out


REPLICATED = ["conv_b_in", "conv_b_dw", "conv_ln_g", "conv_ln_b", "conv_b_out", "gmlp_w_s", "gmlp_b_s",
              "ffn_b_up", "ffn_b_dw", "ffn_b_down", "norm1_g", "norm1_b", "norm2_g", "norm2_b"]
SMALL_SHARDED = ["conv_w_dw", "gmlp_b_in", "gmlp_ln_g", "gmlp_ln_b", "gmlp_b_out", "ffn_w_dw"]
BIG = ["conv_w_in", "conv_w_out", "gmlp_w_in", "gmlp_w_out", "ffn_w_up", "ffn_w_down"]
WEIGHTS = ["conv_w_in", "conv_b_in", "conv_w_dw", "conv_b_dw", "conv_ln_g", "conv_ln_b", "conv_w_out", "conv_b_out",
           "gmlp_w_in", "gmlp_b_in", "gmlp_ln_g", "gmlp_ln_b", "gmlp_w_s", "gmlp_b_s", "gmlp_w_out", "gmlp_b_out",
           "ffn_w_up", "ffn_b_up", "ffn_w_dw", "ffn_b_dw", "ffn_w_down", "ffn_b_down",
           "norm1_g", "norm1_b", "norm2_g", "norm2_b"]


def _from_shards(g, lead_shape):
    nd = len(lead_shape)
    perm = tuple(range(1, nd + 1)) + (0, nd + 1)
    return g.transpose(perm).reshape(tuple(lead_shape) + (-1,))


def _to_shards(full, width):
    lead = full.shape[:-1]
    nd = len(lead)
    parts = full.reshape(lead + (N_DEV, width))
    return parts.transpose((nd,) + tuple(range(nd)) + (nd + 1,))


def _step(p):
    x_in, target_in = p["x"], p["loss_target"]
    bsz, seq, d = x_in.shape
    t = bsz * seq
    assert seq % TM_EW == 0 and TM_EW % CHUNK == 0 and TM_EW // SUBLANES >= CONV_K - 1
    x0 = _to_segments(x_in.reshape(t, d), TM_EW)
    target = _to_segments(target_in.reshape(t, d), TM_EW)
    n_conv, n_gmlp = p["conv_w_in"].shape[0], p["gmlp_w_in"].shape[0]
    fb = p["ffn_w_up"].shape[-1]
    nblk = N_DEV
    half = nblk // 2
    cw = p["conv_w_in"].shape[-1]
    tm = min(TM_MM, t)
    tm_ln = min(TM_LN, t)
    nt = t // tm
    dev = 4 * lax.axis_index("x") + 2 * lax.axis_index("y") + lax.axis_index("c")

    small_shapes = [p[n].shape for n in SMALL_SHARDED]
    small_src = _pack([p[n] for n in SMALL_SHARDED])[None]
    small_all = _all_gather("gather_small_weights", [small_src])[0][0]
    sm = _unpack(small_all, small_shapes, lead=1)
    w_src = []
    for i in range(DEPTH):
        mix = "conv" if i % 2 == 0 else "gmlp"
        w_src += [p[mix + "_w_in"][i // 2].astype(BF16), p[mix + "_w_out"][i // 2].astype(BF16),
                  p["ffn_w_up"][i].T.astype(BF16), p["ffn_w_down"][i].astype(BF16)]
    send_sems, recv_sems, w_land = _gather_start(
        "weights_gather_start", _place_own("weights_place_own", w_src, deps=[small_all]))
    W_IN, W_OUT, W_UP, W_DOWN = range(4)

    def wait_weight(i, k, after):
        return _gather_wait(f"l{i}_weights_wait{k}", w_land[4 * i + k], send_sems, recv_sems, 4 * i + k, after)
    conv_w_dw = _from_shards(sm[0], sm[0].shape[1:-1])
    gmlp_b_in = _from_shards(sm[1], sm[1].shape[1:-1])
    gmlp_ln_g = _from_shards(sm[2], sm[2].shape[1:-1])
    gmlp_ln_b = _from_shards(sm[3], sm[3].shape[1:-1])
    gmlp_b_out = _from_shards(sm[4], sm[4].shape[1:-1])
    ffn_w_dw = sm[5].transpose(1, 0, 2, 3)

    def rows3(a):
        return a.reshape(a.shape[0], 1, a.shape[-1])

    conv_b_in4 = p["conv_b_in"].reshape(n_conv, N_DEV, 1, cw)
    gmlp_b_in4 = gmlp_b_in.reshape(n_gmlp, N_DEV, 1, cw)
    ffn_b_up4 = p["ffn_b_up"].reshape(DEPTH, nblk, 1, fb)
    ffn_b_dw4 = p["ffn_b_dw"].reshape(DEPTH, nblk, 1, fb)
    conv_b_dw3, conv_ln_g3, conv_ln_b3 = rows3(p["conv_b_dw"]), rows3(p["conv_ln_g"]), rows3(p["conv_ln_b"])
    conv_b_out3, gmlp_b_out3, ffn_b_down3 = rows3(p["conv_b_out"]), rows3(gmlp_b_out), rows3(p["ffn_b_down"])
    gmlp_ln_g3, gmlp_ln_b3 = rows3(gmlp_ln_g), rows3(gmlp_ln_b)
    n1g3, n1b3, n2g3, n2b3 = rows3(p["norm1_g"]), rows3(p["norm1_b"]), rows3(p["norm2_g"]), rows3(p["norm2_b"])
    w_tile, bs_tile = _sgu_operands(p["gmlp_w_s"], p["gmlp_b_s"])

    def mm_in(name, xa, wg, l, bias4):
        return _matmul(name, xa, wg, "nn", grid=(nt, N_DEV),
                       a_spec=pl.BlockSpec((tm, d), lambda i, n: (i, 0)),
                       b_spec=pl.BlockSpec((None, d, cw), lambda i, n: (n, 0, 0)),
                       o_spec=pl.BlockSpec((tm, cw), lambda i, n: (i, n)), o_shape=(t, N_DEV * cw), o_dtype=F32,
                       bias=bias4, bias_spec=pl.BlockSpec((None, None, 1, cw), lambda i, n: (l, n, 0, 0)))

    def mm_out_dx(name, dy, w):
        return _matmul(name, dy, w, "nt", grid=(nt,),
                       a_spec=pl.BlockSpec((tm, d), lambda i: (i, 0)),
                       b_spec=pl.BlockSpec((d, d), lambda i: (0, 0)),
                       o_spec=pl.BlockSpec((tm, d), lambda i: (i, 0)), o_shape=(t, d), o_dtype=F32)

    def mm_out_dw(name, sa, dy):
        return _matmul(name, sa, dy, "tn", grid=(nt,), k_axis=0, nk=nt, acc_shape=(d, d),
                       a_spec=pl.BlockSpec((tm, d), lambda k: (k, 0)),
                       b_spec=pl.BlockSpec((tm, d), lambda k: (k, 0)),
                       o_spec=pl.BlockSpec((d, d), lambda k: (0, 0)), o_shape=(d, d), o_dtype=BF16)

    def mm_in_dx(name, dh, wg, res):
        return _matmul(name, dh, wg, "nt", grid=(nt, N_DEV), k_axis=1, nk=N_DEV, acc_shape=(tm, d),
                       a_spec=pl.BlockSpec((tm, cw), lambda i, n: (i, n)),
                       b_spec=pl.BlockSpec((None, d, cw), lambda i, n: (n, 0, 0)),
                       o_spec=pl.BlockSpec((tm, d), lambda i, n: (i, 0)), o_shape=(t, d), o_dtype=F32,
                       res=res, res_spec=pl.BlockSpec((tm, d), lambda i, n: (i, 0)), res_scale=ALPHA)

    def mm_in_dw(name, xa, dh):
        return _matmul(name, xa, dh, "tn", grid=(N_DEV, nt), k_axis=1, nk=nt, acc_shape=(d, cw),
                       a_spec=pl.BlockSpec((tm, d), lambda n, k: (k, 0)),
                       b_spec=pl.BlockSpec((tm, cw), lambda n, k: (k, n)),
                       o_spec=pl.BlockSpec((None, d, cw), lambda n, k: (n, 0, 0)),
                       o_shape=(N_DEV, d, cw), o_dtype=BF16)

    def mm_down_dw(name, a, dy):
        return _matmul(name, a, dy, "tn", grid=(half, nt), k_axis=1, nk=nt, acc_shape=(fb, d),
                       a_spec=pl.BlockSpec((None, tm, fb), lambda n, k: (n, k, 0)),
                       b_spec=pl.BlockSpec((tm, d), lambda n, k: (k, 0)),
                       o_spec=pl.BlockSpec((None, fb, d), lambda n, k: (n, 0, 0)),
                       o_shape=(half, fb, d), o_dtype=BF16)

    def mm_up_dx(name, dh_g, dh_v, w, res):
        part = _matmul(name + "_g", dh_g, w, "nn", grid=(nt, half), k_axis=1, nk=half, acc_shape=(tm, d),
                       a_spec=pl.BlockSpec((None, tm, fb), lambda i, n: (n, i, 0)),
                       b_spec=pl.BlockSpec((None, fb, d), lambda i, n: (n, 0, 0)),
                       o_spec=pl.BlockSpec((tm, d), lambda i, n: (i, 0)), o_shape=(t, d), o_dtype=F32,
                       res=res, res_spec=pl.BlockSpec((tm, d), lambda i, n: (i, 0)), res_scale=ALPHA)
        return _matmul(name + "_v", dh_v, w, "nn", grid=(nt, half), k_axis=1, nk=half, acc_shape=(tm, d),
                       a_spec=pl.BlockSpec((None, tm, fb), lambda i, n: (n, i, 0)),
                       b_spec=pl.BlockSpec((None, fb, d), lambda i, n: (n + half, 0, 0)),
                       o_spec=pl.BlockSpec((tm, d), lambda i, n: (i, 0)), o_shape=(t, d), o_dtype=F32,
                       res=part, res_spec=pl.BlockSpec((tm, d), lambda i, n: (i, 0)), res_scale=1.0)

    def mm_up_dw(name, xa, dh_half, shift, prev=None):
        return _matmul(name, dh_half, xa, "tn", grid=(half, nt), k_axis=1, nk=nt, acc_shape=(fb, d),
                       a_spec=pl.BlockSpec((None, tm, fb), lambda n, k: (n, k, 0)),
                       b_spec=pl.BlockSpec((tm, d), lambda n, k: (k, 0)),
                       o_spec=pl.BlockSpec((None, fb, d), lambda n, k: (n + shift, 0, 0)),
                       o_shape=(nblk, fb, d), o_dtype=BF16, prev=prev)

    saved = []
    xcur = x0
    for i in range(DEPTH):
        j = i // 2
        s = {"x": xcur}
        s["w_in"] = wait_weight(i, W_IN, xcur)
        if i % 2 == 0:
            s["h"] = mm_in(f"l{i}_conv_in", xcur, s["w_in"], j, conv_b_in4)
            s["u"] = _glu(f"l{i}_glu", s["h"])
            s["c"] = _dwconv31(f"l{i}_dwconv", s["u"], conv_w_dw, conv_b_dw3, j, seq)
            s["s"] = _ln_silu(f"l{i}_ln_silu", s["c"], conv_ln_g3, conv_ln_b3, j)
            b_out3 = conv_b_out3
        else:
            s["h"] = mm_in(f"l{i}_gmlp_in", xcur, s["w_in"], j, gmlp_b_in4)
            s["s"] = _sgu(f"l{i}_sgu", s["h"], gmlp_ln_g3, gmlp_ln_b3, w_tile, bs_tile, j)
            b_out3 = gmlp_b_out3
        s["w_out"] = wait_weight(i, W_OUT, s["s"]).reshape(d, d)
        s["x1"], s["xhat1"], s["rstd1"] = _matmul_ln(
            f"l{i}_mixer_out_norm1", s["s"], s["w_out"], xcur, b_out3, n1g3, n1b3, j, i,
            nk=1, a_block=(tm_ln, d), b_block=(d, d))
        s["w_up"] = wait_weight(i, W_UP, s["x1"])
        s["hg"], s["hv"], s["a"] = _ffn_up_act(f"l{i}_ffn_up_act", s["x1"], s["w_up"], ffn_b_up4, ffn_w_dw, ffn_b_dw4,
                                               i, seq)
        s["w_down"] = wait_weight(i, W_DOWN, s["a"]).reshape(half, fb, d)
        xcur, s["xhat2"], s["rstd2"] = _matmul_ln(
            f"l{i}_ffn_down_norm2", s["a"], s["w_down"], s["x1"], ffn_b_down3, n2g3, n2b3, i, i,
            nk=half, a_block=(None, tm_ln, fb), b_block=(None, fb, d))
        saved.append(s)

    loss_row, dx = _loss("loss", xcur, target)

    started = {n: [None] * p[n].shape[0] for n in BIG}
    tokens = []

    def send_grad(n, l, g):
        send_sem, recv_sem, parts, land, token = _scatter_start(f"grad_{n}{l}_scatter_start", g)
        started[n][l] = (send_sem, recv_sem, parts, land)
        tokens.append(token)

    def take_tokens():
        out = list(tokens)
        tokens.clear()
        return out

    gl = {n: [None] * p[n].shape[0] for n in REPLICATED + SMALL_SHARDED}
    for i in reversed(range(DEPTH)):
        j = i // 2
        s = saved[i]
        mix = "conv" if i % 2 == 0 else "gmlp"
        dr2, gl["norm2_g"][i], gl["norm2_b"][i], gl["ffn_b_down"][i] = _ln_res_bwd(
            f"l{i}_norm2_bwd", dx, s["xhat2"], s["rstd2"], n2g3, i, deps=take_tokens())
        send_grad("ffn_w_down", i, mm_down_dw(f"l{i}_ffn_down_dw", s["a"], dr2).reshape(N_DEV, -1, d))
        dcg, dcv, dbg, dbv, dwg, dwv = _ffn_act_bwd(f"l{i}_ffn_act_bwd", dr2, s["w_down"], s["hg"], s["hv"],
                                                    ffn_w_dw, ffn_b_dw4, i, seq, deps=take_tokens())
        gl["ffn_b_dw"][i] = jnp.concatenate([dbg, dbv], axis=0).reshape(1, nblk * fb)
        gl["ffn_w_dw"][i] = jnp.concatenate([dwg[:, :FFN_K], dwv[:, :FFN_K]], axis=0)
        dh_g, dbu_g = _ffn_conv_t(f"l{i}_ffn_conv_t_g", dcg, ffn_w_dw, i, 0, seq)
        dh_v, dbu_v = _ffn_conv_t(f"l{i}_ffn_conv_t_v", dcv, ffn_w_dw, i, half, seq)
        gl["ffn_b_up"][i] = jnp.concatenate([dbu_g, dbu_v], axis=0).reshape(1, nblk * fb)
        dx1 = mm_up_dx(f"l{i}_ffn_up_dx", dh_g, dh_v, s["w_up"], dr2)
        g_up = mm_up_dw(f"l{i}_ffn_up_dw_g", s["x1"], dh_g, 0)
        send_grad("ffn_w_up", i, mm_up_dw(f"l{i}_ffn_up_dw_v", s["x1"], dh_v, half, g_up))
        dr1, gl["norm1_g"][i], gl["norm1_b"][i], gl[mix + "_b_out"][j] = _ln_res_bwd(
            f"l{i}_norm1_bwd", dx1, s["xhat1"], s["rstd1"], n1g3, i, deps=take_tokens())
        ds = mm_out_dx(f"l{i}_{mix}_out_dx", dr1, s["w_out"])
        send_grad(mix + "_w_out", j, mm_out_dw(f"l{i}_{mix}_out_dw", s["s"], dr1).reshape(N_DEV, -1, d))
        if i % 2 == 0:
            dc, gl["conv_ln_g"][j], gl["conv_ln_b"][j], gl["conv_b_dw"][j] = _ln_silu_bwd(
                f"l{i}_ln_silu_bwd", ds, s["c"], conv_ln_g3, conv_ln_b3, j, deps=take_tokens())
            du, dwdw = _dwconv31_bwd(f"l{i}_dwconv_bwd", dc, s["u"], conv_w_dw, j, seq)
            gl["conv_w_dw"][j] = dwdw[:CONV_K]
            dh, gl["conv_b_in"][j] = _glu_bwd(f"l{i}_glu_bwd", du, s["h"])
        else:
            dh, gl["gmlp_b_in"][j], dwt, dbt, gl["gmlp_ln_g"][j], gl["gmlp_ln_b"][j] = _sgu_bwd(
                f"l{i}_sgu_bwd", ds, s["h"], gmlp_ln_g3, gmlp_ln_b3, w_tile, bs_tile, j, deps=take_tokens())
            gl["gmlp_w_s"][j], gl["gmlp_b_s"][j] = _sgu_param_grads(dwt, dbt)
        dx = mm_in_dx(f"l{i}_{mix}_in_dx", dh, s["w_in"], dr1)
        send_grad(mix + "_w_in", j, mm_in_dw(f"l{i}_{mix}_in_dw", s["x"], dh))
    grad_x = _from_segments(dx, TM_EW).reshape(bsz, seq, d)

    full_small = {n: jnp.stack(gl[n]).reshape(p[n].shape) for n in REPLICATED}
    shard_small = {}
    for n in SMALL_SHARDED:
        if n == "ffn_w_dw":
            shard_small[n] = jnp.stack(gl[n]).transpose(1, 0, 2, 3)
        else:
            width = p[n].shape[-1]
            lead = p[n].shape[:-1]
            shard_small[n] = _to_shards(jnp.stack(gl[n]).reshape(lead + (N_DEV * width,)), width)
    flat_shapes = [(1, LANES)] + [p[n].shape for n in REPLICATED] + [(N_DEV,) + p[n].shape for n in SMALL_SHARDED]
    flat_local = _pack([loss_row] + [full_small[n] for n in REPLICATED] + [shard_small[n] for n in SMALL_SHARDED],
                       row_multiple=ROW_TILE_CAP)

    small_parts = _all_gather("gather_small_grads", [flat_local[None]])[0][0]
    small_sum = _sum8("sum_small_grads", small_parts)
    summed = _unpack(small_sum, flat_shapes)
    loss = summed[0][0, 0]
    grads = dict(zip(REPLICATED, summed[1:1 + len(REPLICATED)]))
    for n, g in zip(SMALL_SHARDED, summed[1 + len(REPLICATED):]):
        grads[n] = lax.dynamic_index_in_dim(g, dev, axis=0, keepdims=False)

    flat_started = [st for n in BIG for st in started[n]]
    parts_done, lands_done = _scatter_wait("grads_scatter_wait", flat_started, small_sum)

    delta, new_m, new_v = {}, {}, {}
    dev1 = jnp.reshape(dev, (1,)).astype(jnp.int32)
    pos = 0
    for n in BIG:
        nl = p[n].shape[0]
        state = [p[n], p["m_" + n], p["v_" + n]]
        if n == "ffn_w_up":
            state = [a.transpose(0, 2, 1) for a in state]
        outs = _sum8_adamw(f"adamw_{n}", dev1, lands_done[pos:pos + nl], parts_done[pos:pos + nl], *state)
        if n == "ffn_w_up":
            outs = [a.transpose(0, 2, 1) for a in outs]
        grads[n], delta[n], new_m[n], new_v[n] = outs
        pos += nl
    small = REPLICATED + SMALL_SHARDED
    small_shp = [p[n].shape for n in small]
    d_s, m_s, v_s = _adamw_flat("adamw_small", _pack([grads[n] for n in small]), _pack([p[n] for n in small]),
                                _pack([p["m_" + n] for n in small]), _pack([p["v_" + n] for n in small]))
    for n, dd, mm, vv in zip(small, _unpack(d_s, small_shp), _unpack(m_s, small_shp), _unpack(v_s, small_shp)):
        delta[n], new_m[n], new_v[n] = dd, mm, vv

    return (loss, grad_x, *[grads[n] for n in WEIGHTS], *[delta[n] for n in WEIGHTS],
            *[new_m[n] for n in WEIGHTS], *[new_v[n] for n in WEIGHTS])


def kernel(x, conv_w_in, conv_b_in, conv_w_dw, conv_b_dw, conv_ln_g, conv_ln_b, conv_w_out, conv_b_out, gmlp_w_in, gmlp_b_in, gmlp_ln_g, gmlp_ln_b, gmlp_w_s, gmlp_b_s, gmlp_w_out, gmlp_b_out, ffn_w_up, ffn_b_up, ffn_w_dw, ffn_b_dw, ffn_w_down, ffn_b_down, norm1_g, norm1_b, norm2_g, norm2_b, loss_target, m_conv_w_in, m_conv_b_in, m_conv_w_dw, m_conv_b_dw, m_conv_ln_g, m_conv_ln_b, m_conv_w_out, m_conv_b_out, m_gmlp_w_in, m_gmlp_b_in, m_gmlp_ln_g, m_gmlp_ln_b, m_gmlp_w_s, m_gmlp_b_s, m_gmlp_w_out, m_gmlp_b_out, m_ffn_w_up, m_ffn_b_up, m_ffn_w_dw, m_ffn_b_dw, m_ffn_w_down, m_ffn_b_down, m_norm1_g, m_norm1_b, m_norm2_g, m_norm2_b, v_conv_w_in, v_conv_b_in, v_conv_w_dw, v_conv_b_dw, v_conv_ln_g, v_conv_ln_b, v_conv_w_out, v_conv_b_out, v_gmlp_w_in, v_gmlp_b_in, v_gmlp_ln_g, v_gmlp_ln_b, v_gmlp_w_s, v_gmlp_b_s, v_gmlp_w_out, v_gmlp_b_out, v_ffn_w_up, v_ffn_b_up, v_ffn_w_dw, v_ffn_b_dw, v_ffn_w_down, v_ffn_b_down, v_norm1_g, v_norm1_b, v_norm2_g, v_norm2_b):
    return _step(dict(locals()))
```

```python
import math

import jax
import jax.numpy as jnp
from jax import lax
from jax.experimental import pallas as pl
from jax.experimental.pallas import tpu as pltpu

F32 = jnp.float32
BF16 = jnp.bfloat16
MESH = pl.DeviceIdType.MESH

N_DEV = 8
DEPTH = 4
ALPHA = (2.0 * DEPTH) ** 0.25
LN_EPS = 1e-5
CONV_K = 31
FFN_K = 3
CHUNK = 128
GROUPS = 8
ADAM_LR = 0.001
ADAM_B1 = 0.9
ADAM_B2 = 0.999
ADAM_EPS = 1e-08
ADAM_WD = 0.01
ADAM_STEP = 10
INV_SQRT2 = 1.0 / math.sqrt(2.0)
INV_SQRT2PI = 1.0 / math.sqrt(2.0 * math.pi)

LANES = 128
SUBLANES = 8
VMEM_LIMIT = 48 * 1024 * 1024
TM_MM = 1024
TM_EW = 256


def _call(body, name, grid, in_specs, out_specs, out_shape, scratch=(), aliases=None, deps=()):
    deps = list(deps)
    in_specs = list(in_specs)
    n_in = len(in_specs)
    if deps:
        inner = body

        def body(*refs):
            return inner(*refs[:n_in], *refs[n_in + len(deps):])

        in_specs = in_specs + [pl.BlockSpec(memory_space=pl.ANY)] * len(deps)
    fn = pl.pallas_call(
        body, name=name, grid=grid, in_specs=in_specs, out_specs=out_specs, out_shape=out_shape,
        scratch_shapes=list(scratch), input_output_aliases=aliases or {},
        compiler_params=pltpu.CompilerParams(vmem_limit_bytes=VMEM_LIMIT))
    return lambda *args: fn(*args, *deps)


def _sds(shape, dtype):
    return jax.ShapeDtypeStruct(tuple(shape), dtype)


def _sigmoid(x):
    return 1.0 / (1.0 + jnp.exp(-x))


def _acc_rows(ref, val, first):
    @pl.when(first)
    def _():
        ref[...] = val

    @pl.when(jnp.logical_not(first))
    def _():
        ref[...] += val


def _colsum(v):
    return jnp.sum(v, axis=0, keepdims=True)


_DIMS = {"nn": ((1,), (0,)), "nt": ((1,), (1,)), "tn": ((0,), (0,))}


def _matmul(name, a, b, mode, *, grid, a_spec, b_spec, o_spec, o_shape, o_dtype, k_axis=None, nk=1,
            acc_shape=None, bias=None, bias_spec=None, res=None, res_spec=None, res_scale=1.0):
    dims = (_DIMS[mode], ((), ()))
    has_bias, has_res = bias is not None, res is not None

    def body(*refs):
        a_ref, b_ref = refs[0], refs[1]
        pos = 2
        bias_ref = res_ref = None
        if has_bias:
            bias_ref = refs[pos]
            pos += 1
        if has_res:
            res_ref = refs[pos]
            pos += 1
        o_ref = refs[pos]
        acc_ref = refs[pos + 1] if nk > 1 else None
        p = lax.dot_general(a_ref[...].astype(BF16), b_ref[...].astype(BF16), dims, preferred_element_type=F32)

        def finish(acc):
            if has_bias:
                acc = acc + bias_ref[...]
            if has_res:
                acc = acc + res_scale * res_ref[...]
            o_ref[...] = acc.astype(o_dtype)

        if nk == 1:
            finish(p)
        else:
            k = pl.program_id(k_axis)

            @pl.when(k == 0)
            def _():
                acc_ref[...] = p

            @pl.when(k > 0)
            def _():
                acc_ref[...] += p

            @pl.when(k == nk - 1)
            def _():
                finish(acc_ref[...])

    ins, specs = [a, b], [a_spec, b_spec]
    if has_bias:
        ins.append(bias)
        specs.append(bias_spec)
    if has_res:
        ins.append(res)
        specs.append(res_spec)
    scratch = [pltpu.VMEM(acc_shape, F32)] if nk > 1 else []
    return _call(body, name, grid, specs, o_spec, _sds(o_shape, o_dtype), scratch)(*ins)


TM_LN = 512


def _matmul_ln(name, a, b, x_res, bias3, g3, b3, l_bias, l_norm, *, nk, a_block, b_block):
    t, d = x_res.shape
    tm = min(TM_LN, t)

    def body(a_ref, b_ref, x_ref, bias_ref, g_ref, be_ref, o_ref, xh_ref, rs_ref, *acc):
        p = jnp.dot(a_ref[...].astype(BF16), b_ref[...], preferred_element_type=F32)

        def finish(y):
            xhat, rstd = _ln_stats(ALPHA * x_ref[...] + y + bias_ref[...])
            o_ref[...] = xhat * g_ref[...] + be_ref[...]
            xh_ref[...] = xhat
            rs_ref[...] = rstd

        if nk == 1:
            finish(p)
        else:
            k = pl.program_id(1)

            @pl.when(k == 0)
            def _():
                acc[0][...] = p

            @pl.when(k > 0)
            def _():
                acc[0][...] += p

            @pl.when(k == nk - 1)
            def _():
                finish(acc[0][...])

    if nk == 1:
        grid = (t // tm,)
        a_spec = pl.BlockSpec(a_block, lambda i: (i, 0))
        b_spec = pl.BlockSpec(b_block, lambda i: (0, 0))
    else:
        grid = (t // tm, nk)
        a_spec = pl.BlockSpec(a_block, lambda i, k: (k, i, 0))
        b_spec = pl.BlockSpec(b_block, lambda i, k: (k, 0, 0))
    row = pl.BlockSpec((tm, d), lambda i, *_: (i, 0))
    stat = pl.BlockSpec((tm, 1), lambda i, *_: (i, 0))

    def vec(l):
        return pl.BlockSpec((None, 1, d), lambda *_: (l, 0, 0))

    return _call(body, name, grid, [a_spec, b_spec, row, vec(l_bias), vec(l_norm), vec(l_norm)], [row, row, stat],
                 [_sds((t, d), F32), _sds((t, d), F32), _sds((t, 1), F32)],
                 [pltpu.VMEM((tm, d), F32)] if nk > 1 else [])(a, b, x_res, bias3, g3, b3)


def _mesh_pos():
    return lax.axis_index("x"), lax.axis_index("y"), lax.axis_index("c")


def _any_specs(n):
    return [pl.BlockSpec(memory_space=pl.ANY)] * n


def _all_gather(name, srcs):
    n = len(srcs)

    def body(*refs):
        src, out = refs[:n], refs[n:2 * n]
        send_sems, recv_sems, local_sems = refs[2 * n:]
        x, y, c = _mesh_pos()
        me, sibling = (x, y, c), (x, y, 1 - c)
        chips = [(1 - x, y), (x, 1 - y), (1 - x, 1 - y)]

        def slot(k, p):
            return out[k].at[:, 4 * p[0] + 2 * p[1] + p[2]]

        def copy(k, idx, block, to, s=None):
            return pltpu.make_async_remote_copy(
                src_ref=slot(k, block) if s is None else s, dst_ref=slot(k, block),
                send_sem=send_sems.at[k * 7 + idx], recv_sem=recv_sems.at[k * 7 + idx],
                device_id=to, device_id_type=MESH)

        local = [pltpu.make_async_copy(src[k], slot(k, me), local_sems.at[k]) for k in range(n)]
        for cp in local:
            cp.start()
        first = []
        for k in range(n):
            first.append(copy(k, 0, me, sibling, src[k]))
            for j, chip in enumerate(chips):
                first.append(copy(k, 1 + j, me, (*chip, c), src[k]))
        for cp in first:
            cp.start()
        passed = []
        for j, chip in enumerate(chips):
            for k in range(n):
                copy(k, 1 + j, (*chip, c), me).wait_recv()
                cp = copy(k, 4 + j, (*chip, c), sibling)
                cp.start()
                passed.append(cp)
        for k in range(n):
            copy(k, 0, sibling, me).wait_recv()
            for j, chip in enumerate(chips):
                copy(k, 4 + j, (*chip, 1 - c), me).wait_recv()
        for cp in first + passed:
            cp.wait_send()
        for cp in local:
            cp.wait()

    out_shape = [_sds((s.shape[0], N_DEV) + s.shape[1:], s.dtype) for s in srcs]
    return _call(body, name, (), [pl.BlockSpec(memory_space=pltpu.VMEM)] * n, _any_specs(n), out_shape,
                 [pltpu.SemaphoreType.DMA((7 * n,)), pltpu.SemaphoreType.DMA((7 * n,)),
                  pltpu.SemaphoreType.DMA((n,))])(*srcs)


HBM_SPEC = pl.BlockSpec(memory_space=pltpu.HBM)
SEM_SPEC = pl.BlockSpec(memory_space=pltpu.SEMAPHORE)
N_PEER = N_DEV - 1


def _split_call(body, name, in_specs, out_specs, out_shape, aliases):
    return pl.pallas_call(
        body, name=name, in_specs=in_specs, out_specs=out_specs, out_shape=out_shape, input_output_aliases=aliases,
        compiler_params=pltpu.CompilerParams(has_side_effects=pltpu.SideEffectType.DATAFLOW_SIDE_EFFECTING))


def _peers(x, y, c):
    return [(1 - x if q & 4 else x, 1 - y if q & 2 else y, 1 - c if q & 1 else c) for q in range(1, N_DEV)]


def _in_hbm(a):
    return pltpu.with_memory_space_constraint(a, pltpu.HBM)


def _place_own(name, srcs, deps=()):
    n = len(srcs)

    def body(*refs):
        src, out, sems = refs[:n], refs[n:2 * n], refs[2 * n]
        x, y, c = _mesh_pos()
        dev = 4 * x + 2 * y + c
        copies = [pltpu.make_async_copy(src[k], out[k].at[dev], sems.at[k]) for k in range(n)]
        for cp in copies:
            cp.start()
        for cp in copies:
            cp.wait()

    return _call(body, name, (), [pl.BlockSpec(memory_space=pltpu.VMEM)] * n, _any_specs(n),
                 [_sds((N_DEV,) + s.shape, s.dtype) for s in srcs], [pltpu.SemaphoreType.DMA((n,))],
                 deps=deps)(*srcs)


def _gather_start(name, lands):
    n = len(lands)

    def body(*refs):
        land, send_sems, recv_sems = refs[:n], refs[n], refs[n + 1]
        x, y, c = _mesh_pos()
        dev = 4 * x + 2 * y + c
        for k in range(n):
            for peer in _peers(x, y, c):
                pltpu.make_async_remote_copy(
                    src_ref=land[k].at[dev], dst_ref=land[k].at[dev], send_sem=send_sems.at[k],
                    recv_sem=recv_sems.at[k], device_id=peer, device_id_type=MESH).start()

    outs = _split_call(
        body, name, [HBM_SPEC] * n, [SEM_SPEC, SEM_SPEC] + [HBM_SPEC] * n,
        [pltpu.SemaphoreType.DMA((n,)), pltpu.SemaphoreType.DMA((n,))] + [pltpu.HBM(a.shape, a.dtype) for a in lands],
        {k: 2 + k for k in range(n)})(*[_in_hbm(a) for a in lands])
    return outs[0], outs[1], list(outs[2:])


def _wait_seven(src_ref, dst_ref, send_sem, recv_sem):
    cp = pltpu.make_async_remote_copy(
        src_ref=src_ref.at[pl.ds(0, N_PEER)], dst_ref=dst_ref.at[pl.ds(0, N_PEER)], send_sem=send_sem,
        recv_sem=recv_sem, device_id=_mesh_pos(), device_id_type=MESH)
    cp.wait_send()
    cp.wait_recv()


def _gather_wait(name, land, send_sems, recv_sems, k, after):
    def body(land_ref, send_ref, recv_ref, after_ref, out_ref):
        _wait_seven(land_ref, land_ref, send_ref.at[k], recv_ref.at[k])

    return _split_call(body, name, [HBM_SPEC, SEM_SPEC, SEM_SPEC, pl.BlockSpec(memory_space=pl.ANY)], HBM_SPEC,
                       pltpu.HBM(land.shape, land.dtype), {0: 0})(land, send_sems, recv_sems, after)


def _scatter_start(name, parts):
    def body(parts_ref, land_ref, send_sem, recv_sem, parts_out, land_out, token):
        x, y, c = _mesh_pos()
        dev = 4 * x + 2 * y + c
        for peer in _peers(x, y, c):
            pltpu.make_async_remote_copy(
                src_ref=parts_ref.at[4 * peer[0] + 2 * peer[1] + peer[2]], dst_ref=land_ref.at[dev],
                send_sem=send_sem, recv_sem=recv_sem, device_id=peer, device_id_type=MESH).start()
        token[...] = jnp.zeros_like(token)

    buf = pltpu.HBM(parts.shape, parts.dtype)
    return _split_call(
        body, name, [HBM_SPEC, HBM_SPEC],
        [SEM_SPEC, SEM_SPEC, HBM_SPEC, HBM_SPEC, pl.BlockSpec(memory_space=pltpu.VMEM)],
        [pltpu.SemaphoreType.DMA(()), pltpu.SemaphoreType.DMA(()), buf, buf, _sds((SUBLANES, LANES), F32)],
        {0: 2, 1: 3})(_in_hbm(parts), _in_hbm(lax.empty(parts.shape, parts.dtype)))


def _scatter_wait(name, started, after):
    n = len(started)

    def body(*refs):
        for k in range(n):
            send_sem, recv_sem, parts_ref, land_ref = refs[4 * k:4 * k + 4]
            _wait_seven(parts_ref, land_ref, send_sem, recv_sem)

    flat = [a for s in started for a in s]
    outs = _split_call(
        body, name, [SEM_SPEC, SEM_SPEC, HBM_SPEC, HBM_SPEC] * n + [pl.BlockSpec(memory_space=pl.ANY)],
        [HBM_SPEC, HBM_SPEC] * n, [pltpu.HBM(a.shape, a.dtype) for s in started for a in s[2:]],
        {4 * k + 2 + t: 2 * k + t for k in range(n) for t in range(2)})(*flat, after)
    return list(outs[0::2]), list(outs[1::2])


def _to_segments(a, tile):
    seg = tile // SUBLANES
    return a.reshape((a.shape[0] // tile, SUBLANES, seg) + a.shape[1:]).swapaxes(1, 2).reshape(a.shape)


def _from_segments(a, tile):
    seg = tile // SUBLANES
    return a.reshape((a.shape[0] // tile, seg, SUBLANES) + a.shape[1:]).swapaxes(1, 2).reshape(a.shape)


def _chunk(ref, q):
    return ref[q * SUBLANES:(q + 1) * SUBLANES, :]


def _fill_wrap_prev(x_ref, halo_ref, wrap_ref, n_wrap, n_halo, seg, keep):
    sub = lax.broadcasted_iota(jnp.int32, (SUBLANES, x_ref.shape[-1]), 0)
    for j in range(n_wrap):
        q = seg - n_wrap + j
        hq = q - (seg - n_halo)
        row = halo_ref[hq * SUBLANES + SUBLANES - 1:(hq + 1) * SUBLANES, :] * keep
        wrap_ref[j * SUBLANES:(j + 1) * SUBLANES, :] = jnp.where(sub == 0, row, pltpu.roll(_chunk(x_ref, q), 1, 0))


def _fill_wrap_next(x_ref, halo_ref, wrap_ref, n_wrap, keep):
    sub = lax.broadcasted_iota(jnp.int32, (SUBLANES, x_ref.shape[-1]), 0)
    for j in range(n_wrap):
        row = halo_ref[j * SUBLANES:j * SUBLANES + 1, :] * keep
        wrap_ref[j * SUBLANES:(j + 1) * SUBLANES, :] = jnp.where(
            sub == SUBLANES - 1, row, pltpu.roll(_chunk(x_ref, j), SUBLANES - 1, 0))


def _past(x_ref, wrap_ref, q, d, n_wrap):
    return _chunk(x_ref, q - d) if q >= d else _chunk(wrap_ref, q - d + n_wrap)


def _future(x_ref, wrap_ref, q, d, seg):
    return _chunk(x_ref, q + d) if q + d < seg else _chunk(wrap_ref, q + d - seg)


def _conv_fwd(x_ref, wrap_ref, w_ref, b_ref, out_ref, seg, k_taps):
    bias = jnp.broadcast_to(b_ref[...], (SUBLANES, x_ref.shape[-1]))
    for q in range(seg):
        acc = bias
        for k in range(k_taps):
            acc = acc + w_ref[k:k + 1, :] * _past(x_ref, wrap_ref, q, k_taps - 1 - k, k_taps - 1)
        out_ref[q * SUBLANES:(q + 1) * SUBLANES, :] = acc


def _conv_bwd_data(d_ref, wrap_ref, w_ref, out_ref, seg, k_taps):
    for q in range(seg):
        acc = None
        for k in range(k_taps):
            term = w_ref[k:k + 1, :] * _future(d_ref, wrap_ref, q, k_taps - 1 - k, seg)
            acc = term if acc is None else acc + term
        out_ref[q * SUBLANES:(q + 1) * SUBLANES, :] = acc


def _conv_bwd_taps(d_ref, x_ref, wrap_ref, dw_ref, seg, k_taps):
    for k in range(k_taps):
        part = None
        for q in range(seg):
            term = _chunk(d_ref, q) * _past(x_ref, wrap_ref, q, k_taps - 1 - k, k_taps - 1)
            part = term if part is None else part + term
        dw_ref[k:k + 1, :] += _colsum(part)


def _tile_halo_specs(tm, width_block, n_halo, n_tiles, block_of):
    rows = n_halo * SUBLANES
    per = tm // rows
    tile = pl.BlockSpec(width_block(tm), lambda n, i: block_of(n, i))
    prev = pl.BlockSpec(width_block(rows), lambda n, i: block_of(n, jnp.maximum(i * per - 1, 0)))
    nxt = pl.BlockSpec(width_block(rows), lambda n, i: block_of(n, jnp.minimum((i + 1) * per, n_tiles * per - 1)))
    return tile, prev, nxt


def _ln_stats(v):
    mu = jnp.mean(v, axis=-1, keepdims=True)
    vc = v - mu
    var = jnp.mean(vc * vc, axis=-1, keepdims=True)
    rstd = lax.rsqrt(var + LN_EPS)
    return vc * rstd, rstd


def _ln_backward(dxhat, xhat, rstd):
    m1 = jnp.mean(dxhat, axis=-1, keepdims=True)
    m2 = jnp.mean(dxhat * xhat, axis=-1, keepdims=True)
    return rstd * (dxhat - m1 - xhat * m2)


def _row_spec(tm, width):
    return pl.BlockSpec((tm, width), lambda i: (i, 0))


def _param_spec(l, width):
    return pl.BlockSpec((None, 1, width), lambda *_: (l, 0, 0))


def _ln_res_bwd(name, dout, xhat, rstd, g3, l, deps=()):
    t, d = dout.shape
    tm = min(TM_EW, t)

    def body(do_ref, xh_ref, rs_ref, g_ref, dr_ref, dg_ref, db_ref, dc_ref):
        first = pl.program_id(0) == 0
        do, xhat = do_ref[...], xh_ref[...]
        dr = _ln_backward(do * g_ref[...], xhat, rs_ref[...])
        dr_ref[...] = dr
        _acc_rows(dg_ref, _colsum(do * xhat), first)
        _acc_rows(db_ref, _colsum(do), first)
        _acc_rows(dc_ref, _colsum(dr), first)

    vec = pl.BlockSpec((1, d), lambda i: (0, 0))
    return _call(body, name, (t // tm,),
                 [_row_spec(tm, d), _row_spec(tm, d), _row_spec(tm, 1), _param_spec(l, d)],
                 [_row_spec(tm, d), vec, vec, vec],
                 [_sds((t, d), F32)] + [_sds((1, d), F32)] * 3, deps=deps)(dout, xhat, rstd, g3)


def _glu(name, h):
    t, c2 = h.shape
    c = c2 // 2
    tm = min(TM_EW, t)

    def body(a_ref, g_ref, o_ref):
        o_ref[...] = a_ref[...] * _sigmoid(g_ref[...])

    return _call(body, name, (t // tm,),
                 [pl.BlockSpec((tm, c), lambda i: (i, 0)), pl.BlockSpec((tm, c), lambda i: (i, 1))],
                 _row_spec(tm, c), _sds((t, c), F32))(h, h)


def _glu_bwd(name, du, h):
    t, c2 = h.shape
    c = c2 // 2
    tm = min(TM_EW, t)

    def body(du_ref, a_ref, g_ref, dh_ref, db_ref):
        first = pl.program_id(0) == 0
        du_v, a = du_ref[...], a_ref[...]
        sg = _sigmoid(g_ref[...])
        da = du_v * sg
        dg = du_v * a * sg * (1.0 - sg)
        dh_ref[:, :c] = da.astype(BF16)
        dh_ref[:, c:] = dg.astype(BF16)
        _acc_rows(db_ref.at[:, :c], _colsum(da), first)
        _acc_rows(db_ref.at[:, c:], _colsum(dg), first)

    return _call(body, name, (t // tm,),
                 [_row_spec(tm, c), pl.BlockSpec((tm, c), lambda i: (i, 0)), pl.BlockSpec((tm, c), lambda i: (i, 1))],
                 [_row_spec(tm, c2), pl.BlockSpec((1, c2), lambda i: (0, 0))],
                 [_sds((t, c2), BF16), _sds((1, c2), F32)])(du, h, h)


CONV_CB = 256
TAPS_PAD = 32


def _dwconv31(name, u, w3, b3, l, seq):
    t, c = u.shape
    tm, cb = TM_EW, CONV_CB
    seg, seq_tiles, n_tiles = tm // SUBLANES, seq // tm, t // tm
    n_wrap = CONV_K - 1
    tile, prev, _ = _tile_halo_specs(tm, lambda rows: (rows, cb), seg, n_tiles, lambda n, r: (r, n))

    def body(u_ref, halo_ref, w_ref, b_ref, o_ref, wrap_ref):
        keep = (pl.program_id(1) % seq_tiles != 0).astype(F32)
        _fill_wrap_prev(u_ref, halo_ref, wrap_ref, n_wrap, seg, seg, keep)
        _conv_fwd(u_ref, wrap_ref, w_ref, b_ref, o_ref, seg, CONV_K)

    return _call(body, name, (c // cb, n_tiles),
                 [tile, prev, pl.BlockSpec((None, CONV_K, cb), lambda n, i: (l, 0, n)),
                  pl.BlockSpec((None, 1, cb), lambda n, i: (l, 0, n))],
                 tile, _sds((t, c), F32), [pltpu.VMEM((n_wrap * SUBLANES, cb), F32)])(u, u, w3, b3)


def _dwconv31_bwd(name, dc, u, w3, l, seq):
    t, c = dc.shape
    tm, cb = TM_EW, CONV_CB
    seg, seq_tiles, n_tiles = tm // SUBLANES, seq // tm, t // tm
    n_wrap = CONV_K - 1
    tile, prev, nxt = _tile_halo_specs(tm, lambda rows: (rows, cb), seg, n_tiles, lambda n, r: (r, n))

    def body(dc_ref, dcn_ref, u_ref, up_ref, w_ref, du_ref, dw_ref, dwrap_ref, uwrap_ref):
        i = pl.program_id(1)
        keep_prev = (i % seq_tiles != 0).astype(F32)
        keep_next = (i % seq_tiles != seq_tiles - 1).astype(F32)
        _fill_wrap_next(dc_ref, dcn_ref, dwrap_ref, n_wrap, keep_next)
        _conv_bwd_data(dc_ref, dwrap_ref, w_ref, du_ref, seg, CONV_K)

        @pl.when(i == 0)
        def _():
            dw_ref[...] = jnp.zeros_like(dw_ref)

        _fill_wrap_prev(u_ref, up_ref, uwrap_ref, n_wrap, seg, seg, keep_prev)
        _conv_bwd_taps(dc_ref, u_ref, uwrap_ref, dw_ref, seg, CONV_K)

    wrap = pltpu.VMEM((n_wrap * SUBLANES, cb), F32)
    return _call(body, name, (c // cb, n_tiles),
                 [tile, nxt, tile, prev, pl.BlockSpec((None, CONV_K, cb), lambda n, i: (l, 0, n))],
                 [tile, pl.BlockSpec((TAPS_PAD, cb), lambda n, i: (0, n))],
                 [_sds((t, c), F32), _sds((TAPS_PAD, c), F32)], [wrap, wrap])(dc, dc, u, u, w3)


def _ln_silu(name, cx, g3, b3, l):
    t, d = cx.shape
    tm = min(TM_EW, t)

    def body(c_ref, g_ref, b_ref, o_ref):
        xhat, _ = _ln_stats(c_ref[...])
        nv = xhat * g_ref[...] + b_ref[...]
        o_ref[...] = (nv * _sigmoid(nv)).astype(BF16)

    return _call(body, name, (t // tm,), [_row_spec(tm, d), _param_spec(l, d), _param_spec(l, d)],
                 _row_spec(tm, d), _sds((t, d), BF16))(cx, g3, b3)


def _ln_silu_bwd(name, ds, cx, g3, b3, l, deps=()):
    t, d = cx.shape
    tm = min(TM_EW, t)

    def body(ds_ref, c_ref, g_ref, b_ref, dc_ref, dg_ref, db_ref, dsum_ref):
        first = pl.program_id(0) == 0
        xhat, rstd = _ln_stats(c_ref[...])
        g = g_ref[...]
        nv = xhat * g + b_ref[...]
        sg = _sigmoid(nv)
        dn = ds_ref[...] * (sg * (1.0 + nv * (1.0 - sg)))
        dc = _ln_backward(dn * g, xhat, rstd)
        dc_ref[...] = dc
        _acc_rows(dg_ref, _colsum(dn * xhat), first)
        _acc_rows(db_ref, _colsum(dn), first)
        _acc_rows(dsum_ref, _colsum(dc), first)

    vec = pl.BlockSpec((1, d), lambda i: (0, 0))
    return _call(body, name, (t // tm,),
                 [_row_spec(tm, d), _row_spec(tm, d), _param_spec(l, d), _param_spec(l, d)],
                 [_row_spec(tm, d), vec, vec, vec],
                 [_sds((t, d), F32)] + [_sds((1, d), F32)] * 3, deps=deps)(ds, cx, g3, b3)


FFN_HALO = FFN_K - 1


def _ffn_conv(x_ref, halo_ref, wrap_ref, w_ref, b_ref, keep, seg, out_ref):
    _fill_wrap_prev(x_ref, halo_ref, wrap_ref, FFN_K - 1, FFN_HALO, seg, keep)
    _conv_fwd(x_ref, wrap_ref, w_ref, b_ref, out_ref, seg, FFN_K)


TM_FFN = 512
WRAP_ROWS = FFN_HALO * SUBLANES


def _sub_tiles(x_ref, prev_ref, next_ref, keep_prev, keep_next, n_sub):
    out = []
    for s in range(n_sub):
        tile = x_ref.at[pl.ds(s * TM_EW, TM_EW)]
        prev = prev_ref if s == 0 else x_ref.at[pl.ds(s * TM_EW - WRAP_ROWS, WRAP_ROWS)]
        nxt = next_ref if s == n_sub - 1 else x_ref.at[pl.ds((s + 1) * TM_EW, WRAP_ROWS)]
        out.append((tile, prev, keep_prev if s == 0 else 1.0, nxt, keep_next if s == n_sub - 1 else 1.0))
    return out


def _rows(ref, s, rows):
    return ref.at[pl.ds(s * rows, rows)]


def _ffn_specs(tm, fb, n_tiles):
    return _tile_halo_specs(tm, lambda rows: (None, rows, fb), FFN_HALO, n_tiles, lambda n, r: (n, r, 0))


def _ffn_up_act(name, x, w_up, b_up4, wdw, bdw, l, seq):
    t, d = x.shape
    nb, fb, _ = w_up.shape
    half = nb // 2
    tm = min(TM_FFN, seq)
    n_sub, seg, seq_steps, n_steps = tm // TM_EW, TM_EW // SUBLANES, seq // tm, t // tm
    per = tm // WRAP_ROWS
    nt_dims = (_DIMS["nt"], ((), ()))

    def body(x_ref, xp_ref, ug_ref, uv_ref, bug_ref, buv_ref, wg_ref, wv_ref, bg_ref, bv_ref,
             hg_ref, hv_ref, a_ref, pg_ref, pv_ref, wrap_ref, cg_ref, cv_ref):
        keep = (pl.program_id(1) % seq_steps != 0).astype(F32)
        xb, xpb = x_ref[...].astype(BF16), xp_ref[...].astype(BF16)
        hg_ref[...] = lax.dot_general(xb, ug_ref[...], nt_dims, preferred_element_type=F32) + bug_ref[...]
        pg_ref[...] = lax.dot_general(xpb, ug_ref[...], nt_dims, preferred_element_type=F32) + bug_ref[...]
        hv_ref[...] = lax.dot_general(xb, uv_ref[...], nt_dims, preferred_element_type=F32) + buv_ref[...]
        pv_ref[...] = lax.dot_general(xpb, uv_ref[...], nt_dims, preferred_element_type=F32) + buv_ref[...]
        for s, (tile, prev, kp, _, _) in enumerate(_sub_tiles(hg_ref, pg_ref, None, keep, None, n_sub)):
            _ffn_conv(tile, prev, wrap_ref, wg_ref, bg_ref, kp, seg, _rows(cg_ref, s, TM_EW))
        for s, (tile, prev, kp, _, _) in enumerate(_sub_tiles(hv_ref, pv_ref, None, keep, None, n_sub)):
            _ffn_conv(tile, prev, wrap_ref, wv_ref, bv_ref, kp, seg, _rows(cv_ref, s, TM_EW))
        cg = cg_ref[...]
        a_ref[...] = (cg * _sigmoid(cg) * cv_ref[...]).astype(BF16)

    def blk(shift):
        return pl.BlockSpec((None, fb, d), lambda n, i: (n + shift, 0, 0))

    def vec(shift, rows):
        return pl.BlockSpec((None, None, rows, fb), lambda n, i: (l, n + shift, 0, 0))

    out = pl.BlockSpec((None, tm, fb), lambda n, i: (n, i, 0))
    tmp = pltpu.VMEM((tm, fb), F32)
    halo = pltpu.VMEM((WRAP_ROWS, fb), F32)
    return _call(body, name, (half, n_steps),
                 [pl.BlockSpec((tm, d), lambda n, i: (i, 0)),
                  pl.BlockSpec((WRAP_ROWS, d), lambda n, i: (jnp.maximum(i * per - 1, 0), 0)),
                  blk(0), blk(half), vec(0, 1), vec(half, 1), vec(0, FFN_K), vec(half, FFN_K), vec(0, 1), vec(half, 1)],
                 [out, out, out],
                 [_sds((half, t, fb), F32), _sds((half, t, fb), F32), _sds((half, t, fb), BF16)],
                 [halo, halo, halo, tmp, tmp])(x, x, w_up, w_up, b_up4, b_up4, wdw, wdw, bdw, bdw)


def _ffn_act_bwd(name, dy, w_down, hg, hv, wdw, bdw, l, seq, deps=()):
    half, t, fb = hg.shape
    d = dy.shape[-1]
    tm = min(TM_FFN, seq)
    n_sub, seg, seq_steps, n_steps = tm // TM_EW, TM_EW // SUBLANES, seq // tm, t // tm
    tile, prev, _ = _ffn_specs(tm, fb, n_steps)

    def body(dy_ref, wd_ref, g_ref, gp_ref, v_ref, vp_ref, wg_ref, wv_ref, bg_ref, bv_ref,
             dg_ref, dv_ref, dbg_ref, dbv_ref, dwg_ref, dwv_ref, gwrap_ref, vwrap_ref, cg_ref, cv_ref):
        i = pl.program_id(1)
        first = i == 0
        keep = (i % seq_steps != 0).astype(F32)
        da = lax.dot_general(dy_ref[...].astype(BF16), wd_ref[...], (_DIMS["nt"], ((), ())),
                             preferred_element_type=F32)
        g_tiles = _sub_tiles(g_ref, gp_ref, None, keep, None, n_sub)
        v_tiles = _sub_tiles(v_ref, vp_ref, None, keep, None, n_sub)
        for s in range(n_sub):
            _ffn_conv(g_tiles[s][0], g_tiles[s][1], _rows(gwrap_ref, s, WRAP_ROWS), wg_ref, bg_ref, g_tiles[s][2],
                      seg, _rows(cg_ref, s, TM_EW))
            _ffn_conv(v_tiles[s][0], v_tiles[s][1], _rows(vwrap_ref, s, WRAP_ROWS), wv_ref, bv_ref, v_tiles[s][2],
                      seg, _rows(cv_ref, s, TM_EW))
        cg, cv = cg_ref[...], cv_ref[...]
        sg = _sigmoid(cg)
        dcv = da * cg * sg
        dcg = da * cv * sg * (1.0 + cg * (1.0 - sg))
        dg_ref[...] = dcg
        dv_ref[...] = dcv
        _acc_rows(dbg_ref, _colsum(dcg), first)
        _acc_rows(dbv_ref, _colsum(dcv), first)

        @pl.when(first)
        def _():
            dwg_ref[...] = jnp.zeros_like(dwg_ref)
            dwv_ref[...] = jnp.zeros_like(dwv_ref)

        for s in range(n_sub):
            _conv_bwd_taps(_rows(dg_ref, s, TM_EW), g_tiles[s][0], _rows(gwrap_ref, s, WRAP_ROWS), dwg_ref, seg, FFN_K)
            _conv_bwd_taps(_rows(dv_ref, s, TM_EW), v_tiles[s][0], _rows(vwrap_ref, s, WRAP_ROWS), dwv_ref, seg, FFN_K)

    def vec(shift, rows):
        return pl.BlockSpec((None, None, rows, fb), lambda n, i: (l, n + shift, 0, 0))

    def acc(rows):
        return pl.BlockSpec((None, rows, fb), lambda n, i: (n, 0, 0))

    wrap = pltpu.VMEM((n_sub * WRAP_ROWS, fb), F32)
    tmp = pltpu.VMEM((tm, fb), F32)
    return _call(body, name, (half, n_steps),
                 [pl.BlockSpec((tm, d), lambda n, i: (i, 0)), pl.BlockSpec((None, fb, d), lambda n, i: (n, 0, 0)),
                  tile, prev, tile, prev, vec(0, FFN_K), vec(half, FFN_K), vec(0, 1), vec(half, 1)],
                 [tile, tile, acc(1), acc(1), acc(SUBLANES), acc(SUBLANES)],
                 [_sds((half, t, fb), F32), _sds((half, t, fb), F32), _sds((half, 1, fb), F32),
                  _sds((half, 1, fb), F32), _sds((half, SUBLANES, fb), F32), _sds((half, SUBLANES, fb), F32)],
                 [wrap, wrap, tmp, tmp], deps=deps)(dy, w_down, hg, hg, hv, hv, wdw, wdw, bdw, bdw)


def _ffn_conv_t_dx(name, dcg, dcv, wdw, w_up, res, l, seq):
    half, t, fb = dcg.shape
    nb, d = 2 * half, res.shape[-1]
    tm = min(TM_FFN, seq)
    n_sub, seg, seq_steps, n_steps = tm // TM_EW, TM_EW // SUBLANES, seq // tm, t // tm
    per = tm // WRAP_ROWS

    def body(g_ref, gn_ref, v_ref, vn_ref, w_ref, up_ref, res_ref, dh_ref, db_ref, dx_ref,
             sel_ref, seln_ref, wrap_ref, out_ref, acc_ref):
        i, n = pl.program_id(0), pl.program_id(1)
        keep = (i % seq_steps != seq_steps - 1).astype(F32)

        @pl.when(n < half)
        def _():
            sel_ref[...] = g_ref[...]
            seln_ref[...] = gn_ref[...]

        @pl.when(n >= half)
        def _():
            sel_ref[...] = v_ref[...]
            seln_ref[...] = vn_ref[...]

        for s, (sub, _, _, nx, kn) in enumerate(_sub_tiles(sel_ref, None, seln_ref, None, keep, n_sub)):
            _fill_wrap_next(sub, nx, wrap_ref, FFN_K - 1, kn)
            _conv_bwd_data(sub, wrap_ref, w_ref, _rows(out_ref, s, TM_EW), seg, FFN_K)
        dh = out_ref[...]
        dhb = dh.astype(BF16)
        dh_ref[...] = dhb
        _acc_rows(db_ref.at[n], _colsum(dh), i == 0)
        p = jnp.dot(dhb, up_ref[...], preferred_element_type=F32)

        @pl.when(n == 0)
        def _():
            acc_ref[...] = p

        @pl.when(n > 0)
        def _():
            acc_ref[...] += p

        @pl.when(n == nb - 1)
        def _():
            dx_ref[...] = acc_ref[...] + ALPHA * res_ref[...]

    def src(gate):
        def blk(n):
            return jnp.minimum(n, half - 1) if gate else jnp.maximum(n - half, 0)
        tile = pl.BlockSpec((None, tm, fb), lambda i, n: (blk(n), i, 0))
        nxt = pl.BlockSpec((None, WRAP_ROWS, fb),
                           lambda i, n: (blk(n), jnp.minimum((i + 1) * per, n_steps * per - 1), 0))
        return [tile, nxt]

    row = pl.BlockSpec((tm, d), lambda i, n: (i, 0))
    tmp = pltpu.VMEM((tm, fb), F32)
    halo = pltpu.VMEM((WRAP_ROWS, fb), F32)
    return _call(body, name, (n_steps, nb),
                 src(True) + src(False) +
                 [pl.BlockSpec((None, None, FFN_K, fb), lambda i, n: (l, n, 0, 0)),
                  pl.BlockSpec((None, fb, d), lambda i, n: (n, 0, 0)), row],
                 [pl.BlockSpec((None, tm, fb), lambda i, n: (n, i, 0)),
                  pl.BlockSpec((nb, 1, fb), lambda i, n: (0, 0, 0)), row],
                 [_sds((nb, t, fb), BF16), _sds((nb, 1, fb), F32), _sds((t, d), F32)],
                 [tmp, halo, halo, tmp, pltpu.VMEM((tm, d), F32)])(dcg, dcg, dcv, dcv, wdw, w_up, res)


def _gelu_parts(h):
    cdf = 0.5 * (1.0 + lax.erf(h * INV_SQRT2))
    return h * cdf, cdf


def _seg_axis(a, axis, fn):
    return jnp.moveaxis(fn(jnp.moveaxis(a, axis, 0), TM_EW), 0, axis)


def _sgu_operands(w_s, b_s):
    nl = w_s.shape[0]
    n_sub = TM_EW // CHUNK
    tril = jnp.tril(jnp.ones((CHUNK, CHUNK), dtype=bool))
    w_causal = jnp.where(tril, w_s, 0.0)
    w_tile = (jnp.eye(n_sub, dtype=F32)[None, None, :, None, :, None] * w_causal[:, :, None, :, None, :]).reshape(
        nl, GROUPS, TM_EW, TM_EW)
    w_tile = _seg_axis(_seg_axis(w_tile, 2, _to_segments), 3, _to_segments).astype(BF16)
    bs_tile = jnp.broadcast_to(b_s[:, :, None, :, None], (nl, GROUPS, n_sub, CHUNK, CHUNK)).reshape(
        nl, GROUPS, TM_EW, CHUNK)
    return w_tile, _seg_axis(bs_tile, 2, _to_segments)


def _sgu_param_grads(dwt, dbt):
    n_sub = TM_EW // CHUNK
    tril = jnp.tril(jnp.ones((CHUNK, CHUNK), dtype=bool))
    dwt = _seg_axis(_seg_axis(dwt, 1, _from_segments), 2, _from_segments).reshape(GROUPS, n_sub, CHUNK, n_sub, CHUNK)
    dw = sum(dwt[:, a, :, a, :] for a in range(n_sub))
    db = _seg_axis(dbt, 1, _from_segments).reshape(GROUPS, n_sub, CHUNK).sum(axis=1)
    return jnp.where(tril, dw, 0.0), db


def _sgu(name, h, g3, b3, wt, bst, l):
    t, c2 = h.shape
    c = c2 // 2
    tm = TM_EW

    def body(h_ref, g_ref, b_ref, wt_ref, bs_ref, o_ref):
        z, _ = _gelu_parts(h_ref[...])
        u = z[:, :c]
        xhat, _ = _ln_stats(z[:, c:])
        vnb = (xhat * g_ref[...] + b_ref[...]).astype(BF16)
        for gi in range(GROUPS):
            cs = slice(gi * CHUNK, (gi + 1) * CHUNK)
            sp = jnp.dot(wt_ref[gi], vnb[:, cs], preferred_element_type=F32) + bs_ref[gi]
            o_ref[:, cs] = (u[:, cs] * sp).astype(BF16)

    return _call(body, name, (t // tm,),
                 [_row_spec(tm, c2), _param_spec(l, c), _param_spec(l, c),
                  pl.BlockSpec((None, GROUPS, tm, tm), lambda i: (l, 0, 0, 0)),
                  pl.BlockSpec((None, GROUPS, tm, CHUNK), lambda i: (l, 0, 0, 0))],
                 _row_spec(tm, c), _sds((t, c), BF16))(h, g3, b3, wt, bst)


def _sgu_bwd(name, dq, h, g3, b3, wt, bst, l, deps=()):
    t, c2 = h.shape
    c = c2 // 2
    tm = TM_EW
    n_tiles = t // tm

    def body(dq_ref, h_ref, g_ref, b_ref, wt_ref, bs_ref,
             dh_ref, dbin_ref, dw_ref, dbs_ref, dg_ref, db_ref, du_ref, dvn_ref, bsum_ref):
        i = pl.program_id(0)
        first = i == 0
        hv = h_ref[...]
        z, cdf = _gelu_parts(hv)
        u = z[:, :c]
        xhat, rstd = _ln_stats(z[:, c:])
        g = g_ref[...]
        vnb = (xhat * g + b_ref[...]).astype(BF16)

        @pl.when(first)
        def _():
            dw_ref[...] = jnp.zeros_like(dw_ref)
            bsum_ref[...] = jnp.zeros_like(bsum_ref)

        for gi in range(GROUPS):
            cs = slice(gi * CHUNK, (gi + 1) * CHUNK)
            vb = vnb[:, cs]
            w = wt_ref[gi]
            sp = jnp.dot(w, vb, preferred_element_type=F32) + bs_ref[gi]
            dqb = dq_ref[:, cs]
            du_ref[:, cs] = dqb * sp
            dsp = dqb * u[:, cs]
            bsum_ref[gi] += dsp
            dspb = dsp.astype(BF16)
            dw_ref[gi] += lax.dot_general(dspb, vb, (_DIMS["nt"], ((), ())), preferred_element_type=F32)
            dvn_ref[:, cs] = lax.dot_general(w, dspb, (_DIMS["tn"], ((), ())), preferred_element_type=F32)

        dvn = dvn_ref[...]
        dv = _ln_backward(dvn * g, xhat, rstd)
        pdf = jnp.exp(-0.5 * hv * hv) * INV_SQRT2PI
        dgelu = cdf + hv * pdf
        dhu = du_ref[...] * dgelu[:, :c]
        dhv = dv * dgelu[:, c:]
        dh_ref[:, :c] = dhu.astype(BF16)
        dh_ref[:, c:] = dhv.astype(BF16)
        _acc_rows(dbin_ref.at[:, :c], _colsum(dhu), first)
        _acc_rows(dbin_ref.at[:, c:], _colsum(dhv), first)
        _acc_rows(dg_ref, _colsum(dvn * xhat), first)
        _acc_rows(db_ref, _colsum(dvn), first)

        @pl.when(i == n_tiles - 1)
        def _():
            dbs_ref[...] = jnp.sum(bsum_ref[...], axis=-1)

    vec = pl.BlockSpec((1, c), lambda i: (0, 0))
    return _call(body, name, (n_tiles,),
                 [_row_spec(tm, c), _row_spec(tm, c2), _param_spec(l, c), _param_spec(l, c),
                  pl.BlockSpec((None, GROUPS, tm, tm), lambda i: (l, 0, 0, 0)),
                  pl.BlockSpec((None, GROUPS, tm, CHUNK), lambda i: (l, 0, 0, 0))],
                 [_row_spec(tm, c2), pl.BlockSpec((1, c2), lambda i: (0, 0)),
                  pl.BlockSpec((GROUPS, tm, tm), lambda i: (0, 0, 0)),
                  pl.BlockSpec((GROUPS, tm), lambda i: (0, 0)), vec, vec],
                 [_sds((t, c2), BF16), _sds((1, c2), F32), _sds((GROUPS, tm, tm), F32),
                  _sds((GROUPS, tm), F32), _sds((1, c), F32), _sds((1, c), F32)],
                 [pltpu.VMEM((tm, c), F32), pltpu.VMEM((tm, c), F32), pltpu.VMEM((GROUPS, tm, CHUNK), F32)],
                 deps=deps)(dq, h, g3, b3, wt, bst)


def _loss(name, y, target):
    t, d = y.shape
    tm = min(TM_EW, t)
    n_tiles = t // tm

    def body(y_ref, t_ref, l_ref, dy_ref, acc_ref):
        i = pl.program_id(0)
        diff = y_ref[...] - t_ref[...]
        dy_ref[...] = diff * (1.0 / d)
        _acc_rows(acc_ref, _colsum(diff * diff), i == 0)

        @pl.when(i == n_tiles - 1)
        def _():
            l_ref[...] = jnp.broadcast_to(jnp.sum(acc_ref[...], axis=-1, keepdims=True) * (0.5 / d), (1, LANES))

    return _call(body, name, (n_tiles,), [_row_spec(tm, d), _row_spec(tm, d)],
                 [pl.BlockSpec((1, LANES), lambda i: (0, 0)), _row_spec(tm, d)],
                 [_sds((1, LANES), F32), _sds((t, d), F32)], [pltpu.VMEM((1, d), F32)])(y, target)


def _adamw(g, w, m, v):
    m2 = ADAM_B1 * m + (1.0 - ADAM_B1) * g
    v2 = ADAM_B2 * v + (1.0 - ADAM_B2) * (g * g)
    m_hat = m2 / (1.0 - ADAM_B1 ** ADAM_STEP)
    v_hat = v2 / (1.0 - ADAM_B2 ** ADAM_STEP)
    delta = -ADAM_LR * (m_hat / (jnp.sqrt(v_hat) + ADAM_EPS) + ADAM_WD * w)
    return delta, m2, v2


ROW_TILE_CAP = 512


def _row_tile(rows, cap=ROW_TILE_CAP):
    if rows <= cap:
        return rows
    for tr in range(cap, 15, -16):
        if rows % tr == 0:
            return tr
    return rows


def _sum8_adamw(name, dev, lands, parts, w, m, v):
    nl = len(lands)
    _, r, c = lands[0].shape
    tr = _row_tile(r, cap=128)

    def body(dev_ref, *refs):
        land, own = refs[:nl], refs[nl:2 * nl]
        w_ref, m_ref, v_ref, g_ref, d_ref, m2_ref, v2_ref = refs[2 * nl:]
        layer, me = pl.program_id(0), dev_ref[0]
        for l in range(nl):
            @pl.when(layer == l)
            def _(l=l):
                g = None
                for s in range(N_DEV):
                    part = jnp.where(me == s, own[l][...], land[l][s]).astype(F32)
                    g = part if g is None else g + part
                delta, m2, v2 = _adamw(g, w_ref[...], m_ref[...], v_ref[...])
                g_ref[...] = g
                d_ref[...] = delta
                m2_ref[...] = m2
                v2_ref[...] = v2

    def rows_of(l, a, i):
        return jnp.where(a == l, i, 0)

    spec = pl.BlockSpec((None, tr, c), lambda a, i, dev_ref: (a, i, 0))
    in_specs = [pl.BlockSpec((N_DEV, tr, c), lambda a, i, dev_ref, l=l: (0, rows_of(l, a, i), 0)) for l in range(nl)]
    in_specs += [pl.BlockSpec((None, tr, c), lambda a, i, dev_ref, l=l: (dev_ref[0], rows_of(l, a, i), 0))
                 for l in range(nl)]
    grid_spec = pltpu.PrefetchScalarGridSpec(
        num_scalar_prefetch=1, grid=(nl, r // tr), in_specs=in_specs + [spec] * 3, out_specs=[spec] * 4)
    return pl.pallas_call(
        body, name=name, grid_spec=grid_spec, out_shape=[_sds(w.shape, F32)] * 4,
        compiler_params=pltpu.CompilerParams(vmem_limit_bytes=VMEM_LIMIT))(dev, *lands, *parts, w, m, v)


def _sum8(name, parts):
    _, r, c = parts.shape
    tr = _row_tile(r)

    def body(p_ref, o_ref):
        acc = p_ref[0]
        for s in range(1, N_DEV):
            acc = acc + p_ref[s]
        o_ref[...] = acc

    return _call(body, name, (r // tr,), [pl.BlockSpec((N_DEV, tr, c), lambda i: (0, i, 0))],
                 pl.BlockSpec((tr, c), lambda i: (i, 0)), _sds((r, c), F32))(parts)


def _adamw_flat(name, g, w, m, v):
    r, c = g.shape
    tr = _row_tile(r)

    def body(g_ref, w_ref, m_ref, v_ref, d_ref, m2_ref, v2_ref):
        delta, m2, v2 = _adamw(g_ref[...], w_ref[...], m_ref[...], v_ref[...])
        d_ref[...] = delta
        m2_ref[...] = m2
        v2_ref[...] = v2

    spec = pl.BlockSpec((tr, c), lambda i: (i, 0))
    return _call(body, name, (r // tr,), [spec] * 4, [spec] * 3, [_sds((r, c), F32)] * 3)(g, w, m, v)


def _pack(arrs, row_multiple=SUBLANES):
    pieces, rows = [], 0
    for a in arrs:
        piece = a.reshape(-1, LANES)
        piece = jnp.pad(piece, ((0, (-piece.shape[0]) % SUBLANES), (0, 0)))
        pieces.append(piece)
        rows += piece.shape[0]
    if rows % row_multiple:
        pieces.append(jnp.zeros(((-rows) % row_multiple, LANES), pieces[0].dtype))
    return jnp.concatenate(pieces, axis=0)


def _unpack(buf, shapes, lead=0):
    out, pos = [], 0
    for shp in shapes:
        rows = math.prod(shp) // LANES
        piece = lax.slice_in_dim(buf, pos, pos + rows, axis=lead)
        out.append(piece.reshape(buf.shape[:lead] + tuple(shp)))
        pos += rows + (-rows) % SUBLANES
    return out


REPLICATED = ["conv_b_in", "conv_b_dw", "conv_ln_g", "conv_ln_b", "conv_b_out", "gmlp_w_s", "gmlp_b_s",
              "ffn_b_up", "ffn_b_dw", "ffn_b_down", "norm1_g", "norm1_b", "norm2_g", "norm2_b"]
SMALL_SHARDED = ["conv_w_dw", "gmlp_b_in", "gmlp_ln_g", "gmlp_ln_b", "gmlp_b_out", "ffn_w_dw"]
BIG = ["conv_w_in", "conv_w_out", "gmlp_w_in", "gmlp_w_out", "ffn_w_up", "ffn_w_down"]
WEIGHTS = ["conv_w_in", "conv_b_in", "conv_w_dw", "conv_b_dw", "conv_ln_g", "conv_ln_b", "conv_w_out", "conv_b_out",
           "gmlp_w_in", "gmlp_b_in", "gmlp_ln_g", "gmlp_ln_b", "gmlp_w_s", "gmlp_b_s", "gmlp_w_out", "gmlp_b_out",
           "ffn_w_up", "ffn_b_up", "ffn_w_dw", "ffn_b_dw", "ffn_w_down", "ffn_b_down",
           "norm1_g", "norm1_b", "norm2_g", "norm2_b"]


def _from_shards(g, lead_shape):
    nd = len(lead_shape)
    perm = tuple(range(1, nd + 1)) + (0, nd + 1)
    return g.transpose(perm).reshape(tuple(lead_shape) + (-1,))


def _to_shards(full, width):
    lead = full.shape[:-1]
    nd = len(lead)
    parts = full.reshape(lead + (N_DEV, width))
    return parts.transpose((nd,) + tuple(range(nd)) + (nd + 1,))


def _step(p):
    x_in, target_in = p["x"], p["loss_target"]
    bsz, seq, d = x_in.shape
    t = bsz * seq
    assert seq % TM_EW == 0 and TM_EW % CHUNK == 0 and TM_EW // SUBLANES >= CONV_K - 1
    x0 = _to_segments(x_in.reshape(t, d), TM_EW)
    target = _to_segments(target_in.reshape(t, d), TM_EW)
    n_conv, n_gmlp = p["conv_w_in"].shape[0], p["gmlp_w_in"].shape[0]
    fb = p["ffn_w_up"].shape[-1]
    nblk = N_DEV
    half = nblk // 2
    cw = p["conv_w_in"].shape[-1]
    tm = min(TM_MM, t)
    tm_ln = min(TM_LN, t)
    nt = t // tm
    dev = 4 * lax.axis_index("x") + 2 * lax.axis_index("y") + lax.axis_index("c")

    small_shapes = [p[n].shape for n in SMALL_SHARDED]
    small_src = _pack([p[n] for n in SMALL_SHARDED])[None]
    small_all = _all_gather("gather_small_weights", [small_src])[0][0]
    sm = _unpack(small_all, small_shapes, lead=1)
    w_src = []
    for i in range(DEPTH):
        mix = "conv" if i % 2 == 0 else "gmlp"
        w_src += [p[mix + "_w_in"][i // 2].astype(BF16), p[mix + "_w_out"][i // 2].astype(BF16),
                  p["ffn_w_up"][i].T.astype(BF16), p["ffn_w_down"][i].astype(BF16)]
    send_sems, recv_sems, w_land = _gather_start(
        "weights_gather_start", _place_own("weights_place_own", w_src, deps=[small_all]))
    W_IN, W_OUT, W_UP, W_DOWN = range(4)

    def wait_weight(i, k, after):
        return _gather_wait(f"l{i}_weights_wait{k}", w_land[4 * i + k], send_sems, recv_sems, 4 * i + k, after)
    conv_w_dw = _from_shards(sm[0], sm[0].shape[1:-1])
    gmlp_b_in = _from_shards(sm[1], sm[1].shape[1:-1])
    gmlp_ln_g = _from_shards(sm[2], sm[2].shape[1:-1])
    gmlp_ln_b = _from_shards(sm[3], sm[3].shape[1:-1])
    gmlp_b_out = _from_shards(sm[4], sm[4].shape[1:-1])
    ffn_w_dw = sm[5].transpose(1, 0, 2, 3)

    def rows3(a):
        return a.reshape(a.shape[0], 1, a.shape[-1])

    conv_b_in4 = p["conv_b_in"].reshape(n_conv, N_DEV, 1, cw)
    gmlp_b_in4 = gmlp_b_in.reshape(n_gmlp, N_DEV, 1, cw)
    ffn_b_up4 = p["ffn_b_up"].reshape(DEPTH, nblk, 1, fb)
    ffn_b_dw4 = p["ffn_b_dw"].reshape(DEPTH, nblk, 1, fb)
    conv_b_dw3, conv_ln_g3, conv_ln_b3 = rows3(p["conv_b_dw"]), rows3(p["conv_ln_g"]), rows3(p["conv_ln_b"])
    conv_b_out3, gmlp_b_out3, ffn_b_down3 = rows3(p["conv_b_out"]), rows3(gmlp_b_out), rows3(p["ffn_b_down"])
    gmlp_ln_g3, gmlp_ln_b3 = rows3(gmlp_ln_g), rows3(gmlp_ln_b)
    n1g3, n1b3, n2g3, n2b3 = rows3(p["norm1_g"]), rows3(p["norm1_b"]), rows3(p["norm2_g"]), rows3(p["norm2_b"])
    w_tile, bs_tile = _sgu_operands(p["gmlp_w_s"], p["gmlp_b_s"])

    def mm_in(name, xa, wg, l, bias4):
        return _matmul(name, xa, wg, "nn", grid=(nt, N_DEV),
                       a_spec=pl.BlockSpec((tm, d), lambda i, n: (i, 0)),
                       b_spec=pl.BlockSpec((None, d, cw), lambda i, n: (n, 0, 0)),
                       o_spec=pl.BlockSpec((tm, cw), lambda i, n: (i, n)), o_shape=(t, N_DEV * cw), o_dtype=F32,
                       bias=bias4, bias_spec=pl.BlockSpec((None, None, 1, cw), lambda i, n: (l, n, 0, 0)))

    def mm_out_dx(name, dy, w):
        return _matmul(name, dy, w, "nt", grid=(nt,),
                       a_spec=pl.BlockSpec((tm, d), lambda i: (i, 0)),
                       b_spec=pl.BlockSpec((d, d), lambda i: (0, 0)),
                       o_spec=pl.BlockSpec((tm, d), lambda i: (i, 0)), o_shape=(t, d), o_dtype=F32)

    def mm_out_dw(name, sa, dy):
        return _matmul(name, sa, dy, "tn", grid=(nt,), k_axis=0, nk=nt, acc_shape=(d, d),
                       a_spec=pl.BlockSpec((tm, d), lambda k: (k, 0)),
                       b_spec=pl.BlockSpec((tm, d), lambda k: (k, 0)),
                       o_spec=pl.BlockSpec((d, d), lambda k: (0, 0)), o_shape=(d, d), o_dtype=BF16)

    def mm_in_dx(name, dh, wg, res):
        return _matmul(name, dh, wg, "nt", grid=(nt, N_DEV), k_axis=1, nk=N_DEV, acc_shape=(tm, d),
                       a_spec=pl.BlockSpec((tm, cw), lambda i, n: (i, n)),
                       b_spec=pl.BlockSpec((None, d, cw), lambda i, n: (n, 0, 0)),
                       o_spec=pl.BlockSpec((tm, d), lambda i, n: (i, 0)), o_shape=(t, d), o_dtype=F32,
                       res=res, res_spec=pl.BlockSpec((tm, d), lambda i, n: (i, 0)), res_scale=ALPHA)

    def mm_in_dw(name, xa, dh):
        return _matmul(name, xa, dh, "tn", grid=(N_DEV, nt), k_axis=1, nk=nt, acc_shape=(d, cw),
                       a_spec=pl.BlockSpec((tm, d), lambda n, k: (k, 0)),
                       b_spec=pl.BlockSpec((tm, cw), lambda n, k: (k, n)),
                       o_spec=pl.BlockSpec((None, d, cw), lambda n, k: (n, 0, 0)),
                       o_shape=(N_DEV, d, cw), o_dtype=BF16)

    def mm_down_dw(name, a, dy):
        return _matmul(name, a, dy, "tn", grid=(half, nt), k_axis=1, nk=nt, acc_shape=(fb, d),
                       a_spec=pl.BlockSpec((None, tm, fb), lambda n, k: (n, k, 0)),
                       b_spec=pl.BlockSpec((tm, d), lambda n, k: (k, 0)),
                       o_spec=pl.BlockSpec((None, fb, d), lambda n, k: (n, 0, 0)),
                       o_shape=(half, fb, d), o_dtype=BF16)

    def mm_up_dw(name, xa, dh):
        return _matmul(name, dh, xa, "tn", grid=(nblk, nt), k_axis=1, nk=nt, acc_shape=(fb, d),
                       a_spec=pl.BlockSpec((None, tm, fb), lambda n, k: (n, k, 0)),
                       b_spec=pl.BlockSpec((tm, d), lambda n, k: (k, 0)),
                       o_spec=pl.BlockSpec((None, fb, d), lambda n, k: (n, 0, 0)),
                       o_shape=(nblk, fb, d), o_dtype=BF16)

    saved = []
    xcur = x0
    for i in range(DEPTH):
        j = i // 2
        s = {"x": xcur}
        s["w_in"] = wait_weight(i, W_IN, xcur)
        if i % 2 == 0:
            s["h"] = mm_in(f"l{i}_conv_in", xcur, s["w_in"], j, conv_b_in4)
            s["u"] = _glu(f"l{i}_glu", s["h"])
            s["c"] = _dwconv31(f"l{i}_dwconv", s["u"], conv_w_dw, conv_b_dw3, j, seq)
            s["s"] = _ln_silu(f"l{i}_ln_silu", s["c"], conv_ln_g3, conv_ln_b3, j)
            b_out3 = conv_b_out3
        else:
            s["h"] = mm_in(f"l{i}_gmlp_in", xcur, s["w_in"], j, gmlp_b_in4)
            s["s"] = _sgu(f"l{i}_sgu", s["h"], gmlp_ln_g3, gmlp_ln_b3, w_tile, bs_tile, j)
            b_out3 = gmlp_b_out3
        s["w_out"] = wait_weight(i, W_OUT, s["s"]).reshape(d, d)
        s["x1"], s["xhat1"], s["rstd1"] = _matmul_ln(
            f"l{i}_mixer_out_norm1", s["s"], s["w_out"], xcur, b_out3, n1g3, n1b3, j, i,
            nk=1, a_block=(tm_ln, d), b_block=(d, d))
        s["w_up"] = wait_weight(i, W_UP, s["x1"])
        s["hg"], s["hv"], s["a"] = _ffn_up_act(f"l{i}_ffn_up_act", s["x1"], s["w_up"], ffn_b_up4, ffn_w_dw, ffn_b_dw4,
                                               i, seq)
        s["w_down"] = wait_weight(i, W_DOWN, s["a"]).reshape(half, fb, d)
        xcur, s["xhat2"], s["rstd2"] = _matmul_ln(
            f"l{i}_ffn_down_norm2", s["a"], s["w_down"], s["x1"], ffn_b_down3, n2g3, n2b3, i, i,
            nk=half, a_block=(None, tm_ln, fb), b_block=(None, fb, d))
        saved.append(s)

    loss_row, dx = _loss("loss", xcur, target)

    started = {n: [None] * p[n].shape[0] for n in BIG}
    tokens = []

    def send_grad(n, l, g):
        send_sem, recv_sem, parts, land, token = _scatter_start(f"grad_{n}{l}_scatter_start", g)
        started[n][l] = (send_sem, recv_sem, parts, land)
        tokens.append(token)

    def take_tokens():
        out = list(tokens)
        tokens.clear()
        return out

    gl = {n: [None] * p[n].shape[0] for n in REPLICATED + SMALL_SHARDED}
    for i in reversed(range(DEPTH)):
        j = i // 2
        s = saved[i]
        mix = "conv" if i % 2 == 0 else "gmlp"
        dr2, gl["norm2_g"][i], gl["norm2_b"][i], gl["ffn_b_down"][i] = _ln_res_bwd(
            f"l{i}_norm2_bwd", dx, s["xhat2"], s["rstd2"], n2g3, i, deps=take_tokens())
        send_grad("ffn_w_down", i, mm_down_dw(f"l{i}_ffn_down_dw", s["a"], dr2).reshape(N_DEV, -1, d))
        dcg, dcv, dbg, dbv, dwg, dwv = _ffn_act_bwd(f"l{i}_ffn_act_bwd", dr2, s["w_down"], s["hg"], s["hv"],
                                                    ffn_w_dw, ffn_b_dw4, i, seq, deps=take_tokens())
        gl["ffn_b_dw"][i] = jnp.concatenate([dbg, dbv], axis=0).reshape(1, nblk * fb)
        gl["ffn_w_dw"][i] = jnp.concatenate([dwg[:, :FFN_K], dwv[:, :FFN_K]], axis=0)
        dh, dbu, dx1 = _ffn_conv_t_dx(f"l{i}_ffn_conv_t_dx", dcg, dcv, ffn_w_dw, s["w_up"], dr2, i, seq)
        gl["ffn_b_up"][i] = dbu.reshape(1, nblk * fb)
        send_grad("ffn_w_up", i, mm_up_dw(f"l{i}_ffn_up_dw", s["x1"], dh))
        dr1, gl["norm1_g"][i], gl["norm1_b"][i], gl[mix + "_b_out"][j] = _ln_res_bwd(
            f"l{i}_norm1_bwd", dx1, s["xhat1"], s["rstd1"], n1g3, i, deps=take_tokens())
        ds = mm_out_dx(f"l{i}_{mix}_out_dx", dr1, s["w_out"])
        send_grad(mix + "_w_out", j, mm_out_dw(f"l{i}_{mix}_out_dw", s["s"], dr1).reshape(N_DEV, -1, d))
        if i % 2 == 0:
            dc, gl["conv_ln_g"][j], gl["conv_ln_b"][j], gl["conv_b_dw"][j] = _ln_silu_bwd(
                f"l{i}_ln_silu_bwd", ds, s["c"], conv_ln_g3, conv_ln_b3, j, deps=take_tokens())
            du, dwdw = _dwconv31_bwd(f"l{i}_dwconv_bwd", dc, s["u"], conv_w_dw, j, seq)
            gl["conv_w_dw"][j] = dwdw[:CONV_K]
            dh, gl["conv_b_in"][j] = _glu_bwd(f"l{i}_glu_bwd", du, s["h"])
        else:
            dh, gl["gmlp_b_in"][j], dwt, dbt, gl["gmlp_ln_g"][j], gl["gmlp_ln_b"][j] = _sgu_bwd(
                f"l{i}_sgu_bwd", ds, s["h"], gmlp_ln_g3, gmlp_ln_b3, w_tile, bs_tile, j, deps=take_tokens())
            gl["gmlp_w_s"][j], gl["gmlp_b_s"][j] = _sgu_param_grads(dwt, dbt)
        dx = mm_in_dx(f"l{i}_{mix}_in_dx", dh, s["w_in"], dr1)
        send_grad(mix + "_w_in", j, mm_in_dw(f"l{i}_{mix}_in_dw", s["x"], dh))
    grad_x = _from_segments(dx, TM_EW).reshape(bsz, seq, d)

    full_small = {n: jnp.stack(gl[n]).reshape(p[n].shape) for n in REPLICATED}
    shard_small = {}
    for n in SMALL_SHARDED:
        if n == "ffn_w_dw":
            shard_small[n] = jnp.stack(gl[n]).transpose(1, 0, 2, 3)
        else:
            width = p[n].shape[-1]
            lead = p[n].shape[:-1]
            shard_small[n] = _to_shards(jnp.stack(gl[n]).reshape(lead + (N_DEV * width,)), width)
    flat_shapes = [(1, LANES)] + [p[n].shape for n in REPLICATED] + [(N_DEV,) + p[n].shape for n in SMALL_SHARDED]
    flat_local = _pack([loss_row] + [full_small[n] for n in REPLICATED] + [shard_small[n] for n in SMALL_SHARDED],
                       row_multiple=ROW_TILE_CAP)

    small_parts = _all_gather("gather_small_grads", [flat_local[None]])[0][0]
    small_sum = _sum8("sum_small_grads", small_parts)
    summed = _unpack(small_sum, flat_shapes)
    loss = summed[0][0, 0]
    grads = dict(zip(REPLICATED, summed[1:1 + len(REPLICATED)]))
    for n, g in zip(SMALL_SHARDED, summed[1 + len(REPLICATED):]):
        grads[n] = lax.dynamic_index_in_dim(g, dev, axis=0, keepdims=False)

    flat_started = [st for n in BIG for st in started[n]]
    parts_done, lands_done = _scatter_wait("grads_scatter_wait", flat_started, small_sum)

    delta, new_m, new_v = {}, {}, {}
    dev1 = jnp.reshape(dev, (1,)).astype(jnp.int32)
    pos = 0
    for n in BIG:
        nl = p[n].shape[0]
        state = [p[n], p["m_" + n], p["v_" + n]]
        if n == "ffn_w_up":
            state = [a.transpose(0, 2, 1) for a in state]
        outs = _sum8_adamw(f"adamw_{n}", dev1, lands_done[pos:pos + nl], parts_done[pos:pos + nl], *state)
        if n == "ffn_w_up":
            outs = [a.transpose(0, 2, 1) for a in outs]
        grads[n], delta[n], new_m[n], new_v[n] = outs
        pos += nl
    small = REPLICATED + SMALL_SHARDED
    small_shp = [p[n].shape for n in small]
    d_s, m_s, v_s = _adamw_flat("adamw_small", _pack([grads[n] for n in small]), _pack([p[n] for n in small]),
                                _pack([p["m_" + n] for n in small]), _pack([p["v_" + n] for n in small]))
    for n, dd, mm, vv in zip(small, _unpack(d_s, small_shp), _unpack(m_s, small_shp), _unpack(v_s, small_shp)):
        delta[n], new_m[n], new_v[n] = dd, mm, vv

    return (loss, grad_x, *[grads[n] for n in WEIGHTS], *[delta[n] for n in WEIGHTS],
            *[new_m[n] for n in WEIGHTS], *[new_v[n] for n in WEIGHTS])


def kernel(x, conv_w_in, conv_b_in, conv_w_dw, conv_b_dw, conv_ln_g, conv_ln_b, conv_w_out, conv_b_out, gmlp_w_in, gmlp_b_in, gmlp_ln_g, gmlp_ln_b, gmlp_w_s, gmlp_b_s, gmlp_w_out, gmlp_b_out, ffn_w_up, ffn_b_up, ffn_w_dw, ffn_b_dw, ffn_w_down, ffn_b_down, norm1_g, norm1_b, norm2_g, norm2_b, loss_target, m_conv_w_in, m_conv_b_in, m_conv_w_dw, m_conv_b_dw, m_conv_ln_g, m_conv_ln_b, m_conv_w_out, m_conv_b_out, m_gmlp_w_in, m_gmlp_b_in, m_gmlp_ln_g, m_gmlp_ln_b, m_gmlp_w_s, m_gmlp_b_s, m_gmlp_w_out, m_gmlp_b_out, m_ffn_w_up, m_ffn_b_up, m_ffn_w_dw, m_ffn_b_dw, m_ffn_w_down, m_ffn_b_down, m_norm1_g, m_norm1_b, m_norm2_g, m_norm2_b, v_conv_w_in, v_conv_b_in, v_conv_w_dw, v_conv_b_dw, v_conv_ln_g, v_conv_ln_b, v_conv_w_out, v_conv_b_out, v_gmlp_w_in, v_gmlp_b_in, v_gmlp_ln_g, v_gmlp_ln_b, v_gmlp_w_s, v_gmlp_b_s, v_gmlp_w_out, v_gmlp_b_out, v_ffn_w_up, v_ffn_b_up, v_ffn_w_dw, v_ffn_b_dw, v_ffn_w_down, v_ffn_b_down, v_norm1_g, v_norm1_b, v_norm2_g, v_norm2_b):
    return _step(dict(locals()))
```

```python
import math

import jax
import jax.numpy as jnp
from jax import lax
from jax.experimental import pallas as pl
from jax.experimental.pallas import tpu as pltpu

F32 = jnp.float32
BF16 = jnp.bfloat16
MESH = pl.DeviceIdType.MESH

N_DEV = 8
DEPTH = 4
ALPHA = (2.0 * DEPTH) ** 0.25
LN_EPS = 1e-5
CONV_K = 31
FFN_K = 3
CHUNK = 128
GROUPS = 8
ADAM_LR = 0.001
ADAM_B1 = 0.9
ADAM_B2 = 0.999
ADAM_EPS = 1e-08
ADAM_WD = 0.01
ADAM_STEP = 10
INV_SQRT2 = 1.0 / math.sqrt(2.0)
INV_SQRT2PI = 1.0 / math.sqrt(2.0 * math.pi)

LANES = 128
SUBLANES = 8
VMEM_LIMIT = 48 * 1024 * 1024
TM_MM = 1024
TM_EW = 256


def _call(body, name, grid, in_specs, out_specs, out_shape, scratch=(), aliases=None, deps=()):
    deps = list(deps)
    in_specs = list(in_specs)
    n_in = len(in_specs)
    if deps:
        inner = body

        def body(*refs):
            return inner(*refs[:n_in], *refs[n_in + len(deps):])

        in_specs = in_specs + [pl.BlockSpec(memory_space=pl.ANY)] * len(deps)
    fn = pl.pallas_call(
        body, name=name, grid=grid, in_specs=in_specs, out_specs=out_specs, out_shape=out_shape,
        scratch_shapes=list(scratch), input_output_aliases=aliases or {},
        compiler_params=pltpu.CompilerParams(vmem_limit_bytes=VMEM_LIMIT))
    return lambda *args: fn(*args, *deps)


def _sds(shape, dtype):
    return jax.ShapeDtypeStruct(tuple(shape), dtype)


def _sigmoid(x):
    return 1.0 / (1.0 + jnp.exp(-x))


def _acc_rows(ref, val, first):
    @pl.when(first)
    def _():
        ref[...] = val

    @pl.when(jnp.logical_not(first))
    def _():
        ref[...] += val


def _colsum(v):
    return jnp.sum(v, axis=0, keepdims=True)


_DIMS = {"nn": ((1,), (0,)), "nt": ((1,), (1,)), "tn": ((0,), (0,))}


def _matmul(name, a, b, mode, *, grid, a_spec, b_spec, o_spec, o_shape, o_dtype, k_axis=None, nk=1,
            acc_shape=None, bias=None, bias_spec=None, res=None, res_spec=None, res_scale=1.0, deps=()):
    dims = (_DIMS[mode], ((), ()))
    has_bias, has_res = bias is not None, res is not None

    def body(*refs):
        a_ref, b_ref = refs[0], refs[1]
        pos = 2
        bias_ref = res_ref = None
        if has_bias:
            bias_ref = refs[pos]
            pos += 1
        if has_res:
            res_ref = refs[pos]
            pos += 1
        o_ref = refs[pos]
        acc_ref = refs[pos + 1] if nk > 1 else None
        p = lax.dot_general(a_ref[...].astype(BF16), b_ref[...].astype(BF16), dims, preferred_element_type=F32)

        def finish(acc):
            if has_bias:
                acc = acc + bias_ref[...]
            if has_res:
                acc = acc + res_scale * res_ref[...]
            o_ref[...] = acc.astype(o_dtype)

        if nk == 1:
            finish(p)
        else:
            k = pl.program_id(k_axis)

            @pl.when(k == 0)
            def _():
                acc_ref[...] = p

            @pl.when(k > 0)
            def _():
                acc_ref[...] += p

            @pl.when(k == nk - 1)
            def _():
                finish(acc_ref[...])

    ins, specs = [a, b], [a_spec, b_spec]
    if has_bias:
        ins.append(bias)
        specs.append(bias_spec)
    if has_res:
        ins.append(res)
        specs.append(res_spec)
    scratch = [pltpu.VMEM(acc_shape, F32)] if nk > 1 else []
    return _call(body, name, grid, specs, o_spec, _sds(o_shape, o_dtype), scratch, deps=deps)(*ins)


TM_LN = 512


def _matmul_ln(name, a, b, x_res, bias3, g3, b3, l_bias, l_norm, *, nk, a_block, b_block):
    t, d = x_res.shape
    tm = min(TM_LN, t)

    def body(a_ref, b_ref, x_ref, bias_ref, g_ref, be_ref, o_ref, xh_ref, rs_ref, *acc):
        p = jnp.dot(a_ref[...].astype(BF16), b_ref[...], preferred_element_type=F32)

        def finish(y):
            xhat, rstd = _ln_stats(ALPHA * x_ref[...] + y + bias_ref[...])
            o_ref[...] = xhat * g_ref[...] + be_ref[...]
            xh_ref[...] = xhat
            rs_ref[...] = rstd

        if nk == 1:
            finish(p)
        else:
            k = pl.program_id(1)

            @pl.when(k == 0)
            def _():
                acc[0][...] = p

            @pl.when(k > 0)
            def _():
                acc[0][...] += p

            @pl.when(k == nk - 1)
            def _():
                finish(acc[0][...])

    if nk == 1:
        grid = (t // tm,)
        a_spec = pl.BlockSpec(a_block, lambda i: (i, 0))
        b_spec = pl.BlockSpec(b_block, lambda i: (0, 0))
    else:
        grid = (t // tm, nk)
        a_spec = pl.BlockSpec(a_block, lambda i, k: (k, i, 0))
        b_spec = pl.BlockSpec(b_block, lambda i, k: (k, 0, 0))
    row = pl.BlockSpec((tm, d), lambda i, *_: (i, 0))
    stat = pl.BlockSpec((tm, 1), lambda i, *_: (i, 0))

    def vec(l):
        return pl.BlockSpec((None, 1, d), lambda *_: (l, 0, 0))

    return _call(body, name, grid, [a_spec, b_spec, row, vec(l_bias), vec(l_norm), vec(l_norm)], [row, row, stat],
                 [_sds((t, d), F32), _sds((t, d), F32), _sds((t, 1), F32)],
                 [pltpu.VMEM((tm, d), F32)] if nk > 1 else [])(a, b, x_res, bias3, g3, b3)


def _mesh_pos():
    return lax.axis_index("x"), lax.axis_index("y"), lax.axis_index("c")


def _any_specs(n):
    return [pl.BlockSpec(memory_space=pl.ANY)] * n


def _all_gather(name, srcs):
    n = len(srcs)

    def body(*refs):
        src, out = refs[:n], refs[n:2 * n]
        send_sems, recv_sems, local_sems = refs[2 * n:]
        x, y, c = _mesh_pos()
        me, sibling = (x, y, c), (x, y, 1 - c)
        chips = [(1 - x, y), (x, 1 - y), (1 - x, 1 - y)]

        def slot(k, p):
            return out[k].at[:, 4 * p[0] + 2 * p[1] + p[2]]

        def copy(k, idx, block, to, s=None):
            return pltpu.make_async_remote_copy(
                src_ref=slot(k, block) if s is None else s, dst_ref=slot(k, block),
                send_sem=send_sems.at[k * 7 + idx], recv_sem=recv_sems.at[k * 7 + idx],
                device_id=to, device_id_type=MESH)

        local = [pltpu.make_async_copy(src[k], slot(k, me), local_sems.at[k]) for k in range(n)]
        for cp in local:
            cp.start()
        first = []
        for k in range(n):
            first.append(copy(k, 0, me, sibling, src[k]))
            for j, chip in enumerate(chips):
                first.append(copy(k, 1 + j, me, (*chip, c), src[k]))
        for cp in first:
            cp.start()
        passed = []
        for j, chip in enumerate(chips):
            for k in range(n):
                copy(k, 1 + j, (*chip, c), me).wait_recv()
                cp = copy(k, 4 + j, (*chip, c), sibling)
                cp.start()
                passed.append(cp)
        for k in range(n):
            copy(k, 0, sibling, me).wait_recv()
            for j, chip in enumerate(chips):
                copy(k, 4 + j, (*chip, 1 - c), me).wait_recv()
        for cp in first + passed:
            cp.wait_send()
        for cp in local:
            cp.wait()

    out_shape = [_sds((s.shape[0], N_DEV) + s.shape[1:], s.dtype) for s in srcs]
    return _call(body, name, (), [pl.BlockSpec(memory_space=pltpu.VMEM)] * n, _any_specs(n), out_shape,
                 [pltpu.SemaphoreType.DMA((7 * n,)), pltpu.SemaphoreType.DMA((7 * n,)),
                  pltpu.SemaphoreType.DMA((n,))])(*srcs)


HBM_SPEC = pl.BlockSpec(memory_space=pltpu.HBM)
SEM_SPEC = pl.BlockSpec(memory_space=pltpu.SEMAPHORE)
N_PEER = N_DEV - 1


def _split_call(body, name, in_specs, out_specs, out_shape, aliases):
    return pl.pallas_call(
        body, name=name, in_specs=in_specs, out_specs=out_specs, out_shape=out_shape, input_output_aliases=aliases,
        compiler_params=pltpu.CompilerParams(has_side_effects=pltpu.SideEffectType.DATAFLOW_SIDE_EFFECTING))


def _peers(x, y, c):
    return [(1 - x if q & 4 else x, 1 - y if q & 2 else y, 1 - c if q & 1 else c) for q in range(1, N_DEV)]


def _in_hbm(a):
    return pltpu.with_memory_space_constraint(a, pltpu.HBM)


def _place_own(name, srcs, deps=()):
    n = len(srcs)

    def body(*refs):
        src, out, sems = refs[:n], refs[n:2 * n], refs[2 * n]
        x, y, c = _mesh_pos()
        dev = 4 * x + 2 * y + c
        copies = [pltpu.make_async_copy(src[k], out[k].at[dev], sems.at[k]) for k in range(n)]
        for cp in copies:
            cp.start()
        for cp in copies:
            cp.wait()

    return _call(body, name, (), [pl.BlockSpec(memory_space=pltpu.VMEM)] * n, _any_specs(n),
                 [_sds((N_DEV,) + s.shape, s.dtype) for s in srcs], [pltpu.SemaphoreType.DMA((n,))],
                 deps=deps)(*srcs)


def _gather_start(name, lands):
    n = len(lands)

    def body(*refs):
        land, send_sems, recv_sems = refs[:n], refs[n], refs[n + 1]
        x, y, c = _mesh_pos()
        dev = 4 * x + 2 * y + c
        for k in range(n):
            for peer in _peers(x, y, c):
                pltpu.make_async_remote_copy(
                    src_ref=land[k].at[dev], dst_ref=land[k].at[dev], send_sem=send_sems.at[k],
                    recv_sem=recv_sems.at[k], device_id=peer, device_id_type=MESH).start()
        token = refs[-1]
        token[...] = jnp.zeros_like(token)

    outs = _split_call(
        body, name, [HBM_SPEC] * n, [SEM_SPEC, SEM_SPEC] + [HBM_SPEC] * n + [pl.BlockSpec(memory_space=pltpu.VMEM)],
        [pltpu.SemaphoreType.DMA((n,)), pltpu.SemaphoreType.DMA((n,))] + [pltpu.HBM(a.shape, a.dtype) for a in lands]
        + [_sds((SUBLANES, LANES), F32)],
        {k: 2 + k for k in range(n)})(*[_in_hbm(a) for a in lands])
    return outs[0], outs[1], list(outs[2:2 + n]), outs[-1]


def _wait_seven(src_ref, dst_ref, send_sem, recv_sem):
    cp = pltpu.make_async_remote_copy(
        src_ref=src_ref.at[pl.ds(0, N_PEER)], dst_ref=dst_ref.at[pl.ds(0, N_PEER)], send_sem=send_sem,
        recv_sem=recv_sem, device_id=_mesh_pos(), device_id_type=MESH)
    cp.wait_send()
    cp.wait_recv()


def _gather_wait(name, land, send_sems, recv_sems, k, after):
    def body(land_ref, send_ref, recv_ref, after_ref, out_ref):
        _wait_seven(land_ref, land_ref, send_ref.at[k], recv_ref.at[k])

    return _split_call(body, name, [HBM_SPEC, SEM_SPEC, SEM_SPEC, pl.BlockSpec(memory_space=pl.ANY)], HBM_SPEC,
                       pltpu.HBM(land.shape, land.dtype), {0: 0})(land, send_sems, recv_sems, after)


def _scatter_start(name, parts_list):
    n = len(parts_list)

    def body(*refs):
        x, y, c = _mesh_pos()
        dev = 4 * x + 2 * y + c
        for k in range(n):
            parts_ref, land_ref = refs[2 * k], refs[2 * k + 1]
            send_sem, recv_sem = refs[2 * n + 4 * k], refs[2 * n + 4 * k + 1]
            for peer in _peers(x, y, c):
                pltpu.make_async_remote_copy(
                    src_ref=parts_ref.at[4 * peer[0] + 2 * peer[1] + peer[2]], dst_ref=land_ref.at[dev],
                    send_sem=send_sem, recv_sem=recv_sem, device_id=peer, device_id_type=MESH).start()
        token = refs[-1]
        token[...] = jnp.zeros_like(token)

    ins, out_specs, out_shape, aliases = [], [], [], {}
    for k, parts in enumerate(parts_list):
        buf = pltpu.HBM(parts.shape, parts.dtype)
        ins += [_in_hbm(parts), _in_hbm(lax.empty(parts.shape, parts.dtype))]
        out_specs += [SEM_SPEC, SEM_SPEC, HBM_SPEC, HBM_SPEC]
        out_shape += [pltpu.SemaphoreType.DMA(()), pltpu.SemaphoreType.DMA(()), buf, buf]
        aliases.update({2 * k: 4 * k + 2, 2 * k + 1: 4 * k + 3})
    outs = _split_call(body, name, [HBM_SPEC] * (2 * n), out_specs + [pl.BlockSpec(memory_space=pltpu.VMEM)],
                       out_shape + [_sds((SUBLANES, LANES), F32)], aliases)(*ins)
    return [tuple(outs[4 * k:4 * k + 4]) for k in range(n)], outs[-1]


def _scatter_wait(name, started, after):
    n = len(started)

    def body(*refs):
        for k in range(n):
            send_sem, recv_sem, parts_ref, land_ref = refs[4 * k:4 * k + 4]
            _wait_seven(parts_ref, land_ref, send_sem, recv_sem)

    flat = [a for s in started for a in s]
    outs = _split_call(
        body, name, [SEM_SPEC, SEM_SPEC, HBM_SPEC, HBM_SPEC] * n + [pl.BlockSpec(memory_space=pl.ANY)],
        [HBM_SPEC, HBM_SPEC] * n, [pltpu.HBM(a.shape, a.dtype) for s in started for a in s[2:]],
        {4 * k + 2 + t: 2 * k + t for k in range(n) for t in range(2)})(*flat, after)
    return list(outs[0::2]), list(outs[1::2])


def _to_segments(a, tile):
    seg = tile // SUBLANES
    return a.reshape((a.shape[0] // tile, SUBLANES, seg) + a.shape[1:]).swapaxes(1, 2).reshape(a.shape)


def _from_segments(a, tile):
    seg = tile // SUBLANES
    return a.reshape((a.shape[0] // tile, seg, SUBLANES) + a.shape[1:]).swapaxes(1, 2).reshape(a.shape)


def _chunk(ref, q):
    return ref[q * SUBLANES:(q + 1) * SUBLANES, :]


def _fill_wrap_prev(x_ref, halo_ref, wrap_ref, n_wrap, n_halo, seg, keep):
    sub = lax.broadcasted_iota(jnp.int32, (SUBLANES, x_ref.shape[-1]), 0)
    for j in range(n_wrap):
        q = seg - n_wrap + j
        hq = q - (seg - n_halo)
        row = halo_ref[hq * SUBLANES + SUBLANES - 1:(hq + 1) * SUBLANES, :] * keep
        wrap_ref[j * SUBLANES:(j + 1) * SUBLANES, :] = jnp.where(sub == 0, row, pltpu.roll(_chunk(x_ref, q), 1, 0))


def _fill_wrap_next(x_ref, halo_ref, wrap_ref, n_wrap, keep):
    sub = lax.broadcasted_iota(jnp.int32, (SUBLANES, x_ref.shape[-1]), 0)
    for j in range(n_wrap):
        row = halo_ref[j * SUBLANES:j * SUBLANES + 1, :] * keep
        wrap_ref[j * SUBLANES:(j + 1) * SUBLANES, :] = jnp.where(
            sub == SUBLANES - 1, row, pltpu.roll(_chunk(x_ref, j), SUBLANES - 1, 0))


def _past(x_ref, wrap_ref, q, d, n_wrap):
    return _chunk(x_ref, q - d) if q >= d else _chunk(wrap_ref, q - d + n_wrap)


def _future(x_ref, wrap_ref, q, d, seg):
    return _chunk(x_ref, q + d) if q + d < seg else _chunk(wrap_ref, q + d - seg)


def _conv_fwd(x_ref, wrap_ref, w_ref, b_ref, out_ref, seg, k_taps):
    bias = jnp.broadcast_to(b_ref[...], (SUBLANES, x_ref.shape[-1]))
    for q in range(seg):
        acc = bias
        for k in range(k_taps):
            acc = acc + w_ref[k:k + 1, :] * _past(x_ref, wrap_ref, q, k_taps - 1 - k, k_taps - 1)
        out_ref[q * SUBLANES:(q + 1) * SUBLANES, :] = acc


def _conv_bwd_data(d_ref, wrap_ref, w_ref, out_ref, seg, k_taps):
    for q in range(seg):
        acc = None
        for k in range(k_taps):
            term = w_ref[k:k + 1, :] * _future(d_ref, wrap_ref, q, k_taps - 1 - k, seg)
            acc = term if acc is None else acc + term
        out_ref[q * SUBLANES:(q + 1) * SUBLANES, :] = acc


def _conv_bwd_taps(d_ref, x_ref, wrap_ref, dw_ref, seg, k_taps):
    for k in range(k_taps):
        part = None
        for q in range(seg):
            term = _chunk(d_ref, q) * _past(x_ref, wrap_ref, q, k_taps - 1 - k, k_taps - 1)
            part = term if part is None else part + term
        dw_ref[k:k + 1, :] += _colsum(part)


def _tile_halo_specs(tm, width_block, n_halo, n_tiles, block_of):
    rows = n_halo * SUBLANES
    per = tm // rows
    tile = pl.BlockSpec(width_block(tm), lambda n, i: block_of(n, i))
    prev = pl.BlockSpec(width_block(rows), lambda n, i: block_of(n, jnp.maximum(i * per - 1, 0)))
    nxt = pl.BlockSpec(width_block(rows), lambda n, i: block_of(n, jnp.minimum((i + 1) * per, n_tiles * per - 1)))
    return tile, prev, nxt


def _ln_stats(v):
    mu = jnp.mean(v, axis=-1, keepdims=True)
    vc = v - mu
    var = jnp.mean(vc * vc, axis=-1, keepdims=True)
    rstd = lax.rsqrt(var + LN_EPS)
    return vc * rstd, rstd


def _ln_backward(dxhat, xhat, rstd):
    m1 = jnp.mean(dxhat, axis=-1, keepdims=True)
    m2 = jnp.mean(dxhat * xhat, axis=-1, keepdims=True)
    return rstd * (dxhat - m1 - xhat * m2)


def _row_spec(tm, width):
    return pl.BlockSpec((tm, width), lambda i: (i, 0))


def _param_spec(l, width):
    return pl.BlockSpec((None, 1, width), lambda *_: (l, 0, 0))


def _ln_bwd_rows(dout, xh_ref, rs_ref, g_ref, dr_ref, dg_ref, db_ref, dsum_ref, first):
    xhat = xh_ref[...]
    dr = _ln_backward(dout * g_ref[...], xhat, rs_ref[...])
    dr_ref[...] = dr
    _acc_rows(dg_ref, _colsum(dout * xhat), first)
    _acc_rows(db_ref, _colsum(dout), first)
    _acc_rows(dsum_ref, _colsum(dr), first)


def _ln_bwd_specs(tm, d, l, row_of):
    vec = pl.BlockSpec((1, d), lambda *_: (0, 0))
    ins = [pl.BlockSpec((tm, d), row_of), pl.BlockSpec((tm, 1), row_of), _param_spec(l, d)]
    return ins, [pl.BlockSpec((tm, d), row_of), vec, vec, vec]


def _ln_res_bwd(name, dout, xhat, rstd, g3, l, deps=()):
    t, d = dout.shape
    tm = min(TM_EW, t)

    def body(do_ref, xh_ref, rs_ref, g_ref, dr_ref, dg_ref, db_ref, dc_ref):
        _ln_bwd_rows(do_ref[...], xh_ref, rs_ref, g_ref, dr_ref, dg_ref, db_ref, dc_ref, pl.program_id(0) == 0)

    ins, outs = _ln_bwd_specs(tm, d, l, lambda i: (i, 0))
    return _call(body, name, (t // tm,), [_row_spec(tm, d)] + ins, outs,
                 [_sds((t, d), F32)] + [_sds((1, d), F32)] * 3, deps=deps)(dout, xhat, rstd, g3)


def _matmul_dx_ln_bwd(name, dh, wg, res, xhat, rstd, g3, l, deps=()):
    t, d = res.shape
    nb, _, cw = wg.shape
    tm = min(TM_LN, t)

    def body(a_ref, b_ref, res_ref, xh_ref, rs_ref, g_ref, dr_ref, dg_ref, db_ref, dc_ref, acc_ref):
        i, n = pl.program_id(0), pl.program_id(1)
        p = lax.dot_general(a_ref[...], b_ref[...], (_DIMS["nt"], ((), ())), preferred_element_type=F32)

        @pl.when(n == 0)
        def _():
            acc_ref[...] = p

        @pl.when(n > 0)
        def _():
            acc_ref[...] += p

        @pl.when(n == nb - 1)
        def _():
            _ln_bwd_rows(acc_ref[...] + ALPHA * res_ref[...], xh_ref, rs_ref, g_ref, dr_ref, dg_ref, db_ref, dc_ref,
                         i == 0)

    ins, outs = _ln_bwd_specs(tm, d, l, lambda i, n: (i, 0))
    return _call(body, name, (t // tm, nb),
                 [pl.BlockSpec((tm, cw), lambda i, n: (i, n)), pl.BlockSpec((None, d, cw), lambda i, n: (n, 0, 0)),
                  pl.BlockSpec((tm, d), lambda i, n: (i, 0))] + ins, outs,
                 [_sds((t, d), F32)] + [_sds((1, d), F32)] * 3, [pltpu.VMEM((tm, d), F32)],
                 deps=deps)(dh, wg, res, xhat, rstd, g3)


def _glu(name, h):
    t, c2 = h.shape
    c = c2 // 2
    tm = min(TM_EW, t)

    def body(a_ref, g_ref, o_ref):
        o_ref[...] = a_ref[...] * _sigmoid(g_ref[...])

    return _call(body, name, (t // tm,),
                 [pl.BlockSpec((tm, c), lambda i: (i, 0)), pl.BlockSpec((tm, c), lambda i: (i, 1))],
                 _row_spec(tm, c), _sds((t, c), F32))(h, h)


def _glu_bwd(name, du, h):
    t, c2 = h.shape
    c = c2 // 2
    tm = min(TM_EW, t)

    def body(du_ref, a_ref, g_ref, dh_ref, db_ref):
        first = pl.program_id(0) == 0
        du_v, a = du_ref[...], a_ref[...]
        sg = _sigmoid(g_ref[...])
        da = du_v * sg
        dg = du_v * a * sg * (1.0 - sg)
        dh_ref[:, :c] = da.astype(BF16)
        dh_ref[:, c:] = dg.astype(BF16)
        _acc_rows(db_ref.at[:, :c], _colsum(da), first)
        _acc_rows(db_ref.at[:, c:], _colsum(dg), first)

    return _call(body, name, (t // tm,),
                 [_row_spec(tm, c), pl.BlockSpec((tm, c), lambda i: (i, 0)), pl.BlockSpec((tm, c), lambda i: (i, 1))],
                 [_row_spec(tm, c2), pl.BlockSpec((1, c2), lambda i: (0, 0))],
                 [_sds((t, c2), BF16), _sds((1, c2), F32)])(du, h, h)


CONV_CB = 256
TAPS_PAD = 32


def _dwconv31(name, u, w3, b3, l, seq):
    t, c = u.shape
    tm, cb = TM_EW, CONV_CB
    seg, seq_tiles, n_tiles = tm // SUBLANES, seq // tm, t // tm
    n_wrap = CONV_K - 1
    tile, prev, _ = _tile_halo_specs(tm, lambda rows: (rows, cb), seg, n_tiles, lambda n, r: (r, n))

    def body(u_ref, halo_ref, w_ref, b_ref, o_ref, wrap_ref):
        keep = (pl.program_id(1) % seq_tiles != 0).astype(F32)
        _fill_wrap_prev(u_ref, halo_ref, wrap_ref, n_wrap, seg, seg, keep)
        _conv_fwd(u_ref, wrap_ref, w_ref, b_ref, o_ref, seg, CONV_K)

    return _call(body, name, (c // cb, n_tiles),
                 [tile, prev, pl.BlockSpec((None, CONV_K, cb), lambda n, i: (l, 0, n)),
                  pl.BlockSpec((None, 1, cb), lambda n, i: (l, 0, n))],
                 tile, _sds((t, c), F32), [pltpu.VMEM((n_wrap * SUBLANES, cb), F32)])(u, u, w3, b3)


def _dwconv31_bwd(name, dc, u, w3, l, seq):
    t, c = dc.shape
    tm, cb = TM_EW, CONV_CB
    seg, seq_tiles, n_tiles = tm // SUBLANES, seq // tm, t // tm
    n_wrap = CONV_K - 1
    tile, prev, nxt = _tile_halo_specs(tm, lambda rows: (rows, cb), seg, n_tiles, lambda n, r: (r, n))

    def body(dc_ref, dcn_ref, u_ref, up_ref, w_ref, du_ref, dw_ref, dwrap_ref, uwrap_ref):
        i = pl.program_id(1)
        keep_prev = (i % seq_tiles != 0).astype(F32)
        keep_next = (i % seq_tiles != seq_tiles - 1).astype(F32)
        _fill_wrap_next(dc_ref, dcn_ref, dwrap_ref, n_wrap, keep_next)
        _conv_bwd_data(dc_ref, dwrap_ref, w_ref, du_ref, seg, CONV_K)

        @pl.when(i == 0)
        def _():
            dw_ref[...] = jnp.zeros_like(dw_ref)

        _fill_wrap_prev(u_ref, up_ref, uwrap_ref, n_wrap, seg, seg, keep_prev)
        _conv_bwd_taps(dc_ref, u_ref, uwrap_ref, dw_ref, seg, CONV_K)

    wrap = pltpu.VMEM((n_wrap * SUBLANES, cb), F32)
    return _call(body, name, (c // cb, n_tiles),
                 [tile, nxt, tile, prev, pl.BlockSpec((None, CONV_K, cb), lambda n, i: (l, 0, n))],
                 [tile, pl.BlockSpec((TAPS_PAD, cb), lambda n, i: (0, n))],
                 [_sds((t, c), F32), _sds((TAPS_PAD, c), F32)], [wrap, wrap])(dc, dc, u, u, w3)


def _ln_silu(name, cx, g3, b3, l):
    t, d = cx.shape
    tm = min(TM_EW, t)

    def body(c_ref, g_ref, b_ref, o_ref):
        xhat, _ = _ln_stats(c_ref[...])
        nv = xhat * g_ref[...] + b_ref[...]
        o_ref[...] = (nv * _sigmoid(nv)).astype(BF16)

    return _call(body, name, (t // tm,), [_row_spec(tm, d), _param_spec(l, d), _param_spec(l, d)],
                 _row_spec(tm, d), _sds((t, d), BF16))(cx, g3, b3)


def _ln_silu_bwd(name, ds, cx, g3, b3, l, deps=()):
    t, d = cx.shape
    tm = min(TM_EW, t)

    def body(ds_ref, c_ref, g_ref, b_ref, dc_ref, dg_ref, db_ref, dsum_ref):
        first = pl.program_id(0) == 0
        xhat, rstd = _ln_stats(c_ref[...])
        g = g_ref[...]
        nv = xhat * g + b_ref[...]
        sg = _sigmoid(nv)
        dn = ds_ref[...] * (sg * (1.0 + nv * (1.0 - sg)))
        dc = _ln_backward(dn * g, xhat, rstd)
        dc_ref[...] = dc
        _acc_rows(dg_ref, _colsum(dn * xhat), first)
        _acc_rows(db_ref, _colsum(dn), first)
        _acc_rows(dsum_ref, _colsum(dc), first)

    vec = pl.BlockSpec((1, d), lambda i: (0, 0))
    return _call(body, name, (t // tm,),
                 [_row_spec(tm, d), _row_spec(tm, d), _param_spec(l, d), _param_spec(l, d)],
                 [_row_spec(tm, d), vec, vec, vec],
                 [_sds((t, d), F32)] + [_sds((1, d), F32)] * 3, deps=deps)(ds, cx, g3, b3)


FFN_HALO = FFN_K - 1


def _ffn_conv(x_ref, halo_ref, wrap_ref, w_ref, b_ref, keep, seg, out_ref):
    _fill_wrap_prev(x_ref, halo_ref, wrap_ref, FFN_K - 1, FFN_HALO, seg, keep)
    _conv_fwd(x_ref, wrap_ref, w_ref, b_ref, out_ref, seg, FFN_K)


TM_FFN = 512
WRAP_ROWS = FFN_HALO * SUBLANES


def _sub_tiles(x_ref, prev_ref, next_ref, keep_prev, keep_next, n_sub):
    out = []
    for s in range(n_sub):
        tile = x_ref.at[pl.ds(s * TM_EW, TM_EW)]
        prev = prev_ref if s == 0 else x_ref.at[pl.ds(s * TM_EW - WRAP_ROWS, WRAP_ROWS)]
        nxt = next_ref if s == n_sub - 1 else x_ref.at[pl.ds((s + 1) * TM_EW, WRAP_ROWS)]
        out.append((tile, prev, keep_prev if s == 0 else 1.0, nxt, keep_next if s == n_sub - 1 else 1.0))
    return out


def _rows(ref, s, rows):
    return ref.at[pl.ds(s * rows, rows)]


def _ffn_specs(tm, fb, n_tiles):
    return _tile_halo_specs(tm, lambda rows: (None, rows, fb), FFN_HALO, n_tiles, lambda n, r: (n, r, 0))


def _ffn_up_act(name, x, w_up, b_up4, wdw, bdw, l, seq):
    t, d = x.shape
    nb, fb, _ = w_up.shape
    half = nb // 2
    tm = min(TM_FFN, seq)
    n_sub, seg, seq_steps, n_steps = tm // TM_EW, TM_EW // SUBLANES, seq // tm, t // tm
    per = tm // WRAP_ROWS
    nt_dims = (_DIMS["nt"], ((), ()))

    def body(x_ref, xp_ref, ug_ref, uv_ref, bug_ref, buv_ref, wg_ref, wv_ref, bg_ref, bv_ref,
             hg_ref, hv_ref, a_ref, pg_ref, pv_ref, wrap_ref, cg_ref, cv_ref):
        keep = (pl.program_id(1) % seq_steps != 0).astype(F32)
        xb, xpb = x_ref[...].astype(BF16), xp_ref[...].astype(BF16)
        hg_ref[...] = lax.dot_general(xb, ug_ref[...], nt_dims, preferred_element_type=F32) + bug_ref[...]
        pg_ref[...] = lax.dot_general(xpb, ug_ref[...], nt_dims, preferred_element_type=F32) + bug_ref[...]
        hv_ref[...] = lax.dot_general(xb, uv_ref[...], nt_dims, preferred_element_type=F32) + buv_ref[...]
        pv_ref[...] = lax.dot_general(xpb, uv_ref[...], nt_dims, preferred_element_type=F32) + buv_ref[...]
        for s, (tile, prev, kp, _, _) in enumerate(_sub_tiles(hg_ref, pg_ref, None, keep, None, n_sub)):
            _ffn_conv(tile, prev, wrap_ref, wg_ref, bg_ref, kp, seg, _rows(cg_ref, s, TM_EW))
        for s, (tile, prev, kp, _, _) in enumerate(_sub_tiles(hv_ref, pv_ref, None, keep, None, n_sub)):
            _ffn_conv(tile, prev, wrap_ref, wv_ref, bv_ref, kp, seg, _rows(cv_ref, s, TM_EW))
        cg = cg_ref[...]
        a_ref[...] = (cg * _sigmoid(cg) * cv_ref[...]).astype(BF16)

    def blk(shift):
        return pl.BlockSpec((None, fb, d), lambda n, i: (n + shift, 0, 0))

    def vec(shift, rows):
        return pl.BlockSpec((None, None, rows, fb), lambda n, i: (l, n + shift, 0, 0))

    out = pl.BlockSpec((None, tm, fb), lambda n, i: (n, i, 0))
    tmp = pltpu.VMEM((tm, fb), F32)
    halo = pltpu.VMEM((WRAP_ROWS, fb), F32)
    return _call(body, name, (half, n_steps),
                 [pl.BlockSpec((tm, d), lambda n, i: (i, 0)),
                  pl.BlockSpec((WRAP_ROWS, d), lambda n, i: (jnp.maximum(i * per - 1, 0), 0)),
                  blk(0), blk(half), vec(0, 1), vec(half, 1), vec(0, FFN_K), vec(half, FFN_K), vec(0, 1), vec(half, 1)],
                 [out, out, out],
                 [_sds((half, t, fb), F32), _sds((half, t, fb), F32), _sds((half, t, fb), BF16)],
                 [halo, halo, halo, tmp, tmp])(x, x, w_up, w_up, b_up4, b_up4, wdw, wdw, bdw, bdw)


def _ffn_act_bwd(name, dy, w_down, hg, hv, wdw, bdw, l, seq, deps=()):
    half, t, fb = hg.shape
    d = dy.shape[-1]
    tm = min(TM_FFN, seq)
    n_sub, seg, seq_steps, n_steps = tm // TM_EW, TM_EW // SUBLANES, seq // tm, t // tm
    tile, prev, _ = _ffn_specs(tm, fb, n_steps)

    def body(dy_ref, wd_ref, g_ref, gp_ref, v_ref, vp_ref, wg_ref, wv_ref, bg_ref, bv_ref,
             dg_ref, dv_ref, dbg_ref, dbv_ref, dwg_ref, dwv_ref, gwrap_ref, vwrap_ref, cg_ref, cv_ref):
        i = pl.program_id(1)
        first = i == 0
        keep = (i % seq_steps != 0).astype(F32)
        da = lax.dot_general(dy_ref[...].astype(BF16), wd_ref[...], (_DIMS["nt"], ((), ())),
                             preferred_element_type=F32)
        g_tiles = _sub_tiles(g_ref, gp_ref, None, keep, None, n_sub)
        v_tiles = _sub_tiles(v_ref, vp_ref, None, keep, None, n_sub)
        for s in range(n_sub):
            _ffn_conv(g_tiles[s][0], g_tiles[s][1], _rows(gwrap_ref, s, WRAP_ROWS), wg_ref, bg_ref, g_tiles[s][2],
                      seg, _rows(cg_ref, s, TM_EW))
            _ffn_conv(v_tiles[s][0], v_tiles[s][1], _rows(vwrap_ref, s, WRAP_ROWS), wv_ref, bv_ref, v_tiles[s][2],
                      seg, _rows(cv_ref, s, TM_EW))
        cg, cv = cg_ref[...], cv_ref[...]
        sg = _sigmoid(cg)
        dcv = da * cg * sg
        dcg = da * cv * sg * (1.0 + cg * (1.0 - sg))
        dg_ref[...] = dcg
        dv_ref[...] = dcv
        _acc_rows(dbg_ref, _colsum(dcg), first)
        _acc_rows(dbv_ref, _colsum(dcv), first)

        @pl.when(first)
        def _():
            dwg_ref[...] = jnp.zeros_like(dwg_ref)
            dwv_ref[...] = jnp.zeros_like(dwv_ref)

        for s in range(n_sub):
            _conv_bwd_taps(_rows(dg_ref, s, TM_EW), g_tiles[s][0], _rows(gwrap_ref, s, WRAP_ROWS), dwg_ref, seg, FFN_K)
            _conv_bwd_taps(_rows(dv_ref, s, TM_EW), v_tiles[s][0], _rows(vwrap_ref, s, WRAP_ROWS), dwv_ref, seg, FFN_K)

    def vec(shift, rows):
        return pl.BlockSpec((None, None, rows, fb), lambda n, i: (l, n + shift, 0, 0))

    def acc(rows):
        return pl.BlockSpec((None, rows, fb), lambda n, i: (n, 0, 0))

    wrap = pltpu.VMEM((n_sub * WRAP_ROWS, fb), F32)
    tmp = pltpu.VMEM((tm, fb), F32)
    return _call(body, name, (half, n_steps),
                 [pl.BlockSpec((tm, d), lambda n, i: (i, 0)), pl.BlockSpec((None, fb, d), lambda n, i: (n, 0, 0)),
                  tile, prev, tile, prev, vec(0, FFN_K), vec(half, FFN_K), vec(0, 1), vec(half, 1)],
                 [tile, tile, acc(1), acc(1), acc(SUBLANES), acc(SUBLANES)],
                 [_sds((half, t, fb), F32), _sds((half, t, fb), F32), _sds((half, 1, fb), F32),
                  _sds((half, 1, fb), F32), _sds((half, SUBLANES, fb), F32), _sds((half, SUBLANES, fb), F32)],
                 [wrap, wrap, tmp, tmp], deps=deps)(dy, w_down, hg, hg, hv, hv, wdw, wdw, bdw, bdw)


def _ffn_conv_t_dx(name, dcg, dcv, wdw, w_up, res, xhat, rstd, g3, l, seq):
    half, t, fb = dcg.shape
    nb, d = 2 * half, res.shape[-1]
    tm = min(TM_FFN, seq)
    n_sub, seg, seq_steps, n_steps = tm // TM_EW, TM_EW // SUBLANES, seq // tm, t // tm
    per = tm // WRAP_ROWS

    def body(g_ref, gn_ref, v_ref, vn_ref, w_ref, up_ref, res_ref, xh_ref, rs_ref, gam_ref,
             dh_ref, db_ref, dr_ref, dgam_ref, dbeta_ref, dsum_ref, sel_ref, seln_ref, wrap_ref, out_ref, acc_ref):
        i, n = pl.program_id(0), pl.program_id(1)
        keep = (i % seq_steps != seq_steps - 1).astype(F32)

        @pl.when(n < half)
        def _():
            sel_ref[...] = g_ref[...]
            seln_ref[...] = gn_ref[...]

        @pl.when(n >= half)
        def _():
            sel_ref[...] = v_ref[...]
            seln_ref[...] = vn_ref[...]

        for s, (sub, _, _, nx, kn) in enumerate(_sub_tiles(sel_ref, None, seln_ref, None, keep, n_sub)):
            _fill_wrap_next(sub, nx, wrap_ref, FFN_K - 1, kn)
            _conv_bwd_data(sub, wrap_ref, w_ref, _rows(out_ref, s, TM_EW), seg, FFN_K)
        dh = out_ref[...]
        dhb = dh.astype(BF16)
        dh_ref[...] = dhb
        _acc_rows(db_ref.at[n], _colsum(dh), i == 0)
        p = jnp.dot(dhb, up_ref[...], preferred_element_type=F32)

        @pl.when(n == 0)
        def _():
            acc_ref[...] = p

        @pl.when(n > 0)
        def _():
            acc_ref[...] += p

        @pl.when(n == nb - 1)
        def _():
            _ln_bwd_rows(acc_ref[...] + ALPHA * res_ref[...], xh_ref, rs_ref, gam_ref, dr_ref, dgam_ref, dbeta_ref,
                         dsum_ref, i == 0)

    def src(gate):
        def blk(n):
            return jnp.minimum(n, half - 1) if gate else jnp.maximum(n - half, 0)
        tile = pl.BlockSpec((None, tm, fb), lambda i, n: (blk(n), i, 0))
        nxt = pl.BlockSpec((None, WRAP_ROWS, fb),
                           lambda i, n: (blk(n), jnp.minimum((i + 1) * per, n_steps * per - 1), 0))
        return [tile, nxt]

    row = pl.BlockSpec((tm, d), lambda i, n: (i, 0))
    ln_ins, ln_outs = _ln_bwd_specs(tm, d, l, lambda i, n: (i, 0))
    tmp = pltpu.VMEM((tm, fb), F32)
    halo = pltpu.VMEM((WRAP_ROWS, fb), F32)
    return _call(body, name, (n_steps, nb),
                 src(True) + src(False) +
                 [pl.BlockSpec((None, None, FFN_K, fb), lambda i, n: (l, n, 0, 0)),
                  pl.BlockSpec((None, fb, d), lambda i, n: (n, 0, 0)), row] + ln_ins,
                 [pl.BlockSpec((None, tm, fb), lambda i, n: (n, i, 0)),
                  pl.BlockSpec((nb, 1, fb), lambda i, n: (0, 0, 0))] + ln_outs,
                 [_sds((nb, t, fb), BF16), _sds((nb, 1, fb), F32), _sds((t, d), F32)] + [_sds((1, d), F32)] * 3,
                 [tmp, halo, halo, tmp, pltpu.VMEM((tm, d), F32)])(dcg, dcg, dcv, dcv, wdw, w_up, res, xhat, rstd, g3)


def _gelu_parts(h):
    cdf = 0.5 * (1.0 + lax.erf(h * INV_SQRT2))
    return h * cdf, cdf


def _seg_axis(a, axis, fn):
    return jnp.moveaxis(fn(jnp.moveaxis(a, axis, 0), TM_EW), 0, axis)


def _sgu_operands(w_s, b_s):
    nl = w_s.shape[0]
    n_sub = TM_EW // CHUNK
    tril = jnp.tril(jnp.ones((CHUNK, CHUNK), dtype=bool))
    w_causal = jnp.where(tril, w_s, 0.0)
    w_tile = (jnp.eye(n_sub, dtype=F32)[None, None, :, None, :, None] * w_causal[:, :, None, :, None, :]).reshape(
        nl, GROUPS, TM_EW, TM_EW)
    w_tile = _seg_axis(_seg_axis(w_tile, 2, _to_segments), 3, _to_segments).astype(BF16)
    bs_tile = jnp.broadcast_to(b_s[:, :, None, :, None], (nl, GROUPS, n_sub, CHUNK, CHUNK)).reshape(
        nl, GROUPS, TM_EW, CHUNK)
    return w_tile, _seg_axis(bs_tile, 2, _to_segments)


def _sgu_param_grads(dwt, dbt):
    n_sub = TM_EW // CHUNK
    tril = jnp.tril(jnp.ones((CHUNK, CHUNK), dtype=bool))
    dwt = _seg_axis(_seg_axis(dwt, 1, _from_segments), 2, _from_segments).reshape(GROUPS, n_sub, CHUNK, n_sub, CHUNK)
    dw = sum(dwt[:, a, :, a, :] for a in range(n_sub))
    db = _seg_axis(dbt, 1, _from_segments).reshape(GROUPS, n_sub, CHUNK).sum(axis=1)
    return jnp.where(tril, dw, 0.0), db


def _sgu(name, h, g3, b3, wt, bst, l):
    t, c2 = h.shape
    c = c2 // 2
    tm = TM_EW

    def body(h_ref, g_ref, b_ref, wt_ref, bs_ref, o_ref):
        z, _ = _gelu_parts(h_ref[...])
        u = z[:, :c]
        xhat, _ = _ln_stats(z[:, c:])
        vnb = (xhat * g_ref[...] + b_ref[...]).astype(BF16)
        for gi in range(GROUPS):
            cs = slice(gi * CHUNK, (gi + 1) * CHUNK)
            sp = jnp.dot(wt_ref[gi], vnb[:, cs], preferred_element_type=F32) + bs_ref[gi]
            o_ref[:, cs] = (u[:, cs] * sp).astype(BF16)

    return _call(body, name, (t // tm,),
                 [_row_spec(tm, c2), _param_spec(l, c), _param_spec(l, c),
                  pl.BlockSpec((None, GROUPS, tm, tm), lambda i: (l, 0, 0, 0)),
                  pl.BlockSpec((None, GROUPS, tm, CHUNK), lambda i: (l, 0, 0, 0))],
                 _row_spec(tm, c), _sds((t, c), BF16))(h, g3, b3, wt, bst)


def _sgu_bwd(name, dq, h, g3, b3, wt, bst, l, deps=()):
    t, c2 = h.shape
    c = c2 // 2
    tm = TM_EW
    n_tiles = t // tm

    def body(dq_ref, h_ref, g_ref, b_ref, wt_ref, bs_ref,
             dh_ref, dbin_ref, dw_ref, dbs_ref, dg_ref, db_ref, du_ref, dvn_ref, bsum_ref):
        i = pl.program_id(0)
        first = i == 0
        hv = h_ref[...]
        z, cdf = _gelu_parts(hv)
        u = z[:, :c]
        xhat, rstd = _ln_stats(z[:, c:])
        g = g_ref[...]
        vnb = (xhat * g + b_ref[...]).astype(BF16)

        @pl.when(first)
        def _():
            dw_ref[...] = jnp.zeros_like(dw_ref)
            bsum_ref[...] = jnp.zeros_like(bsum_ref)

        for gi in range(GROUPS):
            cs = slice(gi * CHUNK, (gi + 1) * CHUNK)
            vb = vnb[:, cs]
            w = wt_ref[gi]
            sp = jnp.dot(w, vb, preferred_element_type=F32) + bs_ref[gi]
            dqb = dq_ref[:, cs]
            du_ref[:, cs] = dqb * sp
            dsp = dqb * u[:, cs]
            bsum_ref[gi] += dsp
            dspb = dsp.astype(BF16)
            dw_ref[gi] += lax.dot_general(dspb, vb, (_DIMS["nt"], ((), ())), preferred_element_type=F32)
            dvn_ref[:, cs] = lax.dot_general(w, dspb, (_DIMS["tn"], ((), ())), preferred_element_type=F32)

        dvn = dvn_ref[...]
        dv = _ln_backward(dvn * g, xhat, rstd)
        pdf = jnp.exp(-0.5 * hv * hv) * INV_SQRT2PI
        dgelu = cdf + hv * pdf
        dhu = du_ref[...] * dgelu[:, :c]
        dhv = dv * dgelu[:, c:]
        dh_ref[:, :c] = dhu.astype(BF16)
        dh_ref[:, c:] = dhv.astype(BF16)
        _acc_rows(dbin_ref.at[:, :c], _colsum(dhu), first)
        _acc_rows(dbin_ref.at[:, c:], _colsum(dhv), first)
        _acc_rows(dg_ref, _colsum(dvn * xhat), first)
        _acc_rows(db_ref, _colsum(dvn), first)

        @pl.when(i == n_tiles - 1)
        def _():
            dbs_ref[...] = jnp.sum(bsum_ref[...], axis=-1)

    vec = pl.BlockSpec((1, c), lambda i: (0, 0))
    return _call(body, name, (n_tiles,),
                 [_row_spec(tm, c), _row_spec(tm, c2), _param_spec(l, c), _param_spec(l, c),
                  pl.BlockSpec((None, GROUPS, tm, tm), lambda i: (l, 0, 0, 0)),
                  pl.BlockSpec((None, GROUPS, tm, CHUNK), lambda i: (l, 0, 0, 0))],
                 [_row_spec(tm, c2), pl.BlockSpec((1, c2), lambda i: (0, 0)),
                  pl.BlockSpec((GROUPS, tm, tm), lambda i: (0, 0, 0)),
                  pl.BlockSpec((GROUPS, tm), lambda i: (0, 0)), vec, vec],
                 [_sds((t, c2), BF16), _sds((1, c2), F32), _sds((GROUPS, tm, tm), F32),
                  _sds((GROUPS, tm), F32), _sds((1, c), F32), _sds((1, c), F32)],
                 [pltpu.VMEM((tm, c), F32), pltpu.VMEM((tm, c), F32), pltpu.VMEM((GROUPS, tm, CHUNK), F32)],
                 deps=deps)(dq, h, g3, b3, wt, bst)


def _loss(name, y, target):
    t, d = y.shape
    tm = min(TM_EW, t)
    n_tiles = t // tm

    def body(y_ref, t_ref, l_ref, dy_ref, acc_ref):
        i = pl.program_id(0)
        diff = y_ref[...] - t_ref[...]
        dy_ref[...] = diff * (1.0 / d)
        _acc_rows(acc_ref, _colsum(diff * diff), i == 0)

        @pl.when(i == n_tiles - 1)
        def _():
            l_ref[...] = jnp.broadcast_to(jnp.sum(acc_ref[...], axis=-1, keepdims=True) * (0.5 / d), (1, LANES))

    return _call(body, name, (n_tiles,), [_row_spec(tm, d), _row_spec(tm, d)],
                 [pl.BlockSpec((1, LANES), lambda i: (0, 0)), _row_spec(tm, d)],
                 [_sds((1, LANES), F32), _sds((t, d), F32)], [pltpu.VMEM((1, d), F32)])(y, target)


def _adamw(g, w, m, v):
    m2 = ADAM_B1 * m + (1.0 - ADAM_B1) * g
    v2 = ADAM_B2 * v + (1.0 - ADAM_B2) * (g * g)
    m_hat = m2 / (1.0 - ADAM_B1 ** ADAM_STEP)
    v_hat = v2 / (1.0 - ADAM_B2 ** ADAM_STEP)
    delta = -ADAM_LR * (m_hat / (jnp.sqrt(v_hat) + ADAM_EPS) + ADAM_WD * w)
    return delta, m2, v2


ROW_TILE_CAP = 512


def _row_tile(rows, cap=ROW_TILE_CAP):
    if rows <= cap:
        return rows
    for tr in range(cap, 15, -16):
        if rows % tr == 0:
            return tr
    return rows


def _sum8_adamw(name, dev, lands, parts, w, m, v):
    nl = len(lands)
    _, r, c = lands[0].shape
    tr = _row_tile(r, cap=128)

    def body(dev_ref, *refs):
        land, own = refs[:nl], refs[nl:2 * nl]
        w_ref, m_ref, v_ref, g_ref, d_ref, m2_ref, v2_ref = refs[2 * nl:]
        layer, me = pl.program_id(0), dev_ref[0]
        for l in range(nl):
            @pl.when(layer == l)
            def _(l=l):
                g = None
                for s in range(N_DEV):
                    part = jnp.where(me == s, own[l][...], land[l][s]).astype(F32)
                    g = part if g is None else g + part
                delta, m2, v2 = _adamw(g, w_ref[...], m_ref[...], v_ref[...])
                g_ref[...] = g
                d_ref[...] = delta
                m2_ref[...] = m2
                v2_ref[...] = v2

    def rows_of(l, a, i):
        return jnp.where(a == l, i, 0)

    spec = pl.BlockSpec((None, tr, c), lambda a, i, dev_ref: (a, i, 0))
    in_specs = [pl.BlockSpec((N_DEV, tr, c), lambda a, i, dev_ref, l=l: (0, rows_of(l, a, i), 0)) for l in range(nl)]
    in_specs += [pl.BlockSpec((None, tr, c), lambda a, i, dev_ref, l=l: (dev_ref[0], rows_of(l, a, i), 0))
                 for l in range(nl)]
    grid_spec = pltpu.PrefetchScalarGridSpec(
        num_scalar_prefetch=1, grid=(nl, r // tr), in_specs=in_specs + [spec] * 3, out_specs=[spec] * 4)
    return pl.pallas_call(
        body, name=name, grid_spec=grid_spec, out_shape=[_sds(w.shape, F32)] * 4,
        compiler_params=pltpu.CompilerParams(vmem_limit_bytes=VMEM_LIMIT))(dev, *lands, *parts, w, m, v)


def _sum8(name, parts):
    _, r, c = parts.shape
    tr = _row_tile(r)

    def body(p_ref, o_ref):
        acc = p_ref[0]
        for s in range(1, N_DEV):
            acc = acc + p_ref[s]
        o_ref[...] = acc

    return _call(body, name, (r // tr,), [pl.BlockSpec((N_DEV, tr, c), lambda i: (0, i, 0))],
                 pl.BlockSpec((tr, c), lambda i: (i, 0)), _sds((r, c), F32))(parts)


def _adamw_flat(name, g, w, m, v):
    r, c = g.shape
    tr = _row_tile(r)

    def body(g_ref, w_ref, m_ref, v_ref, d_ref, m2_ref, v2_ref):
        delta, m2, v2 = _adamw(g_ref[...], w_ref[...], m_ref[...], v_ref[...])
        d_ref[...] = delta
        m2_ref[...] = m2
        v2_ref[...] = v2

    spec = pl.BlockSpec((tr, c), lambda i: (i, 0))
    return _call(body, name, (r // tr,), [spec] * 4, [spec] * 3, [_sds((r, c), F32)] * 3)(g, w, m, v)


def _pack(arrs, row_multiple=SUBLANES):
    pieces, rows = [], 0
    for a in arrs:
        piece = a.reshape(-1, LANES)
        piece = jnp.pad(piece, ((0, (-piece.shape[0]) % SUBLANES), (0, 0)))
        pieces.append(piece)
        rows += piece.shape[0]
    if rows % row_multiple:
        pieces.append(jnp.zeros(((-rows) % row_multiple, LANES), pieces[0].dtype))
    return jnp.concatenate(pieces, axis=0)


def _unpack(buf, shapes, lead=0):
    out, pos = [], 0
    for shp in shapes:
        rows = math.prod(shp) // LANES
        piece = lax.slice_in_dim(buf, pos, pos + rows, axis=lead)
        out.append(piece.reshape(buf.shape[:lead] + tuple(shp)))
        pos += rows + (-rows) % SUBLANES
    return out


REPLICATED = ["conv_b_in", "conv_b_dw", "conv_ln_g", "conv_ln_b", "conv_b_out", "gmlp_w_s", "gmlp_b_s",
              "ffn_b_up", "ffn_b_dw", "ffn_b_down", "norm1_g", "norm1_b", "norm2_g", "norm2_b"]
SMALL_SHARDED = ["conv_w_dw", "gmlp_b_in", "gmlp_ln_g", "gmlp_ln_b", "gmlp_b_out", "ffn_w_dw"]
BIG = ["conv_w_in", "conv_w_out", "gmlp_w_in", "gmlp_w_out", "ffn_w_up", "ffn_w_down"]
WEIGHTS = ["conv_w_in", "conv_b_in", "conv_w_dw", "conv_b_dw", "conv_ln_g", "conv_ln_b", "conv_w_out", "conv_b_out",
           "gmlp_w_in", "gmlp_b_in", "gmlp_ln_g", "gmlp_ln_b", "gmlp_w_s", "gmlp_b_s", "gmlp_w_out", "gmlp_b_out",
           "ffn_w_up", "ffn_b_up", "ffn_w_dw", "ffn_b_dw", "ffn_w_down", "ffn_b_down",
           "norm1_g", "norm1_b", "norm2_g", "norm2_b"]


def _from_shards(g, lead_shape):
    nd = len(lead_shape)
    perm = tuple(range(1, nd + 1)) + (0, nd + 1)
    return g.transpose(perm).reshape(tuple(lead_shape) + (-1,))


def _to_shards(full, width):
    lead = full.shape[:-1]
    nd = len(lead)
    parts = full.reshape(lead + (N_DEV, width))
    return parts.transpose((nd,) + tuple(range(nd)) + (nd + 1,))


def _step(p):
    x_in, target_in = p["x"], p["loss_target"]
    bsz, seq, d = x_in.shape
    t = bsz * seq
    assert seq % TM_EW == 0 and TM_EW % CHUNK == 0 and TM_EW // SUBLANES >= CONV_K - 1
    x0 = _to_segments(x_in.reshape(t, d), TM_EW)
    target = _to_segments(target_in.reshape(t, d), TM_EW)
    n_conv, n_gmlp = p["conv_w_in"].shape[0], p["gmlp_w_in"].shape[0]
    fb = p["ffn_w_up"].shape[-1]
    nblk = N_DEV
    half = nblk // 2
    cw = p["conv_w_in"].shape[-1]
    tm = min(TM_MM, t)
    tm_ln = min(TM_LN, t)
    nt = t // tm
    dev = 4 * lax.axis_index("x") + 2 * lax.axis_index("y") + lax.axis_index("c")

    small_shapes = [p[n].shape for n in SMALL_SHARDED]
    small_src = _pack([p[n] for n in SMALL_SHARDED])[None]
    small_all = _all_gather("gather_small_weights", [small_src])[0][0]
    sm = _unpack(small_all, small_shapes, lead=1)
    w_src = []
    for i in range(DEPTH):
        mix = "conv" if i % 2 == 0 else "gmlp"
        w_src += [p[mix + "_w_in"][i // 2].astype(BF16), p[mix + "_w_out"][i // 2].astype(BF16),
                  p["ffn_w_up"][i].T.astype(BF16), p["ffn_w_down"][i].astype(BF16)]
    send_sems, recv_sems, w_land, _ = _gather_start(
        "weights_gather_start", _place_own("weights_place_own", w_src, deps=[small_all]))
    W_IN, W_OUT, W_UP, W_DOWN = range(4)

    def wait_weight(i, k, after):
        return _gather_wait(f"l{i}_weights_wait{k}", w_land[4 * i + k], send_sems, recv_sems, 4 * i + k, after)
    conv_w_dw = _from_shards(sm[0], sm[0].shape[1:-1])
    gmlp_b_in = _from_shards(sm[1], sm[1].shape[1:-1])
    gmlp_ln_g = _from_shards(sm[2], sm[2].shape[1:-1])
    gmlp_ln_b = _from_shards(sm[3], sm[3].shape[1:-1])
    gmlp_b_out = _from_shards(sm[4], sm[4].shape[1:-1])
    ffn_w_dw = sm[5].transpose(1, 0, 2, 3)

    def rows3(a):
        return a.reshape(a.shape[0], 1, a.shape[-1])

    conv_b_in4 = p["conv_b_in"].reshape(n_conv, N_DEV, 1, cw)
    gmlp_b_in4 = gmlp_b_in.reshape(n_gmlp, N_DEV, 1, cw)
    ffn_b_up4 = p["ffn_b_up"].reshape(DEPTH, nblk, 1, fb)
    ffn_b_dw4 = p["ffn_b_dw"].reshape(DEPTH, nblk, 1, fb)
    conv_b_dw3, conv_ln_g3, conv_ln_b3 = rows3(p["conv_b_dw"]), rows3(p["conv_ln_g"]), rows3(p["conv_ln_b"])
    conv_b_out3, gmlp_b_out3, ffn_b_down3 = rows3(p["conv_b_out"]), rows3(gmlp_b_out), rows3(p["ffn_b_down"])
    gmlp_ln_g3, gmlp_ln_b3 = rows3(gmlp_ln_g), rows3(gmlp_ln_b)
    n1g3, n1b3, n2g3, n2b3 = rows3(p["norm1_g"]), rows3(p["norm1_b"]), rows3(p["norm2_g"]), rows3(p["norm2_b"])
    w_tile, bs_tile = _sgu_operands(p["gmlp_w_s"], p["gmlp_b_s"])

    def mm_in(name, xa, wg, l, bias4):
        return _matmul(name, xa, wg, "nn", grid=(nt, N_DEV),
                       a_spec=pl.BlockSpec((tm, d), lambda i, n: (i, 0)),
                       b_spec=pl.BlockSpec((None, d, cw), lambda i, n: (n, 0, 0)),
                       o_spec=pl.BlockSpec((tm, cw), lambda i, n: (i, n)), o_shape=(t, N_DEV * cw), o_dtype=F32,
                       bias=bias4, bias_spec=pl.BlockSpec((None, None, 1, cw), lambda i, n: (l, n, 0, 0)))

    def mm_out_dx(name, dy, w, deps=()):
        return _matmul(name, dy, w, "nt", grid=(nt,),
                       a_spec=pl.BlockSpec((tm, d), lambda i: (i, 0)),
                       b_spec=pl.BlockSpec((d, d), lambda i: (0, 0)),
                       o_spec=pl.BlockSpec((tm, d), lambda i: (i, 0)), o_shape=(t, d), o_dtype=F32, deps=deps)

    def mm_out_dw(name, sa, dy):
        return _matmul(name, sa, dy, "tn", grid=(nt,), k_axis=0, nk=nt, acc_shape=(d, d),
                       a_spec=pl.BlockSpec((tm, d), lambda k: (k, 0)),
                       b_spec=pl.BlockSpec((tm, d), lambda k: (k, 0)),
                       o_spec=pl.BlockSpec((d, d), lambda k: (0, 0)), o_shape=(d, d), o_dtype=BF16)

    def mm_in_dx(name, dh, wg, res):
        return _matmul(name, dh, wg, "nt", grid=(nt, N_DEV), k_axis=1, nk=N_DEV, acc_shape=(tm, d),
                       a_spec=pl.BlockSpec((tm, cw), lambda i, n: (i, n)),
                       b_spec=pl.BlockSpec((None, d, cw), lambda i, n: (n, 0, 0)),
                       o_spec=pl.BlockSpec((tm, d), lambda i, n: (i, 0)), o_shape=(t, d), o_dtype=F32,
                       res=res, res_spec=pl.BlockSpec((tm, d), lambda i, n: (i, 0)), res_scale=ALPHA)

    def mm_in_dw(name, xa, dh):
        return _matmul(name, xa, dh, "tn", grid=(N_DEV, nt), k_axis=1, nk=nt, acc_shape=(d, cw),
                       a_spec=pl.BlockSpec((tm, d), lambda n, k: (k, 0)),
                       b_spec=pl.BlockSpec((tm, cw), lambda n, k: (k, n)),
                       o_spec=pl.BlockSpec((None, d, cw), lambda n, k: (n, 0, 0)),
                       o_shape=(N_DEV, d, cw), o_dtype=BF16)

    def mm_down_dw(name, a, dy, deps=()):
        return _matmul(name, a, dy, "tn", grid=(half, nt), k_axis=1, nk=nt, acc_shape=(fb, d),
                       a_spec=pl.BlockSpec((None, tm, fb), lambda n, k: (n, k, 0)),
                       b_spec=pl.BlockSpec((tm, d), lambda n, k: (k, 0)),
                       o_spec=pl.BlockSpec((None, fb, d), lambda n, k: (n, 0, 0)),
                       o_shape=(half, fb, d), o_dtype=BF16, deps=deps)

    def mm_up_dw(name, xa, dh):
        return _matmul(name, dh, xa, "tn", grid=(nblk, nt), k_axis=1, nk=nt, acc_shape=(fb, d),
                       a_spec=pl.BlockSpec((None, tm, fb), lambda n, k: (n, k, 0)),
                       b_spec=pl.BlockSpec((tm, d), lambda n, k: (k, 0)),
                       o_spec=pl.BlockSpec((None, fb, d), lambda n, k: (n, 0, 0)),
                       o_shape=(nblk, fb, d), o_dtype=BF16)

    saved = []
    xcur = x0
    for i in range(DEPTH):
        j = i // 2
        s = {"x": xcur}
        s["w_in"] = wait_weight(i, W_IN, xcur)
        if i % 2 == 0:
            s["h"] = mm_in(f"l{i}_conv_in", xcur, s["w_in"], j, conv_b_in4)
            s["u"] = _glu(f"l{i}_glu", s["h"])
            s["c"] = _dwconv31(f"l{i}_dwconv", s["u"], conv_w_dw, conv_b_dw3, j, seq)
            s["s"] = _ln_silu(f"l{i}_ln_silu", s["c"], conv_ln_g3, conv_ln_b3, j)
            b_out3 = conv_b_out3
        else:
            s["h"] = mm_in(f"l{i}_gmlp_in", xcur, s["w_in"], j, gmlp_b_in4)
            s["s"] = _sgu(f"l{i}_sgu", s["h"], gmlp_ln_g3, gmlp_ln_b3, w_tile, bs_tile, j)
            b_out3 = gmlp_b_out3
        s["w_out"] = wait_weight(i, W_OUT, s["s"]).reshape(d, d)
        s["x1"], s["xhat1"], s["rstd1"] = _matmul_ln(
            f"l{i}_mixer_out_norm1", s["s"], s["w_out"], xcur, b_out3, n1g3, n1b3, j, i,
            nk=1, a_block=(tm_ln, d), b_block=(d, d))
        s["w_up"] = wait_weight(i, W_UP, s["x1"])
        s["hg"], s["hv"], s["a"] = _ffn_up_act(f"l{i}_ffn_up_act", s["x1"], s["w_up"], ffn_b_up4, ffn_w_dw, ffn_b_dw4,
                                               i, seq)
        s["w_down"] = wait_weight(i, W_DOWN, s["a"]).reshape(half, fb, d)
        xcur, s["xhat2"], s["rstd2"] = _matmul_ln(
            f"l{i}_ffn_down_norm2", s["a"], s["w_down"], s["x1"], ffn_b_down3, n2g3, n2b3, i, i,
            nk=half, a_block=(None, tm_ln, fb), b_block=(None, fb, d))
        saved.append(s)

    loss_row, dx = _loss("loss", xcur, target)

    started = {n: [None] * p[n].shape[0] for n in BIG}
    tokens = []

    def send_grads(name, items):
        done, token = _scatter_start(name, [g for _, _, g in items])
        for (n, l, _), st in zip(items, done):
            started[n][l] = st
        tokens.append(token)

    def take_tokens():
        out = list(tokens)
        tokens.clear()
        return out

    gl = {n: [None] * p[n].shape[0] for n in REPLICATED + SMALL_SHARDED}
    dr2, gl["norm2_g"][DEPTH - 1], gl["norm2_b"][DEPTH - 1], gl["ffn_b_down"][DEPTH - 1] = _ln_res_bwd(
        f"l{DEPTH - 1}_norm2_bwd", dx, saved[-1]["xhat2"], saved[-1]["rstd2"], n2g3, DEPTH - 1)
    for i in reversed(range(DEPTH)):
        j = i // 2
        s = saved[i]
        mix = "conv" if i % 2 == 0 else "gmlp"
        g_down = mm_down_dw(f"l{i}_ffn_down_dw", s["a"], dr2, deps=take_tokens()).reshape(N_DEV, -1, d)
        dcg, dcv, dbg, dbv, dwg, dwv = _ffn_act_bwd(f"l{i}_ffn_act_bwd", dr2, s["w_down"], s["hg"], s["hv"],
                                                    ffn_w_dw, ffn_b_dw4, i, seq)
        gl["ffn_b_dw"][i] = jnp.concatenate([dbg, dbv], axis=0).reshape(1, nblk * fb)
        gl["ffn_w_dw"][i] = jnp.concatenate([dwg[:, :FFN_K], dwv[:, :FFN_K]], axis=0)
        dh, dbu, dr1, gl["norm1_g"][i], gl["norm1_b"][i], gl[mix + "_b_out"][j] = _ffn_conv_t_dx(
            f"l{i}_ffn_conv_t_dx", dcg, dcv, ffn_w_dw, s["w_up"], dr2, s["xhat1"], s["rstd1"], n1g3, i, seq)
        gl["ffn_b_up"][i] = dbu.reshape(1, nblk * fb)
        send_grads(f"l{i}_ffn_grads_scatter_start",
                   [("ffn_w_down", i, g_down), ("ffn_w_up", i, mm_up_dw(f"l{i}_ffn_up_dw", s["x1"], dh))])
        ds = mm_out_dx(f"l{i}_{mix}_out_dx", dr1, s["w_out"], deps=take_tokens())
        g_out = mm_out_dw(f"l{i}_{mix}_out_dw", s["s"], dr1).reshape(N_DEV, -1, d)
        if i % 2 == 0:
            dc, gl["conv_ln_g"][j], gl["conv_ln_b"][j], gl["conv_b_dw"][j] = _ln_silu_bwd(
                f"l{i}_ln_silu_bwd", ds, s["c"], conv_ln_g3, conv_ln_b3, j)
            du, dwdw = _dwconv31_bwd(f"l{i}_dwconv_bwd", dc, s["u"], conv_w_dw, j, seq)
            gl["conv_w_dw"][j] = dwdw[:CONV_K]
            dh, gl["conv_b_in"][j] = _glu_bwd(f"l{i}_glu_bwd", du, s["h"])
        else:
            dh, gl["gmlp_b_in"][j], dwt, dbt, gl["gmlp_ln_g"][j], gl["gmlp_ln_b"][j] = _sgu_bwd(
                f"l{i}_sgu_bwd", ds, s["h"], gmlp_ln_g3, gmlp_ln_b3, w_tile, bs_tile, j)
            gl["gmlp_w_s"][j], gl["gmlp_b_s"][j] = _sgu_param_grads(dwt, dbt)
        if i > 0:
            prev = saved[i - 1]
            dr2, gl["norm2_g"][i - 1], gl["norm2_b"][i - 1], gl["ffn_b_down"][i - 1] = _matmul_dx_ln_bwd(
                f"l{i}_{mix}_in_dx_norm2_bwd", dh, s["w_in"], dr1, prev["xhat2"], prev["rstd2"], n2g3, i - 1)
        else:
            dx = mm_in_dx(f"l{i}_{mix}_in_dx", dh, s["w_in"], dr1)
        send_grads(f"l{i}_mixer_grads_scatter_start",
                   [(mix + "_w_out", j, g_out), (mix + "_w_in", j, mm_in_dw(f"l{i}_{mix}_in_dw", s["x"], dh))])
    grad_x = _from_segments(dx, TM_EW).reshape(bsz, seq, d)

    full_small = {n: jnp.stack(gl[n]).reshape(p[n].shape) for n in REPLICATED}
    shard_small = {}
    for n in SMALL_SHARDED:
        if n == "ffn_w_dw":
            shard_small[n] = jnp.stack(gl[n]).transpose(1, 0, 2, 3)
        else:
            width = p[n].shape[-1]
            lead = p[n].shape[:-1]
            shard_small[n] = _to_shards(jnp.stack(gl[n]).reshape(lead + (N_DEV * width,)), width)
    flat_shapes = [(1, LANES)] + [p[n].shape for n in REPLICATED] + [(N_DEV,) + p[n].shape for n in SMALL_SHARDED]
    flat_local = _pack([loss_row] + [full_small[n] for n in REPLICATED] + [shard_small[n] for n in SMALL_SHARDED],
                       row_multiple=ROW_TILE_CAP)

    small_send, small_recv, small_land, small_token = _gather_start(
        "small_grads_gather_start", _place_own("small_grads_place_own", [flat_local]))

    flat_started = [st for n in BIG for st in started[n]]
    parts_done, lands_done = _scatter_wait("grads_scatter_wait", flat_started, small_token)

    grads, delta, new_m, new_v = {}, {}, {}, {}
    dev1 = jnp.reshape(dev, (1,)).astype(jnp.int32)
    pos = 0
    for n in BIG:
        nl = p[n].shape[0]
        state = [p[n], p["m_" + n], p["v_" + n]]
        if n == "ffn_w_up":
            state = [a.transpose(0, 2, 1) for a in state]
        outs = _sum8_adamw(f"adamw_{n}", dev1, lands_done[pos:pos + nl], parts_done[pos:pos + nl], *state)
        if n == "ffn_w_up":
            outs = [a.transpose(0, 2, 1) for a in outs]
        grads[n], delta[n], new_m[n], new_v[n] = outs
        pos += nl

    small_parts = _gather_wait("small_grads_gather_wait", small_land[0], small_send, small_recv, 0, new_v[BIG[-1]])
    summed = _unpack(_sum8("sum_small_grads", small_parts), flat_shapes)
    loss = summed[0][0, 0]
    grads.update(zip(REPLICATED, summed[1:1 + len(REPLICATED)]))
    for n, g in zip(SMALL_SHARDED, summed[1 + len(REPLICATED):]):
        grads[n] = lax.dynamic_index_in_dim(g, dev, axis=0, keepdims=False)
    small = REPLICATED + SMALL_SHARDED
    small_shp = [p[n].shape for n in small]
    d_s, m_s, v_s = _adamw_flat("adamw_small", _pack([grads[n] for n in small]), _pack([p[n] for n in small]),
                                _pack([p["m_" + n] for n in small]), _pack([p["v_" + n] for n in small]))
    for n, dd, mm, vv in zip(small, _unpack(d_s, small_shp), _unpack(m_s, small_shp), _unpack(v_s, small_shp)):
        delta[n], new_m[n], new_v[n] = dd, mm, vv

    return (loss, grad_x, *[grads[n] for n in WEIGHTS], *[delta[n] for n in WEIGHTS],
            *[new_m[n] for n in WEIGHTS], *[new_v[n] for n in WEIGHTS])


def kernel(x, conv_w_in, conv_b_in, conv_w_dw, conv_b_dw, conv_ln_g, conv_ln_b, conv_w_out, conv_b_out, gmlp_w_in, gmlp_b_in, gmlp_ln_g, gmlp_ln_b, gmlp_w_s, gmlp_b_s, gmlp_w_out, gmlp_b_out, ffn_w_up, ffn_b_up, ffn_w_dw, ffn_b_dw, ffn_w_down, ffn_b_down, norm1_g, norm1_b, norm2_g, norm2_b, loss_target, m_conv_w_in, m_conv_b_in, m_conv_w_dw, m_conv_b_dw, m_conv_ln_g, m_conv_ln_b, m_conv_w_out, m_conv_b_out, m_gmlp_w_in, m_gmlp_b_in, m_gmlp_ln_g, m_gmlp_ln_b, m_gmlp_w_s, m_gmlp_b_s, m_gmlp_w_out, m_gmlp_b_out, m_ffn_w_up, m_ffn_b_up, m_ffn_w_dw, m_ffn_b_dw, m_ffn_w_down, m_ffn_b_down, m_norm1_g, m_norm1_b, m_norm2_g, m_norm2_b, v_conv_w_in, v_conv_b_in, v_conv_w_dw, v_conv_b_dw, v_conv_ln_g, v_conv_ln_b, v_conv_w_out, v_conv_b_out, v_gmlp_w_in, v_gmlp_b_in, v_gmlp_ln_g, v_gmlp_ln_b, v_gmlp_w_s, v_gmlp_b_s, v_gmlp_w_out, v_gmlp_b_out, v_ffn_w_up, v_ffn_b_up, v_ffn_w_dw, v_ffn_b_dw, v_ffn_w_down, v_ffn_b_down, v_norm1_g, v_norm1_b, v_norm2_g, v_norm2_b):
    return _step(dict(locals()))
```

```python
import math

import jax
import jax.numpy as jnp
from jax import lax
from jax.experimental import pallas as pl
from jax.experimental.pallas import tpu as pltpu

F32 = jnp.float32
BF16 = jnp.bfloat16
MESH = pl.DeviceIdType.MESH

N_DEV = 8
DEPTH = 4
ALPHA = (2.0 * DEPTH) ** 0.25
LN_EPS = 1e-5
CONV_K = 31
FFN_K = 3
CHUNK = 128
GROUPS = 8
ADAM_LR = 0.001
ADAM_B1 = 0.9
ADAM_B2 = 0.999
ADAM_EPS = 1e-08
ADAM_WD = 0.01
ADAM_STEP = 10
INV_SQRT2 = 1.0 / math.sqrt(2.0)
INV_SQRT2PI = 1.0 / math.sqrt(2.0 * math.pi)

LANES = 128
SUBLANES = 8
VMEM_LIMIT = 48 * 1024 * 1024
TM_MM = 1024
TM_EW = 256


def _call(body, name, grid, in_specs, out_specs, out_shape, scratch=(), aliases=None, deps=()):
    deps = list(deps)
    in_specs = list(in_specs)
    n_in = len(in_specs)
    if deps:
        inner = body

        def body(*refs):
            return inner(*refs[:n_in], *refs[n_in + len(deps):])

        in_specs = in_specs + [pl.BlockSpec(memory_space=pl.ANY)] * len(deps)
    fn = pl.pallas_call(
        body, name=name, grid=grid, in_specs=in_specs, out_specs=out_specs, out_shape=out_shape,
        scratch_shapes=list(scratch), input_output_aliases=aliases or {},
        compiler_params=pltpu.CompilerParams(vmem_limit_bytes=VMEM_LIMIT))
    return lambda *args: fn(*args, *deps)


def _sds(shape, dtype):
    return jax.ShapeDtypeStruct(tuple(shape), dtype)


def _sigmoid(x):
    return 1.0 / (1.0 + jnp.exp(-x))


def _acc_rows(ref, val, first):
    @pl.when(first)
    def _():
        ref[...] = val

    @pl.when(jnp.logical_not(first))
    def _():
        ref[...] += val


def _colsum(v):
    return jnp.sum(v, axis=0, keepdims=True)


_DIMS = {"nn": ((1,), (0,)), "nt": ((1,), (1,)), "tn": ((0,), (0,))}


def _matmul(name, a, b, mode, *, grid, a_spec, b_spec, o_spec, o_shape, o_dtype, k_axis=None, nk=1,
            acc_shape=None, bias=None, bias_spec=None, res=None, res_spec=None, res_scale=1.0, deps=()):
    dims = (_DIMS[mode], ((), ()))
    has_bias, has_res = bias is not None, res is not None

    def body(*refs):
        a_ref, b_ref = refs[0], refs[1]
        pos = 2
        bias_ref = res_ref = None
        if has_bias:
            bias_ref = refs[pos]
            pos += 1
        if has_res:
            res_ref = refs[pos]
            pos += 1
        o_ref = refs[pos]
        acc_ref = refs[pos + 1] if nk > 1 else None
        p = lax.dot_general(a_ref[...].astype(BF16), b_ref[...].astype(BF16), dims, preferred_element_type=F32)

        def finish(acc):
            if has_bias:
                acc = acc + bias_ref[...]
            if has_res:
                acc = acc + res_scale * res_ref[...]
            o_ref[...] = acc.astype(o_dtype)

        if nk == 1:
            finish(p)
        else:
            k = pl.program_id(k_axis)

            @pl.when(k == 0)
            def _():
                acc_ref[...] = p

            @pl.when(k > 0)
            def _():
                acc_ref[...] += p

            @pl.when(k == nk - 1)
            def _():
                finish(acc_ref[...])

    ins, specs = [a, b], [a_spec, b_spec]
    if has_bias:
        ins.append(bias)
        specs.append(bias_spec)
    if has_res:
        ins.append(res)
        specs.append(res_spec)
    scratch = [pltpu.VMEM(acc_shape, F32)] if nk > 1 else []
    return _call(body, name, grid, specs, o_spec, _sds(o_shape, o_dtype), scratch, deps=deps)(*ins)


TM_LN = 512


def _matmul_ln(name, a, b, x_res, bias3, g3, b3, l_bias, l_norm, *, nk, a_block, b_block):
    t, d = x_res.shape
    tm = min(TM_LN, t)

    def body(a_ref, b_ref, x_ref, bias_ref, g_ref, be_ref, o_ref, xh_ref, rs_ref, *acc):
        p = jnp.dot(a_ref[...].astype(BF16), b_ref[...], preferred_element_type=F32)

        def finish(y):
            xhat, rstd = _ln_stats(ALPHA * x_ref[...] + y + bias_ref[...])
            o_ref[...] = xhat * g_ref[...] + be_ref[...]
            xh_ref[...] = xhat
            rs_ref[...] = rstd

        if nk == 1:
            finish(p)
        else:
            k = pl.program_id(1)

            @pl.when(k == 0)
            def _():
                acc[0][...] = p

            @pl.when(k > 0)
            def _():
                acc[0][...] += p

            @pl.when(k == nk - 1)
            def _():
                finish(acc[0][...])

    if nk == 1:
        grid = (t // tm,)
        a_spec = pl.BlockSpec(a_block, lambda i: (i, 0))
        b_spec = pl.BlockSpec(b_block, lambda i: (0, 0))
    else:
        grid = (t // tm, nk)
        a_spec = pl.BlockSpec(a_block, lambda i, k: (k, i, 0))
        b_spec = pl.BlockSpec(b_block, lambda i, k: (k, 0, 0))
    row = pl.BlockSpec((tm, d), lambda i, *_: (i, 0))
    stat = pl.BlockSpec((tm, 1), lambda i, *_: (i, 0))

    def vec(l):
        return pl.BlockSpec((None, 1, d), lambda *_: (l, 0, 0))

    return _call(body, name, grid, [a_spec, b_spec, row, vec(l_bias), vec(l_norm), vec(l_norm)], [row, row, stat],
                 [_sds((t, d), F32), _sds((t, d), F32), _sds((t, 1), F32)],
                 [pltpu.VMEM((tm, d), F32)] if nk > 1 else [])(a, b, x_res, bias3, g3, b3)


def _mesh_pos():
    return lax.axis_index("x"), lax.axis_index("y"), lax.axis_index("c")


def _any_specs(n):
    return [pl.BlockSpec(memory_space=pl.ANY)] * n


def _all_gather(name, srcs):
    n = len(srcs)

    def body(*refs):
        src, out = refs[:n], refs[n:2 * n]
        send_sems, recv_sems, local_sems = refs[2 * n:]
        x, y, c = _mesh_pos()
        me, sibling = (x, y, c), (x, y, 1 - c)
        chips = [(1 - x, y), (x, 1 - y), (1 - x, 1 - y)]

        def slot(k, p):
            return out[k].at[:, 4 * p[0] + 2 * p[1] + p[2]]

        def copy(k, idx, block, to, s=None):
            return pltpu.make_async_remote_copy(
                src_ref=slot(k, block) if s is None else s, dst_ref=slot(k, block),
                send_sem=send_sems.at[k * 7 + idx], recv_sem=recv_sems.at[k * 7 + idx],
                device_id=to, device_id_type=MESH)

        local = [pltpu.make_async_copy(src[k], slot(k, me), local_sems.at[k]) for k in range(n)]
        for cp in local:
            cp.start()
        first = []
        for k in range(n):
            first.append(copy(k, 0, me, sibling, src[k]))
            for j, chip in enumerate(chips):
                first.append(copy(k, 1 + j, me, (*chip, c), src[k]))
        for cp in first:
            cp.start()
        passed = []
        for j, chip in enumerate(chips):
            for k in range(n):
                copy(k, 1 + j, (*chip, c), me).wait_recv()
                cp = copy(k, 4 + j, (*chip, c), sibling)
                cp.start()
                passed.append(cp)
        for k in range(n):
            copy(k, 0, sibling, me).wait_recv()
            for j, chip in enumerate(chips):
                copy(k, 4 + j, (*chip, 1 - c), me).wait_recv()
        for cp in first + passed:
            cp.wait_send()
        for cp in local:
            cp.wait()

    out_shape = [_sds((s.shape[0], N_DEV) + s.shape[1:], s.dtype) for s in srcs]
    return _call(body, name, (), [pl.BlockSpec(memory_space=pltpu.VMEM)] * n, _any_specs(n), out_shape,
                 [pltpu.SemaphoreType.DMA((7 * n,)), pltpu.SemaphoreType.DMA((7 * n,)),
                  pltpu.SemaphoreType.DMA((n,))])(*srcs)


HBM_SPEC = pl.BlockSpec(memory_space=pltpu.HBM)
SEM_SPEC = pl.BlockSpec(memory_space=pltpu.SEMAPHORE)
N_PEER = N_DEV - 1


def _split_call(body, name, in_specs, out_specs, out_shape, aliases):
    return pl.pallas_call(
        body, name=name, in_specs=in_specs, out_specs=out_specs, out_shape=out_shape, input_output_aliases=aliases,
        compiler_params=pltpu.CompilerParams(has_side_effects=pltpu.SideEffectType.DATAFLOW_SIDE_EFFECTING))


def _peers(x, y, c):
    return [(1 - x if q & 4 else x, 1 - y if q & 2 else y, 1 - c if q & 1 else c) for q in range(1, N_DEV)]


def _in_hbm(a):
    return pltpu.with_memory_space_constraint(a, pltpu.HBM)


def _place_own(name, srcs, deps=()):
    n = len(srcs)

    def body(*refs):
        src, out, sems = refs[:n], refs[n:2 * n], refs[2 * n]
        x, y, c = _mesh_pos()
        dev = 4 * x + 2 * y + c
        copies = [pltpu.make_async_copy(src[k], out[k].at[dev], sems.at[k]) for k in range(n)]
        for cp in copies:
            cp.start()
        for cp in copies:
            cp.wait()

    return _call(body, name, (), [pl.BlockSpec(memory_space=pltpu.VMEM)] * n, _any_specs(n),
                 [_sds((N_DEV,) + s.shape, s.dtype) for s in srcs], [pltpu.SemaphoreType.DMA((n,))],
                 deps=deps)(*srcs)


def _gather_start(name, lands):
    n = len(lands)

    def body(*refs):
        land, send_sems, recv_sems = refs[:n], refs[n], refs[n + 1]
        x, y, c = _mesh_pos()
        dev = 4 * x + 2 * y + c
        for k in range(n):
            for peer in _peers(x, y, c):
                pltpu.make_async_remote_copy(
                    src_ref=land[k].at[dev], dst_ref=land[k].at[dev], send_sem=send_sems.at[k],
                    recv_sem=recv_sems.at[k], device_id=peer, device_id_type=MESH).start()
        token = refs[-1]
        token[...] = jnp.zeros_like(token)

    outs = _split_call(
        body, name, [HBM_SPEC] * n, [SEM_SPEC, SEM_SPEC] + [HBM_SPEC] * n + [pl.BlockSpec(memory_space=pltpu.VMEM)],
        [pltpu.SemaphoreType.DMA((n,)), pltpu.SemaphoreType.DMA((n,))] + [pltpu.HBM(a.shape, a.dtype) for a in lands]
        + [_sds((SUBLANES, LANES), F32)],
        {k: 2 + k for k in range(n)})(*[_in_hbm(a) for a in lands])
    return outs[0], outs[1], list(outs[2:2 + n]), outs[-1]


def _wait_seven(src_ref, dst_ref, send_sem, recv_sem):
    cp = pltpu.make_async_remote_copy(
        src_ref=src_ref.at[pl.ds(0, N_PEER)], dst_ref=dst_ref.at[pl.ds(0, N_PEER)], send_sem=send_sem,
        recv_sem=recv_sem, device_id=_mesh_pos(), device_id_type=MESH)
    cp.wait_send()
    cp.wait_recv()


def _gather_wait(name, land, send_sems, recv_sems, k, after):
    def body(land_ref, send_ref, recv_ref, after_ref, out_ref):
        _wait_seven(land_ref, land_ref, send_ref.at[k], recv_ref.at[k])

    return _split_call(body, name, [HBM_SPEC, SEM_SPEC, SEM_SPEC, pl.BlockSpec(memory_space=pl.ANY)], HBM_SPEC,
                       pltpu.HBM(land.shape, land.dtype), {0: 0})(land, send_sems, recv_sems, after)


def _scatter_start(name, parts_list):
    n = len(parts_list)

    def body(*refs):
        x, y, c = _mesh_pos()
        dev = 4 * x + 2 * y + c
        for k in range(n):
            parts_ref, land_ref = refs[2 * k], refs[2 * k + 1]
            send_sem, recv_sem = refs[2 * n + 4 * k], refs[2 * n + 4 * k + 1]
            for peer in _peers(x, y, c):
                pltpu.make_async_remote_copy(
                    src_ref=parts_ref.at[4 * peer[0] + 2 * peer[1] + peer[2]], dst_ref=land_ref.at[dev],
                    send_sem=send_sem, recv_sem=recv_sem, device_id=peer, device_id_type=MESH).start()
        token = refs[-1]
        token[...] = jnp.zeros_like(token)

    ins, out_specs, out_shape, aliases = [], [], [], {}
    for k, parts in enumerate(parts_list):
        buf = pltpu.HBM(parts.shape, parts.dtype)
        ins += [_in_hbm(parts), _in_hbm(lax.empty(parts.shape, parts.dtype))]
        out_specs += [SEM_SPEC, SEM_SPEC, HBM_SPEC, HBM_SPEC]
        out_shape += [pltpu.SemaphoreType.DMA(()), pltpu.SemaphoreType.DMA(()), buf, buf]
        aliases.update({2 * k: 4 * k + 2, 2 * k + 1: 4 * k + 3})
    outs = _split_call(body, name, [HBM_SPEC] * (2 * n), out_specs + [pl.BlockSpec(memory_space=pltpu.VMEM)],
                       out_shape + [_sds((SUBLANES, LANES), F32)], aliases)(*ins)
    return [tuple(outs[4 * k:4 * k + 4]) for k in range(n)], outs[-1]


def _scatter_wait(name, started, after):
    n = len(started)

    def body(*refs):
        for k in range(n):
            send_sem, recv_sem, parts_ref, land_ref = refs[4 * k:4 * k + 4]
            _wait_seven(parts_ref, land_ref, send_sem, recv_sem)

    flat = [a for s in started for a in s]
    outs = _split_call(
        body, name, [SEM_SPEC, SEM_SPEC, HBM_SPEC, HBM_SPEC] * n + [pl.BlockSpec(memory_space=pl.ANY)],
        [HBM_SPEC, HBM_SPEC] * n, [pltpu.HBM(a.shape, a.dtype) for s in started for a in s[2:]],
        {4 * k + 2 + t: 2 * k + t for k in range(n) for t in range(2)})(*flat, after)
    return list(outs[0::2]), list(outs[1::2])


def _to_segments(a, tile):
    seg = tile // SUBLANES
    return a.reshape((a.shape[0] // tile, SUBLANES, seg) + a.shape[1:]).swapaxes(1, 2).reshape(a.shape)


def _from_segments(a, tile):
    seg = tile // SUBLANES
    return a.reshape((a.shape[0] // tile, seg, SUBLANES) + a.shape[1:]).swapaxes(1, 2).reshape(a.shape)


def _chunk(ref, q):
    return ref[q * SUBLANES:(q + 1) * SUBLANES, :]


def _fill_wrap_prev(x_ref, halo_ref, wrap_ref, n_wrap, n_halo, seg, keep):
    sub = lax.broadcasted_iota(jnp.int32, (SUBLANES, x_ref.shape[-1]), 0)
    for j in range(n_wrap):
        q = seg - n_wrap + j
        hq = q - (seg - n_halo)
        row = halo_ref[hq * SUBLANES + SUBLANES - 1:(hq + 1) * SUBLANES, :] * keep
        wrap_ref[j * SUBLANES:(j + 1) * SUBLANES, :] = jnp.where(sub == 0, row, pltpu.roll(_chunk(x_ref, q), 1, 0))


def _fill_wrap_next(x_ref, halo_ref, wrap_ref, n_wrap, keep):
    sub = lax.broadcasted_iota(jnp.int32, (SUBLANES, x_ref.shape[-1]), 0)
    for j in range(n_wrap):
        row = halo_ref[j * SUBLANES:j * SUBLANES + 1, :] * keep
        wrap_ref[j * SUBLANES:(j + 1) * SUBLANES, :] = jnp.where(
            sub == SUBLANES - 1, row, pltpu.roll(_chunk(x_ref, j), SUBLANES - 1, 0))


def _past(x_ref, wrap_ref, q, d, n_wrap):
    return _chunk(x_ref, q - d) if q >= d else _chunk(wrap_ref, q - d + n_wrap)


def _future(x_ref, wrap_ref, q, d, seg):
    return _chunk(x_ref, q + d) if q + d < seg else _chunk(wrap_ref, q + d - seg)


def _conv_fwd(x_ref, wrap_ref, w_ref, b_ref, out_ref, seg, k_taps):
    bias = jnp.broadcast_to(b_ref[...], (SUBLANES, x_ref.shape[-1]))
    for q in range(seg):
        acc = bias
        for k in range(k_taps):
            acc = acc + w_ref[k:k + 1, :] * _past(x_ref, wrap_ref, q, k_taps - 1 - k, k_taps - 1)
        out_ref[q * SUBLANES:(q + 1) * SUBLANES, :] = acc


def _conv_bwd_data(d_ref, wrap_ref, w_ref, out_ref, seg, k_taps):
    for q in range(seg):
        acc = None
        for k in range(k_taps):
            term = w_ref[k:k + 1, :] * _future(d_ref, wrap_ref, q, k_taps - 1 - k, seg)
            acc = term if acc is None else acc + term
        out_ref[q * SUBLANES:(q + 1) * SUBLANES, :] = acc


def _conv_bwd_taps(d_ref, x_ref, wrap_ref, dw_ref, seg, k_taps):
    for k in range(k_taps):
        part = None
        for q in range(seg):
            term = _chunk(d_ref, q) * _past(x_ref, wrap_ref, q, k_taps - 1 - k, k_taps - 1)
            part = term if part is None else part + term
        dw_ref[k:k + 1, :] += _colsum(part)


def _tile_halo_specs(tm, width_block, n_halo, n_tiles, block_of):
    rows = n_halo * SUBLANES
    per = tm // rows
    tile = pl.BlockSpec(width_block(tm), lambda n, i: block_of(n, i))
    prev = pl.BlockSpec(width_block(rows), lambda n, i: block_of(n, jnp.maximum(i * per - 1, 0)))
    nxt = pl.BlockSpec(width_block(rows), lambda n, i: block_of(n, jnp.minimum((i + 1) * per, n_tiles * per - 1)))
    return tile, prev, nxt


def _ln_stats(v):
    mu = jnp.mean(v, axis=-1, keepdims=True)
    vc = v - mu
    var = jnp.mean(vc * vc, axis=-1, keepdims=True)
    rstd = lax.rsqrt(var + LN_EPS)
    return vc * rstd, rstd


def _ln_backward(dxhat, xhat, rstd):
    m1 = jnp.mean(dxhat, axis=-1, keepdims=True)
    m2 = jnp.mean(dxhat * xhat, axis=-1, keepdims=True)
    return rstd * (dxhat - m1 - xhat * m2)


def _row_spec(tm, width):
    return pl.BlockSpec((tm, width), lambda i: (i, 0))


def _param_spec(l, width):
    return pl.BlockSpec((None, 1, width), lambda *_: (l, 0, 0))


def _ln_bwd_rows(dout, xh_ref, rs_ref, g_ref, dr_ref, dg_ref, db_ref, dsum_ref, first):
    xhat = xh_ref[...]
    dr = _ln_backward(dout * g_ref[...], xhat, rs_ref[...])
    dr_ref[...] = dr
    _acc_rows(dg_ref, _colsum(dout * xhat), first)
    _acc_rows(db_ref, _colsum(dout), first)
    _acc_rows(dsum_ref, _colsum(dr), first)


def _ln_bwd_specs(tm, d, l, row_of):
    vec = pl.BlockSpec((1, d), lambda *_: (0, 0))
    ins = [pl.BlockSpec((tm, d), row_of), pl.BlockSpec((tm, 1), row_of), _param_spec(l, d)]
    return ins, [pl.BlockSpec((tm, d), row_of), vec, vec, vec]


def _ln_res_bwd(name, dout, xhat, rstd, g3, l, deps=()):
    t, d = dout.shape
    tm = min(TM_EW, t)

    def body(do_ref, xh_ref, rs_ref, g_ref, dr_ref, dg_ref, db_ref, dc_ref):
        _ln_bwd_rows(do_ref[...], xh_ref, rs_ref, g_ref, dr_ref, dg_ref, db_ref, dc_ref, pl.program_id(0) == 0)

    ins, outs = _ln_bwd_specs(tm, d, l, lambda i: (i, 0))
    return _call(body, name, (t // tm,), [_row_spec(tm, d)] + ins, outs,
                 [_sds((t, d), F32)] + [_sds((1, d), F32)] * 3, deps=deps)(dout, xhat, rstd, g3)


def _glu(name, h):
    t, c2 = h.shape
    c = c2 // 2
    tm = min(TM_EW, t)

    def body(a_ref, g_ref, o_ref):
        o_ref[...] = a_ref[...] * _sigmoid(g_ref[...])

    return _call(body, name, (t // tm,),
                 [pl.BlockSpec((tm, c), lambda i: (i, 0)), pl.BlockSpec((tm, c), lambda i: (i, 1))],
                 _row_spec(tm, c), _sds((t, c), F32))(h, h)


def _glu_bwd(name, du, h):
    t, c2 = h.shape
    c = c2 // 2
    tm = min(TM_EW, t)

    def body(du_ref, a_ref, g_ref, dh_ref, db_ref):
        first = pl.program_id(0) == 0
        du_v, a = du_ref[...], a_ref[...]
        sg = _sigmoid(g_ref[...])
        da = du_v * sg
        dg = du_v * a * sg * (1.0 - sg)
        dh_ref[:, :c] = da.astype(BF16)
        dh_ref[:, c:] = dg.astype(BF16)
        _acc_rows(db_ref.at[:, :c], _colsum(da), first)
        _acc_rows(db_ref.at[:, c:], _colsum(dg), first)

    return _call(body, name, (t // tm,),
                 [_row_spec(tm, c), pl.BlockSpec((tm, c), lambda i: (i, 0)), pl.BlockSpec((tm, c), lambda i: (i, 1))],
                 [_row_spec(tm, c2), pl.BlockSpec((1, c2), lambda i: (0, 0))],
                 [_sds((t, c2), BF16), _sds((1, c2), F32)])(du, h, h)


CONV_CB = 256
TAPS_PAD = 32


def _dwconv31(name, u, w3, b3, l, seq):
    t, c = u.shape
    tm, cb = TM_EW, CONV_CB
    seg, seq_tiles, n_tiles = tm // SUBLANES, seq // tm, t // tm
    n_wrap = CONV_K - 1
    tile, prev, _ = _tile_halo_specs(tm, lambda rows: (rows, cb), seg, n_tiles, lambda n, r: (r, n))

    def body(u_ref, halo_ref, w_ref, b_ref, o_ref, wrap_ref):
        keep = (pl.program_id(1) % seq_tiles != 0).astype(F32)
        _fill_wrap_prev(u_ref, halo_ref, wrap_ref, n_wrap, seg, seg, keep)
        _conv_fwd(u_ref, wrap_ref, w_ref, b_ref, o_ref, seg, CONV_K)

    return _call(body, name, (c // cb, n_tiles),
                 [tile, prev, pl.BlockSpec((None, CONV_K, cb), lambda n, i: (l, 0, n)),
                  pl.BlockSpec((None, 1, cb), lambda n, i: (l, 0, n))],
                 tile, _sds((t, c), F32), [pltpu.VMEM((n_wrap * SUBLANES, cb), F32)])(u, u, w3, b3)


def _dwconv31_bwd(name, dc, u, w3, l, seq):
    t, c = dc.shape
    tm, cb = TM_EW, CONV_CB
    seg, seq_tiles, n_tiles = tm // SUBLANES, seq // tm, t // tm
    n_wrap = CONV_K - 1
    tile, prev, nxt = _tile_halo_specs(tm, lambda rows: (rows, cb), seg, n_tiles, lambda n, r: (r, n))

    def body(dc_ref, dcn_ref, u_ref, up_ref, w_ref, du_ref, dw_ref, dwrap_ref, uwrap_ref):
        i = pl.program_id(1)
        keep_prev = (i % seq_tiles != 0).astype(F32)
        keep_next = (i % seq_tiles != seq_tiles - 1).astype(F32)
        _fill_wrap_next(dc_ref, dcn_ref, dwrap_ref, n_wrap, keep_next)
        _conv_bwd_data(dc_ref, dwrap_ref, w_ref, du_ref, seg, CONV_K)

        @pl.when(i == 0)
        def _():
            dw_ref[...] = jnp.zeros_like(dw_ref)

        _fill_wrap_prev(u_ref, up_ref, uwrap_ref, n_wrap, seg, seg, keep_prev)
        _conv_bwd_taps(dc_ref, u_ref, uwrap_ref, dw_ref, seg, CONV_K)

    wrap = pltpu.VMEM((n_wrap * SUBLANES, cb), F32)
    return _call(body, name, (c // cb, n_tiles),
                 [tile, nxt, tile, prev, pl.BlockSpec((None, CONV_K, cb), lambda n, i: (l, 0, n))],
                 [tile, pl.BlockSpec((TAPS_PAD, cb), lambda n, i: (0, n))],
                 [_sds((t, c), F32), _sds((TAPS_PAD, c), F32)], [wrap, wrap])(dc, dc, u, u, w3)


def _ln_silu(name, cx, g3, b3, l):
    t, d = cx.shape
    tm = min(TM_EW, t)

    def body(c_ref, g_ref, b_ref, o_ref):
        xhat, _ = _ln_stats(c_ref[...])
        nv = xhat * g_ref[...] + b_ref[...]
        o_ref[...] = (nv * _sigmoid(nv)).astype(BF16)

    return _call(body, name, (t // tm,), [_row_spec(tm, d), _param_spec(l, d), _param_spec(l, d)],
                 _row_spec(tm, d), _sds((t, d), BF16))(cx, g3, b3)


def _ln_silu_bwd(name, ds, cx, g3, b3, l, deps=()):
    t, d = cx.shape
    tm = min(TM_EW, t)

    def body(ds_ref, c_ref, g_ref, b_ref, dc_ref, dg_ref, db_ref, dsum_ref):
        first = pl.program_id(0) == 0
        xhat, rstd = _ln_stats(c_ref[...])
        g = g_ref[...]
        nv = xhat * g + b_ref[...]
        sg = _sigmoid(nv)
        dn = ds_ref[...] * (sg * (1.0 + nv * (1.0 - sg)))
        dc = _ln_backward(dn * g, xhat, rstd)
        dc_ref[...] = dc
        _acc_rows(dg_ref, _colsum(dn * xhat), first)
        _acc_rows(db_ref, _colsum(dn), first)
        _acc_rows(dsum_ref, _colsum(dc), first)

    vec = pl.BlockSpec((1, d), lambda i: (0, 0))
    return _call(body, name, (t // tm,),
                 [_row_spec(tm, d), _row_spec(tm, d), _param_spec(l, d), _param_spec(l, d)],
                 [_row_spec(tm, d), vec, vec, vec],
                 [_sds((t, d), F32)] + [_sds((1, d), F32)] * 3, deps=deps)(ds, cx, g3, b3)


FFN_HALO = FFN_K - 1


def _ffn_conv(x_ref, halo_ref, wrap_ref, w_ref, b_ref, keep, seg, out_ref):
    _fill_wrap_prev(x_ref, halo_ref, wrap_ref, FFN_K - 1, FFN_HALO, seg, keep)
    _conv_fwd(x_ref, wrap_ref, w_ref, b_ref, out_ref, seg, FFN_K)


TM_FFN = 512
WRAP_ROWS = FFN_HALO * SUBLANES


def _sub_tiles(x_ref, prev_ref, next_ref, keep_prev, keep_next, n_sub):
    out = []
    for s in range(n_sub):
        tile = x_ref.at[pl.ds(s * TM_EW, TM_EW)]
        prev = prev_ref if s == 0 else x_ref.at[pl.ds(s * TM_EW - WRAP_ROWS, WRAP_ROWS)]
        nxt = next_ref if s == n_sub - 1 else x_ref.at[pl.ds((s + 1) * TM_EW, WRAP_ROWS)]
        out.append((tile, prev, keep_prev if s == 0 else 1.0, nxt, keep_next if s == n_sub - 1 else 1.0))
    return out


def _rows(ref, s, rows):
    return ref.at[pl.ds(s * rows, rows)]


def _ffn_specs(tm, fb, n_tiles):
    return _tile_halo_specs(tm, lambda rows: (None, rows, fb), FFN_HALO, n_tiles, lambda n, r: (n, r, 0))


def _ffn_up_act(name, x, w_up, b_up4, wdw, bdw, l, seq):
    t, d = x.shape
    nb, fb, _ = w_up.shape
    half = nb // 2
    tm = min(TM_FFN, seq)
    n_sub, seg, seq_steps, n_steps = tm // TM_EW, TM_EW // SUBLANES, seq // tm, t // tm
    per = tm // WRAP_ROWS
    nt_dims = (_DIMS["nt"], ((), ()))

    def body(x_ref, xp_ref, ug_ref, uv_ref, bug_ref, buv_ref, wg_ref, wv_ref, bg_ref, bv_ref,
             hg_ref, hv_ref, a_ref, pg_ref, pv_ref, wrap_ref, cg_ref, cv_ref):
        keep = (pl.program_id(1) % seq_steps != 0).astype(F32)
        xb, xpb = x_ref[...].astype(BF16), xp_ref[...].astype(BF16)
        hg_ref[...] = lax.dot_general(xb, ug_ref[...], nt_dims, preferred_element_type=F32) + bug_ref[...]
        pg_ref[...] = lax.dot_general(xpb, ug_ref[...], nt_dims, preferred_element_type=F32) + bug_ref[...]
        hv_ref[...] = lax.dot_general(xb, uv_ref[...], nt_dims, preferred_element_type=F32) + buv_ref[...]
        pv_ref[...] = lax.dot_general(xpb, uv_ref[...], nt_dims, preferred_element_type=F32) + buv_ref[...]
        for s, (tile, prev, kp, _, _) in enumerate(_sub_tiles(hg_ref, pg_ref, None, keep, None, n_sub)):
            _ffn_conv(tile, prev, wrap_ref, wg_ref, bg_ref, kp, seg, _rows(cg_ref, s, TM_EW))
        for s, (tile, prev, kp, _, _) in enumerate(_sub_tiles(hv_ref, pv_ref, None, keep, None, n_sub)):
            _ffn_conv(tile, prev, wrap_ref, wv_ref, bv_ref, kp, seg, _rows(cv_ref, s, TM_EW))
        cg = cg_ref[...]
        a_ref[...] = (cg * _sigmoid(cg) * cv_ref[...]).astype(BF16)

    def blk(shift):
        return pl.BlockSpec((None, fb, d), lambda n, i: (n + shift, 0, 0))

    def vec(shift, rows):
        return pl.BlockSpec((None, None, rows, fb), lambda n, i: (l, n + shift, 0, 0))

    out = pl.BlockSpec((None, tm, fb), lambda n, i: (n, i, 0))
    tmp = pltpu.VMEM((tm, fb), F32)
    halo = pltpu.VMEM((WRAP_ROWS, fb), F32)
    return _call(body, name, (half, n_steps),
                 [pl.BlockSpec((tm, d), lambda n, i: (i, 0)),
                  pl.BlockSpec((WRAP_ROWS, d), lambda n, i: (jnp.maximum(i * per - 1, 0), 0)),
                  blk(0), blk(half), vec(0, 1), vec(half, 1), vec(0, FFN_K), vec(half, FFN_K), vec(0, 1), vec(half, 1)],
                 [out, out, out],
                 [_sds((half, t, fb), F32), _sds((half, t, fb), F32), _sds((half, t, fb), BF16)],
                 [halo, halo, halo, tmp, tmp])(x, x, w_up, w_up, b_up4, b_up4, wdw, wdw, bdw, bdw)


def _ffn_act_bwd(name, dy, w_down, hg, hv, wdw, bdw, l, seq, deps=()):
    half, t, fb = hg.shape
    d = dy.shape[-1]
    tm = min(TM_FFN, seq)
    n_sub, seg, seq_steps, n_steps = tm // TM_EW, TM_EW // SUBLANES, seq // tm, t // tm
    tile, prev, _ = _ffn_specs(tm, fb, n_steps)

    def body(dy_ref, wd_ref, g_ref, gp_ref, v_ref, vp_ref, wg_ref, wv_ref, bg_ref, bv_ref,
             dg_ref, dv_ref, dbg_ref, dbv_ref, dwg_ref, dwv_ref, gwrap_ref, vwrap_ref, cg_ref, cv_ref):
        i = pl.program_id(1)
        first = i == 0
        keep = (i % seq_steps != 0).astype(F32)
        da = lax.dot_general(dy_ref[...].astype(BF16), wd_ref[...], (_DIMS["nt"], ((), ())),
                             preferred_element_type=F32)
        g_tiles = _sub_tiles(g_ref, gp_ref, None, keep, None, n_sub)
        v_tiles = _sub_tiles(v_ref, vp_ref, None, keep, None, n_sub)
        for s in range(n_sub):
            _ffn_conv(g_tiles[s][0], g_tiles[s][1], _rows(gwrap_ref, s, WRAP_ROWS), wg_ref, bg_ref, g_tiles[s][2],
                      seg, _rows(cg_ref, s, TM_EW))
            _ffn_conv(v_tiles[s][0], v_tiles[s][1], _rows(vwrap_ref, s, WRAP_ROWS), wv_ref, bv_ref, v_tiles[s][2],
                      seg, _rows(cv_ref, s, TM_EW))
        cg, cv = cg_ref[...], cv_ref[...]
        sg = _sigmoid(cg)
        dcv = da * cg * sg
        dcg = da * cv * sg * (1.0 + cg * (1.0 - sg))
        dg_ref[...] = dcg
        dv_ref[...] = dcv
        _acc_rows(dbg_ref, _colsum(dcg), first)
        _acc_rows(dbv_ref, _colsum(dcv), first)

        @pl.when(first)
        def _():
            dwg_ref[...] = jnp.zeros_like(dwg_ref)
            dwv_ref[...] = jnp.zeros_like(dwv_ref)

        for s in range(n_sub):
            _conv_bwd_taps(_rows(dg_ref, s, TM_EW), g_tiles[s][0], _rows(gwrap_ref, s, WRAP_ROWS), dwg_ref, seg, FFN_K)
            _conv_bwd_taps(_rows(dv_ref, s, TM_EW), v_tiles[s][0], _rows(vwrap_ref, s, WRAP_ROWS), dwv_ref, seg, FFN_K)

    def vec(shift, rows):
        return pl.BlockSpec((None, None, rows, fb), lambda n, i: (l, n + shift, 0, 0))

    def acc(rows):
        return pl.BlockSpec((None, rows, fb), lambda n, i: (n, 0, 0))

    wrap = pltpu.VMEM((n_sub * WRAP_ROWS, fb), F32)
    tmp = pltpu.VMEM((tm, fb), F32)
    return _call(body, name, (half, n_steps),
                 [pl.BlockSpec((tm, d), lambda n, i: (i, 0)), pl.BlockSpec((None, fb, d), lambda n, i: (n, 0, 0)),
                  tile, prev, tile, prev, vec(0, FFN_K), vec(half, FFN_K), vec(0, 1), vec(half, 1)],
                 [tile, tile, acc(1), acc(1), acc(SUBLANES), acc(SUBLANES)],
                 [_sds((half, t, fb), F32), _sds((half, t, fb), F32), _sds((half, 1, fb), F32),
                  _sds((half, 1, fb), F32), _sds((half, SUBLANES, fb), F32), _sds((half, SUBLANES, fb), F32)],
                 [wrap, wrap, tmp, tmp], deps=deps)(dy, w_down, hg, hg, hv, hv, wdw, wdw, bdw, bdw)


def _ffn_conv_t_dx(name, dcg, dcv, wdw, w_up, res, xhat, rstd, g3, l, seq):
    half, t, fb = dcg.shape
    nb, d = 2 * half, res.shape[-1]
    tm = min(TM_FFN, seq)
    n_sub, seg, seq_steps, n_steps = tm // TM_EW, TM_EW // SUBLANES, seq // tm, t // tm
    per = tm // WRAP_ROWS

    def body(g_ref, gn_ref, v_ref, vn_ref, w_ref, up_ref, res_ref, xh_ref, rs_ref, gam_ref,
             dh_ref, db_ref, dr_ref, dgam_ref, dbeta_ref, dsum_ref, sel_ref, seln_ref, wrap_ref, out_ref, acc_ref):
        i, n = pl.program_id(0), pl.program_id(1)
        keep = (i % seq_steps != seq_steps - 1).astype(F32)

        @pl.when(n < half)
        def _():
            sel_ref[...] = g_ref[...]
            seln_ref[...] = gn_ref[...]

        @pl.when(n >= half)
        def _():
            sel_ref[...] = v_ref[...]
            seln_ref[...] = vn_ref[...]

        for s, (sub, _, _, nx, kn) in enumerate(_sub_tiles(sel_ref, None, seln_ref, None, keep, n_sub)):
            _fill_wrap_next(sub, nx, wrap_ref, FFN_K - 1, kn)
            _conv_bwd_data(sub, wrap_ref, w_ref, _rows(out_ref, s, TM_EW), seg, FFN_K)
        dh = out_ref[...]
        dhb = dh.astype(BF16)
        dh_ref[...] = dhb
        _acc_rows(db_ref.at[n], _colsum(dh), i == 0)
        p = jnp.dot(dhb, up_ref[...], preferred_element_type=F32)

        @pl.when(n == 0)
        def _():
            acc_ref[...] = p

        @pl.when(n > 0)
        def _():
            acc_ref[...] += p

        @pl.when(n == nb - 1)
        def _():
            _ln_bwd_rows(acc_ref[...] + ALPHA * res_ref[...], xh_ref, rs_ref, gam_ref, dr_ref, dgam_ref, dbeta_ref,
                         dsum_ref, i == 0)

    def src(gate):
        def blk(n):
            return jnp.minimum(n, half - 1) if gate else jnp.maximum(n - half, 0)
        tile = pl.BlockSpec((None, tm, fb), lambda i, n: (blk(n), i, 0))
        nxt = pl.BlockSpec((None, WRAP_ROWS, fb),
                           lambda i, n: (blk(n), jnp.minimum((i + 1) * per, n_steps * per - 1), 0))
        return [tile, nxt]

    row = pl.BlockSpec((tm, d), lambda i, n: (i, 0))
    ln_ins, ln_outs = _ln_bwd_specs(tm, d, l, lambda i, n: (i, 0))
    tmp = pltpu.VMEM((tm, fb), F32)
    halo = pltpu.VMEM((WRAP_ROWS, fb), F32)
    return _call(body, name, (n_steps, nb),
                 src(True) + src(False) +
                 [pl.BlockSpec((None, None, FFN_K, fb), lambda i, n: (l, n, 0, 0)),
                  pl.BlockSpec((None, fb, d), lambda i, n: (n, 0, 0)), row] + ln_ins,
                 [pl.BlockSpec((None, tm, fb), lambda i, n: (n, i, 0)),
                  pl.BlockSpec((nb, 1, fb), lambda i, n: (0, 0, 0))] + ln_outs,
                 [_sds((nb, t, fb), BF16), _sds((nb, 1, fb), F32), _sds((t, d), F32)] + [_sds((1, d), F32)] * 3,
                 [tmp, halo, halo, tmp, pltpu.VMEM((tm, d), F32)])(dcg, dcg, dcv, dcv, wdw, w_up, res, xhat, rstd, g3)


def _gelu_parts(h):
    cdf = 0.5 * (1.0 + lax.erf(h * INV_SQRT2))
    return h * cdf, cdf


def _seg_axis(a, axis, fn):
    return jnp.moveaxis(fn(jnp.moveaxis(a, axis, 0), TM_EW), 0, axis)


def _sgu_operands(w_s, b_s):
    nl = w_s.shape[0]
    n_sub = TM_EW // CHUNK
    tril = jnp.tril(jnp.ones((CHUNK, CHUNK), dtype=bool))
    w_causal = jnp.where(tril, w_s, 0.0)
    w_tile = (jnp.eye(n_sub, dtype=F32)[None, None, :, None, :, None] * w_causal[:, :, None, :, None, :]).reshape(
        nl, GROUPS, TM_EW, TM_EW)
    w_tile = _seg_axis(_seg_axis(w_tile, 2, _to_segments), 3, _to_segments).astype(BF16)
    bs_tile = jnp.broadcast_to(b_s[:, :, None, :, None], (nl, GROUPS, n_sub, CHUNK, CHUNK)).reshape(
        nl, GROUPS, TM_EW, CHUNK)
    return w_tile, _seg_axis(bs_tile, 2, _to_segments)


def _sgu_param_grads(dwt, dbt):
    n_sub = TM_EW // CHUNK
    tril = jnp.tril(jnp.ones((CHUNK, CHUNK), dtype=bool))
    dwt = _seg_axis(_seg_axis(dwt, 1, _from_segments), 2, _from_segments).reshape(GROUPS, n_sub, CHUNK, n_sub, CHUNK)
    dw = sum(dwt[:, a, :, a, :] for a in range(n_sub))
    db = _seg_axis(dbt, 1, _from_segments).reshape(GROUPS, n_sub, CHUNK).sum(axis=1)
    return jnp.where(tril, dw, 0.0), db


def _sgu(name, h, g3, b3, wt, bst, l):
    t, c2 = h.shape
    c = c2 // 2
    tm = TM_EW

    def body(h_ref, g_ref, b_ref, wt_ref, bs_ref, o_ref):
        z, _ = _gelu_parts(h_ref[...])
        u = z[:, :c]
        xhat, _ = _ln_stats(z[:, c:])
        vnb = (xhat * g_ref[...] + b_ref[...]).astype(BF16)
        for gi in range(GROUPS):
            cs = slice(gi * CHUNK, (gi + 1) * CHUNK)
            sp = jnp.dot(wt_ref[gi], vnb[:, cs], preferred_element_type=F32) + bs_ref[gi]
            o_ref[:, cs] = (u[:, cs] * sp).astype(BF16)

    return _call(body, name, (t // tm,),
                 [_row_spec(tm, c2), _param_spec(l, c), _param_spec(l, c),
                  pl.BlockSpec((None, GROUPS, tm, tm), lambda i: (l, 0, 0, 0)),
                  pl.BlockSpec((None, GROUPS, tm, CHUNK), lambda i: (l, 0, 0, 0))],
                 _row_spec(tm, c), _sds((t, c), BF16))(h, g3, b3, wt, bst)


def _sgu_bwd(name, dq, h, g3, b3, wt, bst, l, deps=()):
    t, c2 = h.shape
    c = c2 // 2
    tm = TM_EW
    n_tiles = t // tm

    def body(dq_ref, h_ref, g_ref, b_ref, wt_ref, bs_ref,
             dh_ref, dbin_ref, dw_ref, dbs_ref, dg_ref, db_ref, du_ref, dvn_ref, bsum_ref):
        i = pl.program_id(0)
        first = i == 0
        hv = h_ref[...]
        z, cdf = _gelu_parts(hv)
        u = z[:, :c]
        xhat, rstd = _ln_stats(z[:, c:])
        g = g_ref[...]
        vnb = (xhat * g + b_ref[...]).astype(BF16)

        @pl.when(first)
        def _():
            dw_ref[...] = jnp.zeros_like(dw_ref)
            bsum_ref[...] = jnp.zeros_like(bsum_ref)

        for gi in range(GROUPS):
            cs = slice(gi * CHUNK, (gi + 1) * CHUNK)
            vb = vnb[:, cs]
            w = wt_ref[gi]
            sp = jnp.dot(w, vb, preferred_element_type=F32) + bs_ref[gi]
            dqb = dq_ref[:, cs]
            du_ref[:, cs] = dqb * sp
            dsp = dqb * u[:, cs]
            bsum_ref[gi] += dsp
            dspb = dsp.astype(BF16)
            dw_ref[gi] += lax.dot_general(dspb, vb, (_DIMS["nt"], ((), ())), preferred_element_type=F32)
            dvn_ref[:, cs] = lax.dot_general(w, dspb, (_DIMS["tn"], ((), ())), preferred_element_type=F32)

        dvn = dvn_ref[...]
        dv = _ln_backward(dvn * g, xhat, rstd)
        pdf = jnp.exp(-0.5 * hv * hv) * INV_SQRT2PI
        dgelu = cdf + hv * pdf
        dhu = du_ref[...] * dgelu[:, :c]
        dhv = dv * dgelu[:, c:]
        dh_ref[:, :c] = dhu.astype(BF16)
        dh_ref[:, c:] = dhv.astype(BF16)
        _acc_rows(dbin_ref.at[:, :c], _colsum(dhu), first)
        _acc_rows(dbin_ref.at[:, c:], _colsum(dhv), first)
        _acc_rows(dg_ref, _colsum(dvn * xhat), first)
        _acc_rows(db_ref, _colsum(dvn), first)

        @pl.when(i == n_tiles - 1)
        def _():
            dbs_ref[...] = jnp.sum(bsum_ref[...], axis=-1)

    vec = pl.BlockSpec((1, c), lambda i: (0, 0))
    return _call(body, name, (n_tiles,),
                 [_row_spec(tm, c), _row_spec(tm, c2), _param_spec(l, c), _param_spec(l, c),
                  pl.BlockSpec((None, GROUPS, tm, tm), lambda i: (l, 0, 0, 0)),
                  pl.BlockSpec((None, GROUPS, tm, CHUNK), lambda i: (l, 0, 0, 0))],
                 [_row_spec(tm, c2), pl.BlockSpec((1, c2), lambda i: (0, 0)),
                  pl.BlockSpec((GROUPS, tm, tm), lambda i: (0, 0, 0)),
                  pl.BlockSpec((GROUPS, tm), lambda i: (0, 0)), vec, vec],
                 [_sds((t, c2), BF16), _sds((1, c2), F32), _sds((GROUPS, tm, tm), F32),
                  _sds((GROUPS, tm), F32), _sds((1, c), F32), _sds((1, c), F32)],
                 [pltpu.VMEM((tm, c), F32), pltpu.VMEM((tm, c), F32), pltpu.VMEM((GROUPS, tm, CHUNK), F32)],
                 deps=deps)(dq, h, g3, b3, wt, bst)


def _loss(name, y, target):
    t, d = y.shape
    tm = min(TM_EW, t)
    n_tiles = t // tm

    def body(y_ref, t_ref, l_ref, dy_ref, acc_ref):
        i = pl.program_id(0)
        diff = y_ref[...] - t_ref[...]
        dy_ref[...] = diff * (1.0 / d)
        _acc_rows(acc_ref, _colsum(diff * diff), i == 0)

        @pl.when(i == n_tiles - 1)
        def _():
            l_ref[...] = jnp.broadcast_to(jnp.sum(acc_ref[...], axis=-1, keepdims=True) * (0.5 / d), (1, LANES))

    return _call(body, name, (n_tiles,), [_row_spec(tm, d), _row_spec(tm, d)],
                 [pl.BlockSpec((1, LANES), lambda i: (0, 0)), _row_spec(tm, d)],
                 [_sds((1, LANES), F32), _sds((t, d), F32)], [pltpu.VMEM((1, d), F32)])(y, target)


def _adamw(g, w, m, v):
    m2 = ADAM_B1 * m + (1.0 - ADAM_B1) * g
    v2 = ADAM_B2 * v + (1.0 - ADAM_B2) * (g * g)
    m_hat = m2 / (1.0 - ADAM_B1 ** ADAM_STEP)
    v_hat = v2 / (1.0 - ADAM_B2 ** ADAM_STEP)
    delta = -ADAM_LR * (m_hat / (jnp.sqrt(v_hat) + ADAM_EPS) + ADAM_WD * w)
    return delta, m2, v2


ROW_TILE_CAP = 512


def _row_tile(rows, cap=ROW_TILE_CAP):
    if rows <= cap:
        return rows
    for tr in range(cap, 15, -16):
        if rows % tr == 0:
            return tr
    return rows


def _sum8_adamw(name, dev, lands, parts, w, m, v):
    nl = len(lands)
    _, r, c = lands[0].shape
    tr = _row_tile(r, cap=128)

    def body(dev_ref, *refs):
        land, own = refs[:nl], refs[nl:2 * nl]
        w_ref, m_ref, v_ref, g_ref, d_ref, m2_ref, v2_ref = refs[2 * nl:]
        layer, me = pl.program_id(0), dev_ref[0]
        for l in range(nl):
            @pl.when(layer == l)
            def _(l=l):
                g = None
                for s in range(N_DEV):
                    part = jnp.where(me == s, own[l][...], land[l][s]).astype(F32)
                    g = part if g is None else g + part
                delta, m2, v2 = _adamw(g, w_ref[...], m_ref[...], v_ref[...])
                g_ref[...] = g
                d_ref[...] = delta
                m2_ref[...] = m2
                v2_ref[...] = v2

    def rows_of(l, a, i):
        return jnp.where(a == l, i, 0)

    spec = pl.BlockSpec((None, tr, c), lambda a, i, dev_ref: (a, i, 0))
    in_specs = [pl.BlockSpec((N_DEV, tr, c), lambda a, i, dev_ref, l=l: (0, rows_of(l, a, i), 0)) for l in range(nl)]
    in_specs += [pl.BlockSpec((None, tr, c), lambda a, i, dev_ref, l=l: (dev_ref[0], rows_of(l, a, i), 0))
                 for l in range(nl)]
    grid_spec = pltpu.PrefetchScalarGridSpec(
        num_scalar_prefetch=1, grid=(nl, r // tr), in_specs=in_specs + [spec] * 3, out_specs=[spec] * 4)
    return pl.pallas_call(
        body, name=name, grid_spec=grid_spec, out_shape=[_sds(w.shape, F32)] * 4,
        compiler_params=pltpu.CompilerParams(vmem_limit_bytes=VMEM_LIMIT))(dev, *lands, *parts, w, m, v)


def _sum8(name, parts):
    _, r, c = parts.shape
    tr = _row_tile(r)

    def body(p_ref, o_ref):
        acc = p_ref[0]
        for s in range(1, N_DEV):
            acc = acc + p_ref[s]
        o_ref[...] = acc

    return _call(body, name, (r // tr,), [pl.BlockSpec((N_DEV, tr, c), lambda i: (0, i, 0))],
                 pl.BlockSpec((tr, c), lambda i: (i, 0)), _sds((r, c), F32))(parts)


def _adamw_flat(name, g, w, m, v):
    r, c = g.shape
    tr = _row_tile(r)

    def body(g_ref, w_ref, m_ref, v_ref, d_ref, m2_ref, v2_ref):
        delta, m2, v2 = _adamw(g_ref[...], w_ref[...], m_ref[...], v_ref[...])
        d_ref[...] = delta
        m2_ref[...] = m2
        v2_ref[...] = v2

    spec = pl.BlockSpec((tr, c), lambda i: (i, 0))
    return _call(body, name, (r // tr,), [spec] * 4, [spec] * 3, [_sds((r, c), F32)] * 3)(g, w, m, v)


def _pack(arrs, row_multiple=SUBLANES):
    pieces, rows = [], 0
    for a in arrs:
        piece = a.reshape(-1, LANES)
        piece = jnp.pad(piece, ((0, (-piece.shape[0]) % SUBLANES), (0, 0)))
        pieces.append(piece)
        rows += piece.shape[0]
    if rows % row_multiple:
        pieces.append(jnp.zeros(((-rows) % row_multiple, LANES), pieces[0].dtype))
    return jnp.concatenate(pieces, axis=0)


def _unpack(buf, shapes, lead=0):
    out, pos = [], 0
    for shp in shapes:
        rows = math.prod(shp) // LANES
        piece = lax.slice_in_dim(buf, pos, pos + rows, axis=lead)
        out.append(piece.reshape(buf.shape[:lead] + tuple(shp)))
        pos += rows + (-rows) % SUBLANES
    return out


REPLICATED = ["conv_b_in", "conv_b_dw", "conv_ln_g", "conv_ln_b", "conv_b_out", "gmlp_w_s", "gmlp_b_s",
              "ffn_b_up", "ffn_b_dw", "ffn_b_down", "norm1_g", "norm1_b", "norm2_g", "norm2_b"]
SMALL_SHARDED = ["conv_w_dw", "gmlp_b_in", "gmlp_ln_g", "gmlp_ln_b", "gmlp_b_out", "ffn_w_dw"]
BIG = ["conv_w_in", "conv_w_out", "gmlp_w_in", "gmlp_w_out", "ffn_w_up", "ffn_w_down"]
WEIGHTS = ["conv_w_in", "conv_b_in", "conv_w_dw", "conv_b_dw", "conv_ln_g", "conv_ln_b", "conv_w_out", "conv_b_out",
           "gmlp_w_in", "gmlp_b_in", "gmlp_ln_g", "gmlp_ln_b", "gmlp_w_s", "gmlp_b_s", "gmlp_w_out", "gmlp_b_out",
           "ffn_w_up", "ffn_b_up", "ffn_w_dw", "ffn_b_dw", "ffn_w_down", "ffn_b_down",
           "norm1_g", "norm1_b", "norm2_g", "norm2_b"]


def _from_shards(g, lead_shape):
    nd = len(lead_shape)
    perm = tuple(range(1, nd + 1)) + (0, nd + 1)
    return g.transpose(perm).reshape(tuple(lead_shape) + (-1,))


def _to_shards(full, width):
    lead = full.shape[:-1]
    nd = len(lead)
    parts = full.reshape(lead + (N_DEV, width))
    return parts.transpose((nd,) + tuple(range(nd)) + (nd + 1,))


def _step(p):
    x_in, target_in = p["x"], p["loss_target"]
    bsz, seq, d = x_in.shape
    t = bsz * seq
    assert seq % TM_EW == 0 and TM_EW % CHUNK == 0 and TM_EW // SUBLANES >= CONV_K - 1
    x0 = _to_segments(x_in.reshape(t, d), TM_EW)
    target = _to_segments(target_in.reshape(t, d), TM_EW)
    n_conv, n_gmlp = p["conv_w_in"].shape[0], p["gmlp_w_in"].shape[0]
    fb = p["ffn_w_up"].shape[-1]
    nblk = N_DEV
    half = nblk // 2
    cw = p["conv_w_in"].shape[-1]
    tm = min(TM_MM, t)
    tm_ln = min(TM_LN, t)
    nt = t // tm
    dev = 4 * lax.axis_index("x") + 2 * lax.axis_index("y") + lax.axis_index("c")

    small_shapes = [p[n].shape for n in SMALL_SHARDED]
    small_src = _pack([p[n] for n in SMALL_SHARDED])[None]
    small_all = _all_gather("gather_small_weights", [small_src])[0][0]
    sm = _unpack(small_all, small_shapes, lead=1)
    w_src = []
    for i in range(DEPTH):
        mix = "conv" if i % 2 == 0 else "gmlp"
        w_src += [p[mix + "_w_in"][i // 2].astype(BF16), p[mix + "_w_out"][i // 2].astype(BF16),
                  p["ffn_w_up"][i].T.astype(BF16), p["ffn_w_down"][i].astype(BF16)]
    send_sems, recv_sems, w_land, _ = _gather_start(
        "weights_gather_start", _place_own("weights_place_own", w_src, deps=[small_all]))
    W_IN, W_OUT, W_UP, W_DOWN = range(4)

    def wait_weight(i, k, after):
        return _gather_wait(f"l{i}_weights_wait{k}", w_land[4 * i + k], send_sems, recv_sems, 4 * i + k, after)
    conv_w_dw = _from_shards(sm[0], sm[0].shape[1:-1])
    gmlp_b_in = _from_shards(sm[1], sm[1].shape[1:-1])
    gmlp_ln_g = _from_shards(sm[2], sm[2].shape[1:-1])
    gmlp_ln_b = _from_shards(sm[3], sm[3].shape[1:-1])
    gmlp_b_out = _from_shards(sm[4], sm[4].shape[1:-1])
    ffn_w_dw = sm[5].transpose(1, 0, 2, 3)

    def rows3(a):
        return a.reshape(a.shape[0], 1, a.shape[-1])

    conv_b_in4 = p["conv_b_in"].reshape(n_conv, N_DEV, 1, cw)
    gmlp_b_in4 = gmlp_b_in.reshape(n_gmlp, N_DEV, 1, cw)
    ffn_b_up4 = p["ffn_b_up"].reshape(DEPTH, nblk, 1, fb)
    ffn_b_dw4 = p["ffn_b_dw"].reshape(DEPTH, nblk, 1, fb)
    conv_b_dw3, conv_ln_g3, conv_ln_b3 = rows3(p["conv_b_dw"]), rows3(p["conv_ln_g"]), rows3(p["conv_ln_b"])
    conv_b_out3, gmlp_b_out3, ffn_b_down3 = rows3(p["conv_b_out"]), rows3(gmlp_b_out), rows3(p["ffn_b_down"])
    gmlp_ln_g3, gmlp_ln_b3 = rows3(gmlp_ln_g), rows3(gmlp_ln_b)
    n1g3, n1b3, n2g3, n2b3 = rows3(p["norm1_g"]), rows3(p["norm1_b"]), rows3(p["norm2_g"]), rows3(p["norm2_b"])
    w_tile, bs_tile = _sgu_operands(p["gmlp_w_s"], p["gmlp_b_s"])

    def mm_in(name, xa, wg, l, bias4):
        return _matmul(name, xa, wg, "nn", grid=(nt, N_DEV),
                       a_spec=pl.BlockSpec((tm, d), lambda i, n: (i, 0)),
                       b_spec=pl.BlockSpec((None, d, cw), lambda i, n: (n, 0, 0)),
                       o_spec=pl.BlockSpec((tm, cw), lambda i, n: (i, n)), o_shape=(t, N_DEV * cw), o_dtype=F32,
                       bias=bias4, bias_spec=pl.BlockSpec((None, None, 1, cw), lambda i, n: (l, n, 0, 0)))

    def mm_out_dx(name, dy, w, deps=()):
        return _matmul(name, dy, w, "nt", grid=(nt,),
                       a_spec=pl.BlockSpec((tm, d), lambda i: (i, 0)),
                       b_spec=pl.BlockSpec((d, d), lambda i: (0, 0)),
                       o_spec=pl.BlockSpec((tm, d), lambda i: (i, 0)), o_shape=(t, d), o_dtype=F32, deps=deps)

    def mm_out_dw(name, sa, dy):
        return _matmul(name, sa, dy, "tn", grid=(nt,), k_axis=0, nk=nt, acc_shape=(d, d),
                       a_spec=pl.BlockSpec((tm, d), lambda k: (k, 0)),
                       b_spec=pl.BlockSpec((tm, d), lambda k: (k, 0)),
                       o_spec=pl.BlockSpec((d, d), lambda k: (0, 0)), o_shape=(d, d), o_dtype=BF16)

    def mm_in_dx(name, dh, wg, res):
        return _matmul(name, dh, wg, "nt", grid=(nt, N_DEV), k_axis=1, nk=N_DEV, acc_shape=(tm, d),
                       a_spec=pl.BlockSpec((tm, cw), lambda i, n: (i, n)),
                       b_spec=pl.BlockSpec((None, d, cw), lambda i, n: (n, 0, 0)),
                       o_spec=pl.BlockSpec((tm, d), lambda i, n: (i, 0)), o_shape=(t, d), o_dtype=F32,
                       res=res, res_spec=pl.BlockSpec((tm, d), lambda i, n: (i, 0)), res_scale=ALPHA)

    def mm_in_dw(name, xa, dh):
        return _matmul(name, xa, dh, "tn", grid=(N_DEV, nt), k_axis=1, nk=nt, acc_shape=(d, cw),
                       a_spec=pl.BlockSpec((tm, d), lambda n, k: (k, 0)),
                       b_spec=pl.BlockSpec((tm, cw), lambda n, k: (k, n)),
                       o_spec=pl.BlockSpec((None, d, cw), lambda n, k: (n, 0, 0)),
                       o_shape=(N_DEV, d, cw), o_dtype=BF16)

    def mm_down_dw(name, a, dy, deps=()):
        return _matmul(name, a, dy, "tn", grid=(half, nt), k_axis=1, nk=nt, acc_shape=(fb, d),
                       a_spec=pl.BlockSpec((None, tm, fb), lambda n, k: (n, k, 0)),
                       b_spec=pl.BlockSpec((tm, d), lambda n, k: (k, 0)),
                       o_spec=pl.BlockSpec((None, fb, d), lambda n, k: (n, 0, 0)),
                       o_shape=(half, fb, d), o_dtype=BF16, deps=deps)

    def mm_up_dw(name, xa, dh):
        return _matmul(name, dh, xa, "tn", grid=(nblk, nt), k_axis=1, nk=nt, acc_shape=(fb, d),
                       a_spec=pl.BlockSpec((None, tm, fb), lambda n, k: (n, k, 0)),
                       b_spec=pl.BlockSpec((tm, d), lambda n, k: (k, 0)),
                       o_spec=pl.BlockSpec((None, fb, d), lambda n, k: (n, 0, 0)),
                       o_shape=(nblk, fb, d), o_dtype=BF16)

    saved = []
    xcur = x0
    for i in range(DEPTH):
        j = i // 2
        s = {"x": xcur}
        s["w_in"] = wait_weight(i, W_IN, xcur)
        if i % 2 == 0:
            s["h"] = mm_in(f"l{i}_conv_in", xcur, s["w_in"], j, conv_b_in4)
            s["u"] = _glu(f"l{i}_glu", s["h"])
            s["c"] = _dwconv31(f"l{i}_dwconv", s["u"], conv_w_dw, conv_b_dw3, j, seq)
            s["s"] = _ln_silu(f"l{i}_ln_silu", s["c"], conv_ln_g3, conv_ln_b3, j)
            b_out3 = conv_b_out3
        else:
            s["h"] = mm_in(f"l{i}_gmlp_in", xcur, s["w_in"], j, gmlp_b_in4)
            s["s"] = _sgu(f"l{i}_sgu", s["h"], gmlp_ln_g3, gmlp_ln_b3, w_tile, bs_tile, j)
            b_out3 = gmlp_b_out3
        s["w_out"] = wait_weight(i, W_OUT, s["s"]).reshape(d, d)
        s["x1"], s["xhat1"], s["rstd1"] = _matmul_ln(
            f"l{i}_mixer_out_norm1", s["s"], s["w_out"], xcur, b_out3, n1g3, n1b3, j, i,
            nk=1, a_block=(tm_ln, d), b_block=(d, d))
        s["w_up"] = wait_weight(i, W_UP, s["x1"])
        s["hg"], s["hv"], s["a"] = _ffn_up_act(f"l{i}_ffn_up_act", s["x1"], s["w_up"], ffn_b_up4, ffn_w_dw, ffn_b_dw4,
                                               i, seq)
        s["w_down"] = wait_weight(i, W_DOWN, s["a"]).reshape(half, fb, d)
        xcur, s["xhat2"], s["rstd2"] = _matmul_ln(
            f"l{i}_ffn_down_norm2", s["a"], s["w_down"], s["x1"], ffn_b_down3, n2g3, n2b3, i, i,
            nk=half, a_block=(None, tm_ln, fb), b_block=(None, fb, d))
        saved.append(s)

    loss_row, dx = _loss("loss", xcur, target)

    started = {n: [None] * p[n].shape[0] for n in BIG}
    tokens = []

    def send_grads(name, items):
        done, token = _scatter_start(name, [g for _, _, g in items])
        for (n, l, _), st in zip(items, done):
            started[n][l] = st
        tokens.append(token)

    def take_tokens():
        out = list(tokens)
        tokens.clear()
        return out

    gl = {n: [None] * p[n].shape[0] for n in REPLICATED + SMALL_SHARDED}
    dr2, gl["norm2_g"][DEPTH - 1], gl["norm2_b"][DEPTH - 1], gl["ffn_b_down"][DEPTH - 1] = _ln_res_bwd(
        f"l{DEPTH - 1}_norm2_bwd", dx, saved[-1]["xhat2"], saved[-1]["rstd2"], n2g3, DEPTH - 1)
    for i in reversed(range(DEPTH)):
        j = i // 2
        s = saved[i]
        mix = "conv" if i % 2 == 0 else "gmlp"
        g_down = mm_down_dw(f"l{i}_ffn_down_dw", s["a"], dr2, deps=take_tokens()).reshape(N_DEV, -1, d)
        send_grads(f"l{i}_ffn_down_grad_scatter_start", [("ffn_w_down", i, g_down)])
        dcg, dcv, dbg, dbv, dwg, dwv = _ffn_act_bwd(f"l{i}_ffn_act_bwd", dr2, s["w_down"], s["hg"], s["hv"],
                                                    ffn_w_dw, ffn_b_dw4, i, seq, deps=take_tokens())
        gl["ffn_b_dw"][i] = jnp.concatenate([dbg, dbv], axis=0).reshape(1, nblk * fb)
        gl["ffn_w_dw"][i] = jnp.concatenate([dwg[:, :FFN_K], dwv[:, :FFN_K]], axis=0)
        dh, dbu, dr1, gl["norm1_g"][i], gl["norm1_b"][i], gl[mix + "_b_out"][j] = _ffn_conv_t_dx(
            f"l{i}_ffn_conv_t_dx", dcg, dcv, ffn_w_dw, s["w_up"], dr2, s["xhat1"], s["rstd1"], n1g3, i, seq)
        gl["ffn_b_up"][i] = dbu.reshape(1, nblk * fb)
        send_grads(f"l{i}_ffn_up_grad_scatter_start", [("ffn_w_up", i, mm_up_dw(f"l{i}_ffn_up_dw", s["x1"], dh))])
        ds = mm_out_dx(f"l{i}_{mix}_out_dx", dr1, s["w_out"], deps=take_tokens())
        g_out = mm_out_dw(f"l{i}_{mix}_out_dw", s["s"], dr1).reshape(N_DEV, -1, d)
        if i % 2 == 0:
            dc, gl["conv_ln_g"][j], gl["conv_ln_b"][j], gl["conv_b_dw"][j] = _ln_silu_bwd(
                f"l{i}_ln_silu_bwd", ds, s["c"], conv_ln_g3, conv_ln_b3, j)
            du, dwdw = _dwconv31_bwd(f"l{i}_dwconv_bwd", dc, s["u"], conv_w_dw, j, seq)
            gl["conv_w_dw"][j] = dwdw[:CONV_K]
            dh, gl["conv_b_in"][j] = _glu_bwd(f"l{i}_glu_bwd", du, s["h"])
        else:
            dh, gl["gmlp_b_in"][j], dwt, dbt, gl["gmlp_ln_g"][j], gl["gmlp_ln_b"][j] = _sgu_bwd(
                f"l{i}_sgu_bwd", ds, s["h"], gmlp_ln_g3, gmlp_ln_b3, w_tile, bs_tile, j)
            gl["gmlp_w_s"][j], gl["gmlp_b_s"][j] = _sgu_param_grads(dwt, dbt)
        dx = mm_in_dx(f"l{i}_{mix}_in_dx", dh, s["w_in"], dr1)
        if i > 0:
            prev = saved[i - 1]
            dr2, gl["norm2_g"][i - 1], gl["norm2_b"][i - 1], gl["ffn_b_down"][i - 1] = _ln_res_bwd(
                f"l{i - 1}_norm2_bwd", dx, prev["xhat2"], prev["rstd2"], n2g3, i - 1)
        send_grads(f"l{i}_mixer_grads_scatter_start",
                   [(mix + "_w_out", j, g_out), (mix + "_w_in", j, mm_in_dw(f"l{i}_{mix}_in_dw", s["x"], dh))])
    grad_x = _from_segments(dx, TM_EW).reshape(bsz, seq, d)

    full_small = {n: jnp.stack(gl[n]).reshape(p[n].shape) for n in REPLICATED}
    shard_small = {}
    for n in SMALL_SHARDED:
        if n == "ffn_w_dw":
            shard_small[n] = jnp.stack(gl[n]).transpose(1, 0, 2, 3)
        else:
            width = p[n].shape[-1]
            lead = p[n].shape[:-1]
            shard_small[n] = _to_shards(jnp.stack(gl[n]).reshape(lead + (N_DEV * width,)), width)
    flat_shapes = [(1, LANES)] + [p[n].shape for n in REPLICATED] + [(N_DEV,) + p[n].shape for n in SMALL_SHARDED]
    flat_local = _pack([loss_row] + [full_small[n] for n in REPLICATED] + [shard_small[n] for n in SMALL_SHARDED],
                       row_multiple=ROW_TILE_CAP)

    small_send, small_recv, small_land, small_token = _gather_start(
        "small_grads_gather_start", _place_own("small_grads_place_own", [flat_local]))

    grads, delta, new_m, new_v = {}, {}, {}, {}
    dev1 = jnp.reshape(dev, (1,)).astype(jnp.int32)
    order = ["ffn_w_down", "ffn_w_up", "gmlp_w_out", "gmlp_w_in", "conv_w_out", "conv_w_in"]
    after = small_token
    for n in order:
        parts_done, lands_done = _scatter_wait(f"grads_{n}_scatter_wait", started[n], after)
        state = [p[n], p["m_" + n], p["v_" + n]]
        if n == "ffn_w_up":
            state = [a.transpose(0, 2, 1) for a in state]
        outs = _sum8_adamw(f"adamw_{n}", dev1, lands_done, parts_done, *state)
        after = outs[-1]
        if n == "ffn_w_up":
            outs = [a.transpose(0, 2, 1) for a in outs]
        grads[n], delta[n], new_m[n], new_v[n] = outs

    small_parts = _gather_wait("small_grads_gather_wait", small_land[0], small_send, small_recv, 0, after)
    summed = _unpack(_sum8("sum_small_grads", small_parts), flat_shapes)
    loss = summed[0][0, 0]
    grads.update(zip(REPLICATED, summed[1:1 + len(REPLICATED)]))
    for n, g in zip(SMALL_SHARDED, summed[1 + len(REPLICATED):]):
        grads[n] = lax.dynamic_index_in_dim(g, dev, axis=0, keepdims=False)
    small = REPLICATED + SMALL_SHARDED
    small_shp = [p[n].shape for n in small]
    d_s, m_s, v_s = _adamw_flat("adamw_small", _pack([grads[n] for n in small]), _pack([p[n] for n in small]),
                                _pack([p["m_" + n] for n in small]), _pack([p["v_" + n] for n in small]))
    for n, dd, mm, vv in zip(small, _unpack(d_s, small_shp), _unpack(m_s, small_shp), _unpack(v_s, small_shp)):
        delta[n], new_m[n], new_v[n] = dd, mm, vv

    return (loss, grad_x, *[grads[n] for n in WEIGHTS], *[delta[n] for n in WEIGHTS],
            *[new_m[n] for n in WEIGHTS], *[new_v[n] for n in WEIGHTS])


def kernel(x, conv_w_in, conv_b_in, conv_w_dw, conv_b_dw, conv_ln_g, conv_ln_b, conv_w_out, conv_b_out, gmlp_w_in, gmlp_b_in, gmlp_ln_g, gmlp_ln_b, gmlp_w_s, gmlp_b_s, gmlp_w_out, gmlp_b_out, ffn_w_up, ffn_b_up, ffn_w_dw, ffn_b_dw, ffn_w_down, ffn_b_down, norm1_g, norm1_b, norm2_g, norm2_b, loss_target, m_conv_w_in, m_conv_b_in, m_conv_w_dw, m_conv_b_dw, m_conv_ln_g, m_conv_ln_b, m_conv_w_out, m_conv_b_out, m_gmlp_w_in, m_gmlp_b_in, m_gmlp_ln_g, m_gmlp_ln_b, m_gmlp_w_s, m_gmlp_b_s, m_gmlp_w_out, m_gmlp_b_out, m_ffn_w_up, m_ffn_b_up, m_ffn_w_dw, m_ffn_b_dw, m_ffn_w_down, m_ffn_b_down, m_norm1_g, m_norm1_b, m_norm2_g, m_norm2_b, v_conv_w_in, v_conv_b_in, v_conv_w_dw, v_conv_b_dw, v_conv_ln_g, v_conv_ln_b, v_conv_w_out, v_conv_b_out, v_gmlp_w_in, v_gmlp_b_in, v_gmlp_ln_g, v_gmlp_ln_b, v_gmlp_w_s, v_gmlp_b_s, v_gmlp_w_out, v_gmlp_b_out, v_ffn_w_up, v_ffn_b_up, v_ffn_w_dw, v_ffn_b_dw, v_ffn_w_down, v_ffn_b_down, v_norm1_g, v_norm1_b, v_norm2_g, v_norm2_b):
    return _step(dict(locals()))
```

```python
import math

import jax
import jax.numpy as jnp
from jax import lax
from jax.experimental import pallas as pl
from jax.experimental.pallas import tpu as pltpu

F32 = jnp.float32
BF16 = jnp.bfloat16
MESH = pl.DeviceIdType.MESH

N_DEV = 8
DEPTH = 4
ALPHA = (2.0 * DEPTH) ** 0.25
LN_EPS = 1e-5
CONV_K = 31
FFN_K = 3
CHUNK = 128
GROUPS = 8
ADAM_LR = 0.001
ADAM_B1 = 0.9
ADAM_B2 = 0.999
ADAM_EPS = 1e-08
ADAM_WD = 0.01
ADAM_STEP = 10
INV_SQRT2 = 1.0 / math.sqrt(2.0)
INV_SQRT2PI = 1.0 / math.sqrt(2.0 * math.pi)

LANES = 128
SUBLANES = 8
VMEM_LIMIT = 48 * 1024 * 1024
TM_MM = 1024
TM_EW = 256
TM_ROW = 512


def _call(body, name, grid, in_specs, out_specs, out_shape, scratch=(), aliases=None, deps=()):
    deps = list(deps)
    in_specs = list(in_specs)
    n_in = len(in_specs)
    if deps:
        inner = body

        def body(*refs):
            return inner(*refs[:n_in], *refs[n_in + len(deps):])

        in_specs = in_specs + [pl.BlockSpec(memory_space=pl.ANY)] * len(deps)
    fn = pl.pallas_call(
        body, name=name, grid=grid, in_specs=in_specs, out_specs=out_specs, out_shape=out_shape,
        scratch_shapes=list(scratch), input_output_aliases=aliases or {},
        compiler_params=pltpu.CompilerParams(vmem_limit_bytes=VMEM_LIMIT))
    return lambda *args: fn(*args, *deps)


def _sds(shape, dtype):
    return jax.ShapeDtypeStruct(tuple(shape), dtype)


def _sigmoid(x):
    return 1.0 / (1.0 + jnp.exp(-x))


def _acc_rows(ref, val, first):
    @pl.when(first)
    def _():
        ref[...] = val

    @pl.when(jnp.logical_not(first))
    def _():
        ref[...] += val


def _colsum(v):
    return jnp.sum(v, axis=0, keepdims=True)


_DIMS = {"nn": ((1,), (0,)), "nt": ((1,), (1,)), "tn": ((0,), (0,))}


def _matmul(name, a, b, mode, *, grid, a_spec, b_spec, o_spec, o_shape, o_dtype, k_axis=None, nk=1,
            acc_shape=None, bias=None, bias_spec=None, res=None, res_spec=None, res_scale=1.0, deps=()):
    dims = (_DIMS[mode], ((), ()))
    has_bias, has_res = bias is not None, res is not None

    def body(*refs):
        a_ref, b_ref = refs[0], refs[1]
        pos = 2
        bias_ref = res_ref = None
        if has_bias:
            bias_ref = refs[pos]
            pos += 1
        if has_res:
            res_ref = refs[pos]
            pos += 1
        o_ref = refs[pos]
        acc_ref = refs[pos + 1] if nk > 1 else None
        p = lax.dot_general(a_ref[...].astype(BF16), b_ref[...].astype(BF16), dims, preferred_element_type=F32)

        def finish(acc):
            if has_bias:
                acc = acc + bias_ref[...]
            if has_res:
                acc = acc + res_scale * res_ref[...]
            o_ref[...] = acc.astype(o_dtype)

        if nk == 1:
            finish(p)
        else:
            k = pl.program_id(k_axis)

            @pl.when(k == 0)
            def _():
                acc_ref[...] = p

            @pl.when(k > 0)
            def _():
                acc_ref[...] += p

            @pl.when(k == nk - 1)
            def _():
                finish(acc_ref[...])

    ins, specs = [a, b], [a_spec, b_spec]
    if has_bias:
        ins.append(bias)
        specs.append(bias_spec)
    if has_res:
        ins.append(res)
        specs.append(res_spec)
    scratch = [pltpu.VMEM(acc_shape, F32)] if nk > 1 else []
    return _call(body, name, grid, specs, o_spec, _sds(o_shape, o_dtype), scratch, deps=deps)(*ins)


TM_LN = 512


def _matmul_ln(name, a, b, x_res, bias3, g3, b3, l_bias, l_norm, *, nk, a_block, b_block):
    t, d = x_res.shape
    tm = min(TM_LN, t)

    def body(a_ref, b_ref, x_ref, bias_ref, g_ref, be_ref, o_ref, xh_ref, rs_ref, *acc):
        p = jnp.dot(a_ref[...].astype(BF16), b_ref[...], preferred_element_type=F32)

        def finish(y):
            xhat, rstd = _ln_stats(ALPHA * x_ref[...] + y + bias_ref[...])
            o_ref[...] = xhat * g_ref[...] + be_ref[...]
            xh_ref[...] = xhat
            rs_ref[...] = rstd

        if nk == 1:
            finish(p)
        else:
            k = pl.program_id(1)

            @pl.when(k == 0)
            def _():
                acc[0][...] = p

            @pl.when(k > 0)
            def _():
                acc[0][...] += p

            @pl.when(k == nk - 1)
            def _():
                finish(acc[0][...])

    if nk == 1:
        grid = (t // tm,)
        a_spec = pl.BlockSpec(a_block, lambda i: (i, 0))
        b_spec = pl.BlockSpec(b_block, lambda i: (0, 0))
    else:
        grid = (t // tm, nk)
        a_spec = pl.BlockSpec(a_block, lambda i, k: (k, i, 0))
        b_spec = pl.BlockSpec(b_block, lambda i, k: (k, 0, 0))
    row = pl.BlockSpec((tm, d), lambda i, *_: (i, 0))
    stat = pl.BlockSpec((tm, 1), lambda i, *_: (i, 0))

    def vec(l):
        return pl.BlockSpec((None, 1, d), lambda *_: (l, 0, 0))

    return _call(body, name, grid, [a_spec, b_spec, row, vec(l_bias), vec(l_norm), vec(l_norm)], [row, row, stat],
                 [_sds((t, d), F32), _sds((t, d), F32), _sds((t, 1), F32)],
                 [pltpu.VMEM((tm, d), F32)] if nk > 1 else [])(a, b, x_res, bias3, g3, b3)


def _mesh_pos():
    return lax.axis_index("x"), lax.axis_index("y"), lax.axis_index("c")


def _any_specs(n):
    return [pl.BlockSpec(memory_space=pl.ANY)] * n


def _all_gather(name, srcs):
    n = len(srcs)

    def body(*refs):
        src, out = refs[:n], refs[n:2 * n]
        send_sems, recv_sems, local_sems = refs[2 * n:]
        x, y, c = _mesh_pos()
        me, sibling = (x, y, c), (x, y, 1 - c)
        chips = [(1 - x, y), (x, 1 - y), (1 - x, 1 - y)]

        def slot(k, p):
            return out[k].at[:, 4 * p[0] + 2 * p[1] + p[2]]

        def copy(k, idx, block, to, s=None):
            return pltpu.make_async_remote_copy(
                src_ref=slot(k, block) if s is None else s, dst_ref=slot(k, block),
                send_sem=send_sems.at[k * 7 + idx], recv_sem=recv_sems.at[k * 7 + idx],
                device_id=to, device_id_type=MESH)

        local = [pltpu.make_async_copy(src[k], slot(k, me), local_sems.at[k]) for k in range(n)]
        for cp in local:
            cp.start()
        first = []
        for k in range(n):
            first.append(copy(k, 0, me, sibling, src[k]))
            for j, chip in enumerate(chips):
                first.append(copy(k, 1 + j, me, (*chip, c), src[k]))
        for cp in first:
            cp.start()
        passed = []
        for j, chip in enumerate(chips):
            for k in range(n):
                copy(k, 1 + j, (*chip, c), me).wait_recv()
                cp = copy(k, 4 + j, (*chip, c), sibling)
                cp.start()
                passed.append(cp)
        for k in range(n):
            copy(k, 0, sibling, me).wait_recv()
            for j, chip in enumerate(chips):
                copy(k, 4 + j, (*chip, 1 - c), me).wait_recv()
        for cp in first + passed:
            cp.wait_send()
        for cp in local:
            cp.wait()

    out_shape = [_sds((s.shape[0], N_DEV) + s.shape[1:], s.dtype) for s in srcs]
    return _call(body, name, (), [pl.BlockSpec(memory_space=pltpu.VMEM)] * n, _any_specs(n), out_shape,
                 [pltpu.SemaphoreType.DMA((7 * n,)), pltpu.SemaphoreType.DMA((7 * n,)),
                  pltpu.SemaphoreType.DMA((n,))])(*srcs)


HBM_SPEC = pl.BlockSpec(memory_space=pltpu.HBM)
SEM_SPEC = pl.BlockSpec(memory_space=pltpu.SEMAPHORE)
N_PEER = N_DEV - 1


def _split_call(body, name, in_specs, out_specs, out_shape, aliases):
    return pl.pallas_call(
        body, name=name, in_specs=in_specs, out_specs=out_specs, out_shape=out_shape, input_output_aliases=aliases,
        compiler_params=pltpu.CompilerParams(has_side_effects=pltpu.SideEffectType.DATAFLOW_SIDE_EFFECTING))


def _peers(x, y, c):
    return [(1 - x if q & 4 else x, 1 - y if q & 2 else y, 1 - c if q & 1 else c) for q in range(1, N_DEV)]


def _in_hbm(a):
    return pltpu.with_memory_space_constraint(a, pltpu.HBM)


def _place_own(name, srcs, deps=()):
    n = len(srcs)

    def body(*refs):
        src, out, sems = refs[:n], refs[n:2 * n], refs[2 * n]
        x, y, c = _mesh_pos()
        dev = 4 * x + 2 * y + c
        copies = [pltpu.make_async_copy(src[k], out[k].at[dev], sems.at[k]) for k in range(n)]
        for cp in copies:
            cp.start()
        for cp in copies:
            cp.wait()

    return _call(body, name, (), [pl.BlockSpec(memory_space=pltpu.VMEM)] * n, _any_specs(n),
                 [_sds((N_DEV,) + s.shape, s.dtype) for s in srcs], [pltpu.SemaphoreType.DMA((n,))],
                 deps=deps)(*srcs)


def _gather_start(name, lands):
    n = len(lands)

    def body(*refs):
        land, send_sems, recv_sems = refs[:n], refs[n], refs[n + 1]
        x, y, c = _mesh_pos()
        dev = 4 * x + 2 * y + c
        for k in range(n):
            for peer in _peers(x, y, c):
                pltpu.make_async_remote_copy(
                    src_ref=land[k].at[dev], dst_ref=land[k].at[dev], send_sem=send_sems.at[k],
                    recv_sem=recv_sems.at[k], device_id=peer, device_id_type=MESH).start()
        token = refs[-1]
        token[...] = jnp.zeros_like(token)

    outs = _split_call(
        body, name, [HBM_SPEC] * n, [SEM_SPEC, SEM_SPEC] + [HBM_SPEC] * n + [pl.BlockSpec(memory_space=pltpu.VMEM)],
        [pltpu.SemaphoreType.DMA((n,)), pltpu.SemaphoreType.DMA((n,))] + [pltpu.HBM(a.shape, a.dtype) for a in lands]
        + [_sds((SUBLANES, LANES), F32)],
        {k: 2 + k for k in range(n)})(*[_in_hbm(a) for a in lands])
    return outs[0], outs[1], list(outs[2:2 + n]), outs[-1]


def _wait_seven(src_ref, dst_ref, send_sem, recv_sem):
    cp = pltpu.make_async_remote_copy(
        src_ref=src_ref.at[pl.ds(0, N_PEER)], dst_ref=dst_ref.at[pl.ds(0, N_PEER)], send_sem=send_sem,
        recv_sem=recv_sem, device_id=_mesh_pos(), device_id_type=MESH)
    cp.wait_send()
    cp.wait_recv()


def _gather_wait(name, land, send_sems, recv_sems, k, after):
    def body(land_ref, send_ref, recv_ref, after_ref, out_ref):
        _wait_seven(land_ref, land_ref, send_ref.at[k], recv_ref.at[k])

    return _split_call(body, name, [HBM_SPEC, SEM_SPEC, SEM_SPEC, pl.BlockSpec(memory_space=pl.ANY)], HBM_SPEC,
                       pltpu.HBM(land.shape, land.dtype), {0: 0})(land, send_sems, recv_sems, after)


def _scatter_start(name, parts_list):
    n = len(parts_list)

    def body(*refs):
        x, y, c = _mesh_pos()
        dev = 4 * x + 2 * y + c
        for k in range(n):
            parts_ref, land_ref = refs[2 * k], refs[2 * k + 1]
            send_sem, recv_sem = refs[2 * n + 4 * k], refs[2 * n + 4 * k + 1]
            for peer in _peers(x, y, c):
                pltpu.make_async_remote_copy(
                    src_ref=parts_ref.at[4 * peer[0] + 2 * peer[1] + peer[2]], dst_ref=land_ref.at[dev],
                    send_sem=send_sem, recv_sem=recv_sem, device_id=peer, device_id_type=MESH).start()
        token = refs[-1]
        token[...] = jnp.zeros_like(token)

    ins, out_specs, out_shape, aliases = [], [], [], {}
    for k, parts in enumerate(parts_list):
        buf = pltpu.HBM(parts.shape, parts.dtype)
        ins += [_in_hbm(parts), _in_hbm(lax.empty(parts.shape, parts.dtype))]
        out_specs += [SEM_SPEC, SEM_SPEC, HBM_SPEC, HBM_SPEC]
        out_shape += [pltpu.SemaphoreType.DMA(()), pltpu.SemaphoreType.DMA(()), buf, buf]
        aliases.update({2 * k: 4 * k + 2, 2 * k + 1: 4 * k + 3})
    outs = _split_call(body, name, [HBM_SPEC] * (2 * n), out_specs + [pl.BlockSpec(memory_space=pltpu.VMEM)],
                       out_shape + [_sds((SUBLANES, LANES), F32)], aliases)(*ins)
    return [tuple(outs[4 * k:4 * k + 4]) for k in range(n)], outs[-1]


def _scatter_wait(name, started, after):
    n = len(started)

    def body(*refs):
        for k in range(n):
            send_sem, recv_sem, parts_ref, land_ref = refs[4 * k:4 * k + 4]
            _wait_seven(parts_ref, land_ref, send_sem, recv_sem)

    flat = [a for s in started for a in s]
    outs = _split_call(
        body, name, [SEM_SPEC, SEM_SPEC, HBM_SPEC, HBM_SPEC] * n + [pl.BlockSpec(memory_space=pl.ANY)],
        [HBM_SPEC, HBM_SPEC] * n, [pltpu.HBM(a.shape, a.dtype) for s in started for a in s[2:]],
        {4 * k + 2 + t: 2 * k + t for k in range(n) for t in range(2)})(*flat, after)
    return list(outs[0::2]), list(outs[1::2])


def _to_segments(a, tile):
    seg = tile // SUBLANES
    return a.reshape((a.shape[0] // tile, SUBLANES, seg) + a.shape[1:]).swapaxes(1, 2).reshape(a.shape)


def _from_segments(a, tile):
    seg = tile // SUBLANES
    return a.reshape((a.shape[0] // tile, seg, SUBLANES) + a.shape[1:]).swapaxes(1, 2).reshape(a.shape)


def _chunk(ref, q):
    return ref[q * SUBLANES:(q + 1) * SUBLANES, :]


def _fill_wrap_prev(x_ref, halo_ref, wrap_ref, n_wrap, n_halo, seg, keep):
    sub = lax.broadcasted_iota(jnp.int32, (SUBLANES, x_ref.shape[-1]), 0)
    for j in range(n_wrap):
        q = seg - n_wrap + j
        hq = q - (seg - n_halo)
        row = halo_ref[hq * SUBLANES + SUBLANES - 1:(hq + 1) * SUBLANES, :] * keep
        wrap_ref[j * SUBLANES:(j + 1) * SUBLANES, :] = jnp.where(sub == 0, row, pltpu.roll(_chunk(x_ref, q), 1, 0))


def _fill_wrap_next(x_ref, halo_ref, wrap_ref, n_wrap, keep):
    sub = lax.broadcasted_iota(jnp.int32, (SUBLANES, x_ref.shape[-1]), 0)
    for j in range(n_wrap):
        row = halo_ref[j * SUBLANES:j * SUBLANES + 1, :] * keep
        wrap_ref[j * SUBLANES:(j + 1) * SUBLANES, :] = jnp.where(
            sub == SUBLANES - 1, row, pltpu.roll(_chunk(x_ref, j), SUBLANES - 1, 0))


def _past(x_ref, wrap_ref, q, d, n_wrap):
    return _chunk(x_ref, q - d) if q >= d else _chunk(wrap_ref, q - d + n_wrap)


def _future(x_ref, wrap_ref, q, d, seg):
    return _chunk(x_ref, q + d) if q + d < seg else _chunk(wrap_ref, q + d - seg)


def _conv_fwd(x_ref, wrap_ref, w_ref, b_ref, out_ref, seg, k_taps):
    bias = jnp.broadcast_to(b_ref[...], (SUBLANES, x_ref.shape[-1]))
    for q in range(seg):
        acc = bias
        for k in range(k_taps):
            acc = acc + w_ref[k:k + 1, :] * _past(x_ref, wrap_ref, q, k_taps - 1 - k, k_taps - 1)
        out_ref[q * SUBLANES:(q + 1) * SUBLANES, :] = acc


def _conv_bwd_data(d_ref, wrap_ref, w_ref, out_ref, seg, k_taps):
    for q in range(seg):
        acc = None
        for k in range(k_taps):
            term = w_ref[k:k + 1, :] * _future(d_ref, wrap_ref, q, k_taps - 1 - k, seg)
            acc = term if acc is None else acc + term
        out_ref[q * SUBLANES:(q + 1) * SUBLANES, :] = acc


def _conv_bwd_taps(d_ref, x_ref, wrap_ref, dw_ref, seg, k_taps):
    for k in range(k_taps):
        part = None
        for q in range(seg):
            term = _chunk(d_ref, q) * _past(x_ref, wrap_ref, q, k_taps - 1 - k, k_taps - 1)
            part = term if part is None else part + term
        dw_ref[k:k + 1, :] += _colsum(part)


def _tile_halo_specs(tm, width_block, n_halo, n_tiles, block_of):
    rows = n_halo * SUBLANES
    per = tm // rows
    tile = pl.BlockSpec(width_block(tm), lambda n, i: block_of(n, i))
    prev = pl.BlockSpec(width_block(rows), lambda n, i: block_of(n, jnp.maximum(i * per - 1, 0)))
    nxt = pl.BlockSpec(width_block(rows), lambda n, i: block_of(n, jnp.minimum((i + 1) * per, n_tiles * per - 1)))
    return tile, prev, nxt


def _ln_stats(v):
    mu = jnp.mean(v, axis=-1, keepdims=True)
    vc = v - mu
    var = jnp.mean(vc * vc, axis=-1, keepdims=True)
    rstd = lax.rsqrt(var + LN_EPS)
    return vc * rstd, rstd


def _ln_backward(dxhat, xhat, rstd):
    m1 = jnp.mean(dxhat, axis=-1, keepdims=True)
    m2 = jnp.mean(dxhat * xhat, axis=-1, keepdims=True)
    return rstd * (dxhat - m1 - xhat * m2)


def _row_spec(tm, width):
    return pl.BlockSpec((tm, width), lambda i: (i, 0))


def _param_spec(l, width):
    return pl.BlockSpec((None, 1, width), lambda *_: (l, 0, 0))


def _ln_bwd_rows(dout, xh_ref, rs_ref, g_ref, dr_ref, dg_ref, db_ref, dsum_ref, first):
    xhat = xh_ref[...]
    dr = _ln_backward(dout * g_ref[...], xhat, rs_ref[...])
    dr_ref[...] = dr
    _acc_rows(dg_ref, _colsum(dout * xhat), first)
    _acc_rows(db_ref, _colsum(dout), first)
    _acc_rows(dsum_ref, _colsum(dr), first)


def _ln_bwd_specs(tm, d, l, row_of):
    vec = pl.BlockSpec((1, d), lambda *_: (0, 0))
    ins = [pl.BlockSpec((tm, d), row_of), pl.BlockSpec((tm, 1), row_of), _param_spec(l, d)]
    return ins, [pl.BlockSpec((tm, d), row_of), vec, vec, vec]


def _ln_res_bwd(name, dout, xhat, rstd, g3, l, deps=()):
    t, d = dout.shape
    tm = min(TM_ROW, t)

    def body(do_ref, xh_ref, rs_ref, g_ref, dr_ref, dg_ref, db_ref, dc_ref):
        _ln_bwd_rows(do_ref[...], xh_ref, rs_ref, g_ref, dr_ref, dg_ref, db_ref, dc_ref, pl.program_id(0) == 0)

    ins, outs = _ln_bwd_specs(tm, d, l, lambda i: (i, 0))
    return _call(body, name, (t // tm,), [_row_spec(tm, d)] + ins, outs,
                 [_sds((t, d), F32)] + [_sds((1, d), F32)] * 3, deps=deps)(dout, xhat, rstd, g3)


def _glu(name, h):
    t, c2 = h.shape
    c = c2 // 2
    tm = min(TM_ROW, t)

    def body(a_ref, g_ref, o_ref):
        o_ref[...] = a_ref[...] * _sigmoid(g_ref[...])

    return _call(body, name, (t // tm,),
                 [pl.BlockSpec((tm, c), lambda i: (i, 0)), pl.BlockSpec((tm, c), lambda i: (i, 1))],
                 _row_spec(tm, c), _sds((t, c), F32))(h, h)


def _glu_bwd(name, du, h):
    t, c2 = h.shape
    c = c2 // 2
    tm = min(TM_ROW, t)

    def body(du_ref, a_ref, g_ref, dh_ref, db_ref):
        first = pl.program_id(0) == 0
        du_v, a = du_ref[...], a_ref[...]
        sg = _sigmoid(g_ref[...])
        da = du_v * sg
        dg = du_v * a * sg * (1.0 - sg)
        dh_ref[:, :c] = da.astype(BF16)
        dh_ref[:, c:] = dg.astype(BF16)
        _acc_rows(db_ref.at[:, :c], _colsum(da), first)
        _acc_rows(db_ref.at[:, c:], _colsum(dg), first)

    return _call(body, name, (t // tm,),
                 [_row_spec(tm, c), pl.BlockSpec((tm, c), lambda i: (i, 0)), pl.BlockSpec((tm, c), lambda i: (i, 1))],
                 [_row_spec(tm, c2), pl.BlockSpec((1, c2), lambda i: (0, 0))],
                 [_sds((t, c2), BF16), _sds((1, c2), F32)])(du, h, h)


CONV_CB = 512
TAPS_PAD = 32


def _dwconv31(name, u, w3, b3, l, seq):
    t, c = u.shape
    tm, cb = TM_EW, CONV_CB
    seg, seq_tiles, n_tiles = tm // SUBLANES, seq // tm, t // tm
    n_wrap = CONV_K - 1
    tile, prev, _ = _tile_halo_specs(tm, lambda rows: (rows, cb), seg, n_tiles, lambda n, r: (r, n))

    def body(u_ref, halo_ref, w_ref, b_ref, o_ref, wrap_ref):
        keep = (pl.program_id(1) % seq_tiles != 0).astype(F32)
        _fill_wrap_prev(u_ref, halo_ref, wrap_ref, n_wrap, seg, seg, keep)
        _conv_fwd(u_ref, wrap_ref, w_ref, b_ref, o_ref, seg, CONV_K)

    return _call(body, name, (c // cb, n_tiles),
                 [tile, prev, pl.BlockSpec((None, CONV_K, cb), lambda n, i: (l, 0, n)),
                  pl.BlockSpec((None, 1, cb), lambda n, i: (l, 0, n))],
                 tile, _sds((t, c), F32), [pltpu.VMEM((n_wrap * SUBLANES, cb), F32)])(u, u, w3, b3)


def _dwconv31_bwd(name, dc, u, w3, l, seq):
    t, c = dc.shape
    tm, cb = TM_EW, CONV_CB
    seg, seq_tiles, n_tiles = tm // SUBLANES, seq // tm, t // tm
    n_wrap = CONV_K - 1
    tile, prev, nxt = _tile_halo_specs(tm, lambda rows: (rows, cb), seg, n_tiles, lambda n, r: (r, n))

    def body(dc_ref, dcn_ref, u_ref, up_ref, w_ref, du_ref, dw_ref, dwrap_ref, uwrap_ref):
        i = pl.program_id(1)
        keep_prev = (i % seq_tiles != 0).astype(F32)
        keep_next = (i % seq_tiles != seq_tiles - 1).astype(F32)
        _fill_wrap_next(dc_ref, dcn_ref, dwrap_ref, n_wrap, keep_next)
        _conv_bwd_data(dc_ref, dwrap_ref, w_ref, du_ref, seg, CONV_K)

        @pl.when(i == 0)
        def _():
            dw_ref[...] = jnp.zeros_like(dw_ref)

        _fill_wrap_prev(u_ref, up_ref, uwrap_ref, n_wrap, seg, seg, keep_prev)
        _conv_bwd_taps(dc_ref, u_ref, uwrap_ref, dw_ref, seg, CONV_K)

    wrap = pltpu.VMEM((n_wrap * SUBLANES, cb), F32)
    return _call(body, name, (c // cb, n_tiles),
                 [tile, nxt, tile, prev, pl.BlockSpec((None, CONV_K, cb), lambda n, i: (l, 0, n))],
                 [tile, pl.BlockSpec((TAPS_PAD, cb), lambda n, i: (0, n))],
                 [_sds((t, c), F32), _sds((TAPS_PAD, c), F32)], [wrap, wrap])(dc, dc, u, u, w3)


def _ln_silu(name, cx, g3, b3, l):
    t, d = cx.shape
    tm = min(TM_ROW, t)

    def body(c_ref, g_ref, b_ref, o_ref):
        xhat, _ = _ln_stats(c_ref[...])
        nv = xhat * g_ref[...] + b_ref[...]
        o_ref[...] = (nv * _sigmoid(nv)).astype(BF16)

    return _call(body, name, (t // tm,), [_row_spec(tm, d), _param_spec(l, d), _param_spec(l, d)],
                 _row_spec(tm, d), _sds((t, d), BF16))(cx, g3, b3)


def _ln_silu_bwd(name, ds, cx, g3, b3, l, deps=()):
    t, d = cx.shape
    tm = min(TM_ROW, t)

    def body(ds_ref, c_ref, g_ref, b_ref, dc_ref, dg_ref, db_ref, dsum_ref):
        first = pl.program_id(0) == 0
        xhat, rstd = _ln_stats(c_ref[...])
        g = g_ref[...]
        nv = xhat * g + b_ref[...]
        sg = _sigmoid(nv)
        dn = ds_ref[...] * (sg * (1.0 + nv * (1.0 - sg)))
        dc = _ln_backward(dn * g, xhat, rstd)
        dc_ref[...] = dc
        _acc_rows(dg_ref, _colsum(dn * xhat), first)
        _acc_rows(db_ref, _colsum(dn), first)
        _acc_rows(dsum_ref, _colsum(dc), first)

    vec = pl.BlockSpec((1, d), lambda i: (0, 0))
    return _call(body, name, (t // tm,),
                 [_row_spec(tm, d), _row_spec(tm, d), _param_spec(l, d), _param_spec(l, d)],
                 [_row_spec(tm, d), vec, vec, vec],
                 [_sds((t, d), F32)] + [_sds((1, d), F32)] * 3, deps=deps)(ds, cx, g3, b3)


FFN_HALO = FFN_K - 1


def _ffn_conv(x_ref, halo_ref, wrap_ref, w_ref, b_ref, keep, seg, out_ref):
    _fill_wrap_prev(x_ref, halo_ref, wrap_ref, FFN_K - 1, FFN_HALO, seg, keep)
    _conv_fwd(x_ref, wrap_ref, w_ref, b_ref, out_ref, seg, FFN_K)


TM_FFN = 512
WRAP_ROWS = FFN_HALO * SUBLANES


def _sub_tiles(x_ref, prev_ref, next_ref, keep_prev, keep_next, n_sub):
    out = []
    for s in range(n_sub):
        tile = x_ref.at[pl.ds(s * TM_EW, TM_EW)]
        prev = prev_ref if s == 0 else x_ref.at[pl.ds(s * TM_EW - WRAP_ROWS, WRAP_ROWS)]
        nxt = next_ref if s == n_sub - 1 else x_ref.at[pl.ds((s + 1) * TM_EW, WRAP_ROWS)]
        out.append((tile, prev, keep_prev if s == 0 else 1.0, nxt, keep_next if s == n_sub - 1 else 1.0))
    return out


def _rows(ref, s, rows):
    return ref.at[pl.ds(s * rows, rows)]


def _ffn_specs(tm, fb, n_tiles):
    return _tile_halo_specs(tm, lambda rows: (None, rows, fb), FFN_HALO, n_tiles, lambda n, r: (n, r, 0))


def _ffn_up_act(name, x, w_up, b_up4, wdw, bdw, l, seq):
    t, d = x.shape
    nb, fb, _ = w_up.shape
    half = nb // 2
    tm = min(TM_FFN, seq)
    n_sub, seg, seq_steps, n_steps = tm // TM_EW, TM_EW // SUBLANES, seq // tm, t // tm
    per = tm // WRAP_ROWS
    nt_dims = (_DIMS["nt"], ((), ()))

    def body(x_ref, xp_ref, ug_ref, uv_ref, bug_ref, buv_ref, wg_ref, wv_ref, bg_ref, bv_ref,
             hg_ref, hv_ref, a_ref, pg_ref, pv_ref, gwrap_ref, vwrap_ref, cg_ref, cv_ref):
        keep = (pl.program_id(1) % seq_steps != 0).astype(F32)
        xb, xpb = x_ref[...].astype(BF16), xp_ref[...].astype(BF16)
        hg_ref[...] = lax.dot_general(xb, ug_ref[...], nt_dims, preferred_element_type=F32) + bug_ref[...]
        pg_ref[...] = lax.dot_general(xpb, ug_ref[...], nt_dims, preferred_element_type=F32) + bug_ref[...]
        for s, (tile, prev, kp, _, _) in enumerate(_sub_tiles(hg_ref, pg_ref, None, keep, None, n_sub)):
            _ffn_conv(tile, prev, gwrap_ref, wg_ref, bg_ref, kp, seg, _rows(cg_ref, s, TM_EW))
        hv_ref[...] = lax.dot_general(xb, uv_ref[...], nt_dims, preferred_element_type=F32) + buv_ref[...]
        pv_ref[...] = lax.dot_general(xpb, uv_ref[...], nt_dims, preferred_element_type=F32) + buv_ref[...]
        for s, (tile, prev, kp, _, _) in enumerate(_sub_tiles(hv_ref, pv_ref, None, keep, None, n_sub)):
            _ffn_conv(tile, prev, vwrap_ref, wv_ref, bv_ref, kp, seg, _rows(cv_ref, s, TM_EW))
        cg = cg_ref[...]
        a_ref[...] = (cg * _sigmoid(cg) * cv_ref[...]).astype(BF16)

    def blk(shift):
        return pl.BlockSpec((None, fb, d), lambda n, i: (n + shift, 0, 0))

    def vec(shift, rows):
        return pl.BlockSpec((None, None, rows, fb), lambda n, i: (l, n + shift, 0, 0))

    out = pl.BlockSpec((None, tm, fb), lambda n, i: (n, i, 0))
    tmp = pltpu.VMEM((tm, fb), F32)
    halo = pltpu.VMEM((WRAP_ROWS, fb), F32)
    return _call(body, name, (half, n_steps),
                 [pl.BlockSpec((tm, d), lambda n, i: (i, 0)),
                  pl.BlockSpec((WRAP_ROWS, d), lambda n, i: (jnp.maximum(i * per - 1, 0), 0)),
                  blk(0), blk(half), vec(0, 1), vec(half, 1), vec(0, FFN_K), vec(half, FFN_K), vec(0, 1), vec(half, 1)],
                 [out, out, out],
                 [_sds((half, t, fb), F32), _sds((half, t, fb), F32), _sds((half, t, fb), BF16)],
                 [halo, halo, halo, halo, tmp, tmp])(x, x, w_up, w_up, b_up4, b_up4, wdw, wdw, bdw, bdw)


def _ffn_act_bwd(name, dy, w_down, hg, hv, wdw, bdw, l, seq, deps=()):
    half, t, fb = hg.shape
    d = dy.shape[-1]
    tm = min(TM_FFN, seq)
    n_sub, seg, seq_steps, n_steps = tm // TM_EW, TM_EW // SUBLANES, seq // tm, t // tm
    tile, prev, _ = _ffn_specs(tm, fb, n_steps)

    def body(dy_ref, wd_ref, g_ref, gp_ref, v_ref, vp_ref, wg_ref, wv_ref, bg_ref, bv_ref,
             dg_ref, dv_ref, dbg_ref, dbv_ref, dwg_ref, dwv_ref, gwrap_ref, vwrap_ref, cg_ref, cv_ref):
        i = pl.program_id(1)
        first = i == 0
        keep = (i % seq_steps != 0).astype(F32)
        da = lax.dot_general(dy_ref[...].astype(BF16), wd_ref[...], (_DIMS["nt"], ((), ())),
                             preferred_element_type=F32)
        g_tiles = _sub_tiles(g_ref, gp_ref, None, keep, None, n_sub)
        v_tiles = _sub_tiles(v_ref, vp_ref, None, keep, None, n_sub)
        for s in range(n_sub):
            _ffn_conv(g_tiles[s][0], g_tiles[s][1], _rows(gwrap_ref, s, WRAP_ROWS), wg_ref, bg_ref, g_tiles[s][2],
                      seg, _rows(cg_ref, s, TM_EW))
            _ffn_conv(v_tiles[s][0], v_tiles[s][1], _rows(vwrap_ref, s, WRAP_ROWS), wv_ref, bv_ref, v_tiles[s][2],
                      seg, _rows(cv_ref, s, TM_EW))
        cg, cv = cg_ref[...], cv_ref[...]
        sg = _sigmoid(cg)
        dcv = da * cg * sg
        dcg = da * cv * sg * (1.0 + cg * (1.0 - sg))
        dg_ref[...] = dcg
        dv_ref[...] = dcv
        _acc_rows(dbg_ref, _colsum(dcg), first)
        _acc_rows(dbv_ref, _colsum(dcv), first)

        @pl.when(first)
        def _():
            dwg_ref[...] = jnp.zeros_like(dwg_ref)
            dwv_ref[...] = jnp.zeros_like(dwv_ref)

        for s in range(n_sub):
            _conv_bwd_taps(_rows(dg_ref, s, TM_EW), g_tiles[s][0], _rows(gwrap_ref, s, WRAP_ROWS), dwg_ref, seg, FFN_K)
            _conv_bwd_taps(_rows(dv_ref, s, TM_EW), v_tiles[s][0], _rows(vwrap_ref, s, WRAP_ROWS), dwv_ref, seg, FFN_K)

    def vec(shift, rows):
        return pl.BlockSpec((None, None, rows, fb), lambda n, i: (l, n + shift, 0, 0))

    def acc(rows):
        return pl.BlockSpec((None, rows, fb), lambda n, i: (n, 0, 0))

    wrap = pltpu.VMEM((n_sub * WRAP_ROWS, fb), F32)
    tmp = pltpu.VMEM((tm, fb), F32)
    return _call(body, name, (half, n_steps),
                 [pl.BlockSpec((tm, d), lambda n, i: (i, 0)), pl.BlockSpec((None, fb, d), lambda n, i: (n, 0, 0)),
                  tile, prev, tile, prev, vec(0, FFN_K), vec(half, FFN_K), vec(0, 1), vec(half, 1)],
                 [tile, tile, acc(1), acc(1), acc(SUBLANES), acc(SUBLANES)],
                 [_sds((half, t, fb), F32), _sds((half, t, fb), F32), _sds((half, 1, fb), F32),
                  _sds((half, 1, fb), F32), _sds((half, SUBLANES, fb), F32), _sds((half, SUBLANES, fb), F32)],
                 [wrap, wrap, tmp, tmp], deps=deps)(dy, w_down, hg, hg, hv, hv, wdw, wdw, bdw, bdw)


def _ffn_conv_t_dx(name, dcg, dcv, wdw, w_up, res, xhat, rstd, g3, l, seq, deps=()):
    half, t, fb = dcg.shape
    nb, d = 2 * half, res.shape[-1]
    tm = min(TM_FFN, seq)
    n_sub, seg, seq_steps, n_steps = tm // TM_EW, TM_EW // SUBLANES, seq // tm, t // tm
    per = tm // WRAP_ROWS

    def body(g_ref, gn_ref, v_ref, vn_ref, w_ref, up_ref, res_ref, xh_ref, rs_ref, gam_ref,
             dh_ref, db_ref, dr_ref, dgam_ref, dbeta_ref, dsum_ref, sel_ref, seln_ref, wrap_ref, out_ref, acc_ref):
        i, n = pl.program_id(0), pl.program_id(1)
        keep = (i % seq_steps != seq_steps - 1).astype(F32)

        @pl.when(n < half)
        def _():
            sel_ref[...] = g_ref[...]
            seln_ref[...] = gn_ref[...]

        @pl.when(n >= half)
        def _():
            sel_ref[...] = v_ref[...]
            seln_ref[...] = vn_ref[...]

        for s, (sub, _, _, nx, kn) in enumerate(_sub_tiles(sel_ref, None, seln_ref, None, keep, n_sub)):
            _fill_wrap_next(sub, nx, wrap_ref, FFN_K - 1, kn)
            _conv_bwd_data(sub, wrap_ref, w_ref, _rows(out_ref, s, TM_EW), seg, FFN_K)
        dh = out_ref[...]
        dhb = dh.astype(BF16)
        dh_ref[...] = dhb
        _acc_rows(db_ref.at[n], _colsum(dh), i == 0)
        p = jnp.dot(dhb, up_ref[...], preferred_element_type=F32)

        @pl.when(n == 0)
        def _():
            acc_ref[...] = p

        @pl.when(n > 0)
        def _():
            acc_ref[...] += p

        @pl.when(n == nb - 1)
        def _():
            _ln_bwd_rows(acc_ref[...] + ALPHA * res_ref[...], xh_ref, rs_ref, gam_ref, dr_ref, dgam_ref, dbeta_ref,
                         dsum_ref, i == 0)

    def src(gate):
        def blk(n):
            return jnp.minimum(n, half - 1) if gate else jnp.maximum(n - half, 0)
        tile = pl.BlockSpec((None, tm, fb), lambda i, n: (blk(n), i, 0))
        nxt = pl.BlockSpec((None, WRAP_ROWS, fb),
                           lambda i, n: (blk(n), jnp.minimum((i + 1) * per, n_steps * per - 1), 0))
        return [tile, nxt]

    row = pl.BlockSpec((tm, d), lambda i, n: (i, 0))
    ln_ins, ln_outs = _ln_bwd_specs(tm, d, l, lambda i, n: (i, 0))
    tmp = pltpu.VMEM((tm, fb), F32)
    halo = pltpu.VMEM((WRAP_ROWS, fb), F32)
    return _call(body, name, (n_steps, nb),
                 src(True) + src(False) +
                 [pl.BlockSpec((None, None, FFN_K, fb), lambda i, n: (l, n, 0, 0)),
                  pl.BlockSpec((None, fb, d), lambda i, n: (n, 0, 0)), row] + ln_ins,
                 [pl.BlockSpec((None, tm, fb), lambda i, n: (n, i, 0)),
                  pl.BlockSpec((nb, 1, fb), lambda i, n: (0, 0, 0))] + ln_outs,
                 [_sds((nb, t, fb), BF16), _sds((nb, 1, fb), F32), _sds((t, d), F32)] + [_sds((1, d), F32)] * 3,
                 [tmp, halo, halo, tmp, pltpu.VMEM((tm, d), F32)],
                 deps=deps)(dcg, dcg, dcv, dcv, wdw, w_up, res, xhat, rstd, g3)


def _gelu_parts(h):
    cdf = 0.5 * (1.0 + lax.erf(h * INV_SQRT2))
    return h * cdf, cdf


def _seg_axis(a, axis, fn):
    return jnp.moveaxis(fn(jnp.moveaxis(a, axis, 0), TM_EW), 0, axis)


def _sgu_operands(w_s, b_s):
    nl = w_s.shape[0]
    n_sub = TM_EW // CHUNK
    tril = jnp.tril(jnp.ones((CHUNK, CHUNK), dtype=bool))
    w_causal = jnp.where(tril, w_s, 0.0)
    w_tile = (jnp.eye(n_sub, dtype=F32)[None, None, :, None, :, None] * w_causal[:, :, None, :, None, :]).reshape(
        nl, GROUPS, TM_EW, TM_EW)
    w_tile = _seg_axis(_seg_axis(w_tile, 2, _to_segments), 3, _to_segments).astype(BF16)
    bs_tile = jnp.broadcast_to(b_s[:, :, None, :, None], (nl, GROUPS, n_sub, CHUNK, CHUNK)).reshape(
        nl, GROUPS, TM_EW, CHUNK)
    return w_tile, _seg_axis(bs_tile, 2, _to_segments)


def _sgu_param_grads(dwt, dbt):
    n_sub = TM_EW // CHUNK
    tril = jnp.tril(jnp.ones((CHUNK, CHUNK), dtype=bool))
    dwt = _seg_axis(_seg_axis(dwt, 1, _from_segments), 2, _from_segments).reshape(GROUPS, n_sub, CHUNK, n_sub, CHUNK)
    dw = sum(dwt[:, a, :, a, :] for a in range(n_sub))
    db = _seg_axis(dbt, 1, _from_segments).reshape(GROUPS, n_sub, CHUNK).sum(axis=1)
    return jnp.where(tril, dw, 0.0), db


def _sgu(name, h, g3, b3, wt, bst, l):
    t, c2 = h.shape
    c = c2 // 2
    tm = TM_EW

    def body(h_ref, g_ref, b_ref, wt_ref, bs_ref, o_ref):
        z, _ = _gelu_parts(h_ref[...])
        u = z[:, :c]
        xhat, _ = _ln_stats(z[:, c:])
        vnb = (xhat * g_ref[...] + b_ref[...]).astype(BF16)
        for gi in range(GROUPS):
            cs = slice(gi * CHUNK, (gi + 1) * CHUNK)
            sp = jnp.dot(wt_ref[gi], vnb[:, cs], preferred_element_type=F32) + bs_ref[gi]
            o_ref[:, cs] = (u[:, cs] * sp).astype(BF16)

    return _call(body, name, (t // tm,),
                 [_row_spec(tm, c2), _param_spec(l, c), _param_spec(l, c),
                  pl.BlockSpec((None, GROUPS, tm, tm), lambda i: (l, 0, 0, 0)),
                  pl.BlockSpec((None, GROUPS, tm, CHUNK), lambda i: (l, 0, 0, 0))],
                 _row_spec(tm, c), _sds((t, c), BF16))(h, g3, b3, wt, bst)


def _sgu_bwd(name, dq, h, g3, b3, wt, bst, l, deps=()):
    t, c2 = h.shape
    c = c2 // 2
    tm = TM_EW
    n_tiles = t // tm

    def body(dq_ref, h_ref, g_ref, b_ref, wt_ref, bs_ref,
             dh_ref, dbin_ref, dw_ref, dbs_ref, dg_ref, db_ref, du_ref, dvn_ref, bsum_ref):
        i = pl.program_id(0)
        first = i == 0
        hv = h_ref[...]
        z, cdf = _gelu_parts(hv)
        u = z[:, :c]
        xhat, rstd = _ln_stats(z[:, c:])
        g = g_ref[...]
        vnb = (xhat * g + b_ref[...]).astype(BF16)

        @pl.when(first)
        def _():
            dw_ref[...] = jnp.zeros_like(dw_ref)
            bsum_ref[...] = jnp.zeros_like(bsum_ref)

        for gi in range(GROUPS):
            cs = slice(gi * CHUNK, (gi + 1) * CHUNK)
            vb = vnb[:, cs]
            w = wt_ref[gi]
            sp = jnp.dot(w, vb, preferred_element_type=F32) + bs_ref[gi]
            dqb = dq_ref[:, cs]
            du_ref[:, cs] = dqb * sp
            dsp = dqb * u[:, cs]
            bsum_ref[gi] += dsp
            dspb = dsp.astype(BF16)
            dw_ref[gi] += lax.dot_general(dspb, vb, (_DIMS["nt"], ((), ())), preferred_element_type=F32)
            dvn_ref[:, cs] = lax.dot_general(w, dspb, (_DIMS["tn"], ((), ())), preferred_element_type=F32)

        dvn = dvn_ref[...]
        dv = _ln_backward(dvn * g, xhat, rstd)
        pdf = jnp.exp(-0.5 * hv * hv) * INV_SQRT2PI
        dgelu = cdf + hv * pdf
        dhu = du_ref[...] * dgelu[:, :c]
        dhv = dv * dgelu[:, c:]
        dh_ref[:, :c] = dhu.astype(BF16)
        dh_ref[:, c:] = dhv.astype(BF16)
        _acc_rows(dbin_ref.at[:, :c], _colsum(dhu), first)
        _acc_rows(dbin_ref.at[:, c:], _colsum(dhv), first)
        _acc_rows(dg_ref, _colsum(dvn * xhat), first)
        _acc_rows(db_ref, _colsum(dvn), first)

        @pl.when(i == n_tiles - 1)
        def _():
            dbs_ref[...] = jnp.sum(bsum_ref[...], axis=-1)

    vec = pl.BlockSpec((1, c), lambda i: (0, 0))
    return _call(body, name, (n_tiles,),
                 [_row_spec(tm, c), _row_spec(tm, c2), _param_spec(l, c), _param_spec(l, c),
                  pl.BlockSpec((None, GROUPS, tm, tm), lambda i: (l, 0, 0, 0)),
                  pl.BlockSpec((None, GROUPS, tm, CHUNK), lambda i: (l, 0, 0, 0))],
                 [_row_spec(tm, c2), pl.BlockSpec((1, c2), lambda i: (0, 0)),
                  pl.BlockSpec((GROUPS, tm, tm), lambda i: (0, 0, 0)),
                  pl.BlockSpec((GROUPS, tm), lambda i: (0, 0)), vec, vec],
                 [_sds((t, c2), BF16), _sds((1, c2), F32), _sds((GROUPS, tm, tm), F32),
                  _sds((GROUPS, tm), F32), _sds((1, c), F32), _sds((1, c), F32)],
                 [pltpu.VMEM((tm, c), F32), pltpu.VMEM((tm, c), F32), pltpu.VMEM((GROUPS, tm, CHUNK), F32)],
                 deps=deps)(dq, h, g3, b3, wt, bst)


def _loss(name, y, target):
    t, d = y.shape
    tm = min(TM_ROW, t)
    n_tiles = t // tm

    def body(y_ref, t_ref, l_ref, dy_ref, acc_ref):
        i = pl.program_id(0)
        diff = y_ref[...] - t_ref[...]
        dy_ref[...] = diff * (1.0 / d)
        _acc_rows(acc_ref, _colsum(diff * diff), i == 0)

        @pl.when(i == n_tiles - 1)
        def _():
            l_ref[...] = jnp.broadcast_to(jnp.sum(acc_ref[...], axis=-1, keepdims=True) * (0.5 / d), (1, LANES))

    return _call(body, name, (n_tiles,), [_row_spec(tm, d), _row_spec(tm, d)],
                 [pl.BlockSpec((1, LANES), lambda i: (0, 0)), _row_spec(tm, d)],
                 [_sds((1, LANES), F32), _sds((t, d), F32)], [pltpu.VMEM((1, d), F32)])(y, target)


def _adamw(g, w, m, v):
    m2 = ADAM_B1 * m + (1.0 - ADAM_B1) * g
    v2 = ADAM_B2 * v + (1.0 - ADAM_B2) * (g * g)
    m_hat = m2 / (1.0 - ADAM_B1 ** ADAM_STEP)
    v_hat = v2 / (1.0 - ADAM_B2 ** ADAM_STEP)
    delta = -ADAM_LR * (m_hat / (jnp.sqrt(v_hat) + ADAM_EPS) + ADAM_WD * w)
    return delta, m2, v2


ROW_TILE_CAP = 512


def _row_tile(rows, cap=ROW_TILE_CAP):
    if rows <= cap:
        return rows
    for tr in range(cap, 15, -16):
        if rows % tr == 0:
            return tr
    return rows


def _sum8_adamw(name, dev, lands, parts, w, m, v):
    nl = len(lands)
    _, r, c = lands[0].shape
    tr = _row_tile(r, cap=128)

    def body(dev_ref, *refs):
        land, own = refs[:nl], refs[nl:2 * nl]
        w_ref, m_ref, v_ref, g_ref, d_ref, m2_ref, v2_ref = refs[2 * nl:]
        layer, me = pl.program_id(0), dev_ref[0]
        for l in range(nl):
            @pl.when(layer == l)
            def _(l=l):
                g = None
                for s in range(N_DEV):
                    part = jnp.where(me == s, own[l][...], land[l][s]).astype(F32)
                    g = part if g is None else g + part
                delta, m2, v2 = _adamw(g, w_ref[...], m_ref[...], v_ref[...])
                g_ref[...] = g
                d_ref[...] = delta
                m2_ref[...] = m2
                v2_ref[...] = v2

    def rows_of(l, a, i):
        return jnp.where(a == l, i, 0)

    spec = pl.BlockSpec((None, tr, c), lambda a, i, dev_ref: (a, i, 0))
    in_specs = [pl.BlockSpec((N_DEV, tr, c), lambda a, i, dev_ref, l=l: (0, rows_of(l, a, i), 0)) for l in range(nl)]
    in_specs += [pl.BlockSpec((None, tr, c), lambda a, i, dev_ref, l=l: (dev_ref[0], rows_of(l, a, i), 0))
                 for l in range(nl)]
    grid_spec = pltpu.PrefetchScalarGridSpec(
        num_scalar_prefetch=1, grid=(nl, r // tr), in_specs=in_specs + [spec] * 3, out_specs=[spec] * 4)
    return pl.pallas_call(
        body, name=name, grid_spec=grid_spec, out_shape=[_sds(w.shape, F32)] * 4,
        compiler_params=pltpu.CompilerParams(vmem_limit_bytes=VMEM_LIMIT))(dev, *lands, *parts, w, m, v)


def _sum8(name, parts):
    _, r, c = parts.shape
    tr = _row_tile(r)

    def body(p_ref, o_ref):
        acc = p_ref[0]
        for s in range(1, N_DEV):
            acc = acc + p_ref[s]
        o_ref[...] = acc

    return _call(body, name, (r // tr,), [pl.BlockSpec((N_DEV, tr, c), lambda i: (0, i, 0))],
                 pl.BlockSpec((tr, c), lambda i: (i, 0)), _sds((r, c), F32))(parts)


def _adamw_flat(name, g, w, m, v):
    r, c = g.shape
    tr = _row_tile(r)

    def body(g_ref, w_ref, m_ref, v_ref, d_ref, m2_ref, v2_ref):
        delta, m2, v2 = _adamw(g_ref[...], w_ref[...], m_ref[...], v_ref[...])
        d_ref[...] = delta
        m2_ref[...] = m2
        v2_ref[...] = v2

    spec = pl.BlockSpec((tr, c), lambda i: (i, 0))
    return _call(body, name, (r // tr,), [spec] * 4, [spec] * 3, [_sds((r, c), F32)] * 3)(g, w, m, v)


def _pack(arrs, row_multiple=SUBLANES):
    pieces, rows = [], 0
    for a in arrs:
        piece = a.reshape(-1, LANES)
        piece = jnp.pad(piece, ((0, (-piece.shape[0]) % SUBLANES), (0, 0)))
        pieces.append(piece)
        rows += piece.shape[0]
    if rows % row_multiple:
        pieces.append(jnp.zeros(((-rows) % row_multiple, LANES), pieces[0].dtype))
    return jnp.concatenate(pieces, axis=0)


def _unpack(buf, shapes, lead=0):
    out, pos = [], 0
    for shp in shapes:
        rows = math.prod(shp) // LANES
        piece = lax.slice_in_dim(buf, pos, pos + rows, axis=lead)
        out.append(piece.reshape(buf.shape[:lead] + tuple(shp)))
        pos += rows + (-rows) % SUBLANES
    return out


REPLICATED = ["conv_b_in", "conv_b_dw", "conv_ln_g", "conv_ln_b", "conv_b_out", "gmlp_w_s", "gmlp_b_s",
              "ffn_b_up", "ffn_b_dw", "ffn_b_down", "norm1_g", "norm1_b", "norm2_g", "norm2_b"]
SMALL_SHARDED = ["conv_w_dw", "gmlp_b_in", "gmlp_ln_g", "gmlp_ln_b", "gmlp_b_out", "ffn_w_dw"]
BIG = ["conv_w_in", "conv_w_out", "gmlp_w_in", "gmlp_w_out", "ffn_w_up", "ffn_w_down"]
WEIGHTS = ["conv_w_in", "conv_b_in", "conv_w_dw", "conv_b_dw", "conv_ln_g", "conv_ln_b", "conv_w_out", "conv_b_out",
           "gmlp_w_in", "gmlp_b_in", "gmlp_ln_g", "gmlp_ln_b", "gmlp_w_s", "gmlp_b_s", "gmlp_w_out", "gmlp_b_out",
           "ffn_w_up", "ffn_b_up", "ffn_w_dw", "ffn_b_dw", "ffn_w_down", "ffn_b_down",
           "norm1_g", "norm1_b", "norm2_g", "norm2_b"]


def _from_shards(g, lead_shape):
    nd = len(lead_shape)
    perm = tuple(range(1, nd + 1)) + (0, nd + 1)
    return g.transpose(perm).reshape(tuple(lead_shape) + (-1,))


def _to_shards(full, width):
    lead = full.shape[:-1]
    nd = len(lead)
    parts = full.reshape(lead + (N_DEV, width))
    return parts.transpose((nd,) + tuple(range(nd)) + (nd + 1,))


def _step(p):
    x_in, target_in = p["x"], p["loss_target"]
    bsz, seq, d = x_in.shape
    t = bsz * seq
    assert seq % TM_EW == 0 and TM_EW % CHUNK == 0 and TM_EW // SUBLANES >= CONV_K - 1
    x0 = _to_segments(x_in.reshape(t, d), TM_EW)
    target = _to_segments(target_in.reshape(t, d), TM_EW)
    n_conv, n_gmlp = p["conv_w_in"].shape[0], p["gmlp_w_in"].shape[0]
    fb = p["ffn_w_up"].shape[-1]
    nblk = N_DEV
    half = nblk // 2
    cw = p["conv_w_in"].shape[-1]
    tm = min(TM_MM, t)
    tm_ln = min(TM_LN, t)
    nt = t // tm
    dev = 4 * lax.axis_index("x") + 2 * lax.axis_index("y") + lax.axis_index("c")

    small_shapes = [p[n].shape for n in SMALL_SHARDED]
    small_src = _pack([p[n] for n in SMALL_SHARDED])[None]
    small_all = _all_gather("gather_small_weights", [small_src])[0][0]
    sm = _unpack(small_all, small_shapes, lead=1)
    w_src = []
    for i in range(DEPTH):
        mix = "conv" if i % 2 == 0 else "gmlp"
        w_src += [p[mix + "_w_in"][i // 2].astype(BF16), p[mix + "_w_out"][i // 2].astype(BF16),
                  p["ffn_w_up"][i].T.astype(BF16), p["ffn_w_down"][i].astype(BF16)]
    send_sems, recv_sems, w_land, _ = _gather_start(
        "weights_gather_start", _place_own("weights_place_own", w_src, deps=[small_all]))
    W_IN, W_OUT, W_UP, W_DOWN = range(4)

    def wait_weight(i, k, after):
        return _gather_wait(f"l{i}_weights_wait{k}", w_land[4 * i + k], send_sems, recv_sems, 4 * i + k, after)
    conv_w_dw = _from_shards(sm[0], sm[0].shape[1:-1])
    gmlp_b_in = _from_shards(sm[1], sm[1].shape[1:-1])
    gmlp_ln_g = _from_shards(sm[2], sm[2].shape[1:-1])
    gmlp_ln_b = _from_shards(sm[3], sm[3].shape[1:-1])
    gmlp_b_out = _from_shards(sm[4], sm[4].shape[1:-1])
    ffn_w_dw = sm[5].transpose(1, 0, 2, 3)

    def rows3(a):
        return a.reshape(a.shape[0], 1, a.shape[-1])

    conv_b_in4 = p["conv_b_in"].reshape(n_conv, N_DEV, 1, cw)
    gmlp_b_in4 = gmlp_b_in.reshape(n_gmlp, N_DEV, 1, cw)
    ffn_b_up4 = p["ffn_b_up"].reshape(DEPTH, nblk, 1, fb)
    ffn_b_dw4 = p["ffn_b_dw"].reshape(DEPTH, nblk, 1, fb)
    conv_b_dw3, conv_ln_g3, conv_ln_b3 = rows3(p["conv_b_dw"]), rows3(p["conv_ln_g"]), rows3(p["conv_ln_b"])
    conv_b_out3, gmlp_b_out3, ffn_b_down3 = rows3(p["conv_b_out"]), rows3(gmlp_b_out), rows3(p["ffn_b_down"])
    gmlp_ln_g3, gmlp_ln_b3 = rows3(gmlp_ln_g), rows3(gmlp_ln_b)
    n1g3, n1b3, n2g3, n2b3 = rows3(p["norm1_g"]), rows3(p["norm1_b"]), rows3(p["norm2_g"]), rows3(p["norm2_b"])
    w_tile, bs_tile = _sgu_operands(p["gmlp_w_s"], p["gmlp_b_s"])

    def mm_in(name, xa, wg, l, bias4):
        return _matmul(name, xa, wg, "nn", grid=(nt, N_DEV),
                       a_spec=pl.BlockSpec((tm, d), lambda i, n: (i, 0)),
                       b_spec=pl.BlockSpec((None, d, cw), lambda i, n: (n, 0, 0)),
                       o_spec=pl.BlockSpec((tm, cw), lambda i, n: (i, n)), o_shape=(t, N_DEV * cw), o_dtype=F32,
                       bias=bias4, bias_spec=pl.BlockSpec((None, None, 1, cw), lambda i, n: (l, n, 0, 0)))

    def mm_out_dx(name, dy, w, deps=()):
        return _matmul(name, dy, w, "nt", grid=(nt,),
                       a_spec=pl.BlockSpec((tm, d), lambda i: (i, 0)),
                       b_spec=pl.BlockSpec((d, d), lambda i: (0, 0)),
                       o_spec=pl.BlockSpec((tm, d), lambda i: (i, 0)), o_shape=(t, d), o_dtype=F32, deps=deps)

    def mm_out_dw(name, sa, dy):
        return _matmul(name, sa, dy, "tn", grid=(nt,), k_axis=0, nk=nt, acc_shape=(d, d),
                       a_spec=pl.BlockSpec((tm, d), lambda k: (k, 0)),
                       b_spec=pl.BlockSpec((tm, d), lambda k: (k, 0)),
                       o_spec=pl.BlockSpec((d, d), lambda k: (0, 0)), o_shape=(d, d), o_dtype=BF16)

    def mm_in_dx(name, dh, wg, res):
        return _matmul(name, dh, wg, "nt", grid=(nt, N_DEV), k_axis=1, nk=N_DEV, acc_shape=(tm, d),
                       a_spec=pl.BlockSpec((tm, cw), lambda i, n: (i, n)),
                       b_spec=pl.BlockSpec((None, d, cw), lambda i, n: (n, 0, 0)),
                       o_spec=pl.BlockSpec((tm, d), lambda i, n: (i, 0)), o_shape=(t, d), o_dtype=F32,
                       res=res, res_spec=pl.BlockSpec((tm, d), lambda i, n: (i, 0)), res_scale=ALPHA)

    def mm_in_dw(name, xa, dh):
        return _matmul(name, xa, dh, "tn", grid=(N_DEV, nt), k_axis=1, nk=nt, acc_shape=(d, cw),
                       a_spec=pl.BlockSpec((tm, d), lambda n, k: (k, 0)),
                       b_spec=pl.BlockSpec((tm, cw), lambda n, k: (k, n)),
                       o_spec=pl.BlockSpec((None, d, cw), lambda n, k: (n, 0, 0)),
                       o_shape=(N_DEV, d, cw), o_dtype=BF16)

    def mm_down_dw(name, a, dy, deps=()):
        return _matmul(name, a, dy, "tn", grid=(half, nt), k_axis=1, nk=nt, acc_shape=(fb, d),
                       a_spec=pl.BlockSpec((None, tm, fb), lambda n, k: (n, k, 0)),
                       b_spec=pl.BlockSpec((tm, d), lambda n, k: (k, 0)),
                       o_spec=pl.BlockSpec((None, fb, d), lambda n, k: (n, 0, 0)),
                       o_shape=(half, fb, d), o_dtype=BF16, deps=deps)

    def mm_up_dw(name, xa, dh):
        return _matmul(name, dh, xa, "tn", grid=(nblk, nt), k_axis=1, nk=nt, acc_shape=(fb, d),
                       a_spec=pl.BlockSpec((None, tm, fb), lambda n, k: (n, k, 0)),
                       b_spec=pl.BlockSpec((tm, d), lambda n, k: (k, 0)),
                       o_spec=pl.BlockSpec((None, fb, d), lambda n, k: (n, 0, 0)),
                       o_shape=(nblk, fb, d), o_dtype=BF16)

    saved = []
    xcur = x0
    for i in range(DEPTH):
        j = i // 2
        s = {"x": xcur}
        s["w_in"] = wait_weight(i, W_IN, xcur)
        if i % 2 == 0:
            s["h"] = mm_in(f"l{i}_conv_in", xcur, s["w_in"], j, conv_b_in4)
            s["u"] = _glu(f"l{i}_glu", s["h"])
            s["c"] = _dwconv31(f"l{i}_dwconv", s["u"], conv_w_dw, conv_b_dw3, j, seq)
            s["s"] = _ln_silu(f"l{i}_ln_silu", s["c"], conv_ln_g3, conv_ln_b3, j)
            b_out3 = conv_b_out3
        else:
            s["h"] = mm_in(f"l{i}_gmlp_in", xcur, s["w_in"], j, gmlp_b_in4)
            s["s"] = _sgu(f"l{i}_sgu", s["h"], gmlp_ln_g3, gmlp_ln_b3, w_tile, bs_tile, j)
            b_out3 = gmlp_b_out3
        s["w_out"] = wait_weight(i, W_OUT, s["s"]).reshape(d, d)
        s["x1"], s["xhat1"], s["rstd1"] = _matmul_ln(
            f"l{i}_mixer_out_norm1", s["s"], s["w_out"], xcur, b_out3, n1g3, n1b3, j, i,
            nk=1, a_block=(tm_ln, d), b_block=(d, d))
        s["w_up"] = wait_weight(i, W_UP, s["x1"])
        s["hg"], s["hv"], s["a"] = _ffn_up_act(f"l{i}_ffn_up_act", s["x1"], s["w_up"], ffn_b_up4, ffn_w_dw, ffn_b_dw4,
                                               i, seq)
        s["w_down"] = wait_weight(i, W_DOWN, s["a"]).reshape(half, fb, d)
        xcur, s["xhat2"], s["rstd2"] = _matmul_ln(
            f"l{i}_ffn_down_norm2", s["a"], s["w_down"], s["x1"], ffn_b_down3, n2g3, n2b3, i, i,
            nk=half, a_block=(None, tm_ln, fb), b_block=(None, fb, d))
        saved.append(s)

    loss_row, dx = _loss("loss", xcur, target)

    started = {n: [None] * p[n].shape[0] for n in BIG}
    tokens = []

    def send_grads(name, items):
        done, token = _scatter_start(name, [g for _, _, g in items])
        for (n, l, _), st in zip(items, done):
            started[n][l] = st
        tokens.append(token)

    def take_tokens():
        out = list(tokens)
        tokens.clear()
        return out

    gl = {n: [None] * p[n].shape[0] for n in REPLICATED + SMALL_SHARDED}
    dr2, gl["norm2_g"][DEPTH - 1], gl["norm2_b"][DEPTH - 1], gl["ffn_b_down"][DEPTH - 1] = _ln_res_bwd(
        f"l{DEPTH - 1}_norm2_bwd", dx, saved[-1]["xhat2"], saved[-1]["rstd2"], n2g3, DEPTH - 1)
    for i in reversed(range(DEPTH)):
        j = i // 2
        s = saved[i]
        mix = "conv" if i % 2 == 0 else "gmlp"
        g_down = mm_down_dw(f"l{i}_ffn_down_dw", s["a"], dr2, deps=take_tokens()).reshape(N_DEV, -1, d)
        send_grads(f"l{i}_ffn_down_grad_scatter_start", [("ffn_w_down", i, g_down)])
        dcg, dcv, dbg, dbv, dwg, dwv = _ffn_act_bwd(f"l{i}_ffn_act_bwd", dr2, s["w_down"], s["hg"], s["hv"],
                                                    ffn_w_dw, ffn_b_dw4, i, seq, deps=take_tokens())
        gl["ffn_b_dw"][i] = jnp.concatenate([dbg, dbv], axis=0).reshape(1, nblk * fb)
        gl["ffn_w_dw"][i] = jnp.concatenate([dwg[:, :FFN_K], dwv[:, :FFN_K]], axis=0)
        dh, dbu, dr1, gl["norm1_g"][i], gl["norm1_b"][i], gl[mix + "_b_out"][j] = _ffn_conv_t_dx(
            f"l{i}_ffn_conv_t_dx", dcg, dcv, ffn_w_dw, s["w_up"], dr2, s["xhat1"], s["rstd1"], n1g3, i, seq,
            deps=take_tokens())
        gl["ffn_b_up"][i] = dbu.reshape(1, nblk * fb)
        send_grads(f"l{i}_ffn_up_grad_scatter_start", [("ffn_w_up", i, mm_up_dw(f"l{i}_ffn_up_dw", s["x1"], dh))])
        ds = mm_out_dx(f"l{i}_{mix}_out_dx", dr1, s["w_out"], deps=take_tokens())
        g_out = mm_out_dw(f"l{i}_{mix}_out_dw", s["s"], dr1).reshape(N_DEV, -1, d)
        if i % 2 == 0:
            dc, gl["conv_ln_g"][j], gl["conv_ln_b"][j], gl["conv_b_dw"][j] = _ln_silu_bwd(
                f"l{i}_ln_silu_bwd", ds, s["c"], conv_ln_g3, conv_ln_b3, j)
            du, dwdw = _dwconv31_bwd(f"l{i}_dwconv_bwd", dc, s["u"], conv_w_dw, j, seq)
            gl["conv_w_dw"][j] = dwdw[:CONV_K]
            dh, gl["conv_b_in"][j] = _glu_bwd(f"l{i}_glu_bwd", du, s["h"])
        else:
            dh, gl["gmlp_b_in"][j], dwt, dbt, gl["gmlp_ln_g"][j], gl["gmlp_ln_b"][j] = _sgu_bwd(
                f"l{i}_sgu_bwd", ds, s["h"], gmlp_ln_g3, gmlp_ln_b3, w_tile, bs_tile, j)
            gl["gmlp_w_s"][j], gl["gmlp_b_s"][j] = _sgu_param_grads(dwt, dbt)
        dx = mm_in_dx(f"l{i}_{mix}_in_dx", dh, s["w_in"], dr1)
        if i > 0:
            prev = saved[i - 1]
            dr2, gl["norm2_g"][i - 1], gl["norm2_b"][i - 1], gl["ffn_b_down"][i - 1] = _ln_res_bwd(
                f"l{i - 1}_norm2_bwd", dx, prev["xhat2"], prev["rstd2"], n2g3, i - 1)
        send_grads(f"l{i}_mixer_grads_scatter_start",
                   [(mix + "_w_out", j, g_out), (mix + "_w_in", j, mm_in_dw(f"l{i}_{mix}_in_dw", s["x"], dh))])
    grad_x = _from_segments(dx, TM_EW).reshape(bsz, seq, d)

    full_small = {n: jnp.stack(gl[n]).reshape(p[n].shape) for n in REPLICATED}
    shard_small = {}
    for n in SMALL_SHARDED:
        if n == "ffn_w_dw":
            shard_small[n] = jnp.stack(gl[n]).transpose(1, 0, 2, 3)
        else:
            width = p[n].shape[-1]
            lead = p[n].shape[:-1]
            shard_small[n] = _to_shards(jnp.stack(gl[n]).reshape(lead + (N_DEV * width,)), width)
    flat_shapes = [(1, LANES)] + [p[n].shape for n in REPLICATED] + [(N_DEV,) + p[n].shape for n in SMALL_SHARDED]
    flat_local = _pack([loss_row] + [full_small[n] for n in REPLICATED] + [shard_small[n] for n in SMALL_SHARDED],
                       row_multiple=ROW_TILE_CAP)

    small_send, small_recv, small_land, small_token = _gather_start(
        "small_grads_gather_start", _place_own("small_grads_place_own", [flat_local]))

    grads, delta, new_m, new_v = {}, {}, {}, {}
    dev1 = jnp.reshape(dev, (1,)).astype(jnp.int32)
    order = ["ffn_w_down", "ffn_w_up", "gmlp_w_out", "gmlp_w_in", "conv_w_out", "conv_w_in"]
    after = small_token
    for n in order:
        parts_done, lands_done = _scatter_wait(f"grads_{n}_scatter_wait", started[n], after)
        state = [p[n], p["m_" + n], p["v_" + n]]
        if n == "ffn_w_up":
            state = [a.transpose(0, 2, 1) for a in state]
        outs = _sum8_adamw(f"adamw_{n}", dev1, lands_done, parts_done, *state)
        after = outs[-1]
        if n == "ffn_w_up":
            outs = [a.transpose(0, 2, 1) for a in outs]
        grads[n], delta[n], new_m[n], new_v[n] = outs

    small_parts = _gather_wait("small_grads_gather_wait", small_land[0], small_send, small_recv, 0, after)
    summed = _unpack(_sum8("sum_small_grads", small_parts), flat_shapes)
    loss = summed[0][0, 0]
    grads.update(zip(REPLICATED, summed[1:1 + len(REPLICATED)]))
    for n, g in zip(SMALL_SHARDED, summed[1 + len(REPLICATED):]):
        grads[n] = lax.dynamic_index_in_dim(g, dev, axis=0, keepdims=False)
    small = REPLICATED + SMALL_SHARDED
    small_shp = [p[n].shape for n in small]
    d_s, m_s, v_s = _adamw_flat("adamw_small", _pack([grads[n] for n in small]), _pack([p[n] for n in small]),
                                _pack([p["m_" + n] for n in small]), _pack([p["v_" + n] for n in small]))
    for n, dd, mm, vv in zip(small, _unpack(d_s, small_shp), _unpack(m_s, small_shp), _unpack(v_s, small_shp)):
        delta[n], new_m[n], new_v[n] = dd, mm, vv

    return (loss, grad_x, *[grads[n] for n in WEIGHTS], *[delta[n] for n in WEIGHTS],
            *[new_m[n] for n in WEIGHTS], *[new_v[n] for n in WEIGHTS])


def kernel(x, conv_w_in, conv_b_in, conv_w_dw, conv_b_dw, conv_ln_g, conv_ln_b, conv_w_out, conv_b_out, gmlp_w_in, gmlp_b_in, gmlp_ln_g, gmlp_ln_b, gmlp_w_s, gmlp_b_s, gmlp_w_out, gmlp_b_out, ffn_w_up, ffn_b_up, ffn_w_dw, ffn_b_dw, ffn_w_down, ffn_b_down, norm1_g, norm1_b, norm2_g, norm2_b, loss_target, m_conv_w_in, m_conv_b_in, m_conv_w_dw, m_conv_b_dw, m_conv_ln_g, m_conv_ln_b, m_conv_w_out, m_conv_b_out, m_gmlp_w_in, m_gmlp_b_in, m_gmlp_ln_g, m_gmlp_ln_b, m_gmlp_w_s, m_gmlp_b_s, m_gmlp_w_out, m_gmlp_b_out, m_ffn_w_up, m_ffn_b_up, m_ffn_w_dw, m_ffn_b_dw, m_ffn_w_down, m_ffn_b_down, m_norm1_g, m_norm1_b, m_norm2_g, m_norm2_b, v_conv_w_in, v_conv_b_in, v_conv_w_dw, v_conv_b_dw, v_conv_ln_g, v_conv_ln_b, v_conv_w_out, v_conv_b_out, v_gmlp_w_in, v_gmlp_b_in, v_gmlp_ln_g, v_gmlp_ln_b, v_gmlp_w_s, v_gmlp_b_s, v_gmlp_w_out, v_gmlp_b_out, v_ffn_w_up, v_ffn_b_up, v_ffn_w_dw, v_ffn_b_dw, v_ffn_w_down, v_ffn_b_down, v_norm1_g, v_norm1_b, v_norm2_g, v_norm2_b):
    return _step(dict(locals()))
```

```python
import math

import jax
import jax.numpy as jnp
from jax import lax
from jax.experimental import pallas as pl
from jax.experimental.pallas import tpu as pltpu

F32 = jnp.float32
BF16 = jnp.bfloat16
MESH = pl.DeviceIdType.MESH

N_DEV = 8
DEPTH = 4
ALPHA = (2.0 * DEPTH) ** 0.25
LN_EPS = 1e-5
CONV_K = 31
FFN_K = 3
CHUNK = 128
GROUPS = 8
ADAM_LR = 0.001
ADAM_B1 = 0.9
ADAM_B2 = 0.999
ADAM_EPS = 1e-08
ADAM_WD = 0.01
ADAM_STEP = 10
INV_SQRT2 = 1.0 / math.sqrt(2.0)
INV_SQRT2PI = 1.0 / math.sqrt(2.0 * math.pi)

LANES = 128
SUBLANES = 8
VMEM_LIMIT = 48 * 1024 * 1024
TM_MM = 1024
TM_EW = 256
TM_ROW = 512


def _call(body, name, grid, in_specs, out_specs, out_shape, scratch=(), aliases=None, deps=()):
    deps = list(deps)
    in_specs = list(in_specs)
    n_in = len(in_specs)
    if deps:
        inner = body

        def body(*refs):
            return inner(*refs[:n_in], *refs[n_in + len(deps):])

        in_specs = in_specs + [pl.BlockSpec(memory_space=pl.ANY)] * len(deps)
    fn = pl.pallas_call(
        body, name=name, grid=grid, in_specs=in_specs, out_specs=out_specs, out_shape=out_shape,
        scratch_shapes=list(scratch), input_output_aliases=aliases or {},
        compiler_params=pltpu.CompilerParams(vmem_limit_bytes=VMEM_LIMIT))
    return lambda *args: fn(*args, *deps)


def _sds(shape, dtype):
    return jax.ShapeDtypeStruct(tuple(shape), dtype)


def _sigmoid(x):
    return 1.0 / (1.0 + jnp.exp(-x))


def _acc_rows(ref, val, first):
    @pl.when(first)
    def _():
        ref[...] = val

    @pl.when(jnp.logical_not(first))
    def _():
        ref[...] += val


def _colsum(v):
    return jnp.sum(v, axis=0, keepdims=True)


_DIMS = {"nn": ((1,), (0,)), "nt": ((1,), (1,)), "tn": ((0,), (0,))}


def _matmul(name, a, b, mode, *, grid, a_spec, b_spec, o_spec, o_shape, o_dtype, k_axis=None, nk=1,
            acc_shape=None, bias=None, bias_spec=None, res=None, res_spec=None, res_scale=1.0, deps=()):
    dims = (_DIMS[mode], ((), ()))
    has_bias, has_res = bias is not None, res is not None

    def body(*refs):
        a_ref, b_ref = refs[0], refs[1]
        pos = 2
        bias_ref = res_ref = None
        if has_bias:
            bias_ref = refs[pos]
            pos += 1
        if has_res:
            res_ref = refs[pos]
            pos += 1
        o_ref = refs[pos]
        acc_ref = refs[pos + 1] if nk > 1 else None
        p = lax.dot_general(a_ref[...].astype(BF16), b_ref[...].astype(BF16), dims, preferred_element_type=F32)

        def finish(acc):
            if has_bias:
                acc = acc + bias_ref[...]
            if has_res:
                acc = acc + res_scale * res_ref[...]
            o_ref[...] = acc.astype(o_dtype)

        if nk == 1:
            finish(p)
        else:
            k = pl.program_id(k_axis)

            @pl.when(k == 0)
            def _():
                acc_ref[...] = p

            @pl.when(k > 0)
            def _():
                acc_ref[...] += p

            @pl.when(k == nk - 1)
            def _():
                finish(acc_ref[...])

    ins, specs = [a, b], [a_spec, b_spec]
    if has_bias:
        ins.append(bias)
        specs.append(bias_spec)
    if has_res:
        ins.append(res)
        specs.append(res_spec)
    scratch = [pltpu.VMEM(acc_shape, F32)] if nk > 1 else []
    return _call(body, name, grid, specs, o_spec, _sds(o_shape, o_dtype), scratch, deps=deps)(*ins)


TM_LN = 512


def _matmul_ln(name, a, b, x_res, bias3, g3, b3, l_bias, l_norm):
    t, d = x_res.shape
    tm = min(TM_LN, t)
    blocked = a.ndim == 3

    def body(a_ref, b_ref, x_ref, bias_ref, g_ref, be_ref, o_ref, xh_ref, rs_ref):
        if blocked:
            y = None
            for k in range(a.shape[0]):
                p = jnp.dot(a_ref[k], b_ref[k], preferred_element_type=F32)
                y = p if y is None else y + p
        else:
            y = jnp.dot(a_ref[...], b_ref[...], preferred_element_type=F32)
        xhat, rstd = _ln_stats(ALPHA * x_ref[...] + y + bias_ref[...])
        o_ref[...] = xhat * g_ref[...] + be_ref[...]
        xh_ref[...] = xhat
        rs_ref[...] = rstd

    if blocked:
        a_spec = pl.BlockSpec((a.shape[0], tm, a.shape[2]), lambda i: (0, i, 0))
        b_spec = pl.BlockSpec(b.shape, lambda i: (0, 0, 0))
    else:
        a_spec = pl.BlockSpec((tm, a.shape[1]), lambda i: (i, 0))
        b_spec = pl.BlockSpec(b.shape, lambda i: (0, 0))
    row = pl.BlockSpec((tm, d), lambda i: (i, 0))
    stat = pl.BlockSpec((tm, 1), lambda i: (i, 0))

    def vec(l):
        return pl.BlockSpec((None, 1, d), lambda i: (l, 0, 0))

    return _call(body, name, (t // tm,), [a_spec, b_spec, row, vec(l_bias), vec(l_norm), vec(l_norm)],
                 [row, row, stat], [_sds((t, d), F32), _sds((t, d), F32), _sds((t, 1), F32)])(
                     a, b, x_res, bias3, g3, b3)


def _mesh_pos():
    return lax.axis_index("x"), lax.axis_index("y"), lax.axis_index("c")


def _any_specs(n):
    return [pl.BlockSpec(memory_space=pl.ANY)] * n


def _all_gather(name, srcs):
    n = len(srcs)

    def body(*refs):
        src, out = refs[:n], refs[n:2 * n]
        send_sems, recv_sems, local_sems = refs[2 * n:]
        x, y, c = _mesh_pos()
        me, sibling = (x, y, c), (x, y, 1 - c)
        chips = [(1 - x, y), (x, 1 - y), (1 - x, 1 - y)]

        def slot(k, p):
            return out[k].at[:, 4 * p[0] + 2 * p[1] + p[2]]

        def copy(k, idx, block, to, s=None):
            return pltpu.make_async_remote_copy(
                src_ref=slot(k, block) if s is None else s, dst_ref=slot(k, block),
                send_sem=send_sems.at[k * 7 + idx], recv_sem=recv_sems.at[k * 7 + idx],
                device_id=to, device_id_type=MESH)

        local = [pltpu.make_async_copy(src[k], slot(k, me), local_sems.at[k]) for k in range(n)]
        for cp in local:
            cp.start()
        first = []
        for k in range(n):
            first.append(copy(k, 0, me, sibling, src[k]))
            for j, chip in enumerate(chips):
                first.append(copy(k, 1 + j, me, (*chip, c), src[k]))
        for cp in first:
            cp.start()
        passed = []
        for j, chip in enumerate(chips):
            for k in range(n):
                copy(k, 1 + j, (*chip, c), me).wait_recv()
                cp = copy(k, 4 + j, (*chip, c), sibling)
                cp.start()
                passed.append(cp)
        for k in range(n):
            copy(k, 0, sibling, me).wait_recv()
            for j, chip in enumerate(chips):
                copy(k, 4 + j, (*chip, 1 - c), me).wait_recv()
        for cp in first + passed:
            cp.wait_send()
        for cp in local:
            cp.wait()

    out_shape = [_sds((s.shape[0], N_DEV) + s.shape[1:], s.dtype) for s in srcs]
    return _call(body, name, (), [pl.BlockSpec(memory_space=pltpu.VMEM)] * n, _any_specs(n), out_shape,
                 [pltpu.SemaphoreType.DMA((7 * n,)), pltpu.SemaphoreType.DMA((7 * n,)),
                  pltpu.SemaphoreType.DMA((n,))])(*srcs)


HBM_SPEC = pl.BlockSpec(memory_space=pltpu.HBM)
SEM_SPEC = pl.BlockSpec(memory_space=pltpu.SEMAPHORE)
N_PEER = N_DEV - 1


def _split_call(body, name, in_specs, out_specs, out_shape, aliases):
    return pl.pallas_call(
        body, name=name, in_specs=in_specs, out_specs=out_specs, out_shape=out_shape, input_output_aliases=aliases,
        compiler_params=pltpu.CompilerParams(has_side_effects=pltpu.SideEffectType.DATAFLOW_SIDE_EFFECTING))


def _peers(x, y, c):
    return [(1 - x if q & 4 else x, 1 - y if q & 2 else y, 1 - c if q & 1 else c) for q in range(1, N_DEV)]


def _in_hbm(a):
    return pltpu.with_memory_space_constraint(a, pltpu.HBM)


def _place_own(name, srcs, deps=()):
    n = len(srcs)

    def body(*refs):
        src, out, sems = refs[:n], refs[n:2 * n], refs[2 * n]
        x, y, c = _mesh_pos()
        dev = 4 * x + 2 * y + c
        copies = [pltpu.make_async_copy(src[k], out[k].at[dev], sems.at[k]) for k in range(n)]
        for cp in copies:
            cp.start()
        for cp in copies:
            cp.wait()

    return _call(body, name, (), [pl.BlockSpec(memory_space=pltpu.VMEM)] * n, _any_specs(n),
                 [_sds((N_DEV,) + s.shape, s.dtype) for s in srcs], [pltpu.SemaphoreType.DMA((n,))],
                 deps=deps)(*srcs)


def _gather_start(name, lands):
    n = len(lands)

    def body(*refs):
        land, send_sems, recv_sems = refs[:n], refs[n], refs[n + 1]
        x, y, c = _mesh_pos()
        dev = 4 * x + 2 * y + c
        for k in range(n):
            for peer in _peers(x, y, c):
                pltpu.make_async_remote_copy(
                    src_ref=land[k].at[dev], dst_ref=land[k].at[dev], send_sem=send_sems.at[k],
                    recv_sem=recv_sems.at[k], device_id=peer, device_id_type=MESH).start()
        token = refs[-1]
        token[...] = jnp.zeros_like(token)

    outs = _split_call(
        body, name, [HBM_SPEC] * n, [SEM_SPEC, SEM_SPEC] + [HBM_SPEC] * n + [pl.BlockSpec(memory_space=pltpu.VMEM)],
        [pltpu.SemaphoreType.DMA((n,)), pltpu.SemaphoreType.DMA((n,))] + [pltpu.HBM(a.shape, a.dtype) for a in lands]
        + [_sds((SUBLANES, LANES), F32)],
        {k: 2 + k for k in range(n)})(*[_in_hbm(a) for a in lands])
    return outs[0], outs[1], list(outs[2:2 + n]), outs[-1]


def _wait_seven(src_ref, dst_ref, send_sem, recv_sem):
    cp = pltpu.make_async_remote_copy(
        src_ref=src_ref.at[pl.ds(0, N_PEER)], dst_ref=dst_ref.at[pl.ds(0, N_PEER)], send_sem=send_sem,
        recv_sem=recv_sem, device_id=_mesh_pos(), device_id_type=MESH)
    cp.wait_send()
    cp.wait_recv()


def _gather_wait(name, land, send_sems, recv_sems, k, after):
    def body(land_ref, send_ref, recv_ref, after_ref, out_ref):
        _wait_seven(land_ref, land_ref, send_ref.at[k], recv_ref.at[k])

    return _split_call(body, name, [HBM_SPEC, SEM_SPEC, SEM_SPEC, pl.BlockSpec(memory_space=pl.ANY)], HBM_SPEC,
                       pltpu.HBM(land.shape, land.dtype), {0: 0})(land, send_sems, recv_sems, after)


def _scatter_start(name, parts_list):
    n = len(parts_list)

    def body(*refs):
        x, y, c = _mesh_pos()
        dev = 4 * x + 2 * y + c
        for k in range(n):
            parts_ref, land_ref = refs[2 * k], refs[2 * k + 1]
            send_sem, recv_sem = refs[2 * n + 4 * k], refs[2 * n + 4 * k + 1]
            for peer in _peers(x, y, c):
                pltpu.make_async_remote_copy(
                    src_ref=parts_ref.at[4 * peer[0] + 2 * peer[1] + peer[2]], dst_ref=land_ref.at[dev],
                    send_sem=send_sem, recv_sem=recv_sem, device_id=peer, device_id_type=MESH).start()
        token = refs[-1]
        token[...] = jnp.zeros_like(token)

    ins, out_specs, out_shape, aliases = [], [], [], {}
    for k, parts in enumerate(parts_list):
        buf = pltpu.HBM(parts.shape, parts.dtype)
        ins += [_in_hbm(parts), _in_hbm(lax.empty(parts.shape, parts.dtype))]
        out_specs += [SEM_SPEC, SEM_SPEC, HBM_SPEC, HBM_SPEC]
        out_shape += [pltpu.SemaphoreType.DMA(()), pltpu.SemaphoreType.DMA(()), buf, buf]
        aliases.update({2 * k: 4 * k + 2, 2 * k + 1: 4 * k + 3})
    outs = _split_call(body, name, [HBM_SPEC] * (2 * n), out_specs + [pl.BlockSpec(memory_space=pltpu.VMEM)],
                       out_shape + [_sds((SUBLANES, LANES), F32)], aliases)(*ins)
    return [tuple(outs[4 * k:4 * k + 4]) for k in range(n)], outs[-1]


def _scatter_wait(name, started, after):
    n = len(started)

    def body(*refs):
        for k in range(n):
            send_sem, recv_sem, parts_ref, land_ref = refs[4 * k:4 * k + 4]
            _wait_seven(parts_ref, land_ref, send_sem, recv_sem)

    flat = [a for s in started for a in s]
    outs = _split_call(
        body, name, [SEM_SPEC, SEM_SPEC, HBM_SPEC, HBM_SPEC] * n + [pl.BlockSpec(memory_space=pl.ANY)],
        [HBM_SPEC, HBM_SPEC] * n, [pltpu.HBM(a.shape, a.dtype) for s in started for a in s[2:]],
        {4 * k + 2 + t: 2 * k + t for k in range(n) for t in range(2)})(*flat, after)
    return list(outs[0::2]), list(outs[1::2])


def _to_segments(a, tile):
    seg = tile // SUBLANES
    return a.reshape((a.shape[0] // tile, SUBLANES, seg) + a.shape[1:]).swapaxes(1, 2).reshape(a.shape)


def _from_segments(a, tile):
    seg = tile // SUBLANES
    return a.reshape((a.shape[0] // tile, seg, SUBLANES) + a.shape[1:]).swapaxes(1, 2).reshape(a.shape)


def _chunk(ref, q):
    return ref[q * SUBLANES:(q + 1) * SUBLANES, :]


def _fill_wrap_prev(x_ref, halo_ref, wrap_ref, n_wrap, n_halo, seg, keep):
    sub = lax.broadcasted_iota(jnp.int32, (SUBLANES, x_ref.shape[-1]), 0)
    for j in range(n_wrap):
        q = seg - n_wrap + j
        hq = q - (seg - n_halo)
        row = halo_ref[hq * SUBLANES + SUBLANES - 1:(hq + 1) * SUBLANES, :] * keep
        wrap_ref[j * SUBLANES:(j + 1) * SUBLANES, :] = jnp.where(sub == 0, row, pltpu.roll(_chunk(x_ref, q), 1, 0))


def _fill_wrap_next(x_ref, halo_ref, wrap_ref, n_wrap, keep):
    sub = lax.broadcasted_iota(jnp.int32, (SUBLANES, x_ref.shape[-1]), 0)
    for j in range(n_wrap):
        row = halo_ref[j * SUBLANES:j * SUBLANES + 1, :] * keep
        wrap_ref[j * SUBLANES:(j + 1) * SUBLANES, :] = jnp.where(
            sub == SUBLANES - 1, row, pltpu.roll(_chunk(x_ref, j), SUBLANES - 1, 0))


def _past(x_ref, wrap_ref, q, d, n_wrap):
    return _chunk(x_ref, q - d) if q >= d else _chunk(wrap_ref, q - d + n_wrap)


def _future(x_ref, wrap_ref, q, d, seg):
    return _chunk(x_ref, q + d) if q + d < seg else _chunk(wrap_ref, q + d - seg)


def _conv_fwd(x_ref, wrap_ref, w_ref, b_ref, out_ref, seg, k_taps):
    bias = jnp.broadcast_to(b_ref[...], (SUBLANES, x_ref.shape[-1]))
    for q in range(seg):
        acc = bias
        for k in range(k_taps):
            acc = acc + w_ref[k:k + 1, :] * _past(x_ref, wrap_ref, q, k_taps - 1 - k, k_taps - 1)
        out_ref[q * SUBLANES:(q + 1) * SUBLANES, :] = acc


def _conv_bwd_data(d_ref, wrap_ref, w_ref, out_ref, seg, k_taps):
    for q in range(seg):
        acc = None
        for k in range(k_taps):
            term = w_ref[k:k + 1, :] * _future(d_ref, wrap_ref, q, k_taps - 1 - k, seg)
            acc = term if acc is None else acc + term
        out_ref[q * SUBLANES:(q + 1) * SUBLANES, :] = acc


def _conv_bwd_taps(d_ref, x_ref, wrap_ref, dw_ref, seg, k_taps):
    for k in range(k_taps):
        part = None
        for q in range(seg):
            term = _chunk(d_ref, q) * _past(x_ref, wrap_ref, q, k_taps - 1 - k, k_taps - 1)
            part = term if part is None else part + term
        dw_ref[k:k + 1, :] += _colsum(part)


def _tile_halo_specs(tm, width_block, n_halo, n_tiles, block_of):
    rows = n_halo * SUBLANES
    per = tm // rows
    tile = pl.BlockSpec(width_block(tm), lambda n, i: block_of(n, i))
    prev = pl.BlockSpec(width_block(rows), lambda n, i: block_of(n, jnp.maximum(i * per - 1, 0)))
    nxt = pl.BlockSpec(width_block(rows), lambda n, i: block_of(n, jnp.minimum((i + 1) * per, n_tiles * per - 1)))
    return tile, prev, nxt


def _ln_stats(v):
    mu = jnp.mean(v, axis=-1, keepdims=True)
    vc = v - mu
    var = jnp.mean(vc * vc, axis=-1, keepdims=True)
    rstd = lax.rsqrt(var + LN_EPS)
    return vc * rstd, rstd


def _ln_backward(dxhat, xhat, rstd):
    m1 = jnp.mean(dxhat, axis=-1, keepdims=True)
    m2 = jnp.mean(dxhat * xhat, axis=-1, keepdims=True)
    return rstd * (dxhat - m1 - xhat * m2)


def _row_spec(tm, width):
    return pl.BlockSpec((tm, width), lambda i: (i, 0))


def _param_spec(l, width):
    return pl.BlockSpec((None, 1, width), lambda *_: (l, 0, 0))


def _ln_bwd_rows(dout, xh_ref, rs_ref, g_ref, dr_ref, dg_ref, db_ref, dsum_ref, first):
    xhat = xh_ref[...]
    dr = _ln_backward(dout * g_ref[...], xhat, rs_ref[...])
    dr_ref[...] = dr
    _acc_rows(dg_ref, _colsum(dout * xhat), first)
    _acc_rows(db_ref, _colsum(dout), first)
    _acc_rows(dsum_ref, _colsum(dr), first)


def _ln_bwd_specs(tm, d, l, row_of):
    vec = pl.BlockSpec((1, d), lambda *_: (0, 0))
    ins = [pl.BlockSpec((tm, d), row_of), pl.BlockSpec((tm, 1), row_of), _param_spec(l, d)]
    return ins, [pl.BlockSpec((tm, d), row_of), vec, vec, vec]


def _ln_res_bwd(name, dout, xhat, rstd, g3, l, deps=()):
    t, d = dout.shape
    tm = min(TM_ROW, t)

    def body(do_ref, xh_ref, rs_ref, g_ref, dr_ref, dg_ref, db_ref, dc_ref):
        _ln_bwd_rows(do_ref[...], xh_ref, rs_ref, g_ref, dr_ref, dg_ref, db_ref, dc_ref, pl.program_id(0) == 0)

    ins, outs = _ln_bwd_specs(tm, d, l, lambda i: (i, 0))
    return _call(body, name, (t // tm,), [_row_spec(tm, d)] + ins, outs,
                 [_sds((t, d), F32)] + [_sds((1, d), F32)] * 3, deps=deps)(dout, xhat, rstd, g3)


def _glu(name, h):
    t, c2 = h.shape
    c = c2 // 2
    tm = min(TM_ROW, t)

    def body(a_ref, g_ref, o_ref):
        o_ref[...] = a_ref[...] * _sigmoid(g_ref[...])

    return _call(body, name, (t // tm,),
                 [pl.BlockSpec((tm, c), lambda i: (i, 0)), pl.BlockSpec((tm, c), lambda i: (i, 1))],
                 _row_spec(tm, c), _sds((t, c), F32))(h, h)


def _glu_bwd(name, du, h):
    t, c2 = h.shape
    c = c2 // 2
    tm = min(TM_ROW, t)

    def body(du_ref, a_ref, g_ref, dh_ref, db_ref):
        first = pl.program_id(0) == 0
        du_v, a = du_ref[...], a_ref[...]
        sg = _sigmoid(g_ref[...])
        da = du_v * sg
        dg = du_v * a * sg * (1.0 - sg)
        dh_ref[:, :c] = da.astype(BF16)
        dh_ref[:, c:] = dg.astype(BF16)
        _acc_rows(db_ref.at[:, :c], _colsum(da), first)
        _acc_rows(db_ref.at[:, c:], _colsum(dg), first)

    return _call(body, name, (t // tm,),
                 [_row_spec(tm, c), pl.BlockSpec((tm, c), lambda i: (i, 0)), pl.BlockSpec((tm, c), lambda i: (i, 1))],
                 [_row_spec(tm, c2), pl.BlockSpec((1, c2), lambda i: (0, 0))],
                 [_sds((t, c2), BF16), _sds((1, c2), F32)])(du, h, h)


CONV_CB = 512
TAPS_PAD = 32


def _dwconv31(name, u, w3, b3, l, seq):
    t, c = u.shape
    tm, cb = TM_EW, CONV_CB
    seg, seq_tiles, n_tiles = tm // SUBLANES, seq // tm, t // tm
    n_wrap = CONV_K - 1
    tile, prev, _ = _tile_halo_specs(tm, lambda rows: (rows, cb), seg, n_tiles, lambda n, r: (r, n))

    def body(u_ref, halo_ref, w_ref, b_ref, o_ref, wrap_ref):
        keep = (pl.program_id(1) % seq_tiles != 0).astype(F32)
        _fill_wrap_prev(u_ref, halo_ref, wrap_ref, n_wrap, seg, seg, keep)
        _conv_fwd(u_ref, wrap_ref, w_ref, b_ref, o_ref, seg, CONV_K)

    return _call(body, name, (c // cb, n_tiles),
                 [tile, prev, pl.BlockSpec((None, CONV_K, cb), lambda n, i: (l, 0, n)),
                  pl.BlockSpec((None, 1, cb), lambda n, i: (l, 0, n))],
                 tile, _sds((t, c), F32), [pltpu.VMEM((n_wrap * SUBLANES, cb), F32)])(u, u, w3, b3)


def _dwconv31_bwd(name, dc, u, w3, l, seq):
    t, c = dc.shape
    tm, cb = TM_EW, CONV_CB
    seg, seq_tiles, n_tiles = tm // SUBLANES, seq // tm, t // tm
    n_wrap = CONV_K - 1
    tile, prev, nxt = _tile_halo_specs(tm, lambda rows: (rows, cb), seg, n_tiles, lambda n, r: (r, n))

    def body(dc_ref, dcn_ref, u_ref, up_ref, w_ref, du_ref, dw_ref, dwrap_ref, uwrap_ref):
        i = pl.program_id(1)
        keep_prev = (i % seq_tiles != 0).astype(F32)
        keep_next = (i % seq_tiles != seq_tiles - 1).astype(F32)
        _fill_wrap_next(dc_ref, dcn_ref, dwrap_ref, n_wrap, keep_next)
        _conv_bwd_data(dc_ref, dwrap_ref, w_ref, du_ref, seg, CONV_K)

        @pl.when(i == 0)
        def _():
            dw_ref[...] = jnp.zeros_like(dw_ref)

        _fill_wrap_prev(u_ref, up_ref, uwrap_ref, n_wrap, seg, seg, keep_prev)
        _conv_bwd_taps(dc_ref, u_ref, uwrap_ref, dw_ref, seg, CONV_K)

    wrap = pltpu.VMEM((n_wrap * SUBLANES, cb), F32)
    return _call(body, name, (c // cb, n_tiles),
                 [tile, nxt, tile, prev, pl.BlockSpec((None, CONV_K, cb), lambda n, i: (l, 0, n))],
                 [tile, pl.BlockSpec((TAPS_PAD, cb), lambda n, i: (0, n))],
                 [_sds((t, c), F32), _sds((TAPS_PAD, c), F32)], [wrap, wrap])(dc, dc, u, u, w3)


def _ln_silu(name, cx, g3, b3, l):
    t, d = cx.shape
    tm = min(TM_ROW, t)

    def body(c_ref, g_ref, b_ref, o_ref):
        xhat, _ = _ln_stats(c_ref[...])
        nv = xhat * g_ref[...] + b_ref[...]
        o_ref[...] = (nv * _sigmoid(nv)).astype(BF16)

    return _call(body, name, (t // tm,), [_row_spec(tm, d), _param_spec(l, d), _param_spec(l, d)],
                 _row_spec(tm, d), _sds((t, d), BF16))(cx, g3, b3)


def _ln_silu_bwd(name, ds, cx, g3, b3, l, deps=()):
    t, d = cx.shape
    tm = min(TM_ROW, t)

    def body(ds_ref, c_ref, g_ref, b_ref, dc_ref, dg_ref, db_ref, dsum_ref):
        first = pl.program_id(0) == 0
        xhat, rstd = _ln_stats(c_ref[...])
        g = g_ref[...]
        nv = xhat * g + b_ref[...]
        sg = _sigmoid(nv)
        dn = ds_ref[...] * (sg * (1.0 + nv * (1.0 - sg)))
        dc = _ln_backward(dn * g, xhat, rstd)
        dc_ref[...] = dc
        _acc_rows(dg_ref, _colsum(dn * xhat), first)
        _acc_rows(db_ref, _colsum(dn), first)
        _acc_rows(dsum_ref, _colsum(dc), first)

    vec = pl.BlockSpec((1, d), lambda i: (0, 0))
    return _call(body, name, (t // tm,),
                 [_row_spec(tm, d), _row_spec(tm, d), _param_spec(l, d), _param_spec(l, d)],
                 [_row_spec(tm, d), vec, vec, vec],
                 [_sds((t, d), F32)] + [_sds((1, d), F32)] * 3, deps=deps)(ds, cx, g3, b3)


FFN_HALO = FFN_K - 1


def _ffn_conv(x_ref, halo_ref, wrap_ref, w_ref, b_ref, keep, seg, out_ref):
    _fill_wrap_prev(x_ref, halo_ref, wrap_ref, FFN_K - 1, FFN_HALO, seg, keep)
    _conv_fwd(x_ref, wrap_ref, w_ref, b_ref, out_ref, seg, FFN_K)


TM_FFN = 512
WRAP_ROWS = FFN_HALO * SUBLANES


def _sub_tiles(x_ref, prev_ref, next_ref, keep_prev, keep_next, n_sub):
    out = []
    for s in range(n_sub):
        tile = x_ref.at[pl.ds(s * TM_EW, TM_EW)]
        prev = prev_ref if s == 0 else x_ref.at[pl.ds(s * TM_EW - WRAP_ROWS, WRAP_ROWS)]
        nxt = next_ref if s == n_sub - 1 else x_ref.at[pl.ds((s + 1) * TM_EW, WRAP_ROWS)]
        out.append((tile, prev, keep_prev if s == 0 else 1.0, nxt, keep_next if s == n_sub - 1 else 1.0))
    return out


def _rows(ref, s, rows):
    return ref.at[pl.ds(s * rows, rows)]


def _ffn_specs(tm, fb, n_tiles):
    return _tile_halo_specs(tm, lambda rows: (None, rows, fb), FFN_HALO, n_tiles, lambda n, r: (n, r, 0))


def _ffn_up_act(name, x, w_up, b_up4, wdw, bdw, l, seq):
    t, d = x.shape
    nb, fb, _ = w_up.shape
    half = nb // 2
    tm = min(TM_FFN, seq)
    n_sub, seg, seq_steps, n_steps = tm // TM_EW, TM_EW // SUBLANES, seq // tm, t // tm
    per = tm // WRAP_ROWS
    nt_dims = (_DIMS["nt"], ((), ()))

    def body(x_ref, xp_ref, ug_ref, uv_ref, bug_ref, buv_ref, wg_ref, wv_ref, bg_ref, bv_ref,
             hg_ref, hv_ref, a_ref, pg_ref, pv_ref, gwrap_ref, vwrap_ref, cg_ref, cv_ref):
        keep = (pl.program_id(1) % seq_steps != 0).astype(F32)
        xb, xpb = x_ref[...].astype(BF16), xp_ref[...].astype(BF16)
        hg_ref[...] = lax.dot_general(xb, ug_ref[...], nt_dims, preferred_element_type=F32) + bug_ref[...]
        pg_ref[...] = lax.dot_general(xpb, ug_ref[...], nt_dims, preferred_element_type=F32) + bug_ref[...]
        for s, (tile, prev, kp, _, _) in enumerate(_sub_tiles(hg_ref, pg_ref, None, keep, None, n_sub)):
            _ffn_conv(tile, prev, gwrap_ref, wg_ref, bg_ref, kp, seg, _rows(cg_ref, s, TM_EW))
        hv_ref[...] = lax.dot_general(xb, uv_ref[...], nt_dims, preferred_element_type=F32) + buv_ref[...]
        pv_ref[...] = lax.dot_general(xpb, uv_ref[...], nt_dims, preferred_element_type=F32) + buv_ref[...]
        for s, (tile, prev, kp, _, _) in enumerate(_sub_tiles(hv_ref, pv_ref, None, keep, None, n_sub)):
            _ffn_conv(tile, prev, vwrap_ref, wv_ref, bv_ref, kp, seg, _rows(cv_ref, s, TM_EW))
        cg = cg_ref[...]
        a_ref[...] = (cg * _sigmoid(cg) * cv_ref[...]).astype(BF16)

    def blk(shift):
        return pl.BlockSpec((None, fb, d), lambda n, i: (n + shift, 0, 0))

    def vec(shift, rows):
        return pl.BlockSpec((None, None, rows, fb), lambda n, i: (l, n + shift, 0, 0))

    out = pl.BlockSpec((None, tm, fb), lambda n, i: (n, i, 0))
    tmp = pltpu.VMEM((tm, fb), F32)
    halo = pltpu.VMEM((WRAP_ROWS, fb), F32)
    return _call(body, name, (half, n_steps),
                 [pl.BlockSpec((tm, d), lambda n, i: (i, 0)),
                  pl.BlockSpec((WRAP_ROWS, d), lambda n, i: (jnp.maximum(i * per - 1, 0), 0)),
                  blk(0), blk(half), vec(0, 1), vec(half, 1), vec(0, FFN_K), vec(half, FFN_K), vec(0, 1), vec(half, 1)],
                 [out, out, out],
                 [_sds((half, t, fb), F32), _sds((half, t, fb), F32), _sds((half, t, fb), BF16)],
                 [halo, halo, halo, halo, tmp, tmp])(x, x, w_up, w_up, b_up4, b_up4, wdw, wdw, bdw, bdw)


def _ffn_act_bwd(name, dy, w_down, hg, hv, wdw, bdw, l, seq, deps=()):
    half, t, fb = hg.shape
    d = dy.shape[-1]
    tm = min(TM_FFN, seq)
    n_sub, seg, seq_steps, n_steps = tm // TM_EW, TM_EW // SUBLANES, seq // tm, t // tm
    tile, prev, _ = _ffn_specs(tm, fb, n_steps)

    def body(dy_ref, wd_ref, g_ref, gp_ref, v_ref, vp_ref, wg_ref, wv_ref, bg_ref, bv_ref,
             dg_ref, dv_ref, dbg_ref, dbv_ref, dwg_ref, dwv_ref, gwrap_ref, vwrap_ref, cg_ref, cv_ref):
        i = pl.program_id(1)
        first = i == 0
        keep = (i % seq_steps != 0).astype(F32)
        da = lax.dot_general(dy_ref[...].astype(BF16), wd_ref[...], (_DIMS["nt"], ((), ())),
                             preferred_element_type=F32)
        g_tiles = _sub_tiles(g_ref, gp_ref, None, keep, None, n_sub)
        v_tiles = _sub_tiles(v_ref, vp_ref, None, keep, None, n_sub)
        for s in range(n_sub):
            _ffn_conv(g_tiles[s][0], g_tiles[s][1], _rows(gwrap_ref, s, WRAP_ROWS), wg_ref, bg_ref, g_tiles[s][2],
                      seg, _rows(cg_ref, s, TM_EW))
            _ffn_conv(v_tiles[s][0], v_tiles[s][1], _rows(vwrap_ref, s, WRAP_ROWS), wv_ref, bv_ref, v_tiles[s][2],
                      seg, _rows(cv_ref, s, TM_EW))
        cg, cv = cg_ref[...], cv_ref[...]
        sg = _sigmoid(cg)
        dcv = da * cg * sg
        dcg = da * cv * sg * (1.0 + cg * (1.0 - sg))
        dg_ref[...] = dcg
        dv_ref[...] = dcv
        _acc_rows(dbg_ref, _colsum(dcg), first)
        _acc_rows(dbv_ref, _colsum(dcv), first)

        @pl.when(first)
        def _():
            dwg_ref[...] = jnp.zeros_like(dwg_ref)
            dwv_ref[...] = jnp.zeros_like(dwv_ref)

        for s in range(n_sub):
            _conv_bwd_taps(_rows(dg_ref, s, TM_EW), g_tiles[s][0], _rows(gwrap_ref, s, WRAP_ROWS), dwg_ref, seg, FFN_K)
            _conv_bwd_taps(_rows(dv_ref, s, TM_EW), v_tiles[s][0], _rows(vwrap_ref, s, WRAP_ROWS), dwv_ref, seg, FFN_K)

    def vec(shift, rows):
        return pl.BlockSpec((None, None, rows, fb), lambda n, i: (l, n + shift, 0, 0))

    def acc(rows):
        return pl.BlockSpec((None, rows, fb), lambda n, i: (n, 0, 0))

    wrap = pltpu.VMEM((n_sub * WRAP_ROWS, fb), F32)
    tmp = pltpu.VMEM((tm, fb), F32)
    return _call(body, name, (half, n_steps),
                 [pl.BlockSpec((tm, d), lambda n, i: (i, 0)), pl.BlockSpec((None, fb, d), lambda n, i: (n, 0, 0)),
                  tile, prev, tile, prev, vec(0, FFN_K), vec(half, FFN_K), vec(0, 1), vec(half, 1)],
                 [tile, tile, acc(1), acc(1), acc(SUBLANES), acc(SUBLANES)],
                 [_sds((half, t, fb), F32), _sds((half, t, fb), F32), _sds((half, 1, fb), F32),
                  _sds((half, 1, fb), F32), _sds((half, SUBLANES, fb), F32), _sds((half, SUBLANES, fb), F32)],
                 [wrap, wrap, tmp, tmp], deps=deps)(dy, w_down, hg, hg, hv, hv, wdw, wdw, bdw, bdw)


def _ffn_conv_t_dx(name, dcg, dcv, wdw, w_up, res, xhat, rstd, g3, l, seq, deps=()):
    half, t, fb = dcg.shape
    nb, d = 2 * half, res.shape[-1]
    tm = min(TM_FFN, seq)
    n_sub, seg, seq_steps, n_steps = tm // TM_EW, TM_EW // SUBLANES, seq // tm, t // tm
    per = tm // WRAP_ROWS

    def body(g_ref, gn_ref, v_ref, vn_ref, w_ref, up_ref, res_ref, xh_ref, rs_ref, gam_ref,
             dh_ref, db_ref, dr_ref, dgam_ref, dbeta_ref, dsum_ref, wrap_ref, out_ref, acc_ref):
        i, n = pl.program_id(0), pl.program_id(1)
        keep = (i % seq_steps != seq_steps - 1).astype(F32)

        def conv_t(d_ref, dn_ref):
            for s, (sub, _, _, nx, kn) in enumerate(_sub_tiles(d_ref, None, dn_ref, None, keep, n_sub)):
                _fill_wrap_next(sub, nx, wrap_ref, FFN_K - 1, kn)
                _conv_bwd_data(sub, wrap_ref, w_ref, _rows(out_ref, s, TM_EW), seg, FFN_K)

        @pl.when(n < half)
        def _():
            conv_t(g_ref, gn_ref)

        @pl.when(n >= half)
        def _():
            conv_t(v_ref, vn_ref)

        dh = out_ref[...]
        dhb = dh.astype(BF16)
        dh_ref[...] = dhb
        _acc_rows(db_ref.at[n], _colsum(dh), i == 0)
        p = jnp.dot(dhb, up_ref[...], preferred_element_type=F32)

        @pl.when(n == 0)
        def _():
            acc_ref[...] = p

        @pl.when(n > 0)
        def _():
            acc_ref[...] += p

        @pl.when(n == nb - 1)
        def _():
            _ln_bwd_rows(acc_ref[...] + ALPHA * res_ref[...], xh_ref, rs_ref, gam_ref, dr_ref, dgam_ref, dbeta_ref,
                         dsum_ref, i == 0)

    def src(gate):
        def blk(n):
            return jnp.minimum(n, half - 1) if gate else jnp.maximum(n - half, 0)
        tile = pl.BlockSpec((None, tm, fb), lambda i, n: (blk(n), i, 0))
        nxt = pl.BlockSpec((None, WRAP_ROWS, fb),
                           lambda i, n: (blk(n), jnp.minimum((i + 1) * per, n_steps * per - 1), 0))
        return [tile, nxt]

    row = pl.BlockSpec((tm, d), lambda i, n: (i, 0))
    ln_ins, ln_outs = _ln_bwd_specs(tm, d, l, lambda i, n: (i, 0))
    tmp = pltpu.VMEM((tm, fb), F32)
    halo = pltpu.VMEM((WRAP_ROWS, fb), F32)
    return _call(body, name, (n_steps, nb),
                 src(True) + src(False) +
                 [pl.BlockSpec((None, None, FFN_K, fb), lambda i, n: (l, n, 0, 0)),
                  pl.BlockSpec((None, fb, d), lambda i, n: (n, 0, 0)), row] + ln_ins,
                 [pl.BlockSpec((None, tm, fb), lambda i, n: (n, i, 0)),
                  pl.BlockSpec((nb, 1, fb), lambda i, n: (0, 0, 0))] + ln_outs,
                 [_sds((nb, t, fb), BF16), _sds((nb, 1, fb), F32), _sds((t, d), F32)] + [_sds((1, d), F32)] * 3,
                 [halo, tmp, pltpu.VMEM((tm, d), F32)],
                 deps=deps)(dcg, dcg, dcv, dcv, wdw, w_up, res, xhat, rstd, g3)


def _gelu_parts(h):
    cdf = 0.5 * (1.0 + lax.erf(h * INV_SQRT2))
    return h * cdf, cdf


def _seg_axis(a, axis, fn):
    return jnp.moveaxis(fn(jnp.moveaxis(a, axis, 0), TM_EW), 0, axis)


def _sgu_operands(w_s, b_s):
    nl = w_s.shape[0]
    n_sub = TM_EW // CHUNK
    tril = jnp.tril(jnp.ones((CHUNK, CHUNK), dtype=bool))
    w_causal = jnp.where(tril, w_s, 0.0)
    w_tile = (jnp.eye(n_sub, dtype=F32)[None, None, :, None, :, None] * w_causal[:, :, None, :, None, :]).reshape(
        nl, GROUPS, TM_EW, TM_EW)
    w_tile = _seg_axis(_seg_axis(w_tile, 2, _to_segments), 3, _to_segments).astype(BF16)
    bs_tile = jnp.broadcast_to(b_s[:, :, None, :, None], (nl, GROUPS, n_sub, CHUNK, CHUNK)).reshape(
        nl, GROUPS, TM_EW, CHUNK)
    return w_tile, _seg_axis(bs_tile, 2, _to_segments)


def _sgu_param_grads(dwt, dbt):
    n_sub = TM_EW // CHUNK
    tril = jnp.tril(jnp.ones((CHUNK, CHUNK), dtype=bool))
    dwt = _seg_axis(_seg_axis(dwt, 1, _from_segments), 2, _from_segments).reshape(GROUPS, n_sub, CHUNK, n_sub, CHUNK)
    dw = sum(dwt[:, a, :, a, :] for a in range(n_sub))
    db = _seg_axis(dbt, 1, _from_segments).reshape(GROUPS, n_sub, CHUNK).sum(axis=1)
    return jnp.where(tril, dw, 0.0), db


def _sgu(name, h, g3, b3, wt, bst, l):
    t, c2 = h.shape
    c = c2 // 2
    tm = TM_EW

    def body(h_ref, g_ref, b_ref, wt_ref, bs_ref, o_ref):
        z, _ = _gelu_parts(h_ref[...])
        u = z[:, :c]
        xhat, _ = _ln_stats(z[:, c:])
        vnb = (xhat * g_ref[...] + b_ref[...]).astype(BF16)
        for gi in range(GROUPS):
            cs = slice(gi * CHUNK, (gi + 1) * CHUNK)
            sp = jnp.dot(wt_ref[gi], vnb[:, cs], preferred_element_type=F32) + bs_ref[gi]
            o_ref[:, cs] = (u[:, cs] * sp).astype(BF16)

    return _call(body, name, (t // tm,),
                 [_row_spec(tm, c2), _param_spec(l, c), _param_spec(l, c),
                  pl.BlockSpec((None, GROUPS, tm, tm), lambda i: (l, 0, 0, 0)),
                  pl.BlockSpec((None, GROUPS, tm, CHUNK), lambda i: (l, 0, 0, 0))],
                 _row_spec(tm, c), _sds((t, c), BF16))(h, g3, b3, wt, bst)


def _sgu_bwd(name, dq, h, g3, b3, wt, bst, l, deps=()):
    t, c2 = h.shape
    c = c2 // 2
    tm = TM_EW
    n_tiles = t // tm

    def body(dq_ref, h_ref, g_ref, b_ref, wt_ref, bs_ref,
             dh_ref, dbin_ref, dw_ref, dbs_ref, dg_ref, db_ref, du_ref, dvn_ref, bsum_ref):
        i = pl.program_id(0)
        first = i == 0
        hv = h_ref[...]
        z, cdf = _gelu_parts(hv)
        u = z[:, :c]
        xhat, rstd = _ln_stats(z[:, c:])
        g = g_ref[...]
        vnb = (xhat * g + b_ref[...]).astype(BF16)

        @pl.when(first)
        def _():
            dw_ref[...] = jnp.zeros_like(dw_ref)
            bsum_ref[...] = jnp.zeros_like(bsum_ref)

        for gi in range(GROUPS):
            cs = slice(gi * CHUNK, (gi + 1) * CHUNK)
            vb = vnb[:, cs]
            w = wt_ref[gi]
            sp = jnp.dot(w, vb, preferred_element_type=F32) + bs_ref[gi]
            dqb = dq_ref[:, cs]
            du_ref[:, cs] = dqb * sp
            dsp = dqb * u[:, cs]
            bsum_ref[gi] += dsp
            dspb = dsp.astype(BF16)
            dw_ref[gi] += lax.dot_general(dspb, vb, (_DIMS["nt"], ((), ())), preferred_element_type=F32)
            dvn_ref[:, cs] = lax.dot_general(w, dspb, (_DIMS["tn"], ((), ())), preferred_element_type=F32)

        dvn = dvn_ref[...]
        dv = _ln_backward(dvn * g, xhat, rstd)
        pdf = jnp.exp(-0.5 * hv * hv) * INV_SQRT2PI
        dgelu = cdf + hv * pdf
        dhu = du_ref[...] * dgelu[:, :c]
        dhv = dv * dgelu[:, c:]
        dh_ref[:, :c] = dhu.astype(BF16)
        dh_ref[:, c:] = dhv.astype(BF16)
        _acc_rows(dbin_ref.at[:, :c], _colsum(dhu), first)
        _acc_rows(dbin_ref.at[:, c:], _colsum(dhv), first)
        _acc_rows(dg_ref, _colsum(dvn * xhat), first)
        _acc_rows(db_ref, _colsum(dvn), first)

        @pl.when(i == n_tiles - 1)
        def _():
            dbs_ref[...] = jnp.sum(bsum_ref[...], axis=-1)

    vec = pl.BlockSpec((1, c), lambda i: (0, 0))
    return _call(body, name, (n_tiles,),
                 [_row_spec(tm, c), _row_spec(tm, c2), _param_spec(l, c), _param_spec(l, c),
                  pl.BlockSpec((None, GROUPS, tm, tm), lambda i: (l, 0, 0, 0)),
                  pl.BlockSpec((None, GROUPS, tm, CHUNK), lambda i: (l, 0, 0, 0))],
                 [_row_spec(tm, c2), pl.BlockSpec((1, c2), lambda i: (0, 0)),
                  pl.BlockSpec((GROUPS, tm, tm), lambda i: (0, 0, 0)),
                  pl.BlockSpec((GROUPS, tm), lambda i: (0, 0)), vec, vec],
                 [_sds((t, c2), BF16), _sds((1, c2), F32), _sds((GROUPS, tm, tm), F32),
                  _sds((GROUPS, tm), F32), _sds((1, c), F32), _sds((1, c), F32)],
                 [pltpu.VMEM((tm, c), F32), pltpu.VMEM((tm, c), F32), pltpu.VMEM((GROUPS, tm, CHUNK), F32)],
                 deps=deps)(dq, h, g3, b3, wt, bst)


def _loss(name, y, target):
    t, d = y.shape
    tm = min(TM_ROW, t)
    n_tiles = t // tm

    def body(y_ref, t_ref, l_ref, dy_ref, acc_ref):
        i = pl.program_id(0)
        diff = y_ref[...] - t_ref[...]
        dy_ref[...] = diff * (1.0 / d)
        _acc_rows(acc_ref, _colsum(diff * diff), i == 0)

        @pl.when(i == n_tiles - 1)
        def _():
            l_ref[...] = jnp.broadcast_to(jnp.sum(acc_ref[...], axis=-1, keepdims=True) * (0.5 / d), (1, LANES))

    return _call(body, name, (n_tiles,), [_row_spec(tm, d), _row_spec(tm, d)],
                 [pl.BlockSpec((1, LANES), lambda i: (0, 0)), _row_spec(tm, d)],
                 [_sds((1, LANES), F32), _sds((t, d), F32)], [pltpu.VMEM((1, d), F32)])(y, target)


def _adamw(g, w, m, v):
    m2 = ADAM_B1 * m + (1.0 - ADAM_B1) * g
    v2 = ADAM_B2 * v + (1.0 - ADAM_B2) * (g * g)
    m_hat = m2 / (1.0 - ADAM_B1 ** ADAM_STEP)
    v_hat = v2 / (1.0 - ADAM_B2 ** ADAM_STEP)
    delta = -ADAM_LR * (m_hat / (jnp.sqrt(v_hat) + ADAM_EPS) + ADAM_WD * w)
    return delta, m2, v2


ROW_TILE_CAP = 512


def _row_tile(rows, cap=ROW_TILE_CAP):
    if rows <= cap:
        return rows
    for tr in range(cap, 15, -16):
        if rows % tr == 0:
            return tr
    return rows


def _sum8_adamw(name, dev, lands, parts, w, m, v):
    nl = len(lands)
    _, r, c = lands[0].shape
    tr = _row_tile(r, cap=128)

    def body(dev_ref, *refs):
        land, own = refs[:nl], refs[nl:2 * nl]
        w_ref, m_ref, v_ref, g_ref, d_ref, m2_ref, v2_ref = refs[2 * nl:]
        layer, me = pl.program_id(0), dev_ref[0]
        for l in range(nl):
            @pl.when(layer == l)
            def _(l=l):
                g = None
                for s in range(N_DEV):
                    part = jnp.where(me == s, own[l][...], land[l][s]).astype(F32)
                    g = part if g is None else g + part
                delta, m2, v2 = _adamw(g, w_ref[...], m_ref[...], v_ref[...])
                g_ref[...] = g
                d_ref[...] = delta
                m2_ref[...] = m2
                v2_ref[...] = v2

    def rows_of(l, a, i):
        return jnp.where(a == l, i, 0)

    spec = pl.BlockSpec((None, tr, c), lambda a, i, dev_ref: (a, i, 0))
    in_specs = [pl.BlockSpec((N_DEV, tr, c), lambda a, i, dev_ref, l=l: (0, rows_of(l, a, i), 0)) for l in range(nl)]
    in_specs += [pl.BlockSpec((None, tr, c), lambda a, i, dev_ref, l=l: (dev_ref[0], rows_of(l, a, i), 0))
                 for l in range(nl)]
    grid_spec = pltpu.PrefetchScalarGridSpec(
        num_scalar_prefetch=1, grid=(nl, r // tr), in_specs=in_specs + [spec] * 3, out_specs=[spec] * 4)
    return pl.pallas_call(
        body, name=name, grid_spec=grid_spec, out_shape=[_sds(w.shape, F32)] * 4,
        compiler_params=pltpu.CompilerParams(vmem_limit_bytes=VMEM_LIMIT))(dev, *lands, *parts, w, m, v)


def _sum8(name, parts):
    _, r, c = parts.shape
    tr = _row_tile(r)

    def body(p_ref, o_ref):
        acc = p_ref[0]
        for s in range(1, N_DEV):
            acc = acc + p_ref[s]
        o_ref[...] = acc

    return _call(body, name, (r // tr,), [pl.BlockSpec((N_DEV, tr, c), lambda i: (0, i, 0))],
                 pl.BlockSpec((tr, c), lambda i: (i, 0)), _sds((r, c), F32))(parts)


def _adamw_small(name, gs, ws, ms, vs):
    n = len(gs)

    def body(*refs):
        g, w, m, v = (refs[k * n:(k + 1) * n] for k in range(4))
        d_out, m_out, v_out = (refs[(4 + k) * n:(5 + k) * n] for k in range(3))
        for k in range(n):
            d_out[k][...], m_out[k][...], v_out[k][...] = _adamw(g[k][...], w[k][...], m[k][...], v[k][...])

    vmem = pl.BlockSpec(memory_space=pltpu.VMEM)
    outs = _call(body, name, (), [vmem] * (4 * n), [vmem] * (3 * n), [_sds(w.shape, F32) for w in ws] * 3)(
        *gs, *ws, *ms, *vs)
    return outs[:n], outs[n:2 * n], outs[2 * n:]


def _pack(arrs, row_multiple=SUBLANES):
    pieces, rows = [], 0
    for a in arrs:
        piece = a.reshape(-1, LANES)
        piece = jnp.pad(piece, ((0, (-piece.shape[0]) % SUBLANES), (0, 0)))
        pieces.append(piece)
        rows += piece.shape[0]
    if rows % row_multiple:
        pieces.append(jnp.zeros(((-rows) % row_multiple, LANES), pieces[0].dtype))
    return jnp.concatenate(pieces, axis=0)


def _unpack(buf, shapes, lead=0):
    out, pos = [], 0
    for shp in shapes:
        rows = math.prod(shp) // LANES
        piece = lax.slice_in_dim(buf, pos, pos + rows, axis=lead)
        out.append(piece.reshape(buf.shape[:lead] + tuple(shp)))
        pos += rows + (-rows) % SUBLANES
    return out


REPLICATED = ["conv_b_in", "conv_b_dw", "conv_ln_g", "conv_ln_b", "conv_b_out", "gmlp_w_s", "gmlp_b_s",
              "ffn_b_up", "ffn_b_dw", "ffn_b_down", "norm1_g", "norm1_b", "norm2_g", "norm2_b"]
SMALL_SHARDED = ["conv_w_dw", "gmlp_b_in", "gmlp_ln_g", "gmlp_ln_b", "gmlp_b_out", "ffn_w_dw"]
BIG = ["conv_w_in", "conv_w_out", "gmlp_w_in", "gmlp_w_out", "ffn_w_up", "ffn_w_down"]
WEIGHTS = ["conv_w_in", "conv_b_in", "conv_w_dw", "conv_b_dw", "conv_ln_g", "conv_ln_b", "conv_w_out", "conv_b_out",
           "gmlp_w_in", "gmlp_b_in", "gmlp_ln_g", "gmlp_ln_b", "gmlp_w_s", "gmlp_b_s", "gmlp_w_out", "gmlp_b_out",
           "ffn_w_up", "ffn_b_up", "ffn_w_dw", "ffn_b_dw", "ffn_w_down", "ffn_b_down",
           "norm1_g", "norm1_b", "norm2_g", "norm2_b"]


def _from_shards(g, lead_shape):
    nd = len(lead_shape)
    perm = tuple(range(1, nd + 1)) + (0, nd + 1)
    return g.transpose(perm).reshape(tuple(lead_shape) + (-1,))


def _to_shards(full, width):
    lead = full.shape[:-1]
    nd = len(lead)
    parts = full.reshape(lead + (N_DEV, width))
    return parts.transpose((nd,) + tuple(range(nd)) + (nd + 1,))


def _step(p):
    x_in, target_in = p["x"], p["loss_target"]
    bsz, seq, d = x_in.shape
    t = bsz * seq
    assert seq % TM_EW == 0 and TM_EW % CHUNK == 0 and TM_EW // SUBLANES >= CONV_K - 1
    x0 = _to_segments(x_in.reshape(t, d), TM_EW)
    target = _to_segments(target_in.reshape(t, d), TM_EW)
    n_conv, n_gmlp = p["conv_w_in"].shape[0], p["gmlp_w_in"].shape[0]
    fb = p["ffn_w_up"].shape[-1]
    nblk = N_DEV
    half = nblk // 2
    cw = p["conv_w_in"].shape[-1]
    tm = min(TM_MM, t)
    nt = t // tm
    dev = 4 * lax.axis_index("x") + 2 * lax.axis_index("y") + lax.axis_index("c")

    small_shapes = [p[n].shape for n in SMALL_SHARDED]
    small_src = _pack([p[n] for n in SMALL_SHARDED])[None]
    small_all = _all_gather("gather_small_weights", [small_src])[0][0]
    sm = _unpack(small_all, small_shapes, lead=1)
    w_src = []
    for i in range(DEPTH):
        mix = "conv" if i % 2 == 0 else "gmlp"
        w_src += [p[mix + "_w_in"][i // 2].astype(BF16), p[mix + "_w_out"][i // 2].astype(BF16),
                  p["ffn_w_up"][i].T.astype(BF16), p["ffn_w_down"][i].astype(BF16)]
    send_sems, recv_sems, w_land, _ = _gather_start(
        "weights_gather_start", _place_own("weights_place_own", w_src, deps=[small_all]))
    W_IN, W_OUT, W_UP, W_DOWN = range(4)

    def wait_weight(i, k, after):
        return _gather_wait(f"l{i}_weights_wait{k}", w_land[4 * i + k], send_sems, recv_sems, 4 * i + k, after)
    conv_w_dw = _from_shards(sm[0], sm[0].shape[1:-1])
    gmlp_b_in = _from_shards(sm[1], sm[1].shape[1:-1])
    gmlp_ln_g = _from_shards(sm[2], sm[2].shape[1:-1])
    gmlp_ln_b = _from_shards(sm[3], sm[3].shape[1:-1])
    gmlp_b_out = _from_shards(sm[4], sm[4].shape[1:-1])
    ffn_w_dw = sm[5].transpose(1, 0, 2, 3)

    def rows3(a):
        return a.reshape(a.shape[0], 1, a.shape[-1])

    conv_b_in4 = p["conv_b_in"].reshape(n_conv, N_DEV, 1, cw)
    gmlp_b_in4 = gmlp_b_in.reshape(n_gmlp, N_DEV, 1, cw)
    ffn_b_up4 = p["ffn_b_up"].reshape(DEPTH, nblk, 1, fb)
    ffn_b_dw4 = p["ffn_b_dw"].reshape(DEPTH, nblk, 1, fb)
    conv_b_dw3, conv_ln_g3, conv_ln_b3 = rows3(p["conv_b_dw"]), rows3(p["conv_ln_g"]), rows3(p["conv_ln_b"])
    conv_b_out3, gmlp_b_out3, ffn_b_down3 = rows3(p["conv_b_out"]), rows3(gmlp_b_out), rows3(p["ffn_b_down"])
    gmlp_ln_g3, gmlp_ln_b3 = rows3(gmlp_ln_g), rows3(gmlp_ln_b)
    n1g3, n1b3, n2g3, n2b3 = rows3(p["norm1_g"]), rows3(p["norm1_b"]), rows3(p["norm2_g"]), rows3(p["norm2_b"])
    w_tile, bs_tile = _sgu_operands(p["gmlp_w_s"], p["gmlp_b_s"])

    def mm_in(name, xa, wg, l, bias4):
        return _matmul(name, xa, wg, "nn", grid=(nt, N_DEV),
                       a_spec=pl.BlockSpec((tm, d), lambda i, n: (i, 0)),
                       b_spec=pl.BlockSpec((None, d, cw), lambda i, n: (n, 0, 0)),
                       o_spec=pl.BlockSpec((tm, cw), lambda i, n: (i, n)), o_shape=(t, N_DEV * cw), o_dtype=F32,
                       bias=bias4, bias_spec=pl.BlockSpec((None, None, 1, cw), lambda i, n: (l, n, 0, 0)))

    def mm_out_dx(name, dy, w, deps=()):
        return _matmul(name, dy, w, "nt", grid=(nt,),
                       a_spec=pl.BlockSpec((tm, d), lambda i: (i, 0)),
                       b_spec=pl.BlockSpec((d, d), lambda i: (0, 0)),
                       o_spec=pl.BlockSpec((tm, d), lambda i: (i, 0)), o_shape=(t, d), o_dtype=F32, deps=deps)

    def mm_out_dw(name, sa, dy):
        return _matmul(name, sa, dy, "tn", grid=(nt,), k_axis=0, nk=nt, acc_shape=(d, d),
                       a_spec=pl.BlockSpec((tm, d), lambda k: (k, 0)),
                       b_spec=pl.BlockSpec((tm, d), lambda k: (k, 0)),
                       o_spec=pl.BlockSpec((d, d), lambda k: (0, 0)), o_shape=(d, d), o_dtype=BF16)

    def mm_in_dx(name, dh, wg, res):
        return _matmul(name, dh, wg, "nt", grid=(nt, N_DEV), k_axis=1, nk=N_DEV, acc_shape=(tm, d),
                       a_spec=pl.BlockSpec((tm, cw), lambda i, n: (i, n)),
                       b_spec=pl.BlockSpec((None, d, cw), lambda i, n: (n, 0, 0)),
                       o_spec=pl.BlockSpec((tm, d), lambda i, n: (i, 0)), o_shape=(t, d), o_dtype=F32,
                       res=res, res_spec=pl.BlockSpec((tm, d), lambda i, n: (i, 0)), res_scale=ALPHA)

    def mm_in_dw(name, xa, dh):
        return _matmul(name, xa, dh, "tn", grid=(N_DEV, nt), k_axis=1, nk=nt, acc_shape=(d, cw),
                       a_spec=pl.BlockSpec((tm, d), lambda n, k: (k, 0)),
                       b_spec=pl.BlockSpec((tm, cw), lambda n, k: (k, n)),
                       o_spec=pl.BlockSpec((None, d, cw), lambda n, k: (n, 0, 0)),
                       o_shape=(N_DEV, d, cw), o_dtype=BF16)

    def mm_down_dw(name, a, dy, deps=()):
        return _matmul(name, a, dy, "tn", grid=(half, nt), k_axis=1, nk=nt, acc_shape=(fb, d),
                       a_spec=pl.BlockSpec((None, tm, fb), lambda n, k: (n, k, 0)),
                       b_spec=pl.BlockSpec((tm, d), lambda n, k: (k, 0)),
                       o_spec=pl.BlockSpec((None, fb, d), lambda n, k: (n, 0, 0)),
                       o_shape=(half, fb, d), o_dtype=BF16, deps=deps)

    def mm_up_dw(name, xa, dh):
        return _matmul(name, dh, xa, "tn", grid=(nblk, nt), k_axis=1, nk=nt, acc_shape=(fb, d),
                       a_spec=pl.BlockSpec((None, tm, fb), lambda n, k: (n, k, 0)),
                       b_spec=pl.BlockSpec((tm, d), lambda n, k: (k, 0)),
                       o_spec=pl.BlockSpec((None, fb, d), lambda n, k: (n, 0, 0)),
                       o_shape=(nblk, fb, d), o_dtype=BF16)

    saved = []
    xcur = x0
    for i in range(DEPTH):
        j = i // 2
        s = {"x": xcur}
        s["w_in"] = wait_weight(i, W_IN, xcur)
        if i % 2 == 0:
            s["h"] = mm_in(f"l{i}_conv_in", xcur, s["w_in"], j, conv_b_in4)
            s["u"] = _glu(f"l{i}_glu", s["h"])
            s["c"] = _dwconv31(f"l{i}_dwconv", s["u"], conv_w_dw, conv_b_dw3, j, seq)
            s["s"] = _ln_silu(f"l{i}_ln_silu", s["c"], conv_ln_g3, conv_ln_b3, j)
            b_out3 = conv_b_out3
        else:
            s["h"] = mm_in(f"l{i}_gmlp_in", xcur, s["w_in"], j, gmlp_b_in4)
            s["s"] = _sgu(f"l{i}_sgu", s["h"], gmlp_ln_g3, gmlp_ln_b3, w_tile, bs_tile, j)
            b_out3 = gmlp_b_out3
        s["w_out"] = wait_weight(i, W_OUT, s["s"]).reshape(d, d)
        s["x1"], s["xhat1"], s["rstd1"] = _matmul_ln(
            f"l{i}_mixer_out_norm1", s["s"], s["w_out"], xcur, b_out3, n1g3, n1b3, j, i)
        s["w_up"] = wait_weight(i, W_UP, s["x1"])
        s["hg"], s["hv"], s["a"] = _ffn_up_act(f"l{i}_ffn_up_act", s["x1"], s["w_up"], ffn_b_up4, ffn_w_dw, ffn_b_dw4,
                                               i, seq)
        s["w_down"] = wait_weight(i, W_DOWN, s["a"]).reshape(half, fb, d)
        xcur, s["xhat2"], s["rstd2"] = _matmul_ln(
            f"l{i}_ffn_down_norm2", s["a"], s["w_down"], s["x1"], ffn_b_down3, n2g3, n2b3, i, i)
        saved.append(s)

    loss_row, dx = _loss("loss", xcur, target)

    started = {n: [None] * p[n].shape[0] for n in BIG}
    tokens = []

    def send_grads(name, items):
        done, token = _scatter_start(name, [g for _, _, g in items])
        for (n, l, _), st in zip(items, done):
            started[n][l] = st
        tokens.append(token)

    def take_tokens():
        out = list(tokens)
        tokens.clear()
        return out

    gl = {n: [None] * p[n].shape[0] for n in REPLICATED + SMALL_SHARDED}
    dr2, gl["norm2_g"][DEPTH - 1], gl["norm2_b"][DEPTH - 1], gl["ffn_b_down"][DEPTH - 1] = _ln_res_bwd(
        f"l{DEPTH - 1}_norm2_bwd", dx, saved[-1]["xhat2"], saved[-1]["rstd2"], n2g3, DEPTH - 1)
    for i in reversed(range(DEPTH)):
        j = i // 2
        s = saved[i]
        mix = "conv" if i % 2 == 0 else "gmlp"
        g_down = mm_down_dw(f"l{i}_ffn_down_dw", s["a"], dr2, deps=take_tokens()).reshape(N_DEV, -1, d)
        send_grads(f"l{i}_ffn_down_grad_scatter_start", [("ffn_w_down", i, g_down)])
        dcg, dcv, dbg, dbv, dwg, dwv = _ffn_act_bwd(f"l{i}_ffn_act_bwd", dr2, s["w_down"], s["hg"], s["hv"],
                                                    ffn_w_dw, ffn_b_dw4, i, seq, deps=take_tokens())
        gl["ffn_b_dw"][i] = jnp.concatenate([dbg, dbv], axis=0).reshape(1, nblk * fb)
        gl["ffn_w_dw"][i] = jnp.concatenate([dwg[:, :FFN_K], dwv[:, :FFN_K]], axis=0)
        dh, dbu, dr1, gl["norm1_g"][i], gl["norm1_b"][i], gl[mix + "_b_out"][j] = _ffn_conv_t_dx(
            f"l{i}_ffn_conv_t_dx", dcg, dcv, ffn_w_dw, s["w_up"], dr2, s["xhat1"], s["rstd1"], n1g3, i, seq,
            deps=take_tokens())
        gl["ffn_b_up"][i] = dbu.reshape(1, nblk * fb)
        send_grads(f"l{i}_ffn_up_grad_scatter_start", [("ffn_w_up", i, mm_up_dw(f"l{i}_ffn_up_dw", s["x1"], dh))])
        ds = mm_out_dx(f"l{i}_{mix}_out_dx", dr1, s["w_out"], deps=take_tokens())
        g_out = mm_out_dw(f"l{i}_{mix}_out_dw", s["s"], dr1).reshape(N_DEV, -1, d)
        if i % 2 == 0:
            dc, gl["conv_ln_g"][j], gl["conv_ln_b"][j], gl["conv_b_dw"][j] = _ln_silu_bwd(
                f"l{i}_ln_silu_bwd", ds, s["c"], conv_ln_g3, conv_ln_b3, j)
            du, dwdw = _dwconv31_bwd(f"l{i}_dwconv_bwd", dc, s["u"], conv_w_dw, j, seq)
            gl["conv_w_dw"][j] = dwdw[:CONV_K]
            dh, gl["conv_b_in"][j] = _glu_bwd(f"l{i}_glu_bwd", du, s["h"])
        else:
            dh, gl["gmlp_b_in"][j], dwt, dbt, gl["gmlp_ln_g"][j], gl["gmlp_ln_b"][j] = _sgu_bwd(
                f"l{i}_sgu_bwd", ds, s["h"], gmlp_ln_g3, gmlp_ln_b3, w_tile, bs_tile, j)
            gl["gmlp_w_s"][j], gl["gmlp_b_s"][j] = _sgu_param_grads(dwt, dbt)
        dx = mm_in_dx(f"l{i}_{mix}_in_dx", dh, s["w_in"], dr1)
        if i > 0:
            prev = saved[i - 1]
            dr2, gl["norm2_g"][i - 1], gl["norm2_b"][i - 1], gl["ffn_b_down"][i - 1] = _ln_res_bwd(
                f"l{i - 1}_norm2_bwd", dx, prev["xhat2"], prev["rstd2"], n2g3, i - 1)
        send_grads(f"l{i}_mixer_grads_scatter_start",
                   [(mix + "_w_out", j, g_out), (mix + "_w_in", j, mm_in_dw(f"l{i}_{mix}_in_dw", s["x"], dh))])
    grad_x = _from_segments(dx, TM_EW).reshape(bsz, seq, d)

    full_small = {n: jnp.stack(gl[n]).reshape(p[n].shape) for n in REPLICATED}
    shard_small = {}
    for n in SMALL_SHARDED:
        if n == "ffn_w_dw":
            shard_small[n] = jnp.stack(gl[n]).transpose(1, 0, 2, 3)
        else:
            width = p[n].shape[-1]
            lead = p[n].shape[:-1]
            shard_small[n] = _to_shards(jnp.stack(gl[n]).reshape(lead + (N_DEV * width,)), width)
    flat_shapes = [(1, LANES)] + [p[n].shape for n in REPLICATED] + [(N_DEV,) + p[n].shape for n in SMALL_SHARDED]
    flat_local = _pack([loss_row] + [full_small[n] for n in REPLICATED] + [shard_small[n] for n in SMALL_SHARDED],
                       row_multiple=ROW_TILE_CAP)

    small_send, small_recv, small_land, small_token = _gather_start(
        "small_grads_gather_start", _place_own("small_grads_place_own", [flat_local]))

    grads, delta, new_m, new_v = {}, {}, {}, {}
    dev1 = jnp.reshape(dev, (1,)).astype(jnp.int32)
    order = ["ffn_w_down", "ffn_w_up", "gmlp_w_out", "gmlp_w_in", "conv_w_out", "conv_w_in"]
    after = small_token
    for n in order:
        parts_done, lands_done = _scatter_wait(f"grads_{n}_scatter_wait", started[n], after)
        state = [p[n], p["m_" + n], p["v_" + n]]
        if n == "ffn_w_up":
            state = [a.transpose(0, 2, 1) for a in state]
        outs = _sum8_adamw(f"adamw_{n}", dev1, lands_done, parts_done, *state)
        after = outs[-1]
        if n == "ffn_w_up":
            outs = [a.transpose(0, 2, 1) for a in outs]
        grads[n], delta[n], new_m[n], new_v[n] = outs

    small_parts = _gather_wait("small_grads_gather_wait", small_land[0], small_send, small_recv, 0, after)
    summed = _unpack(_sum8("sum_small_grads", small_parts), flat_shapes)
    loss = summed[0][0, 0]
    grads.update(zip(REPLICATED, summed[1:1 + len(REPLICATED)]))
    for n, g in zip(SMALL_SHARDED, summed[1 + len(REPLICATED):]):
        grads[n] = lax.dynamic_index_in_dim(g, dev, axis=0, keepdims=False)
    small = REPLICATED + SMALL_SHARDED
    d_s, m_s, v_s = _adamw_small("adamw_small", [grads[n] for n in small], [p[n] for n in small],
                                 [p["m_" + n] for n in small], [p["v_" + n] for n in small])
    for n, dd, mm, vv in zip(small, d_s, m_s, v_s):
        delta[n], new_m[n], new_v[n] = dd, mm, vv

    return (loss, grad_x, *[grads[n] for n in WEIGHTS], *[delta[n] for n in WEIGHTS],
            *[new_m[n] for n in WEIGHTS], *[new_v[n] for n in WEIGHTS])


def kernel(x, conv_w_in, conv_b_in, conv_w_dw, conv_b_dw, conv_ln_g, conv_ln_b, conv_w_out, conv_b_out, gmlp_w_in, gmlp_b_in, gmlp_ln_g, gmlp_ln_b, gmlp_w_s, gmlp_b_s, gmlp_w_out, gmlp_b_out, ffn_w_up, ffn_b_up, ffn_w_dw, ffn_b_dw, ffn_w_down, ffn_b_down, norm1_g, norm1_b, norm2_g, norm2_b, loss_target, m_conv_w_in, m_conv_b_in, m_conv_w_dw, m_conv_b_dw, m_conv_ln_g, m_conv_ln_b, m_conv_w_out, m_conv_b_out, m_gmlp_w_in, m_gmlp_b_in, m_gmlp_ln_g, m_gmlp_ln_b, m_gmlp_w_s, m_gmlp_b_s, m_gmlp_w_out, m_gmlp_b_out, m_ffn_w_up, m_ffn_b_up, m_ffn_w_dw, m_ffn_b_dw, m_ffn_w_down, m_ffn_b_down, m_norm1_g, m_norm1_b, m_norm2_g, m_norm2_b, v_conv_w_in, v_conv_b_in, v_conv_w_dw, v_conv_b_dw, v_conv_ln_g, v_conv_ln_b, v_conv_w_out, v_conv_b_out, v_gmlp_w_in, v_gmlp_b_in, v_gmlp_ln_g, v_gmlp_ln_b, v_gmlp_w_s, v_gmlp_b_s, v_gmlp_w_out, v_gmlp_b_out, v_ffn_w_up, v_ffn_b_up, v_ffn_w_dw, v_ffn_b_dw, v_ffn_w_down, v_ffn_b_down, v_norm1_g, v_norm1_b, v_norm2_g, v_norm2_b):
    return _step(dict(locals()))
```

```python
import math

import jax
import jax.numpy as jnp
from jax import lax
from jax.experimental import pallas as pl
from jax.experimental.pallas import tpu as pltpu

F32 = jnp.float32
BF16 = jnp.bfloat16
MESH = pl.DeviceIdType.MESH

N_DEV = 8
DEPTH = 4
ALPHA = (2.0 * DEPTH) ** 0.25
LN_EPS = 1e-5
CONV_K = 31
FFN_K = 3
CHUNK = 128
GROUPS = 8
ADAM_LR = 0.001
ADAM_B1 = 0.9
ADAM_B2 = 0.999
ADAM_EPS = 1e-08
ADAM_WD = 0.01
ADAM_STEP = 10
INV_SQRT2 = 1.0 / math.sqrt(2.0)
INV_SQRT2PI = 1.0 / math.sqrt(2.0 * math.pi)

LANES = 128
SUBLANES = 8
VMEM_LIMIT = 48 * 1024 * 1024
TM_MM = 1024
TK_DW = 2048
TM_EW = 256
TM_ROW = 512


def _call(body, name, grid, in_specs, out_specs, out_shape, scratch=(), aliases=None, deps=()):
    deps = list(deps)
    in_specs = list(in_specs)
    n_in = len(in_specs)
    if deps:
        inner = body

        def body(*refs):
            return inner(*refs[:n_in], *refs[n_in + len(deps):])

        in_specs = in_specs + [pl.BlockSpec(memory_space=pl.ANY)] * len(deps)
    fn = pl.pallas_call(
        body, name=name, grid=grid, in_specs=in_specs, out_specs=out_specs, out_shape=out_shape,
        scratch_shapes=list(scratch), input_output_aliases=aliases or {},
        compiler_params=pltpu.CompilerParams(vmem_limit_bytes=VMEM_LIMIT))
    return lambda *args: fn(*args, *deps)


def _sds(shape, dtype):
    return jax.ShapeDtypeStruct(tuple(shape), dtype)


def _sigmoid(x):
    return 1.0 / (1.0 + jnp.exp(-x))


def _acc_rows(ref, val, first):
    @pl.when(first)
    def _():
        ref[...] = val

    @pl.when(jnp.logical_not(first))
    def _():
        ref[...] += val


def _colsum(v):
    return jnp.sum(v, axis=0, keepdims=True)


_DIMS = {"nn": ((1,), (0,)), "nt": ((1,), (1,)), "tn": ((0,), (0,))}


def _matmul(name, a, b, mode, *, grid, a_spec, b_spec, o_spec, o_shape, o_dtype, k_axis=None, nk=1,
            acc_shape=None, bias=None, bias_spec=None, res=None, res_spec=None, res_scale=1.0, deps=()):
    dims = (_DIMS[mode], ((), ()))
    has_bias, has_res = bias is not None, res is not None

    def body(*refs):
        a_ref, b_ref = refs[0], refs[1]
        pos = 2
        bias_ref = res_ref = None
        if has_bias:
            bias_ref = refs[pos]
            pos += 1
        if has_res:
            res_ref = refs[pos]
            pos += 1
        o_ref = refs[pos]
        acc_ref = refs[pos + 1] if nk > 1 else None
        p = lax.dot_general(a_ref[...].astype(BF16), b_ref[...].astype(BF16), dims, preferred_element_type=F32)

        def finish(acc):
            if has_bias:
                acc = acc + bias_ref[...]
            if has_res:
                acc = acc + res_scale * res_ref[...]
            o_ref[...] = acc.astype(o_dtype)

        if nk == 1:
            finish(p)
        else:
            k = pl.program_id(k_axis)

            @pl.when(k == 0)
            def _():
                acc_ref[...] = p

            @pl.when(k > 0)
            def _():
                acc_ref[...] += p

            @pl.when(k == nk - 1)
            def _():
                finish(acc_ref[...])

    ins, specs = [a, b], [a_spec, b_spec]
    if has_bias:
        ins.append(bias)
        specs.append(bias_spec)
    if has_res:
        ins.append(res)
        specs.append(res_spec)
    scratch = [pltpu.VMEM(acc_shape, F32)] if nk > 1 else []
    return _call(body, name, grid, specs, o_spec, _sds(o_shape, o_dtype), scratch, deps=deps)(*ins)


TM_LN = 512


def _matmul_ln(name, a, b, x_res, bias3, g3, b3, l_bias, l_norm):
    t, d = x_res.shape
    tm = min(TM_LN, t)
    blocked = a.ndim == 3

    def body(a_ref, b_ref, x_ref, bias_ref, g_ref, be_ref, o_ref, xh_ref, rs_ref):
        if blocked:
            y = None
            for k in range(a.shape[0]):
                p = jnp.dot(a_ref[k], b_ref[k], preferred_element_type=F32)
                y = p if y is None else y + p
        else:
            y = jnp.dot(a_ref[...], b_ref[...], preferred_element_type=F32)
        xhat, rstd = _ln_stats(ALPHA * x_ref[...] + y + bias_ref[...])
        o_ref[...] = xhat * g_ref[...] + be_ref[...]
        xh_ref[...] = xhat
        rs_ref[...] = rstd

    if blocked:
        a_spec = pl.BlockSpec((a.shape[0], tm, a.shape[2]), lambda i: (0, i, 0))
        b_spec = pl.BlockSpec(b.shape, lambda i: (0, 0, 0))
    else:
        a_spec = pl.BlockSpec((tm, a.shape[1]), lambda i: (i, 0))
        b_spec = pl.BlockSpec(b.shape, lambda i: (0, 0))
    row = pl.BlockSpec((tm, d), lambda i: (i, 0))
    stat = pl.BlockSpec((tm, 1), lambda i: (i, 0))

    def vec(l):
        return pl.BlockSpec((None, 1, d), lambda i: (l, 0, 0))

    return _call(body, name, (t // tm,), [a_spec, b_spec, row, vec(l_bias), vec(l_norm), vec(l_norm)],
                 [row, row, stat], [_sds((t, d), F32), _sds((t, d), F32), _sds((t, 1), F32)])(
                     a, b, x_res, bias3, g3, b3)


def _mesh_pos():
    return lax.axis_index("x"), lax.axis_index("y"), lax.axis_index("c")


def _any_specs(n):
    return [pl.BlockSpec(memory_space=pl.ANY)] * n


def _all_gather(name, srcs):
    n = len(srcs)

    def body(*refs):
        src, out = refs[:n], refs[n:2 * n]
        send_sems, recv_sems, local_sems = refs[2 * n:]
        x, y, c = _mesh_pos()
        me, sibling = (x, y, c), (x, y, 1 - c)
        chips = [(1 - x, y), (x, 1 - y), (1 - x, 1 - y)]

        def slot(k, p):
            return out[k].at[:, 4 * p[0] + 2 * p[1] + p[2]]

        def copy(k, idx, block, to, s=None):
            return pltpu.make_async_remote_copy(
                src_ref=slot(k, block) if s is None else s, dst_ref=slot(k, block),
                send_sem=send_sems.at[k * 7 + idx], recv_sem=recv_sems.at[k * 7 + idx],
                device_id=to, device_id_type=MESH)

        local = [pltpu.make_async_copy(src[k], slot(k, me), local_sems.at[k]) for k in range(n)]
        for cp in local:
            cp.start()
        first = []
        for k in range(n):
            first.append(copy(k, 0, me, sibling, src[k]))
            for j, chip in enumerate(chips):
                first.append(copy(k, 1 + j, me, (*chip, c), src[k]))
        for cp in first:
            cp.start()
        passed = []
        for j, chip in enumerate(chips):
            for k in range(n):
                copy(k, 1 + j, (*chip, c), me).wait_recv()
                cp = copy(k, 4 + j, (*chip, c), sibling)
                cp.start()
                passed.append(cp)
        for k in range(n):
            copy(k, 0, sibling, me).wait_recv()
            for j, chip in enumerate(chips):
                copy(k, 4 + j, (*chip, 1 - c), me).wait_recv()
        for cp in first + passed:
            cp.wait_send()
        for cp in local:
            cp.wait()

    out_shape = [_sds((s.shape[0], N_DEV) + s.shape[1:], s.dtype) for s in srcs]
    return _call(body, name, (), [pl.BlockSpec(memory_space=pltpu.VMEM)] * n, _any_specs(n), out_shape,
                 [pltpu.SemaphoreType.DMA((7 * n,)), pltpu.SemaphoreType.DMA((7 * n,)),
                  pltpu.SemaphoreType.DMA((n,))])(*srcs)


HBM_SPEC = pl.BlockSpec(memory_space=pltpu.HBM)
SEM_SPEC = pl.BlockSpec(memory_space=pltpu.SEMAPHORE)
N_PEER = N_DEV - 1


def _split_call(body, name, in_specs, out_specs, out_shape, aliases):
    return pl.pallas_call(
        body, name=name, in_specs=in_specs, out_specs=out_specs, out_shape=out_shape, input_output_aliases=aliases,
        compiler_params=pltpu.CompilerParams(has_side_effects=pltpu.SideEffectType.DATAFLOW_SIDE_EFFECTING))


def _peers(x, y, c):
    return [(1 - x if q & 4 else x, 1 - y if q & 2 else y, 1 - c if q & 1 else c) for q in range(1, N_DEV)]


def _in_hbm(a):
    return pltpu.with_memory_space_constraint(a, pltpu.HBM)


def _place_own(name, srcs, deps=()):
    n = len(srcs)

    def body(*refs):
        src, out, sems = refs[:n], refs[n:2 * n], refs[2 * n]
        x, y, c = _mesh_pos()
        dev = 4 * x + 2 * y + c
        copies = [pltpu.make_async_copy(src[k], out[k].at[dev], sems.at[k]) for k in range(n)]
        for cp in copies:
            cp.start()
        for cp in copies:
            cp.wait()

    return _call(body, name, (), [pl.BlockSpec(memory_space=pltpu.VMEM)] * n, _any_specs(n),
                 [_sds((N_DEV,) + s.shape, s.dtype) for s in srcs], [pltpu.SemaphoreType.DMA((n,))],
                 deps=deps)(*srcs)


def _gather_start(name, lands):
    n = len(lands)

    def body(*refs):
        land, send_sems, recv_sems = refs[:n], refs[n], refs[n + 1]
        x, y, c = _mesh_pos()
        dev = 4 * x + 2 * y + c
        for k in range(n):
            for peer in _peers(x, y, c):
                pltpu.make_async_remote_copy(
                    src_ref=land[k].at[dev], dst_ref=land[k].at[dev], send_sem=send_sems.at[k],
                    recv_sem=recv_sems.at[k], device_id=peer, device_id_type=MESH).start()
        token = refs[-1]
        token[...] = jnp.zeros_like(token)

    outs = _split_call(
        body, name, [HBM_SPEC] * n, [SEM_SPEC, SEM_SPEC] + [HBM_SPEC] * n + [pl.BlockSpec(memory_space=pltpu.VMEM)],
        [pltpu.SemaphoreType.DMA((n,)), pltpu.SemaphoreType.DMA((n,))] + [pltpu.HBM(a.shape, a.dtype) for a in lands]
        + [_sds((SUBLANES, LANES), F32)],
        {k: 2 + k for k in range(n)})(*[_in_hbm(a) for a in lands])
    return outs[0], outs[1], list(outs[2:2 + n]), outs[-1]


def _wait_seven(src_ref, dst_ref, send_sem, recv_sem):
    cp = pltpu.make_async_remote_copy(
        src_ref=src_ref.at[pl.ds(0, N_PEER)], dst_ref=dst_ref.at[pl.ds(0, N_PEER)], send_sem=send_sem,
        recv_sem=recv_sem, device_id=_mesh_pos(), device_id_type=MESH)
    cp.wait_send()
    cp.wait_recv()


def _gather_wait(name, land, send_sems, recv_sems, k, after):
    def body(land_ref, send_ref, recv_ref, after_ref, out_ref):
        _wait_seven(land_ref, land_ref, send_ref.at[k], recv_ref.at[k])

    return _split_call(body, name, [HBM_SPEC, SEM_SPEC, SEM_SPEC, pl.BlockSpec(memory_space=pl.ANY)], HBM_SPEC,
                       pltpu.HBM(land.shape, land.dtype), {0: 0})(land, send_sems, recv_sems, after)


def _scatter_start(name, parts_list):
    n = len(parts_list)

    def body(*refs):
        x, y, c = _mesh_pos()
        dev = 4 * x + 2 * y + c
        for k in range(n):
            parts_ref, land_ref = refs[2 * k], refs[2 * k + 1]
            send_sem, recv_sem = refs[2 * n + 4 * k], refs[2 * n + 4 * k + 1]
            for peer in _peers(x, y, c):
                pltpu.make_async_remote_copy(
                    src_ref=parts_ref.at[4 * peer[0] + 2 * peer[1] + peer[2]], dst_ref=land_ref.at[dev],
                    send_sem=send_sem, recv_sem=recv_sem, device_id=peer, device_id_type=MESH).start()
        token = refs[-1]
        token[...] = jnp.zeros_like(token)

    ins, out_specs, out_shape, aliases = [], [], [], {}
    for k, parts in enumerate(parts_list):
        buf = pltpu.HBM(parts.shape, parts.dtype)
        ins += [_in_hbm(parts), _in_hbm(lax.empty(parts.shape, parts.dtype))]
        out_specs += [SEM_SPEC, SEM_SPEC, HBM_SPEC, HBM_SPEC]
        out_shape += [pltpu.SemaphoreType.DMA(()), pltpu.SemaphoreType.DMA(()), buf, buf]
        aliases.update({2 * k: 4 * k + 2, 2 * k + 1: 4 * k + 3})
    outs = _split_call(body, name, [HBM_SPEC] * (2 * n), out_specs + [pl.BlockSpec(memory_space=pltpu.VMEM)],
                       out_shape + [_sds((SUBLANES, LANES), F32)], aliases)(*ins)
    return [tuple(outs[4 * k:4 * k + 4]) for k in range(n)], outs[-1]


def _scatter_wait(name, started, after):
    n = len(started)

    def body(*refs):
        for k in range(n):
            send_sem, recv_sem, parts_ref, land_ref = refs[4 * k:4 * k + 4]
            _wait_seven(parts_ref, land_ref, send_sem, recv_sem)

    flat = [a for s in started for a in s]
    outs = _split_call(
        body, name, [SEM_SPEC, SEM_SPEC, HBM_SPEC, HBM_SPEC] * n + [pl.BlockSpec(memory_space=pl.ANY)],
        [HBM_SPEC, HBM_SPEC] * n, [pltpu.HBM(a.shape, a.dtype) for s in started for a in s[2:]],
        {4 * k + 2 + t: 2 * k + t for k in range(n) for t in range(2)})(*flat, after)
    return list(outs[0::2]), list(outs[1::2])


def _to_segments(a, tile):
    seg = tile // SUBLANES
    return a.reshape((a.shape[0] // tile, SUBLANES, seg) + a.shape[1:]).swapaxes(1, 2).reshape(a.shape)


def _from_segments(a, tile):
    seg = tile // SUBLANES
    return a.reshape((a.shape[0] // tile, seg, SUBLANES) + a.shape[1:]).swapaxes(1, 2).reshape(a.shape)


def _chunk(ref, q):
    return ref[q * SUBLANES:(q + 1) * SUBLANES, :]


def _fill_wrap_prev(x_ref, halo_ref, wrap_ref, n_wrap, n_halo, seg, keep):
    sub = lax.broadcasted_iota(jnp.int32, (SUBLANES, x_ref.shape[-1]), 0)
    for j in range(n_wrap):
        q = seg - n_wrap + j
        hq = q - (seg - n_halo)
        row = halo_ref[hq * SUBLANES + SUBLANES - 1:(hq + 1) * SUBLANES, :] * keep
        wrap_ref[j * SUBLANES:(j + 1) * SUBLANES, :] = jnp.where(sub == 0, row, pltpu.roll(_chunk(x_ref, q), 1, 0))


def _fill_wrap_next(x_ref, halo_ref, wrap_ref, n_wrap, keep):
    sub = lax.broadcasted_iota(jnp.int32, (SUBLANES, x_ref.shape[-1]), 0)
    for j in range(n_wrap):
        row = halo_ref[j * SUBLANES:j * SUBLANES + 1, :] * keep
        wrap_ref[j * SUBLANES:(j + 1) * SUBLANES, :] = jnp.where(
            sub == SUBLANES - 1, row, pltpu.roll(_chunk(x_ref, j), SUBLANES - 1, 0))


def _past(x_ref, wrap_ref, q, d, n_wrap):
    return _chunk(x_ref, q - d) if q >= d else _chunk(wrap_ref, q - d + n_wrap)


def _future(x_ref, wrap_ref, q, d, seg):
    return _chunk(x_ref, q + d) if q + d < seg else _chunk(wrap_ref, q + d - seg)


def _conv_fwd(x_ref, wrap_ref, w_ref, b_ref, out_ref, seg, k_taps):
    bias = jnp.broadcast_to(b_ref[...], (SUBLANES, x_ref.shape[-1]))
    for q in range(seg):
        acc = bias
        for k in range(k_taps):
            acc = acc + w_ref[k:k + 1, :] * _past(x_ref, wrap_ref, q, k_taps - 1 - k, k_taps - 1)
        out_ref[q * SUBLANES:(q + 1) * SUBLANES, :] = acc


def _conv_bwd_data(d_ref, wrap_ref, w_ref, out_ref, seg, k_taps):
    for q in range(seg):
        acc = None
        for k in range(k_taps):
            term = w_ref[k:k + 1, :] * _future(d_ref, wrap_ref, q, k_taps - 1 - k, seg)
            acc = term if acc is None else acc + term
        out_ref[q * SUBLANES:(q + 1) * SUBLANES, :] = acc


def _conv_bwd_taps(d_ref, x_ref, wrap_ref, dw_ref, seg, k_taps):
    for k in range(k_taps):
        part = None
        for q in range(seg):
            term = _chunk(d_ref, q) * _past(x_ref, wrap_ref, q, k_taps - 1 - k, k_taps - 1)
            part = term if part is None else part + term
        dw_ref[k:k + 1, :] += _colsum(part)


def _tile_halo_specs(tm, width_block, n_halo, n_tiles, block_of):
    rows = n_halo * SUBLANES
    per = tm // rows
    tile = pl.BlockSpec(width_block(tm), lambda n, i: block_of(n, i))
    prev = pl.BlockSpec(width_block(rows), lambda n, i: block_of(n, jnp.maximum(i * per - 1, 0)))
    nxt = pl.BlockSpec(width_block(rows), lambda n, i: block_of(n, jnp.minimum((i + 1) * per, n_tiles * per - 1)))
    return tile, prev, nxt


def _ln_stats(v):
    mu = jnp.mean(v, axis=-1, keepdims=True)
    vc = v - mu
    var = jnp.mean(vc * vc, axis=-1, keepdims=True)
    rstd = lax.rsqrt(var + LN_EPS)
    return vc * rstd, rstd


def _ln_backward(dxhat, xhat, rstd):
    m1 = jnp.mean(dxhat, axis=-1, keepdims=True)
    m2 = jnp.mean(dxhat * xhat, axis=-1, keepdims=True)
    return rstd * (dxhat - m1 - xhat * m2)


def _row_spec(tm, width):
    return pl.BlockSpec((tm, width), lambda i: (i, 0))


def _param_spec(l, width):
    return pl.BlockSpec((None, 1, width), lambda *_: (l, 0, 0))


def _ln_bwd_rows(dout, xh_ref, rs_ref, g_ref, dr_ref, dg_ref, db_ref, dsum_ref, first):
    xhat = xh_ref[...]
    dr = _ln_backward(dout * g_ref[...], xhat, rs_ref[...])
    dr_ref[...] = dr
    _acc_rows(dg_ref, _colsum(dout * xhat), first)
    _acc_rows(db_ref, _colsum(dout), first)
    _acc_rows(dsum_ref, _colsum(dr), first)


def _ln_bwd_specs(tm, d, l, row_of):
    vec = pl.BlockSpec((1, d), lambda *_: (0, 0))
    ins = [pl.BlockSpec((tm, d), row_of), pl.BlockSpec((tm, 1), row_of), _param_spec(l, d)]
    return ins, [pl.BlockSpec((tm, d), row_of), vec, vec, vec]


def _ln_res_bwd(name, dout, xhat, rstd, g3, l, deps=()):
    t, d = dout.shape
    tm = min(TM_ROW, t)

    def body(do_ref, xh_ref, rs_ref, g_ref, dr_ref, dg_ref, db_ref, dc_ref):
        _ln_bwd_rows(do_ref[...], xh_ref, rs_ref, g_ref, dr_ref, dg_ref, db_ref, dc_ref, pl.program_id(0) == 0)

    ins, outs = _ln_bwd_specs(tm, d, l, lambda i: (i, 0))
    return _call(body, name, (t // tm,), [_row_spec(tm, d)] + ins, outs,
                 [_sds((t, d), F32)] + [_sds((1, d), F32)] * 3, deps=deps)(dout, xhat, rstd, g3)


def _glu(name, h):
    t, c2 = h.shape
    c = c2 // 2
    tm = min(TM_ROW, t)

    def body(a_ref, g_ref, o_ref):
        o_ref[...] = a_ref[...] * _sigmoid(g_ref[...])

    return _call(body, name, (t // tm,),
                 [pl.BlockSpec((tm, c), lambda i: (i, 0)), pl.BlockSpec((tm, c), lambda i: (i, 1))],
                 _row_spec(tm, c), _sds((t, c), F32))(h, h)


def _glu_bwd(name, du, h):
    t, c2 = h.shape
    c = c2 // 2
    tm = min(TM_ROW, t)

    def body(du_ref, a_ref, g_ref, dh_ref, db_ref):
        first = pl.program_id(0) == 0
        du_v, a = du_ref[...], a_ref[...]
        sg = _sigmoid(g_ref[...])
        da = du_v * sg
        dg = du_v * a * sg * (1.0 - sg)
        dh_ref[:, :c] = da.astype(BF16)
        dh_ref[:, c:] = dg.astype(BF16)
        _acc_rows(db_ref.at[:, :c], _colsum(da), first)
        _acc_rows(db_ref.at[:, c:], _colsum(dg), first)

    return _call(body, name, (t // tm,),
                 [_row_spec(tm, c), pl.BlockSpec((tm, c), lambda i: (i, 0)), pl.BlockSpec((tm, c), lambda i: (i, 1))],
                 [_row_spec(tm, c2), pl.BlockSpec((1, c2), lambda i: (0, 0))],
                 [_sds((t, c2), BF16), _sds((1, c2), F32)])(du, h, h)


CONV_CB = 512
TAPS_PAD = 32


def _dwconv31(name, u, w3, b3, l, seq):
    t, c = u.shape
    tm, cb = TM_EW, CONV_CB
    seg, seq_tiles, n_tiles = tm // SUBLANES, seq // tm, t // tm
    n_wrap = CONV_K - 1
    tile, prev, _ = _tile_halo_specs(tm, lambda rows: (rows, cb), seg, n_tiles, lambda n, r: (r, n))

    def body(u_ref, halo_ref, w_ref, b_ref, o_ref, wrap_ref):
        keep = (pl.program_id(1) % seq_tiles != 0).astype(F32)
        _fill_wrap_prev(u_ref, halo_ref, wrap_ref, n_wrap, seg, seg, keep)
        _conv_fwd(u_ref, wrap_ref, w_ref, b_ref, o_ref, seg, CONV_K)

    return _call(body, name, (c // cb, n_tiles),
                 [tile, prev, pl.BlockSpec((None, CONV_K, cb), lambda n, i: (l, 0, n)),
                  pl.BlockSpec((None, 1, cb), lambda n, i: (l, 0, n))],
                 tile, _sds((t, c), F32), [pltpu.VMEM((n_wrap * SUBLANES, cb), F32)])(u, u, w3, b3)


def _dwconv31_bwd(name, dc, u, w3, l, seq):
    t, c = dc.shape
    tm, cb = TM_EW, CONV_CB
    seg, seq_tiles, n_tiles = tm // SUBLANES, seq // tm, t // tm
    n_wrap = CONV_K - 1
    tile, prev, nxt = _tile_halo_specs(tm, lambda rows: (rows, cb), seg, n_tiles, lambda n, r: (r, n))

    def body(dc_ref, dcn_ref, u_ref, up_ref, w_ref, du_ref, dw_ref, dwrap_ref, uwrap_ref):
        i = pl.program_id(1)
        keep_prev = (i % seq_tiles != 0).astype(F32)
        keep_next = (i % seq_tiles != seq_tiles - 1).astype(F32)
        _fill_wrap_next(dc_ref, dcn_ref, dwrap_ref, n_wrap, keep_next)
        _conv_bwd_data(dc_ref, dwrap_ref, w_ref, du_ref, seg, CONV_K)

        @pl.when(i == 0)
        def _():
            dw_ref[...] = jnp.zeros_like(dw_ref)

        _fill_wrap_prev(u_ref, up_ref, uwrap_ref, n_wrap, seg, seg, keep_prev)
        _conv_bwd_taps(dc_ref, u_ref, uwrap_ref, dw_ref, seg, CONV_K)

    wrap = pltpu.VMEM((n_wrap * SUBLANES, cb), F32)
    return _call(body, name, (c // cb, n_tiles),
                 [tile, nxt, tile, prev, pl.BlockSpec((None, CONV_K, cb), lambda n, i: (l, 0, n))],
                 [tile, pl.BlockSpec((TAPS_PAD, cb), lambda n, i: (0, n))],
                 [_sds((t, c), F32), _sds((TAPS_PAD, c), F32)], [wrap, wrap])(dc, dc, u, u, w3)


def _ln_silu(name, cx, g3, b3, l):
    t, d = cx.shape
    tm = min(TM_ROW, t)

    def body(c_ref, g_ref, b_ref, o_ref):
        xhat, _ = _ln_stats(c_ref[...])
        nv = xhat * g_ref[...] + b_ref[...]
        o_ref[...] = (nv * _sigmoid(nv)).astype(BF16)

    return _call(body, name, (t // tm,), [_row_spec(tm, d), _param_spec(l, d), _param_spec(l, d)],
                 _row_spec(tm, d), _sds((t, d), BF16))(cx, g3, b3)


def _ln_silu_bwd(name, ds, cx, g3, b3, l, deps=()):
    t, d = cx.shape
    tm = min(TM_ROW, t)

    def body(ds_ref, c_ref, g_ref, b_ref, dc_ref, dg_ref, db_ref, dsum_ref):
        first = pl.program_id(0) == 0
        xhat, rstd = _ln_stats(c_ref[...])
        g = g_ref[...]
        nv = xhat * g + b_ref[...]
        sg = _sigmoid(nv)
        dn = ds_ref[...] * (sg * (1.0 + nv * (1.0 - sg)))
        dc = _ln_backward(dn * g, xhat, rstd)
        dc_ref[...] = dc
        _acc_rows(dg_ref, _colsum(dn * xhat), first)
        _acc_rows(db_ref, _colsum(dn), first)
        _acc_rows(dsum_ref, _colsum(dc), first)

    vec = pl.BlockSpec((1, d), lambda i: (0, 0))
    return _call(body, name, (t // tm,),
                 [_row_spec(tm, d), _row_spec(tm, d), _param_spec(l, d), _param_spec(l, d)],
                 [_row_spec(tm, d), vec, vec, vec],
                 [_sds((t, d), F32)] + [_sds((1, d), F32)] * 3, deps=deps)(ds, cx, g3, b3)


FFN_HALO = FFN_K - 1


def _ffn_conv(x_ref, halo_ref, wrap_ref, w_ref, b_ref, keep, seg, out_ref):
    _fill_wrap_prev(x_ref, halo_ref, wrap_ref, FFN_K - 1, FFN_HALO, seg, keep)
    _conv_fwd(x_ref, wrap_ref, w_ref, b_ref, out_ref, seg, FFN_K)


TM_FFN = 512
WRAP_ROWS = FFN_HALO * SUBLANES


def _sub_tiles(x_ref, prev_ref, next_ref, keep_prev, keep_next, n_sub):
    out = []
    for s in range(n_sub):
        tile = x_ref.at[pl.ds(s * TM_EW, TM_EW)]
        prev = prev_ref if s == 0 else x_ref.at[pl.ds(s * TM_EW - WRAP_ROWS, WRAP_ROWS)]
        nxt = next_ref if s == n_sub - 1 else x_ref.at[pl.ds((s + 1) * TM_EW, WRAP_ROWS)]
        out.append((tile, prev, keep_prev if s == 0 else 1.0, nxt, keep_next if s == n_sub - 1 else 1.0))
    return out


def _rows(ref, s, rows):
    return ref.at[pl.ds(s * rows, rows)]


def _ffn_specs(tm, fb, n_tiles):
    return _tile_halo_specs(tm, lambda rows: (None, rows, fb), FFN_HALO, n_tiles, lambda n, r: (n, r, 0))


def _ffn_up_act(name, x, w_up, b_up4, wdw, bdw, l, seq):
    t, d = x.shape
    nb, fb, _ = w_up.shape
    half = nb // 2
    tm = min(TM_FFN, seq)
    n_sub, seg, seq_steps, n_steps = tm // TM_EW, TM_EW // SUBLANES, seq // tm, t // tm
    per = tm // WRAP_ROWS
    nt_dims = (_DIMS["nt"], ((), ()))

    def body(x_ref, xp_ref, ug_ref, uv_ref, bug_ref, buv_ref, wg_ref, wv_ref, bg_ref, bv_ref,
             hg_ref, hv_ref, a_ref, pg_ref, pv_ref, gwrap_ref, vwrap_ref, cg_ref, cv_ref):
        keep = (pl.program_id(1) % seq_steps != 0).astype(F32)
        xb, xpb = x_ref[...].astype(BF16), xp_ref[...].astype(BF16)
        hg_ref[...] = lax.dot_general(xb, ug_ref[...], nt_dims, preferred_element_type=F32) + bug_ref[...]
        pg_ref[...] = lax.dot_general(xpb, ug_ref[...], nt_dims, preferred_element_type=F32) + bug_ref[...]
        for s, (tile, prev, kp, _, _) in enumerate(_sub_tiles(hg_ref, pg_ref, None, keep, None, n_sub)):
            _ffn_conv(tile, prev, gwrap_ref, wg_ref, bg_ref, kp, seg, _rows(cg_ref, s, TM_EW))
        hv_ref[...] = lax.dot_general(xb, uv_ref[...], nt_dims, preferred_element_type=F32) + buv_ref[...]
        pv_ref[...] = lax.dot_general(xpb, uv_ref[...], nt_dims, preferred_element_type=F32) + buv_ref[...]
        for s, (tile, prev, kp, _, _) in enumerate(_sub_tiles(hv_ref, pv_ref, None, keep, None, n_sub)):
            _ffn_conv(tile, prev, vwrap_ref, wv_ref, bv_ref, kp, seg, _rows(cv_ref, s, TM_EW))
        cg = cg_ref[...]
        a_ref[...] = (cg * _sigmoid(cg) * cv_ref[...]).astype(BF16)

    def blk(shift):
        return pl.BlockSpec((None, fb, d), lambda n, i: (n + shift, 0, 0))

    def vec(shift, rows):
        return pl.BlockSpec((None, None, rows, fb), lambda n, i: (l, n + shift, 0, 0))

    out = pl.BlockSpec((None, tm, fb), lambda n, i: (n, i, 0))
    tmp = pltpu.VMEM((tm, fb), F32)
    halo = pltpu.VMEM((WRAP_ROWS, fb), F32)
    return _call(body, name, (half, n_steps),
                 [pl.BlockSpec((tm, d), lambda n, i: (i, 0)),
                  pl.BlockSpec((WRAP_ROWS, d), lambda n, i: (jnp.maximum(i * per - 1, 0), 0)),
                  blk(0), blk(half), vec(0, 1), vec(half, 1), vec(0, FFN_K), vec(half, FFN_K), vec(0, 1), vec(half, 1)],
                 [out, out, out],
                 [_sds((half, t, fb), F32), _sds((half, t, fb), F32), _sds((half, t, fb), BF16)],
                 [halo, halo, halo, halo, tmp, tmp])(x, x, w_up, w_up, b_up4, b_up4, wdw, wdw, bdw, bdw)


def _ffn_act_bwd(name, dy, w_down, hg, hv, wdw, bdw, l, seq, deps=()):
    half, t, fb = hg.shape
    d = dy.shape[-1]
    tm = min(TM_FFN, seq)
    n_sub, seg, seq_steps, n_steps = tm // TM_EW, TM_EW // SUBLANES, seq // tm, t // tm
    tile, prev, _ = _ffn_specs(tm, fb, n_steps)

    def body(dy_ref, wd_ref, g_ref, gp_ref, v_ref, vp_ref, wg_ref, wv_ref, bg_ref, bv_ref,
             dg_ref, dv_ref, dbg_ref, dbv_ref, dwg_ref, dwv_ref, gwrap_ref, vwrap_ref, cg_ref, cv_ref):
        i = pl.program_id(1)
        first = i == 0
        keep = (i % seq_steps != 0).astype(F32)
        da = lax.dot_general(dy_ref[...].astype(BF16), wd_ref[...], (_DIMS["nt"], ((), ())),
                             preferred_element_type=F32)
        g_tiles = _sub_tiles(g_ref, gp_ref, None, keep, None, n_sub)
        v_tiles = _sub_tiles(v_ref, vp_ref, None, keep, None, n_sub)
        for s in range(n_sub):
            _ffn_conv(g_tiles[s][0], g_tiles[s][1], _rows(gwrap_ref, s, WRAP_ROWS), wg_ref, bg_ref, g_tiles[s][2],
                      seg, _rows(cg_ref, s, TM_EW))
            _ffn_conv(v_tiles[s][0], v_tiles[s][1], _rows(vwrap_ref, s, WRAP_ROWS), wv_ref, bv_ref, v_tiles[s][2],
                      seg, _rows(cv_ref, s, TM_EW))
        cg, cv = cg_ref[...], cv_ref[...]
        sg = _sigmoid(cg)
        dcv = da * cg * sg
        dcg = da * cv * sg * (1.0 + cg * (1.0 - sg))
        dg_ref[...] = dcg
        dv_ref[...] = dcv
        _acc_rows(dbg_ref, _colsum(dcg), first)
        _acc_rows(dbv_ref, _colsum(dcv), first)

        @pl.when(first)
        def _():
            dwg_ref[...] = jnp.zeros_like(dwg_ref)
            dwv_ref[...] = jnp.zeros_like(dwv_ref)

        for s in range(n_sub):
            _conv_bwd_taps(_rows(dg_ref, s, TM_EW), g_tiles[s][0], _rows(gwrap_ref, s, WRAP_ROWS), dwg_ref, seg, FFN_K)
            _conv_bwd_taps(_rows(dv_ref, s, TM_EW), v_tiles[s][0], _rows(vwrap_ref, s, WRAP_ROWS), dwv_ref, seg, FFN_K)

    def vec(shift, rows):
        return pl.BlockSpec((None, None, rows, fb), lambda n, i: (l, n + shift, 0, 0))

    def acc(rows):
        return pl.BlockSpec((None, rows, fb), lambda n, i: (n, 0, 0))

    wrap = pltpu.VMEM((n_sub * WRAP_ROWS, fb), F32)
    tmp = pltpu.VMEM((tm, fb), F32)
    return _call(body, name, (half, n_steps),
                 [pl.BlockSpec((tm, d), lambda n, i: (i, 0)), pl.BlockSpec((None, fb, d), lambda n, i: (n, 0, 0)),
                  tile, prev, tile, prev, vec(0, FFN_K), vec(half, FFN_K), vec(0, 1), vec(half, 1)],
                 [tile, tile, acc(1), acc(1), acc(SUBLANES), acc(SUBLANES)],
                 [_sds((half, t, fb), F32), _sds((half, t, fb), F32), _sds((half, 1, fb), F32),
                  _sds((half, 1, fb), F32), _sds((half, SUBLANES, fb), F32), _sds((half, SUBLANES, fb), F32)],
                 [wrap, wrap, tmp, tmp], deps=deps)(dy, w_down, hg, hg, hv, hv, wdw, wdw, bdw, bdw)


def _ffn_conv_t_dx(name, dcg, dcv, wdw, w_up, res, xhat, rstd, g3, l, seq, deps=()):
    half, t, fb = dcg.shape
    nb, d = 2 * half, res.shape[-1]
    tm = min(TM_FFN, seq)
    n_sub, seg, seq_steps, n_steps = tm // TM_EW, TM_EW // SUBLANES, seq // tm, t // tm
    per = tm // WRAP_ROWS

    def body(g_ref, gn_ref, v_ref, vn_ref, w_ref, up_ref, res_ref, xh_ref, rs_ref, gam_ref,
             dh_ref, db_ref, dr_ref, dgam_ref, dbeta_ref, dsum_ref, wrap_ref, out_ref, acc_ref):
        i, n = pl.program_id(0), pl.program_id(1)
        keep = (i % seq_steps != seq_steps - 1).astype(F32)

        def conv_t(d_ref, dn_ref):
            for s, (sub, _, _, nx, kn) in enumerate(_sub_tiles(d_ref, None, dn_ref, None, keep, n_sub)):
                _fill_wrap_next(sub, nx, wrap_ref, FFN_K - 1, kn)
                _conv_bwd_data(sub, wrap_ref, w_ref, _rows(out_ref, s, TM_EW), seg, FFN_K)

        @pl.when(n < half)
        def _():
            conv_t(g_ref, gn_ref)

        @pl.when(n >= half)
        def _():
            conv_t(v_ref, vn_ref)

        dh = out_ref[...]
        dhb = dh.astype(BF16)
        dh_ref[...] = dhb
        _acc_rows(db_ref.at[n], _colsum(dh), i == 0)
        p = jnp.dot(dhb, up_ref[...], preferred_element_type=F32)

        @pl.when(n == 0)
        def _():
            acc_ref[...] = p

        @pl.when(n > 0)
        def _():
            acc_ref[...] += p

        @pl.when(n == nb - 1)
        def _():
            _ln_bwd_rows(acc_ref[...] + ALPHA * res_ref[...], xh_ref, rs_ref, gam_ref, dr_ref, dgam_ref, dbeta_ref,
                         dsum_ref, i == 0)

    def src(gate):
        def blk(n):
            return jnp.minimum(n, half - 1) if gate else jnp.maximum(n - half, 0)
        tile = pl.BlockSpec((None, tm, fb), lambda i, n: (blk(n), i, 0))
        nxt = pl.BlockSpec((None, WRAP_ROWS, fb),
                           lambda i, n: (blk(n), jnp.minimum((i + 1) * per, n_steps * per - 1), 0))
        return [tile, nxt]

    row = pl.BlockSpec((tm, d), lambda i, n: (i, 0))
    ln_ins, ln_outs = _ln_bwd_specs(tm, d, l, lambda i, n: (i, 0))
    tmp = pltpu.VMEM((tm, fb), F32)
    halo = pltpu.VMEM((WRAP_ROWS, fb), F32)
    return _call(body, name, (n_steps, nb),
                 src(True) + src(False) +
                 [pl.BlockSpec((None, None, FFN_K, fb), lambda i, n: (l, n, 0, 0)),
                  pl.BlockSpec((None, fb, d), lambda i, n: (n, 0, 0)), row] + ln_ins,
                 [pl.BlockSpec((None, tm, fb), lambda i, n: (n, i, 0)),
                  pl.BlockSpec((nb, 1, fb), lambda i, n: (0, 0, 0))] + ln_outs,
                 [_sds((nb, t, fb), BF16), _sds((nb, 1, fb), F32), _sds((t, d), F32)] + [_sds((1, d), F32)] * 3,
                 [halo, tmp, pltpu.VMEM((tm, d), F32)],
                 deps=deps)(dcg, dcg, dcv, dcv, wdw, w_up, res, xhat, rstd, g3)


def _gelu_parts(h):
    cdf = 0.5 * (1.0 + lax.erf(h * INV_SQRT2))
    return h * cdf, cdf


def _seg_axis(a, axis, fn):
    return jnp.moveaxis(fn(jnp.moveaxis(a, axis, 0), TM_EW), 0, axis)


def _sgu_operands(w_s, b_s):
    nl = w_s.shape[0]
    n_sub = TM_EW // CHUNK
    tril = jnp.tril(jnp.ones((CHUNK, CHUNK), dtype=bool))
    w_causal = jnp.where(tril, w_s, 0.0)
    w_tile = (jnp.eye(n_sub, dtype=F32)[None, None, :, None, :, None] * w_causal[:, :, None, :, None, :]).reshape(
        nl, GROUPS, TM_EW, TM_EW)
    w_tile = _seg_axis(_seg_axis(w_tile, 2, _to_segments), 3, _to_segments).astype(BF16)
    bs_tile = jnp.broadcast_to(b_s[:, :, None, :, None], (nl, GROUPS, n_sub, CHUNK, CHUNK)).reshape(
        nl, GROUPS, TM_EW, CHUNK)
    return w_tile, _seg_axis(bs_tile, 2, _to_segments)


def _sgu_param_grads(dwt, dbt):
    n_sub = TM_EW // CHUNK
    tril = jnp.tril(jnp.ones((CHUNK, CHUNK), dtype=bool))
    dwt = _seg_axis(_seg_axis(dwt, 1, _from_segments), 2, _from_segments).reshape(GROUPS, n_sub, CHUNK, n_sub, CHUNK)
    dw = sum(dwt[:, a, :, a, :] for a in range(n_sub))
    db = _seg_axis(dbt, 1, _from_segments).reshape(GROUPS, n_sub, CHUNK).sum(axis=1)
    return jnp.where(tril, dw, 0.0), db


def _sgu(name, h, g3, b3, wt, bst, l):
    t, c2 = h.shape
    c = c2 // 2
    tm = TM_EW

    def body(h_ref, g_ref, b_ref, wt_ref, bs_ref, o_ref):
        z, _ = _gelu_parts(h_ref[...])
        u = z[:, :c]
        xhat, _ = _ln_stats(z[:, c:])
        vnb = (xhat * g_ref[...] + b_ref[...]).astype(BF16)
        for gi in range(GROUPS):
            cs = slice(gi * CHUNK, (gi + 1) * CHUNK)
            sp = jnp.dot(wt_ref[gi], vnb[:, cs], preferred_element_type=F32) + bs_ref[gi]
            o_ref[:, cs] = (u[:, cs] * sp).astype(BF16)

    return _call(body, name, (t // tm,),
                 [_row_spec(tm, c2), _param_spec(l, c), _param_spec(l, c),
                  pl.BlockSpec((None, GROUPS, tm, tm), lambda i: (l, 0, 0, 0)),
                  pl.BlockSpec((None, GROUPS, tm, CHUNK), lambda i: (l, 0, 0, 0))],
                 _row_spec(tm, c), _sds((t, c), BF16))(h, g3, b3, wt, bst)


def _sgu_bwd(name, dq, h, g3, b3, wt, bst, l, deps=()):
    t, c2 = h.shape
    c = c2 // 2
    tm = TM_EW
    n_tiles = t // tm

    def body(dq_ref, h_ref, g_ref, b_ref, wt_ref, bs_ref,
             dh_ref, dbin_ref, dw_ref, dbs_ref, dg_ref, db_ref, du_ref, dvn_ref, bsum_ref):
        i = pl.program_id(0)
        first = i == 0
        hv = h_ref[...]
        z, cdf = _gelu_parts(hv)
        u = z[:, :c]
        xhat, rstd = _ln_stats(z[:, c:])
        g = g_ref[...]
        vnb = (xhat * g + b_ref[...]).astype(BF16)

        @pl.when(first)
        def _():
            dw_ref[...] = jnp.zeros_like(dw_ref)
            bsum_ref[...] = jnp.zeros_like(bsum_ref)

        for gi in range(GROUPS):
            cs = slice(gi * CHUNK, (gi + 1) * CHUNK)
            vb = vnb[:, cs]
            w = wt_ref[gi]
            sp = jnp.dot(w, vb, preferred_element_type=F32) + bs_ref[gi]
            dqb = dq_ref[:, cs]
            du_ref[:, cs] = dqb * sp
            dsp = dqb * u[:, cs]
            bsum_ref[gi] += dsp
            dspb = dsp.astype(BF16)
            dw_ref[gi] += lax.dot_general(dspb, vb, (_DIMS["nt"], ((), ())), preferred_element_type=F32)
            dvn_ref[:, cs] = lax.dot_general(w, dspb, (_DIMS["tn"], ((), ())), preferred_element_type=F32)

        dvn = dvn_ref[...]
        dv = _ln_backward(dvn * g, xhat, rstd)
        pdf = jnp.exp(-0.5 * hv * hv) * INV_SQRT2PI
        dgelu = cdf + hv * pdf
        dhu = du_ref[...] * dgelu[:, :c]
        dhv = dv * dgelu[:, c:]
        dh_ref[:, :c] = dhu.astype(BF16)
        dh_ref[:, c:] = dhv.astype(BF16)
        _acc_rows(dbin_ref.at[:, :c], _colsum(dhu), first)
        _acc_rows(dbin_ref.at[:, c:], _colsum(dhv), first)
        _acc_rows(dg_ref, _colsum(dvn * xhat), first)
        _acc_rows(db_ref, _colsum(dvn), first)

        @pl.when(i == n_tiles - 1)
        def _():
            dbs_ref[...] = jnp.sum(bsum_ref[...], axis=-1)

    vec = pl.BlockSpec((1, c), lambda i: (0, 0))
    return _call(body, name, (n_tiles,),
                 [_row_spec(tm, c), _row_spec(tm, c2), _param_spec(l, c), _param_spec(l, c),
                  pl.BlockSpec((None, GROUPS, tm, tm), lambda i: (l, 0, 0, 0)),
                  pl.BlockSpec((None, GROUPS, tm, CHUNK), lambda i: (l, 0, 0, 0))],
                 [_row_spec(tm, c2), pl.BlockSpec((1, c2), lambda i: (0, 0)),
                  pl.BlockSpec((GROUPS, tm, tm), lambda i: (0, 0, 0)),
                  pl.BlockSpec((GROUPS, tm), lambda i: (0, 0)), vec, vec],
                 [_sds((t, c2), BF16), _sds((1, c2), F32), _sds((GROUPS, tm, tm), F32),
                  _sds((GROUPS, tm), F32), _sds((1, c), F32), _sds((1, c), F32)],
                 [pltpu.VMEM((tm, c), F32), pltpu.VMEM((tm, c), F32), pltpu.VMEM((GROUPS, tm, CHUNK), F32)],
                 deps=deps)(dq, h, g3, b3, wt, bst)


def _loss(name, y, target):
    t, d = y.shape
    tm = min(TM_ROW, t)
    n_tiles = t // tm

    def body(y_ref, t_ref, l_ref, dy_ref, acc_ref):
        i = pl.program_id(0)
        diff = y_ref[...] - t_ref[...]
        dy_ref[...] = diff * (1.0 / d)
        _acc_rows(acc_ref, _colsum(diff * diff), i == 0)

        @pl.when(i == n_tiles - 1)
        def _():
            l_ref[...] = jnp.broadcast_to(jnp.sum(acc_ref[...], axis=-1, keepdims=True) * (0.5 / d), (1, LANES))

    return _call(body, name, (n_tiles,), [_row_spec(tm, d), _row_spec(tm, d)],
                 [pl.BlockSpec((1, LANES), lambda i: (0, 0)), _row_spec(tm, d)],
                 [_sds((1, LANES), F32), _sds((t, d), F32)], [pltpu.VMEM((1, d), F32)])(y, target)


def _adamw(g, w, m, v):
    m2 = ADAM_B1 * m + (1.0 - ADAM_B1) * g
    v2 = ADAM_B2 * v + (1.0 - ADAM_B2) * (g * g)
    m_hat = m2 / (1.0 - ADAM_B1 ** ADAM_STEP)
    v_hat = v2 / (1.0 - ADAM_B2 ** ADAM_STEP)
    delta = -ADAM_LR * (m_hat / (jnp.sqrt(v_hat) + ADAM_EPS) + ADAM_WD * w)
    return delta, m2, v2


ROW_TILE_CAP = 512


def _row_tile(rows, cap=ROW_TILE_CAP):
    if rows <= cap:
        return rows
    for tr in range(cap, 15, -16):
        if rows % tr == 0:
            return tr
    return rows


def _sum8_adamw(name, dev, lands, parts, w, m, v):
    nl = len(lands)
    _, r, c = lands[0].shape
    tr = _row_tile(r, cap=128)

    def body(dev_ref, *refs):
        land, own = refs[:nl], refs[nl:2 * nl]
        w_ref, m_ref, v_ref, g_ref, d_ref, m2_ref, v2_ref = refs[2 * nl:]
        layer, me = pl.program_id(0), dev_ref[0]
        for l in range(nl):
            @pl.when(layer == l)
            def _(l=l):
                g = None
                for s in range(N_DEV):
                    part = jnp.where(me == s, own[l][...], land[l][s]).astype(F32)
                    g = part if g is None else g + part
                delta, m2, v2 = _adamw(g, w_ref[...], m_ref[...], v_ref[...])
                g_ref[...] = g
                d_ref[...] = delta
                m2_ref[...] = m2
                v2_ref[...] = v2

    def rows_of(l, a, i):
        return jnp.where(a == l, i, 0)

    spec = pl.BlockSpec((None, tr, c), lambda a, i, dev_ref: (a, i, 0))
    in_specs = [pl.BlockSpec((N_DEV, tr, c), lambda a, i, dev_ref, l=l: (0, rows_of(l, a, i), 0)) for l in range(nl)]
    in_specs += [pl.BlockSpec((None, tr, c), lambda a, i, dev_ref, l=l: (dev_ref[0], rows_of(l, a, i), 0))
                 for l in range(nl)]
    grid_spec = pltpu.PrefetchScalarGridSpec(
        num_scalar_prefetch=1, grid=(nl, r // tr), in_specs=in_specs + [spec] * 3, out_specs=[spec] * 4)
    return pl.pallas_call(
        body, name=name, grid_spec=grid_spec, out_shape=[_sds(w.shape, F32)] * 4,
        compiler_params=pltpu.CompilerParams(vmem_limit_bytes=VMEM_LIMIT))(dev, *lands, *parts, w, m, v)


def _sum8(name, parts):
    _, r, c = parts.shape
    tr = _row_tile(r)

    def body(p_ref, o_ref):
        acc = p_ref[0]
        for s in range(1, N_DEV):
            acc = acc + p_ref[s]
        o_ref[...] = acc

    return _call(body, name, (r // tr,), [pl.BlockSpec((N_DEV, tr, c), lambda i: (0, i, 0))],
                 pl.BlockSpec((tr, c), lambda i: (i, 0)), _sds((r, c), F32))(parts)


def _adamw_small(name, gs, ws, ms, vs):
    n = len(gs)

    def body(*refs):
        g, w, m, v = (refs[k * n:(k + 1) * n] for k in range(4))
        d_out, m_out, v_out = (refs[(4 + k) * n:(5 + k) * n] for k in range(3))
        for k in range(n):
            d_out[k][...], m_out[k][...], v_out[k][...] = _adamw(g[k][...], w[k][...], m[k][...], v[k][...])

    vmem = pl.BlockSpec(memory_space=pltpu.VMEM)
    outs = _call(body, name, (), [vmem] * (4 * n), [vmem] * (3 * n), [_sds(w.shape, F32) for w in ws] * 3)(
        *gs, *ws, *ms, *vs)
    return outs[:n], outs[n:2 * n], outs[2 * n:]


def _pack(arrs, row_multiple=SUBLANES):
    pieces, rows = [], 0
    for a in arrs:
        piece = a.reshape(-1, LANES)
        piece = jnp.pad(piece, ((0, (-piece.shape[0]) % SUBLANES), (0, 0)))
        pieces.append(piece)
        rows += piece.shape[0]
    if rows % row_multiple:
        pieces.append(jnp.zeros(((-rows) % row_multiple, LANES), pieces[0].dtype))
    return jnp.concatenate(pieces, axis=0)


def _unpack(buf, shapes, lead=0):
    out, pos = [], 0
    for shp in shapes:
        rows = math.prod(shp) // LANES
        piece = lax.slice_in_dim(buf, pos, pos + rows, axis=lead)
        out.append(piece.reshape(buf.shape[:lead] + tuple(shp)))
        pos += rows + (-rows) % SUBLANES
    return out


REPLICATED = ["conv_b_in", "conv_b_dw", "conv_ln_g", "conv_ln_b", "conv_b_out", "gmlp_w_s", "gmlp_b_s",
              "ffn_b_up", "ffn_b_dw", "ffn_b_down", "norm1_g", "norm1_b", "norm2_g", "norm2_b"]
SMALL_SHARDED = ["conv_w_dw", "gmlp_b_in", "gmlp_ln_g", "gmlp_ln_b", "gmlp_b_out", "ffn_w_dw"]
BIG = ["conv_w_in", "conv_w_out", "gmlp_w_in", "gmlp_w_out", "ffn_w_up", "ffn_w_down"]
WEIGHTS = ["conv_w_in", "conv_b_in", "conv_w_dw", "conv_b_dw", "conv_ln_g", "conv_ln_b", "conv_w_out", "conv_b_out",
           "gmlp_w_in", "gmlp_b_in", "gmlp_ln_g", "gmlp_ln_b", "gmlp_w_s", "gmlp_b_s", "gmlp_w_out", "gmlp_b_out",
           "ffn_w_up", "ffn_b_up", "ffn_w_dw", "ffn_b_dw", "ffn_w_down", "ffn_b_down",
           "norm1_g", "norm1_b", "norm2_g", "norm2_b"]


def _from_shards(g, lead_shape):
    nd = len(lead_shape)
    perm = tuple(range(1, nd + 1)) + (0, nd + 1)
    return g.transpose(perm).reshape(tuple(lead_shape) + (-1,))


def _to_shards(full, width):
    lead = full.shape[:-1]
    nd = len(lead)
    parts = full.reshape(lead + (N_DEV, width))
    return parts.transpose((nd,) + tuple(range(nd)) + (nd + 1,))


def _step(p):
    x_in, target_in = p["x"], p["loss_target"]
    bsz, seq, d = x_in.shape
    t = bsz * seq
    assert seq % TM_EW == 0 and TM_EW % CHUNK == 0 and TM_EW // SUBLANES >= CONV_K - 1
    x0 = _to_segments(x_in.reshape(t, d), TM_EW)
    target = _to_segments(target_in.reshape(t, d), TM_EW)
    n_conv, n_gmlp = p["conv_w_in"].shape[0], p["gmlp_w_in"].shape[0]
    fb = p["ffn_w_up"].shape[-1]
    nblk = N_DEV
    half = nblk // 2
    cw = p["conv_w_in"].shape[-1]
    tm = min(TM_MM, t)
    nt = t // tm
    tk = min(TK_DW, t)
    ntk = t // tk
    dev = 4 * lax.axis_index("x") + 2 * lax.axis_index("y") + lax.axis_index("c")

    small_shapes = [p[n].shape for n in SMALL_SHARDED]
    small_src = _pack([p[n] for n in SMALL_SHARDED])[None]
    small_all = _all_gather("gather_small_weights", [small_src])[0][0]
    sm = _unpack(small_all, small_shapes, lead=1)
    w_src = []
    for i in range(DEPTH):
        mix = "conv" if i % 2 == 0 else "gmlp"
        w_src += [p[mix + "_w_in"][i // 2].astype(BF16), p[mix + "_w_out"][i // 2].astype(BF16),
                  p["ffn_w_up"][i].T.astype(BF16), p["ffn_w_down"][i].astype(BF16)]
    send_sems, recv_sems, w_land, _ = _gather_start(
        "weights_gather_start", _place_own("weights_place_own", w_src, deps=[small_all]))
    W_IN, W_OUT, W_UP, W_DOWN = range(4)

    def wait_weight(i, k, after):
        return _gather_wait(f"l{i}_weights_wait{k}", w_land[4 * i + k], send_sems, recv_sems, 4 * i + k, after)
    conv_w_dw = _from_shards(sm[0], sm[0].shape[1:-1])
    gmlp_b_in = _from_shards(sm[1], sm[1].shape[1:-1])
    gmlp_ln_g = _from_shards(sm[2], sm[2].shape[1:-1])
    gmlp_ln_b = _from_shards(sm[3], sm[3].shape[1:-1])
    gmlp_b_out = _from_shards(sm[4], sm[4].shape[1:-1])
    ffn_w_dw = sm[5].transpose(1, 0, 2, 3)

    def rows3(a):
        return a.reshape(a.shape[0], 1, a.shape[-1])

    conv_b_in4 = p["conv_b_in"].reshape(n_conv, N_DEV, 1, cw)
    gmlp_b_in4 = gmlp_b_in.reshape(n_gmlp, N_DEV, 1, cw)
    ffn_b_up4 = p["ffn_b_up"].reshape(DEPTH, nblk, 1, fb)
    ffn_b_dw4 = p["ffn_b_dw"].reshape(DEPTH, nblk, 1, fb)
    conv_b_dw3, conv_ln_g3, conv_ln_b3 = rows3(p["conv_b_dw"]), rows3(p["conv_ln_g"]), rows3(p["conv_ln_b"])
    conv_b_out3, gmlp_b_out3, ffn_b_down3 = rows3(p["conv_b_out"]), rows3(gmlp_b_out), rows3(p["ffn_b_down"])
    gmlp_ln_g3, gmlp_ln_b3 = rows3(gmlp_ln_g), rows3(gmlp_ln_b)
    n1g3, n1b3, n2g3, n2b3 = rows3(p["norm1_g"]), rows3(p["norm1_b"]), rows3(p["norm2_g"]), rows3(p["norm2_b"])
    w_tile, bs_tile = _sgu_operands(p["gmlp_w_s"], p["gmlp_b_s"])

    def mm_in(name, xa, wg, l, bias4):
        return _matmul(name, xa, wg, "nn", grid=(nt, N_DEV),
                       a_spec=pl.BlockSpec((tm, d), lambda i, n: (i, 0)),
                       b_spec=pl.BlockSpec((None, d, cw), lambda i, n: (n, 0, 0)),
                       o_spec=pl.BlockSpec((tm, cw), lambda i, n: (i, n)), o_shape=(t, N_DEV * cw), o_dtype=F32,
                       bias=bias4, bias_spec=pl.BlockSpec((None, None, 1, cw), lambda i, n: (l, n, 0, 0)))

    def mm_out_dx(name, dy, w, deps=()):
        return _matmul(name, dy, w, "nt", grid=(nt,),
                       a_spec=pl.BlockSpec((tm, d), lambda i: (i, 0)),
                       b_spec=pl.BlockSpec((d, d), lambda i: (0, 0)),
                       o_spec=pl.BlockSpec((tm, d), lambda i: (i, 0)), o_shape=(t, d), o_dtype=F32, deps=deps)

    def mm_out_dw(name, sa, dy):
        return _matmul(name, sa, dy, "tn", grid=(nt,), k_axis=0, nk=nt, acc_shape=(d, d),
                       a_spec=pl.BlockSpec((tm, d), lambda k: (k, 0)),
                       b_spec=pl.BlockSpec((tm, d), lambda k: (k, 0)),
                       o_spec=pl.BlockSpec((d, d), lambda k: (0, 0)), o_shape=(d, d), o_dtype=BF16)

    def mm_in_dx(name, dh, wg, res):
        tmx = min(TM_LN, t)

        def body(a_ref, b_ref, res_ref, o_ref):
            y = ALPHA * res_ref[...]
            for n in range(N_DEV):
                y = y + lax.dot_general(a_ref[:, n * cw:(n + 1) * cw], b_ref[n], (_DIMS["nt"], ((), ())),
                                        preferred_element_type=F32)
            o_ref[...] = y

        row = pl.BlockSpec((tmx, d), lambda i: (i, 0))
        return _call(body, name, (t // tmx,),
                     [pl.BlockSpec((tmx, N_DEV * cw), lambda i: (i, 0)),
                      pl.BlockSpec((N_DEV, d, cw), lambda i: (0, 0, 0)), row], row, _sds((t, d), F32))(dh, wg, res)

    def mm_in_dw(name, xa, dh):
        return _matmul(name, xa, dh, "tn", grid=(N_DEV, nt), k_axis=1, nk=nt, acc_shape=(d, cw),
                       a_spec=pl.BlockSpec((tm, d), lambda n, k: (k, 0)),
                       b_spec=pl.BlockSpec((tm, cw), lambda n, k: (k, n)),
                       o_spec=pl.BlockSpec((None, d, cw), lambda n, k: (n, 0, 0)),
                       o_shape=(N_DEV, d, cw), o_dtype=BF16)

    def mm_down_dw(name, a, dy, deps=()):
        return _matmul(name, a, dy, "tn", grid=(half, ntk), k_axis=1, nk=ntk, acc_shape=(fb, d),
                       a_spec=pl.BlockSpec((None, tk, fb), lambda n, k: (n, k, 0)),
                       b_spec=pl.BlockSpec((tk, d), lambda n, k: (k, 0)),
                       o_spec=pl.BlockSpec((None, fb, d), lambda n, k: (n, 0, 0)),
                       o_shape=(half, fb, d), o_dtype=BF16, deps=deps)

    def mm_up_dw(name, xa, dh):
        return _matmul(name, dh, xa, "tn", grid=(nblk, ntk), k_axis=1, nk=ntk, acc_shape=(fb, d),
                       a_spec=pl.BlockSpec((None, tk, fb), lambda n, k: (n, k, 0)),
                       b_spec=pl.BlockSpec((tk, d), lambda n, k: (k, 0)),
                       o_spec=pl.BlockSpec((None, fb, d), lambda n, k: (n, 0, 0)),
                       o_shape=(nblk, fb, d), o_dtype=BF16)

    saved = []
    xcur = x0
    for i in range(DEPTH):
        j = i // 2
        s = {"x": xcur}
        s["w_in"] = wait_weight(i, W_IN, xcur)
        if i % 2 == 0:
            s["h"] = mm_in(f"l{i}_conv_in", xcur, s["w_in"], j, conv_b_in4)
            s["u"] = _glu(f"l{i}_glu", s["h"])
            s["c"] = _dwconv31(f"l{i}_dwconv", s["u"], conv_w_dw, conv_b_dw3, j, seq)
            s["s"] = _ln_silu(f"l{i}_ln_silu", s["c"], conv_ln_g3, conv_ln_b3, j)
            b_out3 = conv_b_out3
        else:
            s["h"] = mm_in(f"l{i}_gmlp_in", xcur, s["w_in"], j, gmlp_b_in4)
            s["s"] = _sgu(f"l{i}_sgu", s["h"], gmlp_ln_g3, gmlp_ln_b3, w_tile, bs_tile, j)
            b_out3 = gmlp_b_out3
        s["w_out"] = wait_weight(i, W_OUT, s["s"]).reshape(d, d)
        s["x1"], s["xhat1"], s["rstd1"] = _matmul_ln(
            f"l{i}_mixer_out_norm1", s["s"], s["w_out"], xcur, b_out3, n1g3, n1b3, j, i)
        s["w_up"] = wait_weight(i, W_UP, s["x1"])
        s["hg"], s["hv"], s["a"] = _ffn_up_act(f"l{i}_ffn_up_act", s["x1"], s["w_up"], ffn_b_up4, ffn_w_dw, ffn_b_dw4,
                                               i, seq)
        s["w_down"] = wait_weight(i, W_DOWN, s["a"]).reshape(half, fb, d)
        xcur, s["xhat2"], s["rstd2"] = _matmul_ln(
            f"l{i}_ffn_down_norm2", s["a"], s["w_down"], s["x1"], ffn_b_down3, n2g3, n2b3, i, i)
        saved.append(s)

    loss_row, dx = _loss("loss", xcur, target)

    started = {n: [None] * p[n].shape[0] for n in BIG}
    tokens = []

    def send_grads(name, items):
        done, token = _scatter_start(name, [g for _, _, g in items])
        for (n, l, _), st in zip(items, done):
            started[n][l] = st
        tokens.append(token)

    def take_tokens():
        out = list(tokens)
        tokens.clear()
        return out

    gl = {n: [None] * p[n].shape[0] for n in REPLICATED + SMALL_SHARDED}
    dr2, gl["norm2_g"][DEPTH - 1], gl["norm2_b"][DEPTH - 1], gl["ffn_b_down"][DEPTH - 1] = _ln_res_bwd(
        f"l{DEPTH - 1}_norm2_bwd", dx, saved[-1]["xhat2"], saved[-1]["rstd2"], n2g3, DEPTH - 1)
    for i in reversed(range(DEPTH)):
        j = i // 2
        s = saved[i]
        mix = "conv" if i % 2 == 0 else "gmlp"
        g_down = mm_down_dw(f"l{i}_ffn_down_dw", s["a"], dr2, deps=take_tokens()).reshape(N_DEV, -1, d)
        send_grads(f"l{i}_ffn_down_grad_scatter_start", [("ffn_w_down", i, g_down)])
        dcg, dcv, dbg, dbv, dwg, dwv = _ffn_act_bwd(f"l{i}_ffn_act_bwd", dr2, s["w_down"], s["hg"], s["hv"],
                                                    ffn_w_dw, ffn_b_dw4, i, seq, deps=take_tokens())
        gl["ffn_b_dw"][i] = jnp.concatenate([dbg, dbv], axis=0).reshape(1, nblk * fb)
        gl["ffn_w_dw"][i] = jnp.concatenate([dwg[:, :FFN_K], dwv[:, :FFN_K]], axis=0)
        dh, dbu, dr1, gl["norm1_g"][i], gl["norm1_b"][i], gl[mix + "_b_out"][j] = _ffn_conv_t_dx(
            f"l{i}_ffn_conv_t_dx", dcg, dcv, ffn_w_dw, s["w_up"], dr2, s["xhat1"], s["rstd1"], n1g3, i, seq,
            deps=take_tokens())
        gl["ffn_b_up"][i] = dbu.reshape(1, nblk * fb)
        send_grads(f"l{i}_ffn_up_grad_scatter_start", [("ffn_w_up", i, mm_up_dw(f"l{i}_ffn_up_dw", s["x1"], dh))])
        ds = mm_out_dx(f"l{i}_{mix}_out_dx", dr1, s["w_out"], deps=take_tokens())
        g_out = mm_out_dw(f"l{i}_{mix}_out_dw", s["s"], dr1).reshape(N_DEV, -1, d)
        if i % 2 == 0:
            dc, gl["conv_ln_g"][j], gl["conv_ln_b"][j], gl["conv_b_dw"][j] = _ln_silu_bwd(
                f"l{i}_ln_silu_bwd", ds, s["c"], conv_ln_g3, conv_ln_b3, j)
            du, dwdw = _dwconv31_bwd(f"l{i}_dwconv_bwd", dc, s["u"], conv_w_dw, j, seq)
            gl["conv_w_dw"][j] = dwdw[:CONV_K]
            dh, gl["conv_b_in"][j] = _glu_bwd(f"l{i}_glu_bwd", du, s["h"])
        else:
            dh, gl["gmlp_b_in"][j], dwt, dbt, gl["gmlp_ln_g"][j], gl["gmlp_ln_b"][j] = _sgu_bwd(
                f"l{i}_sgu_bwd", ds, s["h"], gmlp_ln_g3, gmlp_ln_b3, w_tile, bs_tile, j)
            gl["gmlp_w_s"][j], gl["gmlp_b_s"][j] = _sgu_param_grads(dwt, dbt)
        dx = mm_in_dx(f"l{i}_{mix}_in_dx", dh, s["w_in"], dr1)
        if i > 0:
            prev = saved[i - 1]
            dr2, gl["norm2_g"][i - 1], gl["norm2_b"][i - 1], gl["ffn_b_down"][i - 1] = _ln_res_bwd(
                f"l{i - 1}_norm2_bwd", dx, prev["xhat2"], prev["rstd2"], n2g3, i - 1)
        send_grads(f"l{i}_mixer_grads_scatter_start",
                   [(mix + "_w_out", j, g_out), (mix + "_w_in", j, mm_in_dw(f"l{i}_{mix}_in_dw", s["x"], dh))])
    grad_x = _from_segments(dx, TM_EW).reshape(bsz, seq, d)

    full_small = {n: jnp.stack(gl[n]).reshape(p[n].shape) for n in REPLICATED}
    shard_small = {}
    for n in SMALL_SHARDED:
        if n == "ffn_w_dw":
            shard_small[n] = jnp.stack(gl[n]).transpose(1, 0, 2, 3)
        else:
            width = p[n].shape[-1]
            lead = p[n].shape[:-1]
            shard_small[n] = _to_shards(jnp.stack(gl[n]).reshape(lead + (N_DEV * width,)), width)
    flat_shapes = [(1, LANES)] + [p[n].shape for n in REPLICATED] + [(N_DEV,) + p[n].shape for n in SMALL_SHARDED]
    flat_local = _pack([loss_row] + [full_small[n] for n in REPLICATED] + [shard_small[n] for n in SMALL_SHARDED],
                       row_multiple=ROW_TILE_CAP)

    small_send, small_recv, small_land, small_token = _gather_start(
        "small_grads_gather_start", _place_own("small_grads_place_own", [flat_local]))

    grads, delta, new_m, new_v = {}, {}, {}, {}
    dev1 = jnp.reshape(dev, (1,)).astype(jnp.int32)
    order = ["ffn_w_down", "ffn_w_up", "gmlp_w_out", "gmlp_w_in", "conv_w_out", "conv_w_in"]
    after = small_token
    for n in order:
        parts_done, lands_done = _scatter_wait(f"grads_{n}_scatter_wait", started[n], after)
        state = [p[n], p["m_" + n], p["v_" + n]]
        if n == "ffn_w_up":
            state = [a.transpose(0, 2, 1) for a in state]
        outs = _sum8_adamw(f"adamw_{n}", dev1, lands_done, parts_done, *state)
        after = outs[-1]
        if n == "ffn_w_up":
            outs = [a.transpose(0, 2, 1) for a in outs]
        grads[n], delta[n], new_m[n], new_v[n] = outs

    small_parts = _gather_wait("small_grads_gather_wait", small_land[0], small_send, small_recv, 0, after)
    summed = _unpack(_sum8("sum_small_grads", small_parts), flat_shapes)
    loss = summed[0][0, 0]
    grads.update(zip(REPLICATED, summed[1:1 + len(REPLICATED)]))
    for n, g in zip(SMALL_SHARDED, summed[1 + len(REPLICATED):]):
        grads[n] = lax.dynamic_index_in_dim(g, dev, axis=0, keepdims=False)
    small = REPLICATED + SMALL_SHARDED
    d_s, m_s, v_s = _adamw_small("adamw_small", [grads[n] for n in small], [p[n] for n in small],
                                 [p["m_" + n] for n in small], [p["v_" + n] for n in small])
    for n, dd, mm, vv in zip(small, d_s, m_s, v_s):
        delta[n], new_m[n], new_v[n] = dd, mm, vv

    return (loss, grad_x, *[grads[n] for n in WEIGHTS], *[delta[n] for n in WEIGHTS],
            *[new_m[n] for n in WEIGHTS], *[new_v[n] for n in WEIGHTS])


def kernel(x, conv_w_in, conv_b_in, conv_w_dw, conv_b_dw, conv_ln_g, conv_ln_b, conv_w_out, conv_b_out, gmlp_w_in, gmlp_b_in, gmlp_ln_g, gmlp_ln_b, gmlp_w_s, gmlp_b_s, gmlp_w_out, gmlp_b_out, ffn_w_up, ffn_b_up, ffn_w_dw, ffn_b_dw, ffn_w_down, ffn_b_down, norm1_g, norm1_b, norm2_g, norm2_b, loss_target, m_conv_w_in, m_conv_b_in, m_conv_w_dw, m_conv_b_dw, m_conv_ln_g, m_conv_ln_b, m_conv_w_out, m_conv_b_out, m_gmlp_w_in, m_gmlp_b_in, m_gmlp_ln_g, m_gmlp_ln_b, m_gmlp_w_s, m_gmlp_b_s, m_gmlp_w_out, m_gmlp_b_out, m_ffn_w_up, m_ffn_b_up, m_ffn_w_dw, m_ffn_b_dw, m_ffn_w_down, m_ffn_b_down, m_norm1_g, m_norm1_b, m_norm2_g, m_norm2_b, v_conv_w_in, v_conv_b_in, v_conv_w_dw, v_conv_b_dw, v_conv_ln_g, v_conv_ln_b, v_conv_w_out, v_conv_b_out, v_gmlp_w_in, v_gmlp_b_in, v_gmlp_ln_g, v_gmlp_ln_b, v_gmlp_w_s, v_gmlp_b_s, v_gmlp_w_out, v_gmlp_b_out, v_ffn_w_up, v_ffn_b_up, v_ffn_w_dw, v_ffn_b_dw, v_ffn_w_down, v_ffn_b_down, v_norm1_g, v_norm1_b, v_norm2_g, v_norm2_b):
    return _step(dict(locals()))
```

```python
import math

import jax
import jax.numpy as jnp
from jax import lax
from jax.experimental import pallas as pl
from jax.experimental.pallas import tpu as pltpu

F32 = jnp.float32
BF16 = jnp.bfloat16
MESH = pl.DeviceIdType.MESH

N_DEV = 8
DEPTH = 4
ALPHA = (2.0 * DEPTH) ** 0.25
LN_EPS = 1e-5
CONV_K = 31
FFN_K = 3
CHUNK = 128
GROUPS = 8
ADAM_LR = 0.001
ADAM_B1 = 0.9
ADAM_B2 = 0.999
ADAM_EPS = 1e-08
ADAM_WD = 0.01
ADAM_STEP = 10
INV_SQRT2 = 1.0 / math.sqrt(2.0)
INV_SQRT2PI = 1.0 / math.sqrt(2.0 * math.pi)

LANES = 128
SUBLANES = 8
VMEM_LIMIT = 48 * 1024 * 1024
TM_MM = 1024
TK_DW = 2048
TM_EW = 256
TM_ROW = 512


def _call(body, name, grid, in_specs, out_specs, out_shape, scratch=(), aliases=None, deps=()):
    deps = list(deps)
    in_specs = list(in_specs)
    n_in = len(in_specs)
    if deps:
        inner = body

        def body(*refs):
            return inner(*refs[:n_in], *refs[n_in + len(deps):])

        in_specs = in_specs + [pl.BlockSpec(memory_space=pl.ANY)] * len(deps)
    fn = pl.pallas_call(
        body, name=name, grid=grid, in_specs=in_specs, out_specs=out_specs, out_shape=out_shape,
        scratch_shapes=list(scratch), input_output_aliases=aliases or {},
        compiler_params=pltpu.CompilerParams(vmem_limit_bytes=VMEM_LIMIT))
    return lambda *args: fn(*args, *deps)


def _sds(shape, dtype):
    return jax.ShapeDtypeStruct(tuple(shape), dtype)


def _sigmoid(x):
    return 1.0 / (1.0 + jnp.exp(-x))


def _acc_rows(ref, val, first):
    @pl.when(first)
    def _():
        ref[...] = val

    @pl.when(jnp.logical_not(first))
    def _():
        ref[...] += val


def _colsum(v):
    return jnp.sum(v, axis=0, keepdims=True)


_DIMS = {"nn": ((1,), (0,)), "nt": ((1,), (1,)), "tn": ((0,), (0,))}


def _matmul(name, a, b, mode, *, grid, a_spec, b_spec, o_spec, o_shape, o_dtype, k_axis=None, nk=1,
            acc_shape=None, bias=None, bias_spec=None, res=None, res_spec=None, res_scale=1.0, deps=()):
    dims = (_DIMS[mode], ((), ()))
    has_bias, has_res = bias is not None, res is not None

    def body(*refs):
        a_ref, b_ref = refs[0], refs[1]
        pos = 2
        bias_ref = res_ref = None
        if has_bias:
            bias_ref = refs[pos]
            pos += 1
        if has_res:
            res_ref = refs[pos]
            pos += 1
        o_ref = refs[pos]
        acc_ref = refs[pos + 1] if nk > 1 else None
        p = lax.dot_general(a_ref[...].astype(BF16), b_ref[...].astype(BF16), dims, preferred_element_type=F32)

        def finish(acc):
            if has_bias:
                acc = acc + bias_ref[...]
            if has_res:
                acc = acc + res_scale * res_ref[...]
            o_ref[...] = acc.astype(o_dtype)

        if nk == 1:
            finish(p)
        else:
            k = pl.program_id(k_axis)

            @pl.when(k == 0)
            def _():
                acc_ref[...] = p

            @pl.when(k > 0)
            def _():
                acc_ref[...] += p

            @pl.when(k == nk - 1)
            def _():
                finish(acc_ref[...])

    ins, specs = [a, b], [a_spec, b_spec]
    if has_bias:
        ins.append(bias)
        specs.append(bias_spec)
    if has_res:
        ins.append(res)
        specs.append(res_spec)
    scratch = [pltpu.VMEM(acc_shape, F32)] if nk > 1 else []
    return _call(body, name, grid, specs, o_spec, _sds(o_shape, o_dtype), scratch, deps=deps)(*ins)


TM_LN = 512


def _matmul_ln(name, a, b, x_res, bias3, g3, b3, l_bias, l_norm):
    t, d = x_res.shape
    tm = min(TM_LN, t)
    blocked = a.ndim == 3

    def body(a_ref, b_ref, x_ref, bias_ref, g_ref, be_ref, o_ref, xh_ref, rs_ref):
        if blocked:
            y = None
            for k in range(a.shape[0]):
                p = jnp.dot(a_ref[k], b_ref[k], preferred_element_type=F32)
                y = p if y is None else y + p
        else:
            y = jnp.dot(a_ref[...], b_ref[...], preferred_element_type=F32)
        xhat, rstd = _ln_stats(ALPHA * x_ref[...] + y + bias_ref[...])
        o_ref[...] = xhat * g_ref[...] + be_ref[...]
        xh_ref[...] = xhat
        rs_ref[...] = rstd

    if blocked:
        a_spec = pl.BlockSpec((a.shape[0], tm, a.shape[2]), lambda i: (0, i, 0))
        b_spec = pl.BlockSpec(b.shape, lambda i: (0, 0, 0))
    else:
        a_spec = pl.BlockSpec((tm, a.shape[1]), lambda i: (i, 0))
        b_spec = pl.BlockSpec(b.shape, lambda i: (0, 0))
    row = pl.BlockSpec((tm, d), lambda i: (i, 0))
    stat = pl.BlockSpec((tm, 1), lambda i: (i, 0))

    def vec(l):
        return pl.BlockSpec((None, 1, d), lambda i: (l, 0, 0))

    return _call(body, name, (t // tm,), [a_spec, b_spec, row, vec(l_bias), vec(l_norm), vec(l_norm)],
                 [row, row, stat], [_sds((t, d), F32), _sds((t, d), F32), _sds((t, 1), F32)])(
                     a, b, x_res, bias3, g3, b3)


def _mesh_pos():
    return lax.axis_index("x"), lax.axis_index("y"), lax.axis_index("c")


def _any_specs(n):
    return [pl.BlockSpec(memory_space=pl.ANY)] * n


def _all_gather(name, srcs):
    n = len(srcs)

    def body(*refs):
        src, out = refs[:n], refs[n:2 * n]
        send_sems, recv_sems, local_sems = refs[2 * n:]
        x, y, c = _mesh_pos()
        me, sibling = (x, y, c), (x, y, 1 - c)
        chips = [(1 - x, y), (x, 1 - y), (1 - x, 1 - y)]

        def slot(k, p):
            return out[k].at[:, 4 * p[0] + 2 * p[1] + p[2]]

        def copy(k, idx, block, to, s=None):
            return pltpu.make_async_remote_copy(
                src_ref=slot(k, block) if s is None else s, dst_ref=slot(k, block),
                send_sem=send_sems.at[k * 7 + idx], recv_sem=recv_sems.at[k * 7 + idx],
                device_id=to, device_id_type=MESH)

        local = [pltpu.make_async_copy(src[k], slot(k, me), local_sems.at[k]) for k in range(n)]
        for cp in local:
            cp.start()
        first = []
        for k in range(n):
            first.append(copy(k, 0, me, sibling, src[k]))
            for j, chip in enumerate(chips):
                first.append(copy(k, 1 + j, me, (*chip, c), src[k]))
        for cp in first:
            cp.start()
        passed = []
        for j, chip in enumerate(chips):
            for k in range(n):
                copy(k, 1 + j, (*chip, c), me).wait_recv()
                cp = copy(k, 4 + j, (*chip, c), sibling)
                cp.start()
                passed.append(cp)
        for k in range(n):
            copy(k, 0, sibling, me).wait_recv()
            for j, chip in enumerate(chips):
                copy(k, 4 + j, (*chip, 1 - c), me).wait_recv()
        for cp in first + passed:
            cp.wait_send()
        for cp in local:
            cp.wait()

    out_shape = [_sds((s.shape[0], N_DEV) + s.shape[1:], s.dtype) for s in srcs]
    return _call(body, name, (), [pl.BlockSpec(memory_space=pltpu.VMEM)] * n, _any_specs(n), out_shape,
                 [pltpu.SemaphoreType.DMA((7 * n,)), pltpu.SemaphoreType.DMA((7 * n,)),
                  pltpu.SemaphoreType.DMA((n,))])(*srcs)


HBM_SPEC = pl.BlockSpec(memory_space=pltpu.HBM)
SEM_SPEC = pl.BlockSpec(memory_space=pltpu.SEMAPHORE)
N_PEER = N_DEV - 1


def _split_call(body, name, in_specs, out_specs, out_shape, aliases):
    return pl.pallas_call(
        body, name=name, in_specs=in_specs, out_specs=out_specs, out_shape=out_shape, input_output_aliases=aliases,
        compiler_params=pltpu.CompilerParams(has_side_effects=pltpu.SideEffectType.DATAFLOW_SIDE_EFFECTING))


def _peers(x, y, c):
    return [(1 - x if q & 4 else x, 1 - y if q & 2 else y, 1 - c if q & 1 else c) for q in range(1, N_DEV)]


def _in_hbm(a):
    return pltpu.with_memory_space_constraint(a, pltpu.HBM)


def _place_own(name, srcs, deps=()):
    n = len(srcs)

    def body(*refs):
        src, out, sems = refs[:n], refs[n:2 * n], refs[2 * n]
        x, y, c = _mesh_pos()
        dev = 4 * x + 2 * y + c
        copies = [pltpu.make_async_copy(src[k], out[k].at[dev], sems.at[k]) for k in range(n)]
        for cp in copies:
            cp.start()
        for cp in copies:
            cp.wait()

    return _call(body, name, (), [pl.BlockSpec(memory_space=pltpu.VMEM)] * n, _any_specs(n),
                 [_sds((N_DEV,) + s.shape, s.dtype) for s in srcs], [pltpu.SemaphoreType.DMA((n,))],
                 deps=deps)(*srcs)


def _gather_start(name, lands):
    n = len(lands)

    def body(*refs):
        land, send_sems, recv_sems = refs[:n], refs[n], refs[n + 1]
        x, y, c = _mesh_pos()
        dev = 4 * x + 2 * y + c
        for k in range(n):
            for peer in _peers(x, y, c):
                pltpu.make_async_remote_copy(
                    src_ref=land[k].at[dev], dst_ref=land[k].at[dev], send_sem=send_sems.at[k],
                    recv_sem=recv_sems.at[k], device_id=peer, device_id_type=MESH).start()
        token = refs[-1]
        token[...] = jnp.zeros_like(token)

    outs = _split_call(
        body, name, [HBM_SPEC] * n, [SEM_SPEC, SEM_SPEC] + [HBM_SPEC] * n + [pl.BlockSpec(memory_space=pltpu.VMEM)],
        [pltpu.SemaphoreType.DMA((n,)), pltpu.SemaphoreType.DMA((n,))] + [pltpu.HBM(a.shape, a.dtype) for a in lands]
        + [_sds((SUBLANES, LANES), F32)],
        {k: 2 + k for k in range(n)})(*[_in_hbm(a) for a in lands])
    return outs[0], outs[1], list(outs[2:2 + n]), outs[-1]


def _wait_seven(src_ref, dst_ref, send_sem, recv_sem):
    cp = pltpu.make_async_remote_copy(
        src_ref=src_ref.at[pl.ds(0, N_PEER)], dst_ref=dst_ref.at[pl.ds(0, N_PEER)], send_sem=send_sem,
        recv_sem=recv_sem, device_id=_mesh_pos(), device_id_type=MESH)
    cp.wait_send()
    cp.wait_recv()


def _gather_wait(name, land, send_sems, recv_sems, k, after):
    def body(land_ref, send_ref, recv_ref, after_ref, out_ref):
        _wait_seven(land_ref, land_ref, send_ref.at[k], recv_ref.at[k])

    return _split_call(body, name, [HBM_SPEC, SEM_SPEC, SEM_SPEC, pl.BlockSpec(memory_space=pl.ANY)], HBM_SPEC,
                       pltpu.HBM(land.shape, land.dtype), {0: 0})(land, send_sems, recv_sems, after)


def _scatter_start(name, parts_list):
    n = len(parts_list)

    def body(*refs):
        x, y, c = _mesh_pos()
        dev = 4 * x + 2 * y + c
        for k in range(n):
            parts_ref, land_ref = refs[2 * k], refs[2 * k + 1]
            send_sem, recv_sem = refs[2 * n + 4 * k], refs[2 * n + 4 * k + 1]
            for peer in _peers(x, y, c):
                pltpu.make_async_remote_copy(
                    src_ref=parts_ref.at[4 * peer[0] + 2 * peer[1] + peer[2]], dst_ref=land_ref.at[dev],
                    send_sem=send_sem, recv_sem=recv_sem, device_id=peer, device_id_type=MESH).start()
        token = refs[-1]
        token[...] = jnp.zeros_like(token)

    ins, out_specs, out_shape, aliases = [], [], [], {}
    for k, parts in enumerate(parts_list):
        buf = pltpu.HBM(parts.shape, parts.dtype)
        ins += [_in_hbm(parts), _in_hbm(lax.empty(parts.shape, parts.dtype))]
        out_specs += [SEM_SPEC, SEM_SPEC, HBM_SPEC, HBM_SPEC]
        out_shape += [pltpu.SemaphoreType.DMA(()), pltpu.SemaphoreType.DMA(()), buf, buf]
        aliases.update({2 * k: 4 * k + 2, 2 * k + 1: 4 * k + 3})
    outs = _split_call(body, name, [HBM_SPEC] * (2 * n), out_specs + [pl.BlockSpec(memory_space=pltpu.VMEM)],
                       out_shape + [_sds((SUBLANES, LANES), F32)], aliases)(*ins)
    return [tuple(outs[4 * k:4 * k + 4]) for k in range(n)], outs[-1]


def _scatter_wait(name, started, after):
    n = len(started)

    def body(*refs):
        for k in range(n):
            send_sem, recv_sem, parts_ref, land_ref = refs[4 * k:4 * k + 4]
            _wait_seven(parts_ref, land_ref, send_sem, recv_sem)

    flat = [a for s in started for a in s]
    outs = _split_call(
        body, name, [SEM_SPEC, SEM_SPEC, HBM_SPEC, HBM_SPEC] * n + [pl.BlockSpec(memory_space=pl.ANY)],
        [HBM_SPEC, HBM_SPEC] * n, [pltpu.HBM(a.shape, a.dtype) for s in started for a in s[2:]],
        {4 * k + 2 + t: 2 * k + t for k in range(n) for t in range(2)})(*flat, after)
    return list(outs[0::2]), list(outs[1::2])


def _to_segments(a, tile):
    seg = tile // SUBLANES
    return a.reshape((a.shape[0] // tile, SUBLANES, seg) + a.shape[1:]).swapaxes(1, 2).reshape(a.shape)


def _from_segments(a, tile):
    seg = tile // SUBLANES
    return a.reshape((a.shape[0] // tile, seg, SUBLANES) + a.shape[1:]).swapaxes(1, 2).reshape(a.shape)


def _chunk(ref, q):
    return ref[q * SUBLANES:(q + 1) * SUBLANES, :]


def _fill_wrap_prev(x_ref, halo_ref, wrap_ref, n_wrap, n_halo, seg, keep):
    sub = lax.broadcasted_iota(jnp.int32, (SUBLANES, x_ref.shape[-1]), 0)
    for j in range(n_wrap):
        q = seg - n_wrap + j
        hq = q - (seg - n_halo)
        row = halo_ref[hq * SUBLANES + SUBLANES - 1:(hq + 1) * SUBLANES, :] * keep
        wrap_ref[j * SUBLANES:(j + 1) * SUBLANES, :] = jnp.where(sub == 0, row, pltpu.roll(_chunk(x_ref, q), 1, 0))


def _fill_wrap_next(x_ref, halo_ref, wrap_ref, n_wrap, keep):
    sub = lax.broadcasted_iota(jnp.int32, (SUBLANES, x_ref.shape[-1]), 0)
    for j in range(n_wrap):
        row = halo_ref[j * SUBLANES:j * SUBLANES + 1, :] * keep
        wrap_ref[j * SUBLANES:(j + 1) * SUBLANES, :] = jnp.where(
            sub == SUBLANES - 1, row, pltpu.roll(_chunk(x_ref, j), SUBLANES - 1, 0))


def _past(x_ref, wrap_ref, q, d, n_wrap):
    return _chunk(x_ref, q - d) if q >= d else _chunk(wrap_ref, q - d + n_wrap)


def _future(x_ref, wrap_ref, q, d, seg):
    return _chunk(x_ref, q + d) if q + d < seg else _chunk(wrap_ref, q + d - seg)


def _conv_fwd(x_ref, wrap_ref, w_ref, b_ref, out_ref, seg, k_taps):
    bias = jnp.broadcast_to(b_ref[...], (SUBLANES, x_ref.shape[-1]))
    for q in range(seg):
        acc = bias
        for k in range(k_taps):
            acc = acc + w_ref[k:k + 1, :] * _past(x_ref, wrap_ref, q, k_taps - 1 - k, k_taps - 1)
        out_ref[q * SUBLANES:(q + 1) * SUBLANES, :] = acc


def _conv_bwd_data(d_ref, wrap_ref, w_ref, out_ref, seg, k_taps):
    for q in range(seg):
        acc = None
        for k in range(k_taps):
            term = w_ref[k:k + 1, :] * _future(d_ref, wrap_ref, q, k_taps - 1 - k, seg)
            acc = term if acc is None else acc + term
        out_ref[q * SUBLANES:(q + 1) * SUBLANES, :] = acc


def _conv_bwd_taps(d_ref, x_ref, wrap_ref, dw_ref, seg, k_taps):
    for k in range(k_taps):
        part = None
        for q in range(seg):
            term = _chunk(d_ref, q) * _past(x_ref, wrap_ref, q, k_taps - 1 - k, k_taps - 1)
            part = term if part is None else part + term
        dw_ref[k:k + 1, :] += _colsum(part)


def _tile_halo_specs(tm, width_block, n_halo, n_tiles, block_of):
    rows = n_halo * SUBLANES
    per = tm // rows
    tile = pl.BlockSpec(width_block(tm), lambda n, i: block_of(n, i))
    prev = pl.BlockSpec(width_block(rows), lambda n, i: block_of(n, jnp.maximum(i * per - 1, 0)))
    nxt = pl.BlockSpec(width_block(rows), lambda n, i: block_of(n, jnp.minimum((i + 1) * per, n_tiles * per - 1)))
    return tile, prev, nxt


def _ln_stats(v):
    mu = jnp.mean(v, axis=-1, keepdims=True)
    vc = v - mu
    var = jnp.mean(vc * vc, axis=-1, keepdims=True)
    rstd = lax.rsqrt(var + LN_EPS)
    return vc * rstd, rstd


def _ln_backward(dxhat, xhat, rstd):
    m1 = jnp.mean(dxhat, axis=-1, keepdims=True)
    m2 = jnp.mean(dxhat * xhat, axis=-1, keepdims=True)
    return rstd * (dxhat - m1 - xhat * m2)


def _row_spec(tm, width):
    return pl.BlockSpec((tm, width), lambda i: (i, 0))


def _param_spec(l, width):
    return pl.BlockSpec((None, 1, width), lambda *_: (l, 0, 0))


def _ln_bwd_rows(dout, xh_ref, rs_ref, g_ref, dr_ref, dg_ref, db_ref, dsum_ref, first):
    xhat = xh_ref[...]
    dr = _ln_backward(dout * g_ref[...], xhat, rs_ref[...])
    dr_ref[...] = dr
    _acc_rows(dg_ref, _colsum(dout * xhat), first)
    _acc_rows(db_ref, _colsum(dout), first)
    _acc_rows(dsum_ref, _colsum(dr), first)


def _ln_bwd_specs(tm, d, l, row_of):
    vec = pl.BlockSpec((1, d), lambda *_: (0, 0))
    ins = [pl.BlockSpec((tm, d), row_of), pl.BlockSpec((tm, 1), row_of), _param_spec(l, d)]
    return ins, [pl.BlockSpec((tm, d), row_of), vec, vec, vec]


def _ln_res_bwd(name, dout, xhat, rstd, g3, l, deps=()):
    t, d = dout.shape
    tm = min(TM_ROW, t)

    def body(do_ref, xh_ref, rs_ref, g_ref, dr_ref, dg_ref, db_ref, dc_ref):
        _ln_bwd_rows(do_ref[...], xh_ref, rs_ref, g_ref, dr_ref, dg_ref, db_ref, dc_ref, pl.program_id(0) == 0)

    ins, outs = _ln_bwd_specs(tm, d, l, lambda i: (i, 0))
    return _call(body, name, (t // tm,), [_row_spec(tm, d)] + ins, outs,
                 [_sds((t, d), F32)] + [_sds((1, d), F32)] * 3, deps=deps)(dout, xhat, rstd, g3)


def _glu_bwd(name, du, h):
    t, c2 = h.shape
    c = c2 // 2
    tm = min(TM_ROW, t)

    def body(du_ref, a_ref, g_ref, dh_ref, db_ref):
        first = pl.program_id(0) == 0
        du_v, a = du_ref[...], a_ref[...]
        sg = _sigmoid(g_ref[...])
        da = du_v * sg
        dg = du_v * a * sg * (1.0 - sg)
        dh_ref[:, :c] = da.astype(BF16)
        dh_ref[:, c:] = dg.astype(BF16)
        _acc_rows(db_ref.at[:, :c], _colsum(da), first)
        _acc_rows(db_ref.at[:, c:], _colsum(dg), first)

    return _call(body, name, (t // tm,),
                 [_row_spec(tm, c), pl.BlockSpec((tm, c), lambda i: (i, 0)), pl.BlockSpec((tm, c), lambda i: (i, 1))],
                 [_row_spec(tm, c2), pl.BlockSpec((1, c2), lambda i: (0, 0))],
                 [_sds((t, c2), BF16), _sds((1, c2), F32)])(du, h, h)


CONV_CB = 512
TAPS_PAD = 32


def _dwconv31(name, u, w3, b3, l, seq):
    t, c = u.shape
    tm, cb = TM_EW, CONV_CB
    seg, seq_tiles, n_tiles = tm // SUBLANES, seq // tm, t // tm
    n_wrap = CONV_K - 1
    tile, prev, _ = _tile_halo_specs(tm, lambda rows: (rows, cb), seg, n_tiles, lambda n, r: (r, n))

    def body(u_ref, halo_ref, w_ref, b_ref, o_ref, wrap_ref):
        keep = (pl.program_id(1) % seq_tiles != 0).astype(F32)
        _fill_wrap_prev(u_ref, halo_ref, wrap_ref, n_wrap, seg, seg, keep)
        _conv_fwd(u_ref, wrap_ref, w_ref, b_ref, o_ref, seg, CONV_K)

    return _call(body, name, (c // cb, n_tiles),
                 [tile, prev, pl.BlockSpec((None, CONV_K, cb), lambda n, i: (l, 0, n)),
                  pl.BlockSpec((None, 1, cb), lambda n, i: (l, 0, n))],
                 tile, _sds((t, c), F32), [pltpu.VMEM((n_wrap * SUBLANES, cb), F32)])(u, u, w3, b3)


def _dwconv31_bwd(name, dc, u, w3, l, seq):
    t, c = dc.shape
    tm, cb = TM_EW, CONV_CB
    seg, seq_tiles, n_tiles = tm // SUBLANES, seq // tm, t // tm
    n_wrap = CONV_K - 1
    tile, prev, nxt = _tile_halo_specs(tm, lambda rows: (rows, cb), seg, n_tiles, lambda n, r: (r, n))

    def body(dc_ref, dcn_ref, u_ref, up_ref, w_ref, du_ref, dw_ref, dwrap_ref, uwrap_ref):
        i = pl.program_id(1)
        keep_prev = (i % seq_tiles != 0).astype(F32)
        keep_next = (i % seq_tiles != seq_tiles - 1).astype(F32)
        _fill_wrap_next(dc_ref, dcn_ref, dwrap_ref, n_wrap, keep_next)
        _conv_bwd_data(dc_ref, dwrap_ref, w_ref, du_ref, seg, CONV_K)

        @pl.when(i == 0)
        def _():
            dw_ref[...] = jnp.zeros_like(dw_ref)

        _fill_wrap_prev(u_ref, up_ref, uwrap_ref, n_wrap, seg, seg, keep_prev)
        _conv_bwd_taps(dc_ref, u_ref, uwrap_ref, dw_ref, seg, CONV_K)

    wrap = pltpu.VMEM((n_wrap * SUBLANES, cb), F32)
    return _call(body, name, (c // cb, n_tiles),
                 [tile, nxt, tile, prev, pl.BlockSpec((None, CONV_K, cb), lambda n, i: (l, 0, n))],
                 [tile, pl.BlockSpec((TAPS_PAD, cb), lambda n, i: (0, n))],
                 [_sds((t, c), F32), _sds((TAPS_PAD, c), F32)], [wrap, wrap])(dc, dc, u, u, w3)


def _ln_silu(name, cx, g3, b3, l):
    t, d = cx.shape
    tm = min(TM_ROW, t)

    def body(c_ref, g_ref, b_ref, o_ref):
        xhat, _ = _ln_stats(c_ref[...])
        nv = xhat * g_ref[...] + b_ref[...]
        o_ref[...] = (nv * _sigmoid(nv)).astype(BF16)

    return _call(body, name, (t // tm,), [_row_spec(tm, d), _param_spec(l, d), _param_spec(l, d)],
                 _row_spec(tm, d), _sds((t, d), BF16))(cx, g3, b3)


def _ln_silu_bwd(name, ds, cx, g3, b3, l, deps=()):
    t, d = cx.shape
    tm = min(TM_ROW, t)

    def body(ds_ref, c_ref, g_ref, b_ref, dc_ref, dg_ref, db_ref, dsum_ref):
        first = pl.program_id(0) == 0
        xhat, rstd = _ln_stats(c_ref[...])
        g = g_ref[...]
        nv = xhat * g + b_ref[...]
        sg = _sigmoid(nv)
        dn = ds_ref[...] * (sg * (1.0 + nv * (1.0 - sg)))
        dc = _ln_backward(dn * g, xhat, rstd)
        dc_ref[...] = dc
        _acc_rows(dg_ref, _colsum(dn * xhat), first)
        _acc_rows(db_ref, _colsum(dn), first)
        _acc_rows(dsum_ref, _colsum(dc), first)

    vec = pl.BlockSpec((1, d), lambda i: (0, 0))
    return _call(body, name, (t // tm,),
                 [_row_spec(tm, d), _row_spec(tm, d), _param_spec(l, d), _param_spec(l, d)],
                 [_row_spec(tm, d), vec, vec, vec],
                 [_sds((t, d), F32)] + [_sds((1, d), F32)] * 3, deps=deps)(ds, cx, g3, b3)


FFN_HALO = FFN_K - 1


def _ffn_conv(x_ref, halo_ref, wrap_ref, w_ref, b_ref, keep, seg, out_ref):
    _fill_wrap_prev(x_ref, halo_ref, wrap_ref, FFN_K - 1, FFN_HALO, seg, keep)
    _conv_fwd(x_ref, wrap_ref, w_ref, b_ref, out_ref, seg, FFN_K)


TM_FFN = 512
WRAP_ROWS = FFN_HALO * SUBLANES


def _sub_tiles(x_ref, prev_ref, next_ref, keep_prev, keep_next, n_sub):
    out = []
    for s in range(n_sub):
        tile = x_ref.at[pl.ds(s * TM_EW, TM_EW)]
        prev = prev_ref if s == 0 else x_ref.at[pl.ds(s * TM_EW - WRAP_ROWS, WRAP_ROWS)]
        nxt = next_ref if s == n_sub - 1 else x_ref.at[pl.ds((s + 1) * TM_EW, WRAP_ROWS)]
        out.append((tile, prev, keep_prev if s == 0 else 1.0, nxt, keep_next if s == n_sub - 1 else 1.0))
    return out


def _rows(ref, s, rows):
    return ref.at[pl.ds(s * rows, rows)]


def _ffn_specs(tm, fb, n_tiles):
    return _tile_halo_specs(tm, lambda rows: (None, rows, fb), FFN_HALO, n_tiles, lambda n, r: (n, r, 0))


def _ffn_up_act(name, x, w_up, b_up4, wdw, bdw, l, seq):
    t, d = x.shape
    nb, fb, _ = w_up.shape
    half = nb // 2
    tm = min(TM_FFN, seq)
    n_sub, seg, seq_steps, n_steps = tm // TM_EW, TM_EW // SUBLANES, seq // tm, t // tm
    per = tm // WRAP_ROWS
    nt_dims = (_DIMS["nt"], ((), ()))

    def body(x_ref, xp_ref, ug_ref, uv_ref, bug_ref, buv_ref, wg_ref, wv_ref, bg_ref, bv_ref,
             hg_ref, hv_ref, a_ref, pg_ref, pv_ref, gwrap_ref, vwrap_ref, cg_ref, cv_ref):
        keep = (pl.program_id(1) % seq_steps != 0).astype(F32)
        xb, xpb = x_ref[...].astype(BF16), xp_ref[...].astype(BF16)
        hg_ref[...] = lax.dot_general(xb, ug_ref[...], nt_dims, preferred_element_type=F32) + bug_ref[...]
        pg_ref[...] = lax.dot_general(xpb, ug_ref[...], nt_dims, preferred_element_type=F32) + bug_ref[...]
        for s, (tile, prev, kp, _, _) in enumerate(_sub_tiles(hg_ref, pg_ref, None, keep, None, n_sub)):
            _ffn_conv(tile, prev, gwrap_ref, wg_ref, bg_ref, kp, seg, _rows(cg_ref, s, TM_EW))
        hv_ref[...] = lax.dot_general(xb, uv_ref[...], nt_dims, preferred_element_type=F32) + buv_ref[...]
        pv_ref[...] = lax.dot_general(xpb, uv_ref[...], nt_dims, preferred_element_type=F32) + buv_ref[...]
        for s, (tile, prev, kp, _, _) in enumerate(_sub_tiles(hv_ref, pv_ref, None, keep, None, n_sub)):
            _ffn_conv(tile, prev, vwrap_ref, wv_ref, bv_ref, kp, seg, _rows(cv_ref, s, TM_EW))
        cg = cg_ref[...]
        a_ref[...] = (cg * _sigmoid(cg) * cv_ref[...]).astype(BF16)

    def blk(shift):
        return pl.BlockSpec((None, fb, d), lambda n, i: (n + shift, 0, 0))

    def vec(shift, rows):
        return pl.BlockSpec((None, None, rows, fb), lambda n, i: (l, n + shift, 0, 0))

    out = pl.BlockSpec((None, tm, fb), lambda n, i: (n, i, 0))
    tmp = pltpu.VMEM((tm, fb), F32)
    halo = pltpu.VMEM((WRAP_ROWS, fb), F32)
    return _call(body, name, (half, n_steps),
                 [pl.BlockSpec((tm, d), lambda n, i: (i, 0)),
                  pl.BlockSpec((WRAP_ROWS, d), lambda n, i: (jnp.maximum(i * per - 1, 0), 0)),
                  blk(0), blk(half), vec(0, 1), vec(half, 1), vec(0, FFN_K), vec(half, FFN_K), vec(0, 1), vec(half, 1)],
                 [out, out, out],
                 [_sds((half, t, fb), F32), _sds((half, t, fb), F32), _sds((half, t, fb), BF16)],
                 [halo, halo, halo, halo, tmp, tmp])(x, x, w_up, w_up, b_up4, b_up4, wdw, wdw, bdw, bdw)


def _ffn_act_bwd(name, dy, w_down, hg, hv, wdw, bdw, l, seq, deps=()):
    half, t, fb = hg.shape
    d = dy.shape[-1]
    tm = min(TM_FFN, seq)
    n_sub, seg, seq_steps, n_steps = tm // TM_EW, TM_EW // SUBLANES, seq // tm, t // tm
    tile, prev, _ = _ffn_specs(tm, fb, n_steps)

    def body(dy_ref, wd_ref, g_ref, gp_ref, v_ref, vp_ref, wg_ref, wv_ref, bg_ref, bv_ref,
             dg_ref, dv_ref, dbg_ref, dbv_ref, dwg_ref, dwv_ref, gwrap_ref, vwrap_ref, cg_ref, cv_ref):
        i = pl.program_id(1)
        first = i == 0
        keep = (i % seq_steps != 0).astype(F32)
        da = lax.dot_general(dy_ref[...].astype(BF16), wd_ref[...], (_DIMS["nt"], ((), ())),
                             preferred_element_type=F32)
        g_tiles = _sub_tiles(g_ref, gp_ref, None, keep, None, n_sub)
        v_tiles = _sub_tiles(v_ref, vp_ref, None, keep, None, n_sub)
        for s in range(n_sub):
            _ffn_conv(g_tiles[s][0], g_tiles[s][1], _rows(gwrap_ref, s, WRAP_ROWS), wg_ref, bg_ref, g_tiles[s][2],
                      seg, _rows(cg_ref, s, TM_EW))
            _ffn_conv(v_tiles[s][0], v_tiles[s][1], _rows(vwrap_ref, s, WRAP_ROWS), wv_ref, bv_ref, v_tiles[s][2],
                      seg, _rows(cv_ref, s, TM_EW))
        cg, cv = cg_ref[...], cv_ref[...]
        sg = _sigmoid(cg)
        dcv = da * cg * sg
        dcg = da * cv * sg * (1.0 + cg * (1.0 - sg))
        dg_ref[...] = dcg
        dv_ref[...] = dcv
        _acc_rows(dbg_ref, _colsum(dcg), first)
        _acc_rows(dbv_ref, _colsum(dcv), first)

        @pl.when(first)
        def _():
            dwg_ref[...] = jnp.zeros_like(dwg_ref)
            dwv_ref[...] = jnp.zeros_like(dwv_ref)

        for s in range(n_sub):
            _conv_bwd_taps(_rows(dg_ref, s, TM_EW), g_tiles[s][0], _rows(gwrap_ref, s, WRAP_ROWS), dwg_ref, seg, FFN_K)
            _conv_bwd_taps(_rows(dv_ref, s, TM_EW), v_tiles[s][0], _rows(vwrap_ref, s, WRAP_ROWS), dwv_ref, seg, FFN_K)

    def vec(shift, rows):
        return pl.BlockSpec((None, None, rows, fb), lambda n, i: (l, n + shift, 0, 0))

    def acc(rows):
        return pl.BlockSpec((None, rows, fb), lambda n, i: (n, 0, 0))

    wrap = pltpu.VMEM((n_sub * WRAP_ROWS, fb), F32)
    tmp = pltpu.VMEM((tm, fb), F32)
    return _call(body, name, (half, n_steps),
                 [pl.BlockSpec((tm, d), lambda n, i: (i, 0)), pl.BlockSpec((None, fb, d), lambda n, i: (n, 0, 0)),
                  tile, prev, tile, prev, vec(0, FFN_K), vec(half, FFN_K), vec(0, 1), vec(half, 1)],
                 [tile, tile, acc(1), acc(1), acc(SUBLANES), acc(SUBLANES)],
                 [_sds((half, t, fb), F32), _sds((half, t, fb), F32), _sds((half, 1, fb), F32),
                  _sds((half, 1, fb), F32), _sds((half, SUBLANES, fb), F32), _sds((half, SUBLANES, fb), F32)],
                 [wrap, wrap, tmp, tmp], deps=deps)(dy, w_down, hg, hg, hv, hv, wdw, wdw, bdw, bdw)


def _ffn_conv_t_dx(name, dcg, dcv, wdw, w_up, res, xhat, rstd, g3, l, seq, deps=()):
    half, t, fb = dcg.shape
    nb, d = 2 * half, res.shape[-1]
    tm = min(TM_FFN, seq)
    n_sub, seg, seq_steps, n_steps = tm // TM_EW, TM_EW // SUBLANES, seq // tm, t // tm
    per = tm // WRAP_ROWS

    def body(g_ref, gn_ref, v_ref, vn_ref, w_ref, up_ref, res_ref, xh_ref, rs_ref, gam_ref,
             dh_ref, db_ref, dr_ref, dgam_ref, dbeta_ref, dsum_ref, wrap_ref, out_ref, acc_ref):
        i, n = pl.program_id(0), pl.program_id(1)
        keep = (i % seq_steps != seq_steps - 1).astype(F32)

        def conv_t(d_ref, dn_ref):
            for s, (sub, _, _, nx, kn) in enumerate(_sub_tiles(d_ref, None, dn_ref, None, keep, n_sub)):
                _fill_wrap_next(sub, nx, wrap_ref, FFN_K - 1, kn)
                _conv_bwd_data(sub, wrap_ref, w_ref, _rows(out_ref, s, TM_EW), seg, FFN_K)

        @pl.when(n < half)
        def _():
            conv_t(g_ref, gn_ref)

        @pl.when(n >= half)
        def _():
            conv_t(v_ref, vn_ref)

        dh = out_ref[...]
        dhb = dh.astype(BF16)
        dh_ref[...] = dhb
        _acc_rows(db_ref.at[n], _colsum(dh), i == 0)
        p = jnp.dot(dhb, up_ref[...], preferred_element_type=F32)

        @pl.when(n == 0)
        def _():
            acc_ref[...] = p

        @pl.when(n > 0)
        def _():
            acc_ref[...] += p

        @pl.when(n == nb - 1)
        def _():
            _ln_bwd_rows(acc_ref[...] + ALPHA * res_ref[...], xh_ref, rs_ref, gam_ref, dr_ref, dgam_ref, dbeta_ref,
                         dsum_ref, i == 0)

    def src(gate):
        def blk(n):
            return jnp.minimum(n, half - 1) if gate else jnp.maximum(n - half, 0)
        tile = pl.BlockSpec((None, tm, fb), lambda i, n: (blk(n), i, 0))
        nxt = pl.BlockSpec((None, WRAP_ROWS, fb),
                           lambda i, n: (blk(n), jnp.minimum((i + 1) * per, n_steps * per - 1), 0))
        return [tile, nxt]

    row = pl.BlockSpec((tm, d), lambda i, n: (i, 0))
    ln_ins, ln_outs = _ln_bwd_specs(tm, d, l, lambda i, n: (i, 0))
    tmp = pltpu.VMEM((tm, fb), F32)
    halo = pltpu.VMEM((WRAP_ROWS, fb), F32)
    return _call(body, name, (n_steps, nb),
                 src(True) + src(False) +
                 [pl.BlockSpec((None, None, FFN_K, fb), lambda i, n: (l, n, 0, 0)),
                  pl.BlockSpec((None, fb, d), lambda i, n: (n, 0, 0)), row] + ln_ins,
                 [pl.BlockSpec((None, tm, fb), lambda i, n: (n, i, 0)),
                  pl.BlockSpec((nb, 1, fb), lambda i, n: (0, 0, 0))] + ln_outs,
                 [_sds((nb, t, fb), BF16), _sds((nb, 1, fb), F32), _sds((t, d), F32)] + [_sds((1, d), F32)] * 3,
                 [halo, tmp, pltpu.VMEM((tm, d), F32)],
                 deps=deps)(dcg, dcg, dcv, dcv, wdw, w_up, res, xhat, rstd, g3)


def _gelu_parts(h):
    cdf = 0.5 * (1.0 + lax.erf(h * INV_SQRT2))
    return h * cdf, cdf


def _seg_axis(a, axis, fn):
    return jnp.moveaxis(fn(jnp.moveaxis(a, axis, 0), TM_EW), 0, axis)


def _sgu_operands(w_s, b_s):
    nl = w_s.shape[0]
    n_sub = TM_EW // CHUNK
    tril = jnp.tril(jnp.ones((CHUNK, CHUNK), dtype=bool))
    w_causal = jnp.where(tril, w_s, 0.0)
    w_tile = (jnp.eye(n_sub, dtype=F32)[None, None, :, None, :, None] * w_causal[:, :, None, :, None, :]).reshape(
        nl, GROUPS, TM_EW, TM_EW)
    w_tile = _seg_axis(_seg_axis(w_tile, 2, _to_segments), 3, _to_segments).astype(BF16)
    bs_tile = jnp.broadcast_to(b_s[:, :, None, :, None], (nl, GROUPS, n_sub, CHUNK, CHUNK)).reshape(
        nl, GROUPS, TM_EW, CHUNK)
    return w_tile, _seg_axis(bs_tile, 2, _to_segments)


def _sgu_param_grads(dwt, dbt):
    n_sub = TM_EW // CHUNK
    tril = jnp.tril(jnp.ones((CHUNK, CHUNK), dtype=bool))
    dwt = _seg_axis(_seg_axis(dwt, 1, _from_segments), 2, _from_segments).reshape(GROUPS, n_sub, CHUNK, n_sub, CHUNK)
    dw = sum(dwt[:, a, :, a, :] for a in range(n_sub))
    db = _seg_axis(dbt, 1, _from_segments).reshape(GROUPS, n_sub, CHUNK).sum(axis=1)
    return jnp.where(tril, dw, 0.0), db


def _sgu(name, h, g3, b3, wt, bst, l):
    t, c2 = h.shape
    c = c2 // 2
    tm = TM_EW

    def body(h_ref, g_ref, b_ref, wt_ref, bs_ref, o_ref):
        z, _ = _gelu_parts(h_ref[...])
        u = z[:, :c]
        xhat, _ = _ln_stats(z[:, c:])
        vnb = (xhat * g_ref[...] + b_ref[...]).astype(BF16)
        for gi in range(GROUPS):
            cs = slice(gi * CHUNK, (gi + 1) * CHUNK)
            sp = jnp.dot(wt_ref[gi], vnb[:, cs], preferred_element_type=F32) + bs_ref[gi]
            o_ref[:, cs] = (u[:, cs] * sp).astype(BF16)

    return _call(body, name, (t // tm,),
                 [_row_spec(tm, c2), _param_spec(l, c), _param_spec(l, c),
                  pl.BlockSpec((None, GROUPS, tm, tm), lambda i: (l, 0, 0, 0)),
                  pl.BlockSpec((None, GROUPS, tm, CHUNK), lambda i: (l, 0, 0, 0))],
                 _row_spec(tm, c), _sds((t, c), BF16))(h, g3, b3, wt, bst)


def _sgu_bwd(name, dq, h, g3, b3, wt, bst, l, deps=()):
    t, c2 = h.shape
    c = c2 // 2
    tm = TM_EW
    n_tiles = t // tm

    def body(dq_ref, h_ref, g_ref, b_ref, wt_ref, bs_ref,
             dh_ref, dbin_ref, dw_ref, dbs_ref, dg_ref, db_ref, du_ref, dvn_ref, bsum_ref):
        i = pl.program_id(0)
        first = i == 0
        hv = h_ref[...]
        z, cdf = _gelu_parts(hv)
        u = z[:, :c]
        xhat, rstd = _ln_stats(z[:, c:])
        g = g_ref[...]
        vnb = (xhat * g + b_ref[...]).astype(BF16)

        @pl.when(first)
        def _():
            dw_ref[...] = jnp.zeros_like(dw_ref)
            bsum_ref[...] = jnp.zeros_like(bsum_ref)

        for gi in range(GROUPS):
            cs = slice(gi * CHUNK, (gi + 1) * CHUNK)
            vb = vnb[:, cs]
            w = wt_ref[gi]
            sp = jnp.dot(w, vb, preferred_element_type=F32) + bs_ref[gi]
            dqb = dq_ref[:, cs]
            du_ref[:, cs] = dqb * sp
            dsp = dqb * u[:, cs]
            bsum_ref[gi] += dsp
            dspb = dsp.astype(BF16)
            dw_ref[gi] += lax.dot_general(dspb, vb, (_DIMS["nt"], ((), ())), preferred_element_type=F32)
            dvn_ref[:, cs] = lax.dot_general(w, dspb, (_DIMS["tn"], ((), ())), preferred_element_type=F32)

        dvn = dvn_ref[...]
        dv = _ln_backward(dvn * g, xhat, rstd)
        pdf = jnp.exp(-0.5 * hv * hv) * INV_SQRT2PI
        dgelu = cdf + hv * pdf
        dhu = du_ref[...] * dgelu[:, :c]
        dhv = dv * dgelu[:, c:]
        dh_ref[:, :c] = dhu.astype(BF16)
        dh_ref[:, c:] = dhv.astype(BF16)
        _acc_rows(dbin_ref.at[:, :c], _colsum(dhu), first)
        _acc_rows(dbin_ref.at[:, c:], _colsum(dhv), first)
        _acc_rows(dg_ref, _colsum(dvn * xhat), first)
        _acc_rows(db_ref, _colsum(dvn), first)

        @pl.when(i == n_tiles - 1)
        def _():
            dbs_ref[...] = jnp.sum(bsum_ref[...], axis=-1)

    vec = pl.BlockSpec((1, c), lambda i: (0, 0))
    return _call(body, name, (n_tiles,),
                 [_row_spec(tm, c), _row_spec(tm, c2), _param_spec(l, c), _param_spec(l, c),
                  pl.BlockSpec((None, GROUPS, tm, tm), lambda i: (l, 0, 0, 0)),
                  pl.BlockSpec((None, GROUPS, tm, CHUNK), lambda i: (l, 0, 0, 0))],
                 [_row_spec(tm, c2), pl.BlockSpec((1, c2), lambda i: (0, 0)),
                  pl.BlockSpec((GROUPS, tm, tm), lambda i: (0, 0, 0)),
                  pl.BlockSpec((GROUPS, tm), lambda i: (0, 0)), vec, vec],
                 [_sds((t, c2), BF16), _sds((1, c2), F32), _sds((GROUPS, tm, tm), F32),
                  _sds((GROUPS, tm), F32), _sds((1, c), F32), _sds((1, c), F32)],
                 [pltpu.VMEM((tm, c), F32), pltpu.VMEM((tm, c), F32), pltpu.VMEM((GROUPS, tm, CHUNK), F32)],
                 deps=deps)(dq, h, g3, b3, wt, bst)


def _loss(name, y, target):
    t, d = y.shape
    tm = min(TM_ROW, t)
    n_tiles = t // tm

    def body(y_ref, t_ref, l_ref, dy_ref, acc_ref):
        i = pl.program_id(0)
        diff = y_ref[...] - t_ref[...]
        dy_ref[...] = diff * (1.0 / d)
        _acc_rows(acc_ref, _colsum(diff * diff), i == 0)

        @pl.when(i == n_tiles - 1)
        def _():
            l_ref[...] = jnp.broadcast_to(jnp.sum(acc_ref[...], axis=-1, keepdims=True) * (0.5 / d), (1, LANES))

    return _call(body, name, (n_tiles,), [_row_spec(tm, d), _row_spec(tm, d)],
                 [pl.BlockSpec((1, LANES), lambda i: (0, 0)), _row_spec(tm, d)],
                 [_sds((1, LANES), F32), _sds((t, d), F32)], [pltpu.VMEM((1, d), F32)])(y, target)


def _adamw(g, w, m, v):
    m2 = ADAM_B1 * m + (1.0 - ADAM_B1) * g
    v2 = ADAM_B2 * v + (1.0 - ADAM_B2) * (g * g)
    m_hat = m2 / (1.0 - ADAM_B1 ** ADAM_STEP)
    v_hat = v2 / (1.0 - ADAM_B2 ** ADAM_STEP)
    delta = -ADAM_LR * (m_hat / (jnp.sqrt(v_hat) + ADAM_EPS) + ADAM_WD * w)
    return delta, m2, v2


ROW_TILE_CAP = 512


def _row_tile(rows, cap=ROW_TILE_CAP):
    if rows <= cap:
        return rows
    for tr in range(cap, 15, -16):
        if rows % tr == 0:
            return tr
    return rows


def _sum8_adamw(name, dev, lands, parts, w, m, v):
    nl = len(lands)
    _, r, c = lands[0].shape
    tr = _row_tile(r, cap=128)

    def body(dev_ref, *refs):
        land, own = refs[:nl], refs[nl:2 * nl]
        w_ref, m_ref, v_ref, g_ref, d_ref, m2_ref, v2_ref = refs[2 * nl:]
        layer, me = pl.program_id(0), dev_ref[0]
        for l in range(nl):
            @pl.when(layer == l)
            def _(l=l):
                g = None
                for s in range(N_DEV):
                    part = jnp.where(me == s, own[l][...], land[l][s]).astype(F32)
                    g = part if g is None else g + part
                delta, m2, v2 = _adamw(g, w_ref[...], m_ref[...], v_ref[...])
                g_ref[...] = g
                d_ref[...] = delta
                m2_ref[...] = m2
                v2_ref[...] = v2

    def rows_of(l, a, i):
        return jnp.where(a == l, i, 0)

    spec = pl.BlockSpec((None, tr, c), lambda a, i, dev_ref: (a, i, 0))
    in_specs = [pl.BlockSpec((N_DEV, tr, c), lambda a, i, dev_ref, l=l: (0, rows_of(l, a, i), 0)) for l in range(nl)]
    in_specs += [pl.BlockSpec((None, tr, c), lambda a, i, dev_ref, l=l: (dev_ref[0], rows_of(l, a, i), 0))
                 for l in range(nl)]
    grid_spec = pltpu.PrefetchScalarGridSpec(
        num_scalar_prefetch=1, grid=(nl, r // tr), in_specs=in_specs + [spec] * 3, out_specs=[spec] * 4)
    return pl.pallas_call(
        body, name=name, grid_spec=grid_spec, out_shape=[_sds(w.shape, F32)] * 4,
        compiler_params=pltpu.CompilerParams(vmem_limit_bytes=VMEM_LIMIT))(dev, *lands, *parts, w, m, v)


def _sum8(name, parts):
    _, r, c = parts.shape
    tr = _row_tile(r)

    def body(p_ref, o_ref):
        acc = p_ref[0]
        for s in range(1, N_DEV):
            acc = acc + p_ref[s]
        o_ref[...] = acc

    return _call(body, name, (r // tr,), [pl.BlockSpec((N_DEV, tr, c), lambda i: (0, i, 0))],
                 pl.BlockSpec((tr, c), lambda i: (i, 0)), _sds((r, c), F32))(parts)


def _adamw_small(name, gs, ws, ms, vs):
    n = len(gs)

    def body(*refs):
        g, w, m, v = (refs[k * n:(k + 1) * n] for k in range(4))
        d_out, m_out, v_out = (refs[(4 + k) * n:(5 + k) * n] for k in range(3))
        for k in range(n):
            d_out[k][...], m_out[k][...], v_out[k][...] = _adamw(g[k][...], w[k][...], m[k][...], v[k][...])

    vmem = pl.BlockSpec(memory_space=pltpu.VMEM)
    outs = _call(body, name, (), [vmem] * (4 * n), [vmem] * (3 * n), [_sds(w.shape, F32) for w in ws] * 3)(
        *gs, *ws, *ms, *vs)
    return outs[:n], outs[n:2 * n], outs[2 * n:]


def _pack(arrs, row_multiple=SUBLANES):
    pieces, rows = [], 0
    for a in arrs:
        piece = a.reshape(-1, LANES)
        piece = jnp.pad(piece, ((0, (-piece.shape[0]) % SUBLANES), (0, 0)))
        pieces.append(piece)
        rows += piece.shape[0]
    if rows % row_multiple:
        pieces.append(jnp.zeros(((-rows) % row_multiple, LANES), pieces[0].dtype))
    return jnp.concatenate(pieces, axis=0)


def _unpack(buf, shapes, lead=0):
    out, pos = [], 0
    for shp in shapes:
        rows = math.prod(shp) // LANES
        piece = lax.slice_in_dim(buf, pos, pos + rows, axis=lead)
        out.append(piece.reshape(buf.shape[:lead] + tuple(shp)))
        pos += rows + (-rows) % SUBLANES
    return out


REPLICATED = ["conv_b_in", "conv_b_dw", "conv_ln_g", "conv_ln_b", "conv_b_out", "gmlp_w_s", "gmlp_b_s",
              "ffn_b_up", "ffn_b_dw", "ffn_b_down", "norm1_g", "norm1_b", "norm2_g", "norm2_b"]
SMALL_SHARDED = ["conv_w_dw", "gmlp_b_in", "gmlp_ln_g", "gmlp_ln_b", "gmlp_b_out", "ffn_w_dw"]
BIG = ["conv_w_in", "conv_w_out", "gmlp_w_in", "gmlp_w_out", "ffn_w_up", "ffn_w_down"]
WEIGHTS = ["conv_w_in", "conv_b_in", "conv_w_dw", "conv_b_dw", "conv_ln_g", "conv_ln_b", "conv_w_out", "conv_b_out",
           "gmlp_w_in", "gmlp_b_in", "gmlp_ln_g", "gmlp_ln_b", "gmlp_w_s", "gmlp_b_s", "gmlp_w_out", "gmlp_b_out",
           "ffn_w_up", "ffn_b_up", "ffn_w_dw", "ffn_b_dw", "ffn_w_down", "ffn_b_down",
           "norm1_g", "norm1_b", "norm2_g", "norm2_b"]


def _from_shards(g, lead_shape):
    nd = len(lead_shape)
    perm = tuple(range(1, nd + 1)) + (0, nd + 1)
    return g.transpose(perm).reshape(tuple(lead_shape) + (-1,))


def _to_shards(full, width):
    lead = full.shape[:-1]
    nd = len(lead)
    parts = full.reshape(lead + (N_DEV, width))
    return parts.transpose((nd,) + tuple(range(nd)) + (nd + 1,))


def _step(p):
    x_in, target_in = p["x"], p["loss_target"]
    bsz, seq, d = x_in.shape
    t = bsz * seq
    assert seq % TM_EW == 0 and TM_EW % CHUNK == 0 and TM_EW // SUBLANES >= CONV_K - 1
    x0 = _to_segments(x_in.reshape(t, d), TM_EW)
    target = _to_segments(target_in.reshape(t, d), TM_EW)
    n_conv, n_gmlp = p["conv_w_in"].shape[0], p["gmlp_w_in"].shape[0]
    fb = p["ffn_w_up"].shape[-1]
    nblk = N_DEV
    half = nblk // 2
    cw = p["conv_w_in"].shape[-1]
    tm = min(TM_MM, t)
    nt = t // tm
    tk = min(TK_DW, t)
    ntk = t // tk
    dev = 4 * lax.axis_index("x") + 2 * lax.axis_index("y") + lax.axis_index("c")

    small_shapes = [p[n].shape for n in SMALL_SHARDED]
    small_src = _pack([p[n] for n in SMALL_SHARDED])[None]
    small_all = _all_gather("gather_small_weights", [small_src])[0][0]
    sm = _unpack(small_all, small_shapes, lead=1)
    w_src = []
    for i in range(DEPTH):
        mix = "conv" if i % 2 == 0 else "gmlp"
        w_src += [p[mix + "_w_in"][i // 2].astype(BF16), p[mix + "_w_out"][i // 2].astype(BF16),
                  p["ffn_w_up"][i].T.astype(BF16), p["ffn_w_down"][i].astype(BF16)]
    send_sems, recv_sems, w_land, _ = _gather_start(
        "weights_gather_start", _place_own("weights_place_own", w_src, deps=[small_all]))
    W_IN, W_OUT, W_UP, W_DOWN = range(4)

    def wait_weight(i, k, after):
        return _gather_wait(f"l{i}_weights_wait{k}", w_land[4 * i + k], send_sems, recv_sems, 4 * i + k, after)
    conv_w_dw = _from_shards(sm[0], sm[0].shape[1:-1])
    gmlp_b_in = _from_shards(sm[1], sm[1].shape[1:-1])
    gmlp_ln_g = _from_shards(sm[2], sm[2].shape[1:-1])
    gmlp_ln_b = _from_shards(sm[3], sm[3].shape[1:-1])
    gmlp_b_out = _from_shards(sm[4], sm[4].shape[1:-1])
    ffn_w_dw = sm[5].transpose(1, 0, 2, 3)

    def rows3(a):
        return a.reshape(a.shape[0], 1, a.shape[-1])

    conv_b_in4 = p["conv_b_in"].reshape(n_conv, N_DEV, 1, cw)
    gmlp_b_in4 = gmlp_b_in.reshape(n_gmlp, N_DEV, 1, cw)
    ffn_b_up4 = p["ffn_b_up"].reshape(DEPTH, nblk, 1, fb)
    ffn_b_dw4 = p["ffn_b_dw"].reshape(DEPTH, nblk, 1, fb)
    conv_b_dw3, conv_ln_g3, conv_ln_b3 = rows3(p["conv_b_dw"]), rows3(p["conv_ln_g"]), rows3(p["conv_ln_b"])
    conv_b_out3, gmlp_b_out3, ffn_b_down3 = rows3(p["conv_b_out"]), rows3(gmlp_b_out), rows3(p["ffn_b_down"])
    gmlp_ln_g3, gmlp_ln_b3 = rows3(gmlp_ln_g), rows3(gmlp_ln_b)
    n1g3, n1b3, n2g3, n2b3 = rows3(p["norm1_g"]), rows3(p["norm1_b"]), rows3(p["norm2_g"]), rows3(p["norm2_b"])
    w_tile, bs_tile = _sgu_operands(p["gmlp_w_s"], p["gmlp_b_s"])

    def mm_in(name, xa, wg, l, bias4, glu=False):
        tmi = min(TM_LN, t)
        c_half = half * cw

        def body(a_ref, b_ref, bias_ref, h_ref, *u_ref):
            xb = a_ref[...].astype(BF16)
            for n in range(N_DEV):
                h_ref[:, n * cw:(n + 1) * cw] = jnp.dot(xb, b_ref[n], preferred_element_type=F32) + bias_ref[n]
            if glu:
                u_ref[0][...] = h_ref[:, :c_half] * _sigmoid(h_ref[:, c_half:])

        outs = _call(body, name, (t // tmi,),
                     [pl.BlockSpec((tmi, d), lambda i: (i, 0)), pl.BlockSpec((N_DEV, d, cw), lambda i: (0, 0, 0)),
                      pl.BlockSpec((None, N_DEV, 1, cw), lambda i: (l, 0, 0, 0))],
                     [pl.BlockSpec((tmi, N_DEV * cw), lambda i: (i, 0))]
                     + ([pl.BlockSpec((tmi, c_half), lambda i: (i, 0))] if glu else []),
                     [_sds((t, N_DEV * cw), F32)] + ([_sds((t, c_half), F32)] if glu else []))(xa, wg, bias4)
        return outs if glu else outs[0]

    def mm_out_dx(name, dy, w, deps=()):
        return _matmul(name, dy, w, "nt", grid=(nt,),
                       a_spec=pl.BlockSpec((tm, d), lambda i: (i, 0)),
                       b_spec=pl.BlockSpec((d, d), lambda i: (0, 0)),
                       o_spec=pl.BlockSpec((tm, d), lambda i: (i, 0)), o_shape=(t, d), o_dtype=F32, deps=deps)

    def mm_out_dw(name, sa, dy):
        return _matmul(name, sa, dy, "tn", grid=(nt,), k_axis=0, nk=nt, acc_shape=(d, d),
                       a_spec=pl.BlockSpec((tm, d), lambda k: (k, 0)),
                       b_spec=pl.BlockSpec((tm, d), lambda k: (k, 0)),
                       o_spec=pl.BlockSpec((d, d), lambda k: (0, 0)), o_shape=(d, d), o_dtype=BF16)

    def mm_in_dx(name, dh, wg, res):
        tmx = min(TM_LN, t)

        def body(a_ref, b_ref, res_ref, o_ref):
            y = ALPHA * res_ref[...]
            for n in range(N_DEV):
                y = y + lax.dot_general(a_ref[:, n * cw:(n + 1) * cw], b_ref[n], (_DIMS["nt"], ((), ())),
                                        preferred_element_type=F32)
            o_ref[...] = y

        row = pl.BlockSpec((tmx, d), lambda i: (i, 0))
        return _call(body, name, (t // tmx,),
                     [pl.BlockSpec((tmx, N_DEV * cw), lambda i: (i, 0)),
                      pl.BlockSpec((N_DEV, d, cw), lambda i: (0, 0, 0)), row], row, _sds((t, d), F32))(dh, wg, res)

    def mm_in_dw(name, xa, dh):
        def body(a_ref, b_ref, o_ref, acc_ref):
            k = pl.program_id(1)
            p = lax.dot_general(a_ref[...].astype(BF16), b_ref[...], (_DIMS["tn"], ((), ())),
                                preferred_element_type=F32)
            _acc_rows(acc_ref, p, k == 0)

            @pl.when(k == nt - 1)
            def _():
                for n in range(half):
                    o_ref[n] = acc_ref[:, n * cw:(n + 1) * cw].astype(BF16)

        return _call(body, name, (2, nt),
                     [pl.BlockSpec((tm, d), lambda c, k: (k, 0)), pl.BlockSpec((tm, half * cw), lambda c, k: (k, c))],
                     pl.BlockSpec((half, d, cw), lambda c, k: (c, 0, 0)), _sds((N_DEV, d, cw), BF16),
                     [pltpu.VMEM((d, half * cw), F32)])(xa, dh)

    def mm_down_dw(name, a, dy, deps=()):
        return _matmul(name, a, dy, "tn", grid=(half, ntk), k_axis=1, nk=ntk, acc_shape=(fb, d),
                       a_spec=pl.BlockSpec((None, tk, fb), lambda n, k: (n, k, 0)),
                       b_spec=pl.BlockSpec((tk, d), lambda n, k: (k, 0)),
                       o_spec=pl.BlockSpec((None, fb, d), lambda n, k: (n, 0, 0)),
                       o_shape=(half, fb, d), o_dtype=BF16, deps=deps)

    def mm_up_dw(name, xa, dh):
        return _matmul(name, dh, xa, "tn", grid=(nblk, ntk), k_axis=1, nk=ntk, acc_shape=(fb, d),
                       a_spec=pl.BlockSpec((None, tk, fb), lambda n, k: (n, k, 0)),
                       b_spec=pl.BlockSpec((tk, d), lambda n, k: (k, 0)),
                       o_spec=pl.BlockSpec((None, fb, d), lambda n, k: (n, 0, 0)),
                       o_shape=(nblk, fb, d), o_dtype=BF16)

    saved = []
    xcur = x0
    for i in range(DEPTH):
        j = i // 2
        s = {"x": xcur}
        s["w_in"] = wait_weight(i, W_IN, xcur)
        if i % 2 == 0:
            s["h"], s["u"] = mm_in(f"l{i}_conv_in_glu", xcur, s["w_in"], j, conv_b_in4, glu=True)
            s["c"] = _dwconv31(f"l{i}_dwconv", s["u"], conv_w_dw, conv_b_dw3, j, seq)
            s["s"] = _ln_silu(f"l{i}_ln_silu", s["c"], conv_ln_g3, conv_ln_b3, j)
            b_out3 = conv_b_out3
        else:
            s["h"] = mm_in(f"l{i}_gmlp_in", xcur, s["w_in"], j, gmlp_b_in4)
            s["s"] = _sgu(f"l{i}_sgu", s["h"], gmlp_ln_g3, gmlp_ln_b3, w_tile, bs_tile, j)
            b_out3 = gmlp_b_out3
        s["w_out"] = wait_weight(i, W_OUT, s["s"]).reshape(d, d)
        s["x1"], s["xhat1"], s["rstd1"] = _matmul_ln(
            f"l{i}_mixer_out_norm1", s["s"], s["w_out"], xcur, b_out3, n1g3, n1b3, j, i)
        s["w_up"] = wait_weight(i, W_UP, s["x1"])
        s["hg"], s["hv"], s["a"] = _ffn_up_act(f"l{i}_ffn_up_act", s["x1"], s["w_up"], ffn_b_up4, ffn_w_dw, ffn_b_dw4,
                                               i, seq)
        s["w_down"] = wait_weight(i, W_DOWN, s["a"]).reshape(half, fb, d)
        xcur, s["xhat2"], s["rstd2"] = _matmul_ln(
            f"l{i}_ffn_down_norm2", s["a"], s["w_down"], s["x1"], ffn_b_down3, n2g3, n2b3, i, i)
        saved.append(s)

    loss_row, dx = _loss("loss", xcur, target)

    started = {n: [None] * p[n].shape[0] for n in BIG}
    tokens = []

    def send_grads(name, items):
        done, token = _scatter_start(name, [g for _, _, g in items])
        for (n, l, _), st in zip(items, done):
            started[n][l] = st
        tokens.append(token)

    def take_tokens():
        out = list(tokens)
        tokens.clear()
        return out

    gl = {n: [None] * p[n].shape[0] for n in REPLICATED + SMALL_SHARDED}
    dr2, gl["norm2_g"][DEPTH - 1], gl["norm2_b"][DEPTH - 1], gl["ffn_b_down"][DEPTH - 1] = _ln_res_bwd(
        f"l{DEPTH - 1}_norm2_bwd", dx, saved[-1]["xhat2"], saved[-1]["rstd2"], n2g3, DEPTH - 1)
    for i in reversed(range(DEPTH)):
        j = i // 2
        s = saved[i]
        mix = "conv" if i % 2 == 0 else "gmlp"
        g_down = mm_down_dw(f"l{i}_ffn_down_dw", s["a"], dr2, deps=take_tokens()).reshape(N_DEV, -1, d)
        send_grads(f"l{i}_ffn_down_grad_scatter_start", [("ffn_w_down", i, g_down)])
        dcg, dcv, dbg, dbv, dwg, dwv = _ffn_act_bwd(f"l{i}_ffn_act_bwd", dr2, s["w_down"], s["hg"], s["hv"],
                                                    ffn_w_dw, ffn_b_dw4, i, seq, deps=take_tokens())
        gl["ffn_b_dw"][i] = jnp.concatenate([dbg, dbv], axis=0).reshape(1, nblk * fb)
        gl["ffn_w_dw"][i] = jnp.concatenate([dwg[:, :FFN_K], dwv[:, :FFN_K]], axis=0)
        dh, dbu, dr1, gl["norm1_g"][i], gl["norm1_b"][i], gl[mix + "_b_out"][j] = _ffn_conv_t_dx(
            f"l{i}_ffn_conv_t_dx", dcg, dcv, ffn_w_dw, s["w_up"], dr2, s["xhat1"], s["rstd1"], n1g3, i, seq,
            deps=take_tokens())
        gl["ffn_b_up"][i] = dbu.reshape(1, nblk * fb)
        send_grads(f"l{i}_ffn_up_grad_scatter_start", [("ffn_w_up", i, mm_up_dw(f"l{i}_ffn_up_dw", s["x1"], dh))])
        ds = mm_out_dx(f"l{i}_{mix}_out_dx", dr1, s["w_out"], deps=take_tokens())
        g_out = mm_out_dw(f"l{i}_{mix}_out_dw", s["s"], dr1).reshape(N_DEV, -1, d)
        if i % 2 == 0:
            dc, gl["conv_ln_g"][j], gl["conv_ln_b"][j], gl["conv_b_dw"][j] = _ln_silu_bwd(
                f"l{i}_ln_silu_bwd", ds, s["c"], conv_ln_g3, conv_ln_b3, j)
            du, dwdw = _dwconv31_bwd(f"l{i}_dwconv_bwd", dc, s["u"], conv_w_dw, j, seq)
            gl["conv_w_dw"][j] = dwdw[:CONV_K]
            dh, gl["conv_b_in"][j] = _glu_bwd(f"l{i}_glu_bwd", du, s["h"])
        else:
            dh, gl["gmlp_b_in"][j], dwt, dbt, gl["gmlp_ln_g"][j], gl["gmlp_ln_b"][j] = _sgu_bwd(
                f"l{i}_sgu_bwd", ds, s["h"], gmlp_ln_g3, gmlp_ln_b3, w_tile, bs_tile, j)
            gl["gmlp_w_s"][j], gl["gmlp_b_s"][j] = _sgu_param_grads(dwt, dbt)
        dx = mm_in_dx(f"l{i}_{mix}_in_dx", dh, s["w_in"], dr1)
        if i > 0:
            prev = saved[i - 1]
            dr2, gl["norm2_g"][i - 1], gl["norm2_b"][i - 1], gl["ffn_b_down"][i - 1] = _ln_res_bwd(
                f"l{i - 1}_norm2_bwd", dx, prev["xhat2"], prev["rstd2"], n2g3, i - 1)
        send_grads(f"l{i}_mixer_grads_scatter_start",
                   [(mix + "_w_out", j, g_out), (mix + "_w_in", j, mm_in_dw(f"l{i}_{mix}_in_dw", s["x"], dh))])
    grad_x = _from_segments(dx, TM_EW).reshape(bsz, seq, d)

    full_small = {n: jnp.stack(gl[n]).reshape(p[n].shape) for n in REPLICATED}
    shard_small = {}
    for n in SMALL_SHARDED:
        if n == "ffn_w_dw":
            shard_small[n] = jnp.stack(gl[n]).transpose(1, 0, 2, 3)
        else:
            width = p[n].shape[-1]
            lead = p[n].shape[:-1]
            shard_small[n] = _to_shards(jnp.stack(gl[n]).reshape(lead + (N_DEV * width,)), width)
    flat_shapes = [(1, LANES)] + [p[n].shape for n in REPLICATED] + [(N_DEV,) + p[n].shape for n in SMALL_SHARDED]
    flat_local = _pack([loss_row] + [full_small[n] for n in REPLICATED] + [shard_small[n] for n in SMALL_SHARDED],
                       row_multiple=ROW_TILE_CAP)

    small_send, small_recv, small_land, small_token = _gather_start(
        "small_grads_gather_start", _place_own("small_grads_place_own", [flat_local]))

    grads, delta, new_m, new_v = {}, {}, {}, {}
    dev1 = jnp.reshape(dev, (1,)).astype(jnp.int32)
    order = ["ffn_w_down", "ffn_w_up", "gmlp_w_out", "gmlp_w_in", "conv_w_out", "conv_w_in"]
    after = small_token
    for n in order:
        parts_done, lands_done = _scatter_wait(f"grads_{n}_scatter_wait", started[n], after)
        state = [p[n], p["m_" + n], p["v_" + n]]
        if n == "ffn_w_up":
            state = [a.transpose(0, 2, 1) for a in state]
        outs = _sum8_adamw(f"adamw_{n}", dev1, lands_done, parts_done, *state)
        after = outs[-1]
        if n == "ffn_w_up":
            outs = [a.transpose(0, 2, 1) for a in outs]
        grads[n], delta[n], new_m[n], new_v[n] = outs

    small_parts = _gather_wait("small_grads_gather_wait", small_land[0], small_send, small_recv, 0, after)
    summed = _unpack(_sum8("sum_small_grads", small_parts), flat_shapes)
    loss = summed[0][0, 0]
    grads.update(zip(REPLICATED, summed[1:1 + len(REPLICATED)]))
    for n, g in zip(SMALL_SHARDED, summed[1 + len(REPLICATED):]):
        grads[n] = lax.dynamic_index_in_dim(g, dev, axis=0, keepdims=False)
    small = REPLICATED + SMALL_SHARDED
    d_s, m_s, v_s = _adamw_small("adamw_small", [grads[n] for n in small], [p[n] for n in small],
                                 [p["m_" + n] for n in small], [p["v_" + n] for n in small])
    for n, dd, mm, vv in zip(small, d_s, m_s, v_s):
        delta[n], new_m[n], new_v[n] = dd, mm, vv

    return (loss, grad_x, *[grads[n] for n in WEIGHTS], *[delta[n] for n in WEIGHTS],
            *[new_m[n] for n in WEIGHTS], *[new_v[n] for n in WEIGHTS])


def kernel(x, conv_w_in, conv_b_in, conv_w_dw, conv_b_dw, conv_ln_g, conv_ln_b, conv_w_out, conv_b_out, gmlp_w_in, gmlp_b_in, gmlp_ln_g, gmlp_ln_b, gmlp_w_s, gmlp_b_s, gmlp_w_out, gmlp_b_out, ffn_w_up, ffn_b_up, ffn_w_dw, ffn_b_dw, ffn_w_down, ffn_b_down, norm1_g, norm1_b, norm2_g, norm2_b, loss_target, m_conv_w_in, m_conv_b_in, m_conv_w_dw, m_conv_b_dw, m_conv_ln_g, m_conv_ln_b, m_conv_w_out, m_conv_b_out, m_gmlp_w_in, m_gmlp_b_in, m_gmlp_ln_g, m_gmlp_ln_b, m_gmlp_w_s, m_gmlp_b_s, m_gmlp_w_out, m_gmlp_b_out, m_ffn_w_up, m_ffn_b_up, m_ffn_w_dw, m_ffn_b_dw, m_ffn_w_down, m_ffn_b_down, m_norm1_g, m_norm1_b, m_norm2_g, m_norm2_b, v_conv_w_in, v_conv_b_in, v_conv_w_dw, v_conv_b_dw, v_conv_ln_g, v_conv_ln_b, v_conv_w_out, v_conv_b_out, v_gmlp_w_in, v_gmlp_b_in, v_gmlp_ln_g, v_gmlp_ln_b, v_gmlp_w_s, v_gmlp_b_s, v_gmlp_w_out, v_gmlp_b_out, v_ffn_w_up, v_ffn_b_up, v_ffn_w_dw, v_ffn_b_dw, v_ffn_w_down, v_ffn_b_down, v_norm1_g, v_norm1_b, v_norm2_g, v_norm2_b):
    return _step(dict(locals()))
```

```python
import math

import jax
import jax.numpy as jnp
from jax import lax
from jax.experimental import pallas as pl
from jax.experimental.pallas import tpu as pltpu

F32 = jnp.float32
BF16 = jnp.bfloat16
MESH = pl.DeviceIdType.MESH

N_DEV = 8
DEPTH = 4
ALPHA = (2.0 * DEPTH) ** 0.25
LN_EPS = 1e-5
CONV_K = 31
FFN_K = 3
CHUNK = 128
GROUPS = 8
ADAM_LR = 0.001
ADAM_B1 = 0.9
ADAM_B2 = 0.999
ADAM_EPS = 1e-08
ADAM_WD = 0.01
ADAM_STEP = 10
INV_SQRT2 = 1.0 / math.sqrt(2.0)
INV_SQRT2PI = 1.0 / math.sqrt(2.0 * math.pi)

LANES = 128
SUBLANES = 8
VMEM_LIMIT = 48 * 1024 * 1024
TM_MM = 1024
TK_DW = 2048
TM_EW = 256
TM_ROW = 512


def _call(body, name, grid, in_specs, out_specs, out_shape, scratch=(), aliases=None, deps=()):
    deps = list(deps)
    in_specs = list(in_specs)
    n_in = len(in_specs)
    if deps:
        inner = body

        def body(*refs):
            return inner(*refs[:n_in], *refs[n_in + len(deps):])

        in_specs = in_specs + [pl.BlockSpec(memory_space=pl.ANY)] * len(deps)
    fn = pl.pallas_call(
        body, name=name, grid=grid, in_specs=in_specs, out_specs=out_specs, out_shape=out_shape,
        scratch_shapes=list(scratch), input_output_aliases=aliases or {},
        compiler_params=pltpu.CompilerParams(vmem_limit_bytes=VMEM_LIMIT))
    return lambda *args: fn(*args, *deps)


def _sds(shape, dtype):
    return jax.ShapeDtypeStruct(tuple(shape), dtype)


def _sigmoid(x):
    return 1.0 / (1.0 + jnp.exp(-x))


def _acc_rows(ref, val, first):
    @pl.when(first)
    def _():
        ref[...] = val

    @pl.when(jnp.logical_not(first))
    def _():
        ref[...] += val


def _colsum(v):
    return jnp.sum(v, axis=0, keepdims=True)


_DIMS = {"nn": ((1,), (0,)), "nt": ((1,), (1,)), "tn": ((0,), (0,))}


def _matmul(name, a, b, mode, *, grid, a_spec, b_spec, o_spec, o_shape, o_dtype, k_axis=None, nk=1,
            acc_shape=None, bias=None, bias_spec=None, res=None, res_spec=None, res_scale=1.0, deps=()):
    dims = (_DIMS[mode], ((), ()))
    has_bias, has_res = bias is not None, res is not None

    def body(*refs):
        a_ref, b_ref = refs[0], refs[1]
        pos = 2
        bias_ref = res_ref = None
        if has_bias:
            bias_ref = refs[pos]
            pos += 1
        if has_res:
            res_ref = refs[pos]
            pos += 1
        o_ref = refs[pos]
        acc_ref = refs[pos + 1] if nk > 1 else None
        p = lax.dot_general(a_ref[...].astype(BF16), b_ref[...].astype(BF16), dims, preferred_element_type=F32)

        def finish(acc):
            if has_bias:
                acc = acc + bias_ref[...]
            if has_res:
                acc = acc + res_scale * res_ref[...]
            o_ref[...] = acc.astype(o_dtype)

        if nk == 1:
            finish(p)
        else:
            k = pl.program_id(k_axis)

            @pl.when(k == 0)
            def _():
                acc_ref[...] = p

            @pl.when(k > 0)
            def _():
                acc_ref[...] += p

            @pl.when(k == nk - 1)
            def _():
                finish(acc_ref[...])

    ins, specs = [a, b], [a_spec, b_spec]
    if has_bias:
        ins.append(bias)
        specs.append(bias_spec)
    if has_res:
        ins.append(res)
        specs.append(res_spec)
    scratch = [pltpu.VMEM(acc_shape, F32)] if nk > 1 else []
    return _call(body, name, grid, specs, o_spec, _sds(o_shape, o_dtype), scratch, deps=deps)(*ins)


TM_LN = 512


def _matmul_ln(name, a, b, x_res, bias3, g3, b3, l_bias, l_norm):
    t, d = x_res.shape
    tm = min(TM_LN, t)
    blocked = a.ndim == 3

    def body(a_ref, b_ref, x_ref, bias_ref, g_ref, be_ref, o_ref, xh_ref, rs_ref):
        if blocked:
            y = None
            for k in range(a.shape[0]):
                p = jnp.dot(a_ref[k], b_ref[k], preferred_element_type=F32)
                y = p if y is None else y + p
        else:
            y = jnp.dot(a_ref[...], b_ref[...], preferred_element_type=F32)
        xhat, rstd = _ln_stats(ALPHA * x_ref[...] + y + bias_ref[...])
        o_ref[...] = xhat * g_ref[...] + be_ref[...]
        xh_ref[...] = xhat
        rs_ref[...] = rstd

    if blocked:
        a_spec = pl.BlockSpec((a.shape[0], tm, a.shape[2]), lambda i: (0, i, 0))
        b_spec = pl.BlockSpec(b.shape, lambda i: (0, 0, 0))
    else:
        a_spec = pl.BlockSpec((tm, a.shape[1]), lambda i: (i, 0))
        b_spec = pl.BlockSpec(b.shape, lambda i: (0, 0))
    row = pl.BlockSpec((tm, d), lambda i: (i, 0))
    stat = pl.BlockSpec((tm, 1), lambda i: (i, 0))

    def vec(l):
        return pl.BlockSpec((None, 1, d), lambda i: (l, 0, 0))

    return _call(body, name, (t // tm,), [a_spec, b_spec, row, vec(l_bias), vec(l_norm), vec(l_norm)],
                 [row, row, stat], [_sds((t, d), F32), _sds((t, d), F32), _sds((t, 1), F32)])(
                     a, b, x_res, bias3, g3, b3)


def _mesh_pos():
    return lax.axis_index("x"), lax.axis_index("y"), lax.axis_index("c")


def _any_specs(n):
    return [pl.BlockSpec(memory_space=pl.ANY)] * n


def _all_gather(name, srcs):
    n = len(srcs)

    def body(*refs):
        src, out = refs[:n], refs[n:2 * n]
        send_sems, recv_sems, local_sems = refs[2 * n:]
        x, y, c = _mesh_pos()
        me, sibling = (x, y, c), (x, y, 1 - c)
        chips = [(1 - x, y), (x, 1 - y), (1 - x, 1 - y)]

        def slot(k, p):
            return out[k].at[:, 4 * p[0] + 2 * p[1] + p[2]]

        def copy(k, idx, block, to, s=None):
            return pltpu.make_async_remote_copy(
                src_ref=slot(k, block) if s is None else s, dst_ref=slot(k, block),
                send_sem=send_sems.at[k * 7 + idx], recv_sem=recv_sems.at[k * 7 + idx],
                device_id=to, device_id_type=MESH)

        local = [pltpu.make_async_copy(src[k], slot(k, me), local_sems.at[k]) for k in range(n)]
        for cp in local:
            cp.start()
        first = []
        for k in range(n):
            first.append(copy(k, 0, me, sibling, src[k]))
            for j, chip in enumerate(chips):
                first.append(copy(k, 1 + j, me, (*chip, c), src[k]))
        for cp in first:
            cp.start()
        passed = []
        for j, chip in enumerate(chips):
            for k in range(n):
                copy(k, 1 + j, (*chip, c), me).wait_recv()
                cp = copy(k, 4 + j, (*chip, c), sibling)
                cp.start()
                passed.append(cp)
        for k in range(n):
            copy(k, 0, sibling, me).wait_recv()
            for j, chip in enumerate(chips):
                copy(k, 4 + j, (*chip, 1 - c), me).wait_recv()
        for cp in first + passed:
            cp.wait_send()
        for cp in local:
            cp.wait()

    out_shape = [_sds((s.shape[0], N_DEV) + s.shape[1:], s.dtype) for s in srcs]
    return _call(body, name, (), [pl.BlockSpec(memory_space=pltpu.VMEM)] * n, _any_specs(n), out_shape,
                 [pltpu.SemaphoreType.DMA((7 * n,)), pltpu.SemaphoreType.DMA((7 * n,)),
                  pltpu.SemaphoreType.DMA((n,))])(*srcs)


HBM_SPEC = pl.BlockSpec(memory_space=pltpu.HBM)
SEM_SPEC = pl.BlockSpec(memory_space=pltpu.SEMAPHORE)
N_PEER = N_DEV - 1


def _split_call(body, name, in_specs, out_specs, out_shape, aliases):
    return pl.pallas_call(
        body, name=name, in_specs=in_specs, out_specs=out_specs, out_shape=out_shape, input_output_aliases=aliases,
        compiler_params=pltpu.CompilerParams(has_side_effects=pltpu.SideEffectType.DATAFLOW_SIDE_EFFECTING))


def _peers(x, y, c):
    return [(1 - x if q & 4 else x, 1 - y if q & 2 else y, 1 - c if q & 1 else c) for q in range(1, N_DEV)]


def _in_hbm(a):
    return pltpu.with_memory_space_constraint(a, pltpu.HBM)


def _place_own(name, srcs, deps=()):
    n = len(srcs)

    def body(*refs):
        src, out, sems = refs[:n], refs[n:2 * n], refs[2 * n]
        x, y, c = _mesh_pos()
        dev = 4 * x + 2 * y + c
        copies = [pltpu.make_async_copy(src[k], out[k].at[dev], sems.at[k]) for k in range(n)]
        for cp in copies:
            cp.start()
        for cp in copies:
            cp.wait()

    return _call(body, name, (), [pl.BlockSpec(memory_space=pltpu.VMEM)] * n, _any_specs(n),
                 [_sds((N_DEV,) + s.shape, s.dtype) for s in srcs], [pltpu.SemaphoreType.DMA((n,))],
                 deps=deps)(*srcs)


def _gather_start(name, lands):
    n = len(lands)

    def body(*refs):
        land, send_sems, recv_sems = refs[:n], refs[n], refs[n + 1]
        x, y, c = _mesh_pos()
        dev = 4 * x + 2 * y + c
        for k in range(n):
            for peer in _peers(x, y, c):
                pltpu.make_async_remote_copy(
                    src_ref=land[k].at[dev], dst_ref=land[k].at[dev], send_sem=send_sems.at[k],
                    recv_sem=recv_sems.at[k], device_id=peer, device_id_type=MESH).start()
        token = refs[-1]
        token[...] = jnp.zeros_like(token)

    outs = _split_call(
        body, name, [HBM_SPEC] * n, [SEM_SPEC, SEM_SPEC] + [HBM_SPEC] * n + [pl.BlockSpec(memory_space=pltpu.VMEM)],
        [pltpu.SemaphoreType.DMA((n,)), pltpu.SemaphoreType.DMA((n,))] + [pltpu.HBM(a.shape, a.dtype) for a in lands]
        + [_sds((SUBLANES, LANES), F32)],
        {k: 2 + k for k in range(n)})(*[_in_hbm(a) for a in lands])
    return outs[0], outs[1], list(outs[2:2 + n]), outs[-1]


def _wait_seven(src_ref, dst_ref, send_sem, recv_sem):
    cp = pltpu.make_async_remote_copy(
        src_ref=src_ref.at[pl.ds(0, N_PEER)], dst_ref=dst_ref.at[pl.ds(0, N_PEER)], send_sem=send_sem,
        recv_sem=recv_sem, device_id=_mesh_pos(), device_id_type=MESH)
    cp.wait_send()
    cp.wait_recv()


def _gather_wait(name, land, send_sems, recv_sems, k, after):
    def body(land_ref, send_ref, recv_ref, after_ref, out_ref):
        _wait_seven(land_ref, land_ref, send_ref.at[k], recv_ref.at[k])

    return _split_call(body, name, [HBM_SPEC, SEM_SPEC, SEM_SPEC, pl.BlockSpec(memory_space=pl.ANY)], HBM_SPEC,
                       pltpu.HBM(land.shape, land.dtype), {0: 0})(land, send_sems, recv_sems, after)


def _scatter_start(name, parts_list):
    n = len(parts_list)

    def body(*refs):
        x, y, c = _mesh_pos()
        dev = 4 * x + 2 * y + c
        for k in range(n):
            parts_ref, land_ref = refs[2 * k], refs[2 * k + 1]
            send_sem, recv_sem = refs[2 * n + 4 * k], refs[2 * n + 4 * k + 1]
            for peer in _peers(x, y, c):
                pltpu.make_async_remote_copy(
                    src_ref=parts_ref.at[4 * peer[0] + 2 * peer[1] + peer[2]], dst_ref=land_ref.at[dev],
                    send_sem=send_sem, recv_sem=recv_sem, device_id=peer, device_id_type=MESH).start()
        token = refs[-1]
        token[...] = jnp.zeros_like(token)

    ins, out_specs, out_shape, aliases = [], [], [], {}
    for k, parts in enumerate(parts_list):
        buf = pltpu.HBM(parts.shape, parts.dtype)
        ins += [_in_hbm(parts), _in_hbm(lax.empty(parts.shape, parts.dtype))]
        out_specs += [SEM_SPEC, SEM_SPEC, HBM_SPEC, HBM_SPEC]
        out_shape += [pltpu.SemaphoreType.DMA(()), pltpu.SemaphoreType.DMA(()), buf, buf]
        aliases.update({2 * k: 4 * k + 2, 2 * k + 1: 4 * k + 3})
    outs = _split_call(body, name, [HBM_SPEC] * (2 * n), out_specs + [pl.BlockSpec(memory_space=pltpu.VMEM)],
                       out_shape + [_sds((SUBLANES, LANES), F32)], aliases)(*ins)
    return [tuple(outs[4 * k:4 * k + 4]) for k in range(n)], outs[-1]


def _scatter_wait(name, started, after):
    n = len(started)

    def body(*refs):
        for k in range(n):
            send_sem, recv_sem, parts_ref, land_ref = refs[4 * k:4 * k + 4]
            _wait_seven(parts_ref, land_ref, send_sem, recv_sem)

    flat = [a for s in started for a in s]
    outs = _split_call(
        body, name, [SEM_SPEC, SEM_SPEC, HBM_SPEC, HBM_SPEC] * n + [pl.BlockSpec(memory_space=pl.ANY)],
        [HBM_SPEC, HBM_SPEC] * n, [pltpu.HBM(a.shape, a.dtype) for s in started for a in s[2:]],
        {4 * k + 2 + t: 2 * k + t for k in range(n) for t in range(2)})(*flat, after)
    return list(outs[0::2]), list(outs[1::2])


def _to_segments(a, tile):
    seg = tile // SUBLANES
    return a.reshape((a.shape[0] // tile, SUBLANES, seg) + a.shape[1:]).swapaxes(1, 2).reshape(a.shape)


def _from_segments(a, tile):
    seg = tile // SUBLANES
    return a.reshape((a.shape[0] // tile, seg, SUBLANES) + a.shape[1:]).swapaxes(1, 2).reshape(a.shape)


def _chunk(ref, q):
    return ref[q * SUBLANES:(q + 1) * SUBLANES, :]


def _fill_wrap_prev(x_ref, halo_ref, wrap_ref, n_wrap, n_halo, seg, keep):
    sub = lax.broadcasted_iota(jnp.int32, (SUBLANES, x_ref.shape[-1]), 0)
    for j in range(n_wrap):
        q = seg - n_wrap + j
        hq = q - (seg - n_halo)
        row = halo_ref[hq * SUBLANES + SUBLANES - 1:(hq + 1) * SUBLANES, :] * keep
        wrap_ref[j * SUBLANES:(j + 1) * SUBLANES, :] = jnp.where(sub == 0, row, pltpu.roll(_chunk(x_ref, q), 1, 0))


def _fill_wrap_next(x_ref, halo_ref, wrap_ref, n_wrap, keep):
    sub = lax.broadcasted_iota(jnp.int32, (SUBLANES, x_ref.shape[-1]), 0)
    for j in range(n_wrap):
        row = halo_ref[j * SUBLANES:j * SUBLANES + 1, :] * keep
        wrap_ref[j * SUBLANES:(j + 1) * SUBLANES, :] = jnp.where(
            sub == SUBLANES - 1, row, pltpu.roll(_chunk(x_ref, j), SUBLANES - 1, 0))


def _past(x_ref, wrap_ref, q, d, n_wrap):
    return _chunk(x_ref, q - d) if q >= d else _chunk(wrap_ref, q - d + n_wrap)


def _future(x_ref, wrap_ref, q, d, seg):
    return _chunk(x_ref, q + d) if q + d < seg else _chunk(wrap_ref, q + d - seg)


def _conv_fwd(x_ref, wrap_ref, w_ref, b_ref, out_ref, seg, k_taps):
    bias = jnp.broadcast_to(b_ref[...], (SUBLANES, x_ref.shape[-1]))
    for q in range(seg):
        acc = bias
        for k in range(k_taps):
            acc = acc + w_ref[k:k + 1, :] * _past(x_ref, wrap_ref, q, k_taps - 1 - k, k_taps - 1)
        out_ref[q * SUBLANES:(q + 1) * SUBLANES, :] = acc


def _conv_bwd_data(d_ref, wrap_ref, w_ref, out_ref, seg, k_taps):
    for q in range(seg):
        acc = None
        for k in range(k_taps):
            term = w_ref[k:k + 1, :] * _future(d_ref, wrap_ref, q, k_taps - 1 - k, seg)
            acc = term if acc is None else acc + term
        out_ref[q * SUBLANES:(q + 1) * SUBLANES, :] = acc


def _conv_bwd_taps(d_ref, x_ref, wrap_ref, dw_ref, seg, k_taps):
    for k in range(k_taps):
        part = None
        for q in range(seg):
            term = _chunk(d_ref, q) * _past(x_ref, wrap_ref, q, k_taps - 1 - k, k_taps - 1)
            part = term if part is None else part + term
        dw_ref[k:k + 1, :] += _colsum(part)


def _tile_halo_specs(tm, width_block, n_halo, n_tiles, block_of):
    rows = n_halo * SUBLANES
    per = tm // rows
    tile = pl.BlockSpec(width_block(tm), lambda n, i: block_of(n, i))
    prev = pl.BlockSpec(width_block(rows), lambda n, i: block_of(n, jnp.maximum(i * per - 1, 0)))
    nxt = pl.BlockSpec(width_block(rows), lambda n, i: block_of(n, jnp.minimum((i + 1) * per, n_tiles * per - 1)))
    return tile, prev, nxt


def _ln_stats(v):
    mu = jnp.mean(v, axis=-1, keepdims=True)
    vc = v - mu
    var = jnp.mean(vc * vc, axis=-1, keepdims=True)
    rstd = lax.rsqrt(var + LN_EPS)
    return vc * rstd, rstd


def _ln_backward(dxhat, xhat, rstd):
    m1 = jnp.mean(dxhat, axis=-1, keepdims=True)
    m2 = jnp.mean(dxhat * xhat, axis=-1, keepdims=True)
    return rstd * (dxhat - m1 - xhat * m2)


def _row_spec(tm, width):
    return pl.BlockSpec((tm, width), lambda i: (i, 0))


def _param_spec(l, width):
    return pl.BlockSpec((None, 1, width), lambda *_: (l, 0, 0))


def _ln_bwd_rows(dout, xh_ref, rs_ref, g_ref, dr_ref, dg_ref, db_ref, dsum_ref, first):
    xhat = xh_ref[...]
    dr = _ln_backward(dout * g_ref[...], xhat, rs_ref[...])
    dr_ref[...] = dr
    _acc_rows(dg_ref, _colsum(dout * xhat), first)
    _acc_rows(db_ref, _colsum(dout), first)
    _acc_rows(dsum_ref, _colsum(dr), first)


def _ln_bwd_specs(tm, d, l, row_of):
    vec = pl.BlockSpec((1, d), lambda *_: (0, 0))
    ins = [pl.BlockSpec((tm, d), row_of), pl.BlockSpec((tm, 1), row_of), _param_spec(l, d)]
    return ins, [pl.BlockSpec((tm, d), row_of), vec, vec, vec]


def _ln_res_bwd(name, dout, xhat, rstd, g3, l, deps=()):
    t, d = dout.shape
    tm = min(TM_ROW, t)

    def body(do_ref, xh_ref, rs_ref, g_ref, dr_ref, dg_ref, db_ref, dc_ref):
        _ln_bwd_rows(do_ref[...], xh_ref, rs_ref, g_ref, dr_ref, dg_ref, db_ref, dc_ref, pl.program_id(0) == 0)

    ins, outs = _ln_bwd_specs(tm, d, l, lambda i: (i, 0))
    return _call(body, name, (t // tm,), [_row_spec(tm, d)] + ins, outs,
                 [_sds((t, d), F32)] + [_sds((1, d), F32)] * 3, deps=deps)(dout, xhat, rstd, g3)


def _glu_bwd(name, du, h):
    t, c2 = h.shape
    c = c2 // 2
    tm = min(TM_ROW, t)

    def body(du_ref, a_ref, g_ref, dh_ref, db_ref):
        first = pl.program_id(0) == 0
        du_v, a = du_ref[...], a_ref[...]
        sg = _sigmoid(g_ref[...])
        da = du_v * sg
        dg = du_v * a * sg * (1.0 - sg)
        dh_ref[:, :c] = da.astype(BF16)
        dh_ref[:, c:] = dg.astype(BF16)
        _acc_rows(db_ref.at[:, :c], _colsum(da), first)
        _acc_rows(db_ref.at[:, c:], _colsum(dg), first)

    return _call(body, name, (t // tm,),
                 [_row_spec(tm, c), pl.BlockSpec((tm, c), lambda i: (i, 0)), pl.BlockSpec((tm, c), lambda i: (i, 1))],
                 [_row_spec(tm, c2), pl.BlockSpec((1, c2), lambda i: (0, 0))],
                 [_sds((t, c2), BF16), _sds((1, c2), F32)])(du, h, h)


CONV_CB = 512
TAPS_PAD = 32


def _dwconv31(name, u, w3, b3, l, seq):
    t, c = u.shape
    tm, cb = TM_EW, CONV_CB
    seg, seq_tiles, n_tiles = tm // SUBLANES, seq // tm, t // tm
    n_wrap = CONV_K - 1
    tile, prev, _ = _tile_halo_specs(tm, lambda rows: (rows, cb), seg, n_tiles, lambda n, r: (r, n))

    def body(u_ref, halo_ref, w_ref, b_ref, o_ref, wrap_ref):
        keep = (pl.program_id(1) % seq_tiles != 0).astype(F32)
        _fill_wrap_prev(u_ref, halo_ref, wrap_ref, n_wrap, seg, seg, keep)
        _conv_fwd(u_ref, wrap_ref, w_ref, b_ref, o_ref, seg, CONV_K)

    return _call(body, name, (c // cb, n_tiles),
                 [tile, prev, pl.BlockSpec((None, CONV_K, cb), lambda n, i: (l, 0, n)),
                  pl.BlockSpec((None, 1, cb), lambda n, i: (l, 0, n))],
                 tile, _sds((t, c), F32), [pltpu.VMEM((n_wrap * SUBLANES, cb), F32)])(u, u, w3, b3)


def _dwconv31_bwd(name, dc, u, w3, l, seq):
    t, c = dc.shape
    tm, cb = TM_EW, CONV_CB
    seg, seq_tiles, n_tiles = tm // SUBLANES, seq // tm, t // tm
    n_wrap = CONV_K - 1
    tile, prev, nxt = _tile_halo_specs(tm, lambda rows: (rows, cb), seg, n_tiles, lambda n, r: (r, n))

    def body(dc_ref, dcn_ref, u_ref, up_ref, w_ref, du_ref, dw_ref, dwrap_ref, uwrap_ref):
        i = pl.program_id(1)
        keep_prev = (i % seq_tiles != 0).astype(F32)
        keep_next = (i % seq_tiles != seq_tiles - 1).astype(F32)
        _fill_wrap_next(dc_ref, dcn_ref, dwrap_ref, n_wrap, keep_next)
        _conv_bwd_data(dc_ref, dwrap_ref, w_ref, du_ref, seg, CONV_K)

        @pl.when(i == 0)
        def _():
            dw_ref[...] = jnp.zeros_like(dw_ref)

        _fill_wrap_prev(u_ref, up_ref, uwrap_ref, n_wrap, seg, seg, keep_prev)
        _conv_bwd_taps(dc_ref, u_ref, uwrap_ref, dw_ref, seg, CONV_K)

    wrap = pltpu.VMEM((n_wrap * SUBLANES, cb), F32)
    return _call(body, name, (c // cb, n_tiles),
                 [tile, nxt, tile, prev, pl.BlockSpec((None, CONV_K, cb), lambda n, i: (l, 0, n))],
                 [tile, pl.BlockSpec((TAPS_PAD, cb), lambda n, i: (0, n))],
                 [_sds((t, c), F32), _sds((TAPS_PAD, c), F32)], [wrap, wrap])(dc, dc, u, u, w3)


def _ln_silu(name, cx, g3, b3, l):
    t, d = cx.shape
    tm = min(TM_ROW, t)

    def body(c_ref, g_ref, b_ref, o_ref):
        xhat, _ = _ln_stats(c_ref[...])
        nv = xhat * g_ref[...] + b_ref[...]
        o_ref[...] = (nv * _sigmoid(nv)).astype(BF16)

    return _call(body, name, (t // tm,), [_row_spec(tm, d), _param_spec(l, d), _param_spec(l, d)],
                 _row_spec(tm, d), _sds((t, d), BF16))(cx, g3, b3)


def _ln_silu_bwd(name, ds, cx, g3, b3, l, deps=()):
    t, d = cx.shape
    tm = min(TM_ROW, t)

    def body(ds_ref, c_ref, g_ref, b_ref, dc_ref, dg_ref, db_ref, dsum_ref):
        first = pl.program_id(0) == 0
        xhat, rstd = _ln_stats(c_ref[...])
        g = g_ref[...]
        nv = xhat * g + b_ref[...]
        sg = _sigmoid(nv)
        dn = ds_ref[...] * (sg * (1.0 + nv * (1.0 - sg)))
        dc = _ln_backward(dn * g, xhat, rstd)
        dc_ref[...] = dc
        _acc_rows(dg_ref, _colsum(dn * xhat), first)
        _acc_rows(db_ref, _colsum(dn), first)
        _acc_rows(dsum_ref, _colsum(dc), first)

    vec = pl.BlockSpec((1, d), lambda i: (0, 0))
    return _call(body, name, (t // tm,),
                 [_row_spec(tm, d), _row_spec(tm, d), _param_spec(l, d), _param_spec(l, d)],
                 [_row_spec(tm, d), vec, vec, vec],
                 [_sds((t, d), F32)] + [_sds((1, d), F32)] * 3, deps=deps)(ds, cx, g3, b3)


FFN_HALO = FFN_K - 1


def _ffn_conv(x_ref, halo_ref, wrap_ref, w_ref, b_ref, keep, seg, out_ref):
    _fill_wrap_prev(x_ref, halo_ref, wrap_ref, FFN_K - 1, FFN_HALO, seg, keep)
    _conv_fwd(x_ref, wrap_ref, w_ref, b_ref, out_ref, seg, FFN_K)


TM_FFN = 512
WRAP_ROWS = FFN_HALO * SUBLANES


def _sub_tiles(x_ref, prev_ref, next_ref, keep_prev, keep_next, n_sub):
    out = []
    for s in range(n_sub):
        tile = x_ref.at[pl.ds(s * TM_EW, TM_EW)]
        prev = prev_ref if s == 0 else x_ref.at[pl.ds(s * TM_EW - WRAP_ROWS, WRAP_ROWS)]
        nxt = next_ref if s == n_sub - 1 else x_ref.at[pl.ds((s + 1) * TM_EW, WRAP_ROWS)]
        out.append((tile, prev, keep_prev if s == 0 else 1.0, nxt, keep_next if s == n_sub - 1 else 1.0))
    return out


def _rows(ref, s, rows):
    return ref.at[pl.ds(s * rows, rows)]


def _ffn_specs(tm, fb, n_tiles):
    return _tile_halo_specs(tm, lambda rows: (None, rows, fb), FFN_HALO, n_tiles, lambda n, r: (n, r, 0))


def _ffn_up_act(name, x, w_up, b_up4, wdw, bdw, l, seq):
    t, d = x.shape
    nb, fb, _ = w_up.shape
    half = nb // 2
    tm = min(TM_FFN, seq)
    n_sub, seg, seq_steps, n_steps = tm // TM_EW, TM_EW // SUBLANES, seq // tm, t // tm
    per = tm // WRAP_ROWS
    nt_dims = (_DIMS["nt"], ((), ()))

    def body(x_ref, xp_ref, ug_ref, uv_ref, bug_ref, buv_ref, wg_ref, wv_ref, bg_ref, bv_ref,
             hg_ref, hv_ref, cg_ref, cv_ref, a_ref, pg_ref, pv_ref, gwrap_ref, vwrap_ref):
        keep = (pl.program_id(1) % seq_steps != 0).astype(F32)
        xb, xpb = x_ref[...].astype(BF16), xp_ref[...].astype(BF16)
        hg_ref[...] = lax.dot_general(xb, ug_ref[...], nt_dims, preferred_element_type=F32) + bug_ref[...]
        pg_ref[...] = lax.dot_general(xpb, ug_ref[...], nt_dims, preferred_element_type=F32) + bug_ref[...]
        for s, (tile, prev, kp, _, _) in enumerate(_sub_tiles(hg_ref, pg_ref, None, keep, None, n_sub)):
            _ffn_conv(tile, prev, gwrap_ref, wg_ref, bg_ref, kp, seg, _rows(cg_ref, s, TM_EW))
        hv_ref[...] = lax.dot_general(xb, uv_ref[...], nt_dims, preferred_element_type=F32) + buv_ref[...]
        pv_ref[...] = lax.dot_general(xpb, uv_ref[...], nt_dims, preferred_element_type=F32) + buv_ref[...]
        for s, (tile, prev, kp, _, _) in enumerate(_sub_tiles(hv_ref, pv_ref, None, keep, None, n_sub)):
            _ffn_conv(tile, prev, vwrap_ref, wv_ref, bv_ref, kp, seg, _rows(cv_ref, s, TM_EW))
        cg = cg_ref[...]
        a_ref[...] = (cg * _sigmoid(cg) * cv_ref[...]).astype(BF16)

    def blk(shift):
        return pl.BlockSpec((None, fb, d), lambda n, i: (n + shift, 0, 0))

    def vec(shift, rows):
        return pl.BlockSpec((None, None, rows, fb), lambda n, i: (l, n + shift, 0, 0))

    out = pl.BlockSpec((None, tm, fb), lambda n, i: (n, i, 0))
    halo = pltpu.VMEM((WRAP_ROWS, fb), F32)
    return _call(body, name, (half, n_steps),
                 [pl.BlockSpec((tm, d), lambda n, i: (i, 0)),
                  pl.BlockSpec((WRAP_ROWS, d), lambda n, i: (jnp.maximum(i * per - 1, 0), 0)),
                  blk(0), blk(half), vec(0, 1), vec(half, 1), vec(0, FFN_K), vec(half, FFN_K), vec(0, 1), vec(half, 1)],
                 [out] * 5, [_sds((half, t, fb), F32)] * 4 + [_sds((half, t, fb), BF16)],
                 [halo, halo, halo, halo])(x, x, w_up, w_up, b_up4, b_up4, wdw, wdw, bdw, bdw)


def _ffn_act_bwd(name, dy, w_down, hg, hv, cg, cv, seq, deps=()):
    half, t, fb = hg.shape
    d = dy.shape[-1]
    tm = min(TM_FFN, seq)
    n_sub, seg, seq_steps, n_steps = tm // TM_EW, TM_EW // SUBLANES, seq // tm, t // tm
    tile, prev, _ = _ffn_specs(tm, fb, n_steps)

    def body(dy_ref, wd_ref, g_ref, gp_ref, v_ref, vp_ref, cg_ref, cv_ref,
             dg_ref, dv_ref, dbg_ref, dbv_ref, dwg_ref, dwv_ref, gwrap_ref, vwrap_ref):
        i = pl.program_id(1)
        first = i == 0
        keep = (i % seq_steps != 0).astype(F32)
        da = lax.dot_general(dy_ref[...].astype(BF16), wd_ref[...], (_DIMS["nt"], ((), ())),
                             preferred_element_type=F32)
        g_tiles = _sub_tiles(g_ref, gp_ref, None, keep, None, n_sub)
        v_tiles = _sub_tiles(v_ref, vp_ref, None, keep, None, n_sub)
        for s in range(n_sub):
            _fill_wrap_prev(g_tiles[s][0], g_tiles[s][1], _rows(gwrap_ref, s, WRAP_ROWS), FFN_K - 1, FFN_HALO, seg,
                            g_tiles[s][2])
            _fill_wrap_prev(v_tiles[s][0], v_tiles[s][1], _rows(vwrap_ref, s, WRAP_ROWS), FFN_K - 1, FFN_HALO, seg,
                            v_tiles[s][2])
        cg, cv = cg_ref[...], cv_ref[...]
        sg = _sigmoid(cg)
        dcv = da * cg * sg
        dcg = da * cv * sg * (1.0 + cg * (1.0 - sg))
        dg_ref[...] = dcg
        dv_ref[...] = dcv
        _acc_rows(dbg_ref, _colsum(dcg), first)
        _acc_rows(dbv_ref, _colsum(dcv), first)

        @pl.when(first)
        def _():
            dwg_ref[...] = jnp.zeros_like(dwg_ref)
            dwv_ref[...] = jnp.zeros_like(dwv_ref)

        for s in range(n_sub):
            _conv_bwd_taps(_rows(dg_ref, s, TM_EW), g_tiles[s][0], _rows(gwrap_ref, s, WRAP_ROWS), dwg_ref, seg, FFN_K)
            _conv_bwd_taps(_rows(dv_ref, s, TM_EW), v_tiles[s][0], _rows(vwrap_ref, s, WRAP_ROWS), dwv_ref, seg, FFN_K)

    def acc(rows):
        return pl.BlockSpec((None, rows, fb), lambda n, i: (n, 0, 0))

    wrap = pltpu.VMEM((n_sub * WRAP_ROWS, fb), F32)
    return _call(body, name, (half, n_steps),
                 [pl.BlockSpec((tm, d), lambda n, i: (i, 0)), pl.BlockSpec((None, fb, d), lambda n, i: (n, 0, 0)),
                  tile, prev, tile, prev, tile, tile],
                 [tile, tile, acc(1), acc(1), acc(SUBLANES), acc(SUBLANES)],
                 [_sds((half, t, fb), F32), _sds((half, t, fb), F32), _sds((half, 1, fb), F32),
                  _sds((half, 1, fb), F32), _sds((half, SUBLANES, fb), F32), _sds((half, SUBLANES, fb), F32)],
                 [wrap, wrap], deps=deps)(dy, w_down, hg, hg, hv, hv, cg, cv)


def _ffn_conv_t_dx(name, dcg, dcv, wdw, w_up, res, xhat, rstd, g3, l, seq, deps=()):
    half, t, fb = dcg.shape
    nb, d = 2 * half, res.shape[-1]
    tm = min(TM_FFN, seq)
    n_sub, seg, seq_steps, n_steps = tm // TM_EW, TM_EW // SUBLANES, seq // tm, t // tm
    per = tm // WRAP_ROWS

    def body(g_ref, gn_ref, v_ref, vn_ref, w_ref, up_ref, res_ref, xh_ref, rs_ref, gam_ref,
             dh_ref, db_ref, dr_ref, dgam_ref, dbeta_ref, dsum_ref, wrap_ref, out_ref, acc_ref):
        i, n = pl.program_id(0), pl.program_id(1)
        keep = (i % seq_steps != seq_steps - 1).astype(F32)

        def conv_t(d_ref, dn_ref):
            for s, (sub, _, _, nx, kn) in enumerate(_sub_tiles(d_ref, None, dn_ref, None, keep, n_sub)):
                _fill_wrap_next(sub, nx, wrap_ref, FFN_K - 1, kn)
                _conv_bwd_data(sub, wrap_ref, w_ref, _rows(out_ref, s, TM_EW), seg, FFN_K)

        @pl.when(n < half)
        def _():
            conv_t(g_ref, gn_ref)

        @pl.when(n >= half)
        def _():
            conv_t(v_ref, vn_ref)

        dh = out_ref[...]
        dhb = dh.astype(BF16)
        dh_ref[...] = dhb
        _acc_rows(db_ref.at[n], _colsum(dh), i == 0)
        p = jnp.dot(dhb, up_ref[...], preferred_element_type=F32)

        @pl.when(n == 0)
        def _():
            acc_ref[...] = p

        @pl.when(n > 0)
        def _():
            acc_ref[...] += p

        @pl.when(n == nb - 1)
        def _():
            _ln_bwd_rows(acc_ref[...] + ALPHA * res_ref[...], xh_ref, rs_ref, gam_ref, dr_ref, dgam_ref, dbeta_ref,
                         dsum_ref, i == 0)

    def src(gate):
        def blk(n):
            return jnp.minimum(n, half - 1) if gate else jnp.maximum(n - half, 0)
        tile = pl.BlockSpec((None, tm, fb), lambda i, n: (blk(n), i, 0))
        nxt = pl.BlockSpec((None, WRAP_ROWS, fb),
                           lambda i, n: (blk(n), jnp.minimum((i + 1) * per, n_steps * per - 1), 0))
        return [tile, nxt]

    row = pl.BlockSpec((tm, d), lambda i, n: (i, 0))
    ln_ins, ln_outs = _ln_bwd_specs(tm, d, l, lambda i, n: (i, 0))
    tmp = pltpu.VMEM((tm, fb), F32)
    halo = pltpu.VMEM((WRAP_ROWS, fb), F32)
    return _call(body, name, (n_steps, nb),
                 src(True) + src(False) +
                 [pl.BlockSpec((None, None, FFN_K, fb), lambda i, n: (l, n, 0, 0)),
                  pl.BlockSpec((None, fb, d), lambda i, n: (n, 0, 0)), row] + ln_ins,
                 [pl.BlockSpec((None, tm, fb), lambda i, n: (n, i, 0)),
                  pl.BlockSpec((nb, 1, fb), lambda i, n: (0, 0, 0))] + ln_outs,
                 [_sds((nb, t, fb), BF16), _sds((nb, 1, fb), F32), _sds((t, d), F32)] + [_sds((1, d), F32)] * 3,
                 [halo, tmp, pltpu.VMEM((tm, d), F32)],
                 deps=deps)(dcg, dcg, dcv, dcv, wdw, w_up, res, xhat, rstd, g3)


def _gelu_parts(h):
    cdf = 0.5 * (1.0 + lax.erf(h * INV_SQRT2))
    return h * cdf, cdf


def _seg_axis(a, axis, fn):
    return jnp.moveaxis(fn(jnp.moveaxis(a, axis, 0), TM_EW), 0, axis)


def _sgu_operands(w_s, b_s):
    nl = w_s.shape[0]
    n_sub = TM_EW // CHUNK
    tril = jnp.tril(jnp.ones((CHUNK, CHUNK), dtype=bool))
    w_causal = jnp.where(tril, w_s, 0.0)
    w_tile = (jnp.eye(n_sub, dtype=F32)[None, None, :, None, :, None] * w_causal[:, :, None, :, None, :]).reshape(
        nl, GROUPS, TM_EW, TM_EW)
    w_tile = _seg_axis(_seg_axis(w_tile, 2, _to_segments), 3, _to_segments).astype(BF16)
    bs_tile = jnp.broadcast_to(b_s[:, :, None, :, None], (nl, GROUPS, n_sub, CHUNK, CHUNK)).reshape(
        nl, GROUPS, TM_EW, CHUNK)
    return w_tile, _seg_axis(bs_tile, 2, _to_segments)


def _sgu_param_grads(dwt, dbt):
    n_sub = TM_EW // CHUNK
    tril = jnp.tril(jnp.ones((CHUNK, CHUNK), dtype=bool))
    dwt = _seg_axis(_seg_axis(dwt, 1, _from_segments), 2, _from_segments).reshape(GROUPS, n_sub, CHUNK, n_sub, CHUNK)
    dw = sum(dwt[:, a, :, a, :] for a in range(n_sub))
    db = _seg_axis(dbt, 1, _from_segments).reshape(GROUPS, n_sub, CHUNK).sum(axis=1)
    return jnp.where(tril, dw, 0.0), db


def _sgu(name, h, g3, b3, wt, bst, l):
    t, c2 = h.shape
    c = c2 // 2
    tm = TM_EW

    def body(h_ref, g_ref, b_ref, wt_ref, bs_ref, o_ref):
        z, _ = _gelu_parts(h_ref[...])
        u = z[:, :c]
        xhat, _ = _ln_stats(z[:, c:])
        vnb = (xhat * g_ref[...] + b_ref[...]).astype(BF16)
        for gi in range(GROUPS):
            cs = slice(gi * CHUNK, (gi + 1) * CHUNK)
            sp = jnp.dot(wt_ref[gi], vnb[:, cs], preferred_element_type=F32) + bs_ref[gi]
            o_ref[:, cs] = (u[:, cs] * sp).astype(BF16)

    return _call(body, name, (t // tm,),
                 [_row_spec(tm, c2), _param_spec(l, c), _param_spec(l, c),
                  pl.BlockSpec((None, GROUPS, tm, tm), lambda i: (l, 0, 0, 0)),
                  pl.BlockSpec((None, GROUPS, tm, CHUNK), lambda i: (l, 0, 0, 0))],
                 _row_spec(tm, c), _sds((t, c), BF16))(h, g3, b3, wt, bst)


def _sgu_bwd(name, dq, h, g3, b3, wt, bst, l, deps=()):
    t, c2 = h.shape
    c = c2 // 2
    tm = TM_EW
    n_tiles = t // tm

    def body(dq_ref, h_ref, g_ref, b_ref, wt_ref, bs_ref,
             dh_ref, dbin_ref, dw_ref, dbs_ref, dg_ref, db_ref, du_ref, dvn_ref, bsum_ref):
        i = pl.program_id(0)
        first = i == 0
        hv = h_ref[...]
        z, cdf = _gelu_parts(hv)
        u = z[:, :c]
        xhat, rstd = _ln_stats(z[:, c:])
        g = g_ref[...]
        vnb = (xhat * g + b_ref[...]).astype(BF16)

        @pl.when(first)
        def _():
            dw_ref[...] = jnp.zeros_like(dw_ref)
            bsum_ref[...] = jnp.zeros_like(bsum_ref)

        for gi in range(GROUPS):
            cs = slice(gi * CHUNK, (gi + 1) * CHUNK)
            vb = vnb[:, cs]
            w = wt_ref[gi]
            sp = jnp.dot(w, vb, preferred_element_type=F32) + bs_ref[gi]
            dqb = dq_ref[:, cs]
            du_ref[:, cs] = dqb * sp
            dsp = dqb * u[:, cs]
            bsum_ref[gi] += dsp
            dspb = dsp.astype(BF16)
            dw_ref[gi] += lax.dot_general(dspb, vb, (_DIMS["nt"], ((), ())), preferred_element_type=F32)
            dvn_ref[:, cs] = lax.dot_general(w, dspb, (_DIMS["tn"], ((), ())), preferred_element_type=F32)

        dvn = dvn_ref[...]
        dv = _ln_backward(dvn * g, xhat, rstd)
        pdf = jnp.exp(-0.5 * hv * hv) * INV_SQRT2PI
        dgelu = cdf + hv * pdf
        dhu = du_ref[...] * dgelu[:, :c]
        dhv = dv * dgelu[:, c:]
        dh_ref[:, :c] = dhu.astype(BF16)
        dh_ref[:, c:] = dhv.astype(BF16)
        _acc_rows(dbin_ref.at[:, :c], _colsum(dhu), first)
        _acc_rows(dbin_ref.at[:, c:], _colsum(dhv), first)
        _acc_rows(dg_ref, _colsum(dvn * xhat), first)
        _acc_rows(db_ref, _colsum(dvn), first)

        @pl.when(i == n_tiles - 1)
        def _():
            dbs_ref[...] = jnp.sum(bsum_ref[...], axis=-1)

    vec = pl.BlockSpec((1, c), lambda i: (0, 0))
    return _call(body, name, (n_tiles,),
                 [_row_spec(tm, c), _row_spec(tm, c2), _param_spec(l, c), _param_spec(l, c),
                  pl.BlockSpec((None, GROUPS, tm, tm), lambda i: (l, 0, 0, 0)),
                  pl.BlockSpec((None, GROUPS, tm, CHUNK), lambda i: (l, 0, 0, 0))],
                 [_row_spec(tm, c2), pl.BlockSpec((1, c2), lambda i: (0, 0)),
                  pl.BlockSpec((GROUPS, tm, tm), lambda i: (0, 0, 0)),
                  pl.BlockSpec((GROUPS, tm), lambda i: (0, 0)), vec, vec],
                 [_sds((t, c2), BF16), _sds((1, c2), F32), _sds((GROUPS, tm, tm), F32),
                  _sds((GROUPS, tm), F32), _sds((1, c), F32), _sds((1, c), F32)],
                 [pltpu.VMEM((tm, c), F32), pltpu.VMEM((tm, c), F32), pltpu.VMEM((GROUPS, tm, CHUNK), F32)],
                 deps=deps)(dq, h, g3, b3, wt, bst)


def _loss(name, y, target):
    t, d = y.shape
    tm = min(TM_ROW, t)
    n_tiles = t // tm

    def body(y_ref, t_ref, l_ref, dy_ref, acc_ref):
        i = pl.program_id(0)
        diff = y_ref[...] - t_ref[...]
        dy_ref[...] = diff * (1.0 / d)
        _acc_rows(acc_ref, _colsum(diff * diff), i == 0)

        @pl.when(i == n_tiles - 1)
        def _():
            l_ref[...] = jnp.broadcast_to(jnp.sum(acc_ref[...], axis=-1, keepdims=True) * (0.5 / d), (1, LANES))

    return _call(body, name, (n_tiles,), [_row_spec(tm, d), _row_spec(tm, d)],
                 [pl.BlockSpec((1, LANES), lambda i: (0, 0)), _row_spec(tm, d)],
                 [_sds((1, LANES), F32), _sds((t, d), F32)], [pltpu.VMEM((1, d), F32)])(y, target)


def _adamw(g, w, m, v):
    m2 = ADAM_B1 * m + (1.0 - ADAM_B1) * g
    v2 = ADAM_B2 * v + (1.0 - ADAM_B2) * (g * g)
    m_hat = m2 / (1.0 - ADAM_B1 ** ADAM_STEP)
    v_hat = v2 / (1.0 - ADAM_B2 ** ADAM_STEP)
    delta = -ADAM_LR * (m_hat / (jnp.sqrt(v_hat) + ADAM_EPS) + ADAM_WD * w)
    return delta, m2, v2


ROW_TILE_CAP = 512


def _row_tile(rows, cap=ROW_TILE_CAP):
    if rows <= cap:
        return rows
    for tr in range(cap, 15, -16):
        if rows % tr == 0:
            return tr
    return rows


def _sum8_adamw(name, dev, lands, parts, w, m, v):
    nl = len(lands)
    _, r, c = lands[0].shape
    tr = _row_tile(r, cap=128)

    def body(dev_ref, *refs):
        land, own = refs[:nl], refs[nl:2 * nl]
        w_ref, m_ref, v_ref, g_ref, d_ref, m2_ref, v2_ref = refs[2 * nl:]
        layer, me = pl.program_id(0), dev_ref[0]
        for l in range(nl):
            @pl.when(layer == l)
            def _(l=l):
                g = None
                for s in range(N_DEV):
                    part = jnp.where(me == s, own[l][...], land[l][s]).astype(F32)
                    g = part if g is None else g + part
                delta, m2, v2 = _adamw(g, w_ref[...], m_ref[...], v_ref[...])
                g_ref[...] = g
                d_ref[...] = delta
                m2_ref[...] = m2
                v2_ref[...] = v2

    def rows_of(l, a, i):
        return jnp.where(a == l, i, 0)

    spec = pl.BlockSpec((None, tr, c), lambda a, i, dev_ref: (a, i, 0))
    in_specs = [pl.BlockSpec((N_DEV, tr, c), lambda a, i, dev_ref, l=l: (0, rows_of(l, a, i), 0)) for l in range(nl)]
    in_specs += [pl.BlockSpec((None, tr, c), lambda a, i, dev_ref, l=l: (dev_ref[0], rows_of(l, a, i), 0))
                 for l in range(nl)]
    grid_spec = pltpu.PrefetchScalarGridSpec(
        num_scalar_prefetch=1, grid=(nl, r // tr), in_specs=in_specs + [spec] * 3, out_specs=[spec] * 4)
    return pl.pallas_call(
        body, name=name, grid_spec=grid_spec, out_shape=[_sds(w.shape, F32)] * 4,
        compiler_params=pltpu.CompilerParams(vmem_limit_bytes=VMEM_LIMIT))(dev, *lands, *parts, w, m, v)


def _sum8(name, parts):
    _, r, c = parts.shape
    tr = _row_tile(r)

    def body(p_ref, o_ref):
        acc = p_ref[0]
        for s in range(1, N_DEV):
            acc = acc + p_ref[s]
        o_ref[...] = acc

    return _call(body, name, (r // tr,), [pl.BlockSpec((N_DEV, tr, c), lambda i: (0, i, 0))],
                 pl.BlockSpec((tr, c), lambda i: (i, 0)), _sds((r, c), F32))(parts)


def _adamw_small(name, gs, ws, ms, vs):
    n = len(gs)

    def body(*refs):
        g, w, m, v = (refs[k * n:(k + 1) * n] for k in range(4))
        d_out, m_out, v_out = (refs[(4 + k) * n:(5 + k) * n] for k in range(3))
        for k in range(n):
            d_out[k][...], m_out[k][...], v_out[k][...] = _adamw(g[k][...], w[k][...], m[k][...], v[k][...])

    vmem = pl.BlockSpec(memory_space=pltpu.VMEM)
    outs = _call(body, name, (), [vmem] * (4 * n), [vmem] * (3 * n), [_sds(w.shape, F32) for w in ws] * 3)(
        *gs, *ws, *ms, *vs)
    return outs[:n], outs[n:2 * n], outs[2 * n:]


def _pack(arrs, row_multiple=SUBLANES):
    pieces, rows = [], 0
    for a in arrs:
        piece = a.reshape(-1, LANES)
        piece = jnp.pad(piece, ((0, (-piece.shape[0]) % SUBLANES), (0, 0)))
        pieces.append(piece)
        rows += piece.shape[0]
    if rows % row_multiple:
        pieces.append(jnp.zeros(((-rows) % row_multiple, LANES), pieces[0].dtype))
    return jnp.concatenate(pieces, axis=0)


def _unpack(buf, shapes, lead=0):
    out, pos = [], 0
    for shp in shapes:
        rows = math.prod(shp) // LANES
        piece = lax.slice_in_dim(buf, pos, pos + rows, axis=lead)
        out.append(piece.reshape(buf.shape[:lead] + tuple(shp)))
        pos += rows + (-rows) % SUBLANES
    return out


REPLICATED = ["conv_b_in", "conv_b_dw", "conv_ln_g", "conv_ln_b", "conv_b_out", "gmlp_w_s", "gmlp_b_s",
              "ffn_b_up", "ffn_b_dw", "ffn_b_down", "norm1_g", "norm1_b", "norm2_g", "norm2_b"]
SMALL_SHARDED = ["conv_w_dw", "gmlp_b_in", "gmlp_ln_g", "gmlp_ln_b", "gmlp_b_out", "ffn_w_dw"]
BIG = ["conv_w_in", "conv_w_out", "gmlp_w_in", "gmlp_w_out", "ffn_w_up", "ffn_w_down"]
WEIGHTS = ["conv_w_in", "conv_b_in", "conv_w_dw", "conv_b_dw", "conv_ln_g", "conv_ln_b", "conv_w_out", "conv_b_out",
           "gmlp_w_in", "gmlp_b_in", "gmlp_ln_g", "gmlp_ln_b", "gmlp_w_s", "gmlp_b_s", "gmlp_w_out", "gmlp_b_out",
           "ffn_w_up", "ffn_b_up", "ffn_w_dw", "ffn_b_dw", "ffn_w_down", "ffn_b_down",
           "norm1_g", "norm1_b", "norm2_g", "norm2_b"]


def _from_shards(g, lead_shape):
    nd = len(lead_shape)
    perm = tuple(range(1, nd + 1)) + (0, nd + 1)
    return g.transpose(perm).reshape(tuple(lead_shape) + (-1,))


def _to_shards(full, width):
    lead = full.shape[:-1]
    nd = len(lead)
    parts = full.reshape(lead + (N_DEV, width))
    return parts.transpose((nd,) + tuple(range(nd)) + (nd + 1,))


def _step(p):
    x_in, target_in = p["x"], p["loss_target"]
    bsz, seq, d = x_in.shape
    t = bsz * seq
    assert seq % TM_EW == 0 and TM_EW % CHUNK == 0 and TM_EW // SUBLANES >= CONV_K - 1
    x0 = _to_segments(x_in.reshape(t, d), TM_EW)
    target = _to_segments(target_in.reshape(t, d), TM_EW)
    n_conv, n_gmlp = p["conv_w_in"].shape[0], p["gmlp_w_in"].shape[0]
    fb = p["ffn_w_up"].shape[-1]
    nblk = N_DEV
    half = nblk // 2
    cw = p["conv_w_in"].shape[-1]
    tm = min(TM_MM, t)
    nt = t // tm
    tk = min(TK_DW, t)
    ntk = t // tk
    dev = 4 * lax.axis_index("x") + 2 * lax.axis_index("y") + lax.axis_index("c")

    small_shapes = [p[n].shape for n in SMALL_SHARDED]
    small_src = _pack([p[n] for n in SMALL_SHARDED])[None]
    small_all = _all_gather("gather_small_weights", [small_src])[0][0]
    sm = _unpack(small_all, small_shapes, lead=1)
    w_src = []
    for i in range(DEPTH):
        mix = "conv" if i % 2 == 0 else "gmlp"
        w_src += [p[mix + "_w_in"][i // 2].astype(BF16), p[mix + "_w_out"][i // 2].astype(BF16),
                  p["ffn_w_up"][i].T.astype(BF16), p["ffn_w_down"][i].astype(BF16)]
    send_sems, recv_sems, w_land, _ = _gather_start(
        "weights_gather_start", _place_own("weights_place_own", w_src, deps=[small_all]))
    W_IN, W_OUT, W_UP, W_DOWN = range(4)

    def wait_weight(i, k, after):
        return _gather_wait(f"l{i}_weights_wait{k}", w_land[4 * i + k], send_sems, recv_sems, 4 * i + k, after)
    conv_w_dw = _from_shards(sm[0], sm[0].shape[1:-1])
    gmlp_b_in = _from_shards(sm[1], sm[1].shape[1:-1])
    gmlp_ln_g = _from_shards(sm[2], sm[2].shape[1:-1])
    gmlp_ln_b = _from_shards(sm[3], sm[3].shape[1:-1])
    gmlp_b_out = _from_shards(sm[4], sm[4].shape[1:-1])
    ffn_w_dw = sm[5].transpose(1, 0, 2, 3)

    def rows3(a):
        return a.reshape(a.shape[0], 1, a.shape[-1])

    conv_b_in4 = p["conv_b_in"].reshape(n_conv, N_DEV, 1, cw)
    gmlp_b_in4 = gmlp_b_in.reshape(n_gmlp, N_DEV, 1, cw)
    ffn_b_up4 = p["ffn_b_up"].reshape(DEPTH, nblk, 1, fb)
    ffn_b_dw4 = p["ffn_b_dw"].reshape(DEPTH, nblk, 1, fb)
    conv_b_dw3, conv_ln_g3, conv_ln_b3 = rows3(p["conv_b_dw"]), rows3(p["conv_ln_g"]), rows3(p["conv_ln_b"])
    conv_b_out3, gmlp_b_out3, ffn_b_down3 = rows3(p["conv_b_out"]), rows3(gmlp_b_out), rows3(p["ffn_b_down"])
    gmlp_ln_g3, gmlp_ln_b3 = rows3(gmlp_ln_g), rows3(gmlp_ln_b)
    n1g3, n1b3, n2g3, n2b3 = rows3(p["norm1_g"]), rows3(p["norm1_b"]), rows3(p["norm2_g"]), rows3(p["norm2_b"])
    w_tile, bs_tile = _sgu_operands(p["gmlp_w_s"], p["gmlp_b_s"])

    def mm_in(name, xa, wg, l, bias4, glu=False):
        tmi = min(TM_LN, t)
        c_half = half * cw

        def body(a_ref, b_ref, bias_ref, h_ref, *u_ref):
            xb = a_ref[...].astype(BF16)
            for n in range(N_DEV):
                h_ref[:, n * cw:(n + 1) * cw] = jnp.dot(xb, b_ref[n], preferred_element_type=F32) + bias_ref[n]
            if glu:
                u_ref[0][...] = h_ref[:, :c_half] * _sigmoid(h_ref[:, c_half:])

        outs = _call(body, name, (t // tmi,),
                     [pl.BlockSpec((tmi, d), lambda i: (i, 0)), pl.BlockSpec((N_DEV, d, cw), lambda i: (0, 0, 0)),
                      pl.BlockSpec((None, N_DEV, 1, cw), lambda i: (l, 0, 0, 0))],
                     [pl.BlockSpec((tmi, N_DEV * cw), lambda i: (i, 0))]
                     + ([pl.BlockSpec((tmi, c_half), lambda i: (i, 0))] if glu else []),
                     [_sds((t, N_DEV * cw), F32)] + ([_sds((t, c_half), F32)] if glu else []))(xa, wg, bias4)
        return outs if glu else outs[0]

    def mm_out_dx(name, dy, w, deps=()):
        return _matmul(name, dy, w, "nt", grid=(nt,),
                       a_spec=pl.BlockSpec((tm, d), lambda i: (i, 0)),
                       b_spec=pl.BlockSpec((d, d), lambda i: (0, 0)),
                       o_spec=pl.BlockSpec((tm, d), lambda i: (i, 0)), o_shape=(t, d), o_dtype=F32, deps=deps)

    def mm_out_dw(name, sa, dy):
        return _matmul(name, sa, dy, "tn", grid=(nt,), k_axis=0, nk=nt, acc_shape=(d, d),
                       a_spec=pl.BlockSpec((tm, d), lambda k: (k, 0)),
                       b_spec=pl.BlockSpec((tm, d), lambda k: (k, 0)),
                       o_spec=pl.BlockSpec((d, d), lambda k: (0, 0)), o_shape=(d, d), o_dtype=BF16)

    def mm_in_dx(name, dh, wg, res):
        tmx = min(TM_LN, t)

        def body(a_ref, b_ref, res_ref, o_ref):
            y = ALPHA * res_ref[...]
            for n in range(N_DEV):
                y = y + lax.dot_general(a_ref[:, n * cw:(n + 1) * cw], b_ref[n], (_DIMS["nt"], ((), ())),
                                        preferred_element_type=F32)
            o_ref[...] = y

        row = pl.BlockSpec((tmx, d), lambda i: (i, 0))
        return _call(body, name, (t // tmx,),
                     [pl.BlockSpec((tmx, N_DEV * cw), lambda i: (i, 0)),
                      pl.BlockSpec((N_DEV, d, cw), lambda i: (0, 0, 0)), row], row, _sds((t, d), F32))(dh, wg, res)

    def mm_in_dw(name, xa, dh):
        def body(a_ref, b_ref, o_ref, acc_ref):
            k = pl.program_id(1)
            p = lax.dot_general(a_ref[...].astype(BF16), b_ref[...], (_DIMS["tn"], ((), ())),
                                preferred_element_type=F32)
            _acc_rows(acc_ref, p, k == 0)

            @pl.when(k == nt - 1)
            def _():
                for n in range(half):
                    o_ref[n] = acc_ref[:, n * cw:(n + 1) * cw].astype(BF16)

        return _call(body, name, (2, nt),
                     [pl.BlockSpec((tm, d), lambda c, k: (k, 0)), pl.BlockSpec((tm, half * cw), lambda c, k: (k, c))],
                     pl.BlockSpec((half, d, cw), lambda c, k: (c, 0, 0)), _sds((N_DEV, d, cw), BF16),
                     [pltpu.VMEM((d, half * cw), F32)])(xa, dh)

    def mm_down_dw(name, a, dy, deps=()):
        return _matmul(name, a, dy, "tn", grid=(half, ntk), k_axis=1, nk=ntk, acc_shape=(fb, d),
                       a_spec=pl.BlockSpec((None, tk, fb), lambda n, k: (n, k, 0)),
                       b_spec=pl.BlockSpec((tk, d), lambda n, k: (k, 0)),
                       o_spec=pl.BlockSpec((None, fb, d), lambda n, k: (n, 0, 0)),
                       o_shape=(half, fb, d), o_dtype=BF16, deps=deps)

    def mm_up_dw(name, xa, dh):
        return _matmul(name, dh, xa, "tn", grid=(nblk, ntk), k_axis=1, nk=ntk, acc_shape=(fb, d),
                       a_spec=pl.BlockSpec((None, tk, fb), lambda n, k: (n, k, 0)),
                       b_spec=pl.BlockSpec((tk, d), lambda n, k: (k, 0)),
                       o_spec=pl.BlockSpec((None, fb, d), lambda n, k: (n, 0, 0)),
                       o_shape=(nblk, fb, d), o_dtype=BF16)

    saved = []
    xcur = x0
    for i in range(DEPTH):
        j = i // 2
        s = {"x": xcur}
        s["w_in"] = wait_weight(i, W_IN, xcur)
        if i % 2 == 0:
            s["h"], s["u"] = mm_in(f"l{i}_conv_in_glu", xcur, s["w_in"], j, conv_b_in4, glu=True)
            s["c"] = _dwconv31(f"l{i}_dwconv", s["u"], conv_w_dw, conv_b_dw3, j, seq)
            s["s"] = _ln_silu(f"l{i}_ln_silu", s["c"], conv_ln_g3, conv_ln_b3, j)
            b_out3 = conv_b_out3
        else:
            s["h"] = mm_in(f"l{i}_gmlp_in", xcur, s["w_in"], j, gmlp_b_in4)
            s["s"] = _sgu(f"l{i}_sgu", s["h"], gmlp_ln_g3, gmlp_ln_b3, w_tile, bs_tile, j)
            b_out3 = gmlp_b_out3
        s["w_out"] = wait_weight(i, W_OUT, s["s"]).reshape(d, d)
        s["x1"], s["xhat1"], s["rstd1"] = _matmul_ln(
            f"l{i}_mixer_out_norm1", s["s"], s["w_out"], xcur, b_out3, n1g3, n1b3, j, i)
        s["w_up"] = wait_weight(i, W_UP, s["x1"])
        s["hg"], s["hv"], s["cg"], s["cv"], s["a"] = _ffn_up_act(
            f"l{i}_ffn_up_act", s["x1"], s["w_up"], ffn_b_up4, ffn_w_dw, ffn_b_dw4, i, seq)
        s["w_down"] = wait_weight(i, W_DOWN, s["a"]).reshape(half, fb, d)
        xcur, s["xhat2"], s["rstd2"] = _matmul_ln(
            f"l{i}_ffn_down_norm2", s["a"], s["w_down"], s["x1"], ffn_b_down3, n2g3, n2b3, i, i)
        saved.append(s)

    loss_row, dx = _loss("loss", xcur, target)

    started = {n: [None] * p[n].shape[0] for n in BIG}
    tokens = []

    def send_grads(name, items):
        done, token = _scatter_start(name, [g for _, _, g in items])
        for (n, l, _), st in zip(items, done):
            started[n][l] = st
        tokens.append(token)

    def take_tokens():
        out = list(tokens)
        tokens.clear()
        return out

    gl = {n: [None] * p[n].shape[0] for n in REPLICATED + SMALL_SHARDED}
    dr2, gl["norm2_g"][DEPTH - 1], gl["norm2_b"][DEPTH - 1], gl["ffn_b_down"][DEPTH - 1] = _ln_res_bwd(
        f"l{DEPTH - 1}_norm2_bwd", dx, saved[-1]["xhat2"], saved[-1]["rstd2"], n2g3, DEPTH - 1)
    for i in reversed(range(DEPTH)):
        j = i // 2
        s = saved[i]
        mix = "conv" if i % 2 == 0 else "gmlp"
        g_down = mm_down_dw(f"l{i}_ffn_down_dw", s["a"], dr2, deps=take_tokens()).reshape(N_DEV, -1, d)
        send_grads(f"l{i}_ffn_down_grad_scatter_start", [("ffn_w_down", i, g_down)])
        dcg, dcv, dbg, dbv, dwg, dwv = _ffn_act_bwd(f"l{i}_ffn_act_bwd", dr2, s["w_down"], s["hg"], s["hv"],
                                                    s["cg"], s["cv"], seq, deps=take_tokens())
        gl["ffn_b_dw"][i] = jnp.concatenate([dbg, dbv], axis=0).reshape(1, nblk * fb)
        gl["ffn_w_dw"][i] = jnp.concatenate([dwg[:, :FFN_K], dwv[:, :FFN_K]], axis=0)
        dh, dbu, dr1, gl["norm1_g"][i], gl["norm1_b"][i], gl[mix + "_b_out"][j] = _ffn_conv_t_dx(
            f"l{i}_ffn_conv_t_dx", dcg, dcv, ffn_w_dw, s["w_up"], dr2, s["xhat1"], s["rstd1"], n1g3, i, seq,
            deps=take_tokens())
        gl["ffn_b_up"][i] = dbu.reshape(1, nblk * fb)
        send_grads(f"l{i}_ffn_up_grad_scatter_start", [("ffn_w_up", i, mm_up_dw(f"l{i}_ffn_up_dw", s["x1"], dh))])
        ds = mm_out_dx(f"l{i}_{mix}_out_dx", dr1, s["w_out"], deps=take_tokens())
        g_out = mm_out_dw(f"l{i}_{mix}_out_dw", s["s"], dr1).reshape(N_DEV, -1, d)
        if i % 2 == 0:
            dc, gl["conv_ln_g"][j], gl["conv_ln_b"][j], gl["conv_b_dw"][j] = _ln_silu_bwd(
                f"l{i}_ln_silu_bwd", ds, s["c"], conv_ln_g3, conv_ln_b3, j)
            du, dwdw = _dwconv31_bwd(f"l{i}_dwconv_bwd", dc, s["u"], conv_w_dw, j, seq)
            gl["conv_w_dw"][j] = dwdw[:CONV_K]
            dh, gl["conv_b_in"][j] = _glu_bwd(f"l{i}_glu_bwd", du, s["h"])
        else:
            dh, gl["gmlp_b_in"][j], dwt, dbt, gl["gmlp_ln_g"][j], gl["gmlp_ln_b"][j] = _sgu_bwd(
                f"l{i}_sgu_bwd", ds, s["h"], gmlp_ln_g3, gmlp_ln_b3, w_tile, bs_tile, j)
            gl["gmlp_w_s"][j], gl["gmlp_b_s"][j] = _sgu_param_grads(dwt, dbt)
        dx = mm_in_dx(f"l{i}_{mix}_in_dx", dh, s["w_in"], dr1)
        if i > 0:
            prev = saved[i - 1]
            dr2, gl["norm2_g"][i - 1], gl["norm2_b"][i - 1], gl["ffn_b_down"][i - 1] = _ln_res_bwd(
                f"l{i - 1}_norm2_bwd", dx, prev["xhat2"], prev["rstd2"], n2g3, i - 1)
        send_grads(f"l{i}_mixer_grads_scatter_start",
                   [(mix + "_w_out", j, g_out), (mix + "_w_in", j, mm_in_dw(f"l{i}_{mix}_in_dw", s["x"], dh))])
    grad_x = _from_segments(dx, TM_EW).reshape(bsz, seq, d)

    full_small = {n: jnp.stack(gl[n]).reshape(p[n].shape) for n in REPLICATED}
    shard_small = {}
    for n in SMALL_SHARDED:
        if n == "ffn_w_dw":
            shard_small[n] = jnp.stack(gl[n]).transpose(1, 0, 2, 3)
        else:
            width = p[n].shape[-1]
            lead = p[n].shape[:-1]
            shard_small[n] = _to_shards(jnp.stack(gl[n]).reshape(lead + (N_DEV * width,)), width)
    flat_shapes = [(1, LANES)] + [p[n].shape for n in REPLICATED] + [(N_DEV,) + p[n].shape for n in SMALL_SHARDED]
    flat_local = _pack([loss_row] + [full_small[n] for n in REPLICATED] + [shard_small[n] for n in SMALL_SHARDED],
                       row_multiple=ROW_TILE_CAP)

    small_send, small_recv, small_land, small_token = _gather_start(
        "small_grads_gather_start", _place_own("small_grads_place_own", [flat_local]))

    grads, delta, new_m, new_v = {}, {}, {}, {}
    dev1 = jnp.reshape(dev, (1,)).astype(jnp.int32)
    order = ["ffn_w_down", "ffn_w_up", "gmlp_w_out", "gmlp_w_in", "conv_w_out", "conv_w_in"]
    after = small_token
    for n in order:
        parts_done, lands_done = _scatter_wait(f"grads_{n}_scatter_wait", started[n], after)
        state = [p[n], p["m_" + n], p["v_" + n]]
        if n == "ffn_w_up":
            state = [a.transpose(0, 2, 1) for a in state]
        outs = _sum8_adamw(f"adamw_{n}", dev1, lands_done, parts_done, *state)
        after = outs[-1]
        if n == "ffn_w_up":
            outs = [a.transpose(0, 2, 1) for a in outs]
        grads[n], delta[n], new_m[n], new_v[n] = outs

    small_parts = _gather_wait("small_grads_gather_wait", small_land[0], small_send, small_recv, 0, after)
    summed = _unpack(_sum8("sum_small_grads", small_parts), flat_shapes)
    loss = summed[0][0, 0]
    grads.update(zip(REPLICATED, summed[1:1 + len(REPLICATED)]))
    for n, g in zip(SMALL_SHARDED, summed[1 + len(REPLICATED):]):
        grads[n] = lax.dynamic_index_in_dim(g, dev, axis=0, keepdims=False)
    small = REPLICATED + SMALL_SHARDED
    d_s, m_s, v_s = _adamw_small("adamw_small", [grads[n] for n in small], [p[n] for n in small],
                                 [p["m_" + n] for n in small], [p["v_" + n] for n in small])
    for n, dd, mm, vv in zip(small, d_s, m_s, v_s):
        delta[n], new_m[n], new_v[n] = dd, mm, vv

    return (loss, grad_x, *[grads[n] for n in WEIGHTS], *[delta[n] for n in WEIGHTS],
            *[new_m[n] for n in WEIGHTS], *[new_v[n] for n in WEIGHTS])


def kernel(x, conv_w_in, conv_b_in, conv_w_dw, conv_b_dw, conv_ln_g, conv_ln_b, conv_w_out, conv_b_out, gmlp_w_in, gmlp_b_in, gmlp_ln_g, gmlp_ln_b, gmlp_w_s, gmlp_b_s, gmlp_w_out, gmlp_b_out, ffn_w_up, ffn_b_up, ffn_w_dw, ffn_b_dw, ffn_w_down, ffn_b_down, norm1_g, norm1_b, norm2_g, norm2_b, loss_target, m_conv_w_in, m_conv_b_in, m_conv_w_dw, m_conv_b_dw, m_conv_ln_g, m_conv_ln_b, m_conv_w_out, m_conv_b_out, m_gmlp_w_in, m_gmlp_b_in, m_gmlp_ln_g, m_gmlp_ln_b, m_gmlp_w_s, m_gmlp_b_s, m_gmlp_w_out, m_gmlp_b_out, m_ffn_w_up, m_ffn_b_up, m_ffn_w_dw, m_ffn_b_dw, m_ffn_w_down, m_ffn_b_down, m_norm1_g, m_norm1_b, m_norm2_g, m_norm2_b, v_conv_w_in, v_conv_b_in, v_conv_w_dw, v_conv_b_dw, v_conv_ln_g, v_conv_ln_b, v_conv_w_out, v_conv_b_out, v_gmlp_w_in, v_gmlp_b_in, v_gmlp_ln_g, v_gmlp_ln_b, v_gmlp_w_s, v_gmlp_b_s, v_gmlp_w_out, v_gmlp_b_out, v_ffn_w_up, v_ffn_b_up, v_ffn_w_dw, v_ffn_b_dw, v_ffn_w_down, v_ffn_b_down, v_norm1_g, v_norm1_b, v_norm2_g, v_norm2_b):
    return _step(dict(locals()))
```

```python
import math

import jax
import jax.numpy as jnp
from jax import lax
from jax.experimental import pallas as pl
from jax.experimental.pallas import tpu as pltpu

F32 = jnp.float32
BF16 = jnp.bfloat16
MESH = pl.DeviceIdType.MESH

N_DEV = 8
DEPTH = 4
ALPHA = (2.0 * DEPTH) ** 0.25
LN_EPS = 1e-5
CONV_K = 31
FFN_K = 3
CHUNK = 128
GROUPS = 8
ADAM_LR = 0.001
ADAM_B1 = 0.9
ADAM_B2 = 0.999
ADAM_EPS = 1e-08
ADAM_WD = 0.01
ADAM_STEP = 10
INV_SQRT2 = 1.0 / math.sqrt(2.0)
INV_SQRT2PI = 1.0 / math.sqrt(2.0 * math.pi)

LANES = 128
SUBLANES = 8
VMEM_LIMIT = 48 * 1024 * 1024
TM_MM = 1024
TK_DW = 2048
TM_EW = 256
TM_ROW = 512


def _call(body, name, grid, in_specs, out_specs, out_shape, scratch=(), aliases=None, deps=()):
    deps = list(deps)
    in_specs = list(in_specs)
    n_in = len(in_specs)
    if deps:
        inner = body

        def body(*refs):
            return inner(*refs[:n_in], *refs[n_in + len(deps):])

        in_specs = in_specs + [pl.BlockSpec(memory_space=pl.ANY)] * len(deps)
    fn = pl.pallas_call(
        body, name=name, grid=grid, in_specs=in_specs, out_specs=out_specs, out_shape=out_shape,
        scratch_shapes=list(scratch), input_output_aliases=aliases or {},
        compiler_params=pltpu.CompilerParams(vmem_limit_bytes=VMEM_LIMIT))
    return lambda *args: fn(*args, *deps)


def _sds(shape, dtype):
    return jax.ShapeDtypeStruct(tuple(shape), dtype)


def _sigmoid(x):
    return 1.0 / (1.0 + jnp.exp(-x))


def _acc_rows(ref, val, first):
    @pl.when(first)
    def _():
        ref[...] = val

    @pl.when(jnp.logical_not(first))
    def _():
        ref[...] += val


def _colsum(v):
    return jnp.sum(v, axis=0, keepdims=True)


_DIMS = {"nn": ((1,), (0,)), "nt": ((1,), (1,)), "tn": ((0,), (0,))}


def _matmul(name, a, b, mode, *, grid, a_spec, b_spec, o_spec, o_shape, o_dtype, k_axis=None, nk=1,
            acc_shape=None, bias=None, bias_spec=None, res=None, res_spec=None, res_scale=1.0, deps=()):
    dims = (_DIMS[mode], ((), ()))
    has_bias, has_res = bias is not None, res is not None

    def body(*refs):
        a_ref, b_ref = refs[0], refs[1]
        pos = 2
        bias_ref = res_ref = None
        if has_bias:
            bias_ref = refs[pos]
            pos += 1
        if has_res:
            res_ref = refs[pos]
            pos += 1
        o_ref = refs[pos]
        acc_ref = refs[pos + 1] if nk > 1 else None
        p = lax.dot_general(a_ref[...].astype(BF16), b_ref[...].astype(BF16), dims, preferred_element_type=F32)

        def finish(acc):
            if has_bias:
                acc = acc + bias_ref[...]
            if has_res:
                acc = acc + res_scale * res_ref[...]
            o_ref[...] = acc.astype(o_dtype)

        if nk == 1:
            finish(p)
        else:
            k = pl.program_id(k_axis)

            @pl.when(k == 0)
            def _():
                acc_ref[...] = p

            @pl.when(k > 0)
            def _():
                acc_ref[...] += p

            @pl.when(k == nk - 1)
            def _():
                finish(acc_ref[...])

    ins, specs = [a, b], [a_spec, b_spec]
    if has_bias:
        ins.append(bias)
        specs.append(bias_spec)
    if has_res:
        ins.append(res)
        specs.append(res_spec)
    scratch = [pltpu.VMEM(acc_shape, F32)] if nk > 1 else []
    return _call(body, name, grid, specs, o_spec, _sds(o_shape, o_dtype), scratch, deps=deps)(*ins)


TM_LN = 512


def _matmul_ln(name, a, b, x_res, bias3, g3, b3, l_bias, l_norm):
    t, d = x_res.shape
    tm = min(TM_LN, t)
    blocked = a.ndim == 3

    def body(a_ref, b_ref, x_ref, bias_ref, g_ref, be_ref, o_ref, xh_ref, rs_ref):
        if blocked:
            y = None
            for k in range(a.shape[0]):
                p = jnp.dot(a_ref[k], b_ref[k], preferred_element_type=F32)
                y = p if y is None else y + p
        else:
            y = jnp.dot(a_ref[...], b_ref[...], preferred_element_type=F32)
        xhat, rstd = _ln_stats(ALPHA * x_ref[...] + y + bias_ref[...])
        o_ref[...] = xhat * g_ref[...] + be_ref[...]
        xh_ref[...] = xhat
        rs_ref[...] = rstd

    if blocked:
        a_spec = pl.BlockSpec((a.shape[0], tm, a.shape[2]), lambda i: (0, i, 0))
        b_spec = pl.BlockSpec(b.shape, lambda i: (0, 0, 0))
    else:
        a_spec = pl.BlockSpec((tm, a.shape[1]), lambda i: (i, 0))
        b_spec = pl.BlockSpec(b.shape, lambda i: (0, 0))
    row = pl.BlockSpec((tm, d), lambda i: (i, 0))
    stat = pl.BlockSpec((tm, 1), lambda i: (i, 0))

    def vec(l):
        return pl.BlockSpec((None, 1, d), lambda i: (l, 0, 0))

    return _call(body, name, (t // tm,), [a_spec, b_spec, row, vec(l_bias), vec(l_norm), vec(l_norm)],
                 [row, row, stat], [_sds((t, d), F32), _sds((t, d), F32), _sds((t, 1), F32)])(
                     a, b, x_res, bias3, g3, b3)


def _mesh_pos():
    return lax.axis_index("x"), lax.axis_index("y"), lax.axis_index("c")


def _any_specs(n):
    return [pl.BlockSpec(memory_space=pl.ANY)] * n


def _all_gather(name, srcs):
    n = len(srcs)

    def body(*refs):
        src, out = refs[:n], refs[n:2 * n]
        send_sems, recv_sems, local_sems = refs[2 * n:]
        x, y, c = _mesh_pos()
        me, sibling = (x, y, c), (x, y, 1 - c)
        chips = [(1 - x, y), (x, 1 - y), (1 - x, 1 - y)]

        def slot(k, p):
            return out[k].at[:, 4 * p[0] + 2 * p[1] + p[2]]

        def copy(k, idx, block, to, s=None):
            return pltpu.make_async_remote_copy(
                src_ref=slot(k, block) if s is None else s, dst_ref=slot(k, block),
                send_sem=send_sems.at[k * 7 + idx], recv_sem=recv_sems.at[k * 7 + idx],
                device_id=to, device_id_type=MESH)

        local = [pltpu.make_async_copy(src[k], slot(k, me), local_sems.at[k]) for k in range(n)]
        for cp in local:
            cp.start()
        first = []
        for k in range(n):
            first.append(copy(k, 0, me, sibling, src[k]))
            for j, chip in enumerate(chips):
                first.append(copy(k, 1 + j, me, (*chip, c), src[k]))
        for cp in first:
            cp.start()
        passed = []
        for j, chip in enumerate(chips):
            for k in range(n):
                copy(k, 1 + j, (*chip, c), me).wait_recv()
                cp = copy(k, 4 + j, (*chip, c), sibling)
                cp.start()
                passed.append(cp)
        for k in range(n):
            copy(k, 0, sibling, me).wait_recv()
            for j, chip in enumerate(chips):
                copy(k, 4 + j, (*chip, 1 - c), me).wait_recv()
        for cp in first + passed:
            cp.wait_send()
        for cp in local:
            cp.wait()

    out_shape = [_sds((s.shape[0], N_DEV) + s.shape[1:], s.dtype) for s in srcs]
    return _call(body, name, (), [pl.BlockSpec(memory_space=pltpu.VMEM)] * n, _any_specs(n), out_shape,
                 [pltpu.SemaphoreType.DMA((7 * n,)), pltpu.SemaphoreType.DMA((7 * n,)),
                  pltpu.SemaphoreType.DMA((n,))])(*srcs)


HBM_SPEC = pl.BlockSpec(memory_space=pltpu.HBM)
SEM_SPEC = pl.BlockSpec(memory_space=pltpu.SEMAPHORE)
N_PEER = N_DEV - 1


def _split_call(body, name, in_specs, out_specs, out_shape, aliases):
    return pl.pallas_call(
        body, name=name, in_specs=in_specs, out_specs=out_specs, out_shape=out_shape, input_output_aliases=aliases,
        compiler_params=pltpu.CompilerParams(has_side_effects=pltpu.SideEffectType.DATAFLOW_SIDE_EFFECTING))


def _peers(x, y, c):
    return [(1 - x if q & 4 else x, 1 - y if q & 2 else y, 1 - c if q & 1 else c) for q in range(1, N_DEV)]


def _in_hbm(a):
    return pltpu.with_memory_space_constraint(a, pltpu.HBM)


def _place_own(name, srcs, deps=()):
    n = len(srcs)

    def body(*refs):
        src, out, sems = refs[:n], refs[n:2 * n], refs[2 * n]
        x, y, c = _mesh_pos()
        dev = 4 * x + 2 * y + c
        copies = [pltpu.make_async_copy(src[k], out[k].at[dev], sems.at[k]) for k in range(n)]
        for cp in copies:
            cp.start()
        for cp in copies:
            cp.wait()

    return _call(body, name, (), [pl.BlockSpec(memory_space=pltpu.VMEM)] * n, _any_specs(n),
                 [_sds((N_DEV,) + s.shape, s.dtype) for s in srcs], [pltpu.SemaphoreType.DMA((n,))],
                 deps=deps)(*srcs)


def _gather_start(name, lands):
    n = len(lands)

    def body(*refs):
        land, send_sems, recv_sems = refs[:n], refs[n], refs[n + 1]
        x, y, c = _mesh_pos()
        dev = 4 * x + 2 * y + c
        for k in range(n):
            for peer in _peers(x, y, c):
                pltpu.make_async_remote_copy(
                    src_ref=land[k].at[dev], dst_ref=land[k].at[dev], send_sem=send_sems.at[k],
                    recv_sem=recv_sems.at[k], device_id=peer, device_id_type=MESH).start()
        token = refs[-1]
        token[...] = jnp.zeros_like(token)

    outs = _split_call(
        body, name, [HBM_SPEC] * n, [SEM_SPEC, SEM_SPEC] + [HBM_SPEC] * n + [pl.BlockSpec(memory_space=pltpu.VMEM)],
        [pltpu.SemaphoreType.DMA((n,)), pltpu.SemaphoreType.DMA((n,))] + [pltpu.HBM(a.shape, a.dtype) for a in lands]
        + [_sds((SUBLANES, LANES), F32)],
        {k: 2 + k for k in range(n)})(*[_in_hbm(a) for a in lands])
    return outs[0], outs[1], list(outs[2:2 + n]), outs[-1]


def _wait_seven(src_ref, dst_ref, send_sem, recv_sem):
    cp = pltpu.make_async_remote_copy(
        src_ref=src_ref.at[pl.ds(0, N_PEER)], dst_ref=dst_ref.at[pl.ds(0, N_PEER)], send_sem=send_sem,
        recv_sem=recv_sem, device_id=_mesh_pos(), device_id_type=MESH)
    cp.wait_send()
    cp.wait_recv()


def _gather_wait(name, land, send_sems, recv_sems, k, after):
    def body(land_ref, send_ref, recv_ref, after_ref, out_ref):
        _wait_seven(land_ref, land_ref, send_ref.at[k], recv_ref.at[k])

    return _split_call(body, name, [HBM_SPEC, SEM_SPEC, SEM_SPEC, pl.BlockSpec(memory_space=pl.ANY)], HBM_SPEC,
                       pltpu.HBM(land.shape, land.dtype), {0: 0})(land, send_sems, recv_sems, after)


def _scatter_start(name, parts_list):
    n = len(parts_list)

    def body(*refs):
        x, y, c = _mesh_pos()
        dev = 4 * x + 2 * y + c
        for k in range(n):
            parts_ref, land_ref = refs[2 * k], refs[2 * k + 1]
            send_sem, recv_sem = refs[2 * n + 4 * k], refs[2 * n + 4 * k + 1]
            for peer in _peers(x, y, c):
                pltpu.make_async_remote_copy(
                    src_ref=parts_ref.at[4 * peer[0] + 2 * peer[1] + peer[2]], dst_ref=land_ref.at[dev],
                    send_sem=send_sem, recv_sem=recv_sem, device_id=peer, device_id_type=MESH).start()
        token = refs[-1]
        token[...] = jnp.zeros_like(token)

    ins, out_specs, out_shape, aliases = [], [], [], {}
    for k, parts in enumerate(parts_list):
        buf = pltpu.HBM(parts.shape, parts.dtype)
        ins += [_in_hbm(parts), _in_hbm(lax.empty(parts.shape, parts.dtype))]
        out_specs += [SEM_SPEC, SEM_SPEC, HBM_SPEC, HBM_SPEC]
        out_shape += [pltpu.SemaphoreType.DMA(()), pltpu.SemaphoreType.DMA(()), buf, buf]
        aliases.update({2 * k: 4 * k + 2, 2 * k + 1: 4 * k + 3})
    outs = _split_call(body, name, [HBM_SPEC] * (2 * n), out_specs + [pl.BlockSpec(memory_space=pltpu.VMEM)],
                       out_shape + [_sds((SUBLANES, LANES), F32)], aliases)(*ins)
    return [tuple(outs[4 * k:4 * k + 4]) for k in range(n)], outs[-1]


def _scatter_wait(name, started, after):
    n = len(started)

    def body(*refs):
        for k in range(n):
            send_sem, recv_sem, parts_ref, land_ref = refs[4 * k:4 * k + 4]
            _wait_seven(parts_ref, land_ref, send_sem, recv_sem)

    flat = [a for s in started for a in s]
    outs = _split_call(
        body, name, [SEM_SPEC, SEM_SPEC, HBM_SPEC, HBM_SPEC] * n + [pl.BlockSpec(memory_space=pl.ANY)],
        [HBM_SPEC, HBM_SPEC] * n, [pltpu.HBM(a.shape, a.dtype) for s in started for a in s[2:]],
        {4 * k + 2 + t: 2 * k + t for k in range(n) for t in range(2)})(*flat, after)
    return list(outs[0::2]), list(outs[1::2])


def _to_segments(a, tile):
    seg = tile // SUBLANES
    return a.reshape((a.shape[0] // tile, SUBLANES, seg) + a.shape[1:]).swapaxes(1, 2).reshape(a.shape)


def _from_segments(a, tile):
    seg = tile // SUBLANES
    return a.reshape((a.shape[0] // tile, seg, SUBLANES) + a.shape[1:]).swapaxes(1, 2).reshape(a.shape)


def _chunk(ref, q):
    return ref[q * SUBLANES:(q + 1) * SUBLANES, :]


def _fill_wrap_prev(x_ref, halo_ref, wrap_ref, n_wrap, n_halo, seg, keep):
    sub = lax.broadcasted_iota(jnp.int32, (SUBLANES, x_ref.shape[-1]), 0)
    for j in range(n_wrap):
        q = seg - n_wrap + j
        hq = q - (seg - n_halo)
        row = halo_ref[hq * SUBLANES + SUBLANES - 1:(hq + 1) * SUBLANES, :] * keep
        wrap_ref[j * SUBLANES:(j + 1) * SUBLANES, :] = jnp.where(sub == 0, row, pltpu.roll(_chunk(x_ref, q), 1, 0))


def _fill_wrap_next(x_ref, halo_ref, wrap_ref, n_wrap, keep):
    sub = lax.broadcasted_iota(jnp.int32, (SUBLANES, x_ref.shape[-1]), 0)
    for j in range(n_wrap):
        row = halo_ref[j * SUBLANES:j * SUBLANES + 1, :] * keep
        wrap_ref[j * SUBLANES:(j + 1) * SUBLANES, :] = jnp.where(
            sub == SUBLANES - 1, row, pltpu.roll(_chunk(x_ref, j), SUBLANES - 1, 0))


def _past(x_ref, wrap_ref, q, d, n_wrap):
    return _chunk(x_ref, q - d) if q >= d else _chunk(wrap_ref, q - d + n_wrap)


def _future(x_ref, wrap_ref, q, d, seg):
    return _chunk(x_ref, q + d) if q + d < seg else _chunk(wrap_ref, q + d - seg)


def _conv_fwd(x_ref, wrap_ref, w_ref, b_ref, out_ref, seg, k_taps):
    bias = jnp.broadcast_to(b_ref[...], (SUBLANES, x_ref.shape[-1]))
    for q in range(seg):
        acc = bias
        for k in range(k_taps):
            acc = acc + w_ref[k:k + 1, :] * _past(x_ref, wrap_ref, q, k_taps - 1 - k, k_taps - 1)
        out_ref[q * SUBLANES:(q + 1) * SUBLANES, :] = acc


def _conv_bwd_data(d_ref, wrap_ref, w_ref, out_ref, seg, k_taps):
    for q in range(seg):
        acc = None
        for k in range(k_taps):
            term = w_ref[k:k + 1, :] * _future(d_ref, wrap_ref, q, k_taps - 1 - k, seg)
            acc = term if acc is None else acc + term
        out_ref[q * SUBLANES:(q + 1) * SUBLANES, :] = acc


def _conv_bwd_taps(d_ref, x_ref, wrap_ref, dw_ref, seg, k_taps):
    for k in range(k_taps):
        part = None
        for q in range(seg):
            term = _chunk(d_ref, q) * _past(x_ref, wrap_ref, q, k_taps - 1 - k, k_taps - 1)
            part = term if part is None else part + term
        dw_ref[k:k + 1, :] += _colsum(part)


def _tile_halo_specs(tm, width_block, n_halo, n_tiles, block_of):
    rows = n_halo * SUBLANES
    per = tm // rows
    tile = pl.BlockSpec(width_block(tm), lambda n, i: block_of(n, i))
    prev = pl.BlockSpec(width_block(rows), lambda n, i: block_of(n, jnp.maximum(i * per - 1, 0)))
    nxt = pl.BlockSpec(width_block(rows), lambda n, i: block_of(n, jnp.minimum((i + 1) * per, n_tiles * per - 1)))
    return tile, prev, nxt


def _ln_stats(v):
    mu = jnp.mean(v, axis=-1, keepdims=True)
    vc = v - mu
    var = jnp.mean(vc * vc, axis=-1, keepdims=True)
    rstd = lax.rsqrt(var + LN_EPS)
    return vc * rstd, rstd


def _ln_backward(dxhat, xhat, rstd):
    m1 = jnp.mean(dxhat, axis=-1, keepdims=True)
    m2 = jnp.mean(dxhat * xhat, axis=-1, keepdims=True)
    return rstd * (dxhat - m1 - xhat * m2)


def _row_spec(tm, width):
    return pl.BlockSpec((tm, width), lambda i: (i, 0))


def _param_spec(l, width):
    return pl.BlockSpec((None, 1, width), lambda *_: (l, 0, 0))


def _ln_bwd_rows(dout, xh_ref, rs_ref, g_ref, dr_ref, dg_ref, db_ref, dsum_ref, first):
    xhat = xh_ref[...]
    dr = _ln_backward(dout * g_ref[...], xhat, rs_ref[...])
    dr_ref[...] = dr
    _acc_rows(dg_ref, _colsum(dout * xhat), first)
    _acc_rows(db_ref, _colsum(dout), first)
    _acc_rows(dsum_ref, _colsum(dr), first)


def _ln_bwd_specs(tm, d, l, row_of):
    vec = pl.BlockSpec((1, d), lambda *_: (0, 0))
    ins = [pl.BlockSpec((tm, d), row_of), pl.BlockSpec((tm, 1), row_of), _param_spec(l, d)]
    return ins, [pl.BlockSpec((tm, d), row_of), vec, vec, vec]


def _ln_res_bwd(name, dout, xhat, rstd, g3, l, deps=()):
    t, d = dout.shape
    tm = min(TM_ROW, t)

    def body(do_ref, xh_ref, rs_ref, g_ref, dr_ref, dg_ref, db_ref, dc_ref):
        _ln_bwd_rows(do_ref[...], xh_ref, rs_ref, g_ref, dr_ref, dg_ref, db_ref, dc_ref, pl.program_id(0) == 0)

    ins, outs = _ln_bwd_specs(tm, d, l, lambda i: (i, 0))
    return _call(body, name, (t // tm,), [_row_spec(tm, d)] + ins, outs,
                 [_sds((t, d), F32)] + [_sds((1, d), F32)] * 3, deps=deps)(dout, xhat, rstd, g3)


def _glu_bwd(name, du, h):
    t, c2 = h.shape
    c = c2 // 2
    tm = min(TM_ROW, t)

    def body(du_ref, a_ref, g_ref, dh_ref, db_ref):
        first = pl.program_id(0) == 0
        du_v, a = du_ref[...], a_ref[...]
        sg = _sigmoid(g_ref[...])
        da = du_v * sg
        dg = du_v * a * sg * (1.0 - sg)
        dh_ref[:, :c] = da.astype(BF16)
        dh_ref[:, c:] = dg.astype(BF16)
        _acc_rows(db_ref.at[:, :c], _colsum(da), first)
        _acc_rows(db_ref.at[:, c:], _colsum(dg), first)

    return _call(body, name, (t // tm,),
                 [_row_spec(tm, c), pl.BlockSpec((tm, c), lambda i: (i, 0)), pl.BlockSpec((tm, c), lambda i: (i, 1))],
                 [_row_spec(tm, c2), pl.BlockSpec((1, c2), lambda i: (0, 0))],
                 [_sds((t, c2), BF16), _sds((1, c2), F32)])(du, h, h)


CONV_CB = 512
TAPS_PAD = 32


def _dwconv31(name, u, w3, b3, l, seq):
    t, c = u.shape
    tm, cb = TM_EW, CONV_CB
    seg, seq_tiles, n_tiles = tm // SUBLANES, seq // tm, t // tm
    n_wrap = CONV_K - 1
    tile, prev, _ = _tile_halo_specs(tm, lambda rows: (rows, cb), seg, n_tiles, lambda n, r: (r, n))

    def body(u_ref, halo_ref, w_ref, b_ref, o_ref, wrap_ref):
        keep = (pl.program_id(1) % seq_tiles != 0).astype(F32)
        _fill_wrap_prev(u_ref, halo_ref, wrap_ref, n_wrap, seg, seg, keep)
        _conv_fwd(u_ref, wrap_ref, w_ref, b_ref, o_ref, seg, CONV_K)

    return _call(body, name, (c // cb, n_tiles),
                 [tile, prev, pl.BlockSpec((None, CONV_K, cb), lambda n, i: (l, 0, n)),
                  pl.BlockSpec((None, 1, cb), lambda n, i: (l, 0, n))],
                 tile, _sds((t, c), F32), [pltpu.VMEM((n_wrap * SUBLANES, cb), F32)])(u, u, w3, b3)


def _dwconv31_bwd(name, dc, u, w3, l, seq):
    t, c = dc.shape
    tm, cb = TM_EW, CONV_CB
    seg, seq_tiles, n_tiles = tm // SUBLANES, seq // tm, t // tm
    n_wrap = CONV_K - 1
    tile, prev, nxt = _tile_halo_specs(tm, lambda rows: (rows, cb), seg, n_tiles, lambda n, r: (r, n))

    def body(dc_ref, dcn_ref, u_ref, up_ref, w_ref, du_ref, dw_ref, dwrap_ref, uwrap_ref):
        i = pl.program_id(1)
        keep_prev = (i % seq_tiles != 0).astype(F32)
        keep_next = (i % seq_tiles != seq_tiles - 1).astype(F32)
        _fill_wrap_next(dc_ref, dcn_ref, dwrap_ref, n_wrap, keep_next)
        _conv_bwd_data(dc_ref, dwrap_ref, w_ref, du_ref, seg, CONV_K)

        @pl.when(i == 0)
        def _():
            dw_ref[...] = jnp.zeros_like(dw_ref)

        _fill_wrap_prev(u_ref, up_ref, uwrap_ref, n_wrap, seg, seg, keep_prev)
        _conv_bwd_taps(dc_ref, u_ref, uwrap_ref, dw_ref, seg, CONV_K)

    wrap = pltpu.VMEM((n_wrap * SUBLANES, cb), F32)
    return _call(body, name, (c // cb, n_tiles),
                 [tile, nxt, tile, prev, pl.BlockSpec((None, CONV_K, cb), lambda n, i: (l, 0, n))],
                 [tile, pl.BlockSpec((TAPS_PAD, cb), lambda n, i: (0, n))],
                 [_sds((t, c), F32), _sds((TAPS_PAD, c), F32)], [wrap, wrap])(dc, dc, u, u, w3)


def _ln_silu(name, cx, g3, b3, l):
    t, d = cx.shape
    tm = min(TM_ROW, t)

    def body(c_ref, g_ref, b_ref, o_ref):
        xhat, _ = _ln_stats(c_ref[...])
        nv = xhat * g_ref[...] + b_ref[...]
        o_ref[...] = (nv * _sigmoid(nv)).astype(BF16)

    return _call(body, name, (t // tm,), [_row_spec(tm, d), _param_spec(l, d), _param_spec(l, d)],
                 _row_spec(tm, d), _sds((t, d), BF16))(cx, g3, b3)


def _ln_silu_bwd(name, ds, cx, g3, b3, l, deps=()):
    t, d = cx.shape
    tm = min(TM_ROW, t)

    def body(ds_ref, c_ref, g_ref, b_ref, dc_ref, dg_ref, db_ref, dsum_ref):
        first = pl.program_id(0) == 0
        xhat, rstd = _ln_stats(c_ref[...])
        g = g_ref[...]
        nv = xhat * g + b_ref[...]
        sg = _sigmoid(nv)
        dn = ds_ref[...] * (sg * (1.0 + nv * (1.0 - sg)))
        dc = _ln_backward(dn * g, xhat, rstd)
        dc_ref[...] = dc
        _acc_rows(dg_ref, _colsum(dn * xhat), first)
        _acc_rows(db_ref, _colsum(dn), first)
        _acc_rows(dsum_ref, _colsum(dc), first)

    vec = pl.BlockSpec((1, d), lambda i: (0, 0))
    return _call(body, name, (t // tm,),
                 [_row_spec(tm, d), _row_spec(tm, d), _param_spec(l, d), _param_spec(l, d)],
                 [_row_spec(tm, d), vec, vec, vec],
                 [_sds((t, d), F32)] + [_sds((1, d), F32)] * 3, deps=deps)(ds, cx, g3, b3)


FFN_HALO = FFN_K - 1


def _ffn_conv(x_ref, halo_ref, wrap_ref, w_ref, b_ref, keep, seg, out_ref):
    _fill_wrap_prev(x_ref, halo_ref, wrap_ref, FFN_K - 1, FFN_HALO, seg, keep)
    _conv_fwd(x_ref, wrap_ref, w_ref, b_ref, out_ref, seg, FFN_K)


TM_FFN = 512
WRAP_ROWS = FFN_HALO * SUBLANES


def _sub_tiles(x_ref, prev_ref, next_ref, keep_prev, keep_next, n_sub):
    out = []
    for s in range(n_sub):
        tile = x_ref.at[pl.ds(s * TM_EW, TM_EW)]
        prev = prev_ref if s == 0 else x_ref.at[pl.ds(s * TM_EW - WRAP_ROWS, WRAP_ROWS)]
        nxt = next_ref if s == n_sub - 1 else x_ref.at[pl.ds((s + 1) * TM_EW, WRAP_ROWS)]
        out.append((tile, prev, keep_prev if s == 0 else 1.0, nxt, keep_next if s == n_sub - 1 else 1.0))
    return out


def _rows(ref, s, rows):
    return ref.at[pl.ds(s * rows, rows)]


def _ffn_specs(tm, fb, n_tiles):
    return _tile_halo_specs(tm, lambda rows: (None, rows, fb), FFN_HALO, n_tiles, lambda n, r: (n, r, 0))


def _ffn_up_act(name, x, w_up, b_up4, wdw, bdw, l, seq):
    t, d = x.shape
    nb, fb, _ = w_up.shape
    half = nb // 2
    tm = min(TM_FFN, seq)
    n_sub, seg, seq_steps, n_steps = tm // TM_EW, TM_EW // SUBLANES, seq // tm, t // tm
    per = tm // WRAP_ROWS
    nt_dims = (_DIMS["nt"], ((), ()))

    def body(x_ref, xp_ref, ug_ref, uv_ref, bug_ref, buv_ref, wg_ref, wv_ref, bg_ref, bv_ref,
             hg_ref, hv_ref, a_ref, pg_ref, pv_ref, gwrap_ref, vwrap_ref, cg_ref, cv_ref):
        keep = (pl.program_id(1) % seq_steps != 0).astype(F32)
        xb, xpb = x_ref[...].astype(BF16), xp_ref[...].astype(BF16)
        hg_ref[...] = lax.dot_general(xb, ug_ref[...], nt_dims, preferred_element_type=F32) + bug_ref[...]
        pg_ref[...] = lax.dot_general(xpb, ug_ref[...], nt_dims, preferred_element_type=F32) + bug_ref[...]
        for s, (tile, prev, kp, _, _) in enumerate(_sub_tiles(hg_ref, pg_ref, None, keep, None, n_sub)):
            _ffn_conv(tile, prev, gwrap_ref, wg_ref, bg_ref, kp, seg, _rows(cg_ref, s, TM_EW))
        hv_ref[...] = lax.dot_general(xb, uv_ref[...], nt_dims, preferred_element_type=F32) + buv_ref[...]
        pv_ref[...] = lax.dot_general(xpb, uv_ref[...], nt_dims, preferred_element_type=F32) + buv_ref[...]
        for s, (tile, prev, kp, _, _) in enumerate(_sub_tiles(hv_ref, pv_ref, None, keep, None, n_sub)):
            _ffn_conv(tile, prev, vwrap_ref, wv_ref, bv_ref, kp, seg, _rows(cv_ref, s, TM_EW))
        cg = cg_ref[...]
        a_ref[...] = (cg * _sigmoid(cg) * cv_ref[...]).astype(BF16)

    def blk(shift):
        return pl.BlockSpec((None, fb, d), lambda n, i: (n + shift, 0, 0))

    def vec(shift, rows):
        return pl.BlockSpec((None, None, rows, fb), lambda n, i: (l, n + shift, 0, 0))

    out = pl.BlockSpec((None, tm, fb), lambda n, i: (n, i, 0))
    tmp = pltpu.VMEM((tm, fb), F32)
    halo = pltpu.VMEM((WRAP_ROWS, fb), F32)
    return _call(body, name, (half, n_steps),
                 [pl.BlockSpec((tm, d), lambda n, i: (i, 0)),
                  pl.BlockSpec((WRAP_ROWS, d), lambda n, i: (jnp.maximum(i * per - 1, 0), 0)),
                  blk(0), blk(half), vec(0, 1), vec(half, 1), vec(0, FFN_K), vec(half, FFN_K), vec(0, 1), vec(half, 1)],
                 [out, out, out],
                 [_sds((half, t, fb), F32), _sds((half, t, fb), F32), _sds((half, t, fb), BF16)],
                 [halo, halo, halo, halo, tmp, tmp])(x, x, w_up, w_up, b_up4, b_up4, wdw, wdw, bdw, bdw)


def _ffn_act_bwd(name, dy, w_down, hg, hv, wdw, bdw, l, seq, deps=()):
    half, t, fb = hg.shape
    d = dy.shape[-1]
    tm = min(TM_FFN, seq)
    n_sub, seg, seq_steps, n_steps = tm // TM_EW, TM_EW // SUBLANES, seq // tm, t // tm
    tile, prev, _ = _ffn_specs(tm, fb, n_steps)

    def body(dy_ref, wd_ref, g_ref, gp_ref, v_ref, vp_ref, wg_ref, wv_ref, bg_ref, bv_ref,
             dg_ref, dv_ref, dbg_ref, dbv_ref, dwg_ref, dwv_ref, gwrap_ref, vwrap_ref, cg_ref, cv_ref):
        i = pl.program_id(1)
        first = i == 0
        keep = (i % seq_steps != 0).astype(F32)
        da = lax.dot_general(dy_ref[...].astype(BF16), wd_ref[...], (_DIMS["nt"], ((), ())),
                             preferred_element_type=F32)
        g_tiles = _sub_tiles(g_ref, gp_ref, None, keep, None, n_sub)
        v_tiles = _sub_tiles(v_ref, vp_ref, None, keep, None, n_sub)
        for s in range(n_sub):
            _ffn_conv(g_tiles[s][0], g_tiles[s][1], _rows(gwrap_ref, s, WRAP_ROWS), wg_ref, bg_ref, g_tiles[s][2],
                      seg, _rows(cg_ref, s, TM_EW))
            _ffn_conv(v_tiles[s][0], v_tiles[s][1], _rows(vwrap_ref, s, WRAP_ROWS), wv_ref, bv_ref, v_tiles[s][2],
                      seg, _rows(cv_ref, s, TM_EW))
        cg, cv = cg_ref[...], cv_ref[...]
        sg = _sigmoid(cg)
        dcv = da * cg * sg
        dcg = da * cv * sg * (1.0 + cg * (1.0 - sg))
        dg_ref[...] = dcg
        dv_ref[...] = dcv
        _acc_rows(dbg_ref, _colsum(dcg), first)
        _acc_rows(dbv_ref, _colsum(dcv), first)

        @pl.when(first)
        def _():
            dwg_ref[...] = jnp.zeros_like(dwg_ref)
            dwv_ref[...] = jnp.zeros_like(dwv_ref)

        for s in range(n_sub):
            _conv_bwd_taps(_rows(dg_ref, s, TM_EW), g_tiles[s][0], _rows(gwrap_ref, s, WRAP_ROWS), dwg_ref, seg, FFN_K)
            _conv_bwd_taps(_rows(dv_ref, s, TM_EW), v_tiles[s][0], _rows(vwrap_ref, s, WRAP_ROWS), dwv_ref, seg, FFN_K)

    def vec(shift, rows):
        return pl.BlockSpec((None, None, rows, fb), lambda n, i: (l, n + shift, 0, 0))

    def acc(rows):
        return pl.BlockSpec((None, rows, fb), lambda n, i: (n, 0, 0))

    wrap = pltpu.VMEM((n_sub * WRAP_ROWS, fb), F32)
    tmp = pltpu.VMEM((tm, fb), F32)
    return _call(body, name, (half, n_steps),
                 [pl.BlockSpec((tm, d), lambda n, i: (i, 0)), pl.BlockSpec((None, fb, d), lambda n, i: (n, 0, 0)),
                  tile, prev, tile, prev, vec(0, FFN_K), vec(half, FFN_K), vec(0, 1), vec(half, 1)],
                 [tile, tile, acc(1), acc(1), acc(SUBLANES), acc(SUBLANES)],
                 [_sds((half, t, fb), F32), _sds((half, t, fb), F32), _sds((half, 1, fb), F32),
                  _sds((half, 1, fb), F32), _sds((half, SUBLANES, fb), F32), _sds((half, SUBLANES, fb), F32)],
                 [wrap, wrap, tmp, tmp], deps=deps)(dy, w_down, hg, hg, hv, hv, wdw, wdw, bdw, bdw)


def _ffn_conv_t_dx(name, dcg, dcv, wdw, w_up, res, xhat, rstd, g3, l, seq, deps=()):
    half, t, fb = dcg.shape
    nb, d = 2 * half, res.shape[-1]
    tm = min(TM_FFN, seq)
    n_sub, seg, seq_steps, n_steps = tm // TM_EW, TM_EW // SUBLANES, seq // tm, t // tm
    per = tm // WRAP_ROWS

    pair = 2
    n_pairs, half_pairs = nb // pair, half // pair

    def body(g_ref, gn_ref, v_ref, vn_ref, w_ref, up_ref, res_ref, xh_ref, rs_ref, gam_ref,
             dh_ref, db_ref, dr_ref, dgam_ref, dbeta_ref, dsum_ref, wrap_ref, out_ref, acc_ref):
        i, m = pl.program_id(0), pl.program_id(1)
        keep = (i % seq_steps != seq_steps - 1).astype(F32)

        def conv_t(d_ref, dn_ref, b):
            for s, (sub, _, _, nx, kn) in enumerate(_sub_tiles(d_ref.at[b], None, dn_ref.at[b], None, keep, n_sub)):
                _fill_wrap_next(sub, nx, wrap_ref, FFN_K - 1, kn)
                _conv_bwd_data(sub, wrap_ref, w_ref.at[b], _rows(out_ref, s, TM_EW), seg, FFN_K)

        p = None
        for b in range(pair):
            @pl.when(m < half_pairs)
            def _(b=b):
                conv_t(g_ref, gn_ref, b)

            @pl.when(m >= half_pairs)
            def _(b=b):
                conv_t(v_ref, vn_ref, b)

            dh = out_ref[...]
            dhb = dh.astype(BF16)
            dh_ref[b] = dhb
            _acc_rows(db_ref.at[pair * m + b], _colsum(dh), i == 0)
            part = jnp.dot(dhb, up_ref[b], preferred_element_type=F32)
            p = part if p is None else p + part

        @pl.when(m == 0)
        def _():
            acc_ref[...] = p

        @pl.when(m > 0)
        def _():
            acc_ref[...] += p

        @pl.when(m == n_pairs - 1)
        def _():
            _ln_bwd_rows(acc_ref[...] + ALPHA * res_ref[...], xh_ref, rs_ref, gam_ref, dr_ref, dgam_ref, dbeta_ref,
                         dsum_ref, i == 0)

    def src(gate):
        def blk(m):
            return jnp.minimum(m, half_pairs - 1) if gate else jnp.maximum(m - half_pairs, 0)
        tile = pl.BlockSpec((pair, tm, fb), lambda i, m: (blk(m), i, 0))
        nxt = pl.BlockSpec((pair, WRAP_ROWS, fb),
                           lambda i, m: (blk(m), jnp.minimum((i + 1) * per, n_steps * per - 1), 0))
        return [tile, nxt]

    row = pl.BlockSpec((tm, d), lambda i, m: (i, 0))
    ln_ins, ln_outs = _ln_bwd_specs(tm, d, l, lambda i, m: (i, 0))
    tmp = pltpu.VMEM((tm, fb), F32)
    halo = pltpu.VMEM((WRAP_ROWS, fb), F32)
    return _call(body, name, (n_steps, n_pairs),
                 src(True) + src(False) +
                 [pl.BlockSpec((None, pair, FFN_K, fb), lambda i, m: (l, m, 0, 0)),
                  pl.BlockSpec((pair, fb, d), lambda i, m: (m, 0, 0)), row] + ln_ins,
                 [pl.BlockSpec((pair, tm, fb), lambda i, m: (m, i, 0)),
                  pl.BlockSpec((nb, 1, fb), lambda i, m: (0, 0, 0))] + ln_outs,
                 [_sds((nb, t, fb), BF16), _sds((nb, 1, fb), F32), _sds((t, d), F32)] + [_sds((1, d), F32)] * 3,
                 [halo, tmp, pltpu.VMEM((tm, d), F32)],
                 deps=deps)(dcg, dcg, dcv, dcv, wdw, w_up, res, xhat, rstd, g3)


def _gelu_parts(h):
    cdf = 0.5 * (1.0 + lax.erf(h * INV_SQRT2))
    return h * cdf, cdf


def _seg_axis(a, axis, fn):
    return jnp.moveaxis(fn(jnp.moveaxis(a, axis, 0), TM_EW), 0, axis)


def _sgu_operands(w_s, b_s):
    nl = w_s.shape[0]
    n_sub = TM_EW // CHUNK
    tril = jnp.tril(jnp.ones((CHUNK, CHUNK), dtype=bool))
    w_causal = jnp.where(tril, w_s, 0.0)
    w_tile = (jnp.eye(n_sub, dtype=F32)[None, None, :, None, :, None] * w_causal[:, :, None, :, None, :]).reshape(
        nl, GROUPS, TM_EW, TM_EW)
    w_tile = _seg_axis(_seg_axis(w_tile, 2, _to_segments), 3, _to_segments).astype(BF16)
    bs_tile = jnp.broadcast_to(b_s[:, :, None, :, None], (nl, GROUPS, n_sub, CHUNK, CHUNK)).reshape(
        nl, GROUPS, TM_EW, CHUNK)
    return w_tile, _seg_axis(bs_tile, 2, _to_segments)


def _sgu_param_grads(dwt, dbt):
    n_sub = TM_EW // CHUNK
    tril = jnp.tril(jnp.ones((CHUNK, CHUNK), dtype=bool))
    dwt = _seg_axis(_seg_axis(dwt, 1, _from_segments), 2, _from_segments).reshape(GROUPS, n_sub, CHUNK, n_sub, CHUNK)
    dw = sum(dwt[:, a, :, a, :] for a in range(n_sub))
    db = _seg_axis(dbt, 1, _from_segments).reshape(GROUPS, n_sub, CHUNK).sum(axis=1)
    return jnp.where(tril, dw, 0.0), db


def _sgu(name, h, g3, b3, wt, bst, l):
    t, c2 = h.shape
    c = c2 // 2
    tm = TM_EW

    def body(h_ref, g_ref, b_ref, wt_ref, bs_ref, o_ref):
        z, _ = _gelu_parts(h_ref[...])
        u = z[:, :c]
        xhat, _ = _ln_stats(z[:, c:])
        vnb = (xhat * g_ref[...] + b_ref[...]).astype(BF16)
        for gi in range(GROUPS):
            cs = slice(gi * CHUNK, (gi + 1) * CHUNK)
            sp = jnp.dot(wt_ref[gi], vnb[:, cs], preferred_element_type=F32) + bs_ref[gi]
            o_ref[:, cs] = (u[:, cs] * sp).astype(BF16)

    return _call(body, name, (t // tm,),
                 [_row_spec(tm, c2), _param_spec(l, c), _param_spec(l, c),
                  pl.BlockSpec((None, GROUPS, tm, tm), lambda i: (l, 0, 0, 0)),
                  pl.BlockSpec((None, GROUPS, tm, CHUNK), lambda i: (l, 0, 0, 0))],
                 _row_spec(tm, c), _sds((t, c), BF16))(h, g3, b3, wt, bst)


def _sgu_bwd(name, dq, h, g3, b3, wt, bst, l, deps=()):
    t, c2 = h.shape
    c = c2 // 2
    tm = TM_EW
    n_tiles = t // tm

    def body(dq_ref, h_ref, g_ref, b_ref, wt_ref, bs_ref,
             dh_ref, dbin_ref, dw_ref, dbs_ref, dg_ref, db_ref, du_ref, dvn_ref, bsum_ref):
        i = pl.program_id(0)
        first = i == 0
        hv = h_ref[...]
        z, cdf = _gelu_parts(hv)
        u = z[:, :c]
        xhat, rstd = _ln_stats(z[:, c:])
        g = g_ref[...]
        vnb = (xhat * g + b_ref[...]).astype(BF16)

        @pl.when(first)
        def _():
            dw_ref[...] = jnp.zeros_like(dw_ref)
            bsum_ref[...] = jnp.zeros_like(bsum_ref)

        for gi in range(GROUPS):
            cs = slice(gi * CHUNK, (gi + 1) * CHUNK)
            vb = vnb[:, cs]
            w = wt_ref[gi]
            sp = jnp.dot(w, vb, preferred_element_type=F32) + bs_ref[gi]
            dqb = dq_ref[:, cs]
            du_ref[:, cs] = dqb * sp
            dsp = dqb * u[:, cs]
            bsum_ref[gi] += dsp
            dspb = dsp.astype(BF16)
            dw_ref[gi] += lax.dot_general(dspb, vb, (_DIMS["nt"], ((), ())), preferred_element_type=F32)
            dvn_ref[:, cs] = lax.dot_general(w, dspb, (_DIMS["tn"], ((), ())), preferred_element_type=F32)

        dvn = dvn_ref[...]
        dv = _ln_backward(dvn * g, xhat, rstd)
        pdf = jnp.exp(-0.5 * hv * hv) * INV_SQRT2PI
        dgelu = cdf + hv * pdf
        dhu = du_ref[...] * dgelu[:, :c]
        dhv = dv * dgelu[:, c:]
        dh_ref[:, :c] = dhu.astype(BF16)
        dh_ref[:, c:] = dhv.astype(BF16)
        _acc_rows(dbin_ref.at[:, :c], _colsum(dhu), first)
        _acc_rows(dbin_ref.at[:, c:], _colsum(dhv), first)
        _acc_rows(dg_ref, _colsum(dvn * xhat), first)
        _acc_rows(db_ref, _colsum(dvn), first)

        @pl.when(i == n_tiles - 1)
        def _():
            dbs_ref[...] = jnp.sum(bsum_ref[...], axis=-1)

    vec = pl.BlockSpec((1, c), lambda i: (0, 0))
    return _call(body, name, (n_tiles,),
                 [_row_spec(tm, c), _row_spec(tm, c2), _param_spec(l, c), _param_spec(l, c),
                  pl.BlockSpec((None, GROUPS, tm, tm), lambda i: (l, 0, 0, 0)),
                  pl.BlockSpec((None, GROUPS, tm, CHUNK), lambda i: (l, 0, 0, 0))],
                 [_row_spec(tm, c2), pl.BlockSpec((1, c2), lambda i: (0, 0)),
                  pl.BlockSpec((GROUPS, tm, tm), lambda i: (0, 0, 0)),
                  pl.BlockSpec((GROUPS, tm), lambda i: (0, 0)), vec, vec],
                 [_sds((t, c2), BF16), _sds((1, c2), F32), _sds((GROUPS, tm, tm), F32),
                  _sds((GROUPS, tm), F32), _sds((1, c), F32), _sds((1, c), F32)],
                 [pltpu.VMEM((tm, c), F32), pltpu.VMEM((tm, c), F32), pltpu.VMEM((GROUPS, tm, CHUNK), F32)],
                 deps=deps)(dq, h, g3, b3, wt, bst)


def _loss(name, y, target):
    t, d = y.shape
    tm = min(TM_ROW, t)
    n_tiles = t // tm

    def body(y_ref, t_ref, l_ref, dy_ref, acc_ref):
        i = pl.program_id(0)
        diff = y_ref[...] - t_ref[...]
        dy_ref[...] = diff * (1.0 / d)
        _acc_rows(acc_ref, _colsum(diff * diff), i == 0)

        @pl.when(i == n_tiles - 1)
        def _():
            l_ref[...] = jnp.broadcast_to(jnp.sum(acc_ref[...], axis=-1, keepdims=True) * (0.5 / d), (1, LANES))

    return _call(body, name, (n_tiles,), [_row_spec(tm, d), _row_spec(tm, d)],
                 [pl.BlockSpec((1, LANES), lambda i: (0, 0)), _row_spec(tm, d)],
                 [_sds((1, LANES), F32), _sds((t, d), F32)], [pltpu.VMEM((1, d), F32)])(y, target)


def _adamw(g, w, m, v):
    m2 = ADAM_B1 * m + (1.0 - ADAM_B1) * g
    v2 = ADAM_B2 * v + (1.0 - ADAM_B2) * (g * g)
    m_hat = m2 / (1.0 - ADAM_B1 ** ADAM_STEP)
    v_hat = v2 / (1.0 - ADAM_B2 ** ADAM_STEP)
    delta = -ADAM_LR * (m_hat / (jnp.sqrt(v_hat) + ADAM_EPS) + ADAM_WD * w)
    return delta, m2, v2


ROW_TILE_CAP = 512


def _row_tile(rows, cap=ROW_TILE_CAP):
    if rows <= cap:
        return rows
    for tr in range(cap, 15, -16):
        if rows % tr == 0:
            return tr
    return rows


def _sum8_adamw(name, dev, lands, parts, w, m, v):
    nl = len(lands)
    _, r, c = lands[0].shape
    tr = _row_tile(r, cap=128)

    def body(dev_ref, *refs):
        land, own = refs[:nl], refs[nl:2 * nl]
        w_ref, m_ref, v_ref, g_ref, d_ref, m2_ref, v2_ref = refs[2 * nl:]
        layer, me = pl.program_id(0), dev_ref[0]
        for l in range(nl):
            @pl.when(layer == l)
            def _(l=l):
                g = None
                for s in range(N_DEV):
                    part = jnp.where(me == s, own[l][...], land[l][s]).astype(F32)
                    g = part if g is None else g + part
                delta, m2, v2 = _adamw(g, w_ref[...], m_ref[...], v_ref[...])
                g_ref[...] = g
                d_ref[...] = delta
                m2_ref[...] = m2
                v2_ref[...] = v2

    def rows_of(l, a, i):
        return jnp.where(a == l, i, 0)

    spec = pl.BlockSpec((None, tr, c), lambda a, i, dev_ref: (a, i, 0))
    in_specs = [pl.BlockSpec((N_DEV, tr, c), lambda a, i, dev_ref, l=l: (0, rows_of(l, a, i), 0)) for l in range(nl)]
    in_specs += [pl.BlockSpec((None, tr, c), lambda a, i, dev_ref, l=l: (dev_ref[0], rows_of(l, a, i), 0))
                 for l in range(nl)]
    grid_spec = pltpu.PrefetchScalarGridSpec(
        num_scalar_prefetch=1, grid=(nl, r // tr), in_specs=in_specs + [spec] * 3, out_specs=[spec] * 4)
    return pl.pallas_call(
        body, name=name, grid_spec=grid_spec, out_shape=[_sds(w.shape, F32)] * 4,
        compiler_params=pltpu.CompilerParams(vmem_limit_bytes=VMEM_LIMIT))(dev, *lands, *parts, w, m, v)


def _sum8(name, parts):
    _, r, c = parts.shape
    tr = _row_tile(r)

    def body(p_ref, o_ref):
        acc = p_ref[0]
        for s in range(1, N_DEV):
            acc = acc + p_ref[s]
        o_ref[...] = acc

    return _call(body, name, (r // tr,), [pl.BlockSpec((N_DEV, tr, c), lambda i: (0, i, 0))],
                 pl.BlockSpec((tr, c), lambda i: (i, 0)), _sds((r, c), F32))(parts)


def _adamw_small(name, gs, ws, ms, vs):
    n = len(gs)

    def body(*refs):
        g, w, m, v = (refs[k * n:(k + 1) * n] for k in range(4))
        d_out, m_out, v_out = (refs[(4 + k) * n:(5 + k) * n] for k in range(3))
        for k in range(n):
            d_out[k][...], m_out[k][...], v_out[k][...] = _adamw(g[k][...], w[k][...], m[k][...], v[k][...])

    vmem = pl.BlockSpec(memory_space=pltpu.VMEM)
    outs = _call(body, name, (), [vmem] * (4 * n), [vmem] * (3 * n), [_sds(w.shape, F32) for w in ws] * 3)(
        *gs, *ws, *ms, *vs)
    return outs[:n], outs[n:2 * n], outs[2 * n:]


def _pack(arrs, row_multiple=SUBLANES):
    pieces, rows = [], 0
    for a in arrs:
        piece = a.reshape(-1, LANES)
        piece = jnp.pad(piece, ((0, (-piece.shape[0]) % SUBLANES), (0, 0)))
        pieces.append(piece)
        rows += piece.shape[0]
    if rows % row_multiple:
        pieces.append(jnp.zeros(((-rows) % row_multiple, LANES), pieces[0].dtype))
    return jnp.concatenate(pieces, axis=0)


def _unpack(buf, shapes, lead=0):
    out, pos = [], 0
    for shp in shapes:
        rows = math.prod(shp) // LANES
        piece = lax.slice_in_dim(buf, pos, pos + rows, axis=lead)
        out.append(piece.reshape(buf.shape[:lead] + tuple(shp)))
        pos += rows + (-rows) % SUBLANES
    return out


REPLICATED = ["conv_b_in", "conv_b_dw", "conv_ln_g", "conv_ln_b", "conv_b_out", "gmlp_w_s", "gmlp_b_s",
              "ffn_b_up", "ffn_b_dw", "ffn_b_down", "norm1_g", "norm1_b", "norm2_g", "norm2_b"]
SMALL_SHARDED = ["conv_w_dw", "gmlp_b_in", "gmlp_ln_g", "gmlp_ln_b", "gmlp_b_out", "ffn_w_dw"]
BIG = ["conv_w_in", "conv_w_out", "gmlp_w_in", "gmlp_w_out", "ffn_w_up", "ffn_w_down"]
WEIGHTS = ["conv_w_in", "conv_b_in", "conv_w_dw", "conv_b_dw", "conv_ln_g", "conv_ln_b", "conv_w_out", "conv_b_out",
           "gmlp_w_in", "gmlp_b_in", "gmlp_ln_g", "gmlp_ln_b", "gmlp_w_s", "gmlp_b_s", "gmlp_w_out", "gmlp_b_out",
           "ffn_w_up", "ffn_b_up", "ffn_w_dw", "ffn_b_dw", "ffn_w_down", "ffn_b_down",
           "norm1_g", "norm1_b", "norm2_g", "norm2_b"]


def _from_shards(g, lead_shape):
    nd = len(lead_shape)
    perm = tuple(range(1, nd + 1)) + (0, nd + 1)
    return g.transpose(perm).reshape(tuple(lead_shape) + (-1,))


def _to_shards(full, width):
    lead = full.shape[:-1]
    nd = len(lead)
    parts = full.reshape(lead + (N_DEV, width))
    return parts.transpose((nd,) + tuple(range(nd)) + (nd + 1,))


def _step(p):
    x_in, target_in = p["x"], p["loss_target"]
    bsz, seq, d = x_in.shape
    t = bsz * seq
    assert seq % TM_EW == 0 and TM_EW % CHUNK == 0 and TM_EW // SUBLANES >= CONV_K - 1
    x0 = _to_segments(x_in.reshape(t, d), TM_EW)
    target = _to_segments(target_in.reshape(t, d), TM_EW)
    n_conv, n_gmlp = p["conv_w_in"].shape[0], p["gmlp_w_in"].shape[0]
    fb = p["ffn_w_up"].shape[-1]
    nblk = N_DEV
    half = nblk // 2
    cw = p["conv_w_in"].shape[-1]
    tm = min(TM_MM, t)
    nt = t // tm
    tk = min(TK_DW, t)
    ntk = t // tk
    dev = 4 * lax.axis_index("x") + 2 * lax.axis_index("y") + lax.axis_index("c")

    small_shapes = [p[n].shape for n in SMALL_SHARDED]
    small_src = _pack([p[n] for n in SMALL_SHARDED])[None]
    small_all = _all_gather("gather_small_weights", [small_src])[0][0]
    sm = _unpack(small_all, small_shapes, lead=1)
    w_src = []
    for i in range(DEPTH):
        mix = "conv" if i % 2 == 0 else "gmlp"
        w_src += [p[mix + "_w_in"][i // 2].astype(BF16), p[mix + "_w_out"][i // 2].astype(BF16),
                  p["ffn_w_up"][i].T.astype(BF16), p["ffn_w_down"][i].astype(BF16)]
    send_sems, recv_sems, w_land, _ = _gather_start(
        "weights_gather_start", _place_own("weights_place_own", w_src, deps=[small_all]))
    W_IN, W_OUT, W_UP, W_DOWN = range(4)

    def wait_weight(i, k, after):
        return _gather_wait(f"l{i}_weights_wait{k}", w_land[4 * i + k], send_sems, recv_sems, 4 * i + k, after)
    conv_w_dw = _from_shards(sm[0], sm[0].shape[1:-1])
    gmlp_b_in = _from_shards(sm[1], sm[1].shape[1:-1])
    gmlp_ln_g = _from_shards(sm[2], sm[2].shape[1:-1])
    gmlp_ln_b = _from_shards(sm[3], sm[3].shape[1:-1])
    gmlp_b_out = _from_shards(sm[4], sm[4].shape[1:-1])
    ffn_w_dw = sm[5].transpose(1, 0, 2, 3)

    def rows3(a):
        return a.reshape(a.shape[0], 1, a.shape[-1])

    conv_b_in4 = p["conv_b_in"].reshape(n_conv, N_DEV, 1, cw)
    gmlp_b_in4 = gmlp_b_in.reshape(n_gmlp, N_DEV, 1, cw)
    ffn_b_up4 = p["ffn_b_up"].reshape(DEPTH, nblk, 1, fb)
    ffn_b_dw4 = p["ffn_b_dw"].reshape(DEPTH, nblk, 1, fb)
    conv_b_dw3, conv_ln_g3, conv_ln_b3 = rows3(p["conv_b_dw"]), rows3(p["conv_ln_g"]), rows3(p["conv_ln_b"])
    conv_b_out3, gmlp_b_out3, ffn_b_down3 = rows3(p["conv_b_out"]), rows3(gmlp_b_out), rows3(p["ffn_b_down"])
    gmlp_ln_g3, gmlp_ln_b3 = rows3(gmlp_ln_g), rows3(gmlp_ln_b)
    n1g3, n1b3, n2g3, n2b3 = rows3(p["norm1_g"]), rows3(p["norm1_b"]), rows3(p["norm2_g"]), rows3(p["norm2_b"])
    w_tile, bs_tile = _sgu_operands(p["gmlp_w_s"], p["gmlp_b_s"])

    def mm_in(name, xa, wg, l, bias4, glu=False):
        tmi = min(TM_LN, t)
        c_half = half * cw

        def body(a_ref, b_ref, bias_ref, h_ref, *u_ref):
            xb = a_ref[...].astype(BF16)
            for n in range(N_DEV):
                h_ref[:, n * cw:(n + 1) * cw] = jnp.dot(xb, b_ref[n], preferred_element_type=F32) + bias_ref[n]
            if glu:
                u_ref[0][...] = h_ref[:, :c_half] * _sigmoid(h_ref[:, c_half:])

        outs = _call(body, name, (t // tmi,),
                     [pl.BlockSpec((tmi, d), lambda i: (i, 0)), pl.BlockSpec((N_DEV, d, cw), lambda i: (0, 0, 0)),
                      pl.BlockSpec((None, N_DEV, 1, cw), lambda i: (l, 0, 0, 0))],
                     [pl.BlockSpec((tmi, N_DEV * cw), lambda i: (i, 0))]
                     + ([pl.BlockSpec((tmi, c_half), lambda i: (i, 0))] if glu else []),
                     [_sds((t, N_DEV * cw), F32)] + ([_sds((t, c_half), F32)] if glu else []))(xa, wg, bias4)
        return outs if glu else outs[0]

    def mm_out_dx(name, dy, w, deps=()):
        return _matmul(name, dy, w, "nt", grid=(nt,),
                       a_spec=pl.BlockSpec((tm, d), lambda i: (i, 0)),
                       b_spec=pl.BlockSpec((d, d), lambda i: (0, 0)),
                       o_spec=pl.BlockSpec((tm, d), lambda i: (i, 0)), o_shape=(t, d), o_dtype=F32, deps=deps)

    def mm_out_dw(name, sa, dy):
        return _matmul(name, sa, dy, "tn", grid=(nt,), k_axis=0, nk=nt, acc_shape=(d, d),
                       a_spec=pl.BlockSpec((tm, d), lambda k: (k, 0)),
                       b_spec=pl.BlockSpec((tm, d), lambda k: (k, 0)),
                       o_spec=pl.BlockSpec((d, d), lambda k: (0, 0)), o_shape=(d, d), o_dtype=BF16)

    def mm_in_dx(name, dh, wg, res):
        tmx = min(TM_LN, t)

        def body(a_ref, b_ref, res_ref, o_ref):
            y = ALPHA * res_ref[...]
            for n in range(N_DEV):
                y = y + lax.dot_general(a_ref[:, n * cw:(n + 1) * cw], b_ref[n], (_DIMS["nt"], ((), ())),
                                        preferred_element_type=F32)
            o_ref[...] = y

        row = pl.BlockSpec((tmx, d), lambda i: (i, 0))
        return _call(body, name, (t // tmx,),
                     [pl.BlockSpec((tmx, N_DEV * cw), lambda i: (i, 0)),
                      pl.BlockSpec((N_DEV, d, cw), lambda i: (0, 0, 0)), row], row, _sds((t, d), F32))(dh, wg, res)

    def mm_in_dw(name, xa, dh):
        def body(a_ref, b_ref, o_ref, acc_ref):
            k = pl.program_id(1)
            p = lax.dot_general(a_ref[...].astype(BF16), b_ref[...], (_DIMS["tn"], ((), ())),
                                preferred_element_type=F32)
            _acc_rows(acc_ref, p, k == 0)

            @pl.when(k == nt - 1)
            def _():
                for n in range(half):
                    o_ref[n] = acc_ref[:, n * cw:(n + 1) * cw].astype(BF16)

        return _call(body, name, (2, nt),
                     [pl.BlockSpec((tm, d), lambda c, k: (k, 0)), pl.BlockSpec((tm, half * cw), lambda c, k: (k, c))],
                     pl.BlockSpec((half, d, cw), lambda c, k: (c, 0, 0)), _sds((N_DEV, d, cw), BF16),
                     [pltpu.VMEM((d, half * cw), F32)])(xa, dh)

    def mm_down_dw(name, a, dy, deps=()):
        return _matmul(name, a, dy, "tn", grid=(half, ntk), k_axis=1, nk=ntk, acc_shape=(fb, d),
                       a_spec=pl.BlockSpec((None, tk, fb), lambda n, k: (n, k, 0)),
                       b_spec=pl.BlockSpec((tk, d), lambda n, k: (k, 0)),
                       o_spec=pl.BlockSpec((None, fb, d), lambda n, k: (n, 0, 0)),
                       o_shape=(half, fb, d), o_dtype=BF16, deps=deps)

    def mm_up_dw(name, xa, dh):
        return _matmul(name, dh, xa, "tn", grid=(nblk, ntk), k_axis=1, nk=ntk, acc_shape=(fb, d),
                       a_spec=pl.BlockSpec((None, tk, fb), lambda n, k: (n, k, 0)),
                       b_spec=pl.BlockSpec((tk, d), lambda n, k: (k, 0)),
                       o_spec=pl.BlockSpec((None, fb, d), lambda n, k: (n, 0, 0)),
                       o_shape=(nblk, fb, d), o_dtype=BF16)

    saved = []
    xcur = x0
    for i in range(DEPTH):
        j = i // 2
        s = {"x": xcur}
        s["w_in"] = wait_weight(i, W_IN, xcur)
        if i % 2 == 0:
            s["h"], s["u"] = mm_in(f"l{i}_conv_in_glu", xcur, s["w_in"], j, conv_b_in4, glu=True)
            s["c"] = _dwconv31(f"l{i}_dwconv", s["u"], conv_w_dw, conv_b_dw3, j, seq)
            s["s"] = _ln_silu(f"l{i}_ln_silu", s["c"], conv_ln_g3, conv_ln_b3, j)
            b_out3 = conv_b_out3
        else:
            s["h"] = mm_in(f"l{i}_gmlp_in", xcur, s["w_in"], j, gmlp_b_in4)
            s["s"] = _sgu(f"l{i}_sgu", s["h"], gmlp_ln_g3, gmlp_ln_b3, w_tile, bs_tile, j)
            b_out3 = gmlp_b_out3
        s["w_out"] = wait_weight(i, W_OUT, s["s"]).reshape(d, d)
        s["x1"], s["xhat1"], s["rstd1"] = _matmul_ln(
            f"l{i}_mixer_out_norm1", s["s"], s["w_out"], xcur, b_out3, n1g3, n1b3, j, i)
        s["w_up"] = wait_weight(i, W_UP, s["x1"])
        s["hg"], s["hv"], s["a"] = _ffn_up_act(f"l{i}_ffn_up_act", s["x1"], s["w_up"], ffn_b_up4, ffn_w_dw, ffn_b_dw4,
                                               i, seq)
        s["w_down"] = wait_weight(i, W_DOWN, s["a"]).reshape(half, fb, d)
        xcur, s["xhat2"], s["rstd2"] = _matmul_ln(
            f"l{i}_ffn_down_norm2", s["a"], s["w_down"], s["x1"], ffn_b_down3, n2g3, n2b3, i, i)
        saved.append(s)

    loss_row, dx = _loss("loss", xcur, target)

    started = {n: [None] * p[n].shape[0] for n in BIG}
    tokens = []

    def send_grads(name, items):
        done, token = _scatter_start(name, [g for _, _, g in items])
        for (n, l, _), st in zip(items, done):
            started[n][l] = st
        tokens.append(token)

    def take_tokens():
        out = list(tokens)
        tokens.clear()
        return out

    gl = {n: [None] * p[n].shape[0] for n in REPLICATED + SMALL_SHARDED}
    dr2, gl["norm2_g"][DEPTH - 1], gl["norm2_b"][DEPTH - 1], gl["ffn_b_down"][DEPTH - 1] = _ln_res_bwd(
        f"l{DEPTH - 1}_norm2_bwd", dx, saved[-1]["xhat2"], saved[-1]["rstd2"], n2g3, DEPTH - 1)
    for i in reversed(range(DEPTH)):
        j = i // 2
        s = saved[i]
        mix = "conv" if i % 2 == 0 else "gmlp"
        g_down = mm_down_dw(f"l{i}_ffn_down_dw", s["a"], dr2, deps=take_tokens()).reshape(N_DEV, -1, d)
        send_grads(f"l{i}_ffn_down_grad_scatter_start", [("ffn_w_down", i, g_down)])
        dcg, dcv, dbg, dbv, dwg, dwv = _ffn_act_bwd(f"l{i}_ffn_act_bwd", dr2, s["w_down"], s["hg"], s["hv"],
                                                    ffn_w_dw, ffn_b_dw4, i, seq, deps=take_tokens())
        gl["ffn_b_dw"][i] = jnp.concatenate([dbg, dbv], axis=0).reshape(1, nblk * fb)
        gl["ffn_w_dw"][i] = jnp.concatenate([dwg[:, :FFN_K], dwv[:, :FFN_K]], axis=0)
        dh, dbu, dr1, gl["norm1_g"][i], gl["norm1_b"][i], gl[mix + "_b_out"][j] = _ffn_conv_t_dx(
            f"l{i}_ffn_conv_t_dx", dcg, dcv, ffn_w_dw, s["w_up"], dr2, s["xhat1"], s["rstd1"], n1g3, i, seq,
            deps=take_tokens())
        gl["ffn_b_up"][i] = dbu.reshape(1, nblk * fb)
        send_grads(f"l{i}_ffn_up_grad_scatter_start", [("ffn_w_up", i, mm_up_dw(f"l{i}_ffn_up_dw", s["x1"], dh))])
        ds = mm_out_dx(f"l{i}_{mix}_out_dx", dr1, s["w_out"], deps=take_tokens())
        g_out = mm_out_dw(f"l{i}_{mix}_out_dw", s["s"], dr1).reshape(N_DEV, -1, d)
        if i % 2 == 0:
            dc, gl["conv_ln_g"][j], gl["conv_ln_b"][j], gl["conv_b_dw"][j] = _ln_silu_bwd(
                f"l{i}_ln_silu_bwd", ds, s["c"], conv_ln_g3, conv_ln_b3, j)
            du, dwdw = _dwconv31_bwd(f"l{i}_dwconv_bwd", dc, s["u"], conv_w_dw, j, seq)
            gl["conv_w_dw"][j] = dwdw[:CONV_K]
            dh, gl["conv_b_in"][j] = _glu_bwd(f"l{i}_glu_bwd", du, s["h"])
        else:
            dh, gl["gmlp_b_in"][j], dwt, dbt, gl["gmlp_ln_g"][j], gl["gmlp_ln_b"][j] = _sgu_bwd(
                f"l{i}_sgu_bwd", ds, s["h"], gmlp_ln_g3, gmlp_ln_b3, w_tile, bs_tile, j)
            gl["gmlp_w_s"][j], gl["gmlp_b_s"][j] = _sgu_param_grads(dwt, dbt)
        dx = mm_in_dx(f"l{i}_{mix}_in_dx", dh, s["w_in"], dr1)
        if i > 0:
            prev = saved[i - 1]
            dr2, gl["norm2_g"][i - 1], gl["norm2_b"][i - 1], gl["ffn_b_down"][i - 1] = _ln_res_bwd(
                f"l{i - 1}_norm2_bwd", dx, prev["xhat2"], prev["rstd2"], n2g3, i - 1)
        send_grads(f"l{i}_mixer_grads_scatter_start",
                   [(mix + "_w_out", j, g_out), (mix + "_w_in", j, mm_in_dw(f"l{i}_{mix}_in_dw", s["x"], dh))])
    grad_x = _from_segments(dx, TM_EW).reshape(bsz, seq, d)

    full_small = {n: jnp.stack(gl[n]).reshape(p[n].shape) for n in REPLICATED}
    shard_small = {}
    for n in SMALL_SHARDED:
        if n == "ffn_w_dw":
            shard_small[n] = jnp.stack(gl[n]).transpose(1, 0, 2, 3)
        else:
            width = p[n].shape[-1]
            lead = p[n].shape[:-1]
            shard_small[n] = _to_shards(jnp.stack(gl[n]).reshape(lead + (N_DEV * width,)), width)
    flat_shapes = [(1, LANES)] + [p[n].shape for n in REPLICATED] + [(N_DEV,) + p[n].shape for n in SMALL_SHARDED]
    flat_local = _pack([loss_row] + [full_small[n] for n in REPLICATED] + [shard_small[n] for n in SMALL_SHARDED],
                       row_multiple=ROW_TILE_CAP)

    small_send, small_recv, small_land, small_token = _gather_start(
        "small_grads_gather_start", _place_own("small_grads_place_own", [flat_local]))

    grads, delta, new_m, new_v = {}, {}, {}, {}
    dev1 = jnp.reshape(dev, (1,)).astype(jnp.int32)
    order = ["ffn_w_down", "ffn_w_up", "gmlp_w_out", "gmlp_w_in", "conv_w_out", "conv_w_in"]
    after = small_token
    for n in order:
        parts_done, lands_done = _scatter_wait(f"grads_{n}_scatter_wait", started[n], after)
        state = [p[n], p["m_" + n], p["v_" + n]]
        if n == "ffn_w_up":
            state = [a.transpose(0, 2, 1) for a in state]
        outs = _sum8_adamw(f"adamw_{n}", dev1, lands_done, parts_done, *state)
        after = outs[-1]
        if n == "ffn_w_up":
            outs = [a.transpose(0, 2, 1) for a in outs]
        grads[n], delta[n], new_m[n], new_v[n] = outs

    small_parts = _gather_wait("small_grads_gather_wait", small_land[0], small_send, small_recv, 0, after)
    summed = _unpack(_sum8("sum_small_grads", small_parts), flat_shapes)
    loss = summed[0][0, 0]
    grads.update(zip(REPLICATED, summed[1:1 + len(REPLICATED)]))
    for n, g in zip(SMALL_SHARDED, summed[1 + len(REPLICATED):]):
        grads[n] = lax.dynamic_index_in_dim(g, dev, axis=0, keepdims=False)
    small = REPLICATED + SMALL_SHARDED
    d_s, m_s, v_s = _adamw_small("adamw_small", [grads[n] for n in small], [p[n] for n in small],
                                 [p["m_" + n] for n in small], [p["v_" + n] for n in small])
    for n, dd, mm, vv in zip(small, d_s, m_s, v_s):
        delta[n], new_m[n], new_v[n] = dd, mm, vv

    return (loss, grad_x, *[grads[n] for n in WEIGHTS], *[delta[n] for n in WEIGHTS],
            *[new_m[n] for n in WEIGHTS], *[new_v[n] for n in WEIGHTS])


def kernel(x, conv_w_in, conv_b_in, conv_w_dw, conv_b_dw, conv_ln_g, conv_ln_b, conv_w_out, conv_b_out, gmlp_w_in, gmlp_b_in, gmlp_ln_g, gmlp_ln_b, gmlp_w_s, gmlp_b_s, gmlp_w_out, gmlp_b_out, ffn_w_up, ffn_b_up, ffn_w_dw, ffn_b_dw, ffn_w_down, ffn_b_down, norm1_g, norm1_b, norm2_g, norm2_b, loss_target, m_conv_w_in, m_conv_b_in, m_conv_w_dw, m_conv_b_dw, m_conv_ln_g, m_conv_ln_b, m_conv_w_out, m_conv_b_out, m_gmlp_w_in, m_gmlp_b_in, m_gmlp_ln_g, m_gmlp_ln_b, m_gmlp_w_s, m_gmlp_b_s, m_gmlp_w_out, m_gmlp_b_out, m_ffn_w_up, m_ffn_b_up, m_ffn_w_dw, m_ffn_b_dw, m_ffn_w_down, m_ffn_b_down, m_norm1_g, m_norm1_b, m_norm2_g, m_norm2_b, v_conv_w_in, v_conv_b_in, v_conv_w_dw, v_conv_b_dw, v_conv_ln_g, v_conv_ln_b, v_conv_w_out, v_conv_b_out, v_gmlp_w_in, v_gmlp_b_in, v_gmlp_ln_g, v_gmlp_ln_b, v_gmlp_w_s, v_gmlp_b_s, v_gmlp_w_out, v_gmlp_b_out, v_ffn_w_up, v_ffn_b_up, v_ffn_w_dw, v_ffn_b_dw, v_ffn_w_down, v_ffn_b_down, v_norm1_g, v_norm1_b, v_norm2_g, v_norm2_b):
    return _step(dict(locals()))
```

```python
import math

import jax
import jax.numpy as jnp
from jax import lax
from jax.experimental import pallas as pl
from jax.experimental.pallas import tpu as pltpu

F32 = jnp.float32
BF16 = jnp.bfloat16
MESH = pl.DeviceIdType.MESH

N_DEV = 8
DEPTH = 4
ALPHA = (2.0 * DEPTH) ** 0.25
LN_EPS = 1e-5
CONV_K = 31
FFN_K = 3
CHUNK = 128
GROUPS = 8
ADAM_LR = 0.001
ADAM_B1 = 0.9
ADAM_B2 = 0.999
ADAM_EPS = 1e-08
ADAM_WD = 0.01
ADAM_STEP = 10
INV_SQRT2 = 1.0 / math.sqrt(2.0)
INV_SQRT2PI = 1.0 / math.sqrt(2.0 * math.pi)

LANES = 128
SUBLANES = 8
VMEM_LIMIT = 48 * 1024 * 1024
TM_MM = 1024
TK_DW = 2048
TM_EW = 256
TM_ROW = 512


def _call(body, name, grid, in_specs, out_specs, out_shape, scratch=(), aliases=None, deps=()):
    deps = list(deps)
    in_specs = list(in_specs)
    n_in = len(in_specs)
    if deps:
        inner = body

        def body(*refs):
            return inner(*refs[:n_in], *refs[n_in + len(deps):])

        in_specs = in_specs + [pl.BlockSpec(memory_space=pl.ANY)] * len(deps)
    fn = pl.pallas_call(
        body, name=name, grid=grid, in_specs=in_specs, out_specs=out_specs, out_shape=out_shape,
        scratch_shapes=list(scratch), input_output_aliases=aliases or {},
        compiler_params=pltpu.CompilerParams(vmem_limit_bytes=VMEM_LIMIT))
    return lambda *args: fn(*args, *deps)


def _sds(shape, dtype):
    return jax.ShapeDtypeStruct(tuple(shape), dtype)


def _sigmoid(x):
    return 1.0 / (1.0 + jnp.exp(-x))


def _acc_rows(ref, val, first):
    @pl.when(first)
    def _():
        ref[...] = val

    @pl.when(jnp.logical_not(first))
    def _():
        ref[...] += val


def _colsum(v):
    return jnp.sum(v, axis=0, keepdims=True)


_DIMS = {"nn": ((1,), (0,)), "nt": ((1,), (1,)), "tn": ((0,), (0,))}


def _matmul(name, a, b, mode, *, grid, a_spec, b_spec, o_spec, o_shape, o_dtype, k_axis=None, nk=1,
            acc_shape=None, bias=None, bias_spec=None, res=None, res_spec=None, res_scale=1.0, deps=()):
    dims = (_DIMS[mode], ((), ()))
    has_bias, has_res = bias is not None, res is not None

    def body(*refs):
        a_ref, b_ref = refs[0], refs[1]
        pos = 2
        bias_ref = res_ref = None
        if has_bias:
            bias_ref = refs[pos]
            pos += 1
        if has_res:
            res_ref = refs[pos]
            pos += 1
        o_ref = refs[pos]
        acc_ref = refs[pos + 1] if nk > 1 else None
        p = lax.dot_general(a_ref[...].astype(BF16), b_ref[...].astype(BF16), dims, preferred_element_type=F32)

        def finish(acc):
            if has_bias:
                acc = acc + bias_ref[...]
            if has_res:
                acc = acc + res_scale * res_ref[...]
            o_ref[...] = acc.astype(o_dtype)

        if nk == 1:
            finish(p)
        else:
            k = pl.program_id(k_axis)

            @pl.when(k == 0)
            def _():
                acc_ref[...] = p

            @pl.when(k > 0)
            def _():
                acc_ref[...] += p

            @pl.when(k == nk - 1)
            def _():
                finish(acc_ref[...])

    ins, specs = [a, b], [a_spec, b_spec]
    if has_bias:
        ins.append(bias)
        specs.append(bias_spec)
    if has_res:
        ins.append(res)
        specs.append(res_spec)
    scratch = [pltpu.VMEM(acc_shape, F32)] if nk > 1 else []
    return _call(body, name, grid, specs, o_spec, _sds(o_shape, o_dtype), scratch, deps=deps)(*ins)


TM_LN = 512


def _matmul_ln(name, a, b, x_res, bias3, g3, b3, l_bias, l_norm):
    t, d = x_res.shape
    tm = min(TM_LN, t)
    blocked = a.ndim == 3

    def body(a_ref, b_ref, x_ref, bias_ref, g_ref, be_ref, o_ref, xh_ref, rs_ref):
        if blocked:
            y = None
            for k in range(a.shape[0]):
                p = jnp.dot(a_ref[k], b_ref[k], preferred_element_type=F32)
                y = p if y is None else y + p
        else:
            y = jnp.dot(a_ref[...], b_ref[...], preferred_element_type=F32)
        xhat, rstd = _ln_stats(ALPHA * x_ref[...] + y + bias_ref[...])
        o_ref[...] = xhat * g_ref[...] + be_ref[...]
        xh_ref[...] = xhat
        rs_ref[...] = rstd

    if blocked:
        a_spec = pl.BlockSpec((a.shape[0], tm, a.shape[2]), lambda i: (0, i, 0))
        b_spec = pl.BlockSpec(b.shape, lambda i: (0, 0, 0))
    else:
        a_spec = pl.BlockSpec((tm, a.shape[1]), lambda i: (i, 0))
        b_spec = pl.BlockSpec(b.shape, lambda i: (0, 0))
    row = pl.BlockSpec((tm, d), lambda i: (i, 0))
    stat = pl.BlockSpec((tm, 1), lambda i: (i, 0))

    def vec(l):
        return pl.BlockSpec((None, 1, d), lambda i: (l, 0, 0))

    return _call(body, name, (t // tm,), [a_spec, b_spec, row, vec(l_bias), vec(l_norm), vec(l_norm)],
                 [row, row, stat], [_sds((t, d), F32), _sds((t, d), F32), _sds((t, 1), F32)])(
                     a, b, x_res, bias3, g3, b3)


def _mesh_pos():
    return lax.axis_index("x"), lax.axis_index("y"), lax.axis_index("c")


def _any_specs(n):
    return [pl.BlockSpec(memory_space=pl.ANY)] * n


def _all_gather(name, srcs):
    n = len(srcs)

    def body(*refs):
        src, out = refs[:n], refs[n:2 * n]
        send_sems, recv_sems, local_sems = refs[2 * n:]
        x, y, c = _mesh_pos()
        me, sibling = (x, y, c), (x, y, 1 - c)
        chips = [(1 - x, y), (x, 1 - y), (1 - x, 1 - y)]

        def slot(k, p):
            return out[k].at[:, 4 * p[0] + 2 * p[1] + p[2]]

        def copy(k, idx, block, to, s=None):
            return pltpu.make_async_remote_copy(
                src_ref=slot(k, block) if s is None else s, dst_ref=slot(k, block),
                send_sem=send_sems.at[k * 7 + idx], recv_sem=recv_sems.at[k * 7 + idx],
                device_id=to, device_id_type=MESH)

        local = [pltpu.make_async_copy(src[k], slot(k, me), local_sems.at[k]) for k in range(n)]
        for cp in local:
            cp.start()
        first = []
        for k in range(n):
            first.append(copy(k, 0, me, sibling, src[k]))
            for j, chip in enumerate(chips):
                first.append(copy(k, 1 + j, me, (*chip, c), src[k]))
        for cp in first:
            cp.start()
        passed = []
        for j, chip in enumerate(chips):
            for k in range(n):
                copy(k, 1 + j, (*chip, c), me).wait_recv()
                cp = copy(k, 4 + j, (*chip, c), sibling)
                cp.start()
                passed.append(cp)
        for k in range(n):
            copy(k, 0, sibling, me).wait_recv()
            for j, chip in enumerate(chips):
                copy(k, 4 + j, (*chip, 1 - c), me).wait_recv()
        for cp in first + passed:
            cp.wait_send()
        for cp in local:
            cp.wait()

    out_shape = [_sds((s.shape[0], N_DEV) + s.shape[1:], s.dtype) for s in srcs]
    return _call(body, name, (), [pl.BlockSpec(memory_space=pltpu.VMEM)] * n, _any_specs(n), out_shape,
                 [pltpu.SemaphoreType.DMA((7 * n,)), pltpu.SemaphoreType.DMA((7 * n,)),
                  pltpu.SemaphoreType.DMA((n,))])(*srcs)


HBM_SPEC = pl.BlockSpec(memory_space=pltpu.HBM)
SEM_SPEC = pl.BlockSpec(memory_space=pltpu.SEMAPHORE)
N_PEER = N_DEV - 1


def _split_call(body, name, in_specs, out_specs, out_shape, aliases):
    return pl.pallas_call(
        body, name=name, in_specs=in_specs, out_specs=out_specs, out_shape=out_shape, input_output_aliases=aliases,
        compiler_params=pltpu.CompilerParams(has_side_effects=pltpu.SideEffectType.DATAFLOW_SIDE_EFFECTING))


def _peers(x, y, c):
    return [(1 - x if q & 4 else x, 1 - y if q & 2 else y, 1 - c if q & 1 else c) for q in range(1, N_DEV)]


def _in_hbm(a):
    return pltpu.with_memory_space_constraint(a, pltpu.HBM)


def _place_own(name, srcs, deps=()):
    n = len(srcs)

    def body(*refs):
        src, out, sems = refs[:n], refs[n:2 * n], refs[2 * n]
        x, y, c = _mesh_pos()
        dev = 4 * x + 2 * y + c
        copies = [pltpu.make_async_copy(src[k], out[k].at[dev], sems.at[k]) for k in range(n)]
        for cp in copies:
            cp.start()
        for cp in copies:
            cp.wait()

    return _call(body, name, (), [pl.BlockSpec(memory_space=pltpu.VMEM)] * n, _any_specs(n),
                 [_sds((N_DEV,) + s.shape, s.dtype) for s in srcs], [pltpu.SemaphoreType.DMA((n,))],
                 deps=deps)(*srcs)


def _gather_start(name, lands):
    n = len(lands)

    def body(*refs):
        land, send_sems, recv_sems = refs[:n], refs[n], refs[n + 1]
        x, y, c = _mesh_pos()
        dev = 4 * x + 2 * y + c
        for k in range(n):
            for peer in _peers(x, y, c):
                pltpu.make_async_remote_copy(
                    src_ref=land[k].at[dev], dst_ref=land[k].at[dev], send_sem=send_sems.at[k],
                    recv_sem=recv_sems.at[k], device_id=peer, device_id_type=MESH).start()
        token = refs[-1]
        token[...] = jnp.zeros_like(token)

    outs = _split_call(
        body, name, [HBM_SPEC] * n, [SEM_SPEC, SEM_SPEC] + [HBM_SPEC] * n + [pl.BlockSpec(memory_space=pltpu.VMEM)],
        [pltpu.SemaphoreType.DMA((n,)), pltpu.SemaphoreType.DMA((n,))] + [pltpu.HBM(a.shape, a.dtype) for a in lands]
        + [_sds((SUBLANES, LANES), F32)],
        {k: 2 + k for k in range(n)})(*[_in_hbm(a) for a in lands])
    return outs[0], outs[1], list(outs[2:2 + n]), outs[-1]


def _wait_seven(src_ref, dst_ref, send_sem, recv_sem):
    cp = pltpu.make_async_remote_copy(
        src_ref=src_ref.at[pl.ds(0, N_PEER)], dst_ref=dst_ref.at[pl.ds(0, N_PEER)], send_sem=send_sem,
        recv_sem=recv_sem, device_id=_mesh_pos(), device_id_type=MESH)
    cp.wait_send()
    cp.wait_recv()


def _gather_wait(name, land, send_sems, recv_sems, k, after):
    def body(land_ref, send_ref, recv_ref, after_ref, out_ref):
        _wait_seven(land_ref, land_ref, send_ref.at[k], recv_ref.at[k])

    return _split_call(body, name, [HBM_SPEC, SEM_SPEC, SEM_SPEC, pl.BlockSpec(memory_space=pl.ANY)], HBM_SPEC,
                       pltpu.HBM(land.shape, land.dtype), {0: 0})(land, send_sems, recv_sems, after)


def _scatter_start(name, parts_list):
    n = len(parts_list)

    def body(*refs):
        x, y, c = _mesh_pos()
        dev = 4 * x + 2 * y + c
        for k in range(n):
            parts_ref, land_ref = refs[2 * k], refs[2 * k + 1]
            send_sem, recv_sem = refs[2 * n + 4 * k], refs[2 * n + 4 * k + 1]
            for peer in _peers(x, y, c):
                pltpu.make_async_remote_copy(
                    src_ref=parts_ref.at[4 * peer[0] + 2 * peer[1] + peer[2]], dst_ref=land_ref.at[dev],
                    send_sem=send_sem, recv_sem=recv_sem, device_id=peer, device_id_type=MESH).start()
        token = refs[-1]
        token[...] = jnp.zeros_like(token)

    ins, out_specs, out_shape, aliases = [], [], [], {}
    for k, parts in enumerate(parts_list):
        buf = pltpu.HBM(parts.shape, parts.dtype)
        ins += [_in_hbm(parts), _in_hbm(lax.empty(parts.shape, parts.dtype))]
        out_specs += [SEM_SPEC, SEM_SPEC, HBM_SPEC, HBM_SPEC]
        out_shape += [pltpu.SemaphoreType.DMA(()), pltpu.SemaphoreType.DMA(()), buf, buf]
        aliases.update({2 * k: 4 * k + 2, 2 * k + 1: 4 * k + 3})
    outs = _split_call(body, name, [HBM_SPEC] * (2 * n), out_specs + [pl.BlockSpec(memory_space=pltpu.VMEM)],
                       out_shape + [_sds((SUBLANES, LANES), F32)], aliases)(*ins)
    return [tuple(outs[4 * k:4 * k + 4]) for k in range(n)], outs[-1]


def _scatter_wait(name, started, after):
    n = len(started)

    def body(*refs):
        for k in range(n):
            send_sem, recv_sem, parts_ref, land_ref = refs[4 * k:4 * k + 4]
            _wait_seven(parts_ref, land_ref, send_sem, recv_sem)

    flat = [a for s in started for a in s]
    outs = _split_call(
        body, name, [SEM_SPEC, SEM_SPEC, HBM_SPEC, HBM_SPEC] * n + [pl.BlockSpec(memory_space=pl.ANY)],
        [HBM_SPEC, HBM_SPEC] * n, [pltpu.HBM(a.shape, a.dtype) for s in started for a in s[2:]],
        {4 * k + 2 + t: 2 * k + t for k in range(n) for t in range(2)})(*flat, after)
    return list(outs[0::2]), list(outs[1::2])


def _to_segments(a, tile):
    seg = tile // SUBLANES
    return a.reshape((a.shape[0] // tile, SUBLANES, seg) + a.shape[1:]).swapaxes(1, 2).reshape(a.shape)


def _from_segments(a, tile):
    seg = tile // SUBLANES
    return a.reshape((a.shape[0] // tile, seg, SUBLANES) + a.shape[1:]).swapaxes(1, 2).reshape(a.shape)


def _chunk(ref, q):
    return ref[q * SUBLANES:(q + 1) * SUBLANES, :]


def _fill_wrap_prev(x_ref, halo_ref, wrap_ref, n_wrap, n_halo, seg, keep):
    sub = lax.broadcasted_iota(jnp.int32, (SUBLANES, x_ref.shape[-1]), 0)
    for j in range(n_wrap):
        q = seg - n_wrap + j
        hq = q - (seg - n_halo)
        row = halo_ref[hq * SUBLANES + SUBLANES - 1:(hq + 1) * SUBLANES, :] * keep
        wrap_ref[j * SUBLANES:(j + 1) * SUBLANES, :] = jnp.where(sub == 0, row, pltpu.roll(_chunk(x_ref, q), 1, 0))


def _fill_wrap_next(x_ref, halo_ref, wrap_ref, n_wrap, keep):
    sub = lax.broadcasted_iota(jnp.int32, (SUBLANES, x_ref.shape[-1]), 0)
    for j in range(n_wrap):
        row = halo_ref[j * SUBLANES:j * SUBLANES + 1, :] * keep
        wrap_ref[j * SUBLANES:(j + 1) * SUBLANES, :] = jnp.where(
            sub == SUBLANES - 1, row, pltpu.roll(_chunk(x_ref, j), SUBLANES - 1, 0))


def _past(x_ref, wrap_ref, q, d, n_wrap):
    return _chunk(x_ref, q - d) if q >= d else _chunk(wrap_ref, q - d + n_wrap)


def _future(x_ref, wrap_ref, q, d, seg):
    return _chunk(x_ref, q + d) if q + d < seg else _chunk(wrap_ref, q + d - seg)


def _conv_fwd(x_ref, wrap_ref, w_ref, b_ref, out_ref, seg, k_taps):
    bias = jnp.broadcast_to(b_ref[...], (SUBLANES, x_ref.shape[-1]))
    for q in range(seg):
        acc = bias
        for k in range(k_taps):
            acc = acc + w_ref[k:k + 1, :] * _past(x_ref, wrap_ref, q, k_taps - 1 - k, k_taps - 1)
        out_ref[q * SUBLANES:(q + 1) * SUBLANES, :] = acc


def _conv_bwd_data(d_ref, wrap_ref, w_ref, out_ref, seg, k_taps):
    for q in range(seg):
        acc = None
        for k in range(k_taps):
            term = w_ref[k:k + 1, :] * _future(d_ref, wrap_ref, q, k_taps - 1 - k, seg)
            acc = term if acc is None else acc + term
        out_ref[q * SUBLANES:(q + 1) * SUBLANES, :] = acc


def _conv_bwd_taps(d_ref, x_ref, wrap_ref, dw_ref, seg, k_taps):
    for k in range(k_taps):
        part = None
        for q in range(seg):
            term = _chunk(d_ref, q) * _past(x_ref, wrap_ref, q, k_taps - 1 - k, k_taps - 1)
            part = term if part is None else part + term
        dw_ref[k:k + 1, :] += _colsum(part)


def _tile_halo_specs(tm, width_block, n_halo, n_tiles, block_of):
    rows = n_halo * SUBLANES
    per = tm // rows
    tile = pl.BlockSpec(width_block(tm), lambda n, i: block_of(n, i))
    prev = pl.BlockSpec(width_block(rows), lambda n, i: block_of(n, jnp.maximum(i * per - 1, 0)))
    nxt = pl.BlockSpec(width_block(rows), lambda n, i: block_of(n, jnp.minimum((i + 1) * per, n_tiles * per - 1)))
    return tile, prev, nxt


def _ln_stats(v):
    mu = jnp.mean(v, axis=-1, keepdims=True)
    vc = v - mu
    var = jnp.mean(vc * vc, axis=-1, keepdims=True)
    rstd = lax.rsqrt(var + LN_EPS)
    return vc * rstd, rstd


def _ln_backward(dxhat, xhat, rstd):
    m1 = jnp.mean(dxhat, axis=-1, keepdims=True)
    m2 = jnp.mean(dxhat * xhat, axis=-1, keepdims=True)
    return rstd * (dxhat - m1 - xhat * m2)


def _row_spec(tm, width):
    return pl.BlockSpec((tm, width), lambda i: (i, 0))


def _param_spec(l, width):
    return pl.BlockSpec((None, 1, width), lambda *_: (l, 0, 0))


def _ln_bwd_rows(dout, xh_ref, rs_ref, g_ref, dr_ref, dg_ref, db_ref, dsum_ref, first):
    xhat = xh_ref[...]
    dr = _ln_backward(dout * g_ref[...], xhat, rs_ref[...])
    dr_ref[...] = dr
    _acc_rows(dg_ref, _colsum(dout * xhat), first)
    _acc_rows(db_ref, _colsum(dout), first)
    _acc_rows(dsum_ref, _colsum(dr), first)


def _ln_bwd_specs(tm, d, l, row_of):
    vec = pl.BlockSpec((1, d), lambda *_: (0, 0))
    ins = [pl.BlockSpec((tm, d), row_of), pl.BlockSpec((tm, 1), row_of), _param_spec(l, d)]
    return ins, [pl.BlockSpec((tm, d), row_of), vec, vec, vec]


def _ln_res_bwd(name, dout, xhat, rstd, g3, l, deps=()):
    t, d = dout.shape
    tm = min(TM_ROW, t)

    def body(do_ref, xh_ref, rs_ref, g_ref, dr_ref, dg_ref, db_ref, dc_ref):
        _ln_bwd_rows(do_ref[...], xh_ref, rs_ref, g_ref, dr_ref, dg_ref, db_ref, dc_ref, pl.program_id(0) == 0)

    ins, outs = _ln_bwd_specs(tm, d, l, lambda i: (i, 0))
    return _call(body, name, (t // tm,), [_row_spec(tm, d)] + ins, outs,
                 [_sds((t, d), F32)] + [_sds((1, d), F32)] * 3, deps=deps)(dout, xhat, rstd, g3)


def _glu_bwd(name, du, h):
    t, c2 = h.shape
    c = c2 // 2
    tm = min(TM_ROW, t)

    def body(du_ref, a_ref, g_ref, dh_ref, db_ref):
        first = pl.program_id(0) == 0
        du_v, a = du_ref[...], a_ref[...]
        sg = _sigmoid(g_ref[...])
        da = du_v * sg
        dg = du_v * a * sg * (1.0 - sg)
        dh_ref[:, :c] = da.astype(BF16)
        dh_ref[:, c:] = dg.astype(BF16)
        _acc_rows(db_ref.at[:, :c], _colsum(da), first)
        _acc_rows(db_ref.at[:, c:], _colsum(dg), first)

    return _call(body, name, (t // tm,),
                 [_row_spec(tm, c), pl.BlockSpec((tm, c), lambda i: (i, 0)), pl.BlockSpec((tm, c), lambda i: (i, 1))],
                 [_row_spec(tm, c2), pl.BlockSpec((1, c2), lambda i: (0, 0))],
                 [_sds((t, c2), BF16), _sds((1, c2), F32)])(du, h, h)


CONV_CB = 512
TAPS_PAD = 32


def _dwconv31(name, u, w3, b3, l, seq):
    t, c = u.shape
    tm, cb = TM_EW, CONV_CB
    seg, seq_tiles, n_tiles = tm // SUBLANES, seq // tm, t // tm
    n_wrap = CONV_K - 1
    tile, prev, _ = _tile_halo_specs(tm, lambda rows: (rows, cb), seg, n_tiles, lambda n, r: (r, n))

    def body(u_ref, halo_ref, w_ref, b_ref, o_ref, wrap_ref):
        keep = (pl.program_id(1) % seq_tiles != 0).astype(F32)
        _fill_wrap_prev(u_ref, halo_ref, wrap_ref, n_wrap, seg, seg, keep)
        _conv_fwd(u_ref, wrap_ref, w_ref, b_ref, o_ref, seg, CONV_K)

    return _call(body, name, (c // cb, n_tiles),
                 [tile, prev, pl.BlockSpec((None, CONV_K, cb), lambda n, i: (l, 0, n)),
                  pl.BlockSpec((None, 1, cb), lambda n, i: (l, 0, n))],
                 tile, _sds((t, c), F32), [pltpu.VMEM((n_wrap * SUBLANES, cb), F32)])(u, u, w3, b3)


def _dwconv31_bwd(name, dc, u, w3, l, seq):
    t, c = dc.shape
    tm, cb = TM_EW, CONV_CB
    seg, seq_tiles, n_tiles = tm // SUBLANES, seq // tm, t // tm
    n_wrap = CONV_K - 1
    tile, prev, nxt = _tile_halo_specs(tm, lambda rows: (rows, cb), seg, n_tiles, lambda n, r: (r, n))

    def body(dc_ref, dcn_ref, u_ref, up_ref, w_ref, du_ref, dw_ref, dwrap_ref, uwrap_ref):
        i = pl.program_id(1)
        keep_prev = (i % seq_tiles != 0).astype(F32)
        keep_next = (i % seq_tiles != seq_tiles - 1).astype(F32)
        _fill_wrap_next(dc_ref, dcn_ref, dwrap_ref, n_wrap, keep_next)
        _conv_bwd_data(dc_ref, dwrap_ref, w_ref, du_ref, seg, CONV_K)

        @pl.when(i == 0)
        def _():
            dw_ref[...] = jnp.zeros_like(dw_ref)

        _fill_wrap_prev(u_ref, up_ref, uwrap_ref, n_wrap, seg, seg, keep_prev)
        _conv_bwd_taps(dc_ref, u_ref, uwrap_ref, dw_ref, seg, CONV_K)

    wrap = pltpu.VMEM((n_wrap * SUBLANES, cb), F32)
    return _call(body, name, (c // cb, n_tiles),
                 [tile, nxt, tile, prev, pl.BlockSpec((None, CONV_K, cb), lambda n, i: (l, 0, n))],
                 [tile, pl.BlockSpec((TAPS_PAD, cb), lambda n, i: (0, n))],
                 [_sds((t, c), F32), _sds((TAPS_PAD, c), F32)], [wrap, wrap])(dc, dc, u, u, w3)


def _ln_silu(name, cx, g3, b3, l):
    t, d = cx.shape
    tm = min(TM_ROW, t)

    def body(c_ref, g_ref, b_ref, o_ref):
        xhat, _ = _ln_stats(c_ref[...])
        nv = xhat * g_ref[...] + b_ref[...]
        o_ref[...] = (nv * _sigmoid(nv)).astype(BF16)

    return _call(body, name, (t // tm,), [_row_spec(tm, d), _param_spec(l, d), _param_spec(l, d)],
                 _row_spec(tm, d), _sds((t, d), BF16))(cx, g3, b3)


def _ln_silu_bwd(name, ds, cx, g3, b3, l, deps=()):
    t, d = cx.shape
    tm = min(TM_ROW, t)

    def body(ds_ref, c_ref, g_ref, b_ref, dc_ref, dg_ref, db_ref, dsum_ref):
        first = pl.program_id(0) == 0
        xhat, rstd = _ln_stats(c_ref[...])
        g = g_ref[...]
        nv = xhat * g + b_ref[...]
        sg = _sigmoid(nv)
        dn = ds_ref[...] * (sg * (1.0 + nv * (1.0 - sg)))
        dc = _ln_backward(dn * g, xhat, rstd)
        dc_ref[...] = dc
        _acc_rows(dg_ref, _colsum(dn * xhat), first)
        _acc_rows(db_ref, _colsum(dn), first)
        _acc_rows(dsum_ref, _colsum(dc), first)

    vec = pl.BlockSpec((1, d), lambda i: (0, 0))
    return _call(body, name, (t // tm,),
                 [_row_spec(tm, d), _row_spec(tm, d), _param_spec(l, d), _param_spec(l, d)],
                 [_row_spec(tm, d), vec, vec, vec],
                 [_sds((t, d), F32)] + [_sds((1, d), F32)] * 3, deps=deps)(ds, cx, g3, b3)


FFN_HALO = FFN_K - 1


def _ffn_conv(x_ref, halo_ref, wrap_ref, w_ref, b_ref, keep, seg, out_ref):
    _fill_wrap_prev(x_ref, halo_ref, wrap_ref, FFN_K - 1, FFN_HALO, seg, keep)
    _conv_fwd(x_ref, wrap_ref, w_ref, b_ref, out_ref, seg, FFN_K)


TM_FFN = 512
WRAP_ROWS = FFN_HALO * SUBLANES


def _sub_tiles(x_ref, prev_ref, next_ref, keep_prev, keep_next, n_sub):
    out = []
    for s in range(n_sub):
        tile = x_ref.at[pl.ds(s * TM_EW, TM_EW)]
        prev = prev_ref if s == 0 else x_ref.at[pl.ds(s * TM_EW - WRAP_ROWS, WRAP_ROWS)]
        nxt = next_ref if s == n_sub - 1 else x_ref.at[pl.ds((s + 1) * TM_EW, WRAP_ROWS)]
        out.append((tile, prev, keep_prev if s == 0 else 1.0, nxt, keep_next if s == n_sub - 1 else 1.0))
    return out


def _rows(ref, s, rows):
    return ref.at[pl.ds(s * rows, rows)]


def _ffn_specs(tm, fb, n_tiles):
    return _tile_halo_specs(tm, lambda rows: (None, rows, fb), FFN_HALO, n_tiles, lambda n, r: (n, r, 0))


def _ffn_up_act(name, x, w_up, b_up4, wdw, bdw, l, seq):
    t, d = x.shape
    nb, fb, _ = w_up.shape
    half = nb // 2
    tm = min(TM_FFN, seq)
    n_sub, seg, seq_steps, n_steps = tm // TM_EW, TM_EW // SUBLANES, seq // tm, t // tm
    per = tm // WRAP_ROWS
    nt_dims = (_DIMS["nt"], ((), ()))

    def body(x_ref, xp_ref, ug_ref, uv_ref, bug_ref, buv_ref, wg_ref, wv_ref, bg_ref, bv_ref,
             hg_ref, hv_ref, a_ref, pg_ref, pv_ref, gwrap_ref, vwrap_ref, cg_ref, cv_ref):
        keep = (pl.program_id(1) % seq_steps != 0).astype(F32)
        xb, xpb = x_ref[...].astype(BF16), xp_ref[...].astype(BF16)
        hg_ref[...] = lax.dot_general(xb, ug_ref[...], nt_dims, preferred_element_type=F32) + bug_ref[...]
        pg_ref[...] = lax.dot_general(xpb, ug_ref[...], nt_dims, preferred_element_type=F32) + bug_ref[...]
        for s, (tile, prev, kp, _, _) in enumerate(_sub_tiles(hg_ref, pg_ref, None, keep, None, n_sub)):
            _ffn_conv(tile, prev, gwrap_ref, wg_ref, bg_ref, kp, seg, _rows(cg_ref, s, TM_EW))
        hv_ref[...] = lax.dot_general(xb, uv_ref[...], nt_dims, preferred_element_type=F32) + buv_ref[...]
        pv_ref[...] = lax.dot_general(xpb, uv_ref[...], nt_dims, preferred_element_type=F32) + buv_ref[...]
        for s, (tile, prev, kp, _, _) in enumerate(_sub_tiles(hv_ref, pv_ref, None, keep, None, n_sub)):
            _ffn_conv(tile, prev, vwrap_ref, wv_ref, bv_ref, kp, seg, _rows(cv_ref, s, TM_EW))
        cg = cg_ref[...]
        a_ref[...] = (cg * _sigmoid(cg) * cv_ref[...]).astype(BF16)

    def blk(shift):
        return pl.BlockSpec((None, fb, d), lambda n, i: (n + shift, 0, 0))

    def vec(shift, rows):
        return pl.BlockSpec((None, None, rows, fb), lambda n, i: (l, n + shift, 0, 0))

    out = pl.BlockSpec((None, tm, fb), lambda n, i: (n, i, 0))
    tmp = pltpu.VMEM((tm, fb), F32)
    halo = pltpu.VMEM((WRAP_ROWS, fb), F32)
    return _call(body, name, (half, n_steps),
                 [pl.BlockSpec((tm, d), lambda n, i: (i, 0)),
                  pl.BlockSpec((WRAP_ROWS, d), lambda n, i: (jnp.maximum(i * per - 1, 0), 0)),
                  blk(0), blk(half), vec(0, 1), vec(half, 1), vec(0, FFN_K), vec(half, FFN_K), vec(0, 1), vec(half, 1)],
                 [out, out, out],
                 [_sds((half, t, fb), F32), _sds((half, t, fb), F32), _sds((half, t, fb), BF16)],
                 [halo, halo, halo, halo, tmp, tmp])(x, x, w_up, w_up, b_up4, b_up4, wdw, wdw, bdw, bdw)


def _ffn_act_bwd(name, dy, w_down, hg, hv, wdw, bdw, l, seq, deps=()):
    half, t, fb = hg.shape
    d = dy.shape[-1]
    tm = min(TM_FFN, seq)
    n_sub, seg, seq_steps, n_steps = tm // TM_EW, TM_EW // SUBLANES, seq // tm, t // tm
    tile, prev, _ = _ffn_specs(tm, fb, n_steps)

    def body(dy_ref, wd_ref, g_ref, gp_ref, v_ref, vp_ref, wg_ref, wv_ref, bg_ref, bv_ref,
             dg_ref, dv_ref, dbg_ref, dbv_ref, dwg_ref, dwv_ref, gwrap_ref, vwrap_ref, cg_ref, cv_ref):
        i = pl.program_id(1)
        first = i == 0
        keep = (i % seq_steps != 0).astype(F32)
        da = lax.dot_general(dy_ref[...].astype(BF16), wd_ref[...], (_DIMS["nt"], ((), ())),
                             preferred_element_type=F32)
        g_tiles = _sub_tiles(g_ref, gp_ref, None, keep, None, n_sub)
        v_tiles = _sub_tiles(v_ref, vp_ref, None, keep, None, n_sub)
        for s in range(n_sub):
            _ffn_conv(g_tiles[s][0], g_tiles[s][1], _rows(gwrap_ref, s, WRAP_ROWS), wg_ref, bg_ref, g_tiles[s][2],
                      seg, _rows(cg_ref, s, TM_EW))
            _ffn_conv(v_tiles[s][0], v_tiles[s][1], _rows(vwrap_ref, s, WRAP_ROWS), wv_ref, bv_ref, v_tiles[s][2],
                      seg, _rows(cv_ref, s, TM_EW))
        cg, cv = cg_ref[...], cv_ref[...]
        sg = _sigmoid(cg)
        dcv = da * cg * sg
        dcg = da * cv * sg * (1.0 + cg * (1.0 - sg))
        dg_ref[...] = dcg
        dv_ref[...] = dcv
        _acc_rows(dbg_ref, _colsum(dcg), first)
        _acc_rows(dbv_ref, _colsum(dcv), first)

        @pl.when(first)
        def _():
            dwg_ref[...] = jnp.zeros_like(dwg_ref)
            dwv_ref[...] = jnp.zeros_like(dwv_ref)

        for s in range(n_sub):
            _conv_bwd_taps(_rows(dg_ref, s, TM_EW), g_tiles[s][0], _rows(gwrap_ref, s, WRAP_ROWS), dwg_ref, seg, FFN_K)
            _conv_bwd_taps(_rows(dv_ref, s, TM_EW), v_tiles[s][0], _rows(vwrap_ref, s, WRAP_ROWS), dwv_ref, seg, FFN_K)

    def vec(shift, rows):
        return pl.BlockSpec((None, None, rows, fb), lambda n, i: (l, n + shift, 0, 0))

    def acc(rows):
        return pl.BlockSpec((None, rows, fb), lambda n, i: (n, 0, 0))

    wrap = pltpu.VMEM((n_sub * WRAP_ROWS, fb), F32)
    tmp = pltpu.VMEM((tm, fb), F32)
    return _call(body, name, (half, n_steps),
                 [pl.BlockSpec((tm, d), lambda n, i: (i, 0)), pl.BlockSpec((None, fb, d), lambda n, i: (n, 0, 0)),
                  tile, prev, tile, prev, vec(0, FFN_K), vec(half, FFN_K), vec(0, 1), vec(half, 1)],
                 [tile, tile, acc(1), acc(1), acc(SUBLANES), acc(SUBLANES)],
                 [_sds((half, t, fb), F32), _sds((half, t, fb), F32), _sds((half, 1, fb), F32),
                  _sds((half, 1, fb), F32), _sds((half, SUBLANES, fb), F32), _sds((half, SUBLANES, fb), F32)],
                 [wrap, wrap, tmp, tmp], deps=deps)(dy, w_down, hg, hg, hv, hv, wdw, wdw, bdw, bdw)


def _ffn_conv_t_dx(name, dcg, dcv, wdw, w_up, res, xhat, rstd, g3, l, seq, deps=()):
    half, t, fb = dcg.shape
    nb, d = 2 * half, res.shape[-1]
    tm = min(TM_FFN, seq)
    n_sub, seg, seq_steps, n_steps = tm // TM_EW, TM_EW // SUBLANES, seq // tm, t // tm
    per = tm // WRAP_ROWS

    pair = 2
    n_pairs, half_pairs = nb // pair, half // pair

    def body(g_ref, gn_ref, v_ref, vn_ref, w_ref, up_ref, res_ref, xh_ref, rs_ref, gam_ref,
             dh_ref, db_ref, dr_ref, dgam_ref, dbeta_ref, dsum_ref, wrap_ref, out_ref, acc_ref):
        i, m = pl.program_id(0), pl.program_id(1)
        keep = (i % seq_steps != seq_steps - 1).astype(F32)

        def conv_t(d_ref, dn_ref, b):
            for s, (sub, _, _, nx, kn) in enumerate(_sub_tiles(d_ref.at[b], None, dn_ref.at[b], None, keep, n_sub)):
                _fill_wrap_next(sub, nx, wrap_ref, FFN_K - 1, kn)
                _conv_bwd_data(sub, wrap_ref, w_ref.at[b], _rows(out_ref, s, TM_EW), seg, FFN_K)

        p = None
        for b in range(pair):
            @pl.when(m < half_pairs)
            def _(b=b):
                conv_t(g_ref, gn_ref, b)

            @pl.when(m >= half_pairs)
            def _(b=b):
                conv_t(v_ref, vn_ref, b)

            dh = out_ref[...]
            dhb = dh.astype(BF16)
            dh_ref[b] = dhb
            _acc_rows(db_ref.at[pair * m + b], _colsum(dh), i == 0)
            part = jnp.dot(dhb, up_ref[b], preferred_element_type=F32)
            p = part if p is None else p + part

        @pl.when(m == 0)
        def _():
            acc_ref[...] = p

        @pl.when(m > 0)
        def _():
            acc_ref[...] += p

        @pl.when(m == n_pairs - 1)
        def _():
            _ln_bwd_rows(acc_ref[...] + ALPHA * res_ref[...], xh_ref, rs_ref, gam_ref, dr_ref, dgam_ref, dbeta_ref,
                         dsum_ref, i == 0)

    def src(gate):
        def blk(m):
            return jnp.minimum(m, half_pairs - 1) if gate else jnp.maximum(m - half_pairs, 0)
        tile = pl.BlockSpec((pair, tm, fb), lambda i, m: (blk(m), i, 0))
        nxt = pl.BlockSpec((pair, WRAP_ROWS, fb),
                           lambda i, m: (blk(m), jnp.minimum((i + 1) * per, n_steps * per - 1), 0))
        return [tile, nxt]

    row = pl.BlockSpec((tm, d), lambda i, m: (i, 0))
    ln_ins, ln_outs = _ln_bwd_specs(tm, d, l, lambda i, m: (i, 0))
    tmp = pltpu.VMEM((tm, fb), F32)
    halo = pltpu.VMEM((WRAP_ROWS, fb), F32)
    return _call(body, name, (n_steps, n_pairs),
                 src(True) + src(False) +
                 [pl.BlockSpec((None, pair, FFN_K, fb), lambda i, m: (l, m, 0, 0)),
                  pl.BlockSpec((pair, fb, d), lambda i, m: (m, 0, 0)), row] + ln_ins,
                 [pl.BlockSpec((pair, tm, fb), lambda i, m: (m, i, 0)),
                  pl.BlockSpec((nb, 1, fb), lambda i, m: (0, 0, 0))] + ln_outs,
                 [_sds((nb, t, fb), BF16), _sds((nb, 1, fb), F32), _sds((t, d), F32)] + [_sds((1, d), F32)] * 3,
                 [halo, tmp, pltpu.VMEM((tm, d), F32)],
                 deps=deps)(dcg, dcg, dcv, dcv, wdw, w_up, res, xhat, rstd, g3)


def _gelu_parts(h):
    cdf = 0.5 * (1.0 + lax.erf(h * INV_SQRT2))
    return h * cdf, cdf


def _seg_axis(a, axis, fn):
    return jnp.moveaxis(fn(jnp.moveaxis(a, axis, 0), TM_EW), 0, axis)


def _sgu_operands(w_s, b_s):
    nl = w_s.shape[0]
    n_sub = TM_EW // CHUNK
    tril = jnp.tril(jnp.ones((CHUNK, CHUNK), dtype=bool))
    w_causal = jnp.where(tril, w_s, 0.0)
    w_tile = (jnp.eye(n_sub, dtype=F32)[None, None, :, None, :, None] * w_causal[:, :, None, :, None, :]).reshape(
        nl, GROUPS, TM_EW, TM_EW)
    w_tile = _seg_axis(_seg_axis(w_tile, 2, _to_segments), 3, _to_segments).astype(BF16)
    bs_tile = jnp.broadcast_to(b_s[:, :, None, :, None], (nl, GROUPS, n_sub, CHUNK, CHUNK)).reshape(
        nl, GROUPS, TM_EW, CHUNK)
    return w_tile, _seg_axis(bs_tile, 2, _to_segments)


def _sgu_param_grads(dwt, dbt):
    n_sub = TM_EW // CHUNK
    tril = jnp.tril(jnp.ones((CHUNK, CHUNK), dtype=bool))
    dwt = _seg_axis(_seg_axis(dwt, 1, _from_segments), 2, _from_segments).reshape(GROUPS, n_sub, CHUNK, n_sub, CHUNK)
    dw = sum(dwt[:, a, :, a, :] for a in range(n_sub))
    db = _seg_axis(dbt, 1, _from_segments).reshape(GROUPS, n_sub, CHUNK).sum(axis=1)
    return jnp.where(tril, dw, 0.0), db


def _sgu(name, h, g3, b3, wt, bst, l):
    t, c2 = h.shape
    c = c2 // 2
    tm = TM_EW

    def body(h_ref, g_ref, b_ref, wt_ref, bs_ref, o_ref):
        z, _ = _gelu_parts(h_ref[...])
        u = z[:, :c]
        xhat, _ = _ln_stats(z[:, c:])
        vnb = (xhat * g_ref[...] + b_ref[...]).astype(BF16)
        for gi in range(GROUPS):
            cs = slice(gi * CHUNK, (gi + 1) * CHUNK)
            sp = jnp.dot(wt_ref[gi], vnb[:, cs], preferred_element_type=F32) + bs_ref[gi]
            o_ref[:, cs] = (u[:, cs] * sp).astype(BF16)

    return _call(body, name, (t // tm,),
                 [_row_spec(tm, c2), _param_spec(l, c), _param_spec(l, c),
                  pl.BlockSpec((None, GROUPS, tm, tm), lambda i: (l, 0, 0, 0)),
                  pl.BlockSpec((None, GROUPS, tm, CHUNK), lambda i: (l, 0, 0, 0))],
                 _row_spec(tm, c), _sds((t, c), BF16))(h, g3, b3, wt, bst)


def _sgu_bwd(name, dq, h, g3, b3, wt, bst, l, deps=()):
    t, c2 = h.shape
    c = c2 // 2
    tm = TM_EW
    n_tiles = t // tm

    def body(dq_ref, h_ref, g_ref, b_ref, wt_ref, bs_ref,
             dh_ref, dbin_ref, dw_ref, dbs_ref, dg_ref, db_ref, du_ref, dvn_ref, bsum_ref):
        i = pl.program_id(0)
        first = i == 0
        hv = h_ref[...]
        z, cdf = _gelu_parts(hv)
        u = z[:, :c]
        xhat, rstd = _ln_stats(z[:, c:])
        g = g_ref[...]
        vnb = (xhat * g + b_ref[...]).astype(BF16)

        @pl.when(first)
        def _():
            dw_ref[...] = jnp.zeros_like(dw_ref)
            bsum_ref[...] = jnp.zeros_like(bsum_ref)

        for gi in range(GROUPS):
            cs = slice(gi * CHUNK, (gi + 1) * CHUNK)
            vb = vnb[:, cs]
            w = wt_ref[gi]
            sp = jnp.dot(w, vb, preferred_element_type=F32) + bs_ref[gi]
            dqb = dq_ref[:, cs]
            du_ref[:, cs] = dqb * sp
            dsp = dqb * u[:, cs]
            bsum_ref[gi] += dsp
            dspb = dsp.astype(BF16)
            dw_ref[gi] += lax.dot_general(dspb, vb, (_DIMS["nt"], ((), ())), preferred_element_type=F32)
            dvn_ref[:, cs] = lax.dot_general(w, dspb, (_DIMS["tn"], ((), ())), preferred_element_type=F32)

        dvn = dvn_ref[...]
        dv = _ln_backward(dvn * g, xhat, rstd)
        pdf = jnp.exp(-0.5 * hv * hv) * INV_SQRT2PI
        dgelu = cdf + hv * pdf
        dhu = du_ref[...] * dgelu[:, :c]
        dhv = dv * dgelu[:, c:]
        dh_ref[:, :c] = dhu.astype(BF16)
        dh_ref[:, c:] = dhv.astype(BF16)
        _acc_rows(dbin_ref.at[:, :c], _colsum(dhu), first)
        _acc_rows(dbin_ref.at[:, c:], _colsum(dhv), first)
        _acc_rows(dg_ref, _colsum(dvn * xhat), first)
        _acc_rows(db_ref, _colsum(dvn), first)

        @pl.when(i == n_tiles - 1)
        def _():
            dbs_ref[...] = jnp.sum(bsum_ref[...], axis=-1)

    vec = pl.BlockSpec((1, c), lambda i: (0, 0))
    return _call(body, name, (n_tiles,),
                 [_row_spec(tm, c), _row_spec(tm, c2), _param_spec(l, c), _param_spec(l, c),
                  pl.BlockSpec((None, GROUPS, tm, tm), lambda i: (l, 0, 0, 0)),
                  pl.BlockSpec((None, GROUPS, tm, CHUNK), lambda i: (l, 0, 0, 0))],
                 [_row_spec(tm, c2), pl.BlockSpec((1, c2), lambda i: (0, 0)),
                  pl.BlockSpec((GROUPS, tm, tm), lambda i: (0, 0, 0)),
                  pl.BlockSpec((GROUPS, tm), lambda i: (0, 0)), vec, vec],
                 [_sds((t, c2), BF16), _sds((1, c2), F32), _sds((GROUPS, tm, tm), F32),
                  _sds((GROUPS, tm), F32), _sds((1, c), F32), _sds((1, c), F32)],
                 [pltpu.VMEM((tm, c), F32), pltpu.VMEM((tm, c), F32), pltpu.VMEM((GROUPS, tm, CHUNK), F32)],
                 deps=deps)(dq, h, g3, b3, wt, bst)


def _loss(name, y, target):
    t, d = y.shape
    tm = min(TM_ROW, t)
    n_tiles = t // tm

    def body(y_ref, t_ref, l_ref, dy_ref, acc_ref):
        i = pl.program_id(0)
        diff = y_ref[...] - t_ref[...]
        dy_ref[...] = diff * (1.0 / d)
        _acc_rows(acc_ref, _colsum(diff * diff), i == 0)

        @pl.when(i == n_tiles - 1)
        def _():
            l_ref[...] = jnp.broadcast_to(jnp.sum(acc_ref[...], axis=-1, keepdims=True) * (0.5 / d), (1, LANES))

    return _call(body, name, (n_tiles,), [_row_spec(tm, d), _row_spec(tm, d)],
                 [pl.BlockSpec((1, LANES), lambda i: (0, 0)), _row_spec(tm, d)],
                 [_sds((1, LANES), F32), _sds((t, d), F32)], [pltpu.VMEM((1, d), F32)])(y, target)


def _adamw(g, w, m, v):
    m2 = ADAM_B1 * m + (1.0 - ADAM_B1) * g
    v2 = ADAM_B2 * v + (1.0 - ADAM_B2) * (g * g)
    m_hat = m2 / (1.0 - ADAM_B1 ** ADAM_STEP)
    v_hat = v2 / (1.0 - ADAM_B2 ** ADAM_STEP)
    delta = -ADAM_LR * (m_hat / (jnp.sqrt(v_hat) + ADAM_EPS) + ADAM_WD * w)
    return delta, m2, v2


ROW_TILE_CAP = 512


def _row_tile(rows, cap=ROW_TILE_CAP):
    if rows <= cap:
        return rows
    for tr in range(cap, 15, -16):
        if rows % tr == 0:
            return tr
    return rows


def _sum8_adamw(name, dev, lands, parts, w, m, v):
    nl = len(lands)
    _, r, c = lands[0].shape
    tr = _row_tile(r, cap=256)

    def body(dev_ref, *refs):
        land, own = refs[:nl], refs[nl:2 * nl]
        w_ref, m_ref, v_ref, g_ref, d_ref, m2_ref, v2_ref = refs[2 * nl:]
        layer, me = pl.program_id(0), dev_ref[0]
        for l in range(nl):
            @pl.when(layer == l)
            def _(l=l):
                g = None
                for s in range(N_DEV):
                    part = jnp.where(me == s, own[l][...], land[l][s]).astype(F32)
                    g = part if g is None else g + part
                delta, m2, v2 = _adamw(g, w_ref[...], m_ref[...], v_ref[...])
                g_ref[...] = g
                d_ref[...] = delta
                m2_ref[...] = m2
                v2_ref[...] = v2

    def rows_of(l, a, i):
        return jnp.where(a == l, i, 0)

    spec = pl.BlockSpec((None, tr, c), lambda a, i, dev_ref: (a, i, 0))
    in_specs = [pl.BlockSpec((N_DEV, tr, c), lambda a, i, dev_ref, l=l: (0, rows_of(l, a, i), 0)) for l in range(nl)]
    in_specs += [pl.BlockSpec((None, tr, c), lambda a, i, dev_ref, l=l: (dev_ref[0], rows_of(l, a, i), 0))
                 for l in range(nl)]
    grid_spec = pltpu.PrefetchScalarGridSpec(
        num_scalar_prefetch=1, grid=(nl, r // tr), in_specs=in_specs + [spec] * 3, out_specs=[spec] * 4)
    return pl.pallas_call(
        body, name=name, grid_spec=grid_spec, out_shape=[_sds(w.shape, F32)] * 4,
        compiler_params=pltpu.CompilerParams(vmem_limit_bytes=VMEM_LIMIT))(dev, *lands, *parts, w, m, v)


def _sum8(name, parts):
    _, r, c = parts.shape
    tr = _row_tile(r)

    def body(p_ref, o_ref):
        acc = p_ref[0]
        for s in range(1, N_DEV):
            acc = acc + p_ref[s]
        o_ref[...] = acc

    return _call(body, name, (r // tr,), [pl.BlockSpec((N_DEV, tr, c), lambda i: (0, i, 0))],
                 pl.BlockSpec((tr, c), lambda i: (i, 0)), _sds((r, c), F32))(parts)


def _adamw_small(name, gs, ws, ms, vs):
    n = len(gs)

    def body(*refs):
        g, w, m, v = (refs[k * n:(k + 1) * n] for k in range(4))
        d_out, m_out, v_out = (refs[(4 + k) * n:(5 + k) * n] for k in range(3))
        for k in range(n):
            d_out[k][...], m_out[k][...], v_out[k][...] = _adamw(g[k][...], w[k][...], m[k][...], v[k][...])

    vmem = pl.BlockSpec(memory_space=pltpu.VMEM)
    outs = _call(body, name, (), [vmem] * (4 * n), [vmem] * (3 * n), [_sds(w.shape, F32) for w in ws] * 3)(
        *gs, *ws, *ms, *vs)
    return outs[:n], outs[n:2 * n], outs[2 * n:]


def _pack(arrs, row_multiple=SUBLANES):
    pieces, rows = [], 0
    for a in arrs:
        piece = a.reshape(-1, LANES)
        piece = jnp.pad(piece, ((0, (-piece.shape[0]) % SUBLANES), (0, 0)))
        pieces.append(piece)
        rows += piece.shape[0]
    if rows % row_multiple:
        pieces.append(jnp.zeros(((-rows) % row_multiple, LANES), pieces[0].dtype))
    return jnp.concatenate(pieces, axis=0)


def _unpack(buf, shapes, lead=0):
    out, pos = [], 0
    for shp in shapes:
        rows = math.prod(shp) // LANES
        piece = lax.slice_in_dim(buf, pos, pos + rows, axis=lead)
        out.append(piece.reshape(buf.shape[:lead] + tuple(shp)))
        pos += rows + (-rows) % SUBLANES
    return out


REPLICATED = ["conv_b_in", "conv_b_dw", "conv_ln_g", "conv_ln_b", "conv_b_out", "gmlp_w_s", "gmlp_b_s",
              "ffn_b_up", "ffn_b_dw", "ffn_b_down", "norm1_g", "norm1_b", "norm2_g", "norm2_b"]
SMALL_SHARDED = ["conv_w_dw", "gmlp_b_in", "gmlp_ln_g", "gmlp_ln_b", "gmlp_b_out", "ffn_w_dw"]
BIG = ["conv_w_in", "conv_w_out", "gmlp_w_in", "gmlp_w_out", "ffn_w_up", "ffn_w_down"]
WEIGHTS = ["conv_w_in", "conv_b_in", "conv_w_dw", "conv_b_dw", "conv_ln_g", "conv_ln_b", "conv_w_out", "conv_b_out",
           "gmlp_w_in", "gmlp_b_in", "gmlp_ln_g", "gmlp_ln_b", "gmlp_w_s", "gmlp_b_s", "gmlp_w_out", "gmlp_b_out",
           "ffn_w_up", "ffn_b_up", "ffn_w_dw", "ffn_b_dw", "ffn_w_down", "ffn_b_down",
           "norm1_g", "norm1_b", "norm2_g", "norm2_b"]


def _from_shards(g, lead_shape):
    nd = len(lead_shape)
    perm = tuple(range(1, nd + 1)) + (0, nd + 1)
    return g.transpose(perm).reshape(tuple(lead_shape) + (-1,))


def _to_shards(full, width):
    lead = full.shape[:-1]
    nd = len(lead)
    parts = full.reshape(lead + (N_DEV, width))
    return parts.transpose((nd,) + tuple(range(nd)) + (nd + 1,))


def _step(p):
    x_in, target_in = p["x"], p["loss_target"]
    bsz, seq, d = x_in.shape
    t = bsz * seq
    assert seq % TM_EW == 0 and TM_EW % CHUNK == 0 and TM_EW // SUBLANES >= CONV_K - 1
    x0 = _to_segments(x_in.reshape(t, d), TM_EW)
    target = _to_segments(target_in.reshape(t, d), TM_EW)
    n_conv, n_gmlp = p["conv_w_in"].shape[0], p["gmlp_w_in"].shape[0]
    fb = p["ffn_w_up"].shape[-1]
    nblk = N_DEV
    half = nblk // 2
    cw = p["conv_w_in"].shape[-1]
    tm = min(TM_MM, t)
    nt = t // tm
    tk = min(TK_DW, t)
    ntk = t // tk
    dev = 4 * lax.axis_index("x") + 2 * lax.axis_index("y") + lax.axis_index("c")

    small_shapes = [p[n].shape for n in SMALL_SHARDED]
    small_src = _pack([p[n] for n in SMALL_SHARDED])[None]
    small_all = _all_gather("gather_small_weights", [small_src])[0][0]
    sm = _unpack(small_all, small_shapes, lead=1)
    w_src = []
    for i in range(DEPTH):
        mix = "conv" if i % 2 == 0 else "gmlp"
        w_src += [p[mix + "_w_in"][i // 2].astype(BF16), p[mix + "_w_out"][i // 2].astype(BF16),
                  p["ffn_w_up"][i].T.astype(BF16), p["ffn_w_down"][i].astype(BF16)]
    send_sems, recv_sems, w_land, _ = _gather_start(
        "weights_gather_start", _place_own("weights_place_own", w_src, deps=[small_all]))
    W_IN, W_OUT, W_UP, W_DOWN = range(4)

    def wait_weight(i, k, after):
        return _gather_wait(f"l{i}_weights_wait{k}", w_land[4 * i + k], send_sems, recv_sems, 4 * i + k, after)
    conv_w_dw = _from_shards(sm[0], sm[0].shape[1:-1])
    gmlp_b_in = _from_shards(sm[1], sm[1].shape[1:-1])
    gmlp_ln_g = _from_shards(sm[2], sm[2].shape[1:-1])
    gmlp_ln_b = _from_shards(sm[3], sm[3].shape[1:-1])
    gmlp_b_out = _from_shards(sm[4], sm[4].shape[1:-1])
    ffn_w_dw = sm[5].transpose(1, 0, 2, 3)

    def rows3(a):
        return a.reshape(a.shape[0], 1, a.shape[-1])

    conv_b_in4 = p["conv_b_in"].reshape(n_conv, N_DEV, 1, cw)
    gmlp_b_in4 = gmlp_b_in.reshape(n_gmlp, N_DEV, 1, cw)
    ffn_b_up4 = p["ffn_b_up"].reshape(DEPTH, nblk, 1, fb)
    ffn_b_dw4 = p["ffn_b_dw"].reshape(DEPTH, nblk, 1, fb)
    conv_b_dw3, conv_ln_g3, conv_ln_b3 = rows3(p["conv_b_dw"]), rows3(p["conv_ln_g"]), rows3(p["conv_ln_b"])
    conv_b_out3, gmlp_b_out3, ffn_b_down3 = rows3(p["conv_b_out"]), rows3(gmlp_b_out), rows3(p["ffn_b_down"])
    gmlp_ln_g3, gmlp_ln_b3 = rows3(gmlp_ln_g), rows3(gmlp_ln_b)
    n1g3, n1b3, n2g3, n2b3 = rows3(p["norm1_g"]), rows3(p["norm1_b"]), rows3(p["norm2_g"]), rows3(p["norm2_b"])
    w_tile, bs_tile = _sgu_operands(p["gmlp_w_s"], p["gmlp_b_s"])

    def mm_in(name, xa, wg, l, bias4, glu=False):
        tmi = min(TM_LN, t)
        c_half = half * cw

        def body(a_ref, b_ref, bias_ref, h_ref, *u_ref):
            xb = a_ref[...].astype(BF16)
            for n in range(N_DEV):
                h_ref[:, n * cw:(n + 1) * cw] = jnp.dot(xb, b_ref[n], preferred_element_type=F32) + bias_ref[n]
            if glu:
                u_ref[0][...] = h_ref[:, :c_half] * _sigmoid(h_ref[:, c_half:])

        outs = _call(body, name, (t // tmi,),
                     [pl.BlockSpec((tmi, d), lambda i: (i, 0)), pl.BlockSpec((N_DEV, d, cw), lambda i: (0, 0, 0)),
                      pl.BlockSpec((None, N_DEV, 1, cw), lambda i: (l, 0, 0, 0))],
                     [pl.BlockSpec((tmi, N_DEV * cw), lambda i: (i, 0))]
                     + ([pl.BlockSpec((tmi, c_half), lambda i: (i, 0))] if glu else []),
                     [_sds((t, N_DEV * cw), F32)] + ([_sds((t, c_half), F32)] if glu else []))(xa, wg, bias4)
        return outs if glu else outs[0]

    def mm_out_dx(name, dy, w, deps=()):
        return _matmul(name, dy, w, "nt", grid=(nt,),
                       a_spec=pl.BlockSpec((tm, d), lambda i: (i, 0)),
                       b_spec=pl.BlockSpec((d, d), lambda i: (0, 0)),
                       o_spec=pl.BlockSpec((tm, d), lambda i: (i, 0)), o_shape=(t, d), o_dtype=F32, deps=deps)

    def mm_out_dw(name, sa, dy):
        return _matmul(name, sa, dy, "tn", grid=(nt,), k_axis=0, nk=nt, acc_shape=(d, d),
                       a_spec=pl.BlockSpec((tm, d), lambda k: (k, 0)),
                       b_spec=pl.BlockSpec((tm, d), lambda k: (k, 0)),
                       o_spec=pl.BlockSpec((d, d), lambda k: (0, 0)), o_shape=(d, d), o_dtype=BF16)

    def mm_in_dx(name, dh, wg, res, norm=None):
        tmx = min(TM_LN, t)

        def body(a_ref, b_ref, res_ref, *refs):
            y = ALPHA * res_ref[...]
            for n in range(N_DEV):
                y = y + lax.dot_general(a_ref[:, n * cw:(n + 1) * cw], b_ref[n], (_DIMS["nt"], ((), ())),
                                        preferred_element_type=F32)
            if norm is None:
                refs[0][...] = y
            else:
                _ln_bwd_rows(y, *refs, pl.program_id(0) == 0)

        row = pl.BlockSpec((tmx, d), lambda i: (i, 0))
        ins = [pl.BlockSpec((tmx, N_DEV * cw), lambda i: (i, 0)), pl.BlockSpec((N_DEV, d, cw), lambda i: (0, 0, 0)), row]
        if norm is None:
            return _call(body, name, (t // tmx,), ins, row, _sds((t, d), F32))(dh, wg, res)
        ln_ins, ln_outs = _ln_bwd_specs(tmx, d, norm[3], lambda i: (i, 0))
        return _call(body, name, (t // tmx,), ins + ln_ins, ln_outs,
                     [_sds((t, d), F32)] + [_sds((1, d), F32)] * 3)(dh, wg, res, *norm[:3])

    def mm_in_dw(name, xa, dh):
        def body(a_ref, b_ref, o_ref, acc_ref):
            k = pl.program_id(1)
            p = lax.dot_general(a_ref[...].astype(BF16), b_ref[...], (_DIMS["tn"], ((), ())),
                                preferred_element_type=F32)
            _acc_rows(acc_ref, p, k == 0)

            @pl.when(k == nt - 1)
            def _():
                for n in range(half):
                    o_ref[n] = acc_ref[:, n * cw:(n + 1) * cw].astype(BF16)

        return _call(body, name, (2, nt),
                     [pl.BlockSpec((tm, d), lambda c, k: (k, 0)), pl.BlockSpec((tm, half * cw), lambda c, k: (k, c))],
                     pl.BlockSpec((half, d, cw), lambda c, k: (c, 0, 0)), _sds((N_DEV, d, cw), BF16),
                     [pltpu.VMEM((d, half * cw), F32)])(xa, dh)

    def mm_down_dw(name, a, dy, deps=()):
        return _matmul(name, a, dy, "tn", grid=(half, ntk), k_axis=1, nk=ntk, acc_shape=(fb, d),
                       a_spec=pl.BlockSpec((None, tk, fb), lambda n, k: (n, k, 0)),
                       b_spec=pl.BlockSpec((tk, d), lambda n, k: (k, 0)),
                       o_spec=pl.BlockSpec((None, fb, d), lambda n, k: (n, 0, 0)),
                       o_shape=(half, fb, d), o_dtype=BF16, deps=deps)

    def mm_up_dw(name, xa, dh):
        return _matmul(name, dh, xa, "tn", grid=(nblk, ntk), k_axis=1, nk=ntk, acc_shape=(fb, d),
                       a_spec=pl.BlockSpec((None, tk, fb), lambda n, k: (n, k, 0)),
                       b_spec=pl.BlockSpec((tk, d), lambda n, k: (k, 0)),
                       o_spec=pl.BlockSpec((None, fb, d), lambda n, k: (n, 0, 0)),
                       o_shape=(nblk, fb, d), o_dtype=BF16)

    saved = []
    xcur = x0
    for i in range(DEPTH):
        j = i // 2
        s = {"x": xcur}
        s["w_in"] = wait_weight(i, W_IN, xcur)
        if i % 2 == 0:
            s["h"], s["u"] = mm_in(f"l{i}_conv_in_glu", xcur, s["w_in"], j, conv_b_in4, glu=True)
            s["c"] = _dwconv31(f"l{i}_dwconv", s["u"], conv_w_dw, conv_b_dw3, j, seq)
            s["s"] = _ln_silu(f"l{i}_ln_silu", s["c"], conv_ln_g3, conv_ln_b3, j)
            b_out3 = conv_b_out3
        else:
            s["h"] = mm_in(f"l{i}_gmlp_in", xcur, s["w_in"], j, gmlp_b_in4)
            s["s"] = _sgu(f"l{i}_sgu", s["h"], gmlp_ln_g3, gmlp_ln_b3, w_tile, bs_tile, j)
            b_out3 = gmlp_b_out3
        s["w_out"] = wait_weight(i, W_OUT, s["s"]).reshape(d, d)
        s["x1"], s["xhat1"], s["rstd1"] = _matmul_ln(
            f"l{i}_mixer_out_norm1", s["s"], s["w_out"], xcur, b_out3, n1g3, n1b3, j, i)
        s["w_up"] = wait_weight(i, W_UP, s["x1"])
        s["hg"], s["hv"], s["a"] = _ffn_up_act(f"l{i}_ffn_up_act", s["x1"], s["w_up"], ffn_b_up4, ffn_w_dw, ffn_b_dw4,
                                               i, seq)
        s["w_down"] = wait_weight(i, W_DOWN, s["a"]).reshape(half, fb, d)
        xcur, s["xhat2"], s["rstd2"] = _matmul_ln(
            f"l{i}_ffn_down_norm2", s["a"], s["w_down"], s["x1"], ffn_b_down3, n2g3, n2b3, i, i)
        saved.append(s)

    loss_row, dx = _loss("loss", xcur, target)

    started = {n: [None] * p[n].shape[0] for n in BIG}
    tokens = []

    def send_grads(name, items):
        done, token = _scatter_start(name, [g for _, _, g in items])
        for (n, l, _), st in zip(items, done):
            started[n][l] = st
        tokens.append(token)

    def take_tokens():
        out = list(tokens)
        tokens.clear()
        return out

    gl = {n: [None] * p[n].shape[0] for n in REPLICATED + SMALL_SHARDED}
    dr2, gl["norm2_g"][DEPTH - 1], gl["norm2_b"][DEPTH - 1], gl["ffn_b_down"][DEPTH - 1] = _ln_res_bwd(
        f"l{DEPTH - 1}_norm2_bwd", dx, saved[-1]["xhat2"], saved[-1]["rstd2"], n2g3, DEPTH - 1)
    for i in reversed(range(DEPTH)):
        j = i // 2
        s = saved[i]
        mix = "conv" if i % 2 == 0 else "gmlp"
        g_down = mm_down_dw(f"l{i}_ffn_down_dw", s["a"], dr2, deps=take_tokens()).reshape(N_DEV, -1, d)
        send_grads(f"l{i}_ffn_down_grad_scatter_start", [("ffn_w_down", i, g_down)])
        dcg, dcv, dbg, dbv, dwg, dwv = _ffn_act_bwd(f"l{i}_ffn_act_bwd", dr2, s["w_down"], s["hg"], s["hv"],
                                                    ffn_w_dw, ffn_b_dw4, i, seq, deps=take_tokens())
        gl["ffn_b_dw"][i] = jnp.concatenate([dbg, dbv], axis=0).reshape(1, nblk * fb)
        gl["ffn_w_dw"][i] = jnp.concatenate([dwg[:, :FFN_K], dwv[:, :FFN_K]], axis=0)
        dh, dbu, dr1, gl["norm1_g"][i], gl["norm1_b"][i], gl[mix + "_b_out"][j] = _ffn_conv_t_dx(
            f"l{i}_ffn_conv_t_dx", dcg, dcv, ffn_w_dw, s["w_up"], dr2, s["xhat1"], s["rstd1"], n1g3, i, seq,
            deps=take_tokens())
        gl["ffn_b_up"][i] = dbu.reshape(1, nblk * fb)
        send_grads(f"l{i}_ffn_up_grad_scatter_start", [("ffn_w_up", i, mm_up_dw(f"l{i}_ffn_up_dw", s["x1"], dh))])
        ds = mm_out_dx(f"l{i}_{mix}_out_dx", dr1, s["w_out"], deps=take_tokens())
        g_out = mm_out_dw(f"l{i}_{mix}_out_dw", s["s"], dr1).reshape(N_DEV, -1, d)
        if i % 2 == 0:
            dc, gl["conv_ln_g"][j], gl["conv_ln_b"][j], gl["conv_b_dw"][j] = _ln_silu_bwd(
                f"l{i}_ln_silu_bwd", ds, s["c"], conv_ln_g3, conv_ln_b3, j)
            du, dwdw = _dwconv31_bwd(f"l{i}_dwconv_bwd", dc, s["u"], conv_w_dw, j, seq)
            gl["conv_w_dw"][j] = dwdw[:CONV_K]
            dh, gl["conv_b_in"][j] = _glu_bwd(f"l{i}_glu_bwd", du, s["h"])
        else:
            dh, gl["gmlp_b_in"][j], dwt, dbt, gl["gmlp_ln_g"][j], gl["gmlp_ln_b"][j] = _sgu_bwd(
                f"l{i}_sgu_bwd", ds, s["h"], gmlp_ln_g3, gmlp_ln_b3, w_tile, bs_tile, j)
            gl["gmlp_w_s"][j], gl["gmlp_b_s"][j] = _sgu_param_grads(dwt, dbt)
        if i > 0:
            prev = saved[i - 1]
            dr2, gl["norm2_g"][i - 1], gl["norm2_b"][i - 1], gl["ffn_b_down"][i - 1] = mm_in_dx(
                f"l{i}_{mix}_in_dx_norm2_bwd", dh, s["w_in"], dr1, (prev["xhat2"], prev["rstd2"], n2g3, i - 1))
        else:
            dx = mm_in_dx(f"l{i}_{mix}_in_dx", dh, s["w_in"], dr1)
        send_grads(f"l{i}_mixer_grads_scatter_start",
                   [(mix + "_w_out", j, g_out), (mix + "_w_in", j, mm_in_dw(f"l{i}_{mix}_in_dw", s["x"], dh))])
    grad_x = _from_segments(dx, TM_EW).reshape(bsz, seq, d)

    full_small = {n: jnp.stack(gl[n]).reshape(p[n].shape) for n in REPLICATED}
    shard_small = {}
    for n in SMALL_SHARDED:
        if n == "ffn_w_dw":
            shard_small[n] = jnp.stack(gl[n]).transpose(1, 0, 2, 3)
        else:
            width = p[n].shape[-1]
            lead = p[n].shape[:-1]
            shard_small[n] = _to_shards(jnp.stack(gl[n]).reshape(lead + (N_DEV * width,)), width)
    flat_shapes = [(1, LANES)] + [p[n].shape for n in REPLICATED] + [(N_DEV,) + p[n].shape for n in SMALL_SHARDED]
    flat_local = _pack([loss_row] + [full_small[n] for n in REPLICATED] + [shard_small[n] for n in SMALL_SHARDED],
                       row_multiple=ROW_TILE_CAP)

    small_send, small_recv, small_land, small_token = _gather_start(
        "small_grads_gather_start", _place_own("small_grads_place_own", [flat_local]))

    grads, delta, new_m, new_v = {}, {}, {}, {}
    dev1 = jnp.reshape(dev, (1,)).astype(jnp.int32)
    order = ["ffn_w_down", "ffn_w_up", "gmlp_w_out", "gmlp_w_in", "conv_w_out", "conv_w_in"]
    after = small_token
    for n in order:
        parts_done, lands_done = _scatter_wait(f"grads_{n}_scatter_wait", started[n], after)
        state = [p[n], p["m_" + n], p["v_" + n]]
        if n == "ffn_w_up":
            state = [a.transpose(0, 2, 1) for a in state]
        outs = _sum8_adamw(f"adamw_{n}", dev1, lands_done, parts_done, *state)
        after = outs[-1]
        if n == "ffn_w_up":
            outs = [a.transpose(0, 2, 1) for a in outs]
        grads[n], delta[n], new_m[n], new_v[n] = outs

    small_parts = _gather_wait("small_grads_gather_wait", small_land[0], small_send, small_recv, 0, after)
    summed = _unpack(_sum8("sum_small_grads", small_parts), flat_shapes)
    loss = summed[0][0, 0]
    grads.update(zip(REPLICATED, summed[1:1 + len(REPLICATED)]))
    for n, g in zip(SMALL_SHARDED, summed[1 + len(REPLICATED):]):
        grads[n] = lax.dynamic_index_in_dim(g, dev, axis=0, keepdims=False)
    small = REPLICATED + SMALL_SHARDED
    d_s, m_s, v_s = _adamw_small("adamw_small", [grads[n] for n in small], [p[n] for n in small],
                                 [p["m_" + n] for n in small], [p["v_" + n] for n in small])
    for n, dd, mm, vv in zip(small, d_s, m_s, v_s):
        delta[n], new_m[n], new_v[n] = dd, mm, vv

    return (loss, grad_x, *[grads[n] for n in WEIGHTS], *[delta[n] for n in WEIGHTS],
            *[new_m[n] for n in WEIGHTS], *[new_v[n] for n in WEIGHTS])


def kernel(x, conv_w_in, conv_b_in, conv_w_dw, conv_b_dw, conv_ln_g, conv_ln_b, conv_w_out, conv_b_out, gmlp_w_in, gmlp_b_in, gmlp_ln_g, gmlp_ln_b, gmlp_w_s, gmlp_b_s, gmlp_w_out, gmlp_b_out, ffn_w_up, ffn_b_up, ffn_w_dw, ffn_b_dw, ffn_w_down, ffn_b_down, norm1_g, norm1_b, norm2_g, norm2_b, loss_target, m_conv_w_in, m_conv_b_in, m_conv_w_dw, m_conv_b_dw, m_conv_ln_g, m_conv_ln_b, m_conv_w_out, m_conv_b_out, m_gmlp_w_in, m_gmlp_b_in, m_gmlp_ln_g, m_gmlp_ln_b, m_gmlp_w_s, m_gmlp_b_s, m_gmlp_w_out, m_gmlp_b_out, m_ffn_w_up, m_ffn_b_up, m_ffn_w_dw, m_ffn_b_dw, m_ffn_w_down, m_ffn_b_down, m_norm1_g, m_norm1_b, m_norm2_g, m_norm2_b, v_conv_w_in, v_conv_b_in, v_conv_w_dw, v_conv_b_dw, v_conv_ln_g, v_conv_ln_b, v_conv_w_out, v_conv_b_out, v_gmlp_w_in, v_gmlp_b_in, v_gmlp_ln_g, v_gmlp_ln_b, v_gmlp_w_s, v_gmlp_b_s, v_gmlp_w_out, v_gmlp_b_out, v_ffn_w_up, v_ffn_b_up, v_ffn_w_dw, v_ffn_b_dw, v_ffn_w_down, v_ffn_b_down, v_norm1_g, v_norm1_b, v_norm2_g, v_norm2_b):
    return _step(dict(locals()))
```

```python
import math

import jax
import jax.numpy as jnp
from jax import lax
from jax.experimental import pallas as pl
from jax.experimental.pallas import tpu as pltpu

F32 = jnp.float32
BF16 = jnp.bfloat16
MESH = pl.DeviceIdType.MESH

N_DEV = 8
DEPTH = 4
ALPHA = (2.0 * DEPTH) ** 0.25
LN_EPS = 1e-5
CONV_K = 31
FFN_K = 3
CHUNK = 128
GROUPS = 8
ADAM_LR = 0.001
ADAM_B1 = 0.9
ADAM_B2 = 0.999
ADAM_EPS = 1e-08
ADAM_WD = 0.01
ADAM_STEP = 10
INV_SQRT2 = 1.0 / math.sqrt(2.0)
INV_SQRT2PI = 1.0 / math.sqrt(2.0 * math.pi)

LANES = 128
SUBLANES = 8
VMEM_LIMIT = 48 * 1024 * 1024
TM_MM = 1024
TK_DW = 2048
TM_EW = 256
TM_ROW = 512


def _call(body, name, grid, in_specs, out_specs, out_shape, scratch=(), aliases=None, deps=()):
    deps = list(deps)
    in_specs = list(in_specs)
    n_in = len(in_specs)
    if deps:
        inner = body

        def body(*refs):
            return inner(*refs[:n_in], *refs[n_in + len(deps):])

        in_specs = in_specs + [pl.BlockSpec(memory_space=pl.ANY)] * len(deps)
    fn = pl.pallas_call(
        body, name=name, grid=grid, in_specs=in_specs, out_specs=out_specs, out_shape=out_shape,
        scratch_shapes=list(scratch), input_output_aliases=aliases or {},
        compiler_params=pltpu.CompilerParams(vmem_limit_bytes=VMEM_LIMIT))
    return lambda *args: fn(*args, *deps)


def _sds(shape, dtype):
    return jax.ShapeDtypeStruct(tuple(shape), dtype)


def _sigmoid(x):
    return 1.0 / (1.0 + jnp.exp(-x))


def _acc_rows(ref, val, first):
    @pl.when(first)
    def _():
        ref[...] = val

    @pl.when(jnp.logical_not(first))
    def _():
        ref[...] += val


def _colsum(v):
    return jnp.sum(v, axis=0, keepdims=True)


_DIMS = {"nn": ((1,), (0,)), "nt": ((1,), (1,)), "tn": ((0,), (0,))}


def _matmul(name, a, b, mode, *, grid, a_spec, b_spec, o_spec, o_shape, o_dtype, k_axis=None, nk=1,
            acc_shape=None, bias=None, bias_spec=None, res=None, res_spec=None, res_scale=1.0, deps=()):
    dims = (_DIMS[mode], ((), ()))
    has_bias, has_res = bias is not None, res is not None

    def body(*refs):
        a_ref, b_ref = refs[0], refs[1]
        pos = 2
        bias_ref = res_ref = None
        if has_bias:
            bias_ref = refs[pos]
            pos += 1
        if has_res:
            res_ref = refs[pos]
            pos += 1
        o_ref = refs[pos]
        acc_ref = refs[pos + 1] if nk > 1 else None
        p = lax.dot_general(a_ref[...].astype(BF16), b_ref[...].astype(BF16), dims, preferred_element_type=F32)

        def finish(acc):
            if has_bias:
                acc = acc + bias_ref[...]
            if has_res:
                acc = acc + res_scale * res_ref[...]
            o_ref[...] = acc.astype(o_dtype)

        if nk == 1:
            finish(p)
        else:
            k = pl.program_id(k_axis)

            @pl.when(k == 0)
            def _():
                acc_ref[...] = p

            @pl.when(k > 0)
            def _():
                acc_ref[...] += p

            @pl.when(k == nk - 1)
            def _():
                finish(acc_ref[...])

    ins, specs = [a, b], [a_spec, b_spec]
    if has_bias:
        ins.append(bias)
        specs.append(bias_spec)
    if has_res:
        ins.append(res)
        specs.append(res_spec)
    scratch = [pltpu.VMEM(acc_shape, F32)] if nk > 1 else []
    return _call(body, name, grid, specs, o_spec, _sds(o_shape, o_dtype), scratch, deps=deps)(*ins)


TM_LN = 512


def _matmul_ln(name, a, b, x_res, bias3, g3, b3, l_bias, l_norm):
    t, d = x_res.shape
    tm = min(TM_LN, t)
    blocked = a.ndim == 3

    def body(a_ref, b_ref, x_ref, bias_ref, g_ref, be_ref, o_ref, xh_ref, rs_ref):
        if blocked:
            y = None
            for k in range(a.shape[0]):
                p = jnp.dot(a_ref[k], b_ref[k], preferred_element_type=F32)
                y = p if y is None else y + p
        else:
            y = jnp.dot(a_ref[...], b_ref[...], preferred_element_type=F32)
        xhat, rstd = _ln_stats(ALPHA * x_ref[...] + y + bias_ref[...])
        o_ref[...] = xhat * g_ref[...] + be_ref[...]
        xh_ref[...] = xhat
        rs_ref[...] = rstd

    if blocked:
        a_spec = pl.BlockSpec((a.shape[0], tm, a.shape[2]), lambda i: (0, i, 0))
        b_spec = pl.BlockSpec(b.shape, lambda i: (0, 0, 0))
    else:
        a_spec = pl.BlockSpec((tm, a.shape[1]), lambda i: (i, 0))
        b_spec = pl.BlockSpec(b.shape, lambda i: (0, 0))
    row = pl.BlockSpec((tm, d), lambda i: (i, 0))
    stat = pl.BlockSpec((tm, 1), lambda i: (i, 0))

    def vec(l):
        return pl.BlockSpec((None, 1, d), lambda i: (l, 0, 0))

    return _call(body, name, (t // tm,), [a_spec, b_spec, row, vec(l_bias), vec(l_norm), vec(l_norm)],
                 [row, row, stat], [_sds((t, d), F32), _sds((t, d), F32), _sds((t, 1), F32)])(
                     a, b, x_res, bias3, g3, b3)


def _mesh_pos():
    return lax.axis_index("x"), lax.axis_index("y"), lax.axis_index("c")


def _any_specs(n):
    return [pl.BlockSpec(memory_space=pl.ANY)] * n


def _all_gather(name, srcs):
    n = len(srcs)

    def body(*refs):
        src, out = refs[:n], refs[n:2 * n]
        send_sems, recv_sems, local_sems = refs[2 * n:]
        x, y, c = _mesh_pos()
        me, sibling = (x, y, c), (x, y, 1 - c)
        chips = [(1 - x, y), (x, 1 - y), (1 - x, 1 - y)]

        def slot(k, p):
            return out[k].at[:, 4 * p[0] + 2 * p[1] + p[2]]

        def copy(k, idx, block, to, s=None):
            return pltpu.make_async_remote_copy(
                src_ref=slot(k, block) if s is None else s, dst_ref=slot(k, block),
                send_sem=send_sems.at[k * 7 + idx], recv_sem=recv_sems.at[k * 7 + idx],
                device_id=to, device_id_type=MESH)

        local = [pltpu.make_async_copy(src[k], slot(k, me), local_sems.at[k]) for k in range(n)]
        for cp in local:
            cp.start()
        first = []
        for k in range(n):
            first.append(copy(k, 0, me, sibling, src[k]))
            for j, chip in enumerate(chips):
                first.append(copy(k, 1 + j, me, (*chip, c), src[k]))
        for cp in first:
            cp.start()
        passed = []
        for j, chip in enumerate(chips):
            for k in range(n):
                copy(k, 1 + j, (*chip, c), me).wait_recv()
                cp = copy(k, 4 + j, (*chip, c), sibling)
                cp.start()
                passed.append(cp)
        for k in range(n):
            copy(k, 0, sibling, me).wait_recv()
            for j, chip in enumerate(chips):
                copy(k, 4 + j, (*chip, 1 - c), me).wait_recv()
        for cp in first + passed:
            cp.wait_send()
        for cp in local:
            cp.wait()

    out_shape = [_sds((s.shape[0], N_DEV) + s.shape[1:], s.dtype) for s in srcs]
    return _call(body, name, (), [pl.BlockSpec(memory_space=pltpu.VMEM)] * n, _any_specs(n), out_shape,
                 [pltpu.SemaphoreType.DMA((7 * n,)), pltpu.SemaphoreType.DMA((7 * n,)),
                  pltpu.SemaphoreType.DMA((n,))])(*srcs)


HBM_SPEC = pl.BlockSpec(memory_space=pltpu.HBM)
SEM_SPEC = pl.BlockSpec(memory_space=pltpu.SEMAPHORE)
N_PEER = N_DEV - 1


def _split_call(body, name, in_specs, out_specs, out_shape, aliases):
    return pl.pallas_call(
        body, name=name, in_specs=in_specs, out_specs=out_specs, out_shape=out_shape, input_output_aliases=aliases,
        compiler_params=pltpu.CompilerParams(has_side_effects=pltpu.SideEffectType.DATAFLOW_SIDE_EFFECTING))


def _peers(x, y, c):
    return [(1 - x if q & 4 else x, 1 - y if q & 2 else y, 1 - c if q & 1 else c) for q in range(1, N_DEV)]


def _in_hbm(a):
    return pltpu.with_memory_space_constraint(a, pltpu.HBM)


def _place_own(name, srcs, deps=()):
    n = len(srcs)

    def body(*refs):
        src, out, sems = refs[:n], refs[n:2 * n], refs[2 * n]
        x, y, c = _mesh_pos()
        dev = 4 * x + 2 * y + c
        copies = [pltpu.make_async_copy(src[k], out[k].at[dev], sems.at[k]) for k in range(n)]
        for cp in copies:
            cp.start()
        for cp in copies:
            cp.wait()

    return _call(body, name, (), [pl.BlockSpec(memory_space=pltpu.VMEM)] * n, _any_specs(n),
                 [_sds((N_DEV,) + s.shape, s.dtype) for s in srcs], [pltpu.SemaphoreType.DMA((n,))],
                 deps=deps)(*srcs)


def _gather_start(name, lands):
    n = len(lands)

    def body(*refs):
        land, send_sems, recv_sems = refs[:n], refs[n], refs[n + 1]
        x, y, c = _mesh_pos()
        dev = 4 * x + 2 * y + c
        for k in range(n):
            for peer in _peers(x, y, c):
                pltpu.make_async_remote_copy(
                    src_ref=land[k].at[dev], dst_ref=land[k].at[dev], send_sem=send_sems.at[k],
                    recv_sem=recv_sems.at[k], device_id=peer, device_id_type=MESH).start()
        token = refs[-1]
        token[...] = jnp.zeros_like(token)

    outs = _split_call(
        body, name, [HBM_SPEC] * n, [SEM_SPEC, SEM_SPEC] + [HBM_SPEC] * n + [pl.BlockSpec(memory_space=pltpu.VMEM)],
        [pltpu.SemaphoreType.DMA((n,)), pltpu.SemaphoreType.DMA((n,))] + [pltpu.HBM(a.shape, a.dtype) for a in lands]
        + [_sds((SUBLANES, LANES), F32)],
        {k: 2 + k for k in range(n)})(*[_in_hbm(a) for a in lands])
    return outs[0], outs[1], list(outs[2:2 + n]), outs[-1]


def _wait_seven(src_ref, dst_ref, send_sem, recv_sem):
    cp = pltpu.make_async_remote_copy(
        src_ref=src_ref.at[pl.ds(0, N_PEER)], dst_ref=dst_ref.at[pl.ds(0, N_PEER)], send_sem=send_sem,
        recv_sem=recv_sem, device_id=_mesh_pos(), device_id_type=MESH)
    cp.wait_send()
    cp.wait_recv()


def _gather_wait(name, land, send_sems, recv_sems, k, after):
    def body(land_ref, send_ref, recv_ref, after_ref, out_ref):
        _wait_seven(land_ref, land_ref, send_ref.at[k], recv_ref.at[k])

    return _split_call(body, name, [HBM_SPEC, SEM_SPEC, SEM_SPEC, pl.BlockSpec(memory_space=pl.ANY)], HBM_SPEC,
                       pltpu.HBM(land.shape, land.dtype), {0: 0})(land, send_sems, recv_sems, after)


def _scatter_start(name, parts_list):
    n = len(parts_list)

    def body(*refs):
        x, y, c = _mesh_pos()
        dev = 4 * x + 2 * y + c
        for k in range(n):
            parts_ref, land_ref = refs[2 * k], refs[2 * k + 1]
            send_sem, recv_sem = refs[2 * n + 4 * k], refs[2 * n + 4 * k + 1]
            for peer in _peers(x, y, c):
                pltpu.make_async_remote_copy(
                    src_ref=parts_ref.at[4 * peer[0] + 2 * peer[1] + peer[2]], dst_ref=land_ref.at[dev],
                    send_sem=send_sem, recv_sem=recv_sem, device_id=peer, device_id_type=MESH).start()
        token = refs[-1]
        token[...] = jnp.zeros_like(token)

    ins, out_specs, out_shape, aliases = [], [], [], {}
    for k, parts in enumerate(parts_list):
        buf = pltpu.HBM(parts.shape, parts.dtype)
        ins += [_in_hbm(parts), _in_hbm(lax.empty(parts.shape, parts.dtype))]
        out_specs += [SEM_SPEC, SEM_SPEC, HBM_SPEC, HBM_SPEC]
        out_shape += [pltpu.SemaphoreType.DMA(()), pltpu.SemaphoreType.DMA(()), buf, buf]
        aliases.update({2 * k: 4 * k + 2, 2 * k + 1: 4 * k + 3})
    outs = _split_call(body, name, [HBM_SPEC] * (2 * n), out_specs + [pl.BlockSpec(memory_space=pltpu.VMEM)],
                       out_shape + [_sds((SUBLANES, LANES), F32)], aliases)(*ins)
    return [tuple(outs[4 * k:4 * k + 4]) for k in range(n)], outs[-1]


def _scatter_wait(name, started, after):
    n = len(started)

    def body(*refs):
        for k in range(n):
            send_sem, recv_sem, parts_ref, land_ref = refs[4 * k:4 * k + 4]
            _wait_seven(parts_ref, land_ref, send_sem, recv_sem)

    flat = [a for s in started for a in s]
    outs = _split_call(
        body, name, [SEM_SPEC, SEM_SPEC, HBM_SPEC, HBM_SPEC] * n + [pl.BlockSpec(memory_space=pl.ANY)],
        [HBM_SPEC, HBM_SPEC] * n, [pltpu.HBM(a.shape, a.dtype) for s in started for a in s[2:]],
        {4 * k + 2 + t: 2 * k + t for k in range(n) for t in range(2)})(*flat, after)
    return list(outs[0::2]), list(outs[1::2])


def _to_segments(a, tile):
    seg = tile // SUBLANES
    return a.reshape((a.shape[0] // tile, SUBLANES, seg) + a.shape[1:]).swapaxes(1, 2).reshape(a.shape)


def _from_segments(a, tile):
    seg = tile // SUBLANES
    return a.reshape((a.shape[0] // tile, seg, SUBLANES) + a.shape[1:]).swapaxes(1, 2).reshape(a.shape)


def _chunk(ref, q):
    return ref[q * SUBLANES:(q + 1) * SUBLANES, :]


def _fill_wrap_prev(x_ref, halo_ref, wrap_ref, n_wrap, n_halo, seg, keep):
    sub = lax.broadcasted_iota(jnp.int32, (SUBLANES, x_ref.shape[-1]), 0)
    for j in range(n_wrap):
        q = seg - n_wrap + j
        hq = q - (seg - n_halo)
        row = halo_ref[hq * SUBLANES + SUBLANES - 1:(hq + 1) * SUBLANES, :] * keep
        wrap_ref[j * SUBLANES:(j + 1) * SUBLANES, :] = jnp.where(sub == 0, row, pltpu.roll(_chunk(x_ref, q), 1, 0))


def _fill_wrap_next(x_ref, halo_ref, wrap_ref, n_wrap, keep):
    sub = lax.broadcasted_iota(jnp.int32, (SUBLANES, x_ref.shape[-1]), 0)
    for j in range(n_wrap):
        row = halo_ref[j * SUBLANES:j * SUBLANES + 1, :] * keep
        wrap_ref[j * SUBLANES:(j + 1) * SUBLANES, :] = jnp.where(
            sub == SUBLANES - 1, row, pltpu.roll(_chunk(x_ref, j), SUBLANES - 1, 0))


def _past(x_ref, wrap_ref, q, d, n_wrap):
    return _chunk(x_ref, q - d) if q >= d else _chunk(wrap_ref, q - d + n_wrap)


def _future(x_ref, wrap_ref, q, d, seg):
    return _chunk(x_ref, q + d) if q + d < seg else _chunk(wrap_ref, q + d - seg)


def _conv_fwd(x_ref, wrap_ref, w_ref, b_ref, out_ref, seg, k_taps):
    bias = jnp.broadcast_to(b_ref[...], (SUBLANES, x_ref.shape[-1]))
    for q in range(seg):
        acc = bias
        for k in range(k_taps):
            acc = acc + w_ref[k:k + 1, :] * _past(x_ref, wrap_ref, q, k_taps - 1 - k, k_taps - 1)
        out_ref[q * SUBLANES:(q + 1) * SUBLANES, :] = acc


def _conv_bwd_data(d_ref, wrap_ref, w_ref, out_ref, seg, k_taps):
    for q in range(seg):
        acc = None
        for k in range(k_taps):
            term = w_ref[k:k + 1, :] * _future(d_ref, wrap_ref, q, k_taps - 1 - k, seg)
            acc = term if acc is None else acc + term
        out_ref[q * SUBLANES:(q + 1) * SUBLANES, :] = acc


def _conv_bwd_taps(d_ref, x_ref, wrap_ref, dw_ref, seg, k_taps):
    for k in range(k_taps):
        part = None
        for q in range(seg):
            term = _chunk(d_ref, q) * _past(x_ref, wrap_ref, q, k_taps - 1 - k, k_taps - 1)
            part = term if part is None else part + term
        dw_ref[k:k + 1, :] += _colsum(part)


def _tile_halo_specs(tm, width_block, n_halo, n_tiles, block_of):
    rows = n_halo * SUBLANES
    per = tm // rows
    tile = pl.BlockSpec(width_block(tm), lambda n, i: block_of(n, i))
    prev = pl.BlockSpec(width_block(rows), lambda n, i: block_of(n, jnp.maximum(i * per - 1, 0)))
    nxt = pl.BlockSpec(width_block(rows), lambda n, i: block_of(n, jnp.minimum((i + 1) * per, n_tiles * per - 1)))
    return tile, prev, nxt


def _ln_stats(v):
    mu = jnp.mean(v, axis=-1, keepdims=True)
    vc = v - mu
    var = jnp.mean(vc * vc, axis=-1, keepdims=True)
    rstd = lax.rsqrt(var + LN_EPS)
    return vc * rstd, rstd


def _ln_backward(dxhat, xhat, rstd):
    m1 = jnp.mean(dxhat, axis=-1, keepdims=True)
    m2 = jnp.mean(dxhat * xhat, axis=-1, keepdims=True)
    return rstd * (dxhat - m1 - xhat * m2)


def _row_spec(tm, width):
    return pl.BlockSpec((tm, width), lambda i: (i, 0))


def _param_spec(l, width):
    return pl.BlockSpec((None, 1, width), lambda *_: (l, 0, 0))


def _ln_bwd_rows(dout, xh_ref, rs_ref, g_ref, dr_ref, dg_ref, db_ref, dsum_ref, first):
    xhat = xh_ref[...]
    dr = _ln_backward(dout * g_ref[...], xhat, rs_ref[...])
    dr_ref[...] = dr
    _acc_rows(dg_ref, _colsum(dout * xhat), first)
    _acc_rows(db_ref, _colsum(dout), first)
    _acc_rows(dsum_ref, _colsum(dr), first)


def _ln_bwd_specs(tm, d, l, row_of):
    vec = pl.BlockSpec((1, d), lambda *_: (0, 0))
    ins = [pl.BlockSpec((tm, d), row_of), pl.BlockSpec((tm, 1), row_of), _param_spec(l, d)]
    return ins, [pl.BlockSpec((tm, d), row_of), vec, vec, vec]


def _ln_res_bwd(name, dout, xhat, rstd, g3, l, deps=()):
    t, d = dout.shape
    tm = min(TM_ROW, t)

    def body(do_ref, xh_ref, rs_ref, g_ref, dr_ref, dg_ref, db_ref, dc_ref):
        _ln_bwd_rows(do_ref[...], xh_ref, rs_ref, g_ref, dr_ref, dg_ref, db_ref, dc_ref, pl.program_id(0) == 0)

    ins, outs = _ln_bwd_specs(tm, d, l, lambda i: (i, 0))
    return _call(body, name, (t // tm,), [_row_spec(tm, d)] + ins, outs,
                 [_sds((t, d), F32)] + [_sds((1, d), F32)] * 3, deps=deps)(dout, xhat, rstd, g3)


def _glu_bwd(name, du, h):
    t, c2 = h.shape
    c = c2 // 2
    tm = min(TM_ROW, t)

    def body(du_ref, a_ref, g_ref, dh_ref, db_ref):
        first = pl.program_id(0) == 0
        du_v, a = du_ref[...], a_ref[...]
        sg = _sigmoid(g_ref[...])
        da = du_v * sg
        dg = du_v * a * sg * (1.0 - sg)
        dh_ref[:, :c] = da.astype(BF16)
        dh_ref[:, c:] = dg.astype(BF16)
        _acc_rows(db_ref.at[:, :c], _colsum(da), first)
        _acc_rows(db_ref.at[:, c:], _colsum(dg), first)

    return _call(body, name, (t // tm,),
                 [_row_spec(tm, c), pl.BlockSpec((tm, c), lambda i: (i, 0)), pl.BlockSpec((tm, c), lambda i: (i, 1))],
                 [_row_spec(tm, c2), pl.BlockSpec((1, c2), lambda i: (0, 0))],
                 [_sds((t, c2), BF16), _sds((1, c2), F32)])(du, h, h)


CONV_CB = 512
TAPS_PAD = 32


def _dwconv31(name, u, w3, b3, l, seq):
    t, c = u.shape
    tm, cb = TM_EW, CONV_CB
    seg, seq_tiles, n_tiles = tm // SUBLANES, seq // tm, t // tm
    n_wrap = CONV_K - 1
    tile, prev, _ = _tile_halo_specs(tm, lambda rows: (rows, cb), seg, n_tiles, lambda n, r: (r, n))

    def body(u_ref, halo_ref, w_ref, b_ref, o_ref, wrap_ref):
        keep = (pl.program_id(1) % seq_tiles != 0).astype(F32)
        _fill_wrap_prev(u_ref, halo_ref, wrap_ref, n_wrap, seg, seg, keep)
        _conv_fwd(u_ref, wrap_ref, w_ref, b_ref, o_ref, seg, CONV_K)

    return _call(body, name, (c // cb, n_tiles),
                 [tile, prev, pl.BlockSpec((None, CONV_K, cb), lambda n, i: (l, 0, n)),
                  pl.BlockSpec((None, 1, cb), lambda n, i: (l, 0, n))],
                 tile, _sds((t, c), F32), [pltpu.VMEM((n_wrap * SUBLANES, cb), F32)])(u, u, w3, b3)


def _dwconv31_bwd(name, dc, u, w3, l, seq):
    t, c = dc.shape
    tm, cb = TM_EW, CONV_CB
    seg, seq_tiles, n_tiles = tm // SUBLANES, seq // tm, t // tm
    n_wrap = CONV_K - 1
    tile, prev, nxt = _tile_halo_specs(tm, lambda rows: (rows, cb), seg, n_tiles, lambda n, r: (r, n))

    def body(dc_ref, dcn_ref, u_ref, up_ref, w_ref, du_ref, dw_ref, dwrap_ref, uwrap_ref):
        i = pl.program_id(1)
        keep_prev = (i % seq_tiles != 0).astype(F32)
        keep_next = (i % seq_tiles != seq_tiles - 1).astype(F32)
        _fill_wrap_next(dc_ref, dcn_ref, dwrap_ref, n_wrap, keep_next)
        _conv_bwd_data(dc_ref, dwrap_ref, w_ref, du_ref, seg, CONV_K)

        @pl.when(i == 0)
        def _():
            dw_ref[...] = jnp.zeros_like(dw_ref)

        _fill_wrap_prev(u_ref, up_ref, uwrap_ref, n_wrap, seg, seg, keep_prev)
        _conv_bwd_taps(dc_ref, u_ref, uwrap_ref, dw_ref, seg, CONV_K)

    wrap = pltpu.VMEM((n_wrap * SUBLANES, cb), F32)
    return _call(body, name, (c // cb, n_tiles),
                 [tile, nxt, tile, prev, pl.BlockSpec((None, CONV_K, cb), lambda n, i: (l, 0, n))],
                 [tile, pl.BlockSpec((TAPS_PAD, cb), lambda n, i: (0, n))],
                 [_sds((t, c), F32), _sds((TAPS_PAD, c), F32)], [wrap, wrap])(dc, dc, u, u, w3)


def _ln_silu(name, cx, g3, b3, l):
    t, d = cx.shape
    tm = min(TM_ROW, t)

    def body(c_ref, g_ref, b_ref, o_ref):
        xhat, _ = _ln_stats(c_ref[...])
        nv = xhat * g_ref[...] + b_ref[...]
        o_ref[...] = (nv * _sigmoid(nv)).astype(BF16)

    return _call(body, name, (t // tm,), [_row_spec(tm, d), _param_spec(l, d), _param_spec(l, d)],
                 _row_spec(tm, d), _sds((t, d), BF16))(cx, g3, b3)


def _ln_silu_bwd(name, ds, cx, g3, b3, l, deps=()):
    t, d = cx.shape
    tm = min(TM_ROW, t)

    def body(ds_ref, c_ref, g_ref, b_ref, dc_ref, dg_ref, db_ref, dsum_ref):
        first = pl.program_id(0) == 0
        xhat, rstd = _ln_stats(c_ref[...])
        g = g_ref[...]
        nv = xhat * g + b_ref[...]
        sg = _sigmoid(nv)
        dn = ds_ref[...] * (sg * (1.0 + nv * (1.0 - sg)))
        dc = _ln_backward(dn * g, xhat, rstd)
        dc_ref[...] = dc
        _acc_rows(dg_ref, _colsum(dn * xhat), first)
        _acc_rows(db_ref, _colsum(dn), first)
        _acc_rows(dsum_ref, _colsum(dc), first)

    vec = pl.BlockSpec((1, d), lambda i: (0, 0))
    return _call(body, name, (t // tm,),
                 [_row_spec(tm, d), _row_spec(tm, d), _param_spec(l, d), _param_spec(l, d)],
                 [_row_spec(tm, d), vec, vec, vec],
                 [_sds((t, d), F32)] + [_sds((1, d), F32)] * 3, deps=deps)(ds, cx, g3, b3)


FFN_HALO = FFN_K - 1


def _ffn_conv(x_ref, halo_ref, wrap_ref, w_ref, b_ref, keep, seg, out_ref):
    _fill_wrap_prev(x_ref, halo_ref, wrap_ref, FFN_K - 1, FFN_HALO, seg, keep)
    _conv_fwd(x_ref, wrap_ref, w_ref, b_ref, out_ref, seg, FFN_K)


TM_FFN = 512
WRAP_ROWS = FFN_HALO * SUBLANES


def _sub_tiles(x_ref, prev_ref, next_ref, keep_prev, keep_next, n_sub):
    out = []
    for s in range(n_sub):
        tile = x_ref.at[pl.ds(s * TM_EW, TM_EW)]
        prev = prev_ref if s == 0 else x_ref.at[pl.ds(s * TM_EW - WRAP_ROWS, WRAP_ROWS)]
        nxt = next_ref if s == n_sub - 1 else x_ref.at[pl.ds((s + 1) * TM_EW, WRAP_ROWS)]
        out.append((tile, prev, keep_prev if s == 0 else 1.0, nxt, keep_next if s == n_sub - 1 else 1.0))
    return out


def _rows(ref, s, rows):
    return ref.at[pl.ds(s * rows, rows)]


def _ffn_specs(tm, fb, n_tiles):
    return _tile_halo_specs(tm, lambda rows: (None, rows, fb), FFN_HALO, n_tiles, lambda n, r: (n, r, 0))


def _ffn_up_act(name, x, w_up, b_up4, wdw, bdw, l, seq):
    t, d = x.shape
    nb, fb, _ = w_up.shape
    half = nb // 2
    tm = min(TM_FFN, seq)
    n_sub, seg, seq_steps, n_steps = tm // TM_EW, TM_EW // SUBLANES, seq // tm, t // tm
    per = tm // WRAP_ROWS
    nt_dims = (_DIMS["nt"], ((), ()))

    def body(x_ref, xp_ref, ug_ref, uv_ref, bug_ref, buv_ref, wg_ref, wv_ref, bg_ref, bv_ref,
             hg_ref, hv_ref, a_ref, pg_ref, pv_ref, gwrap_ref, vwrap_ref, cg_ref, cv_ref):
        keep = (pl.program_id(1) % seq_steps != 0).astype(F32)
        xb, xpb = x_ref[...].astype(BF16), xp_ref[...].astype(BF16)
        hg_ref[...] = lax.dot_general(xb, ug_ref[...], nt_dims, preferred_element_type=F32) + bug_ref[...]
        pg_ref[...] = lax.dot_general(xpb, ug_ref[...], nt_dims, preferred_element_type=F32) + bug_ref[...]
        for s, (tile, prev, kp, _, _) in enumerate(_sub_tiles(hg_ref, pg_ref, None, keep, None, n_sub)):
            _ffn_conv(tile, prev, gwrap_ref, wg_ref, bg_ref, kp, seg, _rows(cg_ref, s, TM_EW))
        hv_ref[...] = lax.dot_general(xb, uv_ref[...], nt_dims, preferred_element_type=F32) + buv_ref[...]
        pv_ref[...] = lax.dot_general(xpb, uv_ref[...], nt_dims, preferred_element_type=F32) + buv_ref[...]
        for s, (tile, prev, kp, _, _) in enumerate(_sub_tiles(hv_ref, pv_ref, None, keep, None, n_sub)):
            _ffn_conv(tile, prev, vwrap_ref, wv_ref, bv_ref, kp, seg, _rows(cv_ref, s, TM_EW))
        cg = cg_ref[...]
        a_ref[...] = (cg * _sigmoid(cg) * cv_ref[...]).astype(BF16)

    def blk(shift):
        return pl.BlockSpec((None, fb, d), lambda n, i: (n + shift, 0, 0))

    def vec(shift, rows):
        return pl.BlockSpec((None, None, rows, fb), lambda n, i: (l, n + shift, 0, 0))

    out = pl.BlockSpec((None, tm, fb), lambda n, i: (n, i, 0))
    tmp = pltpu.VMEM((tm, fb), F32)
    halo = pltpu.VMEM((WRAP_ROWS, fb), F32)
    return _call(body, name, (half, n_steps),
                 [pl.BlockSpec((tm, d), lambda n, i: (i, 0)),
                  pl.BlockSpec((WRAP_ROWS, d), lambda n, i: (jnp.maximum(i * per - 1, 0), 0)),
                  blk(0), blk(half), vec(0, 1), vec(half, 1), vec(0, FFN_K), vec(half, FFN_K), vec(0, 1), vec(half, 1)],
                 [out, out, out],
                 [_sds((half, t, fb), F32), _sds((half, t, fb), F32), _sds((half, t, fb), BF16)],
                 [halo, halo, halo, halo, tmp, tmp])(x, x, w_up, w_up, b_up4, b_up4, wdw, wdw, bdw, bdw)


def _ffn_act_bwd(name, dy, w_down, hg, hv, wdw, bdw, l, seq, deps=()):
    half, t, fb = hg.shape
    d = dy.shape[-1]
    tm = min(TM_FFN, seq)
    n_sub, seg, seq_steps, n_steps = tm // TM_EW, TM_EW // SUBLANES, seq // tm, t // tm
    tile, prev, _ = _ffn_specs(tm, fb, n_steps)

    def body(dy_ref, wd_ref, g_ref, gp_ref, v_ref, vp_ref, wg_ref, wv_ref, bg_ref, bv_ref,
             dg_ref, dv_ref, dbg_ref, dbv_ref, dwg_ref, dwv_ref, gwrap_ref, vwrap_ref, cg_ref, cv_ref):
        i = pl.program_id(1)
        first = i == 0
        keep = (i % seq_steps != 0).astype(F32)
        da = lax.dot_general(dy_ref[...].astype(BF16), wd_ref[...], (_DIMS["nt"], ((), ())),
                             preferred_element_type=F32)
        g_tiles = _sub_tiles(g_ref, gp_ref, None, keep, None, n_sub)
        v_tiles = _sub_tiles(v_ref, vp_ref, None, keep, None, n_sub)
        for s in range(n_sub):
            _ffn_conv(g_tiles[s][0], g_tiles[s][1], _rows(gwrap_ref, s, WRAP_ROWS), wg_ref, bg_ref, g_tiles[s][2],
                      seg, _rows(cg_ref, s, TM_EW))
            _ffn_conv(v_tiles[s][0], v_tiles[s][1], _rows(vwrap_ref, s, WRAP_ROWS), wv_ref, bv_ref, v_tiles[s][2],
                      seg, _rows(cv_ref, s, TM_EW))
        cg, cv = cg_ref[...], cv_ref[...]
        sg = _sigmoid(cg)
        dcv = da * cg * sg
        dcg = da * cv * sg * (1.0 + cg * (1.0 - sg))
        dg_ref[...] = dcg
        dv_ref[...] = dcv
        _acc_rows(dbg_ref, _colsum(dcg), first)
        _acc_rows(dbv_ref, _colsum(dcv), first)

        @pl.when(first)
        def _():
            dwg_ref[...] = jnp.zeros_like(dwg_ref)
            dwv_ref[...] = jnp.zeros_like(dwv_ref)

        for s in range(n_sub):
            _conv_bwd_taps(_rows(dg_ref, s, TM_EW), g_tiles[s][0], _rows(gwrap_ref, s, WRAP_ROWS), dwg_ref, seg, FFN_K)
            _conv_bwd_taps(_rows(dv_ref, s, TM_EW), v_tiles[s][0], _rows(vwrap_ref, s, WRAP_ROWS), dwv_ref, seg, FFN_K)

    def vec(shift, rows):
        return pl.BlockSpec((None, None, rows, fb), lambda n, i: (l, n + shift, 0, 0))

    def acc(rows):
        return pl.BlockSpec((None, rows, fb), lambda n, i: (n, 0, 0))

    wrap = pltpu.VMEM((n_sub * WRAP_ROWS, fb), F32)
    tmp = pltpu.VMEM((tm, fb), F32)
    return _call(body, name, (half, n_steps),
                 [pl.BlockSpec((tm, d), lambda n, i: (i, 0)), pl.BlockSpec((None, fb, d), lambda n, i: (n, 0, 0)),
                  tile, prev, tile, prev, vec(0, FFN_K), vec(half, FFN_K), vec(0, 1), vec(half, 1)],
                 [tile, tile, acc(1), acc(1), acc(SUBLANES), acc(SUBLANES)],
                 [_sds((half, t, fb), F32), _sds((half, t, fb), F32), _sds((half, 1, fb), F32),
                  _sds((half, 1, fb), F32), _sds((half, SUBLANES, fb), F32), _sds((half, SUBLANES, fb), F32)],
                 [wrap, wrap, tmp, tmp], deps=deps)(dy, w_down, hg, hg, hv, hv, wdw, wdw, bdw, bdw)


def _ffn_conv_t_dx(name, dcg, dcv, wdw, w_up, res, xhat, rstd, g3, l, seq, deps=()):
    half, t, fb = dcg.shape
    nb, d = 2 * half, res.shape[-1]
    tm = min(TM_FFN, seq)
    n_sub, seg, seq_steps, n_steps = tm // TM_EW, TM_EW // SUBLANES, seq // tm, t // tm
    per = tm // WRAP_ROWS

    pair = 2
    n_pairs, half_pairs = nb // pair, half // pair

    def body(g_ref, gn_ref, v_ref, vn_ref, w_ref, up_ref, res_ref, xh_ref, rs_ref, gam_ref,
             dh_ref, db_ref, dr_ref, dgam_ref, dbeta_ref, dsum_ref, wrap_ref, out_ref, acc_ref):
        i, m = pl.program_id(0), pl.program_id(1)
        keep = (i % seq_steps != seq_steps - 1).astype(F32)

        def conv_t(d_ref, dn_ref, b):
            for s, (sub, _, _, nx, kn) in enumerate(_sub_tiles(d_ref.at[b], None, dn_ref.at[b], None, keep, n_sub)):
                _fill_wrap_next(sub, nx, wrap_ref, FFN_K - 1, kn)
                _conv_bwd_data(sub, wrap_ref, w_ref.at[b], _rows(out_ref, s, TM_EW), seg, FFN_K)

        p = None
        for b in range(pair):
            @pl.when(m < half_pairs)
            def _(b=b):
                conv_t(g_ref, gn_ref, b)

            @pl.when(m >= half_pairs)
            def _(b=b):
                conv_t(v_ref, vn_ref, b)

            dh = out_ref[...]
            dhb = dh.astype(BF16)
            dh_ref[b] = dhb
            _acc_rows(db_ref.at[pair * m + b], _colsum(dh), i == 0)
            part = jnp.dot(dhb, up_ref[b], preferred_element_type=F32)
            p = part if p is None else p + part

        @pl.when(m == 0)
        def _():
            acc_ref[...] = p

        @pl.when(m > 0)
        def _():
            acc_ref[...] += p

        @pl.when(m == n_pairs - 1)
        def _():
            _ln_bwd_rows(acc_ref[...] + ALPHA * res_ref[...], xh_ref, rs_ref, gam_ref, dr_ref, dgam_ref, dbeta_ref,
                         dsum_ref, i == 0)

    def src(gate):
        def blk(m):
            return jnp.minimum(m, half_pairs - 1) if gate else jnp.maximum(m - half_pairs, 0)
        tile = pl.BlockSpec((pair, tm, fb), lambda i, m: (blk(m), i, 0))
        nxt = pl.BlockSpec((pair, WRAP_ROWS, fb),
                           lambda i, m: (blk(m), jnp.minimum((i + 1) * per, n_steps * per - 1), 0))
        return [tile, nxt]

    row = pl.BlockSpec((tm, d), lambda i, m: (i, 0))
    ln_ins, ln_outs = _ln_bwd_specs(tm, d, l, lambda i, m: (i, 0))
    tmp = pltpu.VMEM((tm, fb), F32)
    halo = pltpu.VMEM((WRAP_ROWS, fb), F32)
    return _call(body, name, (n_steps, n_pairs),
                 src(True) + src(False) +
                 [pl.BlockSpec((None, pair, FFN_K, fb), lambda i, m: (l, m, 0, 0)),
                  pl.BlockSpec((pair, fb, d), lambda i, m: (m, 0, 0)), row] + ln_ins,
                 [pl.BlockSpec((pair, tm, fb), lambda i, m: (m, i, 0)),
                  pl.BlockSpec((nb, 1, fb), lambda i, m: (0, 0, 0))] + ln_outs,
                 [_sds((nb, t, fb), BF16), _sds((nb, 1, fb), F32), _sds((t, d), F32)] + [_sds((1, d), F32)] * 3,
                 [halo, tmp, pltpu.VMEM((tm, d), F32)],
                 deps=deps)(dcg, dcg, dcv, dcv, wdw, w_up, res, xhat, rstd, g3)


def _gelu_parts(h):
    cdf = 0.5 * (1.0 + lax.erf(h * INV_SQRT2))
    return h * cdf, cdf


def _seg_axis(a, axis, fn):
    return jnp.moveaxis(fn(jnp.moveaxis(a, axis, 0), TM_EW), 0, axis)


def _sgu_operands(w_s, b_s):
    nl = w_s.shape[0]
    n_sub = TM_EW // CHUNK
    tril = jnp.tril(jnp.ones((CHUNK, CHUNK), dtype=bool))
    w_causal = jnp.where(tril, w_s, 0.0)
    w_tile = (jnp.eye(n_sub, dtype=F32)[None, None, :, None, :, None] * w_causal[:, :, None, :, None, :]).reshape(
        nl, GROUPS, TM_EW, TM_EW)
    w_tile = _seg_axis(_seg_axis(w_tile, 2, _to_segments), 3, _to_segments).astype(BF16)
    bs_tile = jnp.broadcast_to(b_s[:, :, None, :, None], (nl, GROUPS, n_sub, CHUNK, CHUNK)).reshape(
        nl, GROUPS, TM_EW, CHUNK)
    return w_tile, _seg_axis(bs_tile, 2, _to_segments)


def _sgu_param_grads(dwt, dbt):
    n_sub = TM_EW // CHUNK
    tril = jnp.tril(jnp.ones((CHUNK, CHUNK), dtype=bool))
    dwt = _seg_axis(_seg_axis(dwt, 1, _from_segments), 2, _from_segments).reshape(GROUPS, n_sub, CHUNK, n_sub, CHUNK)
    dw = sum(dwt[:, a, :, a, :] for a in range(n_sub))
    db = _seg_axis(dbt, 1, _from_segments).reshape(GROUPS, n_sub, CHUNK).sum(axis=1)
    return jnp.where(tril, dw, 0.0), db


def _sgu(name, h, g3, b3, wt, bst, l):
    t, c2 = h.shape
    c = c2 // 2
    tm = TM_EW

    def body(h_ref, g_ref, b_ref, wt_ref, bs_ref, o_ref):
        z, _ = _gelu_parts(h_ref[...])
        u = z[:, :c]
        xhat, _ = _ln_stats(z[:, c:])
        vnb = (xhat * g_ref[...] + b_ref[...]).astype(BF16)
        for gi in range(GROUPS):
            cs = slice(gi * CHUNK, (gi + 1) * CHUNK)
            sp = jnp.dot(wt_ref[gi], vnb[:, cs], preferred_element_type=F32) + bs_ref[gi]
            o_ref[:, cs] = (u[:, cs] * sp).astype(BF16)

    return _call(body, name, (t // tm,),
                 [_row_spec(tm, c2), _param_spec(l, c), _param_spec(l, c),
                  pl.BlockSpec((None, GROUPS, tm, tm), lambda i: (l, 0, 0, 0)),
                  pl.BlockSpec((None, GROUPS, tm, CHUNK), lambda i: (l, 0, 0, 0))],
                 _row_spec(tm, c), _sds((t, c), BF16))(h, g3, b3, wt, bst)


def _sgu_bwd(name, dq, h, g3, b3, wt, bst, l, deps=()):
    t, c2 = h.shape
    c = c2 // 2
    tm = TM_EW
    n_tiles = t // tm

    def body(dq_ref, h_ref, g_ref, b_ref, wt_ref, bs_ref,
             dh_ref, dbin_ref, dw_ref, dbs_ref, dg_ref, db_ref, du_ref, dvn_ref, bsum_ref):
        i = pl.program_id(0)
        first = i == 0
        hv = h_ref[...]
        z, cdf = _gelu_parts(hv)
        u = z[:, :c]
        xhat, rstd = _ln_stats(z[:, c:])
        g = g_ref[...]
        vnb = (xhat * g + b_ref[...]).astype(BF16)

        @pl.when(first)
        def _():
            dw_ref[...] = jnp.zeros_like(dw_ref)
            bsum_ref[...] = jnp.zeros_like(bsum_ref)

        for gi in range(GROUPS):
            cs = slice(gi * CHUNK, (gi + 1) * CHUNK)
            vb = vnb[:, cs]
            w = wt_ref[gi]
            sp = jnp.dot(w, vb, preferred_element_type=F32) + bs_ref[gi]
            dqb = dq_ref[:, cs]
            du_ref[:, cs] = dqb * sp
            dsp = dqb * u[:, cs]
            bsum_ref[gi] += dsp
            dspb = dsp.astype(BF16)
            dw_ref[gi] += lax.dot_general(dspb, vb, (_DIMS["nt"], ((), ())), preferred_element_type=F32)
            dvn_ref[:, cs] = lax.dot_general(w, dspb, (_DIMS["tn"], ((), ())), preferred_element_type=F32)

        dvn = dvn_ref[...]
        dv = _ln_backward(dvn * g, xhat, rstd)
        pdf = jnp.exp(-0.5 * hv * hv) * INV_SQRT2PI
        dgelu = cdf + hv * pdf
        dhu = du_ref[...] * dgelu[:, :c]
        dhv = dv * dgelu[:, c:]
        dh_ref[:, :c] = dhu.astype(BF16)
        dh_ref[:, c:] = dhv.astype(BF16)
        _acc_rows(dbin_ref.at[:, :c], _colsum(dhu), first)
        _acc_rows(dbin_ref.at[:, c:], _colsum(dhv), first)
        _acc_rows(dg_ref, _colsum(dvn * xhat), first)
        _acc_rows(db_ref, _colsum(dvn), first)

        @pl.when(i == n_tiles - 1)
        def _():
            dbs_ref[...] = jnp.sum(bsum_ref[...], axis=-1)

    vec = pl.BlockSpec((1, c), lambda i: (0, 0))
    return _call(body, name, (n_tiles,),
                 [_row_spec(tm, c), _row_spec(tm, c2), _param_spec(l, c), _param_spec(l, c),
                  pl.BlockSpec((None, GROUPS, tm, tm), lambda i: (l, 0, 0, 0)),
                  pl.BlockSpec((None, GROUPS, tm, CHUNK), lambda i: (l, 0, 0, 0))],
                 [_row_spec(tm, c2), pl.BlockSpec((1, c2), lambda i: (0, 0)),
                  pl.BlockSpec((GROUPS, tm, tm), lambda i: (0, 0, 0)),
                  pl.BlockSpec((GROUPS, tm), lambda i: (0, 0)), vec, vec],
                 [_sds((t, c2), BF16), _sds((1, c2), F32), _sds((GROUPS, tm, tm), F32),
                  _sds((GROUPS, tm), F32), _sds((1, c), F32), _sds((1, c), F32)],
                 [pltpu.VMEM((tm, c), F32), pltpu.VMEM((tm, c), F32), pltpu.VMEM((GROUPS, tm, CHUNK), F32)],
                 deps=deps)(dq, h, g3, b3, wt, bst)


def _loss(name, y, target):
    t, d = y.shape
    tm = min(TM_ROW, t)
    n_tiles = t // tm

    def body(y_ref, t_ref, l_ref, dy_ref, acc_ref):
        i = pl.program_id(0)
        diff = y_ref[...] - t_ref[...]
        dy_ref[...] = diff * (1.0 / d)
        _acc_rows(acc_ref, _colsum(diff * diff), i == 0)

        @pl.when(i == n_tiles - 1)
        def _():
            l_ref[...] = jnp.broadcast_to(jnp.sum(acc_ref[...], axis=-1, keepdims=True) * (0.5 / d), (1, LANES))

    return _call(body, name, (n_tiles,), [_row_spec(tm, d), _row_spec(tm, d)],
                 [pl.BlockSpec((1, LANES), lambda i: (0, 0)), _row_spec(tm, d)],
                 [_sds((1, LANES), F32), _sds((t, d), F32)], [pltpu.VMEM((1, d), F32)])(y, target)


def _adamw(g, w, m, v):
    m2 = ADAM_B1 * m + (1.0 - ADAM_B1) * g
    v2 = ADAM_B2 * v + (1.0 - ADAM_B2) * (g * g)
    m_hat = m2 / (1.0 - ADAM_B1 ** ADAM_STEP)
    v_hat = v2 / (1.0 - ADAM_B2 ** ADAM_STEP)
    delta = -ADAM_LR * (m_hat / (jnp.sqrt(v_hat) + ADAM_EPS) + ADAM_WD * w)
    return delta, m2, v2


ROW_TILE_CAP = 512


def _row_tile(rows, cap=ROW_TILE_CAP):
    if rows <= cap:
        return rows
    for tr in range(cap, 15, -16):
        if rows % tr == 0:
            return tr
    return rows


def _sum8_adamw(name, dev, lands, parts, w, m, v):
    nl = len(lands)
    _, r, c = lands[0].shape
    tr = _row_tile(r, cap=256)

    def body(dev_ref, *refs):
        land, own = refs[:nl], refs[nl:2 * nl]
        w_ref, m_ref, v_ref, g_ref, d_ref, m2_ref, v2_ref = refs[2 * nl:]
        layer, me = pl.program_id(0), dev_ref[0]
        for l in range(nl):
            @pl.when(layer == l)
            def _(l=l):
                g = None
                for s in range(N_DEV):
                    part = jnp.where(me == s, own[l][...], land[l][s]).astype(F32)
                    g = part if g is None else g + part
                delta, m2, v2 = _adamw(g, w_ref[...], m_ref[...], v_ref[...])
                g_ref[...] = g
                d_ref[...] = delta
                m2_ref[...] = m2
                v2_ref[...] = v2

    def rows_of(l, a, i):
        return jnp.where(a == l, i, 0)

    spec = pl.BlockSpec((None, tr, c), lambda a, i, dev_ref: (a, i, 0))
    in_specs = [pl.BlockSpec((N_DEV, tr, c), lambda a, i, dev_ref, l=l: (0, rows_of(l, a, i), 0)) for l in range(nl)]
    in_specs += [pl.BlockSpec((None, tr, c), lambda a, i, dev_ref, l=l: (dev_ref[0], rows_of(l, a, i), 0))
                 for l in range(nl)]
    grid_spec = pltpu.PrefetchScalarGridSpec(
        num_scalar_prefetch=1, grid=(nl, r // tr), in_specs=in_specs + [spec] * 3, out_specs=[spec] * 4)
    return pl.pallas_call(
        body, name=name, grid_spec=grid_spec, out_shape=[_sds(w.shape, F32)] * 4,
        compiler_params=pltpu.CompilerParams(vmem_limit_bytes=VMEM_LIMIT))(dev, *lands, *parts, w, m, v)


def _sum8(name, parts):
    _, r, c = parts.shape
    tr = _row_tile(r)

    def body(p_ref, o_ref):
        acc = p_ref[0]
        for s in range(1, N_DEV):
            acc = acc + p_ref[s]
        o_ref[...] = acc

    return _call(body, name, (r // tr,), [pl.BlockSpec((N_DEV, tr, c), lambda i: (0, i, 0))],
                 pl.BlockSpec((tr, c), lambda i: (i, 0)), _sds((r, c), F32))(parts)


def _adamw_small(name, gs, ws, ms, vs):
    n = len(gs)

    def body(*refs):
        g, w, m, v = (refs[k * n:(k + 1) * n] for k in range(4))
        d_out, m_out, v_out = (refs[(4 + k) * n:(5 + k) * n] for k in range(3))
        for k in range(n):
            d_out[k][...], m_out[k][...], v_out[k][...] = _adamw(g[k][...], w[k][...], m[k][...], v[k][...])

    vmem = pl.BlockSpec(memory_space=pltpu.VMEM)
    outs = _call(body, name, (), [vmem] * (4 * n), [vmem] * (3 * n), [_sds(w.shape, F32) for w in ws] * 3)(
        *gs, *ws, *ms, *vs)
    return outs[:n], outs[n:2 * n], outs[2 * n:]


def _pack(arrs, row_multiple=SUBLANES):
    pieces, rows = [], 0
    for a in arrs:
        piece = a.reshape(-1, LANES)
        piece = jnp.pad(piece, ((0, (-piece.shape[0]) % SUBLANES), (0, 0)))
        pieces.append(piece)
        rows += piece.shape[0]
    if rows % row_multiple:
        pieces.append(jnp.zeros(((-rows) % row_multiple, LANES), pieces[0].dtype))
    return jnp.concatenate(pieces, axis=0)


def _unpack(buf, shapes, lead=0):
    out, pos = [], 0
    for shp in shapes:
        rows = math.prod(shp) // LANES
        piece = lax.slice_in_dim(buf, pos, pos + rows, axis=lead)
        out.append(piece.reshape(buf.shape[:lead] + tuple(shp)))
        pos += rows + (-rows) % SUBLANES
    return out


REPLICATED = ["conv_b_in", "conv_b_dw", "conv_ln_g", "conv_ln_b", "conv_b_out", "gmlp_w_s", "gmlp_b_s",
              "ffn_b_up", "ffn_b_dw", "ffn_b_down", "norm1_g", "norm1_b", "norm2_g", "norm2_b"]
SMALL_SHARDED = ["conv_w_dw", "gmlp_b_in", "gmlp_ln_g", "gmlp_ln_b", "gmlp_b_out", "ffn_w_dw"]
BIG = ["conv_w_in", "conv_w_out", "gmlp_w_in", "gmlp_w_out", "ffn_w_up", "ffn_w_down"]
WEIGHTS = ["conv_w_in", "conv_b_in", "conv_w_dw", "conv_b_dw", "conv_ln_g", "conv_ln_b", "conv_w_out", "conv_b_out",
           "gmlp_w_in", "gmlp_b_in", "gmlp_ln_g", "gmlp_ln_b", "gmlp_w_s", "gmlp_b_s", "gmlp_w_out", "gmlp_b_out",
           "ffn_w_up", "ffn_b_up", "ffn_w_dw", "ffn_b_dw", "ffn_w_down", "ffn_b_down",
           "norm1_g", "norm1_b", "norm2_g", "norm2_b"]


def _from_shards(g, lead_shape):
    nd = len(lead_shape)
    perm = tuple(range(1, nd + 1)) + (0, nd + 1)
    return g.transpose(perm).reshape(tuple(lead_shape) + (-1,))


def _to_shards(full, width):
    lead = full.shape[:-1]
    nd = len(lead)
    parts = full.reshape(lead + (N_DEV, width))
    return parts.transpose((nd,) + tuple(range(nd)) + (nd + 1,))


def _step(p):
    x_in, target_in = p["x"], p["loss_target"]
    bsz, seq, d = x_in.shape
    t = bsz * seq
    assert seq % TM_EW == 0 and TM_EW % CHUNK == 0 and TM_EW // SUBLANES >= CONV_K - 1
    x0 = _to_segments(x_in.reshape(t, d), TM_EW)
    target = _to_segments(target_in.reshape(t, d), TM_EW)
    n_conv, n_gmlp = p["conv_w_in"].shape[0], p["gmlp_w_in"].shape[0]
    fb = p["ffn_w_up"].shape[-1]
    nblk = N_DEV
    half = nblk // 2
    cw = p["conv_w_in"].shape[-1]
    tm = min(TM_MM, t)
    nt = t // tm
    tk = min(TK_DW, t)
    ntk = t // tk
    dev = 4 * lax.axis_index("x") + 2 * lax.axis_index("y") + lax.axis_index("c")

    small_shapes = [p[n].shape for n in SMALL_SHARDED]
    small_src = _pack([p[n] for n in SMALL_SHARDED])[None]
    small_all = _all_gather("gather_small_weights", [small_src])[0][0]
    sm = _unpack(small_all, small_shapes, lead=1)
    w_src = []
    for i in range(DEPTH):
        mix = "conv" if i % 2 == 0 else "gmlp"
        w_src += [p[mix + "_w_in"][i // 2].astype(BF16), p[mix + "_w_out"][i // 2].astype(BF16),
                  p["ffn_w_up"][i].T.astype(BF16), p["ffn_w_down"][i].astype(BF16)]
    send_sems, recv_sems, w_land, _ = _gather_start(
        "weights_gather_start", _place_own("weights_place_own", w_src, deps=[small_all]))
    W_IN, W_OUT, W_UP, W_DOWN = range(4)

    def wait_weight(i, k, after):
        return _gather_wait(f"l{i}_weights_wait{k}", w_land[4 * i + k], send_sems, recv_sems, 4 * i + k, after)
    conv_w_dw = _from_shards(sm[0], sm[0].shape[1:-1])
    gmlp_b_in = _from_shards(sm[1], sm[1].shape[1:-1])
    gmlp_ln_g = _from_shards(sm[2], sm[2].shape[1:-1])
    gmlp_ln_b = _from_shards(sm[3], sm[3].shape[1:-1])
    gmlp_b_out = _from_shards(sm[4], sm[4].shape[1:-1])
    ffn_w_dw = sm[5].transpose(1, 0, 2, 3)

    def rows3(a):
        return a.reshape(a.shape[0], 1, a.shape[-1])

    conv_b_in4 = p["conv_b_in"].reshape(n_conv, N_DEV, 1, cw)
    gmlp_b_in4 = gmlp_b_in.reshape(n_gmlp, N_DEV, 1, cw)
    ffn_b_up4 = p["ffn_b_up"].reshape(DEPTH, nblk, 1, fb)
    ffn_b_dw4 = p["ffn_b_dw"].reshape(DEPTH, nblk, 1, fb)
    conv_b_dw3, conv_ln_g3, conv_ln_b3 = rows3(p["conv_b_dw"]), rows3(p["conv_ln_g"]), rows3(p["conv_ln_b"])
    conv_b_out3, gmlp_b_out3, ffn_b_down3 = rows3(p["conv_b_out"]), rows3(gmlp_b_out), rows3(p["ffn_b_down"])
    gmlp_ln_g3, gmlp_ln_b3 = rows3(gmlp_ln_g), rows3(gmlp_ln_b)
    n1g3, n1b3, n2g3, n2b3 = rows3(p["norm1_g"]), rows3(p["norm1_b"]), rows3(p["norm2_g"]), rows3(p["norm2_b"])
    w_tile, bs_tile = _sgu_operands(p["gmlp_w_s"], p["gmlp_b_s"])

    def mm_in(name, xa, wg, l, bias4, glu=False):
        tmi = min(TM_LN, t)
        c_half = half * cw

        def body(a_ref, b_ref, bias_ref, h_ref, *u_ref):
            xb = a_ref[...].astype(BF16)
            for n in range(N_DEV):
                h_ref[:, n * cw:(n + 1) * cw] = jnp.dot(xb, b_ref[n], preferred_element_type=F32) + bias_ref[n]
            if glu:
                u_ref[0][...] = h_ref[:, :c_half] * _sigmoid(h_ref[:, c_half:])

        outs = _call(body, name, (t // tmi,),
                     [pl.BlockSpec((tmi, d), lambda i: (i, 0)), pl.BlockSpec((N_DEV, d, cw), lambda i: (0, 0, 0)),
                      pl.BlockSpec((None, N_DEV, 1, cw), lambda i: (l, 0, 0, 0))],
                     [pl.BlockSpec((tmi, N_DEV * cw), lambda i: (i, 0))]
                     + ([pl.BlockSpec((tmi, c_half), lambda i: (i, 0))] if glu else []),
                     [_sds((t, N_DEV * cw), F32)] + ([_sds((t, c_half), F32)] if glu else []))(xa, wg, bias4)
        return outs if glu else outs[0]

    def mm_out_dx(name, dy, w, deps=()):
        return _matmul(name, dy, w, "nt", grid=(nt,),
                       a_spec=pl.BlockSpec((tm, d), lambda i: (i, 0)),
                       b_spec=pl.BlockSpec((d, d), lambda i: (0, 0)),
                       o_spec=pl.BlockSpec((tm, d), lambda i: (i, 0)), o_shape=(t, d), o_dtype=F32, deps=deps)

    def mm_out_dw(name, sa, dy):
        return _matmul(name, sa, dy, "tn", grid=(nt,), k_axis=0, nk=nt, acc_shape=(d, d),
                       a_spec=pl.BlockSpec((tm, d), lambda k: (k, 0)),
                       b_spec=pl.BlockSpec((tm, d), lambda k: (k, 0)),
                       o_spec=pl.BlockSpec((d, d), lambda k: (0, 0)), o_shape=(d, d), o_dtype=BF16)

    def mm_in_dx(name, dh, wg, res, norm=None):
        tmx = min(TM_LN, t)

        def body(a_ref, b_ref, res_ref, *refs):
            y = ALPHA * res_ref[...]
            for n in range(N_DEV):
                y = y + lax.dot_general(a_ref[:, n * cw:(n + 1) * cw], b_ref[n], (_DIMS["nt"], ((), ())),
                                        preferred_element_type=F32)
            if norm is None:
                refs[0][...] = y
            else:
                _ln_bwd_rows(y, *refs, pl.program_id(0) == 0)

        row = pl.BlockSpec((tmx, d), lambda i: (i, 0))
        ins = [pl.BlockSpec((tmx, N_DEV * cw), lambda i: (i, 0)), pl.BlockSpec((N_DEV, d, cw), lambda i: (0, 0, 0)), row]
        if norm is None:
            return _call(body, name, (t // tmx,), ins, row, _sds((t, d), F32))(dh, wg, res)
        ln_ins, ln_outs = _ln_bwd_specs(tmx, d, norm[3], lambda i: (i, 0))
        return _call(body, name, (t // tmx,), ins + ln_ins, ln_outs,
                     [_sds((t, d), F32)] + [_sds((1, d), F32)] * 3)(dh, wg, res, *norm[:3])

    def mm_in_dw(name, xa, dh):
        def body(a_ref, b_ref, o_ref, acc_ref):
            k = pl.program_id(1)
            p = lax.dot_general(a_ref[...].astype(BF16), b_ref[...], (_DIMS["tn"], ((), ())),
                                preferred_element_type=F32)
            _acc_rows(acc_ref, p, k == 0)

            @pl.when(k == nt - 1)
            def _():
                for n in range(half):
                    o_ref[n] = acc_ref[:, n * cw:(n + 1) * cw].astype(BF16)

        return _call(body, name, (2, nt),
                     [pl.BlockSpec((tm, d), lambda c, k: (k, 0)), pl.BlockSpec((tm, half * cw), lambda c, k: (k, c))],
                     pl.BlockSpec((half, d, cw), lambda c, k: (c, 0, 0)), _sds((N_DEV, d, cw), BF16),
                     [pltpu.VMEM((d, half * cw), F32)])(xa, dh)

    def mm_down_dw(name, a, dy, deps=()):
        return _matmul(name, a, dy, "tn", grid=(half, ntk), k_axis=1, nk=ntk, acc_shape=(fb, d),
                       a_spec=pl.BlockSpec((None, tk, fb), lambda n, k: (n, k, 0)),
                       b_spec=pl.BlockSpec((tk, d), lambda n, k: (k, 0)),
                       o_spec=pl.BlockSpec((None, fb, d), lambda n, k: (n, 0, 0)),
                       o_shape=(half, fb, d), o_dtype=BF16, deps=deps)

    def mm_up_dw(name, xa, dh):
        return _matmul(name, dh, xa, "tn", grid=(nblk, ntk), k_axis=1, nk=ntk, acc_shape=(fb, d),
                       a_spec=pl.BlockSpec((None, tk, fb), lambda n, k: (n, k, 0)),
                       b_spec=pl.BlockSpec((tk, d), lambda n, k: (k, 0)),
                       o_spec=pl.BlockSpec((None, fb, d), lambda n, k: (n, 0, 0)),
                       o_shape=(nblk, fb, d), o_dtype=BF16)

    saved = []
    xcur = x0
    for i in range(DEPTH):
        j = i // 2
        s = {"x": xcur}
        s["w_in"] = wait_weight(i, W_IN, xcur)
        if i % 2 == 0:
            s["h"], s["u"] = mm_in(f"l{i}_conv_in_glu", xcur, s["w_in"], j, conv_b_in4, glu=True)
            s["c"] = _dwconv31(f"l{i}_dwconv", s["u"], conv_w_dw, conv_b_dw3, j, seq)
            s["s"] = _ln_silu(f"l{i}_ln_silu", s["c"], conv_ln_g3, conv_ln_b3, j)
            b_out3 = conv_b_out3
        else:
            s["h"] = mm_in(f"l{i}_gmlp_in", xcur, s["w_in"], j, gmlp_b_in4)
            s["s"] = _sgu(f"l{i}_sgu", s["h"], gmlp_ln_g3, gmlp_ln_b3, w_tile, bs_tile, j)
            b_out3 = gmlp_b_out3
        s["w_out"] = wait_weight(i, W_OUT, s["s"]).reshape(d, d)
        s["x1"], s["xhat1"], s["rstd1"] = _matmul_ln(
            f"l{i}_mixer_out_norm1", s["s"], s["w_out"], xcur, b_out3, n1g3, n1b3, j, i)
        s["w_up"] = wait_weight(i, W_UP, s["x1"])
        s["hg"], s["hv"], s["a"] = _ffn_up_act(f"l{i}_ffn_up_act", s["x1"], s["w_up"], ffn_b_up4, ffn_w_dw, ffn_b_dw4,
                                               i, seq)
        s["w_down"] = wait_weight(i, W_DOWN, s["a"]).reshape(half, fb, d)
        xcur, s["xhat2"], s["rstd2"] = _matmul_ln(
            f"l{i}_ffn_down_norm2", s["a"], s["w_down"], s["x1"], ffn_b_down3, n2g3, n2b3, i, i)
        saved.append(s)

    loss_row, dx = _loss("loss", xcur, target)

    started = {n: [None] * p[n].shape[0] for n in BIG}
    tokens = []

    def send_grads(name, items):
        done, token = _scatter_start(name, [g for _, _, g in items])
        for (n, l, _), st in zip(items, done):
            started[n][l] = st
        tokens.append(token)

    def take_tokens():
        out = list(tokens)
        tokens.clear()
        return out

    gl = {n: [None] * p[n].shape[0] for n in REPLICATED + SMALL_SHARDED}
    dr2, gl["norm2_g"][DEPTH - 1], gl["norm2_b"][DEPTH - 1], gl["ffn_b_down"][DEPTH - 1] = _ln_res_bwd(
        f"l{DEPTH - 1}_norm2_bwd", dx, saved[-1]["xhat2"], saved[-1]["rstd2"], n2g3, DEPTH - 1)
    for i in reversed(range(DEPTH)):
        j = i // 2
        s = saved[i]
        mix = "conv" if i % 2 == 0 else "gmlp"
        g_down = mm_down_dw(f"l{i}_ffn_down_dw", s["a"], dr2, deps=take_tokens()).reshape(N_DEV, -1, d)
        send_grads(f"l{i}_ffn_down_grad_scatter_start", [("ffn_w_down", i, g_down)])
        dcg, dcv, dbg, dbv, dwg, dwv = _ffn_act_bwd(f"l{i}_ffn_act_bwd", dr2, s["w_down"], s["hg"], s["hv"],
                                                    ffn_w_dw, ffn_b_dw4, i, seq, deps=take_tokens())
        gl["ffn_b_dw"][i] = jnp.concatenate([dbg, dbv], axis=0).reshape(1, nblk * fb)
        gl["ffn_w_dw"][i] = jnp.concatenate([dwg[:, :FFN_K], dwv[:, :FFN_K]], axis=0)
        dh, dbu, dr1, gl["norm1_g"][i], gl["norm1_b"][i], gl[mix + "_b_out"][j] = _ffn_conv_t_dx(
            f"l{i}_ffn_conv_t_dx", dcg, dcv, ffn_w_dw, s["w_up"], dr2, s["xhat1"], s["rstd1"], n1g3, i, seq,
            deps=take_tokens())
        gl["ffn_b_up"][i] = dbu.reshape(1, nblk * fb)
        send_grads(f"l{i}_ffn_up_grad_scatter_start", [("ffn_w_up", i, mm_up_dw(f"l{i}_ffn_up_dw", s["x1"], dh))])
        ds = mm_out_dx(f"l{i}_{mix}_out_dx", dr1, s["w_out"], deps=take_tokens())
        g_out = mm_out_dw(f"l{i}_{mix}_out_dw", s["s"], dr1).reshape(N_DEV, -1, d)
        if i % 2 == 0:
            dc, gl["conv_ln_g"][j], gl["conv_ln_b"][j], gl["conv_b_dw"][j] = _ln_silu_bwd(
                f"l{i}_ln_silu_bwd", ds, s["c"], conv_ln_g3, conv_ln_b3, j)
            du, dwdw = _dwconv31_bwd(f"l{i}_dwconv_bwd", dc, s["u"], conv_w_dw, j, seq)
            gl["conv_w_dw"][j] = dwdw[:CONV_K]
            dh, gl["conv_b_in"][j] = _glu_bwd(f"l{i}_glu_bwd", du, s["h"])
        else:
            dh, gl["gmlp_b_in"][j], dwt, dbt, gl["gmlp_ln_g"][j], gl["gmlp_ln_b"][j] = _sgu_bwd(
                f"l{i}_sgu_bwd", ds, s["h"], gmlp_ln_g3, gmlp_ln_b3, w_tile, bs_tile, j)
            gl["gmlp_w_s"][j], gl["gmlp_b_s"][j] = _sgu_param_grads(dwt, dbt)
            if i == 1:
                ws_local = jnp.stack(gl["gmlp_w_s"]).reshape(-1, LANES)
                ws_send, ws_recv, ws_land, ws_token = _gather_start(
                    "w_s_grads_gather_start", _place_own("w_s_grads_place_own", [ws_local]))
                tokens.append(ws_token)
        if i > 0:
            prev = saved[i - 1]
            dr2, gl["norm2_g"][i - 1], gl["norm2_b"][i - 1], gl["ffn_b_down"][i - 1] = mm_in_dx(
                f"l{i}_{mix}_in_dx_norm2_bwd", dh, s["w_in"], dr1, (prev["xhat2"], prev["rstd2"], n2g3, i - 1))
        else:
            dx = mm_in_dx(f"l{i}_{mix}_in_dx", dh, s["w_in"], dr1)
        send_grads(f"l{i}_mixer_grads_scatter_start",
                   [(mix + "_w_out", j, g_out), (mix + "_w_in", j, mm_in_dw(f"l{i}_{mix}_in_dw", s["x"], dh))])
    grad_x = _from_segments(dx, TM_EW).reshape(bsz, seq, d)

    late = [n for n in REPLICATED if n != "gmlp_w_s"]
    full_small = {n: jnp.stack(gl[n]).reshape(p[n].shape) for n in late}
    shard_small = {}
    for n in SMALL_SHARDED:
        if n == "ffn_w_dw":
            shard_small[n] = jnp.stack(gl[n]).transpose(1, 0, 2, 3)
        else:
            width = p[n].shape[-1]
            lead = p[n].shape[:-1]
            shard_small[n] = _to_shards(jnp.stack(gl[n]).reshape(lead + (N_DEV * width,)), width)
    flat_shapes = [(1, LANES)] + [p[n].shape for n in late] + [(N_DEV,) + p[n].shape for n in SMALL_SHARDED]
    flat_local = _pack([loss_row] + [full_small[n] for n in late] + [shard_small[n] for n in SMALL_SHARDED],
                       row_multiple=ROW_TILE_CAP)

    small_send, small_recv, small_land, small_token = _gather_start(
        "small_grads_gather_start", _place_own("small_grads_place_own", [flat_local]))

    grads, delta, new_m, new_v = {}, {}, {}, {}
    dev1 = jnp.reshape(dev, (1,)).astype(jnp.int32)
    order = ["ffn_w_down", "ffn_w_up", "gmlp_w_out", "gmlp_w_in", "conv_w_out", "conv_w_in"]
    after = small_token
    for n in order:
        parts_done, lands_done = _scatter_wait(f"grads_{n}_scatter_wait", started[n], after)
        state = [p[n], p["m_" + n], p["v_" + n]]
        if n == "ffn_w_up":
            state = [a.transpose(0, 2, 1) for a in state]
        outs = _sum8_adamw(f"adamw_{n}", dev1, lands_done, parts_done, *state)
        after = outs[-1]
        if n == "ffn_w_up":
            outs = [a.transpose(0, 2, 1) for a in outs]
        grads[n], delta[n], new_m[n], new_v[n] = outs

    ws_parts = _gather_wait("w_s_grads_gather_wait", ws_land[0], ws_send, ws_recv, 0, after)
    ws_sum = _sum8("sum_w_s_grads", ws_parts)
    grads["gmlp_w_s"] = ws_sum.reshape(p["gmlp_w_s"].shape)
    small_parts = _gather_wait("small_grads_gather_wait", small_land[0], small_send, small_recv, 0, ws_sum)
    summed = _unpack(_sum8("sum_small_grads", small_parts), flat_shapes)
    loss = summed[0][0, 0]
    grads.update(zip(late, summed[1:1 + len(late)]))
    for n, g in zip(SMALL_SHARDED, summed[1 + len(late):]):
        grads[n] = lax.dynamic_index_in_dim(g, dev, axis=0, keepdims=False)
    small = REPLICATED + SMALL_SHARDED
    d_s, m_s, v_s = _adamw_small("adamw_small", [grads[n] for n in small], [p[n] for n in small],
                                 [p["m_" + n] for n in small], [p["v_" + n] for n in small])
    for n, dd, mm, vv in zip(small, d_s, m_s, v_s):
        delta[n], new_m[n], new_v[n] = dd, mm, vv

    return (loss, grad_x, *[grads[n] for n in WEIGHTS], *[delta[n] for n in WEIGHTS],
            *[new_m[n] for n in WEIGHTS], *[new_v[n] for n in WEIGHTS])


def kernel(x, conv_w_in, conv_b_in, conv_w_dw, conv_b_dw, conv_ln_g, conv_ln_b, conv_w_out, conv_b_out, gmlp_w_in, gmlp_b_in, gmlp_ln_g, gmlp_ln_b, gmlp_w_s, gmlp_b_s, gmlp_w_out, gmlp_b_out, ffn_w_up, ffn_b_up, ffn_w_dw, ffn_b_dw, ffn_w_down, ffn_b_down, norm1_g, norm1_b, norm2_g, norm2_b, loss_target, m_conv_w_in, m_conv_b_in, m_conv_w_dw, m_conv_b_dw, m_conv_ln_g, m_conv_ln_b, m_conv_w_out, m_conv_b_out, m_gmlp_w_in, m_gmlp_b_in, m_gmlp_ln_g, m_gmlp_ln_b, m_gmlp_w_s, m_gmlp_b_s, m_gmlp_w_out, m_gmlp_b_out, m_ffn_w_up, m_ffn_b_up, m_ffn_w_dw, m_ffn_b_dw, m_ffn_w_down, m_ffn_b_down, m_norm1_g, m_norm1_b, m_norm2_g, m_norm2_b, v_conv_w_in, v_conv_b_in, v_conv_w_dw, v_conv_b_dw, v_conv_ln_g, v_conv_ln_b, v_conv_w_out, v_conv_b_out, v_gmlp_w_in, v_gmlp_b_in, v_gmlp_ln_g, v_gmlp_ln_b, v_gmlp_w_s, v_gmlp_b_s, v_gmlp_w_out, v_gmlp_b_out, v_ffn_w_up, v_ffn_b_up, v_ffn_w_dw, v_ffn_b_dw, v_ffn_w_down, v_ffn_b_down, v_norm1_g, v_norm1_b, v_norm2_g, v_norm2_b):
    return _step(dict(locals()))
```

```python
import math

import jax
import jax.numpy as jnp
from jax import lax
from jax.experimental import pallas as pl
from jax.experimental.pallas import tpu as pltpu

F32 = jnp.float32
BF16 = jnp.bfloat16
MESH = pl.DeviceIdType.MESH

N_DEV = 8
DEPTH = 4
ALPHA = (2.0 * DEPTH) ** 0.25
LN_EPS = 1e-5
CONV_K = 31
FFN_K = 3
CHUNK = 128
GROUPS = 8
ADAM_LR = 0.001
ADAM_B1 = 0.9
ADAM_B2 = 0.999
ADAM_EPS = 1e-08
ADAM_WD = 0.01
ADAM_STEP = 10
INV_SQRT2 = 1.0 / math.sqrt(2.0)
INV_SQRT2PI = 1.0 / math.sqrt(2.0 * math.pi)

LANES = 128
SUBLANES = 8
VMEM_LIMIT = 56 * 1024 * 1024
TM_MM = 1024
TK_DW = 2048
TM_EW = 256
TM_ROW = 512


def _call(body, name, grid, in_specs, out_specs, out_shape, scratch=(), aliases=None, deps=()):
    deps = list(deps)
    in_specs = list(in_specs)
    n_in = len(in_specs)
    if deps:
        inner = body

        def body(*refs):
            return inner(*refs[:n_in], *refs[n_in + len(deps):])

        in_specs = in_specs + [pl.BlockSpec(memory_space=pl.ANY)] * len(deps)
    fn = pl.pallas_call(
        body, name=name, grid=grid, in_specs=in_specs, out_specs=out_specs, out_shape=out_shape,
        scratch_shapes=list(scratch), input_output_aliases=aliases or {},
        compiler_params=pltpu.CompilerParams(vmem_limit_bytes=VMEM_LIMIT))
    return lambda *args: fn(*args, *deps)


def _sds(shape, dtype):
    return jax.ShapeDtypeStruct(tuple(shape), dtype)


def _sigmoid(x):
    return 1.0 / (1.0 + jnp.exp(-x))


def _acc_rows(ref, val, first):
    @pl.when(first)
    def _():
        ref[...] = val

    @pl.when(jnp.logical_not(first))
    def _():
        ref[...] += val


def _colsum(v):
    return jnp.sum(v, axis=0, keepdims=True)


_DIMS = {"nn": ((1,), (0,)), "nt": ((1,), (1,)), "tn": ((0,), (0,))}


def _matmul(name, a, b, mode, *, grid, a_spec, b_spec, o_spec, o_shape, o_dtype, k_axis=None, nk=1,
            acc_shape=None, bias=None, bias_spec=None, res=None, res_spec=None, res_scale=1.0, deps=()):
    dims = (_DIMS[mode], ((), ()))
    has_bias, has_res = bias is not None, res is not None

    def body(*refs):
        a_ref, b_ref = refs[0], refs[1]
        pos = 2
        bias_ref = res_ref = None
        if has_bias:
            bias_ref = refs[pos]
            pos += 1
        if has_res:
            res_ref = refs[pos]
            pos += 1
        o_ref = refs[pos]
        acc_ref = refs[pos + 1] if nk > 1 else None
        p = lax.dot_general(a_ref[...].astype(BF16), b_ref[...].astype(BF16), dims, preferred_element_type=F32)

        def finish(acc):
            if has_bias:
                acc = acc + bias_ref[...]
            if has_res:
                acc = acc + res_scale * res_ref[...]
            o_ref[...] = acc.astype(o_dtype)

        if nk == 1:
            finish(p)
        else:
            k = pl.program_id(k_axis)

            @pl.when(k == 0)
            def _():
                acc_ref[...] = p

            @pl.when(k > 0)
            def _():
                acc_ref[...] += p

            @pl.when(k == nk - 1)
            def _():
                finish(acc_ref[...])

    ins, specs = [a, b], [a_spec, b_spec]
    if has_bias:
        ins.append(bias)
        specs.append(bias_spec)
    if has_res:
        ins.append(res)
        specs.append(res_spec)
    scratch = [pltpu.VMEM(acc_shape, F32)] if nk > 1 else []
    return _call(body, name, grid, specs, o_spec, _sds(o_shape, o_dtype), scratch, deps=deps)(*ins)


TM_LN = 512


def _matmul_ln(name, a, b, x_res, bias3, g3, b3, l_bias, l_norm):
    t, d = x_res.shape
    tm = min(TM_LN, t)
    blocked = a.ndim == 3

    def body(a_ref, b_ref, x_ref, bias_ref, g_ref, be_ref, o_ref, xh_ref, rs_ref):
        if blocked:
            y = None
            for k in range(a.shape[0]):
                p = jnp.dot(a_ref[k], b_ref[k], preferred_element_type=F32)
                y = p if y is None else y + p
        else:
            y = jnp.dot(a_ref[...], b_ref[...], preferred_element_type=F32)
        xhat, rstd = _ln_stats(ALPHA * x_ref[...] + y + bias_ref[...])
        o_ref[...] = xhat * g_ref[...] + be_ref[...]
        xh_ref[...] = xhat
        rs_ref[...] = rstd

    if blocked:
        a_spec = pl.BlockSpec((a.shape[0], tm, a.shape[2]), lambda i: (0, i, 0))
        b_spec = pl.BlockSpec(b.shape, lambda i: (0, 0, 0))
    else:
        a_spec = pl.BlockSpec((tm, a.shape[1]), lambda i: (i, 0))
        b_spec = pl.BlockSpec(b.shape, lambda i: (0, 0))
    row = pl.BlockSpec((tm, d), lambda i: (i, 0))
    stat = pl.BlockSpec((tm, 1), lambda i: (i, 0))

    def vec(l):
        return pl.BlockSpec((None, 1, d), lambda i: (l, 0, 0))

    return _call(body, name, (t // tm,), [a_spec, b_spec, row, vec(l_bias), vec(l_norm), vec(l_norm)],
                 [row, row, stat], [_sds((t, d), F32), _sds((t, d), F32), _sds((t, 1), F32)])(
                     a, b, x_res, bias3, g3, b3)


def _mesh_pos():
    return lax.axis_index("x"), lax.axis_index("y"), lax.axis_index("c")


def _any_specs(n):
    return [pl.BlockSpec(memory_space=pl.ANY)] * n


def _all_gather(name, srcs):
    n = len(srcs)

    def body(*refs):
        src, out = refs[:n], refs[n:2 * n]
        send_sems, recv_sems, local_sems = refs[2 * n:]
        x, y, c = _mesh_pos()
        me, sibling = (x, y, c), (x, y, 1 - c)
        chips = [(1 - x, y), (x, 1 - y), (1 - x, 1 - y)]

        def slot(k, p):
            return out[k].at[:, 4 * p[0] + 2 * p[1] + p[2]]

        def copy(k, idx, block, to, s=None):
            return pltpu.make_async_remote_copy(
                src_ref=slot(k, block) if s is None else s, dst_ref=slot(k, block),
                send_sem=send_sems.at[k * 7 + idx], recv_sem=recv_sems.at[k * 7 + idx],
                device_id=to, device_id_type=MESH)

        local = [pltpu.make_async_copy(src[k], slot(k, me), local_sems.at[k]) for k in range(n)]
        for cp in local:
            cp.start()
        first = []
        for k in range(n):
            first.append(copy(k, 0, me, sibling, src[k]))
            for j, chip in enumerate(chips):
                first.append(copy(k, 1 + j, me, (*chip, c), src[k]))
        for cp in first:
            cp.start()
        passed = []
        for j, chip in enumerate(chips):
            for k in range(n):
                copy(k, 1 + j, (*chip, c), me).wait_recv()
                cp = copy(k, 4 + j, (*chip, c), sibling)
                cp.start()
                passed.append(cp)
        for k in range(n):
            copy(k, 0, sibling, me).wait_recv()
            for j, chip in enumerate(chips):
                copy(k, 4 + j, (*chip, 1 - c), me).wait_recv()
        for cp in first + passed:
            cp.wait_send()
        for cp in local:
            cp.wait()

    out_shape = [_sds((s.shape[0], N_DEV) + s.shape[1:], s.dtype) for s in srcs]
    return _call(body, name, (), [pl.BlockSpec(memory_space=pltpu.VMEM)] * n, _any_specs(n), out_shape,
                 [pltpu.SemaphoreType.DMA((7 * n,)), pltpu.SemaphoreType.DMA((7 * n,)),
                  pltpu.SemaphoreType.DMA((n,))])(*srcs)


HBM_SPEC = pl.BlockSpec(memory_space=pltpu.HBM)
SEM_SPEC = pl.BlockSpec(memory_space=pltpu.SEMAPHORE)
N_PEER = N_DEV - 1


def _split_call(body, name, in_specs, out_specs, out_shape, aliases):
    return pl.pallas_call(
        body, name=name, in_specs=in_specs, out_specs=out_specs, out_shape=out_shape, input_output_aliases=aliases,
        compiler_params=pltpu.CompilerParams(has_side_effects=pltpu.SideEffectType.DATAFLOW_SIDE_EFFECTING))


def _peers(x, y, c):
    return [(1 - x if q & 4 else x, 1 - y if q & 2 else y, 1 - c if q & 1 else c) for q in range(1, N_DEV)]


def _in_hbm(a):
    return pltpu.with_memory_space_constraint(a, pltpu.HBM)


def _place_own(name, srcs, deps=()):
    n = len(srcs)

    def body(*refs):
        src, out, sems = refs[:n], refs[n:2 * n], refs[2 * n]
        x, y, c = _mesh_pos()
        dev = 4 * x + 2 * y + c
        copies = [pltpu.make_async_copy(src[k], out[k].at[dev], sems.at[k]) for k in range(n)]
        for cp in copies:
            cp.start()
        for cp in copies:
            cp.wait()

    return _call(body, name, (), [pl.BlockSpec(memory_space=pltpu.VMEM)] * n, _any_specs(n),
                 [_sds((N_DEV,) + s.shape, s.dtype) for s in srcs], [pltpu.SemaphoreType.DMA((n,))],
                 deps=deps)(*srcs)


def _gather_start(name, lands):
    n = len(lands)

    def body(*refs):
        land, send_sems, recv_sems = refs[:n], refs[n], refs[n + 1]
        x, y, c = _mesh_pos()
        dev = 4 * x + 2 * y + c
        for k in range(n):
            for peer in _peers(x, y, c):
                pltpu.make_async_remote_copy(
                    src_ref=land[k].at[dev], dst_ref=land[k].at[dev], send_sem=send_sems.at[k],
                    recv_sem=recv_sems.at[k], device_id=peer, device_id_type=MESH).start()
        token = refs[-1]
        token[...] = jnp.zeros_like(token)

    outs = _split_call(
        body, name, [HBM_SPEC] * n, [SEM_SPEC, SEM_SPEC] + [HBM_SPEC] * n + [pl.BlockSpec(memory_space=pltpu.VMEM)],
        [pltpu.SemaphoreType.DMA((n,)), pltpu.SemaphoreType.DMA((n,))] + [pltpu.HBM(a.shape, a.dtype) for a in lands]
        + [_sds((SUBLANES, LANES), F32)],
        {k: 2 + k for k in range(n)})(*[_in_hbm(a) for a in lands])
    return outs[0], outs[1], list(outs[2:2 + n]), outs[-1]


def _wait_seven(src_ref, dst_ref, send_sem, recv_sem):
    cp = pltpu.make_async_remote_copy(
        src_ref=src_ref.at[pl.ds(0, N_PEER)], dst_ref=dst_ref.at[pl.ds(0, N_PEER)], send_sem=send_sem,
        recv_sem=recv_sem, device_id=_mesh_pos(), device_id_type=MESH)
    cp.wait_send()
    cp.wait_recv()


def _gather_wait(name, land, send_sems, recv_sems, k, after):
    def body(land_ref, send_ref, recv_ref, after_ref, out_ref):
        _wait_seven(land_ref, land_ref, send_ref.at[k], recv_ref.at[k])

    return _split_call(body, name, [HBM_SPEC, SEM_SPEC, SEM_SPEC, pl.BlockSpec(memory_space=pl.ANY)], HBM_SPEC,
                       pltpu.HBM(land.shape, land.dtype), {0: 0})(land, send_sems, recv_sems, after)


def _scatter_start(name, parts_list):
    n = len(parts_list)

    def body(*refs):
        x, y, c = _mesh_pos()
        dev = 4 * x + 2 * y + c
        for k in range(n):
            parts_ref, land_ref = refs[2 * k], refs[2 * k + 1]
            send_sem, recv_sem = refs[2 * n + 4 * k], refs[2 * n + 4 * k + 1]
            for peer in _peers(x, y, c):
                pltpu.make_async_remote_copy(
                    src_ref=parts_ref.at[4 * peer[0] + 2 * peer[1] + peer[2]], dst_ref=land_ref.at[dev],
                    send_sem=send_sem, recv_sem=recv_sem, device_id=peer, device_id_type=MESH).start()
        token = refs[-1]
        token[...] = jnp.zeros_like(token)

    ins, out_specs, out_shape, aliases = [], [], [], {}
    for k, parts in enumerate(parts_list):
        buf = pltpu.HBM(parts.shape, parts.dtype)
        ins += [_in_hbm(parts), _in_hbm(lax.empty(parts.shape, parts.dtype))]
        out_specs += [SEM_SPEC, SEM_SPEC, HBM_SPEC, HBM_SPEC]
        out_shape += [pltpu.SemaphoreType.DMA(()), pltpu.SemaphoreType.DMA(()), buf, buf]
        aliases.update({2 * k: 4 * k + 2, 2 * k + 1: 4 * k + 3})
    outs = _split_call(body, name, [HBM_SPEC] * (2 * n), out_specs + [pl.BlockSpec(memory_space=pltpu.VMEM)],
                       out_shape + [_sds((SUBLANES, LANES), F32)], aliases)(*ins)
    return [tuple(outs[4 * k:4 * k + 4]) for k in range(n)], outs[-1]


def _scatter_wait(name, started, after):
    n = len(started)

    def body(*refs):
        for k in range(n):
            send_sem, recv_sem, parts_ref, land_ref = refs[4 * k:4 * k + 4]
            _wait_seven(parts_ref, land_ref, send_sem, recv_sem)

    flat = [a for s in started for a in s]
    outs = _split_call(
        body, name, [SEM_SPEC, SEM_SPEC, HBM_SPEC, HBM_SPEC] * n + [pl.BlockSpec(memory_space=pl.ANY)],
        [HBM_SPEC, HBM_SPEC] * n, [pltpu.HBM(a.shape, a.dtype) for s in started for a in s[2:]],
        {4 * k + 2 + t: 2 * k + t for k in range(n) for t in range(2)})(*flat, after)
    return list(outs[0::2]), list(outs[1::2])


def _to_segments(a, tile):
    seg = tile // SUBLANES
    return a.reshape((a.shape[0] // tile, SUBLANES, seg) + a.shape[1:]).swapaxes(1, 2).reshape(a.shape)


def _from_segments(a, tile):
    seg = tile // SUBLANES
    return a.reshape((a.shape[0] // tile, seg, SUBLANES) + a.shape[1:]).swapaxes(1, 2).reshape(a.shape)


def _chunk(ref, q):
    return ref[q * SUBLANES:(q + 1) * SUBLANES, :]


def _fill_wrap_prev(x_ref, halo_ref, wrap_ref, n_wrap, n_halo, seg, keep):
    sub = lax.broadcasted_iota(jnp.int32, (SUBLANES, x_ref.shape[-1]), 0)
    for j in range(n_wrap):
        q = seg - n_wrap + j
        hq = q - (seg - n_halo)
        row = halo_ref[hq * SUBLANES + SUBLANES - 1:(hq + 1) * SUBLANES, :] * keep
        wrap_ref[j * SUBLANES:(j + 1) * SUBLANES, :] = jnp.where(sub == 0, row, pltpu.roll(_chunk(x_ref, q), 1, 0))


def _fill_wrap_next(x_ref, halo_ref, wrap_ref, n_wrap, keep):
    sub = lax.broadcasted_iota(jnp.int32, (SUBLANES, x_ref.shape[-1]), 0)
    for j in range(n_wrap):
        row = halo_ref[j * SUBLANES:j * SUBLANES + 1, :] * keep
        wrap_ref[j * SUBLANES:(j + 1) * SUBLANES, :] = jnp.where(
            sub == SUBLANES - 1, row, pltpu.roll(_chunk(x_ref, j), SUBLANES - 1, 0))


def _past(x_ref, wrap_ref, q, d, n_wrap):
    return _chunk(x_ref, q - d) if q >= d else _chunk(wrap_ref, q - d + n_wrap)


def _future(x_ref, wrap_ref, q, d, seg):
    return _chunk(x_ref, q + d) if q + d < seg else _chunk(wrap_ref, q + d - seg)


def _conv_fwd(x_ref, wrap_ref, w_ref, b_ref, out_ref, seg, k_taps):
    bias = jnp.broadcast_to(b_ref[...], (SUBLANES, x_ref.shape[-1]))
    for q in range(seg):
        acc = bias
        for k in range(k_taps):
            acc = acc + w_ref[k:k + 1, :] * _past(x_ref, wrap_ref, q, k_taps - 1 - k, k_taps - 1)
        out_ref[q * SUBLANES:(q + 1) * SUBLANES, :] = acc


def _conv_bwd_data(d_ref, wrap_ref, w_ref, out_ref, seg, k_taps):
    for q in range(seg):
        acc = None
        for k in range(k_taps):
            term = w_ref[k:k + 1, :] * _future(d_ref, wrap_ref, q, k_taps - 1 - k, seg)
            acc = term if acc is None else acc + term
        out_ref[q * SUBLANES:(q + 1) * SUBLANES, :] = acc


def _conv_bwd_taps(d_ref, x_ref, wrap_ref, dw_ref, seg, k_taps):
    for k in range(k_taps):
        part = None
        for q in range(seg):
            term = _chunk(d_ref, q) * _past(x_ref, wrap_ref, q, k_taps - 1 - k, k_taps - 1)
            part = term if part is None else part + term
        dw_ref[k:k + 1, :] += _colsum(part)


def _tile_halo_specs(tm, width_block, n_halo, n_tiles, block_of):
    rows = n_halo * SUBLANES
    per = tm // rows
    tile = pl.BlockSpec(width_block(tm), lambda n, i: block_of(n, i))
    prev = pl.BlockSpec(width_block(rows), lambda n, i: block_of(n, jnp.maximum(i * per - 1, 0)))
    nxt = pl.BlockSpec(width_block(rows), lambda n, i: block_of(n, jnp.minimum((i + 1) * per, n_tiles * per - 1)))
    return tile, prev, nxt


def _ln_stats(v):
    mu = jnp.mean(v, axis=-1, keepdims=True)
    vc = v - mu
    var = jnp.mean(vc * vc, axis=-1, keepdims=True)
    rstd = lax.rsqrt(var + LN_EPS)
    return vc * rstd, rstd


def _ln_backward(dxhat, xhat, rstd):
    m1 = jnp.mean(dxhat, axis=-1, keepdims=True)
    m2 = jnp.mean(dxhat * xhat, axis=-1, keepdims=True)
    return rstd * (dxhat - m1 - xhat * m2)


def _row_spec(tm, width):
    return pl.BlockSpec((tm, width), lambda i: (i, 0))


def _param_spec(l, width):
    return pl.BlockSpec((None, 1, width), lambda *_: (l, 0, 0))


def _ln_bwd_rows(dout, xh_ref, rs_ref, g_ref, dr_ref, dg_ref, db_ref, dsum_ref, first):
    xhat = xh_ref[...]
    dr = _ln_backward(dout * g_ref[...], xhat, rs_ref[...])
    dr_ref[...] = dr
    _acc_rows(dg_ref, _colsum(dout * xhat), first)
    _acc_rows(db_ref, _colsum(dout), first)
    _acc_rows(dsum_ref, _colsum(dr), first)


def _ln_bwd_specs(tm, d, l, row_of):
    vec = pl.BlockSpec((1, d), lambda *_: (0, 0))
    ins = [pl.BlockSpec((tm, d), row_of), pl.BlockSpec((tm, 1), row_of), _param_spec(l, d)]
    return ins, [pl.BlockSpec((tm, d), row_of), vec, vec, vec]


def _ln_res_bwd(name, dout, xhat, rstd, g3, l, deps=()):
    t, d = dout.shape
    tm = min(TM_ROW, t)

    def body(do_ref, xh_ref, rs_ref, g_ref, dr_ref, dg_ref, db_ref, dc_ref):
        _ln_bwd_rows(do_ref[...], xh_ref, rs_ref, g_ref, dr_ref, dg_ref, db_ref, dc_ref, pl.program_id(0) == 0)

    ins, outs = _ln_bwd_specs(tm, d, l, lambda i: (i, 0))
    return _call(body, name, (t // tm,), [_row_spec(tm, d)] + ins, outs,
                 [_sds((t, d), F32)] + [_sds((1, d), F32)] * 3, deps=deps)(dout, xhat, rstd, g3)


def _glu_bwd(name, du, h):
    t, c2 = h.shape
    c = c2 // 2
    tm = min(TM_ROW, t)

    def body(du_ref, a_ref, g_ref, dh_ref, db_ref):
        first = pl.program_id(0) == 0
        du_v, a = du_ref[...], a_ref[...]
        sg = _sigmoid(g_ref[...])
        da = du_v * sg
        dg = du_v * a * sg * (1.0 - sg)
        dh_ref[:, :c] = da.astype(BF16)
        dh_ref[:, c:] = dg.astype(BF16)
        _acc_rows(db_ref.at[:, :c], _colsum(da), first)
        _acc_rows(db_ref.at[:, c:], _colsum(dg), first)

    return _call(body, name, (t // tm,),
                 [_row_spec(tm, c), pl.BlockSpec((tm, c), lambda i: (i, 0)), pl.BlockSpec((tm, c), lambda i: (i, 1))],
                 [_row_spec(tm, c2), pl.BlockSpec((1, c2), lambda i: (0, 0))],
                 [_sds((t, c2), BF16), _sds((1, c2), F32)])(du, h, h)


CONV_CB = 512
TAPS_PAD = 32


def _dwconv31(name, u, w3, b3, l, seq):
    t, c = u.shape
    tm, cb = TM_EW, CONV_CB
    seg, seq_tiles, n_tiles = tm // SUBLANES, seq // tm, t // tm
    n_wrap = CONV_K - 1
    tile, prev, _ = _tile_halo_specs(tm, lambda rows: (rows, cb), seg, n_tiles, lambda n, r: (r, n))

    def body(u_ref, halo_ref, w_ref, b_ref, o_ref, wrap_ref):
        keep = (pl.program_id(1) % seq_tiles != 0).astype(F32)
        _fill_wrap_prev(u_ref, halo_ref, wrap_ref, n_wrap, seg, seg, keep)
        _conv_fwd(u_ref, wrap_ref, w_ref, b_ref, o_ref, seg, CONV_K)

    return _call(body, name, (c // cb, n_tiles),
                 [tile, prev, pl.BlockSpec((None, CONV_K, cb), lambda n, i: (l, 0, n)),
                  pl.BlockSpec((None, 1, cb), lambda n, i: (l, 0, n))],
                 tile, _sds((t, c), F32), [pltpu.VMEM((n_wrap * SUBLANES, cb), F32)])(u, u, w3, b3)


def _dwconv31_bwd(name, dc, u, w3, l, seq):
    t, c = dc.shape
    tm, cb = TM_EW, CONV_CB
    seg, seq_tiles, n_tiles = tm // SUBLANES, seq // tm, t // tm
    n_wrap = CONV_K - 1
    tile, prev, nxt = _tile_halo_specs(tm, lambda rows: (rows, cb), seg, n_tiles, lambda n, r: (r, n))

    def body(dc_ref, dcn_ref, u_ref, up_ref, w_ref, du_ref, dw_ref, dwrap_ref, uwrap_ref):
        i = pl.program_id(1)
        keep_prev = (i % seq_tiles != 0).astype(F32)
        keep_next = (i % seq_tiles != seq_tiles - 1).astype(F32)
        _fill_wrap_next(dc_ref, dcn_ref, dwrap_ref, n_wrap, keep_next)
        _conv_bwd_data(dc_ref, dwrap_ref, w_ref, du_ref, seg, CONV_K)

        @pl.when(i == 0)
        def _():
            dw_ref[...] = jnp.zeros_like(dw_ref)

        _fill_wrap_prev(u_ref, up_ref, uwrap_ref, n_wrap, seg, seg, keep_prev)
        _conv_bwd_taps(dc_ref, u_ref, uwrap_ref, dw_ref, seg, CONV_K)

    wrap = pltpu.VMEM((n_wrap * SUBLANES, cb), F32)
    return _call(body, name, (c // cb, n_tiles),
                 [tile, nxt, tile, prev, pl.BlockSpec((None, CONV_K, cb), lambda n, i: (l, 0, n))],
                 [tile, pl.BlockSpec((TAPS_PAD, cb), lambda n, i: (0, n))],
                 [_sds((t, c), F32), _sds((TAPS_PAD, c), F32)], [wrap, wrap])(dc, dc, u, u, w3)


def _ln_silu(name, cx, g3, b3, l):
    t, d = cx.shape
    tm = min(TM_ROW, t)

    def body(c_ref, g_ref, b_ref, o_ref):
        xhat, _ = _ln_stats(c_ref[...])
        nv = xhat * g_ref[...] + b_ref[...]
        o_ref[...] = (nv * _sigmoid(nv)).astype(BF16)

    return _call(body, name, (t // tm,), [_row_spec(tm, d), _param_spec(l, d), _param_spec(l, d)],
                 _row_spec(tm, d), _sds((t, d), BF16))(cx, g3, b3)


def _ln_silu_bwd(name, ds, cx, g3, b3, l, deps=()):
    t, d = cx.shape
    tm = min(TM_ROW, t)

    def body(ds_ref, c_ref, g_ref, b_ref, dc_ref, dg_ref, db_ref, dsum_ref):
        first = pl.program_id(0) == 0
        xhat, rstd = _ln_stats(c_ref[...])
        g = g_ref[...]
        nv = xhat * g + b_ref[...]
        sg = _sigmoid(nv)
        dn = ds_ref[...] * (sg * (1.0 + nv * (1.0 - sg)))
        dc = _ln_backward(dn * g, xhat, rstd)
        dc_ref[...] = dc
        _acc_rows(dg_ref, _colsum(dn * xhat), first)
        _acc_rows(db_ref, _colsum(dn), first)
        _acc_rows(dsum_ref, _colsum(dc), first)

    vec = pl.BlockSpec((1, d), lambda i: (0, 0))
    return _call(body, name, (t // tm,),
                 [_row_spec(tm, d), _row_spec(tm, d), _param_spec(l, d), _param_spec(l, d)],
                 [_row_spec(tm, d), vec, vec, vec],
                 [_sds((t, d), F32)] + [_sds((1, d), F32)] * 3, deps=deps)(ds, cx, g3, b3)


FFN_HALO = FFN_K - 1


def _ffn_conv(x_ref, halo_ref, wrap_ref, w_ref, b_ref, keep, seg, out_ref):
    _fill_wrap_prev(x_ref, halo_ref, wrap_ref, FFN_K - 1, FFN_HALO, seg, keep)
    _conv_fwd(x_ref, wrap_ref, w_ref, b_ref, out_ref, seg, FFN_K)


TM_FFN = 512
WRAP_ROWS = FFN_HALO * SUBLANES


def _sub_tiles(x_ref, prev_ref, next_ref, keep_prev, keep_next, n_sub):
    out = []
    for s in range(n_sub):
        tile = x_ref.at[pl.ds(s * TM_EW, TM_EW)]
        prev = prev_ref if s == 0 else x_ref.at[pl.ds(s * TM_EW - WRAP_ROWS, WRAP_ROWS)]
        nxt = next_ref if s == n_sub - 1 else x_ref.at[pl.ds((s + 1) * TM_EW, WRAP_ROWS)]
        out.append((tile, prev, keep_prev if s == 0 else 1.0, nxt, keep_next if s == n_sub - 1 else 1.0))
    return out


def _rows(ref, s, rows):
    return ref.at[pl.ds(s * rows, rows)]


def _ffn_specs(tm, fb, n_tiles):
    return _tile_halo_specs(tm, lambda rows: (None, rows, fb), FFN_HALO, n_tiles, lambda n, r: (n, r, 0))


def _ffn_up_act(name, x, w_up, b_up4, wdw, bdw, l, seq):
    t, d = x.shape
    nb, fb, _ = w_up.shape
    half = nb // 2
    tm = min(TM_FFN, seq)
    n_sub, seg, seq_steps, n_steps = tm // TM_EW, TM_EW // SUBLANES, seq // tm, t // tm
    per = tm // WRAP_ROWS
    nt_dims = (_DIMS["nt"], ((), ()))

    def body(x_ref, xp_ref, ug_ref, uv_ref, bug_ref, buv_ref, wg_ref, wv_ref, bg_ref, bv_ref,
             hg_ref, hv_ref, a_ref, pg_ref, pv_ref, gwrap_ref, vwrap_ref, cg_ref, cv_ref):
        keep = (pl.program_id(1) % seq_steps != 0).astype(F32)
        xb, xpb = x_ref[...].astype(BF16), xp_ref[...].astype(BF16)
        hg_ref[...] = lax.dot_general(xb, ug_ref[...], nt_dims, preferred_element_type=F32) + bug_ref[...]
        pg_ref[...] = lax.dot_general(xpb, ug_ref[...], nt_dims, preferred_element_type=F32) + bug_ref[...]
        for s, (tile, prev, kp, _, _) in enumerate(_sub_tiles(hg_ref, pg_ref, None, keep, None, n_sub)):
            _ffn_conv(tile, prev, gwrap_ref, wg_ref, bg_ref, kp, seg, _rows(cg_ref, s, TM_EW))
        hv_ref[...] = lax.dot_general(xb, uv_ref[...], nt_dims, preferred_element_type=F32) + buv_ref[...]
        pv_ref[...] = lax.dot_general(xpb, uv_ref[...], nt_dims, preferred_element_type=F32) + buv_ref[...]
        for s, (tile, prev, kp, _, _) in enumerate(_sub_tiles(hv_ref, pv_ref, None, keep, None, n_sub)):
            _ffn_conv(tile, prev, vwrap_ref, wv_ref, bv_ref, kp, seg, _rows(cv_ref, s, TM_EW))
        cg = cg_ref[...]
        a_ref[...] = (cg * _sigmoid(cg) * cv_ref[...]).astype(BF16)

    def blk(shift):
        return pl.BlockSpec((None, fb, d), lambda n, i: (n + shift, 0, 0))

    def vec(shift, rows):
        return pl.BlockSpec((None, None, rows, fb), lambda n, i: (l, n + shift, 0, 0))

    out = pl.BlockSpec((None, tm, fb), lambda n, i: (n, i, 0))
    tmp = pltpu.VMEM((tm, fb), F32)
    halo = pltpu.VMEM((WRAP_ROWS, fb), F32)
    return _call(body, name, (half, n_steps),
                 [pl.BlockSpec((tm, d), lambda n, i: (i, 0)),
                  pl.BlockSpec((WRAP_ROWS, d), lambda n, i: (jnp.maximum(i * per - 1, 0), 0)),
                  blk(0), blk(half), vec(0, 1), vec(half, 1), vec(0, FFN_K), vec(half, FFN_K), vec(0, 1), vec(half, 1)],
                 [out, out, out],
                 [_sds((half, t, fb), F32), _sds((half, t, fb), F32), _sds((half, t, fb), BF16)],
                 [halo, halo, halo, halo, tmp, tmp])(x, x, w_up, w_up, b_up4, b_up4, wdw, wdw, bdw, bdw)


def _ffn_act_bwd(name, dy, w_down, hg, hv, wdw, bdw, l, seq, deps=()):
    half, t, fb = hg.shape
    d = dy.shape[-1]
    tm = min(TM_FFN, seq)
    n_sub, seg, seq_steps, n_steps = tm // TM_EW, TM_EW // SUBLANES, seq // tm, t // tm
    tile, prev, _ = _ffn_specs(tm, fb, n_steps)

    def body(dy_ref, wd_ref, g_ref, gp_ref, v_ref, vp_ref, wg_ref, wv_ref, bg_ref, bv_ref,
             dg_ref, dv_ref, dbg_ref, dbv_ref, dwg_ref, dwv_ref, gwrap_ref, vwrap_ref, cg_ref, cv_ref):
        i = pl.program_id(1)
        first = i == 0
        keep = (i % seq_steps != 0).astype(F32)
        da = lax.dot_general(dy_ref[...].astype(BF16), wd_ref[...], (_DIMS["nt"], ((), ())),
                             preferred_element_type=F32)
        g_tiles = _sub_tiles(g_ref, gp_ref, None, keep, None, n_sub)
        v_tiles = _sub_tiles(v_ref, vp_ref, None, keep, None, n_sub)
        for s in range(n_sub):
            _ffn_conv(g_tiles[s][0], g_tiles[s][1], _rows(gwrap_ref, s, WRAP_ROWS), wg_ref, bg_ref, g_tiles[s][2],
                      seg, _rows(cg_ref, s, TM_EW))
            _ffn_conv(v_tiles[s][0], v_tiles[s][1], _rows(vwrap_ref, s, WRAP_ROWS), wv_ref, bv_ref, v_tiles[s][2],
                      seg, _rows(cv_ref, s, TM_EW))
        cg, cv = cg_ref[...], cv_ref[...]
        sg = _sigmoid(cg)
        dcv = da * cg * sg
        dcg = da * cv * sg * (1.0 + cg * (1.0 - sg))
        dg_ref[...] = dcg
        dv_ref[...] = dcv
        _acc_rows(dbg_ref, _colsum(dcg), first)
        _acc_rows(dbv_ref, _colsum(dcv), first)

        @pl.when(first)
        def _():
            dwg_ref[...] = jnp.zeros_like(dwg_ref)
            dwv_ref[...] = jnp.zeros_like(dwv_ref)

        for s in range(n_sub):
            _conv_bwd_taps(_rows(dg_ref, s, TM_EW), g_tiles[s][0], _rows(gwrap_ref, s, WRAP_ROWS), dwg_ref, seg, FFN_K)
            _conv_bwd_taps(_rows(dv_ref, s, TM_EW), v_tiles[s][0], _rows(vwrap_ref, s, WRAP_ROWS), dwv_ref, seg, FFN_K)

    def vec(shift, rows):
        return pl.BlockSpec((None, None, rows, fb), lambda n, i: (l, n + shift, 0, 0))

    def acc(rows):
        return pl.BlockSpec((None, rows, fb), lambda n, i: (n, 0, 0))

    wrap = pltpu.VMEM((n_sub * WRAP_ROWS, fb), F32)
    tmp = pltpu.VMEM((tm, fb), F32)
    return _call(body, name, (half, n_steps),
                 [pl.BlockSpec((tm, d), lambda n, i: (i, 0)), pl.BlockSpec((None, fb, d), lambda n, i: (n, 0, 0)),
                  tile, prev, tile, prev, vec(0, FFN_K), vec(half, FFN_K), vec(0, 1), vec(half, 1)],
                 [tile, tile, acc(1), acc(1), acc(SUBLANES), acc(SUBLANES)],
                 [_sds((half, t, fb), F32), _sds((half, t, fb), F32), _sds((half, 1, fb), F32),
                  _sds((half, 1, fb), F32), _sds((half, SUBLANES, fb), F32), _sds((half, SUBLANES, fb), F32)],
                 [wrap, wrap, tmp, tmp], deps=deps)(dy, w_down, hg, hg, hv, hv, wdw, wdw, bdw, bdw)


def _ffn_conv_t_dx(name, dcg, dcv, wdw, w_up, res, xhat, rstd, g3, l, seq, deps=()):
    half, t, fb = dcg.shape
    nb, d = 2 * half, res.shape[-1]
    tm = min(TM_FFN, seq)
    n_sub, seg, seq_steps, n_steps = tm // TM_EW, TM_EW // SUBLANES, seq // tm, t // tm
    per = tm // WRAP_ROWS

    pair = 2
    n_pairs, half_pairs = nb // pair, half // pair

    def body(g_ref, gn_ref, v_ref, vn_ref, w_ref, up_ref, res_ref, xh_ref, rs_ref, gam_ref,
             dh_ref, db_ref, dr_ref, dgam_ref, dbeta_ref, dsum_ref, wrap_ref, out_ref, acc_ref):
        i, m = pl.program_id(0), pl.program_id(1)
        keep = (i % seq_steps != seq_steps - 1).astype(F32)

        def conv_t(d_ref, dn_ref, b):
            for s, (sub, _, _, nx, kn) in enumerate(_sub_tiles(d_ref.at[b], None, dn_ref.at[b], None, keep, n_sub)):
                _fill_wrap_next(sub, nx, wrap_ref, FFN_K - 1, kn)
                _conv_bwd_data(sub, wrap_ref, w_ref.at[b], _rows(out_ref, s, TM_EW), seg, FFN_K)

        p = None
        for b in range(pair):
            @pl.when(m < half_pairs)
            def _(b=b):
                conv_t(g_ref, gn_ref, b)

            @pl.when(m >= half_pairs)
            def _(b=b):
                conv_t(v_ref, vn_ref, b)

            dh = out_ref[...]
            dhb = dh.astype(BF16)
            dh_ref[b] = dhb
            _acc_rows(db_ref.at[pair * m + b], _colsum(dh), i == 0)
            part = jnp.dot(dhb, up_ref[b], preferred_element_type=F32)
            p = part if p is None else p + part

        @pl.when(m == 0)
        def _():
            acc_ref[...] = p

        @pl.when(m > 0)
        def _():
            acc_ref[...] += p

        @pl.when(m == n_pairs - 1)
        def _():
            _ln_bwd_rows(acc_ref[...] + ALPHA * res_ref[...], xh_ref, rs_ref, gam_ref, dr_ref, dgam_ref, dbeta_ref,
                         dsum_ref, i == 0)

    def src(gate):
        def blk(m):
            return jnp.minimum(m, half_pairs - 1) if gate else jnp.maximum(m - half_pairs, 0)
        tile = pl.BlockSpec((pair, tm, fb), lambda i, m: (blk(m), i, 0))
        nxt = pl.BlockSpec((pair, WRAP_ROWS, fb),
                           lambda i, m: (blk(m), jnp.minimum((i + 1) * per, n_steps * per - 1), 0))
        return [tile, nxt]

    row = pl.BlockSpec((tm, d), lambda i, m: (i, 0))
    ln_ins, ln_outs = _ln_bwd_specs(tm, d, l, lambda i, m: (i, 0))
    tmp = pltpu.VMEM((tm, fb), F32)
    halo = pltpu.VMEM((WRAP_ROWS, fb), F32)
    return _call(body, name, (n_steps, n_pairs),
                 src(True) + src(False) +
                 [pl.BlockSpec((None, pair, FFN_K, fb), lambda i, m: (l, m, 0, 0)),
                  pl.BlockSpec((pair, fb, d), lambda i, m: (m, 0, 0)), row] + ln_ins,
                 [pl.BlockSpec((pair, tm, fb), lambda i, m: (m, i, 0)),
                  pl.BlockSpec((nb, 1, fb), lambda i, m: (0, 0, 0))] + ln_outs,
                 [_sds((nb, t, fb), BF16), _sds((nb, 1, fb), F32), _sds((t, d), F32)] + [_sds((1, d), F32)] * 3,
                 [halo, tmp, pltpu.VMEM((tm, d), F32)],
                 deps=deps)(dcg, dcg, dcv, dcv, wdw, w_up, res, xhat, rstd, g3)


def _gelu_parts(h):
    cdf = 0.5 * (1.0 + lax.erf(h * INV_SQRT2))
    return h * cdf, cdf


def _seg_axis(a, axis, fn):
    return jnp.moveaxis(fn(jnp.moveaxis(a, axis, 0), TM_EW), 0, axis)


def _sgu_operands(w_s, b_s):
    nl = w_s.shape[0]
    n_sub = TM_EW // CHUNK
    tril = jnp.tril(jnp.ones((CHUNK, CHUNK), dtype=bool))
    w_causal = jnp.where(tril, w_s, 0.0)
    w_tile = (jnp.eye(n_sub, dtype=F32)[None, None, :, None, :, None] * w_causal[:, :, None, :, None, :]).reshape(
        nl, GROUPS, TM_EW, TM_EW)
    w_tile = _seg_axis(_seg_axis(w_tile, 2, _to_segments), 3, _to_segments).astype(BF16)
    bs_tile = jnp.broadcast_to(b_s[:, :, None, :, None], (nl, GROUPS, n_sub, CHUNK, CHUNK)).reshape(
        nl, GROUPS, TM_EW, CHUNK)
    return w_tile, _seg_axis(bs_tile, 2, _to_segments)


def _sgu_param_grads(dwt, dbt):
    n_sub = TM_EW // CHUNK
    tril = jnp.tril(jnp.ones((CHUNK, CHUNK), dtype=bool))
    dwt = _seg_axis(_seg_axis(dwt, 1, _from_segments), 2, _from_segments).reshape(GROUPS, n_sub, CHUNK, n_sub, CHUNK)
    dw = sum(dwt[:, a, :, a, :] for a in range(n_sub))
    db = _seg_axis(dbt, 1, _from_segments).reshape(GROUPS, n_sub, CHUNK).sum(axis=1)
    return jnp.where(tril, dw, 0.0), db


def _sgu(name, h, g3, b3, wt, bst, l):
    t, c2 = h.shape
    c = c2 // 2
    tm = TM_EW

    def body(h_ref, g_ref, b_ref, wt_ref, bs_ref, o_ref):
        z, _ = _gelu_parts(h_ref[...])
        u = z[:, :c]
        xhat, _ = _ln_stats(z[:, c:])
        vnb = (xhat * g_ref[...] + b_ref[...]).astype(BF16)
        for gi in range(GROUPS):
            cs = slice(gi * CHUNK, (gi + 1) * CHUNK)
            sp = jnp.dot(wt_ref[gi], vnb[:, cs], preferred_element_type=F32) + bs_ref[gi]
            o_ref[:, cs] = (u[:, cs] * sp).astype(BF16)

    return _call(body, name, (t // tm,),
                 [_row_spec(tm, c2), _param_spec(l, c), _param_spec(l, c),
                  pl.BlockSpec((None, GROUPS, tm, tm), lambda i: (l, 0, 0, 0)),
                  pl.BlockSpec((None, GROUPS, tm, CHUNK), lambda i: (l, 0, 0, 0))],
                 _row_spec(tm, c), _sds((t, c), BF16))(h, g3, b3, wt, bst)


def _sgu_bwd(name, dq, h, g3, b3, wt, bst, l, deps=()):
    t, c2 = h.shape
    c = c2 // 2
    tm = TM_EW
    n_tiles = t // tm

    def body(dq_ref, h_ref, g_ref, b_ref, wt_ref, bs_ref,
             dh_ref, dbin_ref, dw_ref, dbs_ref, dg_ref, db_ref, du_ref, dvn_ref, bsum_ref):
        i = pl.program_id(0)
        first = i == 0
        hv = h_ref[...]
        z, cdf = _gelu_parts(hv)
        u = z[:, :c]
        xhat, rstd = _ln_stats(z[:, c:])
        g = g_ref[...]
        vnb = (xhat * g + b_ref[...]).astype(BF16)

        @pl.when(first)
        def _():
            dw_ref[...] = jnp.zeros_like(dw_ref)
            bsum_ref[...] = jnp.zeros_like(bsum_ref)

        for gi in range(GROUPS):
            cs = slice(gi * CHUNK, (gi + 1) * CHUNK)
            vb = vnb[:, cs]
            w = wt_ref[gi]
            sp = jnp.dot(w, vb, preferred_element_type=F32) + bs_ref[gi]
            dqb = dq_ref[:, cs]
            du_ref[:, cs] = dqb * sp
            dsp = dqb * u[:, cs]
            bsum_ref[gi] += dsp
            dspb = dsp.astype(BF16)
            dw_ref[gi] += lax.dot_general(dspb, vb, (_DIMS["nt"], ((), ())), preferred_element_type=F32)
            dvn_ref[:, cs] = lax.dot_general(w, dspb, (_DIMS["tn"], ((), ())), preferred_element_type=F32)

        dvn = dvn_ref[...]
        dv = _ln_backward(dvn * g, xhat, rstd)
        pdf = jnp.exp(-0.5 * hv * hv) * INV_SQRT2PI
        dgelu = cdf + hv * pdf
        dhu = du_ref[...] * dgelu[:, :c]
        dhv = dv * dgelu[:, c:]
        dh_ref[:, :c] = dhu.astype(BF16)
        dh_ref[:, c:] = dhv.astype(BF16)
        _acc_rows(dbin_ref.at[:, :c], _colsum(dhu), first)
        _acc_rows(dbin_ref.at[:, c:], _colsum(dhv), first)
        _acc_rows(dg_ref, _colsum(dvn * xhat), first)
        _acc_rows(db_ref, _colsum(dvn), first)

        @pl.when(i == n_tiles - 1)
        def _():
            dbs_ref[...] = jnp.sum(bsum_ref[...], axis=-1)

    vec = pl.BlockSpec((1, c), lambda i: (0, 0))
    return _call(body, name, (n_tiles,),
                 [_row_spec(tm, c), _row_spec(tm, c2), _param_spec(l, c), _param_spec(l, c),
                  pl.BlockSpec((None, GROUPS, tm, tm), lambda i: (l, 0, 0, 0)),
                  pl.BlockSpec((None, GROUPS, tm, CHUNK), lambda i: (l, 0, 0, 0))],
                 [_row_spec(tm, c2), pl.BlockSpec((1, c2), lambda i: (0, 0)),
                  pl.BlockSpec((GROUPS, tm, tm), lambda i: (0, 0, 0)),
                  pl.BlockSpec((GROUPS, tm), lambda i: (0, 0)), vec, vec],
                 [_sds((t, c2), BF16), _sds((1, c2), F32), _sds((GROUPS, tm, tm), F32),
                  _sds((GROUPS, tm), F32), _sds((1, c), F32), _sds((1, c), F32)],
                 [pltpu.VMEM((tm, c), F32), pltpu.VMEM((tm, c), F32), pltpu.VMEM((GROUPS, tm, CHUNK), F32)],
                 deps=deps)(dq, h, g3, b3, wt, bst)


def _loss(name, y, target):
    t, d = y.shape
    tm = min(TM_ROW, t)
    n_tiles = t // tm

    def body(y_ref, t_ref, l_ref, dy_ref, acc_ref):
        i = pl.program_id(0)
        diff = y_ref[...] - t_ref[...]
        dy_ref[...] = diff * (1.0 / d)
        _acc_rows(acc_ref, _colsum(diff * diff), i == 0)

        @pl.when(i == n_tiles - 1)
        def _():
            l_ref[...] = jnp.broadcast_to(jnp.sum(acc_ref[...], axis=-1, keepdims=True) * (0.5 / d), (1, LANES))

    return _call(body, name, (n_tiles,), [_row_spec(tm, d), _row_spec(tm, d)],
                 [pl.BlockSpec((1, LANES), lambda i: (0, 0)), _row_spec(tm, d)],
                 [_sds((1, LANES), F32), _sds((t, d), F32)], [pltpu.VMEM((1, d), F32)])(y, target)


def _adamw(g, w, m, v):
    m2 = ADAM_B1 * m + (1.0 - ADAM_B1) * g
    v2 = ADAM_B2 * v + (1.0 - ADAM_B2) * (g * g)
    m_hat = m2 / (1.0 - ADAM_B1 ** ADAM_STEP)
    v_hat = v2 / (1.0 - ADAM_B2 ** ADAM_STEP)
    delta = -ADAM_LR * (m_hat / (jnp.sqrt(v_hat) + ADAM_EPS) + ADAM_WD * w)
    return delta, m2, v2


ROW_TILE_CAP = 512


def _row_tile(rows, cap=ROW_TILE_CAP):
    if rows <= cap:
        return rows
    for tr in range(cap, 15, -16):
        if rows % tr == 0:
            return tr
    return rows


def _sum8_adamw(name, dev, lands, parts, w, m, v):
    nl = len(lands)
    _, r, c = lands[0].shape
    tr = _row_tile(r, cap=256)

    def body(dev_ref, *refs):
        land, own = refs[:nl], refs[nl:2 * nl]
        w_ref, m_ref, v_ref, g_ref, d_ref, m2_ref, v2_ref = refs[2 * nl:]
        layer, me = pl.program_id(0), dev_ref[0]
        for l in range(nl):
            @pl.when(layer == l)
            def _(l=l):
                g = None
                for s in range(N_DEV):
                    part = jnp.where(me == s, own[l][...], land[l][s]).astype(F32)
                    g = part if g is None else g + part
                delta, m2, v2 = _adamw(g, w_ref[...], m_ref[...], v_ref[...])
                g_ref[...] = g
                d_ref[...] = delta
                m2_ref[...] = m2
                v2_ref[...] = v2

    def rows_of(l, a, i):
        return jnp.where(a == l, i, 0)

    spec = pl.BlockSpec((None, tr, c), lambda a, i, dev_ref: (a, i, 0))
    in_specs = [pl.BlockSpec((N_DEV, tr, c), lambda a, i, dev_ref, l=l: (0, rows_of(l, a, i), 0)) for l in range(nl)]
    in_specs += [pl.BlockSpec((None, tr, c), lambda a, i, dev_ref, l=l: (dev_ref[0], rows_of(l, a, i), 0))
                 for l in range(nl)]
    grid_spec = pltpu.PrefetchScalarGridSpec(
        num_scalar_prefetch=1, grid=(nl, r // tr), in_specs=in_specs + [spec] * 3, out_specs=[spec] * 4)
    return pl.pallas_call(
        body, name=name, grid_spec=grid_spec, out_shape=[_sds(w.shape, F32)] * 4,
        compiler_params=pltpu.CompilerParams(vmem_limit_bytes=VMEM_LIMIT))(dev, *lands, *parts, w, m, v)


def _sum8(name, parts):
    _, r, c = parts.shape
    tr = _row_tile(r)

    def body(p_ref, o_ref):
        acc = p_ref[0]
        for s in range(1, N_DEV):
            acc = acc + p_ref[s]
        o_ref[...] = acc

    return _call(body, name, (r // tr,), [pl.BlockSpec((N_DEV, tr, c), lambda i: (0, i, 0))],
                 pl.BlockSpec((tr, c), lambda i: (i, 0)), _sds((r, c), F32))(parts)


def _adamw_small(name, gs, ws, ms, vs):
    n = len(gs)

    def body(*refs):
        g, w, m, v = (refs[k * n:(k + 1) * n] for k in range(4))
        d_out, m_out, v_out = (refs[(4 + k) * n:(5 + k) * n] for k in range(3))
        for k in range(n):
            d_out[k][...], m_out[k][...], v_out[k][...] = _adamw(g[k][...], w[k][...], m[k][...], v[k][...])

    vmem = pl.BlockSpec(memory_space=pltpu.VMEM)
    outs = _call(body, name, (), [vmem] * (4 * n), [vmem] * (3 * n), [_sds(w.shape, F32) for w in ws] * 3)(
        *gs, *ws, *ms, *vs)
    return outs[:n], outs[n:2 * n], outs[2 * n:]


def _pack(arrs, row_multiple=SUBLANES):
    pieces, rows = [], 0
    for a in arrs:
        piece = a.reshape(-1, LANES)
        piece = jnp.pad(piece, ((0, (-piece.shape[0]) % SUBLANES), (0, 0)))
        pieces.append(piece)
        rows += piece.shape[0]
    if rows % row_multiple:
        pieces.append(jnp.zeros(((-rows) % row_multiple, LANES), pieces[0].dtype))
    return jnp.concatenate(pieces, axis=0)


def _unpack(buf, shapes, lead=0):
    out, pos = [], 0
    for shp in shapes:
        rows = math.prod(shp) // LANES
        piece = lax.slice_in_dim(buf, pos, pos + rows, axis=lead)
        out.append(piece.reshape(buf.shape[:lead] + tuple(shp)))
        pos += rows + (-rows) % SUBLANES
    return out


REPLICATED = ["conv_b_in", "conv_b_dw", "conv_ln_g", "conv_ln_b", "conv_b_out", "gmlp_w_s", "gmlp_b_s",
              "ffn_b_up", "ffn_b_dw", "ffn_b_down", "norm1_g", "norm1_b", "norm2_g", "norm2_b"]
SMALL_SHARDED = ["conv_w_dw", "gmlp_b_in", "gmlp_ln_g", "gmlp_ln_b", "gmlp_b_out", "ffn_w_dw"]
BIG = ["conv_w_in", "conv_w_out", "gmlp_w_in", "gmlp_w_out", "ffn_w_up", "ffn_w_down"]
WEIGHTS = ["conv_w_in", "conv_b_in", "conv_w_dw", "conv_b_dw", "conv_ln_g", "conv_ln_b", "conv_w_out", "conv_b_out",
           "gmlp_w_in", "gmlp_b_in", "gmlp_ln_g", "gmlp_ln_b", "gmlp_w_s", "gmlp_b_s", "gmlp_w_out", "gmlp_b_out",
           "ffn_w_up", "ffn_b_up", "ffn_w_dw", "ffn_b_dw", "ffn_w_down", "ffn_b_down",
           "norm1_g", "norm1_b", "norm2_g", "norm2_b"]


def _from_shards(g, lead_shape):
    nd = len(lead_shape)
    perm = tuple(range(1, nd + 1)) + (0, nd + 1)
    return g.transpose(perm).reshape(tuple(lead_shape) + (-1,))


def _to_shards(full, width):
    lead = full.shape[:-1]
    nd = len(lead)
    parts = full.reshape(lead + (N_DEV, width))
    return parts.transpose((nd,) + tuple(range(nd)) + (nd + 1,))


def _step(p):
    x_in, target_in = p["x"], p["loss_target"]
    bsz, seq, d = x_in.shape
    t = bsz * seq
    assert seq % TM_EW == 0 and TM_EW % CHUNK == 0 and TM_EW // SUBLANES >= CONV_K - 1
    x0 = _to_segments(x_in.reshape(t, d), TM_EW)
    target = _to_segments(target_in.reshape(t, d), TM_EW)
    n_conv, n_gmlp = p["conv_w_in"].shape[0], p["gmlp_w_in"].shape[0]
    fb = p["ffn_w_up"].shape[-1]
    nblk = N_DEV
    half = nblk // 2
    cw = p["conv_w_in"].shape[-1]
    tm = min(TM_MM, t)
    nt = t // tm
    tk = min(TK_DW, t)
    ntk = t // tk
    dev = 4 * lax.axis_index("x") + 2 * lax.axis_index("y") + lax.axis_index("c")

    small_shapes = [p[n].shape for n in SMALL_SHARDED]
    small_src = _pack([p[n] for n in SMALL_SHARDED])[None]
    small_all = _all_gather("gather_small_weights", [small_src])[0][0]
    sm = _unpack(small_all, small_shapes, lead=1)
    w_src = []
    for i in range(DEPTH):
        mix = "conv" if i % 2 == 0 else "gmlp"
        w_src += [p[mix + "_w_in"][i // 2].astype(BF16), p[mix + "_w_out"][i // 2].astype(BF16),
                  p["ffn_w_up"][i].T.astype(BF16), p["ffn_w_down"][i].astype(BF16)]
    send_sems, recv_sems, w_land, _ = _gather_start(
        "weights_gather_start", _place_own("weights_place_own", w_src, deps=[small_all]))
    W_IN, W_OUT, W_UP, W_DOWN = range(4)

    def wait_weight(i, k, after):
        return _gather_wait(f"l{i}_weights_wait{k}", w_land[4 * i + k], send_sems, recv_sems, 4 * i + k, after)
    conv_w_dw = _from_shards(sm[0], sm[0].shape[1:-1])
    gmlp_b_in = _from_shards(sm[1], sm[1].shape[1:-1])
    gmlp_ln_g = _from_shards(sm[2], sm[2].shape[1:-1])
    gmlp_ln_b = _from_shards(sm[3], sm[3].shape[1:-1])
    gmlp_b_out = _from_shards(sm[4], sm[4].shape[1:-1])
    ffn_w_dw = sm[5].transpose(1, 0, 2, 3)

    def rows3(a):
        return a.reshape(a.shape[0], 1, a.shape[-1])

    conv_b_in4 = p["conv_b_in"].reshape(n_conv, N_DEV, 1, cw)
    gmlp_b_in4 = gmlp_b_in.reshape(n_gmlp, N_DEV, 1, cw)
    ffn_b_up4 = p["ffn_b_up"].reshape(DEPTH, nblk, 1, fb)
    ffn_b_dw4 = p["ffn_b_dw"].reshape(DEPTH, nblk, 1, fb)
    conv_b_dw3, conv_ln_g3, conv_ln_b3 = rows3(p["conv_b_dw"]), rows3(p["conv_ln_g"]), rows3(p["conv_ln_b"])
    conv_b_out3, gmlp_b_out3, ffn_b_down3 = rows3(p["conv_b_out"]), rows3(gmlp_b_out), rows3(p["ffn_b_down"])
    gmlp_ln_g3, gmlp_ln_b3 = rows3(gmlp_ln_g), rows3(gmlp_ln_b)
    n1g3, n1b3, n2g3, n2b3 = rows3(p["norm1_g"]), rows3(p["norm1_b"]), rows3(p["norm2_g"]), rows3(p["norm2_b"])
    w_tile, bs_tile = _sgu_operands(p["gmlp_w_s"], p["gmlp_b_s"])

    def mm_in(name, xa, wg, l, bias4, glu=False):
        tmi = min(TM_LN, t)
        c_half = half * cw

        def body(a_ref, b_ref, bias_ref, h_ref, *u_ref):
            xb = a_ref[...].astype(BF16)
            for n in range(N_DEV):
                h_ref[:, n * cw:(n + 1) * cw] = jnp.dot(xb, b_ref[n], preferred_element_type=F32) + bias_ref[n]
            if glu:
                u_ref[0][...] = h_ref[:, :c_half] * _sigmoid(h_ref[:, c_half:])

        outs = _call(body, name, (t // tmi,),
                     [pl.BlockSpec((tmi, d), lambda i: (i, 0)), pl.BlockSpec((N_DEV, d, cw), lambda i: (0, 0, 0)),
                      pl.BlockSpec((None, N_DEV, 1, cw), lambda i: (l, 0, 0, 0))],
                     [pl.BlockSpec((tmi, N_DEV * cw), lambda i: (i, 0))]
                     + ([pl.BlockSpec((tmi, c_half), lambda i: (i, 0))] if glu else []),
                     [_sds((t, N_DEV * cw), F32)] + ([_sds((t, c_half), F32)] if glu else []))(xa, wg, bias4)
        return outs if glu else outs[0]

    def mm_out_dx(name, dy, w, deps=()):
        return _matmul(name, dy, w, "nt", grid=(nt,),
                       a_spec=pl.BlockSpec((tm, d), lambda i: (i, 0)),
                       b_spec=pl.BlockSpec((d, d), lambda i: (0, 0)),
                       o_spec=pl.BlockSpec((tm, d), lambda i: (i, 0)), o_shape=(t, d), o_dtype=F32, deps=deps)

    def mm_out_dw(name, sa, dy):
        return _matmul(name, sa, dy, "tn", grid=(nt,), k_axis=0, nk=nt, acc_shape=(d, d),
                       a_spec=pl.BlockSpec((tm, d), lambda k: (k, 0)),
                       b_spec=pl.BlockSpec((tm, d), lambda k: (k, 0)),
                       o_spec=pl.BlockSpec((d, d), lambda k: (0, 0)), o_shape=(d, d), o_dtype=BF16)

    def mm_in_dx(name, dh, wg, res, norm=None):
        tmx = min(TM_LN, t)

        def body(a_ref, b_ref, res_ref, *refs):
            y = ALPHA * res_ref[...]
            for n in range(N_DEV):
                y = y + lax.dot_general(a_ref[:, n * cw:(n + 1) * cw], b_ref[n], (_DIMS["nt"], ((), ())),
                                        preferred_element_type=F32)
            if norm is None:
                refs[0][...] = y
            else:
                _ln_bwd_rows(y, *refs, pl.program_id(0) == 0)

        row = pl.BlockSpec((tmx, d), lambda i: (i, 0))
        ins = [pl.BlockSpec((tmx, N_DEV * cw), lambda i: (i, 0)), pl.BlockSpec((N_DEV, d, cw), lambda i: (0, 0, 0)), row]
        if norm is None:
            return _call(body, name, (t // tmx,), ins, row, _sds((t, d), F32))(dh, wg, res)
        ln_ins, ln_outs = _ln_bwd_specs(tmx, d, norm[3], lambda i: (i, 0))
        return _call(body, name, (t // tmx,), ins + ln_ins, ln_outs,
                     [_sds((t, d), F32)] + [_sds((1, d), F32)] * 3)(dh, wg, res, *norm[:3])

    def mm_in_dw(name, xa, dh):
        def body(a_ref, b_ref, o_ref, acc_ref):
            k = pl.program_id(1)
            p = lax.dot_general(a_ref[...].astype(BF16), b_ref[...], (_DIMS["tn"], ((), ())),
                                preferred_element_type=F32)
            _acc_rows(acc_ref, p, k == 0)

            @pl.when(k == nt - 1)
            def _():
                for n in range(half):
                    o_ref[n] = acc_ref[:, n * cw:(n + 1) * cw].astype(BF16)

        return _call(body, name, (2, nt),
                     [pl.BlockSpec((tm, d), lambda c, k: (k, 0)), pl.BlockSpec((tm, half * cw), lambda c, k: (k, c))],
                     pl.BlockSpec((half, d, cw), lambda c, k: (c, 0, 0)), _sds((N_DEV, d, cw), BF16),
                     [pltpu.VMEM((d, half * cw), F32)])(xa, dh)

    def mm_down_dw(name, a, dy, deps=()):
        return _matmul(name, a, dy, "tn", grid=(half, ntk), k_axis=1, nk=ntk, acc_shape=(fb, d),
                       a_spec=pl.BlockSpec((None, tk, fb), lambda n, k: (n, k, 0)),
                       b_spec=pl.BlockSpec((tk, d), lambda n, k: (k, 0)),
                       o_spec=pl.BlockSpec((None, fb, d), lambda n, k: (n, 0, 0)),
                       o_shape=(half, fb, d), o_dtype=BF16, deps=deps)

    def mm_up_dw(name, xa, dh):
        return _matmul(name, dh, xa, "tn", grid=(nblk, ntk), k_axis=1, nk=ntk, acc_shape=(fb, d),
                       a_spec=pl.BlockSpec((None, tk, fb), lambda n, k: (n, k, 0)),
                       b_spec=pl.BlockSpec((tk, d), lambda n, k: (k, 0)),
                       o_spec=pl.BlockSpec((None, fb, d), lambda n, k: (n, 0, 0)),
                       o_shape=(nblk, fb, d), o_dtype=BF16)

    saved = []
    xcur = x0
    for i in range(DEPTH):
        j = i // 2
        s = {"x": xcur}
        s["w_in"] = wait_weight(i, W_IN, xcur if i else target)
        if i % 2 == 0:
            s["h"], s["u"] = mm_in(f"l{i}_conv_in_glu", xcur, s["w_in"], j, conv_b_in4, glu=True)
            s["c"] = _dwconv31(f"l{i}_dwconv", s["u"], conv_w_dw, conv_b_dw3, j, seq)
            s["s"] = _ln_silu(f"l{i}_ln_silu", s["c"], conv_ln_g3, conv_ln_b3, j)
            b_out3 = conv_b_out3
        else:
            s["h"] = mm_in(f"l{i}_gmlp_in", xcur, s["w_in"], j, gmlp_b_in4)
            s["s"] = _sgu(f"l{i}_sgu", s["h"], gmlp_ln_g3, gmlp_ln_b3, w_tile, bs_tile, j)
            b_out3 = gmlp_b_out3
        s["w_out"] = wait_weight(i, W_OUT, s["s"]).reshape(d, d)
        s["x1"], s["xhat1"], s["rstd1"] = _matmul_ln(
            f"l{i}_mixer_out_norm1", s["s"], s["w_out"], xcur, b_out3, n1g3, n1b3, j, i)
        s["w_up"] = wait_weight(i, W_UP, s["x1"])
        s["hg"], s["hv"], s["a"] = _ffn_up_act(f"l{i}_ffn_up_act", s["x1"], s["w_up"], ffn_b_up4, ffn_w_dw, ffn_b_dw4,
                                               i, seq)
        s["w_down"] = wait_weight(i, W_DOWN, s["a"]).reshape(half, fb, d)
        xcur, s["xhat2"], s["rstd2"] = _matmul_ln(
            f"l{i}_ffn_down_norm2", s["a"], s["w_down"], s["x1"], ffn_b_down3, n2g3, n2b3, i, i)
        saved.append(s)

    loss_row, dx = _loss("loss", xcur, target)

    started = {n: [None] * p[n].shape[0] for n in BIG}
    tokens = []

    def send_grads(name, items):
        done, token = _scatter_start(name, [g for _, _, g in items])
        for (n, l, _), st in zip(items, done):
            started[n][l] = st
        tokens.append(token)

    def take_tokens():
        out = list(tokens)
        tokens.clear()
        return out

    gl = {n: [None] * p[n].shape[0] for n in REPLICATED + SMALL_SHARDED}
    dr2, gl["norm2_g"][DEPTH - 1], gl["norm2_b"][DEPTH - 1], gl["ffn_b_down"][DEPTH - 1] = _ln_res_bwd(
        f"l{DEPTH - 1}_norm2_bwd", dx, saved[-1]["xhat2"], saved[-1]["rstd2"], n2g3, DEPTH - 1)
    for i in reversed(range(DEPTH)):
        j = i // 2
        s = saved[i]
        mix = "conv" if i % 2 == 0 else "gmlp"
        g_down = mm_down_dw(f"l{i}_ffn_down_dw", s["a"], dr2, deps=take_tokens()).reshape(N_DEV, -1, d)
        send_grads(f"l{i}_ffn_down_grad_scatter_start", [("ffn_w_down", i, g_down)])
        dcg, dcv, dbg, dbv, dwg, dwv = _ffn_act_bwd(f"l{i}_ffn_act_bwd", dr2, s["w_down"], s["hg"], s["hv"],
                                                    ffn_w_dw, ffn_b_dw4, i, seq, deps=take_tokens())
        gl["ffn_b_dw"][i] = jnp.concatenate([dbg, dbv], axis=0).reshape(1, nblk * fb)
        gl["ffn_w_dw"][i] = jnp.concatenate([dwg[:, :FFN_K], dwv[:, :FFN_K]], axis=0)
        dh, dbu, dr1, gl["norm1_g"][i], gl["norm1_b"][i], gl[mix + "_b_out"][j] = _ffn_conv_t_dx(
            f"l{i}_ffn_conv_t_dx", dcg, dcv, ffn_w_dw, s["w_up"], dr2, s["xhat1"], s["rstd1"], n1g3, i, seq,
            deps=take_tokens())
        gl["ffn_b_up"][i] = dbu.reshape(1, nblk * fb)
        send_grads(f"l{i}_ffn_up_grad_scatter_start", [("ffn_w_up", i, mm_up_dw(f"l{i}_ffn_up_dw", s["x1"], dh))])
        ds = mm_out_dx(f"l{i}_{mix}_out_dx", dr1, s["w_out"], deps=take_tokens())
        g_out = mm_out_dw(f"l{i}_{mix}_out_dw", s["s"], dr1).reshape(N_DEV, -1, d)
        if i % 2 == 0:
            dc, gl["conv_ln_g"][j], gl["conv_ln_b"][j], gl["conv_b_dw"][j] = _ln_silu_bwd(
                f"l{i}_ln_silu_bwd", ds, s["c"], conv_ln_g3, conv_ln_b3, j)
            du, dwdw = _dwconv31_bwd(f"l{i}_dwconv_bwd", dc, s["u"], conv_w_dw, j, seq)
            gl["conv_w_dw"][j] = dwdw[:CONV_K]
            dh, gl["conv_b_in"][j] = _glu_bwd(f"l{i}_glu_bwd", du, s["h"])
        else:
            dh, gl["gmlp_b_in"][j], dwt, dbt, gl["gmlp_ln_g"][j], gl["gmlp_ln_b"][j] = _sgu_bwd(
                f"l{i}_sgu_bwd", ds, s["h"], gmlp_ln_g3, gmlp_ln_b3, w_tile, bs_tile, j)
            gl["gmlp_w_s"][j], gl["gmlp_b_s"][j] = _sgu_param_grads(dwt, dbt)
            if i == 1:
                ws_local = jnp.stack(gl["gmlp_w_s"]).reshape(-1, LANES)
                ws_send, ws_recv, ws_land, ws_token = _gather_start(
                    "w_s_grads_gather_start", _place_own("w_s_grads_place_own", [ws_local]))
                tokens.append(ws_token)
        if i > 0:
            prev = saved[i - 1]
            dr2, gl["norm2_g"][i - 1], gl["norm2_b"][i - 1], gl["ffn_b_down"][i - 1] = mm_in_dx(
                f"l{i}_{mix}_in_dx_norm2_bwd", dh, s["w_in"], dr1, (prev["xhat2"], prev["rstd2"], n2g3, i - 1))
        else:
            dx = mm_in_dx(f"l{i}_{mix}_in_dx", dh, s["w_in"], dr1)
        send_grads(f"l{i}_mixer_grads_scatter_start",
                   [(mix + "_w_out", j, g_out), (mix + "_w_in", j, mm_in_dw(f"l{i}_{mix}_in_dw", s["x"], dh))])
    grad_x = _from_segments(dx, TM_EW).reshape(bsz, seq, d)

    late = [n for n in REPLICATED if n != "gmlp_w_s"]
    full_small = {n: jnp.stack(gl[n]).reshape(p[n].shape) for n in late}
    shard_small = {}
    for n in SMALL_SHARDED:
        if n == "ffn_w_dw":
            shard_small[n] = jnp.stack(gl[n]).transpose(1, 0, 2, 3)
        else:
            width = p[n].shape[-1]
            lead = p[n].shape[:-1]
            shard_small[n] = _to_shards(jnp.stack(gl[n]).reshape(lead + (N_DEV * width,)), width)
    flat_shapes = [(1, LANES)] + [p[n].shape for n in late] + [(N_DEV,) + p[n].shape for n in SMALL_SHARDED]
    flat_local = _pack([loss_row] + [full_small[n] for n in late] + [shard_small[n] for n in SMALL_SHARDED],
                       row_multiple=ROW_TILE_CAP)

    small_send, small_recv, small_land, small_token = _gather_start(
        "small_grads_gather_start", _place_own("small_grads_place_own", [flat_local]))

    grads, delta, new_m, new_v = {}, {}, {}, {}
    dev1 = jnp.reshape(dev, (1,)).astype(jnp.int32)
    order = ["ffn_w_down", "ffn_w_up", "gmlp_w_out", "gmlp_w_in", "conv_w_out", "conv_w_in"]
    after = small_token
    for n in order:
        parts_done, lands_done = _scatter_wait(f"grads_{n}_scatter_wait", started[n], after)
        state = [p[n], p["m_" + n], p["v_" + n]]
        if n == "ffn_w_up":
            state = [a.transpose(0, 2, 1) for a in state]
        outs = _sum8_adamw(f"adamw_{n}", dev1, lands_done, parts_done, *state)
        after = outs[-1]
        if n == "ffn_w_up":
            outs = [a.transpose(0, 2, 1) for a in outs]
        grads[n], delta[n], new_m[n], new_v[n] = outs

    ws_parts = _gather_wait("w_s_grads_gather_wait", ws_land[0], ws_send, ws_recv, 0, after)
    ws_sum = _sum8("sum_w_s_grads", ws_parts)
    grads["gmlp_w_s"] = ws_sum.reshape(p["gmlp_w_s"].shape)
    small_parts = _gather_wait("small_grads_gather_wait", small_land[0], small_send, small_recv, 0, ws_sum)
    summed = _unpack(_sum8("sum_small_grads", small_parts), flat_shapes)
    loss = summed[0][0, 0]
    grads.update(zip(late, summed[1:1 + len(late)]))
    for n, g in zip(SMALL_SHARDED, summed[1 + len(late):]):
        grads[n] = lax.dynamic_index_in_dim(g, dev, axis=0, keepdims=False)
    small = REPLICATED + SMALL_SHARDED
    d_s, m_s, v_s = _adamw_small("adamw_small", [grads[n] for n in small], [p[n] for n in small],
                                 [p["m_" + n] for n in small], [p["v_" + n] for n in small])
    for n, dd, mm, vv in zip(small, d_s, m_s, v_s):
        delta[n], new_m[n], new_v[n] = dd, mm, vv

    return (loss, grad_x, *[grads[n] for n in WEIGHTS], *[delta[n] for n in WEIGHTS],
            *[new_m[n] for n in WEIGHTS], *[new_v[n] for n in WEIGHTS])


def kernel(x, conv_w_in, conv_b_in, conv_w_dw, conv_b_dw, conv_ln_g, conv_ln_b, conv_w_out, conv_b_out, gmlp_w_in, gmlp_b_in, gmlp_ln_g, gmlp_ln_b, gmlp_w_s, gmlp_b_s, gmlp_w_out, gmlp_b_out, ffn_w_up, ffn_b_up, ffn_w_dw, ffn_b_dw, ffn_w_down, ffn_b_down, norm1_g, norm1_b, norm2_g, norm2_b, loss_target, m_conv_w_in, m_conv_b_in, m_conv_w_dw, m_conv_b_dw, m_conv_ln_g, m_conv_ln_b, m_conv_w_out, m_conv_b_out, m_gmlp_w_in, m_gmlp_b_in, m_gmlp_ln_g, m_gmlp_ln_b, m_gmlp_w_s, m_gmlp_b_s, m_gmlp_w_out, m_gmlp_b_out, m_ffn_w_up, m_ffn_b_up, m_ffn_w_dw, m_ffn_b_dw, m_ffn_w_down, m_ffn_b_down, m_norm1_g, m_norm1_b, m_norm2_g, m_norm2_b, v_conv_w_in, v_conv_b_in, v_conv_w_dw, v_conv_b_dw, v_conv_ln_g, v_conv_ln_b, v_conv_w_out, v_conv_b_out, v_gmlp_w_in, v_gmlp_b_in, v_gmlp_ln_g, v_gmlp_ln_b, v_gmlp_w_s, v_gmlp_b_s, v_gmlp_w_out, v_gmlp_b_out, v_ffn_w_up, v_ffn_b_up, v_ffn_w_dw, v_ffn_b_dw, v_ffn_w_down, v_ffn_b_down, v_norm1_g, v_norm1_b, v_norm2_g, v_norm2_b):
    return _step(dict(locals()))
```

```python
import math

import jax
import jax.numpy as jnp
from jax import lax
from jax.experimental import pallas as pl
from jax.experimental.pallas import tpu as pltpu

F32 = jnp.float32
BF16 = jnp.bfloat16
MESH = pl.DeviceIdType.MESH

N_DEV = 8
DEPTH = 4
ALPHA = (2.0 * DEPTH) ** 0.25
LN_EPS = 1e-5
CONV_K = 31
FFN_K = 3
CHUNK = 128
GROUPS = 8
ADAM_LR = 0.001
ADAM_B1 = 0.9
ADAM_B2 = 0.999
ADAM_EPS = 1e-08
ADAM_WD = 0.01
ADAM_STEP = 10
INV_SQRT2 = 1.0 / math.sqrt(2.0)
INV_SQRT2PI = 1.0 / math.sqrt(2.0 * math.pi)

LANES = 128
SUBLANES = 8
VMEM_LIMIT = 56 * 1024 * 1024
TM_MM = 1024
TK_DW = 2048
TM_EW = 256
TM_ROW = 512


def _call(body, name, grid, in_specs, out_specs, out_shape, scratch=(), aliases=None, deps=()):
    deps = list(deps)
    in_specs = list(in_specs)
    n_in = len(in_specs)
    if deps:
        inner = body

        def body(*refs):
            return inner(*refs[:n_in], *refs[n_in + len(deps):])

        in_specs = in_specs + [pl.BlockSpec(memory_space=pl.ANY)] * len(deps)
    fn = pl.pallas_call(
        body, name=name, grid=grid, in_specs=in_specs, out_specs=out_specs, out_shape=out_shape,
        scratch_shapes=list(scratch), input_output_aliases=aliases or {},
        compiler_params=pltpu.CompilerParams(vmem_limit_bytes=VMEM_LIMIT))
    return lambda *args: fn(*args, *deps)


def _sds(shape, dtype):
    return jax.ShapeDtypeStruct(tuple(shape), dtype)


def _sigmoid(x):
    return 1.0 / (1.0 + jnp.exp(-x))


def _acc_rows(ref, val, first):
    @pl.when(first)
    def _():
        ref[...] = val

    @pl.when(jnp.logical_not(first))
    def _():
        ref[...] += val


def _colsum(v):
    return jnp.sum(v, axis=0, keepdims=True)


_DIMS = {"nn": ((1,), (0,)), "nt": ((1,), (1,)), "tn": ((0,), (0,))}


def _matmul(name, a, b, mode, *, grid, a_spec, b_spec, o_spec, o_shape, o_dtype, k_axis=None, nk=1,
            acc_shape=None, bias=None, bias_spec=None, res=None, res_spec=None, res_scale=1.0, deps=()):
    dims = (_DIMS[mode], ((), ()))
    has_bias, has_res = bias is not None, res is not None

    def body(*refs):
        a_ref, b_ref = refs[0], refs[1]
        pos = 2
        bias_ref = res_ref = None
        if has_bias:
            bias_ref = refs[pos]
            pos += 1
        if has_res:
            res_ref = refs[pos]
            pos += 1
        o_ref = refs[pos]
        acc_ref = refs[pos + 1] if nk > 1 else None
        p = lax.dot_general(a_ref[...].astype(BF16), b_ref[...].astype(BF16), dims, preferred_element_type=F32)

        def finish(acc):
            if has_bias:
                acc = acc + bias_ref[...]
            if has_res:
                acc = acc + res_scale * res_ref[...]
            o_ref[...] = acc.astype(o_dtype)

        if nk == 1:
            finish(p)
        else:
            k = pl.program_id(k_axis)

            @pl.when(k == 0)
            def _():
                acc_ref[...] = p

            @pl.when(k > 0)
            def _():
                acc_ref[...] += p

            @pl.when(k == nk - 1)
            def _():
                finish(acc_ref[...])

    ins, specs = [a, b], [a_spec, b_spec]
    if has_bias:
        ins.append(bias)
        specs.append(bias_spec)
    if has_res:
        ins.append(res)
        specs.append(res_spec)
    scratch = [pltpu.VMEM(acc_shape, F32)] if nk > 1 else []
    return _call(body, name, grid, specs, o_spec, _sds(o_shape, o_dtype), scratch, deps=deps)(*ins)


TM_LN = 512


def _matmul_ln(name, a, b, x_res, bias3, g3, b3, l_bias, l_norm):
    t, d = x_res.shape
    tm = min(TM_LN, t)
    blocked = a.ndim == 3

    def body(a_ref, b_ref, x_ref, bias_ref, g_ref, be_ref, o_ref, xh_ref, rs_ref):
        if blocked:
            y = None
            for k in range(a.shape[0]):
                p = jnp.dot(a_ref[k], b_ref[k], preferred_element_type=F32)
                y = p if y is None else y + p
        else:
            y = jnp.dot(a_ref[...], b_ref[...], preferred_element_type=F32)
        xhat, rstd = _ln_stats(ALPHA * x_ref[...] + y + bias_ref[...])
        o_ref[...] = xhat * g_ref[...] + be_ref[...]
        xh_ref[...] = xhat
        rs_ref[...] = rstd

    if blocked:
        a_spec = pl.BlockSpec((a.shape[0], tm, a.shape[2]), lambda i: (0, i, 0))
        b_spec = pl.BlockSpec(b.shape, lambda i: (0, 0, 0))
    else:
        a_spec = pl.BlockSpec((tm, a.shape[1]), lambda i: (i, 0))
        b_spec = pl.BlockSpec(b.shape, lambda i: (0, 0))
    row = pl.BlockSpec((tm, d), lambda i: (i, 0))
    stat = pl.BlockSpec((tm, 1), lambda i: (i, 0))

    def vec(l):
        return pl.BlockSpec((None, 1, d), lambda i: (l, 0, 0))

    return _call(body, name, (t // tm,), [a_spec, b_spec, row, vec(l_bias), vec(l_norm), vec(l_norm)],
                 [row, row, stat], [_sds((t, d), F32), _sds((t, d), F32), _sds((t, 1), F32)])(
                     a, b, x_res, bias3, g3, b3)


def _mesh_pos():
    return lax.axis_index("x"), lax.axis_index("y"), lax.axis_index("c")


def _any_specs(n):
    return [pl.BlockSpec(memory_space=pl.ANY)] * n


def _all_gather(name, srcs):
    n = len(srcs)

    def body(*refs):
        src, out = refs[:n], refs[n:2 * n]
        send_sems, recv_sems, local_sems = refs[2 * n:]
        x, y, c = _mesh_pos()
        me, sibling = (x, y, c), (x, y, 1 - c)
        chips = [(1 - x, y), (x, 1 - y), (1 - x, 1 - y)]

        def slot(k, p):
            return out[k].at[:, 4 * p[0] + 2 * p[1] + p[2]]

        def copy(k, idx, block, to, s=None):
            return pltpu.make_async_remote_copy(
                src_ref=slot(k, block) if s is None else s, dst_ref=slot(k, block),
                send_sem=send_sems.at[k * 7 + idx], recv_sem=recv_sems.at[k * 7 + idx],
                device_id=to, device_id_type=MESH)

        local = [pltpu.make_async_copy(src[k], slot(k, me), local_sems.at[k]) for k in range(n)]
        for cp in local:
            cp.start()
        first = []
        for k in range(n):
            first.append(copy(k, 0, me, sibling, src[k]))
            for j, chip in enumerate(chips):
                first.append(copy(k, 1 + j, me, (*chip, c), src[k]))
        for cp in first:
            cp.start()
        passed = []
        for j, chip in enumerate(chips):
            for k in range(n):
                copy(k, 1 + j, (*chip, c), me).wait_recv()
                cp = copy(k, 4 + j, (*chip, c), sibling)
                cp.start()
                passed.append(cp)
        for k in range(n):
            copy(k, 0, sibling, me).wait_recv()
            for j, chip in enumerate(chips):
                copy(k, 4 + j, (*chip, 1 - c), me).wait_recv()
        for cp in first + passed:
            cp.wait_send()
        for cp in local:
            cp.wait()

    out_shape = [_sds((s.shape[0], N_DEV) + s.shape[1:], s.dtype) for s in srcs]
    return _call(body, name, (), [pl.BlockSpec(memory_space=pltpu.VMEM)] * n, _any_specs(n), out_shape,
                 [pltpu.SemaphoreType.DMA((7 * n,)), pltpu.SemaphoreType.DMA((7 * n,)),
                  pltpu.SemaphoreType.DMA((n,))])(*srcs)


HBM_SPEC = pl.BlockSpec(memory_space=pltpu.HBM)
SEM_SPEC = pl.BlockSpec(memory_space=pltpu.SEMAPHORE)
N_PEER = N_DEV - 1


def _split_call(body, name, in_specs, out_specs, out_shape, aliases):
    return pl.pallas_call(
        body, name=name, in_specs=in_specs, out_specs=out_specs, out_shape=out_shape, input_output_aliases=aliases,
        compiler_params=pltpu.CompilerParams(has_side_effects=pltpu.SideEffectType.DATAFLOW_SIDE_EFFECTING))


def _peers(x, y, c):
    return [(1 - x if q & 4 else x, 1 - y if q & 2 else y, 1 - c if q & 1 else c) for q in range(1, N_DEV)]


def _in_hbm(a):
    return pltpu.with_memory_space_constraint(a, pltpu.HBM)


def _place_own(name, srcs, deps=()):
    n = len(srcs)

    def body(*refs):
        src, out, sems = refs[:n], refs[n:2 * n], refs[2 * n]
        x, y, c = _mesh_pos()
        dev = 4 * x + 2 * y + c
        copies = [pltpu.make_async_copy(src[k], out[k].at[dev], sems.at[k]) for k in range(n)]
        for cp in copies:
            cp.start()
        for cp in copies:
            cp.wait()

    return _call(body, name, (), [pl.BlockSpec(memory_space=pltpu.VMEM)] * n, _any_specs(n),
                 [_sds((N_DEV,) + s.shape, s.dtype) for s in srcs], [pltpu.SemaphoreType.DMA((n,))],
                 deps=deps)(*srcs)


def _gather_start(name, lands):
    n = len(lands)

    def body(*refs):
        land, send_sems, recv_sems = refs[:n], refs[n], refs[n + 1]
        x, y, c = _mesh_pos()
        dev = 4 * x + 2 * y + c
        for k in range(n):
            for peer in _peers(x, y, c):
                pltpu.make_async_remote_copy(
                    src_ref=land[k].at[dev], dst_ref=land[k].at[dev], send_sem=send_sems.at[k],
                    recv_sem=recv_sems.at[k], device_id=peer, device_id_type=MESH).start()
        token = refs[-1]
        token[...] = jnp.zeros_like(token)

    outs = _split_call(
        body, name, [HBM_SPEC] * n, [SEM_SPEC, SEM_SPEC] + [HBM_SPEC] * n + [pl.BlockSpec(memory_space=pltpu.VMEM)],
        [pltpu.SemaphoreType.DMA((n,)), pltpu.SemaphoreType.DMA((n,))] + [pltpu.HBM(a.shape, a.dtype) for a in lands]
        + [_sds((SUBLANES, LANES), F32)],
        {k: 2 + k for k in range(n)})(*[_in_hbm(a) for a in lands])
    return outs[0], outs[1], list(outs[2:2 + n]), outs[-1]


def _wait_seven(src_ref, dst_ref, send_sem, recv_sem):
    cp = pltpu.make_async_remote_copy(
        src_ref=src_ref.at[pl.ds(0, N_PEER)], dst_ref=dst_ref.at[pl.ds(0, N_PEER)], send_sem=send_sem,
        recv_sem=recv_sem, device_id=_mesh_pos(), device_id_type=MESH)
    cp.wait_send()
    cp.wait_recv()


def _gather_wait(name, land, send_sems, recv_sems, k, after):
    def body(land_ref, send_ref, recv_ref, after_ref, out_ref):
        _wait_seven(land_ref, land_ref, send_ref.at[k], recv_ref.at[k])

    return _split_call(body, name, [HBM_SPEC, SEM_SPEC, SEM_SPEC, pl.BlockSpec(memory_space=pl.ANY)], HBM_SPEC,
                       pltpu.HBM(land.shape, land.dtype), {0: 0})(land, send_sems, recv_sems, after)


def _scatter_start(name, parts_list):
    n = len(parts_list)

    def body(*refs):
        x, y, c = _mesh_pos()
        dev = 4 * x + 2 * y + c
        for k in range(n):
            parts_ref, land_ref = refs[2 * k], refs[2 * k + 1]
            send_sem, recv_sem = refs[2 * n + 4 * k], refs[2 * n + 4 * k + 1]
            for peer in _peers(x, y, c):
                pltpu.make_async_remote_copy(
                    src_ref=parts_ref.at[4 * peer[0] + 2 * peer[1] + peer[2]], dst_ref=land_ref.at[dev],
                    send_sem=send_sem, recv_sem=recv_sem, device_id=peer, device_id_type=MESH).start()
        token = refs[-1]
        token[...] = jnp.zeros_like(token)

    ins, out_specs, out_shape, aliases = [], [], [], {}
    for k, parts in enumerate(parts_list):
        buf = pltpu.HBM(parts.shape, parts.dtype)
        ins += [_in_hbm(parts), _in_hbm(lax.empty(parts.shape, parts.dtype))]
        out_specs += [SEM_SPEC, SEM_SPEC, HBM_SPEC, HBM_SPEC]
        out_shape += [pltpu.SemaphoreType.DMA(()), pltpu.SemaphoreType.DMA(()), buf, buf]
        aliases.update({2 * k: 4 * k + 2, 2 * k + 1: 4 * k + 3})
    outs = _split_call(body, name, [HBM_SPEC] * (2 * n), out_specs + [pl.BlockSpec(memory_space=pltpu.VMEM)],
                       out_shape + [_sds((SUBLANES, LANES), F32)], aliases)(*ins)
    return [tuple(outs[4 * k:4 * k + 4]) for k in range(n)], outs[-1]


def _scatter_wait(name, started, after):
    n = len(started)

    def body(*refs):
        for k in range(n):
            send_sem, recv_sem, parts_ref, land_ref = refs[4 * k:4 * k + 4]
            _wait_seven(parts_ref, land_ref, send_sem, recv_sem)

    flat = [a for s in started for a in s]
    outs = _split_call(
        body, name, [SEM_SPEC, SEM_SPEC, HBM_SPEC, HBM_SPEC] * n + [pl.BlockSpec(memory_space=pl.ANY)],
        [HBM_SPEC, HBM_SPEC] * n, [pltpu.HBM(a.shape, a.dtype) for s in started for a in s[2:]],
        {4 * k + 2 + t: 2 * k + t for k in range(n) for t in range(2)})(*flat, after)
    return list(outs[0::2]), list(outs[1::2])


def _to_segments(a, tile):
    seg = tile // SUBLANES
    return a.reshape((a.shape[0] // tile, SUBLANES, seg) + a.shape[1:]).swapaxes(1, 2).reshape(a.shape)


def _from_segments(a, tile):
    seg = tile // SUBLANES
    return a.reshape((a.shape[0] // tile, seg, SUBLANES) + a.shape[1:]).swapaxes(1, 2).reshape(a.shape)


def _chunk(ref, q):
    return ref[q * SUBLANES:(q + 1) * SUBLANES, :]


def _fill_wrap_prev(x_ref, halo_ref, wrap_ref, n_wrap, n_halo, seg, keep):
    sub = lax.broadcasted_iota(jnp.int32, (SUBLANES, x_ref.shape[-1]), 0)
    for j in range(n_wrap):
        q = seg - n_wrap + j
        hq = q - (seg - n_halo)
        row = halo_ref[hq * SUBLANES + SUBLANES - 1:(hq + 1) * SUBLANES, :] * keep
        wrap_ref[j * SUBLANES:(j + 1) * SUBLANES, :] = jnp.where(sub == 0, row, pltpu.roll(_chunk(x_ref, q), 1, 0))


def _fill_wrap_next(x_ref, halo_ref, wrap_ref, n_wrap, keep):
    sub = lax.broadcasted_iota(jnp.int32, (SUBLANES, x_ref.shape[-1]), 0)
    for j in range(n_wrap):
        row = halo_ref[j * SUBLANES:j * SUBLANES + 1, :] * keep
        wrap_ref[j * SUBLANES:(j + 1) * SUBLANES, :] = jnp.where(
            sub == SUBLANES - 1, row, pltpu.roll(_chunk(x_ref, j), SUBLANES - 1, 0))


def _past(x_ref, wrap_ref, q, d, n_wrap):
    return _chunk(x_ref, q - d) if q >= d else _chunk(wrap_ref, q - d + n_wrap)


def _future(x_ref, wrap_ref, q, d, seg):
    return _chunk(x_ref, q + d) if q + d < seg else _chunk(wrap_ref, q + d - seg)


def _conv_fwd(x_ref, wrap_ref, w_ref, b_ref, out_ref, seg, k_taps):
    bias = jnp.broadcast_to(b_ref[...], (SUBLANES, x_ref.shape[-1]))
    for q in range(seg):
        acc = bias
        for k in range(k_taps):
            acc = acc + w_ref[k:k + 1, :] * _past(x_ref, wrap_ref, q, k_taps - 1 - k, k_taps - 1)
        out_ref[q * SUBLANES:(q + 1) * SUBLANES, :] = acc


def _conv_bwd_data(d_ref, wrap_ref, w_ref, out_ref, seg, k_taps):
    for q in range(seg):
        acc = None
        for k in range(k_taps):
            term = w_ref[k:k + 1, :] * _future(d_ref, wrap_ref, q, k_taps - 1 - k, seg)
            acc = term if acc is None else acc + term
        out_ref[q * SUBLANES:(q + 1) * SUBLANES, :] = acc


def _conv_bwd_taps(d_ref, x_ref, wrap_ref, dw_ref, seg, k_taps):
    for k in range(k_taps):
        part = None
        for q in range(seg):
            term = _chunk(d_ref, q) * _past(x_ref, wrap_ref, q, k_taps - 1 - k, k_taps - 1)
            part = term if part is None else part + term
        dw_ref[k:k + 1, :] += _colsum(part)


def _tile_halo_specs(tm, width_block, n_halo, n_tiles, block_of):
    rows = n_halo * SUBLANES
    per = tm // rows
    tile = pl.BlockSpec(width_block(tm), lambda n, i: block_of(n, i))
    prev = pl.BlockSpec(width_block(rows), lambda n, i: block_of(n, jnp.maximum(i * per - 1, 0)))
    nxt = pl.BlockSpec(width_block(rows), lambda n, i: block_of(n, jnp.minimum((i + 1) * per, n_tiles * per - 1)))
    return tile, prev, nxt


def _ln_stats(v):
    mu = jnp.mean(v, axis=-1, keepdims=True)
    vc = v - mu
    var = jnp.mean(vc * vc, axis=-1, keepdims=True)
    rstd = lax.rsqrt(var + LN_EPS)
    return vc * rstd, rstd


def _ln_backward(dxhat, xhat, rstd):
    m1 = jnp.mean(dxhat, axis=-1, keepdims=True)
    m2 = jnp.mean(dxhat * xhat, axis=-1, keepdims=True)
    return rstd * (dxhat - m1 - xhat * m2)


def _row_spec(tm, width):
    return pl.BlockSpec((tm, width), lambda i: (i, 0))


def _param_spec(l, width):
    return pl.BlockSpec((None, 1, width), lambda *_: (l, 0, 0))


def _ln_bwd_rows(dout, xh_ref, rs_ref, g_ref, dr_ref, dg_ref, db_ref, dsum_ref, first):
    xhat = xh_ref[...]
    dr = _ln_backward(dout * g_ref[...], xhat, rs_ref[...])
    dr_ref[...] = dr
    _acc_rows(dg_ref, _colsum(dout * xhat), first)
    _acc_rows(db_ref, _colsum(dout), first)
    _acc_rows(dsum_ref, _colsum(dr), first)


def _ln_bwd_specs(tm, d, l, row_of):
    vec = pl.BlockSpec((1, d), lambda *_: (0, 0))
    ins = [pl.BlockSpec((tm, d), row_of), pl.BlockSpec((tm, 1), row_of), _param_spec(l, d)]
    return ins, [pl.BlockSpec((tm, d), row_of), vec, vec, vec]


def _ln_res_bwd(name, dout, xhat, rstd, g3, l, deps=()):
    t, d = dout.shape
    tm = min(TM_ROW, t)

    def body(do_ref, xh_ref, rs_ref, g_ref, dr_ref, dg_ref, db_ref, dc_ref):
        _ln_bwd_rows(do_ref[...], xh_ref, rs_ref, g_ref, dr_ref, dg_ref, db_ref, dc_ref, pl.program_id(0) == 0)

    ins, outs = _ln_bwd_specs(tm, d, l, lambda i: (i, 0))
    return _call(body, name, (t // tm,), [_row_spec(tm, d)] + ins, outs,
                 [_sds((t, d), F32)] + [_sds((1, d), F32)] * 3, deps=deps)(dout, xhat, rstd, g3)


def _glu_bwd(name, du, h):
    t, c2 = h.shape
    c = c2 // 2
    tm = min(TM_ROW, t)

    def body(du_ref, a_ref, g_ref, dh_ref, db_ref):
        first = pl.program_id(0) == 0
        du_v, a = du_ref[...], a_ref[...]
        sg = _sigmoid(g_ref[...])
        da = du_v * sg
        dg = du_v * a * sg * (1.0 - sg)
        dh_ref[:, :c] = da.astype(BF16)
        dh_ref[:, c:] = dg.astype(BF16)
        _acc_rows(db_ref.at[:, :c], _colsum(da), first)
        _acc_rows(db_ref.at[:, c:], _colsum(dg), first)

    return _call(body, name, (t // tm,),
                 [_row_spec(tm, c), pl.BlockSpec((tm, c), lambda i: (i, 0)), pl.BlockSpec((tm, c), lambda i: (i, 1))],
                 [_row_spec(tm, c2), pl.BlockSpec((1, c2), lambda i: (0, 0))],
                 [_sds((t, c2), BF16), _sds((1, c2), F32)])(du, h, h)


CONV_CB = 512
TAPS_PAD = 32


def _dwconv31(name, u, w3, b3, l, seq):
    t, c = u.shape
    tm, cb = TM_EW, CONV_CB
    seg, seq_tiles, n_tiles = tm // SUBLANES, seq // tm, t // tm
    n_wrap = CONV_K - 1
    tile, prev, _ = _tile_halo_specs(tm, lambda rows: (rows, cb), seg, n_tiles, lambda n, r: (r, n))

    def body(u_ref, halo_ref, w_ref, b_ref, o_ref, wrap_ref):
        keep = (pl.program_id(1) % seq_tiles != 0).astype(F32)
        _fill_wrap_prev(u_ref, halo_ref, wrap_ref, n_wrap, seg, seg, keep)
        _conv_fwd(u_ref, wrap_ref, w_ref, b_ref, o_ref, seg, CONV_K)

    return _call(body, name, (c // cb, n_tiles),
                 [tile, prev, pl.BlockSpec((None, CONV_K, cb), lambda n, i: (l, 0, n)),
                  pl.BlockSpec((None, 1, cb), lambda n, i: (l, 0, n))],
                 tile, _sds((t, c), F32), [pltpu.VMEM((n_wrap * SUBLANES, cb), F32)])(u, u, w3, b3)


def _dwconv31_bwd(name, dc, u, w3, l, seq):
    t, c = dc.shape
    tm, cb = TM_EW, CONV_CB
    seg, seq_tiles, n_tiles = tm // SUBLANES, seq // tm, t // tm
    n_wrap = CONV_K - 1
    tile, prev, nxt = _tile_halo_specs(tm, lambda rows: (rows, cb), seg, n_tiles, lambda n, r: (r, n))

    def body(dc_ref, dcn_ref, u_ref, up_ref, w_ref, du_ref, dw_ref, dwrap_ref, uwrap_ref):
        i = pl.program_id(1)
        keep_prev = (i % seq_tiles != 0).astype(F32)
        keep_next = (i % seq_tiles != seq_tiles - 1).astype(F32)
        _fill_wrap_next(dc_ref, dcn_ref, dwrap_ref, n_wrap, keep_next)
        _conv_bwd_data(dc_ref, dwrap_ref, w_ref, du_ref, seg, CONV_K)

        @pl.when(i == 0)
        def _():
            dw_ref[...] = jnp.zeros_like(dw_ref)

        _fill_wrap_prev(u_ref, up_ref, uwrap_ref, n_wrap, seg, seg, keep_prev)
        _conv_bwd_taps(dc_ref, u_ref, uwrap_ref, dw_ref, seg, CONV_K)

    wrap = pltpu.VMEM((n_wrap * SUBLANES, cb), F32)
    return _call(body, name, (c // cb, n_tiles),
                 [tile, nxt, tile, prev, pl.BlockSpec((None, CONV_K, cb), lambda n, i: (l, 0, n))],
                 [tile, pl.BlockSpec((TAPS_PAD, cb), lambda n, i: (0, n))],
                 [_sds((t, c), F32), _sds((TAPS_PAD, c), F32)], [wrap, wrap])(dc, dc, u, u, w3)


def _ln_silu(name, cx, g3, b3, l):
    t, d = cx.shape
    tm = min(TM_ROW, t)

    def body(c_ref, g_ref, b_ref, o_ref):
        xhat, _ = _ln_stats(c_ref[...])
        nv = xhat * g_ref[...] + b_ref[...]
        o_ref[...] = (nv * _sigmoid(nv)).astype(BF16)

    return _call(body, name, (t // tm,), [_row_spec(tm, d), _param_spec(l, d), _param_spec(l, d)],
                 _row_spec(tm, d), _sds((t, d), BF16))(cx, g3, b3)


def _ln_silu_bwd(name, ds, cx, g3, b3, l, deps=()):
    t, d = cx.shape
    tm = min(TM_ROW, t)

    def body(ds_ref, c_ref, g_ref, b_ref, dc_ref, dg_ref, db_ref, dsum_ref):
        first = pl.program_id(0) == 0
        xhat, rstd = _ln_stats(c_ref[...])
        g = g_ref[...]
        nv = xhat * g + b_ref[...]
        sg = _sigmoid(nv)
        dn = ds_ref[...] * (sg * (1.0 + nv * (1.0 - sg)))
        dc = _ln_backward(dn * g, xhat, rstd)
        dc_ref[...] = dc
        _acc_rows(dg_ref, _colsum(dn * xhat), first)
        _acc_rows(db_ref, _colsum(dn), first)
        _acc_rows(dsum_ref, _colsum(dc), first)

    vec = pl.BlockSpec((1, d), lambda i: (0, 0))
    return _call(body, name, (t // tm,),
                 [_row_spec(tm, d), _row_spec(tm, d), _param_spec(l, d), _param_spec(l, d)],
                 [_row_spec(tm, d), vec, vec, vec],
                 [_sds((t, d), F32)] + [_sds((1, d), F32)] * 3, deps=deps)(ds, cx, g3, b3)


FFN_HALO = FFN_K - 1


def _ffn_conv(x_ref, halo_ref, wrap_ref, w_ref, b_ref, keep, seg, out_ref):
    _fill_wrap_prev(x_ref, halo_ref, wrap_ref, FFN_K - 1, FFN_HALO, seg, keep)
    _conv_fwd(x_ref, wrap_ref, w_ref, b_ref, out_ref, seg, FFN_K)


TM_FFN = 512
WRAP_ROWS = FFN_HALO * SUBLANES


def _sub_tiles(x_ref, prev_ref, next_ref, keep_prev, keep_next, n_sub):
    out = []
    for s in range(n_sub):
        tile = x_ref.at[pl.ds(s * TM_EW, TM_EW)]
        prev = prev_ref if s == 0 else x_ref.at[pl.ds(s * TM_EW - WRAP_ROWS, WRAP_ROWS)]
        nxt = next_ref if s == n_sub - 1 else x_ref.at[pl.ds((s + 1) * TM_EW, WRAP_ROWS)]
        out.append((tile, prev, keep_prev if s == 0 else 1.0, nxt, keep_next if s == n_sub - 1 else 1.0))
    return out


def _rows(ref, s, rows):
    return ref.at[pl.ds(s * rows, rows)]


def _ffn_up_act(name, x, w_up, b_up4, wdw, bdw, l, seq):
    t, d = x.shape
    nb, fb, _ = w_up.shape
    half = nb // 2
    tm = min(TM_FFN, seq)
    n_sub, seg, seq_steps, n_steps = tm // TM_EW, TM_EW // SUBLANES, seq // tm, t // tm
    per = tm // WRAP_ROWS
    nt_dims = (_DIMS["nt"], ((), ()))

    def body(x_ref, xp_ref, ug_ref, uv_ref, bug_ref, buv_ref, wg_ref, wv_ref, bg_ref, bv_ref,
             hg_ref, hv_ref, a_ref, pg_ref, pv_ref, gwrap_ref, vwrap_ref, cg_ref, cv_ref):
        keep = (pl.program_id(1) % seq_steps != 0).astype(F32)
        xb, xpb = x_ref[...].astype(BF16), xp_ref[...].astype(BF16)
        hg_ref[...] = lax.dot_general(xb, ug_ref[...], nt_dims, preferred_element_type=F32) + bug_ref[...]
        pg_ref[...] = lax.dot_general(xpb, ug_ref[...], nt_dims, preferred_element_type=F32) + bug_ref[...]
        for s, (tile, prev, kp, _, _) in enumerate(_sub_tiles(hg_ref, pg_ref, None, keep, None, n_sub)):
            _ffn_conv(tile, prev, gwrap_ref, wg_ref, bg_ref, kp, seg, _rows(cg_ref, s, TM_EW))
        hv_ref[...] = lax.dot_general(xb, uv_ref[...], nt_dims, preferred_element_type=F32) + buv_ref[...]
        pv_ref[...] = lax.dot_general(xpb, uv_ref[...], nt_dims, preferred_element_type=F32) + buv_ref[...]
        for s, (tile, prev, kp, _, _) in enumerate(_sub_tiles(hv_ref, pv_ref, None, keep, None, n_sub)):
            _ffn_conv(tile, prev, vwrap_ref, wv_ref, bv_ref, kp, seg, _rows(cv_ref, s, TM_EW))
        cg = cg_ref[...]
        a_ref[...] = (cg * _sigmoid(cg) * cv_ref[...]).astype(BF16)

    def blk(shift):
        return pl.BlockSpec((None, fb, d), lambda n, i: (n + shift, 0, 0))

    def vec(shift, rows):
        return pl.BlockSpec((None, None, rows, fb), lambda n, i: (l, n + shift, 0, 0))

    out = pl.BlockSpec((None, tm, fb), lambda n, i: (n, i, 0))
    tmp = pltpu.VMEM((tm, fb), F32)
    halo = pltpu.VMEM((WRAP_ROWS, fb), F32)
    return _call(body, name, (half, n_steps),
                 [pl.BlockSpec((tm, d), lambda n, i: (i, 0)),
                  pl.BlockSpec((WRAP_ROWS, d), lambda n, i: (jnp.maximum(i * per - 1, 0), 0)),
                  blk(0), blk(half), vec(0, 1), vec(half, 1), vec(0, FFN_K), vec(half, FFN_K), vec(0, 1), vec(half, 1)],
                 [out, out, out],
                 [_sds((half, t, fb), F32), _sds((half, t, fb), F32), _sds((half, t, fb), BF16)],
                 [halo, halo, halo, halo, tmp, tmp])(x, x, w_up, w_up, b_up4, b_up4, wdw, wdw, bdw, bdw)


def _ffn_act_bwd(name, dy, w_down, hg, hv, wdw, bdw, l, seq, deps=()):
    half, t, fb = hg.shape
    d = dy.shape[-1]
    tm = min(TM_FFN, seq)
    n_sub, seg, seq_steps, n_steps = tm // TM_EW, TM_EW // SUBLANES, seq // tm, t // tm
    pair = 2
    tile, prev, _ = _tile_halo_specs(tm, lambda rows: (pair, rows, fb), FFN_HALO, n_steps, lambda m, r: (m, r, 0))

    def body(dy_ref, wd_ref, g_ref, gp_ref, v_ref, vp_ref, wg_ref, wv_ref, bg_ref, bv_ref,
             dg_ref, dv_ref, dbg_ref, dbv_ref, dwg_ref, dwv_ref, gwrap_ref, vwrap_ref, cg_ref, cv_ref):
        i = pl.program_id(1)
        first = i == 0
        keep = (i % seq_steps != 0).astype(F32)
        dyb = dy_ref[...].astype(BF16)

        @pl.when(first)
        def _():
            dwg_ref[...] = jnp.zeros_like(dwg_ref)
            dwv_ref[...] = jnp.zeros_like(dwv_ref)

        for b in range(pair):
            da = lax.dot_general(dyb, wd_ref[b], (_DIMS["nt"], ((), ())), preferred_element_type=F32)
            g_tiles = _sub_tiles(g_ref.at[b], gp_ref.at[b], None, keep, None, n_sub)
            v_tiles = _sub_tiles(v_ref.at[b], vp_ref.at[b], None, keep, None, n_sub)
            for s in range(n_sub):
                _ffn_conv(g_tiles[s][0], g_tiles[s][1], _rows(gwrap_ref, s, WRAP_ROWS), wg_ref.at[b], bg_ref.at[b],
                          g_tiles[s][2], seg, _rows(cg_ref, s, TM_EW))
                _ffn_conv(v_tiles[s][0], v_tiles[s][1], _rows(vwrap_ref, s, WRAP_ROWS), wv_ref.at[b], bv_ref.at[b],
                          v_tiles[s][2], seg, _rows(cv_ref, s, TM_EW))
            cg, cv = cg_ref[...], cv_ref[...]
            sg = _sigmoid(cg)
            dcv = da * cg * sg
            dcg = da * cv * sg * (1.0 + cg * (1.0 - sg))
            dg_ref[b] = dcg
            dv_ref[b] = dcv
            _acc_rows(dbg_ref.at[b], _colsum(dcg), first)
            _acc_rows(dbv_ref.at[b], _colsum(dcv), first)
            for s in range(n_sub):
                _conv_bwd_taps(_rows(dg_ref.at[b], s, TM_EW), g_tiles[s][0], _rows(gwrap_ref, s, WRAP_ROWS),
                               dwg_ref.at[b], seg, FFN_K)
                _conv_bwd_taps(_rows(dv_ref.at[b], s, TM_EW), v_tiles[s][0], _rows(vwrap_ref, s, WRAP_ROWS),
                               dwv_ref.at[b], seg, FFN_K)

    def vec(shift, rows):
        return pl.BlockSpec((None, pair, rows, fb), lambda m, i: (l, m + shift // pair, 0, 0))

    def acc(rows):
        return pl.BlockSpec((pair, rows, fb), lambda m, i: (m, 0, 0))

    wrap = pltpu.VMEM((n_sub * WRAP_ROWS, fb), F32)
    tmp = pltpu.VMEM((tm, fb), F32)
    return _call(body, name, (half // pair, n_steps),
                 [pl.BlockSpec((tm, d), lambda m, i: (i, 0)), pl.BlockSpec((pair, fb, d), lambda m, i: (m, 0, 0)),
                  tile, prev, tile, prev, vec(0, FFN_K), vec(half, FFN_K), vec(0, 1), vec(half, 1)],
                 [tile, tile, acc(1), acc(1), acc(SUBLANES), acc(SUBLANES)],
                 [_sds((half, t, fb), F32), _sds((half, t, fb), F32), _sds((half, 1, fb), F32),
                  _sds((half, 1, fb), F32), _sds((half, SUBLANES, fb), F32), _sds((half, SUBLANES, fb), F32)],
                 [wrap, wrap, tmp, tmp], deps=deps)(dy, w_down, hg, hg, hv, hv, wdw, wdw, bdw, bdw)


def _ffn_conv_t_dx(name, dcg, dcv, wdw, w_up, res, xhat, rstd, g3, l, seq, deps=()):
    half, t, fb = dcg.shape
    nb, d = 2 * half, res.shape[-1]
    tm = min(TM_FFN, seq)
    n_sub, seg, seq_steps, n_steps = tm // TM_EW, TM_EW // SUBLANES, seq // tm, t // tm
    per = tm // WRAP_ROWS

    pair = 2
    n_pairs, half_pairs = nb // pair, half // pair

    def body(g_ref, gn_ref, v_ref, vn_ref, w_ref, up_ref, res_ref, xh_ref, rs_ref, gam_ref,
             dh_ref, db_ref, dr_ref, dgam_ref, dbeta_ref, dsum_ref, wrap_ref, out_ref, acc_ref):
        i, m = pl.program_id(0), pl.program_id(1)
        keep = (i % seq_steps != seq_steps - 1).astype(F32)

        def conv_t(d_ref, dn_ref, b):
            for s, (sub, _, _, nx, kn) in enumerate(_sub_tiles(d_ref.at[b], None, dn_ref.at[b], None, keep, n_sub)):
                _fill_wrap_next(sub, nx, wrap_ref, FFN_K - 1, kn)
                _conv_bwd_data(sub, wrap_ref, w_ref.at[b], _rows(out_ref, s, TM_EW), seg, FFN_K)

        p = None
        for b in range(pair):
            @pl.when(m < half_pairs)
            def _(b=b):
                conv_t(g_ref, gn_ref, b)

            @pl.when(m >= half_pairs)
            def _(b=b):
                conv_t(v_ref, vn_ref, b)

            dh = out_ref[...]
            dhb = dh.astype(BF16)
            dh_ref[b] = dhb
            _acc_rows(db_ref.at[pair * m + b], _colsum(dh), i == 0)
            part = jnp.dot(dhb, up_ref[b], preferred_element_type=F32)
            p = part if p is None else p + part

        @pl.when(m == 0)
        def _():
            acc_ref[...] = p

        @pl.when(m > 0)
        def _():
            acc_ref[...] += p

        @pl.when(m == n_pairs - 1)
        def _():
            _ln_bwd_rows(acc_ref[...] + ALPHA * res_ref[...], xh_ref, rs_ref, gam_ref, dr_ref, dgam_ref, dbeta_ref,
                         dsum_ref, i == 0)

    def src(gate):
        def blk(m):
            return jnp.minimum(m, half_pairs - 1) if gate else jnp.maximum(m - half_pairs, 0)
        tile = pl.BlockSpec((pair, tm, fb), lambda i, m: (blk(m), i, 0))
        nxt = pl.BlockSpec((pair, WRAP_ROWS, fb),
                           lambda i, m: (blk(m), jnp.minimum((i + 1) * per, n_steps * per - 1), 0))
        return [tile, nxt]

    row = pl.BlockSpec((tm, d), lambda i, m: (i, 0))
    ln_ins, ln_outs = _ln_bwd_specs(tm, d, l, lambda i, m: (i, 0))
    tmp = pltpu.VMEM((tm, fb), F32)
    halo = pltpu.VMEM((WRAP_ROWS, fb), F32)
    return _call(body, name, (n_steps, n_pairs),
                 src(True) + src(False) +
                 [pl.BlockSpec((None, pair, FFN_K, fb), lambda i, m: (l, m, 0, 0)),
                  pl.BlockSpec((pair, fb, d), lambda i, m: (m, 0, 0)), row] + ln_ins,
                 [pl.BlockSpec((pair, tm, fb), lambda i, m: (m, i, 0)),
                  pl.BlockSpec((nb, 1, fb), lambda i, m: (0, 0, 0))] + ln_outs,
                 [_sds((nb, t, fb), BF16), _sds((nb, 1, fb), F32), _sds((t, d), F32)] + [_sds((1, d), F32)] * 3,
                 [halo, tmp, pltpu.VMEM((tm, d), F32)],
                 deps=deps)(dcg, dcg, dcv, dcv, wdw, w_up, res, xhat, rstd, g3)


def _gelu_parts(h):
    cdf = 0.5 * (1.0 + lax.erf(h * INV_SQRT2))
    return h * cdf, cdf


def _seg_axis(a, axis, fn):
    return jnp.moveaxis(fn(jnp.moveaxis(a, axis, 0), TM_EW), 0, axis)


def _sgu_operands(w_s, b_s):
    nl = w_s.shape[0]
    n_sub = TM_EW // CHUNK
    tril = jnp.tril(jnp.ones((CHUNK, CHUNK), dtype=bool))
    w_causal = jnp.where(tril, w_s, 0.0)
    w_tile = (jnp.eye(n_sub, dtype=F32)[None, None, :, None, :, None] * w_causal[:, :, None, :, None, :]).reshape(
        nl, GROUPS, TM_EW, TM_EW)
    w_tile = _seg_axis(_seg_axis(w_tile, 2, _to_segments), 3, _to_segments).astype(BF16)
    bs_tile = jnp.broadcast_to(b_s[:, :, None, :, None], (nl, GROUPS, n_sub, CHUNK, CHUNK)).reshape(
        nl, GROUPS, TM_EW, CHUNK)
    return w_tile, _seg_axis(bs_tile, 2, _to_segments)


def _sgu_param_grads(dwt, dbt):
    n_sub = TM_EW // CHUNK
    tril = jnp.tril(jnp.ones((CHUNK, CHUNK), dtype=bool))
    dwt = _seg_axis(_seg_axis(dwt, 1, _from_segments), 2, _from_segments).reshape(GROUPS, n_sub, CHUNK, n_sub, CHUNK)
    dw = sum(dwt[:, a, :, a, :] for a in range(n_sub))
    db = _seg_axis(dbt, 1, _from_segments).reshape(GROUPS, n_sub, CHUNK).sum(axis=1)
    return jnp.where(tril, dw, 0.0), db


def _sgu(name, h, g3, b3, wt, bst, l):
    t, c2 = h.shape
    c = c2 // 2
    tm = TM_EW

    def body(h_ref, g_ref, b_ref, wt_ref, bs_ref, o_ref):
        z, _ = _gelu_parts(h_ref[...])
        u = z[:, :c]
        xhat, _ = _ln_stats(z[:, c:])
        vnb = (xhat * g_ref[...] + b_ref[...]).astype(BF16)
        for gi in range(GROUPS):
            cs = slice(gi * CHUNK, (gi + 1) * CHUNK)
            sp = jnp.dot(wt_ref[gi], vnb[:, cs], preferred_element_type=F32) + bs_ref[gi]
            o_ref[:, cs] = (u[:, cs] * sp).astype(BF16)

    return _call(body, name, (t // tm,),
                 [_row_spec(tm, c2), _param_spec(l, c), _param_spec(l, c),
                  pl.BlockSpec((None, GROUPS, tm, tm), lambda i: (l, 0, 0, 0)),
                  pl.BlockSpec((None, GROUPS, tm, CHUNK), lambda i: (l, 0, 0, 0))],
                 _row_spec(tm, c), _sds((t, c), BF16))(h, g3, b3, wt, bst)


def _sgu_bwd(name, dq, h, g3, b3, wt, bst, l, deps=()):
    t, c2 = h.shape
    c = c2 // 2
    tm = TM_EW
    n_tiles = t // tm

    def body(dq_ref, h_ref, g_ref, b_ref, wt_ref, bs_ref,
             dh_ref, dbin_ref, dw_ref, dbs_ref, dg_ref, db_ref, du_ref, dvn_ref, bsum_ref):
        i = pl.program_id(0)
        first = i == 0
        hv = h_ref[...]
        z, cdf = _gelu_parts(hv)
        u = z[:, :c]
        xhat, rstd = _ln_stats(z[:, c:])
        g = g_ref[...]
        vnb = (xhat * g + b_ref[...]).astype(BF16)

        @pl.when(first)
        def _():
            dw_ref[...] = jnp.zeros_like(dw_ref)
            bsum_ref[...] = jnp.zeros_like(bsum_ref)

        for gi in range(GROUPS):
            cs = slice(gi * CHUNK, (gi + 1) * CHUNK)
            vb = vnb[:, cs]
            w = wt_ref[gi]
            sp = jnp.dot(w, vb, preferred_element_type=F32) + bs_ref[gi]
            dqb = dq_ref[:, cs]
            du_ref[:, cs] = dqb * sp
            dsp = dqb * u[:, cs]
            bsum_ref[gi] += dsp
            dspb = dsp.astype(BF16)
            dw_ref[gi] += lax.dot_general(dspb, vb, (_DIMS["nt"], ((), ())), preferred_element_type=F32)
            dvn_ref[:, cs] = lax.dot_general(w, dspb, (_DIMS["tn"], ((), ())), preferred_element_type=F32)

        dvn = dvn_ref[...]
        dv = _ln_backward(dvn * g, xhat, rstd)
        pdf = jnp.exp(-0.5 * hv * hv) * INV_SQRT2PI
        dgelu = cdf + hv * pdf
        dhu = du_ref[...] * dgelu[:, :c]
        dhv = dv * dgelu[:, c:]
        dh_ref[:, :c] = dhu.astype(BF16)
        dh_ref[:, c:] = dhv.astype(BF16)
        _acc_rows(dbin_ref.at[:, :c], _colsum(dhu), first)
        _acc_rows(dbin_ref.at[:, c:], _colsum(dhv), first)
        _acc_rows(dg_ref, _colsum(dvn * xhat), first)
        _acc_rows(db_ref, _colsum(dvn), first)

        @pl.when(i == n_tiles - 1)
        def _():
            dbs_ref[...] = jnp.sum(bsum_ref[...], axis=-1)

    vec = pl.BlockSpec((1, c), lambda i: (0, 0))
    return _call(body, name, (n_tiles,),
                 [_row_spec(tm, c), _row_spec(tm, c2), _param_spec(l, c), _param_spec(l, c),
                  pl.BlockSpec((None, GROUPS, tm, tm), lambda i: (l, 0, 0, 0)),
                  pl.BlockSpec((None, GROUPS, tm, CHUNK), lambda i: (l, 0, 0, 0))],
                 [_row_spec(tm, c2), pl.BlockSpec((1, c2), lambda i: (0, 0)),
                  pl.BlockSpec((GROUPS, tm, tm), lambda i: (0, 0, 0)),
                  pl.BlockSpec((GROUPS, tm), lambda i: (0, 0)), vec, vec],
                 [_sds((t, c2), BF16), _sds((1, c2), F32), _sds((GROUPS, tm, tm), F32),
                  _sds((GROUPS, tm), F32), _sds((1, c), F32), _sds((1, c), F32)],
                 [pltpu.VMEM((tm, c), F32), pltpu.VMEM((tm, c), F32), pltpu.VMEM((GROUPS, tm, CHUNK), F32)],
                 deps=deps)(dq, h, g3, b3, wt, bst)


def _loss(name, y, target):
    t, d = y.shape
    tm = min(TM_ROW, t)
    n_tiles = t // tm

    def body(y_ref, t_ref, l_ref, dy_ref, acc_ref):
        i = pl.program_id(0)
        diff = y_ref[...] - t_ref[...]
        dy_ref[...] = diff * (1.0 / d)
        _acc_rows(acc_ref, _colsum(diff * diff), i == 0)

        @pl.when(i == n_tiles - 1)
        def _():
            l_ref[...] = jnp.broadcast_to(jnp.sum(acc_ref[...], axis=-1, keepdims=True) * (0.5 / d), (1, LANES))

    return _call(body, name, (n_tiles,), [_row_spec(tm, d), _row_spec(tm, d)],
                 [pl.BlockSpec((1, LANES), lambda i: (0, 0)), _row_spec(tm, d)],
                 [_sds((1, LANES), F32), _sds((t, d), F32)], [pltpu.VMEM((1, d), F32)])(y, target)


def _adamw(g, w, m, v):
    m2 = ADAM_B1 * m + (1.0 - ADAM_B1) * g
    v2 = ADAM_B2 * v + (1.0 - ADAM_B2) * (g * g)
    m_hat = m2 / (1.0 - ADAM_B1 ** ADAM_STEP)
    v_hat = v2 / (1.0 - ADAM_B2 ** ADAM_STEP)
    delta = -ADAM_LR * (m_hat / (jnp.sqrt(v_hat) + ADAM_EPS) + ADAM_WD * w)
    return delta, m2, v2


ROW_TILE_CAP = 512


def _row_tile(rows, cap=ROW_TILE_CAP):
    if rows <= cap:
        return rows
    for tr in range(cap, 15, -16):
        if rows % tr == 0:
            return tr
    return rows


def _sum8_adamw(name, dev, lands, parts, w, m, v):
    nl = len(lands)
    _, r, c = lands[0].shape
    tr = _row_tile(r, cap=256)

    def body(dev_ref, *refs):
        land, own = refs[:nl], refs[nl:2 * nl]
        w_ref, m_ref, v_ref, g_ref, d_ref, m2_ref, v2_ref = refs[2 * nl:]
        layer, me = pl.program_id(0), dev_ref[0]
        for l in range(nl):
            @pl.when(layer == l)
            def _(l=l):
                g = None
                for s in range(N_DEV):
                    part = jnp.where(me == s, own[l][...], land[l][s]).astype(F32)
                    g = part if g is None else g + part
                delta, m2, v2 = _adamw(g, w_ref[...], m_ref[...], v_ref[...])
                g_ref[...] = g
                d_ref[...] = delta
                m2_ref[...] = m2
                v2_ref[...] = v2

    def rows_of(l, a, i):
        return jnp.where(a == l, i, 0)

    spec = pl.BlockSpec((None, tr, c), lambda a, i, dev_ref: (a, i, 0))
    in_specs = [pl.BlockSpec((N_DEV, tr, c), lambda a, i, dev_ref, l=l: (0, rows_of(l, a, i), 0)) for l in range(nl)]
    in_specs += [pl.BlockSpec((None, tr, c), lambda a, i, dev_ref, l=l: (dev_ref[0], rows_of(l, a, i), 0))
                 for l in range(nl)]
    grid_spec = pltpu.PrefetchScalarGridSpec(
        num_scalar_prefetch=1, grid=(nl, r // tr), in_specs=in_specs + [spec] * 3, out_specs=[spec] * 4)
    return pl.pallas_call(
        body, name=name, grid_spec=grid_spec, out_shape=[_sds(w.shape, F32)] * 4,
        compiler_params=pltpu.CompilerParams(vmem_limit_bytes=VMEM_LIMIT))(dev, *lands, *parts, w, m, v)


def _sum8(name, parts):
    _, r, c = parts.shape
    tr = _row_tile(r)

    def body(p_ref, o_ref):
        acc = p_ref[0]
        for s in range(1, N_DEV):
            acc = acc + p_ref[s]
        o_ref[...] = acc

    return _call(body, name, (r // tr,), [pl.BlockSpec((N_DEV, tr, c), lambda i: (0, i, 0))],
                 pl.BlockSpec((tr, c), lambda i: (i, 0)), _sds((r, c), F32))(parts)


def _adamw_small(name, gs, ws, ms, vs):
    n = len(gs)

    def body(*refs):
        g, w, m, v = (refs[k * n:(k + 1) * n] for k in range(4))
        d_out, m_out, v_out = (refs[(4 + k) * n:(5 + k) * n] for k in range(3))
        for k in range(n):
            d_out[k][...], m_out[k][...], v_out[k][...] = _adamw(g[k][...], w[k][...], m[k][...], v[k][...])

    vmem = pl.BlockSpec(memory_space=pltpu.VMEM)
    outs = _call(body, name, (), [vmem] * (4 * n), [vmem] * (3 * n), [_sds(w.shape, F32) for w in ws] * 3)(
        *gs, *ws, *ms, *vs)
    return outs[:n], outs[n:2 * n], outs[2 * n:]


def _pack(arrs, row_multiple=SUBLANES):
    pieces, rows = [], 0
    for a in arrs:
        piece = a.reshape(-1, LANES)
        piece = jnp.pad(piece, ((0, (-piece.shape[0]) % SUBLANES), (0, 0)))
        pieces.append(piece)
        rows += piece.shape[0]
    if rows % row_multiple:
        pieces.append(jnp.zeros(((-rows) % row_multiple, LANES), pieces[0].dtype))
    return jnp.concatenate(pieces, axis=0)


def _unpack(buf, shapes, lead=0):
    out, pos = [], 0
    for shp in shapes:
        rows = math.prod(shp) // LANES
        piece = lax.slice_in_dim(buf, pos, pos + rows, axis=lead)
        out.append(piece.reshape(buf.shape[:lead] + tuple(shp)))
        pos += rows + (-rows) % SUBLANES
    return out


REPLICATED = ["conv_b_in", "conv_b_dw", "conv_ln_g", "conv_ln_b", "conv_b_out", "gmlp_w_s", "gmlp_b_s",
              "ffn_b_up", "ffn_b_dw", "ffn_b_down", "norm1_g", "norm1_b", "norm2_g", "norm2_b"]
SMALL_SHARDED = ["conv_w_dw", "gmlp_b_in", "gmlp_ln_g", "gmlp_ln_b", "gmlp_b_out", "ffn_w_dw"]
BIG = ["conv_w_in", "conv_w_out", "gmlp_w_in", "gmlp_w_out", "ffn_w_up", "ffn_w_down"]
WEIGHTS = ["conv_w_in", "conv_b_in", "conv_w_dw", "conv_b_dw", "conv_ln_g", "conv_ln_b", "conv_w_out", "conv_b_out",
           "gmlp_w_in", "gmlp_b_in", "gmlp_ln_g", "gmlp_ln_b", "gmlp_w_s", "gmlp_b_s", "gmlp_w_out", "gmlp_b_out",
           "ffn_w_up", "ffn_b_up", "ffn_w_dw", "ffn_b_dw", "ffn_w_down", "ffn_b_down",
           "norm1_g", "norm1_b", "norm2_g", "norm2_b"]


def _from_shards(g, lead_shape):
    nd = len(lead_shape)
    perm = tuple(range(1, nd + 1)) + (0, nd + 1)
    return g.transpose(perm).reshape(tuple(lead_shape) + (-1,))


def _to_shards(full, width):
    lead = full.shape[:-1]
    nd = len(lead)
    parts = full.reshape(lead + (N_DEV, width))
    return parts.transpose((nd,) + tuple(range(nd)) + (nd + 1,))


def _step(p):
    x_in, target_in = p["x"], p["loss_target"]
    bsz, seq, d = x_in.shape
    t = bsz * seq
    assert seq % TM_EW == 0 and TM_EW % CHUNK == 0 and TM_EW // SUBLANES >= CONV_K - 1
    x0 = _to_segments(x_in.reshape(t, d), TM_EW)
    target = _to_segments(target_in.reshape(t, d), TM_EW)
    n_conv, n_gmlp = p["conv_w_in"].shape[0], p["gmlp_w_in"].shape[0]
    fb = p["ffn_w_up"].shape[-1]
    nblk = N_DEV
    half = nblk // 2
    cw = p["conv_w_in"].shape[-1]
    tm = min(TM_MM, t)
    nt = t // tm
    tk = min(TK_DW, t)
    ntk = t // tk
    dev = 4 * lax.axis_index("x") + 2 * lax.axis_index("y") + lax.axis_index("c")

    small_shapes = [p[n].shape for n in SMALL_SHARDED]
    small_src = _pack([p[n] for n in SMALL_SHARDED])[None]
    small_all = _all_gather("gather_small_weights", [small_src])[0][0]
    sm = _unpack(small_all, small_shapes, lead=1)
    w_src = []
    for i in range(DEPTH):
        mix = "conv" if i % 2 == 0 else "gmlp"
        w_src += [p[mix + "_w_in"][i // 2].astype(BF16), p[mix + "_w_out"][i // 2].astype(BF16),
                  p["ffn_w_up"][i].T.astype(BF16), p["ffn_w_down"][i].astype(BF16)]
    send_sems, recv_sems, w_land, _ = _gather_start(
        "weights_gather_start", _place_own("weights_place_own", w_src, deps=[small_all]))
    W_IN, W_OUT, W_UP, W_DOWN = range(4)

    def wait_weight(i, k, after):
        return _gather_wait(f"l{i}_weights_wait{k}", w_land[4 * i + k], send_sems, recv_sems, 4 * i + k, after)
    conv_w_dw = _from_shards(sm[0], sm[0].shape[1:-1])
    gmlp_b_in = _from_shards(sm[1], sm[1].shape[1:-1])
    gmlp_ln_g = _from_shards(sm[2], sm[2].shape[1:-1])
    gmlp_ln_b = _from_shards(sm[3], sm[3].shape[1:-1])
    gmlp_b_out = _from_shards(sm[4], sm[4].shape[1:-1])
    ffn_w_dw = sm[5].transpose(1, 0, 2, 3)

    def rows3(a):
        return a.reshape(a.shape[0], 1, a.shape[-1])

    conv_b_in4 = p["conv_b_in"].reshape(n_conv, N_DEV, 1, cw)
    gmlp_b_in4 = gmlp_b_in.reshape(n_gmlp, N_DEV, 1, cw)
    ffn_b_up4 = p["ffn_b_up"].reshape(DEPTH, nblk, 1, fb)
    ffn_b_dw4 = p["ffn_b_dw"].reshape(DEPTH, nblk, 1, fb)
    conv_b_dw3, conv_ln_g3, conv_ln_b3 = rows3(p["conv_b_dw"]), rows3(p["conv_ln_g"]), rows3(p["conv_ln_b"])
    conv_b_out3, gmlp_b_out3, ffn_b_down3 = rows3(p["conv_b_out"]), rows3(gmlp_b_out), rows3(p["ffn_b_down"])
    gmlp_ln_g3, gmlp_ln_b3 = rows3(gmlp_ln_g), rows3(gmlp_ln_b)
    n1g3, n1b3, n2g3, n2b3 = rows3(p["norm1_g"]), rows3(p["norm1_b"]), rows3(p["norm2_g"]), rows3(p["norm2_b"])
    w_tile, bs_tile = _sgu_operands(p["gmlp_w_s"], p["gmlp_b_s"])

    def mm_in(name, xa, wg, l, bias4, glu=False):
        tmi = min(TM_LN, t)
        c_half = half * cw

        def body(a_ref, b_ref, bias_ref, h_ref, *u_ref):
            xb = a_ref[...].astype(BF16)
            for n in range(N_DEV):
                h_ref[:, n * cw:(n + 1) * cw] = jnp.dot(xb, b_ref[n], preferred_element_type=F32) + bias_ref[n]
            if glu:
                u_ref[0][...] = h_ref[:, :c_half] * _sigmoid(h_ref[:, c_half:])

        outs = _call(body, name, (t // tmi,),
                     [pl.BlockSpec((tmi, d), lambda i: (i, 0)), pl.BlockSpec((N_DEV, d, cw), lambda i: (0, 0, 0)),
                      pl.BlockSpec((None, N_DEV, 1, cw), lambda i: (l, 0, 0, 0))],
                     [pl.BlockSpec((tmi, N_DEV * cw), lambda i: (i, 0))]
                     + ([pl.BlockSpec((tmi, c_half), lambda i: (i, 0))] if glu else []),
                     [_sds((t, N_DEV * cw), F32)] + ([_sds((t, c_half), F32)] if glu else []))(xa, wg, bias4)
        return outs if glu else outs[0]

    def mm_out_dx(name, dy, w, deps=()):
        return _matmul(name, dy, w, "nt", grid=(nt,),
                       a_spec=pl.BlockSpec((tm, d), lambda i: (i, 0)),
                       b_spec=pl.BlockSpec((d, d), lambda i: (0, 0)),
                       o_spec=pl.BlockSpec((tm, d), lambda i: (i, 0)), o_shape=(t, d), o_dtype=F32, deps=deps)

    def mm_out_dw(name, sa, dy):
        return _matmul(name, sa, dy, "tn", grid=(nt,), k_axis=0, nk=nt, acc_shape=(d, d),
                       a_spec=pl.BlockSpec((tm, d), lambda k: (k, 0)),
                       b_spec=pl.BlockSpec((tm, d), lambda k: (k, 0)),
                       o_spec=pl.BlockSpec((d, d), lambda k: (0, 0)), o_shape=(d, d), o_dtype=BF16)

    def mm_in_dx(name, dh, wg, res, norm=None):
        tmx = min(TM_LN, t)

        def body(a_ref, b_ref, res_ref, *refs):
            y = ALPHA * res_ref[...]
            for n in range(N_DEV):
                y = y + lax.dot_general(a_ref[:, n * cw:(n + 1) * cw], b_ref[n], (_DIMS["nt"], ((), ())),
                                        preferred_element_type=F32)
            if norm is None:
                refs[0][...] = y
            else:
                _ln_bwd_rows(y, *refs, pl.program_id(0) == 0)

        row = pl.BlockSpec((tmx, d), lambda i: (i, 0))
        ins = [pl.BlockSpec((tmx, N_DEV * cw), lambda i: (i, 0)), pl.BlockSpec((N_DEV, d, cw), lambda i: (0, 0, 0)), row]
        if norm is None:
            return _call(body, name, (t // tmx,), ins, row, _sds((t, d), F32))(dh, wg, res)
        ln_ins, ln_outs = _ln_bwd_specs(tmx, d, norm[3], lambda i: (i, 0))
        return _call(body, name, (t // tmx,), ins + ln_ins, ln_outs,
                     [_sds((t, d), F32)] + [_sds((1, d), F32)] * 3)(dh, wg, res, *norm[:3])

    def mm_in_dw(name, xa, dh):
        def body(a_ref, b_ref, o_ref, acc_ref):
            k = pl.program_id(1)
            p = lax.dot_general(a_ref[...].astype(BF16), b_ref[...], (_DIMS["tn"], ((), ())),
                                preferred_element_type=F32)
            _acc_rows(acc_ref, p, k == 0)

            @pl.when(k == nt - 1)
            def _():
                for n in range(half):
                    o_ref[n] = acc_ref[:, n * cw:(n + 1) * cw].astype(BF16)

        return _call(body, name, (2, nt),
                     [pl.BlockSpec((tm, d), lambda c, k: (k, 0)), pl.BlockSpec((tm, half * cw), lambda c, k: (k, c))],
                     pl.BlockSpec((half, d, cw), lambda c, k: (c, 0, 0)), _sds((N_DEV, d, cw), BF16),
                     [pltpu.VMEM((d, half * cw), F32)])(xa, dh)

    def mm_down_dw(name, a, dy, deps=()):
        return _matmul(name, a, dy, "tn", grid=(half, ntk), k_axis=1, nk=ntk, acc_shape=(fb, d),
                       a_spec=pl.BlockSpec((None, tk, fb), lambda n, k: (n, k, 0)),
                       b_spec=pl.BlockSpec((tk, d), lambda n, k: (k, 0)),
                       o_spec=pl.BlockSpec((None, fb, d), lambda n, k: (n, 0, 0)),
                       o_shape=(half, fb, d), o_dtype=BF16, deps=deps)

    def mm_up_dw(name, xa, dh):
        return _matmul(name, dh, xa, "tn", grid=(nblk, ntk), k_axis=1, nk=ntk, acc_shape=(fb, d),
                       a_spec=pl.BlockSpec((None, tk, fb), lambda n, k: (n, k, 0)),
                       b_spec=pl.BlockSpec((tk, d), lambda n, k: (k, 0)),
                       o_spec=pl.BlockSpec((None, fb, d), lambda n, k: (n, 0, 0)),
                       o_shape=(nblk, fb, d), o_dtype=BF16)

    saved = []
    xcur = x0
    for i in range(DEPTH):
        j = i // 2
        s = {"x": xcur}
        s["w_in"] = wait_weight(i, W_IN, xcur if i else target)
        if i % 2 == 0:
            s["h"], s["u"] = mm_in(f"l{i}_conv_in_glu", xcur, s["w_in"], j, conv_b_in4, glu=True)
            s["c"] = _dwconv31(f"l{i}_dwconv", s["u"], conv_w_dw, conv_b_dw3, j, seq)
            s["s"] = _ln_silu(f"l{i}_ln_silu", s["c"], conv_ln_g3, conv_ln_b3, j)
            b_out3 = conv_b_out3
        else:
            s["h"] = mm_in(f"l{i}_gmlp_in", xcur, s["w_in"], j, gmlp_b_in4)
            s["s"] = _sgu(f"l{i}_sgu", s["h"], gmlp_ln_g3, gmlp_ln_b3, w_tile, bs_tile, j)
            b_out3 = gmlp_b_out3
        s["w_out"] = wait_weight(i, W_OUT, s["s"]).reshape(d, d)
        s["x1"], s["xhat1"], s["rstd1"] = _matmul_ln(
            f"l{i}_mixer_out_norm1", s["s"], s["w_out"], xcur, b_out3, n1g3, n1b3, j, i)
        s["w_up"] = wait_weight(i, W_UP, s["x1"])
        s["hg"], s["hv"], s["a"] = _ffn_up_act(f"l{i}_ffn_up_act", s["x1"], s["w_up"], ffn_b_up4, ffn_w_dw, ffn_b_dw4,
                                               i, seq)
        s["w_down"] = wait_weight(i, W_DOWN, s["a"]).reshape(half, fb, d)
        xcur, s["xhat2"], s["rstd2"] = _matmul_ln(
            f"l{i}_ffn_down_norm2", s["a"], s["w_down"], s["x1"], ffn_b_down3, n2g3, n2b3, i, i)
        saved.append(s)

    loss_row, dx = _loss("loss", xcur, target)

    started = {n: [None] * p[n].shape[0] for n in BIG}
    tokens = []

    def send_grads(name, items):
        done, token = _scatter_start(name, [g for _, _, g in items])
        for (n, l, _), st in zip(items, done):
            started[n][l] = st
        tokens.append(token)

    def take_tokens():
        out = list(tokens)
        tokens.clear()
        return out

    gl = {n: [None] * p[n].shape[0] for n in REPLICATED + SMALL_SHARDED}
    dr2, gl["norm2_g"][DEPTH - 1], gl["norm2_b"][DEPTH - 1], gl["ffn_b_down"][DEPTH - 1] = _ln_res_bwd(
        f"l{DEPTH - 1}_norm2_bwd", dx, saved[-1]["xhat2"], saved[-1]["rstd2"], n2g3, DEPTH - 1)
    for i in reversed(range(DEPTH)):
        j = i // 2
        s = saved[i]
        mix = "conv" if i % 2 == 0 else "gmlp"
        g_down = mm_down_dw(f"l{i}_ffn_down_dw", s["a"], dr2, deps=take_tokens()).reshape(N_DEV, -1, d)
        send_grads(f"l{i}_ffn_down_grad_scatter_start", [("ffn_w_down", i, g_down)])
        dcg, dcv, dbg, dbv, dwg, dwv = _ffn_act_bwd(f"l{i}_ffn_act_bwd", dr2, s["w_down"], s["hg"], s["hv"],
                                                    ffn_w_dw, ffn_b_dw4, i, seq, deps=take_tokens())
        gl["ffn_b_dw"][i] = jnp.concatenate([dbg, dbv], axis=0).reshape(1, nblk * fb)
        gl["ffn_w_dw"][i] = jnp.concatenate([dwg[:, :FFN_K], dwv[:, :FFN_K]], axis=0)
        dh, dbu, dr1, gl["norm1_g"][i], gl["norm1_b"][i], gl[mix + "_b_out"][j] = _ffn_conv_t_dx(
            f"l{i}_ffn_conv_t_dx", dcg, dcv, ffn_w_dw, s["w_up"], dr2, s["xhat1"], s["rstd1"], n1g3, i, seq,
            deps=take_tokens())
        gl["ffn_b_up"][i] = dbu.reshape(1, nblk * fb)
        send_grads(f"l{i}_ffn_up_grad_scatter_start", [("ffn_w_up", i, mm_up_dw(f"l{i}_ffn_up_dw", s["x1"], dh))])
        ds = mm_out_dx(f"l{i}_{mix}_out_dx", dr1, s["w_out"], deps=take_tokens())
        g_out = mm_out_dw(f"l{i}_{mix}_out_dw", s["s"], dr1).reshape(N_DEV, -1, d)
        if i % 2 == 0:
            dc, gl["conv_ln_g"][j], gl["conv_ln_b"][j], gl["conv_b_dw"][j] = _ln_silu_bwd(
                f"l{i}_ln_silu_bwd", ds, s["c"], conv_ln_g3, conv_ln_b3, j)
            du, dwdw = _dwconv31_bwd(f"l{i}_dwconv_bwd", dc, s["u"], conv_w_dw, j, seq)
            gl["conv_w_dw"][j] = dwdw[:CONV_K]
            dh, gl["conv_b_in"][j] = _glu_bwd(f"l{i}_glu_bwd", du, s["h"])
        else:
            dh, gl["gmlp_b_in"][j], dwt, dbt, gl["gmlp_ln_g"][j], gl["gmlp_ln_b"][j] = _sgu_bwd(
                f"l{i}_sgu_bwd", ds, s["h"], gmlp_ln_g3, gmlp_ln_b3, w_tile, bs_tile, j)
            gl["gmlp_w_s"][j], gl["gmlp_b_s"][j] = _sgu_param_grads(dwt, dbt)
            if i == 1:
                ws_local = jnp.stack(gl["gmlp_w_s"]).reshape(-1, LANES)
                ws_send, ws_recv, ws_land, ws_token = _gather_start(
                    "w_s_grads_gather_start", _place_own("w_s_grads_place_own", [ws_local]))
                tokens.append(ws_token)
        if i > 0:
            prev = saved[i - 1]
            dr2, gl["norm2_g"][i - 1], gl["norm2_b"][i - 1], gl["ffn_b_down"][i - 1] = mm_in_dx(
                f"l{i}_{mix}_in_dx_norm2_bwd", dh, s["w_in"], dr1, (prev["xhat2"], prev["rstd2"], n2g3, i - 1))
        else:
            dx = mm_in_dx(f"l{i}_{mix}_in_dx", dh, s["w_in"], dr1)
        send_grads(f"l{i}_mixer_grads_scatter_start",
                   [(mix + "_w_out", j, g_out), (mix + "_w_in", j, mm_in_dw(f"l{i}_{mix}_in_dw", s["x"], dh))])
    grad_x = _from_segments(dx, TM_EW).reshape(bsz, seq, d)

    late = [n for n in REPLICATED if n != "gmlp_w_s"]
    full_small = {n: jnp.stack(gl[n]).reshape(p[n].shape) for n in late}
    shard_small = {}
    for n in SMALL_SHARDED:
        if n == "ffn_w_dw":
            shard_small[n] = jnp.stack(gl[n]).transpose(1, 0, 2, 3)
        else:
            width = p[n].shape[-1]
            lead = p[n].shape[:-1]
            shard_small[n] = _to_shards(jnp.stack(gl[n]).reshape(lead + (N_DEV * width,)), width)
    flat_shapes = [(1, LANES)] + [p[n].shape for n in late] + [(N_DEV,) + p[n].shape for n in SMALL_SHARDED]
    flat_local = _pack([loss_row] + [full_small[n] for n in late] + [shard_small[n] for n in SMALL_SHARDED],
                       row_multiple=ROW_TILE_CAP)

    small_send, small_recv, small_land, small_token = _gather_start(
        "small_grads_gather_start", _place_own("small_grads_place_own", [flat_local]))

    grads, delta, new_m, new_v = {}, {}, {}, {}
    dev1 = jnp.reshape(dev, (1,)).astype(jnp.int32)
    order = ["ffn_w_down", "ffn_w_up", "gmlp_w_out", "gmlp_w_in", "conv_w_out", "conv_w_in"]
    after = small_token
    for n in order:
        parts_done, lands_done = _scatter_wait(f"grads_{n}_scatter_wait", started[n], after)
        state = [p[n], p["m_" + n], p["v_" + n]]
        if n == "ffn_w_up":
            state = [a.transpose(0, 2, 1) for a in state]
        outs = _sum8_adamw(f"adamw_{n}", dev1, lands_done, parts_done, *state)
        after = outs[-1]
        if n == "ffn_w_up":
            outs = [a.transpose(0, 2, 1) for a in outs]
        grads[n], delta[n], new_m[n], new_v[n] = outs

    ws_parts = _gather_wait("w_s_grads_gather_wait", ws_land[0], ws_send, ws_recv, 0, after)
    ws_sum = _sum8("sum_w_s_grads", ws_parts)
    grads["gmlp_w_s"] = ws_sum.reshape(p["gmlp_w_s"].shape)
    small_parts = _gather_wait("small_grads_gather_wait", small_land[0], small_send, small_recv, 0, ws_sum)
    summed = _unpack(_sum8("sum_small_grads", small_parts), flat_shapes)
    loss = summed[0][0, 0]
    grads.update(zip(late, summed[1:1 + len(late)]))
    for n, g in zip(SMALL_SHARDED, summed[1 + len(late):]):
        grads[n] = lax.dynamic_index_in_dim(g, dev, axis=0, keepdims=False)
    small = REPLICATED + SMALL_SHARDED
    d_s, m_s, v_s = _adamw_small("adamw_small", [grads[n] for n in small], [p[n] for n in small],
                                 [p["m_" + n] for n in small], [p["v_" + n] for n in small])
    for n, dd, mm, vv in zip(small, d_s, m_s, v_s):
        delta[n], new_m[n], new_v[n] = dd, mm, vv

    return (loss, grad_x, *[grads[n] for n in WEIGHTS], *[delta[n] for n in WEIGHTS],
            *[new_m[n] for n in WEIGHTS], *[new_v[n] for n in WEIGHTS])


def kernel(x, conv_w_in, conv_b_in, conv_w_dw, conv_b_dw, conv_ln_g, conv_ln_b, conv_w_out, conv_b_out, gmlp_w_in, gmlp_b_in, gmlp_ln_g, gmlp_ln_b, gmlp_w_s, gmlp_b_s, gmlp_w_out, gmlp_b_out, ffn_w_up, ffn_b_up, ffn_w_dw, ffn_b_dw, ffn_w_down, ffn_b_down, norm1_g, norm1_b, norm2_g, norm2_b, loss_target, m_conv_w_in, m_conv_b_in, m_conv_w_dw, m_conv_b_dw, m_conv_ln_g, m_conv_ln_b, m_conv_w_out, m_conv_b_out, m_gmlp_w_in, m_gmlp_b_in, m_gmlp_ln_g, m_gmlp_ln_b, m_gmlp_w_s, m_gmlp_b_s, m_gmlp_w_out, m_gmlp_b_out, m_ffn_w_up, m_ffn_b_up, m_ffn_w_dw, m_ffn_b_dw, m_ffn_w_down, m_ffn_b_down, m_norm1_g, m_norm1_b, m_norm2_g, m_norm2_b, v_conv_w_in, v_conv_b_in, v_conv_w_dw, v_conv_b_dw, v_conv_ln_g, v_conv_ln_b, v_conv_w_out, v_conv_b_out, v_gmlp_w_in, v_gmlp_b_in, v_gmlp_ln_g, v_gmlp_ln_b, v_gmlp_w_s, v_gmlp_b_s, v_gmlp_w_out, v_gmlp_b_out, v_ffn_w_up, v_ffn_b_up, v_ffn_w_dw, v_ffn_b_dw, v_ffn_w_down, v_ffn_b_down, v_norm1_g, v_norm1_b, v_norm2_g, v_norm2_b):
    return _step(dict(locals()))
```

```python
import math

import jax
import jax.numpy as jnp
from jax import lax
from jax.experimental import pallas as pl
from jax.experimental.pallas import tpu as pltpu

F32 = jnp.float32
BF16 = jnp.bfloat16
MESH = pl.DeviceIdType.MESH

N_DEV = 8
DEPTH = 4
ALPHA = (2.0 * DEPTH) ** 0.25
LN_EPS = 1e-5
CONV_K = 31
FFN_K = 3
CHUNK = 128
GROUPS = 8
ADAM_LR = 0.001
ADAM_B1 = 0.9
ADAM_B2 = 0.999
ADAM_EPS = 1e-08
ADAM_WD = 0.01
ADAM_STEP = 10
INV_SQRT2 = 1.0 / math.sqrt(2.0)
INV_SQRT2PI = 1.0 / math.sqrt(2.0 * math.pi)

LANES = 128
SUBLANES = 8
VMEM_LIMIT = 56 * 1024 * 1024
TM_MM = 1024
TK_DW = 2048
TM_EW = 256
TM_ROW = 512


def _call(body, name, grid, in_specs, out_specs, out_shape, scratch=(), aliases=None, deps=()):
    deps = list(deps)
    in_specs = list(in_specs)
    n_in = len(in_specs)
    if deps:
        inner = body

        def body(*refs):
            return inner(*refs[:n_in], *refs[n_in + len(deps):])

        in_specs = in_specs + [pl.BlockSpec(memory_space=pl.ANY)] * len(deps)
    fn = pl.pallas_call(
        body, name=name, grid=grid, in_specs=in_specs, out_specs=out_specs, out_shape=out_shape,
        scratch_shapes=list(scratch), input_output_aliases=aliases or {},
        compiler_params=pltpu.CompilerParams(vmem_limit_bytes=VMEM_LIMIT))
    return lambda *args: fn(*args, *deps)


def _sds(shape, dtype):
    return jax.ShapeDtypeStruct(tuple(shape), dtype)


def _sigmoid(x):
    return 1.0 / (1.0 + jnp.exp(-x))


def _acc_rows(ref, val, first):
    @pl.when(first)
    def _():
        ref[...] = val

    @pl.when(jnp.logical_not(first))
    def _():
        ref[...] += val


def _colsum(v):
    return jnp.sum(v, axis=0, keepdims=True)


_DIMS = {"nn": ((1,), (0,)), "nt": ((1,), (1,)), "tn": ((0,), (0,))}


def _matmul(name, a, b, mode, *, grid, a_spec, b_spec, o_spec, o_shape, o_dtype, k_axis=None, nk=1,
            acc_shape=None, bias=None, bias_spec=None, res=None, res_spec=None, res_scale=1.0, deps=()):
    dims = (_DIMS[mode], ((), ()))
    has_bias, has_res = bias is not None, res is not None

    def body(*refs):
        a_ref, b_ref = refs[0], refs[1]
        pos = 2
        bias_ref = res_ref = None
        if has_bias:
            bias_ref = refs[pos]
            pos += 1
        if has_res:
            res_ref = refs[pos]
            pos += 1
        o_ref = refs[pos]
        acc_ref = refs[pos + 1] if nk > 1 else None
        p = lax.dot_general(a_ref[...].astype(BF16), b_ref[...].astype(BF16), dims, preferred_element_type=F32)

        def finish(acc):
            if has_bias:
                acc = acc + bias_ref[...]
            if has_res:
                acc = acc + res_scale * res_ref[...]
            o_ref[...] = acc.astype(o_dtype)

        if nk == 1:
            finish(p)
        else:
            k = pl.program_id(k_axis)

            @pl.when(k == 0)
            def _():
                acc_ref[...] = p

            @pl.when(k > 0)
            def _():
                acc_ref[...] += p

            @pl.when(k == nk - 1)
            def _():
                finish(acc_ref[...])

    ins, specs = [a, b], [a_spec, b_spec]
    if has_bias:
        ins.append(bias)
        specs.append(bias_spec)
    if has_res:
        ins.append(res)
        specs.append(res_spec)
    scratch = [pltpu.VMEM(acc_shape, F32)] if nk > 1 else []
    return _call(body, name, grid, specs, o_spec, _sds(o_shape, o_dtype), scratch, deps=deps)(*ins)


TM_LN = 512


def _matmul_ln(name, a, b, x_res, bias3, g3, b3, l_bias, l_norm):
    t, d = x_res.shape
    tm = min(TM_LN, t)
    blocked = a.ndim == 3

    def body(a_ref, b_ref, x_ref, bias_ref, g_ref, be_ref, o_ref, ob_ref, xh_ref, rs_ref):
        if blocked:
            y = None
            for k in range(a.shape[0]):
                p = jnp.dot(a_ref[k], b_ref[k], preferred_element_type=F32)
                y = p if y is None else y + p
        else:
            y = jnp.dot(a_ref[...], b_ref[...], preferred_element_type=F32)
        xhat, rstd = _ln_stats(ALPHA * x_ref[...] + y + bias_ref[...])
        out = xhat * g_ref[...] + be_ref[...]
        o_ref[...] = out
        ob_ref[...] = out.astype(BF16)
        xh_ref[...] = xhat
        rs_ref[...] = rstd

    if blocked:
        a_spec = pl.BlockSpec((a.shape[0], tm, a.shape[2]), lambda i: (0, i, 0))
        b_spec = pl.BlockSpec(b.shape, lambda i: (0, 0, 0))
    else:
        a_spec = pl.BlockSpec((tm, a.shape[1]), lambda i: (i, 0))
        b_spec = pl.BlockSpec(b.shape, lambda i: (0, 0))
    row = pl.BlockSpec((tm, d), lambda i: (i, 0))
    stat = pl.BlockSpec((tm, 1), lambda i: (i, 0))

    def vec(l):
        return pl.BlockSpec((None, 1, d), lambda i: (l, 0, 0))

    return _call(body, name, (t // tm,), [a_spec, b_spec, row, vec(l_bias), vec(l_norm), vec(l_norm)],
                 [row, row, row, stat],
                 [_sds((t, d), F32), _sds((t, d), BF16), _sds((t, d), F32), _sds((t, 1), F32)])(
                     a, b, x_res, bias3, g3, b3)


def _mesh_pos():
    return lax.axis_index("x"), lax.axis_index("y"), lax.axis_index("c")


def _any_specs(n):
    return [pl.BlockSpec(memory_space=pl.ANY)] * n


def _all_gather(name, srcs):
    n = len(srcs)

    def body(*refs):
        src, out = refs[:n], refs[n:2 * n]
        send_sems, recv_sems, local_sems = refs[2 * n:]
        x, y, c = _mesh_pos()
        me, sibling = (x, y, c), (x, y, 1 - c)
        chips = [(1 - x, y), (x, 1 - y), (1 - x, 1 - y)]

        def slot(k, p):
            return out[k].at[:, 4 * p[0] + 2 * p[1] + p[2]]

        def copy(k, idx, block, to, s=None):
            return pltpu.make_async_remote_copy(
                src_ref=slot(k, block) if s is None else s, dst_ref=slot(k, block),
                send_sem=send_sems.at[k * 7 + idx], recv_sem=recv_sems.at[k * 7 + idx],
                device_id=to, device_id_type=MESH)

        local = [pltpu.make_async_copy(src[k], slot(k, me), local_sems.at[k]) for k in range(n)]
        for cp in local:
            cp.start()
        first = []
        for k in range(n):
            first.append(copy(k, 0, me, sibling, src[k]))
            for j, chip in enumerate(chips):
                first.append(copy(k, 1 + j, me, (*chip, c), src[k]))
        for cp in first:
            cp.start()
        passed = []
        for j, chip in enumerate(chips):
            for k in range(n):
                copy(k, 1 + j, (*chip, c), me).wait_recv()
                cp = copy(k, 4 + j, (*chip, c), sibling)
                cp.start()
                passed.append(cp)
        for k in range(n):
            copy(k, 0, sibling, me).wait_recv()
            for j, chip in enumerate(chips):
                copy(k, 4 + j, (*chip, 1 - c), me).wait_recv()
        for cp in first + passed:
            cp.wait_send()
        for cp in local:
            cp.wait()

    out_shape = [_sds((s.shape[0], N_DEV) + s.shape[1:], s.dtype) for s in srcs]
    return _call(body, name, (), [pl.BlockSpec(memory_space=pltpu.VMEM)] * n, _any_specs(n), out_shape,
                 [pltpu.SemaphoreType.DMA((7 * n,)), pltpu.SemaphoreType.DMA((7 * n,)),
                  pltpu.SemaphoreType.DMA((n,))])(*srcs)


HBM_SPEC = pl.BlockSpec(memory_space=pltpu.HBM)
SEM_SPEC = pl.BlockSpec(memory_space=pltpu.SEMAPHORE)
N_PEER = N_DEV - 1


def _split_call(body, name, in_specs, out_specs, out_shape, aliases):
    return pl.pallas_call(
        body, name=name, in_specs=in_specs, out_specs=out_specs, out_shape=out_shape, input_output_aliases=aliases,
        compiler_params=pltpu.CompilerParams(has_side_effects=pltpu.SideEffectType.DATAFLOW_SIDE_EFFECTING))


def _peers(x, y, c):
    return [(1 - x if q & 4 else x, 1 - y if q & 2 else y, 1 - c if q & 1 else c) for q in range(1, N_DEV)]


def _in_hbm(a):
    return pltpu.with_memory_space_constraint(a, pltpu.HBM)


def _place_own(name, srcs, deps=()):
    n = len(srcs)

    def body(*refs):
        src, out, sems = refs[:n], refs[n:2 * n], refs[2 * n]
        x, y, c = _mesh_pos()
        dev = 4 * x + 2 * y + c
        copies = [pltpu.make_async_copy(src[k], out[k].at[dev], sems.at[k]) for k in range(n)]
        for cp in copies:
            cp.start()
        for cp in copies:
            cp.wait()

    return _call(body, name, (), [pl.BlockSpec(memory_space=pltpu.VMEM)] * n, _any_specs(n),
                 [_sds((N_DEV,) + s.shape, s.dtype) for s in srcs], [pltpu.SemaphoreType.DMA((n,))],
                 deps=deps)(*srcs)


def _gather_start(name, lands):
    n = len(lands)

    def body(*refs):
        land, send_sems, recv_sems = refs[:n], refs[n], refs[n + 1]
        x, y, c = _mesh_pos()
        dev = 4 * x + 2 * y + c
        for k in range(n):
            for peer in _peers(x, y, c):
                pltpu.make_async_remote_copy(
                    src_ref=land[k].at[dev], dst_ref=land[k].at[dev], send_sem=send_sems.at[k],
                    recv_sem=recv_sems.at[k], device_id=peer, device_id_type=MESH).start()
        token = refs[-1]
        token[...] = jnp.zeros_like(token)

    outs = _split_call(
        body, name, [HBM_SPEC] * n, [SEM_SPEC, SEM_SPEC] + [HBM_SPEC] * n + [pl.BlockSpec(memory_space=pltpu.VMEM)],
        [pltpu.SemaphoreType.DMA((n,)), pltpu.SemaphoreType.DMA((n,))] + [pltpu.HBM(a.shape, a.dtype) for a in lands]
        + [_sds((SUBLANES, LANES), F32)],
        {k: 2 + k for k in range(n)})(*[_in_hbm(a) for a in lands])
    return outs[0], outs[1], list(outs[2:2 + n]), outs[-1]


def _wait_seven(src_ref, dst_ref, send_sem, recv_sem):
    cp = pltpu.make_async_remote_copy(
        src_ref=src_ref.at[pl.ds(0, N_PEER)], dst_ref=dst_ref.at[pl.ds(0, N_PEER)], send_sem=send_sem,
        recv_sem=recv_sem, device_id=_mesh_pos(), device_id_type=MESH)
    cp.wait_send()
    cp.wait_recv()


def _gather_wait(name, land, send_sems, recv_sems, k, after):
    def body(land_ref, send_ref, recv_ref, after_ref, out_ref):
        _wait_seven(land_ref, land_ref, send_ref.at[k], recv_ref.at[k])

    return _split_call(body, name, [HBM_SPEC, SEM_SPEC, SEM_SPEC, pl.BlockSpec(memory_space=pl.ANY)], HBM_SPEC,
                       pltpu.HBM(land.shape, land.dtype), {0: 0})(land, send_sems, recv_sems, after)


def _scatter_start(name, parts_list):
    n = len(parts_list)

    def body(*refs):
        x, y, c = _mesh_pos()
        dev = 4 * x + 2 * y + c
        for k in range(n):
            parts_ref, land_ref = refs[2 * k], refs[2 * k + 1]
            send_sem, recv_sem = refs[2 * n + 4 * k], refs[2 * n + 4 * k + 1]
            for peer in _peers(x, y, c):
                pltpu.make_async_remote_copy(
                    src_ref=parts_ref.at[4 * peer[0] + 2 * peer[1] + peer[2]], dst_ref=land_ref.at[dev],
                    send_sem=send_sem, recv_sem=recv_sem, device_id=peer, device_id_type=MESH).start()
        token = refs[-1]
        token[...] = jnp.zeros_like(token)

    ins, out_specs, out_shape, aliases = [], [], [], {}
    for k, parts in enumerate(parts_list):
        buf = pltpu.HBM(parts.shape, parts.dtype)
        ins += [_in_hbm(parts), _in_hbm(lax.empty(parts.shape, parts.dtype))]
        out_specs += [SEM_SPEC, SEM_SPEC, HBM_SPEC, HBM_SPEC]
        out_shape += [pltpu.SemaphoreType.DMA(()), pltpu.SemaphoreType.DMA(()), buf, buf]
        aliases.update({2 * k: 4 * k + 2, 2 * k + 1: 4 * k + 3})
    outs = _split_call(body, name, [HBM_SPEC] * (2 * n), out_specs + [pl.BlockSpec(memory_space=pltpu.VMEM)],
                       out_shape + [_sds((SUBLANES, LANES), F32)], aliases)(*ins)
    return [tuple(outs[4 * k:4 * k + 4]) for k in range(n)], outs[-1]


def _scatter_wait(name, started, after):
    n = len(started)

    def body(*refs):
        for k in range(n):
            send_sem, recv_sem, parts_ref, land_ref = refs[4 * k:4 * k + 4]
            _wait_seven(parts_ref, land_ref, send_sem, recv_sem)

    flat = [a for s in started for a in s]
    outs = _split_call(
        body, name, [SEM_SPEC, SEM_SPEC, HBM_SPEC, HBM_SPEC] * n + [pl.BlockSpec(memory_space=pl.ANY)],
        [HBM_SPEC, HBM_SPEC] * n, [pltpu.HBM(a.shape, a.dtype) for s in started for a in s[2:]],
        {4 * k + 2 + t: 2 * k + t for k in range(n) for t in range(2)})(*flat, after)
    return list(outs[0::2]), list(outs[1::2])


def _to_segments(a, tile):
    seg = tile // SUBLANES
    return a.reshape((a.shape[0] // tile, SUBLANES, seg) + a.shape[1:]).swapaxes(1, 2).reshape(a.shape)


def _from_segments(a, tile):
    seg = tile // SUBLANES
    return a.reshape((a.shape[0] // tile, seg, SUBLANES) + a.shape[1:]).swapaxes(1, 2).reshape(a.shape)


def _chunk(ref, q):
    return ref[q * SUBLANES:(q + 1) * SUBLANES, :]


def _fill_wrap_prev(x_ref, halo_ref, wrap_ref, n_wrap, n_halo, seg, keep):
    sub = lax.broadcasted_iota(jnp.int32, (SUBLANES, x_ref.shape[-1]), 0)
    for j in range(n_wrap):
        q = seg - n_wrap + j
        hq = q - (seg - n_halo)
        row = halo_ref[hq * SUBLANES + SUBLANES - 1:(hq + 1) * SUBLANES, :] * keep
        wrap_ref[j * SUBLANES:(j + 1) * SUBLANES, :] = jnp.where(sub == 0, row, pltpu.roll(_chunk(x_ref, q), 1, 0))


def _fill_wrap_next(x_ref, halo_ref, wrap_ref, n_wrap, keep):
    sub = lax.broadcasted_iota(jnp.int32, (SUBLANES, x_ref.shape[-1]), 0)
    for j in range(n_wrap):
        row = halo_ref[j * SUBLANES:j * SUBLANES + 1, :] * keep
        wrap_ref[j * SUBLANES:(j + 1) * SUBLANES, :] = jnp.where(
            sub == SUBLANES - 1, row, pltpu.roll(_chunk(x_ref, j), SUBLANES - 1, 0))


def _past(x_ref, wrap_ref, q, d, n_wrap):
    return _chunk(x_ref, q - d) if q >= d else _chunk(wrap_ref, q - d + n_wrap)


def _future(x_ref, wrap_ref, q, d, seg):
    return _chunk(x_ref, q + d) if q + d < seg else _chunk(wrap_ref, q + d - seg)


def _conv_fwd(x_ref, wrap_ref, w_ref, b_ref, out_ref, seg, k_taps):
    bias = jnp.broadcast_to(b_ref[...], (SUBLANES, x_ref.shape[-1]))
    for q in range(seg):
        acc = bias
        for k in range(k_taps):
            acc = acc + w_ref[k:k + 1, :] * _past(x_ref, wrap_ref, q, k_taps - 1 - k, k_taps - 1)
        out_ref[q * SUBLANES:(q + 1) * SUBLANES, :] = acc


def _conv_bwd_data(d_ref, wrap_ref, w_ref, out_ref, seg, k_taps):
    for q in range(seg):
        acc = None
        for k in range(k_taps):
            term = w_ref[k:k + 1, :] * _future(d_ref, wrap_ref, q, k_taps - 1 - k, seg)
            acc = term if acc is None else acc + term
        out_ref[q * SUBLANES:(q + 1) * SUBLANES, :] = acc


def _conv_bwd_taps(d_ref, x_ref, wrap_ref, dw_ref, seg, k_taps):
    for k in range(k_taps):
        part = None
        for q in range(seg):
            term = _chunk(d_ref, q) * _past(x_ref, wrap_ref, q, k_taps - 1 - k, k_taps - 1)
            part = term if part is None else part + term
        dw_ref[k:k + 1, :] += _colsum(part)


def _tile_halo_specs(tm, width_block, n_halo, n_tiles, block_of):
    rows = n_halo * SUBLANES
    per = tm // rows
    tile = pl.BlockSpec(width_block(tm), lambda n, i: block_of(n, i))
    prev = pl.BlockSpec(width_block(rows), lambda n, i: block_of(n, jnp.maximum(i * per - 1, 0)))
    nxt = pl.BlockSpec(width_block(rows), lambda n, i: block_of(n, jnp.minimum((i + 1) * per, n_tiles * per - 1)))
    return tile, prev, nxt


def _ln_stats(v):
    mu = jnp.mean(v, axis=-1, keepdims=True)
    vc = v - mu
    var = jnp.mean(vc * vc, axis=-1, keepdims=True)
    rstd = lax.rsqrt(var + LN_EPS)
    return vc * rstd, rstd


def _ln_backward(dxhat, xhat, rstd):
    m1 = jnp.mean(dxhat, axis=-1, keepdims=True)
    m2 = jnp.mean(dxhat * xhat, axis=-1, keepdims=True)
    return rstd * (dxhat - m1 - xhat * m2)


def _row_spec(tm, width):
    return pl.BlockSpec((tm, width), lambda i: (i, 0))


def _param_spec(l, width):
    return pl.BlockSpec((None, 1, width), lambda *_: (l, 0, 0))


def _ln_bwd_rows(dout, xh_ref, rs_ref, g_ref, dr_ref, dg_ref, db_ref, dsum_ref, first):
    xhat = xh_ref[...]
    dr = _ln_backward(dout * g_ref[...], xhat, rs_ref[...])
    dr_ref[...] = dr
    _acc_rows(dg_ref, _colsum(dout * xhat), first)
    _acc_rows(db_ref, _colsum(dout), first)
    _acc_rows(dsum_ref, _colsum(dr), first)


def _ln_bwd_specs(tm, d, l, row_of):
    vec = pl.BlockSpec((1, d), lambda *_: (0, 0))
    ins = [pl.BlockSpec((tm, d), row_of), pl.BlockSpec((tm, 1), row_of), _param_spec(l, d)]
    return ins, [pl.BlockSpec((tm, d), row_of), vec, vec, vec]


def _ln_res_bwd(name, dout, xhat, rstd, g3, l, deps=()):
    t, d = dout.shape
    tm = min(TM_ROW, t)

    def body(do_ref, xh_ref, rs_ref, g_ref, dr_ref, dg_ref, db_ref, dc_ref):
        _ln_bwd_rows(do_ref[...], xh_ref, rs_ref, g_ref, dr_ref, dg_ref, db_ref, dc_ref, pl.program_id(0) == 0)

    ins, outs = _ln_bwd_specs(tm, d, l, lambda i: (i, 0))
    return _call(body, name, (t // tm,), [_row_spec(tm, d)] + ins, outs,
                 [_sds((t, d), F32)] + [_sds((1, d), F32)] * 3, deps=deps)(dout, xhat, rstd, g3)


def _glu_bwd(name, du, h):
    t, c2 = h.shape
    c = c2 // 2
    tm = min(TM_ROW, t)

    def body(du_ref, a_ref, g_ref, dh_ref, db_ref):
        first = pl.program_id(0) == 0
        du_v, a = du_ref[...], a_ref[...]
        sg = _sigmoid(g_ref[...])
        da = du_v * sg
        dg = du_v * a * sg * (1.0 - sg)
        dh_ref[:, :c] = da.astype(BF16)
        dh_ref[:, c:] = dg.astype(BF16)
        _acc_rows(db_ref.at[:, :c], _colsum(da), first)
        _acc_rows(db_ref.at[:, c:], _colsum(dg), first)

    return _call(body, name, (t // tm,),
                 [_row_spec(tm, c), pl.BlockSpec((tm, c), lambda i: (i, 0)), pl.BlockSpec((tm, c), lambda i: (i, 1))],
                 [_row_spec(tm, c2), pl.BlockSpec((1, c2), lambda i: (0, 0))],
                 [_sds((t, c2), BF16), _sds((1, c2), F32)])(du, h, h)


CONV_CB = 512
TAPS_PAD = 32


def _dwconv31(name, u, w3, b3, l, seq):
    t, c = u.shape
    tm, cb = TM_EW, CONV_CB
    seg, seq_tiles, n_tiles = tm // SUBLANES, seq // tm, t // tm
    n_wrap = CONV_K - 1
    tile, prev, _ = _tile_halo_specs(tm, lambda rows: (rows, cb), seg, n_tiles, lambda n, r: (r, n))

    def body(u_ref, halo_ref, w_ref, b_ref, o_ref, wrap_ref):
        keep = (pl.program_id(1) % seq_tiles != 0).astype(F32)
        _fill_wrap_prev(u_ref, halo_ref, wrap_ref, n_wrap, seg, seg, keep)
        _conv_fwd(u_ref, wrap_ref, w_ref, b_ref, o_ref, seg, CONV_K)

    return _call(body, name, (c // cb, n_tiles),
                 [tile, prev, pl.BlockSpec((None, CONV_K, cb), lambda n, i: (l, 0, n)),
                  pl.BlockSpec((None, 1, cb), lambda n, i: (l, 0, n))],
                 tile, _sds((t, c), F32), [pltpu.VMEM((n_wrap * SUBLANES, cb), F32)])(u, u, w3, b3)


def _dwconv31_bwd(name, dc, u, w3, l, seq):
    t, c = dc.shape
    tm, cb = TM_EW, CONV_CB
    seg, seq_tiles, n_tiles = tm // SUBLANES, seq // tm, t // tm
    n_wrap = CONV_K - 1
    tile, prev, nxt = _tile_halo_specs(tm, lambda rows: (rows, cb), seg, n_tiles, lambda n, r: (r, n))

    def body(dc_ref, dcn_ref, u_ref, up_ref, w_ref, du_ref, dw_ref, dwrap_ref, uwrap_ref):
        i = pl.program_id(1)
        keep_prev = (i % seq_tiles != 0).astype(F32)
        keep_next = (i % seq_tiles != seq_tiles - 1).astype(F32)
        _fill_wrap_next(dc_ref, dcn_ref, dwrap_ref, n_wrap, keep_next)
        _conv_bwd_data(dc_ref, dwrap_ref, w_ref, du_ref, seg, CONV_K)

        @pl.when(i == 0)
        def _():
            dw_ref[...] = jnp.zeros_like(dw_ref)

        _fill_wrap_prev(u_ref, up_ref, uwrap_ref, n_wrap, seg, seg, keep_prev)
        _conv_bwd_taps(dc_ref, u_ref, uwrap_ref, dw_ref, seg, CONV_K)

    wrap = pltpu.VMEM((n_wrap * SUBLANES, cb), F32)
    return _call(body, name, (c // cb, n_tiles),
                 [tile, nxt, tile, prev, pl.BlockSpec((None, CONV_K, cb), lambda n, i: (l, 0, n))],
                 [tile, pl.BlockSpec((TAPS_PAD, cb), lambda n, i: (0, n))],
                 [_sds((t, c), F32), _sds((TAPS_PAD, c), F32)], [wrap, wrap])(dc, dc, u, u, w3)


def _ln_silu(name, cx, g3, b3, l):
    t, d = cx.shape
    tm = min(TM_ROW, t)

    def body(c_ref, g_ref, b_ref, o_ref):
        xhat, _ = _ln_stats(c_ref[...])
        nv = xhat * g_ref[...] + b_ref[...]
        o_ref[...] = (nv * _sigmoid(nv)).astype(BF16)

    return _call(body, name, (t // tm,), [_row_spec(tm, d), _param_spec(l, d), _param_spec(l, d)],
                 _row_spec(tm, d), _sds((t, d), BF16))(cx, g3, b3)


def _ln_silu_bwd(name, ds, cx, g3, b3, l, deps=()):
    t, d = cx.shape
    tm = min(TM_ROW, t)

    def body(ds_ref, c_ref, g_ref, b_ref, dc_ref, dg_ref, db_ref, dsum_ref):
        first = pl.program_id(0) == 0
        xhat, rstd = _ln_stats(c_ref[...])
        g = g_ref[...]
        nv = xhat * g + b_ref[...]
        sg = _sigmoid(nv)
        dn = ds_ref[...] * (sg * (1.0 + nv * (1.0 - sg)))
        dc = _ln_backward(dn * g, xhat, rstd)
        dc_ref[...] = dc
        _acc_rows(dg_ref, _colsum(dn * xhat), first)
        _acc_rows(db_ref, _colsum(dn), first)
        _acc_rows(dsum_ref, _colsum(dc), first)

    vec = pl.BlockSpec((1, d), lambda i: (0, 0))
    return _call(body, name, (t // tm,),
                 [_row_spec(tm, d), _row_spec(tm, d), _param_spec(l, d), _param_spec(l, d)],
                 [_row_spec(tm, d), vec, vec, vec],
                 [_sds((t, d), F32)] + [_sds((1, d), F32)] * 3, deps=deps)(ds, cx, g3, b3)


FFN_HALO = FFN_K - 1


def _ffn_conv(x_ref, halo_ref, wrap_ref, w_ref, b_ref, keep, seg, out_ref):
    _fill_wrap_prev(x_ref, halo_ref, wrap_ref, FFN_K - 1, FFN_HALO, seg, keep)
    _conv_fwd(x_ref, wrap_ref, w_ref, b_ref, out_ref, seg, FFN_K)


TM_FFN = 512
WRAP_ROWS = FFN_HALO * SUBLANES


def _sub_tiles(x_ref, prev_ref, next_ref, keep_prev, keep_next, n_sub):
    out = []
    for s in range(n_sub):
        tile = x_ref.at[pl.ds(s * TM_EW, TM_EW)]
        prev = prev_ref if s == 0 else x_ref.at[pl.ds(s * TM_EW - WRAP_ROWS, WRAP_ROWS)]
        nxt = next_ref if s == n_sub - 1 else x_ref.at[pl.ds((s + 1) * TM_EW, WRAP_ROWS)]
        out.append((tile, prev, keep_prev if s == 0 else 1.0, nxt, keep_next if s == n_sub - 1 else 1.0))
    return out


def _rows(ref, s, rows):
    return ref.at[pl.ds(s * rows, rows)]


def _ffn_specs(tm, fb, n_tiles):
    return _tile_halo_specs(tm, lambda rows: (None, rows, fb), FFN_HALO, n_tiles, lambda n, r: (n, r, 0))


def _ffn_up_act(name, x, w_up, b_up4, wdw, bdw, l, seq):
    t, d = x.shape
    nb, fb, _ = w_up.shape
    half = nb // 2
    tm = min(TM_FFN, seq)
    n_sub, seg, seq_steps, n_steps = tm // TM_EW, TM_EW // SUBLANES, seq // tm, t // tm
    per = tm // WRAP_ROWS
    nt_dims = (_DIMS["nt"], ((), ()))

    def body(x_ref, xp_ref, ug_ref, uv_ref, bug_ref, buv_ref, wg_ref, wv_ref, bg_ref, bv_ref,
             hg_ref, hv_ref, a_ref, pg_ref, pv_ref, gwrap_ref, vwrap_ref, cg_ref, cv_ref):
        keep = (pl.program_id(1) % seq_steps != 0).astype(F32)
        xb, xpb = x_ref[...].astype(BF16), xp_ref[...].astype(BF16)
        hg_ref[...] = lax.dot_general(xb, ug_ref[...], nt_dims, preferred_element_type=F32) + bug_ref[...]
        pg_ref[...] = lax.dot_general(xpb, ug_ref[...], nt_dims, preferred_element_type=F32) + bug_ref[...]
        for s, (tile, prev, kp, _, _) in enumerate(_sub_tiles(hg_ref, pg_ref, None, keep, None, n_sub)):
            _ffn_conv(tile, prev, gwrap_ref, wg_ref, bg_ref, kp, seg, _rows(cg_ref, s, TM_EW))
        hv_ref[...] = lax.dot_general(xb, uv_ref[...], nt_dims, preferred_element_type=F32) + buv_ref[...]
        pv_ref[...] = lax.dot_general(xpb, uv_ref[...], nt_dims, preferred_element_type=F32) + buv_ref[...]
        for s, (tile, prev, kp, _, _) in enumerate(_sub_tiles(hv_ref, pv_ref, None, keep, None, n_sub)):
            _ffn_conv(tile, prev, vwrap_ref, wv_ref, bv_ref, kp, seg, _rows(cv_ref, s, TM_EW))
        cg = cg_ref[...]
        a_ref[...] = (cg * _sigmoid(cg) * cv_ref[...]).astype(BF16)

    def blk(shift):
        return pl.BlockSpec((None, fb, d), lambda n, i: (n + shift, 0, 0))

    def vec(shift, rows):
        return pl.BlockSpec((None, None, rows, fb), lambda n, i: (l, n + shift, 0, 0))

    out = pl.BlockSpec((None, tm, fb), lambda n, i: (n, i, 0))
    tmp = pltpu.VMEM((tm, fb), F32)
    halo = pltpu.VMEM((WRAP_ROWS, fb), F32)
    return _call(body, name, (half, n_steps),
                 [pl.BlockSpec((tm, d), lambda n, i: (i, 0)),
                  pl.BlockSpec((WRAP_ROWS, d), lambda n, i: (jnp.maximum(i * per - 1, 0), 0)),
                  blk(0), blk(half), vec(0, 1), vec(half, 1), vec(0, FFN_K), vec(half, FFN_K), vec(0, 1), vec(half, 1)],
                 [out, out, out],
                 [_sds((half, t, fb), F32), _sds((half, t, fb), F32), _sds((half, t, fb), BF16)],
                 [halo, halo, halo, halo, tmp, tmp])(x, x, w_up, w_up, b_up4, b_up4, wdw, wdw, bdw, bdw)


def _ffn_act_bwd(name, dy, w_down, hg, hv, wdw, bdw, l, seq, deps=()):
    half, t, fb = hg.shape
    d = dy.shape[-1]
    tm = min(TM_FFN, seq)
    n_sub, seg, seq_steps, n_steps = tm // TM_EW, TM_EW // SUBLANES, seq // tm, t // tm
    tile, prev, _ = _ffn_specs(tm, fb, n_steps)

    def body(dy_ref, wd_ref, g_ref, gp_ref, v_ref, vp_ref, wg_ref, wv_ref, bg_ref, bv_ref,
             dg_ref, dv_ref, dbg_ref, dbv_ref, dwg_ref, dwv_ref, gwrap_ref, vwrap_ref, cg_ref, cv_ref):
        i = pl.program_id(1)
        first = i == 0
        keep = (i % seq_steps != 0).astype(F32)
        da = lax.dot_general(dy_ref[...].astype(BF16), wd_ref[...], (_DIMS["nt"], ((), ())),
                             preferred_element_type=F32)
        g_tiles = _sub_tiles(g_ref, gp_ref, None, keep, None, n_sub)
        v_tiles = _sub_tiles(v_ref, vp_ref, None, keep, None, n_sub)
        for s in range(n_sub):
            _ffn_conv(g_tiles[s][0], g_tiles[s][1], _rows(gwrap_ref, s, WRAP_ROWS), wg_ref, bg_ref, g_tiles[s][2],
                      seg, _rows(cg_ref, s, TM_EW))
            _ffn_conv(v_tiles[s][0], v_tiles[s][1], _rows(vwrap_ref, s, WRAP_ROWS), wv_ref, bv_ref, v_tiles[s][2],
                      seg, _rows(cv_ref, s, TM_EW))
        cg, cv = cg_ref[...], cv_ref[...]
        sg = _sigmoid(cg)
        dcv = da * cg * sg
        dcg = da * cv * sg * (1.0 + cg * (1.0 - sg))
        dg_ref[...] = dcg
        dv_ref[...] = dcv
        _acc_rows(dbg_ref, _colsum(dcg), first)
        _acc_rows(dbv_ref, _colsum(dcv), first)

        @pl.when(first)
        def _():
            dwg_ref[...] = jnp.zeros_like(dwg_ref)
            dwv_ref[...] = jnp.zeros_like(dwv_ref)

        for s in range(n_sub):
            _conv_bwd_taps(_rows(dg_ref, s, TM_EW), g_tiles[s][0], _rows(gwrap_ref, s, WRAP_ROWS), dwg_ref, seg, FFN_K)
            _conv_bwd_taps(_rows(dv_ref, s, TM_EW), v_tiles[s][0], _rows(vwrap_ref, s, WRAP_ROWS), dwv_ref, seg, FFN_K)

    def vec(shift, rows):
        return pl.BlockSpec((None, None, rows, fb), lambda n, i: (l, n + shift, 0, 0))

    def acc(rows):
        return pl.BlockSpec((None, rows, fb), lambda n, i: (n, 0, 0))

    wrap = pltpu.VMEM((n_sub * WRAP_ROWS, fb), F32)
    tmp = pltpu.VMEM((tm, fb), F32)
    return _call(body, name, (half, n_steps),
                 [pl.BlockSpec((tm, d), lambda n, i: (i, 0)), pl.BlockSpec((None, fb, d), lambda n, i: (n, 0, 0)),
                  tile, prev, tile, prev, vec(0, FFN_K), vec(half, FFN_K), vec(0, 1), vec(half, 1)],
                 [tile, tile, acc(1), acc(1), acc(SUBLANES), acc(SUBLANES)],
                 [_sds((half, t, fb), F32), _sds((half, t, fb), F32), _sds((half, 1, fb), F32),
                  _sds((half, 1, fb), F32), _sds((half, SUBLANES, fb), F32), _sds((half, SUBLANES, fb), F32)],
                 [wrap, wrap, tmp, tmp], deps=deps)(dy, w_down, hg, hg, hv, hv, wdw, wdw, bdw, bdw)


def _ffn_conv_t_dx(name, dcg, dcv, wdw, w_up, res, xhat, rstd, g3, l, seq, deps=()):
    half, t, fb = dcg.shape
    nb, d = 2 * half, res.shape[-1]
    tm = min(TM_FFN, seq)
    n_sub, seg, seq_steps, n_steps = tm // TM_EW, TM_EW // SUBLANES, seq // tm, t // tm
    per = tm // WRAP_ROWS

    pair = 2
    n_pairs, half_pairs = nb // pair, half // pair

    def body(g_ref, gn_ref, v_ref, vn_ref, w_ref, up_ref, res_ref, xh_ref, rs_ref, gam_ref,
             dh_ref, db_ref, dr_ref, dgam_ref, dbeta_ref, dsum_ref, wrap_ref, out_ref, acc_ref):
        i, m = pl.program_id(0), pl.program_id(1)
        keep = (i % seq_steps != seq_steps - 1).astype(F32)

        def conv_t(d_ref, dn_ref, b):
            for s, (sub, _, _, nx, kn) in enumerate(_sub_tiles(d_ref.at[b], None, dn_ref.at[b], None, keep, n_sub)):
                _fill_wrap_next(sub, nx, wrap_ref, FFN_K - 1, kn)
                _conv_bwd_data(sub, wrap_ref, w_ref.at[b], _rows(out_ref, s, TM_EW), seg, FFN_K)

        p = None
        for b in range(pair):
            @pl.when(m < half_pairs)
            def _(b=b):
                conv_t(g_ref, gn_ref, b)

            @pl.when(m >= half_pairs)
            def _(b=b):
                conv_t(v_ref, vn_ref, b)

            dh = out_ref[...]
            dhb = dh.astype(BF16)
            dh_ref[b] = dhb
            _acc_rows(db_ref.at[pair * m + b], _colsum(dh), i == 0)
            part = jnp.dot(dhb, up_ref[b], preferred_element_type=F32)
            p = part if p is None else p + part

        @pl.when(m == 0)
        def _():
            acc_ref[...] = p

        @pl.when(m > 0)
        def _():
            acc_ref[...] += p

        @pl.when(m == n_pairs - 1)
        def _():
            _ln_bwd_rows(acc_ref[...] + ALPHA * res_ref[...], xh_ref, rs_ref, gam_ref, dr_ref, dgam_ref, dbeta_ref,
                         dsum_ref, i == 0)

    def src(gate):
        def blk(m):
            return jnp.minimum(m, half_pairs - 1) if gate else jnp.maximum(m - half_pairs, 0)
        tile = pl.BlockSpec((pair, tm, fb), lambda i, m: (blk(m), i, 0))
        nxt = pl.BlockSpec((pair, WRAP_ROWS, fb),
                           lambda i, m: (blk(m), jnp.minimum((i + 1) * per, n_steps * per - 1), 0))
        return [tile, nxt]

    row = pl.BlockSpec((tm, d), lambda i, m: (i, 0))
    ln_ins, ln_outs = _ln_bwd_specs(tm, d, l, lambda i, m: (i, 0))
    tmp = pltpu.VMEM((tm, fb), F32)
    halo = pltpu.VMEM((WRAP_ROWS, fb), F32)
    return _call(body, name, (n_steps, n_pairs),
                 src(True) + src(False) +
                 [pl.BlockSpec((None, pair, FFN_K, fb), lambda i, m: (l, m, 0, 0)),
                  pl.BlockSpec((pair, fb, d), lambda i, m: (m, 0, 0)), row] + ln_ins,
                 [pl.BlockSpec((pair, tm, fb), lambda i, m: (m, i, 0)),
                  pl.BlockSpec((nb, 1, fb), lambda i, m: (0, 0, 0))] + ln_outs,
                 [_sds((nb, t, fb), BF16), _sds((nb, 1, fb), F32), _sds((t, d), F32)] + [_sds((1, d), F32)] * 3,
                 [halo, tmp, pltpu.VMEM((tm, d), F32)],
                 deps=deps)(dcg, dcg, dcv, dcv, wdw, w_up, res, xhat, rstd, g3)


def _gelu_parts(h):
    cdf = 0.5 * (1.0 + lax.erf(h * INV_SQRT2))
    return h * cdf, cdf


def _seg_axis(a, axis, fn):
    return jnp.moveaxis(fn(jnp.moveaxis(a, axis, 0), TM_EW), 0, axis)


def _sgu_operands(w_s, b_s):
    nl = w_s.shape[0]
    n_sub = TM_EW // CHUNK
    tril = jnp.tril(jnp.ones((CHUNK, CHUNK), dtype=bool))
    w_causal = jnp.where(tril, w_s, 0.0)
    w_tile = (jnp.eye(n_sub, dtype=F32)[None, None, :, None, :, None] * w_causal[:, :, None, :, None, :]).reshape(
        nl, GROUPS, TM_EW, TM_EW)
    w_tile = _seg_axis(_seg_axis(w_tile, 2, _to_segments), 3, _to_segments).astype(BF16)
    bs_tile = jnp.broadcast_to(b_s[:, :, None, :, None], (nl, GROUPS, n_sub, CHUNK, CHUNK)).reshape(
        nl, GROUPS, TM_EW, CHUNK)
    return w_tile, _seg_axis(bs_tile, 2, _to_segments)


def _sgu_param_grads(dwt, dbt):
    n_sub = TM_EW // CHUNK
    tril = jnp.tril(jnp.ones((CHUNK, CHUNK), dtype=bool))
    dwt = _seg_axis(_seg_axis(dwt, 1, _from_segments), 2, _from_segments).reshape(GROUPS, n_sub, CHUNK, n_sub, CHUNK)
    dw = sum(dwt[:, a, :, a, :] for a in range(n_sub))
    db = _seg_axis(dbt, 1, _from_segments).reshape(GROUPS, n_sub, CHUNK).sum(axis=1)
    return jnp.where(tril, dw, 0.0), db


def _sgu(name, h, g3, b3, wt, bst, l):
    t, c2 = h.shape
    c = c2 // 2
    tm = TM_EW

    def body(h_ref, g_ref, b_ref, wt_ref, bs_ref, o_ref):
        z, _ = _gelu_parts(h_ref[...])
        u = z[:, :c]
        xhat, _ = _ln_stats(z[:, c:])
        vnb = (xhat * g_ref[...] + b_ref[...]).astype(BF16)
        for gi in range(GROUPS):
            cs = slice(gi * CHUNK, (gi + 1) * CHUNK)
            sp = jnp.dot(wt_ref[gi], vnb[:, cs], preferred_element_type=F32) + bs_ref[gi]
            o_ref[:, cs] = (u[:, cs] * sp).astype(BF16)

    return _call(body, name, (t // tm,),
                 [_row_spec(tm, c2), _param_spec(l, c), _param_spec(l, c),
                  pl.BlockSpec((None, GROUPS, tm, tm), lambda i: (l, 0, 0, 0)),
                  pl.BlockSpec((None, GROUPS, tm, CHUNK), lambda i: (l, 0, 0, 0))],
                 _row_spec(tm, c), _sds((t, c), BF16))(h, g3, b3, wt, bst)


def _sgu_bwd(name, dq, h, g3, b3, wt, bst, l, deps=()):
    t, c2 = h.shape
    c = c2 // 2
    tm = TM_EW
    n_tiles = t // tm

    def body(dq_ref, h_ref, g_ref, b_ref, wt_ref, bs_ref,
             dh_ref, dbin_ref, dw_ref, dbs_ref, dg_ref, db_ref, du_ref, dvn_ref, bsum_ref):
        i = pl.program_id(0)
        first = i == 0
        hv = h_ref[...]
        z, cdf = _gelu_parts(hv)
        u = z[:, :c]
        xhat, rstd = _ln_stats(z[:, c:])
        g = g_ref[...]
        vnb = (xhat * g + b_ref[...]).astype(BF16)

        @pl.when(first)
        def _():
            dw_ref[...] = jnp.zeros_like(dw_ref)
            bsum_ref[...] = jnp.zeros_like(bsum_ref)

        for gi in range(GROUPS):
            cs = slice(gi * CHUNK, (gi + 1) * CHUNK)
            vb = vnb[:, cs]
            w = wt_ref[gi]
            sp = jnp.dot(w, vb, preferred_element_type=F32) + bs_ref[gi]
            dqb = dq_ref[:, cs]
            du_ref[:, cs] = dqb * sp
            dsp = dqb * u[:, cs]
            bsum_ref[gi] += dsp
            dspb = dsp.astype(BF16)
            dw_ref[gi] += lax.dot_general(dspb, vb, (_DIMS["nt"], ((), ())), preferred_element_type=F32)
            dvn_ref[:, cs] = lax.dot_general(w, dspb, (_DIMS["tn"], ((), ())), preferred_element_type=F32)

        dvn = dvn_ref[...]
        dv = _ln_backward(dvn * g, xhat, rstd)
        pdf = jnp.exp(-0.5 * hv * hv) * INV_SQRT2PI
        dgelu = cdf + hv * pdf
        dhu = du_ref[...] * dgelu[:, :c]
        dhv = dv * dgelu[:, c:]
        dh_ref[:, :c] = dhu.astype(BF16)
        dh_ref[:, c:] = dhv.astype(BF16)
        _acc_rows(dbin_ref.at[:, :c], _colsum(dhu), first)
        _acc_rows(dbin_ref.at[:, c:], _colsum(dhv), first)
        _acc_rows(dg_ref, _colsum(dvn * xhat), first)
        _acc_rows(db_ref, _colsum(dvn), first)

        @pl.when(i == n_tiles - 1)
        def _():
            dbs_ref[...] = jnp.sum(bsum_ref[...], axis=-1)

    vec = pl.BlockSpec((1, c), lambda i: (0, 0))
    return _call(body, name, (n_tiles,),
                 [_row_spec(tm, c), _row_spec(tm, c2), _param_spec(l, c), _param_spec(l, c),
                  pl.BlockSpec((None, GROUPS, tm, tm), lambda i: (l, 0, 0, 0)),
                  pl.BlockSpec((None, GROUPS, tm, CHUNK), lambda i: (l, 0, 0, 0))],
                 [_row_spec(tm, c2), pl.BlockSpec((1, c2), lambda i: (0, 0)),
                  pl.BlockSpec((GROUPS, tm, tm), lambda i: (0, 0, 0)),
                  pl.BlockSpec((GROUPS, tm), lambda i: (0, 0)), vec, vec],
                 [_sds((t, c2), BF16), _sds((1, c2), F32), _sds((GROUPS, tm, tm), F32),
                  _sds((GROUPS, tm), F32), _sds((1, c), F32), _sds((1, c), F32)],
                 [pltpu.VMEM((tm, c), F32), pltpu.VMEM((tm, c), F32), pltpu.VMEM((GROUPS, tm, CHUNK), F32)],
                 deps=deps)(dq, h, g3, b3, wt, bst)


def _loss(name, y, target):
    t, d = y.shape
    tm = min(TM_ROW, t)
    n_tiles = t // tm

    def body(y_ref, t_ref, l_ref, dy_ref, acc_ref):
        i = pl.program_id(0)
        diff = y_ref[...] - t_ref[...]
        dy_ref[...] = diff * (1.0 / d)
        _acc_rows(acc_ref, _colsum(diff * diff), i == 0)

        @pl.when(i == n_tiles - 1)
        def _():
            l_ref[...] = jnp.broadcast_to(jnp.sum(acc_ref[...], axis=-1, keepdims=True) * (0.5 / d), (1, LANES))

    return _call(body, name, (n_tiles,), [_row_spec(tm, d), _row_spec(tm, d)],
                 [pl.BlockSpec((1, LANES), lambda i: (0, 0)), _row_spec(tm, d)],
                 [_sds((1, LANES), F32), _sds((t, d), F32)], [pltpu.VMEM((1, d), F32)])(y, target)


def _adamw(g, w, m, v):
    m2 = ADAM_B1 * m + (1.0 - ADAM_B1) * g
    v2 = ADAM_B2 * v + (1.0 - ADAM_B2) * (g * g)
    m_hat = m2 / (1.0 - ADAM_B1 ** ADAM_STEP)
    v_hat = v2 / (1.0 - ADAM_B2 ** ADAM_STEP)
    delta = -ADAM_LR * (m_hat / (jnp.sqrt(v_hat) + ADAM_EPS) + ADAM_WD * w)
    return delta, m2, v2


ROW_TILE_CAP = 512


def _row_tile(rows, cap=ROW_TILE_CAP):
    if rows <= cap:
        return rows
    for tr in range(cap, 15, -16):
        if rows % tr == 0:
            return tr
    return rows


def _sum8_adamw(name, dev, lands, parts, w, m, v):
    nl = len(lands)
    _, r, c = lands[0].shape
    tr = _row_tile(r, cap=256)

    def body(dev_ref, *refs):
        land, own = refs[:nl], refs[nl:2 * nl]
        w_ref, m_ref, v_ref, g_ref, d_ref, m2_ref, v2_ref = refs[2 * nl:]
        layer, me = pl.program_id(0), dev_ref[0]
        for l in range(nl):
            @pl.when(layer == l)
            def _(l=l):
                g = None
                for s in range(N_DEV):
                    part = jnp.where(me == s, own[l][...], land[l][s]).astype(F32)
                    g = part if g is None else g + part
                delta, m2, v2 = _adamw(g, w_ref[...], m_ref[...], v_ref[...])
                g_ref[...] = g
                d_ref[...] = delta
                m2_ref[...] = m2
                v2_ref[...] = v2

    def rows_of(l, a, i):
        return jnp.where(a == l, i, 0)

    spec = pl.BlockSpec((None, tr, c), lambda a, i, dev_ref: (a, i, 0))
    in_specs = [pl.BlockSpec((N_DEV, tr, c), lambda a, i, dev_ref, l=l: (0, rows_of(l, a, i), 0)) for l in range(nl)]
    in_specs += [pl.BlockSpec((None, tr, c), lambda a, i, dev_ref, l=l: (dev_ref[0], rows_of(l, a, i), 0))
                 for l in range(nl)]
    grid_spec = pltpu.PrefetchScalarGridSpec(
        num_scalar_prefetch=1, grid=(nl, r // tr), in_specs=in_specs + [spec] * 3, out_specs=[spec] * 4)
    return pl.pallas_call(
        body, name=name, grid_spec=grid_spec, out_shape=[_sds(w.shape, F32)] * 4,
        compiler_params=pltpu.CompilerParams(vmem_limit_bytes=VMEM_LIMIT))(dev, *lands, *parts, w, m, v)


def _sum8(name, parts):
    _, r, c = parts.shape
    tr = _row_tile(r)

    def body(p_ref, o_ref):
        acc = p_ref[0]
        for s in range(1, N_DEV):
            acc = acc + p_ref[s]
        o_ref[...] = acc

    return _call(body, name, (r // tr,), [pl.BlockSpec((N_DEV, tr, c), lambda i: (0, i, 0))],
                 pl.BlockSpec((tr, c), lambda i: (i, 0)), _sds((r, c), F32))(parts)


def _adamw_small(name, gs, ws, ms, vs):
    n = len(gs)

    def body(*refs):
        g, w, m, v = (refs[k * n:(k + 1) * n] for k in range(4))
        d_out, m_out, v_out = (refs[(4 + k) * n:(5 + k) * n] for k in range(3))
        for k in range(n):
            d_out[k][...], m_out[k][...], v_out[k][...] = _adamw(g[k][...], w[k][...], m[k][...], v[k][...])

    vmem = pl.BlockSpec(memory_space=pltpu.VMEM)
    outs = _call(body, name, (), [vmem] * (4 * n), [vmem] * (3 * n), [_sds(w.shape, F32) for w in ws] * 3)(
        *gs, *ws, *ms, *vs)
    return outs[:n], outs[n:2 * n], outs[2 * n:]


def _pack(arrs, row_multiple=SUBLANES):
    pieces, rows = [], 0
    for a in arrs:
        piece = a.reshape(-1, LANES)
        piece = jnp.pad(piece, ((0, (-piece.shape[0]) % SUBLANES), (0, 0)))
        pieces.append(piece)
        rows += piece.shape[0]
    if rows % row_multiple:
        pieces.append(jnp.zeros(((-rows) % row_multiple, LANES), pieces[0].dtype))
    return jnp.concatenate(pieces, axis=0)


def _unpack(buf, shapes, lead=0):
    out, pos = [], 0
    for shp in shapes:
        rows = math.prod(shp) // LANES
        piece = lax.slice_in_dim(buf, pos, pos + rows, axis=lead)
        out.append(piece.reshape(buf.shape[:lead] + tuple(shp)))
        pos += rows + (-rows) % SUBLANES
    return out


REPLICATED = ["conv_b_in", "conv_b_dw", "conv_ln_g", "conv_ln_b", "conv_b_out", "gmlp_w_s", "gmlp_b_s",
              "ffn_b_up", "ffn_b_dw", "ffn_b_down", "norm1_g", "norm1_b", "norm2_g", "norm2_b"]
SMALL_SHARDED = ["conv_w_dw", "gmlp_b_in", "gmlp_ln_g", "gmlp_ln_b", "gmlp_b_out", "ffn_w_dw"]
BIG = ["conv_w_in", "conv_w_out", "gmlp_w_in", "gmlp_w_out", "ffn_w_up", "ffn_w_down"]
WEIGHTS = ["conv_w_in", "conv_b_in", "conv_w_dw", "conv_b_dw", "conv_ln_g", "conv_ln_b", "conv_w_out", "conv_b_out",
           "gmlp_w_in", "gmlp_b_in", "gmlp_ln_g", "gmlp_ln_b", "gmlp_w_s", "gmlp_b_s", "gmlp_w_out", "gmlp_b_out",
           "ffn_w_up", "ffn_b_up", "ffn_w_dw", "ffn_b_dw", "ffn_w_down", "ffn_b_down",
           "norm1_g", "norm1_b", "norm2_g", "norm2_b"]


def _from_shards(g, lead_shape):
    nd = len(lead_shape)
    perm = tuple(range(1, nd + 1)) + (0, nd + 1)
    return g.transpose(perm).reshape(tuple(lead_shape) + (-1,))


def _to_shards(full, width):
    lead = full.shape[:-1]
    nd = len(lead)
    parts = full.reshape(lead + (N_DEV, width))
    return parts.transpose((nd,) + tuple(range(nd)) + (nd + 1,))


def _step(p):
    x_in, target_in = p["x"], p["loss_target"]
    bsz, seq, d = x_in.shape
    t = bsz * seq
    assert seq % TM_EW == 0 and TM_EW % CHUNK == 0 and TM_EW // SUBLANES >= CONV_K - 1
    x0 = _to_segments(x_in.reshape(t, d), TM_EW)
    target = _to_segments(target_in.reshape(t, d), TM_EW)
    n_conv, n_gmlp = p["conv_w_in"].shape[0], p["gmlp_w_in"].shape[0]
    fb = p["ffn_w_up"].shape[-1]
    nblk = N_DEV
    half = nblk // 2
    cw = p["conv_w_in"].shape[-1]
    tm = min(TM_MM, t)
    nt = t // tm
    tk = min(TK_DW, t)
    ntk = t // tk
    dev = 4 * lax.axis_index("x") + 2 * lax.axis_index("y") + lax.axis_index("c")

    small_shapes = [p[n].shape for n in SMALL_SHARDED]
    small_src = _pack([p[n] for n in SMALL_SHARDED])[None]
    small_all = _all_gather("gather_small_weights", [small_src])[0][0]
    sm = _unpack(small_all, small_shapes, lead=1)
    w_src = []
    for i in range(DEPTH):
        mix = "conv" if i % 2 == 0 else "gmlp"
        w_src += [p[mix + "_w_in"][i // 2].astype(BF16), p[mix + "_w_out"][i // 2].astype(BF16),
                  p["ffn_w_up"][i].T.astype(BF16), p["ffn_w_down"][i].astype(BF16)]
    send_sems, recv_sems, w_land, _ = _gather_start(
        "weights_gather_start", _place_own("weights_place_own", w_src, deps=[small_all]))
    W_IN, W_OUT, W_UP, W_DOWN = range(4)

    def wait_weight(i, k, after):
        return _gather_wait(f"l{i}_weights_wait{k}", w_land[4 * i + k], send_sems, recv_sems, 4 * i + k, after)
    conv_w_dw = _from_shards(sm[0], sm[0].shape[1:-1])
    gmlp_b_in = _from_shards(sm[1], sm[1].shape[1:-1])
    gmlp_ln_g = _from_shards(sm[2], sm[2].shape[1:-1])
    gmlp_ln_b = _from_shards(sm[3], sm[3].shape[1:-1])
    gmlp_b_out = _from_shards(sm[4], sm[4].shape[1:-1])
    ffn_w_dw = sm[5].transpose(1, 0, 2, 3)

    def rows3(a):
        return a.reshape(a.shape[0], 1, a.shape[-1])

    conv_b_in4 = p["conv_b_in"].reshape(n_conv, N_DEV, 1, cw)
    gmlp_b_in4 = gmlp_b_in.reshape(n_gmlp, N_DEV, 1, cw)
    ffn_b_up4 = p["ffn_b_up"].reshape(DEPTH, nblk, 1, fb)
    ffn_b_dw4 = p["ffn_b_dw"].reshape(DEPTH, nblk, 1, fb)
    conv_b_dw3, conv_ln_g3, conv_ln_b3 = rows3(p["conv_b_dw"]), rows3(p["conv_ln_g"]), rows3(p["conv_ln_b"])
    conv_b_out3, gmlp_b_out3, ffn_b_down3 = rows3(p["conv_b_out"]), rows3(gmlp_b_out), rows3(p["ffn_b_down"])
    gmlp_ln_g3, gmlp_ln_b3 = rows3(gmlp_ln_g), rows3(gmlp_ln_b)
    n1g3, n1b3, n2g3, n2b3 = rows3(p["norm1_g"]), rows3(p["norm1_b"]), rows3(p["norm2_g"]), rows3(p["norm2_b"])
    w_tile, bs_tile = _sgu_operands(p["gmlp_w_s"], p["gmlp_b_s"])

    def mm_in(name, xa, wg, l, bias4, glu=False):
        tmi = min(TM_LN, t)
        c_half = half * cw

        def body(a_ref, b_ref, bias_ref, h_ref, *u_ref):
            xb = a_ref[...].astype(BF16)
            for n in range(N_DEV):
                h_ref[:, n * cw:(n + 1) * cw] = jnp.dot(xb, b_ref[n], preferred_element_type=F32) + bias_ref[n]
            if glu:
                u_ref[0][...] = h_ref[:, :c_half] * _sigmoid(h_ref[:, c_half:])

        outs = _call(body, name, (t // tmi,),
                     [pl.BlockSpec((tmi, d), lambda i: (i, 0)), pl.BlockSpec((N_DEV, d, cw), lambda i: (0, 0, 0)),
                      pl.BlockSpec((None, N_DEV, 1, cw), lambda i: (l, 0, 0, 0))],
                     [pl.BlockSpec((tmi, N_DEV * cw), lambda i: (i, 0))]
                     + ([pl.BlockSpec((tmi, c_half), lambda i: (i, 0))] if glu else []),
                     [_sds((t, N_DEV * cw), F32)] + ([_sds((t, c_half), F32)] if glu else []))(xa, wg, bias4)
        return outs if glu else outs[0]

    def mm_out_dx(name, dy, w, deps=()):
        return _matmul(name, dy, w, "nt", grid=(nt,),
                       a_spec=pl.BlockSpec((tm, d), lambda i: (i, 0)),
                       b_spec=pl.BlockSpec((d, d), lambda i: (0, 0)),
                       o_spec=pl.BlockSpec((tm, d), lambda i: (i, 0)), o_shape=(t, d), o_dtype=F32, deps=deps)

    def mm_out_dw(name, sa, dy):
        return _matmul(name, sa, dy, "tn", grid=(nt,), k_axis=0, nk=nt, acc_shape=(d, d),
                       a_spec=pl.BlockSpec((tm, d), lambda k: (k, 0)),
                       b_spec=pl.BlockSpec((tm, d), lambda k: (k, 0)),
                       o_spec=pl.BlockSpec((d, d), lambda k: (0, 0)), o_shape=(d, d), o_dtype=BF16)

    def mm_in_dx(name, dh, wg, res, norm=None):
        tmx = min(TM_LN, t)

        def body(a_ref, b_ref, res_ref, *refs):
            y = ALPHA * res_ref[...]
            for n in range(N_DEV):
                y = y + lax.dot_general(a_ref[:, n * cw:(n + 1) * cw], b_ref[n], (_DIMS["nt"], ((), ())),
                                        preferred_element_type=F32)
            if norm is None:
                refs[0][...] = y
            else:
                _ln_bwd_rows(y, *refs, pl.program_id(0) == 0)

        row = pl.BlockSpec((tmx, d), lambda i: (i, 0))
        ins = [pl.BlockSpec((tmx, N_DEV * cw), lambda i: (i, 0)), pl.BlockSpec((N_DEV, d, cw), lambda i: (0, 0, 0)), row]
        if norm is None:
            return _call(body, name, (t // tmx,), ins, row, _sds((t, d), F32))(dh, wg, res)
        ln_ins, ln_outs = _ln_bwd_specs(tmx, d, norm[3], lambda i: (i, 0))
        return _call(body, name, (t // tmx,), ins + ln_ins, ln_outs,
                     [_sds((t, d), F32)] + [_sds((1, d), F32)] * 3)(dh, wg, res, *norm[:3])

    def mm_in_dw(name, xa, dh):
        def body(a_ref, b_ref, o_ref, acc_ref):
            k = pl.program_id(1)
            p = lax.dot_general(a_ref[...].astype(BF16), b_ref[...], (_DIMS["tn"], ((), ())),
                                preferred_element_type=F32)
            _acc_rows(acc_ref, p, k == 0)

            @pl.when(k == nt - 1)
            def _():
                for n in range(half):
                    o_ref[n] = acc_ref[:, n * cw:(n + 1) * cw].astype(BF16)

        return _call(body, name, (2, nt),
                     [pl.BlockSpec((tm, d), lambda c, k: (k, 0)), pl.BlockSpec((tm, half * cw), lambda c, k: (k, c))],
                     pl.BlockSpec((half, d, cw), lambda c, k: (c, 0, 0)), _sds((N_DEV, d, cw), BF16),
                     [pltpu.VMEM((d, half * cw), F32)])(xa, dh)

    def mm_down_dw(name, a, dy, deps=()):
        return _matmul(name, a, dy, "tn", grid=(half, ntk), k_axis=1, nk=ntk, acc_shape=(fb, d),
                       a_spec=pl.BlockSpec((None, tk, fb), lambda n, k: (n, k, 0)),
                       b_spec=pl.BlockSpec((tk, d), lambda n, k: (k, 0)),
                       o_spec=pl.BlockSpec((None, fb, d), lambda n, k: (n, 0, 0)),
                       o_shape=(half, fb, d), o_dtype=BF16, deps=deps)

    def mm_up_dw(name, xa, dh):
        return _matmul(name, dh, xa, "tn", grid=(nblk, ntk), k_axis=1, nk=ntk, acc_shape=(fb, d),
                       a_spec=pl.BlockSpec((None, tk, fb), lambda n, k: (n, k, 0)),
                       b_spec=pl.BlockSpec((tk, d), lambda n, k: (k, 0)),
                       o_spec=pl.BlockSpec((None, fb, d), lambda n, k: (n, 0, 0)),
                       o_shape=(nblk, fb, d), o_dtype=BF16)

    saved = []
    xcur = xcur_b = x0
    for i in range(DEPTH):
        j = i // 2
        s = {"x": xcur_b}
        s["w_in"] = wait_weight(i, W_IN, xcur if i else target)
        if i % 2 == 0:
            s["h"], s["u"] = mm_in(f"l{i}_conv_in_glu", xcur_b, s["w_in"], j, conv_b_in4, glu=True)
            s["c"] = _dwconv31(f"l{i}_dwconv", s["u"], conv_w_dw, conv_b_dw3, j, seq)
            s["s"] = _ln_silu(f"l{i}_ln_silu", s["c"], conv_ln_g3, conv_ln_b3, j)
            b_out3 = conv_b_out3
        else:
            s["h"] = mm_in(f"l{i}_gmlp_in", xcur_b, s["w_in"], j, gmlp_b_in4)
            s["s"] = _sgu(f"l{i}_sgu", s["h"], gmlp_ln_g3, gmlp_ln_b3, w_tile, bs_tile, j)
            b_out3 = gmlp_b_out3
        s["w_out"] = wait_weight(i, W_OUT, s["s"]).reshape(d, d)
        s["x1"], s["x1b"], s["xhat1"], s["rstd1"] = _matmul_ln(
            f"l{i}_mixer_out_norm1", s["s"], s["w_out"], xcur, b_out3, n1g3, n1b3, j, i)
        s["w_up"] = wait_weight(i, W_UP, s["x1"])
        s["hg"], s["hv"], s["a"] = _ffn_up_act(f"l{i}_ffn_up_act", s["x1b"], s["w_up"], ffn_b_up4, ffn_w_dw,
                                               ffn_b_dw4, i, seq)
        s["w_down"] = wait_weight(i, W_DOWN, s["a"]).reshape(half, fb, d)
        xcur, xcur_b, s["xhat2"], s["rstd2"] = _matmul_ln(
            f"l{i}_ffn_down_norm2", s["a"], s["w_down"], s["x1"], ffn_b_down3, n2g3, n2b3, i, i)
        saved.append(s)

    loss_row, dx = _loss("loss", xcur, target)

    started = {n: [None] * p[n].shape[0] for n in BIG}
    tokens = []

    def send_grads(name, items):
        done, token = _scatter_start(name, [g for _, _, g in items])
        for (n, l, _), st in zip(items, done):
            started[n][l] = st
        tokens.append(token)

    def take_tokens():
        out = list(tokens)
        tokens.clear()
        return out

    gl = {n: [None] * p[n].shape[0] for n in REPLICATED + SMALL_SHARDED}
    dr2, gl["norm2_g"][DEPTH - 1], gl["norm2_b"][DEPTH - 1], gl["ffn_b_down"][DEPTH - 1] = _ln_res_bwd(
        f"l{DEPTH - 1}_norm2_bwd", dx, saved[-1]["xhat2"], saved[-1]["rstd2"], n2g3, DEPTH - 1)
    for i in reversed(range(DEPTH)):
        j = i // 2
        s = saved[i]
        mix = "conv" if i % 2 == 0 else "gmlp"
        g_down = mm_down_dw(f"l{i}_ffn_down_dw", s["a"], dr2, deps=take_tokens()).reshape(N_DEV, -1, d)
        send_grads(f"l{i}_ffn_down_grad_scatter_start", [("ffn_w_down", i, g_down)])
        dcg, dcv, dbg, dbv, dwg, dwv = _ffn_act_bwd(f"l{i}_ffn_act_bwd", dr2, s["w_down"], s["hg"], s["hv"],
                                                    ffn_w_dw, ffn_b_dw4, i, seq, deps=take_tokens())
        gl["ffn_b_dw"][i] = jnp.concatenate([dbg, dbv], axis=0).reshape(1, nblk * fb)
        gl["ffn_w_dw"][i] = jnp.concatenate([dwg[:, :FFN_K], dwv[:, :FFN_K]], axis=0)
        dh, dbu, dr1, gl["norm1_g"][i], gl["norm1_b"][i], gl[mix + "_b_out"][j] = _ffn_conv_t_dx(
            f"l{i}_ffn_conv_t_dx", dcg, dcv, ffn_w_dw, s["w_up"], dr2, s["xhat1"], s["rstd1"], n1g3, i, seq,
            deps=take_tokens())
        gl["ffn_b_up"][i] = dbu.reshape(1, nblk * fb)
        send_grads(f"l{i}_ffn_up_grad_scatter_start", [("ffn_w_up", i, mm_up_dw(f"l{i}_ffn_up_dw", s["x1b"], dh))])
        ds = mm_out_dx(f"l{i}_{mix}_out_dx", dr1, s["w_out"], deps=take_tokens())
        g_out = mm_out_dw(f"l{i}_{mix}_out_dw", s["s"], dr1).reshape(N_DEV, -1, d)
        if i % 2 == 0:
            dc, gl["conv_ln_g"][j], gl["conv_ln_b"][j], gl["conv_b_dw"][j] = _ln_silu_bwd(
                f"l{i}_ln_silu_bwd", ds, s["c"], conv_ln_g3, conv_ln_b3, j)
            du, dwdw = _dwconv31_bwd(f"l{i}_dwconv_bwd", dc, s["u"], conv_w_dw, j, seq)
            gl["conv_w_dw"][j] = dwdw[:CONV_K]
            dh, gl["conv_b_in"][j] = _glu_bwd(f"l{i}_glu_bwd", du, s["h"])
        else:
            dh, gl["gmlp_b_in"][j], dwt, dbt, gl["gmlp_ln_g"][j], gl["gmlp_ln_b"][j] = _sgu_bwd(
                f"l{i}_sgu_bwd", ds, s["h"], gmlp_ln_g3, gmlp_ln_b3, w_tile, bs_tile, j)
            gl["gmlp_w_s"][j], gl["gmlp_b_s"][j] = _sgu_param_grads(dwt, dbt)
            if i == 1:
                ws_local = jnp.stack(gl["gmlp_w_s"]).reshape(-1, LANES)
                ws_send, ws_recv, ws_land, ws_token = _gather_start(
                    "w_s_grads_gather_start", _place_own("w_s_grads_place_own", [ws_local]))
                tokens.append(ws_token)
        if i > 0:
            prev = saved[i - 1]
            dr2, gl["norm2_g"][i - 1], gl["norm2_b"][i - 1], gl["ffn_b_down"][i - 1] = mm_in_dx(
                f"l{i}_{mix}_in_dx_norm2_bwd", dh, s["w_in"], dr1, (prev["xhat2"], prev["rstd2"], n2g3, i - 1))
        else:
            dx = mm_in_dx(f"l{i}_{mix}_in_dx", dh, s["w_in"], dr1)
        send_grads(f"l{i}_mixer_grads_scatter_start",
                   [(mix + "_w_out", j, g_out), (mix + "_w_in", j, mm_in_dw(f"l{i}_{mix}_in_dw", s["x"], dh))])
    grad_x = _from_segments(dx, TM_EW).reshape(bsz, seq, d)

    late = [n for n in REPLICATED if n != "gmlp_w_s"]
    full_small = {n: jnp.stack(gl[n]).reshape(p[n].shape) for n in late}
    shard_small = {}
    for n in SMALL_SHARDED:
        if n == "ffn_w_dw":
            shard_small[n] = jnp.stack(gl[n]).transpose(1, 0, 2, 3)
        else:
            width = p[n].shape[-1]
            lead = p[n].shape[:-1]
            shard_small[n] = _to_shards(jnp.stack(gl[n]).reshape(lead + (N_DEV * width,)), width)
    flat_shapes = [(1, LANES)] + [p[n].shape for n in late] + [(N_DEV,) + p[n].shape for n in SMALL_SHARDED]
    flat_local = _pack([loss_row] + [full_small[n] for n in late] + [shard_small[n] for n in SMALL_SHARDED],
                       row_multiple=ROW_TILE_CAP)

    small_send, small_recv, small_land, small_token = _gather_start(
        "small_grads_gather_start", _place_own("small_grads_place_own", [flat_local]))

    grads, delta, new_m, new_v = {}, {}, {}, {}
    dev1 = jnp.reshape(dev, (1,)).astype(jnp.int32)
    order = ["ffn_w_down", "ffn_w_up", "gmlp_w_out", "gmlp_w_in", "conv_w_out", "conv_w_in"]
    after = small_token
    for n in order:
        parts_done, lands_done = _scatter_wait(f"grads_{n}_scatter_wait", started[n], after)
        state = [p[n], p["m_" + n], p["v_" + n]]
        if n == "ffn_w_up":
            state = [a.transpose(0, 2, 1) for a in state]
        outs = _sum8_adamw(f"adamw_{n}", dev1, lands_done, parts_done, *state)
        after = outs[-1]
        if n == "ffn_w_up":
            outs = [a.transpose(0, 2, 1) for a in outs]
        grads[n], delta[n], new_m[n], new_v[n] = outs

    ws_parts = _gather_wait("w_s_grads_gather_wait", ws_land[0], ws_send, ws_recv, 0, after)
    ws_sum = _sum8("sum_w_s_grads", ws_parts)
    grads["gmlp_w_s"] = ws_sum.reshape(p["gmlp_w_s"].shape)
    small_parts = _gather_wait("small_grads_gather_wait", small_land[0], small_send, small_recv, 0, ws_sum)
    summed = _unpack(_sum8("sum_small_grads", small_parts), flat_shapes)
    loss = summed[0][0, 0]
    grads.update(zip(late, summed[1:1 + len(late)]))
    for n, g in zip(SMALL_SHARDED, summed[1 + len(late):]):
        grads[n] = lax.dynamic_index_in_dim(g, dev, axis=0, keepdims=False)
    small = REPLICATED + SMALL_SHARDED
    d_s, m_s, v_s = _adamw_small("adamw_small", [grads[n] for n in small], [p[n] for n in small],
                                 [p["m_" + n] for n in small], [p["v_" + n] for n in small])
    for n, dd, mm, vv in zip(small, d_s, m_s, v_s):
        delta[n], new_m[n], new_v[n] = dd, mm, vv

    return (loss, grad_x, *[grads[n] for n in WEIGHTS], *[delta[n] for n in WEIGHTS],
            *[new_m[n] for n in WEIGHTS], *[new_v[n] for n in WEIGHTS])


def kernel(x, conv_w_in, conv_b_in, conv_w_dw, conv_b_dw, conv_ln_g, conv_ln_b, conv_w_out, conv_b_out, gmlp_w_in, gmlp_b_in, gmlp_ln_g, gmlp_ln_b, gmlp_w_s, gmlp_b_s, gmlp_w_out, gmlp_b_out, ffn_w_up, ffn_b_up, ffn_w_dw, ffn_b_dw, ffn_w_down, ffn_b_down, norm1_g, norm1_b, norm2_g, norm2_b, loss_target, m_conv_w_in, m_conv_b_in, m_conv_w_dw, m_conv_b_dw, m_conv_ln_g, m_conv_ln_b, m_conv_w_out, m_conv_b_out, m_gmlp_w_in, m_gmlp_b_in, m_gmlp_ln_g, m_gmlp_ln_b, m_gmlp_w_s, m_gmlp_b_s, m_gmlp_w_out, m_gmlp_b_out, m_ffn_w_up, m_ffn_b_up, m_ffn_w_dw, m_ffn_b_dw, m_ffn_w_down, m_ffn_b_down, m_norm1_g, m_norm1_b, m_norm2_g, m_norm2_b, v_conv_w_in, v_conv_b_in, v_conv_w_dw, v_conv_b_dw, v_conv_ln_g, v_conv_ln_b, v_conv_w_out, v_conv_b_out, v_gmlp_w_in, v_gmlp_b_in, v_gmlp_ln_g, v_gmlp_ln_b, v_gmlp_w_s, v_gmlp_b_s, v_gmlp_w_out, v_gmlp_b_out, v_ffn_w_up, v_ffn_b_up, v_ffn_w_dw, v_ffn_b_dw, v_ffn_w_down, v_ffn_b_down, v_norm1_g, v_norm1_b, v_norm2_g, v_norm2_b):
    return _step(dict(locals()))
```

```python
import math

import jax
import jax.numpy as jnp
from jax import lax
from jax.experimental import pallas as pl
from jax.experimental.pallas import tpu as pltpu

F32 = jnp.float32
BF16 = jnp.bfloat16
MESH = pl.DeviceIdType.MESH

N_DEV = 8
DEPTH = 4
ALPHA = (2.0 * DEPTH) ** 0.25
LN_EPS = 1e-5
CONV_K = 31
FFN_K = 3
CHUNK = 128
GROUPS = 8
ADAM_LR = 0.001
ADAM_B1 = 0.9
ADAM_B2 = 0.999
ADAM_EPS = 1e-08
ADAM_WD = 0.01
ADAM_STEP = 10
INV_SQRT2 = 1.0 / math.sqrt(2.0)
INV_SQRT2PI = 1.0 / math.sqrt(2.0 * math.pi)

LANES = 128
SUBLANES = 8
VMEM_LIMIT = 56 * 1024 * 1024
TM_MM = 1024
TK_DW = 2048
TM_EW = 256
TM_ROW = 512


def _call(body, name, grid, in_specs, out_specs, out_shape, scratch=(), aliases=None, deps=()):
    deps = list(deps)
    in_specs = list(in_specs)
    n_in = len(in_specs)
    if deps:
        inner = body

        def body(*refs):
            return inner(*refs[:n_in], *refs[n_in + len(deps):])

        in_specs = in_specs + [pl.BlockSpec(memory_space=pl.ANY)] * len(deps)
    fn = pl.pallas_call(
        body, name=name, grid=grid, in_specs=in_specs, out_specs=out_specs, out_shape=out_shape,
        scratch_shapes=list(scratch), input_output_aliases=aliases or {},
        compiler_params=pltpu.CompilerParams(vmem_limit_bytes=VMEM_LIMIT))
    return lambda *args: fn(*args, *deps)


def _sds(shape, dtype):
    return jax.ShapeDtypeStruct(tuple(shape), dtype)


def _sigmoid(x):
    return 1.0 / (1.0 + jnp.exp(-x))


def _acc_rows(ref, val, first):
    @pl.when(first)
    def _():
        ref[...] = val

    @pl.when(jnp.logical_not(first))
    def _():
        ref[...] += val


def _colsum(v):
    return jnp.sum(v, axis=0, keepdims=True)


_DIMS = {"nn": ((1,), (0,)), "nt": ((1,), (1,)), "tn": ((0,), (0,))}


def _matmul(name, a, b, mode, *, grid, a_spec, b_spec, o_spec, o_shape, o_dtype, k_axis=None, nk=1,
            acc_shape=None, bias=None, bias_spec=None, res=None, res_spec=None, res_scale=1.0, deps=()):
    dims = (_DIMS[mode], ((), ()))
    has_bias, has_res = bias is not None, res is not None

    def body(*refs):
        a_ref, b_ref = refs[0], refs[1]
        pos = 2
        bias_ref = res_ref = None
        if has_bias:
            bias_ref = refs[pos]
            pos += 1
        if has_res:
            res_ref = refs[pos]
            pos += 1
        o_ref = refs[pos]
        acc_ref = refs[pos + 1] if nk > 1 else None
        p = lax.dot_general(a_ref[...].astype(BF16), b_ref[...].astype(BF16), dims, preferred_element_type=F32)

        def finish(acc):
            if has_bias:
                acc = acc + bias_ref[...]
            if has_res:
                acc = acc + res_scale * res_ref[...]
            o_ref[...] = acc.astype(o_dtype)

        if nk == 1:
            finish(p)
        else:
            k = pl.program_id(k_axis)

            @pl.when(k == 0)
            def _():
                acc_ref[...] = p

            @pl.when(k > 0)
            def _():
                acc_ref[...] += p

            @pl.when(k == nk - 1)
            def _():
                finish(acc_ref[...])

    ins, specs = [a, b], [a_spec, b_spec]
    if has_bias:
        ins.append(bias)
        specs.append(bias_spec)
    if has_res:
        ins.append(res)
        specs.append(res_spec)
    scratch = [pltpu.VMEM(acc_shape, F32)] if nk > 1 else []
    return _call(body, name, grid, specs, o_spec, _sds(o_shape, o_dtype), scratch, deps=deps)(*ins)


TM_LN = 512


def _matmul_ln(name, a, b, x_res, bias3, g3, b3, l_bias, l_norm):
    t, d = x_res.shape
    tm = min(TM_LN, t)
    blocked = a.ndim == 3

    def body(a_ref, b_ref, x_ref, bias_ref, g_ref, be_ref, o_ref, ob_ref, xh_ref, rs_ref):
        if blocked:
            y = None
            for k in range(a.shape[0]):
                p = jnp.dot(a_ref[k], b_ref[k], preferred_element_type=F32)
                y = p if y is None else y + p
        else:
            y = jnp.dot(a_ref[...], b_ref[...], preferred_element_type=F32)
        xhat, rstd = _ln_stats(ALPHA * x_ref[...] + y + bias_ref[...])
        out = xhat * g_ref[...] + be_ref[...]
        o_ref[...] = out
        ob_ref[...] = out.astype(BF16)
        xh_ref[...] = xhat
        rs_ref[...] = rstd

    if blocked:
        a_spec = pl.BlockSpec((a.shape[0], tm, a.shape[2]), lambda i: (0, i, 0))
        b_spec = pl.BlockSpec(b.shape, lambda i: (0, 0, 0))
    else:
        a_spec = pl.BlockSpec((tm, a.shape[1]), lambda i: (i, 0))
        b_spec = pl.BlockSpec(b.shape, lambda i: (0, 0))
    row = pl.BlockSpec((tm, d), lambda i: (i, 0))
    stat = pl.BlockSpec((tm, 1), lambda i: (i, 0))

    def vec(l):
        return pl.BlockSpec((None, 1, d), lambda i: (l, 0, 0))

    return _call(body, name, (t // tm,), [a_spec, b_spec, row, vec(l_bias), vec(l_norm), vec(l_norm)],
                 [row, row, row, stat],
                 [_sds((t, d), F32), _sds((t, d), BF16), _sds((t, d), F32), _sds((t, 1), F32)])(
                     a, b, x_res, bias3, g3, b3)


def _mesh_pos():
    return lax.axis_index("x"), lax.axis_index("y"), lax.axis_index("c")


def _any_specs(n):
    return [pl.BlockSpec(memory_space=pl.ANY)] * n


def _all_gather(name, srcs):
    n = len(srcs)

    def body(*refs):
        src, out = refs[:n], refs[n:2 * n]
        send_sems, recv_sems, local_sems = refs[2 * n:]
        x, y, c = _mesh_pos()
        me, sibling = (x, y, c), (x, y, 1 - c)
        chips = [(1 - x, y), (x, 1 - y), (1 - x, 1 - y)]

        def slot(k, p):
            return out[k].at[:, 4 * p[0] + 2 * p[1] + p[2]]

        def copy(k, idx, block, to, s=None):
            return pltpu.make_async_remote_copy(
                src_ref=slot(k, block) if s is None else s, dst_ref=slot(k, block),
                send_sem=send_sems.at[k * 7 + idx], recv_sem=recv_sems.at[k * 7 + idx],
                device_id=to, device_id_type=MESH)

        local = [pltpu.make_async_copy(src[k], slot(k, me), local_sems.at[k]) for k in range(n)]
        for cp in local:
            cp.start()
        first = []
        for k in range(n):
            first.append(copy(k, 0, me, sibling, src[k]))
            for j, chip in enumerate(chips):
                first.append(copy(k, 1 + j, me, (*chip, c), src[k]))
        for cp in first:
            cp.start()
        passed = []
        for j, chip in enumerate(chips):
            for k in range(n):
                copy(k, 1 + j, (*chip, c), me).wait_recv()
                cp = copy(k, 4 + j, (*chip, c), sibling)
                cp.start()
                passed.append(cp)
        for k in range(n):
            copy(k, 0, sibling, me).wait_recv()
            for j, chip in enumerate(chips):
                copy(k, 4 + j, (*chip, 1 - c), me).wait_recv()
        for cp in first + passed:
            cp.wait_send()
        for cp in local:
            cp.wait()

    out_shape = [_sds((s.shape[0], N_DEV) + s.shape[1:], s.dtype) for s in srcs]
    return _call(body, name, (), [pl.BlockSpec(memory_space=pltpu.VMEM)] * n, _any_specs(n), out_shape,
                 [pltpu.SemaphoreType.DMA((7 * n,)), pltpu.SemaphoreType.DMA((7 * n,)),
                  pltpu.SemaphoreType.DMA((n,))])(*srcs)


HBM_SPEC = pl.BlockSpec(memory_space=pltpu.HBM)
SEM_SPEC = pl.BlockSpec(memory_space=pltpu.SEMAPHORE)
N_PEER = N_DEV - 1


def _split_call(body, name, in_specs, out_specs, out_shape, aliases):
    return pl.pallas_call(
        body, name=name, in_specs=in_specs, out_specs=out_specs, out_shape=out_shape, input_output_aliases=aliases,
        compiler_params=pltpu.CompilerParams(has_side_effects=pltpu.SideEffectType.DATAFLOW_SIDE_EFFECTING))


def _peers(x, y, c):
    return [(1 - x if q & 4 else x, 1 - y if q & 2 else y, 1 - c if q & 1 else c) for q in range(1, N_DEV)]


def _in_hbm(a):
    return pltpu.with_memory_space_constraint(a, pltpu.HBM)


def _place_own(name, srcs, deps=()):
    n = len(srcs)

    def body(*refs):
        src, out, sems = refs[:n], refs[n:2 * n], refs[2 * n]
        x, y, c = _mesh_pos()
        dev = 4 * x + 2 * y + c
        copies = [pltpu.make_async_copy(src[k], out[k].at[dev], sems.at[k]) for k in range(n)]
        for cp in copies:
            cp.start()
        for cp in copies:
            cp.wait()

    return _call(body, name, (), [pl.BlockSpec(memory_space=pltpu.VMEM)] * n, _any_specs(n),
                 [_sds((N_DEV,) + s.shape, s.dtype) for s in srcs], [pltpu.SemaphoreType.DMA((n,))],
                 deps=deps)(*srcs)


def _gather_start(name, lands):
    n = len(lands)

    def body(*refs):
        land, send_sems, recv_sems = refs[:n], refs[n], refs[n + 1]
        x, y, c = _mesh_pos()
        dev = 4 * x + 2 * y + c
        for k in range(n):
            for peer in _peers(x, y, c):
                pltpu.make_async_remote_copy(
                    src_ref=land[k].at[dev], dst_ref=land[k].at[dev], send_sem=send_sems.at[k],
                    recv_sem=recv_sems.at[k], device_id=peer, device_id_type=MESH).start()
        token = refs[-1]
        token[...] = jnp.zeros_like(token)

    outs = _split_call(
        body, name, [HBM_SPEC] * n, [SEM_SPEC, SEM_SPEC] + [HBM_SPEC] * n + [pl.BlockSpec(memory_space=pltpu.VMEM)],
        [pltpu.SemaphoreType.DMA((n,)), pltpu.SemaphoreType.DMA((n,))] + [pltpu.HBM(a.shape, a.dtype) for a in lands]
        + [_sds((SUBLANES, LANES), F32)],
        {k: 2 + k for k in range(n)})(*[_in_hbm(a) for a in lands])
    return outs[0], outs[1], list(outs[2:2 + n]), outs[-1]


def _wait_seven(src_ref, dst_ref, send_sem, recv_sem):
    cp = pltpu.make_async_remote_copy(
        src_ref=src_ref.at[pl.ds(0, N_PEER)], dst_ref=dst_ref.at[pl.ds(0, N_PEER)], send_sem=send_sem,
        recv_sem=recv_sem, device_id=_mesh_pos(), device_id_type=MESH)
    cp.wait_send()
    cp.wait_recv()


def _gather_wait(name, land, send_sems, recv_sems, k, after):
    def body(land_ref, send_ref, recv_ref, after_ref, out_ref):
        _wait_seven(land_ref, land_ref, send_ref.at[k], recv_ref.at[k])

    return _split_call(body, name, [HBM_SPEC, SEM_SPEC, SEM_SPEC, pl.BlockSpec(memory_space=pl.ANY)], HBM_SPEC,
                       pltpu.HBM(land.shape, land.dtype), {0: 0})(land, send_sems, recv_sems, after)


def _scatter_start(name, parts_list):
    n = len(parts_list)

    def body(*refs):
        x, y, c = _mesh_pos()
        dev = 4 * x + 2 * y + c
        for k in range(n):
            parts_ref, land_ref = refs[2 * k], refs[2 * k + 1]
            send_sem, recv_sem = refs[2 * n + 4 * k], refs[2 * n + 4 * k + 1]
            for peer in _peers(x, y, c):
                pltpu.make_async_remote_copy(
                    src_ref=parts_ref.at[4 * peer[0] + 2 * peer[1] + peer[2]], dst_ref=land_ref.at[dev],
                    send_sem=send_sem, recv_sem=recv_sem, device_id=peer, device_id_type=MESH).start()
        token = refs[-1]
        token[...] = jnp.zeros_like(token)

    ins, out_specs, out_shape, aliases = [], [], [], {}
    for k, parts in enumerate(parts_list):
        buf = pltpu.HBM(parts.shape, parts.dtype)
        ins += [_in_hbm(parts), _in_hbm(lax.empty(parts.shape, parts.dtype))]
        out_specs += [SEM_SPEC, SEM_SPEC, HBM_SPEC, HBM_SPEC]
        out_shape += [pltpu.SemaphoreType.DMA(()), pltpu.SemaphoreType.DMA(()), buf, buf]
        aliases.update({2 * k: 4 * k + 2, 2 * k + 1: 4 * k + 3})
    outs = _split_call(body, name, [HBM_SPEC] * (2 * n), out_specs + [pl.BlockSpec(memory_space=pltpu.VMEM)],
                       out_shape + [_sds((SUBLANES, LANES), F32)], aliases)(*ins)
    return [tuple(outs[4 * k:4 * k + 4]) for k in range(n)], outs[-1]


def _scatter_wait(name, started, after):
    n = len(started)

    def body(*refs):
        for k in range(n):
            send_sem, recv_sem, parts_ref, land_ref = refs[4 * k:4 * k + 4]
            _wait_seven(parts_ref, land_ref, send_sem, recv_sem)

    flat = [a for s in started for a in s]
    outs = _split_call(
        body, name, [SEM_SPEC, SEM_SPEC, HBM_SPEC, HBM_SPEC] * n + [pl.BlockSpec(memory_space=pl.ANY)],
        [HBM_SPEC, HBM_SPEC] * n, [pltpu.HBM(a.shape, a.dtype) for s in started for a in s[2:]],
        {4 * k + 2 + t: 2 * k + t for k in range(n) for t in range(2)})(*flat, after)
    return list(outs[0::2]), list(outs[1::2])


def _to_segments(a, tile):
    seg = tile // SUBLANES
    return a.reshape((a.shape[0] // tile, SUBLANES, seg) + a.shape[1:]).swapaxes(1, 2).reshape(a.shape)


def _from_segments(a, tile):
    seg = tile // SUBLANES
    return a.reshape((a.shape[0] // tile, seg, SUBLANES) + a.shape[1:]).swapaxes(1, 2).reshape(a.shape)


def _chunk(ref, q):
    return ref[q * SUBLANES:(q + 1) * SUBLANES, :]


def _fill_wrap_prev(x_ref, halo_ref, wrap_ref, n_wrap, n_halo, seg, keep):
    sub = lax.broadcasted_iota(jnp.int32, (SUBLANES, x_ref.shape[-1]), 0)
    for j in range(n_wrap):
        q = seg - n_wrap + j
        hq = q - (seg - n_halo)
        row = halo_ref[hq * SUBLANES + SUBLANES - 1:(hq + 1) * SUBLANES, :] * keep
        wrap_ref[j * SUBLANES:(j + 1) * SUBLANES, :] = jnp.where(sub == 0, row, pltpu.roll(_chunk(x_ref, q), 1, 0))


def _fill_wrap_next(x_ref, halo_ref, wrap_ref, n_wrap, keep):
    sub = lax.broadcasted_iota(jnp.int32, (SUBLANES, x_ref.shape[-1]), 0)
    for j in range(n_wrap):
        row = halo_ref[j * SUBLANES:j * SUBLANES + 1, :] * keep
        wrap_ref[j * SUBLANES:(j + 1) * SUBLANES, :] = jnp.where(
            sub == SUBLANES - 1, row, pltpu.roll(_chunk(x_ref, j), SUBLANES - 1, 0))


def _past(x_ref, wrap_ref, q, d, n_wrap):
    return _chunk(x_ref, q - d) if q >= d else _chunk(wrap_ref, q - d + n_wrap)


def _future(x_ref, wrap_ref, q, d, seg):
    return _chunk(x_ref, q + d) if q + d < seg else _chunk(wrap_ref, q + d - seg)


def _conv_fwd(x_ref, wrap_ref, w_ref, b_ref, out_ref, seg, k_taps):
    bias = jnp.broadcast_to(b_ref[...], (SUBLANES, x_ref.shape[-1]))
    for q in range(seg):
        acc = bias
        for k in range(k_taps):
            acc = acc + w_ref[k:k + 1, :] * _past(x_ref, wrap_ref, q, k_taps - 1 - k, k_taps - 1)
        out_ref[q * SUBLANES:(q + 1) * SUBLANES, :] = acc


def _conv_bwd_data(d_ref, wrap_ref, w_ref, out_ref, seg, k_taps):
    for q in range(seg):
        acc = None
        for k in range(k_taps):
            term = w_ref[k:k + 1, :] * _future(d_ref, wrap_ref, q, k_taps - 1 - k, seg)
            acc = term if acc is None else acc + term
        out_ref[q * SUBLANES:(q + 1) * SUBLANES, :] = acc


def _conv_bwd_taps(d_ref, x_ref, wrap_ref, dw_ref, seg, k_taps):
    for k in range(k_taps):
        part = None
        for q in range(seg):
            term = _chunk(d_ref, q) * _past(x_ref, wrap_ref, q, k_taps - 1 - k, k_taps - 1)
            part = term if part is None else part + term
        dw_ref[k:k + 1, :] += _colsum(part)


def _tile_halo_specs(tm, width_block, n_halo, n_tiles, block_of):
    rows = n_halo * SUBLANES
    per = tm // rows
    tile = pl.BlockSpec(width_block(tm), lambda n, i: block_of(n, i))
    prev = pl.BlockSpec(width_block(rows), lambda n, i: block_of(n, jnp.maximum(i * per - 1, 0)))
    nxt = pl.BlockSpec(width_block(rows), lambda n, i: block_of(n, jnp.minimum((i + 1) * per, n_tiles * per - 1)))
    return tile, prev, nxt


def _ln_stats(v):
    mu = jnp.mean(v, axis=-1, keepdims=True)
    vc = v - mu
    var = jnp.mean(vc * vc, axis=-1, keepdims=True)
    rstd = lax.rsqrt(var + LN_EPS)
    return vc * rstd, rstd


def _ln_backward(dxhat, xhat, rstd):
    m1 = jnp.mean(dxhat, axis=-1, keepdims=True)
    m2 = jnp.mean(dxhat * xhat, axis=-1, keepdims=True)
    return rstd * (dxhat - m1 - xhat * m2)


def _row_spec(tm, width):
    return pl.BlockSpec((tm, width), lambda i: (i, 0))


def _param_spec(l, width):
    return pl.BlockSpec((None, 1, width), lambda *_: (l, 0, 0))


def _ln_bwd_rows(dout, xh_ref, rs_ref, g_ref, dr_ref, dg_ref, db_ref, dsum_ref, first, drb_ref=None):
    xhat = xh_ref[...]
    dr = _ln_backward(dout * g_ref[...], xhat, rs_ref[...])
    dr_ref[...] = dr
    if drb_ref is not None:
        drb_ref[...] = dr.astype(BF16)
    _acc_rows(dg_ref, _colsum(dout * xhat), first)
    _acc_rows(db_ref, _colsum(dout), first)
    _acc_rows(dsum_ref, _colsum(dr), first)


def _ln_bwd_specs(tm, d, l, row_of):
    vec = pl.BlockSpec((1, d), lambda *_: (0, 0))
    ins = [pl.BlockSpec((tm, d), row_of), pl.BlockSpec((tm, 1), row_of), _param_spec(l, d)]
    return ins, [pl.BlockSpec((tm, d), row_of), vec, vec, vec]


def _ln_res_bwd(name, dout, xhat, rstd, g3, l, deps=()):
    t, d = dout.shape
    tm = min(TM_ROW, t)

    def body(do_ref, xh_ref, rs_ref, g_ref, dr_ref, dg_ref, db_ref, dc_ref, drb_ref):
        _ln_bwd_rows(do_ref[...], xh_ref, rs_ref, g_ref, dr_ref, dg_ref, db_ref, dc_ref, pl.program_id(0) == 0,
                     drb_ref)

    ins, outs = _ln_bwd_specs(tm, d, l, lambda i: (i, 0))
    return _call(body, name, (t // tm,), [_row_spec(tm, d)] + ins, outs + [_row_spec(tm, d)],
                 [_sds((t, d), F32)] + [_sds((1, d), F32)] * 3 + [_sds((t, d), BF16)],
                 deps=deps)(dout, xhat, rstd, g3)


def _glu_bwd(name, du, h):
    t, c2 = h.shape
    c = c2 // 2
    tm = min(TM_ROW, t)

    def body(du_ref, a_ref, g_ref, dh_ref, db_ref):
        first = pl.program_id(0) == 0
        du_v, a = du_ref[...], a_ref[...]
        sg = _sigmoid(g_ref[...])
        da = du_v * sg
        dg = du_v * a * sg * (1.0 - sg)
        dh_ref[:, :c] = da.astype(BF16)
        dh_ref[:, c:] = dg.astype(BF16)
        _acc_rows(db_ref.at[:, :c], _colsum(da), first)
        _acc_rows(db_ref.at[:, c:], _colsum(dg), first)

    return _call(body, name, (t // tm,),
                 [_row_spec(tm, c), pl.BlockSpec((tm, c), lambda i: (i, 0)), pl.BlockSpec((tm, c), lambda i: (i, 1))],
                 [_row_spec(tm, c2), pl.BlockSpec((1, c2), lambda i: (0, 0))],
                 [_sds((t, c2), BF16), _sds((1, c2), F32)])(du, h, h)


CONV_CB = 512
TAPS_PAD = 32


def _dwconv31(name, u, w3, b3, l, seq):
    t, c = u.shape
    tm, cb = TM_EW, CONV_CB
    seg, seq_tiles, n_tiles = tm // SUBLANES, seq // tm, t // tm
    n_wrap = CONV_K - 1
    tile, prev, _ = _tile_halo_specs(tm, lambda rows: (rows, cb), seg, n_tiles, lambda n, r: (r, n))

    def body(u_ref, halo_ref, w_ref, b_ref, o_ref, wrap_ref):
        keep = (pl.program_id(1) % seq_tiles != 0).astype(F32)
        _fill_wrap_prev(u_ref, halo_ref, wrap_ref, n_wrap, seg, seg, keep)
        _conv_fwd(u_ref, wrap_ref, w_ref, b_ref, o_ref, seg, CONV_K)

    return _call(body, name, (c // cb, n_tiles),
                 [tile, prev, pl.BlockSpec((None, CONV_K, cb), lambda n, i: (l, 0, n)),
                  pl.BlockSpec((None, 1, cb), lambda n, i: (l, 0, n))],
                 tile, _sds((t, c), F32), [pltpu.VMEM((n_wrap * SUBLANES, cb), F32)])(u, u, w3, b3)


def _dwconv31_bwd(name, dc, u, w3, l, seq):
    t, c = dc.shape
    tm, cb = TM_EW, CONV_CB
    seg, seq_tiles, n_tiles = tm // SUBLANES, seq // tm, t // tm
    n_wrap = CONV_K - 1
    tile, prev, nxt = _tile_halo_specs(tm, lambda rows: (rows, cb), seg, n_tiles, lambda n, r: (r, n))

    def body(dc_ref, dcn_ref, u_ref, up_ref, w_ref, du_ref, dw_ref, dwrap_ref, uwrap_ref):
        i = pl.program_id(1)
        keep_prev = (i % seq_tiles != 0).astype(F32)
        keep_next = (i % seq_tiles != seq_tiles - 1).astype(F32)
        _fill_wrap_next(dc_ref, dcn_ref, dwrap_ref, n_wrap, keep_next)
        _conv_bwd_data(dc_ref, dwrap_ref, w_ref, du_ref, seg, CONV_K)

        @pl.when(i == 0)
        def _():
            dw_ref[...] = jnp.zeros_like(dw_ref)

        _fill_wrap_prev(u_ref, up_ref, uwrap_ref, n_wrap, seg, seg, keep_prev)
        _conv_bwd_taps(dc_ref, u_ref, uwrap_ref, dw_ref, seg, CONV_K)

    wrap = pltpu.VMEM((n_wrap * SUBLANES, cb), F32)
    return _call(body, name, (c // cb, n_tiles),
                 [tile, nxt, tile, prev, pl.BlockSpec((None, CONV_K, cb), lambda n, i: (l, 0, n))],
                 [tile, pl.BlockSpec((TAPS_PAD, cb), lambda n, i: (0, n))],
                 [_sds((t, c), F32), _sds((TAPS_PAD, c), F32)], [wrap, wrap])(dc, dc, u, u, w3)


def _ln_silu(name, cx, g3, b3, l):
    t, d = cx.shape
    tm = min(TM_ROW, t)

    def body(c_ref, g_ref, b_ref, o_ref):
        xhat, _ = _ln_stats(c_ref[...])
        nv = xhat * g_ref[...] + b_ref[...]
        o_ref[...] = (nv * _sigmoid(nv)).astype(BF16)

    return _call(body, name, (t // tm,), [_row_spec(tm, d), _param_spec(l, d), _param_spec(l, d)],
                 _row_spec(tm, d), _sds((t, d), BF16))(cx, g3, b3)


def _ln_silu_bwd(name, ds, cx, g3, b3, l, deps=()):
    t, d = cx.shape
    tm = min(TM_ROW, t)

    def body(ds_ref, c_ref, g_ref, b_ref, dc_ref, dg_ref, db_ref, dsum_ref):
        first = pl.program_id(0) == 0
        xhat, rstd = _ln_stats(c_ref[...])
        g = g_ref[...]
        nv = xhat * g + b_ref[...]
        sg = _sigmoid(nv)
        dn = ds_ref[...] * (sg * (1.0 + nv * (1.0 - sg)))
        dc = _ln_backward(dn * g, xhat, rstd)
        dc_ref[...] = dc
        _acc_rows(dg_ref, _colsum(dn * xhat), first)
        _acc_rows(db_ref, _colsum(dn), first)
        _acc_rows(dsum_ref, _colsum(dc), first)

    vec = pl.BlockSpec((1, d), lambda i: (0, 0))
    return _call(body, name, (t // tm,),
                 [_row_spec(tm, d), _row_spec(tm, d), _param_spec(l, d), _param_spec(l, d)],
                 [_row_spec(tm, d), vec, vec, vec],
                 [_sds((t, d), F32)] + [_sds((1, d), F32)] * 3, deps=deps)(ds, cx, g3, b3)


FFN_HALO = FFN_K - 1


def _ffn_conv(x_ref, halo_ref, wrap_ref, w_ref, b_ref, keep, seg, out_ref):
    _fill_wrap_prev(x_ref, halo_ref, wrap_ref, FFN_K - 1, FFN_HALO, seg, keep)
    _conv_fwd(x_ref, wrap_ref, w_ref, b_ref, out_ref, seg, FFN_K)


TM_FFN = 512
WRAP_ROWS = FFN_HALO * SUBLANES


def _sub_tiles(x_ref, prev_ref, next_ref, keep_prev, keep_next, n_sub):
    out = []
    for s in range(n_sub):
        tile = x_ref.at[pl.ds(s * TM_EW, TM_EW)]
        prev = prev_ref if s == 0 else x_ref.at[pl.ds(s * TM_EW - WRAP_ROWS, WRAP_ROWS)]
        nxt = next_ref if s == n_sub - 1 else x_ref.at[pl.ds((s + 1) * TM_EW, WRAP_ROWS)]
        out.append((tile, prev, keep_prev if s == 0 else 1.0, nxt, keep_next if s == n_sub - 1 else 1.0))
    return out


def _rows(ref, s, rows):
    return ref.at[pl.ds(s * rows, rows)]


def _ffn_specs(tm, fb, n_tiles):
    return _tile_halo_specs(tm, lambda rows: (None, rows, fb), FFN_HALO, n_tiles, lambda n, r: (n, r, 0))


def _ffn_up_act(name, x, w_up, b_up4, wdw, bdw, l, seq):
    t, d = x.shape
    nb, fb, _ = w_up.shape
    half = nb // 2
    tm = min(TM_FFN, seq)
    n_sub, seg, seq_steps, n_steps = tm // TM_EW, TM_EW // SUBLANES, seq // tm, t // tm
    per = tm // WRAP_ROWS
    nt_dims = (_DIMS["nt"], ((), ()))

    def body(x_ref, xp_ref, ug_ref, uv_ref, bug_ref, buv_ref, wg_ref, wv_ref, bg_ref, bv_ref,
             hg_ref, hv_ref, a_ref, pg_ref, pv_ref, gwrap_ref, vwrap_ref, cg_ref, cv_ref):
        keep = (pl.program_id(1) % seq_steps != 0).astype(F32)
        xb, xpb = x_ref[...].astype(BF16), xp_ref[...].astype(BF16)
        hg_ref[...] = lax.dot_general(xb, ug_ref[...], nt_dims, preferred_element_type=F32) + bug_ref[...]
        pg_ref[...] = lax.dot_general(xpb, ug_ref[...], nt_dims, preferred_element_type=F32) + bug_ref[...]
        for s, (tile, prev, kp, _, _) in enumerate(_sub_tiles(hg_ref, pg_ref, None, keep, None, n_sub)):
            _ffn_conv(tile, prev, gwrap_ref, wg_ref, bg_ref, kp, seg, _rows(cg_ref, s, TM_EW))
        hv_ref[...] = lax.dot_general(xb, uv_ref[...], nt_dims, preferred_element_type=F32) + buv_ref[...]
        pv_ref[...] = lax.dot_general(xpb, uv_ref[...], nt_dims, preferred_element_type=F32) + buv_ref[...]
        for s, (tile, prev, kp, _, _) in enumerate(_sub_tiles(hv_ref, pv_ref, None, keep, None, n_sub)):
            _ffn_conv(tile, prev, vwrap_ref, wv_ref, bv_ref, kp, seg, _rows(cv_ref, s, TM_EW))
        cg = cg_ref[...]
        a_ref[...] = (cg * _sigmoid(cg) * cv_ref[...]).astype(BF16)

    def blk(shift):
        return pl.BlockSpec((None, fb, d), lambda n, i: (n + shift, 0, 0))

    def vec(shift, rows):
        return pl.BlockSpec((None, None, rows, fb), lambda n, i: (l, n + shift, 0, 0))

    out = pl.BlockSpec((None, tm, fb), lambda n, i: (n, i, 0))
    tmp = pltpu.VMEM((tm, fb), F32)
    halo = pltpu.VMEM((WRAP_ROWS, fb), F32)
    return _call(body, name, (half, n_steps),
                 [pl.BlockSpec((tm, d), lambda n, i: (i, 0)),
                  pl.BlockSpec((WRAP_ROWS, d), lambda n, i: (jnp.maximum(i * per - 1, 0), 0)),
                  blk(0), blk(half), vec(0, 1), vec(half, 1), vec(0, FFN_K), vec(half, FFN_K), vec(0, 1), vec(half, 1)],
                 [out, out, out],
                 [_sds((half, t, fb), F32), _sds((half, t, fb), F32), _sds((half, t, fb), BF16)],
                 [halo, halo, halo, halo, tmp, tmp])(x, x, w_up, w_up, b_up4, b_up4, wdw, wdw, bdw, bdw)


def _ffn_act_bwd(name, dy, w_down, hg, hv, wdw, bdw, l, seq, deps=()):
    half, t, fb = hg.shape
    d = dy.shape[-1]
    tm = min(TM_FFN, seq)
    n_sub, seg, seq_steps, n_steps = tm // TM_EW, TM_EW // SUBLANES, seq // tm, t // tm
    tile, prev, _ = _ffn_specs(tm, fb, n_steps)

    def body(dy_ref, wd_ref, g_ref, gp_ref, v_ref, vp_ref, wg_ref, wv_ref, bg_ref, bv_ref,
             dg_ref, dv_ref, dbg_ref, dbv_ref, dwg_ref, dwv_ref, gwrap_ref, vwrap_ref, cg_ref, cv_ref):
        i = pl.program_id(1)
        first = i == 0
        keep = (i % seq_steps != 0).astype(F32)
        da = lax.dot_general(dy_ref[...].astype(BF16), wd_ref[...], (_DIMS["nt"], ((), ())),
                             preferred_element_type=F32)
        g_tiles = _sub_tiles(g_ref, gp_ref, None, keep, None, n_sub)
        v_tiles = _sub_tiles(v_ref, vp_ref, None, keep, None, n_sub)
        for s in range(n_sub):
            _ffn_conv(g_tiles[s][0], g_tiles[s][1], _rows(gwrap_ref, s, WRAP_ROWS), wg_ref, bg_ref, g_tiles[s][2],
                      seg, _rows(cg_ref, s, TM_EW))
            _ffn_conv(v_tiles[s][0], v_tiles[s][1], _rows(vwrap_ref, s, WRAP_ROWS), wv_ref, bv_ref, v_tiles[s][2],
                      seg, _rows(cv_ref, s, TM_EW))
        cg, cv = cg_ref[...], cv_ref[...]
        sg = _sigmoid(cg)
        dcv = da * cg * sg
        dcg = da * cv * sg * (1.0 + cg * (1.0 - sg))
        dg_ref[...] = dcg
        dv_ref[...] = dcv
        _acc_rows(dbg_ref, _colsum(dcg), first)
        _acc_rows(dbv_ref, _colsum(dcv), first)

        @pl.when(first)
        def _():
            dwg_ref[...] = jnp.zeros_like(dwg_ref)
            dwv_ref[...] = jnp.zeros_like(dwv_ref)

        for s in range(n_sub):
            _conv_bwd_taps(_rows(dg_ref, s, TM_EW), g_tiles[s][0], _rows(gwrap_ref, s, WRAP_ROWS), dwg_ref, seg, FFN_K)
            _conv_bwd_taps(_rows(dv_ref, s, TM_EW), v_tiles[s][0], _rows(vwrap_ref, s, WRAP_ROWS), dwv_ref, seg, FFN_K)

    def vec(shift, rows):
        return pl.BlockSpec((None, None, rows, fb), lambda n, i: (l, n + shift, 0, 0))

    def acc(rows):
        return pl.BlockSpec((None, rows, fb), lambda n, i: (n, 0, 0))

    wrap = pltpu.VMEM((n_sub * WRAP_ROWS, fb), F32)
    tmp = pltpu.VMEM((tm, fb), F32)
    return _call(body, name, (half, n_steps),
                 [pl.BlockSpec((tm, d), lambda n, i: (i, 0)), pl.BlockSpec((None, fb, d), lambda n, i: (n, 0, 0)),
                  tile, prev, tile, prev, vec(0, FFN_K), vec(half, FFN_K), vec(0, 1), vec(half, 1)],
                 [tile, tile, acc(1), acc(1), acc(SUBLANES), acc(SUBLANES)],
                 [_sds((half, t, fb), F32), _sds((half, t, fb), F32), _sds((half, 1, fb), F32),
                  _sds((half, 1, fb), F32), _sds((half, SUBLANES, fb), F32), _sds((half, SUBLANES, fb), F32)],
                 [wrap, wrap, tmp, tmp], deps=deps)(dy, w_down, hg, hg, hv, hv, wdw, wdw, bdw, bdw)


def _ffn_conv_t_dx(name, dcg, dcv, wdw, w_up, res, xhat, rstd, g3, l, seq, deps=()):
    half, t, fb = dcg.shape
    nb, d = 2 * half, res.shape[-1]
    tm = min(TM_FFN, seq)
    n_sub, seg, seq_steps, n_steps = tm // TM_EW, TM_EW // SUBLANES, seq // tm, t // tm
    per = tm // WRAP_ROWS

    pair = 2
    n_pairs, half_pairs = nb // pair, half // pair

    def body(g_ref, gn_ref, v_ref, vn_ref, w_ref, up_ref, res_ref, xh_ref, rs_ref, gam_ref,
             dh_ref, db_ref, dr_ref, dgam_ref, dbeta_ref, dsum_ref, wrap_ref, out_ref, acc_ref):
        i, m = pl.program_id(0), pl.program_id(1)
        keep = (i % seq_steps != seq_steps - 1).astype(F32)

        def conv_t(d_ref, dn_ref, b):
            for s, (sub, _, _, nx, kn) in enumerate(_sub_tiles(d_ref.at[b], None, dn_ref.at[b], None, keep, n_sub)):
                _fill_wrap_next(sub, nx, wrap_ref, FFN_K - 1, kn)
                _conv_bwd_data(sub, wrap_ref, w_ref.at[b], _rows(out_ref, s, TM_EW), seg, FFN_K)

        p = None
        for b in range(pair):
            @pl.when(m < half_pairs)
            def _(b=b):
                conv_t(g_ref, gn_ref, b)

            @pl.when(m >= half_pairs)
            def _(b=b):
                conv_t(v_ref, vn_ref, b)

            dh = out_ref[...]
            dhb = dh.astype(BF16)
            dh_ref[b] = dhb
            _acc_rows(db_ref.at[pair * m + b], _colsum(dh), i == 0)
            part = jnp.dot(dhb, up_ref[b], preferred_element_type=F32)
            p = part if p is None else p + part

        @pl.when(m == 0)
        def _():
            acc_ref[...] = p

        @pl.when(m > 0)
        def _():
            acc_ref[...] += p

        @pl.when(m == n_pairs - 1)
        def _():
            _ln_bwd_rows(acc_ref[...] + ALPHA * res_ref[...], xh_ref, rs_ref, gam_ref, dr_ref, dgam_ref, dbeta_ref,
                         dsum_ref, i == 0)

    def src(gate):
        def blk(m):
            return jnp.minimum(m, half_pairs - 1) if gate else jnp.maximum(m - half_pairs, 0)
        tile = pl.BlockSpec((pair, tm, fb), lambda i, m: (blk(m), i, 0))
        nxt = pl.BlockSpec((pair, WRAP_ROWS, fb),
                           lambda i, m: (blk(m), jnp.minimum((i + 1) * per, n_steps * per - 1), 0))
        return [tile, nxt]

    row = pl.BlockSpec((tm, d), lambda i, m: (i, 0))
    ln_ins, ln_outs = _ln_bwd_specs(tm, d, l, lambda i, m: (i, 0))
    tmp = pltpu.VMEM((tm, fb), F32)
    halo = pltpu.VMEM((WRAP_ROWS, fb), F32)
    return _call(body, name, (n_steps, n_pairs),
                 src(True) + src(False) +
                 [pl.BlockSpec((None, pair, FFN_K, fb), lambda i, m: (l, m, 0, 0)),
                  pl.BlockSpec((pair, fb, d), lambda i, m: (m, 0, 0)), row] + ln_ins,
                 [pl.BlockSpec((pair, tm, fb), lambda i, m: (m, i, 0)),
                  pl.BlockSpec((nb, 1, fb), lambda i, m: (0, 0, 0))] + ln_outs,
                 [_sds((nb, t, fb), BF16), _sds((nb, 1, fb), F32), _sds((t, d), F32)] + [_sds((1, d), F32)] * 3,
                 [halo, tmp, pltpu.VMEM((tm, d), F32)],
                 deps=deps)(dcg, dcg, dcv, dcv, wdw, w_up, res, xhat, rstd, g3)


def _gelu_parts(h):
    cdf = 0.5 * (1.0 + lax.erf(h * INV_SQRT2))
    return h * cdf, cdf


def _seg_axis(a, axis, fn):
    return jnp.moveaxis(fn(jnp.moveaxis(a, axis, 0), TM_EW), 0, axis)


def _sgu_operands(w_s, b_s):
    nl = w_s.shape[0]
    n_sub = TM_EW // CHUNK
    tril = jnp.tril(jnp.ones((CHUNK, CHUNK), dtype=bool))
    w_causal = jnp.where(tril, w_s, 0.0)
    w_tile = (jnp.eye(n_sub, dtype=F32)[None, None, :, None, :, None] * w_causal[:, :, None, :, None, :]).reshape(
        nl, GROUPS, TM_EW, TM_EW)
    w_tile = _seg_axis(_seg_axis(w_tile, 2, _to_segments), 3, _to_segments).astype(BF16)
    bs_tile = jnp.broadcast_to(b_s[:, :, None, :, None], (nl, GROUPS, n_sub, CHUNK, CHUNK)).reshape(
        nl, GROUPS, TM_EW, CHUNK)
    return w_tile, _seg_axis(bs_tile, 2, _to_segments)


def _sgu_param_grads(dwt, dbt):
    n_sub = TM_EW // CHUNK
    tril = jnp.tril(jnp.ones((CHUNK, CHUNK), dtype=bool))
    dwt = _seg_axis(_seg_axis(dwt, 1, _from_segments), 2, _from_segments).reshape(GROUPS, n_sub, CHUNK, n_sub, CHUNK)
    dw = sum(dwt[:, a, :, a, :] for a in range(n_sub))
    db = _seg_axis(dbt, 1, _from_segments).reshape(GROUPS, n_sub, CHUNK).sum(axis=1)
    return jnp.where(tril, dw, 0.0), db


def _sgu(name, h, g3, b3, wt, bst, l):
    t, c2 = h.shape
    c = c2 // 2
    tm = TM_EW

    def body(h_ref, g_ref, b_ref, wt_ref, bs_ref, o_ref):
        z, _ = _gelu_parts(h_ref[...])
        u = z[:, :c]
        xhat, _ = _ln_stats(z[:, c:])
        vnb = (xhat * g_ref[...] + b_ref[...]).astype(BF16)
        for gi in range(GROUPS):
            cs = slice(gi * CHUNK, (gi + 1) * CHUNK)
            sp = jnp.dot(wt_ref[gi], vnb[:, cs], preferred_element_type=F32) + bs_ref[gi]
            o_ref[:, cs] = (u[:, cs] * sp).astype(BF16)

    return _call(body, name, (t // tm,),
                 [_row_spec(tm, c2), _param_spec(l, c), _param_spec(l, c),
                  pl.BlockSpec((None, GROUPS, tm, tm), lambda i: (l, 0, 0, 0)),
                  pl.BlockSpec((None, GROUPS, tm, CHUNK), lambda i: (l, 0, 0, 0))],
                 _row_spec(tm, c), _sds((t, c), BF16))(h, g3, b3, wt, bst)


def _sgu_bwd(name, dq, h, g3, b3, wt, bst, l, deps=()):
    t, c2 = h.shape
    c = c2 // 2
    tm = TM_EW
    n_tiles = t // tm

    def body(dq_ref, h_ref, g_ref, b_ref, wt_ref, bs_ref,
             dh_ref, dbin_ref, dw_ref, dbs_ref, dg_ref, db_ref, du_ref, dvn_ref, bsum_ref):
        i = pl.program_id(0)
        first = i == 0
        hv = h_ref[...]
        z, cdf = _gelu_parts(hv)
        u = z[:, :c]
        xhat, rstd = _ln_stats(z[:, c:])
        g = g_ref[...]
        vnb = (xhat * g + b_ref[...]).astype(BF16)

        @pl.when(first)
        def _():
            dw_ref[...] = jnp.zeros_like(dw_ref)
            bsum_ref[...] = jnp.zeros_like(bsum_ref)

        for gi in range(GROUPS):
            cs = slice(gi * CHUNK, (gi + 1) * CHUNK)
            vb = vnb[:, cs]
            w = wt_ref[gi]
            sp = jnp.dot(w, vb, preferred_element_type=F32) + bs_ref[gi]
            dqb = dq_ref[:, cs]
            du_ref[:, cs] = dqb * sp
            dsp = dqb * u[:, cs]
            bsum_ref[gi] += dsp
            dspb = dsp.astype(BF16)
            dw_ref[gi] += lax.dot_general(dspb, vb, (_DIMS["nt"], ((), ())), preferred_element_type=F32)
            dvn_ref[:, cs] = lax.dot_general(w, dspb, (_DIMS["tn"], ((), ())), preferred_element_type=F32)

        dvn = dvn_ref[...]
        dv = _ln_backward(dvn * g, xhat, rstd)
        pdf = jnp.exp(-0.5 * hv * hv) * INV_SQRT2PI
        dgelu = cdf + hv * pdf
        dhu = du_ref[...] * dgelu[:, :c]
        dhv = dv * dgelu[:, c:]
        dh_ref[:, :c] = dhu.astype(BF16)
        dh_ref[:, c:] = dhv.astype(BF16)
        _acc_rows(dbin_ref.at[:, :c], _colsum(dhu), first)
        _acc_rows(dbin_ref.at[:, c:], _colsum(dhv), first)
        _acc_rows(dg_ref, _colsum(dvn * xhat), first)
        _acc_rows(db_ref, _colsum(dvn), first)

        @pl.when(i == n_tiles - 1)
        def _():
            dbs_ref[...] = jnp.sum(bsum_ref[...], axis=-1)

    vec = pl.BlockSpec((1, c), lambda i: (0, 0))
    return _call(body, name, (n_tiles,),
                 [_row_spec(tm, c), _row_spec(tm, c2), _param_spec(l, c), _param_spec(l, c),
                  pl.BlockSpec((None, GROUPS, tm, tm), lambda i: (l, 0, 0, 0)),
                  pl.BlockSpec((None, GROUPS, tm, CHUNK), lambda i: (l, 0, 0, 0))],
                 [_row_spec(tm, c2), pl.BlockSpec((1, c2), lambda i: (0, 0)),
                  pl.BlockSpec((GROUPS, tm, tm), lambda i: (0, 0, 0)),
                  pl.BlockSpec((GROUPS, tm), lambda i: (0, 0)), vec, vec],
                 [_sds((t, c2), BF16), _sds((1, c2), F32), _sds((GROUPS, tm, tm), F32),
                  _sds((GROUPS, tm), F32), _sds((1, c), F32), _sds((1, c), F32)],
                 [pltpu.VMEM((tm, c), F32), pltpu.VMEM((tm, c), F32), pltpu.VMEM((GROUPS, tm, CHUNK), F32)],
                 deps=deps)(dq, h, g3, b3, wt, bst)


def _loss(name, y, target):
    t, d = y.shape
    tm = min(TM_ROW, t)
    n_tiles = t // tm

    def body(y_ref, t_ref, l_ref, dy_ref, acc_ref):
        i = pl.program_id(0)
        diff = y_ref[...] - t_ref[...]
        dy_ref[...] = diff * (1.0 / d)
        _acc_rows(acc_ref, _colsum(diff * diff), i == 0)

        @pl.when(i == n_tiles - 1)
        def _():
            l_ref[...] = jnp.broadcast_to(jnp.sum(acc_ref[...], axis=-1, keepdims=True) * (0.5 / d), (1, LANES))

    return _call(body, name, (n_tiles,), [_row_spec(tm, d), _row_spec(tm, d)],
                 [pl.BlockSpec((1, LANES), lambda i: (0, 0)), _row_spec(tm, d)],
                 [_sds((1, LANES), F32), _sds((t, d), F32)], [pltpu.VMEM((1, d), F32)])(y, target)


def _adamw(g, w, m, v):
    m2 = ADAM_B1 * m + (1.0 - ADAM_B1) * g
    v2 = ADAM_B2 * v + (1.0 - ADAM_B2) * (g * g)
    m_hat = m2 / (1.0 - ADAM_B1 ** ADAM_STEP)
    v_hat = v2 / (1.0 - ADAM_B2 ** ADAM_STEP)
    delta = -ADAM_LR * (m_hat / (jnp.sqrt(v_hat) + ADAM_EPS) + ADAM_WD * w)
    return delta, m2, v2


ROW_TILE_CAP = 512


def _row_tile(rows, cap=ROW_TILE_CAP):
    if rows <= cap:
        return rows
    for tr in range(cap, 15, -16):
        if rows % tr == 0:
            return tr
    return rows


def _sum8_adamw(name, dev, lands, parts, w, m, v):
    nl = len(lands)
    _, r, c = lands[0].shape
    tr = _row_tile(r, cap=256)

    def body(dev_ref, *refs):
        land, own = refs[:nl], refs[nl:2 * nl]
        w_ref, m_ref, v_ref, g_ref, d_ref, m2_ref, v2_ref = refs[2 * nl:]
        layer, me = pl.program_id(0), dev_ref[0]
        for l in range(nl):
            @pl.when(layer == l)
            def _(l=l):
                g = None
                for s in range(N_DEV):
                    part = jnp.where(me == s, own[l][...], land[l][s]).astype(F32)
                    g = part if g is None else g + part
                delta, m2, v2 = _adamw(g, w_ref[...], m_ref[...], v_ref[...])
                g_ref[...] = g
                d_ref[...] = delta
                m2_ref[...] = m2
                v2_ref[...] = v2

    def rows_of(l, a, i):
        return jnp.where(a == l, i, 0)

    spec = pl.BlockSpec((None, tr, c), lambda a, i, dev_ref: (a, i, 0))
    in_specs = [pl.BlockSpec((N_DEV, tr, c), lambda a, i, dev_ref, l=l: (0, rows_of(l, a, i), 0)) for l in range(nl)]
    in_specs += [pl.BlockSpec((None, tr, c), lambda a, i, dev_ref, l=l: (dev_ref[0], rows_of(l, a, i), 0))
                 for l in range(nl)]
    grid_spec = pltpu.PrefetchScalarGridSpec(
        num_scalar_prefetch=1, grid=(nl, r // tr), in_specs=in_specs + [spec] * 3, out_specs=[spec] * 4)
    return pl.pallas_call(
        body, name=name, grid_spec=grid_spec, out_shape=[_sds(w.shape, F32)] * 4,
        compiler_params=pltpu.CompilerParams(vmem_limit_bytes=VMEM_LIMIT))(dev, *lands, *parts, w, m, v)


def _sum8(name, parts):
    _, r, c = parts.shape
    tr = _row_tile(r)

    def body(p_ref, o_ref):
        acc = p_ref[0]
        for s in range(1, N_DEV):
            acc = acc + p_ref[s]
        o_ref[...] = acc

    return _call(body, name, (r // tr,), [pl.BlockSpec((N_DEV, tr, c), lambda i: (0, i, 0))],
                 pl.BlockSpec((tr, c), lambda i: (i, 0)), _sds((r, c), F32))(parts)


def _adamw_small(name, gs, ws, ms, vs):
    n = len(gs)

    def body(*refs):
        g, w, m, v = (refs[k * n:(k + 1) * n] for k in range(4))
        d_out, m_out, v_out = (refs[(4 + k) * n:(5 + k) * n] for k in range(3))
        for k in range(n):
            d_out[k][...], m_out[k][...], v_out[k][...] = _adamw(g[k][...], w[k][...], m[k][...], v[k][...])

    vmem = pl.BlockSpec(memory_space=pltpu.VMEM)
    outs = _call(body, name, (), [vmem] * (4 * n), [vmem] * (3 * n), [_sds(w.shape, F32) for w in ws] * 3)(
        *gs, *ws, *ms, *vs)
    return outs[:n], outs[n:2 * n], outs[2 * n:]


def _pack(arrs, row_multiple=SUBLANES):
    pieces, rows = [], 0
    for a in arrs:
        piece = a.reshape(-1, LANES)
        piece = jnp.pad(piece, ((0, (-piece.shape[0]) % SUBLANES), (0, 0)))
        pieces.append(piece)
        rows += piece.shape[0]
    if rows % row_multiple:
        pieces.append(jnp.zeros(((-rows) % row_multiple, LANES), pieces[0].dtype))
    return jnp.concatenate(pieces, axis=0)


def _unpack(buf, shapes, lead=0):
    out, pos = [], 0
    for shp in shapes:
        rows = math.prod(shp) // LANES
        piece = lax.slice_in_dim(buf, pos, pos + rows, axis=lead)
        out.append(piece.reshape(buf.shape[:lead] + tuple(shp)))
        pos += rows + (-rows) % SUBLANES
    return out


REPLICATED = ["conv_b_in", "conv_b_dw", "conv_ln_g", "conv_ln_b", "conv_b_out", "gmlp_w_s", "gmlp_b_s",
              "ffn_b_up", "ffn_b_dw", "ffn_b_down", "norm1_g", "norm1_b", "norm2_g", "norm2_b"]
SMALL_SHARDED = ["conv_w_dw", "gmlp_b_in", "gmlp_ln_g", "gmlp_ln_b", "gmlp_b_out", "ffn_w_dw"]
BIG = ["conv_w_in", "conv_w_out", "gmlp_w_in", "gmlp_w_out", "ffn_w_up", "ffn_w_down"]
WEIGHTS = ["conv_w_in", "conv_b_in", "conv_w_dw", "conv_b_dw", "conv_ln_g", "conv_ln_b", "conv_w_out", "conv_b_out",
           "gmlp_w_in", "gmlp_b_in", "gmlp_ln_g", "gmlp_ln_b", "gmlp_w_s", "gmlp_b_s", "gmlp_w_out", "gmlp_b_out",
           "ffn_w_up", "ffn_b_up", "ffn_w_dw", "ffn_b_dw", "ffn_w_down", "ffn_b_down",
           "norm1_g", "norm1_b", "norm2_g", "norm2_b"]


def _from_shards(g, lead_shape):
    nd = len(lead_shape)
    perm = tuple(range(1, nd + 1)) + (0, nd + 1)
    return g.transpose(perm).reshape(tuple(lead_shape) + (-1,))


def _to_shards(full, width):
    lead = full.shape[:-1]
    nd = len(lead)
    parts = full.reshape(lead + (N_DEV, width))
    return parts.transpose((nd,) + tuple(range(nd)) + (nd + 1,))


def _step(p):
    x_in, target_in = p["x"], p["loss_target"]
    bsz, seq, d = x_in.shape
    t = bsz * seq
    assert seq % TM_EW == 0 and TM_EW % CHUNK == 0 and TM_EW // SUBLANES >= CONV_K - 1
    x0 = _to_segments(x_in.reshape(t, d), TM_EW)
    target = _to_segments(target_in.reshape(t, d), TM_EW)
    n_conv, n_gmlp = p["conv_w_in"].shape[0], p["gmlp_w_in"].shape[0]
    fb = p["ffn_w_up"].shape[-1]
    nblk = N_DEV
    half = nblk // 2
    cw = p["conv_w_in"].shape[-1]
    tm = min(TM_MM, t)
    nt = t // tm
    tk = min(TK_DW, t)
    ntk = t // tk
    dev = 4 * lax.axis_index("x") + 2 * lax.axis_index("y") + lax.axis_index("c")

    small_shapes = [p[n].shape for n in SMALL_SHARDED]
    small_src = _pack([p[n] for n in SMALL_SHARDED])[None]
    small_all = _all_gather("gather_small_weights", [small_src])[0][0]
    sm = _unpack(small_all, small_shapes, lead=1)
    w_src = []
    for i in range(DEPTH):
        mix = "conv" if i % 2 == 0 else "gmlp"
        w_src += [p[mix + "_w_in"][i // 2].astype(BF16), p[mix + "_w_out"][i // 2].astype(BF16),
                  p["ffn_w_up"][i].T.astype(BF16), p["ffn_w_down"][i].astype(BF16)]
    send_sems, recv_sems, w_land, _ = _gather_start(
        "weights_gather_start", _place_own("weights_place_own", w_src, deps=[small_all]))
    W_IN, W_OUT, W_UP, W_DOWN = range(4)

    def wait_weight(i, k, after):
        return _gather_wait(f"l{i}_weights_wait{k}", w_land[4 * i + k], send_sems, recv_sems, 4 * i + k, after)
    conv_w_dw = _from_shards(sm[0], sm[0].shape[1:-1])
    gmlp_b_in = _from_shards(sm[1], sm[1].shape[1:-1])
    gmlp_ln_g = _from_shards(sm[2], sm[2].shape[1:-1])
    gmlp_ln_b = _from_shards(sm[3], sm[3].shape[1:-1])
    gmlp_b_out = _from_shards(sm[4], sm[4].shape[1:-1])
    ffn_w_dw = sm[5].transpose(1, 0, 2, 3)

    def rows3(a):
        return a.reshape(a.shape[0], 1, a.shape[-1])

    conv_b_in4 = p["conv_b_in"].reshape(n_conv, N_DEV, 1, cw)
    gmlp_b_in4 = gmlp_b_in.reshape(n_gmlp, N_DEV, 1, cw)
    ffn_b_up4 = p["ffn_b_up"].reshape(DEPTH, nblk, 1, fb)
    ffn_b_dw4 = p["ffn_b_dw"].reshape(DEPTH, nblk, 1, fb)
    conv_b_dw3, conv_ln_g3, conv_ln_b3 = rows3(p["conv_b_dw"]), rows3(p["conv_ln_g"]), rows3(p["conv_ln_b"])
    conv_b_out3, gmlp_b_out3, ffn_b_down3 = rows3(p["conv_b_out"]), rows3(gmlp_b_out), rows3(p["ffn_b_down"])
    gmlp_ln_g3, gmlp_ln_b3 = rows3(gmlp_ln_g), rows3(gmlp_ln_b)
    n1g3, n1b3, n2g3, n2b3 = rows3(p["norm1_g"]), rows3(p["norm1_b"]), rows3(p["norm2_g"]), rows3(p["norm2_b"])
    w_tile, bs_tile = _sgu_operands(p["gmlp_w_s"], p["gmlp_b_s"])

    def mm_in(name, xa, wg, l, bias4, glu=False):
        tmi = min(TM_LN, t)
        c_half = half * cw

        def body(a_ref, b_ref, bias_ref, h_ref, *u_ref):
            xb = a_ref[...].astype(BF16)
            for n in range(N_DEV):
                h_ref[:, n * cw:(n + 1) * cw] = jnp.dot(xb, b_ref[n], preferred_element_type=F32) + bias_ref[n]
            if glu:
                u_ref[0][...] = h_ref[:, :c_half] * _sigmoid(h_ref[:, c_half:])

        outs = _call(body, name, (t // tmi,),
                     [pl.BlockSpec((tmi, d), lambda i: (i, 0)), pl.BlockSpec((N_DEV, d, cw), lambda i: (0, 0, 0)),
                      pl.BlockSpec((None, N_DEV, 1, cw), lambda i: (l, 0, 0, 0))],
                     [pl.BlockSpec((tmi, N_DEV * cw), lambda i: (i, 0))]
                     + ([pl.BlockSpec((tmi, c_half), lambda i: (i, 0))] if glu else []),
                     [_sds((t, N_DEV * cw), F32)] + ([_sds((t, c_half), F32)] if glu else []))(xa, wg, bias4)
        return outs if glu else outs[0]

    def mm_out_dx(name, dy, w, deps=()):
        return _matmul(name, dy, w, "nt", grid=(nt,),
                       a_spec=pl.BlockSpec((tm, d), lambda i: (i, 0)),
                       b_spec=pl.BlockSpec((d, d), lambda i: (0, 0)),
                       o_spec=pl.BlockSpec((tm, d), lambda i: (i, 0)), o_shape=(t, d), o_dtype=F32, deps=deps)

    def mm_out_dw(name, sa, dy):
        return _matmul(name, sa, dy, "tn", grid=(nt,), k_axis=0, nk=nt, acc_shape=(d, d),
                       a_spec=pl.BlockSpec((tm, d), lambda k: (k, 0)),
                       b_spec=pl.BlockSpec((tm, d), lambda k: (k, 0)),
                       o_spec=pl.BlockSpec((d, d), lambda k: (0, 0)), o_shape=(d, d), o_dtype=BF16)

    def mm_in_dx(name, dh, wg, res, norm=None):
        tmx = min(TM_LN, t)

        def body(a_ref, b_ref, res_ref, *refs):
            y = ALPHA * res_ref[...]
            for n in range(N_DEV):
                y = y + lax.dot_general(a_ref[:, n * cw:(n + 1) * cw], b_ref[n], (_DIMS["nt"], ((), ())),
                                        preferred_element_type=F32)
            if norm is None:
                refs[0][...] = y
            else:
                _ln_bwd_rows(y, *refs[:-1], pl.program_id(0) == 0, refs[-1])

        row = pl.BlockSpec((tmx, d), lambda i: (i, 0))
        ins = [pl.BlockSpec((tmx, N_DEV * cw), lambda i: (i, 0)), pl.BlockSpec((N_DEV, d, cw), lambda i: (0, 0, 0)), row]
        if norm is None:
            return _call(body, name, (t // tmx,), ins, row, _sds((t, d), F32))(dh, wg, res)
        ln_ins, ln_outs = _ln_bwd_specs(tmx, d, norm[3], lambda i: (i, 0))
        return _call(body, name, (t // tmx,), ins + ln_ins, ln_outs + [row],
                     [_sds((t, d), F32)] + [_sds((1, d), F32)] * 3 + [_sds((t, d), BF16)])(dh, wg, res, *norm[:3])

    def mm_in_dw(name, xa, dh):
        def body(a_ref, b_ref, o_ref, acc_ref):
            k = pl.program_id(1)
            p = lax.dot_general(a_ref[...].astype(BF16), b_ref[...], (_DIMS["tn"], ((), ())),
                                preferred_element_type=F32)
            _acc_rows(acc_ref, p, k == 0)

            @pl.when(k == nt - 1)
            def _():
                for n in range(half):
                    o_ref[n] = acc_ref[:, n * cw:(n + 1) * cw].astype(BF16)

        return _call(body, name, (2, nt),
                     [pl.BlockSpec((tm, d), lambda c, k: (k, 0)), pl.BlockSpec((tm, half * cw), lambda c, k: (k, c))],
                     pl.BlockSpec((half, d, cw), lambda c, k: (c, 0, 0)), _sds((N_DEV, d, cw), BF16),
                     [pltpu.VMEM((d, half * cw), F32)])(xa, dh)

    def mm_down_dw(name, a, dy, deps=()):
        return _matmul(name, a, dy, "tn", grid=(half, ntk), k_axis=1, nk=ntk, acc_shape=(fb, d),
                       a_spec=pl.BlockSpec((None, tk, fb), lambda n, k: (n, k, 0)),
                       b_spec=pl.BlockSpec((tk, d), lambda n, k: (k, 0)),
                       o_spec=pl.BlockSpec((None, fb, d), lambda n, k: (n, 0, 0)),
                       o_shape=(half, fb, d), o_dtype=BF16, deps=deps)

    def mm_up_dw(name, xa, dh):
        return _matmul(name, dh, xa, "tn", grid=(nblk, ntk), k_axis=1, nk=ntk, acc_shape=(fb, d),
                       a_spec=pl.BlockSpec((None, tk, fb), lambda n, k: (n, k, 0)),
                       b_spec=pl.BlockSpec((tk, d), lambda n, k: (k, 0)),
                       o_spec=pl.BlockSpec((None, fb, d), lambda n, k: (n, 0, 0)),
                       o_shape=(nblk, fb, d), o_dtype=BF16)

    saved = []
    xcur = xcur_b = x0
    for i in range(DEPTH):
        j = i // 2
        s = {"x": xcur_b}
        s["w_in"] = wait_weight(i, W_IN, xcur if i else target)
        if i % 2 == 0:
            s["h"], s["u"] = mm_in(f"l{i}_conv_in_glu", xcur_b, s["w_in"], j, conv_b_in4, glu=True)
            s["c"] = _dwconv31(f"l{i}_dwconv", s["u"], conv_w_dw, conv_b_dw3, j, seq)
            s["s"] = _ln_silu(f"l{i}_ln_silu", s["c"], conv_ln_g3, conv_ln_b3, j)
            b_out3 = conv_b_out3
        else:
            s["h"] = mm_in(f"l{i}_gmlp_in", xcur_b, s["w_in"], j, gmlp_b_in4)
            s["s"] = _sgu(f"l{i}_sgu", s["h"], gmlp_ln_g3, gmlp_ln_b3, w_tile, bs_tile, j)
            b_out3 = gmlp_b_out3
        s["w_out"] = wait_weight(i, W_OUT, s["s"]).reshape(d, d)
        s["x1"], s["x1b"], s["xhat1"], s["rstd1"] = _matmul_ln(
            f"l{i}_mixer_out_norm1", s["s"], s["w_out"], xcur, b_out3, n1g3, n1b3, j, i)
        s["w_up"] = wait_weight(i, W_UP, s["x1"])
        s["hg"], s["hv"], s["a"] = _ffn_up_act(f"l{i}_ffn_up_act", s["x1b"], s["w_up"], ffn_b_up4, ffn_w_dw,
                                               ffn_b_dw4, i, seq)
        s["w_down"] = wait_weight(i, W_DOWN, s["a"]).reshape(half, fb, d)
        xcur, xcur_b, s["xhat2"], s["rstd2"] = _matmul_ln(
            f"l{i}_ffn_down_norm2", s["a"], s["w_down"], s["x1"], ffn_b_down3, n2g3, n2b3, i, i)
        saved.append(s)

    loss_row, dx = _loss("loss", xcur, target)

    started = {n: [None] * p[n].shape[0] for n in BIG}
    tokens = []

    def send_grads(name, items):
        done, token = _scatter_start(name, [g for _, _, g in items])
        for (n, l, _), st in zip(items, done):
            started[n][l] = st
        tokens.append(token)

    def take_tokens():
        out = list(tokens)
        tokens.clear()
        return out

    gl = {n: [None] * p[n].shape[0] for n in REPLICATED + SMALL_SHARDED}
    dr2, gl["norm2_g"][DEPTH - 1], gl["norm2_b"][DEPTH - 1], gl["ffn_b_down"][DEPTH - 1], dr2b = _ln_res_bwd(
        f"l{DEPTH - 1}_norm2_bwd", dx, saved[-1]["xhat2"], saved[-1]["rstd2"], n2g3, DEPTH - 1)
    for i in reversed(range(DEPTH)):
        j = i // 2
        s = saved[i]
        mix = "conv" if i % 2 == 0 else "gmlp"
        g_down = mm_down_dw(f"l{i}_ffn_down_dw", s["a"], dr2b, deps=take_tokens()).reshape(N_DEV, -1, d)
        send_grads(f"l{i}_ffn_down_grad_scatter_start", [("ffn_w_down", i, g_down)])
        dcg, dcv, dbg, dbv, dwg, dwv = _ffn_act_bwd(f"l{i}_ffn_act_bwd", dr2b, s["w_down"], s["hg"], s["hv"],
                                                    ffn_w_dw, ffn_b_dw4, i, seq, deps=take_tokens())
        gl["ffn_b_dw"][i] = jnp.concatenate([dbg, dbv], axis=0).reshape(1, nblk * fb)
        gl["ffn_w_dw"][i] = jnp.concatenate([dwg[:, :FFN_K], dwv[:, :FFN_K]], axis=0)
        dh, dbu, dr1, gl["norm1_g"][i], gl["norm1_b"][i], gl[mix + "_b_out"][j] = _ffn_conv_t_dx(
            f"l{i}_ffn_conv_t_dx", dcg, dcv, ffn_w_dw, s["w_up"], dr2, s["xhat1"], s["rstd1"], n1g3, i, seq,
            deps=take_tokens())
        gl["ffn_b_up"][i] = dbu.reshape(1, nblk * fb)
        send_grads(f"l{i}_ffn_up_grad_scatter_start", [("ffn_w_up", i, mm_up_dw(f"l{i}_ffn_up_dw", s["x1b"], dh))])
        ds = mm_out_dx(f"l{i}_{mix}_out_dx", dr1, s["w_out"], deps=take_tokens())
        g_out = mm_out_dw(f"l{i}_{mix}_out_dw", s["s"], dr1).reshape(N_DEV, -1, d)
        if i % 2 == 0:
            dc, gl["conv_ln_g"][j], gl["conv_ln_b"][j], gl["conv_b_dw"][j] = _ln_silu_bwd(
                f"l{i}_ln_silu_bwd", ds, s["c"], conv_ln_g3, conv_ln_b3, j)
            du, dwdw = _dwconv31_bwd(f"l{i}_dwconv_bwd", dc, s["u"], conv_w_dw, j, seq)
            gl["conv_w_dw"][j] = dwdw[:CONV_K]
            dh, gl["conv_b_in"][j] = _glu_bwd(f"l{i}_glu_bwd", du, s["h"])
        else:
            dh, gl["gmlp_b_in"][j], dwt, dbt, gl["gmlp_ln_g"][j], gl["gmlp_ln_b"][j] = _sgu_bwd(
                f"l{i}_sgu_bwd", ds, s["h"], gmlp_ln_g3, gmlp_ln_b3, w_tile, bs_tile, j)
            gl["gmlp_w_s"][j], gl["gmlp_b_s"][j] = _sgu_param_grads(dwt, dbt)
            if i == 1:
                ws_local = jnp.stack(gl["gmlp_w_s"]).reshape(-1, LANES)
                ws_send, ws_recv, ws_land, ws_token = _gather_start(
                    "w_s_grads_gather_start", _place_own("w_s_grads_place_own", [ws_local]))
                tokens.append(ws_token)
        if i > 0:
            prev = saved[i - 1]
            dr2, gl["norm2_g"][i - 1], gl["norm2_b"][i - 1], gl["ffn_b_down"][i - 1], dr2b = mm_in_dx(
                f"l{i}_{mix}_in_dx_norm2_bwd", dh, s["w_in"], dr1, (prev["xhat2"], prev["rstd2"], n2g3, i - 1))
        else:
            dx = mm_in_dx(f"l{i}_{mix}_in_dx", dh, s["w_in"], dr1)
        send_grads(f"l{i}_mixer_grads_scatter_start",
                   [(mix + "_w_out", j, g_out), (mix + "_w_in", j, mm_in_dw(f"l{i}_{mix}_in_dw", s["x"], dh))])
    grad_x = _from_segments(dx, TM_EW).reshape(bsz, seq, d)

    late = [n for n in REPLICATED if n != "gmlp_w_s"]
    full_small = {n: jnp.stack(gl[n]).reshape(p[n].shape) for n in late}
    shard_small = {}
    for n in SMALL_SHARDED:
        if n == "ffn_w_dw":
            shard_small[n] = jnp.stack(gl[n]).transpose(1, 0, 2, 3)
        else:
            width = p[n].shape[-1]
            lead = p[n].shape[:-1]
            shard_small[n] = _to_shards(jnp.stack(gl[n]).reshape(lead + (N_DEV * width,)), width)
    flat_shapes = [(1, LANES)] + [p[n].shape for n in late] + [(N_DEV,) + p[n].shape for n in SMALL_SHARDED]
    flat_local = _pack([loss_row] + [full_small[n] for n in late] + [shard_small[n] for n in SMALL_SHARDED],
                       row_multiple=ROW_TILE_CAP)

    small_send, small_recv, small_land, small_token = _gather_start(
        "small_grads_gather_start", _place_own("small_grads_place_own", [flat_local]))

    grads, delta, new_m, new_v = {}, {}, {}, {}
    dev1 = jnp.reshape(dev, (1,)).astype(jnp.int32)
    order = ["ffn_w_down", "ffn_w_up", "gmlp_w_out", "gmlp_w_in", "conv_w_out", "conv_w_in"]
    after = small_token
    for n in order:
        parts_done, lands_done = _scatter_wait(f"grads_{n}_scatter_wait", started[n], after)
        state = [p[n], p["m_" + n], p["v_" + n]]
        if n == "ffn_w_up":
            state = [a.transpose(0, 2, 1) for a in state]
        outs = _sum8_adamw(f"adamw_{n}", dev1, lands_done, parts_done, *state)
        after = outs[-1]
        if n == "ffn_w_up":
            outs = [a.transpose(0, 2, 1) for a in outs]
        grads[n], delta[n], new_m[n], new_v[n] = outs

    ws_parts = _gather_wait("w_s_grads_gather_wait", ws_land[0], ws_send, ws_recv, 0, after)
    ws_sum = _sum8("sum_w_s_grads", ws_parts)
    grads["gmlp_w_s"] = ws_sum.reshape(p["gmlp_w_s"].shape)
    small_parts = _gather_wait("small_grads_gather_wait", small_land[0], small_send, small_recv, 0, ws_sum)
    summed = _unpack(_sum8("sum_small_grads", small_parts), flat_shapes)
    loss = summed[0][0, 0]
    grads.update(zip(late, summed[1:1 + len(late)]))
    for n, g in zip(SMALL_SHARDED, summed[1 + len(late):]):
        grads[n] = lax.dynamic_index_in_dim(g, dev, axis=0, keepdims=False)
    small = REPLICATED + SMALL_SHARDED
    d_s, m_s, v_s = _adamw_small("adamw_small", [grads[n] for n in small], [p[n] for n in small],
                                 [p["m_" + n] for n in small], [p["v_" + n] for n in small])
    for n, dd, mm, vv in zip(small, d_s, m_s, v_s):
        delta[n], new_m[n], new_v[n] = dd, mm, vv

    return (loss, grad_x, *[grads[n] for n in WEIGHTS], *[delta[n] for n in WEIGHTS],
            *[new_m[n] for n in WEIGHTS], *[new_v[n] for n in WEIGHTS])


def kernel(x, conv_w_in, conv_b_in, conv_w_dw, conv_b_dw, conv_ln_g, conv_ln_b, conv_w_out, conv_b_out, gmlp_w_in, gmlp_b_in, gmlp_ln_g, gmlp_ln_b, gmlp_w_s, gmlp_b_s, gmlp_w_out, gmlp_b_out, ffn_w_up, ffn_b_up, ffn_w_dw, ffn_b_dw, ffn_w_down, ffn_b_down, norm1_g, norm1_b, norm2_g, norm2_b, loss_target, m_conv_w_in, m_conv_b_in, m_conv_w_dw, m_conv_b_dw, m_conv_ln_g, m_conv_ln_b, m_conv_w_out, m_conv_b_out, m_gmlp_w_in, m_gmlp_b_in, m_gmlp_ln_g, m_gmlp_ln_b, m_gmlp_w_s, m_gmlp_b_s, m_gmlp_w_out, m_gmlp_b_out, m_ffn_w_up, m_ffn_b_up, m_ffn_w_dw, m_ffn_b_dw, m_ffn_w_down, m_ffn_b_down, m_norm1_g, m_norm1_b, m_norm2_g, m_norm2_b, v_conv_w_in, v_conv_b_in, v_conv_w_dw, v_conv_b_dw, v_conv_ln_g, v_conv_ln_b, v_conv_w_out, v_conv_b_out, v_gmlp_w_in, v_gmlp_b_in, v_gmlp_ln_g, v_gmlp_ln_b, v_gmlp_w_s, v_gmlp_b_s, v_gmlp_w_out, v_gmlp_b_out, v_ffn_w_up, v_ffn_b_up, v_ffn_w_dw, v_ffn_b_dw, v_ffn_w_down, v_ffn_b_down, v_norm1_g, v_norm1_b, v_norm2_g, v_norm2_b):
    return _step(dict(locals()))
```

```python
import math

import jax
import jax.numpy as jnp
from jax import lax
from jax.experimental import pallas as pl
from jax.experimental.pallas import tpu as pltpu

F32 = jnp.float32
BF16 = jnp.bfloat16
MESH = pl.DeviceIdType.MESH

N_DEV = 8
DEPTH = 4
ALPHA = (2.0 * DEPTH) ** 0.25
LN_EPS = 1e-5
CONV_K = 31
FFN_K = 3
CHUNK = 128
GROUPS = 8
ADAM_LR = 0.001
ADAM_B1 = 0.9
ADAM_B2 = 0.999
ADAM_EPS = 1e-08
ADAM_WD = 0.01
ADAM_STEP = 10
INV_SQRT2 = 1.0 / math.sqrt(2.0)
INV_SQRT2PI = 1.0 / math.sqrt(2.0 * math.pi)

LANES = 128
SUBLANES = 8
VMEM_LIMIT = 56 * 1024 * 1024
TM_MM = 1024
TK_DW = 2048
TM_EW = 256
TM_ROW = 512


def _call(body, name, grid, in_specs, out_specs, out_shape, scratch=(), aliases=None, deps=()):
    deps = list(deps)
    in_specs = list(in_specs)
    n_in = len(in_specs)
    if deps:
        inner = body

        def body(*refs):
            return inner(*refs[:n_in], *refs[n_in + len(deps):])

        in_specs = in_specs + [pl.BlockSpec(memory_space=pl.ANY)] * len(deps)
    fn = pl.pallas_call(
        body, name=name, grid=grid, in_specs=in_specs, out_specs=out_specs, out_shape=out_shape,
        scratch_shapes=list(scratch), input_output_aliases=aliases or {},
        compiler_params=pltpu.CompilerParams(vmem_limit_bytes=VMEM_LIMIT))
    return lambda *args: fn(*args, *deps)


def _sds(shape, dtype):
    return jax.ShapeDtypeStruct(tuple(shape), dtype)


def _sigmoid(x):
    return 1.0 / (1.0 + jnp.exp(-x))


def _acc_rows(ref, val, first):
    @pl.when(first)
    def _():
        ref[...] = val

    @pl.when(jnp.logical_not(first))
    def _():
        ref[...] += val


def _colsum(v):
    return jnp.sum(v, axis=0, keepdims=True)


_DIMS = {"nn": ((1,), (0,)), "nt": ((1,), (1,)), "tn": ((0,), (0,))}


def _matmul(name, a, b, mode, *, grid, a_spec, b_spec, o_spec, o_shape, o_dtype, k_axis=None, nk=1,
            acc_shape=None, bias=None, bias_spec=None, res=None, res_spec=None, res_scale=1.0, deps=()):
    dims = (_DIMS[mode], ((), ()))
    has_bias, has_res = bias is not None, res is not None

    def body(*refs):
        a_ref, b_ref = refs[0], refs[1]
        pos = 2
        bias_ref = res_ref = None
        if has_bias:
            bias_ref = refs[pos]
            pos += 1
        if has_res:
            res_ref = refs[pos]
            pos += 1
        o_ref = refs[pos]
        acc_ref = refs[pos + 1] if nk > 1 else None
        p = lax.dot_general(a_ref[...].astype(BF16), b_ref[...].astype(BF16), dims, preferred_element_type=F32)

        def finish(acc):
            if has_bias:
                acc = acc + bias_ref[...]
            if has_res:
                acc = acc + res_scale * res_ref[...]
            o_ref[...] = acc.astype(o_dtype)

        if nk == 1:
            finish(p)
        else:
            k = pl.program_id(k_axis)

            @pl.when(k == 0)
            def _():
                acc_ref[...] = p

            @pl.when(k > 0)
            def _():
                acc_ref[...] += p

            @pl.when(k == nk - 1)
            def _():
                finish(acc_ref[...])

    ins, specs = [a, b], [a_spec, b_spec]
    if has_bias:
        ins.append(bias)
        specs.append(bias_spec)
    if has_res:
        ins.append(res)
        specs.append(res_spec)
    scratch = [pltpu.VMEM(acc_shape, F32)] if nk > 1 else []
    return _call(body, name, grid, specs, o_spec, _sds(o_shape, o_dtype), scratch, deps=deps)(*ins)


TM_LN = 512


def _matmul_ln(name, a, b, x_res, bias3, g3, b3, l_bias, l_norm):
    t, d = x_res.shape
    tm = min(TM_LN, t)
    blocked = a.ndim == 3

    def body(a_ref, b_ref, x_ref, bias_ref, g_ref, be_ref, o_ref, ob_ref, xh_ref, rs_ref):
        if blocked:
            y = None
            for k in range(a.shape[0]):
                p = jnp.dot(a_ref[k], b_ref[k], preferred_element_type=F32)
                y = p if y is None else y + p
        else:
            y = jnp.dot(a_ref[...], b_ref[...], preferred_element_type=F32)
        xhat, rstd = _ln_stats(ALPHA * x_ref[...] + y + bias_ref[...])
        out = xhat * g_ref[...] + be_ref[...]
        o_ref[...] = out
        ob_ref[...] = out.astype(BF16)
        xh_ref[...] = xhat
        rs_ref[...] = rstd

    if blocked:
        a_spec = pl.BlockSpec((a.shape[0], tm, a.shape[2]), lambda i: (0, i, 0))
        b_spec = pl.BlockSpec(b.shape, lambda i: (0, 0, 0))
    else:
        a_spec = pl.BlockSpec((tm, a.shape[1]), lambda i: (i, 0))
        b_spec = pl.BlockSpec(b.shape, lambda i: (0, 0))
    row = pl.BlockSpec((tm, d), lambda i: (i, 0))
    stat = pl.BlockSpec((tm, 1), lambda i: (i, 0))

    def vec(l):
        return pl.BlockSpec((None, 1, d), lambda i: (l, 0, 0))

    return _call(body, name, (t // tm,), [a_spec, b_spec, row, vec(l_bias), vec(l_norm), vec(l_norm)],
                 [row, row, row, stat],
                 [_sds((t, d), F32), _sds((t, d), BF16), _sds((t, d), F32), _sds((t, 1), F32)])(
                     a, b, x_res, bias3, g3, b3)


def _mesh_pos():
    return lax.axis_index("x"), lax.axis_index("y"), lax.axis_index("c")


def _any_specs(n):
    return [pl.BlockSpec(memory_space=pl.ANY)] * n


def _all_gather(name, srcs):
    n = len(srcs)

    def body(*refs):
        src, out = refs[:n], refs[n:2 * n]
        send_sems, recv_sems, local_sems = refs[2 * n:]
        x, y, c = _mesh_pos()
        me, sibling = (x, y, c), (x, y, 1 - c)
        chips = [(1 - x, y), (x, 1 - y), (1 - x, 1 - y)]

        def slot(k, p):
            return out[k].at[:, 4 * p[0] + 2 * p[1] + p[2]]

        def copy(k, idx, block, to, s=None):
            return pltpu.make_async_remote_copy(
                src_ref=slot(k, block) if s is None else s, dst_ref=slot(k, block),
                send_sem=send_sems.at[k * 7 + idx], recv_sem=recv_sems.at[k * 7 + idx],
                device_id=to, device_id_type=MESH)

        local = [pltpu.make_async_copy(src[k], slot(k, me), local_sems.at[k]) for k in range(n)]
        for cp in local:
            cp.start()
        first = []
        for k in range(n):
            first.append(copy(k, 0, me, sibling, src[k]))
            for j, chip in enumerate(chips):
                first.append(copy(k, 1 + j, me, (*chip, c), src[k]))
        for cp in first:
            cp.start()
        passed = []
        for j, chip in enumerate(chips):
            for k in range(n):
                copy(k, 1 + j, (*chip, c), me).wait_recv()
                cp = copy(k, 4 + j, (*chip, c), sibling)
                cp.start()
                passed.append(cp)
        for k in range(n):
            copy(k, 0, sibling, me).wait_recv()
            for j, chip in enumerate(chips):
                copy(k, 4 + j, (*chip, 1 - c), me).wait_recv()
        for cp in first + passed:
            cp.wait_send()
        for cp in local:
            cp.wait()

    out_shape = [_sds((s.shape[0], N_DEV) + s.shape[1:], s.dtype) for s in srcs]
    return _call(body, name, (), [pl.BlockSpec(memory_space=pltpu.VMEM)] * n, _any_specs(n), out_shape,
                 [pltpu.SemaphoreType.DMA((7 * n,)), pltpu.SemaphoreType.DMA((7 * n,)),
                  pltpu.SemaphoreType.DMA((n,))])(*srcs)


HBM_SPEC = pl.BlockSpec(memory_space=pltpu.HBM)
SEM_SPEC = pl.BlockSpec(memory_space=pltpu.SEMAPHORE)
N_PEER = N_DEV - 1


def _split_call(body, name, in_specs, out_specs, out_shape, aliases):
    return pl.pallas_call(
        body, name=name, in_specs=in_specs, out_specs=out_specs, out_shape=out_shape, input_output_aliases=aliases,
        compiler_params=pltpu.CompilerParams(has_side_effects=pltpu.SideEffectType.DATAFLOW_SIDE_EFFECTING))


def _peers(x, y, c):
    return [(1 - x if q & 4 else x, 1 - y if q & 2 else y, 1 - c if q & 1 else c) for q in range(1, N_DEV)]


def _in_hbm(a):
    return pltpu.with_memory_space_constraint(a, pltpu.HBM)


def _place_own(name, srcs, deps=()):
    n = len(srcs)

    def body(*refs):
        src, out, sems = refs[:n], refs[n:2 * n], refs[2 * n]
        x, y, c = _mesh_pos()
        dev = 4 * x + 2 * y + c
        copies = [pltpu.make_async_copy(src[k], out[k].at[dev], sems.at[k]) for k in range(n)]
        for cp in copies:
            cp.start()
        for cp in copies:
            cp.wait()

    return _call(body, name, (), [pl.BlockSpec(memory_space=pltpu.VMEM)] * n, _any_specs(n),
                 [_sds((N_DEV,) + s.shape, s.dtype) for s in srcs], [pltpu.SemaphoreType.DMA((n,))],
                 deps=deps)(*srcs)


def _gather_start(name, lands):
    n = len(lands)

    def body(*refs):
        land, send_sems, recv_sems = refs[:n], refs[n], refs[n + 1]
        x, y, c = _mesh_pos()
        dev = 4 * x + 2 * y + c
        for k in range(n):
            for peer in _peers(x, y, c):
                pltpu.make_async_remote_copy(
                    src_ref=land[k].at[dev], dst_ref=land[k].at[dev], send_sem=send_sems.at[k],
                    recv_sem=recv_sems.at[k], device_id=peer, device_id_type=MESH).start()
        token = refs[-1]
        token[...] = jnp.zeros_like(token)

    outs = _split_call(
        body, name, [HBM_SPEC] * n, [SEM_SPEC, SEM_SPEC] + [HBM_SPEC] * n + [pl.BlockSpec(memory_space=pltpu.VMEM)],
        [pltpu.SemaphoreType.DMA((n,)), pltpu.SemaphoreType.DMA((n,))] + [pltpu.HBM(a.shape, a.dtype) for a in lands]
        + [_sds((SUBLANES, LANES), F32)],
        {k: 2 + k for k in range(n)})(*[_in_hbm(a) for a in lands])
    return outs[0], outs[1], list(outs[2:2 + n]), outs[-1]


def _wait_seven(src_ref, dst_ref, send_sem, recv_sem):
    cp = pltpu.make_async_remote_copy(
        src_ref=src_ref.at[pl.ds(0, N_PEER)], dst_ref=dst_ref.at[pl.ds(0, N_PEER)], send_sem=send_sem,
        recv_sem=recv_sem, device_id=_mesh_pos(), device_id_type=MESH)
    cp.wait_send()
    cp.wait_recv()


def _gather_wait(name, land, send_sems, recv_sems, k, after):
    def body(land_ref, send_ref, recv_ref, after_ref, out_ref):
        _wait_seven(land_ref, land_ref, send_ref.at[k], recv_ref.at[k])

    return _split_call(body, name, [HBM_SPEC, SEM_SPEC, SEM_SPEC, pl.BlockSpec(memory_space=pl.ANY)], HBM_SPEC,
                       pltpu.HBM(land.shape, land.dtype), {0: 0})(land, send_sems, recv_sems, after)


def _scatter_start(name, parts_list):
    n = len(parts_list)

    def body(*refs):
        x, y, c = _mesh_pos()
        dev = 4 * x + 2 * y + c
        for k in range(n):
            parts_ref, land_ref = refs[2 * k], refs[2 * k + 1]
            send_sem, recv_sem = refs[2 * n + 4 * k], refs[2 * n + 4 * k + 1]
            for peer in _peers(x, y, c):
                pltpu.make_async_remote_copy(
                    src_ref=parts_ref.at[4 * peer[0] + 2 * peer[1] + peer[2]], dst_ref=land_ref.at[dev],
                    send_sem=send_sem, recv_sem=recv_sem, device_id=peer, device_id_type=MESH).start()
        token = refs[-1]
        token[...] = jnp.zeros_like(token)

    ins, out_specs, out_shape, aliases = [], [], [], {}
    for k, parts in enumerate(parts_list):
        buf = pltpu.HBM(parts.shape, parts.dtype)
        ins += [_in_hbm(parts), _in_hbm(lax.empty(parts.shape, parts.dtype))]
        out_specs += [SEM_SPEC, SEM_SPEC, HBM_SPEC, HBM_SPEC]
        out_shape += [pltpu.SemaphoreType.DMA(()), pltpu.SemaphoreType.DMA(()), buf, buf]
        aliases.update({2 * k: 4 * k + 2, 2 * k + 1: 4 * k + 3})
    outs = _split_call(body, name, [HBM_SPEC] * (2 * n), out_specs + [pl.BlockSpec(memory_space=pltpu.VMEM)],
                       out_shape + [_sds((SUBLANES, LANES), F32)], aliases)(*ins)
    return [tuple(outs[4 * k:4 * k + 4]) for k in range(n)], outs[-1]


def _scatter_wait(name, started, after):
    n = len(started)

    def body(*refs):
        for k in range(n):
            send_sem, recv_sem, parts_ref, land_ref = refs[4 * k:4 * k + 4]
            _wait_seven(parts_ref, land_ref, send_sem, recv_sem)

    flat = [a for s in started for a in s]
    outs = _split_call(
        body, name, [SEM_SPEC, SEM_SPEC, HBM_SPEC, HBM_SPEC] * n + [pl.BlockSpec(memory_space=pl.ANY)],
        [HBM_SPEC, HBM_SPEC] * n, [pltpu.HBM(a.shape, a.dtype) for s in started for a in s[2:]],
        {4 * k + 2 + t: 2 * k + t for k in range(n) for t in range(2)})(*flat, after)
    return list(outs[0::2]), list(outs[1::2])


def _to_segments(a, tile):
    seg = tile // SUBLANES
    return a.reshape((a.shape[0] // tile, SUBLANES, seg) + a.shape[1:]).swapaxes(1, 2).reshape(a.shape)


def _from_segments(a, tile):
    seg = tile // SUBLANES
    return a.reshape((a.shape[0] // tile, seg, SUBLANES) + a.shape[1:]).swapaxes(1, 2).reshape(a.shape)


def _chunk(ref, q):
    return ref[q * SUBLANES:(q + 1) * SUBLANES, :]


def _fill_wrap_prev(x_ref, halo_ref, wrap_ref, n_wrap, n_halo, seg, keep):
    sub = lax.broadcasted_iota(jnp.int32, (SUBLANES, x_ref.shape[-1]), 0)
    for j in range(n_wrap):
        q = seg - n_wrap + j
        hq = q - (seg - n_halo)
        row = halo_ref[hq * SUBLANES + SUBLANES - 1:(hq + 1) * SUBLANES, :] * keep
        wrap_ref[j * SUBLANES:(j + 1) * SUBLANES, :] = jnp.where(sub == 0, row, pltpu.roll(_chunk(x_ref, q), 1, 0))


def _fill_wrap_next(x_ref, halo_ref, wrap_ref, n_wrap, keep):
    sub = lax.broadcasted_iota(jnp.int32, (SUBLANES, x_ref.shape[-1]), 0)
    for j in range(n_wrap):
        row = halo_ref[j * SUBLANES:j * SUBLANES + 1, :] * keep
        wrap_ref[j * SUBLANES:(j + 1) * SUBLANES, :] = jnp.where(
            sub == SUBLANES - 1, row, pltpu.roll(_chunk(x_ref, j), SUBLANES - 1, 0))


def _past(x_ref, wrap_ref, q, d, n_wrap):
    return _chunk(x_ref, q - d) if q >= d else _chunk(wrap_ref, q - d + n_wrap)


def _future(x_ref, wrap_ref, q, d, seg):
    return _chunk(x_ref, q + d) if q + d < seg else _chunk(wrap_ref, q + d - seg)


def _conv_fwd(x_ref, wrap_ref, w_ref, b_ref, out_ref, seg, k_taps):
    bias = jnp.broadcast_to(b_ref[...], (SUBLANES, x_ref.shape[-1]))
    for q in range(seg):
        acc = bias
        for k in range(k_taps):
            acc = acc + w_ref[k:k + 1, :] * _past(x_ref, wrap_ref, q, k_taps - 1 - k, k_taps - 1)
        out_ref[q * SUBLANES:(q + 1) * SUBLANES, :] = acc


def _conv_bwd_data(d_ref, wrap_ref, w_ref, out_ref, seg, k_taps):
    for q in range(seg):
        acc = None
        for k in range(k_taps):
            term = w_ref[k:k + 1, :] * _future(d_ref, wrap_ref, q, k_taps - 1 - k, seg)
            acc = term if acc is None else acc + term
        out_ref[q * SUBLANES:(q + 1) * SUBLANES, :] = acc


def _conv_bwd_taps(d_ref, x_ref, wrap_ref, dw_ref, seg, k_taps):
    for k in range(k_taps):
        part = None
        for q in range(seg):
            term = _chunk(d_ref, q) * _past(x_ref, wrap_ref, q, k_taps - 1 - k, k_taps - 1)
            part = term if part is None else part + term
        dw_ref[k:k + 1, :] += _colsum(part)


def _tile_halo_specs(tm, width_block, n_halo, n_tiles, block_of):
    rows = n_halo * SUBLANES
    per = tm // rows
    tile = pl.BlockSpec(width_block(tm), lambda n, i: block_of(n, i))
    prev = pl.BlockSpec(width_block(rows), lambda n, i: block_of(n, jnp.maximum(i * per - 1, 0)))
    nxt = pl.BlockSpec(width_block(rows), lambda n, i: block_of(n, jnp.minimum((i + 1) * per, n_tiles * per - 1)))
    return tile, prev, nxt


def _ln_stats(v):
    mu = jnp.mean(v, axis=-1, keepdims=True)
    vc = v - mu
    var = jnp.mean(vc * vc, axis=-1, keepdims=True)
    rstd = lax.rsqrt(var + LN_EPS)
    return vc * rstd, rstd


def _ln_backward(dxhat, xhat, rstd):
    m1 = jnp.mean(dxhat, axis=-1, keepdims=True)
    m2 = jnp.mean(dxhat * xhat, axis=-1, keepdims=True)
    return rstd * (dxhat - m1 - xhat * m2)


def _row_spec(tm, width):
    return pl.BlockSpec((tm, width), lambda i: (i, 0))


def _param_spec(l, width):
    return pl.BlockSpec((None, 1, width), lambda *_: (l, 0, 0))


def _ln_bwd_rows(dout, xh_ref, rs_ref, g_ref, dr_ref, dg_ref, db_ref, dsum_ref, first, drb_ref=None):
    xhat = xh_ref[...]
    dr = _ln_backward(dout * g_ref[...], xhat, rs_ref[...])
    dr_ref[...] = dr
    if drb_ref is not None:
        drb_ref[...] = dr.astype(BF16)
    _acc_rows(dg_ref, _colsum(dout * xhat), first)
    _acc_rows(db_ref, _colsum(dout), first)
    _acc_rows(dsum_ref, _colsum(dr), first)


def _ln_bwd_specs(tm, d, l, row_of):
    vec = pl.BlockSpec((1, d), lambda *_: (0, 0))
    ins = [pl.BlockSpec((tm, d), row_of), pl.BlockSpec((tm, 1), row_of), _param_spec(l, d)]
    return ins, [pl.BlockSpec((tm, d), row_of), vec, vec, vec]


def _ln_res_bwd(name, dout, xhat, rstd, g3, l, deps=()):
    t, d = dout.shape
    tm = min(TM_ROW, t)

    def body(do_ref, xh_ref, rs_ref, g_ref, dr_ref, dg_ref, db_ref, dc_ref, drb_ref):
        _ln_bwd_rows(do_ref[...], xh_ref, rs_ref, g_ref, dr_ref, dg_ref, db_ref, dc_ref, pl.program_id(0) == 0,
                     drb_ref)

    ins, outs = _ln_bwd_specs(tm, d, l, lambda i: (i, 0))
    return _call(body, name, (t // tm,), [_row_spec(tm, d)] + ins, outs + [_row_spec(tm, d)],
                 [_sds((t, d), F32)] + [_sds((1, d), F32)] * 3 + [_sds((t, d), BF16)],
                 deps=deps)(dout, xhat, rstd, g3)


def _glu_bwd(name, du, h):
    t, c2 = h.shape
    c = c2 // 2
    tm = min(TM_ROW, t)

    def body(du_ref, a_ref, g_ref, dh_ref, db_ref):
        first = pl.program_id(0) == 0
        du_v, a = du_ref[...], a_ref[...]
        sg = _sigmoid(g_ref[...])
        da = du_v * sg
        dg = du_v * a * sg * (1.0 - sg)
        dh_ref[:, :c] = da.astype(BF16)
        dh_ref[:, c:] = dg.astype(BF16)
        _acc_rows(db_ref.at[:, :c], _colsum(da), first)
        _acc_rows(db_ref.at[:, c:], _colsum(dg), first)

    return _call(body, name, (t // tm,),
                 [_row_spec(tm, c), pl.BlockSpec((tm, c), lambda i: (i, 0)), pl.BlockSpec((tm, c), lambda i: (i, 1))],
                 [_row_spec(tm, c2), pl.BlockSpec((1, c2), lambda i: (0, 0))],
                 [_sds((t, c2), BF16), _sds((1, c2), F32)])(du, h, h)


CONV_CB = 512
TAPS_PAD = 32


def _dwconv31(name, u, w3, b3, l, seq):
    t, c = u.shape
    tm, cb = TM_EW, CONV_CB
    seg, seq_tiles, n_tiles = tm // SUBLANES, seq // tm, t // tm
    n_wrap = CONV_K - 1
    tile, prev, _ = _tile_halo_specs(tm, lambda rows: (rows, cb), seg, n_tiles, lambda n, r: (r, n))

    def body(u_ref, halo_ref, w_ref, b_ref, o_ref, wrap_ref):
        keep = (pl.program_id(1) % seq_tiles != 0).astype(F32)
        _fill_wrap_prev(u_ref, halo_ref, wrap_ref, n_wrap, seg, seg, keep)
        _conv_fwd(u_ref, wrap_ref, w_ref, b_ref, o_ref, seg, CONV_K)

    return _call(body, name, (c // cb, n_tiles),
                 [tile, prev, pl.BlockSpec((None, CONV_K, cb), lambda n, i: (l, 0, n)),
                  pl.BlockSpec((None, 1, cb), lambda n, i: (l, 0, n))],
                 tile, _sds((t, c), F32), [pltpu.VMEM((n_wrap * SUBLANES, cb), F32)])(u, u, w3, b3)


def _dwconv31_bwd(name, dc, u, w3, l, seq):
    t, c = dc.shape
    tm, cb = TM_EW, CONV_CB
    seg, seq_tiles, n_tiles = tm // SUBLANES, seq // tm, t // tm
    n_wrap = CONV_K - 1
    tile, prev, nxt = _tile_halo_specs(tm, lambda rows: (rows, cb), seg, n_tiles, lambda n, r: (r, n))

    def body(dc_ref, dcn_ref, u_ref, up_ref, w_ref, du_ref, dw_ref, dwrap_ref, uwrap_ref):
        i = pl.program_id(1)
        keep_prev = (i % seq_tiles != 0).astype(F32)
        keep_next = (i % seq_tiles != seq_tiles - 1).astype(F32)
        _fill_wrap_next(dc_ref, dcn_ref, dwrap_ref, n_wrap, keep_next)
        _conv_bwd_data(dc_ref, dwrap_ref, w_ref, du_ref, seg, CONV_K)

        @pl.when(i == 0)
        def _():
            dw_ref[...] = jnp.zeros_like(dw_ref)

        _fill_wrap_prev(u_ref, up_ref, uwrap_ref, n_wrap, seg, seg, keep_prev)
        _conv_bwd_taps(dc_ref, u_ref, uwrap_ref, dw_ref, seg, CONV_K)

    wrap = pltpu.VMEM((n_wrap * SUBLANES, cb), F32)
    return _call(body, name, (c // cb, n_tiles),
                 [tile, nxt, tile, prev, pl.BlockSpec((None, CONV_K, cb), lambda n, i: (l, 0, n))],
                 [tile, pl.BlockSpec((TAPS_PAD, cb), lambda n, i: (0, n))],
                 [_sds((t, c), F32), _sds((TAPS_PAD, c), F32)], [wrap, wrap])(dc, dc, u, u, w3)


def _ln_silu(name, cx, g3, b3, l):
    t, d = cx.shape
    tm = min(TM_ROW, t)

    def body(c_ref, g_ref, b_ref, o_ref):
        xhat, _ = _ln_stats(c_ref[...])
        nv = xhat * g_ref[...] + b_ref[...]
        o_ref[...] = (nv * _sigmoid(nv)).astype(BF16)

    return _call(body, name, (t // tm,), [_row_spec(tm, d), _param_spec(l, d), _param_spec(l, d)],
                 _row_spec(tm, d), _sds((t, d), BF16))(cx, g3, b3)


def _ln_silu_bwd(name, ds, cx, g3, b3, l, deps=()):
    t, d = cx.shape
    tm = min(TM_ROW, t)

    def body(ds_ref, c_ref, g_ref, b_ref, dc_ref, dg_ref, db_ref, dsum_ref):
        first = pl.program_id(0) == 0
        xhat, rstd = _ln_stats(c_ref[...])
        g = g_ref[...]
        nv = xhat * g + b_ref[...]
        sg = _sigmoid(nv)
        dn = ds_ref[...] * (sg * (1.0 + nv * (1.0 - sg)))
        dc = _ln_backward(dn * g, xhat, rstd)
        dc_ref[...] = dc
        _acc_rows(dg_ref, _colsum(dn * xhat), first)
        _acc_rows(db_ref, _colsum(dn), first)
        _acc_rows(dsum_ref, _colsum(dc), first)

    vec = pl.BlockSpec((1, d), lambda i: (0, 0))
    return _call(body, name, (t // tm,),
                 [_row_spec(tm, d), _row_spec(tm, d), _param_spec(l, d), _param_spec(l, d)],
                 [_row_spec(tm, d), vec, vec, vec],
                 [_sds((t, d), F32)] + [_sds((1, d), F32)] * 3, deps=deps)(ds, cx, g3, b3)


FFN_HALO = FFN_K - 1


def _ffn_conv(x_ref, halo_ref, wrap_ref, w_ref, b_ref, keep, seg, out_ref):
    _fill_wrap_prev(x_ref, halo_ref, wrap_ref, FFN_K - 1, FFN_HALO, seg, keep)
    _conv_fwd(x_ref, wrap_ref, w_ref, b_ref, out_ref, seg, FFN_K)


TM_FFN = 512
WRAP_ROWS = FFN_HALO * SUBLANES


def _sub_tiles(x_ref, prev_ref, next_ref, keep_prev, keep_next, n_sub):
    out = []
    for s in range(n_sub):
        tile = x_ref.at[pl.ds(s * TM_EW, TM_EW)]
        prev = prev_ref if s == 0 else x_ref.at[pl.ds(s * TM_EW - WRAP_ROWS, WRAP_ROWS)]
        nxt = next_ref if s == n_sub - 1 else x_ref.at[pl.ds((s + 1) * TM_EW, WRAP_ROWS)]
        out.append((tile, prev, keep_prev if s == 0 else 1.0, nxt, keep_next if s == n_sub - 1 else 1.0))
    return out


def _rows(ref, s, rows):
    return ref.at[pl.ds(s * rows, rows)]


def _ffn_specs(tm, fb, n_tiles):
    return _tile_halo_specs(tm, lambda rows: (None, rows, fb), FFN_HALO, n_tiles, lambda n, r: (n, r, 0))


def _ffn_up_act(name, x, w_up, b_up4, wdw, bdw, l, seq):
    t, d = x.shape
    nb, fb, _ = w_up.shape
    half = nb // 2
    tm = min(TM_FFN, seq)
    n_sub, seg, seq_steps, n_steps = tm // TM_EW, TM_EW // SUBLANES, seq // tm, t // tm
    per = tm // WRAP_ROWS
    nt_dims = (_DIMS["nt"], ((), ()))

    def body(x_ref, xp_ref, ug_ref, uv_ref, bug_ref, buv_ref, wg_ref, wv_ref, bg_ref, bv_ref,
             hg_ref, hv_ref, a_ref, pg_ref, pv_ref, gwrap_ref, vwrap_ref, cg_ref, cv_ref):
        keep = (pl.program_id(1) % seq_steps != 0).astype(F32)
        xb, xpb = x_ref[...].astype(BF16), xp_ref[...].astype(BF16)
        hg_ref[...] = lax.dot_general(xb, ug_ref[...], nt_dims, preferred_element_type=F32) + bug_ref[...]
        pg_ref[...] = lax.dot_general(xpb, ug_ref[...], nt_dims, preferred_element_type=F32) + bug_ref[...]
        for s, (tile, prev, kp, _, _) in enumerate(_sub_tiles(hg_ref, pg_ref, None, keep, None, n_sub)):
            _ffn_conv(tile, prev, gwrap_ref, wg_ref, bg_ref, kp, seg, _rows(cg_ref, s, TM_EW))
        hv_ref[...] = lax.dot_general(xb, uv_ref[...], nt_dims, preferred_element_type=F32) + buv_ref[...]
        pv_ref[...] = lax.dot_general(xpb, uv_ref[...], nt_dims, preferred_element_type=F32) + buv_ref[...]
        for s, (tile, prev, kp, _, _) in enumerate(_sub_tiles(hv_ref, pv_ref, None, keep, None, n_sub)):
            _ffn_conv(tile, prev, vwrap_ref, wv_ref, bv_ref, kp, seg, _rows(cv_ref, s, TM_EW))
        cg = cg_ref[...]
        a_ref[...] = (cg * _sigmoid(cg) * cv_ref[...]).astype(BF16)

    def blk(shift):
        return pl.BlockSpec((None, fb, d), lambda n, i: (n + shift, 0, 0))

    def vec(shift, rows):
        return pl.BlockSpec((None, None, rows, fb), lambda n, i: (l, n + shift, 0, 0))

    out = pl.BlockSpec((None, tm, fb), lambda n, i: (n, i, 0))
    tmp = pltpu.VMEM((tm, fb), F32)
    halo = pltpu.VMEM((WRAP_ROWS, fb), F32)
    return _call(body, name, (half, n_steps),
                 [pl.BlockSpec((tm, d), lambda n, i: (i, 0)),
                  pl.BlockSpec((WRAP_ROWS, d), lambda n, i: (jnp.maximum(i * per - 1, 0), 0)),
                  blk(0), blk(half), vec(0, 1), vec(half, 1), vec(0, FFN_K), vec(half, FFN_K), vec(0, 1), vec(half, 1)],
                 [out, out, out],
                 [_sds((half, t, fb), F32), _sds((half, t, fb), F32), _sds((half, t, fb), BF16)],
                 [halo, halo, halo, halo, tmp, tmp])(x, x, w_up, w_up, b_up4, b_up4, wdw, wdw, bdw, bdw)


def _ffn_act_bwd(name, dy, w_down, hg, hv, wdw, bdw, l, seq, deps=()):
    half, t, fb = hg.shape
    d = dy.shape[-1]
    tm = min(TM_FFN, seq)
    n_sub, seg, seq_steps, n_steps = tm // TM_EW, TM_EW // SUBLANES, seq // tm, t // tm
    tile, prev, _ = _ffn_specs(tm, fb, n_steps)

    def body(dy_ref, wd_ref, g_ref, gp_ref, v_ref, vp_ref, wg_ref, wv_ref, bg_ref, bv_ref,
             dg_ref, dv_ref, dbg_ref, dbv_ref, dwg_ref, dwv_ref, gwrap_ref, vwrap_ref, cg_ref, cv_ref):
        i = pl.program_id(1)
        first = i == 0
        keep = (i % seq_steps != 0).astype(F32)
        da = lax.dot_general(dy_ref[...].astype(BF16), wd_ref[...], (_DIMS["nt"], ((), ())),
                             preferred_element_type=F32)
        g_tiles = _sub_tiles(g_ref, gp_ref, None, keep, None, n_sub)
        v_tiles = _sub_tiles(v_ref, vp_ref, None, keep, None, n_sub)
        for s in range(n_sub):
            _ffn_conv(g_tiles[s][0], g_tiles[s][1], _rows(gwrap_ref, s, WRAP_ROWS), wg_ref, bg_ref, g_tiles[s][2],
                      seg, _rows(cg_ref, s, TM_EW))
            _ffn_conv(v_tiles[s][0], v_tiles[s][1], _rows(vwrap_ref, s, WRAP_ROWS), wv_ref, bv_ref, v_tiles[s][2],
                      seg, _rows(cv_ref, s, TM_EW))
        cg, cv = cg_ref[...], cv_ref[...]
        sg = _sigmoid(cg)
        dcv = da * cg * sg
        dcg = da * cv * sg * (1.0 + cg * (1.0 - sg))
        dg_ref[...] = dcg
        dv_ref[...] = dcv
        _acc_rows(dbg_ref, _colsum(dcg), first)
        _acc_rows(dbv_ref, _colsum(dcv), first)

        @pl.when(first)
        def _():
            dwg_ref[...] = jnp.zeros_like(dwg_ref)
            dwv_ref[...] = jnp.zeros_like(dwv_ref)

        for s in range(n_sub):
            _conv_bwd_taps(_rows(dg_ref, s, TM_EW), g_tiles[s][0], _rows(gwrap_ref, s, WRAP_ROWS), dwg_ref, seg, FFN_K)
            _conv_bwd_taps(_rows(dv_ref, s, TM_EW), v_tiles[s][0], _rows(vwrap_ref, s, WRAP_ROWS), dwv_ref, seg, FFN_K)

    def vec(shift, rows):
        return pl.BlockSpec((None, None, rows, fb), lambda n, i: (l, n + shift, 0, 0))

    def acc(rows):
        return pl.BlockSpec((None, rows, fb), lambda n, i: (n, 0, 0))

    wrap = pltpu.VMEM((n_sub * WRAP_ROWS, fb), F32)
    tmp = pltpu.VMEM((tm, fb), F32)
    return _call(body, name, (half, n_steps),
                 [pl.BlockSpec((tm, d), lambda n, i: (i, 0)), pl.BlockSpec((None, fb, d), lambda n, i: (n, 0, 0)),
                  tile, prev, tile, prev, vec(0, FFN_K), vec(half, FFN_K), vec(0, 1), vec(half, 1)],
                 [tile, tile, acc(1), acc(1), acc(SUBLANES), acc(SUBLANES)],
                 [_sds((half, t, fb), F32), _sds((half, t, fb), F32), _sds((half, 1, fb), F32),
                  _sds((half, 1, fb), F32), _sds((half, SUBLANES, fb), F32), _sds((half, SUBLANES, fb), F32)],
                 [wrap, wrap, tmp, tmp], deps=deps)(dy, w_down, hg, hg, hv, hv, wdw, wdw, bdw, bdw)


def _ffn_conv_t_dx(name, dcg, dcv, wdw, w_up, res, xhat, rstd, g3, l, seq, deps=()):
    half, t, fb = dcg.shape
    nb, d = 2 * half, res.shape[-1]
    tm = min(TM_FFN, seq)
    n_sub, seg, seq_steps, n_steps = tm // TM_EW, TM_EW // SUBLANES, seq // tm, t // tm
    per = tm // WRAP_ROWS

    pair = 2
    n_pairs, half_pairs = nb // pair, half // pair

    def body(g_ref, gn_ref, v_ref, vn_ref, w_ref, up_ref, res_ref, xh_ref, rs_ref, gam_ref,
             dh_ref, db_ref, dr_ref, dgam_ref, dbeta_ref, dsum_ref, drb_ref, wrap_ref, out_ref, acc_ref):
        i, m = pl.program_id(0), pl.program_id(1)
        keep = (i % seq_steps != seq_steps - 1).astype(F32)

        def conv_t(d_ref, dn_ref, b):
            for s, (sub, _, _, nx, kn) in enumerate(_sub_tiles(d_ref.at[b], None, dn_ref.at[b], None, keep, n_sub)):
                _fill_wrap_next(sub, nx, wrap_ref, FFN_K - 1, kn)
                _conv_bwd_data(sub, wrap_ref, w_ref.at[b], _rows(out_ref, s, TM_EW), seg, FFN_K)

        p = None
        for b in range(pair):
            @pl.when(m < half_pairs)
            def _(b=b):
                conv_t(g_ref, gn_ref, b)

            @pl.when(m >= half_pairs)
            def _(b=b):
                conv_t(v_ref, vn_ref, b)

            dh = out_ref[...]
            dhb = dh.astype(BF16)
            dh_ref[b] = dhb
            _acc_rows(db_ref.at[pair * m + b], _colsum(dh), i == 0)
            part = jnp.dot(dhb, up_ref[b], preferred_element_type=F32)
            p = part if p is None else p + part

        @pl.when(m == 0)
        def _():
            acc_ref[...] = p

        @pl.when(m > 0)
        def _():
            acc_ref[...] += p

        @pl.when(m == n_pairs - 1)
        def _():
            _ln_bwd_rows(acc_ref[...] + ALPHA * res_ref[...], xh_ref, rs_ref, gam_ref, dr_ref, dgam_ref, dbeta_ref,
                         dsum_ref, i == 0, drb_ref)

    def src(gate):
        def blk(m):
            return jnp.minimum(m, half_pairs - 1) if gate else jnp.maximum(m - half_pairs, 0)
        tile = pl.BlockSpec((pair, tm, fb), lambda i, m: (blk(m), i, 0))
        nxt = pl.BlockSpec((pair, WRAP_ROWS, fb),
                           lambda i, m: (blk(m), jnp.minimum((i + 1) * per, n_steps * per - 1), 0))
        return [tile, nxt]

    row = pl.BlockSpec((tm, d), lambda i, m: (i, 0))
    ln_ins, ln_outs = _ln_bwd_specs(tm, d, l, lambda i, m: (i, 0))
    tmp = pltpu.VMEM((tm, fb), F32)
    halo = pltpu.VMEM((WRAP_ROWS, fb), F32)
    return _call(body, name, (n_steps, n_pairs),
                 src(True) + src(False) +
                 [pl.BlockSpec((None, pair, FFN_K, fb), lambda i, m: (l, m, 0, 0)),
                  pl.BlockSpec((pair, fb, d), lambda i, m: (m, 0, 0)), row] + ln_ins,
                 [pl.BlockSpec((pair, tm, fb), lambda i, m: (m, i, 0)),
                  pl.BlockSpec((nb, 1, fb), lambda i, m: (0, 0, 0))] + ln_outs + [row],
                 [_sds((nb, t, fb), BF16), _sds((nb, 1, fb), F32), _sds((t, d), F32)] + [_sds((1, d), F32)] * 3
                 + [_sds((t, d), BF16)],
                 [halo, tmp, pltpu.VMEM((tm, d), F32)],
                 deps=deps)(dcg, dcg, dcv, dcv, wdw, w_up, res, xhat, rstd, g3)


def _gelu_parts(h):
    cdf = 0.5 * (1.0 + lax.erf(h * INV_SQRT2))
    return h * cdf, cdf


def _seg_axis(a, axis, fn):
    return jnp.moveaxis(fn(jnp.moveaxis(a, axis, 0), TM_EW), 0, axis)


def _sgu_operands(w_s, b_s):
    nl = w_s.shape[0]
    n_sub = TM_EW // CHUNK
    tril = jnp.tril(jnp.ones((CHUNK, CHUNK), dtype=bool))
    w_causal = jnp.where(tril, w_s, 0.0)
    w_tile = (jnp.eye(n_sub, dtype=F32)[None, None, :, None, :, None] * w_causal[:, :, None, :, None, :]).reshape(
        nl, GROUPS, TM_EW, TM_EW)
    w_tile = _seg_axis(_seg_axis(w_tile, 2, _to_segments), 3, _to_segments).astype(BF16)
    bs_tile = jnp.broadcast_to(b_s[:, :, None, :, None], (nl, GROUPS, n_sub, CHUNK, CHUNK)).reshape(
        nl, GROUPS, TM_EW, CHUNK)
    return w_tile, _seg_axis(bs_tile, 2, _to_segments)


def _sgu_param_grads(dwt, dbt):
    n_sub = TM_EW // CHUNK
    tril = jnp.tril(jnp.ones((CHUNK, CHUNK), dtype=bool))
    dwt = _seg_axis(_seg_axis(dwt, 1, _from_segments), 2, _from_segments).reshape(GROUPS, n_sub, CHUNK, n_sub, CHUNK)
    dw = sum(dwt[:, a, :, a, :] for a in range(n_sub))
    db = _seg_axis(dbt, 1, _from_segments).reshape(GROUPS, n_sub, CHUNK).sum(axis=1)
    return jnp.where(tril, dw, 0.0), db


def _sgu(name, h, g3, b3, wt, bst, l):
    t, c2 = h.shape
    c = c2 // 2
    tm = TM_EW

    def body(h_ref, g_ref, b_ref, wt_ref, bs_ref, o_ref):
        z, _ = _gelu_parts(h_ref[...])
        u = z[:, :c]
        xhat, _ = _ln_stats(z[:, c:])
        vnb = (xhat * g_ref[...] + b_ref[...]).astype(BF16)
        for gi in range(GROUPS):
            cs = slice(gi * CHUNK, (gi + 1) * CHUNK)
            sp = jnp.dot(wt_ref[gi], vnb[:, cs], preferred_element_type=F32) + bs_ref[gi]
            o_ref[:, cs] = (u[:, cs] * sp).astype(BF16)

    return _call(body, name, (t // tm,),
                 [_row_spec(tm, c2), _param_spec(l, c), _param_spec(l, c),
                  pl.BlockSpec((None, GROUPS, tm, tm), lambda i: (l, 0, 0, 0)),
                  pl.BlockSpec((None, GROUPS, tm, CHUNK), lambda i: (l, 0, 0, 0))],
                 _row_spec(tm, c), _sds((t, c), BF16))(h, g3, b3, wt, bst)


def _sgu_bwd(name, dq, h, g3, b3, wt, bst, l, deps=()):
    t, c2 = h.shape
    c = c2 // 2
    tm = TM_EW
    n_tiles = t // tm

    def body(dq_ref, h_ref, g_ref, b_ref, wt_ref, bs_ref,
             dh_ref, dbin_ref, dw_ref, dbs_ref, dg_ref, db_ref, du_ref, dvn_ref, bsum_ref):
        i = pl.program_id(0)
        first = i == 0
        hv = h_ref[...]
        z, cdf = _gelu_parts(hv)
        u = z[:, :c]
        xhat, rstd = _ln_stats(z[:, c:])
        g = g_ref[...]
        vnb = (xhat * g + b_ref[...]).astype(BF16)

        @pl.when(first)
        def _():
            dw_ref[...] = jnp.zeros_like(dw_ref)
            bsum_ref[...] = jnp.zeros_like(bsum_ref)

        for gi in range(GROUPS):
            cs = slice(gi * CHUNK, (gi + 1) * CHUNK)
            vb = vnb[:, cs]
            w = wt_ref[gi]
            sp = jnp.dot(w, vb, preferred_element_type=F32) + bs_ref[gi]
            dqb = dq_ref[:, cs]
            du_ref[:, cs] = dqb * sp
            dsp = dqb * u[:, cs]
            bsum_ref[gi] += dsp
            dspb = dsp.astype(BF16)
            dw_ref[gi] += lax.dot_general(dspb, vb, (_DIMS["nt"], ((), ())), preferred_element_type=F32)
            dvn_ref[:, cs] = lax.dot_general(w, dspb, (_DIMS["tn"], ((), ())), preferred_element_type=F32)

        dvn = dvn_ref[...]
        dv = _ln_backward(dvn * g, xhat, rstd)
        pdf = jnp.exp(-0.5 * hv * hv) * INV_SQRT2PI
        dgelu = cdf + hv * pdf
        dhu = du_ref[...] * dgelu[:, :c]
        dhv = dv * dgelu[:, c:]
        dh_ref[:, :c] = dhu.astype(BF16)
        dh_ref[:, c:] = dhv.astype(BF16)
        _acc_rows(dbin_ref.at[:, :c], _colsum(dhu), first)
        _acc_rows(dbin_ref.at[:, c:], _colsum(dhv), first)
        _acc_rows(dg_ref, _colsum(dvn * xhat), first)
        _acc_rows(db_ref, _colsum(dvn), first)

        @pl.when(i == n_tiles - 1)
        def _():
            dbs_ref[...] = jnp.sum(bsum_ref[...], axis=-1)

    vec = pl.BlockSpec((1, c), lambda i: (0, 0))
    return _call(body, name, (n_tiles,),
                 [_row_spec(tm, c), _row_spec(tm, c2), _param_spec(l, c), _param_spec(l, c),
                  pl.BlockSpec((None, GROUPS, tm, tm), lambda i: (l, 0, 0, 0)),
                  pl.BlockSpec((None, GROUPS, tm, CHUNK), lambda i: (l, 0, 0, 0))],
                 [_row_spec(tm, c2), pl.BlockSpec((1, c2), lambda i: (0, 0)),
                  pl.BlockSpec((GROUPS, tm, tm), lambda i: (0, 0, 0)),
                  pl.BlockSpec((GROUPS, tm), lambda i: (0, 0)), vec, vec],
                 [_sds((t, c2), BF16), _sds((1, c2), F32), _sds((GROUPS, tm, tm), F32),
                  _sds((GROUPS, tm), F32), _sds((1, c), F32), _sds((1, c), F32)],
                 [pltpu.VMEM((tm, c), F32), pltpu.VMEM((tm, c), F32), pltpu.VMEM((GROUPS, tm, CHUNK), F32)],
                 deps=deps)(dq, h, g3, b3, wt, bst)


def _loss(name, y, target):
    t, d = y.shape
    tm = min(TM_ROW, t)
    n_tiles = t // tm

    def body(y_ref, t_ref, l_ref, dy_ref, acc_ref):
        i = pl.program_id(0)
        diff = y_ref[...] - t_ref[...]
        dy_ref[...] = diff * (1.0 / d)
        _acc_rows(acc_ref, _colsum(diff * diff), i == 0)

        @pl.when(i == n_tiles - 1)
        def _():
            l_ref[...] = jnp.broadcast_to(jnp.sum(acc_ref[...], axis=-1, keepdims=True) * (0.5 / d), (1, LANES))

    return _call(body, name, (n_tiles,), [_row_spec(tm, d), _row_spec(tm, d)],
                 [pl.BlockSpec((1, LANES), lambda i: (0, 0)), _row_spec(tm, d)],
                 [_sds((1, LANES), F32), _sds((t, d), F32)], [pltpu.VMEM((1, d), F32)])(y, target)


def _adamw(g, w, m, v):
    m2 = ADAM_B1 * m + (1.0 - ADAM_B1) * g
    v2 = ADAM_B2 * v + (1.0 - ADAM_B2) * (g * g)
    m_hat = m2 / (1.0 - ADAM_B1 ** ADAM_STEP)
    v_hat = v2 / (1.0 - ADAM_B2 ** ADAM_STEP)
    delta = -ADAM_LR * (m_hat / (jnp.sqrt(v_hat) + ADAM_EPS) + ADAM_WD * w)
    return delta, m2, v2


ROW_TILE_CAP = 512


def _row_tile(rows, cap=ROW_TILE_CAP):
    if rows <= cap:
        return rows
    for tr in range(cap, 15, -16):
        if rows % tr == 0:
            return tr
    return rows


def _sum8_adamw(name, dev, lands, parts, w, m, v):
    nl = len(lands)
    _, r, c = lands[0].shape
    tr = _row_tile(r, cap=256)

    def body(dev_ref, *refs):
        land, own = refs[:nl], refs[nl:2 * nl]
        w_ref, m_ref, v_ref, g_ref, d_ref, m2_ref, v2_ref = refs[2 * nl:]
        layer, me = pl.program_id(0), dev_ref[0]
        for l in range(nl):
            @pl.when(layer == l)
            def _(l=l):
                g = None
                for s in range(N_DEV):
                    part = jnp.where(me == s, own[l][...], land[l][s]).astype(F32)
                    g = part if g is None else g + part
                delta, m2, v2 = _adamw(g, w_ref[...], m_ref[...], v_ref[...])
                g_ref[...] = g
                d_ref[...] = delta
                m2_ref[...] = m2
                v2_ref[...] = v2

    def rows_of(l, a, i):
        return jnp.where(a == l, i, 0)

    spec = pl.BlockSpec((None, tr, c), lambda a, i, dev_ref: (a, i, 0))
    in_specs = [pl.BlockSpec((N_DEV, tr, c), lambda a, i, dev_ref, l=l: (0, rows_of(l, a, i), 0)) for l in range(nl)]
    in_specs += [pl.BlockSpec((None, tr, c), lambda a, i, dev_ref, l=l: (dev_ref[0], rows_of(l, a, i), 0))
                 for l in range(nl)]
    grid_spec = pltpu.PrefetchScalarGridSpec(
        num_scalar_prefetch=1, grid=(nl, r // tr), in_specs=in_specs + [spec] * 3, out_specs=[spec] * 4)
    return pl.pallas_call(
        body, name=name, grid_spec=grid_spec, out_shape=[_sds(w.shape, F32)] * 4,
        compiler_params=pltpu.CompilerParams(vmem_limit_bytes=VMEM_LIMIT))(dev, *lands, *parts, w, m, v)


def _sum8(name, parts):
    _, r, c = parts.shape
    tr = _row_tile(r)

    def body(p_ref, o_ref):
        acc = p_ref[0]
        for s in range(1, N_DEV):
            acc = acc + p_ref[s]
        o_ref[...] = acc

    return _call(body, name, (r // tr,), [pl.BlockSpec((N_DEV, tr, c), lambda i: (0, i, 0))],
                 pl.BlockSpec((tr, c), lambda i: (i, 0)), _sds((r, c), F32))(parts)


def _adamw_small(name, gs, ws, ms, vs):
    n = len(gs)

    def body(*refs):
        g, w, m, v = (refs[k * n:(k + 1) * n] for k in range(4))
        d_out, m_out, v_out = (refs[(4 + k) * n:(5 + k) * n] for k in range(3))
        for k in range(n):
            d_out[k][...], m_out[k][...], v_out[k][...] = _adamw(g[k][...], w[k][...], m[k][...], v[k][...])

    vmem = pl.BlockSpec(memory_space=pltpu.VMEM)
    outs = _call(body, name, (), [vmem] * (4 * n), [vmem] * (3 * n), [_sds(w.shape, F32) for w in ws] * 3)(
        *gs, *ws, *ms, *vs)
    return outs[:n], outs[n:2 * n], outs[2 * n:]


def _pack(arrs, row_multiple=SUBLANES):
    pieces, rows = [], 0
    for a in arrs:
        piece = a.reshape(-1, LANES)
        piece = jnp.pad(piece, ((0, (-piece.shape[0]) % SUBLANES), (0, 0)))
        pieces.append(piece)
        rows += piece.shape[0]
    if rows % row_multiple:
        pieces.append(jnp.zeros(((-rows) % row_multiple, LANES), pieces[0].dtype))
    return jnp.concatenate(pieces, axis=0)


def _unpack(buf, shapes, lead=0):
    out, pos = [], 0
    for shp in shapes:
        rows = math.prod(shp) // LANES
        piece = lax.slice_in_dim(buf, pos, pos + rows, axis=lead)
        out.append(piece.reshape(buf.shape[:lead] + tuple(shp)))
        pos += rows + (-rows) % SUBLANES
    return out


REPLICATED = ["conv_b_in", "conv_b_dw", "conv_ln_g", "conv_ln_b", "conv_b_out", "gmlp_w_s", "gmlp_b_s",
              "ffn_b_up", "ffn_b_dw", "ffn_b_down", "norm1_g", "norm1_b", "norm2_g", "norm2_b"]
SMALL_SHARDED = ["conv_w_dw", "gmlp_b_in", "gmlp_ln_g", "gmlp_ln_b", "gmlp_b_out", "ffn_w_dw"]
BIG = ["conv_w_in", "conv_w_out", "gmlp_w_in", "gmlp_w_out", "ffn_w_up", "ffn_w_down"]
WEIGHTS = ["conv_w_in", "conv_b_in", "conv_w_dw", "conv_b_dw", "conv_ln_g", "conv_ln_b", "conv_w_out", "conv_b_out",
           "gmlp_w_in", "gmlp_b_in", "gmlp_ln_g", "gmlp_ln_b", "gmlp_w_s", "gmlp_b_s", "gmlp_w_out", "gmlp_b_out",
           "ffn_w_up", "ffn_b_up", "ffn_w_dw", "ffn_b_dw", "ffn_w_down", "ffn_b_down",
           "norm1_g", "norm1_b", "norm2_g", "norm2_b"]


def _from_shards(g, lead_shape):
    nd = len(lead_shape)
    perm = tuple(range(1, nd + 1)) + (0, nd + 1)
    return g.transpose(perm).reshape(tuple(lead_shape) + (-1,))


def _to_shards(full, width):
    lead = full.shape[:-1]
    nd = len(lead)
    parts = full.reshape(lead + (N_DEV, width))
    return parts.transpose((nd,) + tuple(range(nd)) + (nd + 1,))


def _step(p):
    x_in, target_in = p["x"], p["loss_target"]
    bsz, seq, d = x_in.shape
    t = bsz * seq
    assert seq % TM_EW == 0 and TM_EW % CHUNK == 0 and TM_EW // SUBLANES >= CONV_K - 1
    x0 = _to_segments(x_in.reshape(t, d), TM_EW)
    target = _to_segments(target_in.reshape(t, d), TM_EW)
    n_conv, n_gmlp = p["conv_w_in"].shape[0], p["gmlp_w_in"].shape[0]
    fb = p["ffn_w_up"].shape[-1]
    nblk = N_DEV
    half = nblk // 2
    cw = p["conv_w_in"].shape[-1]
    tm = min(TM_MM, t)
    nt = t // tm
    tk = min(TK_DW, t)
    ntk = t // tk
    dev = 4 * lax.axis_index("x") + 2 * lax.axis_index("y") + lax.axis_index("c")

    small_shapes = [p[n].shape for n in SMALL_SHARDED]
    small_src = _pack([p[n] for n in SMALL_SHARDED])[None]
    small_all = _all_gather("gather_small_weights", [small_src])[0][0]
    sm = _unpack(small_all, small_shapes, lead=1)
    w_src = []
    for i in range(DEPTH):
        mix = "conv" if i % 2 == 0 else "gmlp"
        w_src += [p[mix + "_w_in"][i // 2].astype(BF16), p[mix + "_w_out"][i // 2].astype(BF16),
                  p["ffn_w_up"][i].T.astype(BF16), p["ffn_w_down"][i].astype(BF16)]
    send_sems, recv_sems, w_land, _ = _gather_start(
        "weights_gather_start", _place_own("weights_place_own", w_src, deps=[small_all]))
    W_IN, W_OUT, W_UP, W_DOWN = range(4)

    def wait_weight(i, k, after):
        return _gather_wait(f"l{i}_weights_wait{k}", w_land[4 * i + k], send_sems, recv_sems, 4 * i + k, after)
    conv_w_dw = _from_shards(sm[0], sm[0].shape[1:-1])
    gmlp_b_in = _from_shards(sm[1], sm[1].shape[1:-1])
    gmlp_ln_g = _from_shards(sm[2], sm[2].shape[1:-1])
    gmlp_ln_b = _from_shards(sm[3], sm[3].shape[1:-1])
    gmlp_b_out = _from_shards(sm[4], sm[4].shape[1:-1])
    ffn_w_dw = sm[5].transpose(1, 0, 2, 3)

    def rows3(a):
        return a.reshape(a.shape[0], 1, a.shape[-1])

    conv_b_in4 = p["conv_b_in"].reshape(n_conv, N_DEV, 1, cw)
    gmlp_b_in4 = gmlp_b_in.reshape(n_gmlp, N_DEV, 1, cw)
    ffn_b_up4 = p["ffn_b_up"].reshape(DEPTH, nblk, 1, fb)
    ffn_b_dw4 = p["ffn_b_dw"].reshape(DEPTH, nblk, 1, fb)
    conv_b_dw3, conv_ln_g3, conv_ln_b3 = rows3(p["conv_b_dw"]), rows3(p["conv_ln_g"]), rows3(p["conv_ln_b"])
    conv_b_out3, gmlp_b_out3, ffn_b_down3 = rows3(p["conv_b_out"]), rows3(gmlp_b_out), rows3(p["ffn_b_down"])
    gmlp_ln_g3, gmlp_ln_b3 = rows3(gmlp_ln_g), rows3(gmlp_ln_b)
    n1g3, n1b3, n2g3, n2b3 = rows3(p["norm1_g"]), rows3(p["norm1_b"]), rows3(p["norm2_g"]), rows3(p["norm2_b"])
    w_tile, bs_tile = _sgu_operands(p["gmlp_w_s"], p["gmlp_b_s"])

    def mm_in(name, xa, wg, l, bias4, glu=False):
        tmi = min(TM_LN, t)
        c_half = half * cw

        def body(a_ref, b_ref, bias_ref, h_ref, *u_ref):
            xb = a_ref[...].astype(BF16)
            for n in range(N_DEV):
                h_ref[:, n * cw:(n + 1) * cw] = jnp.dot(xb, b_ref[n], preferred_element_type=F32) + bias_ref[n]
            if glu:
                u_ref[0][...] = h_ref[:, :c_half] * _sigmoid(h_ref[:, c_half:])

        outs = _call(body, name, (t // tmi,),
                     [pl.BlockSpec((tmi, d), lambda i: (i, 0)), pl.BlockSpec((N_DEV, d, cw), lambda i: (0, 0, 0)),
                      pl.BlockSpec((None, N_DEV, 1, cw), lambda i: (l, 0, 0, 0))],
                     [pl.BlockSpec((tmi, N_DEV * cw), lambda i: (i, 0))]
                     + ([pl.BlockSpec((tmi, c_half), lambda i: (i, 0))] if glu else []),
                     [_sds((t, N_DEV * cw), F32)] + ([_sds((t, c_half), F32)] if glu else []))(xa, wg, bias4)
        return outs if glu else outs[0]

    def mm_out_dx(name, dy, w, deps=()):
        return _matmul(name, dy, w, "nt", grid=(nt,),
                       a_spec=pl.BlockSpec((tm, d), lambda i: (i, 0)),
                       b_spec=pl.BlockSpec((d, d), lambda i: (0, 0)),
                       o_spec=pl.BlockSpec((tm, d), lambda i: (i, 0)), o_shape=(t, d), o_dtype=F32, deps=deps)

    def mm_out_dw(name, sa, dy):
        return _matmul(name, sa, dy, "tn", grid=(nt,), k_axis=0, nk=nt, acc_shape=(d, d),
                       a_spec=pl.BlockSpec((tm, d), lambda k: (k, 0)),
                       b_spec=pl.BlockSpec((tm, d), lambda k: (k, 0)),
                       o_spec=pl.BlockSpec((d, d), lambda k: (0, 0)), o_shape=(d, d), o_dtype=BF16)

    def mm_in_dx(name, dh, wg, res, norm=None):
        tmx = min(TM_LN, t)

        def body(a_ref, b_ref, res_ref, *refs):
            y = ALPHA * res_ref[...]
            for n in range(N_DEV):
                y = y + lax.dot_general(a_ref[:, n * cw:(n + 1) * cw], b_ref[n], (_DIMS["nt"], ((), ())),
                                        preferred_element_type=F32)
            if norm is None:
                refs[0][...] = y
            else:
                _ln_bwd_rows(y, *refs[:-1], pl.program_id(0) == 0, refs[-1])

        row = pl.BlockSpec((tmx, d), lambda i: (i, 0))
        ins = [pl.BlockSpec((tmx, N_DEV * cw), lambda i: (i, 0)), pl.BlockSpec((N_DEV, d, cw), lambda i: (0, 0, 0)), row]
        if norm is None:
            return _call(body, name, (t // tmx,), ins, row, _sds((t, d), F32))(dh, wg, res)
        ln_ins, ln_outs = _ln_bwd_specs(tmx, d, norm[3], lambda i: (i, 0))
        return _call(body, name, (t // tmx,), ins + ln_ins, ln_outs + [row],
                     [_sds((t, d), F32)] + [_sds((1, d), F32)] * 3 + [_sds((t, d), BF16)])(dh, wg, res, *norm[:3])

    def mm_in_dw(name, xa, dh):
        def body(a_ref, b_ref, o_ref, acc_ref):
            k = pl.program_id(1)
            p = lax.dot_general(a_ref[...].astype(BF16), b_ref[...], (_DIMS["tn"], ((), ())),
                                preferred_element_type=F32)
            _acc_rows(acc_ref, p, k == 0)

            @pl.when(k == nt - 1)
            def _():
                for n in range(half):
                    o_ref[n] = acc_ref[:, n * cw:(n + 1) * cw].astype(BF16)

        return _call(body, name, (2, nt),
                     [pl.BlockSpec((tm, d), lambda c, k: (k, 0)), pl.BlockSpec((tm, half * cw), lambda c, k: (k, c))],
                     pl.BlockSpec((half, d, cw), lambda c, k: (c, 0, 0)), _sds((N_DEV, d, cw), BF16),
                     [pltpu.VMEM((d, half * cw), F32)])(xa, dh)

    def mm_down_dw(name, a, dy, deps=()):
        return _matmul(name, a, dy, "tn", grid=(half, ntk), k_axis=1, nk=ntk, acc_shape=(fb, d),
                       a_spec=pl.BlockSpec((None, tk, fb), lambda n, k: (n, k, 0)),
                       b_spec=pl.BlockSpec((tk, d), lambda n, k: (k, 0)),
                       o_spec=pl.BlockSpec((None, fb, d), lambda n, k: (n, 0, 0)),
                       o_shape=(half, fb, d), o_dtype=BF16, deps=deps)

    def mm_up_dw(name, xa, dh):
        return _matmul(name, dh, xa, "tn", grid=(nblk, ntk), k_axis=1, nk=ntk, acc_shape=(fb, d),
                       a_spec=pl.BlockSpec((None, tk, fb), lambda n, k: (n, k, 0)),
                       b_spec=pl.BlockSpec((tk, d), lambda n, k: (k, 0)),
                       o_spec=pl.BlockSpec((None, fb, d), lambda n, k: (n, 0, 0)),
                       o_shape=(nblk, fb, d), o_dtype=BF16)

    saved = []
    xcur = xcur_b = x0
    for i in range(DEPTH):
        j = i // 2
        s = {"x": xcur_b}
        s["w_in"] = wait_weight(i, W_IN, xcur if i else target)
        if i % 2 == 0:
            s["h"], s["u"] = mm_in(f"l{i}_conv_in_glu", xcur_b, s["w_in"], j, conv_b_in4, glu=True)
            s["c"] = _dwconv31(f"l{i}_dwconv", s["u"], conv_w_dw, conv_b_dw3, j, seq)
            s["s"] = _ln_silu(f"l{i}_ln_silu", s["c"], conv_ln_g3, conv_ln_b3, j)
            b_out3 = conv_b_out3
        else:
            s["h"] = mm_in(f"l{i}_gmlp_in", xcur_b, s["w_in"], j, gmlp_b_in4)
            s["s"] = _sgu(f"l{i}_sgu", s["h"], gmlp_ln_g3, gmlp_ln_b3, w_tile, bs_tile, j)
            b_out3 = gmlp_b_out3
        s["w_out"] = wait_weight(i, W_OUT, s["s"]).reshape(d, d)
        s["x1"], s["x1b"], s["xhat1"], s["rstd1"] = _matmul_ln(
            f"l{i}_mixer_out_norm1", s["s"], s["w_out"], xcur, b_out3, n1g3, n1b3, j, i)
        s["w_up"] = wait_weight(i, W_UP, s["x1"])
        s["hg"], s["hv"], s["a"] = _ffn_up_act(f"l{i}_ffn_up_act", s["x1b"], s["w_up"], ffn_b_up4, ffn_w_dw,
                                               ffn_b_dw4, i, seq)
        s["w_down"] = wait_weight(i, W_DOWN, s["a"]).reshape(half, fb, d)
        xcur, xcur_b, s["xhat2"], s["rstd2"] = _matmul_ln(
            f"l{i}_ffn_down_norm2", s["a"], s["w_down"], s["x1"], ffn_b_down3, n2g3, n2b3, i, i)
        saved.append(s)

    loss_row, dx = _loss("loss", xcur, target)

    started = {n: [None] * p[n].shape[0] for n in BIG}
    tokens = []

    def send_grads(name, items):
        done, token = _scatter_start(name, [g for _, _, g in items])
        for (n, l, _), st in zip(items, done):
            started[n][l] = st
        tokens.append(token)

    def take_tokens():
        out = list(tokens)
        tokens.clear()
        return out

    gl = {n: [None] * p[n].shape[0] for n in REPLICATED + SMALL_SHARDED}
    dr2, gl["norm2_g"][DEPTH - 1], gl["norm2_b"][DEPTH - 1], gl["ffn_b_down"][DEPTH - 1], dr2b = _ln_res_bwd(
        f"l{DEPTH - 1}_norm2_bwd", dx, saved[-1]["xhat2"], saved[-1]["rstd2"], n2g3, DEPTH - 1)
    for i in reversed(range(DEPTH)):
        j = i // 2
        s = saved[i]
        mix = "conv" if i % 2 == 0 else "gmlp"
        g_down = mm_down_dw(f"l{i}_ffn_down_dw", s["a"], dr2b, deps=take_tokens()).reshape(N_DEV, -1, d)
        send_grads(f"l{i}_ffn_down_grad_scatter_start", [("ffn_w_down", i, g_down)])
        dcg, dcv, dbg, dbv, dwg, dwv = _ffn_act_bwd(f"l{i}_ffn_act_bwd", dr2b, s["w_down"], s["hg"], s["hv"],
                                                    ffn_w_dw, ffn_b_dw4, i, seq, deps=take_tokens())
        gl["ffn_b_dw"][i] = jnp.concatenate([dbg, dbv], axis=0).reshape(1, nblk * fb)
        gl["ffn_w_dw"][i] = jnp.concatenate([dwg[:, :FFN_K], dwv[:, :FFN_K]], axis=0)
        dh, dbu, dr1, gl["norm1_g"][i], gl["norm1_b"][i], gl[mix + "_b_out"][j], dr1b = _ffn_conv_t_dx(
            f"l{i}_ffn_conv_t_dx", dcg, dcv, ffn_w_dw, s["w_up"], dr2, s["xhat1"], s["rstd1"], n1g3, i, seq,
            deps=take_tokens())
        gl["ffn_b_up"][i] = dbu.reshape(1, nblk * fb)
        send_grads(f"l{i}_ffn_up_grad_scatter_start", [("ffn_w_up", i, mm_up_dw(f"l{i}_ffn_up_dw", s["x1b"], dh))])
        ds = mm_out_dx(f"l{i}_{mix}_out_dx", dr1b, s["w_out"], deps=take_tokens())
        g_out = mm_out_dw(f"l{i}_{mix}_out_dw", s["s"], dr1b).reshape(N_DEV, -1, d)
        if i % 2 == 0:
            dc, gl["conv_ln_g"][j], gl["conv_ln_b"][j], gl["conv_b_dw"][j] = _ln_silu_bwd(
                f"l{i}_ln_silu_bwd", ds, s["c"], conv_ln_g3, conv_ln_b3, j)
            du, dwdw = _dwconv31_bwd(f"l{i}_dwconv_bwd", dc, s["u"], conv_w_dw, j, seq)
            gl["conv_w_dw"][j] = dwdw[:CONV_K]
            dh, gl["conv_b_in"][j] = _glu_bwd(f"l{i}_glu_bwd", du, s["h"])
        else:
            dh, gl["gmlp_b_in"][j], dwt, dbt, gl["gmlp_ln_g"][j], gl["gmlp_ln_b"][j] = _sgu_bwd(
                f"l{i}_sgu_bwd", ds, s["h"], gmlp_ln_g3, gmlp_ln_b3, w_tile, bs_tile, j)
            gl["gmlp_w_s"][j], gl["gmlp_b_s"][j] = _sgu_param_grads(dwt, dbt)
            if i == 1:
                ws_local = jnp.stack(gl["gmlp_w_s"]).reshape(-1, LANES)
                ws_send, ws_recv, ws_land, ws_token = _gather_start(
                    "w_s_grads_gather_start", _place_own("w_s_grads_place_own", [ws_local]))
                tokens.append(ws_token)
        if i > 0:
            prev = saved[i - 1]
            dr2, gl["norm2_g"][i - 1], gl["norm2_b"][i - 1], gl["ffn_b_down"][i - 1], dr2b = mm_in_dx(
                f"l{i}_{mix}_in_dx_norm2_bwd", dh, s["w_in"], dr1, (prev["xhat2"], prev["rstd2"], n2g3, i - 1))
        else:
            dx = mm_in_dx(f"l{i}_{mix}_in_dx", dh, s["w_in"], dr1)
        send_grads(f"l{i}_mixer_grads_scatter_start",
                   [(mix + "_w_out", j, g_out), (mix + "_w_in", j, mm_in_dw(f"l{i}_{mix}_in_dw", s["x"], dh))])
    grad_x = _from_segments(dx, TM_EW).reshape(bsz, seq, d)

    late = [n for n in REPLICATED if n != "gmlp_w_s"]
    full_small = {n: jnp.stack(gl[n]).reshape(p[n].shape) for n in late}
    shard_small = {}
    for n in SMALL_SHARDED:
        if n == "ffn_w_dw":
            shard_small[n] = jnp.stack(gl[n]).transpose(1, 0, 2, 3)
        else:
            width = p[n].shape[-1]
            lead = p[n].shape[:-1]
            shard_small[n] = _to_shards(jnp.stack(gl[n]).reshape(lead + (N_DEV * width,)), width)
    flat_shapes = [(1, LANES)] + [p[n].shape for n in late] + [(N_DEV,) + p[n].shape for n in SMALL_SHARDED]
    flat_local = _pack([loss_row] + [full_small[n] for n in late] + [shard_small[n] for n in SMALL_SHARDED],
                       row_multiple=ROW_TILE_CAP)

    small_send, small_recv, small_land, small_token = _gather_start(
        "small_grads_gather_start", _place_own("small_grads_place_own", [flat_local]))

    grads, delta, new_m, new_v = {}, {}, {}, {}
    dev1 = jnp.reshape(dev, (1,)).astype(jnp.int32)
    order = ["ffn_w_down", "ffn_w_up", "gmlp_w_out", "gmlp_w_in", "conv_w_out", "conv_w_in"]
    after = small_token
    for n in order:
        parts_done, lands_done = _scatter_wait(f"grads_{n}_scatter_wait", started[n], after)
        state = [p[n], p["m_" + n], p["v_" + n]]
        if n == "ffn_w_up":
            state = [a.transpose(0, 2, 1) for a in state]
        outs = _sum8_adamw(f"adamw_{n}", dev1, lands_done, parts_done, *state)
        after = outs[-1]
        if n == "ffn_w_up":
            outs = [a.transpose(0, 2, 1) for a in outs]
        grads[n], delta[n], new_m[n], new_v[n] = outs

    ws_parts = _gather_wait("w_s_grads_gather_wait", ws_land[0], ws_send, ws_recv, 0, after)
    ws_sum = _sum8("sum_w_s_grads", ws_parts)
    grads["gmlp_w_s"] = ws_sum.reshape(p["gmlp_w_s"].shape)
    small_parts = _gather_wait("small_grads_gather_wait", small_land[0], small_send, small_recv, 0, ws_sum)
    summed = _unpack(_sum8("sum_small_grads", small_parts), flat_shapes)
    loss = summed[0][0, 0]
    grads.update(zip(late, summed[1:1 + len(late)]))
    for n, g in zip(SMALL_SHARDED, summed[1 + len(late):]):
        grads[n] = lax.dynamic_index_in_dim(g, dev, axis=0, keepdims=False)
    small = REPLICATED + SMALL_SHARDED
    d_s, m_s, v_s = _adamw_small("adamw_small", [grads[n] for n in small], [p[n] for n in small],
                                 [p["m_" + n] for n in small], [p["v_" + n] for n in small])
    for n, dd, mm, vv in zip(small, d_s, m_s, v_s):
        delta[n], new_m[n], new_v[n] = dd, mm, vv

    return (loss, grad_x, *[grads[n] for n in WEIGHTS], *[delta[n] for n in WEIGHTS],
            *[new_m[n] for n in WEIGHTS], *[new_v[n] for n in WEIGHTS])


def kernel(x, conv_w_in, conv_b_in, conv_w_dw, conv_b_dw, conv_ln_g, conv_ln_b, conv_w_out, conv_b_out, gmlp_w_in, gmlp_b_in, gmlp_ln_g, gmlp_ln_b, gmlp_w_s, gmlp_b_s, gmlp_w_out, gmlp_b_out, ffn_w_up, ffn_b_up, ffn_w_dw, ffn_b_dw, ffn_w_down, ffn_b_down, norm1_g, norm1_b, norm2_g, norm2_b, loss_target, m_conv_w_in, m_conv_b_in, m_conv_w_dw, m_conv_b_dw, m_conv_ln_g, m_conv_ln_b, m_conv_w_out, m_conv_b_out, m_gmlp_w_in, m_gmlp_b_in, m_gmlp_ln_g, m_gmlp_ln_b, m_gmlp_w_s, m_gmlp_b_s, m_gmlp_w_out, m_gmlp_b_out, m_ffn_w_up, m_ffn_b_up, m_ffn_w_dw, m_ffn_b_dw, m_ffn_w_down, m_ffn_b_down, m_norm1_g, m_norm1_b, m_norm2_g, m_norm2_b, v_conv_w_in, v_conv_b_in, v_conv_w_dw, v_conv_b_dw, v_conv_ln_g, v_conv_ln_b, v_conv_w_out, v_conv_b_out, v_gmlp_w_in, v_gmlp_b_in, v_gmlp_ln_g, v_gmlp_ln_b, v_gmlp_w_s, v_gmlp_b_s, v_gmlp_w_out, v_gmlp_b_out, v_ffn_w_up, v_ffn_b_up, v_ffn_w_dw, v_ffn_b_dw, v_ffn_w_down, v_ffn_b_down, v_norm1_g, v_norm1_b, v_norm2_g, v_norm2_b):
    return _step(dict(locals()))
```

```python
import math

import jax
import jax.numpy as jnp
from jax import lax
from jax.experimental import pallas as pl
from jax.experimental.pallas import tpu as pltpu

F32 = jnp.float32
BF16 = jnp.bfloat16
MESH = pl.DeviceIdType.MESH

N_DEV = 8
DEPTH = 4
ALPHA = (2.0 * DEPTH) ** 0.25
LN_EPS = 1e-5
CONV_K = 31
FFN_K = 3
CHUNK = 128
GROUPS = 8
ADAM_LR = 0.001
ADAM_B1 = 0.9
ADAM_B2 = 0.999
ADAM_EPS = 1e-08
ADAM_WD = 0.01
ADAM_STEP = 10
INV_SQRT2 = 1.0 / math.sqrt(2.0)
INV_SQRT2PI = 1.0 / math.sqrt(2.0 * math.pi)

LANES = 128
SUBLANES = 8
VMEM_LIMIT = 56 * 1024 * 1024
TM_MM = 1024
TK_DW = 2048
TM_EW = 256
TM_ROW = 512


def _call(body, name, grid, in_specs, out_specs, out_shape, scratch=(), aliases=None, deps=()):
    deps = list(deps)
    in_specs = list(in_specs)
    n_in = len(in_specs)
    if deps:
        inner = body

        def body(*refs):
            return inner(*refs[:n_in], *refs[n_in + len(deps):])

        in_specs = in_specs + [pl.BlockSpec(memory_space=pl.ANY)] * len(deps)
    fn = pl.pallas_call(
        body, name=name, grid=grid, in_specs=in_specs, out_specs=out_specs, out_shape=out_shape,
        scratch_shapes=list(scratch), input_output_aliases=aliases or {},
        compiler_params=pltpu.CompilerParams(vmem_limit_bytes=VMEM_LIMIT))
    return lambda *args: fn(*args, *deps)


def _sds(shape, dtype):
    return jax.ShapeDtypeStruct(tuple(shape), dtype)


def _sigmoid(x):
    return 1.0 / (1.0 + jnp.exp(-x))


def _acc_rows(ref, val, first):
    @pl.when(first)
    def _():
        ref[...] = val

    @pl.when(jnp.logical_not(first))
    def _():
        ref[...] += val


def _colsum(v):
    return jnp.sum(v, axis=0, keepdims=True)


_DIMS = {"nn": ((1,), (0,)), "nt": ((1,), (1,)), "tn": ((0,), (0,))}


def _matmul(name, a, b, mode, *, grid, a_spec, b_spec, o_spec, o_shape, o_dtype, k_axis=None, nk=1,
            acc_shape=None, bias=None, bias_spec=None, res=None, res_spec=None, res_scale=1.0, deps=()):
    dims = (_DIMS[mode], ((), ()))
    has_bias, has_res = bias is not None, res is not None

    def body(*refs):
        a_ref, b_ref = refs[0], refs[1]
        pos = 2
        bias_ref = res_ref = None
        if has_bias:
            bias_ref = refs[pos]
            pos += 1
        if has_res:
            res_ref = refs[pos]
            pos += 1
        o_ref = refs[pos]
        acc_ref = refs[pos + 1] if nk > 1 else None
        p = lax.dot_general(a_ref[...].astype(BF16), b_ref[...].astype(BF16), dims, preferred_element_type=F32)

        def finish(acc):
            if has_bias:
                acc = acc + bias_ref[...]
            if has_res:
                acc = acc + res_scale * res_ref[...]
            o_ref[...] = acc.astype(o_dtype)

        if nk == 1:
            finish(p)
        else:
            k = pl.program_id(k_axis)

            @pl.when(k == 0)
            def _():
                acc_ref[...] = p

            @pl.when(k > 0)
            def _():
                acc_ref[...] += p

            @pl.when(k == nk - 1)
            def _():
                finish(acc_ref[...])

    ins, specs = [a, b], [a_spec, b_spec]
    if has_bias:
        ins.append(bias)
        specs.append(bias_spec)
    if has_res:
        ins.append(res)
        specs.append(res_spec)
    scratch = [pltpu.VMEM(acc_shape, F32)] if nk > 1 else []
    return _call(body, name, grid, specs, o_spec, _sds(o_shape, o_dtype), scratch, deps=deps)(*ins)


TM_LN = 512


def _matmul_ln(name, a, b, x_res, bias3, g3, b3, l_bias, l_norm):
    t, d = x_res.shape
    tm = min(TM_LN, t)
    blocked = a.ndim == 3

    def body(a_ref, b_ref, x_ref, bias_ref, g_ref, be_ref, o_ref, ob_ref, xh_ref, rs_ref):
        if blocked:
            y = None
            for k in range(a.shape[0]):
                p = jnp.dot(a_ref[k], b_ref[k], preferred_element_type=F32)
                y = p if y is None else y + p
        else:
            y = jnp.dot(a_ref[...], b_ref[...], preferred_element_type=F32)
        xhat, rstd = _ln_stats(ALPHA * x_ref[...] + y + bias_ref[...])
        out = xhat * g_ref[...] + be_ref[...]
        o_ref[...] = out
        ob_ref[...] = out.astype(BF16)
        xh_ref[...] = xhat
        rs_ref[...] = rstd

    if blocked:
        a_spec = pl.BlockSpec((a.shape[0], tm, a.shape[2]), lambda i: (0, i, 0))
        b_spec = pl.BlockSpec(b.shape, lambda i: (0, 0, 0))
    else:
        a_spec = pl.BlockSpec((tm, a.shape[1]), lambda i: (i, 0))
        b_spec = pl.BlockSpec(b.shape, lambda i: (0, 0))
    row = pl.BlockSpec((tm, d), lambda i: (i, 0))
    stat = pl.BlockSpec((tm, 1), lambda i: (i, 0))

    def vec(l):
        return pl.BlockSpec((None, 1, d), lambda i: (l, 0, 0))

    return _call(body, name, (t // tm,), [a_spec, b_spec, row, vec(l_bias), vec(l_norm), vec(l_norm)],
                 [row, row, row, stat],
                 [_sds((t, d), F32), _sds((t, d), BF16), _sds((t, d), F32), _sds((t, 1), F32)])(
                     a, b, x_res, bias3, g3, b3)


def _mesh_pos():
    return lax.axis_index("x"), lax.axis_index("y"), lax.axis_index("c")


def _any_specs(n):
    return [pl.BlockSpec(memory_space=pl.ANY)] * n


def _all_gather(name, srcs):
    n = len(srcs)

    def body(*refs):
        src, out = refs[:n], refs[n:2 * n]
        send_sems, recv_sems, local_sems = refs[2 * n:]
        x, y, c = _mesh_pos()
        me, sibling = (x, y, c), (x, y, 1 - c)
        chips = [(1 - x, y), (x, 1 - y), (1 - x, 1 - y)]

        def slot(k, p):
            return out[k].at[:, 4 * p[0] + 2 * p[1] + p[2]]

        def copy(k, idx, block, to, s=None):
            return pltpu.make_async_remote_copy(
                src_ref=slot(k, block) if s is None else s, dst_ref=slot(k, block),
                send_sem=send_sems.at[k * 7 + idx], recv_sem=recv_sems.at[k * 7 + idx],
                device_id=to, device_id_type=MESH)

        local = [pltpu.make_async_copy(src[k], slot(k, me), local_sems.at[k]) for k in range(n)]
        for cp in local:
            cp.start()
        first = []
        for k in range(n):
            first.append(copy(k, 0, me, sibling, src[k]))
            for j, chip in enumerate(chips):
                first.append(copy(k, 1 + j, me, (*chip, c), src[k]))
        for cp in first:
            cp.start()
        passed = []
        for j, chip in enumerate(chips):
            for k in range(n):
                copy(k, 1 + j, (*chip, c), me).wait_recv()
                cp = copy(k, 4 + j, (*chip, c), sibling)
                cp.start()
                passed.append(cp)
        for k in range(n):
            copy(k, 0, sibling, me).wait_recv()
            for j, chip in enumerate(chips):
                copy(k, 4 + j, (*chip, 1 - c), me).wait_recv()
        for cp in first + passed:
            cp.wait_send()
        for cp in local:
            cp.wait()

    out_shape = [_sds((s.shape[0], N_DEV) + s.shape[1:], s.dtype) for s in srcs]
    return _call(body, name, (), [pl.BlockSpec(memory_space=pltpu.VMEM)] * n, _any_specs(n), out_shape,
                 [pltpu.SemaphoreType.DMA((7 * n,)), pltpu.SemaphoreType.DMA((7 * n,)),
                  pltpu.SemaphoreType.DMA((n,))])(*srcs)


HBM_SPEC = pl.BlockSpec(memory_space=pltpu.HBM)
SEM_SPEC = pl.BlockSpec(memory_space=pltpu.SEMAPHORE)
N_PEER = N_DEV - 1


def _split_call(body, name, in_specs, out_specs, out_shape, aliases):
    return pl.pallas_call(
        body, name=name, in_specs=in_specs, out_specs=out_specs, out_shape=out_shape, input_output_aliases=aliases,
        compiler_params=pltpu.CompilerParams(has_side_effects=pltpu.SideEffectType.DATAFLOW_SIDE_EFFECTING))


def _peers(x, y, c):
    return [(1 - x if q & 4 else x, 1 - y if q & 2 else y, 1 - c if q & 1 else c) for q in range(1, N_DEV)]


def _in_hbm(a):
    return pltpu.with_memory_space_constraint(a, pltpu.HBM)


def _place_own(name, srcs, deps=()):
    n = len(srcs)

    def body(*refs):
        src, out, sems = refs[:n], refs[n:2 * n], refs[2 * n]
        x, y, c = _mesh_pos()
        dev = 4 * x + 2 * y + c
        copies = [pltpu.make_async_copy(src[k], out[k].at[dev], sems.at[k]) for k in range(n)]
        for cp in copies:
            cp.start()
        for cp in copies:
            cp.wait()

    return _call(body, name, (), [pl.BlockSpec(memory_space=pltpu.VMEM)] * n, _any_specs(n),
                 [_sds((N_DEV,) + s.shape, s.dtype) for s in srcs], [pltpu.SemaphoreType.DMA((n,))],
                 deps=deps)(*srcs)


def _gather_start(name, lands):
    n = len(lands)

    def body(*refs):
        land, send_sems, recv_sems = refs[:n], refs[n], refs[n + 1]
        x, y, c = _mesh_pos()
        dev = 4 * x + 2 * y + c
        for k in range(n):
            for peer in _peers(x, y, c):
                pltpu.make_async_remote_copy(
                    src_ref=land[k].at[dev], dst_ref=land[k].at[dev], send_sem=send_sems.at[k],
                    recv_sem=recv_sems.at[k], device_id=peer, device_id_type=MESH).start()
        token = refs[-1]
        token[...] = jnp.zeros_like(token)

    outs = _split_call(
        body, name, [HBM_SPEC] * n, [SEM_SPEC, SEM_SPEC] + [HBM_SPEC] * n + [pl.BlockSpec(memory_space=pltpu.VMEM)],
        [pltpu.SemaphoreType.DMA((n,)), pltpu.SemaphoreType.DMA((n,))] + [pltpu.HBM(a.shape, a.dtype) for a in lands]
        + [_sds((SUBLANES, LANES), F32)],
        {k: 2 + k for k in range(n)})(*[_in_hbm(a) for a in lands])
    return outs[0], outs[1], list(outs[2:2 + n]), outs[-1]


def _wait_seven(src_ref, dst_ref, send_sem, recv_sem):
    cp = pltpu.make_async_remote_copy(
        src_ref=src_ref.at[pl.ds(0, N_PEER)], dst_ref=dst_ref.at[pl.ds(0, N_PEER)], send_sem=send_sem,
        recv_sem=recv_sem, device_id=_mesh_pos(), device_id_type=MESH)
    cp.wait_send()
    cp.wait_recv()


def _gather_wait(name, land, send_sems, recv_sems, k, after):
    def body(land_ref, send_ref, recv_ref, after_ref, out_ref):
        _wait_seven(land_ref, land_ref, send_ref.at[k], recv_ref.at[k])

    return _split_call(body, name, [HBM_SPEC, SEM_SPEC, SEM_SPEC, pl.BlockSpec(memory_space=pl.ANY)], HBM_SPEC,
                       pltpu.HBM(land.shape, land.dtype), {0: 0})(land, send_sems, recv_sems, after)


def _scatter_start(name, parts_list):
    n = len(parts_list)

    def body(*refs):
        x, y, c = _mesh_pos()
        dev = 4 * x + 2 * y + c
        for k in range(n):
            parts_ref, land_ref = refs[2 * k], refs[2 * k + 1]
            send_sem, recv_sem = refs[2 * n + 4 * k], refs[2 * n + 4 * k + 1]
            for peer in _peers(x, y, c):
                pltpu.make_async_remote_copy(
                    src_ref=parts_ref.at[4 * peer[0] + 2 * peer[1] + peer[2]], dst_ref=land_ref.at[dev],
                    send_sem=send_sem, recv_sem=recv_sem, device_id=peer, device_id_type=MESH).start()
        token = refs[-1]
        token[...] = jnp.zeros_like(token)

    ins, out_specs, out_shape, aliases = [], [], [], {}
    for k, parts in enumerate(parts_list):
        buf = pltpu.HBM(parts.shape, parts.dtype)
        ins += [_in_hbm(parts), _in_hbm(lax.empty(parts.shape, parts.dtype))]
        out_specs += [SEM_SPEC, SEM_SPEC, HBM_SPEC, HBM_SPEC]
        out_shape += [pltpu.SemaphoreType.DMA(()), pltpu.SemaphoreType.DMA(()), buf, buf]
        aliases.update({2 * k: 4 * k + 2, 2 * k + 1: 4 * k + 3})
    outs = _split_call(body, name, [HBM_SPEC] * (2 * n), out_specs + [pl.BlockSpec(memory_space=pltpu.VMEM)],
                       out_shape + [_sds((SUBLANES, LANES), F32)], aliases)(*ins)
    return [tuple(outs[4 * k:4 * k + 4]) for k in range(n)], outs[-1]


def _scatter_wait(name, started, after):
    n = len(started)

    def body(*refs):
        for k in range(n):
            send_sem, recv_sem, parts_ref, land_ref = refs[4 * k:4 * k + 4]
            _wait_seven(parts_ref, land_ref, send_sem, recv_sem)

    flat = [a for s in started for a in s]
    outs = _split_call(
        body, name, [SEM_SPEC, SEM_SPEC, HBM_SPEC, HBM_SPEC] * n + [pl.BlockSpec(memory_space=pl.ANY)],
        [HBM_SPEC, HBM_SPEC] * n, [pltpu.HBM(a.shape, a.dtype) for s in started for a in s[2:]],
        {4 * k + 2 + t: 2 * k + t for k in range(n) for t in range(2)})(*flat, after)
    return list(outs[0::2]), list(outs[1::2])


def _to_segments(a, tile):
    seg = tile // SUBLANES
    return a.reshape((a.shape[0] // tile, SUBLANES, seg) + a.shape[1:]).swapaxes(1, 2).reshape(a.shape)


def _from_segments(a, tile):
    seg = tile // SUBLANES
    return a.reshape((a.shape[0] // tile, seg, SUBLANES) + a.shape[1:]).swapaxes(1, 2).reshape(a.shape)


def _chunk(ref, q):
    return ref[q * SUBLANES:(q + 1) * SUBLANES, :]


def _fill_wrap_prev(x_ref, halo_ref, wrap_ref, n_wrap, n_halo, seg, keep):
    sub = lax.broadcasted_iota(jnp.int32, (SUBLANES, x_ref.shape[-1]), 0)
    for j in range(n_wrap):
        q = seg - n_wrap + j
        hq = q - (seg - n_halo)
        row = halo_ref[hq * SUBLANES + SUBLANES - 1:(hq + 1) * SUBLANES, :] * keep
        wrap_ref[j * SUBLANES:(j + 1) * SUBLANES, :] = jnp.where(sub == 0, row, pltpu.roll(_chunk(x_ref, q), 1, 0))


def _fill_wrap_next(x_ref, halo_ref, wrap_ref, n_wrap, keep):
    sub = lax.broadcasted_iota(jnp.int32, (SUBLANES, x_ref.shape[-1]), 0)
    for j in range(n_wrap):
        row = halo_ref[j * SUBLANES:j * SUBLANES + 1, :] * keep
        wrap_ref[j * SUBLANES:(j + 1) * SUBLANES, :] = jnp.where(
            sub == SUBLANES - 1, row, pltpu.roll(_chunk(x_ref, j), SUBLANES - 1, 0))


def _past(x_ref, wrap_ref, q, d, n_wrap):
    return _chunk(x_ref, q - d) if q >= d else _chunk(wrap_ref, q - d + n_wrap)


def _future(x_ref, wrap_ref, q, d, seg):
    return _chunk(x_ref, q + d) if q + d < seg else _chunk(wrap_ref, q + d - seg)


def _conv_fwd(x_ref, wrap_ref, w_ref, b_ref, out_ref, seg, k_taps):
    bias = jnp.broadcast_to(b_ref[...], (SUBLANES, x_ref.shape[-1]))
    for q in range(seg):
        acc = bias
        for k in range(k_taps):
            acc = acc + w_ref[k:k + 1, :] * _past(x_ref, wrap_ref, q, k_taps - 1 - k, k_taps - 1)
        out_ref[q * SUBLANES:(q + 1) * SUBLANES, :] = acc


def _conv_bwd_data(d_ref, wrap_ref, w_ref, out_ref, seg, k_taps):
    for q in range(seg):
        acc = None
        for k in range(k_taps):
            term = w_ref[k:k + 1, :] * _future(d_ref, wrap_ref, q, k_taps - 1 - k, seg)
            acc = term if acc is None else acc + term
        out_ref[q * SUBLANES:(q + 1) * SUBLANES, :] = acc


def _conv_bwd_taps(d_ref, x_ref, wrap_ref, dw_ref, seg, k_taps):
    for k in range(k_taps):
        part = None
        for q in range(seg):
            term = _chunk(d_ref, q) * _past(x_ref, wrap_ref, q, k_taps - 1 - k, k_taps - 1)
            part = term if part is None else part + term
        dw_ref[k:k + 1, :] += _colsum(part)


def _tile_halo_specs(tm, width_block, n_halo, n_tiles, block_of):
    rows = n_halo * SUBLANES
    per = tm // rows
    tile = pl.BlockSpec(width_block(tm), lambda n, i: block_of(n, i))
    prev = pl.BlockSpec(width_block(rows), lambda n, i: block_of(n, jnp.maximum(i * per - 1, 0)))
    nxt = pl.BlockSpec(width_block(rows), lambda n, i: block_of(n, jnp.minimum((i + 1) * per, n_tiles * per - 1)))
    return tile, prev, nxt


def _ln_stats(v):
    mu = jnp.mean(v, axis=-1, keepdims=True)
    vc = v - mu
    var = jnp.mean(vc * vc, axis=-1, keepdims=True)
    rstd = lax.rsqrt(var + LN_EPS)
    return vc * rstd, rstd


def _ln_backward(dxhat, xhat, rstd):
    m1 = jnp.mean(dxhat, axis=-1, keepdims=True)
    m2 = jnp.mean(dxhat * xhat, axis=-1, keepdims=True)
    return rstd * (dxhat - m1 - xhat * m2)


def _row_spec(tm, width):
    return pl.BlockSpec((tm, width), lambda i: (i, 0))


def _param_spec(l, width):
    return pl.BlockSpec((None, 1, width), lambda *_: (l, 0, 0))


def _ln_bwd_rows(dout, xh_ref, rs_ref, g_ref, dr_ref, dg_ref, db_ref, dsum_ref, first, drb_ref=None):
    xhat = xh_ref[...]
    dr = _ln_backward(dout * g_ref[...], xhat, rs_ref[...])
    dr_ref[...] = dr
    if drb_ref is not None:
        drb_ref[...] = dr.astype(BF16)
    _acc_rows(dg_ref, _colsum(dout * xhat), first)
    _acc_rows(db_ref, _colsum(dout), first)
    _acc_rows(dsum_ref, _colsum(dr), first)


def _ln_bwd_specs(tm, d, l, row_of):
    vec = pl.BlockSpec((1, d), lambda *_: (0, 0))
    ins = [pl.BlockSpec((tm, d), row_of), pl.BlockSpec((tm, 1), row_of), _param_spec(l, d)]
    return ins, [pl.BlockSpec((tm, d), row_of), vec, vec, vec]


def _ln_res_bwd(name, dout, xhat, rstd, g3, l, deps=()):
    t, d = dout.shape
    tm = min(TM_ROW, t)

    def body(do_ref, xh_ref, rs_ref, g_ref, dr_ref, dg_ref, db_ref, dc_ref, drb_ref):
        _ln_bwd_rows(do_ref[...], xh_ref, rs_ref, g_ref, dr_ref, dg_ref, db_ref, dc_ref, pl.program_id(0) == 0,
                     drb_ref)

    ins, outs = _ln_bwd_specs(tm, d, l, lambda i: (i, 0))
    return _call(body, name, (t // tm,), [_row_spec(tm, d)] + ins, outs + [_row_spec(tm, d)],
                 [_sds((t, d), F32)] + [_sds((1, d), F32)] * 3 + [_sds((t, d), BF16)],
                 deps=deps)(dout, xhat, rstd, g3)


def _glu_bwd(name, du, h):
    t, c2 = h.shape
    c = c2 // 2
    tm = min(TM_ROW, t)

    def body(du_ref, a_ref, g_ref, dh_ref, db_ref):
        first = pl.program_id(0) == 0
        du_v, a = du_ref[...], a_ref[...]
        sg = _sigmoid(g_ref[...])
        da = du_v * sg
        dg = du_v * a * sg * (1.0 - sg)
        dh_ref[:, :c] = da.astype(BF16)
        dh_ref[:, c:] = dg.astype(BF16)
        _acc_rows(db_ref.at[:, :c], _colsum(da), first)
        _acc_rows(db_ref.at[:, c:], _colsum(dg), first)

    return _call(body, name, (t // tm,),
                 [_row_spec(tm, c), pl.BlockSpec((tm, c), lambda i: (i, 0)), pl.BlockSpec((tm, c), lambda i: (i, 1))],
                 [_row_spec(tm, c2), pl.BlockSpec((1, c2), lambda i: (0, 0))],
                 [_sds((t, c2), BF16), _sds((1, c2), F32)])(du, h, h)


CONV_CB = 512
TAPS_PAD = 32


def _dwconv31(name, u, w3, b3, l, seq):
    t, c = u.shape
    tm, cb = TM_EW, CONV_CB
    seg, seq_tiles, n_tiles = tm // SUBLANES, seq // tm, t // tm
    n_wrap = CONV_K - 1
    tile, prev, _ = _tile_halo_specs(tm, lambda rows: (rows, cb), seg, n_tiles, lambda n, r: (r, n))

    def body(u_ref, halo_ref, w_ref, b_ref, o_ref, wrap_ref):
        keep = (pl.program_id(1) % seq_tiles != 0).astype(F32)
        _fill_wrap_prev(u_ref, halo_ref, wrap_ref, n_wrap, seg, seg, keep)
        _conv_fwd(u_ref, wrap_ref, w_ref, b_ref, o_ref, seg, CONV_K)

    return _call(body, name, (c // cb, n_tiles),
                 [tile, prev, pl.BlockSpec((None, CONV_K, cb), lambda n, i: (l, 0, n)),
                  pl.BlockSpec((None, 1, cb), lambda n, i: (l, 0, n))],
                 tile, _sds((t, c), F32), [pltpu.VMEM((n_wrap * SUBLANES, cb), F32)])(u, u, w3, b3)


def _dwconv31_bwd(name, dc, u, w3, l, seq):
    t, c = dc.shape
    tm, cb = TM_EW, CONV_CB
    seg, seq_tiles, n_tiles = tm // SUBLANES, seq // tm, t // tm
    n_wrap = CONV_K - 1
    tile, prev, nxt = _tile_halo_specs(tm, lambda rows: (rows, cb), seg, n_tiles, lambda n, r: (r, n))

    def body(dc_ref, dcn_ref, u_ref, up_ref, w_ref, du_ref, dw_ref, dwrap_ref, uwrap_ref):
        i = pl.program_id(1)
        keep_prev = (i % seq_tiles != 0).astype(F32)
        keep_next = (i % seq_tiles != seq_tiles - 1).astype(F32)
        _fill_wrap_next(dc_ref, dcn_ref, dwrap_ref, n_wrap, keep_next)
        _conv_bwd_data(dc_ref, dwrap_ref, w_ref, du_ref, seg, CONV_K)

        @pl.when(i == 0)
        def _():
            dw_ref[...] = jnp.zeros_like(dw_ref)

        _fill_wrap_prev(u_ref, up_ref, uwrap_ref, n_wrap, seg, seg, keep_prev)
        _conv_bwd_taps(dc_ref, u_ref, uwrap_ref, dw_ref, seg, CONV_K)

    wrap = pltpu.VMEM((n_wrap * SUBLANES, cb), F32)
    return _call(body, name, (c // cb, n_tiles),
                 [tile, nxt, tile, prev, pl.BlockSpec((None, CONV_K, cb), lambda n, i: (l, 0, n))],
                 [tile, pl.BlockSpec((TAPS_PAD, cb), lambda n, i: (0, n))],
                 [_sds((t, c), F32), _sds((TAPS_PAD, c), F32)], [wrap, wrap])(dc, dc, u, u, w3)


def _ln_silu(name, cx, g3, b3, l):
    t, d = cx.shape
    tm = min(TM_ROW, t)

    def body(c_ref, g_ref, b_ref, o_ref):
        xhat, _ = _ln_stats(c_ref[...])
        nv = xhat * g_ref[...] + b_ref[...]
        o_ref[...] = (nv * _sigmoid(nv)).astype(BF16)

    return _call(body, name, (t // tm,), [_row_spec(tm, d), _param_spec(l, d), _param_spec(l, d)],
                 _row_spec(tm, d), _sds((t, d), BF16))(cx, g3, b3)


def _ln_silu_bwd(name, ds, cx, g3, b3, l, deps=()):
    t, d = cx.shape
    tm = min(TM_ROW, t)

    def body(ds_ref, c_ref, g_ref, b_ref, dc_ref, dg_ref, db_ref, dsum_ref):
        first = pl.program_id(0) == 0
        xhat, rstd = _ln_stats(c_ref[...])
        g = g_ref[...]
        nv = xhat * g + b_ref[...]
        sg = _sigmoid(nv)
        dn = ds_ref[...] * (sg * (1.0 + nv * (1.0 - sg)))
        dc = _ln_backward(dn * g, xhat, rstd)
        dc_ref[...] = dc
        _acc_rows(dg_ref, _colsum(dn * xhat), first)
        _acc_rows(db_ref, _colsum(dn), first)
        _acc_rows(dsum_ref, _colsum(dc), first)

    vec = pl.BlockSpec((1, d), lambda i: (0, 0))
    return _call(body, name, (t // tm,),
                 [_row_spec(tm, d), _row_spec(tm, d), _param_spec(l, d), _param_spec(l, d)],
                 [_row_spec(tm, d), vec, vec, vec],
                 [_sds((t, d), F32)] + [_sds((1, d), F32)] * 3, deps=deps)(ds, cx, g3, b3)


FFN_HALO = FFN_K - 1


def _ffn_conv(x_ref, halo_ref, wrap_ref, w_ref, b_ref, keep, seg, out_ref):
    _fill_wrap_prev(x_ref, halo_ref, wrap_ref, FFN_K - 1, FFN_HALO, seg, keep)
    _conv_fwd(x_ref, wrap_ref, w_ref, b_ref, out_ref, seg, FFN_K)


TM_FFN = 512
WRAP_ROWS = FFN_HALO * SUBLANES


def _sub_tiles(x_ref, prev_ref, next_ref, keep_prev, keep_next, n_sub):
    out = []
    for s in range(n_sub):
        tile = x_ref.at[pl.ds(s * TM_EW, TM_EW)]
        prev = prev_ref if s == 0 else x_ref.at[pl.ds(s * TM_EW - WRAP_ROWS, WRAP_ROWS)]
        nxt = next_ref if s == n_sub - 1 else x_ref.at[pl.ds((s + 1) * TM_EW, WRAP_ROWS)]
        out.append((tile, prev, keep_prev if s == 0 else 1.0, nxt, keep_next if s == n_sub - 1 else 1.0))
    return out


def _rows(ref, s, rows):
    return ref.at[pl.ds(s * rows, rows)]


def _ffn_specs(tm, fb, n_tiles):
    return _tile_halo_specs(tm, lambda rows: (None, rows, fb), FFN_HALO, n_tiles, lambda n, r: (n, r, 0))


def _ffn_up_act(name, x, w_up, b_up4, wdw, bdw, l, seq):
    t, d = x.shape
    nb, fb, _ = w_up.shape
    half = nb // 2
    tm = min(TM_FFN, seq)
    n_sub, seg, seq_steps, n_steps = tm // TM_EW, TM_EW // SUBLANES, seq // tm, t // tm
    per = tm // WRAP_ROWS
    nt_dims = (_DIMS["nt"], ((), ()))

    def body(x_ref, xp_ref, ug_ref, uv_ref, bug_ref, buv_ref, wg_ref, wv_ref, bg_ref, bv_ref,
             hg_ref, hv_ref, a_ref, pg_ref, pv_ref, gwrap_ref, vwrap_ref, cg_ref, cv_ref):
        keep = (pl.program_id(1) % seq_steps != 0).astype(F32)
        xb, xpb = x_ref[...].astype(BF16), xp_ref[...].astype(BF16)
        hg_ref[...] = lax.dot_general(xb, ug_ref[...], nt_dims, preferred_element_type=F32) + bug_ref[...]
        pg_ref[...] = lax.dot_general(xpb, ug_ref[...], nt_dims, preferred_element_type=F32) + bug_ref[...]
        for s, (tile, prev, kp, _, _) in enumerate(_sub_tiles(hg_ref, pg_ref, None, keep, None, n_sub)):
            _ffn_conv(tile, prev, gwrap_ref, wg_ref, bg_ref, kp, seg, _rows(cg_ref, s, TM_EW))
        hv_ref[...] = lax.dot_general(xb, uv_ref[...], nt_dims, preferred_element_type=F32) + buv_ref[...]
        pv_ref[...] = lax.dot_general(xpb, uv_ref[...], nt_dims, preferred_element_type=F32) + buv_ref[...]
        for s, (tile, prev, kp, _, _) in enumerate(_sub_tiles(hv_ref, pv_ref, None, keep, None, n_sub)):
            _ffn_conv(tile, prev, vwrap_ref, wv_ref, bv_ref, kp, seg, _rows(cv_ref, s, TM_EW))
        cg = cg_ref[...]
        a_ref[...] = (cg * _sigmoid(cg) * cv_ref[...]).astype(BF16)

    def blk(shift):
        return pl.BlockSpec((None, fb, d), lambda n, i: (n + shift, 0, 0))

    def vec(shift, rows):
        return pl.BlockSpec((None, None, rows, fb), lambda n, i: (l, n + shift, 0, 0))

    out = pl.BlockSpec((None, tm, fb), lambda n, i: (n, i, 0))
    tmp = pltpu.VMEM((tm, fb), F32)
    halo = pltpu.VMEM((WRAP_ROWS, fb), F32)
    return _call(body, name, (half, n_steps),
                 [pl.BlockSpec((tm, d), lambda n, i: (i, 0)),
                  pl.BlockSpec((WRAP_ROWS, d), lambda n, i: (jnp.maximum(i * per - 1, 0), 0)),
                  blk(0), blk(half), vec(0, 1), vec(half, 1), vec(0, FFN_K), vec(half, FFN_K), vec(0, 1), vec(half, 1)],
                 [out, out, out],
                 [_sds((half, t, fb), F32), _sds((half, t, fb), F32), _sds((half, t, fb), BF16)],
                 [halo, halo, halo, halo, tmp, tmp])(x, x, w_up, w_up, b_up4, b_up4, wdw, wdw, bdw, bdw)


def _ffn_act_bwd(name, dy, w_down, hg, hv, wdw, bdw, l, seq, deps=()):
    half, t, fb = hg.shape
    d = dy.shape[-1]
    tm = min(TM_FFN, seq)
    n_sub, seg, seq_steps, n_steps = tm // TM_EW, TM_EW // SUBLANES, seq // tm, t // tm
    tile, prev, _ = _ffn_specs(tm, fb, n_steps)

    def body(dy_ref, wd_ref, g_ref, gp_ref, v_ref, vp_ref, wg_ref, wv_ref, bg_ref, bv_ref,
             dg_ref, dv_ref, dbg_ref, dbv_ref, dwg_ref, dwv_ref, gwrap_ref, vwrap_ref, cg_ref, cv_ref):
        i = pl.program_id(1)
        first = i == 0
        keep = (i % seq_steps != 0).astype(F32)
        da = lax.dot_general(dy_ref[...].astype(BF16), wd_ref[...], (_DIMS["nt"], ((), ())),
                             preferred_element_type=F32)
        g_tiles = _sub_tiles(g_ref, gp_ref, None, keep, None, n_sub)
        v_tiles = _sub_tiles(v_ref, vp_ref, None, keep, None, n_sub)
        for s in range(n_sub):
            _ffn_conv(g_tiles[s][0], g_tiles[s][1], _rows(gwrap_ref, s, WRAP_ROWS), wg_ref, bg_ref, g_tiles[s][2],
                      seg, _rows(cg_ref, s, TM_EW))
            _ffn_conv(v_tiles[s][0], v_tiles[s][1], _rows(vwrap_ref, s, WRAP_ROWS), wv_ref, bv_ref, v_tiles[s][2],
                      seg, _rows(cv_ref, s, TM_EW))
        cg, cv = cg_ref[...], cv_ref[...]
        sg = _sigmoid(cg)
        dcv = da * cg * sg
        dcg = da * cv * sg * (1.0 + cg * (1.0 - sg))
        dg_ref[...] = dcg
        dv_ref[...] = dcv
        _acc_rows(dbg_ref, _colsum(dcg), first)
        _acc_rows(dbv_ref, _colsum(dcv), first)

        @pl.when(first)
        def _():
            dwg_ref[...] = jnp.zeros_like(dwg_ref)
            dwv_ref[...] = jnp.zeros_like(dwv_ref)

        for s in range(n_sub):
            _conv_bwd_taps(_rows(dg_ref, s, TM_EW), g_tiles[s][0], _rows(gwrap_ref, s, WRAP_ROWS), dwg_ref, seg, FFN_K)
            _conv_bwd_taps(_rows(dv_ref, s, TM_EW), v_tiles[s][0], _rows(vwrap_ref, s, WRAP_ROWS), dwv_ref, seg, FFN_K)

    def vec(shift, rows):
        return pl.BlockSpec((None, None, rows, fb), lambda n, i: (l, n + shift, 0, 0))

    def acc(rows):
        return pl.BlockSpec((None, rows, fb), lambda n, i: (n, 0, 0))

    wrap = pltpu.VMEM((n_sub * WRAP_ROWS, fb), F32)
    tmp = pltpu.VMEM((tm, fb), F32)
    return _call(body, name, (half, n_steps),
                 [pl.BlockSpec((tm, d), lambda n, i: (i, 0)), pl.BlockSpec((None, fb, d), lambda n, i: (n, 0, 0)),
                  tile, prev, tile, prev, vec(0, FFN_K), vec(half, FFN_K), vec(0, 1), vec(half, 1)],
                 [tile, tile, acc(1), acc(1), acc(SUBLANES), acc(SUBLANES)],
                 [_sds((half, t, fb), F32), _sds((half, t, fb), F32), _sds((half, 1, fb), F32),
                  _sds((half, 1, fb), F32), _sds((half, SUBLANES, fb), F32), _sds((half, SUBLANES, fb), F32)],
                 [wrap, wrap, tmp, tmp], deps=deps)(dy, w_down, hg, hg, hv, hv, wdw, wdw, bdw, bdw)


def _ffn_conv_t_dx(name, dcg, dcv, wdw, w_up, res, xhat, rstd, g3, l, seq, deps=()):
    half, t, fb = dcg.shape
    nb, d = 2 * half, res.shape[-1]
    tm = min(TM_FFN, seq)
    n_sub, seg, seq_steps, n_steps = tm // TM_EW, TM_EW // SUBLANES, seq // tm, t // tm
    per = tm // WRAP_ROWS

    pair = 2
    n_pairs, half_pairs = nb // pair, half // pair

    def body(g_ref, gn_ref, v_ref, vn_ref, w_ref, up_ref, res_ref, xh_ref, rs_ref, gam_ref,
             dh_ref, db_ref, dr_ref, dgam_ref, dbeta_ref, dsum_ref, wrap_ref, out_ref, acc_ref):
        i, m = pl.program_id(0), pl.program_id(1)
        keep = (i % seq_steps != seq_steps - 1).astype(F32)

        def conv_t(d_ref, dn_ref, b):
            for s, (sub, _, _, nx, kn) in enumerate(_sub_tiles(d_ref.at[b], None, dn_ref.at[b], None, keep, n_sub)):
                _fill_wrap_next(sub, nx, wrap_ref, FFN_K - 1, kn)
                _conv_bwd_data(sub, wrap_ref, w_ref.at[b], _rows(out_ref, s, TM_EW), seg, FFN_K)

        p = None
        for b in range(pair):
            @pl.when(m < half_pairs)
            def _(b=b):
                conv_t(g_ref, gn_ref, b)

            @pl.when(m >= half_pairs)
            def _(b=b):
                conv_t(v_ref, vn_ref, b)

            dh = out_ref[...]
            dhb = dh.astype(BF16)
            dh_ref[b] = dhb
            _acc_rows(db_ref.at[pair * m + b], _colsum(dh), i == 0)
            part = jnp.dot(dhb, up_ref[b], preferred_element_type=F32)
            p = part if p is None else p + part

        @pl.when(m == 0)
        def _():
            acc_ref[...] = p

        @pl.when(m > 0)
        def _():
            acc_ref[...] += p

        @pl.when(m == n_pairs - 1)
        def _():
            _ln_bwd_rows(acc_ref[...] + ALPHA * res_ref[...], xh_ref, rs_ref, gam_ref, dr_ref, dgam_ref, dbeta_ref,
                         dsum_ref, i == 0)

    def src(gate):
        def blk(m):
            return jnp.minimum(m, half_pairs - 1) if gate else jnp.maximum(m - half_pairs, 0)
        tile = pl.BlockSpec((pair, tm, fb), lambda i, m: (blk(m), i, 0))
        nxt = pl.BlockSpec((pair, WRAP_ROWS, fb),
                           lambda i, m: (blk(m), jnp.minimum((i + 1) * per, n_steps * per - 1), 0))
        return [tile, nxt]

    row = pl.BlockSpec((tm, d), lambda i, m: (i, 0))
    ln_ins, ln_outs = _ln_bwd_specs(tm, d, l, lambda i, m: (i, 0))
    tmp = pltpu.VMEM((tm, fb), F32)
    halo = pltpu.VMEM((WRAP_ROWS, fb), F32)
    return _call(body, name, (n_steps, n_pairs),
                 src(True) + src(False) +
                 [pl.BlockSpec((None, pair, FFN_K, fb), lambda i, m: (l, m, 0, 0)),
                  pl.BlockSpec((pair, fb, d), lambda i, m: (m, 0, 0)), row] + ln_ins,
                 [pl.BlockSpec((pair, tm, fb), lambda i, m: (m, i, 0)),
                  pl.BlockSpec((nb, 1, fb), lambda i, m: (0, 0, 0))] + ln_outs,
                 [_sds((nb, t, fb), BF16), _sds((nb, 1, fb), F32), _sds((t, d), F32)] + [_sds((1, d), F32)] * 3,
                 [halo, tmp, pltpu.VMEM((tm, d), F32)],
                 deps=deps)(dcg, dcg, dcv, dcv, wdw, w_up, res, xhat, rstd, g3)


def _gelu_parts(h):
    cdf = 0.5 * (1.0 + lax.erf(h * INV_SQRT2))
    return h * cdf, cdf


def _seg_axis(a, axis, fn):
    return jnp.moveaxis(fn(jnp.moveaxis(a, axis, 0), TM_EW), 0, axis)


def _sgu_operands(w_s, b_s):
    nl = w_s.shape[0]
    n_sub = TM_EW // CHUNK
    tril = jnp.tril(jnp.ones((CHUNK, CHUNK), dtype=bool))
    w_causal = jnp.where(tril, w_s, 0.0)
    w_tile = (jnp.eye(n_sub, dtype=F32)[None, None, :, None, :, None] * w_causal[:, :, None, :, None, :]).reshape(
        nl, GROUPS, TM_EW, TM_EW)
    w_tile = _seg_axis(_seg_axis(w_tile, 2, _to_segments), 3, _to_segments).astype(BF16)
    bs_tile = jnp.broadcast_to(b_s[:, :, None, :, None], (nl, GROUPS, n_sub, CHUNK, CHUNK)).reshape(
        nl, GROUPS, TM_EW, CHUNK)
    return w_tile, _seg_axis(bs_tile, 2, _to_segments)


def _sgu_param_grads(dwt, dbt):
    n_sub = TM_EW // CHUNK
    tril = jnp.tril(jnp.ones((CHUNK, CHUNK), dtype=bool))
    dwt = _seg_axis(_seg_axis(dwt, 1, _from_segments), 2, _from_segments).reshape(GROUPS, n_sub, CHUNK, n_sub, CHUNK)
    dw = sum(dwt[:, a, :, a, :] for a in range(n_sub))
    db = _seg_axis(dbt, 1, _from_segments).reshape(GROUPS, n_sub, CHUNK).sum(axis=1)
    return jnp.where(tril, dw, 0.0), db


def _sgu(name, h, g3, b3, wt, bst, l):
    t, c2 = h.shape
    c = c2 // 2
    tm = TM_EW

    def body(h_ref, g_ref, b_ref, wt_ref, bs_ref, o_ref):
        z, _ = _gelu_parts(h_ref[...])
        u = z[:, :c]
        xhat, _ = _ln_stats(z[:, c:])
        vnb = (xhat * g_ref[...] + b_ref[...]).astype(BF16)
        for gi in range(GROUPS):
            cs = slice(gi * CHUNK, (gi + 1) * CHUNK)
            sp = jnp.dot(wt_ref[gi], vnb[:, cs], preferred_element_type=F32) + bs_ref[gi]
            o_ref[:, cs] = (u[:, cs] * sp).astype(BF16)

    return _call(body, name, (t // tm,),
                 [_row_spec(tm, c2), _param_spec(l, c), _param_spec(l, c),
                  pl.BlockSpec((None, GROUPS, tm, tm), lambda i: (l, 0, 0, 0)),
                  pl.BlockSpec((None, GROUPS, tm, CHUNK), lambda i: (l, 0, 0, 0))],
                 _row_spec(tm, c), _sds((t, c), BF16))(h, g3, b3, wt, bst)


def _sgu_bwd(name, dq, h, g3, b3, wt, bst, l, deps=()):
    t, c2 = h.shape
    c = c2 // 2
    tm = TM_EW
    n_tiles = t // tm

    def body(dq_ref, h_ref, g_ref, b_ref, wt_ref, bs_ref,
             dh_ref, dbin_ref, dw_ref, dbs_ref, dg_ref, db_ref, du_ref, dvn_ref, bsum_ref):
        i = pl.program_id(0)
        first = i == 0
        hv = h_ref[...]
        z, cdf = _gelu_parts(hv)
        u = z[:, :c]
        xhat, rstd = _ln_stats(z[:, c:])
        g = g_ref[...]
        vnb = (xhat * g + b_ref[...]).astype(BF16)

        @pl.when(first)
        def _():
            dw_ref[...] = jnp.zeros_like(dw_ref)
            bsum_ref[...] = jnp.zeros_like(bsum_ref)

        for gi in range(GROUPS):
            cs = slice(gi * CHUNK, (gi + 1) * CHUNK)
            vb = vnb[:, cs]
            w = wt_ref[gi]
            sp = jnp.dot(w, vb, preferred_element_type=F32) + bs_ref[gi]
            dqb = dq_ref[:, cs]
            du_ref[:, cs] = dqb * sp
            dsp = dqb * u[:, cs]
            bsum_ref[gi] += dsp
            dspb = dsp.astype(BF16)
            dw_ref[gi] += lax.dot_general(dspb, vb, (_DIMS["nt"], ((), ())), preferred_element_type=F32)
            dvn_ref[:, cs] = lax.dot_general(w, dspb, (_DIMS["tn"], ((), ())), preferred_element_type=F32)

        dvn = dvn_ref[...]
        dv = _ln_backward(dvn * g, xhat, rstd)
        pdf = jnp.exp(-0.5 * hv * hv) * INV_SQRT2PI
        dgelu = cdf + hv * pdf
        dhu = du_ref[...] * dgelu[:, :c]
        dhv = dv * dgelu[:, c:]
        dh_ref[:, :c] = dhu.astype(BF16)
        dh_ref[:, c:] = dhv.astype(BF16)
        _acc_rows(dbin_ref.at[:, :c], _colsum(dhu), first)
        _acc_rows(dbin_ref.at[:, c:], _colsum(dhv), first)
        _acc_rows(dg_ref, _colsum(dvn * xhat), first)
        _acc_rows(db_ref, _colsum(dvn), first)

        @pl.when(i == n_tiles - 1)
        def _():
            dbs_ref[...] = jnp.sum(bsum_ref[...], axis=-1)

    vec = pl.BlockSpec((1, c), lambda i: (0, 0))
    return _call(body, name, (n_tiles,),
                 [_row_spec(tm, c), _row_spec(tm, c2), _param_spec(l, c), _param_spec(l, c),
                  pl.BlockSpec((None, GROUPS, tm, tm), lambda i: (l, 0, 0, 0)),
                  pl.BlockSpec((None, GROUPS, tm, CHUNK), lambda i: (l, 0, 0, 0))],
                 [_row_spec(tm, c2), pl.BlockSpec((1, c2), lambda i: (0, 0)),
                  pl.BlockSpec((GROUPS, tm, tm), lambda i: (0, 0, 0)),
                  pl.BlockSpec((GROUPS, tm), lambda i: (0, 0)), vec, vec],
                 [_sds((t, c2), BF16), _sds((1, c2), F32), _sds((GROUPS, tm, tm), F32),
                  _sds((GROUPS, tm), F32), _sds((1, c), F32), _sds((1, c), F32)],
                 [pltpu.VMEM((tm, c), F32), pltpu.VMEM((tm, c), F32), pltpu.VMEM((GROUPS, tm, CHUNK), F32)],
                 deps=deps)(dq, h, g3, b3, wt, bst)


def _loss(name, y, target):
    t, d = y.shape
    tm = min(TM_ROW, t)
    n_tiles = t // tm

    def body(y_ref, t_ref, l_ref, dy_ref, acc_ref):
        i = pl.program_id(0)
        diff = y_ref[...] - t_ref[...]
        dy_ref[...] = diff * (1.0 / d)
        _acc_rows(acc_ref, _colsum(diff * diff), i == 0)

        @pl.when(i == n_tiles - 1)
        def _():
            l_ref[...] = jnp.broadcast_to(jnp.sum(acc_ref[...], axis=-1, keepdims=True) * (0.5 / d), (1, LANES))

    return _call(body, name, (n_tiles,), [_row_spec(tm, d), _row_spec(tm, d)],
                 [pl.BlockSpec((1, LANES), lambda i: (0, 0)), _row_spec(tm, d)],
                 [_sds((1, LANES), F32), _sds((t, d), F32)], [pltpu.VMEM((1, d), F32)])(y, target)


def _adamw(g, w, m, v):
    m2 = ADAM_B1 * m + (1.0 - ADAM_B1) * g
    v2 = ADAM_B2 * v + (1.0 - ADAM_B2) * (g * g)
    m_hat = m2 / (1.0 - ADAM_B1 ** ADAM_STEP)
    v_hat = v2 / (1.0 - ADAM_B2 ** ADAM_STEP)
    delta = -ADAM_LR * (m_hat / (jnp.sqrt(v_hat) + ADAM_EPS) + ADAM_WD * w)
    return delta, m2, v2


ROW_TILE_CAP = 512


def _row_tile(rows, cap=ROW_TILE_CAP):
    if rows <= cap:
        return rows
    for tr in range(cap, 15, -16):
        if rows % tr == 0:
            return tr
    return rows


def _sum8_adamw(name, dev, lands, parts, w, m, v):
    nl = len(lands)
    _, r, c = lands[0].shape
    tr = _row_tile(r, cap=256)

    def body(dev_ref, *refs):
        land, own = refs[:nl], refs[nl:2 * nl]
        w_ref, m_ref, v_ref, g_ref, d_ref, m2_ref, v2_ref = refs[2 * nl:]
        layer, me = pl.program_id(0), dev_ref[0]
        for l in range(nl):
            @pl.when(layer == l)
            def _(l=l):
                g = None
                for s in range(N_DEV):
                    part = jnp.where(me == s, own[l][...], land[l][s]).astype(F32)
                    g = part if g is None else g + part
                delta, m2, v2 = _adamw(g, w_ref[...], m_ref[...], v_ref[...])
                g_ref[...] = g
                d_ref[...] = delta
                m2_ref[...] = m2
                v2_ref[...] = v2

    def rows_of(l, a, i):
        return jnp.where(a == l, i, 0)

    spec = pl.BlockSpec((None, tr, c), lambda a, i, dev_ref: (a, i, 0))
    in_specs = [pl.BlockSpec((N_DEV, tr, c), lambda a, i, dev_ref, l=l: (0, rows_of(l, a, i), 0)) for l in range(nl)]
    in_specs += [pl.BlockSpec((None, tr, c), lambda a, i, dev_ref, l=l: (dev_ref[0], rows_of(l, a, i), 0))
                 for l in range(nl)]
    grid_spec = pltpu.PrefetchScalarGridSpec(
        num_scalar_prefetch=1, grid=(nl, r // tr), in_specs=in_specs + [spec] * 3, out_specs=[spec] * 4)
    return pl.pallas_call(
        body, name=name, grid_spec=grid_spec, out_shape=[_sds(w.shape, F32)] * 4,
        compiler_params=pltpu.CompilerParams(vmem_limit_bytes=VMEM_LIMIT))(dev, *lands, *parts, w, m, v)


def _sum8(name, parts):
    _, r, c = parts.shape
    tr = _row_tile(r)

    def body(p_ref, o_ref):
        acc = p_ref[0]
        for s in range(1, N_DEV):
            acc = acc + p_ref[s]
        o_ref[...] = acc

    return _call(body, name, (r // tr,), [pl.BlockSpec((N_DEV, tr, c), lambda i: (0, i, 0))],
                 pl.BlockSpec((tr, c), lambda i: (i, 0)), _sds((r, c), F32))(parts)


def _adamw_small(name, gs, ws, ms, vs):
    n = len(gs)

    def body(*refs):
        g, w, m, v = (refs[k * n:(k + 1) * n] for k in range(4))
        d_out, m_out, v_out = (refs[(4 + k) * n:(5 + k) * n] for k in range(3))
        for k in range(n):
            d_out[k][...], m_out[k][...], v_out[k][...] = _adamw(g[k][...], w[k][...], m[k][...], v[k][...])

    vmem = pl.BlockSpec(memory_space=pltpu.VMEM)
    outs = _call(body, name, (), [vmem] * (4 * n), [vmem] * (3 * n), [_sds(w.shape, F32) for w in ws] * 3)(
        *gs, *ws, *ms, *vs)
    return outs[:n], outs[n:2 * n], outs[2 * n:]


def _pack(arrs, row_multiple=SUBLANES):
    pieces, rows = [], 0
    for a in arrs:
        piece = a.reshape(-1, LANES)
        piece = jnp.pad(piece, ((0, (-piece.shape[0]) % SUBLANES), (0, 0)))
        pieces.append(piece)
        rows += piece.shape[0]
    if rows % row_multiple:
        pieces.append(jnp.zeros(((-rows) % row_multiple, LANES), pieces[0].dtype))
    return jnp.concatenate(pieces, axis=0)


def _unpack(buf, shapes, lead=0):
    out, pos = [], 0
    for shp in shapes:
        rows = math.prod(shp) // LANES
        piece = lax.slice_in_dim(buf, pos, pos + rows, axis=lead)
        out.append(piece.reshape(buf.shape[:lead] + tuple(shp)))
        pos += rows + (-rows) % SUBLANES
    return out


REPLICATED = ["conv_b_in", "conv_b_dw", "conv_ln_g", "conv_ln_b", "conv_b_out", "gmlp_w_s", "gmlp_b_s",
              "ffn_b_up", "ffn_b_dw", "ffn_b_down", "norm1_g", "norm1_b", "norm2_g", "norm2_b"]
SMALL_SHARDED = ["conv_w_dw", "gmlp_b_in", "gmlp_ln_g", "gmlp_ln_b", "gmlp_b_out", "ffn_w_dw"]
BIG = ["conv_w_in", "conv_w_out", "gmlp_w_in", "gmlp_w_out", "ffn_w_up", "ffn_w_down"]
WEIGHTS = ["conv_w_in", "conv_b_in", "conv_w_dw", "conv_b_dw", "conv_ln_g", "conv_ln_b", "conv_w_out", "conv_b_out",
           "gmlp_w_in", "gmlp_b_in", "gmlp_ln_g", "gmlp_ln_b", "gmlp_w_s", "gmlp_b_s", "gmlp_w_out", "gmlp_b_out",
           "ffn_w_up", "ffn_b_up", "ffn_w_dw", "ffn_b_dw", "ffn_w_down", "ffn_b_down",
           "norm1_g", "norm1_b", "norm2_g", "norm2_b"]


def _from_shards(g, lead_shape):
    nd = len(lead_shape)
    perm = tuple(range(1, nd + 1)) + (0, nd + 1)
    return g.transpose(perm).reshape(tuple(lead_shape) + (-1,))


def _to_shards(full, width):
    lead = full.shape[:-1]
    nd = len(lead)
    parts = full.reshape(lead + (N_DEV, width))
    return parts.transpose((nd,) + tuple(range(nd)) + (nd + 1,))


def _step(p):
    x_in, target_in = p["x"], p["loss_target"]
    bsz, seq, d = x_in.shape
    t = bsz * seq
    assert seq % TM_EW == 0 and TM_EW % CHUNK == 0 and TM_EW // SUBLANES >= CONV_K - 1
    x0 = _to_segments(x_in.reshape(t, d), TM_EW)
    target = _to_segments(target_in.reshape(t, d), TM_EW)
    n_conv, n_gmlp = p["conv_w_in"].shape[0], p["gmlp_w_in"].shape[0]
    fb = p["ffn_w_up"].shape[-1]
    nblk = N_DEV
    half = nblk // 2
    cw = p["conv_w_in"].shape[-1]
    tm = min(TM_MM, t)
    nt = t // tm
    tk = min(TK_DW, t)
    ntk = t // tk
    dev = 4 * lax.axis_index("x") + 2 * lax.axis_index("y") + lax.axis_index("c")

    small_shapes = [p[n].shape for n in SMALL_SHARDED]
    w_src = [_pack([p[n] for n in SMALL_SHARDED])]
    for i in range(DEPTH):
        mix = "conv" if i % 2 == 0 else "gmlp"
        w_src += [p[mix + "_w_in"][i // 2].astype(BF16), p[mix + "_w_out"][i // 2].astype(BF16),
                  p["ffn_w_up"][i].T.astype(BF16), p["ffn_w_down"][i].astype(BF16)]
    send_sems, recv_sems, w_land, _ = _gather_start("weights_gather_start", _place_own("weights_place_own", w_src))
    W_IN, W_OUT, W_UP, W_DOWN = range(4)

    def wait_weight(i, k, after):
        return _gather_wait(f"l{i}_weights_wait{k}", w_land[1 + 4 * i + k], send_sems, recv_sems, 1 + 4 * i + k, after)

    small_all = _gather_wait("small_weights_wait", w_land[0], send_sems, recv_sems, 0, x0)
    sm = _unpack(small_all, small_shapes, lead=1)
    conv_w_dw = _from_shards(sm[0], sm[0].shape[1:-1])
    gmlp_b_in = _from_shards(sm[1], sm[1].shape[1:-1])
    gmlp_ln_g = _from_shards(sm[2], sm[2].shape[1:-1])
    gmlp_ln_b = _from_shards(sm[3], sm[3].shape[1:-1])
    gmlp_b_out = _from_shards(sm[4], sm[4].shape[1:-1])
    ffn_w_dw = sm[5].transpose(1, 0, 2, 3)

    def rows3(a):
        return a.reshape(a.shape[0], 1, a.shape[-1])

    conv_b_in4 = p["conv_b_in"].reshape(n_conv, N_DEV, 1, cw)
    gmlp_b_in4 = gmlp_b_in.reshape(n_gmlp, N_DEV, 1, cw)
    ffn_b_up4 = p["ffn_b_up"].reshape(DEPTH, nblk, 1, fb)
    ffn_b_dw4 = p["ffn_b_dw"].reshape(DEPTH, nblk, 1, fb)
    conv_b_dw3, conv_ln_g3, conv_ln_b3 = rows3(p["conv_b_dw"]), rows3(p["conv_ln_g"]), rows3(p["conv_ln_b"])
    conv_b_out3, gmlp_b_out3, ffn_b_down3 = rows3(p["conv_b_out"]), rows3(gmlp_b_out), rows3(p["ffn_b_down"])
    gmlp_ln_g3, gmlp_ln_b3 = rows3(gmlp_ln_g), rows3(gmlp_ln_b)
    n1g3, n1b3, n2g3, n2b3 = rows3(p["norm1_g"]), rows3(p["norm1_b"]), rows3(p["norm2_g"]), rows3(p["norm2_b"])
    w_tile, bs_tile = _sgu_operands(p["gmlp_w_s"], p["gmlp_b_s"])

    def mm_in(name, xa, wg, l, bias4, glu=False):
        tmi = min(TM_LN, t)
        c_half = half * cw

        def body(a_ref, b_ref, bias_ref, h_ref, *u_ref):
            xb = a_ref[...].astype(BF16)
            for n in range(N_DEV):
                h_ref[:, n * cw:(n + 1) * cw] = jnp.dot(xb, b_ref[n], preferred_element_type=F32) + bias_ref[n]
            if glu:
                u_ref[0][...] = h_ref[:, :c_half] * _sigmoid(h_ref[:, c_half:])

        outs = _call(body, name, (t // tmi,),
                     [pl.BlockSpec((tmi, d), lambda i: (i, 0)), pl.BlockSpec((N_DEV, d, cw), lambda i: (0, 0, 0)),
                      pl.BlockSpec((None, N_DEV, 1, cw), lambda i: (l, 0, 0, 0))],
                     [pl.BlockSpec((tmi, N_DEV * cw), lambda i: (i, 0))]
                     + ([pl.BlockSpec((tmi, c_half), lambda i: (i, 0))] if glu else []),
                     [_sds((t, N_DEV * cw), F32)] + ([_sds((t, c_half), F32)] if glu else []))(xa, wg, bias4)
        return outs if glu else outs[0]

    def mm_out_dx(name, dy, w, deps=()):
        return _matmul(name, dy, w, "nt", grid=(nt,),
                       a_spec=pl.BlockSpec((tm, d), lambda i: (i, 0)),
                       b_spec=pl.BlockSpec((d, d), lambda i: (0, 0)),
                       o_spec=pl.BlockSpec((tm, d), lambda i: (i, 0)), o_shape=(t, d), o_dtype=F32, deps=deps)

    def mm_out_dw(name, sa, dy):
        return _matmul(name, sa, dy, "tn", grid=(nt,), k_axis=0, nk=nt, acc_shape=(d, d),
                       a_spec=pl.BlockSpec((tm, d), lambda k: (k, 0)),
                       b_spec=pl.BlockSpec((tm, d), lambda k: (k, 0)),
                       o_spec=pl.BlockSpec((d, d), lambda k: (0, 0)), o_shape=(d, d), o_dtype=BF16)

    def mm_in_dx(name, dh, wg, res, norm=None):
        tmx = min(TM_LN, t)

        def body(a_ref, b_ref, res_ref, *refs):
            y = ALPHA * res_ref[...]
            for n in range(N_DEV):
                y = y + lax.dot_general(a_ref[:, n * cw:(n + 1) * cw], b_ref[n], (_DIMS["nt"], ((), ())),
                                        preferred_element_type=F32)
            if norm is None:
                refs[0][...] = y
            else:
                _ln_bwd_rows(y, *refs[:-1], pl.program_id(0) == 0, refs[-1])

        row = pl.BlockSpec((tmx, d), lambda i: (i, 0))
        ins = [pl.BlockSpec((tmx, N_DEV * cw), lambda i: (i, 0)), pl.BlockSpec((N_DEV, d, cw), lambda i: (0, 0, 0)), row]
        if norm is None:
            return _call(body, name, (t // tmx,), ins, row, _sds((t, d), F32))(dh, wg, res)
        ln_ins, ln_outs = _ln_bwd_specs(tmx, d, norm[3], lambda i: (i, 0))
        return _call(body, name, (t // tmx,), ins + ln_ins, ln_outs + [row],
                     [_sds((t, d), F32)] + [_sds((1, d), F32)] * 3 + [_sds((t, d), BF16)])(dh, wg, res, *norm[:3])

    def mm_in_dw(name, xa, dh):
        def body(a_ref, b_ref, o_ref, acc_ref):
            k = pl.program_id(1)
            p = lax.dot_general(a_ref[...].astype(BF16), b_ref[...], (_DIMS["tn"], ((), ())),
                                preferred_element_type=F32)
            _acc_rows(acc_ref, p, k == 0)

            @pl.when(k == nt - 1)
            def _():
                for n in range(half):
                    o_ref[n] = acc_ref[:, n * cw:(n + 1) * cw].astype(BF16)

        return _call(body, name, (2, nt),
                     [pl.BlockSpec((tm, d), lambda c, k: (k, 0)), pl.BlockSpec((tm, half * cw), lambda c, k: (k, c))],
                     pl.BlockSpec((half, d, cw), lambda c, k: (c, 0, 0)), _sds((N_DEV, d, cw), BF16),
                     [pltpu.VMEM((d, half * cw), F32)])(xa, dh)

    def mm_down_dw(name, a, dy, deps=()):
        return _matmul(name, a, dy, "tn", grid=(half, ntk), k_axis=1, nk=ntk, acc_shape=(fb, d),
                       a_spec=pl.BlockSpec((None, tk, fb), lambda n, k: (n, k, 0)),
                       b_spec=pl.BlockSpec((tk, d), lambda n, k: (k, 0)),
                       o_spec=pl.BlockSpec((None, fb, d), lambda n, k: (n, 0, 0)),
                       o_shape=(half, fb, d), o_dtype=BF16, deps=deps)

    def mm_up_dw(name, xa, dh):
        return _matmul(name, dh, xa, "tn", grid=(nblk, ntk), k_axis=1, nk=ntk, acc_shape=(fb, d),
                       a_spec=pl.BlockSpec((None, tk, fb), lambda n, k: (n, k, 0)),
                       b_spec=pl.BlockSpec((tk, d), lambda n, k: (k, 0)),
                       o_spec=pl.BlockSpec((None, fb, d), lambda n, k: (n, 0, 0)),
                       o_shape=(nblk, fb, d), o_dtype=BF16)

    saved = []
    xcur = xcur_b = x0
    for i in range(DEPTH):
        j = i // 2
        s = {"x": xcur_b}
        s["w_in"] = wait_weight(i, W_IN, xcur if i else target)
        if i % 2 == 0:
            s["h"], s["u"] = mm_in(f"l{i}_conv_in_glu", xcur_b, s["w_in"], j, conv_b_in4, glu=True)
            s["c"] = _dwconv31(f"l{i}_dwconv", s["u"], conv_w_dw, conv_b_dw3, j, seq)
            s["s"] = _ln_silu(f"l{i}_ln_silu", s["c"], conv_ln_g3, conv_ln_b3, j)
            b_out3 = conv_b_out3
        else:
            s["h"] = mm_in(f"l{i}_gmlp_in", xcur_b, s["w_in"], j, gmlp_b_in4)
            s["s"] = _sgu(f"l{i}_sgu", s["h"], gmlp_ln_g3, gmlp_ln_b3, w_tile, bs_tile, j)
            b_out3 = gmlp_b_out3
        s["w_out"] = wait_weight(i, W_OUT, s["s"]).reshape(d, d)
        s["x1"], s["x1b"], s["xhat1"], s["rstd1"] = _matmul_ln(
            f"l{i}_mixer_out_norm1", s["s"], s["w_out"], xcur, b_out3, n1g3, n1b3, j, i)
        s["w_up"] = wait_weight(i, W_UP, s["x1"])
        s["hg"], s["hv"], s["a"] = _ffn_up_act(f"l{i}_ffn_up_act", s["x1b"], s["w_up"], ffn_b_up4, ffn_w_dw,
                                               ffn_b_dw4, i, seq)
        s["w_down"] = wait_weight(i, W_DOWN, s["a"]).reshape(half, fb, d)
        xcur, xcur_b, s["xhat2"], s["rstd2"] = _matmul_ln(
            f"l{i}_ffn_down_norm2", s["a"], s["w_down"], s["x1"], ffn_b_down3, n2g3, n2b3, i, i)
        saved.append(s)

    loss_row, dx = _loss("loss", xcur, target)

    started = {n: [None] * p[n].shape[0] for n in BIG}
    tokens = []

    def send_grads(name, items):
        done, token = _scatter_start(name, [g for _, _, g in items])
        for (n, l, _), st in zip(items, done):
            started[n][l] = st
        tokens.append(token)

    def take_tokens():
        out = list(tokens)
        tokens.clear()
        return out

    gl = {n: [None] * p[n].shape[0] for n in REPLICATED + SMALL_SHARDED}
    dr2, gl["norm2_g"][DEPTH - 1], gl["norm2_b"][DEPTH - 1], gl["ffn_b_down"][DEPTH - 1], dr2b = _ln_res_bwd(
        f"l{DEPTH - 1}_norm2_bwd", dx, saved[-1]["xhat2"], saved[-1]["rstd2"], n2g3, DEPTH - 1)
    for i in reversed(range(DEPTH)):
        j = i // 2
        s = saved[i]
        mix = "conv" if i % 2 == 0 else "gmlp"
        g_down = mm_down_dw(f"l{i}_ffn_down_dw", s["a"], dr2b, deps=take_tokens()).reshape(N_DEV, -1, d)
        send_grads(f"l{i}_ffn_down_grad_scatter_start", [("ffn_w_down", i, g_down)])
        dcg, dcv, dbg, dbv, dwg, dwv = _ffn_act_bwd(f"l{i}_ffn_act_bwd", dr2b, s["w_down"], s["hg"], s["hv"],
                                                    ffn_w_dw, ffn_b_dw4, i, seq, deps=take_tokens())
        gl["ffn_b_dw"][i] = jnp.concatenate([dbg, dbv], axis=0).reshape(1, nblk * fb)
        gl["ffn_w_dw"][i] = jnp.concatenate([dwg[:, :FFN_K], dwv[:, :FFN_K]], axis=0)
        dh, dbu, dr1, gl["norm1_g"][i], gl["norm1_b"][i], gl[mix + "_b_out"][j] = _ffn_conv_t_dx(
            f"l{i}_ffn_conv_t_dx", dcg, dcv, ffn_w_dw, s["w_up"], dr2, s["xhat1"], s["rstd1"], n1g3, i, seq,
            deps=take_tokens())
        gl["ffn_b_up"][i] = dbu.reshape(1, nblk * fb)
        send_grads(f"l{i}_ffn_up_grad_scatter_start", [("ffn_w_up", i, mm_up_dw(f"l{i}_ffn_up_dw", s["x1b"], dh))])
        ds = mm_out_dx(f"l{i}_{mix}_out_dx", dr1, s["w_out"], deps=take_tokens())
        g_out = mm_out_dw(f"l{i}_{mix}_out_dw", s["s"], dr1).reshape(N_DEV, -1, d)
        if i % 2 == 0:
            dc, gl["conv_ln_g"][j], gl["conv_ln_b"][j], gl["conv_b_dw"][j] = _ln_silu_bwd(
                f"l{i}_ln_silu_bwd", ds, s["c"], conv_ln_g3, conv_ln_b3, j)
            du, dwdw = _dwconv31_bwd(f"l{i}_dwconv_bwd", dc, s["u"], conv_w_dw, j, seq)
            gl["conv_w_dw"][j] = dwdw[:CONV_K]
            dh, gl["conv_b_in"][j] = _glu_bwd(f"l{i}_glu_bwd", du, s["h"])
        else:
            dh, gl["gmlp_b_in"][j], dwt, dbt, gl["gmlp_ln_g"][j], gl["gmlp_ln_b"][j] = _sgu_bwd(
                f"l{i}_sgu_bwd", ds, s["h"], gmlp_ln_g3, gmlp_ln_b3, w_tile, bs_tile, j)
            gl["gmlp_w_s"][j], gl["gmlp_b_s"][j] = _sgu_param_grads(dwt, dbt)
            if i == 1:
                ws_local = jnp.stack(gl["gmlp_w_s"]).reshape(-1, LANES)
                ws_send, ws_recv, ws_land, ws_token = _gather_start(
                    "w_s_grads_gather_start", _place_own("w_s_grads_place_own", [ws_local]))
                tokens.append(ws_token)
        if i > 0:
            prev = saved[i - 1]
            dr2, gl["norm2_g"][i - 1], gl["norm2_b"][i - 1], gl["ffn_b_down"][i - 1], dr2b = mm_in_dx(
                f"l{i}_{mix}_in_dx_norm2_bwd", dh, s["w_in"], dr1, (prev["xhat2"], prev["rstd2"], n2g3, i - 1))
        else:
            dx = mm_in_dx(f"l{i}_{mix}_in_dx", dh, s["w_in"], dr1)
        send_grads(f"l{i}_mixer_grads_scatter_start",
                   [(mix + "_w_out", j, g_out), (mix + "_w_in", j, mm_in_dw(f"l{i}_{mix}_in_dw", s["x"], dh))])
    grad_x = _from_segments(dx, TM_EW).reshape(bsz, seq, d)

    late = [n for n in REPLICATED if n != "gmlp_w_s"]
    full_small = {n: jnp.stack(gl[n]).reshape(p[n].shape) for n in late}
    shard_small = {}
    for n in SMALL_SHARDED:
        if n == "ffn_w_dw":
            shard_small[n] = jnp.stack(gl[n]).transpose(1, 0, 2, 3)
        else:
            width = p[n].shape[-1]
            lead = p[n].shape[:-1]
            shard_small[n] = _to_shards(jnp.stack(gl[n]).reshape(lead + (N_DEV * width,)), width)
    flat_shapes = [(1, LANES)] + [p[n].shape for n in late] + [(N_DEV,) + p[n].shape for n in SMALL_SHARDED]
    flat_local = _pack([loss_row] + [full_small[n] for n in late] + [shard_small[n] for n in SMALL_SHARDED],
                       row_multiple=ROW_TILE_CAP)

    small_send, small_recv, small_land, small_token = _gather_start(
        "small_grads_gather_start", _place_own("small_grads_place_own", [flat_local]))

    grads, delta, new_m, new_v = {}, {}, {}, {}
    dev1 = jnp.reshape(dev, (1,)).astype(jnp.int32)
    order = ["ffn_w_down", "ffn_w_up", "gmlp_w_out", "gmlp_w_in", "conv_w_out", "conv_w_in"]
    after = small_token
    for n in order:
        parts_done, lands_done = _scatter_wait(f"grads_{n}_scatter_wait", started[n], after)
        state = [p[n], p["m_" + n], p["v_" + n]]
        if n == "ffn_w_up":
            state = [a.transpose(0, 2, 1) for a in state]
        outs = _sum8_adamw(f"adamw_{n}", dev1, lands_done, parts_done, *state)
        after = outs[-1]
        if n == "ffn_w_up":
            outs = [a.transpose(0, 2, 1) for a in outs]
        grads[n], delta[n], new_m[n], new_v[n] = outs

    ws_parts = _gather_wait("w_s_grads_gather_wait", ws_land[0], ws_send, ws_recv, 0, after)
    ws_sum = _sum8("sum_w_s_grads", ws_parts)
    grads["gmlp_w_s"] = ws_sum.reshape(p["gmlp_w_s"].shape)
    small_parts = _gather_wait("small_grads_gather_wait", small_land[0], small_send, small_recv, 0, ws_sum)
    summed = _unpack(_sum8("sum_small_grads", small_parts), flat_shapes)
    loss = summed[0][0, 0]
    grads.update(zip(late, summed[1:1 + len(late)]))
    for n, g in zip(SMALL_SHARDED, summed[1 + len(late):]):
        grads[n] = lax.dynamic_index_in_dim(g, dev, axis=0, keepdims=False)
    small = REPLICATED + SMALL_SHARDED
    d_s, m_s, v_s = _adamw_small("adamw_small", [grads[n] for n in small], [p[n] for n in small],
                                 [p["m_" + n] for n in small], [p["v_" + n] for n in small])
    for n, dd, mm, vv in zip(small, d_s, m_s, v_s):
        delta[n], new_m[n], new_v[n] = dd, mm, vv

    return (loss, grad_x, *[grads[n] for n in WEIGHTS], *[delta[n] for n in WEIGHTS],
            *[new_m[n] for n in WEIGHTS], *[new_v[n] for n in WEIGHTS])


def kernel(x, conv_w_in, conv_b_in, conv_w_dw, conv_b_dw, conv_ln_g, conv_ln_b, conv_w_out, conv_b_out, gmlp_w_in, gmlp_b_in, gmlp_ln_g, gmlp_ln_b, gmlp_w_s, gmlp_b_s, gmlp_w_out, gmlp_b_out, ffn_w_up, ffn_b_up, ffn_w_dw, ffn_b_dw, ffn_w_down, ffn_b_down, norm1_g, norm1_b, norm2_g, norm2_b, loss_target, m_conv_w_in, m_conv_b_in, m_conv_w_dw, m_conv_b_dw, m_conv_ln_g, m_conv_ln_b, m_conv_w_out, m_conv_b_out, m_gmlp_w_in, m_gmlp_b_in, m_gmlp_ln_g, m_gmlp_ln_b, m_gmlp_w_s, m_gmlp_b_s, m_gmlp_w_out, m_gmlp_b_out, m_ffn_w_up, m_ffn_b_up, m_ffn_w_dw, m_ffn_b_dw, m_ffn_w_down, m_ffn_b_down, m_norm1_g, m_norm1_b, m_norm2_g, m_norm2_b, v_conv_w_in, v_conv_b_in, v_conv_w_dw, v_conv_b_dw, v_conv_ln_g, v_conv_ln_b, v_conv_w_out, v_conv_b_out, v_gmlp_w_in, v_gmlp_b_in, v_gmlp_ln_g, v_gmlp_ln_b, v_gmlp_w_s, v_gmlp_b_s, v_gmlp_w_out, v_gmlp_b_out, v_ffn_w_up, v_ffn_b_up, v_ffn_w_dw, v_ffn_b_dw, v_ffn_w_down, v_ffn_b_down, v_norm1_g, v_norm1_b, v_norm2_g, v_norm2_b):
    return _step(dict(locals()))
```

```python
import math

import jax
import jax.numpy as jnp
from jax import lax
from jax.experimental import pallas as pl
from jax.experimental.pallas import tpu as pltpu

F32 = jnp.float32
BF16 = jnp.bfloat16
MESH = pl.DeviceIdType.MESH

N_DEV = 8
DEPTH = 4
ALPHA = (2.0 * DEPTH) ** 0.25
LN_EPS = 1e-5
CONV_K = 31
FFN_K = 3
CHUNK = 128
GROUPS = 8
ADAM_LR = 0.001
ADAM_B1 = 0.9
ADAM_B2 = 0.999
ADAM_EPS = 1e-08
ADAM_WD = 0.01
ADAM_STEP = 10
INV_SQRT2 = 1.0 / math.sqrt(2.0)
INV_SQRT2PI = 1.0 / math.sqrt(2.0 * math.pi)

LANES = 128
SUBLANES = 8
VMEM_LIMIT = 56 * 1024 * 1024
TM_MM = 1024
TK_DW = 2048
TM_EW = 256
TM_ROW = 512


def _call(body, name, grid, in_specs, out_specs, out_shape, scratch=(), aliases=None, deps=()):
    deps = list(deps)
    in_specs = list(in_specs)
    n_in = len(in_specs)
    if deps:
        inner = body

        def body(*refs):
            return inner(*refs[:n_in], *refs[n_in + len(deps):])

        in_specs = in_specs + [pl.BlockSpec(memory_space=pl.ANY)] * len(deps)
    fn = pl.pallas_call(
        body, name=name, grid=grid, in_specs=in_specs, out_specs=out_specs, out_shape=out_shape,
        scratch_shapes=list(scratch), input_output_aliases=aliases or {},
        compiler_params=pltpu.CompilerParams(vmem_limit_bytes=VMEM_LIMIT))
    return lambda *args: fn(*args, *deps)


def _sds(shape, dtype):
    return jax.ShapeDtypeStruct(tuple(shape), dtype)


def _sigmoid(x):
    return 1.0 / (1.0 + jnp.exp(-x))


def _acc_rows(ref, val, first):
    @pl.when(first)
    def _():
        ref[...] = val

    @pl.when(jnp.logical_not(first))
    def _():
        ref[...] += val


def _colsum(v):
    return jnp.sum(v, axis=0, keepdims=True)


_DIMS = {"nn": ((1,), (0,)), "nt": ((1,), (1,)), "tn": ((0,), (0,))}


def _matmul(name, a, b, mode, *, grid, a_spec, b_spec, o_spec, o_shape, o_dtype, k_axis=None, nk=1,
            acc_shape=None, bias=None, bias_spec=None, res=None, res_spec=None, res_scale=1.0, deps=()):
    dims = (_DIMS[mode], ((), ()))
    has_bias, has_res = bias is not None, res is not None

    def body(*refs):
        a_ref, b_ref = refs[0], refs[1]
        pos = 2
        bias_ref = res_ref = None
        if has_bias:
            bias_ref = refs[pos]
            pos += 1
        if has_res:
            res_ref = refs[pos]
            pos += 1
        o_ref = refs[pos]
        acc_ref = refs[pos + 1] if nk > 1 else None
        p = lax.dot_general(a_ref[...].astype(BF16), b_ref[...].astype(BF16), dims, preferred_element_type=F32)

        def finish(acc):
            if has_bias:
                acc = acc + bias_ref[...]
            if has_res:
                acc = acc + res_scale * res_ref[...]
            o_ref[...] = acc.astype(o_dtype)

        if nk == 1:
            finish(p)
        else:
            k = pl.program_id(k_axis)

            @pl.when(k == 0)
            def _():
                acc_ref[...] = p

            @pl.when(k > 0)
            def _():
                acc_ref[...] += p

            @pl.when(k == nk - 1)
            def _():
                finish(acc_ref[...])

    ins, specs = [a, b], [a_spec, b_spec]
    if has_bias:
        ins.append(bias)
        specs.append(bias_spec)
    if has_res:
        ins.append(res)
        specs.append(res_spec)
    scratch = [pltpu.VMEM(acc_shape, F32)] if nk > 1 else []
    return _call(body, name, grid, specs, o_spec, _sds(o_shape, o_dtype), scratch, deps=deps)(*ins)


TM_LN = 512


def _matmul_ln(name, a, b, x_res, bias3, g3, b3, l_bias, l_norm):
    t, d = x_res.shape
    tm = min(TM_LN, t)
    blocked = a.ndim == 3

    def body(a_ref, b_ref, x_ref, bias_ref, g_ref, be_ref, o_ref, ob_ref, xh_ref, rs_ref):
        if blocked:
            y = None
            for k in range(a.shape[0]):
                p = jnp.dot(a_ref[k], b_ref[k], preferred_element_type=F32)
                y = p if y is None else y + p
        else:
            y = jnp.dot(a_ref[...], b_ref[...], preferred_element_type=F32)
        xhat, rstd = _ln_stats(ALPHA * x_ref[...] + y + bias_ref[...])
        out = xhat * g_ref[...] + be_ref[...]
        o_ref[...] = out
        ob_ref[...] = out.astype(BF16)
        xh_ref[...] = xhat
        rs_ref[...] = rstd

    if blocked:
        a_spec = pl.BlockSpec((a.shape[0], tm, a.shape[2]), lambda i: (0, i, 0))
        b_spec = pl.BlockSpec(b.shape, lambda i: (0, 0, 0))
    else:
        a_spec = pl.BlockSpec((tm, a.shape[1]), lambda i: (i, 0))
        b_spec = pl.BlockSpec(b.shape, lambda i: (0, 0))
    row = pl.BlockSpec((tm, d), lambda i: (i, 0))
    stat = pl.BlockSpec((tm, 1), lambda i: (i, 0))

    def vec(l):
        return pl.BlockSpec((None, 1, d), lambda i: (l, 0, 0))

    return _call(body, name, (t // tm,), [a_spec, b_spec, row, vec(l_bias), vec(l_norm), vec(l_norm)],
                 [row, row, row, stat],
                 [_sds((t, d), F32), _sds((t, d), BF16), _sds((t, d), F32), _sds((t, 1), F32)])(
                     a, b, x_res, bias3, g3, b3)


def _mesh_pos():
    return lax.axis_index("x"), lax.axis_index("y"), lax.axis_index("c")


def _any_specs(n):
    return [pl.BlockSpec(memory_space=pl.ANY)] * n


def _all_gather(name, srcs):
    n = len(srcs)

    def body(*refs):
        src, out = refs[:n], refs[n:2 * n]
        send_sems, recv_sems, local_sems = refs[2 * n:]
        x, y, c = _mesh_pos()
        me, sibling = (x, y, c), (x, y, 1 - c)
        chips = [(1 - x, y), (x, 1 - y), (1 - x, 1 - y)]

        def slot(k, p):
            return out[k].at[:, 4 * p[0] + 2 * p[1] + p[2]]

        def copy(k, idx, block, to, s=None):
            return pltpu.make_async_remote_copy(
                src_ref=slot(k, block) if s is None else s, dst_ref=slot(k, block),
                send_sem=send_sems.at[k * 7 + idx], recv_sem=recv_sems.at[k * 7 + idx],
                device_id=to, device_id_type=MESH)

        local = [pltpu.make_async_copy(src[k], slot(k, me), local_sems.at[k]) for k in range(n)]
        for cp in local:
            cp.start()
        first = []
        for k in range(n):
            first.append(copy(k, 0, me, sibling, src[k]))
            for j, chip in enumerate(chips):
                first.append(copy(k, 1 + j, me, (*chip, c), src[k]))
        for cp in first:
            cp.start()
        passed = []
        for j, chip in enumerate(chips):
            for k in range(n):
                copy(k, 1 + j, (*chip, c), me).wait_recv()
                cp = copy(k, 4 + j, (*chip, c), sibling)
                cp.start()
                passed.append(cp)
        for k in range(n):
            copy(k, 0, sibling, me).wait_recv()
            for j, chip in enumerate(chips):
                copy(k, 4 + j, (*chip, 1 - c), me).wait_recv()
        for cp in first + passed:
            cp.wait_send()
        for cp in local:
            cp.wait()

    out_shape = [_sds((s.shape[0], N_DEV) + s.shape[1:], s.dtype) for s in srcs]
    return _call(body, name, (), [pl.BlockSpec(memory_space=pltpu.VMEM)] * n, _any_specs(n), out_shape,
                 [pltpu.SemaphoreType.DMA((7 * n,)), pltpu.SemaphoreType.DMA((7 * n,)),
                  pltpu.SemaphoreType.DMA((n,))])(*srcs)


HBM_SPEC = pl.BlockSpec(memory_space=pltpu.HBM)
SEM_SPEC = pl.BlockSpec(memory_space=pltpu.SEMAPHORE)
N_PEER = N_DEV - 1


def _split_call(body, name, in_specs, out_specs, out_shape, aliases):
    return pl.pallas_call(
        body, name=name, in_specs=in_specs, out_specs=out_specs, out_shape=out_shape, input_output_aliases=aliases,
        compiler_params=pltpu.CompilerParams(has_side_effects=pltpu.SideEffectType.DATAFLOW_SIDE_EFFECTING))


def _peers(x, y, c):
    return [(1 - x if q & 4 else x, 1 - y if q & 2 else y, 1 - c if q & 1 else c) for q in range(1, N_DEV)]


def _in_hbm(a):
    return pltpu.with_memory_space_constraint(a, pltpu.HBM)


def _place_own(name, srcs, deps=()):
    n = len(srcs)

    def body(*refs):
        src, out, sems = refs[:n], refs[n:2 * n], refs[2 * n]
        x, y, c = _mesh_pos()
        dev = 4 * x + 2 * y + c
        copies = [pltpu.make_async_copy(src[k], out[k].at[dev], sems.at[k]) for k in range(n)]
        for cp in copies:
            cp.start()
        for cp in copies:
            cp.wait()

    return _call(body, name, (), [pl.BlockSpec(memory_space=pltpu.VMEM)] * n, _any_specs(n),
                 [_sds((N_DEV,) + s.shape, s.dtype) for s in srcs], [pltpu.SemaphoreType.DMA((n,))],
                 deps=deps)(*srcs)


def _gather_start(name, lands):
    n = len(lands)

    def body(*refs):
        land, send_sems, recv_sems = refs[:n], refs[n], refs[n + 1]
        x, y, c = _mesh_pos()
        dev = 4 * x + 2 * y + c
        for k in range(n):
            for peer in _peers(x, y, c):
                pltpu.make_async_remote_copy(
                    src_ref=land[k].at[dev], dst_ref=land[k].at[dev], send_sem=send_sems.at[k],
                    recv_sem=recv_sems.at[k], device_id=peer, device_id_type=MESH).start()
        token = refs[-1]
        token[...] = jnp.zeros_like(token)

    outs = _split_call(
        body, name, [HBM_SPEC] * n, [SEM_SPEC, SEM_SPEC] + [HBM_SPEC] * n + [pl.BlockSpec(memory_space=pltpu.VMEM)],
        [pltpu.SemaphoreType.DMA((n,)), pltpu.SemaphoreType.DMA((n,))] + [pltpu.HBM(a.shape, a.dtype) for a in lands]
        + [_sds((SUBLANES, LANES), F32)],
        {k: 2 + k for k in range(n)})(*[_in_hbm(a) for a in lands])
    return outs[0], outs[1], list(outs[2:2 + n]), outs[-1]


def _wait_seven(src_ref, dst_ref, send_sem, recv_sem):
    cp = pltpu.make_async_remote_copy(
        src_ref=src_ref.at[pl.ds(0, N_PEER)], dst_ref=dst_ref.at[pl.ds(0, N_PEER)], send_sem=send_sem,
        recv_sem=recv_sem, device_id=_mesh_pos(), device_id_type=MESH)
    cp.wait_send()
    cp.wait_recv()


def _gather_wait(name, land, send_sems, recv_sems, k, after):
    def body(land_ref, send_ref, recv_ref, after_ref, out_ref):
        _wait_seven(land_ref, land_ref, send_ref.at[k], recv_ref.at[k])

    return _split_call(body, name, [HBM_SPEC, SEM_SPEC, SEM_SPEC, pl.BlockSpec(memory_space=pl.ANY)], HBM_SPEC,
                       pltpu.HBM(land.shape, land.dtype), {0: 0})(land, send_sems, recv_sems, after)


def _scatter_start(name, parts_list):
    n = len(parts_list)

    def body(*refs):
        x, y, c = _mesh_pos()
        dev = 4 * x + 2 * y + c
        for k in range(n):
            parts_ref, land_ref = refs[2 * k], refs[2 * k + 1]
            send_sem, recv_sem = refs[2 * n + 4 * k], refs[2 * n + 4 * k + 1]
            for peer in _peers(x, y, c):
                pltpu.make_async_remote_copy(
                    src_ref=parts_ref.at[4 * peer[0] + 2 * peer[1] + peer[2]], dst_ref=land_ref.at[dev],
                    send_sem=send_sem, recv_sem=recv_sem, device_id=peer, device_id_type=MESH).start()
        token = refs[-1]
        token[...] = jnp.zeros_like(token)

    ins, out_specs, out_shape, aliases = [], [], [], {}
    for k, parts in enumerate(parts_list):
        buf = pltpu.HBM(parts.shape, parts.dtype)
        ins += [_in_hbm(parts), _in_hbm(lax.empty(parts.shape, parts.dtype))]
        out_specs += [SEM_SPEC, SEM_SPEC, HBM_SPEC, HBM_SPEC]
        out_shape += [pltpu.SemaphoreType.DMA(()), pltpu.SemaphoreType.DMA(()), buf, buf]
        aliases.update({2 * k: 4 * k + 2, 2 * k + 1: 4 * k + 3})
    outs = _split_call(body, name, [HBM_SPEC] * (2 * n), out_specs + [pl.BlockSpec(memory_space=pltpu.VMEM)],
                       out_shape + [_sds((SUBLANES, LANES), F32)], aliases)(*ins)
    return [tuple(outs[4 * k:4 * k + 4]) for k in range(n)], outs[-1]


def _scatter_wait(name, started, after):
    n = len(started)

    def body(*refs):
        for k in range(n):
            send_sem, recv_sem, parts_ref, land_ref = refs[4 * k:4 * k + 4]
            _wait_seven(parts_ref, land_ref, send_sem, recv_sem)

    flat = [a for s in started for a in s]
    outs = _split_call(
        body, name, [SEM_SPEC, SEM_SPEC, HBM_SPEC, HBM_SPEC] * n + [pl.BlockSpec(memory_space=pl.ANY)],
        [HBM_SPEC, HBM_SPEC] * n, [pltpu.HBM(a.shape, a.dtype) for s in started for a in s[2:]],
        {4 * k + 2 + t: 2 * k + t for k in range(n) for t in range(2)})(*flat, after)
    return list(outs[0::2]), list(outs[1::2])


def _to_segments(a, tile):
    seg = tile // SUBLANES
    return a.reshape((a.shape[0] // tile, SUBLANES, seg) + a.shape[1:]).swapaxes(1, 2).reshape(a.shape)


def _from_segments(a, tile):
    seg = tile // SUBLANES
    return a.reshape((a.shape[0] // tile, seg, SUBLANES) + a.shape[1:]).swapaxes(1, 2).reshape(a.shape)


def _chunk(ref, q):
    return ref[q * SUBLANES:(q + 1) * SUBLANES, :]


def _fill_wrap_prev(x_ref, halo_ref, wrap_ref, n_wrap, n_halo, seg, keep):
    sub = lax.broadcasted_iota(jnp.int32, (SUBLANES, x_ref.shape[-1]), 0)
    for j in range(n_wrap):
        q = seg - n_wrap + j
        hq = q - (seg - n_halo)
        row = halo_ref[hq * SUBLANES + SUBLANES - 1:(hq + 1) * SUBLANES, :] * keep
        wrap_ref[j * SUBLANES:(j + 1) * SUBLANES, :] = jnp.where(sub == 0, row, pltpu.roll(_chunk(x_ref, q), 1, 0))


def _fill_wrap_next(x_ref, halo_ref, wrap_ref, n_wrap, keep):
    sub = lax.broadcasted_iota(jnp.int32, (SUBLANES, x_ref.shape[-1]), 0)
    for j in range(n_wrap):
        row = halo_ref[j * SUBLANES:j * SUBLANES + 1, :] * keep
        wrap_ref[j * SUBLANES:(j + 1) * SUBLANES, :] = jnp.where(
            sub == SUBLANES - 1, row, pltpu.roll(_chunk(x_ref, j), SUBLANES - 1, 0))


def _past(x_ref, wrap_ref, q, d, n_wrap):
    return _chunk(x_ref, q - d) if q >= d else _chunk(wrap_ref, q - d + n_wrap)


def _future(x_ref, wrap_ref, q, d, seg):
    return _chunk(x_ref, q + d) if q + d < seg else _chunk(wrap_ref, q + d - seg)


def _conv_fwd(x_ref, wrap_ref, w_ref, b_ref, out_ref, seg, k_taps):
    bias = jnp.broadcast_to(b_ref[...], (SUBLANES, x_ref.shape[-1]))
    for q in range(seg):
        acc = bias
        for k in range(k_taps):
            acc = acc + w_ref[k:k + 1, :] * _past(x_ref, wrap_ref, q, k_taps - 1 - k, k_taps - 1)
        out_ref[q * SUBLANES:(q + 1) * SUBLANES, :] = acc


def _conv_bwd_data(d_ref, wrap_ref, w_ref, out_ref, seg, k_taps):
    for q in range(seg):
        acc = None
        for k in range(k_taps):
            term = w_ref[k:k + 1, :] * _future(d_ref, wrap_ref, q, k_taps - 1 - k, seg)
            acc = term if acc is None else acc + term
        out_ref[q * SUBLANES:(q + 1) * SUBLANES, :] = acc


def _conv_bwd_taps(d_ref, x_ref, wrap_ref, dw_ref, seg, k_taps):
    for k in range(k_taps):
        part = None
        for q in range(seg):
            term = _chunk(d_ref, q) * _past(x_ref, wrap_ref, q, k_taps - 1 - k, k_taps - 1)
            part = term if part is None else part + term
        dw_ref[k:k + 1, :] += _colsum(part)


def _tile_halo_specs(tm, width_block, n_halo, n_tiles, block_of):
    rows = n_halo * SUBLANES
    per = tm // rows
    tile = pl.BlockSpec(width_block(tm), lambda n, i: block_of(n, i))
    prev = pl.BlockSpec(width_block(rows), lambda n, i: block_of(n, jnp.maximum(i * per - 1, 0)))
    nxt = pl.BlockSpec(width_block(rows), lambda n, i: block_of(n, jnp.minimum((i + 1) * per, n_tiles * per - 1)))
    return tile, prev, nxt


def _ln_stats(v):
    mu = jnp.mean(v, axis=-1, keepdims=True)
    vc = v - mu
    var = jnp.mean(vc * vc, axis=-1, keepdims=True)
    rstd = lax.rsqrt(var + LN_EPS)
    return vc * rstd, rstd


def _ln_backward(dxhat, xhat, rstd):
    m1 = jnp.mean(dxhat, axis=-1, keepdims=True)
    m2 = jnp.mean(dxhat * xhat, axis=-1, keepdims=True)
    return rstd * (dxhat - m1 - xhat * m2)


def _row_spec(tm, width):
    return pl.BlockSpec((tm, width), lambda i: (i, 0))


def _param_spec(l, width):
    return pl.BlockSpec((None, 1, width), lambda *_: (l, 0, 0))


def _ln_bwd_rows(dout, xh_ref, rs_ref, g_ref, dr_ref, dg_ref, db_ref, dsum_ref, first, drb_ref=None):
    xhat = xh_ref[...]
    dr = _ln_backward(dout * g_ref[...], xhat, rs_ref[...])
    dr_ref[...] = dr
    if drb_ref is not None:
        drb_ref[...] = dr.astype(BF16)
    _acc_rows(dg_ref, _colsum(dout * xhat), first)
    _acc_rows(db_ref, _colsum(dout), first)
    _acc_rows(dsum_ref, _colsum(dr), first)


def _ln_bwd_specs(tm, d, l, row_of):
    vec = pl.BlockSpec((1, d), lambda *_: (0, 0))
    ins = [pl.BlockSpec((tm, d), row_of), pl.BlockSpec((tm, 1), row_of), _param_spec(l, d)]
    return ins, [pl.BlockSpec((tm, d), row_of), vec, vec, vec]


def _ln_res_bwd(name, dout, xhat, rstd, g3, l, deps=()):
    t, d = dout.shape
    tm = min(TM_ROW, t)

    def body(do_ref, xh_ref, rs_ref, g_ref, dr_ref, dg_ref, db_ref, dc_ref, drb_ref):
        _ln_bwd_rows(do_ref[...], xh_ref, rs_ref, g_ref, dr_ref, dg_ref, db_ref, dc_ref, pl.program_id(0) == 0,
                     drb_ref)

    ins, outs = _ln_bwd_specs(tm, d, l, lambda i: (i, 0))
    return _call(body, name, (t // tm,), [_row_spec(tm, d)] + ins, outs + [_row_spec(tm, d)],
                 [_sds((t, d), F32)] + [_sds((1, d), F32)] * 3 + [_sds((t, d), BF16)],
                 deps=deps)(dout, xhat, rstd, g3)


def _glu_bwd(name, du, h):
    t, c2 = h.shape
    c = c2 // 2
    tm = min(TM_ROW, t)

    def body(du_ref, a_ref, g_ref, dh_ref, db_ref):
        first = pl.program_id(0) == 0
        du_v, a = du_ref[...], a_ref[...]
        sg = _sigmoid(g_ref[...])
        da = du_v * sg
        dg = du_v * a * sg * (1.0 - sg)
        dh_ref[:, :c] = da.astype(BF16)
        dh_ref[:, c:] = dg.astype(BF16)
        _acc_rows(db_ref.at[:, :c], _colsum(da), first)
        _acc_rows(db_ref.at[:, c:], _colsum(dg), first)

    return _call(body, name, (t // tm,),
                 [_row_spec(tm, c), pl.BlockSpec((tm, c), lambda i: (i, 0)), pl.BlockSpec((tm, c), lambda i: (i, 1))],
                 [_row_spec(tm, c2), pl.BlockSpec((1, c2), lambda i: (0, 0))],
                 [_sds((t, c2), BF16), _sds((1, c2), F32)])(du, h, h)


CONV_CB = 1024
TAPS_PAD = 32


def _dwconv31(name, u, w3, b3, l, seq):
    t, c = u.shape
    tm, cb = TM_EW, CONV_CB
    seg, seq_tiles, n_tiles = tm // SUBLANES, seq // tm, t // tm
    n_wrap = CONV_K - 1
    tile, prev, _ = _tile_halo_specs(tm, lambda rows: (rows, cb), seg, n_tiles, lambda n, r: (r, n))

    def body(u_ref, halo_ref, w_ref, b_ref, o_ref, wrap_ref):
        keep = (pl.program_id(1) % seq_tiles != 0).astype(F32)
        _fill_wrap_prev(u_ref, halo_ref, wrap_ref, n_wrap, seg, seg, keep)
        _conv_fwd(u_ref, wrap_ref, w_ref, b_ref, o_ref, seg, CONV_K)

    return _call(body, name, (c // cb, n_tiles),
                 [tile, prev, pl.BlockSpec((None, CONV_K, cb), lambda n, i: (l, 0, n)),
                  pl.BlockSpec((None, 1, cb), lambda n, i: (l, 0, n))],
                 tile, _sds((t, c), F32), [pltpu.VMEM((n_wrap * SUBLANES, cb), F32)])(u, u, w3, b3)


def _dwconv31_bwd(name, dc, u, w3, l, seq):
    t, c = dc.shape
    tm, cb = TM_EW, CONV_CB
    seg, seq_tiles, n_tiles = tm // SUBLANES, seq // tm, t // tm
    n_wrap = CONV_K - 1
    tile, prev, nxt = _tile_halo_specs(tm, lambda rows: (rows, cb), seg, n_tiles, lambda n, r: (r, n))

    def body(dc_ref, dcn_ref, u_ref, up_ref, w_ref, du_ref, dw_ref, dwrap_ref, uwrap_ref):
        i = pl.program_id(1)
        keep_prev = (i % seq_tiles != 0).astype(F32)
        keep_next = (i % seq_tiles != seq_tiles - 1).astype(F32)
        _fill_wrap_next(dc_ref, dcn_ref, dwrap_ref, n_wrap, keep_next)
        _conv_bwd_data(dc_ref, dwrap_ref, w_ref, du_ref, seg, CONV_K)

        @pl.when(i == 0)
        def _():
            dw_ref[...] = jnp.zeros_like(dw_ref)

        _fill_wrap_prev(u_ref, up_ref, uwrap_ref, n_wrap, seg, seg, keep_prev)
        _conv_bwd_taps(dc_ref, u_ref, uwrap_ref, dw_ref, seg, CONV_K)

    wrap = pltpu.VMEM((n_wrap * SUBLANES, cb), F32)
    return _call(body, name, (c // cb, n_tiles),
                 [tile, nxt, tile, prev, pl.BlockSpec((None, CONV_K, cb), lambda n, i: (l, 0, n))],
                 [tile, pl.BlockSpec((TAPS_PAD, cb), lambda n, i: (0, n))],
                 [_sds((t, c), F32), _sds((TAPS_PAD, c), F32)], [wrap, wrap])(dc, dc, u, u, w3)


def _ln_silu(name, cx, g3, b3, l):
    t, d = cx.shape
    tm = min(TM_ROW, t)

    def body(c_ref, g_ref, b_ref, o_ref):
        xhat, _ = _ln_stats(c_ref[...])
        nv = xhat * g_ref[...] + b_ref[...]
        o_ref[...] = (nv * _sigmoid(nv)).astype(BF16)

    return _call(body, name, (t // tm,), [_row_spec(tm, d), _param_spec(l, d), _param_spec(l, d)],
                 _row_spec(tm, d), _sds((t, d), BF16))(cx, g3, b3)


def _ln_silu_bwd(name, ds, cx, g3, b3, l, deps=()):
    t, d = cx.shape
    tm = min(TM_ROW, t)

    def body(ds_ref, c_ref, g_ref, b_ref, dc_ref, dg_ref, db_ref, dsum_ref):
        first = pl.program_id(0) == 0
        xhat, rstd = _ln_stats(c_ref[...])
        g = g_ref[...]
        nv = xhat * g + b_ref[...]
        sg = _sigmoid(nv)
        dn = ds_ref[...] * (sg * (1.0 + nv * (1.0 - sg)))
        dc = _ln_backward(dn * g, xhat, rstd)
        dc_ref[...] = dc
        _acc_rows(dg_ref, _colsum(dn * xhat), first)
        _acc_rows(db_ref, _colsum(dn), first)
        _acc_rows(dsum_ref, _colsum(dc), first)

    vec = pl.BlockSpec((1, d), lambda i: (0, 0))
    return _call(body, name, (t // tm,),
                 [_row_spec(tm, d), _row_spec(tm, d), _param_spec(l, d), _param_spec(l, d)],
                 [_row_spec(tm, d), vec, vec, vec],
                 [_sds((t, d), F32)] + [_sds((1, d), F32)] * 3, deps=deps)(ds, cx, g3, b3)


FFN_HALO = FFN_K - 1


def _ffn_conv(x_ref, halo_ref, wrap_ref, w_ref, b_ref, keep, seg, out_ref):
    _fill_wrap_prev(x_ref, halo_ref, wrap_ref, FFN_K - 1, FFN_HALO, seg, keep)
    _conv_fwd(x_ref, wrap_ref, w_ref, b_ref, out_ref, seg, FFN_K)


TM_FFN = 512
WRAP_ROWS = FFN_HALO * SUBLANES


def _sub_tiles(x_ref, prev_ref, next_ref, keep_prev, keep_next, n_sub):
    out = []
    for s in range(n_sub):
        tile = x_ref.at[pl.ds(s * TM_EW, TM_EW)]
        prev = prev_ref if s == 0 else x_ref.at[pl.ds(s * TM_EW - WRAP_ROWS, WRAP_ROWS)]
        nxt = next_ref if s == n_sub - 1 else x_ref.at[pl.ds((s + 1) * TM_EW, WRAP_ROWS)]
        out.append((tile, prev, keep_prev if s == 0 else 1.0, nxt, keep_next if s == n_sub - 1 else 1.0))
    return out


def _rows(ref, s, rows):
    return ref.at[pl.ds(s * rows, rows)]


def _ffn_specs(tm, fb, n_tiles):
    return _tile_halo_specs(tm, lambda rows: (None, rows, fb), FFN_HALO, n_tiles, lambda n, r: (n, r, 0))


def _ffn_up_act(name, x, w_up, b_up4, wdw, bdw, l, seq):
    t, d = x.shape
    nb, fb, _ = w_up.shape
    half = nb // 2
    tm = min(TM_FFN, seq)
    n_sub, seg, seq_steps, n_steps = tm // TM_EW, TM_EW // SUBLANES, seq // tm, t // tm
    per = tm // WRAP_ROWS
    nt_dims = (_DIMS["nt"], ((), ()))

    def body(x_ref, xp_ref, ug_ref, uv_ref, bug_ref, buv_ref, wg_ref, wv_ref, bg_ref, bv_ref,
             hg_ref, hv_ref, a_ref, pg_ref, pv_ref, gwrap_ref, vwrap_ref, cg_ref, cv_ref):
        keep = (pl.program_id(1) % seq_steps != 0).astype(F32)
        xb, xpb = x_ref[...].astype(BF16), xp_ref[...].astype(BF16)
        hg_ref[...] = lax.dot_general(xb, ug_ref[...], nt_dims, preferred_element_type=F32) + bug_ref[...]
        pg_ref[...] = lax.dot_general(xpb, ug_ref[...], nt_dims, preferred_element_type=F32) + bug_ref[...]
        for s, (tile, prev, kp, _, _) in enumerate(_sub_tiles(hg_ref, pg_ref, None, keep, None, n_sub)):
            _ffn_conv(tile, prev, gwrap_ref, wg_ref, bg_ref, kp, seg, _rows(cg_ref, s, TM_EW))
        hv_ref[...] = lax.dot_general(xb, uv_ref[...], nt_dims, preferred_element_type=F32) + buv_ref[...]
        pv_ref[...] = lax.dot_general(xpb, uv_ref[...], nt_dims, preferred_element_type=F32) + buv_ref[...]
        for s, (tile, prev, kp, _, _) in enumerate(_sub_tiles(hv_ref, pv_ref, None, keep, None, n_sub)):
            _ffn_conv(tile, prev, vwrap_ref, wv_ref, bv_ref, kp, seg, _rows(cv_ref, s, TM_EW))
        cg = cg_ref[...]
        a_ref[...] = (cg * _sigmoid(cg) * cv_ref[...]).astype(BF16)

    def blk(shift):
        return pl.BlockSpec((None, fb, d), lambda n, i: (n + shift, 0, 0))

    def vec(shift, rows):
        return pl.BlockSpec((None, None, rows, fb), lambda n, i: (l, n + shift, 0, 0))

    out = pl.BlockSpec((None, tm, fb), lambda n, i: (n, i, 0))
    tmp = pltpu.VMEM((tm, fb), F32)
    halo = pltpu.VMEM((WRAP_ROWS, fb), F32)
    return _call(body, name, (half, n_steps),
                 [pl.BlockSpec((tm, d), lambda n, i: (i, 0)),
                  pl.BlockSpec((WRAP_ROWS, d), lambda n, i: (jnp.maximum(i * per - 1, 0), 0)),
                  blk(0), blk(half), vec(0, 1), vec(half, 1), vec(0, FFN_K), vec(half, FFN_K), vec(0, 1), vec(half, 1)],
                 [out, out, out],
                 [_sds((half, t, fb), F32), _sds((half, t, fb), F32), _sds((half, t, fb), BF16)],
                 [halo, halo, halo, halo, tmp, tmp])(x, x, w_up, w_up, b_up4, b_up4, wdw, wdw, bdw, bdw)


def _ffn_act_bwd(name, dy, w_down, hg, hv, wdw, bdw, l, seq, deps=()):
    half, t, fb = hg.shape
    d = dy.shape[-1]
    tm = min(TM_FFN, seq)
    n_sub, seg, seq_steps, n_steps = tm // TM_EW, TM_EW // SUBLANES, seq // tm, t // tm
    tile, prev, _ = _ffn_specs(tm, fb, n_steps)

    def body(dy_ref, wd_ref, g_ref, gp_ref, v_ref, vp_ref, wg_ref, wv_ref, bg_ref, bv_ref,
             dg_ref, dv_ref, dbg_ref, dbv_ref, dwg_ref, dwv_ref, gwrap_ref, vwrap_ref, cg_ref, cv_ref):
        i = pl.program_id(1)
        first = i == 0
        keep = (i % seq_steps != 0).astype(F32)
        da = lax.dot_general(dy_ref[...].astype(BF16), wd_ref[...], (_DIMS["nt"], ((), ())),
                             preferred_element_type=F32)
        g_tiles = _sub_tiles(g_ref, gp_ref, None, keep, None, n_sub)
        v_tiles = _sub_tiles(v_ref, vp_ref, None, keep, None, n_sub)
        for s in range(n_sub):
            _ffn_conv(g_tiles[s][0], g_tiles[s][1], _rows(gwrap_ref, s, WRAP_ROWS), wg_ref, bg_ref, g_tiles[s][2],
                      seg, _rows(cg_ref, s, TM_EW))
            _ffn_conv(v_tiles[s][0], v_tiles[s][1], _rows(vwrap_ref, s, WRAP_ROWS), wv_ref, bv_ref, v_tiles[s][2],
                      seg, _rows(cv_ref, s, TM_EW))
        cg, cv = cg_ref[...], cv_ref[...]
        sg = _sigmoid(cg)
        dcv = da * cg * sg
        dcg = da * cv * sg * (1.0 + cg * (1.0 - sg))
        dg_ref[...] = dcg
        dv_ref[...] = dcv
        _acc_rows(dbg_ref, _colsum(dcg), first)
        _acc_rows(dbv_ref, _colsum(dcv), first)

        @pl.when(first)
        def _():
            dwg_ref[...] = jnp.zeros_like(dwg_ref)
            dwv_ref[...] = jnp.zeros_like(dwv_ref)

        for s in range(n_sub):
            _conv_bwd_taps(_rows(dg_ref, s, TM_EW), g_tiles[s][0], _rows(gwrap_ref, s, WRAP_ROWS), dwg_ref, seg, FFN_K)
            _conv_bwd_taps(_rows(dv_ref, s, TM_EW), v_tiles[s][0], _rows(vwrap_ref, s, WRAP_ROWS), dwv_ref, seg, FFN_K)

    def vec(shift, rows):
        return pl.BlockSpec((None, None, rows, fb), lambda n, i: (l, n + shift, 0, 0))

    def acc(rows):
        return pl.BlockSpec((None, rows, fb), lambda n, i: (n, 0, 0))

    wrap = pltpu.VMEM((n_sub * WRAP_ROWS, fb), F32)
    tmp = pltpu.VMEM((tm, fb), F32)
    return _call(body, name, (half, n_steps),
                 [pl.BlockSpec((tm, d), lambda n, i: (i, 0)), pl.BlockSpec((None, fb, d), lambda n, i: (n, 0, 0)),
                  tile, prev, tile, prev, vec(0, FFN_K), vec(half, FFN_K), vec(0, 1), vec(half, 1)],
                 [tile, tile, acc(1), acc(1), acc(SUBLANES), acc(SUBLANES)],
                 [_sds((half, t, fb), F32), _sds((half, t, fb), F32), _sds((half, 1, fb), F32),
                  _sds((half, 1, fb), F32), _sds((half, SUBLANES, fb), F32), _sds((half, SUBLANES, fb), F32)],
                 [wrap, wrap, tmp, tmp], deps=deps)(dy, w_down, hg, hg, hv, hv, wdw, wdw, bdw, bdw)


def _ffn_conv_t_dx(name, dcg, dcv, wdw, w_up, res, xhat, rstd, g3, l, seq, deps=()):
    half, t, fb = dcg.shape
    nb, d = 2 * half, res.shape[-1]
    tm = min(TM_FFN, seq)
    n_sub, seg, seq_steps, n_steps = tm // TM_EW, TM_EW // SUBLANES, seq // tm, t // tm
    per = tm // WRAP_ROWS

    pair = 2
    n_pairs, half_pairs = nb // pair, half // pair

    def body(g_ref, gn_ref, v_ref, vn_ref, w_ref, up_ref, res_ref, xh_ref, rs_ref, gam_ref,
             dh_ref, db_ref, dr_ref, dgam_ref, dbeta_ref, dsum_ref, wrap_ref, out_ref, acc_ref):
        i, m = pl.program_id(0), pl.program_id(1)
        keep = (i % seq_steps != seq_steps - 1).astype(F32)

        def conv_t(d_ref, dn_ref, b):
            for s, (sub, _, _, nx, kn) in enumerate(_sub_tiles(d_ref.at[b], None, dn_ref.at[b], None, keep, n_sub)):
                _fill_wrap_next(sub, nx, wrap_ref, FFN_K - 1, kn)
                _conv_bwd_data(sub, wrap_ref, w_ref.at[b], _rows(out_ref, s, TM_EW), seg, FFN_K)

        p = None
        for b in range(pair):
            @pl.when(m < half_pairs)
            def _(b=b):
                conv_t(g_ref, gn_ref, b)

            @pl.when(m >= half_pairs)
            def _(b=b):
                conv_t(v_ref, vn_ref, b)

            dh = out_ref[...]
            dhb = dh.astype(BF16)
            dh_ref[b] = dhb
            _acc_rows(db_ref.at[pair * m + b], _colsum(dh), i == 0)
            part = jnp.dot(dhb, up_ref[b], preferred_element_type=F32)
            p = part if p is None else p + part

        @pl.when(m == 0)
        def _():
            acc_ref[...] = p

        @pl.when(m > 0)
        def _():
            acc_ref[...] += p

        @pl.when(m == n_pairs - 1)
        def _():
            _ln_bwd_rows(acc_ref[...] + ALPHA * res_ref[...], xh_ref, rs_ref, gam_ref, dr_ref, dgam_ref, dbeta_ref,
                         dsum_ref, i == 0)

    def src(gate):
        def blk(m):
            return jnp.minimum(m, half_pairs - 1) if gate else jnp.maximum(m - half_pairs, 0)
        tile = pl.BlockSpec((pair, tm, fb), lambda i, m: (blk(m), i, 0))
        nxt = pl.BlockSpec((pair, WRAP_ROWS, fb),
                           lambda i, m: (blk(m), jnp.minimum((i + 1) * per, n_steps * per - 1), 0))
        return [tile, nxt]

    row = pl.BlockSpec((tm, d), lambda i, m: (i, 0))
    ln_ins, ln_outs = _ln_bwd_specs(tm, d, l, lambda i, m: (i, 0))
    tmp = pltpu.VMEM((tm, fb), F32)
    halo = pltpu.VMEM((WRAP_ROWS, fb), F32)
    return _call(body, name, (n_steps, n_pairs),
                 src(True) + src(False) +
                 [pl.BlockSpec((None, pair, FFN_K, fb), lambda i, m: (l, m, 0, 0)),
                  pl.BlockSpec((pair, fb, d), lambda i, m: (m, 0, 0)), row] + ln_ins,
                 [pl.BlockSpec((pair, tm, fb), lambda i, m: (m, i, 0)),
                  pl.BlockSpec((nb, 1, fb), lambda i, m: (0, 0, 0))] + ln_outs,
                 [_sds((nb, t, fb), BF16), _sds((nb, 1, fb), F32), _sds((t, d), F32)] + [_sds((1, d), F32)] * 3,
                 [halo, tmp, pltpu.VMEM((tm, d), F32)],
                 deps=deps)(dcg, dcg, dcv, dcv, wdw, w_up, res, xhat, rstd, g3)


def _gelu_parts(h):
    cdf = 0.5 * (1.0 + lax.erf(h * INV_SQRT2))
    return h * cdf, cdf


def _seg_axis(a, axis, fn):
    return jnp.moveaxis(fn(jnp.moveaxis(a, axis, 0), TM_EW), 0, axis)


def _sgu_operands(w_s, b_s):
    nl = w_s.shape[0]
    n_sub = TM_EW // CHUNK
    tril = jnp.tril(jnp.ones((CHUNK, CHUNK), dtype=bool))
    w_causal = jnp.where(tril, w_s, 0.0)
    w_tile = (jnp.eye(n_sub, dtype=F32)[None, None, :, None, :, None] * w_causal[:, :, None, :, None, :]).reshape(
        nl, GROUPS, TM_EW, TM_EW)
    w_tile = _seg_axis(_seg_axis(w_tile, 2, _to_segments), 3, _to_segments).astype(BF16)
    bs_tile = jnp.broadcast_to(b_s[:, :, None, :, None], (nl, GROUPS, n_sub, CHUNK, CHUNK)).reshape(
        nl, GROUPS, TM_EW, CHUNK)
    return w_tile, _seg_axis(bs_tile, 2, _to_segments)


def _sgu_param_grads(dwt, dbt):
    n_sub = TM_EW // CHUNK
    tril = jnp.tril(jnp.ones((CHUNK, CHUNK), dtype=bool))
    dwt = _seg_axis(_seg_axis(dwt, 1, _from_segments), 2, _from_segments).reshape(GROUPS, n_sub, CHUNK, n_sub, CHUNK)
    dw = sum(dwt[:, a, :, a, :] for a in range(n_sub))
    db = _seg_axis(dbt, 1, _from_segments).reshape(GROUPS, n_sub, CHUNK).sum(axis=1)
    return jnp.where(tril, dw, 0.0), db


def _sgu(name, h, g3, b3, wt, bst, l):
    t, c2 = h.shape
    c = c2 // 2
    tm = TM_EW

    def body(h_ref, g_ref, b_ref, wt_ref, bs_ref, o_ref):
        z, _ = _gelu_parts(h_ref[...])
        u = z[:, :c]
        xhat, _ = _ln_stats(z[:, c:])
        vnb = (xhat * g_ref[...] + b_ref[...]).astype(BF16)
        for gi in range(GROUPS):
            cs = slice(gi * CHUNK, (gi + 1) * CHUNK)
            sp = jnp.dot(wt_ref[gi], vnb[:, cs], preferred_element_type=F32) + bs_ref[gi]
            o_ref[:, cs] = (u[:, cs] * sp).astype(BF16)

    return _call(body, name, (t // tm,),
                 [_row_spec(tm, c2), _param_spec(l, c), _param_spec(l, c),
                  pl.BlockSpec((None, GROUPS, tm, tm), lambda i: (l, 0, 0, 0)),
                  pl.BlockSpec((None, GROUPS, tm, CHUNK), lambda i: (l, 0, 0, 0))],
                 _row_spec(tm, c), _sds((t, c), BF16))(h, g3, b3, wt, bst)


def _sgu_bwd(name, dq, h, g3, b3, wt, bst, l, deps=()):
    t, c2 = h.shape
    c = c2 // 2
    tm = TM_EW
    n_tiles = t // tm

    def body(dq_ref, h_ref, g_ref, b_ref, wt_ref, bs_ref,
             dh_ref, dbin_ref, dw_ref, dbs_ref, dg_ref, db_ref, du_ref, dvn_ref, bsum_ref):
        i = pl.program_id(0)
        first = i == 0
        hv = h_ref[...]
        z, cdf = _gelu_parts(hv)
        u = z[:, :c]
        xhat, rstd = _ln_stats(z[:, c:])
        g = g_ref[...]
        vnb = (xhat * g + b_ref[...]).astype(BF16)

        @pl.when(first)
        def _():
            dw_ref[...] = jnp.zeros_like(dw_ref)
            bsum_ref[...] = jnp.zeros_like(bsum_ref)

        for gi in range(GROUPS):
            cs = slice(gi * CHUNK, (gi + 1) * CHUNK)
            vb = vnb[:, cs]
            w = wt_ref[gi]
            sp = jnp.dot(w, vb, preferred_element_type=F32) + bs_ref[gi]
            dqb = dq_ref[:, cs]
            du_ref[:, cs] = dqb * sp
            dsp = dqb * u[:, cs]
            bsum_ref[gi] += dsp
            dspb = dsp.astype(BF16)
            dw_ref[gi] += lax.dot_general(dspb, vb, (_DIMS["nt"], ((), ())), preferred_element_type=F32)
            dvn_ref[:, cs] = lax.dot_general(w, dspb, (_DIMS["tn"], ((), ())), preferred_element_type=F32)

        dvn = dvn_ref[...]
        dv = _ln_backward(dvn * g, xhat, rstd)
        pdf = jnp.exp(-0.5 * hv * hv) * INV_SQRT2PI
        dgelu = cdf + hv * pdf
        dhu = du_ref[...] * dgelu[:, :c]
        dhv = dv * dgelu[:, c:]
        dh_ref[:, :c] = dhu.astype(BF16)
        dh_ref[:, c:] = dhv.astype(BF16)
        _acc_rows(dbin_ref.at[:, :c], _colsum(dhu), first)
        _acc_rows(dbin_ref.at[:, c:], _colsum(dhv), first)
        _acc_rows(dg_ref, _colsum(dvn * xhat), first)
        _acc_rows(db_ref, _colsum(dvn), first)

        @pl.when(i == n_tiles - 1)
        def _():
            dbs_ref[...] = jnp.sum(bsum_ref[...], axis=-1)

    vec = pl.BlockSpec((1, c), lambda i: (0, 0))
    return _call(body, name, (n_tiles,),
                 [_row_spec(tm, c), _row_spec(tm, c2), _param_spec(l, c), _param_spec(l, c),
                  pl.BlockSpec((None, GROUPS, tm, tm), lambda i: (l, 0, 0, 0)),
                  pl.BlockSpec((None, GROUPS, tm, CHUNK), lambda i: (l, 0, 0, 0))],
                 [_row_spec(tm, c2), pl.BlockSpec((1, c2), lambda i: (0, 0)),
                  pl.BlockSpec((GROUPS, tm, tm), lambda i: (0, 0, 0)),
                  pl.BlockSpec((GROUPS, tm), lambda i: (0, 0)), vec, vec],
                 [_sds((t, c2), BF16), _sds((1, c2), F32), _sds((GROUPS, tm, tm), F32),
                  _sds((GROUPS, tm), F32), _sds((1, c), F32), _sds((1, c), F32)],
                 [pltpu.VMEM((tm, c), F32), pltpu.VMEM((tm, c), F32), pltpu.VMEM((GROUPS, tm, CHUNK), F32)],
                 deps=deps)(dq, h, g3, b3, wt, bst)


def _loss(name, y, target):
    t, d = y.shape
    tm = min(TM_ROW, t)
    n_tiles = t // tm

    def body(y_ref, t_ref, l_ref, dy_ref, acc_ref):
        i = pl.program_id(0)
        diff = y_ref[...] - t_ref[...]
        dy_ref[...] = diff * (1.0 / d)
        _acc_rows(acc_ref, _colsum(diff * diff), i == 0)

        @pl.when(i == n_tiles - 1)
        def _():
            l_ref[...] = jnp.broadcast_to(jnp.sum(acc_ref[...], axis=-1, keepdims=True) * (0.5 / d), (1, LANES))

    return _call(body, name, (n_tiles,), [_row_spec(tm, d), _row_spec(tm, d)],
                 [pl.BlockSpec((1, LANES), lambda i: (0, 0)), _row_spec(tm, d)],
                 [_sds((1, LANES), F32), _sds((t, d), F32)], [pltpu.VMEM((1, d), F32)])(y, target)


def _adamw(g, w, m, v):
    m2 = ADAM_B1 * m + (1.0 - ADAM_B1) * g
    v2 = ADAM_B2 * v + (1.0 - ADAM_B2) * (g * g)
    m_hat = m2 / (1.0 - ADAM_B1 ** ADAM_STEP)
    v_hat = v2 / (1.0 - ADAM_B2 ** ADAM_STEP)
    delta = -ADAM_LR * (m_hat / (jnp.sqrt(v_hat) + ADAM_EPS) + ADAM_WD * w)
    return delta, m2, v2


ROW_TILE_CAP = 512


def _row_tile(rows, cap=ROW_TILE_CAP):
    if rows <= cap:
        return rows
    for tr in range(cap, 15, -16):
        if rows % tr == 0:
            return tr
    return rows


def _sum8_adamw(name, dev, lands, parts, w, m, v):
    nl = len(lands)
    _, r, c = lands[0].shape
    tr = _row_tile(r, cap=256)

    def body(dev_ref, *refs):
        land, own = refs[:nl], refs[nl:2 * nl]
        w_ref, m_ref, v_ref, g_ref, d_ref, m2_ref, v2_ref = refs[2 * nl:]
        layer, me = pl.program_id(0), dev_ref[0]
        for l in range(nl):
            @pl.when(layer == l)
            def _(l=l):
                g = None
                for s in range(N_DEV):
                    part = jnp.where(me == s, own[l][...], land[l][s]).astype(F32)
                    g = part if g is None else g + part
                delta, m2, v2 = _adamw(g, w_ref[...], m_ref[...], v_ref[...])
                g_ref[...] = g
                d_ref[...] = delta
                m2_ref[...] = m2
                v2_ref[...] = v2

    def rows_of(l, a, i):
        return jnp.where(a == l, i, 0)

    spec = pl.BlockSpec((None, tr, c), lambda a, i, dev_ref: (a, i, 0))
    in_specs = [pl.BlockSpec((N_DEV, tr, c), lambda a, i, dev_ref, l=l: (0, rows_of(l, a, i), 0)) for l in range(nl)]
    in_specs += [pl.BlockSpec((None, tr, c), lambda a, i, dev_ref, l=l: (dev_ref[0], rows_of(l, a, i), 0))
                 for l in range(nl)]
    grid_spec = pltpu.PrefetchScalarGridSpec(
        num_scalar_prefetch=1, grid=(nl, r // tr), in_specs=in_specs + [spec] * 3, out_specs=[spec] * 4)
    return pl.pallas_call(
        body, name=name, grid_spec=grid_spec, out_shape=[_sds(w.shape, F32)] * 4,
        compiler_params=pltpu.CompilerParams(vmem_limit_bytes=VMEM_LIMIT))(dev, *lands, *parts, w, m, v)


def _sum8(name, parts):
    _, r, c = parts.shape
    tr = _row_tile(r)

    def body(p_ref, o_ref):
        acc = p_ref[0]
        for s in range(1, N_DEV):
            acc = acc + p_ref[s]
        o_ref[...] = acc

    return _call(body, name, (r // tr,), [pl.BlockSpec((N_DEV, tr, c), lambda i: (0, i, 0))],
                 pl.BlockSpec((tr, c), lambda i: (i, 0)), _sds((r, c), F32))(parts)


def _adamw_small(name, gs, ws, ms, vs):
    n = len(gs)

    def body(*refs):
        g, w, m, v = (refs[k * n:(k + 1) * n] for k in range(4))
        d_out, m_out, v_out = (refs[(4 + k) * n:(5 + k) * n] for k in range(3))
        for k in range(n):
            d_out[k][...], m_out[k][...], v_out[k][...] = _adamw(g[k][...], w[k][...], m[k][...], v[k][...])

    vmem = pl.BlockSpec(memory_space=pltpu.VMEM)
    outs = _call(body, name, (), [vmem] * (4 * n), [vmem] * (3 * n), [_sds(w.shape, F32) for w in ws] * 3)(
        *gs, *ws, *ms, *vs)
    return outs[:n], outs[n:2 * n], outs[2 * n:]


def _pack(arrs, row_multiple=SUBLANES):
    pieces, rows = [], 0
    for a in arrs:
        piece = a.reshape(-1, LANES)
        piece = jnp.pad(piece, ((0, (-piece.shape[0]) % SUBLANES), (0, 0)))
        pieces.append(piece)
        rows += piece.shape[0]
    if rows % row_multiple:
        pieces.append(jnp.zeros(((-rows) % row_multiple, LANES), pieces[0].dtype))
    return jnp.concatenate(pieces, axis=0)


def _unpack(buf, shapes, lead=0):
    out, pos = [], 0
    for shp in shapes:
        rows = math.prod(shp) // LANES
        piece = lax.slice_in_dim(buf, pos, pos + rows, axis=lead)
        out.append(piece.reshape(buf.shape[:lead] + tuple(shp)))
        pos += rows + (-rows) % SUBLANES
    return out


REPLICATED = ["conv_b_in", "conv_b_dw", "conv_ln_g", "conv_ln_b", "conv_b_out", "gmlp_w_s", "gmlp_b_s",
              "ffn_b_up", "ffn_b_dw", "ffn_b_down", "norm1_g", "norm1_b", "norm2_g", "norm2_b"]
SMALL_SHARDED = ["conv_w_dw", "gmlp_b_in", "gmlp_ln_g", "gmlp_ln_b", "gmlp_b_out", "ffn_w_dw"]
BIG = ["conv_w_in", "conv_w_out", "gmlp_w_in", "gmlp_w_out", "ffn_w_up", "ffn_w_down"]
WEIGHTS = ["conv_w_in", "conv_b_in", "conv_w_dw", "conv_b_dw", "conv_ln_g", "conv_ln_b", "conv_w_out", "conv_b_out",
           "gmlp_w_in", "gmlp_b_in", "gmlp_ln_g", "gmlp_ln_b", "gmlp_w_s", "gmlp_b_s", "gmlp_w_out", "gmlp_b_out",
           "ffn_w_up", "ffn_b_up", "ffn_w_dw", "ffn_b_dw", "ffn_w_down", "ffn_b_down",
           "norm1_g", "norm1_b", "norm2_g", "norm2_b"]


def _from_shards(g, lead_shape):
    nd = len(lead_shape)
    perm = tuple(range(1, nd + 1)) + (0, nd + 1)
    return g.transpose(perm).reshape(tuple(lead_shape) + (-1,))


def _to_shards(full, width):
    lead = full.shape[:-1]
    nd = len(lead)
    parts = full.reshape(lead + (N_DEV, width))
    return parts.transpose((nd,) + tuple(range(nd)) + (nd + 1,))


def _step(p):
    x_in, target_in = p["x"], p["loss_target"]
    bsz, seq, d = x_in.shape
    t = bsz * seq
    assert seq % TM_EW == 0 and TM_EW % CHUNK == 0 and TM_EW // SUBLANES >= CONV_K - 1
    x0 = _to_segments(x_in.reshape(t, d), TM_EW)
    target = _to_segments(target_in.reshape(t, d), TM_EW)
    n_conv, n_gmlp = p["conv_w_in"].shape[0], p["gmlp_w_in"].shape[0]
    fb = p["ffn_w_up"].shape[-1]
    nblk = N_DEV
    half = nblk // 2
    cw = p["conv_w_in"].shape[-1]
    tm = min(TM_MM, t)
    nt = t // tm
    tk = min(TK_DW, t)
    ntk = t // tk
    dev = 4 * lax.axis_index("x") + 2 * lax.axis_index("y") + lax.axis_index("c")

    small_shapes = [p[n].shape for n in SMALL_SHARDED]
    w_src = [_pack([p[n] for n in SMALL_SHARDED])]
    for i in range(DEPTH):
        mix = "conv" if i % 2 == 0 else "gmlp"
        w_src += [p[mix + "_w_in"][i // 2].astype(BF16), p[mix + "_w_out"][i // 2].astype(BF16),
                  p["ffn_w_up"][i].T.astype(BF16), p["ffn_w_down"][i].astype(BF16)]
    send_sems, recv_sems, w_land, _ = _gather_start("weights_gather_start", _place_own("weights_place_own", w_src))
    W_IN, W_OUT, W_UP, W_DOWN = range(4)

    def wait_weight(i, k, after):
        return _gather_wait(f"l{i}_weights_wait{k}", w_land[1 + 4 * i + k], send_sems, recv_sems, 1 + 4 * i + k, after)

    small_all = _gather_wait("small_weights_wait", w_land[0], send_sems, recv_sems, 0, x0)
    sm = _unpack(small_all, small_shapes, lead=1)
    conv_w_dw = _from_shards(sm[0], sm[0].shape[1:-1])
    gmlp_b_in = _from_shards(sm[1], sm[1].shape[1:-1])
    gmlp_ln_g = _from_shards(sm[2], sm[2].shape[1:-1])
    gmlp_ln_b = _from_shards(sm[3], sm[3].shape[1:-1])
    gmlp_b_out = _from_shards(sm[4], sm[4].shape[1:-1])
    ffn_w_dw = sm[5].transpose(1, 0, 2, 3)

    def rows3(a):
        return a.reshape(a.shape[0], 1, a.shape[-1])

    conv_b_in4 = p["conv_b_in"].reshape(n_conv, N_DEV, 1, cw)
    gmlp_b_in4 = gmlp_b_in.reshape(n_gmlp, N_DEV, 1, cw)
    ffn_b_up4 = p["ffn_b_up"].reshape(DEPTH, nblk, 1, fb)
    ffn_b_dw4 = p["ffn_b_dw"].reshape(DEPTH, nblk, 1, fb)
    conv_b_dw3, conv_ln_g3, conv_ln_b3 = rows3(p["conv_b_dw"]), rows3(p["conv_ln_g"]), rows3(p["conv_ln_b"])
    conv_b_out3, gmlp_b_out3, ffn_b_down3 = rows3(p["conv_b_out"]), rows3(gmlp_b_out), rows3(p["ffn_b_down"])
    gmlp_ln_g3, gmlp_ln_b3 = rows3(gmlp_ln_g), rows3(gmlp_ln_b)
    n1g3, n1b3, n2g3, n2b3 = rows3(p["norm1_g"]), rows3(p["norm1_b"]), rows3(p["norm2_g"]), rows3(p["norm2_b"])
    w_tile, bs_tile = _sgu_operands(p["gmlp_w_s"], p["gmlp_b_s"])

    def mm_in(name, xa, wg, l, bias4, glu=False):
        tmi = min(TM_LN, t)
        c_half = half * cw

        def body(a_ref, b_ref, bias_ref, h_ref, *u_ref):
            xb = a_ref[...].astype(BF16)
            for n in range(N_DEV):
                h_ref[:, n * cw:(n + 1) * cw] = jnp.dot(xb, b_ref[n], preferred_element_type=F32) + bias_ref[n]
            if glu:
                u_ref[0][...] = h_ref[:, :c_half] * _sigmoid(h_ref[:, c_half:])

        outs = _call(body, name, (t // tmi,),
                     [pl.BlockSpec((tmi, d), lambda i: (i, 0)), pl.BlockSpec((N_DEV, d, cw), lambda i: (0, 0, 0)),
                      pl.BlockSpec((None, N_DEV, 1, cw), lambda i: (l, 0, 0, 0))],
                     [pl.BlockSpec((tmi, N_DEV * cw), lambda i: (i, 0))]
                     + ([pl.BlockSpec((tmi, c_half), lambda i: (i, 0))] if glu else []),
                     [_sds((t, N_DEV * cw), F32)] + ([_sds((t, c_half), F32)] if glu else []))(xa, wg, bias4)
        return outs if glu else outs[0]

    def mm_out_dx(name, dy, w, deps=()):
        return _matmul(name, dy, w, "nt", grid=(nt,),
                       a_spec=pl.BlockSpec((tm, d), lambda i: (i, 0)),
                       b_spec=pl.BlockSpec((d, d), lambda i: (0, 0)),
                       o_spec=pl.BlockSpec((tm, d), lambda i: (i, 0)), o_shape=(t, d), o_dtype=F32, deps=deps)

    def mm_out_dw(name, sa, dy):
        return _matmul(name, sa, dy, "tn", grid=(nt,), k_axis=0, nk=nt, acc_shape=(d, d),
                       a_spec=pl.BlockSpec((tm, d), lambda k: (k, 0)),
                       b_spec=pl.BlockSpec((tm, d), lambda k: (k, 0)),
                       o_spec=pl.BlockSpec((d, d), lambda k: (0, 0)), o_shape=(d, d), o_dtype=BF16)

    def mm_in_dx(name, dh, wg, res, norm=None):
        tmx = min(TM_LN, t)

        def body(a_ref, b_ref, res_ref, *refs):
            y = ALPHA * res_ref[...]
            for n in range(N_DEV):
                y = y + lax.dot_general(a_ref[:, n * cw:(n + 1) * cw], b_ref[n], (_DIMS["nt"], ((), ())),
                                        preferred_element_type=F32)
            if norm is None:
                refs[0][...] = y
            else:
                _ln_bwd_rows(y, *refs[:-1], pl.program_id(0) == 0, refs[-1])

        row = pl.BlockSpec((tmx, d), lambda i: (i, 0))
        ins = [pl.BlockSpec((tmx, N_DEV * cw), lambda i: (i, 0)), pl.BlockSpec((N_DEV, d, cw), lambda i: (0, 0, 0)), row]
        if norm is None:
            return _call(body, name, (t // tmx,), ins, row, _sds((t, d), F32))(dh, wg, res)
        ln_ins, ln_outs = _ln_bwd_specs(tmx, d, norm[3], lambda i: (i, 0))
        return _call(body, name, (t // tmx,), ins + ln_ins, ln_outs + [row],
                     [_sds((t, d), F32)] + [_sds((1, d), F32)] * 3 + [_sds((t, d), BF16)])(dh, wg, res, *norm[:3])

    def mm_in_dw(name, xa, dh):
        def body(a_ref, b_ref, o_ref, acc_ref):
            k = pl.program_id(1)
            p = lax.dot_general(a_ref[...].astype(BF16), b_ref[...], (_DIMS["tn"], ((), ())),
                                preferred_element_type=F32)
            _acc_rows(acc_ref, p, k == 0)

            @pl.when(k == nt - 1)
            def _():
                for n in range(half):
                    o_ref[n] = acc_ref[:, n * cw:(n + 1) * cw].astype(BF16)

        return _call(body, name, (2, nt),
                     [pl.BlockSpec((tm, d), lambda c, k: (k, 0)), pl.BlockSpec((tm, half * cw), lambda c, k: (k, c))],
                     pl.BlockSpec((half, d, cw), lambda c, k: (c, 0, 0)), _sds((N_DEV, d, cw), BF16),
                     [pltpu.VMEM((d, half * cw), F32)])(xa, dh)

    def mm_down_dw(name, a, dy, deps=()):
        return _matmul(name, a, dy, "tn", grid=(half, ntk), k_axis=1, nk=ntk, acc_shape=(fb, d),
                       a_spec=pl.BlockSpec((None, tk, fb), lambda n, k: (n, k, 0)),
                       b_spec=pl.BlockSpec((tk, d), lambda n, k: (k, 0)),
                       o_spec=pl.BlockSpec((None, fb, d), lambda n, k: (n, 0, 0)),
                       o_shape=(half, fb, d), o_dtype=BF16, deps=deps)

    def mm_up_dw(name, xa, dh):
        return _matmul(name, dh, xa, "tn", grid=(nblk, ntk), k_axis=1, nk=ntk, acc_shape=(fb, d),
                       a_spec=pl.BlockSpec((None, tk, fb), lambda n, k: (n, k, 0)),
                       b_spec=pl.BlockSpec((tk, d), lambda n, k: (k, 0)),
                       o_spec=pl.BlockSpec((None, fb, d), lambda n, k: (n, 0, 0)),
                       o_shape=(nblk, fb, d), o_dtype=BF16)

    saved = []
    xcur = xcur_b = x0
    for i in range(DEPTH):
        j = i // 2
        s = {"x": xcur_b}
        s["w_in"] = wait_weight(i, W_IN, xcur if i else target)
        if i % 2 == 0:
            s["h"], s["u"] = mm_in(f"l{i}_conv_in_glu", xcur_b, s["w_in"], j, conv_b_in4, glu=True)
            s["c"] = _dwconv31(f"l{i}_dwconv", s["u"], conv_w_dw, conv_b_dw3, j, seq)
            s["s"] = _ln_silu(f"l{i}_ln_silu", s["c"], conv_ln_g3, conv_ln_b3, j)
            b_out3 = conv_b_out3
        else:
            s["h"] = mm_in(f"l{i}_gmlp_in", xcur_b, s["w_in"], j, gmlp_b_in4)
            s["s"] = _sgu(f"l{i}_sgu", s["h"], gmlp_ln_g3, gmlp_ln_b3, w_tile, bs_tile, j)
            b_out3 = gmlp_b_out3
        s["w_out"] = wait_weight(i, W_OUT, s["s"]).reshape(d, d)
        s["x1"], s["x1b"], s["xhat1"], s["rstd1"] = _matmul_ln(
            f"l{i}_mixer_out_norm1", s["s"], s["w_out"], xcur, b_out3, n1g3, n1b3, j, i)
        s["w_up"] = wait_weight(i, W_UP, s["x1"])
        s["hg"], s["hv"], s["a"] = _ffn_up_act(f"l{i}_ffn_up_act", s["x1b"], s["w_up"], ffn_b_up4, ffn_w_dw,
                                               ffn_b_dw4, i, seq)
        s["w_down"] = wait_weight(i, W_DOWN, s["a"]).reshape(half, fb, d)
        xcur, xcur_b, s["xhat2"], s["rstd2"] = _matmul_ln(
            f"l{i}_ffn_down_norm2", s["a"], s["w_down"], s["x1"], ffn_b_down3, n2g3, n2b3, i, i)
        saved.append(s)

    loss_row, dx = _loss("loss", xcur, target)

    started = {n: [None] * p[n].shape[0] for n in BIG}
    tokens = []

    def send_grads(name, items):
        done, token = _scatter_start(name, [g for _, _, g in items])
        for (n, l, _), st in zip(items, done):
            started[n][l] = st
        tokens.append(token)

    def take_tokens():
        out = list(tokens)
        tokens.clear()
        return out

    gl = {n: [None] * p[n].shape[0] for n in REPLICATED + SMALL_SHARDED}
    dr2, gl["norm2_g"][DEPTH - 1], gl["norm2_b"][DEPTH - 1], gl["ffn_b_down"][DEPTH - 1], dr2b = _ln_res_bwd(
        f"l{DEPTH - 1}_norm2_bwd", dx, saved[-1]["xhat2"], saved[-1]["rstd2"], n2g3, DEPTH - 1)
    for i in reversed(range(DEPTH)):
        j = i // 2
        s = saved[i]
        mix = "conv" if i % 2 == 0 else "gmlp"
        g_down = mm_down_dw(f"l{i}_ffn_down_dw", s["a"], dr2b, deps=take_tokens()).reshape(N_DEV, -1, d)
        send_grads(f"l{i}_ffn_down_grad_scatter_start", [("ffn_w_down", i, g_down)])
        dcg, dcv, dbg, dbv, dwg, dwv = _ffn_act_bwd(f"l{i}_ffn_act_bwd", dr2b, s["w_down"], s["hg"], s["hv"],
                                                    ffn_w_dw, ffn_b_dw4, i, seq, deps=take_tokens())
        gl["ffn_b_dw"][i] = jnp.concatenate([dbg, dbv], axis=0).reshape(1, nblk * fb)
        gl["ffn_w_dw"][i] = jnp.concatenate([dwg[:, :FFN_K], dwv[:, :FFN_K]], axis=0)
        dh, dbu, dr1, gl["norm1_g"][i], gl["norm1_b"][i], gl[mix + "_b_out"][j] = _ffn_conv_t_dx(
            f"l{i}_ffn_conv_t_dx", dcg, dcv, ffn_w_dw, s["w_up"], dr2, s["xhat1"], s["rstd1"], n1g3, i, seq,
            deps=take_tokens())
        gl["ffn_b_up"][i] = dbu.reshape(1, nblk * fb)
        send_grads(f"l{i}_ffn_up_grad_scatter_start", [("ffn_w_up", i, mm_up_dw(f"l{i}_ffn_up_dw", s["x1b"], dh))])
        ds = mm_out_dx(f"l{i}_{mix}_out_dx", dr1, s["w_out"], deps=take_tokens())
        g_out = mm_out_dw(f"l{i}_{mix}_out_dw", s["s"], dr1).reshape(N_DEV, -1, d)
        if i % 2 == 0:
            dc, gl["conv_ln_g"][j], gl["conv_ln_b"][j], gl["conv_b_dw"][j] = _ln_silu_bwd(
                f"l{i}_ln_silu_bwd", ds, s["c"], conv_ln_g3, conv_ln_b3, j)
            du, dwdw = _dwconv31_bwd(f"l{i}_dwconv_bwd", dc, s["u"], conv_w_dw, j, seq)
            gl["conv_w_dw"][j] = dwdw[:CONV_K]
            dh, gl["conv_b_in"][j] = _glu_bwd(f"l{i}_glu_bwd", du, s["h"])
        else:
            dh, gl["gmlp_b_in"][j], dwt, dbt, gl["gmlp_ln_g"][j], gl["gmlp_ln_b"][j] = _sgu_bwd(
                f"l{i}_sgu_bwd", ds, s["h"], gmlp_ln_g3, gmlp_ln_b3, w_tile, bs_tile, j)
            gl["gmlp_w_s"][j], gl["gmlp_b_s"][j] = _sgu_param_grads(dwt, dbt)
            if i == 1:
                ws_local = jnp.stack(gl["gmlp_w_s"]).reshape(-1, LANES)
                ws_send, ws_recv, ws_land, ws_token = _gather_start(
                    "w_s_grads_gather_start", _place_own("w_s_grads_place_own", [ws_local]))
                tokens.append(ws_token)
        if i > 0:
            prev = saved[i - 1]
            dr2, gl["norm2_g"][i - 1], gl["norm2_b"][i - 1], gl["ffn_b_down"][i - 1], dr2b = mm_in_dx(
                f"l{i}_{mix}_in_dx_norm2_bwd", dh, s["w_in"], dr1, (prev["xhat2"], prev["rstd2"], n2g3, i - 1))
        else:
            dx = mm_in_dx(f"l{i}_{mix}_in_dx", dh, s["w_in"], dr1)
        send_grads(f"l{i}_mixer_grads_scatter_start",
                   [(mix + "_w_out", j, g_out), (mix + "_w_in", j, mm_in_dw(f"l{i}_{mix}_in_dw", s["x"], dh))])
    grad_x = _from_segments(dx, TM_EW).reshape(bsz, seq, d)

    late = [n for n in REPLICATED if n != "gmlp_w_s"]
    full_small = {n: jnp.stack(gl[n]).reshape(p[n].shape) for n in late}
    shard_small = {}
    for n in SMALL_SHARDED:
        if n == "ffn_w_dw":
            shard_small[n] = jnp.stack(gl[n]).transpose(1, 0, 2, 3)
        else:
            width = p[n].shape[-1]
            lead = p[n].shape[:-1]
            shard_small[n] = _to_shards(jnp.stack(gl[n]).reshape(lead + (N_DEV * width,)), width)
    flat_shapes = [(1, LANES)] + [p[n].shape for n in late] + [(N_DEV,) + p[n].shape for n in SMALL_SHARDED]
    flat_local = _pack([loss_row] + [full_small[n] for n in late] + [shard_small[n] for n in SMALL_SHARDED],
                       row_multiple=ROW_TILE_CAP)

    small_send, small_recv, small_land, small_token = _gather_start(
        "small_grads_gather_start", _place_own("small_grads_place_own", [flat_local]))

    grads, delta, new_m, new_v = {}, {}, {}, {}
    dev1 = jnp.reshape(dev, (1,)).astype(jnp.int32)
    order = ["ffn_w_down", "ffn_w_up", "gmlp_w_out", "gmlp_w_in", "conv_w_out", "conv_w_in"]
    after = small_token
    for n in order:
        parts_done, lands_done = _scatter_wait(f"grads_{n}_scatter_wait", started[n], after)
        state = [p[n], p["m_" + n], p["v_" + n]]
        if n == "ffn_w_up":
            state = [a.transpose(0, 2, 1) for a in state]
        outs = _sum8_adamw(f"adamw_{n}", dev1, lands_done, parts_done, *state)
        after = outs[-1]
        if n == "ffn_w_up":
            outs = [a.transpose(0, 2, 1) for a in outs]
        grads[n], delta[n], new_m[n], new_v[n] = outs

    ws_parts = _gather_wait("w_s_grads_gather_wait", ws_land[0], ws_send, ws_recv, 0, after)
    ws_sum = _sum8("sum_w_s_grads", ws_parts)
    grads["gmlp_w_s"] = ws_sum.reshape(p["gmlp_w_s"].shape)
    small_parts = _gather_wait("small_grads_gather_wait", small_land[0], small_send, small_recv, 0, ws_sum)
    summed = _unpack(_sum8("sum_small_grads", small_parts), flat_shapes)
    loss = summed[0][0, 0]
    grads.update(zip(late, summed[1:1 + len(late)]))
    for n, g in zip(SMALL_SHARDED, summed[1 + len(late):]):
        grads[n] = lax.dynamic_index_in_dim(g, dev, axis=0, keepdims=False)
    small = REPLICATED + SMALL_SHARDED
    d_s, m_s, v_s = _adamw_small("adamw_small", [grads[n] for n in small], [p[n] for n in small],
                                 [p["m_" + n] for n in small], [p["v_" + n] for n in small])
    for n, dd, mm, vv in zip(small, d_s, m_s, v_s):
        delta[n], new_m[n], new_v[n] = dd, mm, vv

    return (loss, grad_x, *[grads[n] for n in WEIGHTS], *[delta[n] for n in WEIGHTS],
            *[new_m[n] for n in WEIGHTS], *[new_v[n] for n in WEIGHTS])


def kernel(x, conv_w_in, conv_b_in, conv_w_dw, conv_b_dw, conv_ln_g, conv_ln_b, conv_w_out, conv_b_out, gmlp_w_in, gmlp_b_in, gmlp_ln_g, gmlp_ln_b, gmlp_w_s, gmlp_b_s, gmlp_w_out, gmlp_b_out, ffn_w_up, ffn_b_up, ffn_w_dw, ffn_b_dw, ffn_w_down, ffn_b_down, norm1_g, norm1_b, norm2_g, norm2_b, loss_target, m_conv_w_in, m_conv_b_in, m_conv_w_dw, m_conv_b_dw, m_conv_ln_g, m_conv_ln_b, m_conv_w_out, m_conv_b_out, m_gmlp_w_in, m_gmlp_b_in, m_gmlp_ln_g, m_gmlp_ln_b, m_gmlp_w_s, m_gmlp_b_s, m_gmlp_w_out, m_gmlp_b_out, m_ffn_w_up, m_ffn_b_up, m_ffn_w_dw, m_ffn_b_dw, m_ffn_w_down, m_ffn_b_down, m_norm1_g, m_norm1_b, m_norm2_g, m_norm2_b, v_conv_w_in, v_conv_b_in, v_conv_w_dw, v_conv_b_dw, v_conv_ln_g, v_conv_ln_b, v_conv_w_out, v_conv_b_out, v_gmlp_w_in, v_gmlp_b_in, v_gmlp_ln_g, v_gmlp_ln_b, v_gmlp_w_s, v_gmlp_b_s, v_gmlp_w_out, v_gmlp_b_out, v_ffn_w_up, v_ffn_b_up, v_ffn_w_dw, v_ffn_b_dw, v_ffn_w_down, v_ffn_b_down, v_norm1_g, v_norm1_b, v_norm2_g, v_norm2_b):
    return _step(dict(locals()))
```
